```python
import math
import jax, jax.numpy as jnp
from jax import lax
import numpy as np

D_MODEL = 1024
BATCH = 8
SEQ = 4096
DEPTH = 1

HEAD_DIM = 64
FOX_HEADS = D_MODEL // (2 * HEAD_DIM)
FOX_WIDTH = FOX_HEADS * HEAD_DIM
MLA_HEADS = D_MODEL // (2 * HEAD_DIM)
MLA_NOPE_DIM = 64
MLA_ROPE_DIM = 32
MLA_QK_DIM = MLA_NOPE_DIM + MLA_ROPE_DIM
MLA_V_DIM = 64
MLA_WIDTH = MLA_HEADS * MLA_V_DIM
MIX_WIDTH = FOX_WIDTH + MLA_WIDTH
Q_LORA_RANK = 3 * D_MODEL // 8
KV_LORA_RANK = D_MODEL // 4
D_FF = 4 * D_MODEL
BLOCK_Q = 128
ROPE_THETA = 10000.0
EPS = 1e-6

OFF_FQ = 0
OFF_FK = OFF_FQ + FOX_WIDTH
OFF_FV = OFF_FK + FOX_WIDTH
OFF_FF = OFF_FV + FOX_WIDTH
OFF_CQ = OFF_FF + FOX_HEADS
OFF_CKV = OFF_CQ + Q_LORA_RANK
OFF_KR = OFF_CKV + KV_LORA_RANK
IN_COLS = OFF_KR + MLA_ROPE_DIM

kernel_name = "hymba_fox_mla_hybrid_block"


def rmsnorm(x, g):
    x32 = x.astype(jnp.float32)
    y = x32 * lax.rsqrt(jnp.mean(x32 * x32, axis=-1, keepdims=True) + EPS)
    return (y * g.astype(jnp.float32)).astype(x.dtype)


def rope_cos_sin(positions):
    inv_freq = ROPE_THETA ** (-jnp.arange(0, MLA_ROPE_DIM, 2, dtype=jnp.float32) / MLA_ROPE_DIM)
    ang = positions.astype(jnp.float32)[..., None] * inv_freq
    return jnp.cos(ang), jnp.sin(ang)


def apply_rope(x, cos, sin):
    half = x.shape[-1] // 2
    x1 = x[..., :half].astype(jnp.float32)
    x2 = x[..., half:].astype(jnp.float32)
    return jnp.concatenate([x1 * cos - x2 * sin, x2 * cos + x1 * sin], axis=-1).astype(x.dtype)


def causal_block_attention(q, k, v, scale, log_decay_cum=None):
    b, h, s, dk = q.shape
    nb = s // BLOCK_Q
    q_blocks = jnp.moveaxis(q.reshape(b, h, nb, BLOCK_Q, dk), 2, 0)
    starts = jnp.arange(nb, dtype=jnp.int32) * BLOCK_Q
    key_pos = jnp.arange(s, dtype=jnp.int32)

    def attend(q_blk, start, bias):
        logits = jnp.einsum('bhqd,bhkd->bhqk', q_blk, k).astype(jnp.float32) * scale
        if bias is not None:
            logits = logits + bias
        q_pos = start + jnp.arange(BLOCK_Q, dtype=jnp.int32)
        mask = key_pos[None, :] <= q_pos[:, None]
        logits = jnp.where(mask, logits, -jnp.inf)
        p = jax.nn.softmax(logits, axis=-1).astype(v.dtype)
        return jnp.einsum('bhqk,bhkd->bhqd', p, v)

    if log_decay_cum is None:
        out = lax.map(lambda xs: attend(xs[0], xs[1], None), (q_blocks, starts))
    else:
        F = log_decay_cum
        fq_blocks = jnp.moveaxis(F.reshape(b, h, nb, BLOCK_Q), 2, 0)
        out = lax.map(
            lambda xs: attend(xs[0], xs[1], xs[2][..., :, None] - F[:, :, None, :]),
            (q_blocks, starts, fq_blocks))
    return jnp.moveaxis(out, 0, 2).reshape(b, h, s, -1)


def _fwd_setup_inputs(seed: int = 0) -> dict:
    key = jax.random.key(seed)
    ks = jax.random.split(key, 16)
    f32 = jnp.float32

    def w(k, shape, fan_in):
        return jax.random.normal(k, shape, f32) * (fan_in ** -0.5)

    def gain(k, shape):
        return 1.0 + 0.02 * jax.random.normal(k, shape, f32)

    x = jax.random.normal(ks[0], (BATCH, SEQ, D_MODEL), f32)
    offsets = jax.random.randint(ks[1], (BATCH, 1), 0, 64, dtype=jnp.int32)
    positions = jnp.arange(SEQ, dtype=jnp.int32)[None, :] + offsets
    return {
        "x": x,
        "positions": positions,
        "attn_norm_g": gain(ks[2], (DEPTH, D_MODEL)),
        "w_in": w(ks[3], (DEPTH, D_MODEL, IN_COLS), D_MODEL),
        "b_forget": 3.0 + 0.5 * jax.random.normal(ks[4], (DEPTH, FOX_HEADS), f32),
        "q_norm_g": gain(ks[5], (DEPTH, Q_LORA_RANK)),
        "w_uq": w(ks[6], (DEPTH, Q_LORA_RANK, MLA_HEADS * MLA_QK_DIM), Q_LORA_RANK),
        "kv_norm_g": gain(ks[7], (DEPTH, KV_LORA_RANK)),
        "w_ukv": w(ks[8], (DEPTH, KV_LORA_RANK, MLA_HEADS * (MLA_NOPE_DIM + MLA_V_DIM)), KV_LORA_RANK),
        "fox_out_g": gain(ks[9], (DEPTH, FOX_WIDTH)),
        "mla_out_g": gain(ks[10], (DEPTH, MLA_WIDTH)),
        "w_o": w(ks[11], (DEPTH, MIX_WIDTH, D_MODEL), MIX_WIDTH),
        "mlp_norm_g": gain(ks[12], (DEPTH, D_MODEL)),
        "w_up": w(ks[13], (DEPTH, D_MODEL, D_FF), D_MODEL),
        "w_down": w(ks[14], (DEPTH, D_FF, D_MODEL), D_FF),
        "final_norm_g": gain(ks[15], (D_MODEL,)),
    }


def _fwd_reference(x, positions, attn_norm_g, w_in, b_forget, q_norm_g, w_uq, kv_norm_g, w_ukv,
              fox_out_g, mla_out_g, w_o, mlp_norm_g, w_up, w_down, final_norm_g):
    b, s, _ = x.shape
    cos, sin = rope_cos_sin(positions)
    fox_scale = 1.0 / math.sqrt(HEAD_DIM)
    mla_scale = 1.0 / math.sqrt(MLA_QK_DIM)

    for l in range(DEPTH):
        h = rmsnorm(x, attn_norm_g[l])
        proj = jnp.einsum('bsd,dc->bsc', h, w_in[l])

        fq = proj[..., OFF_FQ:OFF_FK].reshape(b, s, FOX_HEADS, HEAD_DIM).transpose(0, 2, 1, 3)
        fk = proj[..., OFF_FK:OFF_FV].reshape(b, s, FOX_HEADS, HEAD_DIM).transpose(0, 2, 1, 3)
        fv = proj[..., OFF_FV:OFF_FF].reshape(b, s, FOX_HEADS, HEAD_DIM).transpose(0, 2, 1, 3)
        f_logit = proj[..., OFF_FF:OFF_CQ].astype(jnp.float32) + b_forget[l].astype(jnp.float32)
        log_f = jax.nn.log_sigmoid(f_logit)
        F = jnp.cumsum(log_f, axis=1).transpose(0, 2, 1)
        fox = causal_block_attention(fq, fk, fv, fox_scale, F)
        fox = fox.transpose(0, 2, 1, 3).reshape(b, s, FOX_WIDTH)

        c_q = rmsnorm(proj[..., OFF_CQ:OFF_CKV], q_norm_g[l])
        c_kv = rmsnorm(proj[..., OFF_CKV:OFF_KR], kv_norm_g[l])
        k_rope = apply_rope(proj[..., OFF_KR:IN_COLS], cos, sin)
        q = jnp.einsum('bsr,rc->bsc', c_q, w_uq[l]).reshape(b, s, MLA_HEADS, MLA_QK_DIM)
        q_nope = q[..., :MLA_NOPE_DIM]
        q_rope = apply_rope(q[..., MLA_NOPE_DIM:], cos[:, :, None, :], sin[:, :, None, :])
        kv = jnp.einsum('bsr,rc->bsc', c_kv, w_ukv[l]).reshape(b, s, MLA_HEADS, MLA_NOPE_DIM + MLA_V_DIM)
        k_nope = kv[..., :MLA_NOPE_DIM]
        mv = kv[..., MLA_NOPE_DIM:]
        mq = jnp.concatenate([q_nope, q_rope], axis=-1).transpose(0, 2, 1, 3)
        mk = jnp.concatenate(
            [k_nope, jnp.broadcast_to(k_rope[:, :, None, :], (b, s, MLA_HEADS, MLA_ROPE_DIM))],
            axis=-1).transpose(0, 2, 1, 3)
        mv = mv.transpose(0, 2, 1, 3)
        mla = causal_block_attention(mq, mk, mv, mla_scale)
        mla = mla.transpose(0, 2, 1, 3).reshape(b, s, MLA_WIDTH)

        mixed = jnp.concatenate([rmsnorm(fox, fox_out_g[l]), rmsnorm(mla, mla_out_g[l])], axis=-1)
        x = x + jnp.einsum('bsc,cd->bsd', mixed, w_o[l])

        h = rmsnorm(x, mlp_norm_g[l])
        u = jnp.einsum('bsd,df->bsf', h, w_up[l])
        x = x + jnp.einsum('bsf,fd->bsd', jnp.square(jax.nn.relu(u)), w_down[l])

    return rmsnorm(x, final_norm_g)


import jax as _jax
import jax.numpy as _jnp

TWIN_FORMAT = 'train_step'
FWD_PARAMS = ['x', 'positions', 'attn_norm_g', 'w_in', 'b_forget', 'q_norm_g', 'w_uq', 'kv_norm_g', 'w_ukv', 'fox_out_g', 'mla_out_g', 'w_o', 'mlp_norm_g', 'w_up', 'w_down', 'final_norm_g']
TWIN_WEIGHTS = ['attn_norm_g', 'w_in', 'b_forget', 'q_norm_g', 'w_uq', 'kv_norm_g', 'w_ukv', 'fox_out_g', 'mla_out_g', 'w_o', 'mlp_norm_g', 'w_up', 'w_down', 'final_norm_g']
TWIN_DIFF_INPUT = 'x'
TWIN_INPUTS = ['x', 'positions', 'attn_norm_g', 'w_in', 'b_forget', 'q_norm_g', 'w_uq', 'kv_norm_g', 'w_ukv', 'fox_out_g', 'mla_out_g', 'w_o', 'mlp_norm_g', 'w_up', 'w_down', 'final_norm_g', 'loss_target', 'm_attn_norm_g', 'm_w_in', 'm_b_forget', 'm_q_norm_g', 'm_w_uq', 'm_kv_norm_g', 'm_w_ukv', 'm_fox_out_g', 'm_mla_out_g', 'm_w_o', 'm_mlp_norm_g', 'm_w_up', 'm_w_down', 'm_final_norm_g', 'v_attn_norm_g', 'v_w_in', 'v_b_forget', 'v_q_norm_g', 'v_w_uq', 'v_kv_norm_g', 'v_w_ukv', 'v_fox_out_g', 'v_mla_out_g', 'v_w_o', 'v_mlp_norm_g', 'v_w_up', 'v_w_down', 'v_final_norm_g']
TWIN_OUTPUTS = ['loss', 'grad_x', 'grad_attn_norm_g', 'grad_w_in', 'grad_b_forget', 'grad_q_norm_g', 'grad_w_uq', 'grad_kv_norm_g', 'grad_w_ukv', 'grad_fox_out_g', 'grad_mla_out_g', 'grad_w_o', 'grad_mlp_norm_g', 'grad_w_up', 'grad_w_down', 'grad_final_norm_g', 'delta_attn_norm_g', 'delta_w_in', 'delta_b_forget', 'delta_q_norm_g', 'delta_w_uq', 'delta_kv_norm_g', 'delta_w_ukv', 'delta_fox_out_g', 'delta_mla_out_g', 'delta_w_o', 'delta_mlp_norm_g', 'delta_w_up', 'delta_w_down', 'delta_final_norm_g', 'new_m_attn_norm_g', 'new_m_w_in', 'new_m_b_forget', 'new_m_q_norm_g', 'new_m_w_uq', 'new_m_kv_norm_g', 'new_m_w_ukv', 'new_m_fox_out_g', 'new_m_mla_out_g', 'new_m_w_o', 'new_m_mlp_norm_g', 'new_m_w_up', 'new_m_w_down', 'new_m_final_norm_g', 'new_v_attn_norm_g', 'new_v_w_in', 'new_v_b_forget', 'new_v_q_norm_g', 'new_v_w_uq', 'new_v_kv_norm_g', 'new_v_w_ukv', 'new_v_fox_out_g', 'new_v_mla_out_g', 'new_v_w_o', 'new_v_mlp_norm_g', 'new_v_w_up', 'new_v_w_down', 'new_v_final_norm_g']
TWIN_LEAF_KINDS = {'loss': 'loss', 'grad_x': 'grad_x', 'grad_attn_norm_g': 'grad_w', 'grad_w_in': 'grad_w', 'grad_b_forget': 'grad_w', 'grad_q_norm_g': 'grad_w', 'grad_w_uq': 'grad_w', 'grad_kv_norm_g': 'grad_w', 'grad_w_ukv': 'grad_w', 'grad_fox_out_g': 'grad_w', 'grad_mla_out_g': 'grad_w', 'grad_w_o': 'grad_w', 'grad_mlp_norm_g': 'grad_w', 'grad_w_up': 'grad_w', 'grad_w_down': 'grad_w', 'grad_final_norm_g': 'grad_w', 'delta_attn_norm_g': 'delta_w', 'delta_w_in': 'delta_w', 'delta_b_forget': 'delta_w', 'delta_q_norm_g': 'delta_w', 'delta_w_uq': 'delta_w', 'delta_kv_norm_g': 'delta_w', 'delta_w_ukv': 'delta_w', 'delta_fox_out_g': 'delta_w', 'delta_mla_out_g': 'delta_w', 'delta_w_o': 'delta_w', 'delta_mlp_norm_g': 'delta_w', 'delta_w_up': 'delta_w', 'delta_w_down': 'delta_w', 'delta_final_norm_g': 'delta_w', 'new_m_attn_norm_g': 'new_m', 'new_m_w_in': 'new_m', 'new_m_b_forget': 'new_m', 'new_m_q_norm_g': 'new_m', 'new_m_w_uq': 'new_m', 'new_m_kv_norm_g': 'new_m', 'new_m_w_ukv': 'new_m', 'new_m_fox_out_g': 'new_m', 'new_m_mla_out_g': 'new_m', 'new_m_w_o': 'new_m', 'new_m_mlp_norm_g': 'new_m', 'new_m_w_up': 'new_m', 'new_m_w_down': 'new_m', 'new_m_final_norm_g': 'new_m', 'new_v_attn_norm_g': 'new_v', 'new_v_w_in': 'new_v', 'new_v_b_forget': 'new_v', 'new_v_q_norm_g': 'new_v', 'new_v_w_uq': 'new_v', 'new_v_kv_norm_g': 'new_v', 'new_v_w_ukv': 'new_v', 'new_v_fox_out_g': 'new_v', 'new_v_mla_out_g': 'new_v', 'new_v_w_o': 'new_v', 'new_v_mlp_norm_g': 'new_v', 'new_v_w_up': 'new_v', 'new_v_w_down': 'new_v', 'new_v_final_norm_g': 'new_v'}


def _forward(args):
    return _fwd_reference(*[args[k] for k in FWD_PARAMS])


def _output_shape():
    out = _jax.eval_shape(lambda: _forward(_fwd_setup_inputs(0)))
    return out.shape, out.dtype

N_MICROBATCH = 1
ADAM_LR = 0.001
ADAM_B1 = 0.9
ADAM_B2 = 0.999
ADAM_EPS = 1e-08
ADAM_WD = 0.01
ADAM_STEP = 10
PER_EXAMPLE_BATCH_AXIS = {'x': 0, 'positions': 0, 'loss_target': 0}
SHARED_INPUTS = []
_WEIGHT_DTYPES = {'attn_norm_g': _jnp.float32, 'w_in': _jnp.float32, 'b_forget': _jnp.float32, 'q_norm_g': _jnp.float32, 'w_uq': _jnp.float32, 'kv_norm_g': _jnp.float32, 'w_ukv': _jnp.float32, 'fox_out_g': _jnp.float32, 'mla_out_g': _jnp.float32, 'w_o': _jnp.float32, 'mlp_norm_g': _jnp.float32, 'w_up': _jnp.float32, 'w_down': _jnp.float32, 'final_norm_g': _jnp.float32}
MOMENT_SCALE = {'attn_norm_g': 2.329287e-01, 'w_in': 1.429075e-01, 'b_forget': 5.660617e-01, 'q_norm_g': 1.700574e-01, 'w_uq': 1.142870e-01, 'kv_norm_g': 3.855734e-01, 'w_ukv': 1.353720e-01, 'fox_out_g': 1.405042e-01, 'mla_out_g': 1.396310e-01, 'w_o': 1.357460e-01, 'mlp_norm_g': 1.342373e-01, 'w_up': 6.724687e-02, 'w_down': 1.273264e-01, 'final_norm_g': 3.219512e+01}


def _to_microbatches(a, axis):
    t = _jnp.moveaxis(a, axis, 0)
    t = t.reshape((N_MICROBATCH, t.shape[0] // N_MICROBATCH) + t.shape[1:])
    return _jnp.moveaxis(t, 1, axis + 1)


def setup_inputs(seed: int = 0) -> dict:
    inp = _fwd_setup_inputs(seed)
    key = _jax.random.fold_in(_jax.random.key(seed), 7919)
    shape, _ = _output_shape()
    out = dict(inp)
    out["loss_target"] = _jax.random.normal(_jax.random.fold_in(key, 0), shape, _jnp.float32)
    for i, name in enumerate(TWIN_WEIGHTS):
        w = inp[name].astype(_jnp.float32)
        if MOMENT_SCALE is None:
            s = _jnp.sqrt(_jnp.mean(_jnp.square(w)) + 1e-30)
        else:
            s = MOMENT_SCALE[name]
        km, kv = _jax.random.split(_jax.random.fold_in(key, i + 1))
        out[name] = w
        out["m_" + name] = s * _jax.random.normal(km, w.shape, _jnp.float32)
        out["v_" + name] = (s * s) * _jax.random.uniform(kv, w.shape, _jnp.float32, 0.5, 1.5)
    if N_MICROBATCH > 1:
        for name, axis in PER_EXAMPLE_BATCH_AXIS.items():
            out[name] = _to_microbatches(out[name], axis)
    return {'x': out['x'], 'positions': out['positions'], 'attn_norm_g': out['attn_norm_g'], 'w_in': out['w_in'], 'b_forget': out['b_forget'], 'q_norm_g': out['q_norm_g'], 'w_uq': out['w_uq'], 'kv_norm_g': out['kv_norm_g'], 'w_ukv': out['w_ukv'], 'fox_out_g': out['fox_out_g'], 'mla_out_g': out['mla_out_g'], 'w_o': out['w_o'], 'mlp_norm_g': out['mlp_norm_g'], 'w_up': out['w_up'], 'w_down': out['w_down'], 'final_norm_g': out['final_norm_g'], 'loss_target': out['loss_target'], 'm_attn_norm_g': out['m_attn_norm_g'], 'm_w_in': out['m_w_in'], 'm_b_forget': out['m_b_forget'], 'm_q_norm_g': out['m_q_norm_g'], 'm_w_uq': out['m_w_uq'], 'm_kv_norm_g': out['m_kv_norm_g'], 'm_w_ukv': out['m_w_ukv'], 'm_fox_out_g': out['m_fox_out_g'], 'm_mla_out_g': out['m_mla_out_g'], 'm_w_o': out['m_w_o'], 'm_mlp_norm_g': out['m_mlp_norm_g'], 'm_w_up': out['m_w_up'], 'm_w_down': out['m_w_down'], 'm_final_norm_g': out['m_final_norm_g'], 'v_attn_norm_g': out['v_attn_norm_g'], 'v_w_in': out['v_w_in'], 'v_b_forget': out['v_b_forget'], 'v_q_norm_g': out['v_q_norm_g'], 'v_w_uq': out['v_w_uq'], 'v_kv_norm_g': out['v_kv_norm_g'], 'v_w_ukv': out['v_w_ukv'], 'v_fox_out_g': out['v_fox_out_g'], 'v_mla_out_g': out['v_mla_out_g'], 'v_w_o': out['v_w_o'], 'v_mlp_norm_g': out['v_mlp_norm_g'], 'v_w_up': out['v_w_up'], 'v_w_down': out['v_w_down'], 'v_final_norm_g': out['v_final_norm_g']}


def _loss(weights, diff, rest, loss_target):
    with _jax.named_scope("forward"):
        args = {**rest, TWIN_DIFF_INPUT: diff, **{k: w.astype(_WEIGHT_DTYPES[k]) for k, w in weights.items()}}
        y = _forward(args)
    with _jax.named_scope("loss_head"):
        err = _jnp.square(y.astype(_jnp.float32) - loss_target)
        return 0.5 * _jnp.sum(_jnp.mean(err, axis=-1)) if err.ndim else 0.5 * err


def _adamw(w, g, m, v):
    m = ADAM_B1 * m + (1.0 - ADAM_B1) * g
    v = ADAM_B2 * v + (1.0 - ADAM_B2) * _jnp.square(g)
    m_hat = m / (1.0 - ADAM_B1 ** ADAM_STEP)
    v_hat = v / (1.0 - ADAM_B2 ** ADAM_STEP)
    delta = -ADAM_LR * (m_hat / (_jnp.sqrt(v_hat) + ADAM_EPS) + ADAM_WD * w)
    return delta, m, v


def reference(x, positions, attn_norm_g, w_in, b_forget, q_norm_g, w_uq, kv_norm_g, w_ukv, fox_out_g, mla_out_g, w_o, mlp_norm_g, w_up, w_down, final_norm_g, loss_target, m_attn_norm_g, m_w_in, m_b_forget, m_q_norm_g, m_w_uq, m_kv_norm_g, m_w_ukv, m_fox_out_g, m_mla_out_g, m_w_o, m_mlp_norm_g, m_w_up, m_w_down, m_final_norm_g, v_attn_norm_g, v_w_in, v_b_forget, v_q_norm_g, v_w_uq, v_kv_norm_g, v_w_ukv, v_fox_out_g, v_mla_out_g, v_w_o, v_mlp_norm_g, v_w_up, v_w_down, v_final_norm_g):
    given = dict(x=x, positions=positions, attn_norm_g=attn_norm_g, w_in=w_in, b_forget=b_forget, q_norm_g=q_norm_g, w_uq=w_uq, kv_norm_g=kv_norm_g, w_ukv=w_ukv, fox_out_g=fox_out_g, mla_out_g=mla_out_g, w_o=w_o, mlp_norm_g=mlp_norm_g, w_up=w_up, w_down=w_down, final_norm_g=final_norm_g, loss_target=loss_target, m_attn_norm_g=m_attn_norm_g, m_w_in=m_w_in, m_b_forget=m_b_forget, m_q_norm_g=m_q_norm_g, m_w_uq=m_w_uq, m_kv_norm_g=m_kv_norm_g, m_w_ukv=m_w_ukv, m_fox_out_g=m_fox_out_g, m_mla_out_g=m_mla_out_g, m_w_o=m_w_o, m_mlp_norm_g=m_mlp_norm_g, m_w_up=m_w_up, m_w_down=m_w_down, m_final_norm_g=m_final_norm_g, v_attn_norm_g=v_attn_norm_g, v_w_in=v_w_in, v_b_forget=v_b_forget, v_q_norm_g=v_q_norm_g, v_w_uq=v_w_uq, v_kv_norm_g=v_kv_norm_g, v_w_ukv=v_w_ukv, v_fox_out_g=v_fox_out_g, v_mla_out_g=v_mla_out_g, v_w_o=v_w_o, v_mlp_norm_g=v_mlp_norm_g, v_w_up=v_w_up, v_w_down=v_w_down, v_final_norm_g=v_final_norm_g)
    weights = {n: given[n] for n in TWIN_WEIGHTS}
    shared = {n: given[n] for n in SHARED_INPUTS}
    per_example = {n: given[n] for n in ['x', 'positions']}
    grad_fn = _jax.value_and_grad(_loss, argnums=(0, 1))

    def one_microbatch(ex, loss_target):
        ex = dict(ex)
        diff = ex.pop(TWIN_DIFF_INPUT)
        return grad_fn(weights, diff, {**shared, **ex}, loss_target)

    if N_MICROBATCH == 1:
        loss, (grad_w, grad_x) = one_microbatch(per_example, given["loss_target"])
    else:
        def body(carry, xs):
            loss_sum, grad_sum = carry
            l_k, (gw_k, gx_k) = one_microbatch(xs[0], xs[1])
            with _jax.named_scope("update"):
                return (loss_sum + l_k, _jax.tree.map(_jnp.add, grad_sum, gw_k)), gx_k

        init = (_jnp.zeros((), _jnp.float32), _jax.tree.map(_jnp.zeros_like, weights))
        (loss, grad_w), grad_x = _jax.lax.scan(body, init, (per_example, given["loss_target"]))
    with _jax.named_scope("update"):
        delta_w, new_m, new_v = {}, {}, {}
        for n in TWIN_WEIGHTS:
            delta_w[n], new_m[n], new_v[n] = _adamw(weights[n], grad_w[n], given["m_" + n], given["v_" + n])
    return (loss, grad_x, *[grad_w[n] for n in TWIN_WEIGHTS], *[delta_w[n] for n in TWIN_WEIGHTS],
            *[new_m[n] for n in TWIN_WEIGHTS], *[new_v[n] for n in TWIN_WEIGHTS])
```

```python
import jax
import jax.numpy as jnp
from jax import lax
from jax.experimental import pallas as pl
from jax.experimental.pallas import tpu as pltpu

F32 = jnp.float32
BF16 = jnp.bfloat16
MESH = pl.DeviceIdType.MESH

EPS = 1e-6
ROPE_THETA = 10000.0
N_HEADS = 8
HEAD_DIM = 64
ROPE_DIM = 32
LANES = 128
Q_RANK = 384
KV_RANK = 256
N_CHIPS = 4
ADAM_LR, ADAM_B1, ADAM_B2, ADAM_EPS, ADAM_WD, ADAM_STEP = 0.001, 0.9, 0.999, 1e-08, 0.01, 10
VMEM_LIMIT = 48 * 1024 * 1024
NT = (((1,), (1,)), ((), ()))
TN = (((0,), (0,)), ((), ()))


def _params(sem=None):
    return pltpu.CompilerParams(dimension_semantics=sem, vmem_limit_bytes=VMEM_LIMIT)


def rmsnorm(x, g, *, n_valid, out_dtype, name, bt=512):
    t, d = x.shape
    bt = min(bt, t)

    def body(x_ref, g_ref, o_ref):
        xv = x_ref[...].astype(F32)
        r = lax.rsqrt(jnp.sum(xv * xv, axis=-1, keepdims=True) * (1.0 / n_valid) + EPS)
        o_ref[...] = (xv * r * g_ref[...]).astype(o_ref.dtype)

    return pl.pallas_call(
        body, name=name, grid=(t // bt,),
        in_specs=[pl.BlockSpec((bt, d), lambda i: (i, 0)), pl.BlockSpec((1, d), lambda i: (0, 0))],
        out_specs=pl.BlockSpec((bt, d), lambda i: (i, 0)),
        out_shape=jax.ShapeDtypeStruct((t, d), out_dtype),
        compiler_params=_params(("parallel",)),
    )(x, g)


def rmsnorm_bwd(dh, x, g, res, *, n_valid, out_dtypes, name, bt=512):
    t, d = x.shape
    bt = min(bt, t)
    has_res = res is not None

    def body(*refs):
        dh_ref, x_ref, g_ref = refs[:3]
        res_ref = refs[3] if has_res else None
        outs = refs[3 + has_res:]
        dx_refs, dg_ref = outs[:-1], outs[-1]
        xv = x_ref[...].astype(F32)
        dhv = dh_ref[...].astype(F32)
        r = lax.rsqrt(jnp.sum(xv * xv, axis=-1, keepdims=True) * (1.0 / n_valid) + EPS)
        u = dhv * g_ref[...]
        dot = jnp.sum(u * xv, axis=-1, keepdims=True)
        dx = r * u - xv * (r * r * r * (1.0 / n_valid) * dot)
        if has_res:
            dx = dx + res_ref[...]
        for o in dx_refs:
            o[...] = dx.astype(o.dtype)

        @pl.when(pl.program_id(0) == 0)
        def _():
            dg_ref[...] = jnp.zeros_like(dg_ref)

        dg_ref[...] += jnp.sum(dhv * (xv * r), axis=0, keepdims=True)

    row = pl.BlockSpec((bt, d), lambda i: (i, 0))
    vec = pl.BlockSpec((1, d), lambda i: (0, 0))
    ins = [dh, x, g] + ([res] if has_res else [])
    outs = pl.pallas_call(
        body, name=name, grid=(t // bt,),
        in_specs=[row, row, vec] + ([row] if has_res else []),
        out_specs=[row] * len(out_dtypes) + [vec],
        out_shape=[jax.ShapeDtypeStruct((t, d), dt) for dt in out_dtypes] + [jax.ShapeDtypeStruct((1, d), F32)],
        compiler_params=_params(("arbitrary",)),
    )(*ins)
    return outs


def mm(a, b, *, trans_b=False, extras=(), epilogue=None, out_dtypes, name, bm=512, bn=512):
    m, k = a.shape
    n = b.shape[0] if trans_b else b.shape[1]
    bm, bn = min(bm, m), min(bn, n)
    n_ex = len(extras)

    def body(*refs):
        a_ref, b_ref = refs[0], refs[1]
        ex = refs[2:2 + n_ex]
        outs = refs[2 + n_ex:]
        dn = NT if trans_b else (((1,), (0,)), ((), ()))
        acc = lax.dot_general(a_ref[...], b_ref[...], dn, preferred_element_type=F32)
        res = epilogue(acc, *[e[...] for e in ex]) if epilogue is not None else (acc,)
        for o, r in zip(outs, res):
            o[...] = r.astype(o.dtype)

    tile = pl.BlockSpec((bm, bn), lambda i, j: (i, j))
    b_spec = pl.BlockSpec((bn, k), lambda i, j: (j, 0)) if trans_b else pl.BlockSpec((k, bn), lambda i, j: (0, j))
    outs = pl.pallas_call(
        body, name=name, grid=(m // bm, n // bn),
        in_specs=[pl.BlockSpec((bm, k), lambda i, j: (i, 0)), b_spec] + [tile] * n_ex,
        out_specs=[tile] * len(out_dtypes),
        out_shape=[jax.ShapeDtypeStruct((m, n), dt) for dt in out_dtypes],
        compiler_params=_params(("parallel", "parallel")),
    )(a, b, *extras)
    return outs


def mm_tn(a, b, *, name, bk=512, bn=512, bt=1024):
    t, k = a.shape
    n = b.shape[1]
    bk, bn, bt = min(bk, k), min(bn, n), min(bt, t)

    def body(a_ref, b_ref, o_ref):
        @pl.when(pl.program_id(2) == 0)
        def _():
            o_ref[...] = jnp.zeros_like(o_ref)

        o_ref[...] += lax.dot_general(a_ref[...], b_ref[...], TN, preferred_element_type=F32)

    return pl.pallas_call(
        body, name=name, grid=(k // bk, n // bn, t // bt),
        in_specs=[pl.BlockSpec((bt, bk), lambda i, j, s: (s, i)), pl.BlockSpec((bt, bn), lambda i, j, s: (s, j))],
        out_specs=pl.BlockSpec((bk, bn), lambda i, j, s: (i, j)),
        out_shape=jax.ShapeDtypeStruct((k, n), F32),
        compiler_params=_params(("parallel", "parallel", "arbitrary")),
    )(a, b)


def _split3(x):
    hi = x.astype(BF16)
    r1 = x - hi.astype(F32)
    mid = r1.astype(BF16)
    lo = (r1 - mid.astype(F32)).astype(BF16)
    return hi, mid, lo


def cumsum_rows(x, *, reverse, name, bc=512):
    t, d = x.shape
    bc = min(bc, t)
    nb = t // bc

    def body(x_ref, o_ref, carry):
        @pl.when(pl.program_id(0) == 0)
        def _():
            carry[...] = jnp.zeros_like(carry)

        r = lax.broadcasted_iota(jnp.int32, (bc, bc), 0)
        c = lax.broadcasted_iota(jnp.int32, (bc, bc), 1)
        tri = jnp.where((r <= c) if reverse else (r >= c), 1.0, 0.0).astype(BF16)
        hi, mid, lo = _split3(x_ref[...])
        dn = (((1,), (0,)), ((), ()))
        s = (lax.dot_general(tri, hi, dn, preferred_element_type=F32)
             + lax.dot_general(tri, mid, dn, preferred_element_type=F32)
             + lax.dot_general(tri, lo, dn, preferred_element_type=F32)) + carry[0:1, :]
        o_ref[...] = s
        carry[0:1, :] = s[0:1, :] if reverse else s[bc - 1:bc, :]

    imap = (lambda i: (nb - 1 - i, 0)) if reverse else (lambda i: (i, 0))
    return pl.pallas_call(
        body, name=name, grid=(nb,),
        in_specs=[pl.BlockSpec((bc, d), imap)], out_specs=pl.BlockSpec((bc, d), imap),
        out_shape=jax.ShapeDtypeStruct((t, d), F32),
        scratch_shapes=[pltpu.VMEM((8, d), F32)],
        compiler_params=_params(("arbitrary",)),
    )(x)


def _rope(x, c, a, b):
    return x * c + pltpu.roll(x, LANES - ROPE_DIM // 2, 1) * a + pltpu.roll(x, ROPE_DIM // 2, 1) * b


def _rope_bwd(d, c, a, b):
    return d * c + pltpu.roll(d * a, ROPE_DIM // 2, 1) + pltpu.roll(d * b, LANES - ROPE_DIM // 2, 1)


S_CQ, S_CKV, S_KR, S_F, S_END = 0, Q_RANK, Q_RANK + KV_RANK, Q_RANK + KV_RANK + LANES, 1024
HW = N_HEADS * LANES


def mla_prep(small, g_q, g_kv, w_uq, w_ukv, tab_c, tab_a, tab_b, b_f, *, name, bt=512):
    t = small.shape[0]
    bt = min(bt, t)

    def body(s_ref, gq_ref, gkv_ref, wq_ref, wkv_ref, c_ref, a_ref, b_ref, bf_ref,
             mq_ref, mk_ref, mv_ref, lf_ref, cqn_ref, ckvn_ref):
        cq = s_ref[:, S_CQ:S_CKV]
        rq = lax.rsqrt(jnp.mean(cq * cq, axis=-1, keepdims=True) + EPS)
        cqn = (cq * rq * gq_ref[...]).astype(BF16)
        ckv = s_ref[:, S_CKV:S_KR]
        rkv = lax.rsqrt(jnp.mean(ckv * ckv, axis=-1, keepdims=True) + EPS)
        ckvn = (ckv * rkv * gkv_ref[...]).astype(BF16)
        cqn_ref[...] = cqn
        ckvn_ref[...] = ckvn
        tc, ta, tb = c_ref[...], a_ref[...], b_ref[...]
        q = jnp.dot(cqn, wq_ref[...], preferred_element_type=F32)
        kv = jnp.dot(ckvn, wkv_ref[...], preferred_element_type=F32)
        kr = _rope(s_ref[:, S_KR:S_F], tc, ta, tb)
        for h in range(N_HEADS):
            sl = slice(h * LANES, (h + 1) * LANES)
            mq_ref[:, sl] = _rope(q[:, sl], tc, ta, tb).astype(BF16)
            mk_ref[:, sl] = (kv[:, sl] + kr).astype(BF16)
        mv_ref[...] = kv[:, HW:].astype(BF16)
        z = s_ref[:, S_F:S_END - LANES] + bf_ref[...]
        lf_ref[...] = jnp.minimum(z, 0.0) - jnp.log(1.0 + jnp.exp(-jnp.abs(z)))

    def row(w):
        return pl.BlockSpec((bt, w), lambda i: (i, 0))

    def full(arr):
        return pl.BlockSpec(arr.shape, lambda i: (0, 0))

    return pl.pallas_call(
        body, name=name, grid=(t // bt,),
        in_specs=[row(S_END), full(g_q), full(g_kv), full(w_uq), full(w_ukv), row(LANES), row(LANES), row(LANES), full(b_f)],
        out_specs=[row(HW), row(HW), row(HW), row(LANES), row(Q_RANK), row(KV_RANK)],
        out_shape=[jax.ShapeDtypeStruct((t, HW), BF16)] * 3 + [jax.ShapeDtypeStruct((t, LANES), F32),
                   jax.ShapeDtypeStruct((t, Q_RANK), BF16), jax.ShapeDtypeStruct((t, KV_RANK), BF16)],
        compiler_params=_params(("parallel",)),
    )(small, g_q, g_kv, w_uq, w_ukv, tab_c, tab_a, tab_b, b_f)


def mla_prep_bwd(dmq, dmk, dmv, dlf, small, g_q, g_kv, w_uq, w_ukv, tab_c, tab_a, tab_b, b_f, *, name, bt=512):
    t = small.shape[0]
    bt = min(bt, t)

    def body(dmq_ref, dmk_ref, dmv_ref, dlf_ref, s_ref, gq_ref, gkv_ref, wq_ref, wkv_ref, c_ref, a_ref, b_ref, bf_ref,
             ds_ref, dq_ref, dkv_ref, dgq_ref, dgkv_ref, db_ref):
        tc, ta, tb = c_ref[...], a_ref[...], b_ref[...]
        lane = lax.broadcasted_iota(jnp.int32, (1, LANES), 1)
        dkr = jnp.zeros((bt, LANES), F32)
        for h in range(N_HEADS):
            sl = slice(h * LANES, (h + 1) * LANES)
            dq_ref[:, sl] = _rope_bwd(dmq_ref[:, sl], tc, ta, tb).astype(BF16)
            dkr = dkr + dmk_ref[:, sl]
        dkv_ref[:, :HW] = dmk_ref[...].astype(BF16)
        dkv_ref[:, HW:] = dmv_ref[...].astype(BF16)
        in_rope = (lane >= HEAD_DIM) & (lane < HEAD_DIM + ROPE_DIM)
        ds_ref[:, S_KR:S_F] = jnp.where(in_rope, _rope_bwd(dkr, tc, ta, tb), 0.0).astype(BF16)

        def norm_bwd(raw, g_ref, dn, dg_ref):
            r = lax.rsqrt(jnp.mean(raw * raw, axis=-1, keepdims=True) + EPS)
            u = dn * g_ref[...]
            dot = jnp.mean(u * raw, axis=-1, keepdims=True)
            dg_ref[...] += jnp.sum(dn * (raw * r), axis=0, keepdims=True)
            return r * u - raw * (r * r * r * dot)

        @pl.when(pl.program_id(0) == 0)
        def _():
            dgq_ref[...] = jnp.zeros_like(dgq_ref)
            dgkv_ref[...] = jnp.zeros_like(dgkv_ref)
            db_ref[...] = jnp.zeros_like(db_ref)

        dcqn = lax.dot_general(dq_ref[...], wq_ref[...], NT, preferred_element_type=F32)
        ds_ref[:, S_CQ:S_CKV] = norm_bwd(s_ref[:, S_CQ:S_CKV], gq_ref, dcqn, dgq_ref).astype(BF16)
        dckvn = lax.dot_general(dkv_ref[...], wkv_ref[...], NT, preferred_element_type=F32)
        ds_ref[:, S_CKV:S_KR] = norm_bwd(s_ref[:, S_CKV:S_KR], gkv_ref, dckvn, dgkv_ref).astype(BF16)
        z = s_ref[:, S_F:S_END - LANES] + bf_ref[...]
        dz = jnp.where(lane < N_HEADS, dlf_ref[...] / (1.0 + jnp.exp(z)), 0.0)
        db_ref[...] += jnp.sum(dz, axis=0, keepdims=True)
        ds_ref[:, S_F:S_END - LANES] = dz.astype(BF16)
        ds_ref[:, S_END - LANES:] = jnp.zeros((bt, LANES), BF16)

    def row(w):
        return pl.BlockSpec((bt, w), lambda i: (i, 0))

    def full(arr):
        return pl.BlockSpec(arr.shape, lambda i: (0, 0))

    def vec(w):
        return pl.BlockSpec((1, w), lambda i: (0, 0))

    return pl.pallas_call(
        body, name=name, grid=(t // bt,),
        in_specs=[row(HW), row(HW), row(HW), row(LANES), row(S_END), full(g_q), full(g_kv), full(w_uq), full(w_ukv),
                  row(LANES), row(LANES), row(LANES), full(b_f)],
        out_specs=[row(S_END), row(HW), row(2 * HW), vec(Q_RANK), vec(KV_RANK), vec(LANES)],
        out_shape=[jax.ShapeDtypeStruct((t, S_END), BF16), jax.ShapeDtypeStruct((t, HW), BF16),
                   jax.ShapeDtypeStruct((t, 2 * HW), BF16), jax.ShapeDtypeStruct((1, Q_RANK), F32),
                   jax.ShapeDtypeStruct((1, KV_RANK), F32), jax.ShapeDtypeStruct((1, LANES), F32)],
        compiler_params=_params(("arbitrary",)),
    )(dmq, dmk, dmv, dlf, small, g_q, g_kv, w_uq, w_ukv, tab_c, tab_a, tab_b, b_f)


def _count_ge(t, starts):
    n = 0
    for s in starts:
        n = n + (t >= s).astype(jnp.int32)
    return n


def _tri_rows(t, nb):
    i = _count_ge(t, [(m * (m + 1)) // 2 for m in range(1, nb)])
    return i, t - (i * (i + 1)) // 2


def _tri_cols(t, nb):
    j = _count_ge(t, [m * nb - (m * (m - 1)) // 2 for m in range(1, nb)])
    return j, j + t - (j * nb - (j * (j - 1)) // 2)


def _head_mask(h, pair):
    lane = lax.broadcasted_iota(jnp.int32, (1, LANES), 1)
    if not pair:
        return lane >= 0
    return jnp.where(lane >= HEAD_DIM, 1, 0) == h % 2


def _pick_lane(tile, h):
    lane = lax.broadcasted_iota(jnp.int32, (1, LANES), 1)
    return jnp.sum(jnp.where(lane == h, tile, 0.0), axis=1, keepdims=True)


def _pick_row(tile, h):
    row = lax.broadcasted_iota(jnp.int32, (tile.shape[0], 1), 0)
    return jnp.sum(jnp.where(row == h, tile, 0.0), axis=0, keepdims=True)


def _logits(q, k, scale, i, j, blk, bias):
    s = lax.dot_general(q, k, NT, preferred_element_type=F32) * scale
    if bias is not None:
        s = s + bias
    row = lax.broadcasted_iota(jnp.int32, (blk, blk), 0)
    col = lax.broadcasted_iota(jnp.int32, (blk, blk), 1)
    return jnp.where(j * blk + col <= i * blk + row, s, -jnp.inf)


def flash_fwd(q_arr, k_arr, v_arr, f_cum, f_cum_t, *, qoff, koff, voff, pair, scale, name, blk=512):
    t = q_arr.shape[0]
    blk = min(blk, t)
    nb = t // blk
    has_bias = f_cum is not None

    def hcol(h):
        return h // 2 if pair else h

    def body(*refs):
        if has_bias:
            q_ref, k_ref, v_ref, f_ref, ft_ref, o_ref, lse_ref, m_sc, l_sc, acc_sc = refs
        else:
            q_ref, k_ref, v_ref, o_ref, lse_ref, m_sc, l_sc, acc_sc = refs
        h = pl.program_id(0)
        i, j = _tri_rows(pl.program_id(1), nb)

        @pl.when(j == 0)
        def _():
            m_sc[...] = jnp.full_like(m_sc, -jnp.inf)
            l_sc[...] = jnp.zeros_like(l_sc)
            acc_sc[...] = jnp.zeros_like(acc_sc)

        hm = _head_mask(h, pair)
        q = jnp.where(hm, q_ref[...], jnp.zeros_like(q_ref[...]))
        bias = (_pick_lane(f_ref[...], h) - _pick_row(ft_ref[...], h)) if has_bias else None
        s = _logits(q, k_ref[...], scale, i, j, blk, bias)
        m_new = jnp.maximum(m_sc[...], jnp.max(s, axis=1, keepdims=True))
        alpha = jnp.exp(m_sc[...] - m_new)
        p = jnp.exp(s - m_new)
        l_sc[...] = alpha * l_sc[...] + jnp.sum(p, axis=1, keepdims=True)
        acc_sc[...] = alpha * acc_sc[...] + jnp.dot(p.astype(BF16), v_ref[...], preferred_element_type=F32)
        m_sc[...] = m_new

        @pl.when(j == i)
        def _():
            o_ref[...] = jnp.where(hm, acc_sc[...] / l_sc[...], 0.0)
            lse_ref[0] = m_sc[...] + jnp.log(l_sc[...])

    def qmap(off):
        return lambda h, s: (_tri_rows(s, nb)[0], off + hcol(h))

    def kmap(off):
        return lambda h, s: (_tri_rows(s, nb)[1], off + hcol(h))

    in_specs = [pl.BlockSpec((blk, LANES), qmap(qoff)), pl.BlockSpec((blk, LANES), kmap(koff)),
                pl.BlockSpec((blk, LANES), kmap(voff))]
    ins = [q_arr, k_arr, v_arr]
    if has_bias:
        in_specs += [pl.BlockSpec((blk, LANES), lambda h, s: (_tri_rows(s, nb)[0], 0)),
                     pl.BlockSpec((N_HEADS, blk), lambda h, s: (0, _tri_rows(s, nb)[1]))]
        ins += [f_cum, f_cum_t]
    return pl.pallas_call(
        body, name=name, grid=(N_HEADS, nb * (nb + 1) // 2),
        in_specs=in_specs,
        out_specs=[pl.BlockSpec((blk, LANES), lambda h, s: (_tri_rows(s, nb)[0], h)),
                   pl.BlockSpec((1, blk, 1), lambda h, s: (h, _tri_rows(s, nb)[0], 0))],
        out_shape=[jax.ShapeDtypeStruct((t, HW), F32), jax.ShapeDtypeStruct((N_HEADS, t, 1), F32)],
        scratch_shapes=[pltpu.VMEM((blk, 1), F32), pltpu.VMEM((blk, 1), F32), pltpu.VMEM((blk, LANES), F32)],
        compiler_params=_params(("parallel", "arbitrary")),
    )(*ins)


def flash_bwd(q_arr, k_arr, v_arr, o_arr, do_arr, lse, f_cum, f_cum_t, *, qoff, koff, voff, pair, scale, name, blk=512):
    t = q_arr.shape[0]
    blk = min(blk, t)
    nb = t // blk
    has_bias = f_cum is not None

    def hcol(h):
        return h // 2 if pair else h

    def body(*refs):
        if has_bias:
            (q_ref, k_ref, v_ref, o_ref, do_ref, lse_ref, f_ref, ft_ref, dq_ref, dk_ref, dv_ref, df_ref, dfq_ref,
             dk_sc, dv_sc, df_sc) = refs
        else:
            q_ref, k_ref, v_ref, o_ref, do_ref, lse_ref, dq_ref, dk_ref, dv_ref, dk_sc, dv_sc = refs
        h = pl.program_id(0)
        j, i = _tri_cols(pl.program_id(1), nb)

        @pl.when(i == j)
        def _():
            dk_sc[...] = jnp.zeros_like(dk_sc)
            dv_sc[...] = jnp.zeros_like(dv_sc)
            if has_bias:
                df_sc[...] = jnp.zeros_like(df_sc)

        hm = _head_mask(h, pair)
        q = jnp.where(hm, q_ref[...], jnp.zeros_like(q_ref[...]))
        k = k_ref[...]
        do = do_ref[...]
        bias = (_pick_lane(f_ref[...], h) - _pick_row(ft_ref[...], h)) if has_bias else None
        s = _logits(q, k, scale, i, j, blk, bias)
        p = jnp.exp(s - lse_ref[0])
        delta = jnp.sum(o_ref[...] * do.astype(F32), axis=1, keepdims=True)
        dp = lax.dot_general(do, v_ref[...], NT, preferred_element_type=F32)
        ds = p * (dp - delta)
        dsb = ds.astype(BF16)
        dv_sc[...] += lax.dot_general(p.astype(BF16), do, TN, preferred_element_type=F32)
        dk_sc[...] += lax.dot_general(dsb, q, TN, preferred_element_type=F32) * scale
        dq = jnp.where(hm, jnp.dot(dsb, k, preferred_element_type=F32) * scale, 0.0)
        rows = pl.ds(pl.multiple_of(i * blk, blk), blk)

        row_sum = jnp.sum(ds, axis=1, keepdims=True) if has_bias else None

        @pl.when(j == 0)
        def _():
            dq_ref[rows, :] = dq
            if has_bias:
                dfq_ref[0, rows, :] = row_sum

        @pl.when(j > 0)
        def _():
            dq_ref[rows, :] += dq
            if has_bias:
                dfq_ref[0, rows, :] += row_sum

        if has_bias:
            df_sc[...] -= jnp.sum(ds, axis=0, keepdims=True)

        @pl.when(i == nb - 1)
        def _():
            dk_ref[...] = dk_sc[...]
            dv_ref[...] = dv_sc[...]
            if has_bias:
                df_ref[0] = df_sc[...]

    def qmap(off):
        return lambda h, s: (_tri_cols(s, nb)[1], off + hcol(h))

    def kmap(off):
        return lambda h, s: (_tri_cols(s, nb)[0], off + hcol(h))

    def own(pos):
        return lambda h, s: (_tri_cols(s, nb)[pos], h)

    in_specs = [pl.BlockSpec((blk, LANES), qmap(qoff)), pl.BlockSpec((blk, LANES), kmap(koff)),
                pl.BlockSpec((blk, LANES), kmap(voff)), pl.BlockSpec((blk, LANES), own(1)),
                pl.BlockSpec((blk, LANES), own(1)), pl.BlockSpec((1, blk, 1), lambda h, s: (h, _tri_cols(s, nb)[1], 0))]
    ins = [q_arr, k_arr, v_arr, o_arr, do_arr, lse]
    out_specs = [pl.BlockSpec((t, LANES), lambda h, s: (0, h)), pl.BlockSpec((blk, LANES), own(0)),
                 pl.BlockSpec((blk, LANES), own(0))]
    out_shape = [jax.ShapeDtypeStruct((t, HW), F32)] * 3
    scratch = [pltpu.VMEM((blk, LANES), F32), pltpu.VMEM((blk, LANES), F32)]
    if has_bias:
        in_specs += [pl.BlockSpec((blk, LANES), lambda h, s: (_tri_cols(s, nb)[1], 0)),
                     pl.BlockSpec((N_HEADS, blk), lambda h, s: (0, _tri_cols(s, nb)[0]))]
        ins += [f_cum, f_cum_t]
        out_specs += [pl.BlockSpec((1, 1, blk), lambda h, s: (h, 0, _tri_cols(s, nb)[0])),
                      pl.BlockSpec((1, t, 1), lambda h, s: (h, 0, 0))]
        out_shape += [jax.ShapeDtypeStruct((N_HEADS, 1, t), F32), jax.ShapeDtypeStruct((N_HEADS, t, 1), F32)]
        scratch += [pltpu.VMEM((1, blk), F32)]
    return pl.pallas_call(
        body, name=name, grid=(N_HEADS, nb * (nb + 1) // 2),
        in_specs=in_specs, out_specs=out_specs, out_shape=out_shape, scratch_shapes=scratch,
        compiler_params=_params(("parallel", "arbitrary")),
    )(*ins)


def fold_pairs(dq, dk, dv, *, name, bt=512):
    t = dq.shape[0]
    bt = min(bt, t)
    half = N_HEADS // 2

    def body(dq_ref, dk_ref, dv_ref, o_ref):
        for n, ref in enumerate((dq_ref, dk_ref, dv_ref)):
            for m in range(half):
                lo = ref[:, (2 * m) * LANES:(2 * m + 1) * LANES]
                hi = ref[:, (2 * m + 1) * LANES:(2 * m + 2) * LANES]
                o_ref[:, (n * half + m) * LANES:(n * half + m + 1) * LANES] = (lo + hi).astype(BF16)

    row = pl.BlockSpec((bt, HW), lambda i: (i, 0))
    return pl.pallas_call(
        body, name=name, grid=(t // bt,), in_specs=[row] * 3,
        out_specs=pl.BlockSpec((bt, 3 * half * LANES), lambda i: (i, 0)),
        out_shape=jax.ShapeDtypeStruct((t, 3 * half * LANES), BF16),
        compiler_params=_params(("parallel",)),
    )(dq, dk, dv)


def loss_head(x2, g, tgt, *, name, bt=512):
    t, d = x2.shape
    bt = min(bt, t)

    def body(x_ref, g_ref, t_ref, loss_ref, dx_ref, dxb_ref, dg_ref):
        @pl.when(pl.program_id(0) == 0)
        def _():
            loss_ref[...] = jnp.zeros_like(loss_ref)
            dg_ref[...] = jnp.zeros_like(dg_ref)

        xv = x_ref[...]
        gv = g_ref[...]
        r = lax.rsqrt(jnp.mean(xv * xv, axis=-1, keepdims=True) + EPS)
        xh = xv * r
        e = xh * gv - t_ref[...]
        loss_ref[...] += 0.5 * jnp.sum(jnp.mean(e * e, axis=-1, keepdims=True), axis=0, keepdims=True)
        dy = e * (1.0 / d)
        dg_ref[...] += jnp.sum(dy * xh, axis=0, keepdims=True)
        u = dy * gv
        dx = r * u - xv * (r * r * r * jnp.mean(u * xv, axis=-1, keepdims=True))
        dx_ref[...] = dx
        dxb_ref[...] = dx.astype(BF16)

    row = pl.BlockSpec((bt, d), lambda i: (i, 0))
    vec = pl.BlockSpec((1, d), lambda i: (0, 0))
    return pl.pallas_call(
        body, name=name, grid=(t // bt,), in_specs=[row, vec, row],
        out_specs=[pl.BlockSpec((1, 1), lambda i: (0, 0)), row, row, vec],
        out_shape=[jax.ShapeDtypeStruct((1, 1), F32), jax.ShapeDtypeStruct((t, d), F32),
                   jax.ShapeDtypeStruct((t, d), BF16), jax.ShapeDtypeStruct((1, d), F32)],
        compiler_params=_params(("arbitrary",)),
    )(x2, g, tgt)


def adamw(w, g, m, v, *, name):
    rws, cols = w.shape
    br = next(b for b in (256, 128, 64, 32, 16, 8) if rws % b == 0)

    def body(w_ref, g_ref, m_ref, v_ref, d_ref, nm_ref, nv_ref):
        gv = g_ref[...]
        nm = ADAM_B1 * m_ref[...] + (1.0 - ADAM_B1) * gv
        nv = ADAM_B2 * v_ref[...] + (1.0 - ADAM_B2) * (gv * gv)
        m_hat = nm / (1.0 - ADAM_B1 ** ADAM_STEP)
        v_hat = nv / (1.0 - ADAM_B2 ** ADAM_STEP)
        d_ref[...] = -ADAM_LR * (m_hat / (jnp.sqrt(v_hat) + ADAM_EPS) + ADAM_WD * w_ref[...])
        nm_ref[...] = nm
        nv_ref[...] = nv

    blk = pl.BlockSpec((br, cols), lambda i: (i, 0))
    return pl.pallas_call(
        body, name=name, grid=(rws // br,), in_specs=[blk] * 4, out_specs=[blk] * 3,
        out_shape=[jax.ShapeDtypeStruct((rws, cols), F32)] * 3,
        compiler_params=_params(("parallel",)),
    )(w, g, m, v)


def add_pair(a, b, *, name):
    n, r, c = a.shape
    br = r // 2

    def body(a_ref, b_ref, o_ref):
        o_ref[...] = (a_ref[...].astype(F32) + b_ref[...].astype(F32)).astype(o_ref.dtype)

    blk = pl.BlockSpec((1, br, c), lambda q, i: (q, i, 0))
    return pl.pallas_call(
        body, name=name, grid=(n, r // br), in_specs=[blk, blk], out_specs=blk,
        out_shape=jax.ShapeDtypeStruct((n, r, c), BF16),
        compiler_params=_params(("parallel", "parallel")),
    )(a, b)


def sum_chips(parts, *, name):
    n, r, c = parts.shape
    br = r // 2

    def body(p_ref, o_ref):
        acc = p_ref[0].astype(F32)
        for q in range(1, n):
            acc = acc + p_ref[q].astype(F32)
        o_ref[...] = acc

    return pl.pallas_call(
        body, name=name, grid=(r // br,),
        in_specs=[pl.BlockSpec((n, br, c), lambda i: (0, i, 0))], out_specs=pl.BlockSpec((br, c), lambda i: (i, 0)),
        out_shape=jax.ShapeDtypeStruct((r, c), F32),
        compiler_params=_params(("parallel",)),
    )(parts)


ANY = pl.BlockSpec(memory_space=pl.ANY)


def _place():
    x, y, c = lax.axis_index("x"), lax.axis_index("y"), lax.axis_index("c")
    chips = [(1 - x, y), (x, 1 - y), (1 - x, 1 - y)]
    return x, y, c, chips


def gather_shards(wpack):
    r, cdim = wpack.shape
    hr = r // 2

    def body(w_ref, out_ref, send_sems, recv_sems, local_sem):
        x, y, c, chips = _place()
        q = 2 * x + y
        sibling = (x, y, 1 - c)

        def half(qq, hf):
            return out_ref.at[qq, pl.ds(hf * hr, hr), :]

        def copy(k, src, dst, to):
            return pltpu.make_async_remote_copy(src_ref=src, dst_ref=dst, send_sem=send_sems.at[k],
                                                recv_sem=recv_sems.at[k], device_id=to, device_id_type=MESH)

        mine = pltpu.make_async_copy(w_ref, out_ref.at[q], local_sem)
        mine.start()
        first = [copy(j, w_ref.at[pl.ds(c * hr, hr), :], half(q, c), (cx, cy, c)) for j, (cx, cy) in enumerate(chips)]
        for cp in first:
            cp.start()
        passed = []
        for j, (cx, cy) in enumerate(chips):
            qj = 2 * cx + cy
            copy(j, half(qj, c), half(qj, c), (cx, cy, c)).wait_recv()
            fw = copy(3 + j, half(qj, c), half(qj, c), sibling)
            fw.start()
            passed.append(fw)
        for j, (cx, cy) in enumerate(chips):
            qj = 2 * cx + cy
            copy(3 + j, half(qj, 1 - c), half(qj, 1 - c), sibling).wait_recv()
        for cp in first + passed:
            cp.wait_send()
        mine.wait()

    return pl.pallas_call(
        body, name="gather_shards", in_specs=[ANY], out_specs=ANY,
        out_shape=jax.ShapeDtypeStruct((N_CHIPS, r, cdim), wpack.dtype),
        scratch_shapes=[pltpu.SemaphoreType.DMA((6,)), pltpu.SemaphoreType.DMA((6,)), pltpu.SemaphoreType.DMA],
    )(wpack)


def swap_halves(g2):
    def body(g_ref, out_ref, send_sem, recv_sem):
        x, y, c, _ = _place()
        cp = pltpu.make_async_remote_copy(src_ref=g_ref.at[1 - c], dst_ref=out_ref, send_sem=send_sem, recv_sem=recv_sem,
                                          device_id=(x, y, 1 - c), device_id_type=MESH)
        cp.start()
        cp.wait()

    return pl.pallas_call(
        body, name="swap_halves", in_specs=[ANY], out_specs=ANY,
        out_shape=jax.ShapeDtypeStruct(g2.shape[1:], g2.dtype),
        scratch_shapes=[pltpu.SemaphoreType.DMA, pltpu.SemaphoreType.DMA],
    )(g2)


def scatter_to_owners(p):
    def body(p_ref, out_ref, send_sems, recv_sems, local_sem):
        x, y, c, chips = _place()
        q = 2 * x + y
        mine = pltpu.make_async_copy(p_ref.at[q], out_ref.at[q], local_sem)
        mine.start()
        sends = []
        for j, (cx, cy) in enumerate(chips):
            cp = pltpu.make_async_remote_copy(src_ref=p_ref.at[2 * cx + cy], dst_ref=out_ref.at[q], send_sem=send_sems.at[j],
                                              recv_sem=recv_sems.at[j], device_id=(cx, cy, c), device_id_type=MESH)
            cp.start()
            sends.append(cp)
        for j, (cx, cy) in enumerate(chips):
            qj = 2 * cx + cy
            pltpu.make_async_remote_copy(src_ref=p_ref.at[qj], dst_ref=out_ref.at[qj], send_sem=send_sems.at[j],
                                         recv_sem=recv_sems.at[j], device_id=(cx, cy, c), device_id_type=MESH).wait_recv()
        for cp in sends:
            cp.wait_send()
        mine.wait()

    return pl.pallas_call(
        body, name="scatter_to_owners", in_specs=[ANY], out_specs=ANY,
        out_shape=jax.ShapeDtypeStruct(p.shape, p.dtype),
        scratch_shapes=[pltpu.SemaphoreType.DMA((3,)), pltpu.SemaphoreType.DMA((3,)), pltpu.SemaphoreType.DMA],
    )(p)


def join_halves(rh):
    def body(r_ref, out_ref, send_sem, recv_sem, local_sem):
        x, y, c, _ = _place()
        mine = pltpu.make_async_copy(r_ref, out_ref.at[c], local_sem)
        mine.start()
        cp = pltpu.make_async_remote_copy(src_ref=r_ref, dst_ref=out_ref.at[c], send_sem=send_sem, recv_sem=recv_sem,
                                          device_id=(x, y, 1 - c), device_id_type=MESH)
        cp.start()
        pltpu.make_async_remote_copy(src_ref=r_ref, dst_ref=out_ref.at[1 - c], send_sem=send_sem, recv_sem=recv_sem,
                                     device_id=(x, y, 1 - c), device_id_type=MESH).wait_recv()
        cp.wait_send()
        mine.wait()

    return pl.pallas_call(
        body, name="join_halves", in_specs=[ANY], out_specs=ANY,
        out_shape=jax.ShapeDtypeStruct((2,) + rh.shape, rh.dtype),
        scratch_shapes=[pltpu.SemaphoreType.DMA, pltpu.SemaphoreType.DMA, pltpu.SemaphoreType.DMA],
    )(rh)


def allreduce_small(s):
    n_dev = 8

    def body(s_ref, out_ref, buf, send_sems, recv_sems):
        x, y, c, _ = _place()
        me = 4 * x + 2 * y + c
        buf[me] = s_ref[...]
        sends = []
        for k in range(1, n_dev):
            px = 1 - x if k & 4 else x
            py = 1 - y if k & 2 else y
            pc = 1 - c if k & 1 else c
            cp = pltpu.make_async_remote_copy(src_ref=s_ref, dst_ref=buf.at[me], send_sem=send_sems.at[k - 1],
                                              recv_sem=recv_sems.at[k - 1], device_id=(px, py, pc), device_id_type=MESH)
            cp.start()
            sends.append((cp, 4 * px + 2 * py + pc))
        for k, (cp, peer) in enumerate(sends):
            pltpu.make_async_remote_copy(src_ref=s_ref, dst_ref=buf.at[peer], send_sem=send_sems.at[k],
                                         recv_sem=recv_sems.at[k], device_id=(x, y, c), device_id_type=MESH).wait_recv()
        for cp, _ in sends:
            cp.wait_send()
        acc = buf[0]
        for d in range(1, n_dev):
            acc = acc + buf[d]
        out_ref[...] = acc

    vm = pl.BlockSpec(memory_space=pltpu.VMEM)
    return pl.pallas_call(
        body, name="allreduce_small", in_specs=[vm], out_specs=vm,
        out_shape=jax.ShapeDtypeStruct(s.shape, F32),
        scratch_shapes=[pltpu.VMEM((n_dev,) + s.shape, F32), pltpu.SemaphoreType.DMA((n_dev - 1,)),
                        pltpu.SemaphoreType.DMA((n_dev - 1,))],
    )(s)


PACK_COLS = 1024
COL_SHARDED = (True, True, True, False, True, False)


def _pack_rows(shapes):
    rows = [s[0] * s[1] // PACK_COLS for s in shapes]
    total = sum(rows)
    return rows, -(-total // 16) * 16


def pack_shards(shards):
    rows, padded = _pack_rows([s.shape for s in shards])
    parts = [s.reshape(r, PACK_COLS) for s, r in zip(shards, rows)]
    parts.append(jnp.zeros((padded - sum(rows), PACK_COLS), shards[0].dtype))
    return jnp.concatenate(parts, axis=0)


def unpack_shards(pack, shapes):
    rows, _ = _pack_rows(shapes)
    out, at = [], 0
    for s, r in zip(shapes, rows):
        out.append(pack[at:at + r].reshape(s))
        at += r
    return out


def unpack_full(wall, shapes):
    rows, _ = _pack_rows(shapes)
    out, at = [], 0
    for (k, n), r, by_col in zip(shapes, rows, COL_SHARDED):
        blk = wall[:, at:at + r].reshape(N_CHIPS, k, n)
        out.append(blk.transpose(1, 0, 2).reshape(k, N_CHIPS * n) if by_col else blk.reshape(N_CHIPS * k, n))
        at += r
    return out


def pack_full(fulls, shapes, dtype):
    rows, padded = _pack_rows(shapes)
    parts = []
    for f, (k, n), r, by_col in zip(fulls, shapes, rows, COL_SHARDED):
        blk = f.reshape(k, N_CHIPS, n).transpose(1, 0, 2) if by_col else f.reshape(N_CHIPS, k, n)
        parts.append(blk.reshape(N_CHIPS, r, PACK_COLS).astype(dtype))
    parts.append(jnp.zeros((N_CHIPS, padded - sum(rows), PACK_COLS), dtype))
    return jnp.concatenate(parts, axis=1)


def _pad_heads(w, width):
    lead = w.shape[:-1]
    w = w.reshape(lead + (N_HEADS, width))
    return jnp.pad(w, [(0, 0)] * len(lead) + [(0, 0), (0, LANES - width)]).reshape(lead + (HW,))


def _unpad_heads(w, width):
    lead = w.shape[:-1]
    return w.reshape(lead + (N_HEADS, LANES))[..., :width].reshape(lead + (N_HEADS * width,))


def _pair_rows(w):
    n = w.shape[1]
    w = w.reshape(N_HEADS // 2, 2, HEAD_DIM, n)
    z = jnp.zeros_like(w[:, 0])
    even = jnp.concatenate([w[:, 0], z], axis=1)
    odd = jnp.concatenate([z, w[:, 1]], axis=1)
    return jnp.stack([even, odd], axis=1).reshape(HW, n)


def _unpair_rows(w):
    n = w.shape[1]
    w = w.reshape(N_HEADS // 2, 2, 2, HEAD_DIM, n)
    return jnp.stack([w[:, 0, 0], w[:, 1, 1]], axis=1).reshape(N_HEADS * HEAD_DIM, n)


FW = N_HEADS * HEAD_DIM
IN_COLS = 3 * FW + N_HEADS + Q_RANK + KV_RANK + ROPE_DIM


def split_w_in(w_in):
    d = w_in.shape[0]
    o_f = 3 * FW
    o_cq = o_f + N_HEADS
    o_ckv = o_cq + Q_RANK
    o_kr = o_ckv + KV_RANK

    def z(n):
        return jnp.zeros((d, n), w_in.dtype)

    small = jnp.concatenate([w_in[:, o_cq:o_ckv], w_in[:, o_ckv:o_kr], z(HEAD_DIM), w_in[:, o_kr:], z(LANES - HEAD_DIM - ROPE_DIM),
                             w_in[:, o_f:o_cq], z(LANES - N_HEADS), z(LANES)], axis=1)
    return w_in[:, :o_f], small


def join_w_in(d_qkv, d_small):
    kr = S_KR + HEAD_DIM
    return jnp.concatenate([d_qkv, d_small[:, S_F:S_F + N_HEADS], d_small[:, S_CQ:S_CKV], d_small[:, S_CKV:S_KR],
                            d_small[:, kr:kr + ROPE_DIM]], axis=1)


def rope_tables(pos):
    t = pos.shape[0]
    inv_freq = ROPE_THETA ** (-jnp.arange(0, ROPE_DIM, 2, dtype=F32) / ROPE_DIM)
    ang = pos.astype(F32)[:, None] * inv_freq
    cos, sin = jnp.cos(ang), jnp.sin(ang)
    half = ROPE_DIM // 2

    def z(n):
        return jnp.zeros((t, n), F32)

    tab_c = jnp.concatenate([jnp.ones((t, HEAD_DIM), F32), cos, cos, z(LANES - HEAD_DIM - ROPE_DIM)], axis=1)
    tab_a = jnp.concatenate([z(HEAD_DIM), -sin, z(half), z(LANES - HEAD_DIM - ROPE_DIM)], axis=1)
    tab_b = jnp.concatenate([z(HEAD_DIM), z(half), sin, z(LANES - HEAD_DIM - ROPE_DIM)], axis=1)
    return tab_c, tab_a, tab_b


def _pad_lanes(v, n):
    return jnp.pad(v, ((0, 0), (0, n - v.shape[1])))


def local_step(xs, pos, tgt, gains, weights):
    g_attn, b_forget, g_q, g_kv, g_fo, g_mo, g_mlp, g_fin = gains
    w_in, w_uq, w_ukv, w_o, w_up, w_down = weights
    d = xs.shape[1]
    fox_scale = 1.0 / (HEAD_DIM ** 0.5)
    mla_scale = 1.0 / ((HEAD_DIM + ROPE_DIM) ** 0.5)

    w_qkv, w_small = split_w_in(w_in)
    w_uq_p = _pad_heads(w_uq, HEAD_DIM + ROPE_DIM)
    kv = w_ukv.reshape(KV_RANK, N_HEADS, 2 * HEAD_DIM)
    w_ukv_p = jnp.concatenate([_pad_heads(kv[:, :, :HEAD_DIM].reshape(KV_RANK, FW), HEAD_DIM),
                               _pad_heads(kv[:, :, HEAD_DIM:].reshape(KV_RANK, FW), HEAD_DIM)], axis=1)
    w_of = _pair_rows(w_o[:FW])
    w_om = _pad_heads(w_o[FW:].T, HEAD_DIM).T
    g_fo_p = _pair_rows(g_fo.T).T
    g_mo_p = _pad_heads(g_mo, HEAD_DIM)
    b_f = _pad_lanes(b_forget, LANES)
    tab_c, tab_a, tab_b = rope_tables(pos)

    h1 = rmsnorm(xs, g_attn, n_valid=d, out_dtype=BF16, name="norm_attn")
    qkv, = mm(h1, w_qkv, out_dtypes=[BF16], name="proj_qkv")
    small, = mm(h1, w_small, out_dtypes=[F32], name="proj_small")
    mq, mk, mv, lf, cqn, ckvn = mla_prep(small, g_q, g_kv, w_uq_p, w_ukv_p, tab_c, tab_a, tab_b, b_f, name="mla_prep")
    f_cum = cumsum_rows(lf, reverse=False, name="gate_cumsum")
    f_cum_t = f_cum[:, :N_HEADS].T
    half = N_HEADS // 2
    fo, lse_f = flash_fwd(qkv, qkv, qkv, f_cum, f_cum_t, qoff=0, koff=half, voff=2 * half, pair=True, scale=fox_scale,
                          name="fox_fwd")
    mo, lse_m = flash_fwd(mq, mk, mv, None, None, qoff=0, koff=0, voff=0, pair=False, scale=mla_scale, name="mla_fwd")
    mix_f = rmsnorm(fo, g_fo_p, n_valid=FW, out_dtype=BF16, name="norm_fox_out")
    mix_m = rmsnorm(mo, g_mo_p, n_valid=FW, out_dtype=BF16, name="norm_mla_out")

    def add_res(acc, res):
        return (acc + res,)

    x1a, = mm(mix_f, w_of, extras=[xs], epilogue=add_res, out_dtypes=[F32], name="out_proj_fox")
    x1, = mm(mix_m, w_om, extras=[x1a], epilogue=add_res, out_dtypes=[F32], name="out_proj_mla")
    h2 = rmsnorm(x1, g_mlp, n_valid=d, out_dtype=BF16, name="norm_mlp")

    def relu2(acc):
        r = jnp.maximum(acc, 0.0)
        return acc, r * r

    u, act = mm(h2, w_up, epilogue=relu2, out_dtypes=[BF16, BF16], name="mlp_up")
    x2, = mm(act, w_down, extras=[x1], epilogue=add_res, out_dtypes=[F32], name="mlp_down")
    loss, dx2, dx2b, dg_fin = loss_head(x2, g_fin, tgt, name="loss_head")

    def relu2_grad(acc, uu):
        return (acc * (2.0 * jnp.maximum(uu.astype(F32), 0.0)),)

    du, = mm(dx2b, w_down, trans_b=True, extras=[u], epilogue=relu2_grad, out_dtypes=[BF16], name="mlp_down_bwd")
    dw_down = mm_tn(act, dx2b, name="dw_down")
    dh2, = mm(du, w_up, trans_b=True, out_dtypes=[F32], name="mlp_up_bwd")
    dw_up = mm_tn(h2, du, name="dw_up")
    dx1, dx1b, dg_mlp = rmsnorm_bwd(dh2, x1, g_mlp, dx2, n_valid=d, out_dtypes=[F32, BF16], name="norm_mlp_bwd")

    dmix_f, = mm(dx1b, w_of, trans_b=True, out_dtypes=[F32], name="out_proj_fox_bwd")
    dmix_m, = mm(dx1b, w_om, trans_b=True, out_dtypes=[F32], name="out_proj_mla_bwd")
    dw_of = mm_tn(mix_f, dx1b, name="dw_o_fox")
    dw_om = mm_tn(mix_m, dx1b, name="dw_o_mla")
    dfo, dg_fo_p = rmsnorm_bwd(dmix_f, fo, g_fo_p, None, n_valid=FW, out_dtypes=[BF16], name="norm_fox_out_bwd")
    dmo, dg_mo_p = rmsnorm_bwd(dmix_m, mo, g_mo_p, None, n_valid=FW, out_dtypes=[BF16], name="norm_mla_out_bwd")
    dfq, dfk, dfv, dfc, dfr = flash_bwd(qkv, qkv, qkv, fo, dfo, lse_f, f_cum, f_cum_t, qoff=0, koff=half, voff=2 * half, pair=True,
                                   scale=fox_scale, name="fox_bwd")
    dmq, dmk, dmv = flash_bwd(mq, mk, mv, mo, dmo, lse_m, None, None, qoff=0, koff=0, voff=0, pair=False, scale=mla_scale,
                              name="mla_bwd")
    dqkv = fold_pairs(dfq, dfk, dfv, name="fox_fold")
    dlf = cumsum_rows(_pad_lanes(dfc[:, 0, :].T + dfr[:, :, 0].T, LANES), reverse=True, name="gate_cumsum_bwd")
    dsmall, dq_u, dkv_u, dg_q, dg_kv, db_f = mla_prep_bwd(dmq, dmk, dmv, dlf, small, g_q, g_kv, w_uq_p, w_ukv_p,
                                                          tab_c, tab_a, tab_b, b_f, name="mla_prep_bwd")
    dw_uq_p = mm_tn(cqn, dq_u, name="dw_uq")
    dw_ukv_p = mm_tn(ckvn, dkv_u, name="dw_ukv")

    dh1a, = mm(dqkv, w_qkv, trans_b=True, out_dtypes=[F32], name="proj_qkv_bwd")
    dh1, = mm(dsmall, w_small, trans_b=True, extras=[dh1a], epilogue=add_res, out_dtypes=[F32], name="proj_small_bwd")
    dw_qkv = mm_tn(h1, dqkv, name="dw_qkv")
    dw_small = mm_tn(h1, dsmall, name="dw_small")
    grad_x, dg_attn = rmsnorm_bwd(dh1, xs, g_attn, dx1, n_valid=d, out_dtypes=[F32], name="norm_attn_bwd")

    dw_in = join_w_in(dw_qkv, dw_small)
    dw_uq = _unpad_heads(dw_uq_p, HEAD_DIM + ROPE_DIM)
    dk_cols = _unpad_heads(dw_ukv_p[:, :HW], HEAD_DIM).reshape(KV_RANK, N_HEADS, HEAD_DIM)
    dv_cols = _unpad_heads(dw_ukv_p[:, HW:], HEAD_DIM).reshape(KV_RANK, N_HEADS, HEAD_DIM)
    dw_ukv = jnp.concatenate([dk_cols, dv_cols], axis=2).reshape(KV_RANK, N_HEADS * 2 * HEAD_DIM)
    dw_o = jnp.concatenate([_unpair_rows(dw_of), _unpad_heads(dw_om.T, HEAD_DIM).T], axis=0)
    dg_fo = _unpair_rows(dg_fo_p.T).T
    dg_mo = _unpad_heads(dg_mo_p, HEAD_DIM)
    d_gains = (dg_attn, db_f[:, :N_HEADS], dg_q, dg_kv, dg_fo, dg_mo, dg_mlp, dg_fin)
    return loss, grad_x, (dw_in, dw_uq, dw_ukv, dw_o, dw_up, dw_down), d_gains


def kernel(x, positions, attn_norm_g, w_in, b_forget, q_norm_g, w_uq, kv_norm_g, w_ukv, fox_out_g, mla_out_g, w_o, mlp_norm_g, w_up, w_down, final_norm_g, loss_target, m_attn_norm_g, m_w_in, m_b_forget, m_q_norm_g, m_w_uq, m_kv_norm_g, m_w_ukv, m_fox_out_g, m_mla_out_g, m_w_o, m_mlp_norm_g, m_w_up, m_w_down, m_final_norm_g, v_attn_norm_g, v_w_in, v_b_forget, v_q_norm_g, v_w_uq, v_kv_norm_g, v_w_ukv, v_fox_out_g, v_mla_out_g, v_w_o, v_mlp_norm_g, v_w_up, v_w_down, v_final_norm_g):
    c = lax.axis_index("c")
    big = [w_in[0], w_uq[0], w_ukv[0], w_o[0], w_up[0], w_down[0]]
    big_m = [m_w_in[0], m_w_uq[0], m_w_ukv[0], m_w_o[0], m_w_up[0], m_w_down[0]]
    big_v = [v_w_in[0], v_w_uq[0], v_w_ukv[0], v_w_o[0], v_w_up[0], v_w_down[0]]
    shapes = [w.shape for w in big]

    def vec(a):
        return a.reshape(1, -1)

    small = [attn_norm_g, b_forget, q_norm_g, kv_norm_g, fox_out_g, mla_out_g, mlp_norm_g, final_norm_g]
    small_m = [m_attn_norm_g, m_b_forget, m_q_norm_g, m_kv_norm_g, m_fox_out_g, m_mla_out_g, m_mlp_norm_g, m_final_norm_g]
    small_v = [v_attn_norm_g, v_b_forget, v_q_norm_g, v_kv_norm_g, v_fox_out_g, v_mla_out_g, v_mlp_norm_g, v_final_norm_g]
    gains = [vec(a) for a in small]

    wall = gather_shards(pack_shards([w.astype(BF16) for w in big]))
    weights = unpack_full(wall, shapes)

    loss, grad_x, d_big, d_small = local_step(x[0], positions[0], loss_target[0], gains, weights)

    gp = pack_full(d_big, shapes, BF16)
    hr = gp.shape[1] // 2
    g2 = gp.reshape(N_CHIPS, 2, hr, PACK_COLS).transpose(1, 0, 2, 3)
    from_sibling = swap_halves(g2)
    kept = lax.dynamic_index_in_dim(g2, c, 0, keepdims=False)
    pair_sum = add_pair(kept, from_sibling, name="add_pair")
    by_chip = scatter_to_owners(pair_sum)
    my_half = sum_chips(by_chip, name="sum_chips")
    g_shard = join_halves(my_half).reshape(2 * hr, PACK_COLS)
    g_big = unpack_shards(g_shard, shapes)

    def rows8(vs):
        return jnp.concatenate([_pad_lanes(vec(a).astype(F32), PACK_COLS) for a in vs], axis=0)

    g_small8 = allreduce_small(rows8(d_small))

    outs_big = [adamw(w, g, m, v, name="adamw_%d" % n) for n, (w, g, m, v) in enumerate(zip(big, g_big, big_m, big_v))]
    d8, m8, v8 = adamw(rows8(small), g_small8, rows8(small_m), rows8(small_v), name="adamw_small")

    def unrows8(a8):
        return [a8[n, :s.size].reshape(s.shape) for n, s in enumerate(small)]

    loss_all = lax.psum(loss[0, 0], ("x", "y", "c"))
    lead = lambda a: a[None]
    order = [0, 1, 2, 3, 4, 5]
    grads, deltas, new_m, new_v = [None] * 14, [None] * 14, [None] * 14, [None] * 14
    big_at = [1, 4, 6, 9, 11, 12]
    small_at = [0, 2, 3, 5, 7, 8, 10, 13]
    for n, at in zip(order, big_at):
        grads[at] = lead(g_big[n])
        deltas[at], new_m[at], new_v[at] = (lead(a) for a in outs_big[n])
    for n, (at, g, dd, mm_, vv) in enumerate(zip(small_at, unrows8(g_small8), unrows8(d8), unrows8(m8), unrows8(v8))):
        grads[at], deltas[at], new_m[at], new_v[at] = g, dd, mm_, vv
    return (loss_all, grad_x[None], *grads, *deltas, *new_m, *new_v)
```

```python
import jax
import jax.numpy as jnp
from jax import lax
from jax.experimental import pallas as pl
from jax.experimental.pallas import tpu as pltpu

F32 = jnp.float32
BF16 = jnp.bfloat16
MESH = pl.DeviceIdType.MESH

EPS = 1e-6
ROPE_THETA = 10000.0
N_HEADS = 8
PAIRS = N_HEADS // 2
HEAD_DIM = 64
ROPE_DIM = 32
LANES = 128
Q_RANK = 384
KV_RANK = 256
N_CHIPS = 4
ADAM_LR, ADAM_B1, ADAM_B2, ADAM_EPS, ADAM_WD, ADAM_STEP = 0.001, 0.9, 0.999, 1e-08, 0.01, 10
VMEM_LIMIT = 48 * 1024 * 1024
LOG2E = 1.4426950408889634
LN2 = 0.6931471805599453
NN = (((1,), (0,)), ((), ()))
NT = (((1,), (1,)), ((), ()))
TN = (((0,), (0,)), ((), ()))


def _params(sem=None):
    return pltpu.CompilerParams(dimension_semantics=sem, vmem_limit_bytes=VMEM_LIMIT)


def rmsnorm(x, g, *, out_dtype, name, bt=512):
    t, d = x.shape
    bt = min(bt, t)

    def body(x_ref, g_ref, o_ref):
        xv = x_ref[...].astype(F32)
        r = lax.rsqrt(jnp.mean(xv * xv, axis=-1, keepdims=True) + EPS)
        o_ref[...] = (xv * r * g_ref[...]).astype(o_ref.dtype)

    return pl.pallas_call(
        body, name=name, grid=(t // bt,),
        in_specs=[pl.BlockSpec((bt, d), lambda i: (i, 0)), pl.BlockSpec((1, d), lambda i: (0, 0))],
        out_specs=pl.BlockSpec((bt, d), lambda i: (i, 0)),
        out_shape=jax.ShapeDtypeStruct((t, d), out_dtype),
        compiler_params=_params(("parallel",)),
    )(x, g)


def rmsnorm_bwd(dh, x, g, res, *, out_dtypes, name, bt=512):
    t, d = x.shape
    bt = min(bt, t)
    has_res = res is not None

    def body(*refs):
        dh_ref, x_ref, g_ref = refs[:3]
        res_ref = refs[3] if has_res else None
        outs = refs[3 + has_res:]
        dx_refs, dg_ref = outs[:-1], outs[-1]
        xv = x_ref[...].astype(F32)
        dhv = dh_ref[...].astype(F32)
        r = lax.rsqrt(jnp.mean(xv * xv, axis=-1, keepdims=True) + EPS)
        u = dhv * g_ref[...]
        dot = jnp.mean(u * xv, axis=-1, keepdims=True)
        dx = r * u - xv * (r * r * r * dot)
        if has_res:
            dx = dx + res_ref[...]
        for o in dx_refs:
            o[...] = dx.astype(o.dtype)

        @pl.when(pl.program_id(0) == 0)
        def _():
            dg_ref[...] = jnp.zeros_like(dg_ref)

        dg_ref[...] += jnp.sum(dhv * (xv * r), axis=0, keepdims=True)

    row = pl.BlockSpec((bt, d), lambda i: (i, 0))
    vec = pl.BlockSpec((1, d), lambda i: (0, 0))
    ins = [dh, x, g] + ([res] if has_res else [])
    return pl.pallas_call(
        body, name=name, grid=(t // bt,),
        in_specs=[row, row, vec] + ([row] if has_res else []),
        out_specs=[row] * len(out_dtypes) + [vec],
        out_shape=[jax.ShapeDtypeStruct((t, d), dt) for dt in out_dtypes] + [jax.ShapeDtypeStruct((1, d), F32)],
        compiler_params=_params(("arbitrary",)),
    )(*ins)


def mm(a, b, *, trans_b=False, extras=(), epilogue=None, out_dtypes, name, bm=512, bn=512):
    m, k = a.shape
    n = b.shape[0] if trans_b else b.shape[1]
    bm, bn = min(bm, m), min(bn, n)
    n_ex = len(extras)

    def body(*refs):
        a_ref, b_ref = refs[0], refs[1]
        ex = refs[2:2 + n_ex]
        outs = refs[2 + n_ex:]
        acc = lax.dot_general(a_ref[...], b_ref[...], NT if trans_b else NN, preferred_element_type=F32)
        res = epilogue(acc, *[e[...] for e in ex]) if epilogue is not None else (acc,)
        for o, r in zip(outs, res):
            o[...] = r.astype(o.dtype)

    tile = pl.BlockSpec((bm, bn), lambda i, j: (i, j))
    b_spec = pl.BlockSpec((bn, k), lambda i, j: (j, 0)) if trans_b else pl.BlockSpec((k, bn), lambda i, j: (0, j))
    return pl.pallas_call(
        body, name=name, grid=(m // bm, n // bn),
        in_specs=[pl.BlockSpec((bm, k), lambda i, j: (i, 0)), b_spec] + [tile] * n_ex,
        out_specs=[tile] * len(out_dtypes),
        out_shape=[jax.ShapeDtypeStruct((m, n), dt) for dt in out_dtypes],
        compiler_params=_params(("parallel", "parallel")),
    )(a, b, *extras)


def mm_tn(a, b, *, name, bk=512, bn=512, bt=1024):
    t, k = a.shape
    n = b.shape[1]
    bk, bn, bt = min(bk, k), min(bn, n), min(bt, t)

    def body(a_ref, b_ref, o_ref):
        @pl.when(pl.program_id(2) == 0)
        def _():
            o_ref[...] = jnp.zeros_like(o_ref)

        o_ref[...] += lax.dot_general(a_ref[...], b_ref[...], TN, preferred_element_type=F32)

    return pl.pallas_call(
        body, name=name, grid=(k // bk, n // bn, t // bt),
        in_specs=[pl.BlockSpec((bt, bk), lambda i, j, s: (s, i)), pl.BlockSpec((bt, bn), lambda i, j, s: (s, j))],
        out_specs=pl.BlockSpec((bk, bn), lambda i, j, s: (i, j)),
        out_shape=jax.ShapeDtypeStruct((k, n), F32),
        compiler_params=_params(("parallel", "parallel", "arbitrary")),
    )(a, b)


def _split3(x):
    hi = x.astype(BF16)
    r1 = x - hi.astype(F32)
    mid = r1.astype(BF16)
    lo = (r1 - mid.astype(F32)).astype(BF16)
    return hi, mid, lo


def cumsum_rows(x, *, reverse, name, bc=512):
    t, d = x.shape
    bc = min(bc, t)
    nb = t // bc

    def body(x_ref, o_ref, carry):
        @pl.when(pl.program_id(0) == 0)
        def _():
            carry[...] = jnp.zeros_like(carry)

        r = lax.broadcasted_iota(jnp.int32, (bc, bc), 0)
        c = lax.broadcasted_iota(jnp.int32, (bc, bc), 1)
        tri = jnp.where((r <= c) if reverse else (r >= c), 1.0, 0.0).astype(BF16)
        hi, mid, lo = _split3(x_ref[...])
        s = (lax.dot_general(tri, hi, NN, preferred_element_type=F32)
             + lax.dot_general(tri, mid, NN, preferred_element_type=F32)
             + lax.dot_general(tri, lo, NN, preferred_element_type=F32)) + carry[0:1, :]
        o_ref[...] = s
        carry[0:1, :] = s[0:1, :] if reverse else s[bc - 1:bc, :]

    imap = (lambda i: (nb - 1 - i, 0)) if reverse else (lambda i: (i, 0))
    return pl.pallas_call(
        body, name=name, grid=(nb,),
        in_specs=[pl.BlockSpec((bc, d), imap)], out_specs=pl.BlockSpec((bc, d), imap),
        out_shape=jax.ShapeDtypeStruct((t, d), F32),
        scratch_shapes=[pltpu.VMEM((8, d), F32)],
        compiler_params=_params(("arbitrary",)),
    )(x)


def _rope(x, c, a, b):
    return x * c + pltpu.roll(x, LANES - ROPE_DIM // 2, 1) * a + pltpu.roll(x, ROPE_DIM // 2, 1) * b


def _rope_bwd(d, c, a, b):
    return d * c + pltpu.roll(d * a, ROPE_DIM // 2, 1) + pltpu.roll(d * b, LANES - ROPE_DIM // 2, 1)


S_CQ, S_CKV, S_KR, S_F, S_END = 0, Q_RANK, Q_RANK + KV_RANK, Q_RANK + KV_RANK + LANES, 1024
HW = N_HEADS * LANES


def mla_prep(small, g_q, g_kv, w_uq, w_ukv, tab_c, tab_a, tab_b, b_f, *, name, bt=512):
    t = small.shape[0]
    bt = min(bt, t)

    def body(s_ref, gq_ref, gkv_ref, wq_ref, wkv_ref, c_ref, a_ref, b_ref, bf_ref,
             mq_ref, mk_ref, mv_ref, lf_ref, cqn_ref, ckvn_ref):
        cq = s_ref[:, S_CQ:S_CKV]
        rq = lax.rsqrt(jnp.mean(cq * cq, axis=-1, keepdims=True) + EPS)
        cqn = (cq * rq * gq_ref[...]).astype(BF16)
        ckv = s_ref[:, S_CKV:S_KR]
        rkv = lax.rsqrt(jnp.mean(ckv * ckv, axis=-1, keepdims=True) + EPS)
        ckvn = (ckv * rkv * gkv_ref[...]).astype(BF16)
        cqn_ref[...] = cqn
        ckvn_ref[...] = ckvn
        tc, ta, tb = c_ref[...], a_ref[...], b_ref[...]
        q = jnp.dot(cqn, wq_ref[...], preferred_element_type=F32)
        kv = jnp.dot(ckvn, wkv_ref[...], preferred_element_type=F32)
        kr = _rope(s_ref[:, S_KR:S_F], tc, ta, tb)
        for h in range(N_HEADS):
            sl = slice(h * LANES, (h + 1) * LANES)
            mq_ref[:, sl] = _rope(q[:, sl], tc, ta, tb).astype(BF16)
            mk_ref[:, sl] = (kv[:, sl] + kr).astype(BF16)
        mv_ref[...] = kv[:, HW:].astype(BF16)
        z = s_ref[:, S_F:S_END - LANES] + bf_ref[...]
        lf_ref[...] = jnp.minimum(z, 0.0) - jnp.log(1.0 + jnp.exp(-jnp.abs(z)))

    def row(w):
        return pl.BlockSpec((bt, w), lambda i: (i, 0))

    def full(arr):
        return pl.BlockSpec(arr.shape, lambda i: (0, 0))

    return pl.pallas_call(
        body, name=name, grid=(t // bt,),
        in_specs=[row(S_END), full(g_q), full(g_kv), full(w_uq), full(w_ukv), row(LANES), row(LANES), row(LANES), full(b_f)],
        out_specs=[row(HW), row(HW), row(HW), row(LANES), row(Q_RANK), row(KV_RANK)],
        out_shape=[jax.ShapeDtypeStruct((t, HW), BF16)] * 3 + [jax.ShapeDtypeStruct((t, LANES), F32),
                   jax.ShapeDtypeStruct((t, Q_RANK), BF16), jax.ShapeDtypeStruct((t, KV_RANK), BF16)],
        compiler_params=_params(("parallel",)),
    )(small, g_q, g_kv, w_uq, w_ukv, tab_c, tab_a, tab_b, b_f)


def mla_prep_bwd(dmq, dmk, dmv, dlf, small, g_q, g_kv, w_uq, w_ukv, tab_c, tab_a, tab_b, b_f, *, name, bt=512):
    t = small.shape[0]
    bt = min(bt, t)

    def body(dmq_ref, dmk_ref, dmv_ref, dlf_ref, s_ref, gq_ref, gkv_ref, wq_ref, wkv_ref, c_ref, a_ref, b_ref, bf_ref,
             ds_ref, dq_ref, dkv_ref, dgq_ref, dgkv_ref, db_ref):
        tc, ta, tb = c_ref[...], a_ref[...], b_ref[...]
        lane = lax.broadcasted_iota(jnp.int32, (1, LANES), 1)
        dkr = jnp.zeros((bt, LANES), F32)
        for h in range(N_HEADS):
            sl = slice(h * LANES, (h + 1) * LANES)
            dq_ref[:, sl] = _rope_bwd(dmq_ref[:, sl], tc, ta, tb).astype(BF16)
            dkr = dkr + dmk_ref[:, sl]
        dkv_ref[:, :HW] = dmk_ref[...].astype(BF16)
        dkv_ref[:, HW:] = dmv_ref[...].astype(BF16)
        in_rope = (lane >= HEAD_DIM) & (lane < HEAD_DIM + ROPE_DIM)
        ds_ref[:, S_KR:S_F] = jnp.where(in_rope, _rope_bwd(dkr, tc, ta, tb), 0.0).astype(BF16)

        def norm_bwd(raw, g_ref, dn, dg_ref):
            r = lax.rsqrt(jnp.mean(raw * raw, axis=-1, keepdims=True) + EPS)
            u = dn * g_ref[...]
            dot = jnp.mean(u * raw, axis=-1, keepdims=True)
            dg_ref[...] += jnp.sum(dn * (raw * r), axis=0, keepdims=True)
            return r * u - raw * (r * r * r * dot)

        @pl.when(pl.program_id(0) == 0)
        def _():
            dgq_ref[...] = jnp.zeros_like(dgq_ref)
            dgkv_ref[...] = jnp.zeros_like(dgkv_ref)
            db_ref[...] = jnp.zeros_like(db_ref)

        dcqn = lax.dot_general(dq_ref[...], wq_ref[...], NT, preferred_element_type=F32)
        ds_ref[:, S_CQ:S_CKV] = norm_bwd(s_ref[:, S_CQ:S_CKV], gq_ref, dcqn, dgq_ref).astype(BF16)
        dckvn = lax.dot_general(dkv_ref[...], wkv_ref[...], NT, preferred_element_type=F32)
        ds_ref[:, S_CKV:S_KR] = norm_bwd(s_ref[:, S_CKV:S_KR], gkv_ref, dckvn, dgkv_ref).astype(BF16)
        z = s_ref[:, S_F:S_END - LANES] + bf_ref[...]
        dz = jnp.where(lane < N_HEADS, dlf_ref[...] / (1.0 + jnp.exp(z)), 0.0)
        db_ref[...] += jnp.sum(dz, axis=0, keepdims=True)
        ds_ref[:, S_F:S_END - LANES] = dz.astype(BF16)
        ds_ref[:, S_END - LANES:] = jnp.zeros((bt, LANES), BF16)

    def row(w):
        return pl.BlockSpec((bt, w), lambda i: (i, 0))

    def full(arr):
        return pl.BlockSpec(arr.shape, lambda i: (0, 0))

    def vec(w):
        return pl.BlockSpec((1, w), lambda i: (0, 0))

    return pl.pallas_call(
        body, name=name, grid=(t // bt,),
        in_specs=[row(HW), row(HW), row(HW), row(LANES), row(S_END), full(g_q), full(g_kv), full(w_uq), full(w_ukv),
                  row(LANES), row(LANES), row(LANES), full(b_f)],
        out_specs=[row(S_END), row(HW), row(2 * HW), vec(Q_RANK), vec(KV_RANK), vec(LANES)],
        out_shape=[jax.ShapeDtypeStruct((t, S_END), BF16), jax.ShapeDtypeStruct((t, HW), BF16),
                   jax.ShapeDtypeStruct((t, 2 * HW), BF16), jax.ShapeDtypeStruct((1, Q_RANK), F32),
                   jax.ShapeDtypeStruct((1, KV_RANK), F32), jax.ShapeDtypeStruct((1, LANES), F32)],
        compiler_params=_params(("arbitrary",)),
    )(dmq, dmk, dmv, dlf, small, g_q, g_kv, w_uq, w_ukv, tab_c, tab_a, tab_b, b_f)


def _lane():
    return lax.broadcasted_iota(jnp.int32, (1, LANES), 1)


def _halves(x):
    zero = jnp.zeros_like(x)
    return [jnp.where(_lane() < HEAD_DIM, x, zero), jnp.where(_lane() >= HEAD_DIM, x, zero)]


def _groups(x):
    return [x[:, :LANES], x[:, LANES:]]


def _lanes01(a, b, rest):
    return jnp.where(_lane() == 0, a, jnp.where(_lane() == 1, b, rest))


def _pick_row(tile, h):
    row = lax.broadcasted_iota(jnp.int32, (tile.shape[0], 1), 0)
    return jnp.sum(jnp.where(row == h, tile, 0.0), axis=0, keepdims=True)


def _below_diagonal(s):
    r = lax.broadcasted_iota(jnp.int32, s.shape, 0)
    c = lax.broadcasted_iota(jnp.int32, s.shape, 1)
    return jnp.where(c <= r, s, -jnp.inf)


def flash_fwd(q_arr, k_arr, v_arr, f_blocks, *, qoff, koff, voff, pair, scale, name, blk=512):
    t = q_arr.shape[0]
    blk = min(blk, t)
    nb = t // blk
    w = LANES if pair else 2 * LANES
    has_bias = f_blocks is not None
    split = _halves if pair else _groups

    def body(*refs):
        if has_bias:
            q_ref, k_ref, v_ref, f_ref, o_ref, st_ref = refs
        else:
            q_ref, k_ref, v_ref, o_ref, st_ref = refs
        g, i = pl.program_id(0), pl.program_id(1)
        qs = split((q_ref[...].astype(F32) * (scale * LOG2E)).astype(BF16))

        def step(j, carry, diagonal):
            rows = pl.ds(pl.multiple_of(j * blk, blk), blk)
            kk, vv = k_ref[rows, :], v_ref[rows, :]
            ks, vs = ([kk, kk], [vv, vv]) if pair else (_groups(kk), _groups(vv))
            out = []
            for n in range(2):
                m, l, acc = carry[n]
                s = lax.dot_general(qs[n], ks[n], NT, preferred_element_type=F32)
                if has_bias:
                    s = s - LOG2E * _pick_row(f_ref[j], 2 * g + n)
                if diagonal:
                    s = _below_diagonal(s)
                m_new = jnp.maximum(m, jnp.max(s, axis=1, keepdims=True))
                alpha = jnp.exp2(m - m_new)
                p = jnp.exp2(s - m_new)
                out.append((m_new, alpha * l + jnp.sum(p, axis=1, keepdims=True),
                            alpha * acc + jnp.dot(p.astype(BF16), vs[n], preferred_element_type=F32)))
            return tuple(out)

        init = tuple((jnp.full((blk, 1), -jnp.inf, F32), jnp.zeros((blk, 1), F32), jnp.zeros((blk, LANES), F32))
                     for _ in range(2))
        carry = lax.fori_loop(0, i, lambda j, c: step(j, c, False), init)
        (ma, la, acca), (mb, lb, accb) = step(i, carry, True)
        oa, ob = acca / la, accb / lb
        o_ref[...] = jnp.where(_lane() < HEAD_DIM, oa, ob) if pair else oa + pltpu.roll(ob, HEAD_DIM, 1)
        st_ref[0] = _lanes01(ma + jnp.log2(la), mb + jnp.log2(lb), 0.0)

    in_specs = [pl.BlockSpec((blk, w), lambda g, i: (i, qoff + g)), pl.BlockSpec((t, w), lambda g, i: (0, koff + g)),
                pl.BlockSpec((t, w), lambda g, i: (0, voff + g))]
    ins = [q_arr, k_arr, v_arr]
    if has_bias:
        in_specs.append(pl.BlockSpec((nb, N_HEADS, blk), lambda g, i: (0, 0, 0)))
        ins.append(f_blocks)
    return pl.pallas_call(
        body, name=name, grid=(PAIRS, nb), in_specs=in_specs,
        out_specs=[pl.BlockSpec((blk, LANES), lambda g, i: (i, g)), pl.BlockSpec((1, blk, LANES), lambda g, i: (g, i, 0))],
        out_shape=[jax.ShapeDtypeStruct((t, PAIRS * LANES), F32), jax.ShapeDtypeStruct((PAIRS, t, LANES), F32)],
        compiler_params=_params(("parallel", "arbitrary")),
    )(*ins)


def attn_stats(o, do, st, *, name, bt=512):
    t = o.shape[0]
    bt = min(bt, t)

    def body(o_ref, do_ref, st_ref, out_ref):
        prod = o_ref[...] * do_ref[...].astype(F32)
        for g in range(PAIRS):
            grp = prod[:, g * LANES:(g + 1) * LANES]
            da = jnp.sum(jnp.where(_lane() < HEAD_DIM, grp, 0.0), axis=1, keepdims=True)
            db = jnp.sum(jnp.where(_lane() >= HEAD_DIM, grp, 0.0), axis=1, keepdims=True)
            out_ref[g] = jnp.where(_lane() == 2, da, jnp.where(_lane() == 3, db, st_ref[g]))

    row = pl.BlockSpec((bt, PAIRS * LANES), lambda i: (i, 0))
    stat = pl.BlockSpec((PAIRS, bt, LANES), lambda i: (0, i, 0))
    return pl.pallas_call(
        body, name=name, grid=(t // bt,), in_specs=[row, row, stat], out_specs=stat,
        out_shape=jax.ShapeDtypeStruct(st.shape, F32),
        compiler_params=_params(("parallel",)),
    )(o, do, st)


def flash_bwd(q_arr, k_arr, v_arr, do_arr, st, f_blocks, *, qoff, koff, voff, pair, scale, name, blk=512):
    t = q_arr.shape[0]
    blk = min(blk, t)
    nb = t // blk
    w = LANES if pair else 2 * LANES
    has_bias = f_blocks is not None
    split = _halves if pair else _groups

    def body(*refs):
        if has_bias:
            q_ref, k_ref, v_ref, do_ref, st_ref, f_ref, dq_ref, dk_ref, dv_ref, dfk_ref, dfq_ref = refs
        else:
            q_ref, k_ref, v_ref, do_ref, st_ref, dq_ref, dk_ref, dv_ref = refs
        g, j = pl.program_id(0), pl.program_id(1)
        kk, vv = k_ref[...], v_ref[...]
        ks, vs = ([kk, kk], [vv, vv]) if pair else (_groups(kk), _groups(vv))
        if has_bias:
            fk = [LOG2E * _pick_row(f_ref[0], 2 * g + n) for n in range(2)]

        def step(i, carry, diagonal):
            rows = pl.ds(pl.multiple_of(i * blk, blk), blk)
            qs = split((q_ref[rows, :].astype(F32) * (scale * LOG2E)).astype(BF16))
            da, db = _halves(do_ref[rows, :].astype(F32))
            dos = [da.astype(BF16), (db if pair else pltpu.roll(db, HEAD_DIM, 1)).astype(BF16)]
            stats = st_ref[0, rows, :]
            new, dqs, row_sums = [], [], []
            for n in range(2):
                dk, dv, dfk = carry[n]
                s = lax.dot_general(qs[n], ks[n], NT, preferred_element_type=F32)
                if has_bias:
                    s = s - fk[n]
                if diagonal:
                    s = _below_diagonal(s)
                p = jnp.exp2(s - stats[:, n:n + 1])
                dp = lax.dot_general(dos[n], vs[n], NT, preferred_element_type=F32)
                ds = p * (dp - stats[:, 2 + n:3 + n])
                dsb = ds.astype(BF16)
                dv = dv + lax.dot_general(p.astype(BF16), dos[n], TN, preferred_element_type=F32)
                dk = dk + lax.dot_general(dsb, qs[n], TN, preferred_element_type=F32)
                dqs.append(jnp.dot(dsb, ks[n], preferred_element_type=F32))
                if has_bias:
                    dfk = dfk - jnp.sum(ds, axis=0, keepdims=True)
                    row_sums.append(jnp.sum(ds, axis=1, keepdims=True))
                new.append((dk, dv, dfk))
            dq = (jnp.where(_lane() < HEAD_DIM, dqs[0], dqs[1]) if pair else jnp.concatenate(dqs, axis=1)) * scale
            rs = _lanes01(row_sums[0], row_sums[1], 0.0) if has_bias else None

            @pl.when(j == 0)
            def _():
                dq_ref[rows, :] = dq
                if has_bias:
                    dfq_ref[0, rows, :] = rs

            @pl.when(j > 0)
            def _():
                dq_ref[rows, :] += dq
                if has_bias:
                    dfq_ref[0, rows, :] += rs

            return tuple(new)

        init = tuple((jnp.zeros((blk, LANES), F32), jnp.zeros((blk, LANES), F32), jnp.zeros((1, blk), F32)) for _ in range(2))
        carry = step(j, init, True)
        (dka, dva, dfa), (dkb, dvb, dfb) = lax.fori_loop(j + 1, nb, lambda i, c: step(i, c, False), carry)
        if pair:
            dk_ref[...] = (dka + dkb) * LN2
            dv_ref[...] = dva + dvb
        else:
            dk_ref[...] = jnp.concatenate([dka, dkb], axis=1) * LN2
            dv_ref[...] = jnp.concatenate([dva, dvb], axis=1)
        if has_bias:
            row = lax.broadcasted_iota(jnp.int32, (N_HEADS, 1), 0)
            dfk_ref[0] = jnp.where(row == 0, dfa, jnp.where(row == 1, dfb, 0.0))

    in_specs = [pl.BlockSpec((t, w), lambda g, j: (0, qoff + g)), pl.BlockSpec((blk, w), lambda g, j: (j, koff + g)),
                pl.BlockSpec((blk, w), lambda g, j: (j, voff + g)), pl.BlockSpec((t, LANES), lambda g, j: (0, g)),
                pl.BlockSpec((1, t, LANES), lambda g, j: (g, 0, 0))]
    ins = [q_arr, k_arr, v_arr, do_arr, st]
    out_specs = [pl.BlockSpec((t, w), lambda g, j: (0, g)), pl.BlockSpec((blk, w), lambda g, j: (j, g)),
                 pl.BlockSpec((blk, w), lambda g, j: (j, g))]
    out_shape = [jax.ShapeDtypeStruct((t, PAIRS * w), F32)] * 3
    if has_bias:
        in_specs.append(pl.BlockSpec((1, N_HEADS, blk), lambda g, j: (j, 0, 0)))
        ins.append(f_blocks)
        out_specs += [pl.BlockSpec((1, N_HEADS, blk), lambda g, j: (g, 0, j)), pl.BlockSpec((1, t, LANES), lambda g, j: (g, 0, 0))]
        out_shape += [jax.ShapeDtypeStruct((PAIRS, N_HEADS, t), F32), jax.ShapeDtypeStruct((PAIRS, t, LANES), F32)]
    return pl.pallas_call(
        body, name=name, grid=(PAIRS, nb), in_specs=in_specs, out_specs=out_specs, out_shape=out_shape,
        compiler_params=_params(("parallel", "arbitrary")),
    )(*ins)


def loss_head(x2, g, tgt, *, name, bt=512):
    t, d = x2.shape
    bt = min(bt, t)

    def body(x_ref, g_ref, t_ref, loss_ref, dx_ref, dxb_ref, dg_ref):
        @pl.when(pl.program_id(0) == 0)
        def _():
            loss_ref[...] = jnp.zeros_like(loss_ref)
            dg_ref[...] = jnp.zeros_like(dg_ref)

        xv = x_ref[...]
        gv = g_ref[...]
        r = lax.rsqrt(jnp.mean(xv * xv, axis=-1, keepdims=True) + EPS)
        xh = xv * r
        e = xh * gv - t_ref[...]
        loss_ref[...] += 0.5 * jnp.sum(jnp.mean(e * e, axis=-1, keepdims=True), axis=0, keepdims=True)
        dy = e * (1.0 / d)
        dg_ref[...] += jnp.sum(dy * xh, axis=0, keepdims=True)
        u = dy * gv
        dx = r * u - xv * (r * r * r * jnp.mean(u * xv, axis=-1, keepdims=True))
        dx_ref[...] = dx
        dxb_ref[...] = dx.astype(BF16)

    row = pl.BlockSpec((bt, d), lambda i: (i, 0))
    vec = pl.BlockSpec((1, d), lambda i: (0, 0))
    return pl.pallas_call(
        body, name=name, grid=(t // bt,), in_specs=[row, vec, row],
        out_specs=[pl.BlockSpec((1, 1), lambda i: (0, 0)), row, row, vec],
        out_shape=[jax.ShapeDtypeStruct((1, 1), F32), jax.ShapeDtypeStruct((t, d), F32),
                   jax.ShapeDtypeStruct((t, d), BF16), jax.ShapeDtypeStruct((1, d), F32)],
        compiler_params=_params(("arbitrary",)),
    )(x2, g, tgt)


def adamw(w, g, m, v, *, name):
    rws, cols = w.shape
    br = next(b for b in (256, 128, 64, 32, 16, 8) if rws % b == 0)

    def body(w_ref, g_ref, m_ref, v_ref, d_ref, nm_ref, nv_ref):
        gv = g_ref[...]
        nm = ADAM_B1 * m_ref[...] + (1.0 - ADAM_B1) * gv
        nv = ADAM_B2 * v_ref[...] + (1.0 - ADAM_B2) * (gv * gv)
        m_hat = nm / (1.0 - ADAM_B1 ** ADAM_STEP)
        v_hat = nv / (1.0 - ADAM_B2 ** ADAM_STEP)
        d_ref[...] = -ADAM_LR * (m_hat / (jnp.sqrt(v_hat) + ADAM_EPS) + ADAM_WD * w_ref[...])
        nm_ref[...] = nm
        nv_ref[...] = nv

    blk = pl.BlockSpec((br, cols), lambda i: (i, 0))
    return pl.pallas_call(
        body, name=name, grid=(rws // br,), in_specs=[blk] * 4, out_specs=[blk] * 3,
        out_shape=[jax.ShapeDtypeStruct((rws, cols), F32)] * 3,
        compiler_params=_params(("parallel",)),
    )(w, g, m, v)


def add_pair(a, b, *, name):
    n, r, c = a.shape
    br = r // 2

    def body(a_ref, b_ref, o_ref):
        o_ref[...] = (a_ref[...].astype(F32) + b_ref[...].astype(F32)).astype(o_ref.dtype)

    blk = pl.BlockSpec((1, br, c), lambda q, i: (q, i, 0))
    return pl.pallas_call(
        body, name=name, grid=(n, r // br), in_specs=[blk, blk], out_specs=blk,
        out_shape=jax.ShapeDtypeStruct((n, r, c), BF16),
        compiler_params=_params(("parallel", "parallel")),
    )(a, b)


def sum_chips(parts, *, name):
    n, r, c = parts.shape
    br = r // 2

    def body(p_ref, o_ref):
        acc = p_ref[0].astype(F32)
        for q in range(1, n):
            acc = acc + p_ref[q].astype(F32)
        o_ref[...] = acc

    return pl.pallas_call(
        body, name=name, grid=(r // br,),
        in_specs=[pl.BlockSpec((n, br, c), lambda i: (0, i, 0))], out_specs=pl.BlockSpec((br, c), lambda i: (i, 0)),
        out_shape=jax.ShapeDtypeStruct((r, c), F32),
        compiler_params=_params(("parallel",)),
    )(parts)


ANY = pl.BlockSpec(memory_space=pl.ANY)


def _place():
    x, y, c = lax.axis_index("x"), lax.axis_index("y"), lax.axis_index("c")
    chips = [(1 - x, y), (x, 1 - y), (1 - x, 1 - y)]
    return x, y, c, chips


def gather_shards(wpack):
    r, cdim = wpack.shape
    hr = r // 2

    def body(w_ref, out_ref, send_sems, recv_sems, local_sem):
        x, y, c, chips = _place()
        q = 2 * x + y
        sibling = (x, y, 1 - c)

        def half(qq, hf):
            return out_ref.at[qq, pl.ds(hf * hr, hr), :]

        def copy(k, src, dst, to):
            return pltpu.make_async_remote_copy(src_ref=src, dst_ref=dst, send_sem=send_sems.at[k],
                                                recv_sem=recv_sems.at[k], device_id=to, device_id_type=MESH)

        mine = pltpu.make_async_copy(w_ref, out_ref.at[q], local_sem)
        mine.start()
        first = [copy(j, w_ref.at[pl.ds(c * hr, hr), :], half(q, c), (cx, cy, c)) for j, (cx, cy) in enumerate(chips)]
        for cp in first:
            cp.start()
        passed = []
        for j, (cx, cy) in enumerate(chips):
            qj = 2 * cx + cy
            copy(j, half(qj, c), half(qj, c), (cx, cy, c)).wait_recv()
            fw = copy(3 + j, half(qj, c), half(qj, c), sibling)
            fw.start()
            passed.append(fw)
        for j, (cx, cy) in enumerate(chips):
            qj = 2 * cx + cy
            copy(3 + j, half(qj, 1 - c), half(qj, 1 - c), sibling).wait_recv()
        for cp in first + passed:
            cp.wait_send()
        mine.wait()

    return pl.pallas_call(
        body, name="gather_shards", in_specs=[ANY], out_specs=ANY,
        out_shape=jax.ShapeDtypeStruct((N_CHIPS, r, cdim), wpack.dtype),
        scratch_shapes=[pltpu.SemaphoreType.DMA((6,)), pltpu.SemaphoreType.DMA((6,)), pltpu.SemaphoreType.DMA],
    )(wpack)


def swap_halves(g2):
    def body(g_ref, out_ref, send_sem, recv_sem):
        x, y, c, _ = _place()
        cp = pltpu.make_async_remote_copy(src_ref=g_ref.at[1 - c], dst_ref=out_ref, send_sem=send_sem, recv_sem=recv_sem,
                                          device_id=(x, y, 1 - c), device_id_type=MESH)
        cp.start()
        cp.wait()

    return pl.pallas_call(
        body, name="swap_halves", in_specs=[ANY], out_specs=ANY,
        out_shape=jax.ShapeDtypeStruct(g2.shape[1:], g2.dtype),
        scratch_shapes=[pltpu.SemaphoreType.DMA, pltpu.SemaphoreType.DMA],
    )(g2)


def scatter_to_owners(p):
    def body(p_ref, out_ref, send_sems, recv_sems, local_sem):
        x, y, c, chips = _place()
        q = 2 * x + y
        mine = pltpu.make_async_copy(p_ref.at[q], out_ref.at[q], local_sem)
        mine.start()
        sends = []
        for j, (cx, cy) in enumerate(chips):
            cp = pltpu.make_async_remote_copy(src_ref=p_ref.at[2 * cx + cy], dst_ref=out_ref.at[q], send_sem=send_sems.at[j],
                                              recv_sem=recv_sems.at[j], device_id=(cx, cy, c), device_id_type=MESH)
            cp.start()
            sends.append(cp)
        for j, (cx, cy) in enumerate(chips):
            qj = 2 * cx + cy
            pltpu.make_async_remote_copy(src_ref=p_ref.at[qj], dst_ref=out_ref.at[qj], send_sem=send_sems.at[j],
                                         recv_sem=recv_sems.at[j], device_id=(cx, cy, c), device_id_type=MESH).wait_recv()
        for cp in sends:
            cp.wait_send()
        mine.wait()

    return pl.pallas_call(
        body, name="scatter_to_owners", in_specs=[ANY], out_specs=ANY,
        out_shape=jax.ShapeDtypeStruct(p.shape, p.dtype),
        scratch_shapes=[pltpu.SemaphoreType.DMA((3,)), pltpu.SemaphoreType.DMA((3,)), pltpu.SemaphoreType.DMA],
    )(p)


def join_halves(rh):
    def body(r_ref, out_ref, send_sem, recv_sem, local_sem):
        x, y, c, _ = _place()
        mine = pltpu.make_async_copy(r_ref, out_ref.at[c], local_sem)
        mine.start()
        cp = pltpu.make_async_remote_copy(src_ref=r_ref, dst_ref=out_ref.at[c], send_sem=send_sem, recv_sem=recv_sem,
                                          device_id=(x, y, 1 - c), device_id_type=MESH)
        cp.start()
        pltpu.make_async_remote_copy(src_ref=r_ref, dst_ref=out_ref.at[1 - c], send_sem=send_sem, recv_sem=recv_sem,
                                     device_id=(x, y, 1 - c), device_id_type=MESH).wait_recv()
        cp.wait_send()
        mine.wait()

    return pl.pallas_call(
        body, name="join_halves", in_specs=[ANY], out_specs=ANY,
        out_shape=jax.ShapeDtypeStruct((2,) + rh.shape, rh.dtype),
        scratch_shapes=[pltpu.SemaphoreType.DMA, pltpu.SemaphoreType.DMA, pltpu.SemaphoreType.DMA],
    )(rh)


def allreduce_small(s):
    n_dev = 8

    def body(s_ref, out_ref, buf, send_sems, recv_sems):
        x, y, c, _ = _place()
        me = 4 * x + 2 * y + c
        buf[me] = s_ref[...]
        sends = []
        for k in range(1, n_dev):
            px = 1 - x if k & 4 else x
            py = 1 - y if k & 2 else y
            pc = 1 - c if k & 1 else c
            cp = pltpu.make_async_remote_copy(src_ref=s_ref, dst_ref=buf.at[me], send_sem=send_sems.at[k - 1],
                                              recv_sem=recv_sems.at[k - 1], device_id=(px, py, pc), device_id_type=MESH)
            cp.start()
            sends.append((cp, 4 * px + 2 * py + pc))
        for k, (cp, peer) in enumerate(sends):
            pltpu.make_async_remote_copy(src_ref=s_ref, dst_ref=buf.at[peer], send_sem=send_sems.at[k],
                                         recv_sem=recv_sems.at[k], device_id=(x, y, c), device_id_type=MESH).wait_recv()
        for cp, _ in sends:
            cp.wait_send()
        acc = buf[0]
        for d in range(1, n_dev):
            acc = acc + buf[d]
        out_ref[...] = acc

    vm = pl.BlockSpec(memory_space=pltpu.VMEM)
    return pl.pallas_call(
        body, name="allreduce_small", in_specs=[vm], out_specs=vm,
        out_shape=jax.ShapeDtypeStruct(s.shape, F32),
        scratch_shapes=[pltpu.VMEM((n_dev,) + s.shape, F32), pltpu.SemaphoreType.DMA((n_dev - 1,)),
                        pltpu.SemaphoreType.DMA((n_dev - 1,))],
    )(s)


PACK_COLS = 1024
COL_SHARDED = (True, True, True, False, True, False)


def _pack_rows(shapes):
    rows = [s[0] * s[1] // PACK_COLS for s in shapes]
    total = sum(rows)
    return rows, -(-total // 16) * 16


def pack_shards(shards):
    rows, padded = _pack_rows([s.shape for s in shards])
    parts = [s.reshape(r, PACK_COLS) for s, r in zip(shards, rows)]
    parts.append(jnp.zeros((padded - sum(rows), PACK_COLS), shards[0].dtype))
    return jnp.concatenate(parts, axis=0)


def unpack_shards(pack, shapes):
    rows, _ = _pack_rows(shapes)
    out, at = [], 0
    for s, r in zip(shapes, rows):
        out.append(pack[at:at + r].reshape(s))
        at += r
    return out


def unpack_full(wall, shapes):
    rows, _ = _pack_rows(shapes)
    out, at = [], 0
    for (k, n), r, by_col in zip(shapes, rows, COL_SHARDED):
        blk = wall[:, at:at + r].reshape(N_CHIPS, k, n)
        out.append(blk.transpose(1, 0, 2).reshape(k, N_CHIPS * n) if by_col else blk.reshape(N_CHIPS * k, n))
        at += r
    return out


def pack_full(fulls, shapes, dtype):
    rows, padded = _pack_rows(shapes)
    parts = []
    for f, (k, n), r, by_col in zip(fulls, shapes, rows, COL_SHARDED):
        blk = f.reshape(k, N_CHIPS, n).transpose(1, 0, 2) if by_col else f.reshape(N_CHIPS, k, n)
        parts.append(blk.reshape(N_CHIPS, r, PACK_COLS).astype(dtype))
    parts.append(jnp.zeros((N_CHIPS, padded - sum(rows), PACK_COLS), dtype))
    return jnp.concatenate(parts, axis=1)


def _pad_heads(w, width):
    lead = w.shape[:-1]
    w = w.reshape(lead + (N_HEADS, width))
    return jnp.pad(w, [(0, 0)] * len(lead) + [(0, 0), (0, LANES - width)]).reshape(lead + (HW,))


def _unpad_heads(w, width):
    lead = w.shape[:-1]
    return w.reshape(lead + (N_HEADS, LANES))[..., :width].reshape(lead + (N_HEADS * width,))


FW = N_HEADS * HEAD_DIM


def split_w_in(w_in):
    d = w_in.shape[0]
    o_f = 3 * FW
    o_cq = o_f + N_HEADS
    o_ckv = o_cq + Q_RANK
    o_kr = o_ckv + KV_RANK

    def z(n):
        return jnp.zeros((d, n), w_in.dtype)

    small = jnp.concatenate([w_in[:, o_cq:o_ckv], w_in[:, o_ckv:o_kr], z(HEAD_DIM), w_in[:, o_kr:], z(LANES - HEAD_DIM - ROPE_DIM),
                             w_in[:, o_f:o_cq], z(LANES - N_HEADS), z(LANES)], axis=1)
    return w_in[:, :o_f], small


def join_w_in(d_qkv, d_small):
    kr = S_KR + HEAD_DIM
    return jnp.concatenate([d_qkv, d_small[:, S_F:S_F + N_HEADS], d_small[:, S_CQ:S_CKV], d_small[:, S_CKV:S_KR],
                            d_small[:, kr:kr + ROPE_DIM]], axis=1)


def rope_tables(pos):
    t = pos.shape[0]
    inv_freq = ROPE_THETA ** (-jnp.arange(0, ROPE_DIM, 2, dtype=F32) / ROPE_DIM)
    ang = pos.astype(F32)[:, None] * inv_freq
    cos, sin = jnp.cos(ang), jnp.sin(ang)
    half = ROPE_DIM // 2

    def z(n):
        return jnp.zeros((t, n), F32)

    tab_c = jnp.concatenate([jnp.ones((t, HEAD_DIM), F32), cos, cos, z(LANES - HEAD_DIM - ROPE_DIM)], axis=1)
    tab_a = jnp.concatenate([z(HEAD_DIM), -sin, z(half), z(LANES - HEAD_DIM - ROPE_DIM)], axis=1)
    tab_b = jnp.concatenate([z(HEAD_DIM), z(half), sin, z(LANES - HEAD_DIM - ROPE_DIM)], axis=1)
    return tab_c, tab_a, tab_b


def _pad_lanes(v, n):
    return jnp.pad(v, ((0, 0), (0, n - v.shape[1])))


ATTN_BLK = 512


def local_step(xs, pos, tgt, gains, weights):
    g_attn, b_forget, g_q, g_kv, g_fo, g_mo, g_mlp, g_fin = gains
    w_in, w_uq, w_ukv, w_o, w_up, w_down = weights
    t = xs.shape[0]
    blk = min(ATTN_BLK, t)
    fox_scale = 1.0 / (HEAD_DIM ** 0.5)
    mla_scale = 1.0 / ((HEAD_DIM + ROPE_DIM) ** 0.5)

    w_qkv, w_small = split_w_in(w_in)
    w_uq_p = _pad_heads(w_uq, HEAD_DIM + ROPE_DIM)
    kv = w_ukv.reshape(KV_RANK, N_HEADS, 2 * HEAD_DIM)
    w_ukv_p = jnp.concatenate([_pad_heads(kv[:, :, :HEAD_DIM].reshape(KV_RANK, FW), HEAD_DIM),
                               _pad_heads(kv[:, :, HEAD_DIM:].reshape(KV_RANK, FW), HEAD_DIM)], axis=1)
    w_of, w_om = w_o[:FW], w_o[FW:]
    b_f = _pad_lanes(b_forget, LANES)
    tab_c, tab_a, tab_b = rope_tables(pos)

    h1 = rmsnorm(xs, g_attn, out_dtype=BF16, name="norm_attn")
    qkv, = mm(h1, w_qkv, out_dtypes=[BF16], name="proj_qkv")
    small, = mm(h1, w_small, out_dtypes=[F32], name="proj_small")
    mq, mk, mv, lf, cqn, ckvn = mla_prep(small, g_q, g_kv, w_uq_p, w_ukv_p, tab_c, tab_a, tab_b, b_f, name="mla_prep")
    f_cum = cumsum_rows(lf, reverse=False, name="gate_cumsum")
    f_blocks = f_cum[:, :N_HEADS].reshape(t // blk, blk, N_HEADS).transpose(0, 2, 1)
    fo, st_f = flash_fwd(qkv, qkv, qkv, f_blocks, qoff=0, koff=PAIRS, voff=2 * PAIRS, pair=True, scale=fox_scale,
                         name="fox_fwd", blk=blk)
    mo, st_m = flash_fwd(mq, mk, mv, None, qoff=0, koff=0, voff=0, pair=False, scale=mla_scale, name="mla_fwd", blk=blk)
    mix_f = rmsnorm(fo, g_fo, out_dtype=BF16, name="norm_fox_out")
    mix_m = rmsnorm(mo, g_mo, out_dtype=BF16, name="norm_mla_out")

    def add_res(acc, res):
        return (acc + res,)

    x1a, = mm(mix_f, w_of, extras=[xs], epilogue=add_res, out_dtypes=[F32], name="out_proj_fox")
    x1, = mm(mix_m, w_om, extras=[x1a], epilogue=add_res, out_dtypes=[F32], name="out_proj_mla")
    h2 = rmsnorm(x1, g_mlp, out_dtype=BF16, name="norm_mlp")

    def relu2(acc):
        r = jnp.maximum(acc, 0.0)
        return acc, r * r

    u, act = mm(h2, w_up, epilogue=relu2, out_dtypes=[BF16, BF16], name="mlp_up")
    x2, = mm(act, w_down, extras=[x1], epilogue=add_res, out_dtypes=[F32], name="mlp_down")
    loss, dx2, dx2b, dg_fin = loss_head(x2, g_fin, tgt, name="loss_head")

    def relu2_grad(acc, uu):
        return (acc * (2.0 * jnp.maximum(uu.astype(F32), 0.0)),)

    du, = mm(dx2b, w_down, trans_b=True, extras=[u], epilogue=relu2_grad, out_dtypes=[BF16], name="mlp_down_bwd")
    dw_down = mm_tn(act, dx2b, name="dw_down")
    dh2, = mm(du, w_up, trans_b=True, out_dtypes=[F32], name="mlp_up_bwd")
    dw_up = mm_tn(h2, du, name="dw_up")
    dx1, dx1b, dg_mlp = rmsnorm_bwd(dh2, x1, g_mlp, dx2, out_dtypes=[F32, BF16], name="norm_mlp_bwd")

    dmix_f, = mm(dx1b, w_of, trans_b=True, out_dtypes=[F32], name="out_proj_fox_bwd")
    dmix_m, = mm(dx1b, w_om, trans_b=True, out_dtypes=[F32], name="out_proj_mla_bwd")
    dw_o = jnp.concatenate([mm_tn(mix_f, dx1b, name="dw_o_fox"), mm_tn(mix_m, dx1b, name="dw_o_mla")], axis=0)
    dfo, dg_fo = rmsnorm_bwd(dmix_f, fo, g_fo, None, out_dtypes=[BF16], name="norm_fox_out_bwd")
    dmo, dg_mo = rmsnorm_bwd(dmix_m, mo, g_mo, None, out_dtypes=[BF16], name="norm_mla_out_bwd")
    st_f = attn_stats(fo, dfo, st_f, name="fox_stats")
    st_m = attn_stats(mo, dmo, st_m, name="mla_stats")
    dfq, dfk, dfv, df_key, df_query = flash_bwd(qkv, qkv, qkv, dfo, st_f, f_blocks, qoff=0, koff=PAIRS, voff=2 * PAIRS,
                                                pair=True, scale=fox_scale, name="fox_bwd", blk=blk)
    dmq, dmk, dmv = flash_bwd(mq, mk, mv, dmo, st_m, None, qoff=0, koff=0, voff=0, pair=False, scale=mla_scale,
                              name="mla_bwd", blk=blk)
    dqkv = jnp.concatenate([dfq, dfk, dfv], axis=1).astype(BF16)
    d_f = df_key[:, :2, :].reshape(N_HEADS, t).T + df_query[:, :, :2].transpose(1, 0, 2).reshape(t, N_HEADS)
    dlf = cumsum_rows(_pad_lanes(d_f, LANES), reverse=True, name="gate_cumsum_bwd")
    dsmall, dq_u, dkv_u, dg_q, dg_kv, db_f = mla_prep_bwd(dmq, dmk, dmv, dlf, small, g_q, g_kv, w_uq_p, w_ukv_p,
                                                          tab_c, tab_a, tab_b, b_f, name="mla_prep_bwd")
    dw_uq_p = mm_tn(cqn, dq_u, name="dw_uq")
    dw_ukv_p = mm_tn(ckvn, dkv_u, name="dw_ukv")

    dh1a, = mm(dqkv, w_qkv, trans_b=True, out_dtypes=[F32], name="proj_qkv_bwd")
    dh1, = mm(dsmall, w_small, trans_b=True, extras=[dh1a], epilogue=add_res, out_dtypes=[F32], name="proj_small_bwd")
    dw_qkv = mm_tn(h1, dqkv, name="dw_qkv")
    dw_small = mm_tn(h1, dsmall, name="dw_small")
    grad_x, dg_attn = rmsnorm_bwd(dh1, xs, g_attn, dx1, out_dtypes=[F32], name="norm_attn_bwd")

    dw_in = join_w_in(dw_qkv, dw_small)
    dw_uq = _unpad_heads(dw_uq_p, HEAD_DIM + ROPE_DIM)
    dk_cols = _unpad_heads(dw_ukv_p[:, :HW], HEAD_DIM).reshape(KV_RANK, N_HEADS, HEAD_DIM)
    dv_cols = _unpad_heads(dw_ukv_p[:, HW:], HEAD_DIM).reshape(KV_RANK, N_HEADS, HEAD_DIM)
    dw_ukv = jnp.concatenate([dk_cols, dv_cols], axis=2).reshape(KV_RANK, N_HEADS * 2 * HEAD_DIM)
    d_gains = (dg_attn, db_f[:, :N_HEADS], dg_q, dg_kv, dg_fo, dg_mo, dg_mlp, dg_fin)
    return loss, grad_x, (dw_in, dw_uq, dw_ukv, dw_o, dw_up, dw_down), d_gains


def kernel(x, positions, attn_norm_g, w_in, b_forget, q_norm_g, w_uq, kv_norm_g, w_ukv, fox_out_g, mla_out_g, w_o, mlp_norm_g, w_up, w_down, final_norm_g, loss_target, m_attn_norm_g, m_w_in, m_b_forget, m_q_norm_g, m_w_uq, m_kv_norm_g, m_w_ukv, m_fox_out_g, m_mla_out_g, m_w_o, m_mlp_norm_g, m_w_up, m_w_down, m_final_norm_g, v_attn_norm_g, v_w_in, v_b_forget, v_q_norm_g, v_w_uq, v_kv_norm_g, v_w_ukv, v_fox_out_g, v_mla_out_g, v_w_o, v_mlp_norm_g, v_w_up, v_w_down, v_final_norm_g):
    c = lax.axis_index("c")
    big = [w_in[0], w_uq[0], w_ukv[0], w_o[0], w_up[0], w_down[0]]
    big_m = [m_w_in[0], m_w_uq[0], m_w_ukv[0], m_w_o[0], m_w_up[0], m_w_down[0]]
    big_v = [v_w_in[0], v_w_uq[0], v_w_ukv[0], v_w_o[0], v_w_up[0], v_w_down[0]]
    shapes = [w.shape for w in big]

    def vec(a):
        return a.reshape(1, -1)

    small = [attn_norm_g, b_forget, q_norm_g, kv_norm_g, fox_out_g, mla_out_g, mlp_norm_g, final_norm_g]
    small_m = [m_attn_norm_g, m_b_forget, m_q_norm_g, m_kv_norm_g, m_fox_out_g, m_mla_out_g, m_mlp_norm_g, m_final_norm_g]
    small_v = [v_attn_norm_g, v_b_forget, v_q_norm_g, v_kv_norm_g, v_fox_out_g, v_mla_out_g, v_mlp_norm_g, v_final_norm_g]
    gains = [vec(a) for a in small]

    wall = gather_shards(pack_shards([w.astype(BF16) for w in big]))
    weights = unpack_full(wall, shapes)

    loss, grad_x, d_big, d_small = local_step(x[0], positions[0], loss_target[0], gains, weights)

    gp = pack_full(d_big, shapes, BF16)
    hr = gp.shape[1] // 2
    g2 = gp.reshape(N_CHIPS, 2, hr, PACK_COLS).transpose(1, 0, 2, 3)
    from_sibling = swap_halves(g2)
    kept = lax.dynamic_index_in_dim(g2, c, 0, keepdims=False)
    pair_sum = add_pair(kept, from_sibling, name="add_pair")
    by_chip = scatter_to_owners(pair_sum)
    my_half = sum_chips(by_chip, name="sum_chips")
    g_shard = join_halves(my_half).reshape(2 * hr, PACK_COLS)
    g_big = unpack_shards(g_shard, shapes)

    def rows8(vs):
        return jnp.concatenate([_pad_lanes(vec(a).astype(F32), PACK_COLS) for a in vs], axis=0)

    g_small8 = allreduce_small(rows8(d_small))

    outs_big = [adamw(w, g, m, v, name="adamw_%d" % n) for n, (w, g, m, v) in enumerate(zip(big, g_big, big_m, big_v))]
    d8, m8, v8 = adamw(rows8(small), g_small8, rows8(small_m), rows8(small_v), name="adamw_small")

    def unrows8(a8):
        return [a8[n, :s.size].reshape(s.shape) for n, s in enumerate(small)]

    loss_all = lax.psum(loss[0, 0], ("x", "y", "c"))
    grads, deltas, new_m, new_v = [None] * 14, [None] * 14, [None] * 14, [None] * 14
    big_at = [1, 4, 6, 9, 11, 12]
    small_at = [0, 2, 3, 5, 7, 8, 10, 13]
    for n, at in enumerate(big_at):
        grads[at] = g_big[n][None]
        deltas[at], new_m[at], new_v[at] = (a[None] for a in outs_big[n])
    for at, g, dd, mm_, vv in zip(small_at, unrows8(g_small8), unrows8(d8), unrows8(m8), unrows8(v8)):
        grads[at], deltas[at], new_m[at], new_v[at] = g, dd, mm_, vv
    return (loss_all, grad_x[None], *grads, *deltas, *new_m, *new_v)
```

```python
import jax
import jax.numpy as jnp
from jax import lax
from jax.experimental import pallas as pl
from jax.experimental.pallas import tpu as pltpu

F32 = jnp.float32
BF16 = jnp.bfloat16
MESH = pl.DeviceIdType.MESH

EPS = 1e-6
ROPE_THETA = 10000.0
N_HEADS = 8
PAIRS = N_HEADS // 2
HEAD_DIM = 64
ROPE_DIM = 32
LANES = 128
Q_RANK = 384
KV_RANK = 256
N_CHIPS = 4
ADAM_LR, ADAM_B1, ADAM_B2, ADAM_EPS, ADAM_WD, ADAM_STEP = 0.001, 0.9, 0.999, 1e-08, 0.01, 10
VMEM_LIMIT = 48 * 1024 * 1024
LOG2E = 1.4426950408889634
LN2 = 0.6931471805599453
NN = (((1,), (0,)), ((), ()))
NT = (((1,), (1,)), ((), ()))
TN = (((0,), (0,)), ((), ()))


def _params(sem=None):
    return pltpu.CompilerParams(dimension_semantics=sem, vmem_limit_bytes=VMEM_LIMIT)


def rmsnorm(x, g, *, out_dtype, name, bt=512):
    t, d = x.shape
    bt = min(bt, t)

    def body(x_ref, g_ref, o_ref):
        xv = x_ref[...].astype(F32)
        r = lax.rsqrt(jnp.mean(xv * xv, axis=-1, keepdims=True) + EPS)
        o_ref[...] = (xv * r * g_ref[...]).astype(o_ref.dtype)

    return pl.pallas_call(
        body, name=name, grid=(t // bt,),
        in_specs=[pl.BlockSpec((bt, d), lambda i: (i, 0)), pl.BlockSpec((1, d), lambda i: (0, 0))],
        out_specs=pl.BlockSpec((bt, d), lambda i: (i, 0)),
        out_shape=jax.ShapeDtypeStruct((t, d), out_dtype),
        compiler_params=_params(("parallel",)),
    )(x, g)


def rmsnorm_bwd(dh, x, g, res, *, out_dtypes, name, bt=512):
    t, d = x.shape
    bt = min(bt, t)
    has_res = res is not None

    def body(*refs):
        dh_ref, x_ref, g_ref = refs[:3]
        res_ref = refs[3] if has_res else None
        outs = refs[3 + has_res:]
        dx_refs, dg_ref = outs[:-1], outs[-1]
        xv = x_ref[...].astype(F32)
        dhv = dh_ref[...].astype(F32)
        r = lax.rsqrt(jnp.mean(xv * xv, axis=-1, keepdims=True) + EPS)
        u = dhv * g_ref[...]
        dot = jnp.mean(u * xv, axis=-1, keepdims=True)
        dx = r * u - xv * (r * r * r * dot)
        if has_res:
            dx = dx + res_ref[...]
        for o in dx_refs:
            o[...] = dx.astype(o.dtype)

        @pl.when(pl.program_id(0) == 0)
        def _():
            dg_ref[...] = jnp.zeros_like(dg_ref)

        dg_ref[...] += jnp.sum(dhv * (xv * r), axis=0, keepdims=True)

    row = pl.BlockSpec((bt, d), lambda i: (i, 0))
    vec = pl.BlockSpec((1, d), lambda i: (0, 0))
    ins = [dh, x, g] + ([res] if has_res else [])
    return pl.pallas_call(
        body, name=name, grid=(t // bt,),
        in_specs=[row, row, vec] + ([row] if has_res else []),
        out_specs=[row] * len(out_dtypes) + [vec],
        out_shape=[jax.ShapeDtypeStruct((t, d), dt) for dt in out_dtypes] + [jax.ShapeDtypeStruct((1, d), F32)],
        compiler_params=_params(("arbitrary",)),
    )(*ins)


def mm(a, b, *, trans_b=False, extras=(), epilogue=None, out_dtypes, name, bm=512, bn=512):
    m, k = a.shape
    n = b.shape[0] if trans_b else b.shape[1]
    bm, bn = min(bm, m), min(bn, n)
    n_ex = len(extras)

    def body(*refs):
        a_ref, b_ref = refs[0], refs[1]
        ex = refs[2:2 + n_ex]
        outs = refs[2 + n_ex:]
        acc = lax.dot_general(a_ref[...], b_ref[...], NT if trans_b else NN, preferred_element_type=F32)
        res = epilogue(acc, *[e[...] for e in ex]) if epilogue is not None else (acc,)
        for o, r in zip(outs, res):
            o[...] = r.astype(o.dtype)

    tile = pl.BlockSpec((bm, bn), lambda i, j: (i, j))
    b_spec = pl.BlockSpec((bn, k), lambda i, j: (j, 0)) if trans_b else pl.BlockSpec((k, bn), lambda i, j: (0, j))
    return pl.pallas_call(
        body, name=name, grid=(m // bm, n // bn),
        in_specs=[pl.BlockSpec((bm, k), lambda i, j: (i, 0)), b_spec] + [tile] * n_ex,
        out_specs=[tile] * len(out_dtypes),
        out_shape=[jax.ShapeDtypeStruct((m, n), dt) for dt in out_dtypes],
        compiler_params=_params(("parallel", "parallel")),
    )(a, b, *extras)


def mm_tn(a, b, *, name, bk=512, bn=512, bt=1024):
    t, k = a.shape
    n = b.shape[1]
    bk, bn, bt = min(bk, k), min(bn, n), min(bt, t)

    def body(a_ref, b_ref, o_ref):
        @pl.when(pl.program_id(2) == 0)
        def _():
            o_ref[...] = jnp.zeros_like(o_ref)

        o_ref[...] += lax.dot_general(a_ref[...], b_ref[...], TN, preferred_element_type=F32)

    return pl.pallas_call(
        body, name=name, grid=(k // bk, n // bn, t // bt),
        in_specs=[pl.BlockSpec((bt, bk), lambda i, j, s: (s, i)), pl.BlockSpec((bt, bn), lambda i, j, s: (s, j))],
        out_specs=pl.BlockSpec((bk, bn), lambda i, j, s: (i, j)),
        out_shape=jax.ShapeDtypeStruct((k, n), F32),
        compiler_params=_params(("parallel", "parallel", "arbitrary")),
    )(a, b)


def _split3(x):
    hi = x.astype(BF16)
    r1 = x - hi.astype(F32)
    mid = r1.astype(BF16)
    lo = (r1 - mid.astype(F32)).astype(BF16)
    return hi, mid, lo


def cumsum_rows(x, *, reverse, name, bc=512):
    t, d = x.shape
    bc = min(bc, t)
    nb = t // bc

    def body(x_ref, o_ref, carry):
        @pl.when(pl.program_id(0) == 0)
        def _():
            carry[...] = jnp.zeros_like(carry)

        r = lax.broadcasted_iota(jnp.int32, (bc, bc), 0)
        c = lax.broadcasted_iota(jnp.int32, (bc, bc), 1)
        tri = jnp.where((r <= c) if reverse else (r >= c), 1.0, 0.0).astype(BF16)
        hi, mid, lo = _split3(x_ref[...])
        s = (lax.dot_general(tri, hi, NN, preferred_element_type=F32)
             + lax.dot_general(tri, mid, NN, preferred_element_type=F32)
             + lax.dot_general(tri, lo, NN, preferred_element_type=F32)) + carry[0:1, :]
        o_ref[...] = s
        carry[0:1, :] = s[0:1, :] if reverse else s[bc - 1:bc, :]

    imap = (lambda i: (nb - 1 - i, 0)) if reverse else (lambda i: (i, 0))
    return pl.pallas_call(
        body, name=name, grid=(nb,),
        in_specs=[pl.BlockSpec((bc, d), imap)], out_specs=pl.BlockSpec((bc, d), imap),
        out_shape=jax.ShapeDtypeStruct((t, d), F32),
        scratch_shapes=[pltpu.VMEM((8, d), F32)],
        compiler_params=_params(("arbitrary",)),
    )(x)


def _rope(x, c, a, b):
    return x * c + pltpu.roll(x, LANES - ROPE_DIM // 2, 1) * a + pltpu.roll(x, ROPE_DIM // 2, 1) * b


def _rope_bwd(d, c, a, b):
    return d * c + pltpu.roll(d * a, ROPE_DIM // 2, 1) + pltpu.roll(d * b, LANES - ROPE_DIM // 2, 1)


S_CQ, S_CKV, S_KR, S_F, S_END = 0, Q_RANK, Q_RANK + KV_RANK, Q_RANK + KV_RANK + LANES, 1024
HW = N_HEADS * LANES


def mla_prep(small, g_q, g_kv, w_uq, w_ukv, tab_c, tab_a, tab_b, b_f, *, name, bt=512):
    t = small.shape[0]
    bt = min(bt, t)

    def body(s_ref, gq_ref, gkv_ref, wq_ref, wkv_ref, c_ref, a_ref, b_ref, bf_ref,
             mq_ref, mk_ref, mv_ref, lf_ref, cqn_ref, ckvn_ref):
        cq = s_ref[:, S_CQ:S_CKV]
        rq = lax.rsqrt(jnp.mean(cq * cq, axis=-1, keepdims=True) + EPS)
        cqn = (cq * rq * gq_ref[...]).astype(BF16)
        ckv = s_ref[:, S_CKV:S_KR]
        rkv = lax.rsqrt(jnp.mean(ckv * ckv, axis=-1, keepdims=True) + EPS)
        ckvn = (ckv * rkv * gkv_ref[...]).astype(BF16)
        cqn_ref[...] = cqn
        ckvn_ref[...] = ckvn
        tc, ta, tb = c_ref[...], a_ref[...], b_ref[...]
        q = jnp.dot(cqn, wq_ref[...], preferred_element_type=F32)
        kv = jnp.dot(ckvn, wkv_ref[...], preferred_element_type=F32)
        kr = _rope(s_ref[:, S_KR:S_F], tc, ta, tb)
        for h in range(N_HEADS):
            sl = slice(h * LANES, (h + 1) * LANES)
            mq_ref[:, sl] = _rope(q[:, sl], tc, ta, tb).astype(BF16)
            mk_ref[:, sl] = (kv[:, sl] + kr).astype(BF16)
        mv_ref[...] = kv[:, HW:].astype(BF16)
        z = s_ref[:, S_F:S_END - LANES] + bf_ref[...]
        lf_ref[...] = jnp.minimum(z, 0.0) - jnp.log(1.0 + jnp.exp(-jnp.abs(z)))

    def row(w):
        return pl.BlockSpec((bt, w), lambda i: (i, 0))

    def full(arr):
        return pl.BlockSpec(arr.shape, lambda i: (0, 0))

    return pl.pallas_call(
        body, name=name, grid=(t // bt,),
        in_specs=[row(S_END), full(g_q), full(g_kv), full(w_uq), full(w_ukv), row(LANES), row(LANES), row(LANES), full(b_f)],
        out_specs=[row(HW), row(HW), row(HW), row(LANES), row(Q_RANK), row(KV_RANK)],
        out_shape=[jax.ShapeDtypeStruct((t, HW), BF16)] * 3 + [jax.ShapeDtypeStruct((t, LANES), F32),
                   jax.ShapeDtypeStruct((t, Q_RANK), BF16), jax.ShapeDtypeStruct((t, KV_RANK), BF16)],
        compiler_params=_params(("parallel",)),
    )(small, g_q, g_kv, w_uq, w_ukv, tab_c, tab_a, tab_b, b_f)


def mla_prep_bwd(dmq, dmk, dmv, dlf, small, g_q, g_kv, w_uq, w_ukv, tab_c, tab_a, tab_b, b_f, *, name, bt=512):
    t = small.shape[0]
    bt = min(bt, t)

    def body(dmq_ref, dmk_ref, dmv_ref, dlf_ref, s_ref, gq_ref, gkv_ref, wq_ref, wkv_ref, c_ref, a_ref, b_ref, bf_ref,
             ds_ref, dq_ref, dkv_ref, dgq_ref, dgkv_ref, db_ref):
        tc, ta, tb = c_ref[...], a_ref[...], b_ref[...]
        lane = lax.broadcasted_iota(jnp.int32, (1, LANES), 1)
        dkr = jnp.zeros((bt, LANES), F32)
        for h in range(N_HEADS):
            sl = slice(h * LANES, (h + 1) * LANES)
            dq_ref[:, sl] = _rope_bwd(dmq_ref[:, sl], tc, ta, tb).astype(BF16)
            dkr = dkr + dmk_ref[:, sl]
        dkv_ref[:, :HW] = dmk_ref[...].astype(BF16)
        dkv_ref[:, HW:] = dmv_ref[...].astype(BF16)
        in_rope = (lane >= HEAD_DIM) & (lane < HEAD_DIM + ROPE_DIM)
        ds_ref[:, S_KR:S_F] = jnp.where(in_rope, _rope_bwd(dkr, tc, ta, tb), 0.0).astype(BF16)

        def norm_bwd(raw, g_ref, dn, dg_ref):
            r = lax.rsqrt(jnp.mean(raw * raw, axis=-1, keepdims=True) + EPS)
            u = dn * g_ref[...]
            dot = jnp.mean(u * raw, axis=-1, keepdims=True)
            dg_ref[...] += jnp.sum(dn * (raw * r), axis=0, keepdims=True)
            return r * u - raw * (r * r * r * dot)

        @pl.when(pl.program_id(0) == 0)
        def _():
            dgq_ref[...] = jnp.zeros_like(dgq_ref)
            dgkv_ref[...] = jnp.zeros_like(dgkv_ref)
            db_ref[...] = jnp.zeros_like(db_ref)

        dcqn = lax.dot_general(dq_ref[...], wq_ref[...], NT, preferred_element_type=F32)
        ds_ref[:, S_CQ:S_CKV] = norm_bwd(s_ref[:, S_CQ:S_CKV], gq_ref, dcqn, dgq_ref).astype(BF16)
        dckvn = lax.dot_general(dkv_ref[...], wkv_ref[...], NT, preferred_element_type=F32)
        ds_ref[:, S_CKV:S_KR] = norm_bwd(s_ref[:, S_CKV:S_KR], gkv_ref, dckvn, dgkv_ref).astype(BF16)
        z = s_ref[:, S_F:S_END - LANES] + bf_ref[...]
        dz = jnp.where(lane < N_HEADS, dlf_ref[...] / (1.0 + jnp.exp(z)), 0.0)
        db_ref[...] += jnp.sum(dz, axis=0, keepdims=True)
        ds_ref[:, S_F:S_END - LANES] = dz.astype(BF16)
        ds_ref[:, S_END - LANES:] = jnp.zeros((bt, LANES), BF16)

    def row(w):
        return pl.BlockSpec((bt, w), lambda i: (i, 0))

    def full(arr):
        return pl.BlockSpec(arr.shape, lambda i: (0, 0))

    def vec(w):
        return pl.BlockSpec((1, w), lambda i: (0, 0))

    return pl.pallas_call(
        body, name=name, grid=(t // bt,),
        in_specs=[row(HW), row(HW), row(HW), row(LANES), row(S_END), full(g_q), full(g_kv), full(w_uq), full(w_ukv),
                  row(LANES), row(LANES), row(LANES), full(b_f)],
        out_specs=[row(S_END), row(HW), row(2 * HW), vec(Q_RANK), vec(KV_RANK), vec(LANES)],
        out_shape=[jax.ShapeDtypeStruct((t, S_END), BF16), jax.ShapeDtypeStruct((t, HW), BF16),
                   jax.ShapeDtypeStruct((t, 2 * HW), BF16), jax.ShapeDtypeStruct((1, Q_RANK), F32),
                   jax.ShapeDtypeStruct((1, KV_RANK), F32), jax.ShapeDtypeStruct((1, LANES), F32)],
        compiler_params=_params(("arbitrary",)),
    )(dmq, dmk, dmv, dlf, small, g_q, g_kv, w_uq, w_ukv, tab_c, tab_a, tab_b, b_f)


def _lane():
    return lax.broadcasted_iota(jnp.int32, (1, LANES), 1)


def _halves(x):
    zero = jnp.zeros_like(x)
    return [jnp.where(_lane() < HEAD_DIM, x, zero), jnp.where(_lane() >= HEAD_DIM, x, zero)]


def _groups(x):
    return [x[:, :LANES], x[:, LANES:]]


def _lanes01(a, b, rest):
    return jnp.where(_lane() == 0, a, jnp.where(_lane() == 1, b, rest))


def _pick_row(tile, h):
    row = lax.broadcasted_iota(jnp.int32, (tile.shape[0], 1), 0)
    return jnp.sum(jnp.where(row == h, tile, 0.0), axis=0, keepdims=True)


def _below_diagonal(s):
    r = lax.broadcasted_iota(jnp.int32, s.shape, 0)
    c = lax.broadcasted_iota(jnp.int32, s.shape, 1)
    return jnp.where(c <= r, s, -jnp.inf)


def flash_fwd(q_arr, k_arr, v_arr, f_blocks, *, qoff, koff, voff, pair, scale, name, blk=512):
    t = q_arr.shape[0]
    blk = min(blk, t)
    nb = t // blk
    w = LANES if pair else 2 * LANES
    has_bias = f_blocks is not None
    split = _halves if pair else _groups

    def body(*refs):
        if has_bias:
            q_ref, k_ref, v_ref, f_ref, o_ref, st_ref = refs
        else:
            q_ref, k_ref, v_ref, o_ref, st_ref = refs
        g, i = pl.program_id(0), pl.program_id(1)
        qs = split((q_ref[...].astype(F32) * (scale * LOG2E)).astype(BF16))

        def step(j, carry, diagonal):
            rows = pl.ds(pl.multiple_of(j * blk, blk), blk)
            kk, vv = k_ref[rows, :], v_ref[rows, :]
            ks, vs = ([kk, kk], [vv, vv]) if pair else (_groups(kk), _groups(vv))
            out = []
            for n in range(2):
                m, l, acc = carry[n]
                s = lax.dot_general(qs[n], ks[n], NT, preferred_element_type=F32)
                if has_bias:
                    s = s - LOG2E * _pick_row(f_ref[j], 2 * g + n)
                if diagonal:
                    s = _below_diagonal(s)
                m_new = jnp.maximum(m, jnp.max(s, axis=1, keepdims=True))
                alpha = jnp.exp2(m - m_new)
                p = jnp.exp2(s - m_new)
                out.append((m_new, alpha * l + jnp.sum(p, axis=1, keepdims=True),
                            alpha * acc + jnp.dot(p.astype(BF16), vs[n], preferred_element_type=F32)))
            return tuple(out)

        init = tuple((jnp.full((blk, 1), -jnp.inf, F32), jnp.zeros((blk, 1), F32), jnp.zeros((blk, LANES), F32))
                     for _ in range(2))
        carry = lax.fori_loop(0, i, lambda j, c: step(j, c, False), init)
        (ma, la, acca), (mb, lb, accb) = step(i, carry, True)
        oa, ob = acca / la, accb / lb
        o_ref[...] = jnp.where(_lane() < HEAD_DIM, oa, ob) if pair else oa + pltpu.roll(ob, HEAD_DIM, 1)
        st_ref[0] = _lanes01(ma + jnp.log2(la), mb + jnp.log2(lb), 0.0)

    in_specs = [pl.BlockSpec((blk, w), lambda g, i: (i, qoff + g)), pl.BlockSpec((t, w), lambda g, i: (0, koff + g)),
                pl.BlockSpec((t, w), lambda g, i: (0, voff + g))]
    ins = [q_arr, k_arr, v_arr]
    if has_bias:
        in_specs.append(pl.BlockSpec((nb, N_HEADS, blk), lambda g, i: (0, 0, 0)))
        ins.append(f_blocks)
    return pl.pallas_call(
        body, name=name, grid=(PAIRS, nb), in_specs=in_specs,
        out_specs=[pl.BlockSpec((blk, LANES), lambda g, i: (i, g)), pl.BlockSpec((1, blk, LANES), lambda g, i: (g, i, 0))],
        out_shape=[jax.ShapeDtypeStruct((t, PAIRS * LANES), F32), jax.ShapeDtypeStruct((PAIRS, t, LANES), F32)],
        compiler_params=_params(("parallel", "arbitrary")),
    )(*ins)


def attn_stats(o, do, st, *, name, bt=512):
    t = o.shape[0]
    bt = min(bt, t)

    def body(o_ref, do_ref, st_ref, out_ref):
        prod = o_ref[...] * do_ref[...].astype(F32)
        for g in range(PAIRS):
            grp = prod[:, g * LANES:(g + 1) * LANES]
            da = jnp.sum(jnp.where(_lane() < HEAD_DIM, grp, 0.0), axis=1, keepdims=True)
            db = jnp.sum(jnp.where(_lane() >= HEAD_DIM, grp, 0.0), axis=1, keepdims=True)
            out_ref[g] = jnp.where(_lane() == 2, da, jnp.where(_lane() == 3, db, st_ref[g]))

    row = pl.BlockSpec((bt, PAIRS * LANES), lambda i: (i, 0))
    stat = pl.BlockSpec((PAIRS, bt, LANES), lambda i: (0, i, 0))
    return pl.pallas_call(
        body, name=name, grid=(t // bt,), in_specs=[row, row, stat], out_specs=stat,
        out_shape=jax.ShapeDtypeStruct(st.shape, F32),
        compiler_params=_params(("parallel",)),
    )(o, do, st)


def flash_bwd(q_arr, k_arr, v_arr, do_arr, st, f_blocks, *, qoff, koff, voff, pair, scale, name, blk=512):
    t = q_arr.shape[0]
    blk = min(blk, t)
    nb = t // blk
    w = LANES if pair else 2 * LANES
    has_bias = f_blocks is not None
    split = _halves if pair else _groups

    def body(*refs):
        if has_bias:
            q_ref, k_ref, v_ref, do_ref, st_ref, f_ref, dq_ref, dk_ref, dv_ref, dfk_ref, dfq_ref = refs
        else:
            q_ref, k_ref, v_ref, do_ref, st_ref, dq_ref, dk_ref, dv_ref = refs
        g, j = pl.program_id(0), pl.program_id(1)
        kk, vv = k_ref[...], v_ref[...]
        ks, vs = ([kk, kk], [vv, vv]) if pair else (_groups(kk), _groups(vv))
        if has_bias:
            fk = [LOG2E * _pick_row(f_ref[0], 2 * g + n) for n in range(2)]

        def step(i, carry, diagonal):
            rows = pl.ds(pl.multiple_of(i * blk, blk), blk)
            qs = split((q_ref[rows, :].astype(F32) * (scale * LOG2E)).astype(BF16))
            da, db = _halves(do_ref[rows, :].astype(F32))
            dos = [da.astype(BF16), (db if pair else pltpu.roll(db, HEAD_DIM, 1)).astype(BF16)]
            stats = st_ref[0, rows, :]
            new, dqs, row_sums = [], [], []
            for n in range(2):
                dk, dv, dfk = carry[n]
                s = lax.dot_general(qs[n], ks[n], NT, preferred_element_type=F32)
                if has_bias:
                    s = s - fk[n]
                if diagonal:
                    s = _below_diagonal(s)
                p = jnp.exp2(s - stats[:, n:n + 1])
                dp = lax.dot_general(dos[n], vs[n], NT, preferred_element_type=F32)
                ds = p * (dp - stats[:, 2 + n:3 + n])
                dsb = ds.astype(BF16)
                dv = dv + lax.dot_general(p.astype(BF16), dos[n], TN, preferred_element_type=F32)
                dk = dk + lax.dot_general(dsb, qs[n], TN, preferred_element_type=F32)
                dqs.append(jnp.dot(dsb, ks[n], preferred_element_type=F32))
                if has_bias:
                    dfk = dfk - jnp.sum(ds, axis=0, keepdims=True)
                    row_sums.append(jnp.sum(ds, axis=1, keepdims=True))
                new.append((dk, dv, dfk))
            dq = (jnp.where(_lane() < HEAD_DIM, dqs[0], dqs[1]) if pair else jnp.concatenate(dqs, axis=1)) * scale
            rs = _lanes01(row_sums[0], row_sums[1], 0.0) if has_bias else None

            @pl.when(j == 0)
            def _():
                dq_ref[rows, :] = dq
                if has_bias:
                    dfq_ref[0, rows, :] = rs

            @pl.when(j > 0)
            def _():
                dq_ref[rows, :] += dq
                if has_bias:
                    dfq_ref[0, rows, :] += rs

            return tuple(new)

        init = tuple((jnp.zeros((blk, LANES), F32), jnp.zeros((blk, LANES), F32), jnp.zeros((1, blk), F32)) for _ in range(2))
        carry = step(j, init, True)
        (dka, dva, dfa), (dkb, dvb, dfb) = lax.fori_loop(j + 1, nb, lambda i, c: step(i, c, False), carry)
        if pair:
            dk_ref[...] = (dka + dkb) * LN2
            dv_ref[...] = dva + dvb
        else:
            dk_ref[...] = jnp.concatenate([dka, dkb], axis=1) * LN2
            dv_ref[...] = jnp.concatenate([dva, dvb], axis=1)
        if has_bias:
            row = lax.broadcasted_iota(jnp.int32, (N_HEADS, 1), 0)
            dfk_ref[0] = jnp.where(row == 0, dfa, jnp.where(row == 1, dfb, 0.0))

    in_specs = [pl.BlockSpec((t, w), lambda g, j: (0, qoff + g)), pl.BlockSpec((blk, w), lambda g, j: (j, koff + g)),
                pl.BlockSpec((blk, w), lambda g, j: (j, voff + g)), pl.BlockSpec((t, LANES), lambda g, j: (0, g)),
                pl.BlockSpec((1, t, LANES), lambda g, j: (g, 0, 0))]
    ins = [q_arr, k_arr, v_arr, do_arr, st]
    out_specs = [pl.BlockSpec((t, w), lambda g, j: (0, g)), pl.BlockSpec((blk, w), lambda g, j: (j, g)),
                 pl.BlockSpec((blk, w), lambda g, j: (j, g))]
    out_shape = [jax.ShapeDtypeStruct((t, PAIRS * w), F32)] * 3
    if has_bias:
        in_specs.append(pl.BlockSpec((1, N_HEADS, blk), lambda g, j: (j, 0, 0)))
        ins.append(f_blocks)
        out_specs += [pl.BlockSpec((1, N_HEADS, blk), lambda g, j: (g, 0, j)), pl.BlockSpec((1, t, LANES), lambda g, j: (g, 0, 0))]
        out_shape += [jax.ShapeDtypeStruct((PAIRS, N_HEADS, t), F32), jax.ShapeDtypeStruct((PAIRS, t, LANES), F32)]
    return pl.pallas_call(
        body, name=name, grid=(PAIRS, nb), in_specs=in_specs, out_specs=out_specs, out_shape=out_shape,
        compiler_params=_params(("parallel", "arbitrary")),
    )(*ins)


def loss_head(x2, g, tgt, *, name, bt=512):
    t, d = x2.shape
    bt = min(bt, t)

    def body(x_ref, g_ref, t_ref, loss_ref, dx_ref, dxb_ref, dg_ref):
        @pl.when(pl.program_id(0) == 0)
        def _():
            loss_ref[...] = jnp.zeros_like(loss_ref)
            dg_ref[...] = jnp.zeros_like(dg_ref)

        xv = x_ref[...]
        gv = g_ref[...]
        r = lax.rsqrt(jnp.mean(xv * xv, axis=-1, keepdims=True) + EPS)
        xh = xv * r
        e = xh * gv - t_ref[...]
        loss_ref[...] += 0.5 * jnp.sum(jnp.mean(e * e, axis=-1, keepdims=True), axis=0, keepdims=True)
        dy = e * (1.0 / d)
        dg_ref[...] += jnp.sum(dy * xh, axis=0, keepdims=True)
        u = dy * gv
        dx = r * u - xv * (r * r * r * jnp.mean(u * xv, axis=-1, keepdims=True))
        dx_ref[...] = dx
        dxb_ref[...] = dx.astype(BF16)

    row = pl.BlockSpec((bt, d), lambda i: (i, 0))
    vec = pl.BlockSpec((1, d), lambda i: (0, 0))
    return pl.pallas_call(
        body, name=name, grid=(t // bt,), in_specs=[row, vec, row],
        out_specs=[pl.BlockSpec((1, 1), lambda i: (0, 0)), row, row, vec],
        out_shape=[jax.ShapeDtypeStruct((1, 1), F32), jax.ShapeDtypeStruct((t, d), F32),
                   jax.ShapeDtypeStruct((t, d), BF16), jax.ShapeDtypeStruct((1, d), F32)],
        compiler_params=_params(("arbitrary",)),
    )(x2, g, tgt)


def adamw(w, g, m, v, *, name):
    rws, cols = w.shape
    br = next(b for b in (256, 128, 64, 32, 16, 8) if rws % b == 0)

    def body(w_ref, g_ref, m_ref, v_ref, d_ref, nm_ref, nv_ref):
        gv = g_ref[...]
        nm = ADAM_B1 * m_ref[...] + (1.0 - ADAM_B1) * gv
        nv = ADAM_B2 * v_ref[...] + (1.0 - ADAM_B2) * (gv * gv)
        m_hat = nm / (1.0 - ADAM_B1 ** ADAM_STEP)
        v_hat = nv / (1.0 - ADAM_B2 ** ADAM_STEP)
        d_ref[...] = -ADAM_LR * (m_hat / (jnp.sqrt(v_hat) + ADAM_EPS) + ADAM_WD * w_ref[...])
        nm_ref[...] = nm
        nv_ref[...] = nv

    blk = pl.BlockSpec((br, cols), lambda i: (i, 0))
    return pl.pallas_call(
        body, name=name, grid=(rws // br,), in_specs=[blk] * 4, out_specs=[blk] * 3,
        out_shape=[jax.ShapeDtypeStruct((rws, cols), F32)] * 3,
        compiler_params=_params(("parallel",)),
    )(w, g, m, v)


def add_pair(a, b, *, name):
    n, r, c = a.shape
    br = r // 2

    def body(a_ref, b_ref, o_ref):
        o_ref[...] = (a_ref[...].astype(F32) + b_ref[...].astype(F32)).astype(o_ref.dtype)

    blk = pl.BlockSpec((1, br, c), lambda q, i: (q, i, 0))
    return pl.pallas_call(
        body, name=name, grid=(n, r // br), in_specs=[blk, blk], out_specs=blk,
        out_shape=jax.ShapeDtypeStruct((n, r, c), BF16),
        compiler_params=_params(("parallel", "parallel")),
    )(a, b)


def sum_chips(parts, *, name):
    n, r, c = parts.shape
    br = r // 2

    def body(p_ref, o_ref):
        acc = p_ref[0].astype(F32)
        for q in range(1, n):
            acc = acc + p_ref[q].astype(F32)
        o_ref[...] = acc

    return pl.pallas_call(
        body, name=name, grid=(r // br,),
        in_specs=[pl.BlockSpec((n, br, c), lambda i: (0, i, 0))], out_specs=pl.BlockSpec((br, c), lambda i: (i, 0)),
        out_shape=jax.ShapeDtypeStruct((r, c), F32),
        compiler_params=_params(("parallel",)),
    )(parts)


ANY = pl.BlockSpec(memory_space=pl.ANY)


def _place():
    x, y, c = lax.axis_index("x"), lax.axis_index("y"), lax.axis_index("c")
    chips = [(1 - x, y), (x, 1 - y), (1 - x, 1 - y)]
    return x, y, c, chips


def gather_shards(wpack):
    r, cdim = wpack.shape
    hr = r // 2

    def body(w_ref, out_ref, send_sems, recv_sems):
        x, y, c, chips = _place()
        q = 2 * x + y
        sibling = (x, y, 1 - c)

        def half(qq, hf):
            return out_ref.at[qq, pl.ds(hf * hr, hr), :]

        def copy(k, src, dst, to):
            return pltpu.make_async_remote_copy(src_ref=src, dst_ref=dst, send_sem=send_sems.at[k],
                                                recv_sem=recv_sems.at[k], device_id=to, device_id_type=MESH)

        first = [copy(j, w_ref.at[pl.ds(c * hr, hr), :], half(q, c), (cx, cy, c)) for j, (cx, cy) in enumerate(chips)]
        for cp in first:
            cp.start()
        passed = []
        for j, (cx, cy) in enumerate(chips):
            qj = 2 * cx + cy
            copy(j, half(qj, c), half(qj, c), (cx, cy, c)).wait_recv()
            fw = copy(3 + j, half(qj, c), half(qj, c), sibling)
            fw.start()
            passed.append(fw)
        for j, (cx, cy) in enumerate(chips):
            qj = 2 * cx + cy
            copy(3 + j, half(qj, 1 - c), half(qj, 1 - c), sibling).wait_recv()
        for cp in first + passed:
            cp.wait_send()

    return pl.pallas_call(
        body, name="gather_shards", in_specs=[ANY], out_specs=ANY,
        out_shape=jax.ShapeDtypeStruct((N_CHIPS, r, cdim), wpack.dtype),
        scratch_shapes=[pltpu.SemaphoreType.DMA((6,)), pltpu.SemaphoreType.DMA((6,))],
    )(wpack)


def swap_halves(g2):
    def body(g_ref, out_ref, send_sem, recv_sem):
        x, y, c, _ = _place()
        cp = pltpu.make_async_remote_copy(src_ref=g_ref.at[1 - c], dst_ref=out_ref, send_sem=send_sem, recv_sem=recv_sem,
                                          device_id=(x, y, 1 - c), device_id_type=MESH)
        cp.start()
        cp.wait()

    return pl.pallas_call(
        body, name="swap_halves", in_specs=[ANY], out_specs=ANY,
        out_shape=jax.ShapeDtypeStruct(g2.shape[1:], g2.dtype),
        scratch_shapes=[pltpu.SemaphoreType.DMA, pltpu.SemaphoreType.DMA],
    )(g2)


def scatter_to_owners(p):
    def body(p_ref, out_ref, send_sems, recv_sems):
        x, y, c, chips = _place()
        q = 2 * x + y
        sends = []
        for j, (cx, cy) in enumerate(chips):
            cp = pltpu.make_async_remote_copy(src_ref=p_ref.at[2 * cx + cy], dst_ref=out_ref.at[q], send_sem=send_sems.at[j],
                                              recv_sem=recv_sems.at[j], device_id=(cx, cy, c), device_id_type=MESH)
            cp.start()
            sends.append(cp)
        for j, (cx, cy) in enumerate(chips):
            qj = 2 * cx + cy
            pltpu.make_async_remote_copy(src_ref=p_ref.at[qj], dst_ref=out_ref.at[qj], send_sem=send_sems.at[j],
                                         recv_sem=recv_sems.at[j], device_id=(cx, cy, c), device_id_type=MESH).wait_recv()
        for cp in sends:
            cp.wait_send()

    return pl.pallas_call(
        body, name="scatter_to_owners", in_specs=[ANY], out_specs=ANY,
        out_shape=jax.ShapeDtypeStruct(p.shape, p.dtype),
        scratch_shapes=[pltpu.SemaphoreType.DMA((3,)), pltpu.SemaphoreType.DMA((3,))],
    )(p)


def swap_with_sibling(rh):
    def body(r_ref, out_ref, send_sem, recv_sem):
        x, y, c, _ = _place()
        cp = pltpu.make_async_remote_copy(src_ref=r_ref, dst_ref=out_ref, send_sem=send_sem, recv_sem=recv_sem,
                                          device_id=(x, y, 1 - c), device_id_type=MESH)
        cp.start()
        cp.wait()

    return pl.pallas_call(
        body, name="swap_with_sibling", in_specs=[ANY], out_specs=ANY,
        out_shape=jax.ShapeDtypeStruct(rh.shape, rh.dtype),
        scratch_shapes=[pltpu.SemaphoreType.DMA, pltpu.SemaphoreType.DMA],
    )(rh)


def allreduce_small(s):
    n_dev = 8

    def body(s_ref, out_ref, buf, send_sems, recv_sems):
        x, y, c, _ = _place()
        me = 4 * x + 2 * y + c
        buf[me] = s_ref[...]
        sends = []
        for k in range(1, n_dev):
            px = 1 - x if k & 4 else x
            py = 1 - y if k & 2 else y
            pc = 1 - c if k & 1 else c
            cp = pltpu.make_async_remote_copy(src_ref=s_ref, dst_ref=buf.at[me], send_sem=send_sems.at[k - 1],
                                              recv_sem=recv_sems.at[k - 1], device_id=(px, py, pc), device_id_type=MESH)
            cp.start()
            sends.append((cp, 4 * px + 2 * py + pc))
        for k, (cp, peer) in enumerate(sends):
            pltpu.make_async_remote_copy(src_ref=s_ref, dst_ref=buf.at[peer], send_sem=send_sems.at[k],
                                         recv_sem=recv_sems.at[k], device_id=(x, y, c), device_id_type=MESH).wait_recv()
        for cp, _ in sends:
            cp.wait_send()
        acc = buf[0]
        for d in range(1, n_dev):
            acc = acc + buf[d]
        out_ref[...] = acc

    vm = pl.BlockSpec(memory_space=pltpu.VMEM)
    return pl.pallas_call(
        body, name="allreduce_small", in_specs=[vm], out_specs=vm,
        out_shape=jax.ShapeDtypeStruct(s.shape, F32),
        scratch_shapes=[pltpu.VMEM((n_dev,) + s.shape, F32), pltpu.SemaphoreType.DMA((n_dev - 1,)),
                        pltpu.SemaphoreType.DMA((n_dev - 1,))],
    )(s)


PACK_COLS = 1024
COL_SHARDED = (True, True, True, False, True, False)


def _pack_rows(shapes):
    rows = [s[0] * s[1] // PACK_COLS for s in shapes]
    total = sum(rows)
    return rows, -(-total // 16) * 16


def pack_shards(shards):
    rows, padded = _pack_rows([s.shape for s in shards])
    parts = [s.reshape(r, PACK_COLS) for s, r in zip(shards, rows)]
    parts.append(jnp.zeros((padded - sum(rows), PACK_COLS), shards[0].dtype))
    return jnp.concatenate(parts, axis=0)


def unpack_shards(pack, shapes):
    rows, _ = _pack_rows(shapes)
    out, at = [], 0
    for s, r in zip(shapes, rows):
        out.append(pack[at:at + r].reshape(s))
        at += r
    return out


def unpack_full(wall, shapes):
    rows, _ = _pack_rows(shapes)
    out, at = [], 0
    for (k, n), r, by_col in zip(shapes, rows, COL_SHARDED):
        blk = wall[:, at:at + r].reshape(N_CHIPS, k, n)
        out.append(blk.transpose(1, 0, 2).reshape(k, N_CHIPS * n) if by_col else blk.reshape(N_CHIPS * k, n))
        at += r
    return out


def pack_full(fulls, shapes, dtype):
    rows, padded = _pack_rows(shapes)
    parts = []
    for f, (k, n), r, by_col in zip(fulls, shapes, rows, COL_SHARDED):
        blk = f.reshape(k, N_CHIPS, n).transpose(1, 0, 2) if by_col else f.reshape(N_CHIPS, k, n)
        parts.append(blk.reshape(N_CHIPS, r, PACK_COLS).astype(dtype))
    parts.append(jnp.zeros((N_CHIPS, padded - sum(rows), PACK_COLS), dtype))
    return jnp.concatenate(parts, axis=1)


def _pad_heads(w, width):
    lead = w.shape[:-1]
    w = w.reshape(lead + (N_HEADS, width))
    return jnp.pad(w, [(0, 0)] * len(lead) + [(0, 0), (0, LANES - width)]).reshape(lead + (HW,))


def _unpad_heads(w, width):
    lead = w.shape[:-1]
    return w.reshape(lead + (N_HEADS, LANES))[..., :width].reshape(lead + (N_HEADS * width,))


FW = N_HEADS * HEAD_DIM


def split_w_in(w_in):
    d = w_in.shape[0]
    o_f = 3 * FW
    o_cq = o_f + N_HEADS
    o_ckv = o_cq + Q_RANK
    o_kr = o_ckv + KV_RANK

    def z(n):
        return jnp.zeros((d, n), w_in.dtype)

    small = jnp.concatenate([w_in[:, o_cq:o_ckv], w_in[:, o_ckv:o_kr], z(HEAD_DIM), w_in[:, o_kr:], z(LANES - HEAD_DIM - ROPE_DIM),
                             w_in[:, o_f:o_cq], z(LANES - N_HEADS), z(LANES)], axis=1)
    return w_in[:, :o_f], small


def join_w_in(d_qkv, d_small):
    kr = S_KR + HEAD_DIM
    return jnp.concatenate([d_qkv, d_small[:, S_F:S_F + N_HEADS], d_small[:, S_CQ:S_CKV], d_small[:, S_CKV:S_KR],
                            d_small[:, kr:kr + ROPE_DIM]], axis=1)


def rope_tables(pos):
    t = pos.shape[0]
    inv_freq = ROPE_THETA ** (-jnp.arange(0, ROPE_DIM, 2, dtype=F32) / ROPE_DIM)
    ang = pos.astype(F32)[:, None] * inv_freq
    cos, sin = jnp.cos(ang), jnp.sin(ang)
    half = ROPE_DIM // 2

    def z(n):
        return jnp.zeros((t, n), F32)

    tab_c = jnp.concatenate([jnp.ones((t, HEAD_DIM), F32), cos, cos, z(LANES - HEAD_DIM - ROPE_DIM)], axis=1)
    tab_a = jnp.concatenate([z(HEAD_DIM), -sin, z(half), z(LANES - HEAD_DIM - ROPE_DIM)], axis=1)
    tab_b = jnp.concatenate([z(HEAD_DIM), z(half), sin, z(LANES - HEAD_DIM - ROPE_DIM)], axis=1)
    return tab_c, tab_a, tab_b


def _pad_lanes(v, n):
    return jnp.pad(v, ((0, 0), (0, n - v.shape[1])))


ATTN_BLK = 512


def local_step(xs, pos, tgt, gains, weights):
    g_attn, b_forget, g_q, g_kv, g_fo, g_mo, g_mlp, g_fin = gains
    w_in, w_uq, w_ukv, w_o, w_up, w_down = weights
    t = xs.shape[0]
    blk = min(ATTN_BLK, t)
    fox_scale = 1.0 / (HEAD_DIM ** 0.5)
    mla_scale = 1.0 / ((HEAD_DIM + ROPE_DIM) ** 0.5)

    w_qkv, w_small = split_w_in(w_in)
    w_uq_p = _pad_heads(w_uq, HEAD_DIM + ROPE_DIM)
    kv = w_ukv.reshape(KV_RANK, N_HEADS, 2 * HEAD_DIM)
    w_ukv_p = jnp.concatenate([_pad_heads(kv[:, :, :HEAD_DIM].reshape(KV_RANK, FW), HEAD_DIM),
                               _pad_heads(kv[:, :, HEAD_DIM:].reshape(KV_RANK, FW), HEAD_DIM)], axis=1)
    w_of, w_om = w_o[:FW], w_o[FW:]
    b_f = _pad_lanes(b_forget, LANES)
    tab_c, tab_a, tab_b = rope_tables(pos)

    h1 = rmsnorm(xs, g_attn, out_dtype=BF16, name="norm_attn")
    qkv, = mm(h1, w_qkv, out_dtypes=[BF16], name="proj_qkv")
    small, = mm(h1, w_small, out_dtypes=[F32], name="proj_small")
    mq, mk, mv, lf, cqn, ckvn = mla_prep(small, g_q, g_kv, w_uq_p, w_ukv_p, tab_c, tab_a, tab_b, b_f, name="mla_prep")
    f_cum = cumsum_rows(lf, reverse=False, name="gate_cumsum")
    f_blocks = f_cum[:, :N_HEADS].reshape(t // blk, blk, N_HEADS).transpose(0, 2, 1)
    fo, st_f = flash_fwd(qkv, qkv, qkv, f_blocks, qoff=0, koff=PAIRS, voff=2 * PAIRS, pair=True, scale=fox_scale,
                         name="fox_fwd", blk=blk)
    mo, st_m = flash_fwd(mq, mk, mv, None, qoff=0, koff=0, voff=0, pair=False, scale=mla_scale, name="mla_fwd", blk=blk)
    mix_f = rmsnorm(fo, g_fo, out_dtype=BF16, name="norm_fox_out")
    mix_m = rmsnorm(mo, g_mo, out_dtype=BF16, name="norm_mla_out")

    def add_res(acc, res):
        return (acc + res,)

    x1a, = mm(mix_f, w_of, extras=[xs], epilogue=add_res, out_dtypes=[F32], name="out_proj_fox")
    x1, = mm(mix_m, w_om, extras=[x1a], epilogue=add_res, out_dtypes=[F32], name="out_proj_mla")
    h2 = rmsnorm(x1, g_mlp, out_dtype=BF16, name="norm_mlp")

    def relu2(acc):
        r = jnp.maximum(acc, 0.0)
        return acc, r * r

    u, act = mm(h2, w_up, epilogue=relu2, out_dtypes=[BF16, BF16], name="mlp_up")
    x2, = mm(act, w_down, extras=[x1], epilogue=add_res, out_dtypes=[F32], name="mlp_down")
    loss, dx2, dx2b, dg_fin = loss_head(x2, g_fin, tgt, name="loss_head")

    def relu2_grad(acc, uu):
        return (acc * (2.0 * jnp.maximum(uu.astype(F32), 0.0)),)

    du, = mm(dx2b, w_down, trans_b=True, extras=[u], epilogue=relu2_grad, out_dtypes=[BF16], name="mlp_down_bwd")
    dw_down = mm_tn(act, dx2b, name="dw_down")
    dh2, = mm(du, w_up, trans_b=True, out_dtypes=[F32], name="mlp_up_bwd")
    dw_up = mm_tn(h2, du, name="dw_up")
    dx1, dx1b, dg_mlp = rmsnorm_bwd(dh2, x1, g_mlp, dx2, out_dtypes=[F32, BF16], name="norm_mlp_bwd")

    dmix_f, = mm(dx1b, w_of, trans_b=True, out_dtypes=[F32], name="out_proj_fox_bwd")
    dmix_m, = mm(dx1b, w_om, trans_b=True, out_dtypes=[F32], name="out_proj_mla_bwd")
    dw_o = jnp.concatenate([mm_tn(mix_f, dx1b, name="dw_o_fox"), mm_tn(mix_m, dx1b, name="dw_o_mla")], axis=0)
    dfo, dg_fo = rmsnorm_bwd(dmix_f, fo, g_fo, None, out_dtypes=[BF16], name="norm_fox_out_bwd")
    dmo, dg_mo = rmsnorm_bwd(dmix_m, mo, g_mo, None, out_dtypes=[BF16], name="norm_mla_out_bwd")
    st_f = attn_stats(fo, dfo, st_f, name="fox_stats")
    st_m = attn_stats(mo, dmo, st_m, name="mla_stats")
    dfq, dfk, dfv, df_key, df_query = flash_bwd(qkv, qkv, qkv, dfo, st_f, f_blocks, qoff=0, koff=PAIRS, voff=2 * PAIRS,
                                                pair=True, scale=fox_scale, name="fox_bwd", blk=blk)
    dmq, dmk, dmv = flash_bwd(mq, mk, mv, dmo, st_m, None, qoff=0, koff=0, voff=0, pair=False, scale=mla_scale,
                              name="mla_bwd", blk=blk)
    dqkv = jnp.concatenate([dfq, dfk, dfv], axis=1).astype(BF16)
    d_f = df_key[:, :2, :].reshape(N_HEADS, t).T + df_query[:, :, :2].transpose(1, 0, 2).reshape(t, N_HEADS)
    dlf = cumsum_rows(_pad_lanes(d_f, LANES), reverse=True, name="gate_cumsum_bwd")
    dsmall, dq_u, dkv_u, dg_q, dg_kv, db_f = mla_prep_bwd(dmq, dmk, dmv, dlf, small, g_q, g_kv, w_uq_p, w_ukv_p,
                                                          tab_c, tab_a, tab_b, b_f, name="mla_prep_bwd")
    dw_uq_p = mm_tn(cqn, dq_u, name="dw_uq")
    dw_ukv_p = mm_tn(ckvn, dkv_u, name="dw_ukv")

    dh1a, = mm(dqkv, w_qkv, trans_b=True, out_dtypes=[F32], name="proj_qkv_bwd")
    dh1, = mm(dsmall, w_small, trans_b=True, extras=[dh1a], epilogue=add_res, out_dtypes=[F32], name="proj_small_bwd")
    dw_qkv = mm_tn(h1, dqkv, name="dw_qkv")
    dw_small = mm_tn(h1, dsmall, name="dw_small")
    grad_x, dg_attn = rmsnorm_bwd(dh1, xs, g_attn, dx1, out_dtypes=[F32], name="norm_attn_bwd")

    dw_in = join_w_in(dw_qkv, dw_small)
    dw_uq = _unpad_heads(dw_uq_p, HEAD_DIM + ROPE_DIM)
    dk_cols = _unpad_heads(dw_ukv_p[:, :HW], HEAD_DIM).reshape(KV_RANK, N_HEADS, HEAD_DIM)
    dv_cols = _unpad_heads(dw_ukv_p[:, HW:], HEAD_DIM).reshape(KV_RANK, N_HEADS, HEAD_DIM)
    dw_ukv = jnp.concatenate([dk_cols, dv_cols], axis=2).reshape(KV_RANK, N_HEADS * 2 * HEAD_DIM)
    d_gains = (dg_attn, db_f[:, :N_HEADS], dg_q, dg_kv, dg_fo, dg_mo, dg_mlp, dg_fin)
    return loss, grad_x, (dw_in, dw_uq, dw_ukv, dw_o, dw_up, dw_down), d_gains


def kernel(x, positions, attn_norm_g, w_in, b_forget, q_norm_g, w_uq, kv_norm_g, w_ukv, fox_out_g, mla_out_g, w_o, mlp_norm_g, w_up, w_down, final_norm_g, loss_target, m_attn_norm_g, m_w_in, m_b_forget, m_q_norm_g, m_w_uq, m_kv_norm_g, m_w_ukv, m_fox_out_g, m_mla_out_g, m_w_o, m_mlp_norm_g, m_w_up, m_w_down, m_final_norm_g, v_attn_norm_g, v_w_in, v_b_forget, v_q_norm_g, v_w_uq, v_kv_norm_g, v_w_ukv, v_fox_out_g, v_mla_out_g, v_w_o, v_mlp_norm_g, v_w_up, v_w_down, v_final_norm_g):
    c = lax.axis_index("c")
    big = [w_in[0], w_uq[0], w_ukv[0], w_o[0], w_up[0], w_down[0]]
    big_m = [m_w_in[0], m_w_uq[0], m_w_ukv[0], m_w_o[0], m_w_up[0], m_w_down[0]]
    big_v = [v_w_in[0], v_w_uq[0], v_w_ukv[0], v_w_o[0], v_w_up[0], v_w_down[0]]
    shapes = [w.shape for w in big]

    def vec(a):
        return a.reshape(1, -1)

    small = [attn_norm_g, b_forget, q_norm_g, kv_norm_g, fox_out_g, mla_out_g, mlp_norm_g, final_norm_g]
    small_m = [m_attn_norm_g, m_b_forget, m_q_norm_g, m_kv_norm_g, m_fox_out_g, m_mla_out_g, m_mlp_norm_g, m_final_norm_g]
    small_v = [v_attn_norm_g, v_b_forget, v_q_norm_g, v_kv_norm_g, v_fox_out_g, v_mla_out_g, v_mlp_norm_g, v_final_norm_g]
    gains = [vec(a) for a in small]

    chip = 2 * lax.axis_index("x") + lax.axis_index("y")
    wpack = pack_shards([w.astype(BF16) for w in big])
    wall = lax.dynamic_update_index_in_dim(gather_shards(wpack), wpack, chip, 0)
    weights = unpack_full(wall, shapes)

    loss, grad_x, d_big, d_small = local_step(x[0], positions[0], loss_target[0], gains, weights)

    gp = pack_full(d_big, shapes, BF16)
    hr = gp.shape[1] // 2
    g2 = gp.reshape(N_CHIPS, 2, hr, PACK_COLS).transpose(1, 0, 2, 3)
    from_sibling = swap_halves(g2)
    kept = lax.dynamic_index_in_dim(g2, c, 0, keepdims=False)
    pair_sum = add_pair(kept, from_sibling, name="add_pair")
    own = lax.dynamic_index_in_dim(pair_sum, chip, 0, keepdims=False)
    by_chip = lax.dynamic_update_index_in_dim(scatter_to_owners(pair_sum), own, chip, 0)
    my_half = sum_chips(by_chip, name="sum_chips")
    other_half = swap_with_sibling(my_half)
    g_shard = jnp.concatenate([jnp.where(c == 0, my_half, other_half), jnp.where(c == 0, other_half, my_half)], axis=0)
    g_big = unpack_shards(g_shard, shapes)

    def rows8(vs):
        return jnp.concatenate([_pad_lanes(vec(a).astype(F32), PACK_COLS) for a in vs], axis=0)

    g_small8 = allreduce_small(rows8(d_small))

    outs_big = [adamw(w, g, m, v, name="adamw_%d" % n) for n, (w, g, m, v) in enumerate(zip(big, g_big, big_m, big_v))]
    d8, m8, v8 = adamw(rows8(small), g_small8, rows8(small_m), rows8(small_v), name="adamw_small")

    def unrows8(a8):
        return [a8[n, :s.size].reshape(s.shape) for n, s in enumerate(small)]

    loss_all = lax.psum(loss[0, 0], ("x", "y", "c"))
    grads, deltas, new_m, new_v = [None] * 14, [None] * 14, [None] * 14, [None] * 14
    big_at = [1, 4, 6, 9, 11, 12]
    small_at = [0, 2, 3, 5, 7, 8, 10, 13]
    for n, at in enumerate(big_at):
        grads[at] = g_big[n][None]
        deltas[at], new_m[at], new_v[at] = (a[None] for a in outs_big[n])
    for at, g, dd, mm_, vv in zip(small_at, unrows8(g_small8), unrows8(d8), unrows8(m8), unrows8(v8)):
        grads[at], deltas[at], new_m[at], new_v[at] = g, dd, mm_, vv
    return (loss_all, grad_x[None], *grads, *deltas, *new_m, *new_v)
```

```python
import jax
import jax.numpy as jnp
from jax import lax
from jax.experimental import pallas as pl
from jax.experimental.pallas import tpu as pltpu

F32 = jnp.float32
BF16 = jnp.bfloat16
MESH = pl.DeviceIdType.MESH

EPS = 1e-6
ROPE_THETA = 10000.0
N_HEADS = 8
PAIRS = N_HEADS // 2
HEAD_DIM = 64
ROPE_DIM = 32
LANES = 128
Q_RANK = 384
KV_RANK = 256
N_CHIPS = 4
ADAM_LR, ADAM_B1, ADAM_B2, ADAM_EPS, ADAM_WD, ADAM_STEP = 0.001, 0.9, 0.999, 1e-08, 0.01, 10
VMEM_LIMIT = 48 * 1024 * 1024
LOG2E = 1.4426950408889634
LN2 = 0.6931471805599453
NN = (((1,), (0,)), ((), ()))
NT = (((1,), (1,)), ((), ()))
TN = (((0,), (0,)), ((), ()))


def _params(sem=None):
    return pltpu.CompilerParams(dimension_semantics=sem, vmem_limit_bytes=VMEM_LIMIT)


def rmsnorm(x, g, *, out_dtype, name, bt=512):
    t, d = x.shape
    bt = min(bt, t)

    def body(x_ref, g_ref, o_ref):
        xv = x_ref[...].astype(F32)
        r = lax.rsqrt(jnp.mean(xv * xv, axis=-1, keepdims=True) + EPS)
        o_ref[...] = (xv * r * g_ref[...]).astype(o_ref.dtype)

    return pl.pallas_call(
        body, name=name, grid=(t // bt,),
        in_specs=[pl.BlockSpec((bt, d), lambda i: (i, 0)), pl.BlockSpec((1, d), lambda i: (0, 0))],
        out_specs=pl.BlockSpec((bt, d), lambda i: (i, 0)),
        out_shape=jax.ShapeDtypeStruct((t, d), out_dtype),
        compiler_params=_params(("parallel",)),
    )(x, g)


def rmsnorm_bwd(dh, x, g, res, *, out_dtypes, name, bt=512):
    t, d = x.shape
    bt = min(bt, t)
    has_res = res is not None

    def body(*refs):
        dh_ref, x_ref, g_ref = refs[:3]
        res_ref = refs[3] if has_res else None
        outs = refs[3 + has_res:]
        dx_refs, dg_ref = outs[:-1], outs[-1]
        xv = x_ref[...].astype(F32)
        dhv = dh_ref[...].astype(F32)
        r = lax.rsqrt(jnp.mean(xv * xv, axis=-1, keepdims=True) + EPS)
        u = dhv * g_ref[...]
        dot = jnp.mean(u * xv, axis=-1, keepdims=True)
        dx = r * u - xv * (r * r * r * dot)
        if has_res:
            dx = dx + res_ref[...]
        for o in dx_refs:
            o[...] = dx.astype(o.dtype)

        @pl.when(pl.program_id(0) == 0)
        def _():
            dg_ref[...] = jnp.zeros_like(dg_ref)

        dg_ref[...] += jnp.sum(dhv * (xv * r), axis=0, keepdims=True)

    row = pl.BlockSpec((bt, d), lambda i: (i, 0))
    vec = pl.BlockSpec((1, d), lambda i: (0, 0))
    ins = [dh, x, g] + ([res] if has_res else [])
    return pl.pallas_call(
        body, name=name, grid=(t // bt,),
        in_specs=[row, row, vec] + ([row] if has_res else []),
        out_specs=[row] * len(out_dtypes) + [vec],
        out_shape=[jax.ShapeDtypeStruct((t, d), dt) for dt in out_dtypes] + [jax.ShapeDtypeStruct((1, d), F32)],
        compiler_params=_params(("arbitrary",)),
    )(*ins)


def _fit(block, dim):
    if dim <= block:
        return dim
    return next(b for b in range(block - block % LANES, 0, -LANES) if dim % b == 0)


def mm(a, b, *, trans_b=False, extras=(), epilogue=None, out_dtypes, name, bm=1024, bn=1024):
    m, k = a.shape
    n = b.shape[0] if trans_b else b.shape[1]
    if k > 2048:
        bm = bm // 2
    bm, bn = _fit(bm, m), _fit(bn, n)
    n_ex = len(extras)

    def body(*refs):
        a_ref, b_ref = refs[0], refs[1]
        ex = refs[2:2 + n_ex]
        outs = refs[2 + n_ex:]
        acc = lax.dot_general(a_ref[...], b_ref[...], NT if trans_b else NN, preferred_element_type=F32)
        res = epilogue(acc, *[e[...] for e in ex]) if epilogue is not None else (acc,)
        for o, r in zip(outs, res):
            o[...] = r.astype(o.dtype)

    tile = pl.BlockSpec((bm, bn), lambda i, j: (i, j))
    b_spec = pl.BlockSpec((bn, k), lambda i, j: (j, 0)) if trans_b else pl.BlockSpec((k, bn), lambda i, j: (0, j))
    return pl.pallas_call(
        body, name=name, grid=(m // bm, n // bn),
        in_specs=[pl.BlockSpec((bm, k), lambda i, j: (i, 0)), b_spec] + [tile] * n_ex,
        out_specs=[tile] * len(out_dtypes),
        out_shape=[jax.ShapeDtypeStruct((m, n), dt) for dt in out_dtypes],
        compiler_params=_params(("parallel", "parallel")),
    )(a, b, *extras)


def mm_tn(a, b, *, name, bk=1024, bn=1024, bt=1024):
    t, k = a.shape
    n = b.shape[1]
    bk, bn, bt = _fit(bk, k), _fit(bn, n), _fit(bt, t)

    def body(a_ref, b_ref, o_ref):
        @pl.when(pl.program_id(2) == 0)
        def _():
            o_ref[...] = jnp.zeros_like(o_ref)

        o_ref[...] += lax.dot_general(a_ref[...], b_ref[...], TN, preferred_element_type=F32)

    return pl.pallas_call(
        body, name=name, grid=(k // bk, n // bn, t // bt),
        in_specs=[pl.BlockSpec((bt, bk), lambda i, j, s: (s, i)), pl.BlockSpec((bt, bn), lambda i, j, s: (s, j))],
        out_specs=pl.BlockSpec((bk, bn), lambda i, j, s: (i, j)),
        out_shape=jax.ShapeDtypeStruct((k, n), F32),
        compiler_params=_params(("parallel", "parallel", "arbitrary")),
    )(a, b)


def _split3(x):
    hi = x.astype(BF16)
    r1 = x - hi.astype(F32)
    mid = r1.astype(BF16)
    lo = (r1 - mid.astype(F32)).astype(BF16)
    return hi, mid, lo


def cumsum_rows(x, *, reverse, name, bc=512):
    t, d = x.shape
    bc = min(bc, t)
    nb = t // bc

    def body(x_ref, o_ref, carry):
        @pl.when(pl.program_id(0) == 0)
        def _():
            carry[...] = jnp.zeros_like(carry)

        r = lax.broadcasted_iota(jnp.int32, (bc, bc), 0)
        c = lax.broadcasted_iota(jnp.int32, (bc, bc), 1)
        tri = jnp.where((r <= c) if reverse else (r >= c), 1.0, 0.0).astype(BF16)
        hi, mid, lo = _split3(x_ref[...])
        s = (lax.dot_general(tri, hi, NN, preferred_element_type=F32)
             + lax.dot_general(tri, mid, NN, preferred_element_type=F32)
             + lax.dot_general(tri, lo, NN, preferred_element_type=F32)) + carry[0:1, :]
        o_ref[...] = s
        carry[0:1, :] = s[0:1, :] if reverse else s[bc - 1:bc, :]

    imap = (lambda i: (nb - 1 - i, 0)) if reverse else (lambda i: (i, 0))
    return pl.pallas_call(
        body, name=name, grid=(nb,),
        in_specs=[pl.BlockSpec((bc, d), imap)], out_specs=pl.BlockSpec((bc, d), imap),
        out_shape=jax.ShapeDtypeStruct((t, d), F32),
        scratch_shapes=[pltpu.VMEM((8, d), F32)],
        compiler_params=_params(("arbitrary",)),
    )(x)


def _rope(x, c, a, b):
    return x * c + pltpu.roll(x, LANES - ROPE_DIM // 2, 1) * a + pltpu.roll(x, ROPE_DIM // 2, 1) * b


def _rope_bwd(d, c, a, b):
    return d * c + pltpu.roll(d * a, ROPE_DIM // 2, 1) + pltpu.roll(d * b, LANES - ROPE_DIM // 2, 1)


S_CQ, S_CKV, S_KR, S_F, S_END = 0, Q_RANK, Q_RANK + KV_RANK, Q_RANK + KV_RANK + LANES, 1024
HW = N_HEADS * LANES


def mla_prep(small, g_q, g_kv, w_uq, w_ukv, tab_c, tab_a, tab_b, b_f, *, name, bt=512):
    t = small.shape[0]
    bt = min(bt, t)

    def body(s_ref, gq_ref, gkv_ref, wq_ref, wkv_ref, c_ref, a_ref, b_ref, bf_ref,
             mq_ref, mk_ref, mv_ref, lf_ref, cqn_ref, ckvn_ref):
        cq = s_ref[:, S_CQ:S_CKV]
        rq = lax.rsqrt(jnp.mean(cq * cq, axis=-1, keepdims=True) + EPS)
        cqn = (cq * rq * gq_ref[...]).astype(BF16)
        ckv = s_ref[:, S_CKV:S_KR]
        rkv = lax.rsqrt(jnp.mean(ckv * ckv, axis=-1, keepdims=True) + EPS)
        ckvn = (ckv * rkv * gkv_ref[...]).astype(BF16)
        cqn_ref[...] = cqn
        ckvn_ref[...] = ckvn
        tc, ta, tb = c_ref[...], a_ref[...], b_ref[...]
        q = jnp.dot(cqn, wq_ref[...], preferred_element_type=F32)
        kv = jnp.dot(ckvn, wkv_ref[...], preferred_element_type=F32)
        kr = _rope(s_ref[:, S_KR:S_F], tc, ta, tb)
        for h in range(N_HEADS):
            sl = slice(h * LANES, (h + 1) * LANES)
            mq_ref[:, sl] = _rope(q[:, sl], tc, ta, tb).astype(BF16)
            mk_ref[:, sl] = (kv[:, sl] + kr).astype(BF16)
        mv_ref[...] = kv[:, HW:].astype(BF16)
        z = s_ref[:, S_F:S_END - LANES] + bf_ref[...]
        lf_ref[...] = jnp.minimum(z, 0.0) - jnp.log(1.0 + jnp.exp(-jnp.abs(z)))

    def row(w):
        return pl.BlockSpec((bt, w), lambda i: (i, 0))

    def full(arr):
        return pl.BlockSpec(arr.shape, lambda i: (0, 0))

    return pl.pallas_call(
        body, name=name, grid=(t // bt,),
        in_specs=[row(S_END), full(g_q), full(g_kv), full(w_uq), full(w_ukv), row(LANES), row(LANES), row(LANES), full(b_f)],
        out_specs=[row(HW), row(HW), row(HW), row(LANES), row(Q_RANK), row(KV_RANK)],
        out_shape=[jax.ShapeDtypeStruct((t, HW), BF16)] * 3 + [jax.ShapeDtypeStruct((t, LANES), F32),
                   jax.ShapeDtypeStruct((t, Q_RANK), BF16), jax.ShapeDtypeStruct((t, KV_RANK), BF16)],
        compiler_params=_params(("parallel",)),
    )(small, g_q, g_kv, w_uq, w_ukv, tab_c, tab_a, tab_b, b_f)


def mla_prep_bwd(dmq, dmk, dmv, dlf, small, g_q, g_kv, w_uq, w_ukv, tab_c, tab_a, tab_b, b_f, *, name, bt=512):
    t = small.shape[0]
    bt = min(bt, t)

    def body(dmq_ref, dmk_ref, dmv_ref, dlf_ref, s_ref, gq_ref, gkv_ref, wq_ref, wkv_ref, c_ref, a_ref, b_ref, bf_ref,
             ds_ref, dq_ref, dkv_ref, dgq_ref, dgkv_ref, db_ref):
        tc, ta, tb = c_ref[...], a_ref[...], b_ref[...]
        lane = lax.broadcasted_iota(jnp.int32, (1, LANES), 1)
        dkr = jnp.zeros((bt, LANES), F32)
        for h in range(N_HEADS):
            sl = slice(h * LANES, (h + 1) * LANES)
            dq_ref[:, sl] = _rope_bwd(dmq_ref[:, sl], tc, ta, tb).astype(BF16)
            dkr = dkr + dmk_ref[:, sl]
        dkv_ref[:, :HW] = dmk_ref[...].astype(BF16)
        dkv_ref[:, HW:] = dmv_ref[...].astype(BF16)
        in_rope = (lane >= HEAD_DIM) & (lane < HEAD_DIM + ROPE_DIM)
        ds_ref[:, S_KR:S_F] = jnp.where(in_rope, _rope_bwd(dkr, tc, ta, tb), 0.0).astype(BF16)

        def norm_bwd(raw, g_ref, dn, dg_ref):
            r = lax.rsqrt(jnp.mean(raw * raw, axis=-1, keepdims=True) + EPS)
            u = dn * g_ref[...]
            dot = jnp.mean(u * raw, axis=-1, keepdims=True)
            dg_ref[...] += jnp.sum(dn * (raw * r), axis=0, keepdims=True)
            return r * u - raw * (r * r * r * dot)

        @pl.when(pl.program_id(0) == 0)
        def _():
            dgq_ref[...] = jnp.zeros_like(dgq_ref)
            dgkv_ref[...] = jnp.zeros_like(dgkv_ref)
            db_ref[...] = jnp.zeros_like(db_ref)

        dcqn = lax.dot_general(dq_ref[...], wq_ref[...], NT, preferred_element_type=F32)
        ds_ref[:, S_CQ:S_CKV] = norm_bwd(s_ref[:, S_CQ:S_CKV], gq_ref, dcqn, dgq_ref).astype(BF16)
        dckvn = lax.dot_general(dkv_ref[...], wkv_ref[...], NT, preferred_element_type=F32)
        ds_ref[:, S_CKV:S_KR] = norm_bwd(s_ref[:, S_CKV:S_KR], gkv_ref, dckvn, dgkv_ref).astype(BF16)
        z = s_ref[:, S_F:S_END - LANES] + bf_ref[...]
        dz = jnp.where(lane < N_HEADS, dlf_ref[...] / (1.0 + jnp.exp(z)), 0.0)
        db_ref[...] += jnp.sum(dz, axis=0, keepdims=True)
        ds_ref[:, S_F:S_END - LANES] = dz.astype(BF16)
        ds_ref[:, S_END - LANES:] = jnp.zeros((bt, LANES), BF16)

    def row(w):
        return pl.BlockSpec((bt, w), lambda i: (i, 0))

    def full(arr):
        return pl.BlockSpec(arr.shape, lambda i: (0, 0))

    def vec(w):
        return pl.BlockSpec((1, w), lambda i: (0, 0))

    return pl.pallas_call(
        body, name=name, grid=(t // bt,),
        in_specs=[row(HW), row(HW), row(HW), row(LANES), row(S_END), full(g_q), full(g_kv), full(w_uq), full(w_ukv),
                  row(LANES), row(LANES), row(LANES), full(b_f)],
        out_specs=[row(S_END), row(HW), row(2 * HW), vec(Q_RANK), vec(KV_RANK), vec(LANES)],
        out_shape=[jax.ShapeDtypeStruct((t, S_END), BF16), jax.ShapeDtypeStruct((t, HW), BF16),
                   jax.ShapeDtypeStruct((t, 2 * HW), BF16), jax.ShapeDtypeStruct((1, Q_RANK), F32),
                   jax.ShapeDtypeStruct((1, KV_RANK), F32), jax.ShapeDtypeStruct((1, LANES), F32)],
        compiler_params=_params(("arbitrary",)),
    )(dmq, dmk, dmv, dlf, small, g_q, g_kv, w_uq, w_ukv, tab_c, tab_a, tab_b, b_f)


def _lane():
    return lax.broadcasted_iota(jnp.int32, (1, LANES), 1)


def _halves(x):
    zero = jnp.zeros_like(x)
    return [jnp.where(_lane() < HEAD_DIM, x, zero), jnp.where(_lane() >= HEAD_DIM, x, zero)]


def _groups(x):
    return [x[:, :LANES], x[:, LANES:]]


def _lanes01(a, b, rest):
    return jnp.where(_lane() == 0, a, jnp.where(_lane() == 1, b, rest))


def _pick_row(tile, h):
    row = lax.broadcasted_iota(jnp.int32, (tile.shape[0], 1), 0)
    return jnp.sum(jnp.where(row == h, tile, 0.0), axis=0, keepdims=True)


def _below_diagonal(s):
    r = lax.broadcasted_iota(jnp.int32, s.shape, 0)
    c = lax.broadcasted_iota(jnp.int32, s.shape, 1)
    return jnp.where(c <= r, s, -jnp.inf)


def flash_fwd(q_arr, k_arr, v_arr, f_blocks, *, qoff, koff, voff, pair, scale, name, blk=512):
    t = q_arr.shape[0]
    blk = min(blk, t)
    nb = t // blk
    w = LANES if pair else 2 * LANES
    has_bias = f_blocks is not None
    split = _halves if pair else _groups

    def body(*refs):
        if has_bias:
            q_ref, k_ref, v_ref, f_ref, o_ref, st_ref = refs
        else:
            q_ref, k_ref, v_ref, o_ref, st_ref = refs
        g, i = pl.program_id(0), pl.program_id(1)
        qs = split((q_ref[...].astype(F32) * (scale * LOG2E)).astype(BF16))

        def step(j, carry, diagonal):
            rows = pl.ds(pl.multiple_of(j * blk, blk), blk)
            kk, vv = k_ref[rows, :], v_ref[rows, :]
            ks, vs = ([kk, kk], [vv, vv]) if pair else (_groups(kk), _groups(vv))
            out = []
            for n in range(2):
                m, l, acc = carry[n]
                s = lax.dot_general(qs[n], ks[n], NT, preferred_element_type=F32)
                if has_bias:
                    s = s - LOG2E * _pick_row(f_ref[j], 2 * g + n)
                if diagonal:
                    s = _below_diagonal(s)
                m_new = jnp.maximum(m, jnp.max(s, axis=1, keepdims=True))
                alpha = jnp.exp2(m - m_new)
                p = jnp.exp2(s - m_new)
                out.append((m_new, alpha * l + jnp.sum(p, axis=1, keepdims=True),
                            alpha * acc + jnp.dot(p.astype(BF16), vs[n], preferred_element_type=F32)))
            return tuple(out)

        init = tuple((jnp.full((blk, 1), -jnp.inf, F32), jnp.zeros((blk, 1), F32), jnp.zeros((blk, LANES), F32))
                     for _ in range(2))
        carry = lax.fori_loop(0, i, lambda j, c: step(j, c, False), init)
        (ma, la, acca), (mb, lb, accb) = step(i, carry, True)
        oa, ob = acca / la, accb / lb
        o_ref[...] = jnp.where(_lane() < HEAD_DIM, oa, ob) if pair else oa + pltpu.roll(ob, HEAD_DIM, 1)
        st_ref[0] = _lanes01(ma + jnp.log2(la), mb + jnp.log2(lb), 0.0)

    in_specs = [pl.BlockSpec((blk, w), lambda g, i: (i, qoff + g)), pl.BlockSpec((t, w), lambda g, i: (0, koff + g)),
                pl.BlockSpec((t, w), lambda g, i: (0, voff + g))]
    ins = [q_arr, k_arr, v_arr]
    if has_bias:
        in_specs.append(pl.BlockSpec((nb, N_HEADS, blk), lambda g, i: (0, 0, 0)))
        ins.append(f_blocks)
    return pl.pallas_call(
        body, name=name, grid=(PAIRS, nb), in_specs=in_specs,
        out_specs=[pl.BlockSpec((blk, LANES), lambda g, i: (i, g)), pl.BlockSpec((1, blk, LANES), lambda g, i: (g, i, 0))],
        out_shape=[jax.ShapeDtypeStruct((t, PAIRS * LANES), F32), jax.ShapeDtypeStruct((PAIRS, t, LANES), F32)],
        compiler_params=_params(("parallel", "arbitrary")),
    )(*ins)


def attn_stats(o, do, st, *, name, bt=512):
    t = o.shape[0]
    bt = min(bt, t)

    def body(o_ref, do_ref, st_ref, out_ref):
        prod = o_ref[...] * do_ref[...].astype(F32)
        for g in range(PAIRS):
            grp = prod[:, g * LANES:(g + 1) * LANES]
            da = jnp.sum(jnp.where(_lane() < HEAD_DIM, grp, 0.0), axis=1, keepdims=True)
            db = jnp.sum(jnp.where(_lane() >= HEAD_DIM, grp, 0.0), axis=1, keepdims=True)
            out_ref[g] = jnp.where(_lane() == 2, da, jnp.where(_lane() == 3, db, st_ref[g]))

    row = pl.BlockSpec((bt, PAIRS * LANES), lambda i: (i, 0))
    stat = pl.BlockSpec((PAIRS, bt, LANES), lambda i: (0, i, 0))
    return pl.pallas_call(
        body, name=name, grid=(t // bt,), in_specs=[row, row, stat], out_specs=stat,
        out_shape=jax.ShapeDtypeStruct(st.shape, F32),
        compiler_params=_params(("parallel",)),
    )(o, do, st)


def flash_bwd(q_arr, k_arr, v_arr, do_arr, st, f_blocks, *, qoff, koff, voff, pair, scale, name, blk=512):
    t = q_arr.shape[0]
    blk = min(blk, t)
    nb = t // blk
    w = LANES if pair else 2 * LANES
    has_bias = f_blocks is not None
    split = _halves if pair else _groups

    def body(*refs):
        if has_bias:
            q_ref, k_ref, v_ref, do_ref, st_ref, f_ref, dq_ref, dk_ref, dv_ref, dfk_ref, dfq_ref = refs
        else:
            q_ref, k_ref, v_ref, do_ref, st_ref, dq_ref, dk_ref, dv_ref = refs
        g, j = pl.program_id(0), pl.program_id(1)
        kk, vv = k_ref[...], v_ref[...]
        ks, vs = ([kk, kk], [vv, vv]) if pair else (_groups(kk), _groups(vv))
        if has_bias:
            fk = [LOG2E * _pick_row(f_ref[0], 2 * g + n) for n in range(2)]

        def step(i, carry, diagonal):
            rows = pl.ds(pl.multiple_of(i * blk, blk), blk)
            qs = split((q_ref[rows, :].astype(F32) * (scale * LOG2E)).astype(BF16))
            da, db = _halves(do_ref[rows, :].astype(F32))
            dos = [da.astype(BF16), (db if pair else pltpu.roll(db, HEAD_DIM, 1)).astype(BF16)]
            stats = st_ref[0, rows, :]
            new, dqs, row_sums = [], [], []
            for n in range(2):
                dk, dv, dfk = carry[n]
                s = lax.dot_general(qs[n], ks[n], NT, preferred_element_type=F32)
                if has_bias:
                    s = s - fk[n]
                if diagonal:
                    s = _below_diagonal(s)
                p = jnp.exp2(s - stats[:, n:n + 1])
                dp = lax.dot_general(dos[n], vs[n], NT, preferred_element_type=F32)
                ds = p * (dp - stats[:, 2 + n:3 + n])
                dsb = ds.astype(BF16)
                dv = dv + lax.dot_general(p.astype(BF16), dos[n], TN, preferred_element_type=F32)
                dk = dk + lax.dot_general(dsb, qs[n], TN, preferred_element_type=F32)
                dqs.append(jnp.dot(dsb, ks[n], preferred_element_type=F32))
                if has_bias:
                    dfk = dfk - jnp.sum(ds, axis=0, keepdims=True)
                    row_sums.append(jnp.sum(ds, axis=1, keepdims=True))
                new.append((dk, dv, dfk))
            dq = (jnp.where(_lane() < HEAD_DIM, dqs[0], dqs[1]) if pair else jnp.concatenate(dqs, axis=1)) * scale
            rs = _lanes01(row_sums[0], row_sums[1], 0.0) if has_bias else None

            @pl.when(j == 0)
            def _():
                dq_ref[rows, :] = dq
                if has_bias:
                    dfq_ref[0, rows, :] = rs

            @pl.when(j > 0)
            def _():
                dq_ref[rows, :] += dq
                if has_bias:
                    dfq_ref[0, rows, :] += rs

            return tuple(new)

        init = tuple((jnp.zeros((blk, LANES), F32), jnp.zeros((blk, LANES), F32), jnp.zeros((1, blk), F32)) for _ in range(2))
        carry = step(j, init, True)
        (dka, dva, dfa), (dkb, dvb, dfb) = lax.fori_loop(j + 1, nb, lambda i, c: step(i, c, False), carry)
        if pair:
            dk_ref[...] = (dka + dkb) * LN2
            dv_ref[...] = dva + dvb
        else:
            dk_ref[...] = jnp.concatenate([dka, dkb], axis=1) * LN2
            dv_ref[...] = jnp.concatenate([dva, dvb], axis=1)
        if has_bias:
            row = lax.broadcasted_iota(jnp.int32, (N_HEADS, 1), 0)
            dfk_ref[0] = jnp.where(row == 0, dfa, jnp.where(row == 1, dfb, 0.0))

    in_specs = [pl.BlockSpec((t, w), lambda g, j: (0, qoff + g)), pl.BlockSpec((blk, w), lambda g, j: (j, koff + g)),
                pl.BlockSpec((blk, w), lambda g, j: (j, voff + g)), pl.BlockSpec((t, LANES), lambda g, j: (0, g)),
                pl.BlockSpec((1, t, LANES), lambda g, j: (g, 0, 0))]
    ins = [q_arr, k_arr, v_arr, do_arr, st]
    out_specs = [pl.BlockSpec((t, w), lambda g, j: (0, g)), pl.BlockSpec((blk, w), lambda g, j: (j, g)),
                 pl.BlockSpec((blk, w), lambda g, j: (j, g))]
    out_shape = [jax.ShapeDtypeStruct((t, PAIRS * w), F32)] * 3
    if has_bias:
        in_specs.append(pl.BlockSpec((1, N_HEADS, blk), lambda g, j: (j, 0, 0)))
        ins.append(f_blocks)
        out_specs += [pl.BlockSpec((1, N_HEADS, blk), lambda g, j: (g, 0, j)), pl.BlockSpec((1, t, LANES), lambda g, j: (g, 0, 0))]
        out_shape += [jax.ShapeDtypeStruct((PAIRS, N_HEADS, t), F32), jax.ShapeDtypeStruct((PAIRS, t, LANES), F32)]
    return pl.pallas_call(
        body, name=name, grid=(PAIRS, nb), in_specs=in_specs, out_specs=out_specs, out_shape=out_shape,
        compiler_params=_params(("parallel", "arbitrary")),
    )(*ins)


def loss_head(x2, g, tgt, *, name, bt=512):
    t, d = x2.shape
    bt = min(bt, t)

    def body(x_ref, g_ref, t_ref, loss_ref, dx_ref, dxb_ref, dg_ref):
        @pl.when(pl.program_id(0) == 0)
        def _():
            loss_ref[...] = jnp.zeros_like(loss_ref)
            dg_ref[...] = jnp.zeros_like(dg_ref)

        xv = x_ref[...]
        gv = g_ref[...]
        r = lax.rsqrt(jnp.mean(xv * xv, axis=-1, keepdims=True) + EPS)
        xh = xv * r
        e = xh * gv - t_ref[...]
        loss_ref[...] += 0.5 * jnp.sum(jnp.mean(e * e, axis=-1, keepdims=True), axis=0, keepdims=True)
        dy = e * (1.0 / d)
        dg_ref[...] += jnp.sum(dy * xh, axis=0, keepdims=True)
        u = dy * gv
        dx = r * u - xv * (r * r * r * jnp.mean(u * xv, axis=-1, keepdims=True))
        dx_ref[...] = dx
        dxb_ref[...] = dx.astype(BF16)

    row = pl.BlockSpec((bt, d), lambda i: (i, 0))
    vec = pl.BlockSpec((1, d), lambda i: (0, 0))
    return pl.pallas_call(
        body, name=name, grid=(t // bt,), in_specs=[row, vec, row],
        out_specs=[pl.BlockSpec((1, 1), lambda i: (0, 0)), row, row, vec],
        out_shape=[jax.ShapeDtypeStruct((1, 1), F32), jax.ShapeDtypeStruct((t, d), F32),
                   jax.ShapeDtypeStruct((t, d), BF16), jax.ShapeDtypeStruct((1, d), F32)],
        compiler_params=_params(("arbitrary",)),
    )(x2, g, tgt)


def adamw(w, g, m, v, *, name):
    rws, cols = w.shape
    br = next(b for b in (256, 128, 64, 32, 16, 8) if rws % b == 0)

    def body(w_ref, g_ref, m_ref, v_ref, d_ref, nm_ref, nv_ref):
        gv = g_ref[...]
        nm = ADAM_B1 * m_ref[...] + (1.0 - ADAM_B1) * gv
        nv = ADAM_B2 * v_ref[...] + (1.0 - ADAM_B2) * (gv * gv)
        m_hat = nm / (1.0 - ADAM_B1 ** ADAM_STEP)
        v_hat = nv / (1.0 - ADAM_B2 ** ADAM_STEP)
        d_ref[...] = -ADAM_LR * (m_hat / (jnp.sqrt(v_hat) + ADAM_EPS) + ADAM_WD * w_ref[...])
        nm_ref[...] = nm
        nv_ref[...] = nv

    blk = pl.BlockSpec((br, cols), lambda i: (i, 0))
    return pl.pallas_call(
        body, name=name, grid=(rws // br,), in_specs=[blk] * 4, out_specs=[blk] * 3,
        out_shape=[jax.ShapeDtypeStruct((rws, cols), F32)] * 3,
        compiler_params=_params(("parallel",)),
    )(w, g, m, v)


def add_pair(a, b, *, name):
    n, r, c = a.shape
    br = r // 2

    def body(a_ref, b_ref, o_ref):
        o_ref[...] = (a_ref[...].astype(F32) + b_ref[...].astype(F32)).astype(o_ref.dtype)

    blk = pl.BlockSpec((1, br, c), lambda q, i: (q, i, 0))
    return pl.pallas_call(
        body, name=name, grid=(n, r // br), in_specs=[blk, blk], out_specs=blk,
        out_shape=jax.ShapeDtypeStruct((n, r, c), BF16),
        compiler_params=_params(("parallel", "parallel")),
    )(a, b)


def sum_chips(parts, *, name):
    n, r, c = parts.shape
    br = r // 2

    def body(p_ref, o_ref):
        acc = p_ref[0].astype(F32)
        for q in range(1, n):
            acc = acc + p_ref[q].astype(F32)
        o_ref[...] = acc

    return pl.pallas_call(
        body, name=name, grid=(r // br,),
        in_specs=[pl.BlockSpec((n, br, c), lambda i: (0, i, 0))], out_specs=pl.BlockSpec((br, c), lambda i: (i, 0)),
        out_shape=jax.ShapeDtypeStruct((r, c), F32),
        compiler_params=_params(("parallel",)),
    )(parts)


ANY = pl.BlockSpec(memory_space=pl.ANY)


def _place():
    x, y, c = lax.axis_index("x"), lax.axis_index("y"), lax.axis_index("c")
    chips = [(1 - x, y), (x, 1 - y), (1 - x, 1 - y)]
    return x, y, c, chips


def gather_shards(wpack):
    r, cdim = wpack.shape
    hr = r // 2

    def body(w_ref, out_ref, send_sems, recv_sems):
        x, y, c, chips = _place()
        q = 2 * x + y
        sibling = (x, y, 1 - c)

        def half(qq, hf):
            return out_ref.at[qq, pl.ds(hf * hr, hr), :]

        def copy(k, src, dst, to):
            return pltpu.make_async_remote_copy(src_ref=src, dst_ref=dst, send_sem=send_sems.at[k],
                                                recv_sem=recv_sems.at[k], device_id=to, device_id_type=MESH)

        first = [copy(j, w_ref.at[pl.ds(c * hr, hr), :], half(q, c), (cx, cy, c)) for j, (cx, cy) in enumerate(chips)]
        for cp in first:
            cp.start()
        passed = []
        for j, (cx, cy) in enumerate(chips):
            qj = 2 * cx + cy
            copy(j, half(qj, c), half(qj, c), (cx, cy, c)).wait_recv()
            fw = copy(3 + j, half(qj, c), half(qj, c), sibling)
            fw.start()
            passed.append(fw)
        for j, (cx, cy) in enumerate(chips):
            qj = 2 * cx + cy
            copy(3 + j, half(qj, 1 - c), half(qj, 1 - c), sibling).wait_recv()
        for cp in first + passed:
            cp.wait_send()

    return pl.pallas_call(
        body, name="gather_shards", in_specs=[ANY], out_specs=ANY,
        out_shape=jax.ShapeDtypeStruct((N_CHIPS, r, cdim), wpack.dtype),
        scratch_shapes=[pltpu.SemaphoreType.DMA((6,)), pltpu.SemaphoreType.DMA((6,))],
    )(wpack)


def swap_halves(g2):
    def body(g_ref, out_ref, send_sem, recv_sem):
        x, y, c, _ = _place()
        cp = pltpu.make_async_remote_copy(src_ref=g_ref.at[1 - c], dst_ref=out_ref, send_sem=send_sem, recv_sem=recv_sem,
                                          device_id=(x, y, 1 - c), device_id_type=MESH)
        cp.start()
        cp.wait()

    return pl.pallas_call(
        body, name="swap_halves", in_specs=[ANY], out_specs=ANY,
        out_shape=jax.ShapeDtypeStruct(g2.shape[1:], g2.dtype),
        scratch_shapes=[pltpu.SemaphoreType.DMA, pltpu.SemaphoreType.DMA],
    )(g2)


def scatter_to_owners(p):
    def body(p_ref, out_ref, send_sems, recv_sems):
        x, y, c, chips = _place()
        q = 2 * x + y
        sends = []
        for j, (cx, cy) in enumerate(chips):
            cp = pltpu.make_async_remote_copy(src_ref=p_ref.at[2 * cx + cy], dst_ref=out_ref.at[q], send_sem=send_sems.at[j],
                                              recv_sem=recv_sems.at[j], device_id=(cx, cy, c), device_id_type=MESH)
            cp.start()
            sends.append(cp)
        for j, (cx, cy) in enumerate(chips):
            qj = 2 * cx + cy
            pltpu.make_async_remote_copy(src_ref=p_ref.at[qj], dst_ref=out_ref.at[qj], send_sem=send_sems.at[j],
                                         recv_sem=recv_sems.at[j], device_id=(cx, cy, c), device_id_type=MESH).wait_recv()
        for cp in sends:
            cp.wait_send()

    return pl.pallas_call(
        body, name="scatter_to_owners", in_specs=[ANY], out_specs=ANY,
        out_shape=jax.ShapeDtypeStruct(p.shape, p.dtype),
        scratch_shapes=[pltpu.SemaphoreType.DMA((3,)), pltpu.SemaphoreType.DMA((3,))],
    )(p)


def swap_with_sibling(rh):
    def body(r_ref, out_ref, send_sem, recv_sem):
        x, y, c, _ = _place()
        cp = pltpu.make_async_remote_copy(src_ref=r_ref, dst_ref=out_ref, send_sem=send_sem, recv_sem=recv_sem,
                                          device_id=(x, y, 1 - c), device_id_type=MESH)
        cp.start()
        cp.wait()

    return pl.pallas_call(
        body, name="swap_with_sibling", in_specs=[ANY], out_specs=ANY,
        out_shape=jax.ShapeDtypeStruct(rh.shape, rh.dtype),
        scratch_shapes=[pltpu.SemaphoreType.DMA, pltpu.SemaphoreType.DMA],
    )(rh)


def allreduce_small(s):
    n_dev = 8

    def body(s_ref, out_ref, buf, send_sems, recv_sems):
        x, y, c, _ = _place()
        me = 4 * x + 2 * y + c
        buf[me] = s_ref[...]
        sends = []
        for k in range(1, n_dev):
            px = 1 - x if k & 4 else x
            py = 1 - y if k & 2 else y
            pc = 1 - c if k & 1 else c
            cp = pltpu.make_async_remote_copy(src_ref=s_ref, dst_ref=buf.at[me], send_sem=send_sems.at[k - 1],
                                              recv_sem=recv_sems.at[k - 1], device_id=(px, py, pc), device_id_type=MESH)
            cp.start()
            sends.append((cp, 4 * px + 2 * py + pc))
        for k, (cp, peer) in enumerate(sends):
            pltpu.make_async_remote_copy(src_ref=s_ref, dst_ref=buf.at[peer], send_sem=send_sems.at[k],
                                         recv_sem=recv_sems.at[k], device_id=(x, y, c), device_id_type=MESH).wait_recv()
        for cp, _ in sends:
            cp.wait_send()
        acc = buf[0]
        for d in range(1, n_dev):
            acc = acc + buf[d]
        out_ref[...] = acc

    vm = pl.BlockSpec(memory_space=pltpu.VMEM)
    return pl.pallas_call(
        body, name="allreduce_small", in_specs=[vm], out_specs=vm,
        out_shape=jax.ShapeDtypeStruct(s.shape, F32),
        scratch_shapes=[pltpu.VMEM((n_dev,) + s.shape, F32), pltpu.SemaphoreType.DMA((n_dev - 1,)),
                        pltpu.SemaphoreType.DMA((n_dev - 1,))],
    )(s)


PACK_COLS = 1024
COL_SHARDED = (True, True, True, False, True, False)


def _pack_rows(shapes):
    rows = [s[0] * s[1] // PACK_COLS for s in shapes]
    total = sum(rows)
    return rows, -(-total // 16) * 16


def pack_shards(shards):
    rows, padded = _pack_rows([s.shape for s in shards])
    parts = [s.reshape(r, PACK_COLS) for s, r in zip(shards, rows)]
    parts.append(jnp.zeros((padded - sum(rows), PACK_COLS), shards[0].dtype))
    return jnp.concatenate(parts, axis=0)


def unpack_shards(pack, shapes):
    rows, _ = _pack_rows(shapes)
    out, at = [], 0
    for s, r in zip(shapes, rows):
        out.append(pack[at:at + r].reshape(s))
        at += r
    return out


def unpack_full(wall, shapes):
    rows, _ = _pack_rows(shapes)
    out, at = [], 0
    for (k, n), r, by_col in zip(shapes, rows, COL_SHARDED):
        blk = wall[:, at:at + r].reshape(N_CHIPS, k, n)
        out.append(blk.transpose(1, 0, 2).reshape(k, N_CHIPS * n) if by_col else blk.reshape(N_CHIPS * k, n))
        at += r
    return out


def pack_full(fulls, shapes, dtype):
    rows, padded = _pack_rows(shapes)
    parts = []
    for f, (k, n), r, by_col in zip(fulls, shapes, rows, COL_SHARDED):
        blk = f.reshape(k, N_CHIPS, n).transpose(1, 0, 2) if by_col else f.reshape(N_CHIPS, k, n)
        parts.append(blk.reshape(N_CHIPS, r, PACK_COLS).astype(dtype))
    parts.append(jnp.zeros((N_CHIPS, padded - sum(rows), PACK_COLS), dtype))
    return jnp.concatenate(parts, axis=1)


def _pad_heads(w, width):
    lead = w.shape[:-1]
    w = w.reshape(lead + (N_HEADS, width))
    return jnp.pad(w, [(0, 0)] * len(lead) + [(0, 0), (0, LANES - width)]).reshape(lead + (HW,))


def _unpad_heads(w, width):
    lead = w.shape[:-1]
    return w.reshape(lead + (N_HEADS, LANES))[..., :width].reshape(lead + (N_HEADS * width,))


FW = N_HEADS * HEAD_DIM


def split_w_in(w_in):
    d = w_in.shape[0]
    o_f = 3 * FW
    o_cq = o_f + N_HEADS
    o_ckv = o_cq + Q_RANK
    o_kr = o_ckv + KV_RANK

    def z(n):
        return jnp.zeros((d, n), w_in.dtype)

    small = jnp.concatenate([w_in[:, o_cq:o_ckv], w_in[:, o_ckv:o_kr], z(HEAD_DIM), w_in[:, o_kr:], z(LANES - HEAD_DIM - ROPE_DIM),
                             w_in[:, o_f:o_cq], z(LANES - N_HEADS), z(LANES)], axis=1)
    return w_in[:, :o_f], small


def join_w_in(d_qkv, d_small):
    kr = S_KR + HEAD_DIM
    return jnp.concatenate([d_qkv, d_small[:, S_F:S_F + N_HEADS], d_small[:, S_CQ:S_CKV], d_small[:, S_CKV:S_KR],
                            d_small[:, kr:kr + ROPE_DIM]], axis=1)


def rope_tables(pos):
    t = pos.shape[0]
    inv_freq = ROPE_THETA ** (-jnp.arange(0, ROPE_DIM, 2, dtype=F32) / ROPE_DIM)
    ang = pos.astype(F32)[:, None] * inv_freq
    cos, sin = jnp.cos(ang), jnp.sin(ang)
    half = ROPE_DIM // 2

    def z(n):
        return jnp.zeros((t, n), F32)

    tab_c = jnp.concatenate([jnp.ones((t, HEAD_DIM), F32), cos, cos, z(LANES - HEAD_DIM - ROPE_DIM)], axis=1)
    tab_a = jnp.concatenate([z(HEAD_DIM), -sin, z(half), z(LANES - HEAD_DIM - ROPE_DIM)], axis=1)
    tab_b = jnp.concatenate([z(HEAD_DIM), z(half), sin, z(LANES - HEAD_DIM - ROPE_DIM)], axis=1)
    return tab_c, tab_a, tab_b


def _pad_lanes(v, n):
    return jnp.pad(v, ((0, 0), (0, n - v.shape[1])))


ATTN_BLK = 512


def local_step(xs, pos, tgt, gains, weights):
    g_attn, b_forget, g_q, g_kv, g_fo, g_mo, g_mlp, g_fin = gains
    w_in, w_uq, w_ukv, w_o, w_up, w_down = weights
    t = xs.shape[0]
    blk = min(ATTN_BLK, t)
    fox_scale = 1.0 / (HEAD_DIM ** 0.5)
    mla_scale = 1.0 / ((HEAD_DIM + ROPE_DIM) ** 0.5)

    w_qkv, w_small = split_w_in(w_in)
    w_uq_p = _pad_heads(w_uq, HEAD_DIM + ROPE_DIM)
    kv = w_ukv.reshape(KV_RANK, N_HEADS, 2 * HEAD_DIM)
    w_ukv_p = jnp.concatenate([_pad_heads(kv[:, :, :HEAD_DIM].reshape(KV_RANK, FW), HEAD_DIM),
                               _pad_heads(kv[:, :, HEAD_DIM:].reshape(KV_RANK, FW), HEAD_DIM)], axis=1)
    w_of, w_om = w_o[:FW], w_o[FW:]
    b_f = _pad_lanes(b_forget, LANES)
    tab_c, tab_a, tab_b = rope_tables(pos)

    h1 = rmsnorm(xs, g_attn, out_dtype=BF16, name="norm_attn")
    qkv, = mm(h1, w_qkv, out_dtypes=[BF16], name="proj_qkv")
    small, = mm(h1, w_small, out_dtypes=[F32], name="proj_small")
    mq, mk, mv, lf, cqn, ckvn = mla_prep(small, g_q, g_kv, w_uq_p, w_ukv_p, tab_c, tab_a, tab_b, b_f, name="mla_prep")
    f_cum = cumsum_rows(lf, reverse=False, name="gate_cumsum")
    f_blocks = f_cum[:, :N_HEADS].reshape(t // blk, blk, N_HEADS).transpose(0, 2, 1)
    fo, st_f = flash_fwd(qkv, qkv, qkv, f_blocks, qoff=0, koff=PAIRS, voff=2 * PAIRS, pair=True, scale=fox_scale,
                         name="fox_fwd", blk=blk)
    mo, st_m = flash_fwd(mq, mk, mv, None, qoff=0, koff=0, voff=0, pair=False, scale=mla_scale, name="mla_fwd", blk=blk)
    mix_f = rmsnorm(fo, g_fo, out_dtype=BF16, name="norm_fox_out")
    mix_m = rmsnorm(mo, g_mo, out_dtype=BF16, name="norm_mla_out")

    def add_res(acc, res):
        return (acc + res,)

    x1a, = mm(mix_f, w_of, extras=[xs], epilogue=add_res, out_dtypes=[F32], name="out_proj_fox")
    x1, = mm(mix_m, w_om, extras=[x1a], epilogue=add_res, out_dtypes=[F32], name="out_proj_mla")
    h2 = rmsnorm(x1, g_mlp, out_dtype=BF16, name="norm_mlp")

    def relu2(acc):
        r = jnp.maximum(acc, 0.0)
        return acc, r * r

    u, act = mm(h2, w_up, epilogue=relu2, out_dtypes=[BF16, BF16], name="mlp_up")
    x2, = mm(act, w_down, extras=[x1], epilogue=add_res, out_dtypes=[F32], name="mlp_down")
    loss, dx2, dx2b, dg_fin = loss_head(x2, g_fin, tgt, name="loss_head")

    def relu2_grad(acc, uu):
        return (acc * (2.0 * jnp.maximum(uu.astype(F32), 0.0)),)

    du, = mm(dx2b, w_down, trans_b=True, extras=[u], epilogue=relu2_grad, out_dtypes=[BF16], name="mlp_down_bwd")
    dw_down = mm_tn(act, dx2b, name="dw_down")
    dh2, = mm(du, w_up, trans_b=True, out_dtypes=[F32], name="mlp_up_bwd")
    dw_up = mm_tn(h2, du, name="dw_up")
    dx1, dx1b, dg_mlp = rmsnorm_bwd(dh2, x1, g_mlp, dx2, out_dtypes=[F32, BF16], name="norm_mlp_bwd")

    dmix_f, = mm(dx1b, w_of, trans_b=True, out_dtypes=[F32], name="out_proj_fox_bwd")
    dmix_m, = mm(dx1b, w_om, trans_b=True, out_dtypes=[F32], name="out_proj_mla_bwd")
    dw_o = jnp.concatenate([mm_tn(mix_f, dx1b, name="dw_o_fox"), mm_tn(mix_m, dx1b, name="dw_o_mla")], axis=0)
    dfo, dg_fo = rmsnorm_bwd(dmix_f, fo, g_fo, None, out_dtypes=[BF16], name="norm_fox_out_bwd")
    dmo, dg_mo = rmsnorm_bwd(dmix_m, mo, g_mo, None, out_dtypes=[BF16], name="norm_mla_out_bwd")
    st_f = attn_stats(fo, dfo, st_f, name="fox_stats")
    st_m = attn_stats(mo, dmo, st_m, name="mla_stats")
    dfq, dfk, dfv, df_key, df_query = flash_bwd(qkv, qkv, qkv, dfo, st_f, f_blocks, qoff=0, koff=PAIRS, voff=2 * PAIRS,
                                                pair=True, scale=fox_scale, name="fox_bwd", blk=blk)
    dmq, dmk, dmv = flash_bwd(mq, mk, mv, dmo, st_m, None, qoff=0, koff=0, voff=0, pair=False, scale=mla_scale,
                              name="mla_bwd", blk=blk)
    dqkv = jnp.concatenate([dfq, dfk, dfv], axis=1).astype(BF16)
    d_f = df_key[:, :2, :].reshape(N_HEADS, t).T + df_query[:, :, :2].transpose(1, 0, 2).reshape(t, N_HEADS)
    dlf = cumsum_rows(_pad_lanes(d_f, LANES), reverse=True, name="gate_cumsum_bwd")
    dsmall, dq_u, dkv_u, dg_q, dg_kv, db_f = mla_prep_bwd(dmq, dmk, dmv, dlf, small, g_q, g_kv, w_uq_p, w_ukv_p,
                                                          tab_c, tab_a, tab_b, b_f, name="mla_prep_bwd")
    dw_uq_p = mm_tn(cqn, dq_u, name="dw_uq")
    dw_ukv_p = mm_tn(ckvn, dkv_u, name="dw_ukv")

    dh1a, = mm(dqkv, w_qkv, trans_b=True, out_dtypes=[F32], name="proj_qkv_bwd")
    dh1, = mm(dsmall, w_small, trans_b=True, extras=[dh1a], epilogue=add_res, out_dtypes=[F32], name="proj_small_bwd")
    dw_qkv = mm_tn(h1, dqkv, name="dw_qkv")
    dw_small = mm_tn(h1, dsmall, name="dw_small")
    grad_x, dg_attn = rmsnorm_bwd(dh1, xs, g_attn, dx1, out_dtypes=[F32], name="norm_attn_bwd")

    dw_in = join_w_in(dw_qkv, dw_small)
    dw_uq = _unpad_heads(dw_uq_p, HEAD_DIM + ROPE_DIM)
    dk_cols = _unpad_heads(dw_ukv_p[:, :HW], HEAD_DIM).reshape(KV_RANK, N_HEADS, HEAD_DIM)
    dv_cols = _unpad_heads(dw_ukv_p[:, HW:], HEAD_DIM).reshape(KV_RANK, N_HEADS, HEAD_DIM)
    dw_ukv = jnp.concatenate([dk_cols, dv_cols], axis=2).reshape(KV_RANK, N_HEADS * 2 * HEAD_DIM)
    d_gains = (dg_attn, db_f[:, :N_HEADS], dg_q, dg_kv, dg_fo, dg_mo, dg_mlp, dg_fin)
    return loss, grad_x, (dw_in, dw_uq, dw_ukv, dw_o, dw_up, dw_down), d_gains


def kernel(x, positions, attn_norm_g, w_in, b_forget, q_norm_g, w_uq, kv_norm_g, w_ukv, fox_out_g, mla_out_g, w_o, mlp_norm_g, w_up, w_down, final_norm_g, loss_target, m_attn_norm_g, m_w_in, m_b_forget, m_q_norm_g, m_w_uq, m_kv_norm_g, m_w_ukv, m_fox_out_g, m_mla_out_g, m_w_o, m_mlp_norm_g, m_w_up, m_w_down, m_final_norm_g, v_attn_norm_g, v_w_in, v_b_forget, v_q_norm_g, v_w_uq, v_kv_norm_g, v_w_ukv, v_fox_out_g, v_mla_out_g, v_w_o, v_mlp_norm_g, v_w_up, v_w_down, v_final_norm_g):
    c = lax.axis_index("c")
    big = [w_in[0], w_uq[0], w_ukv[0], w_o[0], w_up[0], w_down[0]]
    big_m = [m_w_in[0], m_w_uq[0], m_w_ukv[0], m_w_o[0], m_w_up[0], m_w_down[0]]
    big_v = [v_w_in[0], v_w_uq[0], v_w_ukv[0], v_w_o[0], v_w_up[0], v_w_down[0]]
    shapes = [w.shape for w in big]

    def vec(a):
        return a.reshape(1, -1)

    small = [attn_norm_g, b_forget, q_norm_g, kv_norm_g, fox_out_g, mla_out_g, mlp_norm_g, final_norm_g]
    small_m = [m_attn_norm_g, m_b_forget, m_q_norm_g, m_kv_norm_g, m_fox_out_g, m_mla_out_g, m_mlp_norm_g, m_final_norm_g]
    small_v = [v_attn_norm_g, v_b_forget, v_q_norm_g, v_kv_norm_g, v_fox_out_g, v_mla_out_g, v_mlp_norm_g, v_final_norm_g]
    gains = [vec(a) for a in small]

    chip = 2 * lax.axis_index("x") + lax.axis_index("y")
    wpack = pack_shards([w.astype(BF16) for w in big])
    wall = lax.dynamic_update_index_in_dim(gather_shards(wpack), wpack, chip, 0)
    weights = unpack_full(wall, shapes)

    loss, grad_x, d_big, d_small = local_step(x[0], positions[0], loss_target[0], gains, weights)

    gp = pack_full(d_big, shapes, BF16)
    hr = gp.shape[1] // 2
    g2 = gp.reshape(N_CHIPS, 2, hr, PACK_COLS).transpose(1, 0, 2, 3)
    from_sibling = swap_halves(g2)
    kept = lax.dynamic_index_in_dim(g2, c, 0, keepdims=False)
    pair_sum = add_pair(kept, from_sibling, name="add_pair")
    own = lax.dynamic_index_in_dim(pair_sum, chip, 0, keepdims=False)
    by_chip = lax.dynamic_update_index_in_dim(scatter_to_owners(pair_sum), own, chip, 0)
    my_half = sum_chips(by_chip, name="sum_chips")
    other_half = swap_with_sibling(my_half)
    g_shard = jnp.concatenate([jnp.where(c == 0, my_half, other_half), jnp.where(c == 0, other_half, my_half)], axis=0)
    g_big = unpack_shards(g_shard, shapes)

    def rows8(vs):
        return jnp.concatenate([_pad_lanes(vec(a).astype(F32), PACK_COLS) for a in vs], axis=0)

    g_small8 = allreduce_small(rows8(d_small))

    outs_big = [adamw(w, g, m, v, name="adamw_%d" % n) for n, (w, g, m, v) in enumerate(zip(big, g_big, big_m, big_v))]
    d8, m8, v8 = adamw(rows8(small), g_small8, rows8(small_m), rows8(small_v), name="adamw_small")

    def unrows8(a8):
        return [a8[n, :s.size].reshape(s.shape) for n, s in enumerate(small)]

    loss_all = lax.psum(loss[0, 0], ("x", "y", "c"))
    grads, deltas, new_m, new_v = [None] * 14, [None] * 14, [None] * 14, [None] * 14
    big_at = [1, 4, 6, 9, 11, 12]
    small_at = [0, 2, 3, 5, 7, 8, 10, 13]
    for n, at in enumerate(big_at):
        grads[at] = g_big[n][None]
        deltas[at], new_m[at], new_v[at] = (a[None] for a in outs_big[n])
    for at, g, dd, mm_, vv in zip(small_at, unrows8(g_small8), unrows8(d8), unrows8(m8), unrows8(v8)):
        grads[at], deltas[at], new_m[at], new_v[at] = g, dd, mm_, vv
    return (loss_all, grad_x[None], *grads, *deltas, *new_m, *new_v)
```

```python
import jax
import jax.numpy as jnp
from jax import lax
from jax.experimental import pallas as pl
from jax.experimental.pallas import tpu as pltpu

F32 = jnp.float32
BF16 = jnp.bfloat16
MESH = pl.DeviceIdType.MESH

EPS = 1e-6
ROPE_THETA = 10000.0
N_HEADS = 8
PAIRS = N_HEADS // 2
HEAD_DIM = 64
ROPE_DIM = 32
LANES = 128
Q_RANK = 384
KV_RANK = 256
N_CHIPS = 4
ADAM_LR, ADAM_B1, ADAM_B2, ADAM_EPS, ADAM_WD, ADAM_STEP = 0.001, 0.9, 0.999, 1e-08, 0.01, 10
VMEM_LIMIT = 48 * 1024 * 1024
LOG2E = 1.4426950408889634
LN2 = 0.6931471805599453
NN = (((1,), (0,)), ((), ()))
NT = (((1,), (1,)), ((), ()))
TN = (((0,), (0,)), ((), ()))


def _params(sem=None):
    return pltpu.CompilerParams(dimension_semantics=sem, vmem_limit_bytes=VMEM_LIMIT)


def _fit(block, dim):
    if dim <= block:
        return dim
    return next(b for b in range(block - block % LANES, 0, -LANES) if dim % b == 0)


def _row_block(rows):
    return next(b for b in (256, 128, 64, 32, 16, 8) if rows % b == 0)


def rmsnorm(x, g, *, out_dtype, name, bt=512):
    t, d = x.shape
    bt = min(bt, t)

    def body(x_ref, g_ref, o_ref):
        xv = x_ref[...].astype(F32)
        r = lax.rsqrt(jnp.mean(xv * xv, axis=-1, keepdims=True) + EPS)
        o_ref[...] = (xv * r * g_ref[...]).astype(o_ref.dtype)

    return pl.pallas_call(
        body, name=name, grid=(t // bt,),
        in_specs=[pl.BlockSpec((bt, d), lambda i: (i, 0)), pl.BlockSpec((1, d), lambda i: (0, 0))],
        out_specs=pl.BlockSpec((bt, d), lambda i: (i, 0)),
        out_shape=jax.ShapeDtypeStruct((t, d), out_dtype),
        compiler_params=_params(("parallel",)),
    )(x, g)


def rmsnorm_bwd(dh, x, g, res, *, out_dtypes, name, bt=512):
    t, d = x.shape
    bt = min(bt, t)
    has_res = res is not None

    def body(*refs):
        dh_ref, x_ref, g_ref = refs[:3]
        res_ref = refs[3] if has_res else None
        outs = refs[3 + has_res:]
        dx_refs, dg_ref = outs[:-1], outs[-1]
        xv = x_ref[...].astype(F32)
        dhv = dh_ref[...].astype(F32)
        r = lax.rsqrt(jnp.mean(xv * xv, axis=-1, keepdims=True) + EPS)
        u = dhv * g_ref[...]
        dot = jnp.mean(u * xv, axis=-1, keepdims=True)
        dx = r * u - xv * (r * r * r * dot)
        if has_res:
            dx = dx + res_ref[...]
        for o in dx_refs:
            o[...] = dx.astype(o.dtype)

        @pl.when(pl.program_id(0) == 0)
        def _():
            dg_ref[...] = jnp.zeros_like(dg_ref)

        dg_ref[...] += jnp.sum(dhv * (xv * r), axis=0, keepdims=True)

    row = pl.BlockSpec((bt, d), lambda i: (i, 0))
    vec = pl.BlockSpec((1, d), lambda i: (0, 0))
    ins = [dh, x, g] + ([res] if has_res else [])
    return pl.pallas_call(
        body, name=name, grid=(t // bt,),
        in_specs=[row, row, vec] + ([row] if has_res else []),
        out_specs=[row] * len(out_dtypes) + [vec],
        out_shape=[jax.ShapeDtypeStruct((t, d), dt) for dt in out_dtypes] + [jax.ShapeDtypeStruct((1, d), F32)],
        compiler_params=_params(("arbitrary",)),
    )(*ins)


def mm(a, b, *, trans_b=False, extras=(), epilogue=None, out_dtypes, name, bm=1024, bn=1024):
    m, k = a.shape
    n = b.shape[0] if trans_b else b.shape[1]
    if k > 2048:
        bm = bm // 2
    bm, bn = _fit(bm, m), _fit(bn, n)
    n_ex = len(extras)

    def body(*refs):
        a_ref, b_ref = refs[0], refs[1]
        ex = refs[2:2 + n_ex]
        outs = refs[2 + n_ex:]
        acc = lax.dot_general(a_ref[...], b_ref[...], NT if trans_b else NN, preferred_element_type=F32)
        res = epilogue(acc, *[e[...] for e in ex]) if epilogue is not None else (acc,)
        for o, r in zip(outs, res):
            o[...] = r.astype(o.dtype)

    tile = pl.BlockSpec((bm, bn), lambda i, j: (i, j))
    b_spec = pl.BlockSpec((bn, k), lambda i, j: (j, 0)) if trans_b else pl.BlockSpec((k, bn), lambda i, j: (0, j))
    return pl.pallas_call(
        body, name=name, grid=(m // bm, n // bn),
        in_specs=[pl.BlockSpec((bm, k), lambda i, j: (i, 0)), b_spec] + [tile] * n_ex,
        out_specs=[tile] * len(out_dtypes),
        out_shape=[jax.ShapeDtypeStruct((m, n), dt) for dt in out_dtypes],
        compiler_params=_params(("parallel", "parallel")),
    )(a, b, *extras)


def mm_tn(a, b, *, name, col_shards=1, bk=1024, bn=1024, bt=1024):
    t, k = a.shape
    n = b.shape[1]
    ns = n // col_shards
    bk, bn, bt = _fit(bk, k), _fit(bn, ns), _fit(bt, t)
    per = ns // bn

    def body(a_ref, b_ref, o_ref):
        @pl.when(pl.program_id(2) == 0)
        def _():
            o_ref[...] = jnp.zeros_like(o_ref)

        o_ref[...] += lax.dot_general(a_ref[...], b_ref[...], TN, preferred_element_type=F32)

    if col_shards == 1:
        out_spec = pl.BlockSpec((bk, bn), lambda i, j, s: (i, j))
        out_shape = jax.ShapeDtypeStruct((k, n), F32)
    else:
        out_spec = pl.BlockSpec((None, bk, bn), lambda i, j, s: (j // per, i, j % per))
        out_shape = jax.ShapeDtypeStruct((col_shards, k, ns), F32)
    return pl.pallas_call(
        body, name=name, grid=(k // bk, n // bn, t // bt),
        in_specs=[pl.BlockSpec((bt, bk), lambda i, j, s: (s, i)), pl.BlockSpec((bt, bn), lambda i, j, s: (s, j))],
        out_specs=out_spec, out_shape=out_shape,
        compiler_params=_params(("parallel", "parallel", "arbitrary")),
    )(a, b)


def _split3(x):
    hi = x.astype(BF16)
    r1 = x - hi.astype(F32)
    mid = r1.astype(BF16)
    lo = (r1 - mid.astype(F32)).astype(BF16)
    return hi, mid, lo


def cumsum_rows(x, *, reverse, name, bc=512):
    t, d = x.shape
    bc = min(bc, t)
    nb = t // bc

    def body(x_ref, o_ref, carry):
        @pl.when(pl.program_id(0) == 0)
        def _():
            carry[...] = jnp.zeros_like(carry)

        r = lax.broadcasted_iota(jnp.int32, (bc, bc), 0)
        c = lax.broadcasted_iota(jnp.int32, (bc, bc), 1)
        tri = jnp.where((r <= c) if reverse else (r >= c), 1.0, 0.0).astype(BF16)
        hi, mid, lo = _split3(x_ref[...])
        s = (lax.dot_general(tri, hi, NN, preferred_element_type=F32)
             + lax.dot_general(tri, mid, NN, preferred_element_type=F32)
             + lax.dot_general(tri, lo, NN, preferred_element_type=F32)) + carry[0:1, :]
        o_ref[...] = s
        carry[0:1, :] = s[0:1, :] if reverse else s[bc - 1:bc, :]

    imap = (lambda i: (nb - 1 - i, 0)) if reverse else (lambda i: (i, 0))
    return pl.pallas_call(
        body, name=name, grid=(nb,),
        in_specs=[pl.BlockSpec((bc, d), imap)], out_specs=pl.BlockSpec((bc, d), imap),
        out_shape=jax.ShapeDtypeStruct((t, d), F32),
        scratch_shapes=[pltpu.VMEM((8, d), F32)],
        compiler_params=_params(("arbitrary",)),
    )(x)


def _rope(x, c, a, b):
    return x * c + pltpu.roll(x, LANES - ROPE_DIM // 2, 1) * a + pltpu.roll(x, ROPE_DIM // 2, 1) * b


def _rope_bwd(d, c, a, b):
    return d * c + pltpu.roll(d * a, ROPE_DIM // 2, 1) + pltpu.roll(d * b, LANES - ROPE_DIM // 2, 1)


S_CQ, S_CKV, S_KR, S_F, S_END = 0, Q_RANK, Q_RANK + KV_RANK, Q_RANK + KV_RANK + LANES, 1024
HW = N_HEADS * LANES


def mla_prep(small, g_q, g_kv, w_uq, w_ukv, tab_c, tab_a, tab_b, b_f, *, name, bt=512):
    t = small.shape[0]
    bt = min(bt, t)

    def body(s_ref, gq_ref, gkv_ref, wq_ref, wkv_ref, c_ref, a_ref, b_ref, bf_ref,
             mq_ref, mk_ref, mv_ref, lf_ref, cqn_ref, ckvn_ref):
        cq = s_ref[:, S_CQ:S_CKV]
        rq = lax.rsqrt(jnp.mean(cq * cq, axis=-1, keepdims=True) + EPS)
        cqn = (cq * rq * gq_ref[...]).astype(BF16)
        ckv = s_ref[:, S_CKV:S_KR]
        rkv = lax.rsqrt(jnp.mean(ckv * ckv, axis=-1, keepdims=True) + EPS)
        ckvn = (ckv * rkv * gkv_ref[...]).astype(BF16)
        cqn_ref[...] = cqn
        ckvn_ref[...] = ckvn
        tc, ta, tb = c_ref[...], a_ref[...], b_ref[...]
        q = jnp.dot(cqn, wq_ref[...], preferred_element_type=F32)
        kv = jnp.dot(ckvn, wkv_ref[...], preferred_element_type=F32)
        kr = _rope(s_ref[:, S_KR:S_F], tc, ta, tb)
        for h in range(N_HEADS):
            sl = slice(h * LANES, (h + 1) * LANES)
            mq_ref[:, sl] = _rope(q[:, sl], tc, ta, tb).astype(BF16)
            mk_ref[:, sl] = (kv[:, sl] + kr).astype(BF16)
        mv_ref[...] = kv[:, HW:].astype(BF16)
        z = s_ref[:, S_F:S_END - LANES] + bf_ref[...]
        lf_ref[...] = jnp.minimum(z, 0.0) - jnp.log(1.0 + jnp.exp(-jnp.abs(z)))

    def row(w):
        return pl.BlockSpec((bt, w), lambda i: (i, 0))

    def full(arr):
        return pl.BlockSpec(arr.shape, lambda i: (0, 0))

    return pl.pallas_call(
        body, name=name, grid=(t // bt,),
        in_specs=[row(S_END), full(g_q), full(g_kv), full(w_uq), full(w_ukv), row(LANES), row(LANES), row(LANES), full(b_f)],
        out_specs=[row(HW), row(HW), row(HW), row(LANES), row(Q_RANK), row(KV_RANK)],
        out_shape=[jax.ShapeDtypeStruct((t, HW), BF16)] * 3 + [jax.ShapeDtypeStruct((t, LANES), F32),
                   jax.ShapeDtypeStruct((t, Q_RANK), BF16), jax.ShapeDtypeStruct((t, KV_RANK), BF16)],
        compiler_params=_params(("parallel",)),
    )(small, g_q, g_kv, w_uq, w_ukv, tab_c, tab_a, tab_b, b_f)


def mla_prep_bwd(dmq, dmk, dmv, dlf, small, g_q, g_kv, w_uq, w_ukv, tab_c, tab_a, tab_b, b_f, *, name, bt=512):
    t = small.shape[0]
    bt = min(bt, t)

    def body(dmq_ref, dmk_ref, dmv_ref, dlf_ref, s_ref, gq_ref, gkv_ref, wq_ref, wkv_ref, c_ref, a_ref, b_ref, bf_ref,
             ds_ref, dq_ref, dkv_ref, dgq_ref, dgkv_ref, db_ref):
        tc, ta, tb = c_ref[...], a_ref[...], b_ref[...]
        lane = lax.broadcasted_iota(jnp.int32, (1, LANES), 1)
        dkr = jnp.zeros((bt, LANES), F32)
        for h in range(N_HEADS):
            sl = slice(h * LANES, (h + 1) * LANES)
            dq_ref[:, sl] = _rope_bwd(dmq_ref[:, sl], tc, ta, tb).astype(BF16)
            dkr = dkr + dmk_ref[:, sl]
        dkv_ref[:, :HW] = dmk_ref[...].astype(BF16)
        dkv_ref[:, HW:] = dmv_ref[...].astype(BF16)
        in_rope = (lane >= HEAD_DIM) & (lane < HEAD_DIM + ROPE_DIM)
        ds_ref[:, S_KR:S_F] = jnp.where(in_rope, _rope_bwd(dkr, tc, ta, tb), 0.0).astype(BF16)

        def norm_bwd(raw, g_ref, dn, dg_ref):
            r = lax.rsqrt(jnp.mean(raw * raw, axis=-1, keepdims=True) + EPS)
            u = dn * g_ref[...]
            dot = jnp.mean(u * raw, axis=-1, keepdims=True)
            dg_ref[...] += jnp.sum(dn * (raw * r), axis=0, keepdims=True)
            return r * u - raw * (r * r * r * dot)

        @pl.when(pl.program_id(0) == 0)
        def _():
            dgq_ref[...] = jnp.zeros_like(dgq_ref)
            dgkv_ref[...] = jnp.zeros_like(dgkv_ref)
            db_ref[...] = jnp.zeros_like(db_ref)

        dcqn = lax.dot_general(dq_ref[...], wq_ref[...], NT, preferred_element_type=F32)
        ds_ref[:, S_CQ:S_CKV] = norm_bwd(s_ref[:, S_CQ:S_CKV], gq_ref, dcqn, dgq_ref).astype(BF16)
        dckvn = lax.dot_general(dkv_ref[...], wkv_ref[...], NT, preferred_element_type=F32)
        ds_ref[:, S_CKV:S_KR] = norm_bwd(s_ref[:, S_CKV:S_KR], gkv_ref, dckvn, dgkv_ref).astype(BF16)
        z = s_ref[:, S_F:S_END - LANES] + bf_ref[...]
        dz = jnp.where(lane < N_HEADS, dlf_ref[...] / (1.0 + jnp.exp(z)), 0.0)
        db_ref[...] += jnp.sum(dz, axis=0, keepdims=True)
        ds_ref[:, S_F:S_END - LANES] = dz.astype(BF16)
        ds_ref[:, S_END - LANES:] = jnp.zeros((bt, LANES), BF16)

    def row(w):
        return pl.BlockSpec((bt, w), lambda i: (i, 0))

    def full(arr):
        return pl.BlockSpec(arr.shape, lambda i: (0, 0))

    def vec(w):
        return pl.BlockSpec((1, w), lambda i: (0, 0))

    return pl.pallas_call(
        body, name=name, grid=(t // bt,),
        in_specs=[row(HW), row(HW), row(HW), row(LANES), row(S_END), full(g_q), full(g_kv), full(w_uq), full(w_ukv),
                  row(LANES), row(LANES), row(LANES), full(b_f)],
        out_specs=[row(S_END), row(HW), row(2 * HW), vec(Q_RANK), vec(KV_RANK), vec(LANES)],
        out_shape=[jax.ShapeDtypeStruct((t, S_END), BF16), jax.ShapeDtypeStruct((t, HW), BF16),
                   jax.ShapeDtypeStruct((t, 2 * HW), BF16), jax.ShapeDtypeStruct((1, Q_RANK), F32),
                   jax.ShapeDtypeStruct((1, KV_RANK), F32), jax.ShapeDtypeStruct((1, LANES), F32)],
        compiler_params=_params(("arbitrary",)),
    )(dmq, dmk, dmv, dlf, small, g_q, g_kv, w_uq, w_ukv, tab_c, tab_a, tab_b, b_f)


class Side:
    def __init__(self, ins, out_shapes, n_sems, first, last, mid=None):
        self.ins, self.out_shapes, self.n_sems = list(ins), list(out_shapes), n_sems
        self.first, self.mid, self.last = first, mid, last

    def specs(self):
        return [ANY] * len(self.ins), [ANY] * len(self.out_shapes)

    def sems(self):
        return [pltpu.SemaphoreType.DMA((self.n_sems,)), pltpu.SemaphoreType.DMA((self.n_sems,))]


def _lane():
    return lax.broadcasted_iota(jnp.int32, (1, LANES), 1)


def _halves(x):
    zero = jnp.zeros_like(x)
    return [jnp.where(_lane() < HEAD_DIM, x, zero), jnp.where(_lane() >= HEAD_DIM, x, zero)]


def _groups(x):
    return [x[:, :LANES], x[:, LANES:]]


def _lanes01(a, b, rest):
    return jnp.where(_lane() == 0, a, jnp.where(_lane() == 1, b, rest))


def _pick_row(tile, h):
    row = lax.broadcasted_iota(jnp.int32, (tile.shape[0], 1), 0)
    return jnp.sum(jnp.where(row == h, tile, 0.0), axis=0, keepdims=True)


def _below_diagonal(s):
    r = lax.broadcasted_iota(jnp.int32, s.shape, 0)
    c = lax.broadcasted_iota(jnp.int32, s.shape, 1)
    return jnp.where(c <= r, s, -jnp.inf)


def _split_refs(refs, counts):
    out, at = [], 0
    for n in counts:
        out.append(refs[at:at + n])
        at += n
    return out


def flash_fwd(q_arr, k_arr, v_arr, f_blocks, *, qoff, koff, voff, pair, scale, name, blk=512, side=None):
    t = q_arr.shape[0]
    blk = min(blk, t)
    nb = t // blk
    w = LANES if pair else 2 * LANES
    has_bias = f_blocks is not None
    split = _halves if pair else _groups
    ins = [q_arr, k_arr, v_arr] + ([f_blocks] if has_bias else [])
    s_ins, s_outs = (side.ins, side.out_shapes) if side else ([], [])

    def body(*refs):
        main, si, outs, so, sems = _split_refs(refs, [len(ins), len(s_ins), 2, len(s_outs), 2 if side else 0])
        q_ref, k_ref, v_ref = main[:3]
        f_ref = main[3] if has_bias else None
        o_ref, st_ref = outs
        g, i = pl.program_id(0), pl.program_id(1)
        step_id = g * nb + i
        if side:
            @pl.when(step_id == 0)
            def _():
                side.first(si, so, *sems)

            if side.mid is not None:
                @pl.when(step_id == (PAIRS * nb) // 2)
                def _():
                    side.mid(si, so, *sems)

        qs = split((q_ref[...].astype(F32) * (scale * LOG2E)).astype(BF16))

        def step(j, carry, diagonal):
            rows = pl.ds(pl.multiple_of(j * blk, blk), blk)
            kk, vv = k_ref[rows, :], v_ref[rows, :]
            ks, vs = ([kk, kk], [vv, vv]) if pair else (_groups(kk), _groups(vv))
            out = []
            for n in range(2):
                m, l, acc = carry[n]
                s = lax.dot_general(qs[n], ks[n], NT, preferred_element_type=F32)
                if has_bias:
                    s = s - LOG2E * _pick_row(f_ref[j], 2 * g + n)
                if diagonal:
                    s = _below_diagonal(s)
                m_new = jnp.maximum(m, jnp.max(s, axis=1, keepdims=True))
                alpha = jnp.exp2(m - m_new)
                p = jnp.exp2(s - m_new)
                out.append((m_new, alpha * l + jnp.sum(p, axis=1, keepdims=True),
                            alpha * acc + jnp.dot(p.astype(BF16), vs[n], preferred_element_type=F32)))
            return tuple(out)

        init = tuple((jnp.full((blk, 1), -jnp.inf, F32), jnp.zeros((blk, 1), F32), jnp.zeros((blk, LANES), F32))
                     for _ in range(2))
        carry = lax.fori_loop(0, i, lambda j, c: step(j, c, False), init)
        (ma, la, acca), (mb, lb, accb) = step(i, carry, True)
        oa, ob = acca / la, accb / lb
        o_ref[...] = jnp.where(_lane() < HEAD_DIM, oa, ob) if pair else oa + pltpu.roll(ob, HEAD_DIM, 1)
        st_ref[0] = _lanes01(ma + jnp.log2(la), mb + jnp.log2(lb), 0.0)
        if side:
            @pl.when(step_id == PAIRS * nb - 1)
            def _():
                side.last(si, so, *sems)

    in_specs = [pl.BlockSpec((blk, w), lambda g, i: (i, qoff + g)), pl.BlockSpec((t, w), lambda g, i: (0, koff + g)),
                pl.BlockSpec((t, w), lambda g, i: (0, voff + g))]
    if has_bias:
        in_specs.append(pl.BlockSpec((nb, N_HEADS, blk), lambda g, i: (0, 0, 0)))
    s_in_specs, s_out_specs = side.specs() if side else ([], [])
    return pl.pallas_call(
        body, name=name, grid=(PAIRS, nb), in_specs=in_specs + s_in_specs,
        out_specs=[pl.BlockSpec((blk, LANES), lambda g, i: (i, g)), pl.BlockSpec((1, blk, LANES), lambda g, i: (g, i, 0))]
        + s_out_specs,
        out_shape=[jax.ShapeDtypeStruct((t, PAIRS * LANES), F32), jax.ShapeDtypeStruct((PAIRS, t, LANES), F32)] + list(s_outs),
        scratch_shapes=side.sems() if side else [],
        compiler_params=_params(("arbitrary", "arbitrary")),
    )(*ins, *s_ins)


def attn_stats(o, do, st, *, name, bt=512):
    t = o.shape[0]
    bt = min(bt, t)

    def body(o_ref, do_ref, st_ref, out_ref):
        prod = o_ref[...] * do_ref[...].astype(F32)
        for g in range(PAIRS):
            grp = prod[:, g * LANES:(g + 1) * LANES]
            da = jnp.sum(jnp.where(_lane() < HEAD_DIM, grp, 0.0), axis=1, keepdims=True)
            db = jnp.sum(jnp.where(_lane() >= HEAD_DIM, grp, 0.0), axis=1, keepdims=True)
            out_ref[g] = jnp.where(_lane() == 2, da, jnp.where(_lane() == 3, db, st_ref[g]))

    row = pl.BlockSpec((bt, PAIRS * LANES), lambda i: (i, 0))
    stat = pl.BlockSpec((PAIRS, bt, LANES), lambda i: (0, i, 0))
    return pl.pallas_call(
        body, name=name, grid=(t // bt,), in_specs=[row, row, stat], out_specs=stat,
        out_shape=jax.ShapeDtypeStruct(st.shape, F32),
        compiler_params=_params(("parallel",)),
    )(o, do, st)


def flash_bwd(q_arr, k_arr, v_arr, do_arr, st, f_blocks, *, qoff, koff, voff, pair, scale, name, blk=512, side=None):
    t = q_arr.shape[0]
    blk = min(blk, t)
    nb = t // blk
    w = LANES if pair else 2 * LANES
    has_bias = f_blocks is not None
    split = _halves if pair else _groups
    ins = [q_arr, k_arr, v_arr, do_arr, st] + ([f_blocks] if has_bias else [])
    n_out = 5 if has_bias else 3
    s_ins, s_outs = (side.ins, side.out_shapes) if side else ([], [])

    def body(*refs):
        main, si, outs, so, sems = _split_refs(refs, [len(ins), len(s_ins), n_out, len(s_outs), 2 if side else 0])
        q_ref, k_ref, v_ref, do_ref, st_ref = main[:5]
        dq_ref, dk_ref, dv_ref = outs[:3]
        g, j = pl.program_id(0), pl.program_id(1)
        step_id = g * nb + j
        if side:
            @pl.when(step_id == 0)
            def _():
                side.first(si, so, *sems)

        kk, vv = k_ref[...], v_ref[...]
        ks, vs = ([kk, kk], [vv, vv]) if pair else (_groups(kk), _groups(vv))
        if has_bias:
            f_ref, dfk_ref, dfq_ref = main[5], outs[3], outs[4]
            fk = [LOG2E * _pick_row(f_ref[0], 2 * g + n) for n in range(2)]

        def step(i, carry, diagonal):
            rows = pl.ds(pl.multiple_of(i * blk, blk), blk)
            qs = split((q_ref[rows, :].astype(F32) * (scale * LOG2E)).astype(BF16))
            da, db = _halves(do_ref[rows, :].astype(F32))
            dos = [da.astype(BF16), (db if pair else pltpu.roll(db, HEAD_DIM, 1)).astype(BF16)]
            stats = st_ref[0, rows, :]
            new, dqs, row_sums = [], [], []
            for n in range(2):
                dk, dv, dfk = carry[n]
                s = lax.dot_general(qs[n], ks[n], NT, preferred_element_type=F32)
                if has_bias:
                    s = s - fk[n]
                if diagonal:
                    s = _below_diagonal(s)
                p = jnp.exp2(s - stats[:, n:n + 1])
                dp = lax.dot_general(dos[n], vs[n], NT, preferred_element_type=F32)
                ds = p * (dp - stats[:, 2 + n:3 + n])
                dsb = ds.astype(BF16)
                dv = dv + lax.dot_general(p.astype(BF16), dos[n], TN, preferred_element_type=F32)
                dk = dk + lax.dot_general(dsb, qs[n], TN, preferred_element_type=F32)
                dqs.append(jnp.dot(dsb, ks[n], preferred_element_type=F32))
                if has_bias:
                    dfk = dfk - jnp.sum(ds, axis=0, keepdims=True)
                    row_sums.append(jnp.sum(ds, axis=1, keepdims=True))
                new.append((dk, dv, dfk))
            dq = (jnp.where(_lane() < HEAD_DIM, dqs[0], dqs[1]) if pair else jnp.concatenate(dqs, axis=1)) * scale
            rs = _lanes01(row_sums[0], row_sums[1], 0.0) if has_bias else None

            @pl.when(j == 0)
            def _():
                dq_ref[rows, :] = dq
                if has_bias:
                    dfq_ref[0, rows, :] = rs

            @pl.when(j > 0)
            def _():
                dq_ref[rows, :] += dq
                if has_bias:
                    dfq_ref[0, rows, :] += rs

            return tuple(new)

        init = tuple((jnp.zeros((blk, LANES), F32), jnp.zeros((blk, LANES), F32), jnp.zeros((1, blk), F32)) for _ in range(2))
        carry = step(j, init, True)
        (dka, dva, dfa), (dkb, dvb, dfb) = lax.fori_loop(j + 1, nb, lambda i, c: step(i, c, False), carry)
        if pair:
            dk_ref[...] = (dka + dkb) * LN2
            dv_ref[...] = dva + dvb
        else:
            dk_ref[...] = jnp.concatenate([dka, dkb], axis=1) * LN2
            dv_ref[...] = jnp.concatenate([dva, dvb], axis=1)
        if has_bias:
            row = lax.broadcasted_iota(jnp.int32, (N_HEADS, 1), 0)
            dfk_ref[0] = jnp.where(row == 0, dfa, jnp.where(row == 1, dfb, 0.0))
        if side:
            @pl.when(step_id == PAIRS * nb - 1)
            def _():
                side.last(si, so, *sems)

    in_specs = [pl.BlockSpec((t, w), lambda g, j: (0, qoff + g)), pl.BlockSpec((blk, w), lambda g, j: (j, koff + g)),
                pl.BlockSpec((blk, w), lambda g, j: (j, voff + g)), pl.BlockSpec((t, LANES), lambda g, j: (0, g)),
                pl.BlockSpec((1, t, LANES), lambda g, j: (g, 0, 0))]
    out_specs = [pl.BlockSpec((t, w), lambda g, j: (0, g)), pl.BlockSpec((blk, w), lambda g, j: (j, g)),
                 pl.BlockSpec((blk, w), lambda g, j: (j, g))]
    out_shape = [jax.ShapeDtypeStruct((t, PAIRS * w), F32)] * 3
    if has_bias:
        in_specs.append(pl.BlockSpec((1, N_HEADS, blk), lambda g, j: (j, 0, 0)))
        out_specs += [pl.BlockSpec((1, N_HEADS, blk), lambda g, j: (g, 0, j)), pl.BlockSpec((1, t, LANES), lambda g, j: (g, 0, 0))]
        out_shape += [jax.ShapeDtypeStruct((PAIRS, N_HEADS, t), F32), jax.ShapeDtypeStruct((PAIRS, t, LANES), F32)]
    s_in_specs, s_out_specs = side.specs() if side else ([], [])
    return pl.pallas_call(
        body, name=name, grid=(PAIRS, nb), in_specs=in_specs + s_in_specs, out_specs=out_specs + s_out_specs,
        out_shape=out_shape + list(s_outs), scratch_shapes=side.sems() if side else [],
        compiler_params=_params(("arbitrary", "arbitrary")),
    )(*ins, *s_ins)


def loss_head(x2, g, tgt, *, name, bt=512):
    t, d = x2.shape
    bt = min(bt, t)

    def body(x_ref, g_ref, t_ref, loss_ref, dx_ref, dxb_ref, dg_ref):
        @pl.when(pl.program_id(0) == 0)
        def _():
            loss_ref[...] = jnp.zeros_like(loss_ref)
            dg_ref[...] = jnp.zeros_like(dg_ref)

        xv = x_ref[...]
        gv = g_ref[...]
        r = lax.rsqrt(jnp.mean(xv * xv, axis=-1, keepdims=True) + EPS)
        xh = xv * r
        e = xh * gv - t_ref[...]
        loss_ref[...] += 0.5 * jnp.sum(jnp.mean(e * e, axis=-1, keepdims=True), axis=0, keepdims=True)
        dy = e * (1.0 / d)
        dg_ref[...] += jnp.sum(dy * xh, axis=0, keepdims=True)
        u = dy * gv
        dx = r * u - xv * (r * r * r * jnp.mean(u * xv, axis=-1, keepdims=True))
        dx_ref[...] = dx
        dxb_ref[...] = dx.astype(BF16)

    row = pl.BlockSpec((bt, d), lambda i: (i, 0))
    vec = pl.BlockSpec((1, d), lambda i: (0, 0))
    return pl.pallas_call(
        body, name=name, grid=(t // bt,), in_specs=[row, vec, row],
        out_specs=[pl.BlockSpec((1, 1), lambda i: (0, 0)), row, row, vec],
        out_shape=[jax.ShapeDtypeStruct((1, 1), F32), jax.ShapeDtypeStruct((t, d), F32),
                   jax.ShapeDtypeStruct((t, d), BF16), jax.ShapeDtypeStruct((1, d), F32)],
        compiler_params=_params(("arbitrary",)),
    )(x2, g, tgt)


def _adamw_math(w, g, m, v):
    nm = ADAM_B1 * m + (1.0 - ADAM_B1) * g
    nv = ADAM_B2 * v + (1.0 - ADAM_B2) * (g * g)
    m_hat = nm / (1.0 - ADAM_B1 ** ADAM_STEP)
    v_hat = nv / (1.0 - ADAM_B2 ** ADAM_STEP)
    return -ADAM_LR * (m_hat / (jnp.sqrt(v_hat) + ADAM_EPS) + ADAM_WD * w), nm, nv


def adamw(w, g, m, v, *, name):
    rws, cols = w.shape
    br = _row_block(rws)

    def body(w_ref, g_ref, m_ref, v_ref, d_ref, nm_ref, nv_ref):
        d_ref[...], nm_ref[...], nv_ref[...] = _adamw_math(w_ref[...], g_ref[...], m_ref[...], v_ref[...])

    blk = pl.BlockSpec((br, cols), lambda i: (i, 0))
    return pl.pallas_call(
        body, name=name, grid=(rws // br,), in_specs=[blk] * 4, out_specs=[blk] * 3,
        out_shape=[jax.ShapeDtypeStruct((rws, cols), F32)] * 3,
        compiler_params=_params(("parallel",)),
    )(w, g, m, v)


def adamw_halves(w, g_mine, g_other, m, v, core, *, name):
    _, k, n = w.shape
    br = _row_block(k // 2)
    nh = k // 2 // br

    def body(c_ref, w_ref, gm_ref, go_ref, m_ref, v_ref, g_out, d_ref, nm_ref, nv_ref):
        gv = jnp.where(pl.program_id(0) == c_ref[0], gm_ref[...], go_ref[...])
        g_out[0] = gv
        d_ref[0], nm_ref[0], nv_ref[0] = _adamw_math(w_ref[0], gv, m_ref[0], v_ref[0])

    full = pl.BlockSpec((1, br, n), lambda hb, i, c: (0, hb * nh + i, 0))
    half = pl.BlockSpec((br, n), lambda hb, i, c: (i, 0))
    return pl.pallas_call(
        body, name=name,
        grid_spec=pltpu.PrefetchScalarGridSpec(num_scalar_prefetch=1, grid=(2, nh), in_specs=[full, half, half, full, full],
                                               out_specs=[full] * 4),
        out_shape=[jax.ShapeDtypeStruct(w.shape, F32)] * 4,
        compiler_params=_params(("parallel", "parallel")),
    )(core, w, g_mine, g_other, m, v)


def add_pair(dw, recv, core, *, name):
    n4, k, n = dw.shape
    hk = k // 2

    def body(c_ref, a_ref, b_ref, o_ref):
        o_ref[...] = (a_ref[...] + b_ref[...].astype(F32)).astype(BF16)

    return pl.pallas_call(
        body, name=name,
        grid_spec=pltpu.PrefetchScalarGridSpec(
            num_scalar_prefetch=1, grid=(n4,),
            in_specs=[pl.BlockSpec((1, hk, n), lambda q, c: (q, c[0], 0)), pl.BlockSpec((1, hk, n), lambda q, c: (q, 0, 0))],
            out_specs=pl.BlockSpec((1, hk, n), lambda q, c: (q, 0, 0))),
        out_shape=jax.ShapeDtypeStruct((n4, hk, n), BF16),
        compiler_params=_params(("parallel",)),
    )(core, dw, recv)


def sum_chips(parts, *, name):
    n4, r, n = parts.shape
    br = _row_block(r)

    def body(p_ref, o_ref):
        acc = p_ref[0].astype(F32)
        for q in range(1, n4):
            acc = acc + p_ref[q].astype(F32)
        o_ref[...] = acc

    return pl.pallas_call(
        body, name=name, grid=(r // br,),
        in_specs=[pl.BlockSpec((n4, br, n), lambda i: (0, i, 0))], out_specs=pl.BlockSpec((br, n), lambda i: (i, 0)),
        out_shape=jax.ShapeDtypeStruct((r, n), F32),
        compiler_params=_params(("parallel",)),
    )(parts)


ANY = pl.BlockSpec(memory_space=pl.ANY)


def _place():
    x, y, c = lax.axis_index("x"), lax.axis_index("y"), lax.axis_index("c")
    chips = [(1 - x, y), (x, 1 - y), (1 - x, 1 - y)]
    return x, y, c, chips


def _copy(src, dst, send_sems, recv_sems, k, to):
    return pltpu.make_async_remote_copy(src_ref=src, dst_ref=dst, send_sem=send_sems.at[k], recv_sem=recv_sems.at[k],
                                        device_id=to, device_id_type=MESH)


def _half_rows(ref, lead, hf):
    hk = ref.shape[1] // 2
    return ref.at[lead, pl.ds(hf * hk, hk), :]


def _gather_first(srcs, dsts, ssems, rsems):
    x, y, c, chips = _place()
    for ti, (s, d) in enumerate(zip(srcs, dsts)):
        hk = s.shape[0] // 2
        for j, (cx, cy) in enumerate(chips):
            _copy(s.at[pl.ds(c * hk, hk), :], _half_rows(d, 2 * x + y, c), ssems, rsems, 3 * ti + j, (cx, cy, c)).start()


def _gather_mid(srcs, dsts, ssems, rsems):
    x, y, c, chips = _place()
    n1 = 3 * len(srcs)
    for ti, d in enumerate(dsts):
        for j, (cx, cy) in enumerate(chips):
            landed = _half_rows(d, 2 * cx + cy, c)
            _copy(landed, landed, ssems, rsems, 3 * ti + j, (cx, cy, c)).wait_recv()
            _copy(landed, landed, ssems, rsems, n1 + 3 * ti + j, (x, y, 1 - c)).start()


def _gather_last(srcs, dsts, ssems, rsems):
    x, y, c, chips = _place()
    n1 = 3 * len(srcs)
    for ti, (s, d) in enumerate(zip(srcs, dsts)):
        hk = s.shape[0] // 2
        for j, (cx, cy) in enumerate(chips):
            other = _half_rows(d, 2 * cx + cy, 1 - c)
            _copy(other, other, ssems, rsems, n1 + 3 * ti + j, (x, y, 1 - c)).wait_recv()
        for j, (cx, cy) in enumerate(chips):
            mine = s.at[pl.ds(c * hk, hk), :]
            _copy(mine, mine, ssems, rsems, 3 * ti + j, (cx, cy, c)).wait_send()
            _copy(mine, mine, ssems, rsems, n1 + 3 * ti + j, (x, y, 1 - c)).wait_send()


def gather_side(shards):
    return Side(shards, [jax.ShapeDtypeStruct((N_CHIPS,) + s.shape, s.dtype) for s in shards], 6 * len(shards),
                _gather_first, _gather_last, _gather_mid)


def _scatter_first(srcs, dsts, ssems, rsems):
    x, y, c, chips = _place()
    for ti, (s, d) in enumerate(zip(srcs, dsts)):
        for j, (cx, cy) in enumerate(chips):
            _copy(s.at[2 * cx + cy], d.at[2 * x + y], ssems, rsems, 3 * ti + j, (cx, cy, c)).start()


def _scatter_last(srcs, dsts, ssems, rsems):
    x, y, c, chips = _place()
    for ti, (s, d) in enumerate(zip(srcs, dsts)):
        for j, (cx, cy) in enumerate(chips):
            _copy(s.at[2 * cx + cy], d.at[2 * cx + cy], ssems, rsems, 3 * ti + j, (cx, cy, c)).wait_recv()
        for j, (cx, cy) in enumerate(chips):
            _copy(s.at[2 * cx + cy], d.at[2 * cx + cy], ssems, rsems, 3 * ti + j, (cx, cy, c)).wait_send()


def scatter_side(parts):
    return Side(parts, [jax.ShapeDtypeStruct(p.shape, p.dtype) for p in parts], 3 * len(parts), _scatter_first, _scatter_last)


def run_side(side, *, name):
    n_in, n_out = len(side.ins), len(side.out_shapes)

    def body(*refs):
        si, so, sems = _split_refs(refs, [n_in, n_out, 2])
        side.first(si, so, *sems)
        if side.mid is not None:
            side.mid(si, so, *sems)
        side.last(si, so, *sems)

    in_specs, out_specs = side.specs()
    return pl.pallas_call(body, name=name, in_specs=in_specs, out_specs=out_specs, out_shape=side.out_shapes,
                          scratch_shapes=side.sems())(*side.ins)


def swap_sibling(xs, *, name):
    n = len(xs)

    def body(*refs):
        srcs, dsts, sems = _split_refs(refs, [n, n, 2])
        x, y, c, _ = _place()
        copies = [_copy(s, d, *sems, k, (x, y, 1 - c)) for k, (s, d) in enumerate(zip(srcs, dsts))]
        for cp in copies:
            cp.start()
        for cp in copies:
            cp.wait()

    return pl.pallas_call(
        body, name=name, in_specs=[ANY] * n, out_specs=[ANY] * n,
        out_shape=[jax.ShapeDtypeStruct(a.shape, a.dtype) for a in xs],
        scratch_shapes=[pltpu.SemaphoreType.DMA((n,)), pltpu.SemaphoreType.DMA((n,))],
    )(*xs)


def allreduce_small(s):
    n_dev = 8

    def body(s_ref, out_ref, buf, send_sems, recv_sems):
        x, y, c, _ = _place()
        me = 4 * x + 2 * y + c
        buf[me] = s_ref[...]
        sends = []
        for k in range(1, n_dev):
            px = 1 - x if k & 4 else x
            py = 1 - y if k & 2 else y
            pc = 1 - c if k & 1 else c
            cp = _copy(s_ref, buf.at[me], send_sems, recv_sems, k - 1, (px, py, pc))
            cp.start()
            sends.append((cp, 4 * px + 2 * py + pc))
        for k, (cp, peer) in enumerate(sends):
            _copy(s_ref, buf.at[peer], send_sems, recv_sems, k, (x, y, c)).wait_recv()
        for cp, _ in sends:
            cp.wait_send()
        acc = buf[0]
        for d in range(1, n_dev):
            acc = acc + buf[d]
        out_ref[...] = acc

    vm = pl.BlockSpec(memory_space=pltpu.VMEM)
    return pl.pallas_call(
        body, name="allreduce_small", in_specs=[vm], out_specs=vm,
        out_shape=jax.ShapeDtypeStruct(s.shape, F32),
        scratch_shapes=[pltpu.VMEM((n_dev,) + s.shape, F32), pltpu.SemaphoreType.DMA((n_dev - 1,)),
                        pltpu.SemaphoreType.DMA((n_dev - 1,))],
    )(s)


def join_cols(sm):
    n4, k, n = sm.shape
    return sm.transpose(1, 0, 2).reshape(k, n4 * n)


def split_cols(full):
    k, n = full.shape
    return full.reshape(k, N_CHIPS, n // N_CHIPS).transpose(1, 0, 2)


def _pad_heads(w, width):
    lead = w.shape[:-1]
    w = w.reshape(lead + (N_HEADS, width))
    return jnp.pad(w, [(0, 0)] * len(lead) + [(0, 0), (0, LANES - width)]).reshape(lead + (HW,))


def _unpad_heads(w, width):
    lead = w.shape[:-1]
    return w.reshape(lead + (N_HEADS, LANES))[..., :width].reshape(lead + (N_HEADS * width,))


FW = N_HEADS * HEAD_DIM


def split_w_in(w_in):
    d = w_in.shape[0]
    o_f = 3 * FW
    o_cq = o_f + N_HEADS
    o_ckv = o_cq + Q_RANK
    o_kr = o_ckv + KV_RANK

    def z(n):
        return jnp.zeros((d, n), w_in.dtype)

    small = jnp.concatenate([w_in[:, o_cq:o_ckv], w_in[:, o_ckv:o_kr], z(HEAD_DIM), w_in[:, o_kr:], z(LANES - HEAD_DIM - ROPE_DIM),
                             w_in[:, o_f:o_cq], z(LANES - N_HEADS), z(LANES)], axis=1)
    return w_in[:, :o_f], small


def join_w_in(d_qkv, d_small):
    kr = S_KR + HEAD_DIM
    return jnp.concatenate([d_qkv, d_small[:, S_F:S_F + N_HEADS], d_small[:, S_CQ:S_CKV], d_small[:, S_CKV:S_KR],
                            d_small[:, kr:kr + ROPE_DIM]], axis=1)


def rope_tables(pos):
    t = pos.shape[0]
    inv_freq = ROPE_THETA ** (-jnp.arange(0, ROPE_DIM, 2, dtype=F32) / ROPE_DIM)
    ang = pos.astype(F32)[:, None] * inv_freq
    cos, sin = jnp.cos(ang), jnp.sin(ang)
    half = ROPE_DIM // 2

    def z(n):
        return jnp.zeros((t, n), F32)

    tab_c = jnp.concatenate([jnp.ones((t, HEAD_DIM), F32), cos, cos, z(LANES - HEAD_DIM - ROPE_DIM)], axis=1)
    tab_a = jnp.concatenate([z(HEAD_DIM), -sin, z(half), z(LANES - HEAD_DIM - ROPE_DIM)], axis=1)
    tab_b = jnp.concatenate([z(HEAD_DIM), z(half), sin, z(LANES - HEAD_DIM - ROPE_DIM)], axis=1)
    return tab_c, tab_a, tab_b


def _pad_lanes(v, n):
    return jnp.pad(v, ((0, 0), (0, n - v.shape[1])))


ATTN_BLK = 512


def local_step(xs, pos, tgt, gains, w_early, late_weights, fwd_side=None, bwd_side=None):
    g_attn, b_forget, g_q, g_kv, g_fo, g_mo, g_mlp, g_fin = gains
    w_in, w_uq, w_ukv = w_early
    t = xs.shape[0]
    blk = min(ATTN_BLK, t)
    fox_scale = 1.0 / (HEAD_DIM ** 0.5)
    mla_scale = 1.0 / ((HEAD_DIM + ROPE_DIM) ** 0.5)

    w_qkv, w_small = split_w_in(w_in)
    w_uq_p = _pad_heads(w_uq, HEAD_DIM + ROPE_DIM)
    kv = w_ukv.reshape(KV_RANK, N_HEADS, 2 * HEAD_DIM)
    w_ukv_p = jnp.concatenate([_pad_heads(kv[:, :, :HEAD_DIM].reshape(KV_RANK, FW), HEAD_DIM),
                               _pad_heads(kv[:, :, HEAD_DIM:].reshape(KV_RANK, FW), HEAD_DIM)], axis=1)
    b_f = _pad_lanes(b_forget, LANES)
    tab_c, tab_a, tab_b = rope_tables(pos)

    h1 = rmsnorm(xs, g_attn, out_dtype=BF16, name="norm_attn")
    qkv, = mm(h1, w_qkv, out_dtypes=[BF16], name="proj_qkv")
    small, = mm(h1, w_small, out_dtypes=[F32], name="proj_small")
    mq, mk, mv, lf, cqn, ckvn = mla_prep(small, g_q, g_kv, w_uq_p, w_ukv_p, tab_c, tab_a, tab_b, b_f, name="mla_prep")
    f_cum = cumsum_rows(lf, reverse=False, name="gate_cumsum")
    f_blocks = f_cum[:, :N_HEADS].reshape(t // blk, blk, N_HEADS).transpose(0, 2, 1)
    fo, st_f, *gathered = flash_fwd(qkv, qkv, qkv, f_blocks, qoff=0, koff=PAIRS, voff=2 * PAIRS, pair=True,
                                    scale=fox_scale, name="fox_fwd", blk=blk, side=fwd_side)
    w_o, w_up, w_down = late_weights(gathered)
    w_of, w_om = w_o[:FW], w_o[FW:]
    mo, st_m = flash_fwd(mq, mk, mv, None, qoff=0, koff=0, voff=0, pair=False, scale=mla_scale, name="mla_fwd", blk=blk)
    mix_f = rmsnorm(fo, g_fo, out_dtype=BF16, name="norm_fox_out")
    mix_m = rmsnorm(mo, g_mo, out_dtype=BF16, name="norm_mla_out")

    def add_res(acc, res):
        return (acc + res,)

    x1a, = mm(mix_f, w_of, extras=[xs], epilogue=add_res, out_dtypes=[F32], name="out_proj_fox")
    x1, = mm(mix_m, w_om, extras=[x1a], epilogue=add_res, out_dtypes=[F32], name="out_proj_mla")
    h2 = rmsnorm(x1, g_mlp, out_dtype=BF16, name="norm_mlp")

    def relu2(acc):
        r = jnp.maximum(acc, 0.0)
        return acc, r * r

    u, act = mm(h2, w_up, epilogue=relu2, out_dtypes=[BF16, BF16], name="mlp_up")
    x2, = mm(act, w_down, extras=[x1], epilogue=add_res, out_dtypes=[F32], name="mlp_down")
    loss, dx2, dx2b, dg_fin = loss_head(x2, g_fin, tgt, name="loss_head")

    def relu2_grad(acc, uu):
        return (acc * (2.0 * jnp.maximum(uu.astype(F32), 0.0)),)

    du, = mm(dx2b, w_down, trans_b=True, extras=[u], epilogue=relu2_grad, out_dtypes=[BF16], name="mlp_down_bwd")
    dw_down = mm_tn(act, dx2b, name="dw_down").reshape(N_CHIPS, -1, w_down.shape[1])
    dh2, = mm(du, w_up, trans_b=True, out_dtypes=[F32], name="mlp_up_bwd")
    dw_up = mm_tn(h2, du, name="dw_up", col_shards=N_CHIPS)
    dx1, dx1b, dg_mlp = rmsnorm_bwd(dh2, x1, g_mlp, dx2, out_dtypes=[F32, BF16], name="norm_mlp_bwd")

    dmix_f, = mm(dx1b, w_of, trans_b=True, out_dtypes=[F32], name="out_proj_fox_bwd")
    dmix_m, = mm(dx1b, w_om, trans_b=True, out_dtypes=[F32], name="out_proj_mla_bwd")
    dw_o = jnp.concatenate([mm_tn(mix_f, dx1b, name="dw_o_fox"), mm_tn(mix_m, dx1b, name="dw_o_mla")], axis=0)
    dw_o = dw_o.reshape(N_CHIPS, -1, w_o.shape[1])
    side = bwd_side(dw_o, dw_up, dw_down) if bwd_side is not None else None
    dfo, dg_fo = rmsnorm_bwd(dmix_f, fo, g_fo, None, out_dtypes=[BF16], name="norm_fox_out_bwd")
    dmo, dg_mo = rmsnorm_bwd(dmix_m, mo, g_mo, None, out_dtypes=[BF16], name="norm_mla_out_bwd")
    st_f = attn_stats(fo, dfo, st_f, name="fox_stats")
    st_m = attn_stats(mo, dmo, st_m, name="mla_stats")
    dfq, dfk, dfv, df_key, df_query, *scattered = flash_bwd(
        qkv, qkv, qkv, dfo, st_f, f_blocks, qoff=0, koff=PAIRS, voff=2 * PAIRS, pair=True, scale=fox_scale,
        name="fox_bwd", blk=blk, side=side)
    dmq, dmk, dmv = flash_bwd(mq, mk, mv, dmo, st_m, None, qoff=0, koff=0, voff=0, pair=False, scale=mla_scale,
                              name="mla_bwd", blk=blk)
    dqkv = jnp.concatenate([dfq, dfk, dfv], axis=1).astype(BF16)
    d_f = df_key[:, :2, :].reshape(N_HEADS, t).T + df_query[:, :, :2].transpose(1, 0, 2).reshape(t, N_HEADS)
    dlf = cumsum_rows(_pad_lanes(d_f, LANES), reverse=True, name="gate_cumsum_bwd")
    dsmall, dq_u, dkv_u, dg_q, dg_kv, db_f = mla_prep_bwd(dmq, dmk, dmv, dlf, small, g_q, g_kv, w_uq_p, w_ukv_p,
                                                          tab_c, tab_a, tab_b, b_f, name="mla_prep_bwd")
    dw_uq_p = mm_tn(cqn, dq_u, name="dw_uq")
    dw_ukv_p = mm_tn(ckvn, dkv_u, name="dw_ukv")

    dh1a, = mm(dqkv, w_qkv, trans_b=True, out_dtypes=[F32], name="proj_qkv_bwd")
    dh1, = mm(dsmall, w_small, trans_b=True, extras=[dh1a], epilogue=add_res, out_dtypes=[F32], name="proj_small_bwd")
    dw_qkv = mm_tn(h1, dqkv, name="dw_qkv")
    dw_small = mm_tn(h1, dsmall, name="dw_small")
    grad_x, dg_attn = rmsnorm_bwd(dh1, xs, g_attn, dx1, out_dtypes=[F32], name="norm_attn_bwd")

    dw_in = join_w_in(dw_qkv, dw_small)
    dw_uq = _unpad_heads(dw_uq_p, HEAD_DIM + ROPE_DIM)
    dk_cols = _unpad_heads(dw_ukv_p[:, :HW], HEAD_DIM).reshape(KV_RANK, N_HEADS, HEAD_DIM)
    dv_cols = _unpad_heads(dw_ukv_p[:, HW:], HEAD_DIM).reshape(KV_RANK, N_HEADS, HEAD_DIM)
    dw_ukv = jnp.concatenate([dk_cols, dv_cols], axis=2).reshape(KV_RANK, N_HEADS * 2 * HEAD_DIM)
    d_gains = (dg_attn, db_f[:, :N_HEADS], dg_q, dg_kv, dg_fo, dg_mo, dg_mlp, dg_fin)
    return loss, grad_x, (dw_in, dw_uq, dw_ukv), (dw_o, dw_up, dw_down), d_gains, scattered


def kernel(x, positions, attn_norm_g, w_in, b_forget, q_norm_g, w_uq, kv_norm_g, w_ukv, fox_out_g, mla_out_g, w_o, mlp_norm_g, w_up, w_down, final_norm_g, loss_target, m_attn_norm_g, m_w_in, m_b_forget, m_q_norm_g, m_w_uq, m_kv_norm_g, m_w_ukv, m_fox_out_g, m_mla_out_g, m_w_o, m_mlp_norm_g, m_w_up, m_w_down, m_final_norm_g, v_attn_norm_g, v_w_in, v_b_forget, v_q_norm_g, v_w_uq, v_kv_norm_g, v_w_ukv, v_fox_out_g, v_mla_out_g, v_w_o, v_mlp_norm_g, v_w_up, v_w_down, v_final_norm_g):
    core_id = lax.axis_index("c")
    core = core_id.reshape(1).astype(jnp.int32)
    chip = 2 * lax.axis_index("x") + lax.axis_index("y")
    big = [w_in, w_uq, w_ukv, w_o, w_up, w_down]
    big_m = [m_w_in, m_w_uq, m_w_ukv, m_w_o, m_w_up, m_w_down]
    big_v = [v_w_in, v_w_uq, v_w_ukv, v_w_o, v_w_up, v_w_down]
    n_early = 3

    def vec(a):
        return a.reshape(1, -1)

    small = [attn_norm_g, b_forget, q_norm_g, kv_norm_g, fox_out_g, mla_out_g, mlp_norm_g, final_norm_g]
    small_m = [m_attn_norm_g, m_b_forget, m_q_norm_g, m_kv_norm_g, m_fox_out_g, m_mla_out_g, m_mlp_norm_g, m_final_norm_g]
    small_v = [v_attn_norm_g, v_b_forget, v_q_norm_g, v_kv_norm_g, v_fox_out_g, v_mla_out_g, v_mlp_norm_g, v_final_norm_g]
    gains = [vec(a) for a in small]

    shards = [w[0].astype(BF16) for w in big]

    def with_own(gathered, mine):
        return [lax.dynamic_update_index_in_dim(g, s, chip, 0) for g, s in zip(gathered, mine)]

    early = with_own(run_side(gather_side(shards[:n_early]), name="gather_early"), shards[:n_early])
    w_early = [join_cols(g) for g in early]

    def late_weights(gathered):
        g_o, g_up, g_down = with_own(gathered, shards[n_early:])
        return g_o.reshape(-1, g_o.shape[2]), join_cols(g_up), g_down.reshape(-1, g_down.shape[2])

    def pair_sums(grads, name):
        sends = [lax.dynamic_slice_in_dim(g, (1 - core_id) * (g.shape[1] // 2), g.shape[1] // 2, axis=1).astype(BF16)
                 for g in grads]
        return sends

    def add_pairs(grads, recvs, name):
        return [add_pair(g, r, core, name="%s_%d" % (name, n)) for n, (g, r) in enumerate(zip(grads, recvs))]

    def chip_sums(scattered, pairs, name):
        with_mine = [lax.dynamic_update_index_in_dim(s, lax.dynamic_index_in_dim(p, chip, 0, keepdims=True), chip, 0)
                     for s, p in zip(scattered, pairs)]
        return [sum_chips(s, name="%s_%d" % (name, n)) for n, s in enumerate(with_mine)]

    late_pairs = []

    def bwd_side(dw_o, dw_up, dw_down):
        grads = [dw_o, dw_up, dw_down]
        recvs = swap_sibling(pair_sums(grads, "late"), name="swap_late")
        late_pairs.extend(add_pairs(grads, recvs, "add_pair_late"))
        return scatter_side(late_pairs)

    loss, grad_x, d_early, _, d_small, scattered_late = local_step(
        x[0], positions[0], loss_target[0], gains, w_early, late_weights, gather_side(shards[n_early:]), bwd_side)

    halves_late = chip_sums(scattered_late, late_pairs, "sum_chips_late")
    g_early = [split_cols(d) for d in d_early]
    n_late = len(halves_late)
    swapped = swap_sibling(halves_late + pair_sums(g_early, "early"), name="swap_mixed")
    others_late, recvs_early = swapped[:n_late], swapped[n_late:]
    early_pairs = add_pairs(g_early, recvs_early, "add_pair_early")
    scattered_early = run_side(scatter_side(early_pairs), name="scatter_early")
    halves_early = chip_sums(scattered_early, early_pairs, "sum_chips_early")
    others_early = swap_sibling(halves_early, name="swap_early")
    halves = halves_early + halves_late
    others = list(others_early) + list(others_late)

    def rows8(vs):
        return jnp.concatenate([_pad_lanes(vec(a).astype(F32), 1024) for a in vs], axis=0)

    g_small8 = allreduce_small(rows8(d_small))

    outs_big = [adamw_halves(w, gm, go, m, v, core, name="adamw_%d" % n)
                for n, (w, gm, go, m, v) in enumerate(zip(big, halves, others, big_m, big_v))]
    d8, m8, v8 = adamw(rows8(small), g_small8, rows8(small_m), rows8(small_v), name="adamw_small")

    def unrows8(a8):
        return [a8[n, :s.size].reshape(s.shape) for n, s in enumerate(small)]

    loss_all = lax.psum(loss[0, 0], ("x", "y", "c"))
    grads, deltas, new_m, new_v = [None] * 14, [None] * 14, [None] * 14, [None] * 14
    big_at = [1, 4, 6, 9, 11, 12]
    small_at = [0, 2, 3, 5, 7, 8, 10, 13]
    for n, at in enumerate(big_at):
        grads[at], deltas[at], new_m[at], new_v[at] = outs_big[n]
    for at, g, dd, mm_, vv in zip(small_at, unrows8(g_small8), unrows8(d8), unrows8(m8), unrows8(v8)):
        grads[at], deltas[at], new_m[at], new_v[at] = g, dd, mm_, vv
    return (loss_all, grad_x[None], *grads, *deltas, *new_m, *new_v)
```

```python
import jax
import jax.numpy as jnp
from jax import lax
from jax.experimental import pallas as pl
from jax.experimental.pallas import tpu as pltpu

F32 = jnp.float32
BF16 = jnp.bfloat16
MESH = pl.DeviceIdType.MESH

EPS = 1e-6
ROPE_THETA = 10000.0
N_HEADS = 8
PAIRS = N_HEADS // 2
HEAD_DIM = 64
ROPE_DIM = 32
LANES = 128
Q_RANK = 384
KV_RANK = 256
N_CHIPS = 4
ADAM_LR, ADAM_B1, ADAM_B2, ADAM_EPS, ADAM_WD, ADAM_STEP = 0.001, 0.9, 0.999, 1e-08, 0.01, 10
VMEM_LIMIT = 48 * 1024 * 1024
LOG2E = 1.4426950408889634
LN2 = 0.6931471805599453
NN = (((1,), (0,)), ((), ()))
NT = (((1,), (1,)), ((), ()))
TN = (((0,), (0,)), ((), ()))


def _params(sem=None):
    return pltpu.CompilerParams(dimension_semantics=sem, vmem_limit_bytes=VMEM_LIMIT)


def _fit(block, dim):
    if dim <= block:
        return dim
    return next(b for b in range(block - block % LANES, 0, -LANES) if dim % b == 0)


def _row_block(rows):
    return next(b for b in (256, 128, 64, 32, 16, 8) if rows % b == 0)


def rmsnorm(x, g, *, out_dtype, name, bt=512):
    t, d = x.shape
    bt = min(bt, t)

    def body(x_ref, g_ref, o_ref):
        xv = x_ref[...].astype(F32)
        r = lax.rsqrt(jnp.mean(xv * xv, axis=-1, keepdims=True) + EPS)
        o_ref[...] = (xv * r * g_ref[...]).astype(o_ref.dtype)

    return pl.pallas_call(
        body, name=name, grid=(t // bt,),
        in_specs=[pl.BlockSpec((bt, d), lambda i: (i, 0)), pl.BlockSpec((1, d), lambda i: (0, 0))],
        out_specs=pl.BlockSpec((bt, d), lambda i: (i, 0)),
        out_shape=jax.ShapeDtypeStruct((t, d), out_dtype),
        compiler_params=_params(("parallel",)),
    )(x, g)


def rmsnorm_bwd(dh, x, g, res, *, out_dtypes, name, bt=512):
    t, d = x.shape
    bt = min(bt, t)
    has_res = res is not None

    def body(*refs):
        dh_ref, x_ref, g_ref = refs[:3]
        res_ref = refs[3] if has_res else None
        outs = refs[3 + has_res:]
        dx_refs, dg_ref = outs[:-1], outs[-1]
        xv = x_ref[...].astype(F32)
        dhv = dh_ref[...].astype(F32)
        r = lax.rsqrt(jnp.mean(xv * xv, axis=-1, keepdims=True) + EPS)
        u = dhv * g_ref[...]
        dot = jnp.mean(u * xv, axis=-1, keepdims=True)
        dx = r * u - xv * (r * r * r * dot)
        if has_res:
            dx = dx + res_ref[...]
        for o in dx_refs:
            o[...] = dx.astype(o.dtype)

        @pl.when(pl.program_id(0) == 0)
        def _():
            dg_ref[...] = jnp.zeros_like(dg_ref)

        dg_ref[...] += jnp.sum(dhv * (xv * r), axis=0, keepdims=True)

    row = pl.BlockSpec((bt, d), lambda i: (i, 0))
    vec = pl.BlockSpec((1, d), lambda i: (0, 0))
    ins = [dh, x, g] + ([res] if has_res else [])
    return pl.pallas_call(
        body, name=name, grid=(t // bt,),
        in_specs=[row, row, vec] + ([row] if has_res else []),
        out_specs=[row] * len(out_dtypes) + [vec],
        out_shape=[jax.ShapeDtypeStruct((t, d), dt) for dt in out_dtypes] + [jax.ShapeDtypeStruct((1, d), F32)],
        compiler_params=_params(("arbitrary",)),
    )(*ins)


def mm(a, b, *, trans_b=False, extras=(), epilogue=None, out_dtypes, name, bm=1024, bn=1024):
    m, k = a.shape
    n = b.shape[0] if trans_b else b.shape[1]
    if k > 2048:
        bm = bm // 2
    bm, bn = _fit(bm, m), _fit(bn, n)
    n_ex = len(extras)

    def body(*refs):
        a_ref, b_ref = refs[0], refs[1]
        ex = refs[2:2 + n_ex]
        outs = refs[2 + n_ex:]
        acc = lax.dot_general(a_ref[...], b_ref[...], NT if trans_b else NN, preferred_element_type=F32)
        res = epilogue(acc, *[e[...] for e in ex]) if epilogue is not None else (acc,)
        for o, r in zip(outs, res):
            o[...] = r.astype(o.dtype)

    tile = pl.BlockSpec((bm, bn), lambda i, j: (i, j))
    b_spec = pl.BlockSpec((bn, k), lambda i, j: (j, 0)) if trans_b else pl.BlockSpec((k, bn), lambda i, j: (0, j))
    return pl.pallas_call(
        body, name=name, grid=(m // bm, n // bn),
        in_specs=[pl.BlockSpec((bm, k), lambda i, j: (i, 0)), b_spec] + [tile] * n_ex,
        out_specs=[tile] * len(out_dtypes),
        out_shape=[jax.ShapeDtypeStruct((m, n), dt) for dt in out_dtypes],
        compiler_params=_params(("parallel", "parallel")),
    )(a, b, *extras)


def mm_tn(a, b, *, name, col_shards=1, bk=1024, bn=1024, bt=1024):
    t, k = a.shape
    n = b.shape[1]
    ns = n // col_shards
    bk, bn, bt = _fit(bk, k), _fit(bn, ns), _fit(bt, t)
    per = ns // bn

    def body(a_ref, b_ref, o_ref):
        @pl.when(pl.program_id(2) == 0)
        def _():
            o_ref[...] = jnp.zeros_like(o_ref)

        o_ref[...] += lax.dot_general(a_ref[...], b_ref[...], TN, preferred_element_type=F32)

    if col_shards == 1:
        out_spec = pl.BlockSpec((bk, bn), lambda i, j, s: (i, j))
        out_shape = jax.ShapeDtypeStruct((k, n), F32)
    else:
        out_spec = pl.BlockSpec((None, bk, bn), lambda i, j, s: (j // per, i, j % per))
        out_shape = jax.ShapeDtypeStruct((col_shards, k, ns), F32)
    return pl.pallas_call(
        body, name=name, grid=(k // bk, n // bn, t // bt),
        in_specs=[pl.BlockSpec((bt, bk), lambda i, j, s: (s, i)), pl.BlockSpec((bt, bn), lambda i, j, s: (s, j))],
        out_specs=out_spec, out_shape=out_shape,
        compiler_params=_params(("parallel", "parallel", "arbitrary")),
    )(a, b)


def _split3(x):
    hi = x.astype(BF16)
    r1 = x - hi.astype(F32)
    mid = r1.astype(BF16)
    lo = (r1 - mid.astype(F32)).astype(BF16)
    return hi, mid, lo


def cumsum_rows(x, *, reverse, name, bc=512):
    t, d = x.shape
    bc = min(bc, t)
    nb = t // bc

    def body(x_ref, o_ref, carry):
        @pl.when(pl.program_id(0) == 0)
        def _():
            carry[...] = jnp.zeros_like(carry)

        r = lax.broadcasted_iota(jnp.int32, (bc, bc), 0)
        c = lax.broadcasted_iota(jnp.int32, (bc, bc), 1)
        tri = jnp.where((r <= c) if reverse else (r >= c), 1.0, 0.0).astype(BF16)
        hi, mid, lo = _split3(x_ref[...])
        s = (lax.dot_general(tri, hi, NN, preferred_element_type=F32)
             + lax.dot_general(tri, mid, NN, preferred_element_type=F32)
             + lax.dot_general(tri, lo, NN, preferred_element_type=F32)) + carry[0:1, :]
        o_ref[...] = s
        carry[0:1, :] = s[0:1, :] if reverse else s[bc - 1:bc, :]

    imap = (lambda i: (nb - 1 - i, 0)) if reverse else (lambda i: (i, 0))
    return pl.pallas_call(
        body, name=name, grid=(nb,),
        in_specs=[pl.BlockSpec((bc, d), imap)], out_specs=pl.BlockSpec((bc, d), imap),
        out_shape=jax.ShapeDtypeStruct((t, d), F32),
        scratch_shapes=[pltpu.VMEM((8, d), F32)],
        compiler_params=_params(("arbitrary",)),
    )(x)


def _rope(x, c, a, b):
    return x * c + pltpu.roll(x, LANES - ROPE_DIM // 2, 1) * a + pltpu.roll(x, ROPE_DIM // 2, 1) * b


def _rope_bwd(d, c, a, b):
    return d * c + pltpu.roll(d * a, ROPE_DIM // 2, 1) + pltpu.roll(d * b, LANES - ROPE_DIM // 2, 1)


S_CQ, S_CKV, S_KR, S_F, S_END = 0, Q_RANK, Q_RANK + KV_RANK, Q_RANK + KV_RANK + LANES, 1024
HW = N_HEADS * LANES
FW = N_HEADS * HEAD_DIM


def mla_prep(small, g_q, g_kv, w_uq, w_ukv, tab_c, tab_a, tab_b, b_f, *, name, bt=512):
    t = small.shape[0]
    bt = min(bt, t)

    def body(s_ref, gq_ref, gkv_ref, wq_ref, wkv_ref, c_ref, a_ref, b_ref, bf_ref,
             mq_ref, mk_ref, mv_ref, lf_ref, cqn_ref, ckvn_ref):
        cq = s_ref[:, S_CQ:S_CKV]
        rq = lax.rsqrt(jnp.mean(cq * cq, axis=-1, keepdims=True) + EPS)
        cqn = (cq * rq * gq_ref[...]).astype(BF16)
        ckv = s_ref[:, S_CKV:S_KR]
        rkv = lax.rsqrt(jnp.mean(ckv * ckv, axis=-1, keepdims=True) + EPS)
        ckvn = (ckv * rkv * gkv_ref[...]).astype(BF16)
        cqn_ref[...] = cqn
        ckvn_ref[...] = ckvn
        tc, ta, tb = c_ref[...], a_ref[...], b_ref[...]
        q = jnp.dot(cqn, wq_ref[...], preferred_element_type=F32)
        kv = jnp.dot(ckvn, wkv_ref[...], preferred_element_type=F32)
        kr = _rope(s_ref[:, S_KR:S_F], tc, ta, tb)
        for h in range(N_HEADS):
            sl = slice(h * LANES, (h + 1) * LANES)
            mq_ref[:, sl] = _rope(q[:, sl], tc, ta, tb).astype(BF16)
            mk_ref[:, sl] = (kv[:, sl] + kr).astype(BF16)
        mv_ref[...] = kv[:, HW:].astype(BF16)
        z = s_ref[:, S_F:S_END - LANES] + bf_ref[...]
        lf_ref[...] = jnp.minimum(z, 0.0) - jnp.log(1.0 + jnp.exp(-jnp.abs(z)))

    def row(w):
        return pl.BlockSpec((bt, w), lambda i: (i, 0))

    def full(arr):
        return pl.BlockSpec(arr.shape, lambda i: (0, 0))

    return pl.pallas_call(
        body, name=name, grid=(t // bt,),
        in_specs=[row(S_END), full(g_q), full(g_kv), full(w_uq), full(w_ukv), row(LANES), row(LANES), row(LANES), full(b_f)],
        out_specs=[row(HW), row(HW), row(FW), row(LANES), row(Q_RANK), row(KV_RANK)],
        out_shape=[jax.ShapeDtypeStruct((t, HW), BF16)] * 2 + [jax.ShapeDtypeStruct((t, FW), BF16), jax.ShapeDtypeStruct((t, LANES), F32),
                   jax.ShapeDtypeStruct((t, Q_RANK), BF16), jax.ShapeDtypeStruct((t, KV_RANK), BF16)],
        compiler_params=_params(("parallel",)),
    )(small, g_q, g_kv, w_uq, w_ukv, tab_c, tab_a, tab_b, b_f)


def mla_prep_bwd(dmq, dmk, dmv, dlf, small, g_q, g_kv, w_uq, w_ukv, tab_c, tab_a, tab_b, b_f, *, name, bt=512):
    t = small.shape[0]
    bt = min(bt, t)

    def body(dmq_ref, dmk_ref, dmv_ref, dlf_ref, s_ref, gq_ref, gkv_ref, wq_ref, wkv_ref, c_ref, a_ref, b_ref, bf_ref,
             ds_ref, dq_ref, dkv_ref, dgq_ref, dgkv_ref, db_ref):
        tc, ta, tb = c_ref[...], a_ref[...], b_ref[...]
        lane = lax.broadcasted_iota(jnp.int32, (1, LANES), 1)
        dkr = jnp.zeros((bt, LANES), F32)
        for h in range(N_HEADS):
            sl = slice(h * LANES, (h + 1) * LANES)
            dq_ref[:, sl] = _rope_bwd(dmq_ref[:, sl], tc, ta, tb).astype(BF16)
            dkr = dkr + dmk_ref[:, sl]
        dkv_ref[:, :HW] = dmk_ref[...].astype(BF16)
        dkv_ref[:, HW:] = dmv_ref[...].astype(BF16)
        in_rope = (lane >= HEAD_DIM) & (lane < HEAD_DIM + ROPE_DIM)
        ds_ref[:, S_KR:S_F] = jnp.where(in_rope, _rope_bwd(dkr, tc, ta, tb), 0.0).astype(BF16)

        def norm_bwd(raw, g_ref, dn, dg_ref):
            r = lax.rsqrt(jnp.mean(raw * raw, axis=-1, keepdims=True) + EPS)
            u = dn * g_ref[...]
            dot = jnp.mean(u * raw, axis=-1, keepdims=True)
            dg_ref[...] += jnp.sum(dn * (raw * r), axis=0, keepdims=True)
            return r * u - raw * (r * r * r * dot)

        @pl.when(pl.program_id(0) == 0)
        def _():
            dgq_ref[...] = jnp.zeros_like(dgq_ref)
            dgkv_ref[...] = jnp.zeros_like(dgkv_ref)
            db_ref[...] = jnp.zeros_like(db_ref)

        dcqn = lax.dot_general(dq_ref[...], wq_ref[...], NT, preferred_element_type=F32)
        ds_ref[:, S_CQ:S_CKV] = norm_bwd(s_ref[:, S_CQ:S_CKV], gq_ref, dcqn, dgq_ref).astype(BF16)
        dckvn = lax.dot_general(dkv_ref[...], wkv_ref[...], NT, preferred_element_type=F32)
        ds_ref[:, S_CKV:S_KR] = norm_bwd(s_ref[:, S_CKV:S_KR], gkv_ref, dckvn, dgkv_ref).astype(BF16)
        z = s_ref[:, S_F:S_END - LANES] + bf_ref[...]
        dz = jnp.where(lane < N_HEADS, dlf_ref[...] / (1.0 + jnp.exp(z)), 0.0)
        db_ref[...] += jnp.sum(dz, axis=0, keepdims=True)
        ds_ref[:, S_F:S_END - LANES] = dz.astype(BF16)
        ds_ref[:, S_END - LANES:] = jnp.zeros((bt, LANES), BF16)

    def row(w):
        return pl.BlockSpec((bt, w), lambda i: (i, 0))

    def full(arr):
        return pl.BlockSpec(arr.shape, lambda i: (0, 0))

    def vec(w):
        return pl.BlockSpec((1, w), lambda i: (0, 0))

    return pl.pallas_call(
        body, name=name, grid=(t // bt,),
        in_specs=[row(HW), row(HW), row(FW), row(LANES), row(S_END), full(g_q), full(g_kv), full(w_uq), full(w_ukv),
                  row(LANES), row(LANES), row(LANES), full(b_f)],
        out_specs=[row(S_END), row(HW), row(HW + FW), vec(Q_RANK), vec(KV_RANK), vec(LANES)],
        out_shape=[jax.ShapeDtypeStruct((t, S_END), BF16), jax.ShapeDtypeStruct((t, HW), BF16),
                   jax.ShapeDtypeStruct((t, HW + FW), BF16), jax.ShapeDtypeStruct((1, Q_RANK), F32),
                   jax.ShapeDtypeStruct((1, KV_RANK), F32), jax.ShapeDtypeStruct((1, LANES), F32)],
        compiler_params=_params(("arbitrary",)),
    )(dmq, dmk, dmv, dlf, small, g_q, g_kv, w_uq, w_ukv, tab_c, tab_a, tab_b, b_f)


class Side:
    def __init__(self, ins, out_shapes, n_sems, first, last, mid=None):
        self.ins, self.out_shapes, self.n_sems = list(ins), list(out_shapes), n_sems
        self.first, self.mid, self.last = first, mid, last

    def specs(self):
        return [ANY] * len(self.ins), [ANY] * len(self.out_shapes)

    def sems(self):
        return [pltpu.SemaphoreType.DMA((self.n_sems,)), pltpu.SemaphoreType.DMA((self.n_sems,))]


def _lane():
    return lax.broadcasted_iota(jnp.int32, (1, LANES), 1)


def _halves(x):
    zero = jnp.zeros_like(x)
    return [jnp.where(_lane() < HEAD_DIM, x, zero), jnp.where(_lane() >= HEAD_DIM, x, zero)]


def _groups(x):
    return [x[:, :LANES], x[:, LANES:]]


def _lanes01(a, b, rest):
    return jnp.where(_lane() == 0, a, jnp.where(_lane() == 1, b, rest))


def _pick_row(tile, h):
    row = lax.broadcasted_iota(jnp.int32, (tile.shape[0], 1), 0)
    return jnp.sum(jnp.where(row == h, tile, 0.0), axis=0, keepdims=True)


def _pick_lane(tile, h):
    return jnp.sum(jnp.where(_lane() == h, tile, 0.0), axis=1, keepdims=True)


def _row_halves(x):
    row = lax.broadcasted_iota(jnp.int32, (LANES, 1), 0)
    zero = jnp.zeros_like(x)
    return [jnp.where(row < HEAD_DIM, x, zero), jnp.where(row >= HEAD_DIM, x, zero)]


def _below_diagonal(s):
    r = lax.broadcasted_iota(jnp.int32, s.shape, 0)
    c = lax.broadcasted_iota(jnp.int32, s.shape, 1)
    return jnp.where(c <= r, s, -jnp.inf)


def _above_diagonal(s):
    r = lax.broadcasted_iota(jnp.int32, s.shape, 0)
    c = lax.broadcasted_iota(jnp.int32, s.shape, 1)
    return jnp.where(r <= c, s, -jnp.inf)


def to_blocks_t(x, blk):
    t, c = x.shape
    return x.reshape(t // blk, blk, c).transpose(0, 2, 1)


def _split_refs(refs, counts):
    out, at = [], 0
    for n in counts:
        out.append(refs[at:at + n])
        at += n
    return out


def flash_fwd(qt_arr, k_arr, vt_arr, f_cum, *, qoff, koff, voff, pair, scale, name, blk=512, side=None):
    t = k_arr.shape[0]
    blk = min(blk, t)
    nb = t // blk
    w = LANES if pair else 2 * LANES
    has_bias = f_cum is not None
    ins = [qt_arr, k_arr, vt_arr] + ([f_cum] if has_bias else [])
    s_ins, s_outs = (side.ins, side.out_shapes) if side else ([], [])

    def body(*refs):
        main, si, outs, so, sems = _split_refs(refs, [len(ins), len(s_ins), 2, len(s_outs), 2 if side else 0])
        qt_ref, k_ref, vt_ref = main[:3]
        f_ref = main[3] if has_bias else None
        o_ref, st_ref = outs
        g, i = pl.program_id(0), pl.program_id(1)
        step_id = g * nb + i
        if side:
            @pl.when(step_id == 0)
            def _():
                side.first(si, so, *sems)

            if side.mid is not None:
                @pl.when(step_id == (PAIRS * nb) // 2)
                def _():
                    side.mid(si, so, *sems)

        qt = (qt_ref[0].astype(F32) * (scale * LOG2E)).astype(BF16)
        qts = _row_halves(qt) if pair else [qt[:LANES], qt[LANES:]]

        def step(j, carry, diagonal):
            rows = pl.ds(pl.multiple_of(j * blk, blk), blk)
            kk = k_ref[rows, :]
            ks = [kk, kk] if pair else _groups(kk)
            vt = vt_ref[j]
            out = []
            for n in range(2):
                m, l, acc = carry[n]
                s = jnp.dot(ks[n], qts[n], preferred_element_type=F32)
                if has_bias:
                    s = s - LOG2E * _pick_lane(f_ref[rows, :], 2 * g + n)
                if diagonal:
                    s = _above_diagonal(s)
                m_new = jnp.maximum(m, jnp.max(s, axis=0, keepdims=True))
                alpha = jnp.exp2(m - m_new)
                p = jnp.exp2(s - m_new)
                out.append((m_new, alpha * l + jnp.sum(p, axis=0, keepdims=True),
                            alpha * acc + jnp.dot(vt[n * HEAD_DIM:(n + 1) * HEAD_DIM], p.astype(BF16),
                                                  preferred_element_type=F32)))
            return tuple(out)

        init = tuple((jnp.full((1, blk), -jnp.inf, F32), jnp.zeros((1, blk), F32), jnp.zeros((HEAD_DIM, blk), F32))
                     for _ in range(2))
        carry = lax.fori_loop(0, i, lambda j, c: step(j, c, False), init)
        (ma, la, acca), (mb, lb, accb) = step(i, carry, True)
        o_ref[...] = jnp.concatenate([acca / la, accb / lb], axis=0).T
        row = lax.broadcasted_iota(jnp.int32, (LANES, 1), 0)
        st_ref[0] = jnp.where(row == 0, ma + jnp.log2(la), jnp.where(row == 1, mb + jnp.log2(lb), 0.0)).T
        if side:
            @pl.when(step_id == PAIRS * nb - 1)
            def _():
                side.last(si, so, *sems)

    in_specs = [pl.BlockSpec((1, w, blk), lambda g, i: (i, qoff + g, 0)), pl.BlockSpec((t, w), lambda g, i: (0, koff + g)),
                pl.BlockSpec((nb, LANES, blk), lambda g, i: (0, voff + g, 0))]
    if has_bias:
        in_specs.append(pl.BlockSpec((t, LANES), lambda g, i: (0, 0)))
    s_in_specs, s_out_specs = side.specs() if side else ([], [])
    return pl.pallas_call(
        body, name=name, grid=(PAIRS, nb), in_specs=in_specs + s_in_specs,
        out_specs=[pl.BlockSpec((blk, LANES), lambda g, i: (i, g)), pl.BlockSpec((1, blk, LANES), lambda g, i: (g, i, 0))]
        + s_out_specs,
        out_shape=[jax.ShapeDtypeStruct((t, PAIRS * LANES), F32), jax.ShapeDtypeStruct((PAIRS, t, LANES), F32)] + list(s_outs),
        scratch_shapes=side.sems() if side else [],
        compiler_params=_params(("arbitrary", "arbitrary")),
    )(*ins, *s_ins)


def attn_stats(o, do, st, *, name, bt=512):
    t = o.shape[0]
    bt = min(bt, t)

    def body(o_ref, do_ref, st_ref, out_ref):
        prod = o_ref[...] * do_ref[...].astype(F32)
        for g in range(PAIRS):
            grp = prod[:, g * LANES:(g + 1) * LANES]
            da = jnp.sum(jnp.where(_lane() < HEAD_DIM, grp, 0.0), axis=1, keepdims=True)
            db = jnp.sum(jnp.where(_lane() >= HEAD_DIM, grp, 0.0), axis=1, keepdims=True)
            out_ref[g] = jnp.where(_lane() == 2, da, jnp.where(_lane() == 3, db, st_ref[g]))

    row = pl.BlockSpec((bt, PAIRS * LANES), lambda i: (i, 0))
    stat = pl.BlockSpec((PAIRS, bt, LANES), lambda i: (0, i, 0))
    return pl.pallas_call(
        body, name=name, grid=(t // bt,), in_specs=[row, row, stat], out_specs=stat,
        out_shape=jax.ShapeDtypeStruct(st.shape, F32),
        compiler_params=_params(("parallel",)),
    )(o, do, st)


def flash_bwd(q_arr, qt_arr, k_arr, v_arr, do_arr, dot_arr, st, f_blocks, *, qoff, koff, voff, pair, scale, name, blk=512,
              side=None):
    t = q_arr.shape[0]
    blk = min(blk, t)
    nb = t // blk
    w = LANES if pair else 2 * LANES
    hw = w // 2
    has_bias = f_blocks is not None
    split = _halves if pair else _groups
    ins = [q_arr, qt_arr, k_arr, v_arr, do_arr, dot_arr, st] + ([f_blocks] if has_bias else [])
    n_out = 5 if has_bias else 3
    s_ins, s_outs = (side.ins, side.out_shapes) if side else ([], [])

    def body(*refs):
        main, si, outs, so, sems = _split_refs(refs, [len(ins), len(s_ins), n_out, len(s_outs), 2 if side else 0])
        q_ref, qt_ref, k_ref, v_ref, do_ref, dot_ref, st_ref = main[:7]
        dq_ref, dk_ref, dv_ref = outs[:3]
        g, j = pl.program_id(0), pl.program_id(1)
        step_id = g * nb + j
        if side:
            @pl.when(step_id == 0)
            def _():
                side.first(si, so, *sems)

        kk, vv = k_ref[...], v_ref[...]
        ks = [kk, kk] if pair else _groups(kk)
        if has_bias:
            f_ref, dfk_ref, dfq_ref = main[7], outs[3], outs[4]
            fk = [LOG2E * _pick_row(f_ref[0], 2 * g + n) for n in range(2)]

        def step(i, carry, diagonal):
            rows = pl.ds(pl.multiple_of(i * blk, blk), blk)
            qs = split((q_ref[rows, :].astype(F32) * (scale * LOG2E)).astype(BF16))
            qt = (qt_ref[i].astype(F32) * (scale * LOG2E)).astype(BF16)
            dos = [h.astype(BF16) for h in _halves(do_ref[rows, :].astype(F32))]
            dot = dot_ref[i]
            stats = st_ref[0, rows, :]
            new, dqs, row_sums = [], [], []
            for n in range(2):
                dkt, dvt, dfk = carry[n]
                s = lax.dot_general(qs[n], ks[n], NT, preferred_element_type=F32)
                if has_bias:
                    s = s - fk[n]
                if diagonal:
                    s = _below_diagonal(s)
                p = jnp.exp2(s - stats[:, n:n + 1])
                dp = lax.dot_general(dos[n], vv, NT, preferred_element_type=F32)
                ds = p * (dp - stats[:, 2 + n:3 + n])
                dsb = ds.astype(BF16)
                dvt = dvt + jnp.dot(dot[n * HEAD_DIM:(n + 1) * HEAD_DIM], p.astype(BF16), preferred_element_type=F32)
                dkt = dkt + jnp.dot(qt[n * hw:(n + 1) * hw], dsb, preferred_element_type=F32)
                dqs.append(jnp.dot(dsb, ks[n], preferred_element_type=F32))
                if has_bias:
                    dfk = dfk - jnp.sum(ds, axis=0, keepdims=True)
                    row_sums.append(jnp.sum(ds, axis=1, keepdims=True))
                new.append((dkt, dvt, dfk))
            dq = (jnp.where(_lane() < HEAD_DIM, dqs[0], dqs[1]) if pair else jnp.concatenate(dqs, axis=1)) * scale
            rs = _lanes01(row_sums[0], row_sums[1], 0.0) if has_bias else None

            @pl.when(j == 0)
            def _():
                dq_ref[rows, :] = dq
                if has_bias:
                    dfq_ref[0, rows, :] = rs

            @pl.when(j > 0)
            def _():
                dq_ref[rows, :] += dq
                if has_bias:
                    dfq_ref[0, rows, :] += rs

            return tuple(new)

        init = tuple((jnp.zeros((hw, blk), F32), jnp.zeros((HEAD_DIM, blk), F32), jnp.zeros((1, blk), F32)) for _ in range(2))
        carry = step(j, init, True)
        (dka, dva, dfa), (dkb, dvb, dfb) = lax.fori_loop(j + 1, nb, lambda i, c: step(i, c, False), carry)
        dk_ref[...] = jnp.concatenate([dka, dkb], axis=0).T * LN2
        dv_ref[...] = jnp.concatenate([dva, dvb], axis=0).T
        if has_bias:
            row = lax.broadcasted_iota(jnp.int32, (N_HEADS, 1), 0)
            dfk_ref[0] = jnp.where(row == 0, dfa, jnp.where(row == 1, dfb, 0.0))
        if side:
            @pl.when(step_id == PAIRS * nb - 1)
            def _():
                side.last(si, so, *sems)

    in_specs = [pl.BlockSpec((t, w), lambda g, j: (0, qoff + g)), pl.BlockSpec((nb, w, blk), lambda g, j: (0, qoff + g, 0)),
                pl.BlockSpec((blk, w), lambda g, j: (j, koff + g)), pl.BlockSpec((blk, LANES), lambda g, j: (j, voff + g)),
                pl.BlockSpec((t, LANES), lambda g, j: (0, g)), pl.BlockSpec((nb, LANES, blk), lambda g, j: (0, g, 0)),
                pl.BlockSpec((1, t, LANES), lambda g, j: (g, 0, 0))]
    out_specs = [pl.BlockSpec((t, w), lambda g, j: (0, g)), pl.BlockSpec((blk, w), lambda g, j: (j, g)),
                 pl.BlockSpec((blk, LANES), lambda g, j: (j, g))]
    out_shape = [jax.ShapeDtypeStruct((t, PAIRS * w), F32)] * 2 + [jax.ShapeDtypeStruct((t, PAIRS * LANES), F32)]
    if has_bias:
        in_specs.append(pl.BlockSpec((1, N_HEADS, blk), lambda g, j: (j, 0, 0)))
        out_specs += [pl.BlockSpec((1, N_HEADS, blk), lambda g, j: (g, 0, j)), pl.BlockSpec((1, t, LANES), lambda g, j: (g, 0, 0))]
        out_shape += [jax.ShapeDtypeStruct((PAIRS, N_HEADS, t), F32), jax.ShapeDtypeStruct((PAIRS, t, LANES), F32)]
    s_in_specs, s_out_specs = side.specs() if side else ([], [])
    return pl.pallas_call(
        body, name=name, grid=(PAIRS, nb), in_specs=in_specs + s_in_specs, out_specs=out_specs + s_out_specs,
        out_shape=out_shape + list(s_outs), scratch_shapes=side.sems() if side else [],
        compiler_params=_params(("arbitrary", "arbitrary")),
    )(*ins, *s_ins)


def loss_head(x2, g, tgt, *, name, bt=512):
    t, d = x2.shape
    bt = min(bt, t)

    def body(x_ref, g_ref, t_ref, loss_ref, dx_ref, dxb_ref, dg_ref):
        @pl.when(pl.program_id(0) == 0)
        def _():
            loss_ref[...] = jnp.zeros_like(loss_ref)
            dg_ref[...] = jnp.zeros_like(dg_ref)

        xv = x_ref[...]
        gv = g_ref[...]
        r = lax.rsqrt(jnp.mean(xv * xv, axis=-1, keepdims=True) + EPS)
        xh = xv * r
        e = xh * gv - t_ref[...]
        loss_ref[...] += 0.5 * jnp.sum(jnp.mean(e * e, axis=-1, keepdims=True), axis=0, keepdims=True)
        dy = e * (1.0 / d)
        dg_ref[...] += jnp.sum(dy * xh, axis=0, keepdims=True)
        u = dy * gv
        dx = r * u - xv * (r * r * r * jnp.mean(u * xv, axis=-1, keepdims=True))
        dx_ref[...] = dx
        dxb_ref[...] = dx.astype(BF16)

    row = pl.BlockSpec((bt, d), lambda i: (i, 0))
    vec = pl.BlockSpec((1, d), lambda i: (0, 0))
    return pl.pallas_call(
        body, name=name, grid=(t // bt,), in_specs=[row, vec, row],
        out_specs=[pl.BlockSpec((1, 1), lambda i: (0, 0)), row, row, vec],
        out_shape=[jax.ShapeDtypeStruct((1, 1), F32), jax.ShapeDtypeStruct((t, d), F32),
                   jax.ShapeDtypeStruct((t, d), BF16), jax.ShapeDtypeStruct((1, d), F32)],
        compiler_params=_params(("arbitrary",)),
    )(x2, g, tgt)


def _adamw_math(w, g, m, v):
    nm = ADAM_B1 * m + (1.0 - ADAM_B1) * g
    nv = ADAM_B2 * v + (1.0 - ADAM_B2) * (g * g)
    m_hat = nm / (1.0 - ADAM_B1 ** ADAM_STEP)
    v_hat = nv / (1.0 - ADAM_B2 ** ADAM_STEP)
    return -ADAM_LR * (m_hat / (jnp.sqrt(v_hat) + ADAM_EPS) + ADAM_WD * w), nm, nv


def adamw(w, g, m, v, *, name):
    rws, cols = w.shape
    br = _row_block(rws)

    def body(w_ref, g_ref, m_ref, v_ref, d_ref, nm_ref, nv_ref):
        d_ref[...], nm_ref[...], nv_ref[...] = _adamw_math(w_ref[...], g_ref[...], m_ref[...], v_ref[...])

    blk = pl.BlockSpec((br, cols), lambda i: (i, 0))
    return pl.pallas_call(
        body, name=name, grid=(rws // br,), in_specs=[blk] * 4, out_specs=[blk] * 3,
        out_shape=[jax.ShapeDtypeStruct((rws, cols), F32)] * 3,
        compiler_params=_params(("parallel",)),
    )(w, g, m, v)


def adamw_halves(w, g_mine, g_other, m, v, core, *, name):
    _, k, n = w.shape
    br = _row_block(k // 2)
    nh = k // 2 // br

    def body(c_ref, w_ref, gm_ref, go_ref, m_ref, v_ref, g_out, d_ref, nm_ref, nv_ref):
        gv = jnp.where(pl.program_id(0) == c_ref[0], gm_ref[...], go_ref[...])
        g_out[0] = gv
        d_ref[0], nm_ref[0], nv_ref[0] = _adamw_math(w_ref[0], gv, m_ref[0], v_ref[0])

    full = pl.BlockSpec((1, br, n), lambda hb, i, c: (0, hb * nh + i, 0))
    half = pl.BlockSpec((br, n), lambda hb, i, c: (i, 0))
    return pl.pallas_call(
        body, name=name,
        grid_spec=pltpu.PrefetchScalarGridSpec(num_scalar_prefetch=1, grid=(2, nh), in_specs=[full, half, half, full, full],
                                               out_specs=[full] * 4),
        out_shape=[jax.ShapeDtypeStruct(w.shape, F32)] * 4,
        compiler_params=_params(("parallel", "parallel")),
    )(core, w, g_mine, g_other, m, v)


def add_pair(dw, recv, core, *, name):
    n4, k, n = dw.shape
    hk = k // 2

    def body(c_ref, a_ref, b_ref, o_ref):
        o_ref[...] = (a_ref[...] + b_ref[...].astype(F32)).astype(BF16)

    return pl.pallas_call(
        body, name=name,
        grid_spec=pltpu.PrefetchScalarGridSpec(
            num_scalar_prefetch=1, grid=(n4,),
            in_specs=[pl.BlockSpec((1, hk, n), lambda q, c: (q, c[0], 0)), pl.BlockSpec((1, hk, n), lambda q, c: (q, 0, 0))],
            out_specs=pl.BlockSpec((1, hk, n), lambda q, c: (q, 0, 0))),
        out_shape=jax.ShapeDtypeStruct((n4, hk, n), BF16),
        compiler_params=_params(("parallel",)),
    )(core, dw, recv)


def sum_chips(parts, *, name):
    n4, r, n = parts.shape
    br = _row_block(r)

    def body(p_ref, o_ref):
        acc = p_ref[0].astype(F32)
        for q in range(1, n4):
            acc = acc + p_ref[q].astype(F32)
        o_ref[...] = acc

    return pl.pallas_call(
        body, name=name, grid=(r // br,),
        in_specs=[pl.BlockSpec((n4, br, n), lambda i: (0, i, 0))], out_specs=pl.BlockSpec((br, n), lambda i: (i, 0)),
        out_shape=jax.ShapeDtypeStruct((r, n), F32),
        compiler_params=_params(("parallel",)),
    )(parts)


ANY = pl.BlockSpec(memory_space=pl.ANY)


def _place():
    x, y, c = lax.axis_index("x"), lax.axis_index("y"), lax.axis_index("c")
    chips = [(1 - x, y), (x, 1 - y), (1 - x, 1 - y)]
    return x, y, c, chips


def _copy(src, dst, send_sems, recv_sems, k, to):
    return pltpu.make_async_remote_copy(src_ref=src, dst_ref=dst, send_sem=send_sems.at[k], recv_sem=recv_sems.at[k],
                                        device_id=to, device_id_type=MESH)


def _half_rows(ref, lead, hf):
    hk = ref.shape[1] // 2
    return ref.at[lead, pl.ds(hf * hk, hk), :]


def _gather_first(srcs, dsts, ssems, rsems):
    x, y, c, chips = _place()
    for ti, (s, d) in enumerate(zip(srcs, dsts)):
        hk = s.shape[0] // 2
        for j, (cx, cy) in enumerate(chips):
            _copy(s.at[pl.ds(c * hk, hk), :], _half_rows(d, 2 * x + y, c), ssems, rsems, 3 * ti + j, (cx, cy, c)).start()


def _gather_mid(srcs, dsts, ssems, rsems):
    x, y, c, chips = _place()
    n1 = 3 * len(srcs)
    for ti, d in enumerate(dsts):
        for j, (cx, cy) in enumerate(chips):
            landed = _half_rows(d, 2 * cx + cy, c)
            _copy(landed, landed, ssems, rsems, 3 * ti + j, (cx, cy, c)).wait_recv()
            _copy(landed, landed, ssems, rsems, n1 + 3 * ti + j, (x, y, 1 - c)).start()


def _gather_last(srcs, dsts, ssems, rsems):
    x, y, c, chips = _place()
    n1 = 3 * len(srcs)
    for ti, (s, d) in enumerate(zip(srcs, dsts)):
        hk = s.shape[0] // 2
        for j, (cx, cy) in enumerate(chips):
            other = _half_rows(d, 2 * cx + cy, 1 - c)
            _copy(other, other, ssems, rsems, n1 + 3 * ti + j, (x, y, 1 - c)).wait_recv()
        for j, (cx, cy) in enumerate(chips):
            mine = s.at[pl.ds(c * hk, hk), :]
            _copy(mine, mine, ssems, rsems, 3 * ti + j, (cx, cy, c)).wait_send()
            _copy(mine, mine, ssems, rsems, n1 + 3 * ti + j, (x, y, 1 - c)).wait_send()


def gather_side(shards):
    return Side(shards, [jax.ShapeDtypeStruct((N_CHIPS,) + s.shape, s.dtype) for s in shards], 6 * len(shards),
                _gather_first, _gather_last, _gather_mid)


def _scatter_first(srcs, dsts, ssems, rsems):
    x, y, c, chips = _place()
    for ti, (s, d) in enumerate(zip(srcs, dsts)):
        for j, (cx, cy) in enumerate(chips):
            _copy(s.at[2 * cx + cy], d.at[2 * x + y], ssems, rsems, 3 * ti + j, (cx, cy, c)).start()


def _scatter_last(srcs, dsts, ssems, rsems):
    x, y, c, chips = _place()
    for ti, (s, d) in enumerate(zip(srcs, dsts)):
        for j, (cx, cy) in enumerate(chips):
            _copy(s.at[2 * cx + cy], d.at[2 * cx + cy], ssems, rsems, 3 * ti + j, (cx, cy, c)).wait_recv()
        for j, (cx, cy) in enumerate(chips):
            _copy(s.at[2 * cx + cy], d.at[2 * cx + cy], ssems, rsems, 3 * ti + j, (cx, cy, c)).wait_send()


def scatter_side(parts):
    return Side(parts, [jax.ShapeDtypeStruct(p.shape, p.dtype) for p in parts], 3 * len(parts), _scatter_first, _scatter_last)


def run_side(side, *, name):
    n_in, n_out = len(side.ins), len(side.out_shapes)

    def body(*refs):
        si, so, sems = _split_refs(refs, [n_in, n_out, 2])
        side.first(si, so, *sems)
        if side.mid is not None:
            side.mid(si, so, *sems)
        side.last(si, so, *sems)

    in_specs, out_specs = side.specs()
    return pl.pallas_call(body, name=name, in_specs=in_specs, out_specs=out_specs, out_shape=side.out_shapes,
                          scratch_shapes=side.sems())(*side.ins)


def swap_sibling(xs, *, name):
    n = len(xs)

    def body(*refs):
        srcs, dsts, sems = _split_refs(refs, [n, n, 2])
        x, y, c, _ = _place()
        copies = [_copy(s, d, *sems, k, (x, y, 1 - c)) for k, (s, d) in enumerate(zip(srcs, dsts))]
        for cp in copies:
            cp.start()
        for cp in copies:
            cp.wait()

    return pl.pallas_call(
        body, name=name, in_specs=[ANY] * n, out_specs=[ANY] * n,
        out_shape=[jax.ShapeDtypeStruct(a.shape, a.dtype) for a in xs],
        scratch_shapes=[pltpu.SemaphoreType.DMA((n,)), pltpu.SemaphoreType.DMA((n,))],
    )(*xs)


def allreduce_small(s):
    n_dev = 8

    def body(s_ref, out_ref, buf, send_sems, recv_sems):
        x, y, c, _ = _place()
        me = 4 * x + 2 * y + c
        buf[me] = s_ref[...]
        sends = []
        for k in range(1, n_dev):
            px = 1 - x if k & 4 else x
            py = 1 - y if k & 2 else y
            pc = 1 - c if k & 1 else c
            cp = _copy(s_ref, buf.at[me], send_sems, recv_sems, k - 1, (px, py, pc))
            cp.start()
            sends.append((cp, 4 * px + 2 * py + pc))
        for k, (cp, peer) in enumerate(sends):
            _copy(s_ref, buf.at[peer], send_sems, recv_sems, k, (x, y, c)).wait_recv()
        for cp, _ in sends:
            cp.wait_send()
        acc = buf[0]
        for d in range(1, n_dev):
            acc = acc + buf[d]
        out_ref[...] = acc

    vm = pl.BlockSpec(memory_space=pltpu.VMEM)
    return pl.pallas_call(
        body, name="allreduce_small", in_specs=[vm], out_specs=vm,
        out_shape=jax.ShapeDtypeStruct(s.shape, F32),
        scratch_shapes=[pltpu.VMEM((n_dev,) + s.shape, F32), pltpu.SemaphoreType.DMA((n_dev - 1,)),
                        pltpu.SemaphoreType.DMA((n_dev - 1,))],
    )(s)


def join_cols(sm):
    n4, k, n = sm.shape
    return sm.transpose(1, 0, 2).reshape(k, n4 * n)


def split_cols(full):
    k, n = full.shape
    return full.reshape(k, N_CHIPS, n // N_CHIPS).transpose(1, 0, 2)


def _pad_heads(w, width):
    lead = w.shape[:-1]
    w = w.reshape(lead + (N_HEADS, width))
    return jnp.pad(w, [(0, 0)] * len(lead) + [(0, 0), (0, LANES - width)]).reshape(lead + (HW,))


def _unpad_heads(w, width):
    lead = w.shape[:-1]
    return w.reshape(lead + (N_HEADS, LANES))[..., :width].reshape(lead + (N_HEADS * width,))


def split_w_in(w_in):
    d = w_in.shape[0]
    o_f = 3 * FW
    o_cq = o_f + N_HEADS
    o_ckv = o_cq + Q_RANK
    o_kr = o_ckv + KV_RANK

    def z(n):
        return jnp.zeros((d, n), w_in.dtype)

    small = jnp.concatenate([w_in[:, o_cq:o_ckv], w_in[:, o_ckv:o_kr], z(HEAD_DIM), w_in[:, o_kr:], z(LANES - HEAD_DIM - ROPE_DIM),
                             w_in[:, o_f:o_cq], z(LANES - N_HEADS), z(LANES)], axis=1)
    return w_in[:, :o_f], small


def join_w_in(d_qkv, d_small):
    kr = S_KR + HEAD_DIM
    return jnp.concatenate([d_qkv, d_small[:, S_F:S_F + N_HEADS], d_small[:, S_CQ:S_CKV], d_small[:, S_CKV:S_KR],
                            d_small[:, kr:kr + ROPE_DIM]], axis=1)


def rope_tables(pos):
    t = pos.shape[0]
    inv_freq = ROPE_THETA ** (-jnp.arange(0, ROPE_DIM, 2, dtype=F32) / ROPE_DIM)
    ang = pos.astype(F32)[:, None] * inv_freq
    cos, sin = jnp.cos(ang), jnp.sin(ang)
    half = ROPE_DIM // 2

    def z(n):
        return jnp.zeros((t, n), F32)

    tab_c = jnp.concatenate([jnp.ones((t, HEAD_DIM), F32), cos, cos, z(LANES - HEAD_DIM - ROPE_DIM)], axis=1)
    tab_a = jnp.concatenate([z(HEAD_DIM), -sin, z(half), z(LANES - HEAD_DIM - ROPE_DIM)], axis=1)
    tab_b = jnp.concatenate([z(HEAD_DIM), z(half), sin, z(LANES - HEAD_DIM - ROPE_DIM)], axis=1)
    return tab_c, tab_a, tab_b


def _pad_lanes(v, n):
    return jnp.pad(v, ((0, 0), (0, n - v.shape[1])))


ATTN_BLK = 512


def local_step(xs, pos, tgt, gains, w_early, late_weights, fwd_side=None, bwd_side=None):
    g_attn, b_forget, g_q, g_kv, g_fo, g_mo, g_mlp, g_fin = gains
    w_in, w_uq, w_ukv = w_early
    t = xs.shape[0]
    blk = min(ATTN_BLK, t)
    fox_scale = 1.0 / (HEAD_DIM ** 0.5)
    mla_scale = 1.0 / ((HEAD_DIM + ROPE_DIM) ** 0.5)

    w_qkv, w_small = split_w_in(w_in)
    w_uq_p = _pad_heads(w_uq, HEAD_DIM + ROPE_DIM)
    kv = w_ukv.reshape(KV_RANK, N_HEADS, 2 * HEAD_DIM)
    w_ukv_p = jnp.concatenate([_pad_heads(kv[:, :, :HEAD_DIM].reshape(KV_RANK, FW), HEAD_DIM),
                               kv[:, :, HEAD_DIM:].reshape(KV_RANK, FW)], axis=1)
    b_f = _pad_lanes(b_forget, LANES)
    tab_c, tab_a, tab_b = rope_tables(pos)

    h1 = rmsnorm(xs, g_attn, out_dtype=BF16, name="norm_attn")
    qkv, = mm(h1, w_qkv, out_dtypes=[BF16], name="proj_qkv")
    small, = mm(h1, w_small, out_dtypes=[F32], name="proj_small")
    mq, mk, mv, lf, cqn, ckvn = mla_prep(small, g_q, g_kv, w_uq_p, w_ukv_p, tab_c, tab_a, tab_b, b_f, name="mla_prep")
    f_cum = cumsum_rows(lf, reverse=False, name="gate_cumsum")
    f_blocks = f_cum[:, :N_HEADS].reshape(t // blk, blk, N_HEADS).transpose(0, 2, 1)
    qkv_t, mq_t, mv_t = to_blocks_t(qkv, blk), to_blocks_t(mq, blk), to_blocks_t(mv, blk)
    fo, st_f, *gathered = flash_fwd(qkv_t, qkv, qkv_t, f_cum, qoff=0, koff=PAIRS, voff=2 * PAIRS, pair=True,
                                    scale=fox_scale, name="fox_fwd", blk=blk, side=fwd_side)
    w_o, w_up, w_down = late_weights(gathered)
    w_of, w_om = w_o[:FW], w_o[FW:]
    mo, st_m = flash_fwd(mq_t, mk, mv_t, None, qoff=0, koff=0, voff=0, pair=False, scale=mla_scale, name="mla_fwd", blk=blk)
    mix_f = rmsnorm(fo, g_fo, out_dtype=BF16, name="norm_fox_out")
    mix_m = rmsnorm(mo, g_mo, out_dtype=BF16, name="norm_mla_out")

    def add_res(acc, res):
        return (acc + res,)

    x1a, = mm(mix_f, w_of, extras=[xs], epilogue=add_res, out_dtypes=[F32], name="out_proj_fox")
    x1, = mm(mix_m, w_om, extras=[x1a], epilogue=add_res, out_dtypes=[F32], name="out_proj_mla")
    h2 = rmsnorm(x1, g_mlp, out_dtype=BF16, name="norm_mlp")

    def relu2(acc):
        r = jnp.maximum(acc, 0.0)
        return acc, r * r

    u, act = mm(h2, w_up, epilogue=relu2, out_dtypes=[BF16, BF16], name="mlp_up")
    x2, = mm(act, w_down, extras=[x1], epilogue=add_res, out_dtypes=[F32], name="mlp_down")
    loss, dx2, dx2b, dg_fin = loss_head(x2, g_fin, tgt, name="loss_head")

    def relu2_grad(acc, uu):
        return (acc * (2.0 * jnp.maximum(uu.astype(F32), 0.0)),)

    du, = mm(dx2b, w_down, trans_b=True, extras=[u], epilogue=relu2_grad, out_dtypes=[BF16], name="mlp_down_bwd")
    dw_down = mm_tn(act, dx2b, name="dw_down").reshape(N_CHIPS, -1, w_down.shape[1])
    dh2, = mm(du, w_up, trans_b=True, out_dtypes=[F32], name="mlp_up_bwd")
    dw_up = mm_tn(h2, du, name="dw_up", col_shards=N_CHIPS)
    dx1, dx1b, dg_mlp = rmsnorm_bwd(dh2, x1, g_mlp, dx2, out_dtypes=[F32, BF16], name="norm_mlp_bwd")

    dmix_f, = mm(dx1b, w_of, trans_b=True, out_dtypes=[F32], name="out_proj_fox_bwd")
    dmix_m, = mm(dx1b, w_om, trans_b=True, out_dtypes=[F32], name="out_proj_mla_bwd")
    dw_o = jnp.concatenate([mm_tn(mix_f, dx1b, name="dw_o_fox"), mm_tn(mix_m, dx1b, name="dw_o_mla")], axis=0)
    dw_o = dw_o.reshape(N_CHIPS, -1, w_o.shape[1])
    side = bwd_side(dw_o, dw_up, dw_down) if bwd_side is not None else None
    dfo, dg_fo = rmsnorm_bwd(dmix_f, fo, g_fo, None, out_dtypes=[BF16], name="norm_fox_out_bwd")
    dmo, dg_mo = rmsnorm_bwd(dmix_m, mo, g_mo, None, out_dtypes=[BF16], name="norm_mla_out_bwd")
    st_f = attn_stats(fo, dfo, st_f, name="fox_stats")
    st_m = attn_stats(mo, dmo, st_m, name="mla_stats")
    dfq, dfk, dfv, df_key, df_query, *scattered = flash_bwd(
        qkv, qkv_t, qkv, qkv, dfo, to_blocks_t(dfo, blk), st_f, f_blocks, qoff=0, koff=PAIRS, voff=2 * PAIRS, pair=True,
        scale=fox_scale, name="fox_bwd", blk=blk, side=side)
    dmq, dmk, dmv = flash_bwd(mq, mq_t, mk, mv, dmo, to_blocks_t(dmo, blk), st_m, None, qoff=0, koff=0, voff=0, pair=False,
                              scale=mla_scale, name="mla_bwd", blk=blk)
    dqkv = jnp.concatenate([dfq, dfk, dfv], axis=1).astype(BF16)
    d_f = df_key[:, :2, :].reshape(N_HEADS, t).T + df_query[:, :, :2].transpose(1, 0, 2).reshape(t, N_HEADS)
    dlf = cumsum_rows(_pad_lanes(d_f, LANES), reverse=True, name="gate_cumsum_bwd")
    dsmall, dq_u, dkv_u, dg_q, dg_kv, db_f = mla_prep_bwd(dmq, dmk, dmv, dlf, small, g_q, g_kv, w_uq_p, w_ukv_p,
                                                          tab_c, tab_a, tab_b, b_f, name="mla_prep_bwd")
    dw_uq_p = mm_tn(cqn, dq_u, name="dw_uq")
    dw_ukv_p = mm_tn(ckvn, dkv_u, name="dw_ukv")

    dh1a, = mm(dqkv, w_qkv, trans_b=True, out_dtypes=[F32], name="proj_qkv_bwd")
    dh1, = mm(dsmall, w_small, trans_b=True, extras=[dh1a], epilogue=add_res, out_dtypes=[F32], name="proj_small_bwd")
    dw_qkv = mm_tn(h1, dqkv, name="dw_qkv")
    dw_small = mm_tn(h1, dsmall, name="dw_small")
    grad_x, dg_attn = rmsnorm_bwd(dh1, xs, g_attn, dx1, out_dtypes=[F32], name="norm_attn_bwd")

    dw_in = join_w_in(dw_qkv, dw_small)
    dw_uq = _unpad_heads(dw_uq_p, HEAD_DIM + ROPE_DIM)
    dk_cols = _unpad_heads(dw_ukv_p[:, :HW], HEAD_DIM).reshape(KV_RANK, N_HEADS, HEAD_DIM)
    dv_cols = dw_ukv_p[:, HW:].reshape(KV_RANK, N_HEADS, HEAD_DIM)
    dw_ukv = jnp.concatenate([dk_cols, dv_cols], axis=2).reshape(KV_RANK, N_HEADS * 2 * HEAD_DIM)
    d_gains = (dg_attn, db_f[:, :N_HEADS], dg_q, dg_kv, dg_fo, dg_mo, dg_mlp, dg_fin)
    return loss, grad_x, (dw_in, dw_uq, dw_ukv), (dw_o, dw_up, dw_down), d_gains, scattered


def kernel(x, positions, attn_norm_g, w_in, b_forget, q_norm_g, w_uq, kv_norm_g, w_ukv, fox_out_g, mla_out_g, w_o, mlp_norm_g, w_up, w_down, final_norm_g, loss_target, m_attn_norm_g, m_w_in, m_b_forget, m_q_norm_g, m_w_uq, m_kv_norm_g, m_w_ukv, m_fox_out_g, m_mla_out_g, m_w_o, m_mlp_norm_g, m_w_up, m_w_down, m_final_norm_g, v_attn_norm_g, v_w_in, v_b_forget, v_q_norm_g, v_w_uq, v_kv_norm_g, v_w_ukv, v_fox_out_g, v_mla_out_g, v_w_o, v_mlp_norm_g, v_w_up, v_w_down, v_final_norm_g):
    core_id = lax.axis_index("c")
    core = core_id.reshape(1).astype(jnp.int32)
    chip = 2 * lax.axis_index("x") + lax.axis_index("y")
    big = [w_in, w_uq, w_ukv, w_o, w_up, w_down]
    big_m = [m_w_in, m_w_uq, m_w_ukv, m_w_o, m_w_up, m_w_down]
    big_v = [v_w_in, v_w_uq, v_w_ukv, v_w_o, v_w_up, v_w_down]
    n_early = 3

    def vec(a):
        return a.reshape(1, -1)

    small = [attn_norm_g, b_forget, q_norm_g, kv_norm_g, fox_out_g, mla_out_g, mlp_norm_g, final_norm_g]
    small_m = [m_attn_norm_g, m_b_forget, m_q_norm_g, m_kv_norm_g, m_fox_out_g, m_mla_out_g, m_mlp_norm_g, m_final_norm_g]
    small_v = [v_attn_norm_g, v_b_forget, v_q_norm_g, v_kv_norm_g, v_fox_out_g, v_mla_out_g, v_mlp_norm_g, v_final_norm_g]
    gains = [vec(a) for a in small]

    shards = [w[0].astype(BF16) for w in big]

    def with_own(gathered, mine):
        return [lax.dynamic_update_index_in_dim(g, s, chip, 0) for g, s in zip(gathered, mine)]

    early = with_own(run_side(gather_side(shards[:n_early]), name="gather_early"), shards[:n_early])
    w_early = [join_cols(g) for g in early]

    def late_weights(gathered):
        g_o, g_up, g_down = with_own(gathered, shards[n_early:])
        return g_o.reshape(-1, g_o.shape[2]), join_cols(g_up), g_down.reshape(-1, g_down.shape[2])

    def pair_sums(grads, name):
        sends = [lax.dynamic_slice_in_dim(g, (1 - core_id) * (g.shape[1] // 2), g.shape[1] // 2, axis=1).astype(BF16)
                 for g in grads]
        return sends

    def add_pairs(grads, recvs, name):
        return [add_pair(g, r, core, name="%s_%d" % (name, n)) for n, (g, r) in enumerate(zip(grads, recvs))]

    def chip_sums(scattered, pairs, name):
        with_mine = [lax.dynamic_update_index_in_dim(s, lax.dynamic_index_in_dim(p, chip, 0, keepdims=True), chip, 0)
                     for s, p in zip(scattered, pairs)]
        return [sum_chips(s, name="%s_%d" % (name, n)) for n, s in enumerate(with_mine)]

    late_pairs = []

    def bwd_side(dw_o, dw_up, dw_down):
        grads = [dw_o, dw_up, dw_down]
        recvs = swap_sibling(pair_sums(grads, "late"), name="swap_late")
        late_pairs.extend(add_pairs(grads, recvs, "add_pair_late"))
        return scatter_side(late_pairs)

    loss, grad_x, d_early, _, d_small, scattered_late = local_step(
        x[0], positions[0], loss_target[0], gains, w_early, late_weights, gather_side(shards[n_early:]), bwd_side)

    halves_late = chip_sums(scattered_late, late_pairs, "sum_chips_late")
    g_early = [split_cols(d) for d in d_early]
    n_late = len(halves_late)
    swapped = swap_sibling(halves_late + pair_sums(g_early, "early"), name="swap_mixed")
    others_late, recvs_early = swapped[:n_late], swapped[n_late:]
    early_pairs = add_pairs(g_early, recvs_early, "add_pair_early")
    scattered_early = run_side(scatter_side(early_pairs), name="scatter_early")
    halves_early = chip_sums(scattered_early, early_pairs, "sum_chips_early")
    others_early = swap_sibling(halves_early, name="swap_early")
    halves = halves_early + halves_late
    others = list(others_early) + list(others_late)

    def rows8(vs):
        return jnp.concatenate([_pad_lanes(vec(a).astype(F32), 1024) for a in vs], axis=0)

    g_small8 = allreduce_small(rows8(d_small))

    outs_big = [adamw_halves(w, gm, go, m, v, core, name="adamw_%d" % n)
                for n, (w, gm, go, m, v) in enumerate(zip(big, halves, others, big_m, big_v))]
    d8, m8, v8 = adamw(rows8(small), g_small8, rows8(small_m), rows8(small_v), name="adamw_small")

    def unrows8(a8):
        return [a8[n, :s.size].reshape(s.shape) for n, s in enumerate(small)]

    loss_all = lax.psum(loss[0, 0], ("x", "y", "c"))
    grads, deltas, new_m, new_v = [None] * 14, [None] * 14, [None] * 14, [None] * 14
    big_at = [1, 4, 6, 9, 11, 12]
    small_at = [0, 2, 3, 5, 7, 8, 10, 13]
    for n, at in enumerate(big_at):
        grads[at], deltas[at], new_m[at], new_v[at] = outs_big[n]
    for at, g, dd, mm_, vv in zip(small_at, unrows8(g_small8), unrows8(d8), unrows8(m8), unrows8(v8)):
        grads[at], deltas[at], new_m[at], new_v[at] = g, dd, mm_, vv
    return (loss_all, grad_x[None], *grads, *deltas, *new_m, *new_v)
```

```python
import jax
import jax.numpy as jnp
from jax import lax
from jax.experimental import pallas as pl
from jax.experimental.pallas import tpu as pltpu

F32 = jnp.float32
BF16 = jnp.bfloat16
MESH = pl.DeviceIdType.MESH

EPS = 1e-6
ROPE_THETA = 10000.0
N_HEADS = 8
PAIRS = N_HEADS // 2
HEAD_DIM = 64
ROPE_DIM = 32
LANES = 128
Q_RANK = 384
KV_RANK = 256
N_CHIPS = 4
ADAM_LR, ADAM_B1, ADAM_B2, ADAM_EPS, ADAM_WD, ADAM_STEP = 0.001, 0.9, 0.999, 1e-08, 0.01, 10
VMEM_LIMIT = 48 * 1024 * 1024
LOG2E = 1.4426950408889634
LN2 = 0.6931471805599453
NN = (((1,), (0,)), ((), ()))
NT = (((1,), (1,)), ((), ()))
TN = (((0,), (0,)), ((), ()))


def _params(sem=None):
    return pltpu.CompilerParams(dimension_semantics=sem, vmem_limit_bytes=VMEM_LIMIT)


def _fit(block, dim):
    if dim <= block:
        return dim
    return next(b for b in range(block - block % LANES, 0, -LANES) if dim % b == 0)


def _row_block(rows):
    return next(b for b in (256, 128, 64, 32, 16, 8) if rows % b == 0)


def rmsnorm(x, g, *, out_dtype, name, bt=512):
    t, d = x.shape
    bt = min(bt, t)

    def body(x_ref, g_ref, o_ref):
        xv = x_ref[...].astype(F32)
        r = lax.rsqrt(jnp.mean(xv * xv, axis=-1, keepdims=True) + EPS)
        o_ref[...] = (xv * r * g_ref[...]).astype(o_ref.dtype)

    return pl.pallas_call(
        body, name=name, grid=(t // bt,),
        in_specs=[pl.BlockSpec((bt, d), lambda i: (i, 0)), pl.BlockSpec((1, d), lambda i: (0, 0))],
        out_specs=pl.BlockSpec((bt, d), lambda i: (i, 0)),
        out_shape=jax.ShapeDtypeStruct((t, d), out_dtype),
        compiler_params=_params(("parallel",)),
    )(x, g)


def rmsnorm_bwd(dh, x, g, res, *, out_dtypes, name, bt=512):
    t, d = x.shape
    bt = min(bt, t)
    has_res = res is not None

    def body(*refs):
        dh_ref, x_ref, g_ref = refs[:3]
        res_ref = refs[3] if has_res else None
        outs = refs[3 + has_res:]
        dx_refs, dg_ref = outs[:-1], outs[-1]
        xv = x_ref[...].astype(F32)
        dhv = dh_ref[...].astype(F32)
        r = lax.rsqrt(jnp.mean(xv * xv, axis=-1, keepdims=True) + EPS)
        u = dhv * g_ref[...]
        dot = jnp.mean(u * xv, axis=-1, keepdims=True)
        dx = r * u - xv * (r * r * r * dot)
        if has_res:
            dx = dx + res_ref[...]
        for o in dx_refs:
            o[...] = dx.astype(o.dtype)

        @pl.when(pl.program_id(0) == 0)
        def _():
            dg_ref[...] = jnp.zeros_like(dg_ref)

        dg_ref[...] += jnp.sum(dhv * (xv * r), axis=0, keepdims=True)

    row = pl.BlockSpec((bt, d), lambda i: (i, 0))
    vec = pl.BlockSpec((1, d), lambda i: (0, 0))
    ins = [dh, x, g] + ([res] if has_res else [])
    return pl.pallas_call(
        body, name=name, grid=(t // bt,),
        in_specs=[row, row, vec] + ([row] if has_res else []),
        out_specs=[row] * len(out_dtypes) + [vec],
        out_shape=[jax.ShapeDtypeStruct((t, d), dt) for dt in out_dtypes] + [jax.ShapeDtypeStruct((1, d), F32)],
        compiler_params=_params(("arbitrary",)),
    )(*ins)


def mm(a, b, *, trans_b=False, extras=(), epilogue=None, out_dtypes, name, bm=1024, bn=1024):
    m, k = a.shape
    n = b.shape[0] if trans_b else b.shape[1]
    if k > 2048:
        bm = bm // 2
    bm, bn = _fit(bm, m), _fit(bn, n)
    n_ex = len(extras)

    def body(*refs):
        a_ref, b_ref = refs[0], refs[1]
        ex = refs[2:2 + n_ex]
        outs = refs[2 + n_ex:]
        acc = lax.dot_general(a_ref[...], b_ref[...], NT if trans_b else NN, preferred_element_type=F32)
        res = epilogue(acc, *[e[...] for e in ex]) if epilogue is not None else (acc,)
        for o, r in zip(outs, res):
            o[...] = r.astype(o.dtype)

    tile = pl.BlockSpec((bm, bn), lambda i, j: (i, j))
    b_spec = pl.BlockSpec((bn, k), lambda i, j: (j, 0)) if trans_b else pl.BlockSpec((k, bn), lambda i, j: (0, j))
    return pl.pallas_call(
        body, name=name, grid=(m // bm, n // bn),
        in_specs=[pl.BlockSpec((bm, k), lambda i, j: (i, 0)), b_spec] + [tile] * n_ex,
        out_specs=[tile] * len(out_dtypes),
        out_shape=[jax.ShapeDtypeStruct((m, n), dt) for dt in out_dtypes],
        compiler_params=_params(("parallel", "parallel")),
    )(a, b, *extras)


def mm_tn(a, b, *, name, col_shards=1, bk=1024, bn=1024, bt=1024):
    t, k = a.shape
    n = b.shape[1]
    ns = n // col_shards
    bk, bn, bt = _fit(bk, k), _fit(bn, ns), _fit(bt, t)
    per = ns // bn

    def body(a_ref, b_ref, o_ref):
        @pl.when(pl.program_id(2) == 0)
        def _():
            o_ref[...] = jnp.zeros_like(o_ref)

        o_ref[...] += lax.dot_general(a_ref[...], b_ref[...], TN, preferred_element_type=F32)

    if col_shards == 1:
        out_spec = pl.BlockSpec((bk, bn), lambda i, j, s: (i, j))
        out_shape = jax.ShapeDtypeStruct((k, n), F32)
    else:
        out_spec = pl.BlockSpec((None, bk, bn), lambda i, j, s: (j // per, i, j % per))
        out_shape = jax.ShapeDtypeStruct((col_shards, k, ns), F32)
    return pl.pallas_call(
        body, name=name, grid=(k // bk, n // bn, t // bt),
        in_specs=[pl.BlockSpec((bt, bk), lambda i, j, s: (s, i)), pl.BlockSpec((bt, bn), lambda i, j, s: (s, j))],
        out_specs=out_spec, out_shape=out_shape,
        compiler_params=_params(("parallel", "parallel", "arbitrary")),
    )(a, b)


def _split3(x):
    hi = x.astype(BF16)
    r1 = x - hi.astype(F32)
    mid = r1.astype(BF16)
    lo = (r1 - mid.astype(F32)).astype(BF16)
    return hi, mid, lo


def cumsum_rows(x, *, reverse, name, bc=512):
    t, d = x.shape
    bc = min(bc, t)
    nb = t // bc

    def body(x_ref, o_ref, carry):
        @pl.when(pl.program_id(0) == 0)
        def _():
            carry[...] = jnp.zeros_like(carry)

        r = lax.broadcasted_iota(jnp.int32, (bc, bc), 0)
        c = lax.broadcasted_iota(jnp.int32, (bc, bc), 1)
        tri = jnp.where((r <= c) if reverse else (r >= c), 1.0, 0.0).astype(BF16)
        hi, mid, lo = _split3(x_ref[...])
        s = (lax.dot_general(tri, hi, NN, preferred_element_type=F32)
             + lax.dot_general(tri, mid, NN, preferred_element_type=F32)
             + lax.dot_general(tri, lo, NN, preferred_element_type=F32)) + carry[0:1, :]
        o_ref[...] = s
        carry[0:1, :] = s[0:1, :] if reverse else s[bc - 1:bc, :]

    imap = (lambda i: (nb - 1 - i, 0)) if reverse else (lambda i: (i, 0))
    return pl.pallas_call(
        body, name=name, grid=(nb,),
        in_specs=[pl.BlockSpec((bc, d), imap)], out_specs=pl.BlockSpec((bc, d), imap),
        out_shape=jax.ShapeDtypeStruct((t, d), F32),
        scratch_shapes=[pltpu.VMEM((8, d), F32)],
        compiler_params=_params(("arbitrary",)),
    )(x)


def _rope(x, c, a, b):
    return x * c + pltpu.roll(x, LANES - ROPE_DIM // 2, 1) * a + pltpu.roll(x, ROPE_DIM // 2, 1) * b


def _rope_bwd(d, c, a, b):
    return d * c + pltpu.roll(d * a, ROPE_DIM // 2, 1) + pltpu.roll(d * b, LANES - ROPE_DIM // 2, 1)


S_CQ, S_CKV, S_KR, S_F, S_END = 0, Q_RANK, Q_RANK + KV_RANK, Q_RANK + KV_RANK + LANES, 1024
HW = N_HEADS * LANES
FW = N_HEADS * HEAD_DIM


def mla_prep(small, g_q, g_kv, w_uq, w_ukv, tab_c, tab_a, tab_b, b_f, *, name, bt=512):
    t = small.shape[0]
    bt = min(bt, t)

    def body(s_ref, gq_ref, gkv_ref, wq_ref, wkv_ref, c_ref, a_ref, b_ref, bf_ref,
             mq_ref, mk_ref, mv_ref, lf_ref, cqn_ref, ckvn_ref):
        cq = s_ref[:, S_CQ:S_CKV]
        rq = lax.rsqrt(jnp.mean(cq * cq, axis=-1, keepdims=True) + EPS)
        cqn = (cq * rq * gq_ref[...]).astype(BF16)
        ckv = s_ref[:, S_CKV:S_KR]
        rkv = lax.rsqrt(jnp.mean(ckv * ckv, axis=-1, keepdims=True) + EPS)
        ckvn = (ckv * rkv * gkv_ref[...]).astype(BF16)
        cqn_ref[...] = cqn
        ckvn_ref[...] = ckvn
        tc, ta, tb = c_ref[...], a_ref[...], b_ref[...]
        q = jnp.dot(cqn, wq_ref[...], preferred_element_type=F32)
        kv = jnp.dot(ckvn, wkv_ref[...], preferred_element_type=F32)
        kr = _rope(s_ref[:, S_KR:S_F], tc, ta, tb)
        for h in range(N_HEADS):
            sl = slice(h * LANES, (h + 1) * LANES)
            mq_ref[:, sl] = _rope(q[:, sl], tc, ta, tb).astype(BF16)
            mk_ref[:, sl] = (kv[:, sl] + kr).astype(BF16)
        mv_ref[...] = kv[:, HW:].astype(BF16)
        z = s_ref[:, S_F:S_END - LANES] + bf_ref[...]
        lf_ref[...] = jnp.minimum(z, 0.0) - jnp.log(1.0 + jnp.exp(-jnp.abs(z)))

    def row(w):
        return pl.BlockSpec((bt, w), lambda i: (i, 0))

    def full(arr):
        return pl.BlockSpec(arr.shape, lambda i: (0, 0))

    return pl.pallas_call(
        body, name=name, grid=(t // bt,),
        in_specs=[row(S_END), full(g_q), full(g_kv), full(w_uq), full(w_ukv), row(LANES), row(LANES), row(LANES), full(b_f)],
        out_specs=[row(HW), row(HW), row(FW), row(LANES), row(Q_RANK), row(KV_RANK)],
        out_shape=[jax.ShapeDtypeStruct((t, HW), BF16)] * 2 + [jax.ShapeDtypeStruct((t, FW), BF16), jax.ShapeDtypeStruct((t, LANES), F32),
                   jax.ShapeDtypeStruct((t, Q_RANK), BF16), jax.ShapeDtypeStruct((t, KV_RANK), BF16)],
        compiler_params=_params(("parallel",)),
    )(small, g_q, g_kv, w_uq, w_ukv, tab_c, tab_a, tab_b, b_f)


def mla_prep_bwd(dmq, dmk, dmv, dlf, small, g_q, g_kv, w_uq, w_ukv, tab_c, tab_a, tab_b, b_f, *, name, bt=512):
    t = small.shape[0]
    bt = min(bt, t)

    def body(dmq_ref, dmk_ref, dmv_ref, dlf_ref, s_ref, gq_ref, gkv_ref, wq_ref, wkv_ref, c_ref, a_ref, b_ref, bf_ref,
             ds_ref, dq_ref, dkv_ref, dgq_ref, dgkv_ref, db_ref):
        tc, ta, tb = c_ref[...], a_ref[...], b_ref[...]
        lane = lax.broadcasted_iota(jnp.int32, (1, LANES), 1)
        dkr = jnp.zeros((bt, LANES), F32)
        for h in range(N_HEADS):
            sl = slice(h * LANES, (h + 1) * LANES)
            dq_ref[:, sl] = _rope_bwd(dmq_ref[:, sl], tc, ta, tb).astype(BF16)
            dkr = dkr + dmk_ref[:, sl]
        dkv_ref[:, :HW] = dmk_ref[...].astype(BF16)
        dkv_ref[:, HW:] = dmv_ref[...].astype(BF16)
        in_rope = (lane >= HEAD_DIM) & (lane < HEAD_DIM + ROPE_DIM)
        ds_ref[:, S_KR:S_F] = jnp.where(in_rope, _rope_bwd(dkr, tc, ta, tb), 0.0).astype(BF16)

        def norm_bwd(raw, g_ref, dn, dg_ref):
            r = lax.rsqrt(jnp.mean(raw * raw, axis=-1, keepdims=True) + EPS)
            u = dn * g_ref[...]
            dot = jnp.mean(u * raw, axis=-1, keepdims=True)
            dg_ref[...] += jnp.sum(dn * (raw * r), axis=0, keepdims=True)
            return r * u - raw * (r * r * r * dot)

        @pl.when(pl.program_id(0) == 0)
        def _():
            dgq_ref[...] = jnp.zeros_like(dgq_ref)
            dgkv_ref[...] = jnp.zeros_like(dgkv_ref)
            db_ref[...] = jnp.zeros_like(db_ref)

        dcqn = lax.dot_general(dq_ref[...], wq_ref[...], NT, preferred_element_type=F32)
        ds_ref[:, S_CQ:S_CKV] = norm_bwd(s_ref[:, S_CQ:S_CKV], gq_ref, dcqn, dgq_ref).astype(BF16)
        dckvn = lax.dot_general(dkv_ref[...], wkv_ref[...], NT, preferred_element_type=F32)
        ds_ref[:, S_CKV:S_KR] = norm_bwd(s_ref[:, S_CKV:S_KR], gkv_ref, dckvn, dgkv_ref).astype(BF16)
        z = s_ref[:, S_F:S_END - LANES] + bf_ref[...]
        dz = jnp.where(lane < N_HEADS, dlf_ref[...] / (1.0 + jnp.exp(z)), 0.0)
        db_ref[...] += jnp.sum(dz, axis=0, keepdims=True)
        ds_ref[:, S_F:S_END - LANES] = dz.astype(BF16)
        ds_ref[:, S_END - LANES:] = jnp.zeros((bt, LANES), BF16)

    def row(w):
        return pl.BlockSpec((bt, w), lambda i: (i, 0))

    def full(arr):
        return pl.BlockSpec(arr.shape, lambda i: (0, 0))

    def vec(w):
        return pl.BlockSpec((1, w), lambda i: (0, 0))

    return pl.pallas_call(
        body, name=name, grid=(t // bt,),
        in_specs=[row(HW), row(HW), row(FW), row(LANES), row(S_END), full(g_q), full(g_kv), full(w_uq), full(w_ukv),
                  row(LANES), row(LANES), row(LANES), full(b_f)],
        out_specs=[row(S_END), row(HW), row(HW + FW), vec(Q_RANK), vec(KV_RANK), vec(LANES)],
        out_shape=[jax.ShapeDtypeStruct((t, S_END), BF16), jax.ShapeDtypeStruct((t, HW), BF16),
                   jax.ShapeDtypeStruct((t, HW + FW), BF16), jax.ShapeDtypeStruct((1, Q_RANK), F32),
                   jax.ShapeDtypeStruct((1, KV_RANK), F32), jax.ShapeDtypeStruct((1, LANES), F32)],
        compiler_params=_params(("arbitrary",)),
    )(dmq, dmk, dmv, dlf, small, g_q, g_kv, w_uq, w_ukv, tab_c, tab_a, tab_b, b_f)


class Side:
    def __init__(self, ins, out_shapes, n_sems, first, last, mid=None):
        self.ins, self.out_shapes, self.n_sems = list(ins), list(out_shapes), n_sems
        self.first, self.mid, self.last = first, mid, last

    def specs(self):
        return [ANY] * len(self.ins), [ANY] * len(self.out_shapes)

    def sems(self):
        return [pltpu.SemaphoreType.DMA((self.n_sems,)), pltpu.SemaphoreType.DMA((self.n_sems,))]


def _lane():
    return lax.broadcasted_iota(jnp.int32, (1, LANES), 1)


def _halves(x):
    zero = jnp.zeros_like(x)
    return [jnp.where(_lane() < HEAD_DIM, x, zero), jnp.where(_lane() >= HEAD_DIM, x, zero)]


def _groups(x):
    return [x[:, :LANES], x[:, LANES:]]


def _lanes01(a, b, rest):
    return jnp.where(_lane() == 0, a, jnp.where(_lane() == 1, b, rest))


def _pick_row(tile, h):
    row = lax.broadcasted_iota(jnp.int32, (tile.shape[0], 1), 0)
    return jnp.sum(jnp.where(row == h, tile, 0.0), axis=0, keepdims=True)


def _pick_lane(tile, h):
    return jnp.sum(jnp.where(_lane() == h, tile, 0.0), axis=1, keepdims=True)


def _row_halves(x):
    row = lax.broadcasted_iota(jnp.int32, (LANES, 1), 0)
    zero = jnp.zeros_like(x)
    return [jnp.where(row < HEAD_DIM, x, zero), jnp.where(row >= HEAD_DIM, x, zero)]


def _below_diagonal(s):
    r = lax.broadcasted_iota(jnp.int32, s.shape, 0)
    c = lax.broadcasted_iota(jnp.int32, s.shape, 1)
    return jnp.where(c <= r, s, -jnp.inf)


def _above_diagonal(s):
    r = lax.broadcasted_iota(jnp.int32, s.shape, 0)
    c = lax.broadcasted_iota(jnp.int32, s.shape, 1)
    return jnp.where(r <= c, s, -jnp.inf)


def to_blocks_t(x, blk):
    t, c = x.shape
    return x.reshape(t // blk, blk, c).transpose(0, 2, 1)


def _split_refs(refs, counts):
    out, at = [], 0
    for n in counts:
        out.append(refs[at:at + n])
        at += n
    return out


def flash_fwd(qt_arr, k_arr, vt_arr, f_cum, *, qoff, koff, voff, pair, scale, name, blk=512, side=None):
    t = k_arr.shape[0]
    tblk = qt_arr.shape[2]
    blk = max(min(blk, t), tblk)
    sub = blk // tblk
    nb = t // blk
    w = LANES if pair else 2 * LANES
    has_bias = f_cum is not None
    ins = [qt_arr, k_arr, vt_arr] + ([f_cum] if has_bias else [])

    def wide(ref, first):
        parts = [ref[first + u] for u in range(sub)]
        return parts[0] if sub == 1 else jnp.concatenate(parts, axis=1)
    s_ins, s_outs = (side.ins, side.out_shapes) if side else ([], [])

    def body(*refs):
        main, si, outs, so, sems = _split_refs(refs, [len(ins), len(s_ins), 2, len(s_outs), 2 if side else 0])
        qt_ref, k_ref, vt_ref = main[:3]
        f_ref = main[3] if has_bias else None
        o_ref, st_ref = outs
        g, i = pl.program_id(0), pl.program_id(1)
        step_id = g * nb + i
        if side:
            @pl.when(step_id == 0)
            def _():
                side.first(si, so, *sems)

            if side.mid is not None:
                @pl.when(step_id == (PAIRS * nb) // 2)
                def _():
                    side.mid(si, so, *sems)

        qt = (wide(qt_ref, 0).astype(F32) * (scale * LOG2E)).astype(BF16)
        qts = _row_halves(qt) if pair else [qt[:LANES], qt[LANES:]]

        def step(j, carry, diagonal):
            rows = pl.ds(pl.multiple_of(j * blk, blk), blk)
            kk = k_ref[rows, :]
            ks = [kk, kk] if pair else _groups(kk)
            vt = wide(vt_ref, sub * j)
            out = []
            for n in range(2):
                m, l, acc = carry[n]
                s = jnp.dot(ks[n], qts[n], preferred_element_type=F32)
                if has_bias:
                    s = s - LOG2E * _pick_lane(f_ref[rows, :], 2 * g + n)
                if diagonal:
                    s = _above_diagonal(s)
                m_new = jnp.maximum(m, jnp.max(s, axis=0, keepdims=True))
                alpha = jnp.exp2(m - m_new)
                p = jnp.exp2(s - m_new)
                out.append((m_new, alpha * l + jnp.sum(p, axis=0, keepdims=True),
                            alpha * acc + jnp.dot(vt[n * HEAD_DIM:(n + 1) * HEAD_DIM], p.astype(BF16),
                                                  preferred_element_type=F32)))
            return tuple(out)

        init = tuple((jnp.full((1, blk), -jnp.inf, F32), jnp.zeros((1, blk), F32), jnp.zeros((HEAD_DIM, blk), F32))
                     for _ in range(2))
        carry = lax.fori_loop(0, i, lambda j, c: step(j, c, False), init)
        (ma, la, acca), (mb, lb, accb) = step(i, carry, True)
        o_ref[...] = jnp.concatenate([acca / la, accb / lb], axis=0).T
        row = lax.broadcasted_iota(jnp.int32, (LANES, 1), 0)
        st_ref[0] = jnp.where(row == 0, ma + jnp.log2(la), jnp.where(row == 1, mb + jnp.log2(lb), 0.0)).T
        if side:
            @pl.when(step_id == PAIRS * nb - 1)
            def _():
                side.last(si, so, *sems)

    in_specs = [pl.BlockSpec((sub, w, tblk), lambda g, i: (i, qoff + g, 0)), pl.BlockSpec((t, w), lambda g, i: (0, koff + g)),
                pl.BlockSpec((t // tblk, LANES, tblk), lambda g, i: (0, voff + g, 0))]
    if has_bias:
        in_specs.append(pl.BlockSpec((t, LANES), lambda g, i: (0, 0)))
    s_in_specs, s_out_specs = side.specs() if side else ([], [])
    return pl.pallas_call(
        body, name=name, grid=(PAIRS, nb), in_specs=in_specs + s_in_specs,
        out_specs=[pl.BlockSpec((blk, LANES), lambda g, i: (i, g)), pl.BlockSpec((1, blk, LANES), lambda g, i: (g, i, 0))]
        + s_out_specs,
        out_shape=[jax.ShapeDtypeStruct((t, PAIRS * LANES), F32), jax.ShapeDtypeStruct((PAIRS, t, LANES), F32)] + list(s_outs),
        scratch_shapes=side.sems() if side else [],
        compiler_params=_params(("arbitrary", "arbitrary")),
    )(*ins, *s_ins)


def attn_stats(o, do, st, *, name, bt=512):
    t = o.shape[0]
    bt = min(bt, t)

    def body(o_ref, do_ref, st_ref, out_ref):
        prod = o_ref[...] * do_ref[...].astype(F32)
        for g in range(PAIRS):
            grp = prod[:, g * LANES:(g + 1) * LANES]
            da = jnp.sum(jnp.where(_lane() < HEAD_DIM, grp, 0.0), axis=1, keepdims=True)
            db = jnp.sum(jnp.where(_lane() >= HEAD_DIM, grp, 0.0), axis=1, keepdims=True)
            out_ref[g] = jnp.where(_lane() == 2, da, jnp.where(_lane() == 3, db, st_ref[g]))

    row = pl.BlockSpec((bt, PAIRS * LANES), lambda i: (i, 0))
    stat = pl.BlockSpec((PAIRS, bt, LANES), lambda i: (0, i, 0))
    return pl.pallas_call(
        body, name=name, grid=(t // bt,), in_specs=[row, row, stat], out_specs=stat,
        out_shape=jax.ShapeDtypeStruct(st.shape, F32),
        compiler_params=_params(("parallel",)),
    )(o, do, st)


def flash_bwd(q_arr, qt_arr, k_arr, v_arr, do_arr, dot_arr, st, f_blocks, *, qoff, koff, voff, pair, scale, name, blk=512,
              side=None):
    t = q_arr.shape[0]
    blk = min(blk, t)
    nb = t // blk
    w = LANES if pair else 2 * LANES
    hw = w // 2
    has_bias = f_blocks is not None
    split = _halves if pair else _groups
    ins = [q_arr, qt_arr, k_arr, v_arr, do_arr, dot_arr, st] + ([f_blocks] if has_bias else [])
    n_out = 5 if has_bias else 3
    s_ins, s_outs = (side.ins, side.out_shapes) if side else ([], [])

    def body(*refs):
        main, si, outs, so, sems = _split_refs(refs, [len(ins), len(s_ins), n_out, len(s_outs), 2 if side else 0])
        q_ref, qt_ref, k_ref, v_ref, do_ref, dot_ref, st_ref = main[:7]
        dq_ref, dk_ref, dv_ref = outs[:3]
        g, j = pl.program_id(0), pl.program_id(1)
        step_id = g * nb + j
        if side:
            @pl.when(step_id == 0)
            def _():
                side.first(si, so, *sems)

        kk, vv = k_ref[...], v_ref[...]
        ks = [kk, kk] if pair else _groups(kk)
        if has_bias:
            f_ref, dfk_ref, dfq_ref = main[7], outs[3], outs[4]
            fk = [LOG2E * _pick_row(f_ref[0], 2 * g + n) for n in range(2)]

        def step(i, carry, diagonal):
            rows = pl.ds(pl.multiple_of(i * blk, blk), blk)
            qs = split((q_ref[rows, :].astype(F32) * (scale * LOG2E)).astype(BF16))
            qt = (qt_ref[i].astype(F32) * (scale * LOG2E)).astype(BF16)
            dos = [h.astype(BF16) for h in _halves(do_ref[rows, :].astype(F32))]
            dot = dot_ref[i]
            stats = st_ref[0, rows, :]
            new, dqs, row_sums = [], [], []
            for n in range(2):
                dkt, dvt, dfk = carry[n]
                s = lax.dot_general(qs[n], ks[n], NT, preferred_element_type=F32)
                if has_bias:
                    s = s - fk[n]
                if diagonal:
                    s = _below_diagonal(s)
                p = jnp.exp2(s - stats[:, n:n + 1])
                dp = lax.dot_general(dos[n], vv, NT, preferred_element_type=F32)
                ds = p * (dp - stats[:, 2 + n:3 + n])
                dsb = ds.astype(BF16)
                dvt = dvt + jnp.dot(dot[n * HEAD_DIM:(n + 1) * HEAD_DIM], p.astype(BF16), preferred_element_type=F32)
                dkt = dkt + jnp.dot(qt[n * hw:(n + 1) * hw], dsb, preferred_element_type=F32)
                dqs.append(jnp.dot(dsb, ks[n], preferred_element_type=F32))
                if has_bias:
                    dfk = dfk - jnp.sum(ds, axis=0, keepdims=True)
                    row_sums.append(jnp.sum(ds, axis=1, keepdims=True))
                new.append((dkt, dvt, dfk))
            dq = (jnp.where(_lane() < HEAD_DIM, dqs[0], dqs[1]) if pair else jnp.concatenate(dqs, axis=1)) * scale
            rs = _lanes01(row_sums[0], row_sums[1], 0.0) if has_bias else None

            @pl.when(j == 0)
            def _():
                dq_ref[rows, :] = dq
                if has_bias:
                    dfq_ref[0, rows, :] = rs

            @pl.when(j > 0)
            def _():
                dq_ref[rows, :] += dq
                if has_bias:
                    dfq_ref[0, rows, :] += rs

            return tuple(new)

        init = tuple((jnp.zeros((hw, blk), F32), jnp.zeros((HEAD_DIM, blk), F32), jnp.zeros((1, blk), F32)) for _ in range(2))
        carry = step(j, init, True)
        (dka, dva, dfa), (dkb, dvb, dfb) = lax.fori_loop(j + 1, nb, lambda i, c: step(i, c, False), carry)
        dk_ref[...] = jnp.concatenate([dka, dkb], axis=0).T * LN2
        dv_ref[...] = jnp.concatenate([dva, dvb], axis=0).T
        if has_bias:
            row = lax.broadcasted_iota(jnp.int32, (N_HEADS, 1), 0)
            dfk_ref[0] = jnp.where(row == 0, dfa, jnp.where(row == 1, dfb, 0.0))
        if side:
            @pl.when(step_id == PAIRS * nb - 1)
            def _():
                side.last(si, so, *sems)

    in_specs = [pl.BlockSpec((t, w), lambda g, j: (0, qoff + g)), pl.BlockSpec((nb, w, blk), lambda g, j: (0, qoff + g, 0)),
                pl.BlockSpec((blk, w), lambda g, j: (j, koff + g)), pl.BlockSpec((blk, LANES), lambda g, j: (j, voff + g)),
                pl.BlockSpec((t, LANES), lambda g, j: (0, g)), pl.BlockSpec((nb, LANES, blk), lambda g, j: (0, g, 0)),
                pl.BlockSpec((1, t, LANES), lambda g, j: (g, 0, 0))]
    out_specs = [pl.BlockSpec((t, w), lambda g, j: (0, g)), pl.BlockSpec((blk, w), lambda g, j: (j, g)),
                 pl.BlockSpec((blk, LANES), lambda g, j: (j, g))]
    out_shape = [jax.ShapeDtypeStruct((t, PAIRS * w), F32)] * 2 + [jax.ShapeDtypeStruct((t, PAIRS * LANES), F32)]
    if has_bias:
        in_specs.append(pl.BlockSpec((1, N_HEADS, blk), lambda g, j: (j, 0, 0)))
        out_specs += [pl.BlockSpec((1, N_HEADS, blk), lambda g, j: (g, 0, j)), pl.BlockSpec((1, t, LANES), lambda g, j: (g, 0, 0))]
        out_shape += [jax.ShapeDtypeStruct((PAIRS, N_HEADS, t), F32), jax.ShapeDtypeStruct((PAIRS, t, LANES), F32)]
    s_in_specs, s_out_specs = side.specs() if side else ([], [])
    return pl.pallas_call(
        body, name=name, grid=(PAIRS, nb), in_specs=in_specs + s_in_specs, out_specs=out_specs + s_out_specs,
        out_shape=out_shape + list(s_outs), scratch_shapes=side.sems() if side else [],
        compiler_params=_params(("arbitrary", "arbitrary")),
    )(*ins, *s_ins)


def loss_head(x2, g, tgt, *, name, bt=512):
    t, d = x2.shape
    bt = min(bt, t)

    def body(x_ref, g_ref, t_ref, loss_ref, dx_ref, dxb_ref, dg_ref):
        @pl.when(pl.program_id(0) == 0)
        def _():
            loss_ref[...] = jnp.zeros_like(loss_ref)
            dg_ref[...] = jnp.zeros_like(dg_ref)

        xv = x_ref[...]
        gv = g_ref[...]
        r = lax.rsqrt(jnp.mean(xv * xv, axis=-1, keepdims=True) + EPS)
        xh = xv * r
        e = xh * gv - t_ref[...]
        loss_ref[...] += 0.5 * jnp.sum(jnp.mean(e * e, axis=-1, keepdims=True), axis=0, keepdims=True)
        dy = e * (1.0 / d)
        dg_ref[...] += jnp.sum(dy * xh, axis=0, keepdims=True)
        u = dy * gv
        dx = r * u - xv * (r * r * r * jnp.mean(u * xv, axis=-1, keepdims=True))
        dx_ref[...] = dx
        dxb_ref[...] = dx.astype(BF16)

    row = pl.BlockSpec((bt, d), lambda i: (i, 0))
    vec = pl.BlockSpec((1, d), lambda i: (0, 0))
    return pl.pallas_call(
        body, name=name, grid=(t // bt,), in_specs=[row, vec, row],
        out_specs=[pl.BlockSpec((1, 1), lambda i: (0, 0)), row, row, vec],
        out_shape=[jax.ShapeDtypeStruct((1, 1), F32), jax.ShapeDtypeStruct((t, d), F32),
                   jax.ShapeDtypeStruct((t, d), BF16), jax.ShapeDtypeStruct((1, d), F32)],
        compiler_params=_params(("arbitrary",)),
    )(x2, g, tgt)


def _adamw_math(w, g, m, v):
    nm = ADAM_B1 * m + (1.0 - ADAM_B1) * g
    nv = ADAM_B2 * v + (1.0 - ADAM_B2) * (g * g)
    m_hat = nm / (1.0 - ADAM_B1 ** ADAM_STEP)
    v_hat = nv / (1.0 - ADAM_B2 ** ADAM_STEP)
    return -ADAM_LR * (m_hat / (jnp.sqrt(v_hat) + ADAM_EPS) + ADAM_WD * w), nm, nv


def adamw(w, g, m, v, *, name):
    rws, cols = w.shape
    br = _row_block(rws)

    def body(w_ref, g_ref, m_ref, v_ref, d_ref, nm_ref, nv_ref):
        d_ref[...], nm_ref[...], nv_ref[...] = _adamw_math(w_ref[...], g_ref[...], m_ref[...], v_ref[...])

    blk = pl.BlockSpec((br, cols), lambda i: (i, 0))
    return pl.pallas_call(
        body, name=name, grid=(rws // br,), in_specs=[blk] * 4, out_specs=[blk] * 3,
        out_shape=[jax.ShapeDtypeStruct((rws, cols), F32)] * 3,
        compiler_params=_params(("parallel",)),
    )(w, g, m, v)


def adamw_halves(w, g_mine, g_other, m, v, core, *, name):
    _, k, n = w.shape
    br = _row_block(k // 2)
    nh = k // 2 // br

    def body(c_ref, w_ref, gm_ref, go_ref, m_ref, v_ref, g_out, d_ref, nm_ref, nv_ref):
        gv = jnp.where(pl.program_id(0) == c_ref[0], gm_ref[...], go_ref[...])
        g_out[0] = gv
        d_ref[0], nm_ref[0], nv_ref[0] = _adamw_math(w_ref[0], gv, m_ref[0], v_ref[0])

    full = pl.BlockSpec((1, br, n), lambda hb, i, c: (0, hb * nh + i, 0))
    half = pl.BlockSpec((br, n), lambda hb, i, c: (i, 0))
    return pl.pallas_call(
        body, name=name,
        grid_spec=pltpu.PrefetchScalarGridSpec(num_scalar_prefetch=1, grid=(2, nh), in_specs=[full, half, half, full, full],
                                               out_specs=[full] * 4),
        out_shape=[jax.ShapeDtypeStruct(w.shape, F32)] * 4,
        compiler_params=_params(("parallel", "parallel")),
    )(core, w, g_mine, g_other, m, v)


def add_pair(dw, recv, core, *, name):
    n4, k, n = dw.shape
    hk = k // 2

    def body(c_ref, a_ref, b_ref, o_ref):
        o_ref[...] = (a_ref[...] + b_ref[...].astype(F32)).astype(BF16)

    return pl.pallas_call(
        body, name=name,
        grid_spec=pltpu.PrefetchScalarGridSpec(
            num_scalar_prefetch=1, grid=(n4,),
            in_specs=[pl.BlockSpec((1, hk, n), lambda q, c: (q, c[0], 0)), pl.BlockSpec((1, hk, n), lambda q, c: (q, 0, 0))],
            out_specs=pl.BlockSpec((1, hk, n), lambda q, c: (q, 0, 0))),
        out_shape=jax.ShapeDtypeStruct((n4, hk, n), BF16),
        compiler_params=_params(("parallel",)),
    )(core, dw, recv)


def sum_chips(parts, *, name):
    n4, r, n = parts.shape
    br = _row_block(r)

    def body(p_ref, o_ref):
        acc = p_ref[0].astype(F32)
        for q in range(1, n4):
            acc = acc + p_ref[q].astype(F32)
        o_ref[...] = acc

    return pl.pallas_call(
        body, name=name, grid=(r // br,),
        in_specs=[pl.BlockSpec((n4, br, n), lambda i: (0, i, 0))], out_specs=pl.BlockSpec((br, n), lambda i: (i, 0)),
        out_shape=jax.ShapeDtypeStruct((r, n), F32),
        compiler_params=_params(("parallel",)),
    )(parts)


ANY = pl.BlockSpec(memory_space=pl.ANY)


def _place():
    x, y, c = lax.axis_index("x"), lax.axis_index("y"), lax.axis_index("c")
    chips = [(1 - x, y), (x, 1 - y), (1 - x, 1 - y)]
    return x, y, c, chips


def _copy(src, dst, send_sems, recv_sems, k, to):
    return pltpu.make_async_remote_copy(src_ref=src, dst_ref=dst, send_sem=send_sems.at[k], recv_sem=recv_sems.at[k],
                                        device_id=to, device_id_type=MESH)


def _half_rows(ref, lead, hf):
    hk = ref.shape[1] // 2
    return ref.at[lead, pl.ds(hf * hk, hk), :]


def _gather_first(srcs, dsts, ssems, rsems):
    x, y, c, chips = _place()
    for ti, (s, d) in enumerate(zip(srcs, dsts)):
        hk = s.shape[0] // 2
        for j, (cx, cy) in enumerate(chips):
            _copy(s.at[pl.ds(c * hk, hk), :], _half_rows(d, 2 * x + y, c), ssems, rsems, 3 * ti + j, (cx, cy, c)).start()


def _gather_mid(srcs, dsts, ssems, rsems):
    x, y, c, chips = _place()
    n1 = 3 * len(srcs)
    for ti, d in enumerate(dsts):
        for j, (cx, cy) in enumerate(chips):
            landed = _half_rows(d, 2 * cx + cy, c)
            _copy(landed, landed, ssems, rsems, 3 * ti + j, (cx, cy, c)).wait_recv()
            _copy(landed, landed, ssems, rsems, n1 + 3 * ti + j, (x, y, 1 - c)).start()


def _gather_last(srcs, dsts, ssems, rsems):
    x, y, c, chips = _place()
    n1 = 3 * len(srcs)
    for ti, (s, d) in enumerate(zip(srcs, dsts)):
        hk = s.shape[0] // 2
        for j, (cx, cy) in enumerate(chips):
            other = _half_rows(d, 2 * cx + cy, 1 - c)
            _copy(other, other, ssems, rsems, n1 + 3 * ti + j, (x, y, 1 - c)).wait_recv()
        for j, (cx, cy) in enumerate(chips):
            mine = s.at[pl.ds(c * hk, hk), :]
            _copy(mine, mine, ssems, rsems, 3 * ti + j, (cx, cy, c)).wait_send()
            _copy(mine, mine, ssems, rsems, n1 + 3 * ti + j, (x, y, 1 - c)).wait_send()


def gather_side(shards):
    return Side(shards, [jax.ShapeDtypeStruct((N_CHIPS,) + s.shape, s.dtype) for s in shards], 6 * len(shards),
                _gather_first, _gather_last, _gather_mid)


def _scatter_first(srcs, dsts, ssems, rsems):
    x, y, c, chips = _place()
    for ti, (s, d) in enumerate(zip(srcs, dsts)):
        for j, (cx, cy) in enumerate(chips):
            _copy(s.at[2 * cx + cy], d.at[2 * x + y], ssems, rsems, 3 * ti + j, (cx, cy, c)).start()


def _scatter_last(srcs, dsts, ssems, rsems):
    x, y, c, chips = _place()
    for ti, (s, d) in enumerate(zip(srcs, dsts)):
        for j, (cx, cy) in enumerate(chips):
            _copy(s.at[2 * cx + cy], d.at[2 * cx + cy], ssems, rsems, 3 * ti + j, (cx, cy, c)).wait_recv()
        for j, (cx, cy) in enumerate(chips):
            _copy(s.at[2 * cx + cy], d.at[2 * cx + cy], ssems, rsems, 3 * ti + j, (cx, cy, c)).wait_send()


def scatter_side(parts):
    return Side(parts, [jax.ShapeDtypeStruct(p.shape, p.dtype) for p in parts], 3 * len(parts), _scatter_first, _scatter_last)


def run_side(side, *, name):
    n_in, n_out = len(side.ins), len(side.out_shapes)

    def body(*refs):
        si, so, sems = _split_refs(refs, [n_in, n_out, 2])
        side.first(si, so, *sems)
        if side.mid is not None:
            side.mid(si, so, *sems)
        side.last(si, so, *sems)

    in_specs, out_specs = side.specs()
    return pl.pallas_call(body, name=name, in_specs=in_specs, out_specs=out_specs, out_shape=side.out_shapes,
                          scratch_shapes=side.sems())(*side.ins)


def swap_sibling(xs, *, name):
    n = len(xs)

    def body(*refs):
        srcs, dsts, sems = _split_refs(refs, [n, n, 2])
        x, y, c, _ = _place()
        copies = [_copy(s, d, *sems, k, (x, y, 1 - c)) for k, (s, d) in enumerate(zip(srcs, dsts))]
        for cp in copies:
            cp.start()
        for cp in copies:
            cp.wait()

    return pl.pallas_call(
        body, name=name, in_specs=[ANY] * n, out_specs=[ANY] * n,
        out_shape=[jax.ShapeDtypeStruct(a.shape, a.dtype) for a in xs],
        scratch_shapes=[pltpu.SemaphoreType.DMA((n,)), pltpu.SemaphoreType.DMA((n,))],
    )(*xs)


def allreduce_small(s):
    n_dev = 8

    def body(s_ref, out_ref, buf, send_sems, recv_sems):
        x, y, c, _ = _place()
        me = 4 * x + 2 * y + c
        buf[me] = s_ref[...]
        sends = []
        for k in range(1, n_dev):
            px = 1 - x if k & 4 else x
            py = 1 - y if k & 2 else y
            pc = 1 - c if k & 1 else c
            cp = _copy(s_ref, buf.at[me], send_sems, recv_sems, k - 1, (px, py, pc))
            cp.start()
            sends.append((cp, 4 * px + 2 * py + pc))
        for k, (cp, peer) in enumerate(sends):
            _copy(s_ref, buf.at[peer], send_sems, recv_sems, k, (x, y, c)).wait_recv()
        for cp, _ in sends:
            cp.wait_send()
        acc = buf[0]
        for d in range(1, n_dev):
            acc = acc + buf[d]
        out_ref[...] = acc

    vm = pl.BlockSpec(memory_space=pltpu.VMEM)
    return pl.pallas_call(
        body, name="allreduce_small", in_specs=[vm], out_specs=vm,
        out_shape=jax.ShapeDtypeStruct(s.shape, F32),
        scratch_shapes=[pltpu.VMEM((n_dev,) + s.shape, F32), pltpu.SemaphoreType.DMA((n_dev - 1,)),
                        pltpu.SemaphoreType.DMA((n_dev - 1,))],
    )(s)


def join_cols(sm):
    n4, k, n = sm.shape
    return sm.transpose(1, 0, 2).reshape(k, n4 * n)


def split_cols(full):
    k, n = full.shape
    return full.reshape(k, N_CHIPS, n // N_CHIPS).transpose(1, 0, 2)


def _pad_heads(w, width):
    lead = w.shape[:-1]
    w = w.reshape(lead + (N_HEADS, width))
    return jnp.pad(w, [(0, 0)] * len(lead) + [(0, 0), (0, LANES - width)]).reshape(lead + (HW,))


def _unpad_heads(w, width):
    lead = w.shape[:-1]
    return w.reshape(lead + (N_HEADS, LANES))[..., :width].reshape(lead + (N_HEADS * width,))


def split_w_in(w_in):
    d = w_in.shape[0]
    o_f = 3 * FW
    o_cq = o_f + N_HEADS
    o_ckv = o_cq + Q_RANK
    o_kr = o_ckv + KV_RANK

    def z(n):
        return jnp.zeros((d, n), w_in.dtype)

    small = jnp.concatenate([w_in[:, o_cq:o_ckv], w_in[:, o_ckv:o_kr], z(HEAD_DIM), w_in[:, o_kr:], z(LANES - HEAD_DIM - ROPE_DIM),
                             w_in[:, o_f:o_cq], z(LANES - N_HEADS), z(LANES)], axis=1)
    return w_in[:, :o_f], small


def join_w_in(d_qkv, d_small):
    kr = S_KR + HEAD_DIM
    return jnp.concatenate([d_qkv, d_small[:, S_F:S_F + N_HEADS], d_small[:, S_CQ:S_CKV], d_small[:, S_CKV:S_KR],
                            d_small[:, kr:kr + ROPE_DIM]], axis=1)


def rope_tables(pos):
    t = pos.shape[0]
    inv_freq = ROPE_THETA ** (-jnp.arange(0, ROPE_DIM, 2, dtype=F32) / ROPE_DIM)
    ang = pos.astype(F32)[:, None] * inv_freq
    cos, sin = jnp.cos(ang), jnp.sin(ang)
    half = ROPE_DIM // 2

    def z(n):
        return jnp.zeros((t, n), F32)

    tab_c = jnp.concatenate([jnp.ones((t, HEAD_DIM), F32), cos, cos, z(LANES - HEAD_DIM - ROPE_DIM)], axis=1)
    tab_a = jnp.concatenate([z(HEAD_DIM), -sin, z(half), z(LANES - HEAD_DIM - ROPE_DIM)], axis=1)
    tab_b = jnp.concatenate([z(HEAD_DIM), z(half), sin, z(LANES - HEAD_DIM - ROPE_DIM)], axis=1)
    return tab_c, tab_a, tab_b


def _pad_lanes(v, n):
    return jnp.pad(v, ((0, 0), (0, n - v.shape[1])))


ATTN_BLK = 512
ATTN_FWD_BLK = 1024


def local_step(xs, pos, tgt, gains, w_early, late_weights, fwd_side=None, bwd_side=None):
    g_attn, b_forget, g_q, g_kv, g_fo, g_mo, g_mlp, g_fin = gains
    w_in, w_uq, w_ukv = w_early
    t = xs.shape[0]
    blk = min(ATTN_BLK, t)
    fox_scale = 1.0 / (HEAD_DIM ** 0.5)
    mla_scale = 1.0 / ((HEAD_DIM + ROPE_DIM) ** 0.5)

    w_qkv, w_small = split_w_in(w_in)
    w_uq_p = _pad_heads(w_uq, HEAD_DIM + ROPE_DIM)
    kv = w_ukv.reshape(KV_RANK, N_HEADS, 2 * HEAD_DIM)
    w_ukv_p = jnp.concatenate([_pad_heads(kv[:, :, :HEAD_DIM].reshape(KV_RANK, FW), HEAD_DIM),
                               kv[:, :, HEAD_DIM:].reshape(KV_RANK, FW)], axis=1)
    b_f = _pad_lanes(b_forget, LANES)
    tab_c, tab_a, tab_b = rope_tables(pos)

    h1 = rmsnorm(xs, g_attn, out_dtype=BF16, name="norm_attn")
    qkv, = mm(h1, w_qkv, out_dtypes=[BF16], name="proj_qkv")
    small, = mm(h1, w_small, out_dtypes=[F32], name="proj_small")
    mq, mk, mv, lf, cqn, ckvn = mla_prep(small, g_q, g_kv, w_uq_p, w_ukv_p, tab_c, tab_a, tab_b, b_f, name="mla_prep")
    f_cum = cumsum_rows(lf, reverse=False, name="gate_cumsum")
    f_blocks = f_cum[:, :N_HEADS].reshape(t // blk, blk, N_HEADS).transpose(0, 2, 1)
    qkv_t, mq_t, mv_t = to_blocks_t(qkv, blk), to_blocks_t(mq, blk), to_blocks_t(mv, blk)
    fo, st_f, *gathered = flash_fwd(qkv_t, qkv, qkv_t, f_cum, qoff=0, koff=PAIRS, voff=2 * PAIRS, pair=True,
                                    scale=fox_scale, name="fox_fwd", blk=ATTN_FWD_BLK, side=fwd_side)
    w_o, w_up, w_down = late_weights(gathered)
    w_of, w_om = w_o[:FW], w_o[FW:]
    mo, st_m = flash_fwd(mq_t, mk, mv_t, None, qoff=0, koff=0, voff=0, pair=False, scale=mla_scale, name="mla_fwd",
                         blk=ATTN_FWD_BLK)
    mix_f = rmsnorm(fo, g_fo, out_dtype=BF16, name="norm_fox_out")
    mix_m = rmsnorm(mo, g_mo, out_dtype=BF16, name="norm_mla_out")

    def add_res(acc, res):
        return (acc + res,)

    x1a, = mm(mix_f, w_of, extras=[xs], epilogue=add_res, out_dtypes=[F32], name="out_proj_fox")
    x1, = mm(mix_m, w_om, extras=[x1a], epilogue=add_res, out_dtypes=[F32], name="out_proj_mla")
    h2 = rmsnorm(x1, g_mlp, out_dtype=BF16, name="norm_mlp")

    def relu2(acc):
        r = jnp.maximum(acc, 0.0)
        return acc, r * r

    u, act = mm(h2, w_up, epilogue=relu2, out_dtypes=[BF16, BF16], name="mlp_up")
    x2, = mm(act, w_down, extras=[x1], epilogue=add_res, out_dtypes=[F32], name="mlp_down")
    loss, dx2, dx2b, dg_fin = loss_head(x2, g_fin, tgt, name="loss_head")

    def relu2_grad(acc, uu):
        return (acc * (2.0 * jnp.maximum(uu.astype(F32), 0.0)),)

    du, = mm(dx2b, w_down, trans_b=True, extras=[u], epilogue=relu2_grad, out_dtypes=[BF16], name="mlp_down_bwd")
    dw_down = mm_tn(act, dx2b, name="dw_down").reshape(N_CHIPS, -1, w_down.shape[1])
    dh2, = mm(du, w_up, trans_b=True, out_dtypes=[F32], name="mlp_up_bwd")
    dw_up = mm_tn(h2, du, name="dw_up", col_shards=N_CHIPS)
    dx1, dx1b, dg_mlp = rmsnorm_bwd(dh2, x1, g_mlp, dx2, out_dtypes=[F32, BF16], name="norm_mlp_bwd")

    dmix_f, = mm(dx1b, w_of, trans_b=True, out_dtypes=[F32], name="out_proj_fox_bwd")
    dmix_m, = mm(dx1b, w_om, trans_b=True, out_dtypes=[F32], name="out_proj_mla_bwd")
    dw_o = jnp.concatenate([mm_tn(mix_f, dx1b, name="dw_o_fox"), mm_tn(mix_m, dx1b, name="dw_o_mla")], axis=0)
    dw_o = dw_o.reshape(N_CHIPS, -1, w_o.shape[1])
    side = bwd_side(dw_o, dw_up, dw_down) if bwd_side is not None else None
    dfo, dg_fo = rmsnorm_bwd(dmix_f, fo, g_fo, None, out_dtypes=[BF16], name="norm_fox_out_bwd")
    dmo, dg_mo = rmsnorm_bwd(dmix_m, mo, g_mo, None, out_dtypes=[BF16], name="norm_mla_out_bwd")
    st_f = attn_stats(fo, dfo, st_f, name="fox_stats")
    st_m = attn_stats(mo, dmo, st_m, name="mla_stats")
    dfq, dfk, dfv, df_key, df_query, *scattered = flash_bwd(
        qkv, qkv_t, qkv, qkv, dfo, to_blocks_t(dfo, blk), st_f, f_blocks, qoff=0, koff=PAIRS, voff=2 * PAIRS, pair=True,
        scale=fox_scale, name="fox_bwd", blk=blk, side=side)
    dmq, dmk, dmv = flash_bwd(mq, mq_t, mk, mv, dmo, to_blocks_t(dmo, blk), st_m, None, qoff=0, koff=0, voff=0, pair=False,
                              scale=mla_scale, name="mla_bwd", blk=blk)
    dqkv = jnp.concatenate([dfq, dfk, dfv], axis=1).astype(BF16)
    d_f = df_key[:, :2, :].reshape(N_HEADS, t).T + df_query[:, :, :2].transpose(1, 0, 2).reshape(t, N_HEADS)
    dlf = cumsum_rows(_pad_lanes(d_f, LANES), reverse=True, name="gate_cumsum_bwd")
    dsmall, dq_u, dkv_u, dg_q, dg_kv, db_f = mla_prep_bwd(dmq, dmk, dmv, dlf, small, g_q, g_kv, w_uq_p, w_ukv_p,
                                                          tab_c, tab_a, tab_b, b_f, name="mla_prep_bwd")
    dw_uq_p = mm_tn(cqn, dq_u, name="dw_uq")
    dw_ukv_p = mm_tn(ckvn, dkv_u, name="dw_ukv")

    dh1a, = mm(dqkv, w_qkv, trans_b=True, out_dtypes=[F32], name="proj_qkv_bwd")
    dh1, = mm(dsmall, w_small, trans_b=True, extras=[dh1a], epilogue=add_res, out_dtypes=[F32], name="proj_small_bwd")
    dw_qkv = mm_tn(h1, dqkv, name="dw_qkv")
    dw_small = mm_tn(h1, dsmall, name="dw_small")
    grad_x, dg_attn = rmsnorm_bwd(dh1, xs, g_attn, dx1, out_dtypes=[F32], name="norm_attn_bwd")

    dw_in = join_w_in(dw_qkv, dw_small)
    dw_uq = _unpad_heads(dw_uq_p, HEAD_DIM + ROPE_DIM)
    dk_cols = _unpad_heads(dw_ukv_p[:, :HW], HEAD_DIM).reshape(KV_RANK, N_HEADS, HEAD_DIM)
    dv_cols = dw_ukv_p[:, HW:].reshape(KV_RANK, N_HEADS, HEAD_DIM)
    dw_ukv = jnp.concatenate([dk_cols, dv_cols], axis=2).reshape(KV_RANK, N_HEADS * 2 * HEAD_DIM)
    d_gains = (dg_attn, db_f[:, :N_HEADS], dg_q, dg_kv, dg_fo, dg_mo, dg_mlp, dg_fin)
    return loss, grad_x, (dw_in, dw_uq, dw_ukv), (dw_o, dw_up, dw_down), d_gains, scattered


def kernel(x, positions, attn_norm_g, w_in, b_forget, q_norm_g, w_uq, kv_norm_g, w_ukv, fox_out_g, mla_out_g, w_o, mlp_norm_g, w_up, w_down, final_norm_g, loss_target, m_attn_norm_g, m_w_in, m_b_forget, m_q_norm_g, m_w_uq, m_kv_norm_g, m_w_ukv, m_fox_out_g, m_mla_out_g, m_w_o, m_mlp_norm_g, m_w_up, m_w_down, m_final_norm_g, v_attn_norm_g, v_w_in, v_b_forget, v_q_norm_g, v_w_uq, v_kv_norm_g, v_w_ukv, v_fox_out_g, v_mla_out_g, v_w_o, v_mlp_norm_g, v_w_up, v_w_down, v_final_norm_g):
    core_id = lax.axis_index("c")
    core = core_id.reshape(1).astype(jnp.int32)
    chip = 2 * lax.axis_index("x") + lax.axis_index("y")
    big = [w_in, w_uq, w_ukv, w_o, w_up, w_down]
    big_m = [m_w_in, m_w_uq, m_w_ukv, m_w_o, m_w_up, m_w_down]
    big_v = [v_w_in, v_w_uq, v_w_ukv, v_w_o, v_w_up, v_w_down]
    n_early = 3

    def vec(a):
        return a.reshape(1, -1)

    small = [attn_norm_g, b_forget, q_norm_g, kv_norm_g, fox_out_g, mla_out_g, mlp_norm_g, final_norm_g]
    small_m = [m_attn_norm_g, m_b_forget, m_q_norm_g, m_kv_norm_g, m_fox_out_g, m_mla_out_g, m_mlp_norm_g, m_final_norm_g]
    small_v = [v_attn_norm_g, v_b_forget, v_q_norm_g, v_kv_norm_g, v_fox_out_g, v_mla_out_g, v_mlp_norm_g, v_final_norm_g]
    gains = [vec(a) for a in small]

    shards = [w[0].astype(BF16) for w in big]

    def with_own(gathered, mine):
        return [lax.dynamic_update_index_in_dim(g, s, chip, 0) for g, s in zip(gathered, mine)]

    early = with_own(run_side(gather_side(shards[:n_early]), name="gather_early"), shards[:n_early])
    w_early = [join_cols(g) for g in early]

    def late_weights(gathered):
        g_o, g_up, g_down = with_own(gathered, shards[n_early:])
        return g_o.reshape(-1, g_o.shape[2]), join_cols(g_up), g_down.reshape(-1, g_down.shape[2])

    def pair_sums(grads, name):
        sends = [lax.dynamic_slice_in_dim(g, (1 - core_id) * (g.shape[1] // 2), g.shape[1] // 2, axis=1).astype(BF16)
                 for g in grads]
        return sends

    def add_pairs(grads, recvs, name):
        return [add_pair(g, r, core, name="%s_%d" % (name, n)) for n, (g, r) in enumerate(zip(grads, recvs))]

    def chip_sums(scattered, pairs, name):
        with_mine = [lax.dynamic_update_index_in_dim(s, lax.dynamic_index_in_dim(p, chip, 0, keepdims=True), chip, 0)
                     for s, p in zip(scattered, pairs)]
        return [sum_chips(s, name="%s_%d" % (name, n)) for n, s in enumerate(with_mine)]

    late_pairs = []

    def bwd_side(dw_o, dw_up, dw_down):
        grads = [dw_o, dw_up, dw_down]
        recvs = swap_sibling(pair_sums(grads, "late"), name="swap_late")
        late_pairs.extend(add_pairs(grads, recvs, "add_pair_late"))
        return scatter_side(late_pairs)

    loss, grad_x, d_early, _, d_small, scattered_late = local_step(
        x[0], positions[0], loss_target[0], gains, w_early, late_weights, gather_side(shards[n_early:]), bwd_side)

    halves_late = chip_sums(scattered_late, late_pairs, "sum_chips_late")
    g_early = [split_cols(d) for d in d_early]
    n_late = len(halves_late)
    swapped = swap_sibling(halves_late + pair_sums(g_early, "early"), name="swap_mixed")
    others_late, recvs_early = swapped[:n_late], swapped[n_late:]
    early_pairs = add_pairs(g_early, recvs_early, "add_pair_early")
    scattered_early = run_side(scatter_side(early_pairs), name="scatter_early")
    halves_early = chip_sums(scattered_early, early_pairs, "sum_chips_early")
    others_early = swap_sibling(halves_early, name="swap_early")
    halves = halves_early + halves_late
    others = list(others_early) + list(others_late)

    def rows8(vs):
        return jnp.concatenate([_pad_lanes(vec(a).astype(F32), 1024) for a in vs], axis=0)

    g_small8 = allreduce_small(rows8(d_small))

    outs_big = [adamw_halves(w, gm, go, m, v, core, name="adamw_%d" % n)
                for n, (w, gm, go, m, v) in enumerate(zip(big, halves, others, big_m, big_v))]
    d8, m8, v8 = adamw(rows8(small), g_small8, rows8(small_m), rows8(small_v), name="adamw_small")

    def unrows8(a8):
        return [a8[n, :s.size].reshape(s.shape) for n, s in enumerate(small)]

    loss_all = lax.psum(loss[0, 0], ("x", "y", "c"))
    grads, deltas, new_m, new_v = [None] * 14, [None] * 14, [None] * 14, [None] * 14
    big_at = [1, 4, 6, 9, 11, 12]
    small_at = [0, 2, 3, 5, 7, 8, 10, 13]
    for n, at in enumerate(big_at):
        grads[at], deltas[at], new_m[at], new_v[at] = outs_big[n]
    for at, g, dd, mm_, vv in zip(small_at, unrows8(g_small8), unrows8(d8), unrows8(m8), unrows8(v8)):
        grads[at], deltas[at], new_m[at], new_v[at] = g, dd, mm_, vv
    return (loss_all, grad_x[None], *grads, *deltas, *new_m, *new_v)
```

```python
import jax
import jax.numpy as jnp
from jax import lax
from jax.experimental import pallas as pl
from jax.experimental.pallas import tpu as pltpu

F32 = jnp.float32
BF16 = jnp.bfloat16
MESH = pl.DeviceIdType.MESH

EPS = 1e-6
ROPE_THETA = 10000.0
N_HEADS = 8
PAIRS = N_HEADS // 2
HEAD_DIM = 64
ROPE_DIM = 32
LANES = 128
Q_RANK = 384
KV_RANK = 256
N_CHIPS = 4
ADAM_LR, ADAM_B1, ADAM_B2, ADAM_EPS, ADAM_WD, ADAM_STEP = 0.001, 0.9, 0.999, 1e-08, 0.01, 10
VMEM_LIMIT = 48 * 1024 * 1024
LOG2E = 1.4426950408889634
LN2 = 0.6931471805599453
NN = (((1,), (0,)), ((), ()))
NT = (((1,), (1,)), ((), ()))
TN = (((0,), (0,)), ((), ()))


def _params(sem=None):
    return pltpu.CompilerParams(dimension_semantics=sem, vmem_limit_bytes=VMEM_LIMIT)


def _fit(block, dim):
    if dim <= block:
        return dim
    return next(b for b in range(block - block % LANES, 0, -LANES) if dim % b == 0)


def _row_block(rows):
    return next(b for b in (256, 128, 64, 32, 16, 8) if rows % b == 0)


def rmsnorm(x, g, *, out_dtype, name, bt=512):
    t, d = x.shape
    bt = min(bt, t)

    def body(x_ref, g_ref, o_ref):
        xv = x_ref[...].astype(F32)
        r = lax.rsqrt(jnp.mean(xv * xv, axis=-1, keepdims=True) + EPS)
        o_ref[...] = (xv * r * g_ref[...]).astype(o_ref.dtype)

    return pl.pallas_call(
        body, name=name, grid=(t // bt,),
        in_specs=[pl.BlockSpec((bt, d), lambda i: (i, 0)), pl.BlockSpec((1, d), lambda i: (0, 0))],
        out_specs=pl.BlockSpec((bt, d), lambda i: (i, 0)),
        out_shape=jax.ShapeDtypeStruct((t, d), out_dtype),
        compiler_params=_params(("parallel",)),
    )(x, g)


def rmsnorm_bwd(dh, x, g, res, *, out_dtypes, name, bt=512):
    t, d = x.shape
    bt = min(bt, t)
    has_res = res is not None

    def body(*refs):
        dh_ref, x_ref, g_ref = refs[:3]
        res_ref = refs[3] if has_res else None
        outs = refs[3 + has_res:]
        dx_refs, dg_ref = outs[:-1], outs[-1]
        xv = x_ref[...].astype(F32)
        dhv = dh_ref[...].astype(F32)
        r = lax.rsqrt(jnp.mean(xv * xv, axis=-1, keepdims=True) + EPS)
        u = dhv * g_ref[...]
        dot = jnp.mean(u * xv, axis=-1, keepdims=True)
        dx = r * u - xv * (r * r * r * dot)
        if has_res:
            dx = dx + res_ref[...]
        for o in dx_refs:
            o[...] = dx.astype(o.dtype)

        @pl.when(pl.program_id(0) == 0)
        def _():
            dg_ref[...] = jnp.zeros_like(dg_ref)

        dg_ref[...] += jnp.sum(dhv * (xv * r), axis=0, keepdims=True)

    row = pl.BlockSpec((bt, d), lambda i: (i, 0))
    vec = pl.BlockSpec((1, d), lambda i: (0, 0))
    ins = [dh, x, g] + ([res] if has_res else [])
    return pl.pallas_call(
        body, name=name, grid=(t // bt,),
        in_specs=[row, row, vec] + ([row] if has_res else []),
        out_specs=[row] * len(out_dtypes) + [vec],
        out_shape=[jax.ShapeDtypeStruct((t, d), dt) for dt in out_dtypes] + [jax.ShapeDtypeStruct((1, d), F32)],
        compiler_params=_params(("arbitrary",)),
    )(*ins)


def mm(a, b, *, trans_b=False, extras=(), epilogue=None, out_dtypes, name, bm=1024, bn=1024):
    m, k = a.shape
    n = b.shape[0] if trans_b else b.shape[1]
    if k > 2048:
        bm = bm // 2
    bm, bn = _fit(bm, m), _fit(bn, n)
    n_ex = len(extras)

    def body(*refs):
        a_ref, b_ref = refs[0], refs[1]
        ex = refs[2:2 + n_ex]
        outs = refs[2 + n_ex:]
        acc = lax.dot_general(a_ref[...], b_ref[...], NT if trans_b else NN, preferred_element_type=F32)
        res = epilogue(acc, *[e[...] for e in ex]) if epilogue is not None else (acc,)
        for o, r in zip(outs, res):
            o[...] = r.astype(o.dtype)

    tile = pl.BlockSpec((bm, bn), lambda i, j: (i, j))
    b_spec = pl.BlockSpec((bn, k), lambda i, j: (j, 0)) if trans_b else pl.BlockSpec((k, bn), lambda i, j: (0, j))
    return pl.pallas_call(
        body, name=name, grid=(m // bm, n // bn),
        in_specs=[pl.BlockSpec((bm, k), lambda i, j: (i, 0)), b_spec] + [tile] * n_ex,
        out_specs=[tile] * len(out_dtypes),
        out_shape=[jax.ShapeDtypeStruct((m, n), dt) for dt in out_dtypes],
        compiler_params=_params(("parallel", "parallel")),
    )(a, b, *extras)


def mm_tn(a, b, *, name, col_shards=1, bk=1024, bn=1024, bt=1024):
    t, k = a.shape
    n = b.shape[1]
    ns = n // col_shards
    bk, bn, bt = _fit(bk, k), _fit(bn, ns), _fit(bt, t)
    per = ns // bn

    def body(a_ref, b_ref, o_ref):
        @pl.when(pl.program_id(2) == 0)
        def _():
            o_ref[...] = jnp.zeros_like(o_ref)

        o_ref[...] += lax.dot_general(a_ref[...], b_ref[...], TN, preferred_element_type=F32)

    if col_shards == 1:
        out_spec = pl.BlockSpec((bk, bn), lambda i, j, s: (i, j))
        out_shape = jax.ShapeDtypeStruct((k, n), F32)
    else:
        out_spec = pl.BlockSpec((None, bk, bn), lambda i, j, s: (j // per, i, j % per))
        out_shape = jax.ShapeDtypeStruct((col_shards, k, ns), F32)
    return pl.pallas_call(
        body, name=name, grid=(k // bk, n // bn, t // bt),
        in_specs=[pl.BlockSpec((bt, bk), lambda i, j, s: (s, i)), pl.BlockSpec((bt, bn), lambda i, j, s: (s, j))],
        out_specs=out_spec, out_shape=out_shape,
        compiler_params=_params(("parallel", "parallel", "arbitrary")),
    )(a, b)


def _split3(x):
    hi = x.astype(BF16)
    r1 = x - hi.astype(F32)
    mid = r1.astype(BF16)
    lo = (r1 - mid.astype(F32)).astype(BF16)
    return hi, mid, lo


def cumsum_rows(x, *, reverse, name, bc=512):
    t, d = x.shape
    bc = min(bc, t)
    nb = t // bc

    def body(x_ref, o_ref, carry):
        @pl.when(pl.program_id(0) == 0)
        def _():
            carry[...] = jnp.zeros_like(carry)

        r = lax.broadcasted_iota(jnp.int32, (bc, bc), 0)
        c = lax.broadcasted_iota(jnp.int32, (bc, bc), 1)
        tri = jnp.where((r <= c) if reverse else (r >= c), 1.0, 0.0).astype(BF16)
        hi, mid, lo = _split3(x_ref[...])
        s = (lax.dot_general(tri, hi, NN, preferred_element_type=F32)
             + lax.dot_general(tri, mid, NN, preferred_element_type=F32)
             + lax.dot_general(tri, lo, NN, preferred_element_type=F32)) + carry[0:1, :]
        o_ref[...] = s
        carry[0:1, :] = s[0:1, :] if reverse else s[bc - 1:bc, :]

    imap = (lambda i: (nb - 1 - i, 0)) if reverse else (lambda i: (i, 0))
    return pl.pallas_call(
        body, name=name, grid=(nb,),
        in_specs=[pl.BlockSpec((bc, d), imap)], out_specs=pl.BlockSpec((bc, d), imap),
        out_shape=jax.ShapeDtypeStruct((t, d), F32),
        scratch_shapes=[pltpu.VMEM((8, d), F32)],
        compiler_params=_params(("arbitrary",)),
    )(x)


def _rope(x, c, a, b):
    return x * c + pltpu.roll(x, LANES - ROPE_DIM // 2, 1) * a + pltpu.roll(x, ROPE_DIM // 2, 1) * b


def _rope_bwd(d, c, a, b):
    return d * c + pltpu.roll(d * a, ROPE_DIM // 2, 1) + pltpu.roll(d * b, LANES - ROPE_DIM // 2, 1)


S_CQ, S_CKV, S_KR, S_F, S_END = 0, Q_RANK, Q_RANK + KV_RANK, Q_RANK + KV_RANK + LANES, 1024
HW = N_HEADS * LANES
FW = N_HEADS * HEAD_DIM


def mla_prep(small, g_q, g_kv, w_uq, w_ukv, tab_c, tab_a, tab_b, b_f, *, name, bt=512):
    t = small.shape[0]
    bt = min(bt, t)

    def body(s_ref, gq_ref, gkv_ref, wq_ref, wkv_ref, c_ref, a_ref, b_ref, bf_ref,
             mq_ref, mk_ref, mv_ref, lf_ref, cqn_ref, ckvn_ref):
        cq = s_ref[:, S_CQ:S_CKV]
        rq = lax.rsqrt(jnp.mean(cq * cq, axis=-1, keepdims=True) + EPS)
        cqn = (cq * rq * gq_ref[...]).astype(BF16)
        ckv = s_ref[:, S_CKV:S_KR]
        rkv = lax.rsqrt(jnp.mean(ckv * ckv, axis=-1, keepdims=True) + EPS)
        ckvn = (ckv * rkv * gkv_ref[...]).astype(BF16)
        cqn_ref[...] = cqn
        ckvn_ref[...] = ckvn
        tc, ta, tb = c_ref[...], a_ref[...], b_ref[...]
        q = jnp.dot(cqn, wq_ref[...], preferred_element_type=F32)
        kv = jnp.dot(ckvn, wkv_ref[...], preferred_element_type=F32)
        kr = _rope(s_ref[:, S_KR:S_F], tc, ta, tb)
        for h in range(N_HEADS):
            sl = slice(h * LANES, (h + 1) * LANES)
            mq_ref[:, sl] = _rope(q[:, sl], tc, ta, tb).astype(BF16)
            mk_ref[:, sl] = (kv[:, sl] + kr).astype(BF16)
        mv_ref[...] = kv[:, HW:].astype(BF16)
        z = s_ref[:, S_F:S_END - LANES] + bf_ref[...]
        lf_ref[...] = jnp.minimum(z, 0.0) - jnp.log(1.0 + jnp.exp(-jnp.abs(z)))

    def row(w):
        return pl.BlockSpec((bt, w), lambda i: (i, 0))

    def full(arr):
        return pl.BlockSpec(arr.shape, lambda i: (0, 0))

    return pl.pallas_call(
        body, name=name, grid=(t // bt,),
        in_specs=[row(S_END), full(g_q), full(g_kv), full(w_uq), full(w_ukv), row(LANES), row(LANES), row(LANES), full(b_f)],
        out_specs=[row(HW), row(HW), row(FW), row(LANES), row(Q_RANK), row(KV_RANK)],
        out_shape=[jax.ShapeDtypeStruct((t, HW), BF16)] * 2 + [jax.ShapeDtypeStruct((t, FW), BF16), jax.ShapeDtypeStruct((t, LANES), F32),
                   jax.ShapeDtypeStruct((t, Q_RANK), BF16), jax.ShapeDtypeStruct((t, KV_RANK), BF16)],
        compiler_params=_params(("parallel",)),
    )(small, g_q, g_kv, w_uq, w_ukv, tab_c, tab_a, tab_b, b_f)


def mla_prep_bwd(dmq, dmk, dmv, dlf, small, g_q, g_kv, w_uq, w_ukv, tab_c, tab_a, tab_b, b_f, *, name, bt=512):
    t = small.shape[0]
    bt = min(bt, t)

    def body(dmq_ref, dmk_ref, dmv_ref, dlf_ref, s_ref, gq_ref, gkv_ref, wq_ref, wkv_ref, c_ref, a_ref, b_ref, bf_ref,
             ds_ref, dq_ref, dkv_ref, dgq_ref, dgkv_ref, db_ref):
        tc, ta, tb = c_ref[...], a_ref[...], b_ref[...]
        lane = lax.broadcasted_iota(jnp.int32, (1, LANES), 1)
        dkr = jnp.zeros((bt, LANES), F32)
        for h in range(N_HEADS):
            sl = slice(h * LANES, (h + 1) * LANES)
            dq_ref[:, sl] = _rope_bwd(dmq_ref[:, sl], tc, ta, tb).astype(BF16)
            dkr = dkr + dmk_ref[:, sl]
        dkv_ref[:, :HW] = dmk_ref[...].astype(BF16)
        dkv_ref[:, HW:] = dmv_ref[...].astype(BF16)
        in_rope = (lane >= HEAD_DIM) & (lane < HEAD_DIM + ROPE_DIM)
        ds_ref[:, S_KR:S_F] = jnp.where(in_rope, _rope_bwd(dkr, tc, ta, tb), 0.0).astype(BF16)

        def norm_bwd(raw, g_ref, dn, dg_ref):
            r = lax.rsqrt(jnp.mean(raw * raw, axis=-1, keepdims=True) + EPS)
            u = dn * g_ref[...]
            dot = jnp.mean(u * raw, axis=-1, keepdims=True)
            dg_ref[...] += jnp.sum(dn * (raw * r), axis=0, keepdims=True)
            return r * u - raw * (r * r * r * dot)

        @pl.when(pl.program_id(0) == 0)
        def _():
            dgq_ref[...] = jnp.zeros_like(dgq_ref)
            dgkv_ref[...] = jnp.zeros_like(dgkv_ref)
            db_ref[...] = jnp.zeros_like(db_ref)

        dcqn = lax.dot_general(dq_ref[...], wq_ref[...], NT, preferred_element_type=F32)
        ds_ref[:, S_CQ:S_CKV] = norm_bwd(s_ref[:, S_CQ:S_CKV], gq_ref, dcqn, dgq_ref).astype(BF16)
        dckvn = lax.dot_general(dkv_ref[...], wkv_ref[...], NT, preferred_element_type=F32)
        ds_ref[:, S_CKV:S_KR] = norm_bwd(s_ref[:, S_CKV:S_KR], gkv_ref, dckvn, dgkv_ref).astype(BF16)
        z = s_ref[:, S_F:S_END - LANES] + bf_ref[...]
        dz = jnp.where(lane < N_HEADS, dlf_ref[...] / (1.0 + jnp.exp(z)), 0.0)
        db_ref[...] += jnp.sum(dz, axis=0, keepdims=True)
        ds_ref[:, S_F:S_END - LANES] = dz.astype(BF16)
        ds_ref[:, S_END - LANES:] = jnp.zeros((bt, LANES), BF16)

    def row(w):
        return pl.BlockSpec((bt, w), lambda i: (i, 0))

    def full(arr):
        return pl.BlockSpec(arr.shape, lambda i: (0, 0))

    def vec(w):
        return pl.BlockSpec((1, w), lambda i: (0, 0))

    return pl.pallas_call(
        body, name=name, grid=(t // bt,),
        in_specs=[row(HW), row(HW), row(FW), row(LANES), row(S_END), full(g_q), full(g_kv), full(w_uq), full(w_ukv),
                  row(LANES), row(LANES), row(LANES), full(b_f)],
        out_specs=[row(S_END), row(HW), row(HW + FW), vec(Q_RANK), vec(KV_RANK), vec(LANES)],
        out_shape=[jax.ShapeDtypeStruct((t, S_END), BF16), jax.ShapeDtypeStruct((t, HW), BF16),
                   jax.ShapeDtypeStruct((t, HW + FW), BF16), jax.ShapeDtypeStruct((1, Q_RANK), F32),
                   jax.ShapeDtypeStruct((1, KV_RANK), F32), jax.ShapeDtypeStruct((1, LANES), F32)],
        compiler_params=_params(("arbitrary",)),
    )(dmq, dmk, dmv, dlf, small, g_q, g_kv, w_uq, w_ukv, tab_c, tab_a, tab_b, b_f)


class Side:
    def __init__(self, ins, out_shapes, n_sems, first, last, mid=None):
        self.ins, self.out_shapes, self.n_sems = list(ins), list(out_shapes), n_sems
        self.first, self.mid, self.last = first, mid, last

    def specs(self):
        return [ANY] * len(self.ins), [ANY] * len(self.out_shapes)

    def sems(self):
        return [pltpu.SemaphoreType.DMA((self.n_sems,)), pltpu.SemaphoreType.DMA((self.n_sems,))]


def _lane():
    return lax.broadcasted_iota(jnp.int32, (1, LANES), 1)


def _halves(x):
    zero = jnp.zeros_like(x)
    return [jnp.where(_lane() < HEAD_DIM, x, zero), jnp.where(_lane() >= HEAD_DIM, x, zero)]


def _groups(x):
    return [x[:, :LANES], x[:, LANES:]]


def _lanes01(a, b, rest):
    return jnp.where(_lane() == 0, a, jnp.where(_lane() == 1, b, rest))


def _pick_row(tile, h):
    row = lax.broadcasted_iota(jnp.int32, (tile.shape[0], 1), 0)
    return jnp.sum(jnp.where(row == h, tile, 0.0), axis=0, keepdims=True)


def _pick_lane(tile, h):
    return jnp.sum(jnp.where(_lane() == h, tile, 0.0), axis=1, keepdims=True)


def _row_halves(x):
    row = lax.broadcasted_iota(jnp.int32, (LANES, 1), 0)
    zero = jnp.zeros_like(x)
    return [jnp.where(row < HEAD_DIM, x, zero), jnp.where(row >= HEAD_DIM, x, zero)]


def _below_diagonal(s):
    r = lax.broadcasted_iota(jnp.int32, s.shape, 0)
    c = lax.broadcasted_iota(jnp.int32, s.shape, 1)
    return jnp.where(c <= r, s, -jnp.inf)


def _above_diagonal(s):
    r = lax.broadcasted_iota(jnp.int32, s.shape, 0)
    c = lax.broadcasted_iota(jnp.int32, s.shape, 1)
    return jnp.where(r <= c, s, -jnp.inf)


def to_blocks_t(x, blk):
    t, c = x.shape
    return x.reshape(t // blk, blk, c).transpose(0, 2, 1)


def _split_refs(refs, counts):
    out, at = [], 0
    for n in counts:
        out.append(refs[at:at + n])
        at += n
    return out


def flash_fwd(qt_arr, k_arr, vt_arr, f_cum, *, qoff, koff, voff, pair, scale, name, blk=512, side=None):
    t = k_arr.shape[0]
    tblk = qt_arr.shape[2]
    blk = max(min(blk, t), tblk)
    sub = blk // tblk
    nb = t // blk
    w = LANES if pair else 2 * LANES
    has_bias = f_cum is not None
    ins = [qt_arr, k_arr, vt_arr] + ([f_cum] if has_bias else [])

    def wide(ref, first):
        parts = [ref[first + u] for u in range(sub)]
        return parts[0] if sub == 1 else jnp.concatenate(parts, axis=1)
    s_ins, s_outs = (side.ins, side.out_shapes) if side else ([], [])

    def body(*refs):
        main, si, outs, so, sems = _split_refs(refs, [len(ins), len(s_ins), 2, len(s_outs), 2 if side else 0])
        qt_ref, k_ref, vt_ref = main[:3]
        f_ref = main[3] if has_bias else None
        o_ref, st_ref = outs
        g, i = pl.program_id(0), pl.program_id(1)
        step_id = g * nb + i
        if side:
            @pl.when(step_id == 0)
            def _():
                side.first(si, so, *sems)

            if side.mid is not None:
                @pl.when(step_id == (3 * PAIRS * nb) // 4)
                def _():
                    side.mid(si, so, *sems)

        qt = (wide(qt_ref, 0).astype(F32) * (scale * LOG2E)).astype(BF16)
        qts = _row_halves(qt) if pair else [qt[:LANES], qt[LANES:]]

        def step(j, carry, diagonal):
            rows = pl.ds(pl.multiple_of(j * blk, blk), blk)
            kk = k_ref[rows, :]
            ks = [kk, kk] if pair else _groups(kk)
            vt = wide(vt_ref, sub * j)
            out = []
            for n in range(2):
                m, l, acc = carry[n]
                s = jnp.dot(ks[n], qts[n], preferred_element_type=F32)
                if has_bias:
                    s = s - LOG2E * _pick_lane(f_ref[rows, :], 2 * g + n)
                if diagonal:
                    s = _above_diagonal(s)
                m_new = jnp.maximum(m, jnp.max(s, axis=0, keepdims=True))
                alpha = jnp.exp2(m - m_new)
                p = jnp.exp2(s - m_new)
                out.append((m_new, alpha * l + jnp.sum(p, axis=0, keepdims=True),
                            alpha * acc + jnp.dot(vt[n * HEAD_DIM:(n + 1) * HEAD_DIM], p.astype(BF16),
                                                  preferred_element_type=F32)))
            return tuple(out)

        init = tuple((jnp.full((1, blk), -jnp.inf, F32), jnp.zeros((1, blk), F32), jnp.zeros((HEAD_DIM, blk), F32))
                     for _ in range(2))
        carry = lax.fori_loop(0, i, lambda j, c: step(j, c, False), init)
        (ma, la, acca), (mb, lb, accb) = step(i, carry, True)
        o_ref[...] = jnp.concatenate([acca / la, accb / lb], axis=0).T
        row = lax.broadcasted_iota(jnp.int32, (LANES, 1), 0)
        st_ref[0] = jnp.where(row == 0, ma + jnp.log2(la), jnp.where(row == 1, mb + jnp.log2(lb), 0.0)).T
        if side:
            @pl.when(step_id == PAIRS * nb - 1)
            def _():
                side.last(si, so, *sems)

    in_specs = [pl.BlockSpec((sub, w, tblk), lambda g, i: (i, qoff + g, 0)), pl.BlockSpec((t, w), lambda g, i: (0, koff + g)),
                pl.BlockSpec((t // tblk, LANES, tblk), lambda g, i: (0, voff + g, 0))]
    if has_bias:
        in_specs.append(pl.BlockSpec((t, LANES), lambda g, i: (0, 0)))
    s_in_specs, s_out_specs = side.specs() if side else ([], [])
    return pl.pallas_call(
        body, name=name, grid=(PAIRS, nb), in_specs=in_specs + s_in_specs,
        out_specs=[pl.BlockSpec((blk, LANES), lambda g, i: (i, g)), pl.BlockSpec((1, blk, LANES), lambda g, i: (g, i, 0))]
        + s_out_specs,
        out_shape=[jax.ShapeDtypeStruct((t, PAIRS * LANES), F32), jax.ShapeDtypeStruct((PAIRS, t, LANES), F32)] + list(s_outs),
        scratch_shapes=side.sems() if side else [],
        compiler_params=_params(("arbitrary", "arbitrary")),
    )(*ins, *s_ins)


def mix_norm(fo, mo, g_fo, g_mo, *, name, bt=512):
    t, d = fo.shape
    bt = min(bt, t)

    def body(fo_ref, mo_ref, gf_ref, gm_ref, o_ref):
        for n, (x_ref, g_ref) in enumerate(((fo_ref, gf_ref), (mo_ref, gm_ref))):
            xv = x_ref[...]
            r = lax.rsqrt(jnp.mean(xv * xv, axis=-1, keepdims=True) + EPS)
            o_ref[:, n * d:(n + 1) * d] = (xv * r * g_ref[...]).astype(BF16)

    row = pl.BlockSpec((bt, d), lambda i: (i, 0))
    vec = pl.BlockSpec((1, d), lambda i: (0, 0))
    return pl.pallas_call(
        body, name=name, grid=(t // bt,), in_specs=[row, row, vec, vec],
        out_specs=pl.BlockSpec((bt, 2 * d), lambda i: (i, 0)),
        out_shape=jax.ShapeDtypeStruct((t, 2 * d), BF16),
        compiler_params=_params(("parallel",)),
    )(fo, mo, g_fo, g_mo)


def mix_norm_bwd(dmixed, fo, mo, g_fo, g_mo, st_f, st_m, *, name, bt=512):
    t, d = fo.shape
    bt = min(bt, t)

    def body(dm_ref, fo_ref, mo_ref, gf_ref, gm_ref, sf_ref, sm_ref, dfo_ref, dmo_ref, dgf_ref, dgm_ref, sfo_ref, smo_ref):
        @pl.when(pl.program_id(0) == 0)
        def _():
            dgf_ref[...] = jnp.zeros_like(dgf_ref)
            dgm_ref[...] = jnp.zeros_like(dgm_ref)

        groups = ((fo_ref, gf_ref, dfo_ref, dgf_ref, sf_ref, sfo_ref), (mo_ref, gm_ref, dmo_ref, dgm_ref, sm_ref, smo_ref))
        for n, (x_ref, g_ref, dx_ref, dg_ref, st_ref, sto_ref) in enumerate(groups):
            xv = x_ref[...]
            dhv = dm_ref[:, n * d:(n + 1) * d]
            r = lax.rsqrt(jnp.mean(xv * xv, axis=-1, keepdims=True) + EPS)
            u = dhv * g_ref[...]
            dxb = (r * u - xv * (r * r * r * jnp.mean(u * xv, axis=-1, keepdims=True))).astype(BF16)
            dx_ref[...] = dxb
            dg_ref[...] += jnp.sum(dhv * (xv * r), axis=0, keepdims=True)
            prod = xv * dxb.astype(F32)
            for g in range(PAIRS):
                grp = prod[:, g * LANES:(g + 1) * LANES]
                da = jnp.sum(jnp.where(_lane() < HEAD_DIM, grp, 0.0), axis=1, keepdims=True)
                db = jnp.sum(jnp.where(_lane() >= HEAD_DIM, grp, 0.0), axis=1, keepdims=True)
                sto_ref[g] = jnp.where(_lane() == 2, da, jnp.where(_lane() == 3, db, st_ref[g]))

    row = pl.BlockSpec((bt, d), lambda i: (i, 0))
    vec = pl.BlockSpec((1, d), lambda i: (0, 0))
    stat = pl.BlockSpec((PAIRS, bt, LANES), lambda i: (0, i, 0))
    return pl.pallas_call(
        body, name=name, grid=(t // bt,),
        in_specs=[pl.BlockSpec((bt, 2 * d), lambda i: (i, 0)), row, row, vec, vec, stat, stat],
        out_specs=[row, row, vec, vec, stat, stat],
        out_shape=[jax.ShapeDtypeStruct((t, d), BF16)] * 2 + [jax.ShapeDtypeStruct((1, d), F32)] * 2
        + [jax.ShapeDtypeStruct(st_f.shape, F32)] * 2,
        compiler_params=_params(("arbitrary",)),
    )(dmixed, fo, mo, g_fo, g_mo, st_f, st_m)


def flash_bwd(q_arr, qt_arr, k_arr, v_arr, do_arr, dot_arr, st, f_blocks, *, qoff, koff, voff, pair, scale, name, blk=512,
              side=None):
    t = q_arr.shape[0]
    blk = min(blk, t)
    nb = t // blk
    w = LANES if pair else 2 * LANES
    hw = w // 2
    has_bias = f_blocks is not None
    split = _halves if pair else _groups
    ins = [q_arr, qt_arr, k_arr, v_arr, do_arr, dot_arr, st] + ([f_blocks] if has_bias else [])
    n_out = 5 if has_bias else 3
    s_ins, s_outs = (side.ins, side.out_shapes) if side else ([], [])

    def body(*refs):
        main, si, outs, so, sems = _split_refs(refs, [len(ins), len(s_ins), n_out, len(s_outs), 2 if side else 0])
        q_ref, qt_ref, k_ref, v_ref, do_ref, dot_ref, st_ref = main[:7]
        dq_ref, dk_ref, dv_ref = outs[:3]
        g, j = pl.program_id(0), pl.program_id(1)
        step_id = g * nb + j
        if side:
            @pl.when(step_id == 0)
            def _():
                side.first(si, so, *sems)

        kk, vv = k_ref[...], v_ref[...]
        ks = [kk, kk] if pair else _groups(kk)
        if has_bias:
            f_ref, dfk_ref, dfq_ref = main[7], outs[3], outs[4]
            fk = [LOG2E * _pick_row(f_ref[0], 2 * g + n) for n in range(2)]

        def step(i, carry, diagonal):
            rows = pl.ds(pl.multiple_of(i * blk, blk), blk)
            qs = split((q_ref[rows, :].astype(F32) * (scale * LOG2E)).astype(BF16))
            qt = (qt_ref[i].astype(F32) * (scale * LOG2E)).astype(BF16)
            dos = [h.astype(BF16) for h in _halves(do_ref[rows, :].astype(F32))]
            dot = dot_ref[i]
            stats = st_ref[0, rows, :]
            new, dqs, row_sums = [], [], []
            for n in range(2):
                dkt, dvt, dfk = carry[n]
                s = lax.dot_general(qs[n], ks[n], NT, preferred_element_type=F32)
                if has_bias:
                    s = s - fk[n]
                if diagonal:
                    s = _below_diagonal(s)
                p = jnp.exp2(s - stats[:, n:n + 1])
                dp = lax.dot_general(dos[n], vv, NT, preferred_element_type=F32)
                ds = p * (dp - stats[:, 2 + n:3 + n])
                dsb = ds.astype(BF16)
                dvt = dvt + jnp.dot(dot[n * HEAD_DIM:(n + 1) * HEAD_DIM], p.astype(BF16), preferred_element_type=F32)
                dkt = dkt + jnp.dot(qt[n * hw:(n + 1) * hw], dsb, preferred_element_type=F32)
                dqs.append(jnp.dot(dsb, ks[n], preferred_element_type=F32))
                if has_bias:
                    dfk = dfk - jnp.sum(ds, axis=0, keepdims=True)
                    row_sums.append(jnp.sum(ds, axis=1, keepdims=True))
                new.append((dkt, dvt, dfk))
            dq = (jnp.where(_lane() < HEAD_DIM, dqs[0], dqs[1]) if pair else jnp.concatenate(dqs, axis=1)) * scale
            rs = _lanes01(row_sums[0], row_sums[1], 0.0) if has_bias else None

            @pl.when(j == 0)
            def _():
                dq_ref[rows, :] = dq
                if has_bias:
                    dfq_ref[0, rows, :] = rs

            @pl.when(j > 0)
            def _():
                dq_ref[rows, :] += dq
                if has_bias:
                    dfq_ref[0, rows, :] += rs

            return tuple(new)

        init = tuple((jnp.zeros((hw, blk), F32), jnp.zeros((HEAD_DIM, blk), F32), jnp.zeros((1, blk), F32)) for _ in range(2))
        carry = step(j, init, True)
        (dka, dva, dfa), (dkb, dvb, dfb) = lax.fori_loop(j + 1, nb, lambda i, c: step(i, c, False), carry)
        dk_ref[...] = jnp.concatenate([dka, dkb], axis=0).T * LN2
        dv_ref[...] = jnp.concatenate([dva, dvb], axis=0).T
        if has_bias:
            row = lax.broadcasted_iota(jnp.int32, (N_HEADS, 1), 0)
            dfk_ref[0] = jnp.where(row == 0, dfa, jnp.where(row == 1, dfb, 0.0))
        if side:
            @pl.when(step_id == PAIRS * nb - 1)
            def _():
                side.last(si, so, *sems)

    in_specs = [pl.BlockSpec((t, w), lambda g, j: (0, qoff + g)), pl.BlockSpec((nb, w, blk), lambda g, j: (0, qoff + g, 0)),
                pl.BlockSpec((blk, w), lambda g, j: (j, koff + g)), pl.BlockSpec((blk, LANES), lambda g, j: (j, voff + g)),
                pl.BlockSpec((t, LANES), lambda g, j: (0, g)), pl.BlockSpec((nb, LANES, blk), lambda g, j: (0, g, 0)),
                pl.BlockSpec((1, t, LANES), lambda g, j: (g, 0, 0))]
    out_specs = [pl.BlockSpec((t, w), lambda g, j: (0, g)), pl.BlockSpec((blk, w), lambda g, j: (j, g)),
                 pl.BlockSpec((blk, LANES), lambda g, j: (j, g))]
    out_shape = [jax.ShapeDtypeStruct((t, PAIRS * w), F32)] * 2 + [jax.ShapeDtypeStruct((t, PAIRS * LANES), F32)]
    if has_bias:
        in_specs.append(pl.BlockSpec((1, N_HEADS, blk), lambda g, j: (j, 0, 0)))
        out_specs += [pl.BlockSpec((1, N_HEADS, blk), lambda g, j: (g, 0, j)), pl.BlockSpec((1, t, LANES), lambda g, j: (g, 0, 0))]
        out_shape += [jax.ShapeDtypeStruct((PAIRS, N_HEADS, t), F32), jax.ShapeDtypeStruct((PAIRS, t, LANES), F32)]
    s_in_specs, s_out_specs = side.specs() if side else ([], [])
    return pl.pallas_call(
        body, name=name, grid=(PAIRS, nb), in_specs=in_specs + s_in_specs, out_specs=out_specs + s_out_specs,
        out_shape=out_shape + list(s_outs), scratch_shapes=side.sems() if side else [],
        compiler_params=_params(("arbitrary", "arbitrary")),
    )(*ins, *s_ins)


def loss_head(x2, g, tgt, *, name, bt=512):
    t, d = x2.shape
    bt = min(bt, t)

    def body(x_ref, g_ref, t_ref, loss_ref, dx_ref, dxb_ref, dg_ref):
        @pl.when(pl.program_id(0) == 0)
        def _():
            loss_ref[...] = jnp.zeros_like(loss_ref)
            dg_ref[...] = jnp.zeros_like(dg_ref)

        xv = x_ref[...]
        gv = g_ref[...]
        r = lax.rsqrt(jnp.mean(xv * xv, axis=-1, keepdims=True) + EPS)
        xh = xv * r
        e = xh * gv - t_ref[...]
        loss_ref[...] += 0.5 * jnp.sum(jnp.mean(e * e, axis=-1, keepdims=True), axis=0, keepdims=True)
        dy = e * (1.0 / d)
        dg_ref[...] += jnp.sum(dy * xh, axis=0, keepdims=True)
        u = dy * gv
        dx = r * u - xv * (r * r * r * jnp.mean(u * xv, axis=-1, keepdims=True))
        dx_ref[...] = dx
        dxb_ref[...] = dx.astype(BF16)

    row = pl.BlockSpec((bt, d), lambda i: (i, 0))
    vec = pl.BlockSpec((1, d), lambda i: (0, 0))
    return pl.pallas_call(
        body, name=name, grid=(t // bt,), in_specs=[row, vec, row],
        out_specs=[pl.BlockSpec((1, 1), lambda i: (0, 0)), row, row, vec],
        out_shape=[jax.ShapeDtypeStruct((1, 1), F32), jax.ShapeDtypeStruct((t, d), F32),
                   jax.ShapeDtypeStruct((t, d), BF16), jax.ShapeDtypeStruct((1, d), F32)],
        compiler_params=_params(("arbitrary",)),
    )(x2, g, tgt)


def _adamw_math(w, g, m, v):
    nm = ADAM_B1 * m + (1.0 - ADAM_B1) * g
    nv = ADAM_B2 * v + (1.0 - ADAM_B2) * (g * g)
    m_hat = nm / (1.0 - ADAM_B1 ** ADAM_STEP)
    v_hat = nv / (1.0 - ADAM_B2 ** ADAM_STEP)
    return -ADAM_LR * (m_hat / (jnp.sqrt(v_hat) + ADAM_EPS) + ADAM_WD * w), nm, nv


def adamw(w, g, m, v, *, name):
    rws, cols = w.shape
    br = _row_block(rws)

    def body(w_ref, g_ref, m_ref, v_ref, d_ref, nm_ref, nv_ref):
        d_ref[...], nm_ref[...], nv_ref[...] = _adamw_math(w_ref[...], g_ref[...], m_ref[...], v_ref[...])

    blk = pl.BlockSpec((br, cols), lambda i: (i, 0))
    return pl.pallas_call(
        body, name=name, grid=(rws // br,), in_specs=[blk] * 4, out_specs=[blk] * 3,
        out_shape=[jax.ShapeDtypeStruct((rws, cols), F32)] * 3,
        compiler_params=_params(("parallel",)),
    )(w, g, m, v)


def adamw_halves(w, g_mine, g_other, m, v, core, *, name):
    _, k, n = w.shape
    br = _row_block(k // 2)
    nh = k // 2 // br

    def body(c_ref, w_ref, gm_ref, go_ref, m_ref, v_ref, g_out, d_ref, nm_ref, nv_ref):
        gv = jnp.where(pl.program_id(0) == c_ref[0], gm_ref[...], go_ref[...])
        g_out[0] = gv
        d_ref[0], nm_ref[0], nv_ref[0] = _adamw_math(w_ref[0], gv, m_ref[0], v_ref[0])

    full = pl.BlockSpec((1, br, n), lambda hb, i, c: (0, hb * nh + i, 0))
    half = pl.BlockSpec((br, n), lambda hb, i, c: (i, 0))
    return pl.pallas_call(
        body, name=name,
        grid_spec=pltpu.PrefetchScalarGridSpec(num_scalar_prefetch=1, grid=(2, nh), in_specs=[full, half, half, full, full],
                                               out_specs=[full] * 4),
        out_shape=[jax.ShapeDtypeStruct(w.shape, F32)] * 4,
        compiler_params=_params(("parallel", "parallel")),
    )(core, w, g_mine, g_other, m, v)


def add_pair(dw, recv, core, *, name):
    n4, k, n = dw.shape
    hk = k // 2

    def body(c_ref, a_ref, b_ref, o_ref):
        o_ref[...] = (a_ref[...] + b_ref[...].astype(F32)).astype(BF16)

    return pl.pallas_call(
        body, name=name,
        grid_spec=pltpu.PrefetchScalarGridSpec(
            num_scalar_prefetch=1, grid=(n4,),
            in_specs=[pl.BlockSpec((1, hk, n), lambda q, c: (q, c[0], 0)), pl.BlockSpec((1, hk, n), lambda q, c: (q, 0, 0))],
            out_specs=pl.BlockSpec((1, hk, n), lambda q, c: (q, 0, 0))),
        out_shape=jax.ShapeDtypeStruct((n4, hk, n), BF16),
        compiler_params=_params(("parallel",)),
    )(core, dw, recv)


def sum_chips(parts, *, name):
    n4, r, n = parts.shape
    br = _row_block(r)

    def body(p_ref, o_ref):
        acc = p_ref[0].astype(F32)
        for q in range(1, n4):
            acc = acc + p_ref[q].astype(F32)
        o_ref[...] = acc

    return pl.pallas_call(
        body, name=name, grid=(r // br,),
        in_specs=[pl.BlockSpec((n4, br, n), lambda i: (0, i, 0))], out_specs=pl.BlockSpec((br, n), lambda i: (i, 0)),
        out_shape=jax.ShapeDtypeStruct((r, n), F32),
        compiler_params=_params(("parallel",)),
    )(parts)


ANY = pl.BlockSpec(memory_space=pl.ANY)


def _place():
    x, y, c = lax.axis_index("x"), lax.axis_index("y"), lax.axis_index("c")
    chips = [(1 - x, y), (x, 1 - y), (1 - x, 1 - y)]
    return x, y, c, chips


def _copy(src, dst, send_sems, recv_sems, k, to):
    return pltpu.make_async_remote_copy(src_ref=src, dst_ref=dst, send_sem=send_sems.at[k], recv_sem=recv_sems.at[k],
                                        device_id=to, device_id_type=MESH)


def _half_rows(ref, lead, hf):
    hk = ref.shape[1] // 2
    return ref.at[lead, pl.ds(hf * hk, hk), :]


def _gather_first(srcs, dsts, ssems, rsems):
    x, y, c, chips = _place()
    for ti, (s, d) in enumerate(zip(srcs, dsts)):
        hk = s.shape[0] // 2
        for j, (cx, cy) in enumerate(chips):
            _copy(s.at[pl.ds(c * hk, hk), :], _half_rows(d, 2 * x + y, c), ssems, rsems, 3 * ti + j, (cx, cy, c)).start()


def _gather_mid(srcs, dsts, ssems, rsems):
    x, y, c, chips = _place()
    n1 = 3 * len(srcs)
    for ti, d in enumerate(dsts):
        for j, (cx, cy) in enumerate(chips):
            landed = _half_rows(d, 2 * cx + cy, c)
            _copy(landed, landed, ssems, rsems, 3 * ti + j, (cx, cy, c)).wait_recv()
            _copy(landed, landed, ssems, rsems, n1 + 3 * ti + j, (x, y, 1 - c)).start()


def _gather_last(srcs, dsts, ssems, rsems):
    x, y, c, chips = _place()
    n1 = 3 * len(srcs)
    for ti, (s, d) in enumerate(zip(srcs, dsts)):
        hk = s.shape[0] // 2
        for j, (cx, cy) in enumerate(chips):
            other = _half_rows(d, 2 * cx + cy, 1 - c)
            _copy(other, other, ssems, rsems, n1 + 3 * ti + j, (x, y, 1 - c)).wait_recv()
        for j, (cx, cy) in enumerate(chips):
            mine = s.at[pl.ds(c * hk, hk), :]
            _copy(mine, mine, ssems, rsems, 3 * ti + j, (cx, cy, c)).wait_send()
            _copy(mine, mine, ssems, rsems, n1 + 3 * ti + j, (x, y, 1 - c)).wait_send()


def gather_side(shards):
    return Side(shards, [jax.ShapeDtypeStruct((N_CHIPS,) + s.shape, s.dtype) for s in shards], 6 * len(shards),
                _gather_first, _gather_last, _gather_mid)


def _scatter_first(srcs, dsts, ssems, rsems):
    x, y, c, chips = _place()
    for ti, (s, d) in enumerate(zip(srcs, dsts)):
        for j, (cx, cy) in enumerate(chips):
            _copy(s.at[2 * cx + cy], d.at[2 * x + y], ssems, rsems, 3 * ti + j, (cx, cy, c)).start()


def _scatter_last(srcs, dsts, ssems, rsems):
    x, y, c, chips = _place()
    for ti, (s, d) in enumerate(zip(srcs, dsts)):
        for j, (cx, cy) in enumerate(chips):
            _copy(s.at[2 * cx + cy], d.at[2 * cx + cy], ssems, rsems, 3 * ti + j, (cx, cy, c)).wait_recv()
        for j, (cx, cy) in enumerate(chips):
            _copy(s.at[2 * cx + cy], d.at[2 * cx + cy], ssems, rsems, 3 * ti + j, (cx, cy, c)).wait_send()


def scatter_side(parts):
    return Side(parts, [jax.ShapeDtypeStruct(p.shape, p.dtype) for p in parts], 3 * len(parts), _scatter_first, _scatter_last)


def run_side(side, *, name):
    n_in, n_out = len(side.ins), len(side.out_shapes)

    def body(*refs):
        si, so, sems = _split_refs(refs, [n_in, n_out, 2])
        side.first(si, so, *sems)
        if side.mid is not None:
            side.mid(si, so, *sems)
        side.last(si, so, *sems)

    in_specs, out_specs = side.specs()
    return pl.pallas_call(body, name=name, in_specs=in_specs, out_specs=out_specs, out_shape=side.out_shapes,
                          scratch_shapes=side.sems())(*side.ins)


def swap_sibling(xs, *, name):
    n = len(xs)

    def body(*refs):
        srcs, dsts, sems = _split_refs(refs, [n, n, 2])
        x, y, c, _ = _place()
        copies = [_copy(s, d, *sems, k, (x, y, 1 - c)) for k, (s, d) in enumerate(zip(srcs, dsts))]
        for cp in copies:
            cp.start()
        for cp in copies:
            cp.wait()

    return pl.pallas_call(
        body, name=name, in_specs=[ANY] * n, out_specs=[ANY] * n,
        out_shape=[jax.ShapeDtypeStruct(a.shape, a.dtype) for a in xs],
        scratch_shapes=[pltpu.SemaphoreType.DMA((n,)), pltpu.SemaphoreType.DMA((n,))],
    )(*xs)


def allreduce_small(s):
    n_dev = 8

    def body(s_ref, out_ref, buf, send_sems, recv_sems):
        x, y, c, _ = _place()
        me = 4 * x + 2 * y + c
        buf[me] = s_ref[...]
        sends = []
        for k in range(1, n_dev):
            px = 1 - x if k & 4 else x
            py = 1 - y if k & 2 else y
            pc = 1 - c if k & 1 else c
            cp = _copy(s_ref, buf.at[me], send_sems, recv_sems, k - 1, (px, py, pc))
            cp.start()
            sends.append((cp, 4 * px + 2 * py + pc))
        for k, (cp, peer) in enumerate(sends):
            _copy(s_ref, buf.at[peer], send_sems, recv_sems, k, (x, y, c)).wait_recv()
        for cp, _ in sends:
            cp.wait_send()
        acc = buf[0]
        for d in range(1, n_dev):
            acc = acc + buf[d]
        out_ref[...] = acc

    vm = pl.BlockSpec(memory_space=pltpu.VMEM)
    return pl.pallas_call(
        body, name="allreduce_small", in_specs=[vm], out_specs=vm,
        out_shape=jax.ShapeDtypeStruct(s.shape, F32),
        scratch_shapes=[pltpu.VMEM((n_dev,) + s.shape, F32), pltpu.SemaphoreType.DMA((n_dev - 1,)),
                        pltpu.SemaphoreType.DMA((n_dev - 1,))],
    )(s)


def join_cols(sm):
    n4, k, n = sm.shape
    return sm.transpose(1, 0, 2).reshape(k, n4 * n)


def split_cols(full):
    k, n = full.shape
    return full.reshape(k, N_CHIPS, n // N_CHIPS).transpose(1, 0, 2)


def _pad_heads(w, width):
    lead = w.shape[:-1]
    w = w.reshape(lead + (N_HEADS, width))
    return jnp.pad(w, [(0, 0)] * len(lead) + [(0, 0), (0, LANES - width)]).reshape(lead + (HW,))


def _unpad_heads(w, width):
    lead = w.shape[:-1]
    return w.reshape(lead + (N_HEADS, LANES))[..., :width].reshape(lead + (N_HEADS * width,))


def split_w_in(w_in):
    d = w_in.shape[0]
    o_f = 3 * FW
    o_cq = o_f + N_HEADS
    o_ckv = o_cq + Q_RANK
    o_kr = o_ckv + KV_RANK

    def z(n):
        return jnp.zeros((d, n), w_in.dtype)

    small = jnp.concatenate([w_in[:, o_cq:o_ckv], w_in[:, o_ckv:o_kr], z(HEAD_DIM), w_in[:, o_kr:], z(LANES - HEAD_DIM - ROPE_DIM),
                             w_in[:, o_f:o_cq], z(LANES - N_HEADS), z(LANES)], axis=1)
    return w_in[:, :o_f], small


def join_w_in(d_qkv, d_small):
    kr = S_KR + HEAD_DIM
    return jnp.concatenate([d_qkv, d_small[:, S_F:S_F + N_HEADS], d_small[:, S_CQ:S_CKV], d_small[:, S_CKV:S_KR],
                            d_small[:, kr:kr + ROPE_DIM]], axis=1)


def rope_tables(pos):
    t = pos.shape[0]
    inv_freq = ROPE_THETA ** (-jnp.arange(0, ROPE_DIM, 2, dtype=F32) / ROPE_DIM)
    ang = pos.astype(F32)[:, None] * inv_freq
    cos, sin = jnp.cos(ang), jnp.sin(ang)
    half = ROPE_DIM // 2

    def z(n):
        return jnp.zeros((t, n), F32)

    tab_c = jnp.concatenate([jnp.ones((t, HEAD_DIM), F32), cos, cos, z(LANES - HEAD_DIM - ROPE_DIM)], axis=1)
    tab_a = jnp.concatenate([z(HEAD_DIM), -sin, z(half), z(LANES - HEAD_DIM - ROPE_DIM)], axis=1)
    tab_b = jnp.concatenate([z(HEAD_DIM), z(half), sin, z(LANES - HEAD_DIM - ROPE_DIM)], axis=1)
    return tab_c, tab_a, tab_b


def _pad_lanes(v, n):
    return jnp.pad(v, ((0, 0), (0, n - v.shape[1])))


ATTN_BLK = 512
ATTN_FWD_BLK = 1024


def local_step(xs, pos, tgt, gains, w_early, late_weights, fwd_side=None, bwd_side=None):
    g_attn, b_forget, g_q, g_kv, g_fo, g_mo, g_mlp, g_fin = gains
    w_in, w_uq, w_ukv = w_early
    t = xs.shape[0]
    blk = min(ATTN_BLK, t)
    fox_scale = 1.0 / (HEAD_DIM ** 0.5)
    mla_scale = 1.0 / ((HEAD_DIM + ROPE_DIM) ** 0.5)

    w_qkv, w_small = split_w_in(w_in)
    w_uq_p = _pad_heads(w_uq, HEAD_DIM + ROPE_DIM)
    kv = w_ukv.reshape(KV_RANK, N_HEADS, 2 * HEAD_DIM)
    w_ukv_p = jnp.concatenate([_pad_heads(kv[:, :, :HEAD_DIM].reshape(KV_RANK, FW), HEAD_DIM),
                               kv[:, :, HEAD_DIM:].reshape(KV_RANK, FW)], axis=1)
    b_f = _pad_lanes(b_forget, LANES)
    tab_c, tab_a, tab_b = rope_tables(pos)

    h1 = rmsnorm(xs, g_attn, out_dtype=BF16, name="norm_attn")
    qkv, = mm(h1, w_qkv, out_dtypes=[BF16], name="proj_qkv")
    small, = mm(h1, w_small, out_dtypes=[F32], name="proj_small")
    mq, mk, mv, lf, cqn, ckvn = mla_prep(small, g_q, g_kv, w_uq_p, w_ukv_p, tab_c, tab_a, tab_b, b_f, name="mla_prep")
    f_cum = cumsum_rows(lf, reverse=False, name="gate_cumsum")
    f_blocks = f_cum[:, :N_HEADS].reshape(t // blk, blk, N_HEADS).transpose(0, 2, 1)
    qkv_t, mq_t, mv_t = to_blocks_t(qkv, blk), to_blocks_t(mq, blk), to_blocks_t(mv, blk)
    fo, st_f, *gathered = flash_fwd(qkv_t, qkv, qkv_t, f_cum, qoff=0, koff=PAIRS, voff=2 * PAIRS, pair=True,
                                    scale=fox_scale, name="fox_fwd", blk=ATTN_FWD_BLK, side=fwd_side)
    w_o, w_up, w_down = late_weights(gathered)
    mo, st_m = flash_fwd(mq_t, mk, mv_t, None, qoff=0, koff=0, voff=0, pair=False, scale=mla_scale, name="mla_fwd",
                         blk=ATTN_FWD_BLK)
    mixed = mix_norm(fo, mo, g_fo, g_mo, name="norm_mix")

    def add_res(acc, res):
        return (acc + res,)

    x1, = mm(mixed, w_o, extras=[xs], epilogue=add_res, out_dtypes=[F32], name="out_proj")
    h2 = rmsnorm(x1, g_mlp, out_dtype=BF16, name="norm_mlp")

    def relu2(acc):
        r = jnp.maximum(acc, 0.0)
        return acc, r * r

    u, act = mm(h2, w_up, epilogue=relu2, out_dtypes=[BF16, BF16], name="mlp_up")
    x2, = mm(act, w_down, extras=[x1], epilogue=add_res, out_dtypes=[F32], name="mlp_down")
    loss, dx2, dx2b, dg_fin = loss_head(x2, g_fin, tgt, name="loss_head")

    def relu2_grad(acc, uu):
        return (acc * (2.0 * jnp.maximum(uu.astype(F32), 0.0)),)

    du, = mm(dx2b, w_down, trans_b=True, extras=[u], epilogue=relu2_grad, out_dtypes=[BF16], name="mlp_down_bwd")
    dw_down = mm_tn(act, dx2b, name="dw_down").reshape(N_CHIPS, -1, w_down.shape[1])
    dh2, = mm(du, w_up, trans_b=True, out_dtypes=[F32], name="mlp_up_bwd")
    dw_up = mm_tn(h2, du, name="dw_up", col_shards=N_CHIPS)
    dx1, dx1b, dg_mlp = rmsnorm_bwd(dh2, x1, g_mlp, dx2, out_dtypes=[F32, BF16], name="norm_mlp_bwd")

    dmixed, = mm(dx1b, w_o, trans_b=True, out_dtypes=[F32], name="out_proj_bwd")
    dw_o = mm_tn(mixed, dx1b, name="dw_o").reshape(N_CHIPS, -1, w_o.shape[1])
    side = bwd_side(dw_o, dw_up, dw_down) if bwd_side is not None else None
    dfo, dmo, dg_fo, dg_mo, st_f, st_m = mix_norm_bwd(dmixed, fo, mo, g_fo, g_mo, st_f, st_m, name="norm_mix_bwd")
    dfq, dfk, dfv, df_key, df_query, *scattered = flash_bwd(
        qkv, qkv_t, qkv, qkv, dfo, to_blocks_t(dfo, blk), st_f, f_blocks, qoff=0, koff=PAIRS, voff=2 * PAIRS, pair=True,
        scale=fox_scale, name="fox_bwd", blk=blk, side=side)
    dmq, dmk, dmv = flash_bwd(mq, mq_t, mk, mv, dmo, to_blocks_t(dmo, blk), st_m, None, qoff=0, koff=0, voff=0, pair=False,
                              scale=mla_scale, name="mla_bwd", blk=blk)
    dqkv = jnp.concatenate([dfq, dfk, dfv], axis=1).astype(BF16)
    d_f = df_key[:, :2, :].reshape(N_HEADS, t).T + df_query[:, :, :2].transpose(1, 0, 2).reshape(t, N_HEADS)
    dlf = cumsum_rows(_pad_lanes(d_f, LANES), reverse=True, name="gate_cumsum_bwd")
    dsmall, dq_u, dkv_u, dg_q, dg_kv, db_f = mla_prep_bwd(dmq, dmk, dmv, dlf, small, g_q, g_kv, w_uq_p, w_ukv_p,
                                                          tab_c, tab_a, tab_b, b_f, name="mla_prep_bwd")
    dw_uq_p = mm_tn(cqn, dq_u, name="dw_uq")
    dw_ukv_p = mm_tn(ckvn, dkv_u, name="dw_ukv")

    dh1a, = mm(dqkv, w_qkv, trans_b=True, out_dtypes=[F32], name="proj_qkv_bwd")
    dh1, = mm(dsmall, w_small, trans_b=True, extras=[dh1a], epilogue=add_res, out_dtypes=[F32], name="proj_small_bwd")
    dw_qkv = mm_tn(h1, dqkv, name="dw_qkv")
    dw_small = mm_tn(h1, dsmall, name="dw_small")
    grad_x, dg_attn = rmsnorm_bwd(dh1, xs, g_attn, dx1, out_dtypes=[F32], name="norm_attn_bwd")

    dw_in = join_w_in(dw_qkv, dw_small)
    dw_uq = _unpad_heads(dw_uq_p, HEAD_DIM + ROPE_DIM)
    dk_cols = _unpad_heads(dw_ukv_p[:, :HW], HEAD_DIM).reshape(KV_RANK, N_HEADS, HEAD_DIM)
    dv_cols = dw_ukv_p[:, HW:].reshape(KV_RANK, N_HEADS, HEAD_DIM)
    dw_ukv = jnp.concatenate([dk_cols, dv_cols], axis=2).reshape(KV_RANK, N_HEADS * 2 * HEAD_DIM)
    d_gains = (dg_attn, db_f[:, :N_HEADS], dg_q, dg_kv, dg_fo, dg_mo, dg_mlp, dg_fin)
    return loss, grad_x, (dw_in, dw_uq, dw_ukv), (dw_o, dw_up, dw_down), d_gains, scattered


def kernel(x, positions, attn_norm_g, w_in, b_forget, q_norm_g, w_uq, kv_norm_g, w_ukv, fox_out_g, mla_out_g, w_o, mlp_norm_g, w_up, w_down, final_norm_g, loss_target, m_attn_norm_g, m_w_in, m_b_forget, m_q_norm_g, m_w_uq, m_kv_norm_g, m_w_ukv, m_fox_out_g, m_mla_out_g, m_w_o, m_mlp_norm_g, m_w_up, m_w_down, m_final_norm_g, v_attn_norm_g, v_w_in, v_b_forget, v_q_norm_g, v_w_uq, v_kv_norm_g, v_w_ukv, v_fox_out_g, v_mla_out_g, v_w_o, v_mlp_norm_g, v_w_up, v_w_down, v_final_norm_g):
    core_id = lax.axis_index("c")
    core = core_id.reshape(1).astype(jnp.int32)
    chip = 2 * lax.axis_index("x") + lax.axis_index("y")
    big = [w_in, w_uq, w_ukv, w_o, w_up, w_down]
    big_m = [m_w_in, m_w_uq, m_w_ukv, m_w_o, m_w_up, m_w_down]
    big_v = [v_w_in, v_w_uq, v_w_ukv, v_w_o, v_w_up, v_w_down]
    n_early = 3

    def vec(a):
        return a.reshape(1, -1)

    small = [attn_norm_g, b_forget, q_norm_g, kv_norm_g, fox_out_g, mla_out_g, mlp_norm_g, final_norm_g]
    small_m = [m_attn_norm_g, m_b_forget, m_q_norm_g, m_kv_norm_g, m_fox_out_g, m_mla_out_g, m_mlp_norm_g, m_final_norm_g]
    small_v = [v_attn_norm_g, v_b_forget, v_q_norm_g, v_kv_norm_g, v_fox_out_g, v_mla_out_g, v_mlp_norm_g, v_final_norm_g]
    gains = [vec(a) for a in small]

    shards = [w[0].astype(BF16) for w in big]

    def with_own(gathered, mine):
        return [lax.dynamic_update_index_in_dim(g, s, chip, 0) for g, s in zip(gathered, mine)]

    early = with_own(run_side(gather_side(shards[:n_early]), name="gather_early"), shards[:n_early])
    w_early = [join_cols(g) for g in early]

    def late_weights(gathered):
        g_o, g_up, g_down = with_own(gathered, shards[n_early:])
        return g_o.reshape(-1, g_o.shape[2]), join_cols(g_up), g_down.reshape(-1, g_down.shape[2])

    def pair_sums(grads, name):
        sends = [lax.dynamic_slice_in_dim(g, (1 - core_id) * (g.shape[1] // 2), g.shape[1] // 2, axis=1).astype(BF16)
                 for g in grads]
        return sends

    def add_pairs(grads, recvs, name):
        return [add_pair(g, r, core, name="%s_%d" % (name, n)) for n, (g, r) in enumerate(zip(grads, recvs))]

    def chip_sums(scattered, pairs, name):
        with_mine = [lax.dynamic_update_index_in_dim(s, lax.dynamic_index_in_dim(p, chip, 0, keepdims=True), chip, 0)
                     for s, p in zip(scattered, pairs)]
        return [sum_chips(s, name="%s_%d" % (name, n)) for n, s in enumerate(with_mine)]

    late_pairs = []

    def bwd_side(dw_o, dw_up, dw_down):
        grads = [dw_o, dw_up, dw_down]
        recvs = swap_sibling(pair_sums(grads, "late"), name="swap_late")
        late_pairs.extend(add_pairs(grads, recvs, "add_pair_late"))
        return scatter_side(late_pairs)

    loss, grad_x, d_early, _, d_small, scattered_late = local_step(
        x[0], positions[0], loss_target[0], gains, w_early, late_weights, gather_side(shards[n_early:]), bwd_side)

    halves_late = chip_sums(scattered_late, late_pairs, "sum_chips_late")
    g_early = [split_cols(d) for d in d_early]
    n_late = len(halves_late)
    swapped = swap_sibling(halves_late + pair_sums(g_early, "early"), name="swap_mixed")
    others_late, recvs_early = swapped[:n_late], swapped[n_late:]
    early_pairs = add_pairs(g_early, recvs_early, "add_pair_early")
    scattered_early = run_side(scatter_side(early_pairs), name="scatter_early")
    halves_early = chip_sums(scattered_early, early_pairs, "sum_chips_early")
    others_early = swap_sibling(halves_early, name="swap_early")
    halves = halves_early + halves_late
    others = list(others_early) + list(others_late)

    def rows8(vs):
        return jnp.concatenate([_pad_lanes(vec(a).astype(F32), 1024) for a in vs], axis=0)

    g_small8 = allreduce_small(rows8(d_small))

    outs_big = [adamw_halves(w, gm, go, m, v, core, name="adamw_%d" % n)
                for n, (w, gm, go, m, v) in enumerate(zip(big, halves, others, big_m, big_v))]
    d8, m8, v8 = adamw(rows8(small), g_small8, rows8(small_m), rows8(small_v), name="adamw_small")

    def unrows8(a8):
        return [a8[n, :s.size].reshape(s.shape) for n, s in enumerate(small)]

    loss_all = lax.psum(loss[0, 0], ("x", "y", "c"))
    grads, deltas, new_m, new_v = [None] * 14, [None] * 14, [None] * 14, [None] * 14
    big_at = [1, 4, 6, 9, 11, 12]
    small_at = [0, 2, 3, 5, 7, 8, 10, 13]
    for n, at in enumerate(big_at):
        grads[at], deltas[at], new_m[at], new_v[at] = outs_big[n]
    for at, g, dd, mm_, vv in zip(small_at, unrows8(g_small8), unrows8(d8), unrows8(m8), unrows8(v8)):
        grads[at], deltas[at], new_m[at], new_v[at] = g, dd, mm_, vv
    return (loss_all, grad_x[None], *grads, *deltas, *new_m, *new_v)
```

```python
import jax
import jax.numpy as jnp
from jax import lax
from jax.experimental import pallas as pl
from jax.experimental.pallas import tpu as pltpu

F32 = jnp.float32
BF16 = jnp.bfloat16
MESH = pl.DeviceIdType.MESH

EPS = 1e-6
ROPE_THETA = 10000.0
N_HEADS = 8
PAIRS = N_HEADS // 2
HEAD_DIM = 64
ROPE_DIM = 32
LANES = 128
Q_RANK = 384
KV_RANK = 256
N_CHIPS = 4
ADAM_LR, ADAM_B1, ADAM_B2, ADAM_EPS, ADAM_WD, ADAM_STEP = 0.001, 0.9, 0.999, 1e-08, 0.01, 10
VMEM_LIMIT = 48 * 1024 * 1024
LOG2E = 1.4426950408889634
LN2 = 0.6931471805599453
NN = (((1,), (0,)), ((), ()))
NT = (((1,), (1,)), ((), ()))
TN = (((0,), (0,)), ((), ()))


def _params(sem=None):
    return pltpu.CompilerParams(dimension_semantics=sem, vmem_limit_bytes=VMEM_LIMIT)


def _fit(block, dim):
    if dim <= block:
        return dim
    return next(b for b in range(block - block % LANES, 0, -LANES) if dim % b == 0)


def _row_block(rows):
    return next(b for b in (256, 128, 64, 32, 16, 8) if rows % b == 0)


def rmsnorm(x, g, *, out_dtype, name, bt=512):
    t, d = x.shape
    bt = min(bt, t)

    def body(x_ref, g_ref, o_ref):
        xv = x_ref[...].astype(F32)
        r = lax.rsqrt(jnp.mean(xv * xv, axis=-1, keepdims=True) + EPS)
        o_ref[...] = (xv * r * g_ref[...]).astype(o_ref.dtype)

    return pl.pallas_call(
        body, name=name, grid=(t // bt,),
        in_specs=[pl.BlockSpec((bt, d), lambda i: (i, 0)), pl.BlockSpec((1, d), lambda i: (0, 0))],
        out_specs=pl.BlockSpec((bt, d), lambda i: (i, 0)),
        out_shape=jax.ShapeDtypeStruct((t, d), out_dtype),
        compiler_params=_params(("parallel",)),
    )(x, g)


def rmsnorm_bwd(dh, x, g, res, *, out_dtypes, name, bt=512):
    t, d = x.shape
    bt = min(bt, t)
    has_res = res is not None

    def body(*refs):
        dh_ref, x_ref, g_ref = refs[:3]
        res_ref = refs[3] if has_res else None
        outs = refs[3 + has_res:]
        dx_refs, dg_ref = outs[:-1], outs[-1]
        xv = x_ref[...].astype(F32)
        dhv = dh_ref[...].astype(F32)
        r = lax.rsqrt(jnp.mean(xv * xv, axis=-1, keepdims=True) + EPS)
        u = dhv * g_ref[...]
        dot = jnp.mean(u * xv, axis=-1, keepdims=True)
        dx = r * u - xv * (r * r * r * dot)
        if has_res:
            dx = dx + res_ref[...]
        for o in dx_refs:
            o[...] = dx.astype(o.dtype)

        @pl.when(pl.program_id(0) == 0)
        def _():
            dg_ref[...] = jnp.zeros_like(dg_ref)

        dg_ref[...] += jnp.sum(dhv * (xv * r), axis=0, keepdims=True)

    row = pl.BlockSpec((bt, d), lambda i: (i, 0))
    vec = pl.BlockSpec((1, d), lambda i: (0, 0))
    ins = [dh, x, g] + ([res] if has_res else [])
    return pl.pallas_call(
        body, name=name, grid=(t // bt,),
        in_specs=[row, row, vec] + ([row] if has_res else []),
        out_specs=[row] * len(out_dtypes) + [vec],
        out_shape=[jax.ShapeDtypeStruct((t, d), dt) for dt in out_dtypes] + [jax.ShapeDtypeStruct((1, d), F32)],
        compiler_params=_params(("arbitrary",)),
    )(*ins)


def mm(a, b, *, trans_b=False, extras=(), epilogue=None, out_dtypes, name, bm=1024, bn=1024):
    m, k = a.shape
    n = b.shape[0] if trans_b else b.shape[1]
    if k > 2048:
        bm = bm // 2
    bm, bn = _fit(bm, m), _fit(bn, n)
    n_ex = len(extras)

    def body(*refs):
        a_ref, b_ref = refs[0], refs[1]
        ex = refs[2:2 + n_ex]
        outs = refs[2 + n_ex:]
        acc = lax.dot_general(a_ref[...], b_ref[...], NT if trans_b else NN, preferred_element_type=F32)
        res = epilogue(acc, *[e[...] for e in ex]) if epilogue is not None else (acc,)
        for o, r in zip(outs, res):
            o[...] = r.astype(o.dtype)

    tile = pl.BlockSpec((bm, bn), lambda i, j: (i, j))
    b_spec = pl.BlockSpec((bn, k), lambda i, j: (j, 0)) if trans_b else pl.BlockSpec((k, bn), lambda i, j: (0, j))
    return pl.pallas_call(
        body, name=name, grid=(m // bm, n // bn),
        in_specs=[pl.BlockSpec((bm, k), lambda i, j: (i, 0)), b_spec] + [tile] * n_ex,
        out_specs=[tile] * len(out_dtypes),
        out_shape=[jax.ShapeDtypeStruct((m, n), dt) for dt in out_dtypes],
        compiler_params=_params(("parallel", "parallel")),
    )(a, b, *extras)


def mm_tn(a, b, *, name, col_shards=1, bk=1024, bn=1024, bt=1024):
    t, k = a.shape
    n = b.shape[1]
    ns = n // col_shards
    bk, bn, bt = _fit(bk, k), _fit(bn, ns), _fit(bt, t)
    per = ns // bn

    def body(a_ref, b_ref, o_ref):
        @pl.when(pl.program_id(2) == 0)
        def _():
            o_ref[...] = jnp.zeros_like(o_ref)

        o_ref[...] += lax.dot_general(a_ref[...], b_ref[...], TN, preferred_element_type=F32)

    if col_shards == 1:
        out_spec = pl.BlockSpec((bk, bn), lambda i, j, s: (i, j))
        out_shape = jax.ShapeDtypeStruct((k, n), F32)
    else:
        out_spec = pl.BlockSpec((None, bk, bn), lambda i, j, s: (j // per, i, j % per))
        out_shape = jax.ShapeDtypeStruct((col_shards, k, ns), F32)
    return pl.pallas_call(
        body, name=name, grid=(k // bk, n // bn, t // bt),
        in_specs=[pl.BlockSpec((bt, bk), lambda i, j, s: (s, i)), pl.BlockSpec((bt, bn), lambda i, j, s: (s, j))],
        out_specs=out_spec, out_shape=out_shape,
        compiler_params=_params(("parallel", "parallel", "arbitrary")),
    )(a, b)


def _split3(x):
    hi = x.astype(BF16)
    r1 = x - hi.astype(F32)
    mid = r1.astype(BF16)
    lo = (r1 - mid.astype(F32)).astype(BF16)
    return hi, mid, lo


def cumsum_rows(x, *, reverse, name, bc=512):
    t, d = x.shape
    bc = min(bc, t)
    nb = t // bc

    def body(x_ref, o_ref, carry):
        @pl.when(pl.program_id(0) == 0)
        def _():
            carry[...] = jnp.zeros_like(carry)

        r = lax.broadcasted_iota(jnp.int32, (bc, bc), 0)
        c = lax.broadcasted_iota(jnp.int32, (bc, bc), 1)
        tri = jnp.where((r <= c) if reverse else (r >= c), 1.0, 0.0).astype(BF16)
        hi, mid, lo = _split3(x_ref[...])
        s = (lax.dot_general(tri, hi, NN, preferred_element_type=F32)
             + lax.dot_general(tri, mid, NN, preferred_element_type=F32)
             + lax.dot_general(tri, lo, NN, preferred_element_type=F32)) + carry[0:1, :]
        o_ref[...] = s
        carry[0:1, :] = s[0:1, :] if reverse else s[bc - 1:bc, :]

    imap = (lambda i: (nb - 1 - i, 0)) if reverse else (lambda i: (i, 0))
    return pl.pallas_call(
        body, name=name, grid=(nb,),
        in_specs=[pl.BlockSpec((bc, d), imap)], out_specs=pl.BlockSpec((bc, d), imap),
        out_shape=jax.ShapeDtypeStruct((t, d), F32),
        scratch_shapes=[pltpu.VMEM((8, d), F32)],
        compiler_params=_params(("arbitrary",)),
    )(x)


def _rope(x, c, a, b):
    return x * c + pltpu.roll(x, LANES - ROPE_DIM // 2, 1) * a + pltpu.roll(x, ROPE_DIM // 2, 1) * b


def _rope_bwd(d, c, a, b):
    return d * c + pltpu.roll(d * a, ROPE_DIM // 2, 1) + pltpu.roll(d * b, LANES - ROPE_DIM // 2, 1)


S_CQ, S_CKV, S_KR, S_F, S_END = 0, Q_RANK, Q_RANK + KV_RANK, Q_RANK + KV_RANK + LANES, 1024
HW = N_HEADS * LANES
FW = N_HEADS * HEAD_DIM


def mla_prep(small, g_q, g_kv, w_uq, w_ukv, tab_c, tab_a, tab_b, b_f, *, name, bt=512):
    t = small.shape[0]
    bt = min(bt, t)

    def body(s_ref, gq_ref, gkv_ref, wq_ref, wkv_ref, c_ref, a_ref, b_ref, bf_ref,
             mq_ref, mk_ref, mv_ref, lf_ref, cqn_ref, ckvn_ref):
        cq = s_ref[:, S_CQ:S_CKV]
        rq = lax.rsqrt(jnp.mean(cq * cq, axis=-1, keepdims=True) + EPS)
        cqn = (cq * rq * gq_ref[...]).astype(BF16)
        ckv = s_ref[:, S_CKV:S_KR]
        rkv = lax.rsqrt(jnp.mean(ckv * ckv, axis=-1, keepdims=True) + EPS)
        ckvn = (ckv * rkv * gkv_ref[...]).astype(BF16)
        cqn_ref[...] = cqn
        ckvn_ref[...] = ckvn
        tc, ta, tb = c_ref[...], a_ref[...], b_ref[...]
        q = jnp.dot(cqn, wq_ref[...], preferred_element_type=F32)
        kv = jnp.dot(ckvn, wkv_ref[...], preferred_element_type=F32)
        kr = _rope(s_ref[:, S_KR:S_F], tc, ta, tb)
        for h in range(N_HEADS):
            sl = slice(h * LANES, (h + 1) * LANES)
            mq_ref[:, sl] = _rope(q[:, sl], tc, ta, tb).astype(BF16)
            mk_ref[:, sl] = (kv[:, sl] + kr).astype(BF16)
        mv_ref[...] = kv[:, HW:].astype(BF16)
        z = s_ref[:, S_F:S_END - LANES] + bf_ref[...]
        lf_ref[...] = jnp.minimum(z, 0.0) - jnp.log(1.0 + jnp.exp(-jnp.abs(z)))

    def row(w):
        return pl.BlockSpec((bt, w), lambda i: (i, 0))

    def full(arr):
        return pl.BlockSpec(arr.shape, lambda i: (0, 0))

    return pl.pallas_call(
        body, name=name, grid=(t // bt,),
        in_specs=[row(S_END), full(g_q), full(g_kv), full(w_uq), full(w_ukv), row(LANES), row(LANES), row(LANES), full(b_f)],
        out_specs=[row(HW), row(HW), row(FW), row(LANES), row(Q_RANK), row(KV_RANK)],
        out_shape=[jax.ShapeDtypeStruct((t, HW), BF16)] * 2 + [jax.ShapeDtypeStruct((t, FW), BF16), jax.ShapeDtypeStruct((t, LANES), F32),
                   jax.ShapeDtypeStruct((t, Q_RANK), BF16), jax.ShapeDtypeStruct((t, KV_RANK), BF16)],
        compiler_params=_params(("parallel",)),
    )(small, g_q, g_kv, w_uq, w_ukv, tab_c, tab_a, tab_b, b_f)


def mla_prep_bwd(dmq, dmk, dmv, dlf, small, g_q, g_kv, w_uq, w_ukv, tab_c, tab_a, tab_b, b_f, *, name, bt=512):
    t = small.shape[0]
    bt = min(bt, t)

    def body(dmq_ref, dmk_ref, dmv_ref, dlf_ref, s_ref, gq_ref, gkv_ref, wq_ref, wkv_ref, c_ref, a_ref, b_ref, bf_ref,
             ds_ref, dq_ref, dkv_ref, dgq_ref, dgkv_ref, db_ref):
        tc, ta, tb = c_ref[...], a_ref[...], b_ref[...]
        lane = lax.broadcasted_iota(jnp.int32, (1, LANES), 1)
        dkr = jnp.zeros((bt, LANES), F32)
        for h in range(N_HEADS):
            sl = slice(h * LANES, (h + 1) * LANES)
            dq_ref[:, sl] = _rope_bwd(dmq_ref[:, sl], tc, ta, tb).astype(BF16)
            dkr = dkr + dmk_ref[:, sl]
        dkv_ref[:, :HW] = dmk_ref[...].astype(BF16)
        dkv_ref[:, HW:] = dmv_ref[...].astype(BF16)
        in_rope = (lane >= HEAD_DIM) & (lane < HEAD_DIM + ROPE_DIM)
        ds_ref[:, S_KR:S_F] = jnp.where(in_rope, _rope_bwd(dkr, tc, ta, tb), 0.0).astype(BF16)

        def norm_bwd(raw, g_ref, dn, dg_ref):
            r = lax.rsqrt(jnp.mean(raw * raw, axis=-1, keepdims=True) + EPS)
            u = dn * g_ref[...]
            dot = jnp.mean(u * raw, axis=-1, keepdims=True)
            dg_ref[...] += jnp.sum(dn * (raw * r), axis=0, keepdims=True)
            return r * u - raw * (r * r * r * dot)

        @pl.when(pl.program_id(0) == 0)
        def _():
            dgq_ref[...] = jnp.zeros_like(dgq_ref)
            dgkv_ref[...] = jnp.zeros_like(dgkv_ref)
            db_ref[...] = jnp.zeros_like(db_ref)

        dcqn = lax.dot_general(dq_ref[...], wq_ref[...], NT, preferred_element_type=F32)
        ds_ref[:, S_CQ:S_CKV] = norm_bwd(s_ref[:, S_CQ:S_CKV], gq_ref, dcqn, dgq_ref).astype(BF16)
        dckvn = lax.dot_general(dkv_ref[...], wkv_ref[...], NT, preferred_element_type=F32)
        ds_ref[:, S_CKV:S_KR] = norm_bwd(s_ref[:, S_CKV:S_KR], gkv_ref, dckvn, dgkv_ref).astype(BF16)
        z = s_ref[:, S_F:S_END - LANES] + bf_ref[...]
        dz = jnp.where(lane < N_HEADS, dlf_ref[...] / (1.0 + jnp.exp(z)), 0.0)
        db_ref[...] += jnp.sum(dz, axis=0, keepdims=True)
        ds_ref[:, S_F:S_END - LANES] = dz.astype(BF16)
        ds_ref[:, S_END - LANES:] = jnp.zeros((bt, LANES), BF16)

    def row(w):
        return pl.BlockSpec((bt, w), lambda i: (i, 0))

    def full(arr):
        return pl.BlockSpec(arr.shape, lambda i: (0, 0))

    def vec(w):
        return pl.BlockSpec((1, w), lambda i: (0, 0))

    return pl.pallas_call(
        body, name=name, grid=(t // bt,),
        in_specs=[row(HW), row(HW), row(FW), row(LANES), row(S_END), full(g_q), full(g_kv), full(w_uq), full(w_ukv),
                  row(LANES), row(LANES), row(LANES), full(b_f)],
        out_specs=[row(S_END), row(HW), row(HW + FW), vec(Q_RANK), vec(KV_RANK), vec(LANES)],
        out_shape=[jax.ShapeDtypeStruct((t, S_END), BF16), jax.ShapeDtypeStruct((t, HW), BF16),
                   jax.ShapeDtypeStruct((t, HW + FW), BF16), jax.ShapeDtypeStruct((1, Q_RANK), F32),
                   jax.ShapeDtypeStruct((1, KV_RANK), F32), jax.ShapeDtypeStruct((1, LANES), F32)],
        compiler_params=_params(("arbitrary",)),
    )(dmq, dmk, dmv, dlf, small, g_q, g_kv, w_uq, w_ukv, tab_c, tab_a, tab_b, b_f)


class Side:
    def __init__(self, ins, out_shapes, n_sems, first, last, mid=None):
        self.ins, self.out_shapes, self.n_sems = list(ins), list(out_shapes), n_sems
        self.first, self.mid, self.last = first, mid, last

    def specs(self):
        return [ANY] * len(self.ins), [ANY] * len(self.out_shapes)

    def sems(self):
        return [pltpu.SemaphoreType.DMA((self.n_sems,)), pltpu.SemaphoreType.DMA((self.n_sems,))]


def _lane():
    return lax.broadcasted_iota(jnp.int32, (1, LANES), 1)


def _halves(x):
    zero = jnp.zeros_like(x)
    return [jnp.where(_lane() < HEAD_DIM, x, zero), jnp.where(_lane() >= HEAD_DIM, x, zero)]


def _groups(x):
    return [x[:, :LANES], x[:, LANES:]]


def _lanes01(a, b, rest):
    return jnp.where(_lane() == 0, a, jnp.where(_lane() == 1, b, rest))


def _pick_row(tile, h):
    row = lax.broadcasted_iota(jnp.int32, (tile.shape[0], 1), 0)
    return jnp.sum(jnp.where(row == h, tile, 0.0), axis=0, keepdims=True)


def _pick_lane(tile, h):
    return jnp.sum(jnp.where(_lane() == h, tile, 0.0), axis=1, keepdims=True)


def _row_halves(x):
    row = lax.broadcasted_iota(jnp.int32, (LANES, 1), 0)
    zero = jnp.zeros_like(x)
    return [jnp.where(row < HEAD_DIM, x, zero), jnp.where(row >= HEAD_DIM, x, zero)]


def _below_diagonal(s, lead=0):
    r = lax.broadcasted_iota(jnp.int32, s.shape, 0)
    c = lax.broadcasted_iota(jnp.int32, s.shape, 1)
    return jnp.where(c <= r + lead, s, -jnp.inf)


def _above_diagonal(s):
    r = lax.broadcasted_iota(jnp.int32, s.shape, 0)
    c = lax.broadcasted_iota(jnp.int32, s.shape, 1)
    return jnp.where(r <= c, s, -jnp.inf)


def to_blocks_t(x, blk):
    t, c = x.shape
    return x.reshape(t // blk, blk, c).transpose(0, 2, 1)


def _split_refs(refs, counts):
    out, at = [], 0
    for n in counts:
        out.append(refs[at:at + n])
        at += n
    return out


def flash_fwd(qt_arr, k_arr, vt_arr, f_cum, *, qoff, koff, voff, pair, scale, name, blk=512, side=None):
    t = k_arr.shape[0]
    tblk = qt_arr.shape[2]
    blk = max(min(blk, t), tblk)
    sub = blk // tblk
    nb = t // blk
    w = LANES if pair else 2 * LANES
    has_bias = f_cum is not None
    ins = [qt_arr, k_arr, vt_arr] + ([f_cum] if has_bias else [])

    def wide(ref, first):
        parts = [ref[first + u] for u in range(sub)]
        return parts[0] if sub == 1 else jnp.concatenate(parts, axis=1)
    s_ins, s_outs = (side.ins, side.out_shapes) if side else ([], [])

    def body(*refs):
        main, si, outs, so, sems = _split_refs(refs, [len(ins), len(s_ins), 2, len(s_outs), 2 if side else 0])
        qt_ref, k_ref, vt_ref = main[:3]
        f_ref = main[3] if has_bias else None
        o_ref, st_ref = outs
        g, i = pl.program_id(0), pl.program_id(1)
        step_id = g * nb + i
        if side:
            @pl.when(step_id == 0)
            def _():
                side.first(si, so, *sems)

            if side.mid is not None:
                @pl.when(step_id == (3 * PAIRS * nb) // 4)
                def _():
                    side.mid(si, so, *sems)

        qt = (wide(qt_ref, 0).astype(F32) * (scale * LOG2E)).astype(BF16)
        qts = _row_halves(qt) if pair else [qt[:LANES], qt[LANES:]]

        def step(j, carry, diagonal):
            rows = pl.ds(pl.multiple_of(j * blk, blk), blk)
            kk = k_ref[rows, :]
            ks = [kk, kk] if pair else _groups(kk)
            vt = wide(vt_ref, sub * j)
            out = []
            for n in range(2):
                m, l, acc = carry[n]
                s = jnp.dot(ks[n], qts[n], preferred_element_type=F32)
                if has_bias:
                    s = s - LOG2E * _pick_lane(f_ref[rows, :], 2 * g + n)
                if diagonal:
                    s = _above_diagonal(s)
                m_new = jnp.maximum(m, jnp.max(s, axis=0, keepdims=True))
                alpha = jnp.exp2(m - m_new)
                p = jnp.exp2(s - m_new)
                out.append((m_new, alpha * l + jnp.sum(p, axis=0, keepdims=True),
                            alpha * acc + jnp.dot(vt[n * HEAD_DIM:(n + 1) * HEAD_DIM], p.astype(BF16),
                                                  preferred_element_type=F32)))
            return tuple(out)

        init = tuple((jnp.full((1, blk), -jnp.inf, F32), jnp.zeros((1, blk), F32), jnp.zeros((HEAD_DIM, blk), F32))
                     for _ in range(2))
        carry = lax.fori_loop(0, i, lambda j, c: step(j, c, False), init)
        (ma, la, acca), (mb, lb, accb) = step(i, carry, True)
        o_ref[...] = jnp.concatenate([acca / la, accb / lb], axis=0).T
        row = lax.broadcasted_iota(jnp.int32, (LANES, 1), 0)
        st_ref[0] = jnp.where(row == 0, ma + jnp.log2(la), jnp.where(row == 1, mb + jnp.log2(lb), 0.0)).T
        if side:
            @pl.when(step_id == PAIRS * nb - 1)
            def _():
                side.last(si, so, *sems)

    in_specs = [pl.BlockSpec((sub, w, tblk), lambda g, i: (i, qoff + g, 0)), pl.BlockSpec((t, w), lambda g, i: (0, koff + g)),
                pl.BlockSpec((t // tblk, LANES, tblk), lambda g, i: (0, voff + g, 0))]
    if has_bias:
        in_specs.append(pl.BlockSpec((t, LANES), lambda g, i: (0, 0)))
    s_in_specs, s_out_specs = side.specs() if side else ([], [])
    return pl.pallas_call(
        body, name=name, grid=(PAIRS, nb), in_specs=in_specs + s_in_specs,
        out_specs=[pl.BlockSpec((blk, LANES), lambda g, i: (i, g)), pl.BlockSpec((1, blk, LANES), lambda g, i: (g, i, 0))]
        + s_out_specs,
        out_shape=[jax.ShapeDtypeStruct((t, PAIRS * LANES), F32), jax.ShapeDtypeStruct((PAIRS, t, LANES), F32)] + list(s_outs),
        scratch_shapes=side.sems() if side else [],
        compiler_params=_params(("arbitrary", "arbitrary")),
    )(*ins, *s_ins)


def mix_norm(fo, mo, g_fo, g_mo, *, name, bt=512):
    t, d = fo.shape
    bt = min(bt, t)

    def body(fo_ref, mo_ref, gf_ref, gm_ref, o_ref):
        for n, (x_ref, g_ref) in enumerate(((fo_ref, gf_ref), (mo_ref, gm_ref))):
            xv = x_ref[...]
            r = lax.rsqrt(jnp.mean(xv * xv, axis=-1, keepdims=True) + EPS)
            o_ref[:, n * d:(n + 1) * d] = (xv * r * g_ref[...]).astype(BF16)

    row = pl.BlockSpec((bt, d), lambda i: (i, 0))
    vec = pl.BlockSpec((1, d), lambda i: (0, 0))
    return pl.pallas_call(
        body, name=name, grid=(t // bt,), in_specs=[row, row, vec, vec],
        out_specs=pl.BlockSpec((bt, 2 * d), lambda i: (i, 0)),
        out_shape=jax.ShapeDtypeStruct((t, 2 * d), BF16),
        compiler_params=_params(("parallel",)),
    )(fo, mo, g_fo, g_mo)


def mix_norm_bwd(dmixed, fo, mo, g_fo, g_mo, st_f, st_m, *, name, bt=512):
    t, d = fo.shape
    bt = min(bt, t)

    def body(dm_ref, fo_ref, mo_ref, gf_ref, gm_ref, sf_ref, sm_ref, dfo_ref, dmo_ref, dgf_ref, dgm_ref, sfo_ref, smo_ref):
        @pl.when(pl.program_id(0) == 0)
        def _():
            dgf_ref[...] = jnp.zeros_like(dgf_ref)
            dgm_ref[...] = jnp.zeros_like(dgm_ref)

        groups = ((fo_ref, gf_ref, dfo_ref, dgf_ref, sf_ref, sfo_ref), (mo_ref, gm_ref, dmo_ref, dgm_ref, sm_ref, smo_ref))
        for n, (x_ref, g_ref, dx_ref, dg_ref, st_ref, sto_ref) in enumerate(groups):
            xv = x_ref[...]
            dhv = dm_ref[:, n * d:(n + 1) * d]
            r = lax.rsqrt(jnp.mean(xv * xv, axis=-1, keepdims=True) + EPS)
            u = dhv * g_ref[...]
            dxb = (r * u - xv * (r * r * r * jnp.mean(u * xv, axis=-1, keepdims=True))).astype(BF16)
            dx_ref[...] = dxb
            dg_ref[...] += jnp.sum(dhv * (xv * r), axis=0, keepdims=True)
            prod = xv * dxb.astype(F32)
            for g in range(PAIRS):
                grp = prod[:, g * LANES:(g + 1) * LANES]
                da = jnp.sum(jnp.where(_lane() < HEAD_DIM, grp, 0.0), axis=1, keepdims=True)
                db = jnp.sum(jnp.where(_lane() >= HEAD_DIM, grp, 0.0), axis=1, keepdims=True)
                sto_ref[g] = jnp.where(_lane() == 2, da, jnp.where(_lane() == 3, db, st_ref[g]))

    row = pl.BlockSpec((bt, d), lambda i: (i, 0))
    vec = pl.BlockSpec((1, d), lambda i: (0, 0))
    stat = pl.BlockSpec((PAIRS, bt, LANES), lambda i: (0, i, 0))
    return pl.pallas_call(
        body, name=name, grid=(t // bt,),
        in_specs=[pl.BlockSpec((bt, 2 * d), lambda i: (i, 0)), row, row, vec, vec, stat, stat],
        out_specs=[row, row, vec, vec, stat, stat],
        out_shape=[jax.ShapeDtypeStruct((t, d), BF16)] * 2 + [jax.ShapeDtypeStruct((1, d), F32)] * 2
        + [jax.ShapeDtypeStruct(st_f.shape, F32)] * 2,
        compiler_params=_params(("arbitrary",)),
    )(dmixed, fo, mo, g_fo, g_mo, st_f, st_m)


def flash_bwd(q_arr, qt_arr, k_arr, v_arr, do_arr, dot_arr, st, f_blocks, *, qoff, koff, voff, pair, scale, name, qblk=1024,
              side=None):
    t = q_arr.shape[0]
    blk = qt_arr.shape[2]
    qblk = max(min(qblk, t), blk)
    sub = qblk // blk
    nb, nbq = t // blk, t // qblk
    w = LANES if pair else 2 * LANES
    hw = w // 2
    has_bias = f_blocks is not None
    split = _halves if pair else _groups
    ins = [q_arr, qt_arr, k_arr, v_arr, do_arr, dot_arr, st] + ([f_blocks] if has_bias else [])
    n_out = 5 if has_bias else 3
    s_ins, s_outs = (side.ins, side.out_shapes) if side else ([], [])

    def wide(ref, first):
        parts = [ref[first + u] for u in range(sub)]
        return parts[0] if sub == 1 else jnp.concatenate(parts, axis=1)

    def body(*refs):
        main, si, outs, so, sems = _split_refs(refs, [len(ins), len(s_ins), n_out, len(s_outs), 2 if side else 0])
        q_ref, qt_ref, k_ref, v_ref, do_ref, dot_ref, st_ref = main[:7]
        dq_ref, dk_ref, dv_ref = outs[:3]
        g, j = pl.program_id(0), pl.program_id(1)
        step_id = g * nb + j
        if side:
            @pl.when(step_id == 0)
            def _():
                side.first(si, so, *sems)

        kk, vv = k_ref[...], v_ref[...]
        ks = [kk, kk] if pair else _groups(kk)
        if has_bias:
            f_ref, dfk_ref, dfq_ref = main[7], outs[3], outs[4]
            fk = [LOG2E * _pick_row(f_ref[0], 2 * g + n) for n in range(2)]

        def step(i, carry, diagonal):
            rows = pl.ds(pl.multiple_of(i * qblk, qblk), qblk)
            qs = split((q_ref[rows, :].astype(F32) * (scale * LOG2E)).astype(BF16))
            qt = (wide(qt_ref, sub * i).astype(F32) * (scale * LOG2E)).astype(BF16)
            dos = [h.astype(BF16) for h in _halves(do_ref[rows, :].astype(F32))]
            dot = wide(dot_ref, sub * i)
            stats = st_ref[0, rows, :]
            new, dqs, row_sums = [], [], []
            for n in range(2):
                dkt, dvt, dfk = carry[n]
                s = lax.dot_general(qs[n], ks[n], NT, preferred_element_type=F32)
                if has_bias:
                    s = s - fk[n]
                if diagonal:
                    s = _below_diagonal(s, i * qblk - j * blk)
                p = jnp.exp2(s - stats[:, n:n + 1])
                dp = lax.dot_general(dos[n], vv, NT, preferred_element_type=F32)
                ds = p * (dp - stats[:, 2 + n:3 + n])
                dsb = ds.astype(BF16)
                dvt = dvt + jnp.dot(dot[n * HEAD_DIM:(n + 1) * HEAD_DIM], p.astype(BF16), preferred_element_type=F32)
                dkt = dkt + jnp.dot(qt[n * hw:(n + 1) * hw], dsb, preferred_element_type=F32)
                dqs.append(jnp.dot(dsb, ks[n], preferred_element_type=F32))
                if has_bias:
                    dfk = dfk - jnp.sum(ds, axis=0, keepdims=True)
                    row_sums.append(jnp.sum(ds, axis=1, keepdims=True))
                new.append((dkt, dvt, dfk))
            dq = (jnp.where(_lane() < HEAD_DIM, dqs[0], dqs[1]) if pair else jnp.concatenate(dqs, axis=1)) * scale
            rs = _lanes01(row_sums[0], row_sums[1], 0.0) if has_bias else None

            @pl.when(j == 0)
            def _():
                dq_ref[rows, :] = dq
                if has_bias:
                    dfq_ref[0, rows, :] = rs

            @pl.when(j > 0)
            def _():
                dq_ref[rows, :] += dq
                if has_bias:
                    dfq_ref[0, rows, :] += rs

            return tuple(new)

        init = tuple((jnp.zeros((hw, blk), F32), jnp.zeros((HEAD_DIM, blk), F32), jnp.zeros((1, blk), F32)) for _ in range(2))
        first = j // sub
        carry = step(first, init, True)
        (dka, dva, dfa), (dkb, dvb, dfb) = lax.fori_loop(first + 1, nbq, lambda i, c: step(i, c, False), carry)
        dk_ref[...] = jnp.concatenate([dka, dkb], axis=0).T * LN2
        dv_ref[...] = jnp.concatenate([dva, dvb], axis=0).T
        if has_bias:
            row = lax.broadcasted_iota(jnp.int32, (N_HEADS, 1), 0)
            dfk_ref[0] = jnp.where(row == 0, dfa, jnp.where(row == 1, dfb, 0.0))
        if side:
            @pl.when(step_id == PAIRS * nb - 1)
            def _():
                side.last(si, so, *sems)

    in_specs = [pl.BlockSpec((t, w), lambda g, j: (0, qoff + g)), pl.BlockSpec((nb, w, blk), lambda g, j: (0, qoff + g, 0)),
                pl.BlockSpec((blk, w), lambda g, j: (j, koff + g)), pl.BlockSpec((blk, LANES), lambda g, j: (j, voff + g)),
                pl.BlockSpec((t, LANES), lambda g, j: (0, g)), pl.BlockSpec((nb, LANES, blk), lambda g, j: (0, g, 0)),
                pl.BlockSpec((1, t, LANES), lambda g, j: (g, 0, 0))]
    out_specs = [pl.BlockSpec((t, w), lambda g, j: (0, g)), pl.BlockSpec((blk, w), lambda g, j: (j, g)),
                 pl.BlockSpec((blk, LANES), lambda g, j: (j, g))]
    out_shape = [jax.ShapeDtypeStruct((t, PAIRS * w), F32)] * 2 + [jax.ShapeDtypeStruct((t, PAIRS * LANES), F32)]
    if has_bias:
        in_specs.append(pl.BlockSpec((1, N_HEADS, blk), lambda g, j: (j, 0, 0)))
        out_specs += [pl.BlockSpec((1, N_HEADS, blk), lambda g, j: (g, 0, j)), pl.BlockSpec((1, t, LANES), lambda g, j: (g, 0, 0))]
        out_shape += [jax.ShapeDtypeStruct((PAIRS, N_HEADS, t), F32), jax.ShapeDtypeStruct((PAIRS, t, LANES), F32)]
    s_in_specs, s_out_specs = side.specs() if side else ([], [])
    return pl.pallas_call(
        body, name=name, grid=(PAIRS, nb), in_specs=in_specs + s_in_specs, out_specs=out_specs + s_out_specs,
        out_shape=out_shape + list(s_outs), scratch_shapes=side.sems() if side else [],
        compiler_params=_params(("arbitrary", "arbitrary")),
    )(*ins, *s_ins)


def loss_head(x2, g, tgt, *, name, bt=512):
    t, d = x2.shape
    bt = min(bt, t)

    def body(x_ref, g_ref, t_ref, loss_ref, dx_ref, dxb_ref, dg_ref):
        @pl.when(pl.program_id(0) == 0)
        def _():
            loss_ref[...] = jnp.zeros_like(loss_ref)
            dg_ref[...] = jnp.zeros_like(dg_ref)

        xv = x_ref[...]
        gv = g_ref[...]
        r = lax.rsqrt(jnp.mean(xv * xv, axis=-1, keepdims=True) + EPS)
        xh = xv * r
        e = xh * gv - t_ref[...]
        loss_ref[...] += 0.5 * jnp.sum(jnp.mean(e * e, axis=-1, keepdims=True), axis=0, keepdims=True)
        dy = e * (1.0 / d)
        dg_ref[...] += jnp.sum(dy * xh, axis=0, keepdims=True)
        u = dy * gv
        dx = r * u - xv * (r * r * r * jnp.mean(u * xv, axis=-1, keepdims=True))
        dx_ref[...] = dx
        dxb_ref[...] = dx.astype(BF16)

    row = pl.BlockSpec((bt, d), lambda i: (i, 0))
    vec = pl.BlockSpec((1, d), lambda i: (0, 0))
    return pl.pallas_call(
        body, name=name, grid=(t // bt,), in_specs=[row, vec, row],
        out_specs=[pl.BlockSpec((1, 1), lambda i: (0, 0)), row, row, vec],
        out_shape=[jax.ShapeDtypeStruct((1, 1), F32), jax.ShapeDtypeStruct((t, d), F32),
                   jax.ShapeDtypeStruct((t, d), BF16), jax.ShapeDtypeStruct((1, d), F32)],
        compiler_params=_params(("arbitrary",)),
    )(x2, g, tgt)


def _adamw_math(w, g, m, v):
    nm = ADAM_B1 * m + (1.0 - ADAM_B1) * g
    nv = ADAM_B2 * v + (1.0 - ADAM_B2) * (g * g)
    m_hat = nm / (1.0 - ADAM_B1 ** ADAM_STEP)
    v_hat = nv / (1.0 - ADAM_B2 ** ADAM_STEP)
    return -ADAM_LR * (m_hat / (jnp.sqrt(v_hat) + ADAM_EPS) + ADAM_WD * w), nm, nv


def adamw(w, g, m, v, *, name):
    rws, cols = w.shape
    br = _row_block(rws)

    def body(w_ref, g_ref, m_ref, v_ref, d_ref, nm_ref, nv_ref):
        d_ref[...], nm_ref[...], nv_ref[...] = _adamw_math(w_ref[...], g_ref[...], m_ref[...], v_ref[...])

    blk = pl.BlockSpec((br, cols), lambda i: (i, 0))
    return pl.pallas_call(
        body, name=name, grid=(rws // br,), in_specs=[blk] * 4, out_specs=[blk] * 3,
        out_shape=[jax.ShapeDtypeStruct((rws, cols), F32)] * 3,
        compiler_params=_params(("parallel",)),
    )(w, g, m, v)


def adamw_halves(w, g_mine, g_other, m, v, core, *, name):
    _, k, n = w.shape
    br = _row_block(k // 2)
    nh = k // 2 // br

    def body(c_ref, w_ref, gm_ref, go_ref, m_ref, v_ref, g_out, d_ref, nm_ref, nv_ref):
        gv = jnp.where(pl.program_id(0) == c_ref[0], gm_ref[...], go_ref[...])
        g_out[0] = gv
        d_ref[0], nm_ref[0], nv_ref[0] = _adamw_math(w_ref[0], gv, m_ref[0], v_ref[0])

    full = pl.BlockSpec((1, br, n), lambda hb, i, c: (0, hb * nh + i, 0))
    half = pl.BlockSpec((br, n), lambda hb, i, c: (i, 0))
    return pl.pallas_call(
        body, name=name,
        grid_spec=pltpu.PrefetchScalarGridSpec(num_scalar_prefetch=1, grid=(2, nh), in_specs=[full, half, half, full, full],
                                               out_specs=[full] * 4),
        out_shape=[jax.ShapeDtypeStruct(w.shape, F32)] * 4,
        compiler_params=_params(("parallel", "parallel")),
    )(core, w, g_mine, g_other, m, v)


def add_pair(dw, recv, core, *, name):
    n4, k, n = dw.shape
    hk = k // 2

    def body(c_ref, a_ref, b_ref, o_ref):
        o_ref[...] = (a_ref[...] + b_ref[...].astype(F32)).astype(BF16)

    return pl.pallas_call(
        body, name=name,
        grid_spec=pltpu.PrefetchScalarGridSpec(
            num_scalar_prefetch=1, grid=(n4,),
            in_specs=[pl.BlockSpec((1, hk, n), lambda q, c: (q, c[0], 0)), pl.BlockSpec((1, hk, n), lambda q, c: (q, 0, 0))],
            out_specs=pl.BlockSpec((1, hk, n), lambda q, c: (q, 0, 0))),
        out_shape=jax.ShapeDtypeStruct((n4, hk, n), BF16),
        compiler_params=_params(("parallel",)),
    )(core, dw, recv)


def sum_chips(parts, *, name):
    n4, r, n = parts.shape
    br = _row_block(r)

    def body(p_ref, o_ref):
        acc = p_ref[0].astype(F32)
        for q in range(1, n4):
            acc = acc + p_ref[q].astype(F32)
        o_ref[...] = acc

    return pl.pallas_call(
        body, name=name, grid=(r // br,),
        in_specs=[pl.BlockSpec((n4, br, n), lambda i: (0, i, 0))], out_specs=pl.BlockSpec((br, n), lambda i: (i, 0)),
        out_shape=jax.ShapeDtypeStruct((r, n), F32),
        compiler_params=_params(("parallel",)),
    )(parts)


ANY = pl.BlockSpec(memory_space=pl.ANY)


def _place():
    x, y, c = lax.axis_index("x"), lax.axis_index("y"), lax.axis_index("c")
    chips = [(1 - x, y), (x, 1 - y), (1 - x, 1 - y)]
    return x, y, c, chips


def _copy(src, dst, send_sems, recv_sems, k, to):
    return pltpu.make_async_remote_copy(src_ref=src, dst_ref=dst, send_sem=send_sems.at[k], recv_sem=recv_sems.at[k],
                                        device_id=to, device_id_type=MESH)


def _half_rows(ref, lead, hf):
    hk = ref.shape[1] // 2
    return ref.at[lead, pl.ds(hf * hk, hk), :]


def _gather_first(srcs, dsts, ssems, rsems):
    x, y, c, chips = _place()
    for ti, (s, d) in enumerate(zip(srcs, dsts)):
        hk = s.shape[0] // 2
        for j, (cx, cy) in enumerate(chips):
            _copy(s.at[pl.ds(c * hk, hk), :], _half_rows(d, 2 * x + y, c), ssems, rsems, 3 * ti + j, (cx, cy, c)).start()


def _gather_mid(srcs, dsts, ssems, rsems):
    x, y, c, chips = _place()
    n1 = 3 * len(srcs)
    for ti, d in enumerate(dsts):
        for j, (cx, cy) in enumerate(chips):
            landed = _half_rows(d, 2 * cx + cy, c)
            _copy(landed, landed, ssems, rsems, 3 * ti + j, (cx, cy, c)).wait_recv()
            _copy(landed, landed, ssems, rsems, n1 + 3 * ti + j, (x, y, 1 - c)).start()


def _gather_last(srcs, dsts, ssems, rsems):
    x, y, c, chips = _place()
    n1 = 3 * len(srcs)
    for ti, (s, d) in enumerate(zip(srcs, dsts)):
        hk = s.shape[0] // 2
        for j, (cx, cy) in enumerate(chips):
            other = _half_rows(d, 2 * cx + cy, 1 - c)
            _copy(other, other, ssems, rsems, n1 + 3 * ti + j, (x, y, 1 - c)).wait_recv()
        for j, (cx, cy) in enumerate(chips):
            mine = s.at[pl.ds(c * hk, hk), :]
            _copy(mine, mine, ssems, rsems, 3 * ti + j, (cx, cy, c)).wait_send()
            _copy(mine, mine, ssems, rsems, n1 + 3 * ti + j, (x, y, 1 - c)).wait_send()


def gather_side(shards):
    return Side(shards, [jax.ShapeDtypeStruct((N_CHIPS,) + s.shape, s.dtype) for s in shards], 6 * len(shards),
                _gather_first, _gather_last, _gather_mid)


def _scatter_first(srcs, dsts, ssems, rsems):
    x, y, c, chips = _place()
    for ti, (s, d) in enumerate(zip(srcs, dsts)):
        for j, (cx, cy) in enumerate(chips):
            _copy(s.at[2 * cx + cy], d.at[2 * x + y], ssems, rsems, 3 * ti + j, (cx, cy, c)).start()


def _scatter_last(srcs, dsts, ssems, rsems):
    x, y, c, chips = _place()
    for ti, (s, d) in enumerate(zip(srcs, dsts)):
        for j, (cx, cy) in enumerate(chips):
            _copy(s.at[2 * cx + cy], d.at[2 * cx + cy], ssems, rsems, 3 * ti + j, (cx, cy, c)).wait_recv()
        for j, (cx, cy) in enumerate(chips):
            _copy(s.at[2 * cx + cy], d.at[2 * cx + cy], ssems, rsems, 3 * ti + j, (cx, cy, c)).wait_send()


def scatter_side(parts):
    return Side(parts, [jax.ShapeDtypeStruct(p.shape, p.dtype) for p in parts], 3 * len(parts), _scatter_first, _scatter_last)


def run_side(side, *, name):
    n_in, n_out = len(side.ins), len(side.out_shapes)

    def body(*refs):
        si, so, sems = _split_refs(refs, [n_in, n_out, 2])
        side.first(si, so, *sems)
        if side.mid is not None:
            side.mid(si, so, *sems)
        side.last(si, so, *sems)

    in_specs, out_specs = side.specs()
    return pl.pallas_call(body, name=name, in_specs=in_specs, out_specs=out_specs, out_shape=side.out_shapes,
                          scratch_shapes=side.sems())(*side.ins)


def swap_sibling(xs, *, name):
    n = len(xs)

    def body(*refs):
        srcs, dsts, sems = _split_refs(refs, [n, n, 2])
        x, y, c, _ = _place()
        copies = [_copy(s, d, *sems, k, (x, y, 1 - c)) for k, (s, d) in enumerate(zip(srcs, dsts))]
        for cp in copies:
            cp.start()
        for cp in copies:
            cp.wait()

    return pl.pallas_call(
        body, name=name, in_specs=[ANY] * n, out_specs=[ANY] * n,
        out_shape=[jax.ShapeDtypeStruct(a.shape, a.dtype) for a in xs],
        scratch_shapes=[pltpu.SemaphoreType.DMA((n,)), pltpu.SemaphoreType.DMA((n,))],
    )(*xs)


def allreduce_small(s):
    n_dev = 8

    def body(s_ref, out_ref, buf, send_sems, recv_sems):
        x, y, c, _ = _place()
        me = 4 * x + 2 * y + c
        buf[me] = s_ref[...]
        sends = []
        for k in range(1, n_dev):
            px = 1 - x if k & 4 else x
            py = 1 - y if k & 2 else y
            pc = 1 - c if k & 1 else c
            cp = _copy(s_ref, buf.at[me], send_sems, recv_sems, k - 1, (px, py, pc))
            cp.start()
            sends.append((cp, 4 * px + 2 * py + pc))
        for k, (cp, peer) in enumerate(sends):
            _copy(s_ref, buf.at[peer], send_sems, recv_sems, k, (x, y, c)).wait_recv()
        for cp, _ in sends:
            cp.wait_send()
        acc = buf[0]
        for d in range(1, n_dev):
            acc = acc + buf[d]
        out_ref[...] = acc

    vm = pl.BlockSpec(memory_space=pltpu.VMEM)
    return pl.pallas_call(
        body, name="allreduce_small", in_specs=[vm], out_specs=vm,
        out_shape=jax.ShapeDtypeStruct(s.shape, F32),
        scratch_shapes=[pltpu.VMEM((n_dev,) + s.shape, F32), pltpu.SemaphoreType.DMA((n_dev - 1,)),
                        pltpu.SemaphoreType.DMA((n_dev - 1,))],
    )(s)


def join_cols(sm):
    n4, k, n = sm.shape
    return sm.transpose(1, 0, 2).reshape(k, n4 * n)


def split_cols(full):
    k, n = full.shape
    return full.reshape(k, N_CHIPS, n // N_CHIPS).transpose(1, 0, 2)


def _pad_heads(w, width):
    lead = w.shape[:-1]
    w = w.reshape(lead + (N_HEADS, width))
    return jnp.pad(w, [(0, 0)] * len(lead) + [(0, 0), (0, LANES - width)]).reshape(lead + (HW,))


def _unpad_heads(w, width):
    lead = w.shape[:-1]
    return w.reshape(lead + (N_HEADS, LANES))[..., :width].reshape(lead + (N_HEADS * width,))


def split_w_in(w_in):
    d = w_in.shape[0]
    o_f = 3 * FW
    o_cq = o_f + N_HEADS
    o_ckv = o_cq + Q_RANK
    o_kr = o_ckv + KV_RANK

    def z(n):
        return jnp.zeros((d, n), w_in.dtype)

    small = jnp.concatenate([w_in[:, o_cq:o_ckv], w_in[:, o_ckv:o_kr], z(HEAD_DIM), w_in[:, o_kr:], z(LANES - HEAD_DIM - ROPE_DIM),
                             w_in[:, o_f:o_cq], z(LANES - N_HEADS), z(LANES)], axis=1)
    return w_in[:, :o_f], small


def join_w_in(d_qkv, d_small):
    kr = S_KR + HEAD_DIM
    return jnp.concatenate([d_qkv, d_small[:, S_F:S_F + N_HEADS], d_small[:, S_CQ:S_CKV], d_small[:, S_CKV:S_KR],
                            d_small[:, kr:kr + ROPE_DIM]], axis=1)


def rope_tables(pos):
    t = pos.shape[0]
    inv_freq = ROPE_THETA ** (-jnp.arange(0, ROPE_DIM, 2, dtype=F32) / ROPE_DIM)
    ang = pos.astype(F32)[:, None] * inv_freq
    cos, sin = jnp.cos(ang), jnp.sin(ang)
    half = ROPE_DIM // 2

    def z(n):
        return jnp.zeros((t, n), F32)

    tab_c = jnp.concatenate([jnp.ones((t, HEAD_DIM), F32), cos, cos, z(LANES - HEAD_DIM - ROPE_DIM)], axis=1)
    tab_a = jnp.concatenate([z(HEAD_DIM), -sin, z(half), z(LANES - HEAD_DIM - ROPE_DIM)], axis=1)
    tab_b = jnp.concatenate([z(HEAD_DIM), z(half), sin, z(LANES - HEAD_DIM - ROPE_DIM)], axis=1)
    return tab_c, tab_a, tab_b


def _pad_lanes(v, n):
    return jnp.pad(v, ((0, 0), (0, n - v.shape[1])))


ATTN_BLK = 512
ATTN_FWD_BLK = 1024
ATTN_BWD_QBLK = 1024


def local_step(xs, pos, tgt, gains, w_early, late_weights, fwd_side=None, bwd_side=None):
    g_attn, b_forget, g_q, g_kv, g_fo, g_mo, g_mlp, g_fin = gains
    w_in, w_uq, w_ukv = w_early
    t = xs.shape[0]
    blk = min(ATTN_BLK, t)
    fox_scale = 1.0 / (HEAD_DIM ** 0.5)
    mla_scale = 1.0 / ((HEAD_DIM + ROPE_DIM) ** 0.5)

    w_qkv, w_small = split_w_in(w_in)
    w_uq_p = _pad_heads(w_uq, HEAD_DIM + ROPE_DIM)
    kv = w_ukv.reshape(KV_RANK, N_HEADS, 2 * HEAD_DIM)
    w_ukv_p = jnp.concatenate([_pad_heads(kv[:, :, :HEAD_DIM].reshape(KV_RANK, FW), HEAD_DIM),
                               kv[:, :, HEAD_DIM:].reshape(KV_RANK, FW)], axis=1)
    b_f = _pad_lanes(b_forget, LANES)
    tab_c, tab_a, tab_b = rope_tables(pos)

    h1 = rmsnorm(xs, g_attn, out_dtype=BF16, name="norm_attn")
    qkv, = mm(h1, w_qkv, out_dtypes=[BF16], name="proj_qkv")
    small, = mm(h1, w_small, out_dtypes=[F32], name="proj_small")
    mq, mk, mv, lf, cqn, ckvn = mla_prep(small, g_q, g_kv, w_uq_p, w_ukv_p, tab_c, tab_a, tab_b, b_f, name="mla_prep")
    f_cum = cumsum_rows(lf, reverse=False, name="gate_cumsum")
    f_blocks = f_cum[:, :N_HEADS].reshape(t // blk, blk, N_HEADS).transpose(0, 2, 1)
    qkv_t, mq_t, mv_t = to_blocks_t(qkv, blk), to_blocks_t(mq, blk), to_blocks_t(mv, blk)
    fo, st_f, *gathered = flash_fwd(qkv_t, qkv, qkv_t, f_cum, qoff=0, koff=PAIRS, voff=2 * PAIRS, pair=True,
                                    scale=fox_scale, name="fox_fwd", blk=ATTN_FWD_BLK, side=fwd_side)
    w_o, w_up, w_down = late_weights(gathered)
    mo, st_m = flash_fwd(mq_t, mk, mv_t, None, qoff=0, koff=0, voff=0, pair=False, scale=mla_scale, name="mla_fwd",
                         blk=ATTN_FWD_BLK)
    mixed = mix_norm(fo, mo, g_fo, g_mo, name="norm_mix")

    def add_res(acc, res):
        return (acc + res,)

    x1, = mm(mixed, w_o, extras=[xs], epilogue=add_res, out_dtypes=[F32], name="out_proj")
    h2 = rmsnorm(x1, g_mlp, out_dtype=BF16, name="norm_mlp")

    def relu2(acc):
        r = jnp.maximum(acc, 0.0)
        return acc, r * r

    u, act = mm(h2, w_up, epilogue=relu2, out_dtypes=[BF16, BF16], name="mlp_up")
    x2, = mm(act, w_down, extras=[x1], epilogue=add_res, out_dtypes=[F32], name="mlp_down")
    loss, dx2, dx2b, dg_fin = loss_head(x2, g_fin, tgt, name="loss_head")

    def relu2_grad(acc, uu):
        return (acc * (2.0 * jnp.maximum(uu.astype(F32), 0.0)),)

    du, = mm(dx2b, w_down, trans_b=True, extras=[u], epilogue=relu2_grad, out_dtypes=[BF16], name="mlp_down_bwd")
    dw_down = mm_tn(act, dx2b, name="dw_down").reshape(N_CHIPS, -1, w_down.shape[1])
    dh2, = mm(du, w_up, trans_b=True, out_dtypes=[F32], name="mlp_up_bwd")
    dw_up = mm_tn(h2, du, name="dw_up", col_shards=N_CHIPS)
    dx1, dx1b, dg_mlp = rmsnorm_bwd(dh2, x1, g_mlp, dx2, out_dtypes=[F32, BF16], name="norm_mlp_bwd")

    dmixed, = mm(dx1b, w_o, trans_b=True, out_dtypes=[F32], name="out_proj_bwd")
    dw_o = mm_tn(mixed, dx1b, name="dw_o").reshape(N_CHIPS, -1, w_o.shape[1])
    side = bwd_side(dw_o, dw_up, dw_down) if bwd_side is not None else None
    dfo, dmo, dg_fo, dg_mo, st_f, st_m = mix_norm_bwd(dmixed, fo, mo, g_fo, g_mo, st_f, st_m, name="norm_mix_bwd")
    dfq, dfk, dfv, df_key, df_query, *scattered = flash_bwd(
        qkv, qkv_t, qkv, qkv, dfo, to_blocks_t(dfo, blk), st_f, f_blocks, qoff=0, koff=PAIRS, voff=2 * PAIRS, pair=True,
        scale=fox_scale, name="fox_bwd", qblk=ATTN_BWD_QBLK, side=side)
    dmq, dmk, dmv = flash_bwd(mq, mq_t, mk, mv, dmo, to_blocks_t(dmo, blk), st_m, None, qoff=0, koff=0, voff=0, pair=False,
                              scale=mla_scale, name="mla_bwd", qblk=ATTN_BWD_QBLK)
    dqkv = jnp.concatenate([dfq, dfk, dfv], axis=1).astype(BF16)
    d_f = df_key[:, :2, :].reshape(N_HEADS, t).T + df_query[:, :, :2].transpose(1, 0, 2).reshape(t, N_HEADS)
    dlf = cumsum_rows(_pad_lanes(d_f, LANES), reverse=True, name="gate_cumsum_bwd")
    dsmall, dq_u, dkv_u, dg_q, dg_kv, db_f = mla_prep_bwd(dmq, dmk, dmv, dlf, small, g_q, g_kv, w_uq_p, w_ukv_p,
                                                          tab_c, tab_a, tab_b, b_f, name="mla_prep_bwd")
    dw_uq_p = mm_tn(cqn, dq_u, name="dw_uq")
    dw_ukv_p = mm_tn(ckvn, dkv_u, name="dw_ukv")

    dh1a, = mm(dqkv, w_qkv, trans_b=True, out_dtypes=[F32], name="proj_qkv_bwd")
    dh1, = mm(dsmall, w_small, trans_b=True, extras=[dh1a], epilogue=add_res, out_dtypes=[F32], name="proj_small_bwd")
    dw_qkv = mm_tn(h1, dqkv, name="dw_qkv")
    dw_small = mm_tn(h1, dsmall, name="dw_small")
    grad_x, dg_attn = rmsnorm_bwd(dh1, xs, g_attn, dx1, out_dtypes=[F32], name="norm_attn_bwd")

    dw_in = join_w_in(dw_qkv, dw_small)
    dw_uq = _unpad_heads(dw_uq_p, HEAD_DIM + ROPE_DIM)
    dk_cols = _unpad_heads(dw_ukv_p[:, :HW], HEAD_DIM).reshape(KV_RANK, N_HEADS, HEAD_DIM)
    dv_cols = dw_ukv_p[:, HW:].reshape(KV_RANK, N_HEADS, HEAD_DIM)
    dw_ukv = jnp.concatenate([dk_cols, dv_cols], axis=2).reshape(KV_RANK, N_HEADS * 2 * HEAD_DIM)
    d_gains = (dg_attn, db_f[:, :N_HEADS], dg_q, dg_kv, dg_fo, dg_mo, dg_mlp, dg_fin)
    return loss, grad_x, (dw_in, dw_uq, dw_ukv), (dw_o, dw_up, dw_down), d_gains, scattered


def kernel(x, positions, attn_norm_g, w_in, b_forget, q_norm_g, w_uq, kv_norm_g, w_ukv, fox_out_g, mla_out_g, w_o, mlp_norm_g, w_up, w_down, final_norm_g, loss_target, m_attn_norm_g, m_w_in, m_b_forget, m_q_norm_g, m_w_uq, m_kv_norm_g, m_w_ukv, m_fox_out_g, m_mla_out_g, m_w_o, m_mlp_norm_g, m_w_up, m_w_down, m_final_norm_g, v_attn_norm_g, v_w_in, v_b_forget, v_q_norm_g, v_w_uq, v_kv_norm_g, v_w_ukv, v_fox_out_g, v_mla_out_g, v_w_o, v_mlp_norm_g, v_w_up, v_w_down, v_final_norm_g):
    core_id = lax.axis_index("c")
    core = core_id.reshape(1).astype(jnp.int32)
    chip = 2 * lax.axis_index("x") + lax.axis_index("y")
    big = [w_in, w_uq, w_ukv, w_o, w_up, w_down]
    big_m = [m_w_in, m_w_uq, m_w_ukv, m_w_o, m_w_up, m_w_down]
    big_v = [v_w_in, v_w_uq, v_w_ukv, v_w_o, v_w_up, v_w_down]
    n_early = 3

    def vec(a):
        return a.reshape(1, -1)

    small = [attn_norm_g, b_forget, q_norm_g, kv_norm_g, fox_out_g, mla_out_g, mlp_norm_g, final_norm_g]
    small_m = [m_attn_norm_g, m_b_forget, m_q_norm_g, m_kv_norm_g, m_fox_out_g, m_mla_out_g, m_mlp_norm_g, m_final_norm_g]
    small_v = [v_attn_norm_g, v_b_forget, v_q_norm_g, v_kv_norm_g, v_fox_out_g, v_mla_out_g, v_mlp_norm_g, v_final_norm_g]
    gains = [vec(a) for a in small]

    shards = [w[0].astype(BF16) for w in big]

    def with_own(gathered, mine):
        return [lax.dynamic_update_index_in_dim(g, s, chip, 0) for g, s in zip(gathered, mine)]

    early = with_own(run_side(gather_side(shards[:n_early]), name="gather_early"), shards[:n_early])
    w_early = [join_cols(g) for g in early]

    def late_weights(gathered):
        g_o, g_up, g_down = with_own(gathered, shards[n_early:])
        return g_o.reshape(-1, g_o.shape[2]), join_cols(g_up), g_down.reshape(-1, g_down.shape[2])

    def pair_sums(grads, name):
        sends = [lax.dynamic_slice_in_dim(g, (1 - core_id) * (g.shape[1] // 2), g.shape[1] // 2, axis=1).astype(BF16)
                 for g in grads]
        return sends

    def add_pairs(grads, recvs, name):
        return [add_pair(g, r, core, name="%s_%d" % (name, n)) for n, (g, r) in enumerate(zip(grads, recvs))]

    def chip_sums(scattered, pairs, name):
        with_mine = [lax.dynamic_update_index_in_dim(s, lax.dynamic_index_in_dim(p, chip, 0, keepdims=True), chip, 0)
                     for s, p in zip(scattered, pairs)]
        return [sum_chips(s, name="%s_%d" % (name, n)) for n, s in enumerate(with_mine)]

    late_pairs = []

    def bwd_side(dw_o, dw_up, dw_down):
        grads = [dw_o, dw_up, dw_down]
        recvs = swap_sibling(pair_sums(grads, "late"), name="swap_late")
        late_pairs.extend(add_pairs(grads, recvs, "add_pair_late"))
        return scatter_side(late_pairs)

    loss, grad_x, d_early, _, d_small, scattered_late = local_step(
        x[0], positions[0], loss_target[0], gains, w_early, late_weights, gather_side(shards[n_early:]), bwd_side)

    halves_late = chip_sums(scattered_late, late_pairs, "sum_chips_late")
    g_early = [split_cols(d) for d in d_early]
    n_late = len(halves_late)
    swapped = swap_sibling(halves_late + pair_sums(g_early, "early"), name="swap_mixed")
    others_late, recvs_early = swapped[:n_late], swapped[n_late:]
    early_pairs = add_pairs(g_early, recvs_early, "add_pair_early")
    scattered_early = run_side(scatter_side(early_pairs), name="scatter_early")
    halves_early = chip_sums(scattered_early, early_pairs, "sum_chips_early")
    others_early = swap_sibling(halves_early, name="swap_early")
    halves = halves_early + halves_late
    others = list(others_early) + list(others_late)

    def rows8(vs):
        return jnp.concatenate([_pad_lanes(vec(a).astype(F32), 1024) for a in vs], axis=0)

    g_small8 = allreduce_small(rows8(d_small))

    outs_big = [adamw_halves(w, gm, go, m, v, core, name="adamw_%d" % n)
                for n, (w, gm, go, m, v) in enumerate(zip(big, halves, others, big_m, big_v))]
    d8, m8, v8 = adamw(rows8(small), g_small8, rows8(small_m), rows8(small_v), name="adamw_small")

    def unrows8(a8):
        return [a8[n, :s.size].reshape(s.shape) for n, s in enumerate(small)]

    loss_all = lax.psum(loss[0, 0], ("x", "y", "c"))
    grads, deltas, new_m, new_v = [None] * 14, [None] * 14, [None] * 14, [None] * 14
    big_at = [1, 4, 6, 9, 11, 12]
    small_at = [0, 2, 3, 5, 7, 8, 10, 13]
    for n, at in enumerate(big_at):
        grads[at], deltas[at], new_m[at], new_v[at] = outs_big[n]
    for at, g, dd, mm_, vv in zip(small_at, unrows8(g_small8), unrows8(d8), unrows8(m8), unrows8(v8)):
        grads[at], deltas[at], new_m[at], new_v[at] = g, dd, mm_, vv
    return (loss_all, grad_x[None], *grads, *deltas, *new_m, *new_v)
```

```python
import jax
import jax.numpy as jnp
from jax import lax
from jax.experimental import pallas as pl
from jax.experimental.pallas import tpu as pltpu

F32 = jnp.float32
BF16 = jnp.bfloat16
MESH = pl.DeviceIdType.MESH

EPS = 1e-6
ROPE_THETA = 10000.0
N_HEADS = 8
PAIRS = N_HEADS // 2
HEAD_DIM = 64
ROPE_DIM = 32
LANES = 128
Q_RANK = 384
KV_RANK = 256
N_CHIPS = 4
ADAM_LR, ADAM_B1, ADAM_B2, ADAM_EPS, ADAM_WD, ADAM_STEP = 0.001, 0.9, 0.999, 1e-08, 0.01, 10
VMEM_LIMIT = 48 * 1024 * 1024
LOG2E = 1.4426950408889634
LN2 = 0.6931471805599453
NN = (((1,), (0,)), ((), ()))
NT = (((1,), (1,)), ((), ()))
TN = (((0,), (0,)), ((), ()))


def _params(sem=None):
    return pltpu.CompilerParams(dimension_semantics=sem, vmem_limit_bytes=VMEM_LIMIT)


def _fit(block, dim):
    if dim <= block:
        return dim
    return next(b for b in range(block - block % LANES, 0, -LANES) if dim % b == 0)


def _row_block(rows):
    return next(b for b in (256, 128, 64, 32, 16, 8) if rows % b == 0)


def rmsnorm(x, g, *, out_dtype, name, bt=512):
    t, d = x.shape
    bt = min(bt, t)

    def body(x_ref, g_ref, o_ref):
        xv = x_ref[...].astype(F32)
        r = lax.rsqrt(jnp.mean(xv * xv, axis=-1, keepdims=True) + EPS)
        o_ref[...] = (xv * r * g_ref[...]).astype(o_ref.dtype)

    return pl.pallas_call(
        body, name=name, grid=(t // bt,),
        in_specs=[pl.BlockSpec((bt, d), lambda i: (i, 0)), pl.BlockSpec((1, d), lambda i: (0, 0))],
        out_specs=pl.BlockSpec((bt, d), lambda i: (i, 0)),
        out_shape=jax.ShapeDtypeStruct((t, d), out_dtype),
        compiler_params=_params(("parallel",)),
    )(x, g)


def mm(a, b, *, trans_b=False, extras=(), vecs=(), epilogue=None, out_dtypes, n_sums=0, name, bm=1024, bn=1024):
    a_list = list(a) if isinstance(a, (list, tuple)) else [a]
    b_list = list(b) if isinstance(b, (list, tuple)) else [b]
    m = a_list[0].shape[0]
    n = b_list[0].shape[0] if trans_b else b_list[0].shape[1]
    ks = [x.shape[1] for x in a_list]
    if sum(ks) > 2048:
        bm = bm // 2
    bm, bn = _fit(bm, m), _fit(bn, n)
    assert n_sums == 0 or bn == n
    n_ab, n_ex, n_vec, n_out = len(a_list), len(extras), len(vecs), len(out_dtypes)

    def body(*refs):
        a_refs, b_refs, ex, vs, outs, sums = _split_refs(refs, [n_ab, n_ab, n_ex, n_vec, n_out, n_sums])
        acc = None
        for a_ref, b_ref in zip(a_refs, b_refs):
            part = lax.dot_general(a_ref[...], b_ref[...], NT if trans_b else NN, preferred_element_type=F32)
            acc = part if acc is None else acc + part
        res = epilogue(acc, *[e[...] for e in ex], *[v[...] for v in vs]) if epilogue is not None else (acc,)
        for o, r in zip(outs, res[:n_out]):
            o[...] = r.astype(o.dtype)
        if n_sums:
            @pl.when(pl.program_id(0) == 0)
            def _():
                for s_ref in sums:
                    s_ref[...] = jnp.zeros_like(s_ref)

            for s_ref, r in zip(sums, res[n_out:]):
                s_ref[...] += r

    tile = pl.BlockSpec((bm, bn), lambda i, j: (i, j))
    vec = pl.BlockSpec((1, bn), lambda i, j: (0, j))
    a_specs = [pl.BlockSpec((bm, k), lambda i, j: (i, 0)) for k in ks]
    b_specs = [pl.BlockSpec((bn, k), lambda i, j: (j, 0)) if trans_b else pl.BlockSpec((k, bn), lambda i, j: (0, j)) for k in ks]
    return pl.pallas_call(
        body, name=name, grid=(m // bm, n // bn),
        in_specs=a_specs + b_specs + [tile] * n_ex + [vec] * n_vec,
        out_specs=[tile] * n_out + [vec] * n_sums,
        out_shape=[jax.ShapeDtypeStruct((m, n), dt) for dt in out_dtypes] + [jax.ShapeDtypeStruct((1, n), F32)] * n_sums,
        compiler_params=_params(("arbitrary", "arbitrary") if n_sums else ("parallel", "parallel")),
    )(*a_list, *b_list, *extras, *vecs)


def mm_tn(a, b, *, name, col_shards=1, bk=1024, bn=1024, bt=1024):
    t, k = a.shape
    n = b.shape[1]
    ns = n // col_shards
    bk, bn, bt = _fit(bk, k), _fit(bn, ns), _fit(bt, t)
    per = ns // bn

    def body(a_ref, b_ref, o_ref):
        @pl.when(pl.program_id(2) == 0)
        def _():
            o_ref[...] = jnp.zeros_like(o_ref)

        o_ref[...] += lax.dot_general(a_ref[...], b_ref[...], TN, preferred_element_type=F32)

    if col_shards == 1:
        out_spec = pl.BlockSpec((bk, bn), lambda i, j, s: (i, j))
        out_shape = jax.ShapeDtypeStruct((k, n), F32)
    else:
        out_spec = pl.BlockSpec((None, bk, bn), lambda i, j, s: (j // per, i, j % per))
        out_shape = jax.ShapeDtypeStruct((col_shards, k, ns), F32)
    return pl.pallas_call(
        body, name=name, grid=(k // bk, n // bn, t // bt),
        in_specs=[pl.BlockSpec((bt, bk), lambda i, j, s: (s, i)), pl.BlockSpec((bt, bn), lambda i, j, s: (s, j))],
        out_specs=out_spec, out_shape=out_shape,
        compiler_params=_params(("parallel", "parallel", "arbitrary")),
    )(a, b)


def _split3(x):
    hi = x.astype(BF16)
    r1 = x - hi.astype(F32)
    mid = r1.astype(BF16)
    lo = (r1 - mid.astype(F32)).astype(BF16)
    return hi, mid, lo


def cumsum_rows(x, *, reverse, name, bc=512):
    t, d = x.shape
    bc = min(bc, t)
    nb = t // bc

    def body(x_ref, o_ref, carry):
        @pl.when(pl.program_id(0) == 0)
        def _():
            carry[...] = jnp.zeros_like(carry)

        r = lax.broadcasted_iota(jnp.int32, (bc, bc), 0)
        c = lax.broadcasted_iota(jnp.int32, (bc, bc), 1)
        tri = jnp.where((r <= c) if reverse else (r >= c), 1.0, 0.0).astype(BF16)
        hi, mid, lo = _split3(x_ref[...])
        s = (lax.dot_general(tri, hi, NN, preferred_element_type=F32)
             + lax.dot_general(tri, mid, NN, preferred_element_type=F32)
             + lax.dot_general(tri, lo, NN, preferred_element_type=F32)) + carry[0:1, :]
        o_ref[...] = s
        carry[0:1, :] = s[0:1, :] if reverse else s[bc - 1:bc, :]

    imap = (lambda i: (nb - 1 - i, 0)) if reverse else (lambda i: (i, 0))
    return pl.pallas_call(
        body, name=name, grid=(nb,),
        in_specs=[pl.BlockSpec((bc, d), imap)], out_specs=pl.BlockSpec((bc, d), imap),
        out_shape=jax.ShapeDtypeStruct((t, d), F32),
        scratch_shapes=[pltpu.VMEM((8, d), F32)],
        compiler_params=_params(("arbitrary",)),
    )(x)


def _rope(x, c, a, b):
    return x * c + pltpu.roll(x, LANES - ROPE_DIM // 2, 1) * a + pltpu.roll(x, ROPE_DIM // 2, 1) * b


def _rope_bwd(d, c, a, b):
    return d * c + pltpu.roll(d * a, ROPE_DIM // 2, 1) + pltpu.roll(d * b, LANES - ROPE_DIM // 2, 1)


S_CQ, S_CKV, S_KR, S_F, S_END = 0, Q_RANK, Q_RANK + KV_RANK, Q_RANK + KV_RANK + LANES, 1024
HW = N_HEADS * LANES
FW = N_HEADS * HEAD_DIM


def mla_prep(small, g_q, g_kv, w_uq, w_ukv, tab_c, tab_a, tab_b, b_f, *, name, bt=512):
    t = small.shape[0]
    bt = min(bt, t)

    def body(s_ref, gq_ref, gkv_ref, wq_ref, wkv_ref, c_ref, a_ref, b_ref, bf_ref,
             mq_ref, mk_ref, mv_ref, lf_ref, cqn_ref, ckvn_ref):
        cq = s_ref[:, S_CQ:S_CKV]
        rq = lax.rsqrt(jnp.mean(cq * cq, axis=-1, keepdims=True) + EPS)
        cqn = (cq * rq * gq_ref[...]).astype(BF16)
        ckv = s_ref[:, S_CKV:S_KR]
        rkv = lax.rsqrt(jnp.mean(ckv * ckv, axis=-1, keepdims=True) + EPS)
        ckvn = (ckv * rkv * gkv_ref[...]).astype(BF16)
        cqn_ref[...] = cqn
        ckvn_ref[...] = ckvn
        tc, ta, tb = c_ref[...], a_ref[...], b_ref[...]
        q = jnp.dot(cqn, wq_ref[...], preferred_element_type=F32)
        kv = jnp.dot(ckvn, wkv_ref[...], preferred_element_type=F32)
        kr = _rope(s_ref[:, S_KR:S_F], tc, ta, tb)
        for h in range(N_HEADS):
            sl = slice(h * LANES, (h + 1) * LANES)
            mq_ref[:, sl] = _rope(q[:, sl], tc, ta, tb).astype(BF16)
            mk_ref[:, sl] = (kv[:, sl] + kr).astype(BF16)
        mv_ref[...] = kv[:, HW:].astype(BF16)
        z = s_ref[:, S_F:S_END - LANES] + bf_ref[...]
        lf_ref[...] = jnp.minimum(z, 0.0) - jnp.log(1.0 + jnp.exp(-jnp.abs(z)))

    def row(w):
        return pl.BlockSpec((bt, w), lambda i: (i, 0))

    def full(arr):
        return pl.BlockSpec(arr.shape, lambda i: (0, 0))

    return pl.pallas_call(
        body, name=name, grid=(t // bt,),
        in_specs=[row(S_END), full(g_q), full(g_kv), full(w_uq), full(w_ukv), row(LANES), row(LANES), row(LANES), full(b_f)],
        out_specs=[row(HW), row(HW), row(FW), row(LANES), row(Q_RANK), row(KV_RANK)],
        out_shape=[jax.ShapeDtypeStruct((t, HW), BF16)] * 2 + [jax.ShapeDtypeStruct((t, FW), BF16), jax.ShapeDtypeStruct((t, LANES), F32),
                   jax.ShapeDtypeStruct((t, Q_RANK), BF16), jax.ShapeDtypeStruct((t, KV_RANK), BF16)],
        compiler_params=_params(("parallel",)),
    )(small, g_q, g_kv, w_uq, w_ukv, tab_c, tab_a, tab_b, b_f)


def mla_prep_bwd(dmq, dmk, dmv, dlf, small, g_q, g_kv, w_uq, w_ukv, tab_c, tab_a, tab_b, b_f, *, name, bt=512):
    t = small.shape[0]
    bt = min(bt, t)

    def body(dmq_ref, dmk_ref, dmv_ref, dlf_ref, s_ref, gq_ref, gkv_ref, wq_ref, wkv_ref, c_ref, a_ref, b_ref, bf_ref,
             ds_ref, dq_ref, dkv_ref, dgq_ref, dgkv_ref, db_ref):
        tc, ta, tb = c_ref[...], a_ref[...], b_ref[...]
        lane = lax.broadcasted_iota(jnp.int32, (1, LANES), 1)
        dkr = jnp.zeros((bt, LANES), F32)
        for h in range(N_HEADS):
            sl = slice(h * LANES, (h + 1) * LANES)
            dq_ref[:, sl] = _rope_bwd(dmq_ref[:, sl], tc, ta, tb).astype(BF16)
            dkr = dkr + dmk_ref[:, sl]
        dkv_ref[:, :HW] = dmk_ref[...].astype(BF16)
        dkv_ref[:, HW:] = dmv_ref[...].astype(BF16)
        in_rope = (lane >= HEAD_DIM) & (lane < HEAD_DIM + ROPE_DIM)
        ds_ref[:, S_KR:S_F] = jnp.where(in_rope, _rope_bwd(dkr, tc, ta, tb), 0.0).astype(BF16)

        def norm_bwd(raw, g_ref, dn, dg_ref):
            r = lax.rsqrt(jnp.mean(raw * raw, axis=-1, keepdims=True) + EPS)
            u = dn * g_ref[...]
            dot = jnp.mean(u * raw, axis=-1, keepdims=True)
            dg_ref[...] += jnp.sum(dn * (raw * r), axis=0, keepdims=True)
            return r * u - raw * (r * r * r * dot)

        @pl.when(pl.program_id(0) == 0)
        def _():
            dgq_ref[...] = jnp.zeros_like(dgq_ref)
            dgkv_ref[...] = jnp.zeros_like(dgkv_ref)
            db_ref[...] = jnp.zeros_like(db_ref)

        dcqn = lax.dot_general(dq_ref[...], wq_ref[...], NT, preferred_element_type=F32)
        ds_ref[:, S_CQ:S_CKV] = norm_bwd(s_ref[:, S_CQ:S_CKV], gq_ref, dcqn, dgq_ref).astype(BF16)
        dckvn = lax.dot_general(dkv_ref[...], wkv_ref[...], NT, preferred_element_type=F32)
        ds_ref[:, S_CKV:S_KR] = norm_bwd(s_ref[:, S_CKV:S_KR], gkv_ref, dckvn, dgkv_ref).astype(BF16)
        z = s_ref[:, S_F:S_END - LANES] + bf_ref[...]
        dz = jnp.where(lane < N_HEADS, dlf_ref[...] / (1.0 + jnp.exp(z)), 0.0)
        db_ref[...] += jnp.sum(dz, axis=0, keepdims=True)
        ds_ref[:, S_F:S_END - LANES] = dz.astype(BF16)
        ds_ref[:, S_END - LANES:] = jnp.zeros((bt, LANES), BF16)

    def row(w):
        return pl.BlockSpec((bt, w), lambda i: (i, 0))

    def full(arr):
        return pl.BlockSpec(arr.shape, lambda i: (0, 0))

    def vec(w):
        return pl.BlockSpec((1, w), lambda i: (0, 0))

    return pl.pallas_call(
        body, name=name, grid=(t // bt,),
        in_specs=[row(HW), row(HW), row(FW), row(LANES), row(S_END), full(g_q), full(g_kv), full(w_uq), full(w_ukv),
                  row(LANES), row(LANES), row(LANES), full(b_f)],
        out_specs=[row(S_END), row(HW), row(HW + FW), vec(Q_RANK), vec(KV_RANK), vec(LANES)],
        out_shape=[jax.ShapeDtypeStruct((t, S_END), BF16), jax.ShapeDtypeStruct((t, HW), BF16),
                   jax.ShapeDtypeStruct((t, HW + FW), BF16), jax.ShapeDtypeStruct((1, Q_RANK), F32),
                   jax.ShapeDtypeStruct((1, KV_RANK), F32), jax.ShapeDtypeStruct((1, LANES), F32)],
        compiler_params=_params(("arbitrary",)),
    )(dmq, dmk, dmv, dlf, small, g_q, g_kv, w_uq, w_ukv, tab_c, tab_a, tab_b, b_f)


class Side:
    def __init__(self, ins, out_shapes, n_sems, first, last, mid=None):
        self.ins, self.out_shapes, self.n_sems = list(ins), list(out_shapes), n_sems
        self.first, self.mid, self.last = first, mid, last

    def specs(self):
        return [ANY] * len(self.ins), [ANY] * len(self.out_shapes)

    def sems(self):
        return [pltpu.SemaphoreType.DMA((self.n_sems,)), pltpu.SemaphoreType.DMA((self.n_sems,))]


def _lane():
    return lax.broadcasted_iota(jnp.int32, (1, LANES), 1)


def _halves(x):
    zero = jnp.zeros_like(x)
    return [jnp.where(_lane() < HEAD_DIM, x, zero), jnp.where(_lane() >= HEAD_DIM, x, zero)]


def _groups(x):
    return [x[:, :LANES], x[:, LANES:]]


def _pick_row(tile, h):
    row = lax.broadcasted_iota(jnp.int32, (tile.shape[0], 1), 0)
    return jnp.sum(jnp.where(row == h, tile, 0.0), axis=0, keepdims=True)


def _pick_lane(tile, h):
    return jnp.sum(jnp.where(_lane() == h, tile, 0.0), axis=1, keepdims=True)


def _row_halves(x):
    row = lax.broadcasted_iota(jnp.int32, (LANES, 1), 0)
    zero = jnp.zeros_like(x)
    return [jnp.where(row < HEAD_DIM, x, zero), jnp.where(row >= HEAD_DIM, x, zero)]


def _below_diagonal(s, lead=0):
    r = lax.broadcasted_iota(jnp.int32, s.shape, 0)
    c = lax.broadcasted_iota(jnp.int32, s.shape, 1)
    return jnp.where(c <= r + lead, s, -jnp.inf)


def _above_diagonal(s):
    r = lax.broadcasted_iota(jnp.int32, s.shape, 0)
    c = lax.broadcasted_iota(jnp.int32, s.shape, 1)
    return jnp.where(r <= c, s, -jnp.inf)


def to_blocks_t(x, blk):
    t, c = x.shape
    return x.reshape(t // blk, blk, c).transpose(0, 2, 1)


def _split_refs(refs, counts):
    out, at = [], 0
    for n in counts:
        out.append(refs[at:at + n])
        at += n
    return out


def flash_fwd(qt_arr, k_arr, vt_arr, f_cum, *, qoff, koff, voff, pair, scale, name, blk=512, side=None):
    t = k_arr.shape[0]
    tblk = qt_arr.shape[2]
    blk = max(min(blk, t), tblk)
    sub = blk // tblk
    nb = t // blk
    w = LANES if pair else 2 * LANES
    has_bias = f_cum is not None
    ins = [qt_arr, k_arr, vt_arr] + ([f_cum] if has_bias else [])

    def wide(ref, first):
        parts = [ref[first + u] for u in range(sub)]
        return parts[0] if sub == 1 else jnp.concatenate(parts, axis=1)
    s_ins, s_outs = (side.ins, side.out_shapes) if side else ([], [])

    def body(*refs):
        main, si, outs, so, sems = _split_refs(refs, [len(ins), len(s_ins), 2, len(s_outs), 2 if side else 0])
        qt_ref, k_ref, vt_ref = main[:3]
        f_ref = main[3] if has_bias else None
        o_ref, st_ref = outs
        g, i = pl.program_id(0), pl.program_id(1)
        step_id = g * nb + i
        if side:
            @pl.when(step_id == 0)
            def _():
                side.first(si, so, *sems)

            if side.mid is not None:
                @pl.when(step_id == (3 * PAIRS * nb) // 4)
                def _():
                    side.mid(si, so, *sems)

        qt = (wide(qt_ref, 0).astype(F32) * (scale * LOG2E)).astype(BF16)
        qts = _row_halves(qt) if pair else [qt[:LANES], qt[LANES:]]

        def step(j, carry, diagonal):
            rows = pl.ds(pl.multiple_of(j * blk, blk), blk)
            kk = k_ref[rows, :]
            ks = [kk, kk] if pair else _groups(kk)
            vt = wide(vt_ref, sub * j)
            out = []
            for n in range(2):
                m, l, acc = carry[n]
                s = jnp.dot(ks[n], qts[n], preferred_element_type=F32)
                if has_bias:
                    s = s - LOG2E * _pick_lane(f_ref[rows, :], 2 * g + n)
                if diagonal:
                    s = _above_diagonal(s)
                m_new = jnp.maximum(m, jnp.max(s, axis=0, keepdims=True))
                alpha = jnp.exp2(m - m_new)
                p = jnp.exp2(s - m_new)
                out.append((m_new, alpha * l + jnp.sum(p, axis=0, keepdims=True),
                            alpha * acc + jnp.dot(vt[n * HEAD_DIM:(n + 1) * HEAD_DIM], p.astype(BF16),
                                                  preferred_element_type=F32)))
            return tuple(out)

        init = tuple((jnp.full((1, blk), -jnp.inf, F32), jnp.zeros((1, blk), F32), jnp.zeros((HEAD_DIM, blk), F32))
                     for _ in range(2))
        carry = lax.fori_loop(0, i, lambda j, c: step(j, c, False), init)
        (ma, la, acca), (mb, lb, accb) = step(i, carry, True)
        o_ref[...] = jnp.concatenate([acca / la, accb / lb], axis=0).T
        row = lax.broadcasted_iota(jnp.int32, (LANES, 1), 0)
        st_ref[0] = jnp.where(row == 0, ma + jnp.log2(la), jnp.where(row == 1, mb + jnp.log2(lb), 0.0)).T
        if side:
            @pl.when(step_id == PAIRS * nb - 1)
            def _():
                side.last(si, so, *sems)

    in_specs = [pl.BlockSpec((sub, w, tblk), lambda g, i: (i, qoff + g, 0)), pl.BlockSpec((t, w), lambda g, i: (0, koff + g)),
                pl.BlockSpec((t // tblk, LANES, tblk), lambda g, i: (0, voff + g, 0))]
    if has_bias:
        in_specs.append(pl.BlockSpec((t, LANES), lambda g, i: (0, 0)))
    s_in_specs, s_out_specs = side.specs() if side else ([], [])
    return pl.pallas_call(
        body, name=name, grid=(PAIRS, nb), in_specs=in_specs + s_in_specs,
        out_specs=[pl.BlockSpec((blk, LANES), lambda g, i: (i, g)), pl.BlockSpec((1, blk, LANES), lambda g, i: (g, i, 0))]
        + s_out_specs,
        out_shape=[jax.ShapeDtypeStruct((t, PAIRS * LANES), F32), jax.ShapeDtypeStruct((PAIRS, t, LANES), F32)] + list(s_outs),
        scratch_shapes=side.sems() if side else [],
        compiler_params=_params(("arbitrary", "arbitrary")),
    )(*ins, *s_ins)


def mix_norm(fo, mo, g_fo, g_mo, *, name, bt=512):
    t, d = fo.shape
    bt = min(bt, t)

    def body(fo_ref, mo_ref, gf_ref, gm_ref, o_ref):
        for n, (x_ref, g_ref) in enumerate(((fo_ref, gf_ref), (mo_ref, gm_ref))):
            xv = x_ref[...]
            r = lax.rsqrt(jnp.mean(xv * xv, axis=-1, keepdims=True) + EPS)
            o_ref[:, n * d:(n + 1) * d] = (xv * r * g_ref[...]).astype(BF16)

    row = pl.BlockSpec((bt, d), lambda i: (i, 0))
    vec = pl.BlockSpec((1, d), lambda i: (0, 0))
    return pl.pallas_call(
        body, name=name, grid=(t // bt,), in_specs=[row, row, vec, vec],
        out_specs=pl.BlockSpec((bt, 2 * d), lambda i: (i, 0)),
        out_shape=jax.ShapeDtypeStruct((t, 2 * d), BF16),
        compiler_params=_params(("parallel",)),
    )(fo, mo, g_fo, g_mo)


def mix_norm_bwd(dmixed, fo, mo, g_fo, g_mo, st_f, st_m, *, name, bt=512):
    t, d = fo.shape
    bt = min(bt, t)

    def body(dm_ref, fo_ref, mo_ref, gf_ref, gm_ref, sf_ref, sm_ref, dfo_ref, dmo_ref, dgf_ref, dgm_ref, sfo_ref, smo_ref):
        @pl.when(pl.program_id(0) == 0)
        def _():
            dgf_ref[...] = jnp.zeros_like(dgf_ref)
            dgm_ref[...] = jnp.zeros_like(dgm_ref)

        groups = ((fo_ref, gf_ref, dfo_ref, dgf_ref, sf_ref, sfo_ref), (mo_ref, gm_ref, dmo_ref, dgm_ref, sm_ref, smo_ref))
        for n, (x_ref, g_ref, dx_ref, dg_ref, st_ref, sto_ref) in enumerate(groups):
            xv = x_ref[...]
            dhv = dm_ref[:, n * d:(n + 1) * d]
            r = lax.rsqrt(jnp.mean(xv * xv, axis=-1, keepdims=True) + EPS)
            u = dhv * g_ref[...]
            dxb = (r * u - xv * (r * r * r * jnp.mean(u * xv, axis=-1, keepdims=True))).astype(BF16)
            dx_ref[...] = dxb
            dg_ref[...] += jnp.sum(dhv * (xv * r), axis=0, keepdims=True)
            prod = xv * dxb.astype(F32)
            for g in range(PAIRS):
                grp = prod[:, g * LANES:(g + 1) * LANES]
                da = jnp.sum(jnp.where(_lane() < HEAD_DIM, grp, 0.0), axis=1, keepdims=True)
                db = jnp.sum(jnp.where(_lane() >= HEAD_DIM, grp, 0.0), axis=1, keepdims=True)
                sto_ref[g] = jnp.where(_lane() == 2, da, jnp.where(_lane() == 3, db, st_ref[g]))

    row = pl.BlockSpec((bt, d), lambda i: (i, 0))
    vec = pl.BlockSpec((1, d), lambda i: (0, 0))
    stat = pl.BlockSpec((PAIRS, bt, LANES), lambda i: (0, i, 0))
    return pl.pallas_call(
        body, name=name, grid=(t // bt,),
        in_specs=[pl.BlockSpec((bt, 2 * d), lambda i: (i, 0)), row, row, vec, vec, stat, stat],
        out_specs=[row, row, vec, vec, stat, stat],
        out_shape=[jax.ShapeDtypeStruct((t, d), BF16)] * 2 + [jax.ShapeDtypeStruct((1, d), F32)] * 2
        + [jax.ShapeDtypeStruct(st_f.shape, F32)] * 2,
        compiler_params=_params(("arbitrary",)),
    )(dmixed, fo, mo, g_fo, g_mo, st_f, st_m)


def flash_bwd(q_arr, qt_arr, k_arr, v_arr, do_arr, dot_arr, st, f_blocks, *, qoff, koff, voff, pair, scale, name, qblk=1024,
              side=None):
    t = q_arr.shape[0]
    blk = qt_arr.shape[2]
    qblk = max(min(qblk, t), blk)
    sub = qblk // blk
    nb, nbq = t // blk, t // qblk
    w = LANES if pair else 2 * LANES
    hw = w // 2
    has_bias = f_blocks is not None
    split = _halves if pair else _groups
    ins = [q_arr, qt_arr, k_arr, v_arr, do_arr, dot_arr, st] + ([f_blocks] if has_bias else [])
    n_out = 4 if has_bias else 3
    s_ins, s_outs = (side.ins, side.out_shapes) if side else ([], [])

    def wide(ref, first):
        parts = [ref[first + u] for u in range(sub)]
        return parts[0] if sub == 1 else jnp.concatenate(parts, axis=1)

    def body(*refs):
        main, si, outs, so, sems = _split_refs(refs, [len(ins), len(s_ins), n_out, len(s_outs), 2 if side else 0])
        q_ref, qt_ref, k_ref, v_ref, do_ref, dot_ref, st_ref = main[:7]
        dq_ref, dk_ref, dv_ref = outs[:3]
        g, j = pl.program_id(0), pl.program_id(1)
        step_id = g * nb + j
        if side:
            @pl.when(step_id == 0)
            def _():
                side.first(si, so, *sems)

        kk, vv = k_ref[...], v_ref[...]
        ks = [kk, kk] if pair else _groups(kk)
        if has_bias:
            f_ref, df_ref = main[7], outs[3]
            fk = [LOG2E * _pick_row(f_ref[0], 2 * g + n) for n in range(2)]

            @pl.when(step_id == 0)
            def _():
                df_ref[...] = jnp.zeros_like(df_ref)

        def step(i, carry, diagonal):
            rows = pl.ds(pl.multiple_of(i * qblk, qblk), qblk)
            qs = split((q_ref[rows, :].astype(F32) * (scale * LOG2E)).astype(BF16))
            qt = (wide(qt_ref, sub * i).astype(F32) * (scale * LOG2E)).astype(BF16)
            dos = [h.astype(BF16) for h in _halves(do_ref[rows, :].astype(F32))]
            dot = wide(dot_ref, sub * i)
            stats = st_ref[0, rows, :]
            new, dqs, row_sums = [], [], []
            for n in range(2):
                dkt, dvt, dfk = carry[n]
                s = lax.dot_general(qs[n], ks[n], NT, preferred_element_type=F32)
                if has_bias:
                    s = s - fk[n]
                if diagonal:
                    s = _below_diagonal(s, i * qblk - j * blk)
                p = jnp.exp2(s - stats[:, n:n + 1])
                dp = lax.dot_general(dos[n], vv, NT, preferred_element_type=F32)
                ds = p * (dp - stats[:, 2 + n:3 + n])
                dsb = ds.astype(BF16)
                dvt = dvt + jnp.dot(dot[n * HEAD_DIM:(n + 1) * HEAD_DIM], p.astype(BF16), preferred_element_type=F32)
                dkt = dkt + jnp.dot(qt[n * hw:(n + 1) * hw], dsb, preferred_element_type=F32)
                dqs.append(jnp.dot(dsb, ks[n], preferred_element_type=F32))
                if has_bias:
                    dfk = dfk - jnp.sum(ds, axis=0, keepdims=True)
                    row_sums.append(jnp.sum(ds, axis=1, keepdims=True))
                new.append((dkt, dvt, dfk))
            dq = (jnp.where(_lane() < HEAD_DIM, dqs[0], dqs[1]) if pair else jnp.concatenate(dqs, axis=1)) * scale
            if has_bias:
                df_ref[rows, :] += jnp.where(_lane() == 2 * g, row_sums[0], jnp.where(_lane() == 2 * g + 1, row_sums[1], 0.0))

            @pl.when(j == 0)
            def _():
                dq_ref[rows, :] = dq

            @pl.when(j > 0)
            def _():
                dq_ref[rows, :] += dq

            return tuple(new)

        init = tuple((jnp.zeros((hw, blk), F32), jnp.zeros((HEAD_DIM, blk), F32), jnp.zeros((1, blk), F32)) for _ in range(2))
        first = j // sub
        carry = step(first, init, True)
        (dka, dva, dfa), (dkb, dvb, dfb) = lax.fori_loop(first + 1, nbq, lambda i, c: step(i, c, False), carry)
        dk_ref[...] = jnp.concatenate([dka, dkb], axis=0).T * LN2
        dv_ref[...] = jnp.concatenate([dva, dvb], axis=0).T
        if has_bias:
            row = lax.broadcasted_iota(jnp.int32, (LANES, 1), 0)
            by_head = jnp.where(row == 2 * g, dfa, jnp.where(row == 2 * g + 1, dfb, 0.0))
            df_ref[pl.ds(pl.multiple_of(j * blk, blk), blk), :] += by_head.T
        if side:
            @pl.when(step_id == PAIRS * nb - 1)
            def _():
                side.last(si, so, *sems)

    in_specs = [pl.BlockSpec((t, w), lambda g, j: (0, qoff + g)), pl.BlockSpec((nb, w, blk), lambda g, j: (0, qoff + g, 0)),
                pl.BlockSpec((blk, w), lambda g, j: (j, koff + g)), pl.BlockSpec((blk, LANES), lambda g, j: (j, voff + g)),
                pl.BlockSpec((t, LANES), lambda g, j: (0, g)), pl.BlockSpec((nb, LANES, blk), lambda g, j: (0, g, 0)),
                pl.BlockSpec((1, t, LANES), lambda g, j: (g, 0, 0))]
    out_specs = [pl.BlockSpec((t, w), lambda g, j: (0, g)), pl.BlockSpec((blk, w), lambda g, j: (j, g)),
                 pl.BlockSpec((blk, LANES), lambda g, j: (j, g))]
    out_shape = [jax.ShapeDtypeStruct((t, PAIRS * w), F32)] * 2 + [jax.ShapeDtypeStruct((t, PAIRS * LANES), F32)]
    if has_bias:
        in_specs.append(pl.BlockSpec((1, N_HEADS, blk), lambda g, j: (j, 0, 0)))
        out_specs.append(pl.BlockSpec((t, LANES), lambda g, j: (0, 0)))
        out_shape.append(jax.ShapeDtypeStruct((t, LANES), F32))
    s_in_specs, s_out_specs = side.specs() if side else ([], [])
    return pl.pallas_call(
        body, name=name, grid=(PAIRS, nb), in_specs=in_specs + s_in_specs, out_specs=out_specs + s_out_specs,
        out_shape=out_shape + list(s_outs), scratch_shapes=side.sems() if side else [],
        compiler_params=_params(("arbitrary", "arbitrary")),
    )(*ins, *s_ins)


def _adamw_math(w, g, m, v):
    nm = ADAM_B1 * m + (1.0 - ADAM_B1) * g
    nv = ADAM_B2 * v + (1.0 - ADAM_B2) * (g * g)
    m_hat = nm / (1.0 - ADAM_B1 ** ADAM_STEP)
    v_hat = nv / (1.0 - ADAM_B2 ** ADAM_STEP)
    return -ADAM_LR * (m_hat / (jnp.sqrt(v_hat) + ADAM_EPS) + ADAM_WD * w), nm, nv


def adamw(w, g, m, v, *, name):
    rws, cols = w.shape
    br = _row_block(rws)

    def body(w_ref, g_ref, m_ref, v_ref, d_ref, nm_ref, nv_ref):
        d_ref[...], nm_ref[...], nv_ref[...] = _adamw_math(w_ref[...], g_ref[...], m_ref[...], v_ref[...])

    blk = pl.BlockSpec((br, cols), lambda i: (i, 0))
    return pl.pallas_call(
        body, name=name, grid=(rws // br,), in_specs=[blk] * 4, out_specs=[blk] * 3,
        out_shape=[jax.ShapeDtypeStruct((rws, cols), F32)] * 3,
        compiler_params=_params(("parallel",)),
    )(w, g, m, v)


def adamw_halves(w, g_mine, g_other, m, v, core, *, name):
    _, k, n = w.shape
    br = _row_block(k // 2)
    nh = k // 2 // br

    def body(c_ref, w_ref, gm_ref, go_ref, m_ref, v_ref, g_out, d_ref, nm_ref, nv_ref):
        gv = jnp.where(pl.program_id(0) == c_ref[0], gm_ref[...], go_ref[...])
        g_out[0] = gv
        d_ref[0], nm_ref[0], nv_ref[0] = _adamw_math(w_ref[0], gv, m_ref[0], v_ref[0])

    full = pl.BlockSpec((1, br, n), lambda hb, i, c: (0, hb * nh + i, 0))
    half = pl.BlockSpec((br, n), lambda hb, i, c: (i, 0))
    return pl.pallas_call(
        body, name=name,
        grid_spec=pltpu.PrefetchScalarGridSpec(num_scalar_prefetch=1, grid=(2, nh), in_specs=[full, half, half, full, full],
                                               out_specs=[full] * 4),
        out_shape=[jax.ShapeDtypeStruct(w.shape, F32)] * 4,
        compiler_params=_params(("parallel", "parallel")),
    )(core, w, g_mine, g_other, m, v)


def add_pair(dw, recv, core, *, name):
    n4, k, n = dw.shape
    hk = k // 2

    def body(c_ref, a_ref, b_ref, o_ref):
        o_ref[...] = (a_ref[...] + b_ref[...].astype(F32)).astype(BF16)

    return pl.pallas_call(
        body, name=name,
        grid_spec=pltpu.PrefetchScalarGridSpec(
            num_scalar_prefetch=1, grid=(n4,),
            in_specs=[pl.BlockSpec((1, hk, n), lambda q, c: (q, c[0], 0)), pl.BlockSpec((1, hk, n), lambda q, c: (q, 0, 0))],
            out_specs=pl.BlockSpec((1, hk, n), lambda q, c: (q, 0, 0))),
        out_shape=jax.ShapeDtypeStruct((n4, hk, n), BF16),
        compiler_params=_params(("parallel",)),
    )(core, dw, recv)


def sum_chips(parts, *, name):
    n4, r, n = parts.shape
    br = _row_block(r)

    def body(p_ref, o_ref):
        acc = p_ref[0].astype(F32)
        for q in range(1, n4):
            acc = acc + p_ref[q].astype(F32)
        o_ref[...] = acc

    return pl.pallas_call(
        body, name=name, grid=(r // br,),
        in_specs=[pl.BlockSpec((n4, br, n), lambda i: (0, i, 0))], out_specs=pl.BlockSpec((br, n), lambda i: (i, 0)),
        out_shape=jax.ShapeDtypeStruct((r, n), F32),
        compiler_params=_params(("parallel",)),
    )(parts)


ANY = pl.BlockSpec(memory_space=pl.ANY)


def _place():
    x, y, c = lax.axis_index("x"), lax.axis_index("y"), lax.axis_index("c")
    chips = [(1 - x, y), (x, 1 - y), (1 - x, 1 - y)]
    return x, y, c, chips


def _copy(src, dst, send_sems, recv_sems, k, to):
    return pltpu.make_async_remote_copy(src_ref=src, dst_ref=dst, send_sem=send_sems.at[k], recv_sem=recv_sems.at[k],
                                        device_id=to, device_id_type=MESH)


def _half_rows(ref, lead, hf):
    hk = ref.shape[1] // 2
    return ref.at[lead, pl.ds(hf * hk, hk), :]


def _gather_first(srcs, dsts, ssems, rsems):
    x, y, c, chips = _place()
    for ti, (s, d) in enumerate(zip(srcs, dsts)):
        hk = s.shape[0] // 2
        for j, (cx, cy) in enumerate(chips):
            _copy(s.at[pl.ds(c * hk, hk), :], _half_rows(d, 2 * x + y, c), ssems, rsems, 3 * ti + j, (cx, cy, c)).start()


def _gather_mid(srcs, dsts, ssems, rsems):
    x, y, c, chips = _place()
    n1 = 3 * len(srcs)
    for ti, d in enumerate(dsts):
        for j, (cx, cy) in enumerate(chips):
            landed = _half_rows(d, 2 * cx + cy, c)
            _copy(landed, landed, ssems, rsems, 3 * ti + j, (cx, cy, c)).wait_recv()
            _copy(landed, landed, ssems, rsems, n1 + 3 * ti + j, (x, y, 1 - c)).start()


def _gather_last(srcs, dsts, ssems, rsems):
    x, y, c, chips = _place()
    n1 = 3 * len(srcs)
    for ti, (s, d) in enumerate(zip(srcs, dsts)):
        hk = s.shape[0] // 2
        for j, (cx, cy) in enumerate(chips):
            other = _half_rows(d, 2 * cx + cy, 1 - c)
            _copy(other, other, ssems, rsems, n1 + 3 * ti + j, (x, y, 1 - c)).wait_recv()
        for j, (cx, cy) in enumerate(chips):
            mine = s.at[pl.ds(c * hk, hk), :]
            _copy(mine, mine, ssems, rsems, 3 * ti + j, (cx, cy, c)).wait_send()
            _copy(mine, mine, ssems, rsems, n1 + 3 * ti + j, (x, y, 1 - c)).wait_send()


def gather_side(shards):
    return Side(shards, [jax.ShapeDtypeStruct((N_CHIPS,) + s.shape, s.dtype) for s in shards], 6 * len(shards),
                _gather_first, _gather_last, _gather_mid)


def _scatter_first(srcs, dsts, ssems, rsems):
    x, y, c, chips = _place()
    for ti, (s, d) in enumerate(zip(srcs, dsts)):
        for j, (cx, cy) in enumerate(chips):
            _copy(s.at[2 * cx + cy], d.at[2 * x + y], ssems, rsems, 3 * ti + j, (cx, cy, c)).start()


def _scatter_last(srcs, dsts, ssems, rsems):
    x, y, c, chips = _place()
    for ti, (s, d) in enumerate(zip(srcs, dsts)):
        for j, (cx, cy) in enumerate(chips):
            _copy(s.at[2 * cx + cy], d.at[2 * cx + cy], ssems, rsems, 3 * ti + j, (cx, cy, c)).wait_recv()
        for j, (cx, cy) in enumerate(chips):
            _copy(s.at[2 * cx + cy], d.at[2 * cx + cy], ssems, rsems, 3 * ti + j, (cx, cy, c)).wait_send()


def scatter_side(parts):
    return Side(parts, [jax.ShapeDtypeStruct(p.shape, p.dtype) for p in parts], 3 * len(parts), _scatter_first, _scatter_last)


def run_side(side, *, name):
    n_in, n_out = len(side.ins), len(side.out_shapes)

    def body(*refs):
        si, so, sems = _split_refs(refs, [n_in, n_out, 2])
        side.first(si, so, *sems)
        if side.mid is not None:
            side.mid(si, so, *sems)
        side.last(si, so, *sems)

    in_specs, out_specs = side.specs()
    return pl.pallas_call(body, name=name, in_specs=in_specs, out_specs=out_specs, out_shape=side.out_shapes,
                          scratch_shapes=side.sems())(*side.ins)


def swap_sibling(xs, *, name):
    n = len(xs)

    def body(*refs):
        srcs, dsts, sems = _split_refs(refs, [n, n, 2])
        x, y, c, _ = _place()
        copies = [_copy(s, d, *sems, k, (x, y, 1 - c)) for k, (s, d) in enumerate(zip(srcs, dsts))]
        for cp in copies:
            cp.start()
        for cp in copies:
            cp.wait()

    return pl.pallas_call(
        body, name=name, in_specs=[ANY] * n, out_specs=[ANY] * n,
        out_shape=[jax.ShapeDtypeStruct(a.shape, a.dtype) for a in xs],
        scratch_shapes=[pltpu.SemaphoreType.DMA((n,)), pltpu.SemaphoreType.DMA((n,))],
    )(*xs)


def allreduce_small(s):
    n_dev = 8

    def body(s_ref, out_ref, buf, send_sems, recv_sems):
        x, y, c, _ = _place()
        me = 4 * x + 2 * y + c
        buf[me] = s_ref[...]
        sends = []
        for k in range(1, n_dev):
            px = 1 - x if k & 4 else x
            py = 1 - y if k & 2 else y
            pc = 1 - c if k & 1 else c
            cp = _copy(s_ref, buf.at[me], send_sems, recv_sems, k - 1, (px, py, pc))
            cp.start()
            sends.append((cp, 4 * px + 2 * py + pc))
        for k, (cp, peer) in enumerate(sends):
            _copy(s_ref, buf.at[peer], send_sems, recv_sems, k, (x, y, c)).wait_recv()
        for cp, _ in sends:
            cp.wait_send()
        acc = buf[0]
        for d in range(1, n_dev):
            acc = acc + buf[d]
        out_ref[...] = acc

    vm = pl.BlockSpec(memory_space=pltpu.VMEM)
    return pl.pallas_call(
        body, name="allreduce_small", in_specs=[vm], out_specs=vm,
        out_shape=jax.ShapeDtypeStruct(s.shape, F32),
        scratch_shapes=[pltpu.VMEM((n_dev,) + s.shape, F32), pltpu.SemaphoreType.DMA((n_dev - 1,)),
                        pltpu.SemaphoreType.DMA((n_dev - 1,))],
    )(s)


def join_cols(sm):
    n4, k, n = sm.shape
    return sm.transpose(1, 0, 2).reshape(k, n4 * n)


def split_cols(full):
    k, n = full.shape
    return full.reshape(k, N_CHIPS, n // N_CHIPS).transpose(1, 0, 2)


def _pad_heads(w, width):
    lead = w.shape[:-1]
    w = w.reshape(lead + (N_HEADS, width))
    return jnp.pad(w, [(0, 0)] * len(lead) + [(0, 0), (0, LANES - width)]).reshape(lead + (HW,))


def _unpad_heads(w, width):
    lead = w.shape[:-1]
    return w.reshape(lead + (N_HEADS, LANES))[..., :width].reshape(lead + (N_HEADS * width,))


def split_w_in(w_in):
    d = w_in.shape[0]
    o_f = 3 * FW
    o_cq = o_f + N_HEADS
    o_ckv = o_cq + Q_RANK
    o_kr = o_ckv + KV_RANK

    def z(n):
        return jnp.zeros((d, n), w_in.dtype)

    small = jnp.concatenate([w_in[:, o_cq:o_ckv], w_in[:, o_ckv:o_kr], z(HEAD_DIM), w_in[:, o_kr:], z(LANES - HEAD_DIM - ROPE_DIM),
                             w_in[:, o_f:o_cq], z(LANES - N_HEADS), z(LANES)], axis=1)
    return w_in[:, :o_f], small


def join_w_in(d_qkv, d_small):
    kr = S_KR + HEAD_DIM
    return jnp.concatenate([d_qkv, d_small[:, S_F:S_F + N_HEADS], d_small[:, S_CQ:S_CKV], d_small[:, S_CKV:S_KR],
                            d_small[:, kr:kr + ROPE_DIM]], axis=1)


def rope_tables(pos):
    t = pos.shape[0]
    inv_freq = ROPE_THETA ** (-jnp.arange(0, ROPE_DIM, 2, dtype=F32) / ROPE_DIM)
    ang = pos.astype(F32)[:, None] * inv_freq
    cos, sin = jnp.cos(ang), jnp.sin(ang)
    half = ROPE_DIM // 2

    def z(n):
        return jnp.zeros((t, n), F32)

    tab_c = jnp.concatenate([jnp.ones((t, HEAD_DIM), F32), cos, cos, z(LANES - HEAD_DIM - ROPE_DIM)], axis=1)
    tab_a = jnp.concatenate([z(HEAD_DIM), -sin, z(half), z(LANES - HEAD_DIM - ROPE_DIM)], axis=1)
    tab_b = jnp.concatenate([z(HEAD_DIM), z(half), sin, z(LANES - HEAD_DIM - ROPE_DIM)], axis=1)
    return tab_c, tab_a, tab_b


def _pad_lanes(v, n):
    return jnp.pad(v, ((0, 0), (0, n - v.shape[1])))


ATTN_BLK = 512
ATTN_FWD_BLK = 1024
ATTN_BWD_QBLK = 1024


def local_step(xs, pos, tgt, gains, w_early, late_weights, fwd_side=None, bwd_side=None):
    g_attn, b_forget, g_q, g_kv, g_fo, g_mo, g_mlp, g_fin = gains
    w_in, w_uq, w_ukv = w_early
    t = xs.shape[0]
    blk = min(ATTN_BLK, t)
    fox_scale = 1.0 / (HEAD_DIM ** 0.5)
    mla_scale = 1.0 / ((HEAD_DIM + ROPE_DIM) ** 0.5)

    w_qkv, w_small = split_w_in(w_in)
    w_uq_p = _pad_heads(w_uq, HEAD_DIM + ROPE_DIM)
    kv = w_ukv.reshape(KV_RANK, N_HEADS, 2 * HEAD_DIM)
    w_ukv_p = jnp.concatenate([_pad_heads(kv[:, :, :HEAD_DIM].reshape(KV_RANK, FW), HEAD_DIM),
                               kv[:, :, HEAD_DIM:].reshape(KV_RANK, FW)], axis=1)
    b_f = _pad_lanes(b_forget, LANES)
    tab_c, tab_a, tab_b = rope_tables(pos)

    h1 = rmsnorm(xs, g_attn, out_dtype=BF16, name="norm_attn")
    qkv, = mm(h1, w_qkv, out_dtypes=[BF16], name="proj_qkv")
    small, = mm(h1, w_small, out_dtypes=[F32], name="proj_small")
    mq, mk, mv, lf, cqn, ckvn = mla_prep(small, g_q, g_kv, w_uq_p, w_ukv_p, tab_c, tab_a, tab_b, b_f, name="mla_prep")
    f_cum = cumsum_rows(lf, reverse=False, name="gate_cumsum")
    f_blocks = f_cum[:, :N_HEADS].reshape(t // blk, blk, N_HEADS).transpose(0, 2, 1)
    qkv_t, mq_t, mv_t = to_blocks_t(qkv, blk), to_blocks_t(mq, blk), to_blocks_t(mv, blk)
    fo, st_f, *gathered = flash_fwd(qkv_t, qkv, qkv_t, f_cum, qoff=0, koff=PAIRS, voff=2 * PAIRS, pair=True,
                                    scale=fox_scale, name="fox_fwd", blk=ATTN_FWD_BLK, side=fwd_side)
    w_o, w_up, w_down = late_weights(gathered)
    mo, st_m = flash_fwd(mq_t, mk, mv_t, None, qoff=0, koff=0, voff=0, pair=False, scale=mla_scale, name="mla_fwd",
                         blk=ATTN_FWD_BLK)
    mixed = mix_norm(fo, mo, g_fo, g_mo, name="norm_mix")

    def inv_rms(v):
        return lax.rsqrt(jnp.mean(v * v, axis=-1, keepdims=True) + EPS)

    def residual_then_norm(acc, res, g):
        xn = acc + res
        return xn, xn * inv_rms(xn) * g

    def norm_bwd(dh, xn, res, g):
        r = inv_rms(xn)
        uu = dh * g
        return (r * uu - xn * (r * r * r * jnp.mean(uu * xn, axis=-1, keepdims=True)) + res,
                jnp.sum(dh * (xn * r), axis=0, keepdims=True))

    def norm_bwd2(dh, xn, res, g):
        dx, dg = norm_bwd(dh, xn, res, g)
        return dx, dx, dg

    def residual_then_loss(acc, res, target, g):
        xn = acc + res
        r = inv_rms(xn)
        xh = xn * r
        e = xh * g - target
        part = 0.5 * jnp.sum(jnp.mean(e * e, axis=-1, keepdims=True), axis=0, keepdims=True)
        dy = e * (1.0 / xn.shape[1])
        uu = dy * g
        dx = r * uu - xn * (r * r * r * jnp.mean(uu * xn, axis=-1, keepdims=True))
        return dx, dx, jnp.sum(dy * xh, axis=0, keepdims=True), part + jnp.zeros_like(g)

    x1, h2 = mm(mixed, w_o, extras=[xs], vecs=[g_mlp], epilogue=residual_then_norm, out_dtypes=[F32, BF16], name="out_proj")

    def relu2(acc):
        r = jnp.maximum(acc, 0.0)
        return acc, r * r

    u, act = mm(h2, w_up, epilogue=relu2, out_dtypes=[BF16, BF16], name="mlp_up")
    dx2, dx2b, dg_fin, loss_row = mm(act, w_down, extras=[x1, tgt], vecs=[g_fin], epilogue=residual_then_loss,
                                     out_dtypes=[F32, BF16], n_sums=2, name="mlp_down_loss")
    loss = loss_row[:, :1]

    def relu2_grad(acc, uu):
        return (acc * (2.0 * jnp.maximum(uu.astype(F32), 0.0)),)

    du, = mm(dx2b, w_down, trans_b=True, extras=[u], epilogue=relu2_grad, out_dtypes=[BF16], name="mlp_down_bwd")
    dw_down = mm_tn(act, dx2b, name="dw_down").reshape(N_CHIPS, -1, w_down.shape[1])
    dx1, dx1b, dg_mlp = mm(du, w_up, trans_b=True, extras=[x1, dx2], vecs=[g_mlp], epilogue=norm_bwd2,
                           out_dtypes=[F32, BF16], n_sums=1, name="mlp_up_bwd")
    dw_up = mm_tn(h2, du, name="dw_up", col_shards=N_CHIPS)

    dmixed, = mm(dx1b, w_o, trans_b=True, out_dtypes=[F32], name="out_proj_bwd")
    dw_o = mm_tn(mixed, dx1b, name="dw_o").reshape(N_CHIPS, -1, w_o.shape[1])
    side = bwd_side(dw_o, dw_up, dw_down) if bwd_side is not None else None
    dfo, dmo, dg_fo, dg_mo, st_f, st_m = mix_norm_bwd(dmixed, fo, mo, g_fo, g_mo, st_f, st_m, name="norm_mix_bwd")
    dfq, dfk, dfv, d_f, *scattered = flash_bwd(
        qkv, qkv_t, qkv, qkv, dfo, to_blocks_t(dfo, blk), st_f, f_blocks, qoff=0, koff=PAIRS, voff=2 * PAIRS, pair=True,
        scale=fox_scale, name="fox_bwd", qblk=ATTN_BWD_QBLK, side=side)
    dmq, dmk, dmv = flash_bwd(mq, mq_t, mk, mv, dmo, to_blocks_t(dmo, blk), st_m, None, qoff=0, koff=0, voff=0, pair=False,
                              scale=mla_scale, name="mla_bwd", qblk=ATTN_BWD_QBLK)
    dqkv = jnp.concatenate([dfq, dfk, dfv], axis=1).astype(BF16)
    dlf = cumsum_rows(d_f, reverse=True, name="gate_cumsum_bwd")
    dsmall, dq_u, dkv_u, dg_q, dg_kv, db_f = mla_prep_bwd(dmq, dmk, dmv, dlf, small, g_q, g_kv, w_uq_p, w_ukv_p,
                                                          tab_c, tab_a, tab_b, b_f, name="mla_prep_bwd")
    dw_uq_p = mm_tn(cqn, dq_u, name="dw_uq")
    dw_ukv_p = mm_tn(ckvn, dkv_u, name="dw_ukv")

    grad_x, dg_attn = mm([dqkv, dsmall], [w_qkv, w_small], trans_b=True, extras=[xs, dx1], vecs=[g_attn], epilogue=norm_bwd,
                         out_dtypes=[F32], n_sums=1, name="proj_bwd")
    dw_qkv = mm_tn(h1, dqkv, name="dw_qkv")
    dw_small = mm_tn(h1, dsmall, name="dw_small")

    dw_in = join_w_in(dw_qkv, dw_small)
    dw_uq = _unpad_heads(dw_uq_p, HEAD_DIM + ROPE_DIM)
    dk_cols = _unpad_heads(dw_ukv_p[:, :HW], HEAD_DIM).reshape(KV_RANK, N_HEADS, HEAD_DIM)
    dv_cols = dw_ukv_p[:, HW:].reshape(KV_RANK, N_HEADS, HEAD_DIM)
    dw_ukv = jnp.concatenate([dk_cols, dv_cols], axis=2).reshape(KV_RANK, N_HEADS * 2 * HEAD_DIM)
    d_gains = (dg_attn, db_f[:, :N_HEADS], dg_q, dg_kv, dg_fo, dg_mo, dg_mlp, dg_fin)
    return loss, grad_x, (dw_in, dw_uq, dw_ukv), (dw_o, dw_up, dw_down), d_gains, scattered


def kernel(x, positions, attn_norm_g, w_in, b_forget, q_norm_g, w_uq, kv_norm_g, w_ukv, fox_out_g, mla_out_g, w_o, mlp_norm_g, w_up, w_down, final_norm_g, loss_target, m_attn_norm_g, m_w_in, m_b_forget, m_q_norm_g, m_w_uq, m_kv_norm_g, m_w_ukv, m_fox_out_g, m_mla_out_g, m_w_o, m_mlp_norm_g, m_w_up, m_w_down, m_final_norm_g, v_attn_norm_g, v_w_in, v_b_forget, v_q_norm_g, v_w_uq, v_kv_norm_g, v_w_ukv, v_fox_out_g, v_mla_out_g, v_w_o, v_mlp_norm_g, v_w_up, v_w_down, v_final_norm_g):
    core_id = lax.axis_index("c")
    core = core_id.reshape(1).astype(jnp.int32)
    chip = 2 * lax.axis_index("x") + lax.axis_index("y")
    big = [w_in, w_uq, w_ukv, w_o, w_up, w_down]
    big_m = [m_w_in, m_w_uq, m_w_ukv, m_w_o, m_w_up, m_w_down]
    big_v = [v_w_in, v_w_uq, v_w_ukv, v_w_o, v_w_up, v_w_down]
    n_early = 3

    def vec(a):
        return a.reshape(1, -1)

    small = [attn_norm_g, b_forget, q_norm_g, kv_norm_g, fox_out_g, mla_out_g, mlp_norm_g, final_norm_g]
    small_m = [m_attn_norm_g, m_b_forget, m_q_norm_g, m_kv_norm_g, m_fox_out_g, m_mla_out_g, m_mlp_norm_g, m_final_norm_g]
    small_v = [v_attn_norm_g, v_b_forget, v_q_norm_g, v_kv_norm_g, v_fox_out_g, v_mla_out_g, v_mlp_norm_g, v_final_norm_g]
    gains = [vec(a) for a in small]

    shards = [w[0].astype(BF16) for w in big]

    def with_own(gathered, mine):
        return [lax.dynamic_update_index_in_dim(g, s, chip, 0) for g, s in zip(gathered, mine)]

    early = with_own(run_side(gather_side(shards[:n_early]), name="gather_early"), shards[:n_early])
    w_early = [join_cols(g) for g in early]

    def late_weights(gathered):
        g_o, g_up, g_down = with_own(gathered, shards[n_early:])
        return g_o.reshape(-1, g_o.shape[2]), join_cols(g_up), g_down.reshape(-1, g_down.shape[2])

    def pair_sums(grads, name):
        sends = [lax.dynamic_slice_in_dim(g, (1 - core_id) * (g.shape[1] // 2), g.shape[1] // 2, axis=1).astype(BF16)
                 for g in grads]
        return sends

    def add_pairs(grads, recvs, name):
        return [add_pair(g, r, core, name="%s_%d" % (name, n)) for n, (g, r) in enumerate(zip(grads, recvs))]

    def chip_sums(scattered, pairs, name):
        with_mine = [lax.dynamic_update_index_in_dim(s, lax.dynamic_index_in_dim(p, chip, 0, keepdims=True), chip, 0)
                     for s, p in zip(scattered, pairs)]
        return [sum_chips(s, name="%s_%d" % (name, n)) for n, s in enumerate(with_mine)]

    late_pairs = []

    def bwd_side(dw_o, dw_up, dw_down):
        grads = [dw_o, dw_up, dw_down]
        recvs = swap_sibling(pair_sums(grads, "late"), name="swap_late")
        late_pairs.extend(add_pairs(grads, recvs, "add_pair_late"))
        return scatter_side(late_pairs)

    loss, grad_x, d_early, _, d_small, scattered_late = local_step(
        x[0], positions[0], loss_target[0], gains, w_early, late_weights, gather_side(shards[n_early:]), bwd_side)

    halves_late = chip_sums(scattered_late, late_pairs, "sum_chips_late")
    g_early = [split_cols(d) for d in d_early]
    n_late = len(halves_late)
    swapped = swap_sibling(halves_late + pair_sums(g_early, "early"), name="swap_mixed")
    others_late, recvs_early = swapped[:n_late], swapped[n_late:]
    early_pairs = add_pairs(g_early, recvs_early, "add_pair_early")
    scattered_early = run_side(scatter_side(early_pairs), name="scatter_early")
    halves_early = chip_sums(scattered_early, early_pairs, "sum_chips_early")
    others_early = swap_sibling(halves_early, name="swap_early")
    halves = halves_early + halves_late
    others = list(others_early) + list(others_late)

    def rows8(vs):
        return jnp.concatenate([_pad_lanes(vec(a).astype(F32), 1024) for a in vs], axis=0)

    with_loss = [jnp.concatenate([d, loss], axis=1) if n == 1 else d for n, d in enumerate(d_small)]
    g_small8 = allreduce_small(rows8(with_loss))

    outs_big = [adamw_halves(w, gm, go, m, v, core, name="adamw_%d" % n)
                for n, (w, gm, go, m, v) in enumerate(zip(big, halves, others, big_m, big_v))]
    d8, m8, v8 = adamw(rows8(small), g_small8, rows8(small_m), rows8(small_v), name="adamw_small")

    def unrows8(a8):
        return [a8[n, :s.size].reshape(s.shape) for n, s in enumerate(small)]

    loss_all = g_small8[1, N_HEADS]
    grads, deltas, new_m, new_v = [None] * 14, [None] * 14, [None] * 14, [None] * 14
    big_at = [1, 4, 6, 9, 11, 12]
    small_at = [0, 2, 3, 5, 7, 8, 10, 13]
    for n, at in enumerate(big_at):
        grads[at], deltas[at], new_m[at], new_v[at] = outs_big[n]
    for at, g, dd, mm_, vv in zip(small_at, unrows8(g_small8), unrows8(d8), unrows8(m8), unrows8(v8)):
        grads[at], deltas[at], new_m[at], new_v[at] = g, dd, mm_, vv
    return (loss_all, grad_x[None], *grads, *deltas, *new_m, *new_v)
```

```python
import jax
import jax.numpy as jnp
from jax import lax
from jax.experimental import pallas as pl
from jax.experimental.pallas import tpu as pltpu

F32 = jnp.float32
BF16 = jnp.bfloat16
MESH = pl.DeviceIdType.MESH

EPS = 1e-6
ROPE_THETA = 10000.0
N_HEADS = 8
PAIRS = N_HEADS // 2
HEAD_DIM = 64
ROPE_DIM = 32
LANES = 128
Q_RANK = 384
KV_RANK = 256
N_CHIPS = 4
ADAM_LR, ADAM_B1, ADAM_B2, ADAM_EPS, ADAM_WD, ADAM_STEP = 0.001, 0.9, 0.999, 1e-08, 0.01, 10
VMEM_LIMIT = 48 * 1024 * 1024
LOG2E = 1.4426950408889634
LN2 = 0.6931471805599453
NN = (((1,), (0,)), ((), ()))
NT = (((1,), (1,)), ((), ()))
TN = (((0,), (0,)), ((), ()))


def _params(sem=None):
    return pltpu.CompilerParams(dimension_semantics=sem, vmem_limit_bytes=VMEM_LIMIT)


def _fit(block, dim):
    if dim <= block:
        return dim
    return next(b for b in range(block - block % LANES, 0, -LANES) if dim % b == 0)


def _row_block(rows):
    return next(b for b in (256, 128, 64, 32, 16, 8) if rows % b == 0)


def gridded(body, *, name, grid, in_specs, out_specs, out_shape, ins, semantics, side=None):
    if side is None:
        return pl.pallas_call(body, name=name, grid=grid, in_specs=in_specs, out_specs=out_specs, out_shape=out_shape,
                              compiler_params=_params(semantics))(*ins)
    n_in, n_out = len(in_specs), len(out_specs)
    steps = 1
    for extent in grid:
        steps *= extent

    def riding(*refs):
        main_in, s_in, main_out, s_out, sems = _split_refs(refs, [n_in, len(side.ins), n_out, len(side.out_shapes), 2])
        step = 0
        for axis, extent in enumerate(grid):
            step = step * extent + pl.program_id(axis)

        @pl.when(step == 0)
        def _():
            side.first(s_in, s_out, *sems)

        body(*main_in, *main_out)

        @pl.when(step == steps - 1)
        def _():
            side.last(s_in, s_out, *sems)

    s_in_specs, s_out_specs = side.specs()
    return pl.pallas_call(
        riding, name=name, grid=grid, in_specs=list(in_specs) + s_in_specs, out_specs=list(out_specs) + s_out_specs,
        out_shape=list(out_shape) + side.out_shapes, scratch_shapes=side.sems(),
        compiler_params=_params(("arbitrary",) * len(grid)))(*ins, *side.ins)


def rmsnorm(x, g, *, out_dtype, name, bt=512):
    t, d = x.shape
    bt = min(bt, t)

    def body(x_ref, g_ref, o_ref):
        xv = x_ref[...].astype(F32)
        r = lax.rsqrt(jnp.mean(xv * xv, axis=-1, keepdims=True) + EPS)
        o_ref[...] = (xv * r * g_ref[...]).astype(o_ref.dtype)

    return pl.pallas_call(
        body, name=name, grid=(t // bt,),
        in_specs=[pl.BlockSpec((bt, d), lambda i: (i, 0)), pl.BlockSpec((1, d), lambda i: (0, 0))],
        out_specs=pl.BlockSpec((bt, d), lambda i: (i, 0)),
        out_shape=jax.ShapeDtypeStruct((t, d), out_dtype),
        compiler_params=_params(("parallel",)),
    )(x, g)


def mm(a, b, *, trans_b=False, extras=(), vecs=(), epilogue=None, out_dtypes, n_sums=0, name, bm=1024, bn=1024, side=None):
    a_list = list(a) if isinstance(a, (list, tuple)) else [a]
    b_list = list(b) if isinstance(b, (list, tuple)) else [b]
    m = a_list[0].shape[0]
    n = b_list[0].shape[0] if trans_b else b_list[0].shape[1]
    ks = [x.shape[1] for x in a_list]
    if sum(ks) > 2048:
        bm = bm // 2
    bm, bn = _fit(bm, m), _fit(bn, n)
    assert n_sums == 0 or bn == n
    n_ab, n_ex, n_vec, n_out = len(a_list), len(extras), len(vecs), len(out_dtypes)

    def body(*refs):
        a_refs, b_refs, ex, vs, outs, sums = _split_refs(refs, [n_ab, n_ab, n_ex, n_vec, n_out, n_sums])
        acc = None
        for a_ref, b_ref in zip(a_refs, b_refs):
            part = lax.dot_general(a_ref[...], b_ref[...], NT if trans_b else NN, preferred_element_type=F32)
            acc = part if acc is None else acc + part
        res = epilogue(acc, *[e[...] for e in ex], *[v[...] for v in vs]) if epilogue is not None else (acc,)
        for o, r in zip(outs, res[:n_out]):
            o[...] = r.astype(o.dtype)
        if n_sums:
            @pl.when(pl.program_id(0) == 0)
            def _():
                for s_ref in sums:
                    s_ref[...] = jnp.zeros_like(s_ref)

            for s_ref, r in zip(sums, res[n_out:]):
                s_ref[...] += r

    tile = pl.BlockSpec((bm, bn), lambda i, j: (i, j))
    vec = pl.BlockSpec((1, bn), lambda i, j: (0, j))
    a_specs = [pl.BlockSpec((bm, k), lambda i, j: (i, 0)) for k in ks]
    b_specs = [pl.BlockSpec((bn, k), lambda i, j: (j, 0)) if trans_b else pl.BlockSpec((k, bn), lambda i, j: (0, j)) for k in ks]
    return gridded(
        body, name=name, grid=(m // bm, n // bn),
        in_specs=a_specs + b_specs + [tile] * n_ex + [vec] * n_vec,
        out_specs=[tile] * n_out + [vec] * n_sums,
        out_shape=[jax.ShapeDtypeStruct((m, n), dt) for dt in out_dtypes] + [jax.ShapeDtypeStruct((1, n), F32)] * n_sums,
        ins=[*a_list, *b_list, *extras, *vecs],
        semantics=("arbitrary", "arbitrary") if n_sums else ("parallel", "parallel"), side=side)


def mm_tn(a, b, *, name, col_shards=1, bk=1024, bn=1024, bt=1024):
    t, k = a.shape
    n = b.shape[1]
    ns = n // col_shards
    bk, bn, bt = _fit(bk, k), _fit(bn, ns), _fit(bt, t)
    per = ns // bn

    def body(a_ref, b_ref, o_ref):
        @pl.when(pl.program_id(2) == 0)
        def _():
            o_ref[...] = jnp.zeros_like(o_ref)

        o_ref[...] += lax.dot_general(a_ref[...], b_ref[...], TN, preferred_element_type=F32)

    if col_shards == 1:
        out_spec = pl.BlockSpec((bk, bn), lambda i, j, s: (i, j))
        out_shape = jax.ShapeDtypeStruct((k, n), F32)
    else:
        out_spec = pl.BlockSpec((None, bk, bn), lambda i, j, s: (j // per, i, j % per))
        out_shape = jax.ShapeDtypeStruct((col_shards, k, ns), F32)
    return pl.pallas_call(
        body, name=name, grid=(k // bk, n // bn, t // bt),
        in_specs=[pl.BlockSpec((bt, bk), lambda i, j, s: (s, i)), pl.BlockSpec((bt, bn), lambda i, j, s: (s, j))],
        out_specs=out_spec, out_shape=out_shape,
        compiler_params=_params(("parallel", "parallel", "arbitrary")),
    )(a, b)


def _split3(x):
    hi = x.astype(BF16)
    r1 = x - hi.astype(F32)
    mid = r1.astype(BF16)
    lo = (r1 - mid.astype(F32)).astype(BF16)
    return hi, mid, lo


def cumsum_rows(x, *, reverse, name, bc=512):
    t, d = x.shape
    bc = min(bc, t)
    nb = t // bc

    def body(x_ref, o_ref, carry):
        @pl.when(pl.program_id(0) == 0)
        def _():
            carry[...] = jnp.zeros_like(carry)

        r = lax.broadcasted_iota(jnp.int32, (bc, bc), 0)
        c = lax.broadcasted_iota(jnp.int32, (bc, bc), 1)
        tri = jnp.where((r <= c) if reverse else (r >= c), 1.0, 0.0).astype(BF16)
        hi, mid, lo = _split3(x_ref[...])
        s = (lax.dot_general(tri, hi, NN, preferred_element_type=F32)
             + lax.dot_general(tri, mid, NN, preferred_element_type=F32)
             + lax.dot_general(tri, lo, NN, preferred_element_type=F32)) + carry[0:1, :]
        o_ref[...] = s
        carry[0:1, :] = s[0:1, :] if reverse else s[bc - 1:bc, :]

    imap = (lambda i: (nb - 1 - i, 0)) if reverse else (lambda i: (i, 0))
    return pl.pallas_call(
        body, name=name, grid=(nb,),
        in_specs=[pl.BlockSpec((bc, d), imap)], out_specs=pl.BlockSpec((bc, d), imap),
        out_shape=jax.ShapeDtypeStruct((t, d), F32),
        scratch_shapes=[pltpu.VMEM((8, d), F32)],
        compiler_params=_params(("arbitrary",)),
    )(x)


def _rope(x, c, a, b):
    return x * c + pltpu.roll(x, LANES - ROPE_DIM // 2, 1) * a + pltpu.roll(x, ROPE_DIM // 2, 1) * b


def _rope_bwd(d, c, a, b):
    return d * c + pltpu.roll(d * a, ROPE_DIM // 2, 1) + pltpu.roll(d * b, LANES - ROPE_DIM // 2, 1)


S_CQ, S_CKV, S_KR, S_F, S_END = 0, Q_RANK, Q_RANK + KV_RANK, Q_RANK + KV_RANK + LANES, 1024
HW = N_HEADS * LANES
FW = N_HEADS * HEAD_DIM


def mla_prep(small, g_q, g_kv, w_uq, w_ukv, tab_c, tab_a, tab_b, b_f, *, name, bt=512):
    t = small.shape[0]
    bt = min(bt, t)

    def body(s_ref, gq_ref, gkv_ref, wq_ref, wkv_ref, c_ref, a_ref, b_ref, bf_ref,
             mq_ref, mk_ref, mv_ref, lf_ref, cqn_ref, ckvn_ref):
        cq = s_ref[:, S_CQ:S_CKV]
        rq = lax.rsqrt(jnp.mean(cq * cq, axis=-1, keepdims=True) + EPS)
        cqn = (cq * rq * gq_ref[...]).astype(BF16)
        ckv = s_ref[:, S_CKV:S_KR]
        rkv = lax.rsqrt(jnp.mean(ckv * ckv, axis=-1, keepdims=True) + EPS)
        ckvn = (ckv * rkv * gkv_ref[...]).astype(BF16)
        cqn_ref[...] = cqn
        ckvn_ref[...] = ckvn
        tc, ta, tb = c_ref[...], a_ref[...], b_ref[...]
        q = jnp.dot(cqn, wq_ref[...], preferred_element_type=F32)
        kv = jnp.dot(ckvn, wkv_ref[...], preferred_element_type=F32)
        kr = _rope(s_ref[:, S_KR:S_F], tc, ta, tb)
        for h in range(N_HEADS):
            sl = slice(h * LANES, (h + 1) * LANES)
            mq_ref[:, sl] = _rope(q[:, sl], tc, ta, tb).astype(BF16)
            mk_ref[:, sl] = (kv[:, sl] + kr).astype(BF16)
        mv_ref[...] = kv[:, HW:].astype(BF16)
        z = s_ref[:, S_F:S_END - LANES] + bf_ref[...]
        lf_ref[...] = jnp.minimum(z, 0.0) - jnp.log(1.0 + jnp.exp(-jnp.abs(z)))

    def row(w):
        return pl.BlockSpec((bt, w), lambda i: (i, 0))

    def full(arr):
        return pl.BlockSpec(arr.shape, lambda i: (0, 0))

    return pl.pallas_call(
        body, name=name, grid=(t // bt,),
        in_specs=[row(S_END), full(g_q), full(g_kv), full(w_uq), full(w_ukv), row(LANES), row(LANES), row(LANES), full(b_f)],
        out_specs=[row(HW), row(HW), row(FW), row(LANES), row(Q_RANK), row(KV_RANK)],
        out_shape=[jax.ShapeDtypeStruct((t, HW), BF16)] * 2 + [jax.ShapeDtypeStruct((t, FW), BF16), jax.ShapeDtypeStruct((t, LANES), F32),
                   jax.ShapeDtypeStruct((t, Q_RANK), BF16), jax.ShapeDtypeStruct((t, KV_RANK), BF16)],
        compiler_params=_params(("parallel",)),
    )(small, g_q, g_kv, w_uq, w_ukv, tab_c, tab_a, tab_b, b_f)


def mla_prep_bwd(dmq, dmk, dmv, dlf, small, g_q, g_kv, w_uq, w_ukv, tab_c, tab_a, tab_b, b_f, *, name, bt=512):
    t = small.shape[0]
    bt = min(bt, t)

    def body(dmq_ref, dmk_ref, dmv_ref, dlf_ref, s_ref, gq_ref, gkv_ref, wq_ref, wkv_ref, c_ref, a_ref, b_ref, bf_ref,
             ds_ref, dq_ref, dkv_ref, dgq_ref, dgkv_ref, db_ref):
        tc, ta, tb = c_ref[...], a_ref[...], b_ref[...]
        lane = lax.broadcasted_iota(jnp.int32, (1, LANES), 1)
        dkr = jnp.zeros((bt, LANES), F32)
        for h in range(N_HEADS):
            sl = slice(h * LANES, (h + 1) * LANES)
            dq_ref[:, sl] = _rope_bwd(dmq_ref[:, sl], tc, ta, tb).astype(BF16)
            dkr = dkr + dmk_ref[:, sl]
        dkv_ref[:, :HW] = dmk_ref[...].astype(BF16)
        dkv_ref[:, HW:] = dmv_ref[...].astype(BF16)
        in_rope = (lane >= HEAD_DIM) & (lane < HEAD_DIM + ROPE_DIM)
        ds_ref[:, S_KR:S_F] = jnp.where(in_rope, _rope_bwd(dkr, tc, ta, tb), 0.0).astype(BF16)

        def norm_bwd(raw, g_ref, dn, dg_ref):
            r = lax.rsqrt(jnp.mean(raw * raw, axis=-1, keepdims=True) + EPS)
            u = dn * g_ref[...]
            dot = jnp.mean(u * raw, axis=-1, keepdims=True)
            dg_ref[...] += jnp.sum(dn * (raw * r), axis=0, keepdims=True)
            return r * u - raw * (r * r * r * dot)

        @pl.when(pl.program_id(0) == 0)
        def _():
            dgq_ref[...] = jnp.zeros_like(dgq_ref)
            dgkv_ref[...] = jnp.zeros_like(dgkv_ref)
            db_ref[...] = jnp.zeros_like(db_ref)

        dcqn = lax.dot_general(dq_ref[...], wq_ref[...], NT, preferred_element_type=F32)
        ds_ref[:, S_CQ:S_CKV] = norm_bwd(s_ref[:, S_CQ:S_CKV], gq_ref, dcqn, dgq_ref).astype(BF16)
        dckvn = lax.dot_general(dkv_ref[...], wkv_ref[...], NT, preferred_element_type=F32)
        ds_ref[:, S_CKV:S_KR] = norm_bwd(s_ref[:, S_CKV:S_KR], gkv_ref, dckvn, dgkv_ref).astype(BF16)
        z = s_ref[:, S_F:S_END - LANES] + bf_ref[...]
        dz = jnp.where(lane < N_HEADS, dlf_ref[...] / (1.0 + jnp.exp(z)), 0.0)
        db_ref[...] += jnp.sum(dz, axis=0, keepdims=True)
        ds_ref[:, S_F:S_END - LANES] = dz.astype(BF16)
        ds_ref[:, S_END - LANES:] = jnp.zeros((bt, LANES), BF16)

    def row(w):
        return pl.BlockSpec((bt, w), lambda i: (i, 0))

    def full(arr):
        return pl.BlockSpec(arr.shape, lambda i: (0, 0))

    def vec(w):
        return pl.BlockSpec((1, w), lambda i: (0, 0))

    return pl.pallas_call(
        body, name=name, grid=(t // bt,),
        in_specs=[row(HW), row(HW), row(FW), row(LANES), row(S_END), full(g_q), full(g_kv), full(w_uq), full(w_ukv),
                  row(LANES), row(LANES), row(LANES), full(b_f)],
        out_specs=[row(S_END), row(HW), row(HW + FW), vec(Q_RANK), vec(KV_RANK), vec(LANES)],
        out_shape=[jax.ShapeDtypeStruct((t, S_END), BF16), jax.ShapeDtypeStruct((t, HW), BF16),
                   jax.ShapeDtypeStruct((t, HW + FW), BF16), jax.ShapeDtypeStruct((1, Q_RANK), F32),
                   jax.ShapeDtypeStruct((1, KV_RANK), F32), jax.ShapeDtypeStruct((1, LANES), F32)],
        compiler_params=_params(("arbitrary",)),
    )(dmq, dmk, dmv, dlf, small, g_q, g_kv, w_uq, w_ukv, tab_c, tab_a, tab_b, b_f)


class Side:
    def __init__(self, ins, out_shapes, n_sems, first, last, mid=None):
        self.ins, self.out_shapes, self.n_sems = list(ins), list(out_shapes), n_sems
        self.first, self.mid, self.last = first, mid, last

    def specs(self):
        return [ANY] * len(self.ins), [ANY] * len(self.out_shapes)

    def sems(self):
        return [pltpu.SemaphoreType.DMA((self.n_sems,)), pltpu.SemaphoreType.DMA((self.n_sems,))]


def _lane():
    return lax.broadcasted_iota(jnp.int32, (1, LANES), 1)


def _halves(x):
    zero = jnp.zeros_like(x)
    return [jnp.where(_lane() < HEAD_DIM, x, zero), jnp.where(_lane() >= HEAD_DIM, x, zero)]


def _groups(x):
    return [x[:, :LANES], x[:, LANES:]]


def _pick_row(tile, h):
    row = lax.broadcasted_iota(jnp.int32, (tile.shape[0], 1), 0)
    return jnp.sum(jnp.where(row == h, tile, 0.0), axis=0, keepdims=True)


def _pick_lane(tile, h):
    return jnp.sum(jnp.where(_lane() == h, tile, 0.0), axis=1, keepdims=True)


def _row_halves(x):
    row = lax.broadcasted_iota(jnp.int32, (LANES, 1), 0)
    zero = jnp.zeros_like(x)
    return [jnp.where(row < HEAD_DIM, x, zero), jnp.where(row >= HEAD_DIM, x, zero)]


def _below_diagonal(s, lead=0):
    r = lax.broadcasted_iota(jnp.int32, s.shape, 0)
    c = lax.broadcasted_iota(jnp.int32, s.shape, 1)
    return jnp.where(c <= r + lead, s, -jnp.inf)


def _above_diagonal(s):
    r = lax.broadcasted_iota(jnp.int32, s.shape, 0)
    c = lax.broadcasted_iota(jnp.int32, s.shape, 1)
    return jnp.where(r <= c, s, -jnp.inf)


def to_blocks_t(x, blk):
    t, c = x.shape
    return x.reshape(t // blk, blk, c).transpose(0, 2, 1)


def _split_refs(refs, counts):
    out, at = [], 0
    for n in counts:
        out.append(refs[at:at + n])
        at += n
    return out


def flash_fwd(qt_arr, k_arr, vt_arr, f_cum, *, qoff, koff, voff, pair, scale, name, blk=512, side=None):
    t = k_arr.shape[0]
    tblk = qt_arr.shape[2]
    blk = max(min(blk, t), tblk)
    sub = blk // tblk
    nb = t // blk
    w = LANES if pair else 2 * LANES
    has_bias = f_cum is not None
    ins = [qt_arr, k_arr, vt_arr] + ([f_cum] if has_bias else [])

    def wide(ref, first):
        parts = [ref[first + u] for u in range(sub)]
        return parts[0] if sub == 1 else jnp.concatenate(parts, axis=1)
    s_ins, s_outs = (side.ins, side.out_shapes) if side else ([], [])

    def body(*refs):
        main, si, outs, so, sems = _split_refs(refs, [len(ins), len(s_ins), 2, len(s_outs), 2 if side else 0])
        qt_ref, k_ref, vt_ref = main[:3]
        f_ref = main[3] if has_bias else None
        o_ref, st_ref = outs
        g, i = pl.program_id(0), pl.program_id(1)
        step_id = g * nb + i
        if side:
            @pl.when(step_id == 0)
            def _():
                side.first(si, so, *sems)

            if side.mid is not None:
                @pl.when(step_id == (3 * PAIRS * nb) // 4)
                def _():
                    side.mid(si, so, *sems)

        qt = (wide(qt_ref, 0).astype(F32) * (scale * LOG2E)).astype(BF16)
        qts = _row_halves(qt) if pair else [qt[:LANES], qt[LANES:]]

        def step(j, carry, diagonal):
            rows = pl.ds(pl.multiple_of(j * blk, blk), blk)
            kk = k_ref[rows, :]
            ks = [kk, kk] if pair else _groups(kk)
            vt = wide(vt_ref, sub * j)
            out = []
            for n in range(2):
                m, l, acc = carry[n]
                s = jnp.dot(ks[n], qts[n], preferred_element_type=F32)
                if has_bias:
                    s = s - LOG2E * _pick_lane(f_ref[rows, :], 2 * g + n)
                if diagonal:
                    s = _above_diagonal(s)
                m_new = jnp.maximum(m, jnp.max(s, axis=0, keepdims=True))
                alpha = jnp.exp2(m - m_new)
                p = jnp.exp2(s - m_new)
                out.append((m_new, alpha * l + jnp.sum(p, axis=0, keepdims=True),
                            alpha * acc + jnp.dot(vt[n * HEAD_DIM:(n + 1) * HEAD_DIM], p.astype(BF16),
                                                  preferred_element_type=F32)))
            return tuple(out)

        init = tuple((jnp.full((1, blk), -jnp.inf, F32), jnp.zeros((1, blk), F32), jnp.zeros((HEAD_DIM, blk), F32))
                     for _ in range(2))
        carry = lax.fori_loop(0, i, lambda j, c: step(j, c, False), init)
        (ma, la, acca), (mb, lb, accb) = step(i, carry, True)
        o_ref[...] = jnp.concatenate([acca / la, accb / lb], axis=0).T
        row = lax.broadcasted_iota(jnp.int32, (LANES, 1), 0)
        st_ref[0] = jnp.where(row == 0, ma + jnp.log2(la), jnp.where(row == 1, mb + jnp.log2(lb), 0.0)).T
        if side:
            @pl.when(step_id == PAIRS * nb - 1)
            def _():
                side.last(si, so, *sems)

    in_specs = [pl.BlockSpec((sub, w, tblk), lambda g, i: (i, qoff + g, 0)), pl.BlockSpec((t, w), lambda g, i: (0, koff + g)),
                pl.BlockSpec((t // tblk, LANES, tblk), lambda g, i: (0, voff + g, 0))]
    if has_bias:
        in_specs.append(pl.BlockSpec((t, LANES), lambda g, i: (0, 0)))
    s_in_specs, s_out_specs = side.specs() if side else ([], [])
    return pl.pallas_call(
        body, name=name, grid=(PAIRS, nb), in_specs=in_specs + s_in_specs,
        out_specs=[pl.BlockSpec((blk, LANES), lambda g, i: (i, g)), pl.BlockSpec((1, blk, LANES), lambda g, i: (g, i, 0))]
        + s_out_specs,
        out_shape=[jax.ShapeDtypeStruct((t, PAIRS * LANES), F32), jax.ShapeDtypeStruct((PAIRS, t, LANES), F32)] + list(s_outs),
        scratch_shapes=side.sems() if side else [],
        compiler_params=_params(("arbitrary", "arbitrary")),
    )(*ins, *s_ins)


def mix_norm(fo, mo, g_fo, g_mo, *, name, bt=512):
    t, d = fo.shape
    bt = min(bt, t)

    def body(fo_ref, mo_ref, gf_ref, gm_ref, o_ref):
        for n, (x_ref, g_ref) in enumerate(((fo_ref, gf_ref), (mo_ref, gm_ref))):
            xv = x_ref[...]
            r = lax.rsqrt(jnp.mean(xv * xv, axis=-1, keepdims=True) + EPS)
            o_ref[:, n * d:(n + 1) * d] = (xv * r * g_ref[...]).astype(BF16)

    row = pl.BlockSpec((bt, d), lambda i: (i, 0))
    vec = pl.BlockSpec((1, d), lambda i: (0, 0))
    return pl.pallas_call(
        body, name=name, grid=(t // bt,), in_specs=[row, row, vec, vec],
        out_specs=pl.BlockSpec((bt, 2 * d), lambda i: (i, 0)),
        out_shape=jax.ShapeDtypeStruct((t, 2 * d), BF16),
        compiler_params=_params(("parallel",)),
    )(fo, mo, g_fo, g_mo)


def mix_norm_bwd(dmixed, fo, mo, g_fo, g_mo, st_f, st_m, *, name, bt=512, side=None):
    t, d = fo.shape
    bt = min(bt, t)

    def body(dm_ref, fo_ref, mo_ref, gf_ref, gm_ref, sf_ref, sm_ref, dfo_ref, dmo_ref, dgf_ref, dgm_ref, sfo_ref, smo_ref):
        @pl.when(pl.program_id(0) == 0)
        def _():
            dgf_ref[...] = jnp.zeros_like(dgf_ref)
            dgm_ref[...] = jnp.zeros_like(dgm_ref)

        groups = ((fo_ref, gf_ref, dfo_ref, dgf_ref, sf_ref, sfo_ref), (mo_ref, gm_ref, dmo_ref, dgm_ref, sm_ref, smo_ref))
        for n, (x_ref, g_ref, dx_ref, dg_ref, st_ref, sto_ref) in enumerate(groups):
            xv = x_ref[...]
            dhv = dm_ref[:, n * d:(n + 1) * d]
            r = lax.rsqrt(jnp.mean(xv * xv, axis=-1, keepdims=True) + EPS)
            u = dhv * g_ref[...]
            dxb = (r * u - xv * (r * r * r * jnp.mean(u * xv, axis=-1, keepdims=True))).astype(BF16)
            dx_ref[...] = dxb
            dg_ref[...] += jnp.sum(dhv * (xv * r), axis=0, keepdims=True)
            prod = xv * dxb.astype(F32)
            for g in range(PAIRS):
                grp = prod[:, g * LANES:(g + 1) * LANES]
                da = jnp.sum(jnp.where(_lane() < HEAD_DIM, grp, 0.0), axis=1, keepdims=True)
                db = jnp.sum(jnp.where(_lane() >= HEAD_DIM, grp, 0.0), axis=1, keepdims=True)
                sto_ref[g] = jnp.where(_lane() == 2, da, jnp.where(_lane() == 3, db, st_ref[g]))

    row = pl.BlockSpec((bt, d), lambda i: (i, 0))
    vec = pl.BlockSpec((1, d), lambda i: (0, 0))
    stat = pl.BlockSpec((PAIRS, bt, LANES), lambda i: (0, i, 0))
    return gridded(
        body, name=name, grid=(t // bt,),
        in_specs=[pl.BlockSpec((bt, 2 * d), lambda i: (i, 0)), row, row, vec, vec, stat, stat],
        out_specs=[row, row, vec, vec, stat, stat],
        out_shape=[jax.ShapeDtypeStruct((t, d), BF16)] * 2 + [jax.ShapeDtypeStruct((1, d), F32)] * 2
        + [jax.ShapeDtypeStruct(st_f.shape, F32)] * 2,
        ins=[dmixed, fo, mo, g_fo, g_mo, st_f, st_m], semantics=("arbitrary",), side=side)


def flash_bwd(q_arr, qt_arr, k_arr, v_arr, do_arr, dot_arr, st, f_blocks, *, qoff, koff, voff, pair, scale, name, qblk=1024,
              side=None):
    t = q_arr.shape[0]
    blk = qt_arr.shape[2]
    qblk = max(min(qblk, t), blk)
    sub = qblk // blk
    nb, nbq = t // blk, t // qblk
    w = LANES if pair else 2 * LANES
    hw = w // 2
    has_bias = f_blocks is not None
    split = _halves if pair else _groups
    ins = [q_arr, qt_arr, k_arr, v_arr, do_arr, dot_arr, st] + ([f_blocks] if has_bias else [])
    n_out = 4 if has_bias else 3
    s_ins, s_outs = (side.ins, side.out_shapes) if side else ([], [])

    def wide(ref, first):
        parts = [ref[first + u] for u in range(sub)]
        return parts[0] if sub == 1 else jnp.concatenate(parts, axis=1)

    def body(*refs):
        main, si, outs, so, sems = _split_refs(refs, [len(ins), len(s_ins), n_out, len(s_outs), 2 if side else 0])
        q_ref, qt_ref, k_ref, v_ref, do_ref, dot_ref, st_ref = main[:7]
        dq_ref, dk_ref, dv_ref = outs[:3]
        g, j = pl.program_id(0), pl.program_id(1)
        step_id = g * nb + j
        if side:
            @pl.when(step_id == 0)
            def _():
                side.first(si, so, *sems)

        kk, vv = k_ref[...], v_ref[...]
        ks = [kk, kk] if pair else _groups(kk)
        if has_bias:
            f_ref, df_ref = main[7], outs[3]
            fk = [LOG2E * _pick_row(f_ref[0], 2 * g + n) for n in range(2)]

            @pl.when(step_id == 0)
            def _():
                df_ref[...] = jnp.zeros_like(df_ref)

        def step(i, carry, diagonal):
            rows = pl.ds(pl.multiple_of(i * qblk, qblk), qblk)
            qs = split((q_ref[rows, :].astype(F32) * (scale * LOG2E)).astype(BF16))
            qt = (wide(qt_ref, sub * i).astype(F32) * (scale * LOG2E)).astype(BF16)
            dos = [h.astype(BF16) for h in _halves(do_ref[rows, :].astype(F32))]
            dot = wide(dot_ref, sub * i)
            stats = st_ref[0, rows, :]
            new, dqs, row_sums = [], [], []
            for n in range(2):
                dkt, dvt, dfk = carry[n]
                s = lax.dot_general(qs[n], ks[n], NT, preferred_element_type=F32)
                if has_bias:
                    s = s - fk[n]
                if diagonal:
                    s = _below_diagonal(s, i * qblk - j * blk)
                p = jnp.exp2(s - stats[:, n:n + 1])
                dp = lax.dot_general(dos[n], vv, NT, preferred_element_type=F32)
                ds = p * (dp - stats[:, 2 + n:3 + n])
                dsb = ds.astype(BF16)
                dvt = dvt + jnp.dot(dot[n * HEAD_DIM:(n + 1) * HEAD_DIM], p.astype(BF16), preferred_element_type=F32)
                dkt = dkt + jnp.dot(qt[n * hw:(n + 1) * hw], dsb, preferred_element_type=F32)
                dqs.append(jnp.dot(dsb, ks[n], preferred_element_type=F32))
                if has_bias:
                    dfk = dfk - jnp.sum(ds, axis=0, keepdims=True)
                    row_sums.append(jnp.sum(ds, axis=1, keepdims=True))
                new.append((dkt, dvt, dfk))
            dq = (jnp.where(_lane() < HEAD_DIM, dqs[0], dqs[1]) if pair else jnp.concatenate(dqs, axis=1)) * scale
            if has_bias:
                df_ref[rows, :] += jnp.where(_lane() == 2 * g, row_sums[0], jnp.where(_lane() == 2 * g + 1, row_sums[1], 0.0))

            @pl.when(j == 0)
            def _():
                dq_ref[rows, :] = dq

            @pl.when(j > 0)
            def _():
                dq_ref[rows, :] += dq

            return tuple(new)

        init = tuple((jnp.zeros((hw, blk), F32), jnp.zeros((HEAD_DIM, blk), F32), jnp.zeros((1, blk), F32)) for _ in range(2))
        first = j // sub
        carry = step(first, init, True)
        (dka, dva, dfa), (dkb, dvb, dfb) = lax.fori_loop(first + 1, nbq, lambda i, c: step(i, c, False), carry)
        dk_ref[...] = jnp.concatenate([dka, dkb], axis=0).T * LN2
        dv_ref[...] = jnp.concatenate([dva, dvb], axis=0).T
        if has_bias:
            row = lax.broadcasted_iota(jnp.int32, (LANES, 1), 0)
            by_head = jnp.where(row == 2 * g, dfa, jnp.where(row == 2 * g + 1, dfb, 0.0))
            df_ref[pl.ds(pl.multiple_of(j * blk, blk), blk), :] += by_head.T
        if side:
            @pl.when(step_id == PAIRS * nb - 1)
            def _():
                side.last(si, so, *sems)

    in_specs = [pl.BlockSpec((t, w), lambda g, j: (0, qoff + g)), pl.BlockSpec((nb, w, blk), lambda g, j: (0, qoff + g, 0)),
                pl.BlockSpec((blk, w), lambda g, j: (j, koff + g)), pl.BlockSpec((blk, LANES), lambda g, j: (j, voff + g)),
                pl.BlockSpec((t, LANES), lambda g, j: (0, g)), pl.BlockSpec((nb, LANES, blk), lambda g, j: (0, g, 0)),
                pl.BlockSpec((1, t, LANES), lambda g, j: (g, 0, 0))]
    out_specs = [pl.BlockSpec((t, w), lambda g, j: (0, g)), pl.BlockSpec((blk, w), lambda g, j: (j, g)),
                 pl.BlockSpec((blk, LANES), lambda g, j: (j, g))]
    out_shape = [jax.ShapeDtypeStruct((t, PAIRS * w), F32)] * 2 + [jax.ShapeDtypeStruct((t, PAIRS * LANES), F32)]
    if has_bias:
        in_specs.append(pl.BlockSpec((1, N_HEADS, blk), lambda g, j: (j, 0, 0)))
        out_specs.append(pl.BlockSpec((t, LANES), lambda g, j: (0, 0)))
        out_shape.append(jax.ShapeDtypeStruct((t, LANES), F32))
    s_in_specs, s_out_specs = side.specs() if side else ([], [])
    return pl.pallas_call(
        body, name=name, grid=(PAIRS, nb), in_specs=in_specs + s_in_specs, out_specs=out_specs + s_out_specs,
        out_shape=out_shape + list(s_outs), scratch_shapes=side.sems() if side else [],
        compiler_params=_params(("arbitrary", "arbitrary")),
    )(*ins, *s_ins)


def _adamw_math(w, g, m, v):
    nm = ADAM_B1 * m + (1.0 - ADAM_B1) * g
    nv = ADAM_B2 * v + (1.0 - ADAM_B2) * (g * g)
    m_hat = nm / (1.0 - ADAM_B1 ** ADAM_STEP)
    v_hat = nv / (1.0 - ADAM_B2 ** ADAM_STEP)
    return -ADAM_LR * (m_hat / (jnp.sqrt(v_hat) + ADAM_EPS) + ADAM_WD * w), nm, nv


def adamw(w, g, m, v, *, name):
    rws, cols = w.shape
    br = _row_block(rws)

    def body(w_ref, g_ref, m_ref, v_ref, d_ref, nm_ref, nv_ref):
        d_ref[...], nm_ref[...], nv_ref[...] = _adamw_math(w_ref[...], g_ref[...], m_ref[...], v_ref[...])

    blk = pl.BlockSpec((br, cols), lambda i: (i, 0))
    return pl.pallas_call(
        body, name=name, grid=(rws // br,), in_specs=[blk] * 4, out_specs=[blk] * 3,
        out_shape=[jax.ShapeDtypeStruct((rws, cols), F32)] * 3,
        compiler_params=_params(("parallel",)),
    )(w, g, m, v)


def adamw_halves(w, g_mine, g_other, m, v, core, *, name):
    _, k, n = w.shape
    br = _row_block(k // 2)
    nh = k // 2 // br

    def body(c_ref, w_ref, gm_ref, go_ref, m_ref, v_ref, g_out, d_ref, nm_ref, nv_ref):
        gv = jnp.where(pl.program_id(0) == c_ref[0], gm_ref[...], go_ref[...])
        g_out[0] = gv
        d_ref[0], nm_ref[0], nv_ref[0] = _adamw_math(w_ref[0], gv, m_ref[0], v_ref[0])

    full = pl.BlockSpec((1, br, n), lambda hb, i, c: (0, hb * nh + i, 0))
    half = pl.BlockSpec((br, n), lambda hb, i, c: (i, 0))
    return pl.pallas_call(
        body, name=name,
        grid_spec=pltpu.PrefetchScalarGridSpec(num_scalar_prefetch=1, grid=(2, nh), in_specs=[full, half, half, full, full],
                                               out_specs=[full] * 4),
        out_shape=[jax.ShapeDtypeStruct(w.shape, F32)] * 4,
        compiler_params=_params(("parallel", "parallel")),
    )(core, w, g_mine, g_other, m, v)


def add_pair(dw, recv, core, *, name):
    n4, k, n = dw.shape
    hk = k // 2

    def body(c_ref, a_ref, b_ref, o_ref):
        o_ref[...] = (a_ref[...] + b_ref[...].astype(F32)).astype(BF16)

    return pl.pallas_call(
        body, name=name,
        grid_spec=pltpu.PrefetchScalarGridSpec(
            num_scalar_prefetch=1, grid=(n4,),
            in_specs=[pl.BlockSpec((1, hk, n), lambda q, c: (q, c[0], 0)), pl.BlockSpec((1, hk, n), lambda q, c: (q, 0, 0))],
            out_specs=pl.BlockSpec((1, hk, n), lambda q, c: (q, 0, 0))),
        out_shape=jax.ShapeDtypeStruct((n4, hk, n), BF16),
        compiler_params=_params(("parallel",)),
    )(core, dw, recv)


def sum_chips(parts, *, name):
    n4, r, n = parts.shape
    br = _row_block(r)

    def body(p_ref, o_ref):
        acc = p_ref[0].astype(F32)
        for q in range(1, n4):
            acc = acc + p_ref[q].astype(F32)
        o_ref[...] = acc

    return pl.pallas_call(
        body, name=name, grid=(r // br,),
        in_specs=[pl.BlockSpec((n4, br, n), lambda i: (0, i, 0))], out_specs=pl.BlockSpec((br, n), lambda i: (i, 0)),
        out_shape=jax.ShapeDtypeStruct((r, n), F32),
        compiler_params=_params(("parallel",)),
    )(parts)


ANY = pl.BlockSpec(memory_space=pl.ANY)


def _place():
    x, y, c = lax.axis_index("x"), lax.axis_index("y"), lax.axis_index("c")
    chips = [(1 - x, y), (x, 1 - y), (1 - x, 1 - y)]
    return x, y, c, chips


def _copy(src, dst, send_sems, recv_sems, k, to):
    return pltpu.make_async_remote_copy(src_ref=src, dst_ref=dst, send_sem=send_sems.at[k], recv_sem=recv_sems.at[k],
                                        device_id=to, device_id_type=MESH)


def _half_rows(ref, lead, hf):
    hk = ref.shape[1] // 2
    return ref.at[lead, pl.ds(hf * hk, hk), :]


def _gather_first(srcs, dsts, ssems, rsems):
    x, y, c, chips = _place()
    for ti, (s, d) in enumerate(zip(srcs, dsts)):
        hk = s.shape[0] // 2
        for j, (cx, cy) in enumerate(chips):
            _copy(s.at[pl.ds(c * hk, hk), :], _half_rows(d, 2 * x + y, c), ssems, rsems, 3 * ti + j, (cx, cy, c)).start()


def _gather_mid(srcs, dsts, ssems, rsems):
    x, y, c, chips = _place()
    n1 = 3 * len(srcs)
    for ti, d in enumerate(dsts):
        for j, (cx, cy) in enumerate(chips):
            landed = _half_rows(d, 2 * cx + cy, c)
            _copy(landed, landed, ssems, rsems, 3 * ti + j, (cx, cy, c)).wait_recv()
            _copy(landed, landed, ssems, rsems, n1 + 3 * ti + j, (x, y, 1 - c)).start()


def _gather_last(srcs, dsts, ssems, rsems):
    x, y, c, chips = _place()
    n1 = 3 * len(srcs)
    for ti, (s, d) in enumerate(zip(srcs, dsts)):
        hk = s.shape[0] // 2
        for j, (cx, cy) in enumerate(chips):
            other = _half_rows(d, 2 * cx + cy, 1 - c)
            _copy(other, other, ssems, rsems, n1 + 3 * ti + j, (x, y, 1 - c)).wait_recv()
        for j, (cx, cy) in enumerate(chips):
            mine = s.at[pl.ds(c * hk, hk), :]
            _copy(mine, mine, ssems, rsems, 3 * ti + j, (cx, cy, c)).wait_send()
            _copy(mine, mine, ssems, rsems, n1 + 3 * ti + j, (x, y, 1 - c)).wait_send()


def gather_side(shards):
    return Side(shards, [jax.ShapeDtypeStruct((N_CHIPS,) + s.shape, s.dtype) for s in shards], 6 * len(shards),
                _gather_first, _gather_last, _gather_mid)


def _scatter_first(srcs, dsts, ssems, rsems):
    x, y, c, chips = _place()
    for ti, (s, d) in enumerate(zip(srcs, dsts)):
        for j, (cx, cy) in enumerate(chips):
            _copy(s.at[2 * cx + cy], d.at[2 * x + y], ssems, rsems, 3 * ti + j, (cx, cy, c)).start()


def _scatter_last(srcs, dsts, ssems, rsems):
    x, y, c, chips = _place()
    for ti, (s, d) in enumerate(zip(srcs, dsts)):
        for j, (cx, cy) in enumerate(chips):
            _copy(s.at[2 * cx + cy], d.at[2 * cx + cy], ssems, rsems, 3 * ti + j, (cx, cy, c)).wait_recv()
        for j, (cx, cy) in enumerate(chips):
            _copy(s.at[2 * cx + cy], d.at[2 * cx + cy], ssems, rsems, 3 * ti + j, (cx, cy, c)).wait_send()


def scatter_side(parts):
    return Side(parts, [jax.ShapeDtypeStruct(p.shape, p.dtype) for p in parts], 3 * len(parts), _scatter_first, _scatter_last)


def run_side(side, *, name):
    n_in, n_out = len(side.ins), len(side.out_shapes)

    def body(*refs):
        si, so, sems = _split_refs(refs, [n_in, n_out, 2])
        side.first(si, so, *sems)
        if side.mid is not None:
            side.mid(si, so, *sems)
        side.last(si, so, *sems)

    in_specs, out_specs = side.specs()
    return pl.pallas_call(body, name=name, in_specs=in_specs, out_specs=out_specs, out_shape=side.out_shapes,
                          scratch_shapes=side.sems())(*side.ins)


def _swap_first(srcs, dsts, ssems, rsems):
    x, y, c, _ = _place()
    for k, (s, d) in enumerate(zip(srcs, dsts)):
        _copy(s, d, ssems, rsems, k, (x, y, 1 - c)).start()


def _swap_last(srcs, dsts, ssems, rsems):
    x, y, c, _ = _place()
    for k, (s, d) in enumerate(zip(srcs, dsts)):
        _copy(s, d, ssems, rsems, k, (x, y, 1 - c)).wait()


def swap_side(xs):
    return Side(xs, [jax.ShapeDtypeStruct(a.shape, a.dtype) for a in xs], len(xs), _swap_first, _swap_last)


def allreduce_small(s):
    n_dev = 8

    def body(s_ref, out_ref, buf, send_sems, recv_sems):
        x, y, c, _ = _place()
        me = 4 * x + 2 * y + c
        buf[me] = s_ref[...]
        sends = []
        for k in range(1, n_dev):
            px = 1 - x if k & 4 else x
            py = 1 - y if k & 2 else y
            pc = 1 - c if k & 1 else c
            cp = _copy(s_ref, buf.at[me], send_sems, recv_sems, k - 1, (px, py, pc))
            cp.start()
            sends.append((cp, 4 * px + 2 * py + pc))
        for k, (cp, peer) in enumerate(sends):
            _copy(s_ref, buf.at[peer], send_sems, recv_sems, k, (x, y, c)).wait_recv()
        for cp, _ in sends:
            cp.wait_send()
        acc = buf[0]
        for d in range(1, n_dev):
            acc = acc + buf[d]
        out_ref[...] = acc

    vm = pl.BlockSpec(memory_space=pltpu.VMEM)
    return pl.pallas_call(
        body, name="allreduce_small", in_specs=[vm], out_specs=vm,
        out_shape=jax.ShapeDtypeStruct(s.shape, F32),
        scratch_shapes=[pltpu.VMEM((n_dev,) + s.shape, F32), pltpu.SemaphoreType.DMA((n_dev - 1,)),
                        pltpu.SemaphoreType.DMA((n_dev - 1,))],
    )(s)


def join_cols(sm):
    n4, k, n = sm.shape
    return sm.transpose(1, 0, 2).reshape(k, n4 * n)


def split_cols(full):
    k, n = full.shape
    return full.reshape(k, N_CHIPS, n // N_CHIPS).transpose(1, 0, 2)


def _pad_heads(w, width):
    lead = w.shape[:-1]
    w = w.reshape(lead + (N_HEADS, width))
    return jnp.pad(w, [(0, 0)] * len(lead) + [(0, 0), (0, LANES - width)]).reshape(lead + (HW,))


def _unpad_heads(w, width):
    lead = w.shape[:-1]
    return w.reshape(lead + (N_HEADS, LANES))[..., :width].reshape(lead + (N_HEADS * width,))


def split_w_in(w_in):
    d = w_in.shape[0]
    o_f = 3 * FW
    o_cq = o_f + N_HEADS
    o_ckv = o_cq + Q_RANK
    o_kr = o_ckv + KV_RANK

    def z(n):
        return jnp.zeros((d, n), w_in.dtype)

    small = jnp.concatenate([w_in[:, o_cq:o_ckv], w_in[:, o_ckv:o_kr], z(HEAD_DIM), w_in[:, o_kr:], z(LANES - HEAD_DIM - ROPE_DIM),
                             w_in[:, o_f:o_cq], z(LANES - N_HEADS), z(LANES)], axis=1)
    return w_in[:, :o_f], small


def join_w_in(d_qkv, d_small):
    kr = S_KR + HEAD_DIM
    return jnp.concatenate([d_qkv, d_small[:, S_F:S_F + N_HEADS], d_small[:, S_CQ:S_CKV], d_small[:, S_CKV:S_KR],
                            d_small[:, kr:kr + ROPE_DIM]], axis=1)


def rope_tables(pos):
    t = pos.shape[0]
    inv_freq = ROPE_THETA ** (-jnp.arange(0, ROPE_DIM, 2, dtype=F32) / ROPE_DIM)
    ang = pos.astype(F32)[:, None] * inv_freq
    cos, sin = jnp.cos(ang), jnp.sin(ang)
    half = ROPE_DIM // 2

    def z(n):
        return jnp.zeros((t, n), F32)

    tab_c = jnp.concatenate([jnp.ones((t, HEAD_DIM), F32), cos, cos, z(LANES - HEAD_DIM - ROPE_DIM)], axis=1)
    tab_a = jnp.concatenate([z(HEAD_DIM), -sin, z(half), z(LANES - HEAD_DIM - ROPE_DIM)], axis=1)
    tab_b = jnp.concatenate([z(HEAD_DIM), z(half), sin, z(LANES - HEAD_DIM - ROPE_DIM)], axis=1)
    return tab_c, tab_a, tab_b


def _pad_lanes(v, n):
    return jnp.pad(v, ((0, 0), (0, n - v.shape[1])))


ATTN_BLK = 512
ATTN_FWD_BLK = 1024
ATTN_BWD_QBLK = 1024


def local_step(xs, pos, tgt, gains, w_early, late_weights, fwd_side=None, reduction=None):
    g_attn, b_forget, g_q, g_kv, g_fo, g_mo, g_mlp, g_fin = gains
    w_in, w_uq, w_ukv = w_early
    t = xs.shape[0]
    blk = min(ATTN_BLK, t)
    fox_scale = 1.0 / (HEAD_DIM ** 0.5)
    mla_scale = 1.0 / ((HEAD_DIM + ROPE_DIM) ** 0.5)

    w_qkv, w_small = split_w_in(w_in)
    w_uq_p = _pad_heads(w_uq, HEAD_DIM + ROPE_DIM)
    kv = w_ukv.reshape(KV_RANK, N_HEADS, 2 * HEAD_DIM)
    w_ukv_p = jnp.concatenate([_pad_heads(kv[:, :, :HEAD_DIM].reshape(KV_RANK, FW), HEAD_DIM),
                               kv[:, :, HEAD_DIM:].reshape(KV_RANK, FW)], axis=1)
    b_f = _pad_lanes(b_forget, LANES)
    tab_c, tab_a, tab_b = rope_tables(pos)

    h1 = rmsnorm(xs, g_attn, out_dtype=BF16, name="norm_attn")
    qkv, = mm(h1, w_qkv, out_dtypes=[BF16], name="proj_qkv")
    small, = mm(h1, w_small, out_dtypes=[F32], name="proj_small")
    mq, mk, mv, lf, cqn, ckvn = mla_prep(small, g_q, g_kv, w_uq_p, w_ukv_p, tab_c, tab_a, tab_b, b_f, name="mla_prep")
    f_cum = cumsum_rows(lf, reverse=False, name="gate_cumsum")
    f_blocks = f_cum[:, :N_HEADS].reshape(t // blk, blk, N_HEADS).transpose(0, 2, 1)
    qkv_t, mq_t, mv_t = to_blocks_t(qkv, blk), to_blocks_t(mq, blk), to_blocks_t(mv, blk)
    fo, st_f, *gathered = flash_fwd(qkv_t, qkv, qkv_t, f_cum, qoff=0, koff=PAIRS, voff=2 * PAIRS, pair=True,
                                    scale=fox_scale, name="fox_fwd", blk=ATTN_FWD_BLK, side=fwd_side)
    w_o, w_up, w_down = late_weights(gathered)
    mo, st_m = flash_fwd(mq_t, mk, mv_t, None, qoff=0, koff=0, voff=0, pair=False, scale=mla_scale, name="mla_fwd",
                         blk=ATTN_FWD_BLK)
    mixed = mix_norm(fo, mo, g_fo, g_mo, name="norm_mix")

    def inv_rms(v):
        return lax.rsqrt(jnp.mean(v * v, axis=-1, keepdims=True) + EPS)

    def residual_then_norm(acc, res, g):
        xn = acc + res
        return xn, xn * inv_rms(xn) * g

    def norm_bwd(dh, xn, res, g):
        r = inv_rms(xn)
        uu = dh * g
        return (r * uu - xn * (r * r * r * jnp.mean(uu * xn, axis=-1, keepdims=True)) + res,
                jnp.sum(dh * (xn * r), axis=0, keepdims=True))

    def norm_bwd2(dh, xn, res, g):
        dx, dg = norm_bwd(dh, xn, res, g)
        return dx, dx, dg

    def residual_then_loss(acc, res, target, g):
        xn = acc + res
        r = inv_rms(xn)
        xh = xn * r
        e = xh * g - target
        part = 0.5 * jnp.sum(jnp.mean(e * e, axis=-1, keepdims=True), axis=0, keepdims=True)
        dy = e * (1.0 / xn.shape[1])
        uu = dy * g
        dx = r * uu - xn * (r * r * r * jnp.mean(uu * xn, axis=-1, keepdims=True))
        return dx, dx, jnp.sum(dy * xh, axis=0, keepdims=True), part + jnp.zeros_like(g)

    x1, h2 = mm(mixed, w_o, extras=[xs], vecs=[g_mlp], epilogue=residual_then_norm, out_dtypes=[F32, BF16], name="out_proj")

    def relu2(acc):
        r = jnp.maximum(acc, 0.0)
        return acc, r * r

    u, act = mm(h2, w_up, epilogue=relu2, out_dtypes=[BF16, BF16], name="mlp_up")
    dx2, dx2b, dg_fin, loss_row = mm(act, w_down, extras=[x1, tgt], vecs=[g_fin], epilogue=residual_then_loss,
                                     out_dtypes=[F32, BF16], n_sums=2, name="mlp_down_loss")
    loss = loss_row[:, :1]

    def relu2_grad(acc, uu):
        return (acc * (2.0 * jnp.maximum(uu.astype(F32), 0.0)),)

    du, = mm(dx2b, w_down, trans_b=True, extras=[u], epilogue=relu2_grad, out_dtypes=[BF16], name="mlp_down_bwd")
    dw_down = mm_tn(act, dx2b, name="dw_down").reshape(N_CHIPS, -1, w_down.shape[1])
    dx1, dx1b, dg_mlp = mm(du, w_up, trans_b=True, extras=[x1, dx2], vecs=[g_mlp], epilogue=norm_bwd2,
                           out_dtypes=[F32, BF16], n_sums=1, name="mlp_up_bwd")
    dw_up = mm_tn(h2, du, name="dw_up", col_shards=N_CHIPS)

    dmixed, = mm(dx1b, w_o, trans_b=True, out_dtypes=[F32], name="out_proj_bwd")
    dw_o = mm_tn(mixed, dx1b, name="dw_o").reshape(N_CHIPS, -1, w_o.shape[1])
    late = (dw_o, dw_up, dw_down)
    red = reduction
    dfo, dmo, dg_fo, dg_mo, st_f, st_m, *got = mix_norm_bwd(dmixed, fo, mo, g_fo, g_mo, st_f, st_m, name="norm_mix_bwd",
                                                            side=red.late_swap(late) if red else None)
    dfq, dfk, dfv, d_f, *got = flash_bwd(
        qkv, qkv_t, qkv, qkv, dfo, to_blocks_t(dfo, blk), st_f, f_blocks, qoff=0, koff=PAIRS, voff=2 * PAIRS, pair=True,
        scale=fox_scale, name="fox_bwd", qblk=ATTN_BWD_QBLK, side=red.late_scatter(got) if red else None)
    dmq, dmk, dmv, *got = flash_bwd(mq, mq_t, mk, mv, dmo, to_blocks_t(dmo, blk), st_m, None, qoff=0, koff=0, voff=0,
                                    pair=False, scale=mla_scale, name="mla_bwd", qblk=ATTN_BWD_QBLK,
                                    side=red.late_halves(got) if red else None)
    if red:
        red.late_done(got)
    dqkv = jnp.concatenate([dfq, dfk, dfv], axis=1).astype(BF16)
    dlf = cumsum_rows(d_f, reverse=True, name="gate_cumsum_bwd")
    dsmall, dq_u, dkv_u, dg_q, dg_kv, db_f = mla_prep_bwd(dmq, dmk, dmv, dlf, small, g_q, g_kv, w_uq_p, w_ukv_p,
                                                          tab_c, tab_a, tab_b, b_f, name="mla_prep_bwd")
    dw_uq_p = mm_tn(cqn, dq_u, name="dw_uq")
    dw_ukv_p = mm_tn(ckvn, dkv_u, name="dw_ukv")

    dw_qkv = mm_tn(h1, dqkv, name="dw_qkv")
    dw_small = mm_tn(h1, dsmall, name="dw_small")
    dw_in = join_w_in(dw_qkv, dw_small)
    dw_uq = _unpad_heads(dw_uq_p, HEAD_DIM + ROPE_DIM)
    dk_cols = _unpad_heads(dw_ukv_p[:, :HW], HEAD_DIM).reshape(KV_RANK, N_HEADS, HEAD_DIM)
    dv_cols = dw_ukv_p[:, HW:].reshape(KV_RANK, N_HEADS, HEAD_DIM)
    dw_ukv = jnp.concatenate([dk_cols, dv_cols], axis=2).reshape(KV_RANK, N_HEADS * 2 * HEAD_DIM)
    early = tuple(split_cols(d) for d in (dw_in, dw_uq, dw_ukv))
    grad_x, dg_attn, *got = mm([dqkv, dsmall], [w_qkv, w_small], trans_b=True, extras=[xs, dx1], vecs=[g_attn],
                               epilogue=norm_bwd, out_dtypes=[F32], n_sums=1, name="proj_bwd",
                               side=red.early_scatter(early) if red else None)
    if red:
        red.early_done(got)
    d_gains = (dg_attn, db_f[:, :N_HEADS], dg_q, dg_kv, dg_fo, dg_mo, dg_mlp, dg_fin)
    return loss, grad_x, early, late, d_gains


class GradReduction:
    def __init__(self, core_id, chip):
        self.core_id, self.chip = core_id, chip
        self.core = core_id.reshape(1).astype(jnp.int32)
        self.grads, self.pairs, self.halves, self.others = {}, {}, {}, {}

    def _other_halves(self, grads):
        return [lax.dynamic_slice_in_dim(g, (1 - self.core_id) * (g.shape[1] // 2), g.shape[1] // 2, axis=1).astype(BF16)
                for g in grads]

    def _add_pairs(self, group, recvs):
        self.pairs[group] = [add_pair(g, r, self.core, name="add_pair_%s_%d" % (group, n))
                             for n, (g, r) in enumerate(zip(self.grads[group], recvs))]
        return scatter_side(self.pairs[group])

    def _chip_sums(self, group, scattered):
        chip = self.chip
        with_mine = [lax.dynamic_update_index_in_dim(s, lax.dynamic_index_in_dim(p, chip, 0, keepdims=True), chip, 0)
                     for s, p in zip(scattered, self.pairs[group])]
        self.halves[group] = [sum_chips(s, name="sum_chips_%s_%d" % (group, n)) for n, s in enumerate(with_mine)]
        return self.halves[group]

    def late_swap(self, grads):
        self.grads["late"] = list(grads)
        return swap_side(self._other_halves(grads))

    def late_scatter(self, recvs):
        return self._add_pairs("late", recvs)

    def late_halves(self, scattered):
        return swap_side(self._chip_sums("late", scattered))

    def late_done(self, others):
        self.others["late"] = list(others)

    def early_scatter(self, grads):
        self.grads["early"] = list(grads)
        return self._add_pairs("early", run_side(swap_side(self._other_halves(grads)), name="swap_early_sends"))

    def early_done(self, scattered):
        self.others["early"] = list(run_side(swap_side(self._chip_sums("early", scattered)), name="swap_early_halves"))

def kernel(x, positions, attn_norm_g, w_in, b_forget, q_norm_g, w_uq, kv_norm_g, w_ukv, fox_out_g, mla_out_g, w_o, mlp_norm_g, w_up, w_down, final_norm_g, loss_target, m_attn_norm_g, m_w_in, m_b_forget, m_q_norm_g, m_w_uq, m_kv_norm_g, m_w_ukv, m_fox_out_g, m_mla_out_g, m_w_o, m_mlp_norm_g, m_w_up, m_w_down, m_final_norm_g, v_attn_norm_g, v_w_in, v_b_forget, v_q_norm_g, v_w_uq, v_kv_norm_g, v_w_ukv, v_fox_out_g, v_mla_out_g, v_w_o, v_mlp_norm_g, v_w_up, v_w_down, v_final_norm_g):
    core_id = lax.axis_index("c")
    core = core_id.reshape(1).astype(jnp.int32)
    chip = 2 * lax.axis_index("x") + lax.axis_index("y")
    big = [w_in, w_uq, w_ukv, w_o, w_up, w_down]
    big_m = [m_w_in, m_w_uq, m_w_ukv, m_w_o, m_w_up, m_w_down]
    big_v = [v_w_in, v_w_uq, v_w_ukv, v_w_o, v_w_up, v_w_down]
    n_early = 3

    def vec(a):
        return a.reshape(1, -1)

    small = [attn_norm_g, b_forget, q_norm_g, kv_norm_g, fox_out_g, mla_out_g, mlp_norm_g, final_norm_g]
    small_m = [m_attn_norm_g, m_b_forget, m_q_norm_g, m_kv_norm_g, m_fox_out_g, m_mla_out_g, m_mlp_norm_g, m_final_norm_g]
    small_v = [v_attn_norm_g, v_b_forget, v_q_norm_g, v_kv_norm_g, v_fox_out_g, v_mla_out_g, v_mlp_norm_g, v_final_norm_g]
    gains = [vec(a) for a in small]

    shards = [w[0].astype(BF16) for w in big]

    def with_own(gathered, mine):
        return [lax.dynamic_update_index_in_dim(g, s, chip, 0) for g, s in zip(gathered, mine)]

    early = with_own(run_side(gather_side(shards[:n_early]), name="gather_early"), shards[:n_early])
    w_early = [join_cols(g) for g in early]

    def late_weights(gathered):
        g_o, g_up, g_down = with_own(gathered, shards[n_early:])
        return g_o.reshape(-1, g_o.shape[2]), join_cols(g_up), g_down.reshape(-1, g_down.shape[2])

    reduction = GradReduction(core_id, chip)
    loss, grad_x, _, _, d_small = local_step(
        x[0], positions[0], loss_target[0], gains, w_early, late_weights, gather_side(shards[n_early:]), reduction)
    halves = reduction.halves["early"] + reduction.halves["late"]
    others = reduction.others["early"] + reduction.others["late"]

    def rows8(vs):
        return jnp.concatenate([_pad_lanes(vec(a).astype(F32), 1024) for a in vs], axis=0)

    with_loss = [jnp.concatenate([d, loss], axis=1) if n == 1 else d for n, d in enumerate(d_small)]
    g_small8 = allreduce_small(rows8(with_loss))

    outs_big = [adamw_halves(w, gm, go, m, v, core, name="adamw_%d" % n)
                for n, (w, gm, go, m, v) in enumerate(zip(big, halves, others, big_m, big_v))]
    d8, m8, v8 = adamw(rows8(small), g_small8, rows8(small_m), rows8(small_v), name="adamw_small")

    def unrows8(a8):
        return [a8[n, :s.size].reshape(s.shape) for n, s in enumerate(small)]

    loss_all = g_small8[1, N_HEADS]
    grads, deltas, new_m, new_v = [None] * 14, [None] * 14, [None] * 14, [None] * 14
    big_at = [1, 4, 6, 9, 11, 12]
    small_at = [0, 2, 3, 5, 7, 8, 10, 13]
    for n, at in enumerate(big_at):
        grads[at], deltas[at], new_m[at], new_v[at] = outs_big[n]
    for at, g, dd, mm_, vv in zip(small_at, unrows8(g_small8), unrows8(d8), unrows8(m8), unrows8(v8)):
        grads[at], deltas[at], new_m[at], new_v[at] = g, dd, mm_, vv
    return (loss_all, grad_x[None], *grads, *deltas, *new_m, *new_v)
```

```python
import jax
import jax.numpy as jnp
from jax import lax
from jax.experimental import pallas as pl
from jax.experimental.pallas import tpu as pltpu

F32 = jnp.float32
BF16 = jnp.bfloat16
MESH = pl.DeviceIdType.MESH

EPS = 1e-6
ROPE_THETA = 10000.0
N_HEADS = 8
PAIRS = N_HEADS // 2
HEAD_DIM = 64
ROPE_DIM = 32
LANES = 128
Q_RANK = 384
KV_RANK = 256
N_CHIPS = 4
ADAM_LR, ADAM_B1, ADAM_B2, ADAM_EPS, ADAM_WD, ADAM_STEP = 0.001, 0.9, 0.999, 1e-08, 0.01, 10
VMEM_LIMIT = 48 * 1024 * 1024
LOG2E = 1.4426950408889634
LN2 = 0.6931471805599453
NN = (((1,), (0,)), ((), ()))
NT = (((1,), (1,)), ((), ()))
TN = (((0,), (0,)), ((), ()))


def _params(sem=None):
    return pltpu.CompilerParams(dimension_semantics=sem, vmem_limit_bytes=VMEM_LIMIT)


def _fit(block, dim):
    if dim <= block:
        return dim
    return next(b for b in range(block - block % LANES, 0, -LANES) if dim % b == 0)


def _row_block(rows):
    return next(b for b in (256, 128, 64, 32, 16, 8) if rows % b == 0)


def gridded(body, *, name, grid, in_specs, out_specs, out_shape, ins, semantics, side=None):
    if side is None:
        return pl.pallas_call(body, name=name, grid=grid, in_specs=in_specs, out_specs=out_specs, out_shape=out_shape,
                              compiler_params=_params(semantics))(*ins)
    n_in, n_out = len(in_specs), len(out_specs)
    steps = 1
    for extent in grid:
        steps *= extent

    def riding(*refs):
        main_in, s_in, main_out, s_out, sems = _split_refs(refs, [n_in, len(side.ins), n_out, len(side.out_shapes), 2])
        step = 0
        for axis, extent in enumerate(grid):
            step = step * extent + pl.program_id(axis)

        @pl.when(step == 0)
        def _():
            side.first(s_in, s_out, *sems)

        body(*main_in, *main_out)

        @pl.when(step == steps - 1)
        def _():
            side.last(s_in, s_out, *sems)

    s_in_specs, s_out_specs = side.specs()
    return pl.pallas_call(
        riding, name=name, grid=grid, in_specs=list(in_specs) + s_in_specs, out_specs=list(out_specs) + s_out_specs,
        out_shape=list(out_shape) + side.out_shapes, scratch_shapes=side.sems(),
        compiler_params=_params(("arbitrary",) * len(grid)))(*ins, *side.ins)


def rmsnorm(x, g, *, out_dtype, name, bt=512):
    t, d = x.shape
    bt = min(bt, t)

    def body(x_ref, g_ref, o_ref):
        xv = x_ref[...].astype(F32)
        r = lax.rsqrt(jnp.mean(xv * xv, axis=-1, keepdims=True) + EPS)
        o_ref[...] = (xv * r * g_ref[...]).astype(o_ref.dtype)

    return pl.pallas_call(
        body, name=name, grid=(t // bt,),
        in_specs=[pl.BlockSpec((bt, d), lambda i: (i, 0)), pl.BlockSpec((1, d), lambda i: (0, 0))],
        out_specs=pl.BlockSpec((bt, d), lambda i: (i, 0)),
        out_shape=jax.ShapeDtypeStruct((t, d), out_dtype),
        compiler_params=_params(("parallel",)),
    )(x, g)


def mm(a, b, *, trans_b=False, extras=(), vecs=(), epilogue=None, out_dtypes, n_sums=0, name, bm=1024, bn=1024, side=None):
    a_list = list(a) if isinstance(a, (list, tuple)) else [a]
    b_list = list(b) if isinstance(b, (list, tuple)) else [b]
    m = a_list[0].shape[0]
    n = b_list[0].shape[0] if trans_b else b_list[0].shape[1]
    ks = [x.shape[1] for x in a_list]
    if sum(ks) > 2048:
        bm = bm // 2
    bm, bn = _fit(bm, m), _fit(bn, n)
    assert n_sums == 0 or bn == n
    n_ab, n_ex, n_vec, n_out = len(a_list), len(extras), len(vecs), len(out_dtypes)

    def body(*refs):
        a_refs, b_refs, ex, vs, outs, sums = _split_refs(refs, [n_ab, n_ab, n_ex, n_vec, n_out, n_sums])
        acc = None
        for a_ref, b_ref in zip(a_refs, b_refs):
            part = lax.dot_general(a_ref[...], b_ref[...], NT if trans_b else NN, preferred_element_type=F32)
            acc = part if acc is None else acc + part
        res = epilogue(acc, *[e[...] for e in ex], *[v[...] for v in vs]) if epilogue is not None else (acc,)
        for o, r in zip(outs, res[:n_out]):
            o[...] = r.astype(o.dtype)
        if n_sums:
            @pl.when(pl.program_id(0) == 0)
            def _():
                for s_ref in sums:
                    s_ref[...] = jnp.zeros_like(s_ref)

            for s_ref, r in zip(sums, res[n_out:]):
                s_ref[...] += r

    tile = pl.BlockSpec((bm, bn), lambda i, j: (i, j))
    vec = pl.BlockSpec((1, bn), lambda i, j: (0, j))
    a_specs = [pl.BlockSpec((bm, k), lambda i, j: (i, 0)) for k in ks]
    b_specs = [pl.BlockSpec((bn, k), lambda i, j: (j, 0)) if trans_b else pl.BlockSpec((k, bn), lambda i, j: (0, j)) for k in ks]
    return gridded(
        body, name=name, grid=(m // bm, n // bn),
        in_specs=a_specs + b_specs + [tile] * n_ex + [vec] * n_vec,
        out_specs=[tile] * n_out + [vec] * n_sums,
        out_shape=[jax.ShapeDtypeStruct((m, n), dt) for dt in out_dtypes] + [jax.ShapeDtypeStruct((1, n), F32)] * n_sums,
        ins=[*a_list, *b_list, *extras, *vecs],
        semantics=("arbitrary", "arbitrary") if n_sums else ("parallel", "parallel"), side=side)


def mm_tn(a, b, *, name, col_shards=1, bk=1024, bn=1024, bt=1024):
    t, k = a.shape
    n = b.shape[1]
    ns = n // col_shards
    bk, bn, bt = _fit(bk, k), _fit(bn, ns), _fit(bt, t)
    per = ns // bn

    def body(a_ref, b_ref, o_ref):
        @pl.when(pl.program_id(2) == 0)
        def _():
            o_ref[...] = jnp.zeros_like(o_ref)

        o_ref[...] += lax.dot_general(a_ref[...], b_ref[...], TN, preferred_element_type=F32)

    if col_shards == 1:
        out_spec = pl.BlockSpec((bk, bn), lambda i, j, s: (i, j))
        out_shape = jax.ShapeDtypeStruct((k, n), F32)
    else:
        out_spec = pl.BlockSpec((None, bk, bn), lambda i, j, s: (j // per, i, j % per))
        out_shape = jax.ShapeDtypeStruct((col_shards, k, ns), F32)
    return pl.pallas_call(
        body, name=name, grid=(k // bk, n // bn, t // bt),
        in_specs=[pl.BlockSpec((bt, bk), lambda i, j, s: (s, i)), pl.BlockSpec((bt, bn), lambda i, j, s: (s, j))],
        out_specs=out_spec, out_shape=out_shape,
        compiler_params=_params(("parallel", "parallel", "arbitrary")),
    )(a, b)


def _split3(x):
    hi = x.astype(BF16)
    r1 = x - hi.astype(F32)
    mid = r1.astype(BF16)
    lo = (r1 - mid.astype(F32)).astype(BF16)
    return hi, mid, lo


def cumsum_rows(x, *, reverse, name, bc=512):
    t, d = x.shape
    bc = min(bc, t)
    nb = t // bc

    def body(x_ref, o_ref, carry):
        @pl.when(pl.program_id(0) == 0)
        def _():
            carry[...] = jnp.zeros_like(carry)

        r = lax.broadcasted_iota(jnp.int32, (bc, bc), 0)
        c = lax.broadcasted_iota(jnp.int32, (bc, bc), 1)
        tri = jnp.where((r <= c) if reverse else (r >= c), 1.0, 0.0).astype(BF16)
        hi, mid, lo = _split3(x_ref[...])
        s = (lax.dot_general(tri, hi, NN, preferred_element_type=F32)
             + lax.dot_general(tri, mid, NN, preferred_element_type=F32)
             + lax.dot_general(tri, lo, NN, preferred_element_type=F32)) + carry[0:1, :]
        o_ref[...] = s
        carry[0:1, :] = s[0:1, :] if reverse else s[bc - 1:bc, :]

    imap = (lambda i: (nb - 1 - i, 0)) if reverse else (lambda i: (i, 0))
    return pl.pallas_call(
        body, name=name, grid=(nb,),
        in_specs=[pl.BlockSpec((bc, d), imap)], out_specs=pl.BlockSpec((bc, d), imap),
        out_shape=jax.ShapeDtypeStruct((t, d), F32),
        scratch_shapes=[pltpu.VMEM((8, d), F32)],
        compiler_params=_params(("arbitrary",)),
    )(x)


def _rope(x, c, a, b):
    return x * c + pltpu.roll(x, LANES - ROPE_DIM // 2, 1) * a + pltpu.roll(x, ROPE_DIM // 2, 1) * b


def _rope_bwd(d, c, a, b):
    return d * c + pltpu.roll(d * a, ROPE_DIM // 2, 1) + pltpu.roll(d * b, LANES - ROPE_DIM // 2, 1)


S_CQ, S_CKV, S_KR, S_F, S_END = 0, Q_RANK, Q_RANK + KV_RANK, Q_RANK + KV_RANK + LANES, 1024
HW = N_HEADS * LANES
FW = N_HEADS * HEAD_DIM


def mla_prep(small, g_q, g_kv, w_uq, w_ukv, tab_c, tab_a, tab_b, b_f, *, name, bt=512):
    t = small.shape[0]
    bt = min(bt, t)

    def body(s_ref, gq_ref, gkv_ref, wq_ref, wkv_ref, c_ref, a_ref, b_ref, bf_ref,
             mq_ref, mk_ref, mv_ref, lf_ref, cqn_ref, ckvn_ref):
        cq = s_ref[:, S_CQ:S_CKV]
        rq = lax.rsqrt(jnp.mean(cq * cq, axis=-1, keepdims=True) + EPS)
        cqn = (cq * rq * gq_ref[...]).astype(BF16)
        ckv = s_ref[:, S_CKV:S_KR]
        rkv = lax.rsqrt(jnp.mean(ckv * ckv, axis=-1, keepdims=True) + EPS)
        ckvn = (ckv * rkv * gkv_ref[...]).astype(BF16)
        cqn_ref[...] = cqn
        ckvn_ref[...] = ckvn
        tc, ta, tb = c_ref[...], a_ref[...], b_ref[...]
        q = jnp.dot(cqn, wq_ref[...], preferred_element_type=F32)
        kv = jnp.dot(ckvn, wkv_ref[...], preferred_element_type=F32)
        kr = _rope(s_ref[:, S_KR:S_F], tc, ta, tb)
        for h in range(N_HEADS):
            sl = slice(h * LANES, (h + 1) * LANES)
            mq_ref[:, sl] = _rope(q[:, sl], tc, ta, tb).astype(BF16)
            mk_ref[:, sl] = (kv[:, sl] + kr).astype(BF16)
        mv_ref[...] = kv[:, HW:].astype(BF16)
        z = s_ref[:, S_F:S_END - LANES] + bf_ref[...]
        lf_ref[...] = jnp.minimum(z, 0.0) - jnp.log(1.0 + jnp.exp(-jnp.abs(z)))

    def row(w):
        return pl.BlockSpec((bt, w), lambda i: (i, 0))

    def full(arr):
        return pl.BlockSpec(arr.shape, lambda i: (0, 0))

    return pl.pallas_call(
        body, name=name, grid=(t // bt,),
        in_specs=[row(S_END), full(g_q), full(g_kv), full(w_uq), full(w_ukv), row(LANES), row(LANES), row(LANES), full(b_f)],
        out_specs=[row(HW), row(HW), row(FW), row(LANES), row(Q_RANK), row(KV_RANK)],
        out_shape=[jax.ShapeDtypeStruct((t, HW), BF16)] * 2 + [jax.ShapeDtypeStruct((t, FW), BF16), jax.ShapeDtypeStruct((t, LANES), F32),
                   jax.ShapeDtypeStruct((t, Q_RANK), BF16), jax.ShapeDtypeStruct((t, KV_RANK), BF16)],
        compiler_params=_params(("parallel",)),
    )(small, g_q, g_kv, w_uq, w_ukv, tab_c, tab_a, tab_b, b_f)


def mla_prep_bwd(dmq, dmk, dmv, dlf, small, g_q, g_kv, w_uq, w_ukv, tab_c, tab_a, tab_b, b_f, *, name, bt=512):
    t = small.shape[0]
    bt = min(bt, t)

    def body(dmq_ref, dmk_ref, dmv_ref, dlf_ref, s_ref, gq_ref, gkv_ref, wq_ref, wkv_ref, c_ref, a_ref, b_ref, bf_ref,
             ds_ref, dq_ref, dkv_ref, dgq_ref, dgkv_ref, db_ref):
        tc, ta, tb = c_ref[...], a_ref[...], b_ref[...]
        lane = lax.broadcasted_iota(jnp.int32, (1, LANES), 1)
        dkr = jnp.zeros((bt, LANES), F32)
        for h in range(N_HEADS):
            sl = slice(h * LANES, (h + 1) * LANES)
            dq_ref[:, sl] = _rope_bwd(dmq_ref[:, sl], tc, ta, tb).astype(BF16)
            dkr = dkr + dmk_ref[:, sl]
        dkv_ref[:, :HW] = dmk_ref[...].astype(BF16)
        dkv_ref[:, HW:] = dmv_ref[...].astype(BF16)
        in_rope = (lane >= HEAD_DIM) & (lane < HEAD_DIM + ROPE_DIM)
        ds_ref[:, S_KR:S_F] = jnp.where(in_rope, _rope_bwd(dkr, tc, ta, tb), 0.0).astype(BF16)

        def norm_bwd(raw, g_ref, dn, dg_ref):
            r = lax.rsqrt(jnp.mean(raw * raw, axis=-1, keepdims=True) + EPS)
            u = dn * g_ref[...]
            dot = jnp.mean(u * raw, axis=-1, keepdims=True)
            dg_ref[...] += jnp.sum(dn * (raw * r), axis=0, keepdims=True)
            return r * u - raw * (r * r * r * dot)

        @pl.when(pl.program_id(0) == 0)
        def _():
            dgq_ref[...] = jnp.zeros_like(dgq_ref)
            dgkv_ref[...] = jnp.zeros_like(dgkv_ref)
            db_ref[...] = jnp.zeros_like(db_ref)

        dcqn = lax.dot_general(dq_ref[...], wq_ref[...], NT, preferred_element_type=F32)
        ds_ref[:, S_CQ:S_CKV] = norm_bwd(s_ref[:, S_CQ:S_CKV], gq_ref, dcqn, dgq_ref).astype(BF16)
        dckvn = lax.dot_general(dkv_ref[...], wkv_ref[...], NT, preferred_element_type=F32)
        ds_ref[:, S_CKV:S_KR] = norm_bwd(s_ref[:, S_CKV:S_KR], gkv_ref, dckvn, dgkv_ref).astype(BF16)
        z = s_ref[:, S_F:S_END - LANES] + bf_ref[...]
        dz = jnp.where(lane < N_HEADS, dlf_ref[...] / (1.0 + jnp.exp(z)), 0.0)
        db_ref[...] += jnp.sum(dz, axis=0, keepdims=True)
        ds_ref[:, S_F:S_END - LANES] = dz.astype(BF16)
        ds_ref[:, S_END - LANES:] = jnp.zeros((bt, LANES), BF16)

    def row(w):
        return pl.BlockSpec((bt, w), lambda i: (i, 0))

    def full(arr):
        return pl.BlockSpec(arr.shape, lambda i: (0, 0))

    def vec(w):
        return pl.BlockSpec((1, w), lambda i: (0, 0))

    return pl.pallas_call(
        body, name=name, grid=(t // bt,),
        in_specs=[row(HW), row(HW), row(FW), row(LANES), row(S_END), full(g_q), full(g_kv), full(w_uq), full(w_ukv),
                  row(LANES), row(LANES), row(LANES), full(b_f)],
        out_specs=[row(S_END), row(HW), row(HW + FW), vec(Q_RANK), vec(KV_RANK), vec(LANES)],
        out_shape=[jax.ShapeDtypeStruct((t, S_END), BF16), jax.ShapeDtypeStruct((t, HW), BF16),
                   jax.ShapeDtypeStruct((t, HW + FW), BF16), jax.ShapeDtypeStruct((1, Q_RANK), F32),
                   jax.ShapeDtypeStruct((1, KV_RANK), F32), jax.ShapeDtypeStruct((1, LANES), F32)],
        compiler_params=_params(("arbitrary",)),
    )(dmq, dmk, dmv, dlf, small, g_q, g_kv, w_uq, w_ukv, tab_c, tab_a, tab_b, b_f)


class Side:
    def __init__(self, ins, out_shapes, n_sems, first, last, mid=None):
        self.ins, self.out_shapes, self.n_sems = list(ins), list(out_shapes), n_sems
        self.first, self.mid, self.last = first, mid, last

    def specs(self):
        return [ANY] * len(self.ins), [ANY] * len(self.out_shapes)

    def sems(self):
        return [pltpu.SemaphoreType.DMA((self.n_sems,)), pltpu.SemaphoreType.DMA((self.n_sems,))]


def _lane():
    return lax.broadcasted_iota(jnp.int32, (1, LANES), 1)


def _halves(x):
    zero = jnp.zeros_like(x)
    return [jnp.where(_lane() < HEAD_DIM, x, zero), jnp.where(_lane() >= HEAD_DIM, x, zero)]


def _groups(x):
    return [x[:, :LANES], x[:, LANES:]]


def _pick_row(tile, h):
    row = lax.broadcasted_iota(jnp.int32, (tile.shape[0], 1), 0)
    return jnp.sum(jnp.where(row == h, tile, 0.0), axis=0, keepdims=True)


def _pick_lane(tile, h):
    return jnp.sum(jnp.where(_lane() == h, tile, 0.0), axis=1, keepdims=True)


def _row_halves(x):
    row = lax.broadcasted_iota(jnp.int32, (LANES, 1), 0)
    zero = jnp.zeros_like(x)
    return [jnp.where(row < HEAD_DIM, x, zero), jnp.where(row >= HEAD_DIM, x, zero)]


def _below_diagonal(s, lead=0):
    r = lax.broadcasted_iota(jnp.int32, s.shape, 0)
    c = lax.broadcasted_iota(jnp.int32, s.shape, 1)
    return jnp.where(c <= r + lead, s, -jnp.inf)


def _above_diagonal(s):
    r = lax.broadcasted_iota(jnp.int32, s.shape, 0)
    c = lax.broadcasted_iota(jnp.int32, s.shape, 1)
    return jnp.where(r <= c, s, -jnp.inf)


def to_blocks_t(x, blk):
    t, c = x.shape
    return x.reshape(t // blk, blk, c).transpose(0, 2, 1)


def _split_refs(refs, counts):
    out, at = [], 0
    for n in counts:
        out.append(refs[at:at + n])
        at += n
    return out


def flash_fwd(qt_arr, k_arr, vt_arr, f_cum, *, qoff, koff, voff, pair, scale, name, blk=512, side=None):
    t = k_arr.shape[0]
    tblk = qt_arr.shape[2]
    blk = max(min(blk, t), tblk)
    sub = blk // tblk
    nb = t // blk
    w = LANES if pair else 2 * LANES
    has_bias = f_cum is not None
    ins = [qt_arr, k_arr, vt_arr] + ([f_cum] if has_bias else [])

    def wide(ref, first):
        parts = [ref[first + u] for u in range(sub)]
        return parts[0] if sub == 1 else jnp.concatenate(parts, axis=1)
    s_ins, s_outs = (side.ins, side.out_shapes) if side else ([], [])

    def body(*refs):
        main, si, outs, so, sems = _split_refs(refs, [len(ins), len(s_ins), 2, len(s_outs), 2 if side else 0])
        qt_ref, k_ref, vt_ref = main[:3]
        f_ref = main[3] if has_bias else None
        o_ref, st_ref = outs
        g, i = pl.program_id(0), pl.program_id(1)
        step_id = g * nb + i
        if side:
            @pl.when(step_id == 0)
            def _():
                side.first(si, so, *sems)

            if side.mid is not None:
                @pl.when(step_id == (3 * PAIRS * nb) // 4)
                def _():
                    side.mid(si, so, *sems)

        qt = (wide(qt_ref, 0).astype(F32) * (scale * LOG2E)).astype(BF16)
        qts = _row_halves(qt) if pair else [qt[:LANES], qt[LANES:]]

        def step(j, carry, diagonal):
            rows = pl.ds(pl.multiple_of(j * blk, blk), blk)
            kk = k_ref[rows, :]
            ks = [kk, kk] if pair else _groups(kk)
            vt = wide(vt_ref, sub * j)
            out = []
            for n in range(2):
                m, l, acc = carry[n]
                s = jnp.dot(ks[n], qts[n], preferred_element_type=F32)
                if has_bias:
                    s = s - LOG2E * _pick_lane(f_ref[rows, :], 2 * g + n)
                if diagonal:
                    s = _above_diagonal(s)
                m_new = jnp.maximum(m, jnp.max(s, axis=0, keepdims=True))
                alpha = jnp.exp2(m - m_new)
                p = jnp.exp2(s - m_new)
                out.append((m_new, alpha * l + jnp.sum(p, axis=0, keepdims=True),
                            alpha * acc + jnp.dot(vt[n * HEAD_DIM:(n + 1) * HEAD_DIM], p.astype(BF16),
                                                  preferred_element_type=F32)))
            return tuple(out)

        init = tuple((jnp.full((1, blk), -jnp.inf, F32), jnp.zeros((1, blk), F32), jnp.zeros((HEAD_DIM, blk), F32))
                     for _ in range(2))
        carry = lax.fori_loop(0, i, lambda j, c: step(j, c, False), init)
        (ma, la, acca), (mb, lb, accb) = step(i, carry, True)
        o_ref[...] = jnp.concatenate([acca / la, accb / lb], axis=0).T
        row = lax.broadcasted_iota(jnp.int32, (LANES, 1), 0)
        st_ref[0] = jnp.where(row == 0, ma + jnp.log2(la), jnp.where(row == 1, mb + jnp.log2(lb), 0.0)).T
        if side:
            @pl.when(step_id == PAIRS * nb - 1)
            def _():
                side.last(si, so, *sems)

    in_specs = [pl.BlockSpec((sub, w, tblk), lambda g, i: (i, qoff + g, 0)), pl.BlockSpec((t, w), lambda g, i: (0, koff + g)),
                pl.BlockSpec((t // tblk, LANES, tblk), lambda g, i: (0, voff + g, 0))]
    if has_bias:
        in_specs.append(pl.BlockSpec((t, LANES), lambda g, i: (0, 0)))
    s_in_specs, s_out_specs = side.specs() if side else ([], [])
    return pl.pallas_call(
        body, name=name, grid=(PAIRS, nb), in_specs=in_specs + s_in_specs,
        out_specs=[pl.BlockSpec((blk, LANES), lambda g, i: (i, g)), pl.BlockSpec((1, blk, LANES), lambda g, i: (g, i, 0))]
        + s_out_specs,
        out_shape=[jax.ShapeDtypeStruct((t, PAIRS * LANES), F32), jax.ShapeDtypeStruct((PAIRS, t, LANES), F32)] + list(s_outs),
        scratch_shapes=side.sems() if side else [],
        compiler_params=_params(("arbitrary", "arbitrary")),
    )(*ins, *s_ins)


def mix_norm(fo, mo, g_fo, g_mo, *, name, bt=512):
    t, d = fo.shape
    bt = min(bt, t)

    def body(fo_ref, mo_ref, gf_ref, gm_ref, o_ref):
        for n, (x_ref, g_ref) in enumerate(((fo_ref, gf_ref), (mo_ref, gm_ref))):
            xv = x_ref[...]
            r = lax.rsqrt(jnp.mean(xv * xv, axis=-1, keepdims=True) + EPS)
            o_ref[:, n * d:(n + 1) * d] = (xv * r * g_ref[...]).astype(BF16)

    row = pl.BlockSpec((bt, d), lambda i: (i, 0))
    vec = pl.BlockSpec((1, d), lambda i: (0, 0))
    return pl.pallas_call(
        body, name=name, grid=(t // bt,), in_specs=[row, row, vec, vec],
        out_specs=pl.BlockSpec((bt, 2 * d), lambda i: (i, 0)),
        out_shape=jax.ShapeDtypeStruct((t, 2 * d), BF16),
        compiler_params=_params(("parallel",)),
    )(fo, mo, g_fo, g_mo)


def mix_norm_bwd(dmixed, fo, mo, g_fo, g_mo, st_f, st_m, *, name, bt=512, side=None):
    t, d = fo.shape
    bt = min(bt, t)

    def body(dm_ref, fo_ref, mo_ref, gf_ref, gm_ref, sf_ref, sm_ref, dfo_ref, dmo_ref, dgf_ref, dgm_ref, sfo_ref, smo_ref):
        @pl.when(pl.program_id(0) == 0)
        def _():
            dgf_ref[...] = jnp.zeros_like(dgf_ref)
            dgm_ref[...] = jnp.zeros_like(dgm_ref)

        groups = ((fo_ref, gf_ref, dfo_ref, dgf_ref, sf_ref, sfo_ref), (mo_ref, gm_ref, dmo_ref, dgm_ref, sm_ref, smo_ref))
        for n, (x_ref, g_ref, dx_ref, dg_ref, st_ref, sto_ref) in enumerate(groups):
            xv = x_ref[...]
            dhv = dm_ref[:, n * d:(n + 1) * d]
            r = lax.rsqrt(jnp.mean(xv * xv, axis=-1, keepdims=True) + EPS)
            u = dhv * g_ref[...]
            dxb = (r * u - xv * (r * r * r * jnp.mean(u * xv, axis=-1, keepdims=True))).astype(BF16)
            dx_ref[...] = dxb
            dg_ref[...] += jnp.sum(dhv * (xv * r), axis=0, keepdims=True)
            prod = xv * dxb.astype(F32)
            for g in range(PAIRS):
                grp = prod[:, g * LANES:(g + 1) * LANES]
                da = jnp.sum(jnp.where(_lane() < HEAD_DIM, grp, 0.0), axis=1, keepdims=True)
                db = jnp.sum(jnp.where(_lane() >= HEAD_DIM, grp, 0.0), axis=1, keepdims=True)
                sto_ref[g] = jnp.where(_lane() == 2, da, jnp.where(_lane() == 3, db, st_ref[g]))

    row = pl.BlockSpec((bt, d), lambda i: (i, 0))
    vec = pl.BlockSpec((1, d), lambda i: (0, 0))
    stat = pl.BlockSpec((PAIRS, bt, LANES), lambda i: (0, i, 0))
    return gridded(
        body, name=name, grid=(t // bt,),
        in_specs=[pl.BlockSpec((bt, 2 * d), lambda i: (i, 0)), row, row, vec, vec, stat, stat],
        out_specs=[row, row, vec, vec, stat, stat],
        out_shape=[jax.ShapeDtypeStruct((t, d), BF16)] * 2 + [jax.ShapeDtypeStruct((1, d), F32)] * 2
        + [jax.ShapeDtypeStruct(st_f.shape, F32)] * 2,
        ins=[dmixed, fo, mo, g_fo, g_mo, st_f, st_m], semantics=("arbitrary",), side=side)


def flash_bwd(q_arr, qt_arr, k_arr, v_arr, do_arr, dot_arr, st, f_blocks, *, qoff, koff, voff, pair, scale, name, qblk=1024,
              side=None):
    t = q_arr.shape[0]
    blk = qt_arr.shape[2]
    qblk = max(min(qblk, t), blk)
    sub = qblk // blk
    nb, nbq = t // blk, t // qblk
    w = LANES if pair else 2 * LANES
    hw = w // 2
    has_bias = f_blocks is not None
    split = _halves if pair else _groups
    ins = [q_arr, qt_arr, k_arr, v_arr, do_arr, dot_arr, st] + ([f_blocks] if has_bias else [])
    n_out = 4 if has_bias else 3
    s_ins, s_outs = (side.ins, side.out_shapes) if side else ([], [])

    def wide(ref, first):
        parts = [ref[first + u] for u in range(sub)]
        return parts[0] if sub == 1 else jnp.concatenate(parts, axis=1)

    def body(*refs):
        main, si, outs, so, sems = _split_refs(refs, [len(ins), len(s_ins), n_out, len(s_outs), 2 if side else 0])
        q_ref, qt_ref, k_ref, v_ref, do_ref, dot_ref, st_ref = main[:7]
        dq_ref, dk_ref, dv_ref = outs[:3]
        g, j = pl.program_id(0), pl.program_id(1)
        step_id = g * nb + j
        if side:
            @pl.when(step_id == 0)
            def _():
                side.first(si, so, *sems)

        kk, vv = k_ref[...], v_ref[...]
        ks = [kk, kk] if pair else _groups(kk)
        if has_bias:
            f_ref, df_ref = main[7], outs[3]
            fk = [LOG2E * _pick_row(f_ref[0], 2 * g + n) for n in range(2)]

            @pl.when(step_id == 0)
            def _():
                df_ref[...] = jnp.zeros_like(df_ref)

        def step(i, carry, diagonal):
            rows = pl.ds(pl.multiple_of(i * qblk, qblk), qblk)
            qs = split((q_ref[rows, :].astype(F32) * (scale * LOG2E)).astype(BF16))
            qt = (wide(qt_ref, sub * i).astype(F32) * (scale * LOG2E)).astype(BF16)
            dos = [h.astype(BF16) for h in _halves(do_ref[rows, :].astype(F32))]
            dot = wide(dot_ref, sub * i)
            stats = st_ref[0, rows, :]
            new, dqs, row_sums = [], [], []
            for n in range(2):
                dkt, dvt, dfk = carry[n]
                s = lax.dot_general(qs[n], ks[n], NT, preferred_element_type=F32)
                if has_bias:
                    s = s - fk[n]
                if diagonal:
                    s = _below_diagonal(s, i * qblk - j * blk)
                p = jnp.exp2(s - stats[:, n:n + 1])
                dp = lax.dot_general(dos[n], vv, NT, preferred_element_type=F32)
                ds = p * (dp - stats[:, 2 + n:3 + n])
                dsb = ds.astype(BF16)
                dvt = dvt + jnp.dot(dot[n * HEAD_DIM:(n + 1) * HEAD_DIM], p.astype(BF16), preferred_element_type=F32)
                dkt = dkt + jnp.dot(qt[n * hw:(n + 1) * hw], dsb, preferred_element_type=F32)
                dqs.append(jnp.dot(dsb, ks[n], preferred_element_type=F32))
                if has_bias:
                    dfk = dfk - jnp.sum(ds, axis=0, keepdims=True)
                    row_sums.append(jnp.sum(ds, axis=1, keepdims=True))
                new.append((dkt, dvt, dfk))
            dq = (jnp.where(_lane() < HEAD_DIM, dqs[0], dqs[1]) if pair else jnp.concatenate(dqs, axis=1)) * scale
            if has_bias:
                df_ref[rows, :] += jnp.where(_lane() == 2 * g, row_sums[0], jnp.where(_lane() == 2 * g + 1, row_sums[1], 0.0))

            @pl.when(j == 0)
            def _():
                dq_ref[rows, :] = dq

            @pl.when(j > 0)
            def _():
                dq_ref[rows, :] += dq

            return tuple(new)

        init = tuple((jnp.zeros((hw, blk), F32), jnp.zeros((HEAD_DIM, blk), F32), jnp.zeros((1, blk), F32)) for _ in range(2))
        first = j // sub
        carry = step(first, init, True)
        (dka, dva, dfa), (dkb, dvb, dfb) = lax.fori_loop(first + 1, nbq, lambda i, c: step(i, c, False), carry)
        dk_ref[...] = jnp.concatenate([dka, dkb], axis=0).T * LN2
        dv_ref[...] = jnp.concatenate([dva, dvb], axis=0).T
        if has_bias:
            row = lax.broadcasted_iota(jnp.int32, (LANES, 1), 0)
            by_head = jnp.where(row == 2 * g, dfa, jnp.where(row == 2 * g + 1, dfb, 0.0))
            df_ref[pl.ds(pl.multiple_of(j * blk, blk), blk), :] += by_head.T
        if side:
            @pl.when(step_id == PAIRS * nb - 1)
            def _():
                side.last(si, so, *sems)

    in_specs = [pl.BlockSpec((t, w), lambda g, j: (0, qoff + g)), pl.BlockSpec((nb, w, blk), lambda g, j: (0, qoff + g, 0)),
                pl.BlockSpec((blk, w), lambda g, j: (j, koff + g)), pl.BlockSpec((blk, LANES), lambda g, j: (j, voff + g)),
                pl.BlockSpec((t, LANES), lambda g, j: (0, g)), pl.BlockSpec((nb, LANES, blk), lambda g, j: (0, g, 0)),
                pl.BlockSpec((1, t, LANES), lambda g, j: (g, 0, 0))]
    out_specs = [pl.BlockSpec((t, w), lambda g, j: (0, g)), pl.BlockSpec((blk, w), lambda g, j: (j, g)),
                 pl.BlockSpec((blk, LANES), lambda g, j: (j, g))]
    out_shape = [jax.ShapeDtypeStruct((t, PAIRS * w), F32)] * 2 + [jax.ShapeDtypeStruct((t, PAIRS * LANES), F32)]
    if has_bias:
        in_specs.append(pl.BlockSpec((1, N_HEADS, blk), lambda g, j: (j, 0, 0)))
        out_specs.append(pl.BlockSpec((t, LANES), lambda g, j: (0, 0)))
        out_shape.append(jax.ShapeDtypeStruct((t, LANES), F32))
    s_in_specs, s_out_specs = side.specs() if side else ([], [])
    return pl.pallas_call(
        body, name=name, grid=(PAIRS, nb), in_specs=in_specs + s_in_specs, out_specs=out_specs + s_out_specs,
        out_shape=out_shape + list(s_outs), scratch_shapes=side.sems() if side else [],
        compiler_params=_params(("arbitrary", "arbitrary")),
    )(*ins, *s_ins)


def _adamw_math(w, g, m, v):
    nm = ADAM_B1 * m + (1.0 - ADAM_B1) * g
    nv = ADAM_B2 * v + (1.0 - ADAM_B2) * (g * g)
    m_hat = nm / (1.0 - ADAM_B1 ** ADAM_STEP)
    v_hat = nv / (1.0 - ADAM_B2 ** ADAM_STEP)
    return -ADAM_LR * (m_hat / (jnp.sqrt(v_hat) + ADAM_EPS) + ADAM_WD * w), nm, nv


def adamw(w, g, m, v, *, name):
    rws, cols = w.shape
    br = _row_block(rws)

    def body(w_ref, g_ref, m_ref, v_ref, d_ref, nm_ref, nv_ref):
        d_ref[...], nm_ref[...], nv_ref[...] = _adamw_math(w_ref[...], g_ref[...], m_ref[...], v_ref[...])

    blk = pl.BlockSpec((br, cols), lambda i: (i, 0))
    return pl.pallas_call(
        body, name=name, grid=(rws // br,), in_specs=[blk] * 4, out_specs=[blk] * 3,
        out_shape=[jax.ShapeDtypeStruct((rws, cols), F32)] * 3,
        compiler_params=_params(("parallel",)),
    )(w, g, m, v)


def adamw_halves(w, g_mine, g_other, m, v, core, *, name):
    _, k, n = w.shape
    br = _row_block(k // 2)
    nh = k // 2 // br

    def body(c_ref, w_ref, gm_ref, go_ref, m_ref, v_ref, g_out, d_ref, nm_ref, nv_ref):
        gv = jnp.where(pl.program_id(0) == c_ref[0], gm_ref[...], go_ref[...])
        g_out[0] = gv
        d_ref[0], nm_ref[0], nv_ref[0] = _adamw_math(w_ref[0], gv, m_ref[0], v_ref[0])

    full = pl.BlockSpec((1, br, n), lambda hb, i, c: (0, hb * nh + i, 0))
    half = pl.BlockSpec((br, n), lambda hb, i, c: (i, 0))
    return pl.pallas_call(
        body, name=name,
        grid_spec=pltpu.PrefetchScalarGridSpec(num_scalar_prefetch=1, grid=(2, nh), in_specs=[full, half, half, full, full],
                                               out_specs=[full] * 4),
        out_shape=[jax.ShapeDtypeStruct(w.shape, F32)] * 4,
        compiler_params=_params(("parallel", "parallel")),
    )(core, w, g_mine, g_other, m, v)


def adamw_halves_t(wt, gt_mine, gt_other, mt, vt, core, *, name, bc=128):
    n, k = wt.shape
    nh = k // 2 // bc

    def body(c_ref, w_ref, gm_ref, go_ref, m_ref, v_ref, g_out, d_ref, nm_ref, nv_ref):
        gv = jnp.where(pl.program_id(0) == c_ref[0], gm_ref[...], go_ref[...])
        g_out[...] = gv
        d_ref[...], nm_ref[...], nv_ref[...] = _adamw_math(w_ref[...], gv, m_ref[...], v_ref[...])

    full = pl.BlockSpec((n, bc), lambda hb, i, c: (0, hb * nh + i))
    half = pl.BlockSpec((n, bc), lambda hb, i, c: (0, i))
    return pl.pallas_call(
        body, name=name,
        grid_spec=pltpu.PrefetchScalarGridSpec(num_scalar_prefetch=1, grid=(2, nh), in_specs=[full, half, half, full, full],
                                               out_specs=[full] * 4),
        out_shape=[jax.ShapeDtypeStruct(wt.shape, F32)] * 4,
        compiler_params=_params(("parallel", "parallel")),
    )(core, wt, gt_mine, gt_other, mt, vt)


def add_pair(dw, recv, core, *, name):
    n4, k, n = dw.shape
    hk = k // 2

    def body(c_ref, a_ref, b_ref, o_ref):
        o_ref[...] = (a_ref[...] + b_ref[...].astype(F32)).astype(BF16)

    return pl.pallas_call(
        body, name=name,
        grid_spec=pltpu.PrefetchScalarGridSpec(
            num_scalar_prefetch=1, grid=(n4,),
            in_specs=[pl.BlockSpec((1, hk, n), lambda q, c: (q, c[0], 0)), pl.BlockSpec((1, hk, n), lambda q, c: (q, 0, 0))],
            out_specs=pl.BlockSpec((1, hk, n), lambda q, c: (q, 0, 0))),
        out_shape=jax.ShapeDtypeStruct((n4, hk, n), BF16),
        compiler_params=_params(("parallel",)),
    )(core, dw, recv)


def sum_chips(parts, *, name):
    n4, r, n = parts.shape
    br = _row_block(r)

    def body(p_ref, o_ref):
        acc = p_ref[0].astype(F32)
        for q in range(1, n4):
            acc = acc + p_ref[q].astype(F32)
        o_ref[...] = acc

    return pl.pallas_call(
        body, name=name, grid=(r // br,),
        in_specs=[pl.BlockSpec((n4, br, n), lambda i: (0, i, 0))], out_specs=pl.BlockSpec((br, n), lambda i: (i, 0)),
        out_shape=jax.ShapeDtypeStruct((r, n), F32),
        compiler_params=_params(("parallel",)),
    )(parts)


ANY = pl.BlockSpec(memory_space=pl.ANY)


def _place():
    x, y, c = lax.axis_index("x"), lax.axis_index("y"), lax.axis_index("c")
    chips = [(1 - x, y), (x, 1 - y), (1 - x, 1 - y)]
    return x, y, c, chips


def _copy(src, dst, send_sems, recv_sems, k, to):
    return pltpu.make_async_remote_copy(src_ref=src, dst_ref=dst, send_sem=send_sems.at[k], recv_sem=recv_sems.at[k],
                                        device_id=to, device_id_type=MESH)


def _half_rows(ref, lead, hf):
    hk = ref.shape[1] // 2
    return ref.at[lead, pl.ds(hf * hk, hk), :]


def _gather_first(srcs, dsts, ssems, rsems):
    x, y, c, chips = _place()
    for ti, (s, d) in enumerate(zip(srcs, dsts)):
        hk = s.shape[0] // 2
        for j, (cx, cy) in enumerate(chips):
            _copy(s.at[pl.ds(c * hk, hk), :], _half_rows(d, 2 * x + y, c), ssems, rsems, 3 * ti + j, (cx, cy, c)).start()


def _gather_mid(srcs, dsts, ssems, rsems):
    x, y, c, chips = _place()
    n1 = 3 * len(srcs)
    for ti, d in enumerate(dsts):
        for j, (cx, cy) in enumerate(chips):
            landed = _half_rows(d, 2 * cx + cy, c)
            _copy(landed, landed, ssems, rsems, 3 * ti + j, (cx, cy, c)).wait_recv()
            _copy(landed, landed, ssems, rsems, n1 + 3 * ti + j, (x, y, 1 - c)).start()


def _gather_last(srcs, dsts, ssems, rsems):
    x, y, c, chips = _place()
    n1 = 3 * len(srcs)
    for ti, (s, d) in enumerate(zip(srcs, dsts)):
        hk = s.shape[0] // 2
        for j, (cx, cy) in enumerate(chips):
            other = _half_rows(d, 2 * cx + cy, 1 - c)
            _copy(other, other, ssems, rsems, n1 + 3 * ti + j, (x, y, 1 - c)).wait_recv()
        for j, (cx, cy) in enumerate(chips):
            mine = s.at[pl.ds(c * hk, hk), :]
            _copy(mine, mine, ssems, rsems, 3 * ti + j, (cx, cy, c)).wait_send()
            _copy(mine, mine, ssems, rsems, n1 + 3 * ti + j, (x, y, 1 - c)).wait_send()


def gather_side(shards):
    return Side(shards, [jax.ShapeDtypeStruct((N_CHIPS,) + s.shape, s.dtype) for s in shards], 6 * len(shards),
                _gather_first, _gather_last, _gather_mid)


def _scatter_first(srcs, dsts, ssems, rsems):
    x, y, c, chips = _place()
    for ti, (s, d) in enumerate(zip(srcs, dsts)):
        for j, (cx, cy) in enumerate(chips):
            _copy(s.at[2 * cx + cy], d.at[2 * x + y], ssems, rsems, 3 * ti + j, (cx, cy, c)).start()


def _scatter_last(srcs, dsts, ssems, rsems):
    x, y, c, chips = _place()
    for ti, (s, d) in enumerate(zip(srcs, dsts)):
        for j, (cx, cy) in enumerate(chips):
            _copy(s.at[2 * cx + cy], d.at[2 * cx + cy], ssems, rsems, 3 * ti + j, (cx, cy, c)).wait_recv()
        for j, (cx, cy) in enumerate(chips):
            _copy(s.at[2 * cx + cy], d.at[2 * cx + cy], ssems, rsems, 3 * ti + j, (cx, cy, c)).wait_send()


def scatter_side(parts):
    return Side(parts, [jax.ShapeDtypeStruct(p.shape, p.dtype) for p in parts], 3 * len(parts), _scatter_first, _scatter_last)


def run_side(side, *, name):
    n_in, n_out = len(side.ins), len(side.out_shapes)

    def body(*refs):
        si, so, sems = _split_refs(refs, [n_in, n_out, 2])
        side.first(si, so, *sems)
        if side.mid is not None:
            side.mid(si, so, *sems)
        side.last(si, so, *sems)

    in_specs, out_specs = side.specs()
    return pl.pallas_call(body, name=name, in_specs=in_specs, out_specs=out_specs, out_shape=side.out_shapes,
                          scratch_shapes=side.sems())(*side.ins)


def _swap_first(srcs, dsts, ssems, rsems):
    x, y, c, _ = _place()
    for k, (s, d) in enumerate(zip(srcs, dsts)):
        _copy(s, d, ssems, rsems, k, (x, y, 1 - c)).start()


def _swap_last(srcs, dsts, ssems, rsems):
    x, y, c, _ = _place()
    for k, (s, d) in enumerate(zip(srcs, dsts)):
        _copy(s, d, ssems, rsems, k, (x, y, 1 - c)).wait()


def swap_side(xs):
    return Side(xs, [jax.ShapeDtypeStruct(a.shape, a.dtype) for a in xs], len(xs), _swap_first, _swap_last)


def allreduce_small(s):
    n_dev = 8

    def body(s_ref, out_ref, buf, send_sems, recv_sems):
        x, y, c, _ = _place()
        me = 4 * x + 2 * y + c
        buf[me] = s_ref[...]
        sends = []
        for k in range(1, n_dev):
            px = 1 - x if k & 4 else x
            py = 1 - y if k & 2 else y
            pc = 1 - c if k & 1 else c
            cp = _copy(s_ref, buf.at[me], send_sems, recv_sems, k - 1, (px, py, pc))
            cp.start()
            sends.append((cp, 4 * px + 2 * py + pc))
        for k, (cp, peer) in enumerate(sends):
            _copy(s_ref, buf.at[peer], send_sems, recv_sems, k, (x, y, c)).wait_recv()
        for cp, _ in sends:
            cp.wait_send()
        acc = buf[0]
        for d in range(1, n_dev):
            acc = acc + buf[d]
        out_ref[...] = acc

    vm = pl.BlockSpec(memory_space=pltpu.VMEM)
    return pl.pallas_call(
        body, name="allreduce_small", in_specs=[vm], out_specs=vm,
        out_shape=jax.ShapeDtypeStruct(s.shape, F32),
        scratch_shapes=[pltpu.VMEM((n_dev,) + s.shape, F32), pltpu.SemaphoreType.DMA((n_dev - 1,)),
                        pltpu.SemaphoreType.DMA((n_dev - 1,))],
    )(s)


def join_cols(sm):
    n4, k, n = sm.shape
    return sm.transpose(1, 0, 2).reshape(k, n4 * n)


def split_cols(full):
    k, n = full.shape
    return full.reshape(k, N_CHIPS, n // N_CHIPS).transpose(1, 0, 2)


def _pad_heads(w, width):
    lead = w.shape[:-1]
    w = w.reshape(lead + (N_HEADS, width))
    return jnp.pad(w, [(0, 0)] * len(lead) + [(0, 0), (0, LANES - width)]).reshape(lead + (HW,))


def _unpad_heads(w, width):
    lead = w.shape[:-1]
    return w.reshape(lead + (N_HEADS, LANES))[..., :width].reshape(lead + (N_HEADS * width,))


def split_w_in(w_in):
    d = w_in.shape[0]
    o_f = 3 * FW
    o_cq = o_f + N_HEADS
    o_ckv = o_cq + Q_RANK
    o_kr = o_ckv + KV_RANK

    def z(n):
        return jnp.zeros((d, n), w_in.dtype)

    small = jnp.concatenate([w_in[:, o_cq:o_ckv], w_in[:, o_ckv:o_kr], z(HEAD_DIM), w_in[:, o_kr:], z(LANES - HEAD_DIM - ROPE_DIM),
                             w_in[:, o_f:o_cq], z(LANES - N_HEADS), z(LANES)], axis=1)
    return w_in[:, :o_f], small


def join_w_in(d_qkv, d_small):
    kr = S_KR + HEAD_DIM
    return jnp.concatenate([d_qkv, d_small[:, S_F:S_F + N_HEADS], d_small[:, S_CQ:S_CKV], d_small[:, S_CKV:S_KR],
                            d_small[:, kr:kr + ROPE_DIM]], axis=1)


def rope_tables(pos):
    t = pos.shape[0]
    inv_freq = ROPE_THETA ** (-jnp.arange(0, ROPE_DIM, 2, dtype=F32) / ROPE_DIM)
    ang = pos.astype(F32)[:, None] * inv_freq
    cos, sin = jnp.cos(ang), jnp.sin(ang)
    half = ROPE_DIM // 2

    def z(n):
        return jnp.zeros((t, n), F32)

    tab_c = jnp.concatenate([jnp.ones((t, HEAD_DIM), F32), cos, cos, z(LANES - HEAD_DIM - ROPE_DIM)], axis=1)
    tab_a = jnp.concatenate([z(HEAD_DIM), -sin, z(half), z(LANES - HEAD_DIM - ROPE_DIM)], axis=1)
    tab_b = jnp.concatenate([z(HEAD_DIM), z(half), sin, z(LANES - HEAD_DIM - ROPE_DIM)], axis=1)
    return tab_c, tab_a, tab_b


def _pad_lanes(v, n):
    return jnp.pad(v, ((0, 0), (0, n - v.shape[1])))


ATTN_BLK = 512
ATTN_FWD_BLK = 1024
ATTN_BWD_QBLK = 1024


def local_step(xs, pos, tgt, gains, w_early, late_weights, fwd_side=None, reduction=None):
    g_attn, b_forget, g_q, g_kv, g_fo, g_mo, g_mlp, g_fin = gains
    w_in, w_uq, w_ukv = w_early
    t = xs.shape[0]
    blk = min(ATTN_BLK, t)
    fox_scale = 1.0 / (HEAD_DIM ** 0.5)
    mla_scale = 1.0 / ((HEAD_DIM + ROPE_DIM) ** 0.5)

    w_qkv, w_small = split_w_in(w_in)
    w_uq_p = _pad_heads(w_uq, HEAD_DIM + ROPE_DIM)
    kv = w_ukv.reshape(KV_RANK, N_HEADS, 2 * HEAD_DIM)
    w_ukv_p = jnp.concatenate([_pad_heads(kv[:, :, :HEAD_DIM].reshape(KV_RANK, FW), HEAD_DIM),
                               kv[:, :, HEAD_DIM:].reshape(KV_RANK, FW)], axis=1)
    b_f = _pad_lanes(b_forget, LANES)
    tab_c, tab_a, tab_b = rope_tables(pos)

    h1 = rmsnorm(xs, g_attn, out_dtype=BF16, name="norm_attn")
    qkv, = mm(h1, w_qkv, out_dtypes=[BF16], name="proj_qkv")
    small, = mm(h1, w_small, out_dtypes=[F32], name="proj_small")
    mq, mk, mv, lf, cqn, ckvn = mla_prep(small, g_q, g_kv, w_uq_p, w_ukv_p, tab_c, tab_a, tab_b, b_f, name="mla_prep")
    f_cum = cumsum_rows(lf, reverse=False, name="gate_cumsum")
    f_blocks = f_cum[:, :N_HEADS].reshape(t // blk, blk, N_HEADS).transpose(0, 2, 1)
    qkv_t, mq_t, mv_t = to_blocks_t(qkv, blk), to_blocks_t(mq, blk), to_blocks_t(mv, blk)
    fo, st_f, *gathered = flash_fwd(qkv_t, qkv, qkv_t, f_cum, qoff=0, koff=PAIRS, voff=2 * PAIRS, pair=True,
                                    scale=fox_scale, name="fox_fwd", blk=ATTN_FWD_BLK, side=fwd_side)
    w_o, w_up, w_down = late_weights(gathered)
    mo, st_m = flash_fwd(mq_t, mk, mv_t, None, qoff=0, koff=0, voff=0, pair=False, scale=mla_scale, name="mla_fwd",
                         blk=ATTN_FWD_BLK)
    mixed = mix_norm(fo, mo, g_fo, g_mo, name="norm_mix")

    def inv_rms(v):
        return lax.rsqrt(jnp.mean(v * v, axis=-1, keepdims=True) + EPS)

    def residual_then_norm(acc, res, g):
        xn = acc + res
        return xn, xn * inv_rms(xn) * g

    def norm_bwd(dh, xn, res, g):
        r = inv_rms(xn)
        uu = dh * g
        return (r * uu - xn * (r * r * r * jnp.mean(uu * xn, axis=-1, keepdims=True)) + res,
                jnp.sum(dh * (xn * r), axis=0, keepdims=True))

    def norm_bwd2(dh, xn, res, g):
        dx, dg = norm_bwd(dh, xn, res, g)
        return dx, dx, dg

    def residual_then_loss(acc, res, target, g):
        xn = acc + res
        r = inv_rms(xn)
        xh = xn * r
        e = xh * g - target
        part = 0.5 * jnp.sum(jnp.mean(e * e, axis=-1, keepdims=True), axis=0, keepdims=True)
        dy = e * (1.0 / xn.shape[1])
        uu = dy * g
        dx = r * uu - xn * (r * r * r * jnp.mean(uu * xn, axis=-1, keepdims=True))
        return dx, dx, jnp.sum(dy * xh, axis=0, keepdims=True), part + jnp.zeros_like(g)

    x1, h2 = mm(mixed, w_o, extras=[xs], vecs=[g_mlp], epilogue=residual_then_norm, out_dtypes=[F32, BF16], name="out_proj")

    def relu2(acc):
        r = jnp.maximum(acc, 0.0)
        return acc, r * r

    u, act = mm(h2, w_up, epilogue=relu2, out_dtypes=[BF16, BF16], name="mlp_up")
    dx2, dx2b, dg_fin, loss_row = mm(act, w_down, extras=[x1, tgt], vecs=[g_fin], epilogue=residual_then_loss,
                                     out_dtypes=[F32, BF16], n_sums=2, name="mlp_down_loss")
    loss = loss_row[:, :1]

    def relu2_grad(acc, uu):
        return (acc * (2.0 * jnp.maximum(uu.astype(F32), 0.0)),)

    du, = mm(dx2b, w_down, trans_b=True, extras=[u], epilogue=relu2_grad, out_dtypes=[BF16], name="mlp_down_bwd")
    dw_down = mm_tn(act, dx2b, name="dw_down").reshape(N_CHIPS, -1, w_down.shape[1])
    dx1, dx1b, dg_mlp = mm(du, w_up, trans_b=True, extras=[x1, dx2], vecs=[g_mlp], epilogue=norm_bwd2,
                           out_dtypes=[F32, BF16], n_sums=1, name="mlp_up_bwd")
    dw_up = mm_tn(h2, du, name="dw_up", col_shards=N_CHIPS)

    dmixed, = mm(dx1b, w_o, trans_b=True, out_dtypes=[F32], name="out_proj_bwd")
    dw_o = mm_tn(mixed, dx1b, name="dw_o").reshape(N_CHIPS, -1, w_o.shape[1])
    late = (dw_o, dw_up, dw_down)
    red = reduction
    dfo, dmo, dg_fo, dg_mo, st_f, st_m, *got = mix_norm_bwd(dmixed, fo, mo, g_fo, g_mo, st_f, st_m, name="norm_mix_bwd",
                                                            side=red.late_swap(late) if red else None)
    dfq, dfk, dfv, d_f, *got = flash_bwd(
        qkv, qkv_t, qkv, qkv, dfo, to_blocks_t(dfo, blk), st_f, f_blocks, qoff=0, koff=PAIRS, voff=2 * PAIRS, pair=True,
        scale=fox_scale, name="fox_bwd", qblk=ATTN_BWD_QBLK, side=red.late_scatter(got) if red else None)
    dmq, dmk, dmv, *got = flash_bwd(mq, mq_t, mk, mv, dmo, to_blocks_t(dmo, blk), st_m, None, qoff=0, koff=0, voff=0,
                                    pair=False, scale=mla_scale, name="mla_bwd", qblk=ATTN_BWD_QBLK,
                                    side=red.late_halves(got) if red else None)
    if red:
        red.late_done(got)
    dqkv = jnp.concatenate([dfq, dfk, dfv], axis=1).astype(BF16)
    dlf = cumsum_rows(d_f, reverse=True, name="gate_cumsum_bwd")
    dsmall, dq_u, dkv_u, dg_q, dg_kv, db_f = mla_prep_bwd(dmq, dmk, dmv, dlf, small, g_q, g_kv, w_uq_p, w_ukv_p,
                                                          tab_c, tab_a, tab_b, b_f, name="mla_prep_bwd")
    dw_uq_p = mm_tn(cqn, dq_u, name="dw_uq")
    dw_ukv_p = mm_tn(ckvn, dkv_u, name="dw_ukv")

    dw_qkv = mm_tn(h1, dqkv, name="dw_qkv")
    dw_small = mm_tn(h1, dsmall, name="dw_small")
    dw_in = join_w_in(dw_qkv, dw_small)
    dw_uq = _unpad_heads(dw_uq_p, HEAD_DIM + ROPE_DIM)
    dk_cols = _unpad_heads(dw_ukv_p[:, :HW], HEAD_DIM).reshape(KV_RANK, N_HEADS, HEAD_DIM)
    dv_cols = dw_ukv_p[:, HW:].reshape(KV_RANK, N_HEADS, HEAD_DIM)
    dw_ukv = jnp.concatenate([dk_cols, dv_cols], axis=2).reshape(KV_RANK, N_HEADS * 2 * HEAD_DIM)
    early = tuple(split_cols(d) for d in (dw_in, dw_uq, dw_ukv))
    grad_x, dg_attn, *got = mm([dqkv, dsmall], [w_qkv, w_small], trans_b=True, extras=[xs, dx1], vecs=[g_attn],
                               epilogue=norm_bwd, out_dtypes=[F32], n_sums=1, name="proj_bwd",
                               side=red.early_scatter(early) if red else None)
    if red:
        red.early_done(got)
    d_gains = (dg_attn, db_f[:, :N_HEADS], dg_q, dg_kv, dg_fo, dg_mo, dg_mlp, dg_fin)
    return loss, grad_x, early, late, d_gains


class GradReduction:
    def __init__(self, core_id, chip):
        self.core_id, self.chip = core_id, chip
        self.core = core_id.reshape(1).astype(jnp.int32)
        self.grads, self.pairs, self.halves, self.others = {}, {}, {}, {}

    def _other_halves(self, grads):
        return [lax.dynamic_slice_in_dim(g, (1 - self.core_id) * (g.shape[1] // 2), g.shape[1] // 2, axis=1).astype(BF16)
                for g in grads]

    def _add_pairs(self, group, recvs):
        self.pairs[group] = [add_pair(g, r, self.core, name="add_pair_%s_%d" % (group, n))
                             for n, (g, r) in enumerate(zip(self.grads[group], recvs))]
        return scatter_side(self.pairs[group])

    def _chip_sums(self, group, scattered):
        chip = self.chip
        with_mine = [lax.dynamic_update_index_in_dim(s, lax.dynamic_index_in_dim(p, chip, 0, keepdims=True), chip, 0)
                     for s, p in zip(scattered, self.pairs[group])]
        self.halves[group] = [sum_chips(s, name="sum_chips_%s_%d" % (group, n)) for n, s in enumerate(with_mine)]
        return self.halves[group]

    def late_swap(self, grads):
        self.grads["late"] = list(grads)
        return swap_side(self._other_halves(grads))

    def late_scatter(self, recvs):
        return self._add_pairs("late", recvs)

    def late_halves(self, scattered):
        return swap_side(self._chip_sums("late", scattered))

    def late_done(self, others):
        self.others["late"] = list(others)

    def early_scatter(self, grads):
        self.grads["early"] = list(grads)
        return self._add_pairs("early", run_side(swap_side(self._other_halves(grads)), name="swap_early_sends"))

    def early_done(self, scattered):
        self.others["early"] = list(run_side(swap_side(self._chip_sums("early", scattered)), name="swap_early_halves"))

def kernel(x, positions, attn_norm_g, w_in, b_forget, q_norm_g, w_uq, kv_norm_g, w_ukv, fox_out_g, mla_out_g, w_o, mlp_norm_g, w_up, w_down, final_norm_g, loss_target, m_attn_norm_g, m_w_in, m_b_forget, m_q_norm_g, m_w_uq, m_kv_norm_g, m_w_ukv, m_fox_out_g, m_mla_out_g, m_w_o, m_mlp_norm_g, m_w_up, m_w_down, m_final_norm_g, v_attn_norm_g, v_w_in, v_b_forget, v_q_norm_g, v_w_uq, v_kv_norm_g, v_w_ukv, v_fox_out_g, v_mla_out_g, v_w_o, v_mlp_norm_g, v_w_up, v_w_down, v_final_norm_g):
    core_id = lax.axis_index("c")
    core = core_id.reshape(1).astype(jnp.int32)
    chip = 2 * lax.axis_index("x") + lax.axis_index("y")
    big = [w_in, w_uq, w_ukv, w_o, w_up, w_down]
    big_m = [m_w_in, m_w_uq, m_w_ukv, m_w_o, m_w_up, m_w_down]
    big_v = [v_w_in, v_w_uq, v_w_ukv, v_w_o, v_w_up, v_w_down]
    n_early = 3

    def vec(a):
        return a.reshape(1, -1)

    small = [attn_norm_g, b_forget, q_norm_g, kv_norm_g, fox_out_g, mla_out_g, mlp_norm_g, final_norm_g]
    small_m = [m_attn_norm_g, m_b_forget, m_q_norm_g, m_kv_norm_g, m_fox_out_g, m_mla_out_g, m_mlp_norm_g, m_final_norm_g]
    small_v = [v_attn_norm_g, v_b_forget, v_q_norm_g, v_kv_norm_g, v_fox_out_g, v_mla_out_g, v_mlp_norm_g, v_final_norm_g]
    gains = [vec(a) for a in small]

    shards = [w[0].astype(BF16) for w in big]

    def with_own(gathered, mine):
        return [lax.dynamic_update_index_in_dim(g, s, chip, 0) for g, s in zip(gathered, mine)]

    early = with_own(run_side(gather_side(shards[:n_early]), name="gather_early"), shards[:n_early])
    w_early = [join_cols(g) for g in early]

    def late_weights(gathered):
        g_o, g_up, g_down = with_own(gathered, shards[n_early:])
        return g_o.reshape(-1, g_o.shape[2]), join_cols(g_up), g_down.reshape(-1, g_down.shape[2])

    reduction = GradReduction(core_id, chip)
    loss, grad_x, _, _, d_small = local_step(
        x[0], positions[0], loss_target[0], gains, w_early, late_weights, gather_side(shards[n_early:]), reduction)
    halves = reduction.halves["early"] + reduction.halves["late"]
    others = reduction.others["early"] + reduction.others["late"]

    def rows8(vs):
        return jnp.concatenate([_pad_lanes(vec(a).astype(F32), 1024) for a in vs], axis=0)

    with_loss = [jnp.concatenate([d, loss], axis=1) if n == 1 else d for n, d in enumerate(d_small)]
    g_small8 = allreduce_small(rows8(with_loss))

    outs_big = []
    for n, (w, gm, go, m, v) in enumerate(zip(big, halves, others, big_m, big_v)):
        _, k, cols = w.shape
        if cols % LANES and (k // 2) % LANES == 0:
            outs = adamw_halves_t(w[0].T, gm.T, go.T, m[0].T, v[0].T, core, name="adamw_%d" % n)
            outs_big.append([o.T[None] for o in outs])
        else:
            outs_big.append(adamw_halves(w, gm, go, m, v, core, name="adamw_%d" % n))
    d8, m8, v8 = adamw(rows8(small), g_small8, rows8(small_m), rows8(small_v), name="adamw_small")

    def unrows8(a8):
        return [a8[n, :s.size].reshape(s.shape) for n, s in enumerate(small)]

    loss_all = g_small8[1, N_HEADS]
    grads, deltas, new_m, new_v = [None] * 14, [None] * 14, [None] * 14, [None] * 14
    big_at = [1, 4, 6, 9, 11, 12]
    small_at = [0, 2, 3, 5, 7, 8, 10, 13]
    for n, at in enumerate(big_at):
        grads[at], deltas[at], new_m[at], new_v[at] = outs_big[n]
    for at, g, dd, mm_, vv in zip(small_at, unrows8(g_small8), unrows8(d8), unrows8(m8), unrows8(v8)):
        grads[at], deltas[at], new_m[at], new_v[at] = g, dd, mm_, vv
    return (loss_all, grad_x[None], *grads, *deltas, *new_m, *new_v)
```

```python
import jax
import jax.numpy as jnp
from jax import lax
from jax.experimental import pallas as pl
from jax.experimental.pallas import tpu as pltpu

F32 = jnp.float32
BF16 = jnp.bfloat16
MESH = pl.DeviceIdType.MESH

EPS = 1e-6
ROPE_THETA = 10000.0
N_HEADS = 8
PAIRS = N_HEADS // 2
HEAD_DIM = 64
ROPE_DIM = 32
LANES = 128
Q_RANK = 384
KV_RANK = 256
N_CHIPS = 4
ADAM_LR, ADAM_B1, ADAM_B2, ADAM_EPS, ADAM_WD, ADAM_STEP = 0.001, 0.9, 0.999, 1e-08, 0.01, 10
VMEM_LIMIT = 48 * 1024 * 1024
LOG2E = 1.4426950408889634
LN2 = 0.6931471805599453
NN = (((1,), (0,)), ((), ()))
NT = (((1,), (1,)), ((), ()))
TN = (((0,), (0,)), ((), ()))


def _params(sem=None):
    return pltpu.CompilerParams(dimension_semantics=sem, vmem_limit_bytes=VMEM_LIMIT)


def _fit(block, dim):
    if dim <= block:
        return dim
    return next(b for b in range(block - block % LANES, 0, -LANES) if dim % b == 0)


def _row_block(rows):
    return next(b for b in (256, 128, 64, 32, 16, 8) if rows % b == 0)


def gridded(body, *, name, grid, in_specs, out_specs, out_shape, ins, semantics, side=None):
    if side is None:
        return pl.pallas_call(body, name=name, grid=grid, in_specs=in_specs, out_specs=out_specs, out_shape=out_shape,
                              compiler_params=_params(semantics))(*ins)
    n_in, n_out = len(in_specs), len(out_specs)
    steps = 1
    for extent in grid:
        steps *= extent

    def riding(*refs):
        main_in, s_in, main_out, s_out, sems = _split_refs(refs, [n_in, len(side.ins), n_out, len(side.out_shapes), 2])
        step = 0
        for axis, extent in enumerate(grid):
            step = step * extent + pl.program_id(axis)

        @pl.when(step == 0)
        def _():
            side.first(s_in, s_out, *sems)

        body(*main_in, *main_out)

        @pl.when(step == steps - 1)
        def _():
            side.last(s_in, s_out, *sems)

    s_in_specs, s_out_specs = side.specs()
    return pl.pallas_call(
        riding, name=name, grid=grid, in_specs=list(in_specs) + s_in_specs, out_specs=list(out_specs) + s_out_specs,
        out_shape=list(out_shape) + side.out_shapes, scratch_shapes=side.sems(),
        compiler_params=_params(("arbitrary",) * len(grid)))(*ins, *side.ins)


def rmsnorm(x, g, *, out_dtype, name, bt=512):
    t, d = x.shape
    bt = min(bt, t)

    def body(x_ref, g_ref, o_ref):
        xv = x_ref[...].astype(F32)
        r = lax.rsqrt(jnp.mean(xv * xv, axis=-1, keepdims=True) + EPS)
        o_ref[...] = (xv * r * g_ref[...]).astype(o_ref.dtype)

    return pl.pallas_call(
        body, name=name, grid=(t // bt,),
        in_specs=[pl.BlockSpec((bt, d), lambda i: (i, 0)), pl.BlockSpec((1, d), lambda i: (0, 0))],
        out_specs=pl.BlockSpec((bt, d), lambda i: (i, 0)),
        out_shape=jax.ShapeDtypeStruct((t, d), out_dtype),
        compiler_params=_params(("parallel",)),
    )(x, g)


def mm(a, b, *, trans_b=False, extras=(), vecs=(), epilogue=None, out_dtypes, n_sums=0, name, bm=1024, bn=1024, side=None):
    a_list = list(a) if isinstance(a, (list, tuple)) else [a]
    b_list = list(b) if isinstance(b, (list, tuple)) else [b]
    m = a_list[0].shape[0]
    n = b_list[0].shape[0] if trans_b else b_list[0].shape[1]
    ks = [x.shape[1] for x in a_list]
    if sum(ks) > 2048:
        bm = bm // 2
    bm, bn = _fit(bm, m), _fit(bn, n)
    assert n_sums == 0 or bn == n
    n_ab, n_ex, n_vec, n_out = len(a_list), len(extras), len(vecs), len(out_dtypes)

    def body(*refs):
        a_refs, b_refs, ex, vs, outs, sums = _split_refs(refs, [n_ab, n_ab, n_ex, n_vec, n_out, n_sums])
        acc = None
        for a_ref, b_ref in zip(a_refs, b_refs):
            part = lax.dot_general(a_ref[...], b_ref[...], NT if trans_b else NN, preferred_element_type=F32)
            acc = part if acc is None else acc + part
        res = epilogue(acc, *[e[...] for e in ex], *[v[...] for v in vs]) if epilogue is not None else (acc,)
        for o, r in zip(outs, res[:n_out]):
            o[...] = r.astype(o.dtype)
        if n_sums:
            @pl.when(pl.program_id(0) == 0)
            def _():
                for s_ref in sums:
                    s_ref[...] = jnp.zeros_like(s_ref)

            for s_ref, r in zip(sums, res[n_out:]):
                s_ref[...] += r

    tile = pl.BlockSpec((bm, bn), lambda i, j: (i, j))
    vec = pl.BlockSpec((1, bn), lambda i, j: (0, j))
    a_specs = [pl.BlockSpec((bm, k), lambda i, j: (i, 0)) for k in ks]
    b_specs = [pl.BlockSpec((bn, k), lambda i, j: (j, 0)) if trans_b else pl.BlockSpec((k, bn), lambda i, j: (0, j)) for k in ks]
    return gridded(
        body, name=name, grid=(m // bm, n // bn),
        in_specs=a_specs + b_specs + [tile] * n_ex + [vec] * n_vec,
        out_specs=[tile] * n_out + [vec] * n_sums,
        out_shape=[jax.ShapeDtypeStruct((m, n), dt) for dt in out_dtypes] + [jax.ShapeDtypeStruct((1, n), F32)] * n_sums,
        ins=[*a_list, *b_list, *extras, *vecs],
        semantics=("arbitrary", "arbitrary") if n_sums else ("parallel", "parallel"), side=side)


def mm_tn(a, b, *, name, col_shards=1, bk=1024, bn=1024, bt=1024):
    t, k = a.shape
    n = b.shape[1]
    ns = n // col_shards
    bk, bn, bt = _fit(bk, k), _fit(bn, ns), _fit(bt, t)
    per = ns // bn

    def body(a_ref, b_ref, o_ref):
        @pl.when(pl.program_id(2) == 0)
        def _():
            o_ref[...] = jnp.zeros_like(o_ref)

        o_ref[...] += lax.dot_general(a_ref[...], b_ref[...], TN, preferred_element_type=F32)

    if col_shards == 1:
        out_spec = pl.BlockSpec((bk, bn), lambda i, j, s: (i, j))
        out_shape = jax.ShapeDtypeStruct((k, n), F32)
    else:
        out_spec = pl.BlockSpec((None, bk, bn), lambda i, j, s: (j // per, i, j % per))
        out_shape = jax.ShapeDtypeStruct((col_shards, k, ns), F32)
    return pl.pallas_call(
        body, name=name, grid=(k // bk, n // bn, t // bt),
        in_specs=[pl.BlockSpec((bt, bk), lambda i, j, s: (s, i)), pl.BlockSpec((bt, bn), lambda i, j, s: (s, j))],
        out_specs=out_spec, out_shape=out_shape,
        compiler_params=_params(("parallel", "parallel", "arbitrary")),
    )(a, b)


def _split3(x):
    hi = x.astype(BF16)
    r1 = x - hi.astype(F32)
    mid = r1.astype(BF16)
    lo = (r1 - mid.astype(F32)).astype(BF16)
    return hi, mid, lo


def cumsum_rows(x, *, reverse, name, bc=512):
    t, d = x.shape
    bc = min(bc, t)
    nb = t // bc

    def body(x_ref, o_ref, carry):
        @pl.when(pl.program_id(0) == 0)
        def _():
            carry[...] = jnp.zeros_like(carry)

        r = lax.broadcasted_iota(jnp.int32, (bc, bc), 0)
        c = lax.broadcasted_iota(jnp.int32, (bc, bc), 1)
        tri = jnp.where((r <= c) if reverse else (r >= c), 1.0, 0.0).astype(BF16)
        hi, mid, lo = _split3(x_ref[...])
        s = (lax.dot_general(tri, hi, NN, preferred_element_type=F32)
             + lax.dot_general(tri, mid, NN, preferred_element_type=F32)
             + lax.dot_general(tri, lo, NN, preferred_element_type=F32)) + carry[0:1, :]
        o_ref[...] = s
        carry[0:1, :] = s[0:1, :] if reverse else s[bc - 1:bc, :]

    imap = (lambda i: (nb - 1 - i, 0)) if reverse else (lambda i: (i, 0))
    return pl.pallas_call(
        body, name=name, grid=(nb,),
        in_specs=[pl.BlockSpec((bc, d), imap)], out_specs=pl.BlockSpec((bc, d), imap),
        out_shape=jax.ShapeDtypeStruct((t, d), F32),
        scratch_shapes=[pltpu.VMEM((8, d), F32)],
        compiler_params=_params(("arbitrary",)),
    )(x)


def _rope(x, c, a, b):
    return x * c + pltpu.roll(x, LANES - ROPE_DIM // 2, 1) * a + pltpu.roll(x, ROPE_DIM // 2, 1) * b


def _rope_bwd(d, c, a, b):
    return d * c + pltpu.roll(d * a, ROPE_DIM // 2, 1) + pltpu.roll(d * b, LANES - ROPE_DIM // 2, 1)


S_CQ, S_CKV, S_KR, S_F, S_END = 0, Q_RANK, Q_RANK + KV_RANK, Q_RANK + KV_RANK + LANES, 1024
HW = N_HEADS * LANES
FW = N_HEADS * HEAD_DIM


def mla_prep(small, g_q, g_kv, w_uq, w_ukv, tab_c, tab_a, tab_b, b_f, *, name, bt=512):
    t = small.shape[0]
    bt = min(bt, t)

    def body(s_ref, gq_ref, gkv_ref, wq_ref, wkv_ref, c_ref, a_ref, b_ref, bf_ref,
             mq_ref, mk_ref, mv_ref, lf_ref, cqn_ref, ckvn_ref):
        cq = s_ref[:, S_CQ:S_CKV]
        rq = lax.rsqrt(jnp.mean(cq * cq, axis=-1, keepdims=True) + EPS)
        cqn = (cq * rq * gq_ref[...]).astype(BF16)
        ckv = s_ref[:, S_CKV:S_KR]
        rkv = lax.rsqrt(jnp.mean(ckv * ckv, axis=-1, keepdims=True) + EPS)
        ckvn = (ckv * rkv * gkv_ref[...]).astype(BF16)
        cqn_ref[...] = cqn
        ckvn_ref[...] = ckvn
        tc, ta, tb = c_ref[...], a_ref[...], b_ref[...]
        q = jnp.dot(cqn, wq_ref[...], preferred_element_type=F32)
        kv = jnp.dot(ckvn, wkv_ref[...], preferred_element_type=F32)
        kr = _rope(s_ref[:, S_KR:S_F], tc, ta, tb)
        for h in range(N_HEADS):
            sl = slice(h * LANES, (h + 1) * LANES)
            mq_ref[:, sl] = _rope(q[:, sl], tc, ta, tb).astype(BF16)
            mk_ref[:, sl] = (kv[:, sl] + kr).astype(BF16)
        mv_ref[...] = kv[:, HW:].astype(BF16)
        z = s_ref[:, S_F:S_END - LANES] + bf_ref[...]
        lf_ref[...] = jnp.minimum(z, 0.0) - jnp.log(1.0 + jnp.exp(-jnp.abs(z)))

    def row(w):
        return pl.BlockSpec((bt, w), lambda i: (i, 0))

    def full(arr):
        return pl.BlockSpec(arr.shape, lambda i: (0, 0))

    return pl.pallas_call(
        body, name=name, grid=(t // bt,),
        in_specs=[row(S_END), full(g_q), full(g_kv), full(w_uq), full(w_ukv), row(LANES), row(LANES), row(LANES), full(b_f)],
        out_specs=[row(HW), row(HW), row(FW), row(LANES), row(Q_RANK), row(KV_RANK)],
        out_shape=[jax.ShapeDtypeStruct((t, HW), BF16)] * 2 + [jax.ShapeDtypeStruct((t, FW), BF16), jax.ShapeDtypeStruct((t, LANES), F32),
                   jax.ShapeDtypeStruct((t, Q_RANK), BF16), jax.ShapeDtypeStruct((t, KV_RANK), BF16)],
        compiler_params=_params(("parallel",)),
    )(small, g_q, g_kv, w_uq, w_ukv, tab_c, tab_a, tab_b, b_f)


def mla_prep_bwd(dmq, dmk, dmv, dlf, small, g_q, g_kv, w_uq, w_ukv, tab_c, tab_a, tab_b, b_f, *, name, bt=512):
    t = small.shape[0]
    bt = min(bt, t)

    def body(dmq_ref, dmk_ref, dmv_ref, dlf_ref, s_ref, gq_ref, gkv_ref, wq_ref, wkv_ref, c_ref, a_ref, b_ref, bf_ref,
             ds_ref, dq_ref, dkv_ref, dgq_ref, dgkv_ref, db_ref):
        tc, ta, tb = c_ref[...], a_ref[...], b_ref[...]
        lane = lax.broadcasted_iota(jnp.int32, (1, LANES), 1)
        dkr = jnp.zeros((bt, LANES), F32)
        for h in range(N_HEADS):
            sl = slice(h * LANES, (h + 1) * LANES)
            dq_ref[:, sl] = _rope_bwd(dmq_ref[:, sl], tc, ta, tb).astype(BF16)
            dkr = dkr + dmk_ref[:, sl]
        dkv_ref[:, :HW] = dmk_ref[...].astype(BF16)
        dkv_ref[:, HW:] = dmv_ref[...].astype(BF16)
        in_rope = (lane >= HEAD_DIM) & (lane < HEAD_DIM + ROPE_DIM)
        ds_ref[:, S_KR:S_F] = jnp.where(in_rope, _rope_bwd(dkr, tc, ta, tb), 0.0).astype(BF16)

        def norm_bwd(raw, g_ref, dn, dg_ref):
            r = lax.rsqrt(jnp.mean(raw * raw, axis=-1, keepdims=True) + EPS)
            u = dn * g_ref[...]
            dot = jnp.mean(u * raw, axis=-1, keepdims=True)
            dg_ref[...] += jnp.sum(dn * (raw * r), axis=0, keepdims=True)
            return r * u - raw * (r * r * r * dot)

        @pl.when(pl.program_id(0) == 0)
        def _():
            dgq_ref[...] = jnp.zeros_like(dgq_ref)
            dgkv_ref[...] = jnp.zeros_like(dgkv_ref)
            db_ref[...] = jnp.zeros_like(db_ref)

        dcqn = lax.dot_general(dq_ref[...], wq_ref[...], NT, preferred_element_type=F32)
        ds_ref[:, S_CQ:S_CKV] = norm_bwd(s_ref[:, S_CQ:S_CKV], gq_ref, dcqn, dgq_ref).astype(BF16)
        dckvn = lax.dot_general(dkv_ref[...], wkv_ref[...], NT, preferred_element_type=F32)
        ds_ref[:, S_CKV:S_KR] = norm_bwd(s_ref[:, S_CKV:S_KR], gkv_ref, dckvn, dgkv_ref).astype(BF16)
        z = s_ref[:, S_F:S_END - LANES] + bf_ref[...]
        dz = jnp.where(lane < N_HEADS, dlf_ref[...] / (1.0 + jnp.exp(z)), 0.0)
        db_ref[...] += jnp.sum(dz, axis=0, keepdims=True)
        ds_ref[:, S_F:S_END - LANES] = dz.astype(BF16)
        ds_ref[:, S_END - LANES:] = jnp.zeros((bt, LANES), BF16)

    def row(w):
        return pl.BlockSpec((bt, w), lambda i: (i, 0))

    def full(arr):
        return pl.BlockSpec(arr.shape, lambda i: (0, 0))

    def vec(w):
        return pl.BlockSpec((1, w), lambda i: (0, 0))

    return pl.pallas_call(
        body, name=name, grid=(t // bt,),
        in_specs=[row(HW), row(HW), row(FW), row(LANES), row(S_END), full(g_q), full(g_kv), full(w_uq), full(w_ukv),
                  row(LANES), row(LANES), row(LANES), full(b_f)],
        out_specs=[row(S_END), row(HW), row(HW + FW), vec(Q_RANK), vec(KV_RANK), vec(LANES)],
        out_shape=[jax.ShapeDtypeStruct((t, S_END), BF16), jax.ShapeDtypeStruct((t, HW), BF16),
                   jax.ShapeDtypeStruct((t, HW + FW), BF16), jax.ShapeDtypeStruct((1, Q_RANK), F32),
                   jax.ShapeDtypeStruct((1, KV_RANK), F32), jax.ShapeDtypeStruct((1, LANES), F32)],
        compiler_params=_params(("arbitrary",)),
    )(dmq, dmk, dmv, dlf, small, g_q, g_kv, w_uq, w_ukv, tab_c, tab_a, tab_b, b_f)


class Side:
    def __init__(self, ins, out_shapes, n_sems, first, last, mid=None):
        self.ins, self.out_shapes, self.n_sems = list(ins), list(out_shapes), n_sems
        self.first, self.mid, self.last = first, mid, last

    def specs(self):
        return [ANY] * len(self.ins), [ANY] * len(self.out_shapes)

    def sems(self):
        return [pltpu.SemaphoreType.DMA((self.n_sems,)), pltpu.SemaphoreType.DMA((self.n_sems,))]


def _lane():
    return lax.broadcasted_iota(jnp.int32, (1, LANES), 1)


def _halves(x):
    zero = jnp.zeros_like(x)
    return [jnp.where(_lane() < HEAD_DIM, x, zero), jnp.where(_lane() >= HEAD_DIM, x, zero)]


def _groups(x):
    return [x[:, :LANES], x[:, LANES:]]


def _pick_row(tile, h):
    row = lax.broadcasted_iota(jnp.int32, (tile.shape[0], 1), 0)
    return jnp.sum(jnp.where(row == h, tile, 0.0), axis=0, keepdims=True)


def _pick_lane(tile, h):
    return jnp.sum(jnp.where(_lane() == h, tile, 0.0), axis=1, keepdims=True)


def _row_halves(x):
    row = lax.broadcasted_iota(jnp.int32, (LANES, 1), 0)
    zero = jnp.zeros_like(x)
    return [jnp.where(row < HEAD_DIM, x, zero), jnp.where(row >= HEAD_DIM, x, zero)]


def _below_diagonal(s, lead=0):
    r = lax.broadcasted_iota(jnp.int32, s.shape, 0)
    c = lax.broadcasted_iota(jnp.int32, s.shape, 1)
    return jnp.where(c <= r + lead, s, -jnp.inf)


def _above_diagonal(s):
    r = lax.broadcasted_iota(jnp.int32, s.shape, 0)
    c = lax.broadcasted_iota(jnp.int32, s.shape, 1)
    return jnp.where(r <= c, s, -jnp.inf)


def to_blocks_t(x, blk):
    t, c = x.shape
    return x.reshape(t // blk, blk, c).transpose(0, 2, 1)


def _split_refs(refs, counts):
    out, at = [], 0
    for n in counts:
        out.append(refs[at:at + n])
        at += n
    return out


def flash_fwd(qt_arr, k_arr, vt_arr, f_cum, *, qoff, koff, voff, pair, scale, name, blk=512, side=None):
    t = k_arr.shape[0]
    tblk = qt_arr.shape[2]
    blk = max(min(blk, t), tblk)
    sub = blk // tblk
    nb = t // blk
    w = LANES if pair else 2 * LANES
    has_bias = f_cum is not None
    ins = [qt_arr, k_arr, vt_arr] + ([f_cum] if has_bias else [])

    def wide(ref, first):
        parts = [ref[first + u] for u in range(sub)]
        return parts[0] if sub == 1 else jnp.concatenate(parts, axis=1)
    s_ins, s_outs = (side.ins, side.out_shapes) if side else ([], [])

    def body(*refs):
        main, si, outs, so, sems = _split_refs(refs, [len(ins), len(s_ins), 2, len(s_outs), 2 if side else 0])
        qt_ref, k_ref, vt_ref = main[:3]
        f_ref = main[3] if has_bias else None
        o_ref, st_ref = outs
        g, i = pl.program_id(0), pl.program_id(1)
        step_id = g * nb + i
        if side:
            @pl.when(step_id == 0)
            def _():
                side.first(si, so, *sems)

            if side.mid is not None:
                @pl.when(step_id == (3 * PAIRS * nb) // 4)
                def _():
                    side.mid(si, so, *sems)

        qt = (wide(qt_ref, 0).astype(F32) * (scale * LOG2E)).astype(BF16)
        qts = _row_halves(qt) if pair else [qt[:LANES], qt[LANES:]]

        def step(j, carry, diagonal):
            rows = pl.ds(pl.multiple_of(j * blk, blk), blk)
            kk = k_ref[rows, :]
            ks = [kk, kk] if pair else _groups(kk)
            vt = wide(vt_ref, sub * j)
            out = []
            for n in range(2):
                m, l, acc = carry[n]
                s = jnp.dot(ks[n], qts[n], preferred_element_type=F32)
                if has_bias:
                    s = s - LOG2E * _pick_lane(f_ref[rows, :], 2 * g + n)
                if diagonal:
                    s = _above_diagonal(s)
                m_new = jnp.maximum(m, jnp.max(s, axis=0, keepdims=True))
                alpha = jnp.exp2(m - m_new)
                p = jnp.exp2(s - m_new)
                out.append((m_new, alpha * l + jnp.sum(p, axis=0, keepdims=True),
                            alpha * acc + jnp.dot(vt[n * HEAD_DIM:(n + 1) * HEAD_DIM], p.astype(BF16),
                                                  preferred_element_type=F32)))
            return tuple(out)

        init = tuple((jnp.full((1, blk), -jnp.inf, F32), jnp.zeros((1, blk), F32), jnp.zeros((HEAD_DIM, blk), F32))
                     for _ in range(2))
        carry = lax.fori_loop(0, i, lambda j, c: step(j, c, False), init)
        (ma, la, acca), (mb, lb, accb) = step(i, carry, True)
        o_ref[...] = jnp.concatenate([acca / la, accb / lb], axis=0).T
        row = lax.broadcasted_iota(jnp.int32, (LANES, 1), 0)
        st_ref[0] = jnp.where(row == 0, ma + jnp.log2(la), jnp.where(row == 1, mb + jnp.log2(lb), 0.0)).T
        if side:
            @pl.when(step_id == PAIRS * nb - 1)
            def _():
                side.last(si, so, *sems)

    in_specs = [pl.BlockSpec((sub, w, tblk), lambda g, i: (i, qoff + g, 0)), pl.BlockSpec((t, w), lambda g, i: (0, koff + g)),
                pl.BlockSpec((t // tblk, LANES, tblk), lambda g, i: (0, voff + g, 0))]
    if has_bias:
        in_specs.append(pl.BlockSpec((t, LANES), lambda g, i: (0, 0)))
    s_in_specs, s_out_specs = side.specs() if side else ([], [])
    return pl.pallas_call(
        body, name=name, grid=(PAIRS, nb), in_specs=in_specs + s_in_specs,
        out_specs=[pl.BlockSpec((blk, LANES), lambda g, i: (i, g)), pl.BlockSpec((1, blk, LANES), lambda g, i: (g, i, 0))]
        + s_out_specs,
        out_shape=[jax.ShapeDtypeStruct((t, PAIRS * LANES), F32), jax.ShapeDtypeStruct((PAIRS, t, LANES), F32)] + list(s_outs),
        scratch_shapes=side.sems() if side else [],
        compiler_params=_params(("arbitrary", "arbitrary")),
    )(*ins, *s_ins)


def mix_norm(fo, mo, g_fo, g_mo, *, name, bt=512):
    t, d = fo.shape
    bt = min(bt, t)

    def body(fo_ref, mo_ref, gf_ref, gm_ref, o_ref):
        for n, (x_ref, g_ref) in enumerate(((fo_ref, gf_ref), (mo_ref, gm_ref))):
            xv = x_ref[...]
            r = lax.rsqrt(jnp.mean(xv * xv, axis=-1, keepdims=True) + EPS)
            o_ref[:, n * d:(n + 1) * d] = (xv * r * g_ref[...]).astype(BF16)

    row = pl.BlockSpec((bt, d), lambda i: (i, 0))
    vec = pl.BlockSpec((1, d), lambda i: (0, 0))
    return pl.pallas_call(
        body, name=name, grid=(t // bt,), in_specs=[row, row, vec, vec],
        out_specs=pl.BlockSpec((bt, 2 * d), lambda i: (i, 0)),
        out_shape=jax.ShapeDtypeStruct((t, 2 * d), BF16),
        compiler_params=_params(("parallel",)),
    )(fo, mo, g_fo, g_mo)


def mix_norm_bwd(dmixed, fo, mo, g_fo, g_mo, st_f, st_m, *, name, bt=512, side=None):
    t, d = fo.shape
    bt = min(bt, t)

    def body(dm_ref, fo_ref, mo_ref, gf_ref, gm_ref, sf_ref, sm_ref, dfo_ref, dmo_ref, dgf_ref, dgm_ref, sfo_ref, smo_ref):
        @pl.when(pl.program_id(0) == 0)
        def _():
            dgf_ref[...] = jnp.zeros_like(dgf_ref)
            dgm_ref[...] = jnp.zeros_like(dgm_ref)

        groups = ((fo_ref, gf_ref, dfo_ref, dgf_ref, sf_ref, sfo_ref), (mo_ref, gm_ref, dmo_ref, dgm_ref, sm_ref, smo_ref))
        for n, (x_ref, g_ref, dx_ref, dg_ref, st_ref, sto_ref) in enumerate(groups):
            xv = x_ref[...]
            dhv = dm_ref[:, n * d:(n + 1) * d]
            r = lax.rsqrt(jnp.mean(xv * xv, axis=-1, keepdims=True) + EPS)
            u = dhv * g_ref[...]
            dxb = (r * u - xv * (r * r * r * jnp.mean(u * xv, axis=-1, keepdims=True))).astype(BF16)
            dx_ref[...] = dxb
            dg_ref[...] += jnp.sum(dhv * (xv * r), axis=0, keepdims=True)
            prod = xv * dxb.astype(F32)
            for g in range(PAIRS):
                grp = prod[:, g * LANES:(g + 1) * LANES]
                da = jnp.sum(jnp.where(_lane() < HEAD_DIM, grp, 0.0), axis=1, keepdims=True)
                db = jnp.sum(jnp.where(_lane() >= HEAD_DIM, grp, 0.0), axis=1, keepdims=True)
                sto_ref[g] = jnp.where(_lane() == 2, da, jnp.where(_lane() == 3, db, st_ref[g]))

    row = pl.BlockSpec((bt, d), lambda i: (i, 0))
    vec = pl.BlockSpec((1, d), lambda i: (0, 0))
    stat = pl.BlockSpec((PAIRS, bt, LANES), lambda i: (0, i, 0))
    return gridded(
        body, name=name, grid=(t // bt,),
        in_specs=[pl.BlockSpec((bt, 2 * d), lambda i: (i, 0)), row, row, vec, vec, stat, stat],
        out_specs=[row, row, vec, vec, stat, stat],
        out_shape=[jax.ShapeDtypeStruct((t, d), BF16)] * 2 + [jax.ShapeDtypeStruct((1, d), F32)] * 2
        + [jax.ShapeDtypeStruct(st_f.shape, F32)] * 2,
        ins=[dmixed, fo, mo, g_fo, g_mo, st_f, st_m], semantics=("arbitrary",), side=side)


def flash_bwd(q_arr, qt_arr, k_arr, v_arr, do_arr, dot_arr, st, f_blocks, *, qoff, koff, voff, pair, scale, name, qblk=1024,
              side=None):
    t = q_arr.shape[0]
    blk = qt_arr.shape[2]
    qblk = max(min(qblk, t), blk)
    sub = qblk // blk
    nb, nbq = t // blk, t // qblk
    w = LANES if pair else 2 * LANES
    hw = w // 2
    has_bias = f_blocks is not None
    split = _halves if pair else _groups
    ins = [q_arr, qt_arr, k_arr, v_arr, do_arr, dot_arr, st] + ([f_blocks] if has_bias else [])
    n_out = 4 if has_bias else 3
    s_ins, s_outs = (side.ins, side.out_shapes) if side else ([], [])

    def wide(ref, first):
        parts = [ref[first + u] for u in range(sub)]
        return parts[0] if sub == 1 else jnp.concatenate(parts, axis=1)

    def body(*refs):
        main, si, outs, so, sems = _split_refs(refs, [len(ins), len(s_ins), n_out, len(s_outs), 2 if side else 0])
        q_ref, qt_ref, k_ref, v_ref, do_ref, dot_ref, st_ref = main[:7]
        dq_ref, dk_ref, dv_ref = outs[:3]
        g, j = pl.program_id(0), pl.program_id(1)
        step_id = g * nb + j
        if side:
            @pl.when(step_id == 0)
            def _():
                side.first(si, so, *sems)

        kk, vv = k_ref[...], v_ref[...]
        ks = [kk, kk] if pair else _groups(kk)
        if has_bias:
            f_ref, df_ref = main[7], outs[3]
            fk = [LOG2E * _pick_row(f_ref[0], 2 * g + n) for n in range(2)]

            @pl.when(step_id == 0)
            def _():
                df_ref[...] = jnp.zeros_like(df_ref)

        def step(i, carry, diagonal):
            rows = pl.ds(pl.multiple_of(i * qblk, qblk), qblk)
            qs = split((q_ref[rows, :].astype(F32) * (scale * LOG2E)).astype(BF16))
            qt = (wide(qt_ref, sub * i).astype(F32) * (scale * LOG2E)).astype(BF16)
            dos = [h.astype(BF16) for h in _halves(do_ref[rows, :].astype(F32))]
            dot = wide(dot_ref, sub * i)
            stats = st_ref[0, rows, :]
            new, dqs, row_sums = [], [], []
            for n in range(2):
                dkt, dvt, dfk = carry[n]
                s = lax.dot_general(qs[n], ks[n], NT, preferred_element_type=F32)
                if has_bias:
                    s = s - fk[n]
                if diagonal:
                    s = _below_diagonal(s, i * qblk - j * blk)
                p = jnp.exp2(s - stats[:, n:n + 1])
                dp = lax.dot_general(dos[n], vv, NT, preferred_element_type=F32)
                ds = p * (dp - stats[:, 2 + n:3 + n])
                dsb = ds.astype(BF16)
                dvt = dvt + jnp.dot(dot[n * HEAD_DIM:(n + 1) * HEAD_DIM], p.astype(BF16), preferred_element_type=F32)
                dkt = dkt + jnp.dot(qt[n * hw:(n + 1) * hw], dsb, preferred_element_type=F32)
                dqs.append(jnp.dot(dsb, ks[n], preferred_element_type=F32))
                if has_bias:
                    dfk = dfk - jnp.sum(ds, axis=0, keepdims=True)
                    row_sums.append(jnp.sum(ds, axis=1, keepdims=True))
                new.append((dkt, dvt, dfk))
            dq = (jnp.where(_lane() < HEAD_DIM, dqs[0], dqs[1]) if pair else jnp.concatenate(dqs, axis=1)) * scale
            if has_bias:
                df_ref[rows, :] += jnp.where(_lane() == 2 * g, row_sums[0], jnp.where(_lane() == 2 * g + 1, row_sums[1], 0.0))

            @pl.when(j == 0)
            def _():
                dq_ref[rows, :] = dq

            @pl.when(j > 0)
            def _():
                dq_ref[rows, :] += dq

            return tuple(new)

        init = tuple((jnp.zeros((hw, blk), F32), jnp.zeros((HEAD_DIM, blk), F32), jnp.zeros((1, blk), F32)) for _ in range(2))
        first = j // sub
        carry = step(first, init, True)
        (dka, dva, dfa), (dkb, dvb, dfb) = lax.fori_loop(first + 1, nbq, lambda i, c: step(i, c, False), carry)
        dk_ref[...] = jnp.concatenate([dka, dkb], axis=0).T * LN2
        dv_ref[...] = jnp.concatenate([dva, dvb], axis=0).T
        if has_bias:
            row = lax.broadcasted_iota(jnp.int32, (LANES, 1), 0)
            by_head = jnp.where(row == 2 * g, dfa, jnp.where(row == 2 * g + 1, dfb, 0.0))
            df_ref[pl.ds(pl.multiple_of(j * blk, blk), blk), :] += by_head.T
        if side:
            @pl.when(step_id == PAIRS * nb - 1)
            def _():
                side.last(si, so, *sems)

    in_specs = [pl.BlockSpec((t, w), lambda g, j: (0, qoff + g)), pl.BlockSpec((nb, w, blk), lambda g, j: (0, qoff + g, 0)),
                pl.BlockSpec((blk, w), lambda g, j: (j, koff + g)), pl.BlockSpec((blk, LANES), lambda g, j: (j, voff + g)),
                pl.BlockSpec((t, LANES), lambda g, j: (0, g)), pl.BlockSpec((nb, LANES, blk), lambda g, j: (0, g, 0)),
                pl.BlockSpec((1, t, LANES), lambda g, j: (g, 0, 0))]
    out_specs = [pl.BlockSpec((t, w), lambda g, j: (0, g)), pl.BlockSpec((blk, w), lambda g, j: (j, g)),
                 pl.BlockSpec((blk, LANES), lambda g, j: (j, g))]
    out_shape = [jax.ShapeDtypeStruct((t, PAIRS * w), F32)] * 2 + [jax.ShapeDtypeStruct((t, PAIRS * LANES), F32)]
    if has_bias:
        in_specs.append(pl.BlockSpec((1, N_HEADS, blk), lambda g, j: (j, 0, 0)))
        out_specs.append(pl.BlockSpec((t, LANES), lambda g, j: (0, 0)))
        out_shape.append(jax.ShapeDtypeStruct((t, LANES), F32))
    s_in_specs, s_out_specs = side.specs() if side else ([], [])
    return pl.pallas_call(
        body, name=name, grid=(PAIRS, nb), in_specs=in_specs + s_in_specs, out_specs=out_specs + s_out_specs,
        out_shape=out_shape + list(s_outs), scratch_shapes=side.sems() if side else [],
        compiler_params=_params(("arbitrary", "arbitrary")),
    )(*ins, *s_ins)


def _adamw_math(w, g, m, v):
    nm = ADAM_B1 * m + (1.0 - ADAM_B1) * g
    nv = ADAM_B2 * v + (1.0 - ADAM_B2) * (g * g)
    m_hat = nm / (1.0 - ADAM_B1 ** ADAM_STEP)
    v_hat = nv / (1.0 - ADAM_B2 ** ADAM_STEP)
    return -ADAM_LR * (m_hat / (jnp.sqrt(v_hat) + ADAM_EPS) + ADAM_WD * w), nm, nv


def adamw(w, g, m, v, *, name):
    rws, cols = w.shape
    br = _row_block(rws)

    def body(w_ref, g_ref, m_ref, v_ref, d_ref, nm_ref, nv_ref):
        d_ref[...], nm_ref[...], nv_ref[...] = _adamw_math(w_ref[...], g_ref[...], m_ref[...], v_ref[...])

    blk = pl.BlockSpec((br, cols), lambda i: (i, 0))
    return pl.pallas_call(
        body, name=name, grid=(rws // br,), in_specs=[blk] * 4, out_specs=[blk] * 3,
        out_shape=[jax.ShapeDtypeStruct((rws, cols), F32)] * 3,
        compiler_params=_params(("parallel",)),
    )(w, g, m, v)


def adamw_halves(w, g_mine, g_other, m, v, core, *, name):
    _, k, n = w.shape
    br = _row_block(k // 2)
    nh = k // 2 // br

    def body(c_ref, w_ref, gm_ref, go_ref, m_ref, v_ref, g_out, d_ref, nm_ref, nv_ref):
        gv = jnp.where(pl.program_id(0) == c_ref[0], gm_ref[...], go_ref[...])
        g_out[0] = gv
        d_ref[0], nm_ref[0], nv_ref[0] = _adamw_math(w_ref[0], gv, m_ref[0], v_ref[0])

    full = pl.BlockSpec((1, br, n), lambda hb, i, c: (0, hb * nh + i, 0))
    half = pl.BlockSpec((br, n), lambda hb, i, c: (i, 0))
    return pl.pallas_call(
        body, name=name,
        grid_spec=pltpu.PrefetchScalarGridSpec(num_scalar_prefetch=1, grid=(2, nh), in_specs=[full, half, half, full, full],
                                               out_specs=[full] * 4),
        out_shape=[jax.ShapeDtypeStruct(w.shape, F32)] * 4,
        compiler_params=_params(("parallel", "parallel")),
    )(core, w, g_mine, g_other, m, v)


def adamw_halves_t(wt, gt_mine, gt_other, mt, vt, core, *, name, bc=128):
    n, k = wt.shape
    nh = k // 2 // bc

    def body(c_ref, w_ref, gm_ref, go_ref, m_ref, v_ref, g_out, d_ref, nm_ref, nv_ref):
        gv = jnp.where(pl.program_id(0) == c_ref[0], gm_ref[...], go_ref[...])
        g_out[...] = gv
        d_ref[...], nm_ref[...], nv_ref[...] = _adamw_math(w_ref[...], gv, m_ref[...], v_ref[...])

    full = pl.BlockSpec((n, bc), lambda hb, i, c: (0, hb * nh + i))
    half = pl.BlockSpec((n, bc), lambda hb, i, c: (0, i))
    return pl.pallas_call(
        body, name=name,
        grid_spec=pltpu.PrefetchScalarGridSpec(num_scalar_prefetch=1, grid=(2, nh), in_specs=[full, half, half, full, full],
                                               out_specs=[full] * 4),
        out_shape=[jax.ShapeDtypeStruct(wt.shape, F32)] * 4,
        compiler_params=_params(("parallel", "parallel")),
    )(core, wt, gt_mine, gt_other, mt, vt)


def add_pair(dw, recv, core, *, name):
    n4, k, n = dw.shape
    half = (1, k // 2, n) if split_axis(k) == 0 else (1, k, n // 2)
    mine = (lambda q, c: (q, c[0], 0)) if split_axis(k) == 0 else (lambda q, c: (q, 0, c[0]))

    def body(c_ref, a_ref, b_ref, o_ref):
        o_ref[...] = (a_ref[...] + b_ref[...].astype(F32)).astype(BF16)

    return pl.pallas_call(
        body, name=name,
        grid_spec=pltpu.PrefetchScalarGridSpec(
            num_scalar_prefetch=1, grid=(n4,),
            in_specs=[pl.BlockSpec(half, mine), pl.BlockSpec(half, lambda q, c: (q, 0, 0))],
            out_specs=pl.BlockSpec(half, lambda q, c: (q, 0, 0))),
        out_shape=jax.ShapeDtypeStruct((n4,) + half[1:], BF16),
        compiler_params=_params(("parallel",)),
    )(core, dw, recv)


def sum_chips(parts, *, name):
    n4, r, n = parts.shape
    if r % 16 == 0:
        br, bc = _row_block(r), n
    else:
        br, bc = r, LANES

    def body(p_ref, o_ref):
        acc = p_ref[0].astype(F32)
        for q in range(1, n4):
            acc = acc + p_ref[q].astype(F32)
        o_ref[...] = acc

    return pl.pallas_call(
        body, name=name, grid=(r // br, n // bc),
        in_specs=[pl.BlockSpec((n4, br, bc), lambda i, j: (0, i, j))], out_specs=pl.BlockSpec((br, bc), lambda i, j: (i, j)),
        out_shape=jax.ShapeDtypeStruct((r, n), F32),
        compiler_params=_params(("parallel", "parallel")),
    )(parts)


ANY = pl.BlockSpec(memory_space=pl.ANY)


def _place():
    x, y, c = lax.axis_index("x"), lax.axis_index("y"), lax.axis_index("c")
    chips = [(1 - x, y), (x, 1 - y), (1 - x, 1 - y)]
    return x, y, c, chips


def _copy(src, dst, send_sems, recv_sems, k, to):
    return pltpu.make_async_remote_copy(src_ref=src, dst_ref=dst, send_sem=send_sems.at[k], recv_sem=recv_sems.at[k],
                                        device_id=to, device_id_type=MESH)


def split_axis(rows):
    return 0 if rows % 32 == 0 else 1


def _half(ref, lead, hf):
    rows, cols = ref.shape[-2:]
    if split_axis(rows) == 0:
        at = (pl.ds(hf * (rows // 2), rows // 2), slice(None))
    else:
        at = (slice(None), pl.ds(hf * (cols // 2), cols // 2))
    return ref.at[at] if lead is None else ref.at[(lead,) + at]


def _gather_first(srcs, dsts, ssems, rsems):
    x, y, c, chips = _place()
    for ti, (s, d) in enumerate(zip(srcs, dsts)):
        for j, (cx, cy) in enumerate(chips):
            _copy(_half(s, None, c), _half(d, 2 * x + y, c), ssems, rsems, 3 * ti + j, (cx, cy, c)).start()


def _gather_mid(srcs, dsts, ssems, rsems):
    x, y, c, chips = _place()
    n1 = 3 * len(srcs)
    for ti, d in enumerate(dsts):
        for j, (cx, cy) in enumerate(chips):
            landed = _half(d, 2 * cx + cy, c)
            _copy(landed, landed, ssems, rsems, 3 * ti + j, (cx, cy, c)).wait_recv()
            _copy(landed, landed, ssems, rsems, n1 + 3 * ti + j, (x, y, 1 - c)).start()


def _gather_last(srcs, dsts, ssems, rsems):
    x, y, c, chips = _place()
    n1 = 3 * len(srcs)
    for ti, (s, d) in enumerate(zip(srcs, dsts)):
        for j, (cx, cy) in enumerate(chips):
            other = _half(d, 2 * cx + cy, 1 - c)
            _copy(other, other, ssems, rsems, n1 + 3 * ti + j, (x, y, 1 - c)).wait_recv()
        for j, (cx, cy) in enumerate(chips):
            mine = _half(s, None, c)
            _copy(mine, mine, ssems, rsems, 3 * ti + j, (cx, cy, c)).wait_send()
            _copy(mine, mine, ssems, rsems, n1 + 3 * ti + j, (x, y, 1 - c)).wait_send()


def gather_side(shards):
    return Side(shards, [jax.ShapeDtypeStruct((N_CHIPS,) + s.shape, s.dtype) for s in shards], 6 * len(shards),
                _gather_first, _gather_last, _gather_mid)


def _scatter_first(srcs, dsts, ssems, rsems):
    x, y, c, chips = _place()
    for ti, (s, d) in enumerate(zip(srcs, dsts)):
        for j, (cx, cy) in enumerate(chips):
            _copy(s.at[2 * cx + cy], d.at[2 * x + y], ssems, rsems, 3 * ti + j, (cx, cy, c)).start()


def _scatter_last(srcs, dsts, ssems, rsems):
    x, y, c, chips = _place()
    for ti, (s, d) in enumerate(zip(srcs, dsts)):
        for j, (cx, cy) in enumerate(chips):
            _copy(s.at[2 * cx + cy], d.at[2 * cx + cy], ssems, rsems, 3 * ti + j, (cx, cy, c)).wait_recv()
        for j, (cx, cy) in enumerate(chips):
            _copy(s.at[2 * cx + cy], d.at[2 * cx + cy], ssems, rsems, 3 * ti + j, (cx, cy, c)).wait_send()


def scatter_side(parts):
    return Side(parts, [jax.ShapeDtypeStruct(p.shape, p.dtype) for p in parts], 3 * len(parts), _scatter_first, _scatter_last)


def run_side(side, *, name):
    n_in, n_out = len(side.ins), len(side.out_shapes)

    def body(*refs):
        si, so, sems = _split_refs(refs, [n_in, n_out, 2])
        side.first(si, so, *sems)
        if side.mid is not None:
            side.mid(si, so, *sems)
        side.last(si, so, *sems)

    in_specs, out_specs = side.specs()
    return pl.pallas_call(body, name=name, in_specs=in_specs, out_specs=out_specs, out_shape=side.out_shapes,
                          scratch_shapes=side.sems())(*side.ins)


def _swap_first(srcs, dsts, ssems, rsems):
    x, y, c, _ = _place()
    for k, (s, d) in enumerate(zip(srcs, dsts)):
        _copy(s, d, ssems, rsems, k, (x, y, 1 - c)).start()


def _swap_last(srcs, dsts, ssems, rsems):
    x, y, c, _ = _place()
    for k, (s, d) in enumerate(zip(srcs, dsts)):
        _copy(s, d, ssems, rsems, k, (x, y, 1 - c)).wait()


def swap_side(xs):
    return Side(xs, [jax.ShapeDtypeStruct(a.shape, a.dtype) for a in xs], len(xs), _swap_first, _swap_last)


def allreduce_small(s):
    n_dev = 8

    def body(s_ref, out_ref, buf, send_sems, recv_sems):
        x, y, c, _ = _place()
        me = 4 * x + 2 * y + c
        buf[me] = s_ref[...]
        sends = []
        for k in range(1, n_dev):
            px = 1 - x if k & 4 else x
            py = 1 - y if k & 2 else y
            pc = 1 - c if k & 1 else c
            cp = _copy(s_ref, buf.at[me], send_sems, recv_sems, k - 1, (px, py, pc))
            cp.start()
            sends.append((cp, 4 * px + 2 * py + pc))
        for k, (cp, peer) in enumerate(sends):
            _copy(s_ref, buf.at[peer], send_sems, recv_sems, k, (x, y, c)).wait_recv()
        for cp, _ in sends:
            cp.wait_send()
        acc = buf[0]
        for d in range(1, n_dev):
            acc = acc + buf[d]
        out_ref[...] = acc

    vm = pl.BlockSpec(memory_space=pltpu.VMEM)
    return pl.pallas_call(
        body, name="allreduce_small", in_specs=[vm], out_specs=vm,
        out_shape=jax.ShapeDtypeStruct(s.shape, F32),
        scratch_shapes=[pltpu.VMEM((n_dev,) + s.shape, F32), pltpu.SemaphoreType.DMA((n_dev - 1,)),
                        pltpu.SemaphoreType.DMA((n_dev - 1,))],
    )(s)


def join_cols(sm):
    n4, k, n = sm.shape
    return sm.transpose(1, 0, 2).reshape(k, n4 * n)


def split_cols(full):
    k, n = full.shape
    return full.reshape(k, N_CHIPS, n // N_CHIPS).transpose(1, 0, 2)


def _pad_heads(w, width):
    lead = w.shape[:-1]
    w = w.reshape(lead + (N_HEADS, width))
    return jnp.pad(w, [(0, 0)] * len(lead) + [(0, 0), (0, LANES - width)]).reshape(lead + (HW,))


def _unpad_heads(w, width):
    lead = w.shape[:-1]
    return w.reshape(lead + (N_HEADS, LANES))[..., :width].reshape(lead + (N_HEADS * width,))


def split_w_in_t(w_in_t):
    d = w_in_t.shape[1]
    o_f = 3 * FW
    o_cq = o_f + N_HEADS
    o_ckv = o_cq + Q_RANK
    o_kr = o_ckv + KV_RANK

    def z(n):
        return jnp.zeros((n, d), w_in_t.dtype)

    small = jnp.concatenate([w_in_t[o_cq:o_ckv], w_in_t[o_ckv:o_kr], z(HEAD_DIM), w_in_t[o_kr:], z(LANES - HEAD_DIM - ROPE_DIM),
                             w_in_t[o_f:o_cq], z(LANES - N_HEADS), z(LANES)], axis=0)
    return w_in_t[:o_f], small


def join_w_in_t(d_qkv_t, d_small_t):
    kr = S_KR + HEAD_DIM
    return jnp.concatenate([d_qkv_t, d_small_t[S_F:S_F + N_HEADS], d_small_t[S_CQ:S_CKV], d_small_t[S_CKV:S_KR],
                            d_small_t[kr:kr + ROPE_DIM]], axis=0)


def rope_tables(pos):
    t = pos.shape[0]
    inv_freq = ROPE_THETA ** (-jnp.arange(0, ROPE_DIM, 2, dtype=F32) / ROPE_DIM)
    ang = pos.astype(F32)[:, None] * inv_freq
    cos, sin = jnp.cos(ang), jnp.sin(ang)
    half = ROPE_DIM // 2

    def z(n):
        return jnp.zeros((t, n), F32)

    tab_c = jnp.concatenate([jnp.ones((t, HEAD_DIM), F32), cos, cos, z(LANES - HEAD_DIM - ROPE_DIM)], axis=1)
    tab_a = jnp.concatenate([z(HEAD_DIM), -sin, z(half), z(LANES - HEAD_DIM - ROPE_DIM)], axis=1)
    tab_b = jnp.concatenate([z(HEAD_DIM), z(half), sin, z(LANES - HEAD_DIM - ROPE_DIM)], axis=1)
    return tab_c, tab_a, tab_b


def _pad_lanes(v, n):
    return jnp.pad(v, ((0, 0), (0, n - v.shape[1])))


ATTN_BLK = 512
ATTN_FWD_BLK = 1024
ATTN_BWD_QBLK = 1024


def local_step(xs, pos, tgt, gains, w_early, late_weights, fwd_side=None, reduction=None):
    g_attn, b_forget, g_q, g_kv, g_fo, g_mo, g_mlp, g_fin = gains
    w_in_t, w_uq, w_ukv = w_early
    t = xs.shape[0]
    blk = min(ATTN_BLK, t)
    fox_scale = 1.0 / (HEAD_DIM ** 0.5)
    mla_scale = 1.0 / ((HEAD_DIM + ROPE_DIM) ** 0.5)

    w_qkv_t, w_small_t = split_w_in_t(w_in_t)
    w_uq_p = _pad_heads(w_uq, HEAD_DIM + ROPE_DIM)
    kv = w_ukv.reshape(KV_RANK, N_HEADS, 2 * HEAD_DIM)
    w_ukv_p = jnp.concatenate([_pad_heads(kv[:, :, :HEAD_DIM].reshape(KV_RANK, FW), HEAD_DIM),
                               kv[:, :, HEAD_DIM:].reshape(KV_RANK, FW)], axis=1)
    b_f = _pad_lanes(b_forget, LANES)
    tab_c, tab_a, tab_b = rope_tables(pos)

    h1 = rmsnorm(xs, g_attn, out_dtype=BF16, name="norm_attn")
    qkv, = mm(h1, w_qkv_t, trans_b=True, out_dtypes=[BF16], name="proj_qkv")
    small, = mm(h1, w_small_t, trans_b=True, out_dtypes=[F32], name="proj_small")
    mq, mk, mv, lf, cqn, ckvn = mla_prep(small, g_q, g_kv, w_uq_p, w_ukv_p, tab_c, tab_a, tab_b, b_f, name="mla_prep")
    f_cum = cumsum_rows(lf, reverse=False, name="gate_cumsum")
    f_blocks = f_cum[:, :N_HEADS].reshape(t // blk, blk, N_HEADS).transpose(0, 2, 1)
    qkv_t, mq_t, mv_t = to_blocks_t(qkv, blk), to_blocks_t(mq, blk), to_blocks_t(mv, blk)
    fo, st_f, *gathered = flash_fwd(qkv_t, qkv, qkv_t, f_cum, qoff=0, koff=PAIRS, voff=2 * PAIRS, pair=True,
                                    scale=fox_scale, name="fox_fwd", blk=ATTN_FWD_BLK, side=fwd_side)
    w_o, w_up, w_down = late_weights(gathered)
    mo, st_m = flash_fwd(mq_t, mk, mv_t, None, qoff=0, koff=0, voff=0, pair=False, scale=mla_scale, name="mla_fwd",
                         blk=ATTN_FWD_BLK)
    mixed = mix_norm(fo, mo, g_fo, g_mo, name="norm_mix")

    def inv_rms(v):
        return lax.rsqrt(jnp.mean(v * v, axis=-1, keepdims=True) + EPS)

    def residual_then_norm(acc, res, g):
        xn = acc + res
        return xn, xn * inv_rms(xn) * g

    def norm_bwd(dh, xn, res, g):
        r = inv_rms(xn)
        uu = dh * g
        return (r * uu - xn * (r * r * r * jnp.mean(uu * xn, axis=-1, keepdims=True)) + res,
                jnp.sum(dh * (xn * r), axis=0, keepdims=True))

    def norm_bwd2(dh, xn, res, g):
        dx, dg = norm_bwd(dh, xn, res, g)
        return dx, dx, dg

    def residual_then_loss(acc, res, target, g):
        xn = acc + res
        r = inv_rms(xn)
        xh = xn * r
        e = xh * g - target
        part = 0.5 * jnp.sum(jnp.mean(e * e, axis=-1, keepdims=True), axis=0, keepdims=True)
        dy = e * (1.0 / xn.shape[1])
        uu = dy * g
        dx = r * uu - xn * (r * r * r * jnp.mean(uu * xn, axis=-1, keepdims=True))
        return dx, dx, jnp.sum(dy * xh, axis=0, keepdims=True), part + jnp.zeros_like(g)

    x1, h2 = mm(mixed, w_o, extras=[xs], vecs=[g_mlp], epilogue=residual_then_norm, out_dtypes=[F32, BF16], name="out_proj")

    def relu2(acc):
        r = jnp.maximum(acc, 0.0)
        return acc, r * r

    u, act = mm(h2, w_up, epilogue=relu2, out_dtypes=[BF16, BF16], name="mlp_up")
    dx2, dx2b, dg_fin, loss_row = mm(act, w_down, extras=[x1, tgt], vecs=[g_fin], epilogue=residual_then_loss,
                                     out_dtypes=[F32, BF16], n_sums=2, name="mlp_down_loss")
    loss = loss_row[:, :1]

    def relu2_grad(acc, uu):
        return (acc * (2.0 * jnp.maximum(uu.astype(F32), 0.0)),)

    du, = mm(dx2b, w_down, trans_b=True, extras=[u], epilogue=relu2_grad, out_dtypes=[BF16], name="mlp_down_bwd")
    dw_down = mm_tn(act, dx2b, name="dw_down").reshape(N_CHIPS, -1, w_down.shape[1])
    dx1, dx1b, dg_mlp = mm(du, w_up, trans_b=True, extras=[x1, dx2], vecs=[g_mlp], epilogue=norm_bwd2,
                           out_dtypes=[F32, BF16], n_sums=1, name="mlp_up_bwd")
    dw_up = mm_tn(h2, du, name="dw_up", col_shards=N_CHIPS)

    dmixed, = mm(dx1b, w_o, trans_b=True, out_dtypes=[F32], name="out_proj_bwd")
    dw_o = mm_tn(mixed, dx1b, name="dw_o").reshape(N_CHIPS, -1, w_o.shape[1])
    late = (dw_o, dw_up, dw_down)
    red = reduction
    dfo, dmo, dg_fo, dg_mo, st_f, st_m, *got = mix_norm_bwd(dmixed, fo, mo, g_fo, g_mo, st_f, st_m, name="norm_mix_bwd",
                                                            side=red.late_swap(late) if red else None)
    dfq, dfk, dfv, d_f, *got = flash_bwd(
        qkv, qkv_t, qkv, qkv, dfo, to_blocks_t(dfo, blk), st_f, f_blocks, qoff=0, koff=PAIRS, voff=2 * PAIRS, pair=True,
        scale=fox_scale, name="fox_bwd", qblk=ATTN_BWD_QBLK, side=red.late_scatter(got) if red else None)
    dmq, dmk, dmv, *got = flash_bwd(mq, mq_t, mk, mv, dmo, to_blocks_t(dmo, blk), st_m, None, qoff=0, koff=0, voff=0,
                                    pair=False, scale=mla_scale, name="mla_bwd", qblk=ATTN_BWD_QBLK,
                                    side=red.late_halves(got) if red else None)
    if red:
        red.late_done(got)
    dqkv = jnp.concatenate([dfq, dfk, dfv], axis=1).astype(BF16)
    dlf = cumsum_rows(d_f, reverse=True, name="gate_cumsum_bwd")
    dsmall, dq_u, dkv_u, dg_q, dg_kv, db_f = mla_prep_bwd(dmq, dmk, dmv, dlf, small, g_q, g_kv, w_uq_p, w_ukv_p,
                                                          tab_c, tab_a, tab_b, b_f, name="mla_prep_bwd")
    dw_uq_p = mm_tn(cqn, dq_u, name="dw_uq")
    dw_ukv_p = mm_tn(ckvn, dkv_u, name="dw_ukv")

    dw_in_t = join_w_in_t(mm_tn(dqkv, h1, name="dw_qkv"), mm_tn(dsmall, h1, name="dw_small"))
    dw_uq = _unpad_heads(dw_uq_p, HEAD_DIM + ROPE_DIM)
    dk_cols = _unpad_heads(dw_ukv_p[:, :HW], HEAD_DIM).reshape(KV_RANK, N_HEADS, HEAD_DIM)
    dv_cols = dw_ukv_p[:, HW:].reshape(KV_RANK, N_HEADS, HEAD_DIM)
    dw_ukv = jnp.concatenate([dk_cols, dv_cols], axis=2).reshape(KV_RANK, N_HEADS * 2 * HEAD_DIM)
    early = (dw_in_t.reshape(N_CHIPS, -1, dw_in_t.shape[1]), split_cols(dw_uq), split_cols(dw_ukv))
    grad_x, dg_attn, *got = mm([dqkv, dsmall], [w_qkv_t, w_small_t], extras=[xs, dx1], vecs=[g_attn],
                               epilogue=norm_bwd, out_dtypes=[F32], n_sums=1, name="proj_bwd",
                               side=red.early_scatter(early) if red else None)
    if red:
        red.early_done(got)
    d_gains = (dg_attn, db_f[:, :N_HEADS], dg_q, dg_kv, dg_fo, dg_mo, dg_mlp, dg_fin)
    return loss, grad_x, early, late, d_gains


class GradReduction:
    def __init__(self, core_id, chip):
        self.core_id, self.chip = core_id, chip
        self.core = core_id.reshape(1).astype(jnp.int32)
        self.grads, self.pairs, self.halves, self.others = {}, {}, {}, {}

    def _other_halves(self, grads):
        out = []
        for g in grads:
            axis = 1 + split_axis(g.shape[1])
            size = g.shape[axis] // 2
            out.append(lax.dynamic_slice_in_dim(g, (1 - self.core_id) * size, size, axis=axis).astype(BF16))
        return out

    def _add_pairs(self, group, recvs):
        self.pairs[group] = [add_pair(g, r, self.core, name="add_pair_%s_%d" % (group, n))
                             for n, (g, r) in enumerate(zip(self.grads[group], recvs))]
        return scatter_side(self.pairs[group])

    def _chip_sums(self, group, scattered):
        chip = self.chip
        with_mine = [lax.dynamic_update_index_in_dim(s, lax.dynamic_index_in_dim(p, chip, 0, keepdims=True), chip, 0)
                     for s, p in zip(scattered, self.pairs[group])]
        self.halves[group] = [sum_chips(s, name="sum_chips_%s_%d" % (group, n)) for n, s in enumerate(with_mine)]
        return self.halves[group]

    def late_swap(self, grads):
        self.grads["late"] = list(grads)
        return swap_side(self._other_halves(grads))

    def late_scatter(self, recvs):
        return self._add_pairs("late", recvs)

    def late_halves(self, scattered):
        return swap_side(self._chip_sums("late", scattered))

    def late_done(self, others):
        self.others["late"] = list(others)

    def early_scatter(self, grads):
        self.grads["early"] = list(grads)
        return self._add_pairs("early", run_side(swap_side(self._other_halves(grads)), name="swap_early_sends"))

    def early_done(self, scattered):
        self.others["early"] = list(run_side(swap_side(self._chip_sums("early", scattered)), name="swap_early_halves"))

def kernel(x, positions, attn_norm_g, w_in, b_forget, q_norm_g, w_uq, kv_norm_g, w_ukv, fox_out_g, mla_out_g, w_o, mlp_norm_g, w_up, w_down, final_norm_g, loss_target, m_attn_norm_g, m_w_in, m_b_forget, m_q_norm_g, m_w_uq, m_kv_norm_g, m_w_ukv, m_fox_out_g, m_mla_out_g, m_w_o, m_mlp_norm_g, m_w_up, m_w_down, m_final_norm_g, v_attn_norm_g, v_w_in, v_b_forget, v_q_norm_g, v_w_uq, v_kv_norm_g, v_w_ukv, v_fox_out_g, v_mla_out_g, v_w_o, v_mlp_norm_g, v_w_up, v_w_down, v_final_norm_g):
    core_id = lax.axis_index("c")
    core = core_id.reshape(1).astype(jnp.int32)
    chip = 2 * lax.axis_index("x") + lax.axis_index("y")
    big = [w_in, w_uq, w_ukv, w_o, w_up, w_down]
    big_m = [m_w_in, m_w_uq, m_w_ukv, m_w_o, m_w_up, m_w_down]
    big_v = [v_w_in, v_w_uq, v_w_ukv, v_w_o, v_w_up, v_w_down]
    n_early = 3

    def vec(a):
        return a.reshape(1, -1)

    small = [attn_norm_g, b_forget, q_norm_g, kv_norm_g, fox_out_g, mla_out_g, mlp_norm_g, final_norm_g]
    small_m = [m_attn_norm_g, m_b_forget, m_q_norm_g, m_kv_norm_g, m_fox_out_g, m_mla_out_g, m_mlp_norm_g, m_final_norm_g]
    small_v = [v_attn_norm_g, v_b_forget, v_q_norm_g, v_kv_norm_g, v_fox_out_g, v_mla_out_g, v_mlp_norm_g, v_final_norm_g]
    gains = [vec(a) for a in small]

    shards = [(w[0].T if n == 0 else w[0]).astype(BF16) for n, w in enumerate(big)]

    def with_own(gathered, mine):
        return [lax.dynamic_update_index_in_dim(g, s, chip, 0) for g, s in zip(gathered, mine)]

    early = with_own(run_side(gather_side(shards[:n_early]), name="gather_early"), shards[:n_early])
    w_early = [early[0].reshape(-1, early[0].shape[2])] + [join_cols(g) for g in early[1:]]

    def late_weights(gathered):
        g_o, g_up, g_down = with_own(gathered, shards[n_early:])
        return g_o.reshape(-1, g_o.shape[2]), join_cols(g_up), g_down.reshape(-1, g_down.shape[2])

    reduction = GradReduction(core_id, chip)
    loss, grad_x, _, _, d_small = local_step(
        x[0], positions[0], loss_target[0], gains, w_early, late_weights, gather_side(shards[n_early:]), reduction)
    halves = reduction.halves["early"] + reduction.halves["late"]
    others = reduction.others["early"] + reduction.others["late"]

    def rows8(vs):
        return jnp.concatenate([_pad_lanes(vec(a).astype(F32), 1024) for a in vs], axis=0)

    with_loss = [jnp.concatenate([d, loss], axis=1) if n == 1 else d for n, d in enumerate(d_small)]
    g_small8 = allreduce_small(rows8(with_loss))

    outs_big = []
    for n, (w, gm, go, m, v) in enumerate(zip(big, halves, others, big_m, big_v)):
        _, k, cols = w.shape
        if n == 0:
            outs = adamw_halves_t(w[0].T, gm, go, m[0].T, v[0].T, core, name="adamw_%d" % n)
            outs_big.append([o.T[None] for o in outs])
        else:
            outs_big.append(adamw_halves(w, gm, go, m, v, core, name="adamw_%d" % n))
    d8, m8, v8 = adamw(rows8(small), g_small8, rows8(small_m), rows8(small_v), name="adamw_small")

    def unrows8(a8):
        return [a8[n, :s.size].reshape(s.shape) for n, s in enumerate(small)]

    loss_all = g_small8[1, N_HEADS]
    grads, deltas, new_m, new_v = [None] * 14, [None] * 14, [None] * 14, [None] * 14
    big_at = [1, 4, 6, 9, 11, 12]
    small_at = [0, 2, 3, 5, 7, 8, 10, 13]
    for n, at in enumerate(big_at):
        grads[at], deltas[at], new_m[at], new_v[at] = outs_big[n]
    for at, g, dd, mm_, vv in zip(small_at, unrows8(g_small8), unrows8(d8), unrows8(m8), unrows8(v8)):
        grads[at], deltas[at], new_m[at], new_v[at] = g, dd, mm_, vv
    return (loss_all, grad_x[None], *grads, *deltas, *new_m, *new_v)
```

```python
import jax
import jax.numpy as jnp
from jax import lax
from jax.experimental import pallas as pl
from jax.experimental.pallas import tpu as pltpu

F32 = jnp.float32
BF16 = jnp.bfloat16
MESH = pl.DeviceIdType.MESH

EPS = 1e-6
ROPE_THETA = 10000.0
N_HEADS = 8
PAIRS = N_HEADS // 2
HEAD_DIM = 64
ROPE_DIM = 32
LANES = 128
Q_RANK = 384
KV_RANK = 256
N_CHIPS = 4
ADAM_LR, ADAM_B1, ADAM_B2, ADAM_EPS, ADAM_WD, ADAM_STEP = 0.001, 0.9, 0.999, 1e-08, 0.01, 10
VMEM_LIMIT = 48 * 1024 * 1024
LOG2E = 1.4426950408889634
LN2 = 0.6931471805599453
NN = (((1,), (0,)), ((), ()))
NT = (((1,), (1,)), ((), ()))
TN = (((0,), (0,)), ((), ()))


def _params(sem=None):
    return pltpu.CompilerParams(dimension_semantics=sem, vmem_limit_bytes=VMEM_LIMIT)


def _fit(block, dim):
    if dim <= block:
        return dim
    return next(b for b in range(block - block % LANES, 0, -LANES) if dim % b == 0)


def _row_block(rows):
    return next(b for b in (256, 128, 64, 32, 16, 8) if rows % b == 0)


def gridded(body, *, name, grid, in_specs, out_specs, out_shape, ins, semantics, side=None):
    if side is None:
        return pl.pallas_call(body, name=name, grid=grid, in_specs=in_specs, out_specs=out_specs, out_shape=out_shape,
                              compiler_params=_params(semantics))(*ins)
    n_in, n_out = len(in_specs), len(out_specs)
    steps = 1
    for extent in grid:
        steps *= extent

    def riding(*refs):
        main_in, s_in, main_out, s_out, sems = _split_refs(refs, [n_in, len(side.ins), n_out, len(side.out_shapes), 2])
        step = 0
        for axis, extent in enumerate(grid):
            step = step * extent + pl.program_id(axis)

        @pl.when(step == 0)
        def _():
            side.first(s_in, s_out, *sems)

        body(*main_in, *main_out)

        @pl.when(step == steps - 1)
        def _():
            side.last(s_in, s_out, *sems)

    s_in_specs, s_out_specs = side.specs()
    return pl.pallas_call(
        riding, name=name, grid=grid, in_specs=list(in_specs) + s_in_specs, out_specs=list(out_specs) + s_out_specs,
        out_shape=list(out_shape) + side.out_shapes, scratch_shapes=side.sems(),
        compiler_params=_params(("arbitrary",) * len(grid)))(*ins, *side.ins)


def rmsnorm(x, g, *, out_dtype, name, bt=512):
    t, d = x.shape
    bt = min(bt, t)

    def body(x_ref, g_ref, o_ref):
        xv = x_ref[...].astype(F32)
        r = lax.rsqrt(jnp.mean(xv * xv, axis=-1, keepdims=True) + EPS)
        o_ref[...] = (xv * r * g_ref[...]).astype(o_ref.dtype)

    return pl.pallas_call(
        body, name=name, grid=(t // bt,),
        in_specs=[pl.BlockSpec((bt, d), lambda i: (i, 0)), pl.BlockSpec((1, d), lambda i: (0, 0))],
        out_specs=pl.BlockSpec((bt, d), lambda i: (i, 0)),
        out_shape=jax.ShapeDtypeStruct((t, d), out_dtype),
        compiler_params=_params(("parallel",)),
    )(x, g)


def mm(a, b, *, trans_b=False, extras=(), vecs=(), epilogue=None, out_dtypes, n_sums=0, t_blk=None, name, bm=1024, bn=1024,
       side=None):
    a_list = list(a) if isinstance(a, (list, tuple)) else [a]
    b_list = list(b) if isinstance(b, (list, tuple)) else [b]
    m = a_list[0].shape[0]
    n = b_list[0].shape[0] if trans_b else b_list[0].shape[1]
    ks = [x.shape[1] for x in a_list]
    if sum(ks) > 2048:
        bm = bm // 2
    bm, bn = _fit(bm, m), _fit(bn, n)
    assert n_sums == 0 or bn == n
    n_ab, n_ex, n_vec, n_out = len(a_list), len(extras), len(vecs), len(out_dtypes)
    n_t = 0 if t_blk is None else 1

    def body(*refs):
        a_refs, b_refs, ex, vs, outs, t_outs, sums = _split_refs(refs, [n_ab, n_ab, n_ex, n_vec, n_out, n_t, n_sums])
        acc = None
        for a_ref, b_ref in zip(a_refs, b_refs):
            part = lax.dot_general(a_ref[...], b_ref[...], NT if trans_b else NN, preferred_element_type=F32)
            acc = part if acc is None else acc + part
        res = epilogue(acc, *[e[...] for e in ex], *[v[...] for v in vs]) if epilogue is not None else (acc,)
        for o, r in zip(outs, res[:n_out]):
            o[...] = r.astype(o.dtype)
        for t_ref in t_outs:
            for u in range(bm // t_blk):
                t_ref[u] = res[0][u * t_blk:(u + 1) * t_blk, :].T.astype(t_ref.dtype)
        if n_sums:
            @pl.when(pl.program_id(0) == 0)
            def _():
                for s_ref in sums:
                    s_ref[...] = jnp.zeros_like(s_ref)

            for s_ref, r in zip(sums, res[n_out:]):
                s_ref[...] += r

    tile = pl.BlockSpec((bm, bn), lambda i, j: (i, j))
    vec = pl.BlockSpec((1, bn), lambda i, j: (0, j))
    t_specs, t_shapes = [], []
    if t_blk is not None:
        t_specs = [pl.BlockSpec((bm // t_blk, bn, t_blk), lambda i, j: (i, j, 0))]
        t_shapes = [jax.ShapeDtypeStruct((m // t_blk, n, t_blk), out_dtypes[0])]
    a_specs = [pl.BlockSpec((bm, k), lambda i, j: (i, 0)) for k in ks]
    b_specs = [pl.BlockSpec((bn, k), lambda i, j: (j, 0)) if trans_b else pl.BlockSpec((k, bn), lambda i, j: (0, j)) for k in ks]
    return gridded(
        body, name=name, grid=(m // bm, n // bn),
        in_specs=a_specs + b_specs + [tile] * n_ex + [vec] * n_vec,
        out_specs=[tile] * n_out + t_specs + [vec] * n_sums,
        out_shape=[jax.ShapeDtypeStruct((m, n), dt) for dt in out_dtypes] + t_shapes + [jax.ShapeDtypeStruct((1, n), F32)] * n_sums,
        ins=[*a_list, *b_list, *extras, *vecs],
        semantics=("arbitrary", "arbitrary") if n_sums else ("parallel", "parallel"), side=side)


def mm_tn(a, b, *, name, col_shards=1, bk=1024, bn=1024, bt=1024):
    t, k = a.shape
    n = b.shape[1]
    ns = n // col_shards
    bk, bn, bt = _fit(bk, k), _fit(bn, ns), _fit(bt, t)
    per = ns // bn

    def body(a_ref, b_ref, o_ref):
        @pl.when(pl.program_id(2) == 0)
        def _():
            o_ref[...] = jnp.zeros_like(o_ref)

        o_ref[...] += lax.dot_general(a_ref[...], b_ref[...], TN, preferred_element_type=F32)

    if col_shards == 1:
        out_spec = pl.BlockSpec((bk, bn), lambda i, j, s: (i, j))
        out_shape = jax.ShapeDtypeStruct((k, n), F32)
    else:
        out_spec = pl.BlockSpec((None, bk, bn), lambda i, j, s: (j // per, i, j % per))
        out_shape = jax.ShapeDtypeStruct((col_shards, k, ns), F32)
    return pl.pallas_call(
        body, name=name, grid=(k // bk, n // bn, t // bt),
        in_specs=[pl.BlockSpec((bt, bk), lambda i, j, s: (s, i)), pl.BlockSpec((bt, bn), lambda i, j, s: (s, j))],
        out_specs=out_spec, out_shape=out_shape,
        compiler_params=_params(("parallel", "parallel", "arbitrary")),
    )(a, b)


def _split3(x):
    hi = x.astype(BF16)
    r1 = x - hi.astype(F32)
    mid = r1.astype(BF16)
    lo = (r1 - mid.astype(F32)).astype(BF16)
    return hi, mid, lo


def cumsum_rows(x, *, reverse, name, bc=512):
    t, d = x.shape
    bc = min(bc, t)
    nb = t // bc

    def body(x_ref, o_ref, carry):
        @pl.when(pl.program_id(0) == 0)
        def _():
            carry[...] = jnp.zeros_like(carry)

        r = lax.broadcasted_iota(jnp.int32, (bc, bc), 0)
        c = lax.broadcasted_iota(jnp.int32, (bc, bc), 1)
        tri = jnp.where((r <= c) if reverse else (r >= c), 1.0, 0.0).astype(BF16)
        hi, mid, lo = _split3(x_ref[...])
        s = (lax.dot_general(tri, hi, NN, preferred_element_type=F32)
             + lax.dot_general(tri, mid, NN, preferred_element_type=F32)
             + lax.dot_general(tri, lo, NN, preferred_element_type=F32)) + carry[0:1, :]
        o_ref[...] = s
        carry[0:1, :] = s[0:1, :] if reverse else s[bc - 1:bc, :]

    imap = (lambda i: (nb - 1 - i, 0)) if reverse else (lambda i: (i, 0))
    return pl.pallas_call(
        body, name=name, grid=(nb,),
        in_specs=[pl.BlockSpec((bc, d), imap)], out_specs=pl.BlockSpec((bc, d), imap),
        out_shape=jax.ShapeDtypeStruct((t, d), F32),
        scratch_shapes=[pltpu.VMEM((8, d), F32)],
        compiler_params=_params(("arbitrary",)),
    )(x)


def _rope(x, c, a, b):
    return x * c + pltpu.roll(x, LANES - ROPE_DIM // 2, 1) * a + pltpu.roll(x, ROPE_DIM // 2, 1) * b


def _rope_bwd(d, c, a, b):
    return d * c + pltpu.roll(d * a, ROPE_DIM // 2, 1) + pltpu.roll(d * b, LANES - ROPE_DIM // 2, 1)


S_CQ, S_CKV, S_KR, S_F, S_END = 0, Q_RANK, Q_RANK + KV_RANK, Q_RANK + KV_RANK + LANES, 1024
HW = N_HEADS * LANES
FW = N_HEADS * HEAD_DIM


def mla_prep(small, g_q, g_kv, w_uq, w_ukv, tab_c, tab_a, tab_b, b_f, *, name, bt=512):
    t = small.shape[0]
    bt = min(bt, t)

    def body(s_ref, gq_ref, gkv_ref, wq_ref, wkv_ref, c_ref, a_ref, b_ref, bf_ref,
             mq_ref, mk_ref, mv_ref, lf_ref, cqn_ref, ckvn_ref, mqt_ref, mvt_ref):
        cq = s_ref[:, S_CQ:S_CKV]
        rq = lax.rsqrt(jnp.mean(cq * cq, axis=-1, keepdims=True) + EPS)
        cqn = (cq * rq * gq_ref[...]).astype(BF16)
        ckv = s_ref[:, S_CKV:S_KR]
        rkv = lax.rsqrt(jnp.mean(ckv * ckv, axis=-1, keepdims=True) + EPS)
        ckvn = (ckv * rkv * gkv_ref[...]).astype(BF16)
        cqn_ref[...] = cqn
        ckvn_ref[...] = ckvn
        tc, ta, tb = c_ref[...], a_ref[...], b_ref[...]
        q = jnp.dot(cqn, wq_ref[...], preferred_element_type=F32)
        kv = jnp.dot(ckvn, wkv_ref[...], preferred_element_type=F32)
        kr = _rope(s_ref[:, S_KR:S_F], tc, ta, tb)
        for h in range(N_HEADS):
            sl = slice(h * LANES, (h + 1) * LANES)
            roped = _rope(q[:, sl], tc, ta, tb)
            mq_ref[:, sl] = roped.astype(BF16)
            mqt_ref[0, sl, :] = roped.T.astype(BF16)
            mk_ref[:, sl] = (kv[:, sl] + kr).astype(BF16)
        mv_ref[...] = kv[:, HW:].astype(BF16)
        mvt_ref[0] = kv[:, HW:].T.astype(BF16)
        z = s_ref[:, S_F:S_END - LANES] + bf_ref[...]
        lf_ref[...] = jnp.minimum(z, 0.0) - jnp.log(1.0 + jnp.exp(-jnp.abs(z)))

    def row(w):
        return pl.BlockSpec((bt, w), lambda i: (i, 0))

    def full(arr):
        return pl.BlockSpec(arr.shape, lambda i: (0, 0))

    return pl.pallas_call(
        body, name=name, grid=(t // bt,),
        in_specs=[row(S_END), full(g_q), full(g_kv), full(w_uq), full(w_ukv), row(LANES), row(LANES), row(LANES), full(b_f)],
        out_specs=[row(HW), row(HW), row(FW), row(LANES), row(Q_RANK), row(KV_RANK),
                   pl.BlockSpec((1, HW, bt), lambda i: (i, 0, 0)), pl.BlockSpec((1, FW, bt), lambda i: (i, 0, 0))],
        out_shape=[jax.ShapeDtypeStruct((t, HW), BF16)] * 2 + [jax.ShapeDtypeStruct((t, FW), BF16), jax.ShapeDtypeStruct((t, LANES), F32),
                   jax.ShapeDtypeStruct((t, Q_RANK), BF16), jax.ShapeDtypeStruct((t, KV_RANK), BF16),
                   jax.ShapeDtypeStruct((t // bt, HW, bt), BF16), jax.ShapeDtypeStruct((t // bt, FW, bt), BF16)],
        compiler_params=_params(("parallel",)),
    )(small, g_q, g_kv, w_uq, w_ukv, tab_c, tab_a, tab_b, b_f)


def mla_prep_bwd(dmq, dmk, dmv, dlf, small, g_q, g_kv, w_uq, w_ukv, tab_c, tab_a, tab_b, b_f, *, name, bt=512):
    t = small.shape[0]
    bt = min(bt, t)

    def body(dmq_ref, dmk_ref, dmv_ref, dlf_ref, s_ref, gq_ref, gkv_ref, wq_ref, wkv_ref, c_ref, a_ref, b_ref, bf_ref,
             ds_ref, dq_ref, dkv_ref, dgq_ref, dgkv_ref, db_ref):
        tc, ta, tb = c_ref[...], a_ref[...], b_ref[...]
        lane = lax.broadcasted_iota(jnp.int32, (1, LANES), 1)
        dkr = jnp.zeros((bt, LANES), F32)
        for h in range(N_HEADS):
            sl = slice(h * LANES, (h + 1) * LANES)
            dq_ref[:, sl] = _rope_bwd(dmq_ref[:, sl], tc, ta, tb).astype(BF16)
            dkr = dkr + dmk_ref[:, sl]
        dkv_ref[:, :HW] = dmk_ref[...].astype(BF16)
        dkv_ref[:, HW:] = dmv_ref[...].astype(BF16)
        in_rope = (lane >= HEAD_DIM) & (lane < HEAD_DIM + ROPE_DIM)
        ds_ref[:, S_KR:S_F] = jnp.where(in_rope, _rope_bwd(dkr, tc, ta, tb), 0.0).astype(BF16)

        def norm_bwd(raw, g_ref, dn, dg_ref):
            r = lax.rsqrt(jnp.mean(raw * raw, axis=-1, keepdims=True) + EPS)
            u = dn * g_ref[...]
            dot = jnp.mean(u * raw, axis=-1, keepdims=True)
            dg_ref[...] += jnp.sum(dn * (raw * r), axis=0, keepdims=True)
            return r * u - raw * (r * r * r * dot)

        @pl.when(pl.program_id(0) == 0)
        def _():
            dgq_ref[...] = jnp.zeros_like(dgq_ref)
            dgkv_ref[...] = jnp.zeros_like(dgkv_ref)
            db_ref[...] = jnp.zeros_like(db_ref)

        dcqn = lax.dot_general(dq_ref[...], wq_ref[...], NT, preferred_element_type=F32)
        ds_ref[:, S_CQ:S_CKV] = norm_bwd(s_ref[:, S_CQ:S_CKV], gq_ref, dcqn, dgq_ref).astype(BF16)
        dckvn = lax.dot_general(dkv_ref[...], wkv_ref[...], NT, preferred_element_type=F32)
        ds_ref[:, S_CKV:S_KR] = norm_bwd(s_ref[:, S_CKV:S_KR], gkv_ref, dckvn, dgkv_ref).astype(BF16)
        z = s_ref[:, S_F:S_END - LANES] + bf_ref[...]
        dz = jnp.where(lane < N_HEADS, dlf_ref[...] / (1.0 + jnp.exp(z)), 0.0)
        db_ref[...] += jnp.sum(dz, axis=0, keepdims=True)
        ds_ref[:, S_F:S_END - LANES] = dz.astype(BF16)
        ds_ref[:, S_END - LANES:] = jnp.zeros((bt, LANES), BF16)

    def row(w):
        return pl.BlockSpec((bt, w), lambda i: (i, 0))

    def full(arr):
        return pl.BlockSpec(arr.shape, lambda i: (0, 0))

    def vec(w):
        return pl.BlockSpec((1, w), lambda i: (0, 0))

    return pl.pallas_call(
        body, name=name, grid=(t // bt,),
        in_specs=[row(HW), row(HW), row(FW), row(LANES), row(S_END), full(g_q), full(g_kv), full(w_uq), full(w_ukv),
                  row(LANES), row(LANES), row(LANES), full(b_f)],
        out_specs=[row(S_END), row(HW), row(HW + FW), vec(Q_RANK), vec(KV_RANK), vec(LANES)],
        out_shape=[jax.ShapeDtypeStruct((t, S_END), BF16), jax.ShapeDtypeStruct((t, HW), BF16),
                   jax.ShapeDtypeStruct((t, HW + FW), BF16), jax.ShapeDtypeStruct((1, Q_RANK), F32),
                   jax.ShapeDtypeStruct((1, KV_RANK), F32), jax.ShapeDtypeStruct((1, LANES), F32)],
        compiler_params=_params(("arbitrary",)),
    )(dmq, dmk, dmv, dlf, small, g_q, g_kv, w_uq, w_ukv, tab_c, tab_a, tab_b, b_f)


class Side:
    def __init__(self, ins, out_shapes, n_sems, first, last, mid=None):
        self.ins, self.out_shapes, self.n_sems = list(ins), list(out_shapes), n_sems
        self.first, self.mid, self.last = first, mid, last

    def specs(self):
        return [ANY] * len(self.ins), [ANY] * len(self.out_shapes)

    def sems(self):
        return [pltpu.SemaphoreType.DMA((self.n_sems,)), pltpu.SemaphoreType.DMA((self.n_sems,))]


def _lane():
    return lax.broadcasted_iota(jnp.int32, (1, LANES), 1)


def _halves(x):
    zero = jnp.zeros_like(x)
    return [jnp.where(_lane() < HEAD_DIM, x, zero), jnp.where(_lane() >= HEAD_DIM, x, zero)]


def _groups(x):
    return [x[:, :LANES], x[:, LANES:]]


def _pick_row(tile, h):
    row = lax.broadcasted_iota(jnp.int32, (tile.shape[0], 1), 0)
    return jnp.sum(jnp.where(row == h, tile, 0.0), axis=0, keepdims=True)


def _pick_lane(tile, h):
    return jnp.sum(jnp.where(_lane() == h, tile, 0.0), axis=1, keepdims=True)


def _row_halves(x):
    row = lax.broadcasted_iota(jnp.int32, (LANES, 1), 0)
    zero = jnp.zeros_like(x)
    return [jnp.where(row < HEAD_DIM, x, zero), jnp.where(row >= HEAD_DIM, x, zero)]


def _below_diagonal(s, lead=0):
    r = lax.broadcasted_iota(jnp.int32, s.shape, 0)
    c = lax.broadcasted_iota(jnp.int32, s.shape, 1)
    return jnp.where(c <= r + lead, s, -jnp.inf)


def _above_diagonal(s):
    r = lax.broadcasted_iota(jnp.int32, s.shape, 0)
    c = lax.broadcasted_iota(jnp.int32, s.shape, 1)
    return jnp.where(r <= c, s, -jnp.inf)


def _split_refs(refs, counts):
    out, at = [], 0
    for n in counts:
        out.append(refs[at:at + n])
        at += n
    return out


def flash_fwd(qt_arr, k_arr, vt_arr, f_cum, *, qoff, koff, voff, pair, scale, name, blk=512, side=None):
    t = k_arr.shape[0]
    tblk = qt_arr.shape[2]
    blk = max(min(blk, t), tblk)
    sub = blk // tblk
    nb = t // blk
    w = LANES if pair else 2 * LANES
    has_bias = f_cum is not None
    ins = [qt_arr, k_arr, vt_arr] + ([f_cum] if has_bias else [])

    def wide(ref, first):
        parts = [ref[first + u] for u in range(sub)]
        return parts[0] if sub == 1 else jnp.concatenate(parts, axis=1)
    s_ins, s_outs = (side.ins, side.out_shapes) if side else ([], [])

    def body(*refs):
        main, si, outs, so, sems = _split_refs(refs, [len(ins), len(s_ins), 2, len(s_outs), 2 if side else 0])
        qt_ref, k_ref, vt_ref = main[:3]
        f_ref = main[3] if has_bias else None
        o_ref, st_ref = outs
        g, i = pl.program_id(0), pl.program_id(1)
        step_id = g * nb + i
        if side:
            @pl.when(step_id == 0)
            def _():
                side.first(si, so, *sems)

            if side.mid is not None:
                @pl.when(step_id == (3 * PAIRS * nb) // 4)
                def _():
                    side.mid(si, so, *sems)

        qt = (wide(qt_ref, 0).astype(F32) * (scale * LOG2E)).astype(BF16)
        qts = _row_halves(qt) if pair else [qt[:LANES], qt[LANES:]]

        def step(j, carry, diagonal):
            rows = pl.ds(pl.multiple_of(j * blk, blk), blk)
            kk = k_ref[rows, :]
            ks = [kk, kk] if pair else _groups(kk)
            vt = wide(vt_ref, sub * j)
            out = []
            for n in range(2):
                m, l, acc = carry[n]
                s = jnp.dot(ks[n], qts[n], preferred_element_type=F32)
                if has_bias:
                    s = s - LOG2E * _pick_lane(f_ref[rows, :], 2 * g + n)
                if diagonal:
                    s = _above_diagonal(s)
                m_new = jnp.maximum(m, jnp.max(s, axis=0, keepdims=True))
                alpha = jnp.exp2(m - m_new)
                p = jnp.exp2(s - m_new)
                out.append((m_new, alpha * l + jnp.sum(p, axis=0, keepdims=True),
                            alpha * acc + jnp.dot(vt[n * HEAD_DIM:(n + 1) * HEAD_DIM], p.astype(BF16),
                                                  preferred_element_type=F32)))
            return tuple(out)

        init = tuple((jnp.full((1, blk), -jnp.inf, F32), jnp.zeros((1, blk), F32), jnp.zeros((HEAD_DIM, blk), F32))
                     for _ in range(2))
        carry = lax.fori_loop(0, i, lambda j, c: step(j, c, False), init)
        (ma, la, acca), (mb, lb, accb) = step(i, carry, True)
        o_ref[...] = jnp.concatenate([acca / la, accb / lb], axis=0).T
        row = lax.broadcasted_iota(jnp.int32, (LANES, 1), 0)
        st_ref[0] = jnp.where(row == 0, ma + jnp.log2(la), jnp.where(row == 1, mb + jnp.log2(lb), 0.0)).T
        if side:
            @pl.when(step_id == PAIRS * nb - 1)
            def _():
                side.last(si, so, *sems)

    in_specs = [pl.BlockSpec((sub, w, tblk), lambda g, i: (i, qoff + g, 0)), pl.BlockSpec((t, w), lambda g, i: (0, koff + g)),
                pl.BlockSpec((t // tblk, LANES, tblk), lambda g, i: (0, voff + g, 0))]
    if has_bias:
        in_specs.append(pl.BlockSpec((t, LANES), lambda g, i: (0, 0)))
    s_in_specs, s_out_specs = side.specs() if side else ([], [])
    return pl.pallas_call(
        body, name=name, grid=(PAIRS, nb), in_specs=in_specs + s_in_specs,
        out_specs=[pl.BlockSpec((blk, LANES), lambda g, i: (i, g)), pl.BlockSpec((1, blk, LANES), lambda g, i: (g, i, 0))]
        + s_out_specs,
        out_shape=[jax.ShapeDtypeStruct((t, PAIRS * LANES), F32), jax.ShapeDtypeStruct((PAIRS, t, LANES), F32)] + list(s_outs),
        scratch_shapes=side.sems() if side else [],
        compiler_params=_params(("arbitrary", "arbitrary")),
    )(*ins, *s_ins)


def mix_norm(fo, mo, g_fo, g_mo, *, name, bt=512):
    t, d = fo.shape
    bt = min(bt, t)

    def body(fo_ref, mo_ref, gf_ref, gm_ref, o_ref):
        for n, (x_ref, g_ref) in enumerate(((fo_ref, gf_ref), (mo_ref, gm_ref))):
            xv = x_ref[...]
            r = lax.rsqrt(jnp.mean(xv * xv, axis=-1, keepdims=True) + EPS)
            o_ref[:, n * d:(n + 1) * d] = (xv * r * g_ref[...]).astype(BF16)

    row = pl.BlockSpec((bt, d), lambda i: (i, 0))
    vec = pl.BlockSpec((1, d), lambda i: (0, 0))
    return pl.pallas_call(
        body, name=name, grid=(t // bt,), in_specs=[row, row, vec, vec],
        out_specs=pl.BlockSpec((bt, 2 * d), lambda i: (i, 0)),
        out_shape=jax.ShapeDtypeStruct((t, 2 * d), BF16),
        compiler_params=_params(("parallel",)),
    )(fo, mo, g_fo, g_mo)


def mix_norm_bwd(dmixed, fo, mo, g_fo, g_mo, st_f, st_m, *, name, bt=512, side=None):
    t, d = fo.shape
    bt = min(bt, t)

    def body(dm_ref, fo_ref, mo_ref, gf_ref, gm_ref, sf_ref, sm_ref, dfo_ref, dmo_ref, dgf_ref, dgm_ref, sfo_ref, smo_ref,
             dfot_ref, dmot_ref):
        @pl.when(pl.program_id(0) == 0)
        def _():
            dgf_ref[...] = jnp.zeros_like(dgf_ref)
            dgm_ref[...] = jnp.zeros_like(dgm_ref)

        groups = ((fo_ref, gf_ref, dfo_ref, dgf_ref, sf_ref, sfo_ref, dfot_ref),
                  (mo_ref, gm_ref, dmo_ref, dgm_ref, sm_ref, smo_ref, dmot_ref))
        for n, (x_ref, g_ref, dx_ref, dg_ref, st_ref, sto_ref, dxt_ref) in enumerate(groups):
            xv = x_ref[...]
            dhv = dm_ref[:, n * d:(n + 1) * d]
            r = lax.rsqrt(jnp.mean(xv * xv, axis=-1, keepdims=True) + EPS)
            u = dhv * g_ref[...]
            dxf = r * u - xv * (r * r * r * jnp.mean(u * xv, axis=-1, keepdims=True))
            dxb = dxf.astype(BF16)
            dx_ref[...] = dxb
            dxt_ref[0] = dxf.T.astype(BF16)
            dg_ref[...] += jnp.sum(dhv * (xv * r), axis=0, keepdims=True)
            prod = xv * dxb.astype(F32)
            for g in range(PAIRS):
                grp = prod[:, g * LANES:(g + 1) * LANES]
                da = jnp.sum(jnp.where(_lane() < HEAD_DIM, grp, 0.0), axis=1, keepdims=True)
                db = jnp.sum(jnp.where(_lane() >= HEAD_DIM, grp, 0.0), axis=1, keepdims=True)
                sto_ref[g] = jnp.where(_lane() == 2, da, jnp.where(_lane() == 3, db, st_ref[g]))

    row = pl.BlockSpec((bt, d), lambda i: (i, 0))
    vec = pl.BlockSpec((1, d), lambda i: (0, 0))
    stat = pl.BlockSpec((PAIRS, bt, LANES), lambda i: (0, i, 0))
    return gridded(
        body, name=name, grid=(t // bt,),
        in_specs=[pl.BlockSpec((bt, 2 * d), lambda i: (i, 0)), row, row, vec, vec, stat, stat],
        out_specs=[row, row, vec, vec, stat, stat] + [pl.BlockSpec((1, d, bt), lambda i: (i, 0, 0))] * 2,
        out_shape=[jax.ShapeDtypeStruct((t, d), BF16)] * 2 + [jax.ShapeDtypeStruct((1, d), F32)] * 2
        + [jax.ShapeDtypeStruct(st_f.shape, F32)] * 2 + [jax.ShapeDtypeStruct((t // bt, d, bt), BF16)] * 2,
        ins=[dmixed, fo, mo, g_fo, g_mo, st_f, st_m], semantics=("arbitrary",), side=side)


def flash_bwd(q_arr, qt_arr, k_arr, v_arr, do_arr, dot_arr, st, f_blocks, *, qoff, koff, voff, pair, scale, name, qblk=1024,
              side=None):
    t = q_arr.shape[0]
    blk = qt_arr.shape[2]
    qblk = max(min(qblk, t), blk)
    sub = qblk // blk
    nb, nbq = t // blk, t // qblk
    w = LANES if pair else 2 * LANES
    hw = w // 2
    has_bias = f_blocks is not None
    split = _halves if pair else _groups
    ins = [q_arr, qt_arr, k_arr, v_arr, do_arr, dot_arr, st] + ([f_blocks] if has_bias else [])
    n_out = 4 if has_bias else 3
    s_ins, s_outs = (side.ins, side.out_shapes) if side else ([], [])

    def wide(ref, first):
        parts = [ref[first + u] for u in range(sub)]
        return parts[0] if sub == 1 else jnp.concatenate(parts, axis=1)

    def body(*refs):
        main, si, outs, so, sems = _split_refs(refs, [len(ins), len(s_ins), n_out, len(s_outs), 2 if side else 0])
        q_ref, qt_ref, k_ref, v_ref, do_ref, dot_ref, st_ref = main[:7]
        dq_ref, dk_ref, dv_ref = outs[:3]
        g, j = pl.program_id(0), pl.program_id(1)
        step_id = g * nb + j
        if side:
            @pl.when(step_id == 0)
            def _():
                side.first(si, so, *sems)

        @pl.when(j == 0)
        def _():
            dq_ref[...] = jnp.zeros_like(dq_ref)

        kk, vv = k_ref[...], v_ref[...]
        ks = [kk, kk] if pair else _groups(kk)
        if has_bias:
            f_ref, df_ref = main[7], outs[3]
            fk = [LOG2E * _pick_row(f_ref[0], 2 * g + n) for n in range(2)]

            @pl.when(step_id == 0)
            def _():
                df_ref[...] = jnp.zeros_like(df_ref)

        def step(i, carry, diagonal):
            rows = pl.ds(pl.multiple_of(i * qblk, qblk), qblk)
            qs = split((q_ref[rows, :].astype(F32) * (scale * LOG2E)).astype(BF16))
            qt = (wide(qt_ref, sub * i).astype(F32) * (scale * LOG2E)).astype(BF16)
            dos = [h.astype(BF16) for h in _halves(do_ref[rows, :].astype(F32))]
            dot = wide(dot_ref, sub * i)
            stats = st_ref[0, rows, :]
            new, dqs, row_sums = [], [], []
            for n in range(2):
                dkt, dvt, dfk = carry[n]
                s = lax.dot_general(qs[n], ks[n], NT, preferred_element_type=F32)
                if has_bias:
                    s = s - fk[n]
                if diagonal:
                    s = _below_diagonal(s, i * qblk - j * blk)
                p = jnp.exp2(s - stats[:, n:n + 1])
                dp = lax.dot_general(dos[n], vv, NT, preferred_element_type=F32)
                ds = p * (dp - stats[:, 2 + n:3 + n])
                dsb = ds.astype(BF16)
                dvt = dvt + jnp.dot(dot[n * HEAD_DIM:(n + 1) * HEAD_DIM], p.astype(BF16), preferred_element_type=F32)
                dkt = dkt + jnp.dot(qt[n * hw:(n + 1) * hw], dsb, preferred_element_type=F32)
                dqs.append(jnp.dot(dsb, ks[n], preferred_element_type=F32))
                if has_bias:
                    dfk = dfk - jnp.sum(ds, axis=0, keepdims=True)
                    row_sums.append(jnp.sum(ds, axis=1, keepdims=True))
                new.append((dkt, dvt, dfk))
            dq = (jnp.where(_lane() < HEAD_DIM, dqs[0], dqs[1]) if pair else jnp.concatenate(dqs, axis=1)) * scale
            if has_bias:
                df_ref[rows, :] += jnp.where(_lane() == 2 * g, row_sums[0], jnp.where(_lane() == 2 * g + 1, row_sums[1], 0.0))
            dq_ref[rows, :] += dq
            return tuple(new)

        init = tuple((jnp.zeros((hw, blk), F32), jnp.zeros((HEAD_DIM, blk), F32), jnp.zeros((1, blk), F32)) for _ in range(2))
        first = j // sub
        carry = step(first, init, True)
        (dka, dva, dfa), (dkb, dvb, dfb) = lax.fori_loop(first + 1, nbq, lambda i, c: step(i, c, False), carry)
        dk_ref[...] = jnp.concatenate([dka, dkb], axis=0).T * LN2
        dv_ref[...] = jnp.concatenate([dva, dvb], axis=0).T
        if has_bias:
            row = lax.broadcasted_iota(jnp.int32, (LANES, 1), 0)
            by_head = jnp.where(row == 2 * g, dfa, jnp.where(row == 2 * g + 1, dfb, 0.0))
            df_ref[pl.ds(pl.multiple_of(j * blk, blk), blk), :] += by_head.T
        if side:
            @pl.when(step_id == PAIRS * nb - 1)
            def _():
                side.last(si, so, *sems)

    in_specs = [pl.BlockSpec((t, w), lambda g, j: (0, qoff + g)), pl.BlockSpec((nb, w, blk), lambda g, j: (0, qoff + g, 0)),
                pl.BlockSpec((blk, w), lambda g, j: (j, koff + g)), pl.BlockSpec((blk, LANES), lambda g, j: (j, voff + g)),
                pl.BlockSpec((t, LANES), lambda g, j: (0, g)), pl.BlockSpec((nb, LANES, blk), lambda g, j: (0, g, 0)),
                pl.BlockSpec((1, t, LANES), lambda g, j: (g, 0, 0))]
    out_specs = [pl.BlockSpec((t, w), lambda g, j: (0, g)), pl.BlockSpec((blk, w), lambda g, j: (j, g)),
                 pl.BlockSpec((blk, LANES), lambda g, j: (j, g))]
    out_shape = [jax.ShapeDtypeStruct((t, PAIRS * w), F32)] * 2 + [jax.ShapeDtypeStruct((t, PAIRS * LANES), F32)]
    if has_bias:
        in_specs.append(pl.BlockSpec((1, N_HEADS, blk), lambda g, j: (j, 0, 0)))
        out_specs.append(pl.BlockSpec((t, LANES), lambda g, j: (0, 0)))
        out_shape.append(jax.ShapeDtypeStruct((t, LANES), F32))
    s_in_specs, s_out_specs = side.specs() if side else ([], [])
    return pl.pallas_call(
        body, name=name, grid=(PAIRS, nb), in_specs=in_specs + s_in_specs, out_specs=out_specs + s_out_specs,
        out_shape=out_shape + list(s_outs), scratch_shapes=side.sems() if side else [],
        compiler_params=_params(("arbitrary", "arbitrary")),
    )(*ins, *s_ins)


def _adamw_math(w, g, m, v):
    nm = ADAM_B1 * m + (1.0 - ADAM_B1) * g
    nv = ADAM_B2 * v + (1.0 - ADAM_B2) * (g * g)
    m_hat = nm / (1.0 - ADAM_B1 ** ADAM_STEP)
    v_hat = nv / (1.0 - ADAM_B2 ** ADAM_STEP)
    return -ADAM_LR * (m_hat / (jnp.sqrt(v_hat) + ADAM_EPS) + ADAM_WD * w), nm, nv


def adamw(w, g, m, v, *, name):
    rws, cols = w.shape
    br = _row_block(rws)

    def body(w_ref, g_ref, m_ref, v_ref, d_ref, nm_ref, nv_ref):
        d_ref[...], nm_ref[...], nv_ref[...] = _adamw_math(w_ref[...], g_ref[...], m_ref[...], v_ref[...])

    blk = pl.BlockSpec((br, cols), lambda i: (i, 0))
    return pl.pallas_call(
        body, name=name, grid=(rws // br,), in_specs=[blk] * 4, out_specs=[blk] * 3,
        out_shape=[jax.ShapeDtypeStruct((rws, cols), F32)] * 3,
        compiler_params=_params(("parallel",)),
    )(w, g, m, v)


def adamw_halves(w, g_mine, g_other, m, v, core, *, name):
    _, k, n = w.shape
    br = _row_block(k // 2)
    nh = k // 2 // br

    def body(c_ref, w_ref, gm_ref, go_ref, m_ref, v_ref, g_out, d_ref, nm_ref, nv_ref):
        gv = jnp.where(pl.program_id(0) == c_ref[0], gm_ref[...], go_ref[...])
        g_out[0] = gv
        d_ref[0], nm_ref[0], nv_ref[0] = _adamw_math(w_ref[0], gv, m_ref[0], v_ref[0])

    full = pl.BlockSpec((1, br, n), lambda hb, i, c: (0, hb * nh + i, 0))
    half = pl.BlockSpec((br, n), lambda hb, i, c: (i, 0))
    return pl.pallas_call(
        body, name=name,
        grid_spec=pltpu.PrefetchScalarGridSpec(num_scalar_prefetch=1, grid=(2, nh), in_specs=[full, half, half, full, full],
                                               out_specs=[full] * 4),
        out_shape=[jax.ShapeDtypeStruct(w.shape, F32)] * 4,
        compiler_params=_params(("parallel", "parallel")),
    )(core, w, g_mine, g_other, m, v)


def adamw_halves_t(wt, gt_mine, gt_other, mt, vt, core, *, name, bc=128):
    n, k = wt.shape
    nh = k // 2 // bc

    def body(c_ref, w_ref, gm_ref, go_ref, m_ref, v_ref, g_out, d_ref, nm_ref, nv_ref):
        gv = jnp.where(pl.program_id(0) == c_ref[0], gm_ref[...], go_ref[...])
        g_out[...] = gv
        d_ref[...], nm_ref[...], nv_ref[...] = _adamw_math(w_ref[...], gv, m_ref[...], v_ref[...])

    full = pl.BlockSpec((n, bc), lambda hb, i, c: (0, hb * nh + i))
    half = pl.BlockSpec((n, bc), lambda hb, i, c: (0, i))
    return pl.pallas_call(
        body, name=name,
        grid_spec=pltpu.PrefetchScalarGridSpec(num_scalar_prefetch=1, grid=(2, nh), in_specs=[full, half, half, full, full],
                                               out_specs=[full] * 4),
        out_shape=[jax.ShapeDtypeStruct(wt.shape, F32)] * 4,
        compiler_params=_params(("parallel", "parallel")),
    )(core, wt, gt_mine, gt_other, mt, vt)


def add_pair(dw, recv, core, *, name):
    n4, k, n = dw.shape
    half = (1, k // 2, n) if split_axis(k) == 0 else (1, k, n // 2)
    mine = (lambda q, c: (q, c[0], 0)) if split_axis(k) == 0 else (lambda q, c: (q, 0, c[0]))

    def body(c_ref, a_ref, b_ref, o_ref):
        o_ref[...] = (a_ref[...] + b_ref[...].astype(F32)).astype(BF16)

    return pl.pallas_call(
        body, name=name,
        grid_spec=pltpu.PrefetchScalarGridSpec(
            num_scalar_prefetch=1, grid=(n4,),
            in_specs=[pl.BlockSpec(half, mine), pl.BlockSpec(half, lambda q, c: (q, 0, 0))],
            out_specs=pl.BlockSpec(half, lambda q, c: (q, 0, 0))),
        out_shape=jax.ShapeDtypeStruct((n4,) + half[1:], BF16),
        compiler_params=_params(("parallel",)),
    )(core, dw, recv)


def sum_chips(parts, *, name):
    n4, r, n = parts.shape
    if r % 16 == 0:
        br, bc = _row_block(r), n
    else:
        br, bc = r, LANES

    def body(p_ref, o_ref):
        acc = p_ref[0].astype(F32)
        for q in range(1, n4):
            acc = acc + p_ref[q].astype(F32)
        o_ref[...] = acc

    return pl.pallas_call(
        body, name=name, grid=(r // br, n // bc),
        in_specs=[pl.BlockSpec((n4, br, bc), lambda i, j: (0, i, j))], out_specs=pl.BlockSpec((br, bc), lambda i, j: (i, j)),
        out_shape=jax.ShapeDtypeStruct((r, n), F32),
        compiler_params=_params(("parallel", "parallel")),
    )(parts)


ANY = pl.BlockSpec(memory_space=pl.ANY)


def _place():
    x, y, c = lax.axis_index("x"), lax.axis_index("y"), lax.axis_index("c")
    chips = [(1 - x, y), (x, 1 - y), (1 - x, 1 - y)]
    return x, y, c, chips


def _copy(src, dst, send_sems, recv_sems, k, to):
    return pltpu.make_async_remote_copy(src_ref=src, dst_ref=dst, send_sem=send_sems.at[k], recv_sem=recv_sems.at[k],
                                        device_id=to, device_id_type=MESH)


def split_axis(rows):
    return 0 if rows % 32 == 0 else 1


def _half(ref, lead, hf):
    rows, cols = ref.shape[-2:]
    if split_axis(rows) == 0:
        at = (pl.ds(hf * (rows // 2), rows // 2), slice(None))
    else:
        at = (slice(None), pl.ds(hf * (cols // 2), cols // 2))
    return ref.at[at] if lead is None else ref.at[(lead,) + at]


def _gather_first(srcs, dsts, ssems, rsems):
    x, y, c, chips = _place()
    for ti, (s, d) in enumerate(zip(srcs, dsts)):
        for j, (cx, cy) in enumerate(chips):
            _copy(_half(s, None, c), _half(d, 2 * x + y, c), ssems, rsems, 3 * ti + j, (cx, cy, c)).start()


def _gather_mid(srcs, dsts, ssems, rsems):
    x, y, c, chips = _place()
    n1 = 3 * len(srcs)
    for ti, d in enumerate(dsts):
        for j, (cx, cy) in enumerate(chips):
            landed = _half(d, 2 * cx + cy, c)
            _copy(landed, landed, ssems, rsems, 3 * ti + j, (cx, cy, c)).wait_recv()
            _copy(landed, landed, ssems, rsems, n1 + 3 * ti + j, (x, y, 1 - c)).start()


def _gather_last(srcs, dsts, ssems, rsems):
    x, y, c, chips = _place()
    n1 = 3 * len(srcs)
    for ti, (s, d) in enumerate(zip(srcs, dsts)):
        for j, (cx, cy) in enumerate(chips):
            other = _half(d, 2 * cx + cy, 1 - c)
            _copy(other, other, ssems, rsems, n1 + 3 * ti + j, (x, y, 1 - c)).wait_recv()
        for j, (cx, cy) in enumerate(chips):
            mine = _half(s, None, c)
            _copy(mine, mine, ssems, rsems, 3 * ti + j, (cx, cy, c)).wait_send()
            _copy(mine, mine, ssems, rsems, n1 + 3 * ti + j, (x, y, 1 - c)).wait_send()


def gather_side(shards):
    return Side(shards, [jax.ShapeDtypeStruct((N_CHIPS,) + s.shape, s.dtype) for s in shards], 6 * len(shards),
                _gather_first, _gather_last, _gather_mid)


def _scatter_first(srcs, dsts, ssems, rsems):
    x, y, c, chips = _place()
    for ti, (s, d) in enumerate(zip(srcs, dsts)):
        for j, (cx, cy) in enumerate(chips):
            _copy(s.at[2 * cx + cy], d.at[2 * x + y], ssems, rsems, 3 * ti + j, (cx, cy, c)).start()


def _scatter_last(srcs, dsts, ssems, rsems):
    x, y, c, chips = _place()
    for ti, (s, d) in enumerate(zip(srcs, dsts)):
        for j, (cx, cy) in enumerate(chips):
            _copy(s.at[2 * cx + cy], d.at[2 * cx + cy], ssems, rsems, 3 * ti + j, (cx, cy, c)).wait_recv()
        for j, (cx, cy) in enumerate(chips):
            _copy(s.at[2 * cx + cy], d.at[2 * cx + cy], ssems, rsems, 3 * ti + j, (cx, cy, c)).wait_send()


def scatter_side(parts):
    return Side(parts, [jax.ShapeDtypeStruct(p.shape, p.dtype) for p in parts], 3 * len(parts), _scatter_first, _scatter_last)


def run_side(side, *, name):
    n_in, n_out = len(side.ins), len(side.out_shapes)

    def body(*refs):
        si, so, sems = _split_refs(refs, [n_in, n_out, 2])
        side.first(si, so, *sems)
        if side.mid is not None:
            side.mid(si, so, *sems)
        side.last(si, so, *sems)

    in_specs, out_specs = side.specs()
    return pl.pallas_call(body, name=name, in_specs=in_specs, out_specs=out_specs, out_shape=side.out_shapes,
                          scratch_shapes=side.sems())(*side.ins)


def _swap_first(srcs, dsts, ssems, rsems):
    x, y, c, _ = _place()
    for k, (s, d) in enumerate(zip(srcs, dsts)):
        _copy(s, d, ssems, rsems, k, (x, y, 1 - c)).start()


def _swap_last(srcs, dsts, ssems, rsems):
    x, y, c, _ = _place()
    for k, (s, d) in enumerate(zip(srcs, dsts)):
        _copy(s, d, ssems, rsems, k, (x, y, 1 - c)).wait()


def swap_side(xs):
    return Side(xs, [jax.ShapeDtypeStruct(a.shape, a.dtype) for a in xs], len(xs), _swap_first, _swap_last)


def allreduce_small(s):
    n_dev = 8

    def body(s_ref, out_ref, buf, send_sems, recv_sems):
        x, y, c, _ = _place()
        me = 4 * x + 2 * y + c
        buf[me] = s_ref[...]
        sends = []
        for k in range(1, n_dev):
            px = 1 - x if k & 4 else x
            py = 1 - y if k & 2 else y
            pc = 1 - c if k & 1 else c
            cp = _copy(s_ref, buf.at[me], send_sems, recv_sems, k - 1, (px, py, pc))
            cp.start()
            sends.append((cp, 4 * px + 2 * py + pc))
        for k, (cp, peer) in enumerate(sends):
            _copy(s_ref, buf.at[peer], send_sems, recv_sems, k, (x, y, c)).wait_recv()
        for cp, _ in sends:
            cp.wait_send()
        acc = buf[0]
        for d in range(1, n_dev):
            acc = acc + buf[d]
        out_ref[...] = acc

    vm = pl.BlockSpec(memory_space=pltpu.VMEM)
    return pl.pallas_call(
        body, name="allreduce_small", in_specs=[vm], out_specs=vm,
        out_shape=jax.ShapeDtypeStruct(s.shape, F32),
        scratch_shapes=[pltpu.VMEM((n_dev,) + s.shape, F32), pltpu.SemaphoreType.DMA((n_dev - 1,)),
                        pltpu.SemaphoreType.DMA((n_dev - 1,))],
    )(s)


def join_cols(sm):
    n4, k, n = sm.shape
    return sm.transpose(1, 0, 2).reshape(k, n4 * n)


def split_cols(full):
    k, n = full.shape
    return full.reshape(k, N_CHIPS, n // N_CHIPS).transpose(1, 0, 2)


def _pad_heads(w, width):
    lead = w.shape[:-1]
    w = w.reshape(lead + (N_HEADS, width))
    return jnp.pad(w, [(0, 0)] * len(lead) + [(0, 0), (0, LANES - width)]).reshape(lead + (HW,))


def _unpad_heads(w, width):
    lead = w.shape[:-1]
    return w.reshape(lead + (N_HEADS, LANES))[..., :width].reshape(lead + (N_HEADS * width,))


def split_w_in_t(w_in_t):
    d = w_in_t.shape[1]
    o_f = 3 * FW
    o_cq = o_f + N_HEADS
    o_ckv = o_cq + Q_RANK
    o_kr = o_ckv + KV_RANK

    def z(n):
        return jnp.zeros((n, d), w_in_t.dtype)

    small = jnp.concatenate([w_in_t[o_cq:o_ckv], w_in_t[o_ckv:o_kr], z(HEAD_DIM), w_in_t[o_kr:], z(LANES - HEAD_DIM - ROPE_DIM),
                             w_in_t[o_f:o_cq], z(LANES - N_HEADS), z(LANES)], axis=0)
    return w_in_t[:o_f], small


def join_w_in_t(d_qkv_t, d_small_t):
    kr = S_KR + HEAD_DIM
    return jnp.concatenate([d_qkv_t, d_small_t[S_F:S_F + N_HEADS], d_small_t[S_CQ:S_CKV], d_small_t[S_CKV:S_KR],
                            d_small_t[kr:kr + ROPE_DIM]], axis=0)


def rope_tables(pos):
    t = pos.shape[0]
    inv_freq = ROPE_THETA ** (-jnp.arange(0, ROPE_DIM, 2, dtype=F32) / ROPE_DIM)
    ang = pos.astype(F32)[:, None] * inv_freq
    cos, sin = jnp.cos(ang), jnp.sin(ang)
    half = ROPE_DIM // 2

    def z(n):
        return jnp.zeros((t, n), F32)

    tab_c = jnp.concatenate([jnp.ones((t, HEAD_DIM), F32), cos, cos, z(LANES - HEAD_DIM - ROPE_DIM)], axis=1)
    tab_a = jnp.concatenate([z(HEAD_DIM), -sin, z(half), z(LANES - HEAD_DIM - ROPE_DIM)], axis=1)
    tab_b = jnp.concatenate([z(HEAD_DIM), z(half), sin, z(LANES - HEAD_DIM - ROPE_DIM)], axis=1)
    return tab_c, tab_a, tab_b


def _pad_lanes(v, n):
    return jnp.pad(v, ((0, 0), (0, n - v.shape[1])))


ATTN_BLK = 512
ATTN_FWD_BLK = 1024
ATTN_BWD_QBLK = 1024


def local_step(xs, pos, tgt, gains, w_early, late_weights, fwd_side=None, reduction=None):
    g_attn, b_forget, g_q, g_kv, g_fo, g_mo, g_mlp, g_fin = gains
    w_in_t, w_uq, w_ukv = w_early
    t = xs.shape[0]
    blk = min(ATTN_BLK, t)
    fox_scale = 1.0 / (HEAD_DIM ** 0.5)
    mla_scale = 1.0 / ((HEAD_DIM + ROPE_DIM) ** 0.5)

    w_qkv_t, w_small_t = split_w_in_t(w_in_t)
    w_uq_p = _pad_heads(w_uq, HEAD_DIM + ROPE_DIM)
    kv = w_ukv.reshape(KV_RANK, N_HEADS, 2 * HEAD_DIM)
    w_ukv_p = jnp.concatenate([_pad_heads(kv[:, :, :HEAD_DIM].reshape(KV_RANK, FW), HEAD_DIM),
                               kv[:, :, HEAD_DIM:].reshape(KV_RANK, FW)], axis=1)
    b_f = _pad_lanes(b_forget, LANES)
    tab_c, tab_a, tab_b = rope_tables(pos)

    h1 = rmsnorm(xs, g_attn, out_dtype=BF16, name="norm_attn")
    qkv, qkv_t = mm(h1, w_qkv_t, trans_b=True, out_dtypes=[BF16], t_blk=blk, name="proj_qkv")
    small, = mm(h1, w_small_t, trans_b=True, out_dtypes=[F32], name="proj_small")
    mq, mk, mv, lf, cqn, ckvn, mq_t, mv_t = mla_prep(small, g_q, g_kv, w_uq_p, w_ukv_p, tab_c, tab_a, tab_b, b_f,
                                                     name="mla_prep", bt=blk)
    f_cum = cumsum_rows(lf, reverse=False, name="gate_cumsum")
    f_blocks = f_cum[:, :N_HEADS].reshape(t // blk, blk, N_HEADS).transpose(0, 2, 1)
    fo, st_f, *gathered = flash_fwd(qkv_t, qkv, qkv_t, f_cum, qoff=0, koff=PAIRS, voff=2 * PAIRS, pair=True,
                                    scale=fox_scale, name="fox_fwd", blk=ATTN_FWD_BLK, side=fwd_side)
    w_o, w_up, w_down = late_weights(gathered)
    mo, st_m = flash_fwd(mq_t, mk, mv_t, None, qoff=0, koff=0, voff=0, pair=False, scale=mla_scale, name="mla_fwd",
                         blk=ATTN_FWD_BLK)
    mixed = mix_norm(fo, mo, g_fo, g_mo, name="norm_mix")

    def inv_rms(v):
        return lax.rsqrt(jnp.mean(v * v, axis=-1, keepdims=True) + EPS)

    def residual_then_norm(acc, res, g):
        xn = acc + res
        return xn, xn * inv_rms(xn) * g

    def norm_bwd(dh, xn, res, g):
        r = inv_rms(xn)
        uu = dh * g
        return (r * uu - xn * (r * r * r * jnp.mean(uu * xn, axis=-1, keepdims=True)) + res,
                jnp.sum(dh * (xn * r), axis=0, keepdims=True))

    def norm_bwd2(dh, xn, res, g):
        dx, dg = norm_bwd(dh, xn, res, g)
        return dx, dx, dg

    def residual_then_loss(acc, res, target, g):
        xn = acc + res
        r = inv_rms(xn)
        xh = xn * r
        e = xh * g - target
        part = 0.5 * jnp.sum(jnp.mean(e * e, axis=-1, keepdims=True), axis=0, keepdims=True)
        dy = e * (1.0 / xn.shape[1])
        uu = dy * g
        dx = r * uu - xn * (r * r * r * jnp.mean(uu * xn, axis=-1, keepdims=True))
        return dx, dx, jnp.sum(dy * xh, axis=0, keepdims=True), part + jnp.zeros_like(g)

    x1, h2 = mm(mixed, w_o, extras=[xs], vecs=[g_mlp], epilogue=residual_then_norm, out_dtypes=[F32, BF16], name="out_proj")

    def relu2(acc):
        r = jnp.maximum(acc, 0.0)
        return acc, r * r

    u, act = mm(h2, w_up, epilogue=relu2, out_dtypes=[BF16, BF16], name="mlp_up")
    dx2, dx2b, dg_fin, loss_row = mm(act, w_down, extras=[x1, tgt], vecs=[g_fin], epilogue=residual_then_loss,
                                     out_dtypes=[F32, BF16], n_sums=2, name="mlp_down_loss")
    loss = loss_row[:, :1]

    def relu2_grad(acc, uu):
        return (acc * (2.0 * jnp.maximum(uu.astype(F32), 0.0)),)

    du, = mm(dx2b, w_down, trans_b=True, extras=[u], epilogue=relu2_grad, out_dtypes=[BF16], name="mlp_down_bwd")
    dw_down = mm_tn(act, dx2b, name="dw_down").reshape(N_CHIPS, -1, w_down.shape[1])
    dx1, dx1b, dg_mlp = mm(du, w_up, trans_b=True, extras=[x1, dx2], vecs=[g_mlp], epilogue=norm_bwd2,
                           out_dtypes=[F32, BF16], n_sums=1, name="mlp_up_bwd")
    dw_up = mm_tn(h2, du, name="dw_up", col_shards=N_CHIPS)

    dmixed, = mm(dx1b, w_o, trans_b=True, out_dtypes=[F32], name="out_proj_bwd")
    dw_o = mm_tn(mixed, dx1b, name="dw_o").reshape(N_CHIPS, -1, w_o.shape[1])
    late = (dw_o, dw_up, dw_down)
    red = reduction
    dfo, dmo, dg_fo, dg_mo, st_f, st_m, dfo_t, dmo_t, *got = mix_norm_bwd(
        dmixed, fo, mo, g_fo, g_mo, st_f, st_m, name="norm_mix_bwd", bt=blk, side=red.late_swap(late) if red else None)
    dfq, dfk, dfv, d_f, *got = flash_bwd(
        qkv, qkv_t, qkv, qkv, dfo, dfo_t, st_f, f_blocks, qoff=0, koff=PAIRS, voff=2 * PAIRS, pair=True,
        scale=fox_scale, name="fox_bwd", qblk=ATTN_BWD_QBLK, side=red.late_scatter(got) if red else None)
    dmq, dmk, dmv, *got = flash_bwd(mq, mq_t, mk, mv, dmo, dmo_t, st_m, None, qoff=0, koff=0, voff=0,
                                    pair=False, scale=mla_scale, name="mla_bwd", qblk=ATTN_BWD_QBLK,
                                    side=red.late_halves(got) if red else None)
    if red:
        red.late_done(got)
    dqkv = jnp.concatenate([dfq, dfk, dfv], axis=1).astype(BF16)
    dlf = cumsum_rows(d_f, reverse=True, name="gate_cumsum_bwd")
    dsmall, dq_u, dkv_u, dg_q, dg_kv, db_f = mla_prep_bwd(dmq, dmk, dmv, dlf, small, g_q, g_kv, w_uq_p, w_ukv_p,
                                                          tab_c, tab_a, tab_b, b_f, name="mla_prep_bwd")
    dw_uq_p = mm_tn(cqn, dq_u, name="dw_uq")
    dw_ukv_p = mm_tn(ckvn, dkv_u, name="dw_ukv")

    dw_in_t = join_w_in_t(mm_tn(dqkv, h1, name="dw_qkv"), mm_tn(dsmall, h1, name="dw_small"))
    dw_uq = _unpad_heads(dw_uq_p, HEAD_DIM + ROPE_DIM)
    dk_cols = _unpad_heads(dw_ukv_p[:, :HW], HEAD_DIM).reshape(KV_RANK, N_HEADS, HEAD_DIM)
    dv_cols = dw_ukv_p[:, HW:].reshape(KV_RANK, N_HEADS, HEAD_DIM)
    dw_ukv = jnp.concatenate([dk_cols, dv_cols], axis=2).reshape(KV_RANK, N_HEADS * 2 * HEAD_DIM)
    early = (dw_in_t.reshape(N_CHIPS, -1, dw_in_t.shape[1]), split_cols(dw_uq), split_cols(dw_ukv))
    grad_x, dg_attn, *got = mm([dqkv, dsmall], [w_qkv_t, w_small_t], extras=[xs, dx1], vecs=[g_attn],
                               epilogue=norm_bwd, out_dtypes=[F32], n_sums=1, name="proj_bwd",
                               side=red.early_scatter(early) if red else None)
    if red:
        red.early_done(got)
    d_gains = (dg_attn, db_f[:, :N_HEADS], dg_q, dg_kv, dg_fo, dg_mo, dg_mlp, dg_fin)
    return loss, grad_x, early, late, d_gains


class GradReduction:
    def __init__(self, core_id, chip):
        self.core_id, self.chip = core_id, chip
        self.core = core_id.reshape(1).astype(jnp.int32)
        self.grads, self.pairs, self.halves, self.others = {}, {}, {}, {}

    def _other_halves(self, grads):
        out = []
        for g in grads:
            axis = 1 + split_axis(g.shape[1])
            size = g.shape[axis] // 2
            out.append(lax.dynamic_slice_in_dim(g, (1 - self.core_id) * size, size, axis=axis).astype(BF16))
        return out

    def _add_pairs(self, group, recvs):
        self.pairs[group] = [add_pair(g, r, self.core, name="add_pair_%s_%d" % (group, n))
                             for n, (g, r) in enumerate(zip(self.grads[group], recvs))]
        return scatter_side(self.pairs[group])

    def _chip_sums(self, group, scattered):
        chip = self.chip
        with_mine = [lax.dynamic_update_index_in_dim(s, lax.dynamic_index_in_dim(p, chip, 0, keepdims=True), chip, 0)
                     for s, p in zip(scattered, self.pairs[group])]
        self.halves[group] = [sum_chips(s, name="sum_chips_%s_%d" % (group, n)) for n, s in enumerate(with_mine)]
        return self.halves[group]

    def late_swap(self, grads):
        self.grads["late"] = list(grads)
        return swap_side(self._other_halves(grads))

    def late_scatter(self, recvs):
        return self._add_pairs("late", recvs)

    def late_halves(self, scattered):
        return swap_side(self._chip_sums("late", scattered))

    def late_done(self, others):
        self.others["late"] = list(others)

    def early_scatter(self, grads):
        self.grads["early"] = list(grads)
        return self._add_pairs("early", run_side(swap_side(self._other_halves(grads)), name="swap_early_sends"))

    def early_done(self, scattered):
        self.others["early"] = list(run_side(swap_side(self._chip_sums("early", scattered)), name="swap_early_halves"))

def kernel(x, positions, attn_norm_g, w_in, b_forget, q_norm_g, w_uq, kv_norm_g, w_ukv, fox_out_g, mla_out_g, w_o, mlp_norm_g, w_up, w_down, final_norm_g, loss_target, m_attn_norm_g, m_w_in, m_b_forget, m_q_norm_g, m_w_uq, m_kv_norm_g, m_w_ukv, m_fox_out_g, m_mla_out_g, m_w_o, m_mlp_norm_g, m_w_up, m_w_down, m_final_norm_g, v_attn_norm_g, v_w_in, v_b_forget, v_q_norm_g, v_w_uq, v_kv_norm_g, v_w_ukv, v_fox_out_g, v_mla_out_g, v_w_o, v_mlp_norm_g, v_w_up, v_w_down, v_final_norm_g):
    core_id = lax.axis_index("c")
    core = core_id.reshape(1).astype(jnp.int32)
    chip = 2 * lax.axis_index("x") + lax.axis_index("y")
    big = [w_in, w_uq, w_ukv, w_o, w_up, w_down]
    big_m = [m_w_in, m_w_uq, m_w_ukv, m_w_o, m_w_up, m_w_down]
    big_v = [v_w_in, v_w_uq, v_w_ukv, v_w_o, v_w_up, v_w_down]
    n_early = 3

    def vec(a):
        return a.reshape(1, -1)

    small = [attn_norm_g, b_forget, q_norm_g, kv_norm_g, fox_out_g, mla_out_g, mlp_norm_g, final_norm_g]
    small_m = [m_attn_norm_g, m_b_forget, m_q_norm_g, m_kv_norm_g, m_fox_out_g, m_mla_out_g, m_mlp_norm_g, m_final_norm_g]
    small_v = [v_attn_norm_g, v_b_forget, v_q_norm_g, v_kv_norm_g, v_fox_out_g, v_mla_out_g, v_mlp_norm_g, v_final_norm_g]
    gains = [vec(a) for a in small]

    shards = [(w[0].T if n == 0 else w[0]).astype(BF16) for n, w in enumerate(big)]

    def with_own(gathered, mine):
        return [lax.dynamic_update_index_in_dim(g, s, chip, 0) for g, s in zip(gathered, mine)]

    early = with_own(run_side(gather_side(shards[:n_early]), name="gather_early"), shards[:n_early])
    w_early = [early[0].reshape(-1, early[0].shape[2])] + [join_cols(g) for g in early[1:]]

    def late_weights(gathered):
        g_o, g_up, g_down = with_own(gathered, shards[n_early:])
        return g_o.reshape(-1, g_o.shape[2]), join_cols(g_up), g_down.reshape(-1, g_down.shape[2])

    reduction = GradReduction(core_id, chip)
    loss, grad_x, _, _, d_small = local_step(
        x[0], positions[0], loss_target[0], gains, w_early, late_weights, gather_side(shards[n_early:]), reduction)
    halves = reduction.halves["early"] + reduction.halves["late"]
    others = reduction.others["early"] + reduction.others["late"]

    def rows8(vs):
        return jnp.concatenate([_pad_lanes(vec(a).astype(F32), 1024) for a in vs], axis=0)

    with_loss = [jnp.concatenate([d, loss], axis=1) if n == 1 else d for n, d in enumerate(d_small)]
    g_small8 = allreduce_small(rows8(with_loss))

    outs_big = []
    for n, (w, gm, go, m, v) in enumerate(zip(big, halves, others, big_m, big_v)):
        _, k, cols = w.shape
        if n == 0:
            outs = adamw_halves_t(w[0].T, gm, go, m[0].T, v[0].T, core, name="adamw_%d" % n)
            outs_big.append([o.T[None] for o in outs])
        else:
            outs_big.append(adamw_halves(w, gm, go, m, v, core, name="adamw_%d" % n))
    d8, m8, v8 = adamw(rows8(small), g_small8, rows8(small_m), rows8(small_v), name="adamw_small")

    def unrows8(a8):
        return [a8[n, :s.size].reshape(s.shape) for n, s in enumerate(small)]

    loss_all = g_small8[1, N_HEADS]
    grads, deltas, new_m, new_v = [None] * 14, [None] * 14, [None] * 14, [None] * 14
    big_at = [1, 4, 6, 9, 11, 12]
    small_at = [0, 2, 3, 5, 7, 8, 10, 13]
    for n, at in enumerate(big_at):
        grads[at], deltas[at], new_m[at], new_v[at] = outs_big[n]
    for at, g, dd, mm_, vv in zip(small_at, unrows8(g_small8), unrows8(d8), unrows8(m8), unrows8(v8)):
        grads[at], deltas[at], new_m[at], new_v[at] = g, dd, mm_, vv
    return (loss_all, grad_x[None], *grads, *deltas, *new_m, *new_v)
```

```python
import jax
import jax.numpy as jnp
from jax import lax
from jax.experimental import pallas as pl
from jax.experimental.pallas import tpu as pltpu

F32 = jnp.float32
BF16 = jnp.bfloat16
MESH = pl.DeviceIdType.MESH

EPS = 1e-6
ROPE_THETA = 10000.0
N_HEADS = 8
PAIRS = N_HEADS // 2
HEAD_DIM = 64
ROPE_DIM = 32
LANES = 128
Q_RANK = 384
KV_RANK = 256
N_CHIPS = 4
ADAM_LR, ADAM_B1, ADAM_B2, ADAM_EPS, ADAM_WD, ADAM_STEP = 0.001, 0.9, 0.999, 1e-08, 0.01, 10
VMEM_LIMIT = 48 * 1024 * 1024
LOG2E = 1.4426950408889634
LN2 = 0.6931471805599453
NN = (((1,), (0,)), ((), ()))
NT = (((1,), (1,)), ((), ()))
TN = (((0,), (0,)), ((), ()))


def _params(sem=None):
    return pltpu.CompilerParams(dimension_semantics=sem, vmem_limit_bytes=VMEM_LIMIT)


def _fit(block, dim):
    if dim <= block:
        return dim
    return next(b for b in range(block - block % LANES, 0, -LANES) if dim % b == 0)


def _row_block(rows):
    return next(b for b in (256, 128, 64, 32, 16, 8) if rows % b == 0)


def gridded(body, *, name, grid, in_specs, out_specs, out_shape, ins, semantics, side=None):
    if side is None:
        return pl.pallas_call(body, name=name, grid=grid, in_specs=in_specs, out_specs=out_specs, out_shape=out_shape,
                              compiler_params=_params(semantics))(*ins)
    n_in, n_out = len(in_specs), len(out_specs)
    steps = 1
    for extent in grid:
        steps *= extent

    def riding(*refs):
        main_in, s_in, main_out, s_out, sems = _split_refs(refs, [n_in, len(side.ins), n_out, len(side.out_shapes), 2])
        step = 0
        for axis, extent in enumerate(grid):
            step = step * extent + pl.program_id(axis)

        @pl.when(step == 0)
        def _():
            side.first(s_in, s_out, *sems)

        body(*main_in, *main_out)

        @pl.when(step == steps - 1)
        def _():
            if side.mid is not None:
                side.mid(s_in, s_out, *sems)
            side.last(s_in, s_out, *sems)

    s_in_specs, s_out_specs = side.specs()
    return pl.pallas_call(
        riding, name=name, grid=grid, in_specs=list(in_specs) + s_in_specs, out_specs=list(out_specs) + s_out_specs,
        out_shape=list(out_shape) + side.out_shapes, scratch_shapes=side.sems(),
        compiler_params=_params(("arbitrary",) * len(grid)))(*ins, *side.ins)


def rmsnorm(x, g, *, out_dtype, name, bt=512, side=None):
    t, d = x.shape
    bt = min(bt, t)

    def body(x_ref, g_ref, o_ref):
        xv = x_ref[...].astype(F32)
        r = lax.rsqrt(jnp.mean(xv * xv, axis=-1, keepdims=True) + EPS)
        o_ref[...] = (xv * r * g_ref[...]).astype(o_ref.dtype)

    return gridded(
        body, name=name, grid=(t // bt,),
        in_specs=[pl.BlockSpec((bt, d), lambda i: (i, 0)), pl.BlockSpec((1, d), lambda i: (0, 0))],
        out_specs=[pl.BlockSpec((bt, d), lambda i: (i, 0))],
        out_shape=[jax.ShapeDtypeStruct((t, d), out_dtype)],
        ins=[x, g], semantics=("parallel",), side=side)


def mm(a, b, *, trans_b=False, a_pro=None, extras=(), vecs=(), epilogue=None, out_dtypes, n_sums=0, t_blk=None, name,
       bm=1024, bn=1024, side=None):
    a_list = list(a) if isinstance(a, (list, tuple)) else [a]
    b_list = list(b) if isinstance(b, (list, tuple)) else [b]
    m = a_list[0].shape[0]
    n = b_list[0].shape[0] if trans_b else b_list[0].shape[1]
    ks = [x.shape[1] for x in a_list]
    if sum(ks) > 2048:
        bm = bm // 2
    bm, bn = _fit(bm, m), _fit(bn, n)
    assert n_sums == 0 or bn == n
    n_ab, n_ex, n_vec, n_out = len(a_list), len(extras), len(vecs), len(out_dtypes)
    n_t = 0 if t_blk is None else 1

    def body(*refs):
        a_refs, b_refs, ex, vs, outs, t_outs, sums = _split_refs(refs, [n_ab, n_ab, n_ex, n_vec, n_out, n_t, n_sums])
        acc = None
        for a_ref, b_ref in zip(a_refs, b_refs):
            a_tile = a_ref[...] if a_pro is None else a_pro(a_ref[...])
            part = lax.dot_general(a_tile, b_ref[...], NT if trans_b else NN, preferred_element_type=F32)
            acc = part if acc is None else acc + part
        res = epilogue(acc, *[e[...] for e in ex], *[v[...] for v in vs]) if epilogue is not None else (acc,)
        for o, r in zip(outs, res[:n_out]):
            o[...] = r.astype(o.dtype)
        for t_ref in t_outs:
            for u in range(bm // t_blk):
                t_ref[u] = res[0][u * t_blk:(u + 1) * t_blk, :].T.astype(t_ref.dtype)
        if n_sums:
            @pl.when(pl.program_id(0) == 0)
            def _():
                for s_ref in sums:
                    s_ref[...] = jnp.zeros_like(s_ref)

            for s_ref, r in zip(sums, res[n_out:]):
                s_ref[...] += r

    tile = pl.BlockSpec((bm, bn), lambda i, j: (i, j))
    vec = pl.BlockSpec((1, bn), lambda i, j: (0, j))
    t_specs, t_shapes = [], []
    if t_blk is not None:
        t_specs = [pl.BlockSpec((bm // t_blk, bn, t_blk), lambda i, j: (i, j, 0))]
        t_shapes = [jax.ShapeDtypeStruct((m // t_blk, n, t_blk), out_dtypes[0])]
    a_specs = [pl.BlockSpec((bm, k), lambda i, j: (i, 0)) for k in ks]
    b_specs = [pl.BlockSpec((bn, k), lambda i, j: (j, 0)) if trans_b else pl.BlockSpec((k, bn), lambda i, j: (0, j)) for k in ks]
    return gridded(
        body, name=name, grid=(m // bm, n // bn),
        in_specs=a_specs + b_specs + [tile] * n_ex + [vec] * n_vec,
        out_specs=[tile] * n_out + t_specs + [vec] * n_sums,
        out_shape=[jax.ShapeDtypeStruct((m, n), dt) for dt in out_dtypes] + t_shapes + [jax.ShapeDtypeStruct((1, n), F32)] * n_sums,
        ins=[*a_list, *b_list, *extras, *vecs],
        semantics=("arbitrary", "arbitrary") if n_sums else ("parallel", "parallel"), side=side)


def mm_tn(a, b, *, a_pro=None, name, col_shards=1, bk=1024, bn=1024, bt=1024):
    t, k = a.shape
    n = b.shape[1]
    ns = n // col_shards
    bk, bn, bt = _fit(bk, k), _fit(bn, ns), _fit(bt, t)
    per = ns // bn

    def body(a_ref, b_ref, o_ref):
        @pl.when(pl.program_id(2) == 0)
        def _():
            o_ref[...] = jnp.zeros_like(o_ref)

        a_tile = a_ref[...] if a_pro is None else a_pro(a_ref[...])
        o_ref[...] += lax.dot_general(a_tile, b_ref[...], TN, preferred_element_type=F32)

    if col_shards == 1:
        out_spec = pl.BlockSpec((bk, bn), lambda i, j, s: (i, j))
        out_shape = jax.ShapeDtypeStruct((k, n), F32)
    else:
        out_spec = pl.BlockSpec((None, bk, bn), lambda i, j, s: (j // per, i, j % per))
        out_shape = jax.ShapeDtypeStruct((col_shards, k, ns), F32)
    return pl.pallas_call(
        body, name=name, grid=(k // bk, n // bn, t // bt),
        in_specs=[pl.BlockSpec((bt, bk), lambda i, j, s: (s, i)), pl.BlockSpec((bt, bn), lambda i, j, s: (s, j))],
        out_specs=out_spec, out_shape=out_shape,
        compiler_params=_params(("parallel", "parallel", "arbitrary")),
    )(a, b)


def _split3(x):
    hi = x.astype(BF16)
    r1 = x - hi.astype(F32)
    mid = r1.astype(BF16)
    lo = (r1 - mid.astype(F32)).astype(BF16)
    return hi, mid, lo


def cumsum_rows(x, *, reverse, name, bc=512):
    t, d = x.shape
    bc = min(bc, t)
    nb = t // bc

    def body(x_ref, o_ref, carry):
        @pl.when(pl.program_id(0) == 0)
        def _():
            carry[...] = jnp.zeros_like(carry)

        r = lax.broadcasted_iota(jnp.int32, (bc, bc), 0)
        c = lax.broadcasted_iota(jnp.int32, (bc, bc), 1)
        tri = jnp.where((r <= c) if reverse else (r >= c), 1.0, 0.0).astype(BF16)
        hi, mid, lo = _split3(x_ref[...])
        s = (lax.dot_general(tri, hi, NN, preferred_element_type=F32)
             + lax.dot_general(tri, mid, NN, preferred_element_type=F32)
             + lax.dot_general(tri, lo, NN, preferred_element_type=F32)) + carry[0:1, :]
        o_ref[...] = s
        carry[0:1, :] = s[0:1, :] if reverse else s[bc - 1:bc, :]

    imap = (lambda i: (nb - 1 - i, 0)) if reverse else (lambda i: (i, 0))
    return pl.pallas_call(
        body, name=name, grid=(nb,),
        in_specs=[pl.BlockSpec((bc, d), imap)], out_specs=pl.BlockSpec((bc, d), imap),
        out_shape=jax.ShapeDtypeStruct((t, d), F32),
        scratch_shapes=[pltpu.VMEM((8, d), F32)],
        compiler_params=_params(("arbitrary",)),
    )(x)


def _rope(x, c, a, b):
    return x * c + pltpu.roll(x, LANES - ROPE_DIM // 2, 1) * a + pltpu.roll(x, ROPE_DIM // 2, 1) * b


def _rope_bwd(d, c, a, b):
    return d * c + pltpu.roll(d * a, ROPE_DIM // 2, 1) + pltpu.roll(d * b, LANES - ROPE_DIM // 2, 1)


S_CQ, S_CKV, S_KR, S_F, S_END = 0, Q_RANK, Q_RANK + KV_RANK, Q_RANK + KV_RANK + LANES, 1024
HW = N_HEADS * LANES
FW = N_HEADS * HEAD_DIM


def mla_prep(small, g_q, g_kv, w_uq, w_ukv, tab_c, tab_a, tab_b, b_f, *, name, bt=512):
    t = small.shape[0]
    bt = min(bt, t)

    def body(s_ref, gq_ref, gkv_ref, wq_ref, wkv_ref, c_ref, a_ref, b_ref, bf_ref,
             mq_ref, mk_ref, mv_ref, lf_ref, cqn_ref, ckvn_ref, mqt_ref, mvt_ref):
        cq = s_ref[:, S_CQ:S_CKV]
        rq = lax.rsqrt(jnp.mean(cq * cq, axis=-1, keepdims=True) + EPS)
        cqn = (cq * rq * gq_ref[...]).astype(BF16)
        ckv = s_ref[:, S_CKV:S_KR]
        rkv = lax.rsqrt(jnp.mean(ckv * ckv, axis=-1, keepdims=True) + EPS)
        ckvn = (ckv * rkv * gkv_ref[...]).astype(BF16)
        cqn_ref[...] = cqn
        ckvn_ref[...] = ckvn
        tc, ta, tb = c_ref[...], a_ref[...], b_ref[...]
        q = jnp.dot(cqn, wq_ref[...], preferred_element_type=F32)
        kv = jnp.dot(ckvn, wkv_ref[...], preferred_element_type=F32)
        kr = _rope(s_ref[:, S_KR:S_F], tc, ta, tb)
        for h in range(N_HEADS):
            sl = slice(h * LANES, (h + 1) * LANES)
            roped = _rope(q[:, sl], tc, ta, tb)
            mq_ref[:, sl] = roped.astype(BF16)
            mqt_ref[0, sl, :] = roped.T.astype(BF16)
            mk_ref[:, sl] = (kv[:, sl] + kr).astype(BF16)
        mv_ref[...] = kv[:, HW:].astype(BF16)
        mvt_ref[0] = kv[:, HW:].T.astype(BF16)
        z = s_ref[:, S_F:S_END - LANES] + bf_ref[...]
        lf_ref[...] = jnp.minimum(z, 0.0) - jnp.log(1.0 + jnp.exp(-jnp.abs(z)))

    def row(w):
        return pl.BlockSpec((bt, w), lambda i: (i, 0))

    def full(arr):
        return pl.BlockSpec(arr.shape, lambda i: (0, 0))

    return pl.pallas_call(
        body, name=name, grid=(t // bt,),
        in_specs=[row(S_END), full(g_q), full(g_kv), full(w_uq), full(w_ukv), row(LANES), row(LANES), row(LANES), full(b_f)],
        out_specs=[row(HW), row(HW), row(FW), row(LANES), row(Q_RANK), row(KV_RANK),
                   pl.BlockSpec((1, HW, bt), lambda i: (i, 0, 0)), pl.BlockSpec((1, FW, bt), lambda i: (i, 0, 0))],
        out_shape=[jax.ShapeDtypeStruct((t, HW), BF16)] * 2 + [jax.ShapeDtypeStruct((t, FW), BF16), jax.ShapeDtypeStruct((t, LANES), F32),
                   jax.ShapeDtypeStruct((t, Q_RANK), BF16), jax.ShapeDtypeStruct((t, KV_RANK), BF16),
                   jax.ShapeDtypeStruct((t // bt, HW, bt), BF16), jax.ShapeDtypeStruct((t // bt, FW, bt), BF16)],
        compiler_params=_params(("parallel",)),
    )(small, g_q, g_kv, w_uq, w_ukv, tab_c, tab_a, tab_b, b_f)


def mla_prep_bwd(dmq, dmk, dmv, dlf, small, g_q, g_kv, w_uq, w_ukv, tab_c, tab_a, tab_b, b_f, *, name, bt=512):
    t = small.shape[0]
    bt = min(bt, t)

    def body(dmq_ref, dmk_ref, dmv_ref, dlf_ref, s_ref, gq_ref, gkv_ref, wq_ref, wkv_ref, c_ref, a_ref, b_ref, bf_ref,
             ds_ref, dq_ref, dkv_ref, dgq_ref, dgkv_ref, db_ref):
        tc, ta, tb = c_ref[...], a_ref[...], b_ref[...]
        lane = lax.broadcasted_iota(jnp.int32, (1, LANES), 1)
        dkr = jnp.zeros((bt, LANES), F32)
        for h in range(N_HEADS):
            sl = slice(h * LANES, (h + 1) * LANES)
            dq_ref[:, sl] = _rope_bwd(dmq_ref[:, sl], tc, ta, tb).astype(BF16)
            dkr = dkr + dmk_ref[:, sl]
        dkv_ref[:, :HW] = dmk_ref[...].astype(BF16)
        dkv_ref[:, HW:] = dmv_ref[...].astype(BF16)
        in_rope = (lane >= HEAD_DIM) & (lane < HEAD_DIM + ROPE_DIM)
        ds_ref[:, S_KR:S_F] = jnp.where(in_rope, _rope_bwd(dkr, tc, ta, tb), 0.0).astype(BF16)

        def norm_bwd(raw, g_ref, dn, dg_ref):
            r = lax.rsqrt(jnp.mean(raw * raw, axis=-1, keepdims=True) + EPS)
            u = dn * g_ref[...]
            dot = jnp.mean(u * raw, axis=-1, keepdims=True)
            dg_ref[...] += jnp.sum(dn * (raw * r), axis=0, keepdims=True)
            return r * u - raw * (r * r * r * dot)

        @pl.when(pl.program_id(0) == 0)
        def _():
            dgq_ref[...] = jnp.zeros_like(dgq_ref)
            dgkv_ref[...] = jnp.zeros_like(dgkv_ref)
            db_ref[...] = jnp.zeros_like(db_ref)

        dcqn = lax.dot_general(dq_ref[...], wq_ref[...], NT, preferred_element_type=F32)
        ds_ref[:, S_CQ:S_CKV] = norm_bwd(s_ref[:, S_CQ:S_CKV], gq_ref, dcqn, dgq_ref).astype(BF16)
        dckvn = lax.dot_general(dkv_ref[...], wkv_ref[...], NT, preferred_element_type=F32)
        ds_ref[:, S_CKV:S_KR] = norm_bwd(s_ref[:, S_CKV:S_KR], gkv_ref, dckvn, dgkv_ref).astype(BF16)
        z = s_ref[:, S_F:S_END - LANES] + bf_ref[...]
        dz = jnp.where(lane < N_HEADS, dlf_ref[...] / (1.0 + jnp.exp(z)), 0.0)
        db_ref[...] += jnp.sum(dz, axis=0, keepdims=True)
        ds_ref[:, S_F:S_END - LANES] = dz.astype(BF16)
        ds_ref[:, S_END - LANES:] = jnp.zeros((bt, LANES), BF16)

    def row(w):
        return pl.BlockSpec((bt, w), lambda i: (i, 0))

    def full(arr):
        return pl.BlockSpec(arr.shape, lambda i: (0, 0))

    def vec(w):
        return pl.BlockSpec((1, w), lambda i: (0, 0))

    return pl.pallas_call(
        body, name=name, grid=(t // bt,),
        in_specs=[row(HW), row(HW), row(FW), row(LANES), row(S_END), full(g_q), full(g_kv), full(w_uq), full(w_ukv),
                  row(LANES), row(LANES), row(LANES), full(b_f)],
        out_specs=[row(S_END), row(HW), row(HW + FW), vec(Q_RANK), vec(KV_RANK), vec(LANES)],
        out_shape=[jax.ShapeDtypeStruct((t, S_END), BF16), jax.ShapeDtypeStruct((t, HW), BF16),
                   jax.ShapeDtypeStruct((t, HW + FW), BF16), jax.ShapeDtypeStruct((1, Q_RANK), F32),
                   jax.ShapeDtypeStruct((1, KV_RANK), F32), jax.ShapeDtypeStruct((1, LANES), F32)],
        compiler_params=_params(("arbitrary",)),
    )(dmq, dmk, dmv, dlf, small, g_q, g_kv, w_uq, w_ukv, tab_c, tab_a, tab_b, b_f)


class Side:
    def __init__(self, ins, out_shapes, n_sems, first, last, mid=None):
        self.ins, self.out_shapes, self.n_sems = list(ins), list(out_shapes), n_sems
        self.first, self.mid, self.last = first, mid, last

    def specs(self):
        return [ANY] * len(self.ins), [ANY] * len(self.out_shapes)

    def sems(self):
        return [pltpu.SemaphoreType.DMA((self.n_sems,)), pltpu.SemaphoreType.DMA((self.n_sems,))]


def _lane():
    return lax.broadcasted_iota(jnp.int32, (1, LANES), 1)


def _halves(x):
    zero = jnp.zeros_like(x)
    return [jnp.where(_lane() < HEAD_DIM, x, zero), jnp.where(_lane() >= HEAD_DIM, x, zero)]


def _groups(x):
    return [x[:, :LANES], x[:, LANES:]]


def _pick_row(tile, h):
    row = lax.broadcasted_iota(jnp.int32, (tile.shape[0], 1), 0)
    return jnp.sum(jnp.where(row == h, tile, 0.0), axis=0, keepdims=True)


def _pick_lane(tile, h):
    return jnp.sum(jnp.where(_lane() == h, tile, 0.0), axis=1, keepdims=True)


def _row_halves(x):
    row = lax.broadcasted_iota(jnp.int32, (LANES, 1), 0)
    zero = jnp.zeros_like(x)
    return [jnp.where(row < HEAD_DIM, x, zero), jnp.where(row >= HEAD_DIM, x, zero)]


def _below_diagonal(s, lead=0):
    r = lax.broadcasted_iota(jnp.int32, s.shape, 0)
    c = lax.broadcasted_iota(jnp.int32, s.shape, 1)
    return jnp.where(c <= r + lead, s, -jnp.inf)


def _above_diagonal(s):
    r = lax.broadcasted_iota(jnp.int32, s.shape, 0)
    c = lax.broadcasted_iota(jnp.int32, s.shape, 1)
    return jnp.where(r <= c, s, -jnp.inf)


def _split_refs(refs, counts):
    out, at = [], 0
    for n in counts:
        out.append(refs[at:at + n])
        at += n
    return out


def flash_fwd(qt_arr, k_arr, vt_arr, f_cum, *, qoff, koff, voff, pair, scale, name, blk=512, side=None):
    t = k_arr.shape[0]
    tblk = qt_arr.shape[2]
    blk = max(min(blk, t), tblk)
    sub = blk // tblk
    nb = t // blk
    w = LANES if pair else 2 * LANES
    has_bias = f_cum is not None
    ins = [qt_arr, k_arr, vt_arr] + ([f_cum] if has_bias else [])

    def wide(ref, first):
        parts = [ref[first + u] for u in range(sub)]
        return parts[0] if sub == 1 else jnp.concatenate(parts, axis=1)
    s_ins, s_outs = (side.ins, side.out_shapes) if side else ([], [])

    def body(*refs):
        main, si, outs, so, sems = _split_refs(refs, [len(ins), len(s_ins), 2, len(s_outs), 2 if side else 0])
        qt_ref, k_ref, vt_ref = main[:3]
        f_ref = main[3] if has_bias else None
        o_ref, st_ref = outs
        g, i = pl.program_id(0), pl.program_id(1)
        step_id = g * nb + i
        if side:
            @pl.when(step_id == 0)
            def _():
                side.first(si, so, *sems)

            if side.mid is not None:
                @pl.when(step_id == (3 * PAIRS * nb) // 4)
                def _():
                    side.mid(si, so, *sems)

        qt = (wide(qt_ref, 0).astype(F32) * (scale * LOG2E)).astype(BF16)
        qts = _row_halves(qt) if pair else [qt[:LANES], qt[LANES:]]

        def step(j, carry, diagonal):
            rows = pl.ds(pl.multiple_of(j * blk, blk), blk)
            kk = k_ref[rows, :]
            ks = [kk, kk] if pair else _groups(kk)
            vt = wide(vt_ref, sub * j)
            out = []
            for n in range(2):
                m, l, acc = carry[n]
                s = jnp.dot(ks[n], qts[n], preferred_element_type=F32)
                if has_bias:
                    s = s - LOG2E * _pick_lane(f_ref[rows, :], 2 * g + n)
                if diagonal:
                    s = _above_diagonal(s)
                m_new = jnp.maximum(m, jnp.max(s, axis=0, keepdims=True))
                alpha = jnp.exp2(m - m_new)
                p = jnp.exp2(s - m_new)
                out.append((m_new, alpha * l + jnp.sum(p, axis=0, keepdims=True),
                            alpha * acc + jnp.dot(vt[n * HEAD_DIM:(n + 1) * HEAD_DIM], p.astype(BF16),
                                                  preferred_element_type=F32)))
            return tuple(out)

        init = tuple((jnp.full((1, blk), -jnp.inf, F32), jnp.zeros((1, blk), F32), jnp.zeros((HEAD_DIM, blk), F32))
                     for _ in range(2))
        carry = lax.fori_loop(0, i, lambda j, c: step(j, c, False), init)
        (ma, la, acca), (mb, lb, accb) = step(i, carry, True)
        o_ref[...] = jnp.concatenate([acca / la, accb / lb], axis=0).T
        row = lax.broadcasted_iota(jnp.int32, (LANES, 1), 0)
        st_ref[0] = jnp.where(row == 0, ma + jnp.log2(la), jnp.where(row == 1, mb + jnp.log2(lb), 0.0)).T
        if side:
            @pl.when(step_id == PAIRS * nb - 1)
            def _():
                side.last(si, so, *sems)

    in_specs = [pl.BlockSpec((sub, w, tblk), lambda g, i: (i, qoff + g, 0)), pl.BlockSpec((t, w), lambda g, i: (0, koff + g)),
                pl.BlockSpec((t // tblk, LANES, tblk), lambda g, i: (0, voff + g, 0))]
    if has_bias:
        in_specs.append(pl.BlockSpec((t, LANES), lambda g, i: (0, 0)))
    s_in_specs, s_out_specs = side.specs() if side else ([], [])
    return pl.pallas_call(
        body, name=name, grid=(PAIRS, nb), in_specs=in_specs + s_in_specs,
        out_specs=[pl.BlockSpec((blk, LANES), lambda g, i: (i, g)), pl.BlockSpec((1, blk, LANES), lambda g, i: (g, i, 0))]
        + s_out_specs,
        out_shape=[jax.ShapeDtypeStruct((t, PAIRS * LANES), F32), jax.ShapeDtypeStruct((PAIRS, t, LANES), F32)] + list(s_outs),
        scratch_shapes=side.sems() if side else [],
        compiler_params=_params(("arbitrary", "arbitrary")),
    )(*ins, *s_ins)


def mix_norm(fo, mo, g_fo, g_mo, *, name, bt=512):
    t, d = fo.shape
    bt = min(bt, t)

    def body(fo_ref, mo_ref, gf_ref, gm_ref, o_ref):
        for n, (x_ref, g_ref) in enumerate(((fo_ref, gf_ref), (mo_ref, gm_ref))):
            xv = x_ref[...]
            r = lax.rsqrt(jnp.mean(xv * xv, axis=-1, keepdims=True) + EPS)
            o_ref[:, n * d:(n + 1) * d] = (xv * r * g_ref[...]).astype(BF16)

    row = pl.BlockSpec((bt, d), lambda i: (i, 0))
    vec = pl.BlockSpec((1, d), lambda i: (0, 0))
    return pl.pallas_call(
        body, name=name, grid=(t // bt,), in_specs=[row, row, vec, vec],
        out_specs=pl.BlockSpec((bt, 2 * d), lambda i: (i, 0)),
        out_shape=jax.ShapeDtypeStruct((t, 2 * d), BF16),
        compiler_params=_params(("parallel",)),
    )(fo, mo, g_fo, g_mo)


def mix_norm_bwd(dmixed, fo, mo, g_fo, g_mo, st_f, st_m, *, name, bt=512, side=None):
    t, d = fo.shape
    bt = min(bt, t)

    def body(dm_ref, fo_ref, mo_ref, gf_ref, gm_ref, sf_ref, sm_ref, dfo_ref, dmo_ref, dgf_ref, dgm_ref, sfo_ref, smo_ref,
             dfot_ref, dmot_ref):
        @pl.when(pl.program_id(0) == 0)
        def _():
            dgf_ref[...] = jnp.zeros_like(dgf_ref)
            dgm_ref[...] = jnp.zeros_like(dgm_ref)

        groups = ((fo_ref, gf_ref, dfo_ref, dgf_ref, sf_ref, sfo_ref, dfot_ref),
                  (mo_ref, gm_ref, dmo_ref, dgm_ref, sm_ref, smo_ref, dmot_ref))
        for n, (x_ref, g_ref, dx_ref, dg_ref, st_ref, sto_ref, dxt_ref) in enumerate(groups):
            xv = x_ref[...]
            dhv = dm_ref[:, n * d:(n + 1) * d]
            r = lax.rsqrt(jnp.mean(xv * xv, axis=-1, keepdims=True) + EPS)
            u = dhv * g_ref[...]
            dxf = r * u - xv * (r * r * r * jnp.mean(u * xv, axis=-1, keepdims=True))
            dxb = dxf.astype(BF16)
            dx_ref[...] = dxb
            dxt_ref[0] = dxf.T.astype(BF16)
            dg_ref[...] += jnp.sum(dhv * (xv * r), axis=0, keepdims=True)
            prod = xv * dxb.astype(F32)
            for g in range(PAIRS):
                grp = prod[:, g * LANES:(g + 1) * LANES]
                da = jnp.sum(jnp.where(_lane() < HEAD_DIM, grp, 0.0), axis=1, keepdims=True)
                db = jnp.sum(jnp.where(_lane() >= HEAD_DIM, grp, 0.0), axis=1, keepdims=True)
                sto_ref[g] = jnp.where(_lane() == 2, da, jnp.where(_lane() == 3, db, st_ref[g]))

    row = pl.BlockSpec((bt, d), lambda i: (i, 0))
    vec = pl.BlockSpec((1, d), lambda i: (0, 0))
    stat = pl.BlockSpec((PAIRS, bt, LANES), lambda i: (0, i, 0))
    return gridded(
        body, name=name, grid=(t // bt,),
        in_specs=[pl.BlockSpec((bt, 2 * d), lambda i: (i, 0)), row, row, vec, vec, stat, stat],
        out_specs=[row, row, vec, vec, stat, stat] + [pl.BlockSpec((1, d, bt), lambda i: (i, 0, 0))] * 2,
        out_shape=[jax.ShapeDtypeStruct((t, d), BF16)] * 2 + [jax.ShapeDtypeStruct((1, d), F32)] * 2
        + [jax.ShapeDtypeStruct(st_f.shape, F32)] * 2 + [jax.ShapeDtypeStruct((t // bt, d, bt), BF16)] * 2,
        ins=[dmixed, fo, mo, g_fo, g_mo, st_f, st_m], semantics=("arbitrary",), side=side)


def flash_bwd(q_arr, qt_arr, k_arr, v_arr, do_arr, dot_arr, st, f_blocks, *, qoff, koff, voff, pair, scale, name, qblk=1024,
              side=None):
    t = q_arr.shape[0]
    blk = qt_arr.shape[2]
    qblk = max(min(qblk, t), blk)
    sub = qblk // blk
    nb, nbq = t // blk, t // qblk
    w = LANES if pair else 2 * LANES
    hw = w // 2
    has_bias = f_blocks is not None
    split = _halves if pair else _groups
    ins = [q_arr, qt_arr, k_arr, v_arr, do_arr, dot_arr, st] + ([f_blocks] if has_bias else [])
    n_out = 4 if has_bias else 3
    s_ins, s_outs = (side.ins, side.out_shapes) if side else ([], [])

    def wide(ref, first):
        parts = [ref[first + u] for u in range(sub)]
        return parts[0] if sub == 1 else jnp.concatenate(parts, axis=1)

    def body(*refs):
        main, si, outs, so, sems = _split_refs(refs, [len(ins), len(s_ins), n_out, len(s_outs), 2 if side else 0])
        q_ref, qt_ref, k_ref, v_ref, do_ref, dot_ref, st_ref = main[:7]
        dq_ref, dk_ref, dv_ref = outs[:3]
        g, j = pl.program_id(0), pl.program_id(1)
        step_id = g * nb + j
        if side:
            @pl.when(step_id == 0)
            def _():
                side.first(si, so, *sems)

        @pl.when(j == 0)
        def _():
            dq_ref[...] = jnp.zeros_like(dq_ref)

        kk, vv = k_ref[...], v_ref[...]
        ks = [kk, kk] if pair else _groups(kk)
        if has_bias:
            f_ref, df_ref = main[7], outs[3]
            fk = [LOG2E * _pick_row(f_ref[0], 2 * g + n) for n in range(2)]

            @pl.when(step_id == 0)
            def _():
                df_ref[...] = jnp.zeros_like(df_ref)

        def step(i, carry, diagonal):
            rows = pl.ds(pl.multiple_of(i * qblk, qblk), qblk)
            qs = split((q_ref[rows, :].astype(F32) * (scale * LOG2E)).astype(BF16))
            qt = (wide(qt_ref, sub * i).astype(F32) * (scale * LOG2E)).astype(BF16)
            dos = [h.astype(BF16) for h in _halves(do_ref[rows, :].astype(F32))]
            dot = wide(dot_ref, sub * i)
            stats = st_ref[0, rows, :]
            new, dqs, row_sums = [], [], []
            for n in range(2):
                dkt, dvt, dfk = carry[n]
                s = lax.dot_general(qs[n], ks[n], NT, preferred_element_type=F32)
                if has_bias:
                    s = s - fk[n]
                if diagonal:
                    s = _below_diagonal(s, i * qblk - j * blk)
                p = jnp.exp2(s - stats[:, n:n + 1])
                dp = lax.dot_general(dos[n], vv, NT, preferred_element_type=F32)
                ds = p * (dp - stats[:, 2 + n:3 + n])
                dsb = ds.astype(BF16)
                dvt = dvt + jnp.dot(dot[n * HEAD_DIM:(n + 1) * HEAD_DIM], p.astype(BF16), preferred_element_type=F32)
                dkt = dkt + jnp.dot(qt[n * hw:(n + 1) * hw], dsb, preferred_element_type=F32)
                dqs.append(jnp.dot(dsb, ks[n], preferred_element_type=F32))
                if has_bias:
                    dfk = dfk - jnp.sum(ds, axis=0, keepdims=True)
                    row_sums.append(jnp.sum(ds, axis=1, keepdims=True))
                new.append((dkt, dvt, dfk))
            dq = (jnp.where(_lane() < HEAD_DIM, dqs[0], dqs[1]) if pair else jnp.concatenate(dqs, axis=1)) * scale
            if has_bias:
                df_ref[rows, :] += jnp.where(_lane() == 2 * g, row_sums[0], jnp.where(_lane() == 2 * g + 1, row_sums[1], 0.0))
            dq_ref[rows, :] += dq
            return tuple(new)

        init = tuple((jnp.zeros((hw, blk), F32), jnp.zeros((HEAD_DIM, blk), F32), jnp.zeros((1, blk), F32)) for _ in range(2))
        first = j // sub
        carry = step(first, init, True)
        (dka, dva, dfa), (dkb, dvb, dfb) = lax.fori_loop(first + 1, nbq, lambda i, c: step(i, c, False), carry)
        dk_ref[...] = jnp.concatenate([dka, dkb], axis=0).T * LN2
        dv_ref[...] = jnp.concatenate([dva, dvb], axis=0).T
        if has_bias:
            row = lax.broadcasted_iota(jnp.int32, (LANES, 1), 0)
            by_head = jnp.where(row == 2 * g, dfa, jnp.where(row == 2 * g + 1, dfb, 0.0))
            df_ref[pl.ds(pl.multiple_of(j * blk, blk), blk), :] += by_head.T
        if side:
            @pl.when(step_id == PAIRS * nb - 1)
            def _():
                side.last(si, so, *sems)

    in_specs = [pl.BlockSpec((t, w), lambda g, j: (0, qoff + g)), pl.BlockSpec((nb, w, blk), lambda g, j: (0, qoff + g, 0)),
                pl.BlockSpec((blk, w), lambda g, j: (j, koff + g)), pl.BlockSpec((blk, LANES), lambda g, j: (j, voff + g)),
                pl.BlockSpec((t, LANES), lambda g, j: (0, g)), pl.BlockSpec((nb, LANES, blk), lambda g, j: (0, g, 0)),
                pl.BlockSpec((1, t, LANES), lambda g, j: (g, 0, 0))]
    out_specs = [pl.BlockSpec((t, w), lambda g, j: (0, g)), pl.BlockSpec((blk, w), lambda g, j: (j, g)),
                 pl.BlockSpec((blk, LANES), lambda g, j: (j, g))]
    out_shape = [jax.ShapeDtypeStruct((t, PAIRS * w), F32)] * 2 + [jax.ShapeDtypeStruct((t, PAIRS * LANES), F32)]
    if has_bias:
        in_specs.append(pl.BlockSpec((1, N_HEADS, blk), lambda g, j: (j, 0, 0)))
        out_specs.append(pl.BlockSpec((t, LANES), lambda g, j: (0, 0)))
        out_shape.append(jax.ShapeDtypeStruct((t, LANES), F32))
    s_in_specs, s_out_specs = side.specs() if side else ([], [])
    return pl.pallas_call(
        body, name=name, grid=(PAIRS, nb), in_specs=in_specs + s_in_specs, out_specs=out_specs + s_out_specs,
        out_shape=out_shape + list(s_outs), scratch_shapes=side.sems() if side else [],
        compiler_params=_params(("arbitrary", "arbitrary")),
    )(*ins, *s_ins)


def _adamw_math(w, g, m, v):
    nm = ADAM_B1 * m + (1.0 - ADAM_B1) * g
    nv = ADAM_B2 * v + (1.0 - ADAM_B2) * (g * g)
    m_hat = nm / (1.0 - ADAM_B1 ** ADAM_STEP)
    v_hat = nv / (1.0 - ADAM_B2 ** ADAM_STEP)
    return -ADAM_LR * (m_hat / (jnp.sqrt(v_hat) + ADAM_EPS) + ADAM_WD * w), nm, nv


def adamw(w, g, m, v, *, name):
    rws, cols = w.shape
    br = _row_block(rws)

    def body(w_ref, g_ref, m_ref, v_ref, d_ref, nm_ref, nv_ref):
        d_ref[...], nm_ref[...], nv_ref[...] = _adamw_math(w_ref[...], g_ref[...], m_ref[...], v_ref[...])

    blk = pl.BlockSpec((br, cols), lambda i: (i, 0))
    return pl.pallas_call(
        body, name=name, grid=(rws // br,), in_specs=[blk] * 4, out_specs=[blk] * 3,
        out_shape=[jax.ShapeDtypeStruct((rws, cols), F32)] * 3,
        compiler_params=_params(("parallel",)),
    )(w, g, m, v)


def adamw_halves(w, g_mine, g_other, m, v, core, *, name):
    _, k, n = w.shape
    br = _row_block(k // 2)
    nh = k // 2 // br

    def body(c_ref, w_ref, gm_ref, go_ref, m_ref, v_ref, g_out, d_ref, nm_ref, nv_ref):
        gv = jnp.where(pl.program_id(0) == c_ref[0], gm_ref[...], go_ref[...])
        g_out[0] = gv
        d_ref[0], nm_ref[0], nv_ref[0] = _adamw_math(w_ref[0], gv, m_ref[0], v_ref[0])

    full = pl.BlockSpec((1, br, n), lambda hb, i, c: (0, hb * nh + i, 0))
    half = pl.BlockSpec((br, n), lambda hb, i, c: (i, 0))
    return pl.pallas_call(
        body, name=name,
        grid_spec=pltpu.PrefetchScalarGridSpec(num_scalar_prefetch=1, grid=(2, nh), in_specs=[full, half, half, full, full],
                                               out_specs=[full] * 4),
        out_shape=[jax.ShapeDtypeStruct(w.shape, F32)] * 4,
        compiler_params=_params(("parallel", "parallel")),
    )(core, w, g_mine, g_other, m, v)


def adamw_halves_t(wt, gt_mine, gt_other, mt, vt, core, *, name, bc=128):
    n, k = wt.shape
    nh = k // 2 // bc

    def body(c_ref, w_ref, gm_ref, go_ref, m_ref, v_ref, g_out, d_ref, nm_ref, nv_ref):
        gv = jnp.where(pl.program_id(0) == c_ref[0], gm_ref[...], go_ref[...])
        g_out[...] = gv
        d_ref[...], nm_ref[...], nv_ref[...] = _adamw_math(w_ref[...], gv, m_ref[...], v_ref[...])

    full = pl.BlockSpec((n, bc), lambda hb, i, c: (0, hb * nh + i))
    half = pl.BlockSpec((n, bc), lambda hb, i, c: (0, i))
    return pl.pallas_call(
        body, name=name,
        grid_spec=pltpu.PrefetchScalarGridSpec(num_scalar_prefetch=1, grid=(2, nh), in_specs=[full, half, half, full, full],
                                               out_specs=[full] * 4),
        out_shape=[jax.ShapeDtypeStruct(wt.shape, F32)] * 4,
        compiler_params=_params(("parallel", "parallel")),
    )(core, wt, gt_mine, gt_other, mt, vt)


def add_pair(dw, recv, core, *, name):
    n4, k, n = dw.shape
    half = (1, k // 2, n) if split_axis(k) == 0 else (1, k, n // 2)
    mine = (lambda q, c: (q, c[0], 0)) if split_axis(k) == 0 else (lambda q, c: (q, 0, c[0]))

    def body(c_ref, a_ref, b_ref, o_ref):
        o_ref[...] = (a_ref[...] + b_ref[...].astype(F32)).astype(BF16)

    return pl.pallas_call(
        body, name=name,
        grid_spec=pltpu.PrefetchScalarGridSpec(
            num_scalar_prefetch=1, grid=(n4,),
            in_specs=[pl.BlockSpec(half, mine), pl.BlockSpec(half, lambda q, c: (q, 0, 0))],
            out_specs=pl.BlockSpec(half, lambda q, c: (q, 0, 0))),
        out_shape=jax.ShapeDtypeStruct((n4,) + half[1:], BF16),
        compiler_params=_params(("parallel",)),
    )(core, dw, recv)


def sum_chips(parts, *, name):
    n4, r, n = parts.shape
    if r % 16 == 0:
        br, bc = _row_block(r), n
    else:
        br, bc = r, LANES

    def body(p_ref, o_ref):
        acc = p_ref[0].astype(F32)
        for q in range(1, n4):
            acc = acc + p_ref[q].astype(F32)
        o_ref[...] = acc

    return pl.pallas_call(
        body, name=name, grid=(r // br, n // bc),
        in_specs=[pl.BlockSpec((n4, br, bc), lambda i, j: (0, i, j))], out_specs=pl.BlockSpec((br, bc), lambda i, j: (i, j)),
        out_shape=jax.ShapeDtypeStruct((r, n), F32),
        compiler_params=_params(("parallel", "parallel")),
    )(parts)


ANY = pl.BlockSpec(memory_space=pl.ANY)


def _place():
    x, y, c = lax.axis_index("x"), lax.axis_index("y"), lax.axis_index("c")
    chips = [(1 - x, y), (x, 1 - y), (1 - x, 1 - y)]
    return x, y, c, chips


def _copy(src, dst, send_sems, recv_sems, k, to):
    return pltpu.make_async_remote_copy(src_ref=src, dst_ref=dst, send_sem=send_sems.at[k], recv_sem=recv_sems.at[k],
                                        device_id=to, device_id_type=MESH)


def split_axis(rows):
    return 0 if rows % 32 == 0 else 1


def _half(ref, lead, hf):
    rows, cols = ref.shape[-2:]
    if split_axis(rows) == 0:
        at = (pl.ds(hf * (rows // 2), rows // 2), slice(None))
    else:
        at = (slice(None), pl.ds(hf * (cols // 2), cols // 2))
    return ref.at[at] if lead is None else ref.at[(lead,) + at]


def _gather_first(srcs, dsts, ssems, rsems):
    x, y, c, chips = _place()
    for ti, (s, d) in enumerate(zip(srcs, dsts)):
        for j, (cx, cy) in enumerate(chips):
            _copy(_half(s, None, c), _half(d, 2 * x + y, c), ssems, rsems, 3 * ti + j, (cx, cy, c)).start()


def _gather_mid(srcs, dsts, ssems, rsems):
    x, y, c, chips = _place()
    n1 = 3 * len(srcs)
    for ti, d in enumerate(dsts):
        for j, (cx, cy) in enumerate(chips):
            landed = _half(d, 2 * cx + cy, c)
            _copy(landed, landed, ssems, rsems, 3 * ti + j, (cx, cy, c)).wait_recv()
            _copy(landed, landed, ssems, rsems, n1 + 3 * ti + j, (x, y, 1 - c)).start()


def _gather_last(srcs, dsts, ssems, rsems):
    x, y, c, chips = _place()
    n1 = 3 * len(srcs)
    for ti, (s, d) in enumerate(zip(srcs, dsts)):
        for j, (cx, cy) in enumerate(chips):
            other = _half(d, 2 * cx + cy, 1 - c)
            _copy(other, other, ssems, rsems, n1 + 3 * ti + j, (x, y, 1 - c)).wait_recv()
        for j, (cx, cy) in enumerate(chips):
            mine = _half(s, None, c)
            _copy(mine, mine, ssems, rsems, 3 * ti + j, (cx, cy, c)).wait_send()
            _copy(mine, mine, ssems, rsems, n1 + 3 * ti + j, (x, y, 1 - c)).wait_send()


def gather_side(shards):
    return Side(shards, [jax.ShapeDtypeStruct((N_CHIPS,) + s.shape, s.dtype) for s in shards], 6 * len(shards),
                _gather_first, _gather_last, _gather_mid)


def _scatter_first(srcs, dsts, ssems, rsems):
    x, y, c, chips = _place()
    for ti, (s, d) in enumerate(zip(srcs, dsts)):
        for j, (cx, cy) in enumerate(chips):
            _copy(s.at[2 * cx + cy], d.at[2 * x + y], ssems, rsems, 3 * ti + j, (cx, cy, c)).start()


def _scatter_last(srcs, dsts, ssems, rsems):
    x, y, c, chips = _place()
    for ti, (s, d) in enumerate(zip(srcs, dsts)):
        for j, (cx, cy) in enumerate(chips):
            _copy(s.at[2 * cx + cy], d.at[2 * cx + cy], ssems, rsems, 3 * ti + j, (cx, cy, c)).wait_recv()
        for j, (cx, cy) in enumerate(chips):
            _copy(s.at[2 * cx + cy], d.at[2 * cx + cy], ssems, rsems, 3 * ti + j, (cx, cy, c)).wait_send()


def scatter_side(parts):
    return Side(parts, [jax.ShapeDtypeStruct(p.shape, p.dtype) for p in parts], 3 * len(parts), _scatter_first, _scatter_last)


def run_side(side, *, name):
    n_in, n_out = len(side.ins), len(side.out_shapes)

    def body(*refs):
        si, so, sems = _split_refs(refs, [n_in, n_out, 2])
        side.first(si, so, *sems)
        if side.mid is not None:
            side.mid(si, so, *sems)
        side.last(si, so, *sems)

    in_specs, out_specs = side.specs()
    return pl.pallas_call(body, name=name, in_specs=in_specs, out_specs=out_specs, out_shape=side.out_shapes,
                          scratch_shapes=side.sems())(*side.ins)


def _swap_first(srcs, dsts, ssems, rsems):
    x, y, c, _ = _place()
    for k, (s, d) in enumerate(zip(srcs, dsts)):
        _copy(s, d, ssems, rsems, k, (x, y, 1 - c)).start()


def _swap_last(srcs, dsts, ssems, rsems):
    x, y, c, _ = _place()
    for k, (s, d) in enumerate(zip(srcs, dsts)):
        _copy(s, d, ssems, rsems, k, (x, y, 1 - c)).wait()


def swap_side(xs):
    return Side(xs, [jax.ShapeDtypeStruct(a.shape, a.dtype) for a in xs], len(xs), _swap_first, _swap_last)


def allreduce_small(s):
    n_dev = 8

    def body(s_ref, out_ref, buf, send_sems, recv_sems):
        x, y, c, _ = _place()
        me = 4 * x + 2 * y + c
        buf[me] = s_ref[...]
        sends = []
        for k in range(1, n_dev):
            px = 1 - x if k & 4 else x
            py = 1 - y if k & 2 else y
            pc = 1 - c if k & 1 else c
            cp = _copy(s_ref, buf.at[me], send_sems, recv_sems, k - 1, (px, py, pc))
            cp.start()
            sends.append((cp, 4 * px + 2 * py + pc))
        for k, (cp, peer) in enumerate(sends):
            _copy(s_ref, buf.at[peer], send_sems, recv_sems, k, (x, y, c)).wait_recv()
        for cp, _ in sends:
            cp.wait_send()
        acc = buf[0]
        for d in range(1, n_dev):
            acc = acc + buf[d]
        out_ref[...] = acc

    vm = pl.BlockSpec(memory_space=pltpu.VMEM)
    return pl.pallas_call(
        body, name="allreduce_small", in_specs=[vm], out_specs=vm,
        out_shape=jax.ShapeDtypeStruct(s.shape, F32),
        scratch_shapes=[pltpu.VMEM((n_dev,) + s.shape, F32), pltpu.SemaphoreType.DMA((n_dev - 1,)),
                        pltpu.SemaphoreType.DMA((n_dev - 1,))],
    )(s)


def join_cols(sm):
    n4, k, n = sm.shape
    return sm.transpose(1, 0, 2).reshape(k, n4 * n)


def split_cols(full):
    k, n = full.shape
    return full.reshape(k, N_CHIPS, n // N_CHIPS).transpose(1, 0, 2)


def _pad_heads(w, width):
    lead = w.shape[:-1]
    w = w.reshape(lead + (N_HEADS, width))
    return jnp.pad(w, [(0, 0)] * len(lead) + [(0, 0), (0, LANES - width)]).reshape(lead + (HW,))


def _unpad_heads(w, width):
    lead = w.shape[:-1]
    return w.reshape(lead + (N_HEADS, LANES))[..., :width].reshape(lead + (N_HEADS * width,))


def split_w_in_t(w_in_t):
    d = w_in_t.shape[1]
    o_f = 3 * FW
    o_cq = o_f + N_HEADS
    o_ckv = o_cq + Q_RANK
    o_kr = o_ckv + KV_RANK

    def z(n):
        return jnp.zeros((n, d), w_in_t.dtype)

    small = jnp.concatenate([w_in_t[o_cq:o_ckv], w_in_t[o_ckv:o_kr], z(HEAD_DIM), w_in_t[o_kr:], z(LANES - HEAD_DIM - ROPE_DIM),
                             w_in_t[o_f:o_cq], z(LANES - N_HEADS), z(LANES)], axis=0)
    return w_in_t[:o_f], small


def join_w_in_t(d_qkv_t, d_small_t):
    kr = S_KR + HEAD_DIM
    return jnp.concatenate([d_qkv_t, d_small_t[S_F:S_F + N_HEADS], d_small_t[S_CQ:S_CKV], d_small_t[S_CKV:S_KR],
                            d_small_t[kr:kr + ROPE_DIM]], axis=0)


def rope_tables(pos):
    t = pos.shape[0]
    inv_freq = ROPE_THETA ** (-jnp.arange(0, ROPE_DIM, 2, dtype=F32) / ROPE_DIM)
    ang = pos.astype(F32)[:, None] * inv_freq
    cos, sin = jnp.cos(ang), jnp.sin(ang)
    half = ROPE_DIM // 2

    def z(n):
        return jnp.zeros((t, n), F32)

    tab_c = jnp.concatenate([jnp.ones((t, HEAD_DIM), F32), cos, cos, z(LANES - HEAD_DIM - ROPE_DIM)], axis=1)
    tab_a = jnp.concatenate([z(HEAD_DIM), -sin, z(half), z(LANES - HEAD_DIM - ROPE_DIM)], axis=1)
    tab_b = jnp.concatenate([z(HEAD_DIM), z(half), sin, z(LANES - HEAD_DIM - ROPE_DIM)], axis=1)
    return tab_c, tab_a, tab_b


def _pad_lanes(v, n):
    return jnp.pad(v, ((0, 0), (0, n - v.shape[1])))


ATTN_BLK = 512
ATTN_FWD_BLK = 1024
ATTN_BWD_QBLK = 1024


def local_step(xs, pos, tgt, gains, early_weights, late_weights, early_side=None, fwd_side=None, reduction=None):
    g_attn, b_forget, g_q, g_kv, g_fo, g_mo, g_mlp, g_fin = gains
    t = xs.shape[0]
    blk = min(ATTN_BLK, t)
    fox_scale = 1.0 / (HEAD_DIM ** 0.5)
    mla_scale = 1.0 / ((HEAD_DIM + ROPE_DIM) ** 0.5)

    h1, *gathered = rmsnorm(xs, g_attn, out_dtype=BF16, name="norm_attn", side=early_side)
    w_in_t, w_uq, w_ukv = early_weights(gathered)
    w_qkv_t, w_small_t = split_w_in_t(w_in_t)
    w_uq_p = _pad_heads(w_uq, HEAD_DIM + ROPE_DIM)
    kv = w_ukv.reshape(KV_RANK, N_HEADS, 2 * HEAD_DIM)
    w_ukv_p = jnp.concatenate([_pad_heads(kv[:, :, :HEAD_DIM].reshape(KV_RANK, FW), HEAD_DIM),
                               kv[:, :, HEAD_DIM:].reshape(KV_RANK, FW)], axis=1)
    b_f = _pad_lanes(b_forget, LANES)
    tab_c, tab_a, tab_b = rope_tables(pos)

    qkv, qkv_t = mm(h1, w_qkv_t, trans_b=True, out_dtypes=[BF16], t_blk=blk, name="proj_qkv")
    small, = mm(h1, w_small_t, trans_b=True, out_dtypes=[F32], name="proj_small")
    mq, mk, mv, lf, cqn, ckvn, mq_t, mv_t = mla_prep(small, g_q, g_kv, w_uq_p, w_ukv_p, tab_c, tab_a, tab_b, b_f,
                                                     name="mla_prep", bt=blk)
    f_cum = cumsum_rows(lf, reverse=False, name="gate_cumsum")
    f_blocks = f_cum[:, :N_HEADS].reshape(t // blk, blk, N_HEADS).transpose(0, 2, 1)
    fo, st_f, *gathered = flash_fwd(qkv_t, qkv, qkv_t, f_cum, qoff=0, koff=PAIRS, voff=2 * PAIRS, pair=True,
                                    scale=fox_scale, name="fox_fwd", blk=ATTN_FWD_BLK, side=fwd_side)
    w_o, w_up, w_down = late_weights(gathered)
    mo, st_m = flash_fwd(mq_t, mk, mv_t, None, qoff=0, koff=0, voff=0, pair=False, scale=mla_scale, name="mla_fwd",
                         blk=ATTN_FWD_BLK)
    mixed = mix_norm(fo, mo, g_fo, g_mo, name="norm_mix")

    def inv_rms(v):
        return lax.rsqrt(jnp.mean(v * v, axis=-1, keepdims=True) + EPS)

    def residual_then_norm(acc, res, g):
        xn = acc + res
        return xn, xn * inv_rms(xn) * g

    def norm_bwd(dh, xn, res, g):
        r = inv_rms(xn)
        uu = dh * g
        return (r * uu - xn * (r * r * r * jnp.mean(uu * xn, axis=-1, keepdims=True)) + res,
                jnp.sum(dh * (xn * r), axis=0, keepdims=True))

    def norm_bwd2(dh, xn, res, g):
        dx, dg = norm_bwd(dh, xn, res, g)
        return dx, dx, dg

    def residual_then_loss(acc, res, target, g):
        xn = acc + res
        r = inv_rms(xn)
        xh = xn * r
        e = xh * g - target
        part = 0.5 * jnp.sum(jnp.mean(e * e, axis=-1, keepdims=True), axis=0, keepdims=True)
        dy = e * (1.0 / xn.shape[1])
        uu = dy * g
        dx = r * uu - xn * (r * r * r * jnp.mean(uu * xn, axis=-1, keepdims=True))
        return dx, dx, jnp.sum(dy * xh, axis=0, keepdims=True), part + jnp.zeros_like(g)

    x1, h2 = mm(mixed, w_o, extras=[xs], vecs=[g_mlp], epilogue=residual_then_norm, out_dtypes=[F32, BF16], name="out_proj")

    def relu2(uu):
        r = jnp.maximum(uu.astype(F32), 0.0)
        return (r * r).astype(BF16)

    u, = mm(h2, w_up, out_dtypes=[BF16], name="mlp_up")
    dx2, dx2b, dg_fin, loss_row = mm(u, w_down, a_pro=relu2, extras=[x1, tgt], vecs=[g_fin], epilogue=residual_then_loss,
                                     out_dtypes=[F32, BF16], n_sums=2, name="mlp_down_loss")
    loss = loss_row[:, :1]

    def relu2_grad(acc, uu):
        return (acc * (2.0 * jnp.maximum(uu.astype(F32), 0.0)),)

    du, = mm(dx2b, w_down, trans_b=True, extras=[u], epilogue=relu2_grad, out_dtypes=[BF16], name="mlp_down_bwd")
    dw_down = mm_tn(u, dx2b, a_pro=relu2, name="dw_down").reshape(N_CHIPS, -1, w_down.shape[1])
    dx1, dx1b, dg_mlp = mm(du, w_up, trans_b=True, extras=[x1, dx2], vecs=[g_mlp], epilogue=norm_bwd2,
                           out_dtypes=[F32, BF16], n_sums=1, name="mlp_up_bwd")
    dw_up = mm_tn(h2, du, name="dw_up", col_shards=N_CHIPS)

    dmixed, = mm(dx1b, w_o, trans_b=True, out_dtypes=[F32], name="out_proj_bwd")
    dw_o = mm_tn(mixed, dx1b, name="dw_o").reshape(N_CHIPS, -1, w_o.shape[1])
    late = (dw_o, dw_up, dw_down)
    red = reduction
    dfo, dmo, dg_fo, dg_mo, st_f, st_m, dfo_t, dmo_t, *got = mix_norm_bwd(
        dmixed, fo, mo, g_fo, g_mo, st_f, st_m, name="norm_mix_bwd", bt=blk, side=red.late_swap(late) if red else None)
    dfq, dfk, dfv, d_f, *got = flash_bwd(
        qkv, qkv_t, qkv, qkv, dfo, dfo_t, st_f, f_blocks, qoff=0, koff=PAIRS, voff=2 * PAIRS, pair=True,
        scale=fox_scale, name="fox_bwd", qblk=ATTN_BWD_QBLK, side=red.late_scatter(got) if red else None)
    dmq, dmk, dmv, *got = flash_bwd(mq, mq_t, mk, mv, dmo, dmo_t, st_m, None, qoff=0, koff=0, voff=0,
                                    pair=False, scale=mla_scale, name="mla_bwd", qblk=ATTN_BWD_QBLK,
                                    side=red.late_halves(got) if red else None)
    if red:
        red.late_done(got)
    dqkv = jnp.concatenate([dfq, dfk, dfv], axis=1).astype(BF16)
    dlf = cumsum_rows(d_f, reverse=True, name="gate_cumsum_bwd")
    dsmall, dq_u, dkv_u, dg_q, dg_kv, db_f = mla_prep_bwd(dmq, dmk, dmv, dlf, small, g_q, g_kv, w_uq_p, w_ukv_p,
                                                          tab_c, tab_a, tab_b, b_f, name="mla_prep_bwd")
    dw_uq_p = mm_tn(cqn, dq_u, name="dw_uq")
    dw_ukv_p = mm_tn(ckvn, dkv_u, name="dw_ukv")

    dw_in_t = join_w_in_t(mm_tn(dqkv, h1, name="dw_qkv"), mm_tn(dsmall, h1, name="dw_small"))
    dw_uq = _unpad_heads(dw_uq_p, HEAD_DIM + ROPE_DIM)
    dk_cols = _unpad_heads(dw_ukv_p[:, :HW], HEAD_DIM).reshape(KV_RANK, N_HEADS, HEAD_DIM)
    dv_cols = dw_ukv_p[:, HW:].reshape(KV_RANK, N_HEADS, HEAD_DIM)
    dw_ukv = jnp.concatenate([dk_cols, dv_cols], axis=2).reshape(KV_RANK, N_HEADS * 2 * HEAD_DIM)
    early = (dw_in_t.reshape(N_CHIPS, -1, dw_in_t.shape[1]), split_cols(dw_uq), split_cols(dw_ukv))
    grad_x, dg_attn, *got = mm([dqkv, dsmall], [w_qkv_t, w_small_t], extras=[xs, dx1], vecs=[g_attn],
                               epilogue=norm_bwd, out_dtypes=[F32], n_sums=1, name="proj_bwd",
                               side=red.early_scatter(early) if red else None)
    if red:
        red.early_done(got)
    d_gains = (dg_attn, db_f[:, :N_HEADS], dg_q, dg_kv, dg_fo, dg_mo, dg_mlp, dg_fin)
    return loss, grad_x, early, late, d_gains


class GradReduction:
    def __init__(self, core_id, chip):
        self.core_id, self.chip = core_id, chip
        self.core = core_id.reshape(1).astype(jnp.int32)
        self.grads, self.pairs, self.halves, self.others = {}, {}, {}, {}

    def _other_halves(self, grads):
        out = []
        for g in grads:
            axis = 1 + split_axis(g.shape[1])
            size = g.shape[axis] // 2
            out.append(lax.dynamic_slice_in_dim(g, (1 - self.core_id) * size, size, axis=axis).astype(BF16))
        return out

    def _add_pairs(self, group, recvs):
        self.pairs[group] = [add_pair(g, r, self.core, name="add_pair_%s_%d" % (group, n))
                             for n, (g, r) in enumerate(zip(self.grads[group], recvs))]
        return scatter_side(self.pairs[group])

    def _chip_sums(self, group, scattered):
        chip = self.chip
        with_mine = [lax.dynamic_update_index_in_dim(s, lax.dynamic_index_in_dim(p, chip, 0, keepdims=True), chip, 0)
                     for s, p in zip(scattered, self.pairs[group])]
        self.halves[group] = [sum_chips(s, name="sum_chips_%s_%d" % (group, n)) for n, s in enumerate(with_mine)]
        return self.halves[group]

    def late_swap(self, grads):
        self.grads["late"] = list(grads)
        return swap_side(self._other_halves(grads))

    def late_scatter(self, recvs):
        return self._add_pairs("late", recvs)

    def late_halves(self, scattered):
        return swap_side(self._chip_sums("late", scattered))

    def late_done(self, others):
        self.others["late"] = list(others)

    def early_scatter(self, grads):
        self.grads["early"] = list(grads)
        return self._add_pairs("early", run_side(swap_side(self._other_halves(grads)), name="swap_early_sends"))

    def early_done(self, scattered):
        self.others["early"] = list(run_side(swap_side(self._chip_sums("early", scattered)), name="swap_early_halves"))

def kernel(x, positions, attn_norm_g, w_in, b_forget, q_norm_g, w_uq, kv_norm_g, w_ukv, fox_out_g, mla_out_g, w_o, mlp_norm_g, w_up, w_down, final_norm_g, loss_target, m_attn_norm_g, m_w_in, m_b_forget, m_q_norm_g, m_w_uq, m_kv_norm_g, m_w_ukv, m_fox_out_g, m_mla_out_g, m_w_o, m_mlp_norm_g, m_w_up, m_w_down, m_final_norm_g, v_attn_norm_g, v_w_in, v_b_forget, v_q_norm_g, v_w_uq, v_kv_norm_g, v_w_ukv, v_fox_out_g, v_mla_out_g, v_w_o, v_mlp_norm_g, v_w_up, v_w_down, v_final_norm_g):
    core_id = lax.axis_index("c")
    core = core_id.reshape(1).astype(jnp.int32)
    chip = 2 * lax.axis_index("x") + lax.axis_index("y")
    big = [w_in, w_uq, w_ukv, w_o, w_up, w_down]
    big_m = [m_w_in, m_w_uq, m_w_ukv, m_w_o, m_w_up, m_w_down]
    big_v = [v_w_in, v_w_uq, v_w_ukv, v_w_o, v_w_up, v_w_down]
    n_early = 3

    def vec(a):
        return a.reshape(1, -1)

    small = [attn_norm_g, b_forget, q_norm_g, kv_norm_g, fox_out_g, mla_out_g, mlp_norm_g, final_norm_g]
    small_m = [m_attn_norm_g, m_b_forget, m_q_norm_g, m_kv_norm_g, m_fox_out_g, m_mla_out_g, m_mlp_norm_g, m_final_norm_g]
    small_v = [v_attn_norm_g, v_b_forget, v_q_norm_g, v_kv_norm_g, v_fox_out_g, v_mla_out_g, v_mlp_norm_g, v_final_norm_g]
    gains = [vec(a) for a in small]

    shards = [(w[0].T if n == 0 else w[0]).astype(BF16) for n, w in enumerate(big)]

    def with_own(gathered, mine):
        return [lax.dynamic_update_index_in_dim(g, s, chip, 0) for g, s in zip(gathered, mine)]

    def early_weights(gathered):
        g_in, g_uq, g_ukv = with_own(gathered, shards[:n_early])
        return g_in.reshape(-1, g_in.shape[2]), join_cols(g_uq), join_cols(g_ukv)

    def late_weights(gathered):
        g_o, g_up, g_down = with_own(gathered, shards[n_early:])
        return g_o.reshape(-1, g_o.shape[2]), join_cols(g_up), g_down.reshape(-1, g_down.shape[2])

    reduction = GradReduction(core_id, chip)
    loss, grad_x, _, _, d_small = local_step(
        x[0], positions[0], loss_target[0], gains, early_weights, late_weights, gather_side(shards[:n_early]),
        gather_side(shards[n_early:]), reduction)
    halves = reduction.halves["early"] + reduction.halves["late"]
    others = reduction.others["early"] + reduction.others["late"]

    def rows8(vs):
        return jnp.concatenate([_pad_lanes(vec(a).astype(F32), 1024) for a in vs], axis=0)

    with_loss = [jnp.concatenate([d, loss], axis=1) if n == 1 else d for n, d in enumerate(d_small)]
    g_small8 = allreduce_small(rows8(with_loss))

    outs_big = []
    for n, (w, gm, go, m, v) in enumerate(zip(big, halves, others, big_m, big_v)):
        _, k, cols = w.shape
        if n == 0:
            outs = adamw_halves_t(w[0].T, gm, go, m[0].T, v[0].T, core, name="adamw_%d" % n)
            outs_big.append([o.T[None] for o in outs])
        else:
            outs_big.append(adamw_halves(w, gm, go, m, v, core, name="adamw_%d" % n))
    d8, m8, v8 = adamw(rows8(small), g_small8, rows8(small_m), rows8(small_v), name="adamw_small")

    def unrows8(a8):
        return [a8[n, :s.size].reshape(s.shape) for n, s in enumerate(small)]

    loss_all = g_small8[1, N_HEADS]
    grads, deltas, new_m, new_v = [None] * 14, [None] * 14, [None] * 14, [None] * 14
    big_at = [1, 4, 6, 9, 11, 12]
    small_at = [0, 2, 3, 5, 7, 8, 10, 13]
    for n, at in enumerate(big_at):
        grads[at], deltas[at], new_m[at], new_v[at] = outs_big[n]
    for at, g, dd, mm_, vv in zip(small_at, unrows8(g_small8), unrows8(d8), unrows8(m8), unrows8(v8)):
        grads[at], deltas[at], new_m[at], new_v[at] = g, dd, mm_, vv
    return (loss_all, grad_x[None], *grads, *deltas, *new_m, *new_v)
```

```python
import jax
import jax.numpy as jnp
from jax import lax
from jax.experimental import pallas as pl
from jax.experimental.pallas import tpu as pltpu

F32 = jnp.float32
BF16 = jnp.bfloat16
MESH = pl.DeviceIdType.MESH

EPS = 1e-6
ROPE_THETA = 10000.0
N_HEADS = 8
PAIRS = N_HEADS // 2
HEAD_DIM = 64
ROPE_DIM = 32
LANES = 128
Q_RANK = 384
KV_RANK = 256
N_CHIPS = 4
ADAM_LR, ADAM_B1, ADAM_B2, ADAM_EPS, ADAM_WD, ADAM_STEP = 0.001, 0.9, 0.999, 1e-08, 0.01, 10
VMEM_LIMIT = 48 * 1024 * 1024
LOG2E = 1.4426950408889634
LN2 = 0.6931471805599453
NN = (((1,), (0,)), ((), ()))
NT = (((1,), (1,)), ((), ()))
TN = (((0,), (0,)), ((), ()))


def _params(sem=None):
    return pltpu.CompilerParams(dimension_semantics=sem, vmem_limit_bytes=VMEM_LIMIT)


def _fit(block, dim):
    if dim <= block:
        return dim
    return next(b for b in range(block - block % LANES, 0, -LANES) if dim % b == 0)


def _row_block(rows):
    return next(b for b in (256, 128, 64, 32, 16, 8) if rows % b == 0)


def gridded(body, *, name, grid, in_specs, out_specs, out_shape, ins, semantics, side=None):
    if side is None:
        return pl.pallas_call(body, name=name, grid=grid, in_specs=in_specs, out_specs=out_specs, out_shape=out_shape,
                              compiler_params=_params(semantics))(*ins)
    n_in, n_out = len(in_specs), len(out_specs)
    steps = 1
    for extent in grid:
        steps *= extent

    def riding(*refs):
        main_in, s_in, main_out, s_out, sems = _split_refs(refs, [n_in, len(side.ins), n_out, len(side.out_shapes), 2])
        step = 0
        for axis, extent in enumerate(grid):
            step = step * extent + pl.program_id(axis)

        @pl.when(step == 0)
        def _():
            side.first(s_in, s_out, *sems)

        body(*main_in, *main_out)

        @pl.when(step == steps - 1)
        def _():
            if side.mid is not None:
                side.mid(s_in, s_out, *sems)
            side.last(s_in, s_out, *sems)

    s_in_specs, s_out_specs = side.specs()
    return pl.pallas_call(
        riding, name=name, grid=grid, in_specs=list(in_specs) + s_in_specs, out_specs=list(out_specs) + s_out_specs,
        out_shape=list(out_shape) + side.out_shapes, scratch_shapes=side.sems(),
        compiler_params=_params(("arbitrary",) * len(grid)))(*ins, *side.ins)


def rmsnorm(x, g, *, out_dtype, name, bt=512, side=None):
    t, d = x.shape
    bt = min(bt, t)

    def body(x_ref, g_ref, o_ref):
        xv = x_ref[...].astype(F32)
        r = lax.rsqrt(jnp.mean(xv * xv, axis=-1, keepdims=True) + EPS)
        o_ref[...] = (xv * r * g_ref[...]).astype(o_ref.dtype)

    return gridded(
        body, name=name, grid=(t // bt,),
        in_specs=[pl.BlockSpec((bt, d), lambda i: (i, 0)), pl.BlockSpec((1, d), lambda i: (0, 0))],
        out_specs=[pl.BlockSpec((bt, d), lambda i: (i, 0))],
        out_shape=[jax.ShapeDtypeStruct((t, d), out_dtype)],
        ins=[x, g], semantics=("parallel",), side=side)


def mm(a, b, *, trans_b=False, a_pro=None, extras=(), vecs=(), epilogue=None, out_dtypes, n_sums=0, t_blk=None, name,
       bm=1024, bn=1024, side=None):
    a_list = list(a) if isinstance(a, (list, tuple)) else [a]
    b_list = list(b) if isinstance(b, (list, tuple)) else [b]
    m = a_list[0].shape[0]
    n = b_list[0].shape[0] if trans_b else b_list[0].shape[1]
    ks = [x.shape[1] for x in a_list]
    if sum(ks) > 2048:
        bm = bm // 2
    bm, bn = _fit(bm, m), _fit(bn, n)
    assert n_sums == 0 or bn == n
    n_ab, n_ex, n_vec, n_out = len(a_list), len(extras), len(vecs), len(out_dtypes)
    n_t = 0 if t_blk is None else 1

    def body(*refs):
        a_refs, b_refs, ex, vs, outs, t_outs, sums = _split_refs(refs, [n_ab, n_ab, n_ex, n_vec, n_out, n_t, n_sums])
        acc = None
        for a_ref, b_ref in zip(a_refs, b_refs):
            a_tile = a_ref[...] if a_pro is None else a_pro(a_ref[...])
            part = lax.dot_general(a_tile, b_ref[...], NT if trans_b else NN, preferred_element_type=F32)
            acc = part if acc is None else acc + part
        res = epilogue(acc, *[e[...] for e in ex], *[v[...] for v in vs]) if epilogue is not None else (acc,)
        for o, r in zip(outs, res[:n_out]):
            o[...] = r.astype(o.dtype)
        for t_ref in t_outs:
            for u in range(bm // t_blk):
                t_ref[u] = res[0][u * t_blk:(u + 1) * t_blk, :].T.astype(t_ref.dtype)
        if n_sums:
            @pl.when(pl.program_id(0) == 0)
            def _():
                for s_ref in sums:
                    s_ref[...] = jnp.zeros_like(s_ref)

            for s_ref, r in zip(sums, res[n_out:]):
                s_ref[...] += r

    tile = pl.BlockSpec((bm, bn), lambda i, j: (i, j))
    vec = pl.BlockSpec((1, bn), lambda i, j: (0, j))
    t_specs, t_shapes = [], []
    if t_blk is not None:
        t_specs = [pl.BlockSpec((bm // t_blk, bn, t_blk), lambda i, j: (i, j, 0))]
        t_shapes = [jax.ShapeDtypeStruct((m // t_blk, n, t_blk), out_dtypes[0])]
    a_specs = [pl.BlockSpec((bm, k), lambda i, j: (i, 0)) for k in ks]
    b_specs = [pl.BlockSpec((bn, k), lambda i, j: (j, 0)) if trans_b else pl.BlockSpec((k, bn), lambda i, j: (0, j)) for k in ks]
    return gridded(
        body, name=name, grid=(m // bm, n // bn),
        in_specs=a_specs + b_specs + [tile] * n_ex + [vec] * n_vec,
        out_specs=[tile] * n_out + t_specs + [vec] * n_sums,
        out_shape=[jax.ShapeDtypeStruct((m, n), dt) for dt in out_dtypes] + t_shapes + [jax.ShapeDtypeStruct((1, n), F32)] * n_sums,
        ins=[*a_list, *b_list, *extras, *vecs],
        semantics=("arbitrary", "arbitrary") if n_sums else ("parallel", "parallel"), side=side)


def mm_tn(a, b, *, a_pro=None, name, col_shards=1, bk=1024, bn=1024, bt=1024):
    t, k = a.shape
    n = b.shape[1]
    ns = n // col_shards
    bk, bn, bt = _fit(bk, k), _fit(bn, ns), _fit(bt, t)
    per = ns // bn

    def body(a_ref, b_ref, o_ref):
        @pl.when(pl.program_id(2) == 0)
        def _():
            o_ref[...] = jnp.zeros_like(o_ref)

        a_tile = a_ref[...] if a_pro is None else a_pro(a_ref[...])
        o_ref[...] += lax.dot_general(a_tile, b_ref[...], TN, preferred_element_type=F32)

    if col_shards == 1:
        out_spec = pl.BlockSpec((bk, bn), lambda i, j, s: (i, j))
        out_shape = jax.ShapeDtypeStruct((k, n), F32)
    else:
        out_spec = pl.BlockSpec((None, bk, bn), lambda i, j, s: (j // per, i, j % per))
        out_shape = jax.ShapeDtypeStruct((col_shards, k, ns), F32)
    return pl.pallas_call(
        body, name=name, grid=(k // bk, n // bn, t // bt),
        in_specs=[pl.BlockSpec((bt, bk), lambda i, j, s: (s, i)), pl.BlockSpec((bt, bn), lambda i, j, s: (s, j))],
        out_specs=out_spec, out_shape=out_shape,
        compiler_params=_params(("parallel", "parallel", "arbitrary")),
    )(a, b)


def _split3(x):
    hi = x.astype(BF16)
    r1 = x - hi.astype(F32)
    mid = r1.astype(BF16)
    lo = (r1 - mid.astype(F32)).astype(BF16)
    return hi, mid, lo


def cumsum_rows(x, *, reverse, name, bc=512):
    t, d = x.shape
    bc = min(bc, t)
    nb = t // bc

    def body(x_ref, o_ref, carry):
        @pl.when(pl.program_id(0) == 0)
        def _():
            carry[...] = jnp.zeros_like(carry)

        r = lax.broadcasted_iota(jnp.int32, (bc, bc), 0)
        c = lax.broadcasted_iota(jnp.int32, (bc, bc), 1)
        tri = jnp.where((r <= c) if reverse else (r >= c), 1.0, 0.0).astype(BF16)
        hi, mid, lo = _split3(x_ref[...])
        s = (lax.dot_general(tri, hi, NN, preferred_element_type=F32)
             + lax.dot_general(tri, mid, NN, preferred_element_type=F32)
             + lax.dot_general(tri, lo, NN, preferred_element_type=F32)) + carry[0:1, :]
        o_ref[...] = s
        carry[0:1, :] = s[0:1, :] if reverse else s[bc - 1:bc, :]

    imap = (lambda i: (nb - 1 - i, 0)) if reverse else (lambda i: (i, 0))
    return pl.pallas_call(
        body, name=name, grid=(nb,),
        in_specs=[pl.BlockSpec((bc, d), imap)], out_specs=pl.BlockSpec((bc, d), imap),
        out_shape=jax.ShapeDtypeStruct((t, d), F32),
        scratch_shapes=[pltpu.VMEM((8, d), F32)],
        compiler_params=_params(("arbitrary",)),
    )(x)


def _rope(x, c, a, b):
    return x * c + pltpu.roll(x, LANES - ROPE_DIM // 2, 1) * a + pltpu.roll(x, ROPE_DIM // 2, 1) * b


def _rope_bwd(d, c, a, b):
    return d * c + pltpu.roll(d * a, ROPE_DIM // 2, 1) + pltpu.roll(d * b, LANES - ROPE_DIM // 2, 1)


S_CQ, S_CKV, S_KR, S_F, S_END = 0, Q_RANK, Q_RANK + KV_RANK, Q_RANK + KV_RANK + LANES, 1024
HW = N_HEADS * LANES
FW = N_HEADS * HEAD_DIM


def mla_prep(small, g_q, g_kv, w_uq, w_ukv, tab_c, tab_a, tab_b, b_f, *, name, bt=512):
    t = small.shape[0]
    bt = min(bt, t)

    def body(s_ref, gq_ref, gkv_ref, wq_ref, wkv_ref, c_ref, a_ref, b_ref, bf_ref,
             mq_ref, mk_ref, mv_ref, lf_ref, cqn_ref, ckvn_ref, mqt_ref, mvt_ref):
        cq = s_ref[:, S_CQ:S_CKV]
        rq = lax.rsqrt(jnp.mean(cq * cq, axis=-1, keepdims=True) + EPS)
        cqn = (cq * rq * gq_ref[...]).astype(BF16)
        ckv = s_ref[:, S_CKV:S_KR]
        rkv = lax.rsqrt(jnp.mean(ckv * ckv, axis=-1, keepdims=True) + EPS)
        ckvn = (ckv * rkv * gkv_ref[...]).astype(BF16)
        cqn_ref[...] = cqn
        ckvn_ref[...] = ckvn
        tc, ta, tb = c_ref[...], a_ref[...], b_ref[...]
        q = jnp.dot(cqn, wq_ref[...], preferred_element_type=F32)
        kv = jnp.dot(ckvn, wkv_ref[...], preferred_element_type=F32)
        kr = _rope(s_ref[:, S_KR:S_F], tc, ta, tb)
        for h in range(N_HEADS):
            sl = slice(h * LANES, (h + 1) * LANES)
            roped = _rope(q[:, sl], tc, ta, tb)
            mq_ref[:, sl] = roped.astype(BF16)
            mqt_ref[0, sl, :] = roped.T.astype(BF16)
            mk_ref[:, sl] = (kv[:, sl] + kr).astype(BF16)
        mv_ref[...] = kv[:, HW:].astype(BF16)
        mvt_ref[0] = kv[:, HW:].T.astype(BF16)
        z = s_ref[:, S_F:S_END - LANES] + bf_ref[...]
        lf_ref[...] = jnp.minimum(z, 0.0) - jnp.log(1.0 + jnp.exp(-jnp.abs(z)))

    def row(w):
        return pl.BlockSpec((bt, w), lambda i: (i, 0))

    def full(arr):
        return pl.BlockSpec(arr.shape, lambda i: (0, 0))

    return pl.pallas_call(
        body, name=name, grid=(t // bt,),
        in_specs=[row(S_END), full(g_q), full(g_kv), full(w_uq), full(w_ukv), row(LANES), row(LANES), row(LANES), full(b_f)],
        out_specs=[row(HW), row(HW), row(FW), row(LANES), row(Q_RANK), row(KV_RANK),
                   pl.BlockSpec((1, HW, bt), lambda i: (i, 0, 0)), pl.BlockSpec((1, FW, bt), lambda i: (i, 0, 0))],
        out_shape=[jax.ShapeDtypeStruct((t, HW), BF16)] * 2 + [jax.ShapeDtypeStruct((t, FW), BF16), jax.ShapeDtypeStruct((t, LANES), F32),
                   jax.ShapeDtypeStruct((t, Q_RANK), BF16), jax.ShapeDtypeStruct((t, KV_RANK), BF16),
                   jax.ShapeDtypeStruct((t // bt, HW, bt), BF16), jax.ShapeDtypeStruct((t // bt, FW, bt), BF16)],
        compiler_params=_params(("parallel",)),
    )(small, g_q, g_kv, w_uq, w_ukv, tab_c, tab_a, tab_b, b_f)


def mla_prep_bwd(dmq, dmk, dmv, dlf, small, g_q, g_kv, w_uq, w_ukv, tab_c, tab_a, tab_b, b_f, *, name, bt=512):
    t = small.shape[0]
    bt = min(bt, t)

    def body(dmq_ref, dmk_ref, dmv_ref, dlf_ref, s_ref, gq_ref, gkv_ref, wq_ref, wkv_ref, c_ref, a_ref, b_ref, bf_ref,
             ds_ref, dq_ref, dkv_ref, dgq_ref, dgkv_ref, db_ref):
        tc, ta, tb = c_ref[...], a_ref[...], b_ref[...]
        lane = lax.broadcasted_iota(jnp.int32, (1, LANES), 1)
        dkr = jnp.zeros((bt, LANES), F32)
        for h in range(N_HEADS):
            sl = slice(h * LANES, (h + 1) * LANES)
            dq_ref[:, sl] = _rope_bwd(dmq_ref[:, sl], tc, ta, tb).astype(BF16)
            dkr = dkr + dmk_ref[:, sl]
        dkv_ref[:, :HW] = dmk_ref[...].astype(BF16)
        dkv_ref[:, HW:] = dmv_ref[...].astype(BF16)
        in_rope = (lane >= HEAD_DIM) & (lane < HEAD_DIM + ROPE_DIM)
        ds_ref[:, S_KR:S_F] = jnp.where(in_rope, _rope_bwd(dkr, tc, ta, tb), 0.0).astype(BF16)

        def norm_bwd(raw, g_ref, dn, dg_ref):
            r = lax.rsqrt(jnp.mean(raw * raw, axis=-1, keepdims=True) + EPS)
            u = dn * g_ref[...]
            dot = jnp.mean(u * raw, axis=-1, keepdims=True)
            dg_ref[...] += jnp.sum(dn * (raw * r), axis=0, keepdims=True)
            return r * u - raw * (r * r * r * dot)

        @pl.when(pl.program_id(0) == 0)
        def _():
            dgq_ref[...] = jnp.zeros_like(dgq_ref)
            dgkv_ref[...] = jnp.zeros_like(dgkv_ref)
            db_ref[...] = jnp.zeros_like(db_ref)

        dcqn = lax.dot_general(dq_ref[...], wq_ref[...], NT, preferred_element_type=F32)
        ds_ref[:, S_CQ:S_CKV] = norm_bwd(s_ref[:, S_CQ:S_CKV], gq_ref, dcqn, dgq_ref).astype(BF16)
        dckvn = lax.dot_general(dkv_ref[...], wkv_ref[...], NT, preferred_element_type=F32)
        ds_ref[:, S_CKV:S_KR] = norm_bwd(s_ref[:, S_CKV:S_KR], gkv_ref, dckvn, dgkv_ref).astype(BF16)
        z = s_ref[:, S_F:S_END - LANES] + bf_ref[...]
        dz = jnp.where(lane < N_HEADS, dlf_ref[...] / (1.0 + jnp.exp(z)), 0.0)
        db_ref[...] += jnp.sum(dz, axis=0, keepdims=True)
        ds_ref[:, S_F:S_END - LANES] = dz.astype(BF16)
        ds_ref[:, S_END - LANES:] = jnp.zeros((bt, LANES), BF16)

    def row(w):
        return pl.BlockSpec((bt, w), lambda i: (i, 0))

    def full(arr):
        return pl.BlockSpec(arr.shape, lambda i: (0, 0))

    def vec(w):
        return pl.BlockSpec((1, w), lambda i: (0, 0))

    return pl.pallas_call(
        body, name=name, grid=(t // bt,),
        in_specs=[row(HW), row(HW), row(FW), row(LANES), row(S_END), full(g_q), full(g_kv), full(w_uq), full(w_ukv),
                  row(LANES), row(LANES), row(LANES), full(b_f)],
        out_specs=[row(S_END), row(HW), row(HW + FW), vec(Q_RANK), vec(KV_RANK), vec(LANES)],
        out_shape=[jax.ShapeDtypeStruct((t, S_END), BF16), jax.ShapeDtypeStruct((t, HW), BF16),
                   jax.ShapeDtypeStruct((t, HW + FW), BF16), jax.ShapeDtypeStruct((1, Q_RANK), F32),
                   jax.ShapeDtypeStruct((1, KV_RANK), F32), jax.ShapeDtypeStruct((1, LANES), F32)],
        compiler_params=_params(("arbitrary",)),
    )(dmq, dmk, dmv, dlf, small, g_q, g_kv, w_uq, w_ukv, tab_c, tab_a, tab_b, b_f)


class Side:
    def __init__(self, ins, out_shapes, n_sems, first, last, mid=None):
        self.ins, self.out_shapes, self.n_sems = list(ins), list(out_shapes), n_sems
        self.first, self.mid, self.last = first, mid, last

    def specs(self):
        return [ANY] * len(self.ins), [ANY] * len(self.out_shapes)

    def sems(self):
        return [pltpu.SemaphoreType.DMA((self.n_sems,)), pltpu.SemaphoreType.DMA((self.n_sems,))]


def _lane():
    return lax.broadcasted_iota(jnp.int32, (1, LANES), 1)


def _halves(x):
    zero = jnp.zeros_like(x)
    return [jnp.where(_lane() < HEAD_DIM, x, zero), jnp.where(_lane() >= HEAD_DIM, x, zero)]


def _groups(x):
    return [x[:, :LANES], x[:, LANES:]]


def _pick_row(tile, h):
    row = lax.broadcasted_iota(jnp.int32, (tile.shape[0], 1), 0)
    return jnp.sum(jnp.where(row == h, tile, 0.0), axis=0, keepdims=True)


def _pick_lane(tile, h):
    return jnp.sum(jnp.where(_lane() == h, tile, 0.0), axis=1, keepdims=True)


def _row_halves(x):
    row = lax.broadcasted_iota(jnp.int32, (LANES, 1), 0)
    zero = jnp.zeros_like(x)
    return [jnp.where(row < HEAD_DIM, x, zero), jnp.where(row >= HEAD_DIM, x, zero)]


def _below_diagonal(s, lead=0):
    r = lax.broadcasted_iota(jnp.int32, s.shape, 0)
    c = lax.broadcasted_iota(jnp.int32, s.shape, 1)
    return jnp.where(c <= r + lead, s, -jnp.inf)


def _above_diagonal(s):
    r = lax.broadcasted_iota(jnp.int32, s.shape, 0)
    c = lax.broadcasted_iota(jnp.int32, s.shape, 1)
    return jnp.where(r <= c, s, -jnp.inf)


def _split_refs(refs, counts):
    out, at = [], 0
    for n in counts:
        out.append(refs[at:at + n])
        at += n
    return out


def flash_fwd(qt_arr, k_arr, vt_arr, f_cum, *, qoff, koff, voff, pair, scale, name, blk=512, side=None):
    t = k_arr.shape[0]
    tblk = qt_arr.shape[2]
    blk = max(min(blk, t), tblk)
    sub = blk // tblk
    nb = t // blk
    w = LANES if pair else 2 * LANES
    has_bias = f_cum is not None
    ins = [qt_arr, k_arr, vt_arr] + ([f_cum] if has_bias else [])

    def wide(ref, first):
        parts = [ref[first + u] for u in range(sub)]
        return parts[0] if sub == 1 else jnp.concatenate(parts, axis=1)
    s_ins, s_outs = (side.ins, side.out_shapes) if side else ([], [])

    def body(*refs):
        main, si, outs, so, sems = _split_refs(refs, [len(ins), len(s_ins), 2, len(s_outs), 2 if side else 0])
        qt_ref, k_ref, vt_ref = main[:3]
        f_ref = main[3] if has_bias else None
        o_ref, st_ref = outs
        g, i = pl.program_id(0), pl.program_id(1)
        step_id = g * nb + i
        if side:
            @pl.when(step_id == 0)
            def _():
                side.first(si, so, *sems)

            if side.mid is not None:
                @pl.when(step_id == (3 * PAIRS * nb) // 4)
                def _():
                    side.mid(si, so, *sems)

        qt = (wide(qt_ref, 0).astype(F32) * (scale * LOG2E)).astype(BF16)
        qts = _row_halves(qt) if pair else [qt[:LANES], qt[LANES:]]

        def step(j, carry, diagonal):
            rows = pl.ds(pl.multiple_of(j * blk, blk), blk)
            kk = k_ref[rows, :]
            ks = [kk, kk] if pair else _groups(kk)
            vt = wide(vt_ref, sub * j)
            out = []
            for n in range(2):
                m, l, acc = carry[n]
                s = jnp.dot(ks[n], qts[n], preferred_element_type=F32)
                if has_bias:
                    s = s - LOG2E * _pick_lane(f_ref[rows, :], 2 * g + n)
                if diagonal:
                    s = _above_diagonal(s)
                m_new = jnp.maximum(m, jnp.max(s, axis=0, keepdims=True))
                alpha = jnp.exp2(m - m_new)
                p = jnp.exp2(s - m_new)
                out.append((m_new, alpha * l + jnp.sum(p, axis=0, keepdims=True),
                            alpha * acc + jnp.dot(vt[n * HEAD_DIM:(n + 1) * HEAD_DIM], p.astype(BF16),
                                                  preferred_element_type=F32)))
            return tuple(out)

        def diagonal_in_halves(carry):
            h = tblk
            halves = [pl.ds(pl.multiple_of(i * blk, blk), h), pl.ds(pl.multiple_of(i * blk + h, h), h)]
            k_top, k_bot = ([x, x] if pair else _groups(x) for x in (k_ref[halves[0], :], k_ref[halves[1], :]))
            vt_top, vt_bot = vt_ref[sub * i], vt_ref[sub * i + 1]
            out = []
            for n in range(2):
                m, l, acc = carry[n]
                heads = slice(n * HEAD_DIM, (n + 1) * HEAD_DIM)
                s_top = jnp.dot(k_top[n], qts[n], preferred_element_type=F32)
                s_bot = jnp.dot(k_bot[n], qts[n][:, h:], preferred_element_type=F32)
                if has_bias:
                    s_top = s_top - LOG2E * _pick_lane(f_ref[halves[0], :], 2 * g + n)
                    s_bot = s_bot - LOG2E * _pick_lane(f_ref[halves[1], :], 2 * g + n)
                s_top, s_bot = _above_diagonal(s_top), _above_diagonal(s_bot)
                m_top = jnp.maximum(m, jnp.max(s_top, axis=0, keepdims=True))
                m_new = jnp.concatenate([m_top[:, :h], jnp.maximum(m_top[:, h:], jnp.max(s_bot, axis=0, keepdims=True))], axis=1)
                alpha = jnp.exp2(m - m_new)
                p_top = jnp.exp2(s_top - m_new)
                p_bot = jnp.exp2(s_bot - m_new[:, h:])
                late_l = jnp.concatenate([jnp.zeros((1, h), F32), jnp.sum(p_bot, axis=0, keepdims=True)], axis=1)
                late_acc = jnp.concatenate([jnp.zeros((HEAD_DIM, h), F32),
                                            jnp.dot(vt_bot[heads], p_bot.astype(BF16), preferred_element_type=F32)], axis=1)
                out.append((m_new, alpha * l + jnp.sum(p_top, axis=0, keepdims=True) + late_l,
                            alpha * acc + jnp.dot(vt_top[heads], p_top.astype(BF16), preferred_element_type=F32) + late_acc))
            return tuple(out)

        init = tuple((jnp.full((1, blk), -jnp.inf, F32), jnp.zeros((1, blk), F32), jnp.zeros((HEAD_DIM, blk), F32))
                     for _ in range(2))
        carry = lax.fori_loop(0, i, lambda j, c: step(j, c, False), init)
        (ma, la, acca), (mb, lb, accb) = diagonal_in_halves(carry) if sub == 2 else step(i, carry, True)
        o_ref[...] = jnp.concatenate([acca / la, accb / lb], axis=0).T
        row = lax.broadcasted_iota(jnp.int32, (LANES, 1), 0)
        st_ref[0] = jnp.where(row == 0, ma + jnp.log2(la), jnp.where(row == 1, mb + jnp.log2(lb), 0.0)).T
        if side:
            @pl.when(step_id == PAIRS * nb - 1)
            def _():
                side.last(si, so, *sems)

    in_specs = [pl.BlockSpec((sub, w, tblk), lambda g, i: (i, qoff + g, 0)), pl.BlockSpec((t, w), lambda g, i: (0, koff + g)),
                pl.BlockSpec((t // tblk, LANES, tblk), lambda g, i: (0, voff + g, 0))]
    if has_bias:
        in_specs.append(pl.BlockSpec((t, LANES), lambda g, i: (0, 0)))
    s_in_specs, s_out_specs = side.specs() if side else ([], [])
    return pl.pallas_call(
        body, name=name, grid=(PAIRS, nb), in_specs=in_specs + s_in_specs,
        out_specs=[pl.BlockSpec((blk, LANES), lambda g, i: (i, g)), pl.BlockSpec((1, blk, LANES), lambda g, i: (g, i, 0))]
        + s_out_specs,
        out_shape=[jax.ShapeDtypeStruct((t, PAIRS * LANES), F32), jax.ShapeDtypeStruct((PAIRS, t, LANES), F32)] + list(s_outs),
        scratch_shapes=side.sems() if side else [],
        compiler_params=_params(("arbitrary", "arbitrary")),
    )(*ins, *s_ins)


def mix_norm(fo, mo, g_fo, g_mo, *, name, bt=512):
    t, d = fo.shape
    bt = min(bt, t)

    def body(fo_ref, mo_ref, gf_ref, gm_ref, o_ref):
        for n, (x_ref, g_ref) in enumerate(((fo_ref, gf_ref), (mo_ref, gm_ref))):
            xv = x_ref[...]
            r = lax.rsqrt(jnp.mean(xv * xv, axis=-1, keepdims=True) + EPS)
            o_ref[:, n * d:(n + 1) * d] = (xv * r * g_ref[...]).astype(BF16)

    row = pl.BlockSpec((bt, d), lambda i: (i, 0))
    vec = pl.BlockSpec((1, d), lambda i: (0, 0))
    return pl.pallas_call(
        body, name=name, grid=(t // bt,), in_specs=[row, row, vec, vec],
        out_specs=pl.BlockSpec((bt, 2 * d), lambda i: (i, 0)),
        out_shape=jax.ShapeDtypeStruct((t, 2 * d), BF16),
        compiler_params=_params(("parallel",)),
    )(fo, mo, g_fo, g_mo)


def mix_norm_bwd(dmixed, fo, mo, g_fo, g_mo, st_f, st_m, *, name, bt=512, side=None):
    t, d = fo.shape
    bt = min(bt, t)

    def body(dm_ref, fo_ref, mo_ref, gf_ref, gm_ref, sf_ref, sm_ref, dfo_ref, dmo_ref, dgf_ref, dgm_ref, sfo_ref, smo_ref,
             dfot_ref, dmot_ref):
        @pl.when(pl.program_id(0) == 0)
        def _():
            dgf_ref[...] = jnp.zeros_like(dgf_ref)
            dgm_ref[...] = jnp.zeros_like(dgm_ref)

        groups = ((fo_ref, gf_ref, dfo_ref, dgf_ref, sf_ref, sfo_ref, dfot_ref),
                  (mo_ref, gm_ref, dmo_ref, dgm_ref, sm_ref, smo_ref, dmot_ref))
        for n, (x_ref, g_ref, dx_ref, dg_ref, st_ref, sto_ref, dxt_ref) in enumerate(groups):
            xv = x_ref[...]
            dhv = dm_ref[:, n * d:(n + 1) * d]
            r = lax.rsqrt(jnp.mean(xv * xv, axis=-1, keepdims=True) + EPS)
            u = dhv * g_ref[...]
            dxf = r * u - xv * (r * r * r * jnp.mean(u * xv, axis=-1, keepdims=True))
            dxb = dxf.astype(BF16)
            dx_ref[...] = dxb
            dxt_ref[0] = dxf.T.astype(BF16)
            dg_ref[...] += jnp.sum(dhv * (xv * r), axis=0, keepdims=True)
            prod = xv * dxb.astype(F32)
            for g in range(PAIRS):
                grp = prod[:, g * LANES:(g + 1) * LANES]
                da = jnp.sum(jnp.where(_lane() < HEAD_DIM, grp, 0.0), axis=1, keepdims=True)
                db = jnp.sum(jnp.where(_lane() >= HEAD_DIM, grp, 0.0), axis=1, keepdims=True)
                sto_ref[g] = jnp.where(_lane() == 2, da, jnp.where(_lane() == 3, db, st_ref[g]))

    row = pl.BlockSpec((bt, d), lambda i: (i, 0))
    vec = pl.BlockSpec((1, d), lambda i: (0, 0))
    stat = pl.BlockSpec((PAIRS, bt, LANES), lambda i: (0, i, 0))
    return gridded(
        body, name=name, grid=(t // bt,),
        in_specs=[pl.BlockSpec((bt, 2 * d), lambda i: (i, 0)), row, row, vec, vec, stat, stat],
        out_specs=[row, row, vec, vec, stat, stat] + [pl.BlockSpec((1, d, bt), lambda i: (i, 0, 0))] * 2,
        out_shape=[jax.ShapeDtypeStruct((t, d), BF16)] * 2 + [jax.ShapeDtypeStruct((1, d), F32)] * 2
        + [jax.ShapeDtypeStruct(st_f.shape, F32)] * 2 + [jax.ShapeDtypeStruct((t // bt, d, bt), BF16)] * 2,
        ins=[dmixed, fo, mo, g_fo, g_mo, st_f, st_m], semantics=("arbitrary",), side=side)


def flash_bwd(q_arr, qt_arr, k_arr, v_arr, do_arr, dot_arr, st, f_blocks, *, qoff, koff, voff, pair, scale, name, qblk=1024,
              side=None):
    t = q_arr.shape[0]
    blk = qt_arr.shape[2]
    qblk = max(min(qblk, t), blk)
    sub = qblk // blk
    nb, nbq = t // blk, t // qblk
    w = LANES if pair else 2 * LANES
    hw = w // 2
    has_bias = f_blocks is not None
    split = _halves if pair else _groups
    ins = [q_arr, qt_arr, k_arr, v_arr, do_arr, dot_arr, st] + ([f_blocks] if has_bias else [])
    n_out = 4 if has_bias else 3
    s_ins, s_outs = (side.ins, side.out_shapes) if side else ([], [])

    def wide(ref, first):
        parts = [ref[first + u] for u in range(sub)]
        return parts[0] if sub == 1 else jnp.concatenate(parts, axis=1)

    def body(*refs):
        main, si, outs, so, sems = _split_refs(refs, [len(ins), len(s_ins), n_out, len(s_outs), 2 if side else 0])
        q_ref, qt_ref, k_ref, v_ref, do_ref, dot_ref, st_ref = main[:7]
        dq_ref, dk_ref, dv_ref = outs[:3]
        g, j = pl.program_id(0), pl.program_id(1)
        step_id = g * nb + j
        if side:
            @pl.when(step_id == 0)
            def _():
                side.first(si, so, *sems)

        @pl.when(j == 0)
        def _():
            dq_ref[...] = jnp.zeros_like(dq_ref)

        kk, vv = k_ref[...], v_ref[...]
        ks = [kk, kk] if pair else _groups(kk)
        if has_bias:
            f_ref, df_ref = main[7], outs[3]
            fk = [LOG2E * _pick_row(f_ref[0], 2 * g + n) for n in range(2)]

            @pl.when(step_id == 0)
            def _():
                df_ref[...] = jnp.zeros_like(df_ref)

        def step(i, carry, diagonal):
            rows = pl.ds(pl.multiple_of(i * qblk, qblk), qblk)
            qs = split((q_ref[rows, :].astype(F32) * (scale * LOG2E)).astype(BF16))
            qt = (wide(qt_ref, sub * i).astype(F32) * (scale * LOG2E)).astype(BF16)
            dos = [h.astype(BF16) for h in _halves(do_ref[rows, :].astype(F32))]
            dot = wide(dot_ref, sub * i)
            stats = st_ref[0, rows, :]
            new, dqs, row_sums = [], [], []
            for n in range(2):
                dkt, dvt, dfk = carry[n]
                s = lax.dot_general(qs[n], ks[n], NT, preferred_element_type=F32)
                if has_bias:
                    s = s - fk[n]
                if diagonal:
                    s = _below_diagonal(s, i * qblk - j * blk)
                p = jnp.exp2(s - stats[:, n:n + 1])
                dp = lax.dot_general(dos[n], vv, NT, preferred_element_type=F32)
                ds = p * (dp - stats[:, 2 + n:3 + n])
                dsb = ds.astype(BF16)
                dvt = dvt + jnp.dot(dot[n * HEAD_DIM:(n + 1) * HEAD_DIM], p.astype(BF16), preferred_element_type=F32)
                dkt = dkt + jnp.dot(qt[n * hw:(n + 1) * hw], dsb, preferred_element_type=F32)
                dqs.append(jnp.dot(dsb, ks[n], preferred_element_type=F32))
                if has_bias:
                    dfk = dfk - jnp.sum(ds, axis=0, keepdims=True)
                    row_sums.append(jnp.sum(ds, axis=1, keepdims=True))
                new.append((dkt, dvt, dfk))
            dq = (jnp.where(_lane() < HEAD_DIM, dqs[0], dqs[1]) if pair else jnp.concatenate(dqs, axis=1)) * scale
            if has_bias:
                df_ref[rows, :] += jnp.where(_lane() == 2 * g, row_sums[0], jnp.where(_lane() == 2 * g + 1, row_sums[1], 0.0))
            dq_ref[rows, :] += dq
            return tuple(new)

        init = tuple((jnp.zeros((hw, blk), F32), jnp.zeros((HEAD_DIM, blk), F32), jnp.zeros((1, blk), F32)) for _ in range(2))
        first = j // sub
        carry = step(first, init, True)
        (dka, dva, dfa), (dkb, dvb, dfb) = lax.fori_loop(first + 1, nbq, lambda i, c: step(i, c, False), carry)
        dk_ref[...] = jnp.concatenate([dka, dkb], axis=0).T * LN2
        dv_ref[...] = jnp.concatenate([dva, dvb], axis=0).T
        if has_bias:
            row = lax.broadcasted_iota(jnp.int32, (LANES, 1), 0)
            by_head = jnp.where(row == 2 * g, dfa, jnp.where(row == 2 * g + 1, dfb, 0.0))
            df_ref[pl.ds(pl.multiple_of(j * blk, blk), blk), :] += by_head.T
        if side:
            @pl.when(step_id == PAIRS * nb - 1)
            def _():
                side.last(si, so, *sems)

    in_specs = [pl.BlockSpec((t, w), lambda g, j: (0, qoff + g)), pl.BlockSpec((nb, w, blk), lambda g, j: (0, qoff + g, 0)),
                pl.BlockSpec((blk, w), lambda g, j: (j, koff + g)), pl.BlockSpec((blk, LANES), lambda g, j: (j, voff + g)),
                pl.BlockSpec((t, LANES), lambda g, j: (0, g)), pl.BlockSpec((nb, LANES, blk), lambda g, j: (0, g, 0)),
                pl.BlockSpec((1, t, LANES), lambda g, j: (g, 0, 0))]
    out_specs = [pl.BlockSpec((t, w), lambda g, j: (0, g)), pl.BlockSpec((blk, w), lambda g, j: (j, g)),
                 pl.BlockSpec((blk, LANES), lambda g, j: (j, g))]
    out_shape = [jax.ShapeDtypeStruct((t, PAIRS * w), F32)] * 2 + [jax.ShapeDtypeStruct((t, PAIRS * LANES), F32)]
    if has_bias:
        in_specs.append(pl.BlockSpec((1, N_HEADS, blk), lambda g, j: (j, 0, 0)))
        out_specs.append(pl.BlockSpec((t, LANES), lambda g, j: (0, 0)))
        out_shape.append(jax.ShapeDtypeStruct((t, LANES), F32))
    s_in_specs, s_out_specs = side.specs() if side else ([], [])
    return pl.pallas_call(
        body, name=name, grid=(PAIRS, nb), in_specs=in_specs + s_in_specs, out_specs=out_specs + s_out_specs,
        out_shape=out_shape + list(s_outs), scratch_shapes=side.sems() if side else [],
        compiler_params=_params(("arbitrary", "arbitrary")),
    )(*ins, *s_ins)


def _adamw_math(w, g, m, v):
    nm = ADAM_B1 * m + (1.0 - ADAM_B1) * g
    nv = ADAM_B2 * v + (1.0 - ADAM_B2) * (g * g)
    m_hat = nm / (1.0 - ADAM_B1 ** ADAM_STEP)
    v_hat = nv / (1.0 - ADAM_B2 ** ADAM_STEP)
    return -ADAM_LR * (m_hat / (jnp.sqrt(v_hat) + ADAM_EPS) + ADAM_WD * w), nm, nv


def adamw(w, g, m, v, *, name):
    rws, cols = w.shape
    br = _row_block(rws)

    def body(w_ref, g_ref, m_ref, v_ref, d_ref, nm_ref, nv_ref):
        d_ref[...], nm_ref[...], nv_ref[...] = _adamw_math(w_ref[...], g_ref[...], m_ref[...], v_ref[...])

    blk = pl.BlockSpec((br, cols), lambda i: (i, 0))
    return pl.pallas_call(
        body, name=name, grid=(rws // br,), in_specs=[blk] * 4, out_specs=[blk] * 3,
        out_shape=[jax.ShapeDtypeStruct((rws, cols), F32)] * 3,
        compiler_params=_params(("parallel",)),
    )(w, g, m, v)


def adamw_halves(w, g_mine, g_other, m, v, core, *, name):
    _, k, n = w.shape
    br = _row_block(k // 2)
    nh = k // 2 // br

    def body(c_ref, w_ref, gm_ref, go_ref, m_ref, v_ref, g_out, d_ref, nm_ref, nv_ref):
        gv = jnp.where(pl.program_id(0) == c_ref[0], gm_ref[...], go_ref[...])
        g_out[0] = gv
        d_ref[0], nm_ref[0], nv_ref[0] = _adamw_math(w_ref[0], gv, m_ref[0], v_ref[0])

    full = pl.BlockSpec((1, br, n), lambda hb, i, c: (0, hb * nh + i, 0))
    half = pl.BlockSpec((br, n), lambda hb, i, c: (i, 0))
    return pl.pallas_call(
        body, name=name,
        grid_spec=pltpu.PrefetchScalarGridSpec(num_scalar_prefetch=1, grid=(2, nh), in_specs=[full, half, half, full, full],
                                               out_specs=[full] * 4),
        out_shape=[jax.ShapeDtypeStruct(w.shape, F32)] * 4,
        compiler_params=_params(("parallel", "parallel")),
    )(core, w, g_mine, g_other, m, v)


def adamw_halves_t(wt, gt_mine, gt_other, mt, vt, core, *, name, bc=128):
    n, k = wt.shape
    nh = k // 2 // bc

    def body(c_ref, w_ref, gm_ref, go_ref, m_ref, v_ref, g_out, d_ref, nm_ref, nv_ref):
        gv = jnp.where(pl.program_id(0) == c_ref[0], gm_ref[...], go_ref[...])
        g_out[...] = gv
        d_ref[...], nm_ref[...], nv_ref[...] = _adamw_math(w_ref[...], gv, m_ref[...], v_ref[...])

    full = pl.BlockSpec((n, bc), lambda hb, i, c: (0, hb * nh + i))
    half = pl.BlockSpec((n, bc), lambda hb, i, c: (0, i))
    return pl.pallas_call(
        body, name=name,
        grid_spec=pltpu.PrefetchScalarGridSpec(num_scalar_prefetch=1, grid=(2, nh), in_specs=[full, half, half, full, full],
                                               out_specs=[full] * 4),
        out_shape=[jax.ShapeDtypeStruct(wt.shape, F32)] * 4,
        compiler_params=_params(("parallel", "parallel")),
    )(core, wt, gt_mine, gt_other, mt, vt)


def add_pair(dw, recv, core, *, name):
    n4, k, n = dw.shape
    half = (1, k // 2, n) if split_axis(k) == 0 else (1, k, n // 2)
    mine = (lambda q, c: (q, c[0], 0)) if split_axis(k) == 0 else (lambda q, c: (q, 0, c[0]))

    def body(c_ref, a_ref, b_ref, o_ref):
        o_ref[...] = (a_ref[...] + b_ref[...].astype(F32)).astype(BF16)

    return pl.pallas_call(
        body, name=name,
        grid_spec=pltpu.PrefetchScalarGridSpec(
            num_scalar_prefetch=1, grid=(n4,),
            in_specs=[pl.BlockSpec(half, mine), pl.BlockSpec(half, lambda q, c: (q, 0, 0))],
            out_specs=pl.BlockSpec(half, lambda q, c: (q, 0, 0))),
        out_shape=jax.ShapeDtypeStruct((n4,) + half[1:], BF16),
        compiler_params=_params(("parallel",)),
    )(core, dw, recv)


def sum_chips(parts, *, name):
    n4, r, n = parts.shape
    if r % 16 == 0:
        br, bc = _row_block(r), n
    else:
        br, bc = r, LANES

    def body(p_ref, o_ref):
        acc = p_ref[0].astype(F32)
        for q in range(1, n4):
            acc = acc + p_ref[q].astype(F32)
        o_ref[...] = acc

    return pl.pallas_call(
        body, name=name, grid=(r // br, n // bc),
        in_specs=[pl.BlockSpec((n4, br, bc), lambda i, j: (0, i, j))], out_specs=pl.BlockSpec((br, bc), lambda i, j: (i, j)),
        out_shape=jax.ShapeDtypeStruct((r, n), F32),
        compiler_params=_params(("parallel", "parallel")),
    )(parts)


ANY = pl.BlockSpec(memory_space=pl.ANY)


def _place():
    x, y, c = lax.axis_index("x"), lax.axis_index("y"), lax.axis_index("c")
    chips = [(1 - x, y), (x, 1 - y), (1 - x, 1 - y)]
    return x, y, c, chips


def _copy(src, dst, send_sems, recv_sems, k, to):
    return pltpu.make_async_remote_copy(src_ref=src, dst_ref=dst, send_sem=send_sems.at[k], recv_sem=recv_sems.at[k],
                                        device_id=to, device_id_type=MESH)


def split_axis(rows):
    return 0 if rows % 32 == 0 else 1


def _half(ref, lead, hf):
    rows, cols = ref.shape[-2:]
    if split_axis(rows) == 0:
        at = (pl.ds(hf * (rows // 2), rows // 2), slice(None))
    else:
        at = (slice(None), pl.ds(hf * (cols // 2), cols // 2))
    return ref.at[at] if lead is None else ref.at[(lead,) + at]


def _gather_first(srcs, dsts, ssems, rsems):
    x, y, c, chips = _place()
    for ti, (s, d) in enumerate(zip(srcs, dsts)):
        for j, (cx, cy) in enumerate(chips):
            _copy(_half(s, None, c), _half(d, 2 * x + y, c), ssems, rsems, 3 * ti + j, (cx, cy, c)).start()


def _gather_mid(srcs, dsts, ssems, rsems):
    x, y, c, chips = _place()
    n1 = 3 * len(srcs)
    for ti, d in enumerate(dsts):
        for j, (cx, cy) in enumerate(chips):
            landed = _half(d, 2 * cx + cy, c)
            _copy(landed, landed, ssems, rsems, 3 * ti + j, (cx, cy, c)).wait_recv()
            _copy(landed, landed, ssems, rsems, n1 + 3 * ti + j, (x, y, 1 - c)).start()


def _gather_last(srcs, dsts, ssems, rsems):
    x, y, c, chips = _place()
    n1 = 3 * len(srcs)
    for ti, (s, d) in enumerate(zip(srcs, dsts)):
        for j, (cx, cy) in enumerate(chips):
            other = _half(d, 2 * cx + cy, 1 - c)
            _copy(other, other, ssems, rsems, n1 + 3 * ti + j, (x, y, 1 - c)).wait_recv()
        for j, (cx, cy) in enumerate(chips):
            mine = _half(s, None, c)
            _copy(mine, mine, ssems, rsems, 3 * ti + j, (cx, cy, c)).wait_send()
            _copy(mine, mine, ssems, rsems, n1 + 3 * ti + j, (x, y, 1 - c)).wait_send()


def gather_side(shards):
    return Side(shards, [jax.ShapeDtypeStruct((N_CHIPS,) + s.shape, s.dtype) for s in shards], 6 * len(shards),
                _gather_first, _gather_last, _gather_mid)


def _scatter_first(srcs, dsts, ssems, rsems):
    x, y, c, chips = _place()
    for ti, (s, d) in enumerate(zip(srcs, dsts)):
        for j, (cx, cy) in enumerate(chips):
            _copy(s.at[2 * cx + cy], d.at[2 * x + y], ssems, rsems, 3 * ti + j, (cx, cy, c)).start()


def _scatter_last(srcs, dsts, ssems, rsems):
    x, y, c, chips = _place()
    for ti, (s, d) in enumerate(zip(srcs, dsts)):
        for j, (cx, cy) in enumerate(chips):
            _copy(s.at[2 * cx + cy], d.at[2 * cx + cy], ssems, rsems, 3 * ti + j, (cx, cy, c)).wait_recv()
        for j, (cx, cy) in enumerate(chips):
            _copy(s.at[2 * cx + cy], d.at[2 * cx + cy], ssems, rsems, 3 * ti + j, (cx, cy, c)).wait_send()


def scatter_side(parts):
    return Side(parts, [jax.ShapeDtypeStruct(p.shape, p.dtype) for p in parts], 3 * len(parts), _scatter_first, _scatter_last)


def run_side(side, *, name):
    n_in, n_out = len(side.ins), len(side.out_shapes)

    def body(*refs):
        si, so, sems = _split_refs(refs, [n_in, n_out, 2])
        side.first(si, so, *sems)
        if side.mid is not None:
            side.mid(si, so, *sems)
        side.last(si, so, *sems)

    in_specs, out_specs = side.specs()
    return pl.pallas_call(body, name=name, in_specs=in_specs, out_specs=out_specs, out_shape=side.out_shapes,
                          scratch_shapes=side.sems())(*side.ins)


def _swap_first(srcs, dsts, ssems, rsems):
    x, y, c, _ = _place()
    for k, (s, d) in enumerate(zip(srcs, dsts)):
        _copy(s, d, ssems, rsems, k, (x, y, 1 - c)).start()


def _swap_last(srcs, dsts, ssems, rsems):
    x, y, c, _ = _place()
    for k, (s, d) in enumerate(zip(srcs, dsts)):
        _copy(s, d, ssems, rsems, k, (x, y, 1 - c)).wait()


def swap_side(xs):
    return Side(xs, [jax.ShapeDtypeStruct(a.shape, a.dtype) for a in xs], len(xs), _swap_first, _swap_last)


def allreduce_small(s):
    n_dev = 8

    def body(s_ref, out_ref, buf, send_sems, recv_sems):
        x, y, c, _ = _place()
        me = 4 * x + 2 * y + c
        buf[me] = s_ref[...]
        sends = []
        for k in range(1, n_dev):
            px = 1 - x if k & 4 else x
            py = 1 - y if k & 2 else y
            pc = 1 - c if k & 1 else c
            cp = _copy(s_ref, buf.at[me], send_sems, recv_sems, k - 1, (px, py, pc))
            cp.start()
            sends.append((cp, 4 * px + 2 * py + pc))
        for k, (cp, peer) in enumerate(sends):
            _copy(s_ref, buf.at[peer], send_sems, recv_sems, k, (x, y, c)).wait_recv()
        for cp, _ in sends:
            cp.wait_send()
        acc = buf[0]
        for d in range(1, n_dev):
            acc = acc + buf[d]
        out_ref[...] = acc

    vm = pl.BlockSpec(memory_space=pltpu.VMEM)
    return pl.pallas_call(
        body, name="allreduce_small", in_specs=[vm], out_specs=vm,
        out_shape=jax.ShapeDtypeStruct(s.shape, F32),
        scratch_shapes=[pltpu.VMEM((n_dev,) + s.shape, F32), pltpu.SemaphoreType.DMA((n_dev - 1,)),
                        pltpu.SemaphoreType.DMA((n_dev - 1,))],
    )(s)


def join_cols(sm):
    n4, k, n = sm.shape
    return sm.transpose(1, 0, 2).reshape(k, n4 * n)


def split_cols(full):
    k, n = full.shape
    return full.reshape(k, N_CHIPS, n // N_CHIPS).transpose(1, 0, 2)


def _pad_heads(w, width):
    lead = w.shape[:-1]
    w = w.reshape(lead + (N_HEADS, width))
    return jnp.pad(w, [(0, 0)] * len(lead) + [(0, 0), (0, LANES - width)]).reshape(lead + (HW,))


def _unpad_heads(w, width):
    lead = w.shape[:-1]
    return w.reshape(lead + (N_HEADS, LANES))[..., :width].reshape(lead + (N_HEADS * width,))


def split_w_in_t(w_in_t):
    d = w_in_t.shape[1]
    o_f = 3 * FW
    o_cq = o_f + N_HEADS
    o_ckv = o_cq + Q_RANK
    o_kr = o_ckv + KV_RANK

    def z(n):
        return jnp.zeros((n, d), w_in_t.dtype)

    small = jnp.concatenate([w_in_t[o_cq:o_ckv], w_in_t[o_ckv:o_kr], z(HEAD_DIM), w_in_t[o_kr:], z(LANES - HEAD_DIM - ROPE_DIM),
                             w_in_t[o_f:o_cq], z(LANES - N_HEADS), z(LANES)], axis=0)
    return w_in_t[:o_f], small


def join_w_in_t(d_qkv_t, d_small_t):
    kr = S_KR + HEAD_DIM
    return jnp.concatenate([d_qkv_t, d_small_t[S_F:S_F + N_HEADS], d_small_t[S_CQ:S_CKV], d_small_t[S_CKV:S_KR],
                            d_small_t[kr:kr + ROPE_DIM]], axis=0)


def rope_tables(pos):
    t = pos.shape[0]
    inv_freq = ROPE_THETA ** (-jnp.arange(0, ROPE_DIM, 2, dtype=F32) / ROPE_DIM)
    ang = pos.astype(F32)[:, None] * inv_freq
    cos, sin = jnp.cos(ang), jnp.sin(ang)
    half = ROPE_DIM // 2

    def z(n):
        return jnp.zeros((t, n), F32)

    tab_c = jnp.concatenate([jnp.ones((t, HEAD_DIM), F32), cos, cos, z(LANES - HEAD_DIM - ROPE_DIM)], axis=1)
    tab_a = jnp.concatenate([z(HEAD_DIM), -sin, z(half), z(LANES - HEAD_DIM - ROPE_DIM)], axis=1)
    tab_b = jnp.concatenate([z(HEAD_DIM), z(half), sin, z(LANES - HEAD_DIM - ROPE_DIM)], axis=1)
    return tab_c, tab_a, tab_b


def _pad_lanes(v, n):
    return jnp.pad(v, ((0, 0), (0, n - v.shape[1])))


ATTN_BLK = 512
ATTN_FWD_BLK = 1024
ATTN_BWD_QBLK = 1024


def local_step(xs, pos, tgt, gains, early_weights, late_weights, early_side=None, fwd_side=None, reduction=None):
    g_attn, b_forget, g_q, g_kv, g_fo, g_mo, g_mlp, g_fin = gains
    t = xs.shape[0]
    blk = min(ATTN_BLK, t)
    fox_scale = 1.0 / (HEAD_DIM ** 0.5)
    mla_scale = 1.0 / ((HEAD_DIM + ROPE_DIM) ** 0.5)

    h1, *gathered = rmsnorm(xs, g_attn, out_dtype=BF16, name="norm_attn", side=early_side)
    w_in_t, w_uq, w_ukv = early_weights(gathered)
    w_qkv_t, w_small_t = split_w_in_t(w_in_t)
    w_uq_p = _pad_heads(w_uq, HEAD_DIM + ROPE_DIM)
    kv = w_ukv.reshape(KV_RANK, N_HEADS, 2 * HEAD_DIM)
    w_ukv_p = jnp.concatenate([_pad_heads(kv[:, :, :HEAD_DIM].reshape(KV_RANK, FW), HEAD_DIM),
                               kv[:, :, HEAD_DIM:].reshape(KV_RANK, FW)], axis=1)
    b_f = _pad_lanes(b_forget, LANES)
    tab_c, tab_a, tab_b = rope_tables(pos)

    qkv, qkv_t = mm(h1, w_qkv_t, trans_b=True, out_dtypes=[BF16], t_blk=blk, name="proj_qkv")
    small, = mm(h1, w_small_t, trans_b=True, out_dtypes=[F32], name="proj_small")
    mq, mk, mv, lf, cqn, ckvn, mq_t, mv_t = mla_prep(small, g_q, g_kv, w_uq_p, w_ukv_p, tab_c, tab_a, tab_b, b_f,
                                                     name="mla_prep", bt=blk)
    f_cum = cumsum_rows(lf, reverse=False, name="gate_cumsum")
    f_blocks = f_cum[:, :N_HEADS].reshape(t // blk, blk, N_HEADS).transpose(0, 2, 1)
    fo, st_f, *gathered = flash_fwd(qkv_t, qkv, qkv_t, f_cum, qoff=0, koff=PAIRS, voff=2 * PAIRS, pair=True,
                                    scale=fox_scale, name="fox_fwd", blk=ATTN_FWD_BLK, side=fwd_side)
    w_o, w_up, w_down = late_weights(gathered)
    mo, st_m = flash_fwd(mq_t, mk, mv_t, None, qoff=0, koff=0, voff=0, pair=False, scale=mla_scale, name="mla_fwd",
                         blk=ATTN_FWD_BLK)
    mixed = mix_norm(fo, mo, g_fo, g_mo, name="norm_mix")

    def inv_rms(v):
        return lax.rsqrt(jnp.mean(v * v, axis=-1, keepdims=True) + EPS)

    def residual_then_norm(acc, res, g):
        xn = acc + res
        return xn, xn * inv_rms(xn) * g

    def norm_bwd(dh, xn, res, g):
        r = inv_rms(xn)
        uu = dh * g
        return (r * uu - xn * (r * r * r * jnp.mean(uu * xn, axis=-1, keepdims=True)) + res,
                jnp.sum(dh * (xn * r), axis=0, keepdims=True))

    def norm_bwd2(dh, xn, res, g):
        dx, dg = norm_bwd(dh, xn, res, g)
        return dx, dx, dg

    def residual_then_loss(acc, res, target, g):
        xn = acc + res
        r = inv_rms(xn)
        xh = xn * r
        e = xh * g - target
        part = 0.5 * jnp.sum(jnp.mean(e * e, axis=-1, keepdims=True), axis=0, keepdims=True)
        dy = e * (1.0 / xn.shape[1])
        uu = dy * g
        dx = r * uu - xn * (r * r * r * jnp.mean(uu * xn, axis=-1, keepdims=True))
        return dx, dx, jnp.sum(dy * xh, axis=0, keepdims=True), part + jnp.zeros_like(g)

    x1, h2 = mm(mixed, w_o, extras=[xs], vecs=[g_mlp], epilogue=residual_then_norm, out_dtypes=[F32, BF16], name="out_proj")

    def relu2(uu):
        r = jnp.maximum(uu.astype(F32), 0.0)
        return (r * r).astype(BF16)

    u, = mm(h2, w_up, out_dtypes=[BF16], name="mlp_up")
    dx2, dx2b, dg_fin, loss_row = mm(u, w_down, a_pro=relu2, extras=[x1, tgt], vecs=[g_fin], epilogue=residual_then_loss,
                                     out_dtypes=[F32, BF16], n_sums=2, name="mlp_down_loss")
    loss = loss_row[:, :1]

    def relu2_grad(acc, uu):
        return (acc * (2.0 * jnp.maximum(uu.astype(F32), 0.0)),)

    du, = mm(dx2b, w_down, trans_b=True, extras=[u], epilogue=relu2_grad, out_dtypes=[BF16], name="mlp_down_bwd")
    dw_down = mm_tn(u, dx2b, a_pro=relu2, name="dw_down").reshape(N_CHIPS, -1, w_down.shape[1])
    dx1, dx1b, dg_mlp = mm(du, w_up, trans_b=True, extras=[x1, dx2], vecs=[g_mlp], epilogue=norm_bwd2,
                           out_dtypes=[F32, BF16], n_sums=1, name="mlp_up_bwd")
    dw_up = mm_tn(h2, du, name="dw_up", col_shards=N_CHIPS)

    dmixed, = mm(dx1b, w_o, trans_b=True, out_dtypes=[F32], name="out_proj_bwd")
    dw_o = mm_tn(mixed, dx1b, name="dw_o").reshape(N_CHIPS, -1, w_o.shape[1])
    late = (dw_o, dw_up, dw_down)
    red = reduction
    dfo, dmo, dg_fo, dg_mo, st_f, st_m, dfo_t, dmo_t, *got = mix_norm_bwd(
        dmixed, fo, mo, g_fo, g_mo, st_f, st_m, name="norm_mix_bwd", bt=blk, side=red.late_swap(late) if red else None)
    dfq, dfk, dfv, d_f, *got = flash_bwd(
        qkv, qkv_t, qkv, qkv, dfo, dfo_t, st_f, f_blocks, qoff=0, koff=PAIRS, voff=2 * PAIRS, pair=True,
        scale=fox_scale, name="fox_bwd", qblk=ATTN_BWD_QBLK, side=red.late_scatter(got) if red else None)
    dmq, dmk, dmv, *got = flash_bwd(mq, mq_t, mk, mv, dmo, dmo_t, st_m, None, qoff=0, koff=0, voff=0,
                                    pair=False, scale=mla_scale, name="mla_bwd", qblk=ATTN_BWD_QBLK,
                                    side=red.late_halves(got) if red else None)
    if red:
        red.late_done(got)
    dqkv = jnp.concatenate([dfq, dfk, dfv], axis=1).astype(BF16)
    dlf = cumsum_rows(d_f, reverse=True, name="gate_cumsum_bwd")
    dsmall, dq_u, dkv_u, dg_q, dg_kv, db_f = mla_prep_bwd(dmq, dmk, dmv, dlf, small, g_q, g_kv, w_uq_p, w_ukv_p,
                                                          tab_c, tab_a, tab_b, b_f, name="mla_prep_bwd")
    dw_uq_p = mm_tn(cqn, dq_u, name="dw_uq")
    dw_ukv_p = mm_tn(ckvn, dkv_u, name="dw_ukv")

    dw_in_t = join_w_in_t(mm_tn(dqkv, h1, name="dw_qkv"), mm_tn(dsmall, h1, name="dw_small"))
    dw_uq = _unpad_heads(dw_uq_p, HEAD_DIM + ROPE_DIM)
    dk_cols = _unpad_heads(dw_ukv_p[:, :HW], HEAD_DIM).reshape(KV_RANK, N_HEADS, HEAD_DIM)
    dv_cols = dw_ukv_p[:, HW:].reshape(KV_RANK, N_HEADS, HEAD_DIM)
    dw_ukv = jnp.concatenate([dk_cols, dv_cols], axis=2).reshape(KV_RANK, N_HEADS * 2 * HEAD_DIM)
    early = (dw_in_t.reshape(N_CHIPS, -1, dw_in_t.shape[1]), split_cols(dw_uq), split_cols(dw_ukv))
    grad_x, dg_attn, *got = mm([dqkv, dsmall], [w_qkv_t, w_small_t], extras=[xs, dx1], vecs=[g_attn],
                               epilogue=norm_bwd, out_dtypes=[F32], n_sums=1, name="proj_bwd",
                               side=red.early_scatter(early) if red else None)
    if red:
        red.early_done(got)
    d_gains = (dg_attn, db_f[:, :N_HEADS], dg_q, dg_kv, dg_fo, dg_mo, dg_mlp, dg_fin)
    return loss, grad_x, early, late, d_gains


class GradReduction:
    def __init__(self, core_id, chip):
        self.core_id, self.chip = core_id, chip
        self.core = core_id.reshape(1).astype(jnp.int32)
        self.grads, self.pairs, self.halves, self.others = {}, {}, {}, {}

    def _other_halves(self, grads):
        out = []
        for g in grads:
            axis = 1 + split_axis(g.shape[1])
            size = g.shape[axis] // 2
            out.append(lax.dynamic_slice_in_dim(g, (1 - self.core_id) * size, size, axis=axis).astype(BF16))
        return out

    def _add_pairs(self, group, recvs):
        self.pairs[group] = [add_pair(g, r, self.core, name="add_pair_%s_%d" % (group, n))
                             for n, (g, r) in enumerate(zip(self.grads[group], recvs))]
        return scatter_side(self.pairs[group])

    def _chip_sums(self, group, scattered):
        chip = self.chip
        with_mine = [lax.dynamic_update_index_in_dim(s, lax.dynamic_index_in_dim(p, chip, 0, keepdims=True), chip, 0)
                     for s, p in zip(scattered, self.pairs[group])]
        self.halves[group] = [sum_chips(s, name="sum_chips_%s_%d" % (group, n)) for n, s in enumerate(with_mine)]
        return self.halves[group]

    def late_swap(self, grads):
        self.grads["late"] = list(grads)
        return swap_side(self._other_halves(grads))

    def late_scatter(self, recvs):
        return self._add_pairs("late", recvs)

    def late_halves(self, scattered):
        return swap_side(self._chip_sums("late", scattered))

    def late_done(self, others):
        self.others["late"] = list(others)

    def early_scatter(self, grads):
        self.grads["early"] = list(grads)
        return self._add_pairs("early", run_side(swap_side(self._other_halves(grads)), name="swap_early_sends"))

    def early_done(self, scattered):
        self.others["early"] = list(run_side(swap_side(self._chip_sums("early", scattered)), name="swap_early_halves"))

def kernel(x, positions, attn_norm_g, w_in, b_forget, q_norm_g, w_uq, kv_norm_g, w_ukv, fox_out_g, mla_out_g, w_o, mlp_norm_g, w_up, w_down, final_norm_g, loss_target, m_attn_norm_g, m_w_in, m_b_forget, m_q_norm_g, m_w_uq, m_kv_norm_g, m_w_ukv, m_fox_out_g, m_mla_out_g, m_w_o, m_mlp_norm_g, m_w_up, m_w_down, m_final_norm_g, v_attn_norm_g, v_w_in, v_b_forget, v_q_norm_g, v_w_uq, v_kv_norm_g, v_w_ukv, v_fox_out_g, v_mla_out_g, v_w_o, v_mlp_norm_g, v_w_up, v_w_down, v_final_norm_g):
    core_id = lax.axis_index("c")
    core = core_id.reshape(1).astype(jnp.int32)
    chip = 2 * lax.axis_index("x") + lax.axis_index("y")
    big = [w_in, w_uq, w_ukv, w_o, w_up, w_down]
    big_m = [m_w_in, m_w_uq, m_w_ukv, m_w_o, m_w_up, m_w_down]
    big_v = [v_w_in, v_w_uq, v_w_ukv, v_w_o, v_w_up, v_w_down]
    n_early = 3

    def vec(a):
        return a.reshape(1, -1)

    small = [attn_norm_g, b_forget, q_norm_g, kv_norm_g, fox_out_g, mla_out_g, mlp_norm_g, final_norm_g]
    small_m = [m_attn_norm_g, m_b_forget, m_q_norm_g, m_kv_norm_g, m_fox_out_g, m_mla_out_g, m_mlp_norm_g, m_final_norm_g]
    small_v = [v_attn_norm_g, v_b_forget, v_q_norm_g, v_kv_norm_g, v_fox_out_g, v_mla_out_g, v_mlp_norm_g, v_final_norm_g]
    gains = [vec(a) for a in small]

    shards = [(w[0].T if n == 0 else w[0]).astype(BF16) for n, w in enumerate(big)]

    def with_own(gathered, mine):
        return [lax.dynamic_update_index_in_dim(g, s, chip, 0) for g, s in zip(gathered, mine)]

    def early_weights(gathered):
        g_in, g_uq, g_ukv = with_own(gathered, shards[:n_early])
        return g_in.reshape(-1, g_in.shape[2]), join_cols(g_uq), join_cols(g_ukv)

    def late_weights(gathered):
        g_o, g_up, g_down = with_own(gathered, shards[n_early:])
        return g_o.reshape(-1, g_o.shape[2]), join_cols(g_up), g_down.reshape(-1, g_down.shape[2])

    reduction = GradReduction(core_id, chip)
    loss, grad_x, _, _, d_small = local_step(
        x[0], positions[0], loss_target[0], gains, early_weights, late_weights, gather_side(shards[:n_early]),
        gather_side(shards[n_early:]), reduction)
    halves = reduction.halves["early"] + reduction.halves["late"]
    others = reduction.others["early"] + reduction.others["late"]

    def rows8(vs):
        return jnp.concatenate([_pad_lanes(vec(a).astype(F32), 1024) for a in vs], axis=0)

    with_loss = [jnp.concatenate([d, loss], axis=1) if n == 1 else d for n, d in enumerate(d_small)]
    g_small8 = allreduce_small(rows8(with_loss))

    outs_big = []
    for n, (w, gm, go, m, v) in enumerate(zip(big, halves, others, big_m, big_v)):
        _, k, cols = w.shape
        if n == 0:
            outs = adamw_halves_t(w[0].T, gm, go, m[0].T, v[0].T, core, name="adamw_%d" % n)
            outs_big.append([o.T[None] for o in outs])
        else:
            outs_big.append(adamw_halves(w, gm, go, m, v, core, name="adamw_%d" % n))
    d8, m8, v8 = adamw(rows8(small), g_small8, rows8(small_m), rows8(small_v), name="adamw_small")

    def unrows8(a8):
        return [a8[n, :s.size].reshape(s.shape) for n, s in enumerate(small)]

    loss_all = g_small8[1, N_HEADS]
    grads, deltas, new_m, new_v = [None] * 14, [None] * 14, [None] * 14, [None] * 14
    big_at = [1, 4, 6, 9, 11, 12]
    small_at = [0, 2, 3, 5, 7, 8, 10, 13]
    for n, at in enumerate(big_at):
        grads[at], deltas[at], new_m[at], new_v[at] = outs_big[n]
    for at, g, dd, mm_, vv in zip(small_at, unrows8(g_small8), unrows8(d8), unrows8(m8), unrows8(v8)):
        grads[at], deltas[at], new_m[at], new_v[at] = g, dd, mm_, vv
    return (loss_all, grad_x[None], *grads, *deltas, *new_m, *new_v)
```

```python
import jax
import jax.numpy as jnp
from jax import lax
from jax.experimental import pallas as pl
from jax.experimental.pallas import tpu as pltpu

F32 = jnp.float32
BF16 = jnp.bfloat16
MESH = pl.DeviceIdType.MESH

EPS = 1e-6
ROPE_THETA = 10000.0
N_HEADS = 8
PAIRS = N_HEADS // 2
HEAD_DIM = 64
ROPE_DIM = 32
LANES = 128
Q_RANK = 384
KV_RANK = 256
N_CHIPS = 4
ADAM_LR, ADAM_B1, ADAM_B2, ADAM_EPS, ADAM_WD, ADAM_STEP = 0.001, 0.9, 0.999, 1e-08, 0.01, 10
VMEM_LIMIT = 48 * 1024 * 1024
LOG2E = 1.4426950408889634
LN2 = 0.6931471805599453
NN = (((1,), (0,)), ((), ()))
NT = (((1,), (1,)), ((), ()))
TN = (((0,), (0,)), ((), ()))


def _params(sem=None):
    return pltpu.CompilerParams(dimension_semantics=sem, vmem_limit_bytes=VMEM_LIMIT)


def _fit(block, dim):
    if dim <= block:
        return dim
    return next(b for b in range(block - block % LANES, 0, -LANES) if dim % b == 0)


def _row_block(rows):
    return next(b for b in (256, 128, 64, 32, 16, 8) if rows % b == 0)


def gridded(body, *, name, grid, in_specs, out_specs, out_shape, ins, semantics, side=None):
    if side is None:
        return pl.pallas_call(body, name=name, grid=grid, in_specs=in_specs, out_specs=out_specs, out_shape=out_shape,
                              compiler_params=_params(semantics))(*ins)
    n_in, n_out = len(in_specs), len(out_specs)
    steps = 1
    for extent in grid:
        steps *= extent

    def riding(*refs):
        main_in, s_in, main_out, s_out, sems = _split_refs(refs, [n_in, len(side.ins), n_out, len(side.out_shapes), 2])
        step = 0
        for axis, extent in enumerate(grid):
            step = step * extent + pl.program_id(axis)

        @pl.when(step == 0)
        def _():
            side.first(s_in, s_out, *sems)

        body(*main_in, *main_out)

        @pl.when(step == steps - 1)
        def _():
            if side.mid is not None:
                side.mid(s_in, s_out, *sems)
            side.last(s_in, s_out, *sems)

    s_in_specs, s_out_specs = side.specs()
    return pl.pallas_call(
        riding, name=name, grid=grid, in_specs=list(in_specs) + s_in_specs, out_specs=list(out_specs) + s_out_specs,
        out_shape=list(out_shape) + side.out_shapes, scratch_shapes=side.sems(),
        compiler_params=_params(("arbitrary",) * len(grid)))(*ins, *side.ins)


def rmsnorm(x, g, *, out_dtype, name, bt=512, side=None):
    t, d = x.shape
    bt = min(bt, t)

    def body(x_ref, g_ref, o_ref):
        xv = x_ref[...].astype(F32)
        r = lax.rsqrt(jnp.mean(xv * xv, axis=-1, keepdims=True) + EPS)
        o_ref[...] = (xv * r * g_ref[...]).astype(o_ref.dtype)

    return gridded(
        body, name=name, grid=(t // bt,),
        in_specs=[pl.BlockSpec((bt, d), lambda i: (i, 0)), pl.BlockSpec((1, d), lambda i: (0, 0))],
        out_specs=[pl.BlockSpec((bt, d), lambda i: (i, 0))],
        out_shape=[jax.ShapeDtypeStruct((t, d), out_dtype)],
        ins=[x, g], semantics=("parallel",), side=side)


def mm(a, b, *, trans_b=False, a_pro=None, extras=(), vecs=(), epilogue=None, out_dtypes, n_sums=0, t_blk=None, name,
       bm=1024, bn=1024, side=None):
    a_list = list(a) if isinstance(a, (list, tuple)) else [a]
    b_list = list(b) if isinstance(b, (list, tuple)) else [b]
    m = a_list[0].shape[0]
    n = b_list[0].shape[0] if trans_b else b_list[0].shape[1]
    ks = [x.shape[1] for x in a_list]
    if sum(ks) > 2048:
        bm = bm // 2
    bm, bn = _fit(bm, m), _fit(bn, n)
    assert n_sums == 0 or bn == n
    n_ab, n_ex, n_vec, n_out = len(a_list), len(extras), len(vecs), len(out_dtypes)
    n_t = 0 if t_blk is None else 1

    def body(*refs):
        a_refs, b_refs, ex, vs, outs, t_outs, sums = _split_refs(refs, [n_ab, n_ab, n_ex, n_vec, n_out, n_t, n_sums])
        acc = None
        for a_ref, b_ref in zip(a_refs, b_refs):
            a_tile = a_ref[...] if a_pro is None else a_pro(a_ref[...])
            part = lax.dot_general(a_tile, b_ref[...], NT if trans_b else NN, preferred_element_type=F32)
            acc = part if acc is None else acc + part
        res = epilogue(acc, *[e[...] for e in ex], *[v[...] for v in vs]) if epilogue is not None else (acc,)
        for o, r in zip(outs, res[:n_out]):
            o[...] = r.astype(o.dtype)
        for t_ref in t_outs:
            for u in range(bm // t_blk):
                t_ref[u] = res[0][u * t_blk:(u + 1) * t_blk, :].T.astype(t_ref.dtype)
        if n_sums:
            @pl.when(pl.program_id(0) == 0)
            def _():
                for s_ref in sums:
                    s_ref[...] = jnp.zeros_like(s_ref)

            for s_ref, r in zip(sums, res[n_out:]):
                s_ref[...] += r

    tile = pl.BlockSpec((bm, bn), lambda i, j: (i, j))
    vec = pl.BlockSpec((1, bn), lambda i, j: (0, j))
    t_specs, t_shapes = [], []
    if t_blk is not None:
        t_specs = [pl.BlockSpec((bm // t_blk, bn, t_blk), lambda i, j: (i, j, 0))]
        t_shapes = [jax.ShapeDtypeStruct((m // t_blk, n, t_blk), out_dtypes[0])]
    a_specs = [pl.BlockSpec((bm, k), lambda i, j: (i, 0)) for k in ks]
    b_specs = [pl.BlockSpec((bn, k), lambda i, j: (j, 0)) if trans_b else pl.BlockSpec((k, bn), lambda i, j: (0, j)) for k in ks]
    return gridded(
        body, name=name, grid=(m // bm, n // bn),
        in_specs=a_specs + b_specs + [tile] * n_ex + [vec] * n_vec,
        out_specs=[tile] * n_out + t_specs + [vec] * n_sums,
        out_shape=[jax.ShapeDtypeStruct((m, n), dt) for dt in out_dtypes] + t_shapes + [jax.ShapeDtypeStruct((1, n), F32)] * n_sums,
        ins=[*a_list, *b_list, *extras, *vecs],
        semantics=("arbitrary", "arbitrary") if n_sums else ("parallel", "parallel"), side=side)


def mm_tn(a, b, *, a_pro=None, name, col_shards=1, bk=1024, bn=1024, bt=1024):
    t, k = a.shape
    n = b.shape[1]
    ns = n // col_shards
    bk, bn, bt = _fit(bk, k), _fit(bn, ns), _fit(bt, t)
    per = ns // bn

    def body(a_ref, b_ref, o_ref):
        @pl.when(pl.program_id(2) == 0)
        def _():
            o_ref[...] = jnp.zeros_like(o_ref)

        a_tile = a_ref[...] if a_pro is None else a_pro(a_ref[...])
        o_ref[...] += lax.dot_general(a_tile, b_ref[...], TN, preferred_element_type=F32)

    if col_shards == 1:
        out_spec = pl.BlockSpec((bk, bn), lambda i, j, s: (i, j))
        out_shape = jax.ShapeDtypeStruct((k, n), F32)
    else:
        out_spec = pl.BlockSpec((None, bk, bn), lambda i, j, s: (j // per, i, j % per))
        out_shape = jax.ShapeDtypeStruct((col_shards, k, ns), F32)
    return pl.pallas_call(
        body, name=name, grid=(k // bk, n // bn, t // bt),
        in_specs=[pl.BlockSpec((bt, bk), lambda i, j, s: (s, i)), pl.BlockSpec((bt, bn), lambda i, j, s: (s, j))],
        out_specs=out_spec, out_shape=out_shape,
        compiler_params=_params(("parallel", "parallel", "arbitrary")),
    )(a, b)


def _split3(x):
    hi = x.astype(BF16)
    r1 = x - hi.astype(F32)
    mid = r1.astype(BF16)
    lo = (r1 - mid.astype(F32)).astype(BF16)
    return hi, mid, lo


def cumsum_rows(x, *, reverse, name, bc=512):
    t, d = x.shape
    bc = min(bc, t)
    nb = t // bc

    def body(x_ref, o_ref, carry):
        @pl.when(pl.program_id(0) == 0)
        def _():
            carry[...] = jnp.zeros_like(carry)

        r = lax.broadcasted_iota(jnp.int32, (bc, bc), 0)
        c = lax.broadcasted_iota(jnp.int32, (bc, bc), 1)
        tri = jnp.where((r <= c) if reverse else (r >= c), 1.0, 0.0).astype(BF16)
        hi, mid, lo = _split3(x_ref[...])
        s = (lax.dot_general(tri, hi, NN, preferred_element_type=F32)
             + lax.dot_general(tri, mid, NN, preferred_element_type=F32)
             + lax.dot_general(tri, lo, NN, preferred_element_type=F32)) + carry[0:1, :]
        o_ref[...] = s
        carry[0:1, :] = s[0:1, :] if reverse else s[bc - 1:bc, :]

    imap = (lambda i: (nb - 1 - i, 0)) if reverse else (lambda i: (i, 0))
    return pl.pallas_call(
        body, name=name, grid=(nb,),
        in_specs=[pl.BlockSpec((bc, d), imap)], out_specs=pl.BlockSpec((bc, d), imap),
        out_shape=jax.ShapeDtypeStruct((t, d), F32),
        scratch_shapes=[pltpu.VMEM((8, d), F32)],
        compiler_params=_params(("arbitrary",)),
    )(x)


def _rope(x, c, a, b):
    return x * c + pltpu.roll(x, LANES - ROPE_DIM // 2, 1) * a + pltpu.roll(x, ROPE_DIM // 2, 1) * b


def _rope_bwd(d, c, a, b):
    return d * c + pltpu.roll(d * a, ROPE_DIM // 2, 1) + pltpu.roll(d * b, LANES - ROPE_DIM // 2, 1)


S_CQ, S_CKV, S_KR, S_F, S_END = 0, Q_RANK, Q_RANK + KV_RANK, Q_RANK + KV_RANK + LANES, 1024
HW = N_HEADS * LANES
FW = N_HEADS * HEAD_DIM


def mla_prep(small, g_q, g_kv, w_uq, w_ukv, tab_c, tab_a, tab_b, b_f, *, name, bt=512):
    t = small.shape[0]
    bt = min(bt, t)

    def body(s_ref, gq_ref, gkv_ref, wq_ref, wkv_ref, c_ref, a_ref, b_ref, bf_ref,
             mq_ref, mk_ref, mv_ref, lf_ref, cqn_ref, ckvn_ref, mqt_ref, mvt_ref):
        cq = s_ref[:, S_CQ:S_CKV]
        rq = lax.rsqrt(jnp.mean(cq * cq, axis=-1, keepdims=True) + EPS)
        cqn = (cq * rq * gq_ref[...]).astype(BF16)
        ckv = s_ref[:, S_CKV:S_KR]
        rkv = lax.rsqrt(jnp.mean(ckv * ckv, axis=-1, keepdims=True) + EPS)
        ckvn = (ckv * rkv * gkv_ref[...]).astype(BF16)
        cqn_ref[...] = cqn
        ckvn_ref[...] = ckvn
        tc, ta, tb = c_ref[...], a_ref[...], b_ref[...]
        q = jnp.dot(cqn, wq_ref[...], preferred_element_type=F32)
        kv = jnp.dot(ckvn, wkv_ref[...], preferred_element_type=F32)
        kr = _rope(s_ref[:, S_KR:S_F], tc, ta, tb)
        for h in range(N_HEADS):
            sl = slice(h * LANES, (h + 1) * LANES)
            roped = _rope(q[:, sl], tc, ta, tb)
            mq_ref[:, sl] = roped.astype(BF16)
            mqt_ref[0, sl, :] = roped.T.astype(BF16)
            mk_ref[:, sl] = (kv[:, sl] + kr).astype(BF16)
        mv_ref[...] = kv[:, HW:].astype(BF16)
        mvt_ref[0] = kv[:, HW:].T.astype(BF16)
        z = s_ref[:, S_F:S_END - LANES] + bf_ref[...]
        lf_ref[...] = jnp.minimum(z, 0.0) - jnp.log(1.0 + jnp.exp(-jnp.abs(z)))

    def row(w):
        return pl.BlockSpec((bt, w), lambda i: (i, 0))

    def full(arr):
        return pl.BlockSpec(arr.shape, lambda i: (0, 0))

    return pl.pallas_call(
        body, name=name, grid=(t // bt,),
        in_specs=[row(S_END), full(g_q), full(g_kv), full(w_uq), full(w_ukv), row(LANES), row(LANES), row(LANES), full(b_f)],
        out_specs=[row(HW), row(HW), row(FW), row(LANES), row(Q_RANK), row(KV_RANK),
                   pl.BlockSpec((1, HW, bt), lambda i: (i, 0, 0)), pl.BlockSpec((1, FW, bt), lambda i: (i, 0, 0))],
        out_shape=[jax.ShapeDtypeStruct((t, HW), BF16)] * 2 + [jax.ShapeDtypeStruct((t, FW), BF16), jax.ShapeDtypeStruct((t, LANES), F32),
                   jax.ShapeDtypeStruct((t, Q_RANK), BF16), jax.ShapeDtypeStruct((t, KV_RANK), BF16),
                   jax.ShapeDtypeStruct((t // bt, HW, bt), BF16), jax.ShapeDtypeStruct((t // bt, FW, bt), BF16)],
        compiler_params=_params(("parallel",)),
    )(small, g_q, g_kv, w_uq, w_ukv, tab_c, tab_a, tab_b, b_f)


def mla_prep_bwd(dmq, dmk, dmv, dlf, small, g_q, g_kv, w_uq, w_ukv, tab_c, tab_a, tab_b, b_f, *, name, bt=512):
    t = small.shape[0]
    bt = min(bt, t)

    def body(dmq_ref, dmk_ref, dmv_ref, dlf_ref, s_ref, gq_ref, gkv_ref, wq_ref, wkv_ref, c_ref, a_ref, b_ref, bf_ref,
             ds_ref, dq_ref, dkv_ref, dgq_ref, dgkv_ref, db_ref):
        tc, ta, tb = c_ref[...], a_ref[...], b_ref[...]
        lane = lax.broadcasted_iota(jnp.int32, (1, LANES), 1)
        dkr = jnp.zeros((bt, LANES), F32)
        for h in range(N_HEADS):
            sl = slice(h * LANES, (h + 1) * LANES)
            dq_ref[:, sl] = _rope_bwd(dmq_ref[:, sl], tc, ta, tb).astype(BF16)
            dkr = dkr + dmk_ref[:, sl]
        dkv_ref[:, :HW] = dmk_ref[...].astype(BF16)
        dkv_ref[:, HW:] = dmv_ref[...].astype(BF16)
        in_rope = (lane >= HEAD_DIM) & (lane < HEAD_DIM + ROPE_DIM)
        ds_ref[:, S_KR:S_F] = jnp.where(in_rope, _rope_bwd(dkr, tc, ta, tb), 0.0).astype(BF16)

        def norm_bwd(raw, g_ref, dn, dg_ref):
            r = lax.rsqrt(jnp.mean(raw * raw, axis=-1, keepdims=True) + EPS)
            u = dn * g_ref[...]
            dot = jnp.mean(u * raw, axis=-1, keepdims=True)
            dg_ref[...] += jnp.sum(dn * (raw * r), axis=0, keepdims=True)
            return r * u - raw * (r * r * r * dot)

        @pl.when(pl.program_id(0) == 0)
        def _():
            dgq_ref[...] = jnp.zeros_like(dgq_ref)
            dgkv_ref[...] = jnp.zeros_like(dgkv_ref)
            db_ref[...] = jnp.zeros_like(db_ref)

        dcqn = lax.dot_general(dq_ref[...], wq_ref[...], NT, preferred_element_type=F32)
        ds_ref[:, S_CQ:S_CKV] = norm_bwd(s_ref[:, S_CQ:S_CKV], gq_ref, dcqn, dgq_ref).astype(BF16)
        dckvn = lax.dot_general(dkv_ref[...], wkv_ref[...], NT, preferred_element_type=F32)
        ds_ref[:, S_CKV:S_KR] = norm_bwd(s_ref[:, S_CKV:S_KR], gkv_ref, dckvn, dgkv_ref).astype(BF16)
        z = s_ref[:, S_F:S_END - LANES] + bf_ref[...]
        dz = jnp.where(lane < N_HEADS, dlf_ref[...] / (1.0 + jnp.exp(z)), 0.0)
        db_ref[...] += jnp.sum(dz, axis=0, keepdims=True)
        ds_ref[:, S_F:S_END - LANES] = dz.astype(BF16)
        ds_ref[:, S_END - LANES:] = jnp.zeros((bt, LANES), BF16)

    def row(w):
        return pl.BlockSpec((bt, w), lambda i: (i, 0))

    def full(arr):
        return pl.BlockSpec(arr.shape, lambda i: (0, 0))

    def vec(w):
        return pl.BlockSpec((1, w), lambda i: (0, 0))

    return pl.pallas_call(
        body, name=name, grid=(t // bt,),
        in_specs=[row(HW), row(HW), row(FW), row(LANES), row(S_END), full(g_q), full(g_kv), full(w_uq), full(w_ukv),
                  row(LANES), row(LANES), row(LANES), full(b_f)],
        out_specs=[row(S_END), row(HW), row(HW + FW), vec(Q_RANK), vec(KV_RANK), vec(LANES)],
        out_shape=[jax.ShapeDtypeStruct((t, S_END), BF16), jax.ShapeDtypeStruct((t, HW), BF16),
                   jax.ShapeDtypeStruct((t, HW + FW), BF16), jax.ShapeDtypeStruct((1, Q_RANK), F32),
                   jax.ShapeDtypeStruct((1, KV_RANK), F32), jax.ShapeDtypeStruct((1, LANES), F32)],
        compiler_params=_params(("arbitrary",)),
    )(dmq, dmk, dmv, dlf, small, g_q, g_kv, w_uq, w_ukv, tab_c, tab_a, tab_b, b_f)


class Side:
    def __init__(self, ins, out_shapes, n_sems, first, last, mid=None):
        self.ins, self.out_shapes, self.n_sems = list(ins), list(out_shapes), n_sems
        self.first, self.mid, self.last = first, mid, last

    def specs(self):
        return [ANY] * len(self.ins), [ANY] * len(self.out_shapes)

    def sems(self):
        return [pltpu.SemaphoreType.DMA((self.n_sems,)), pltpu.SemaphoreType.DMA((self.n_sems,))]


def _lane():
    return lax.broadcasted_iota(jnp.int32, (1, LANES), 1)


def _halves(x):
    zero = jnp.zeros_like(x)
    return [jnp.where(_lane() < HEAD_DIM, x, zero), jnp.where(_lane() >= HEAD_DIM, x, zero)]


def _groups(x):
    return [x[:, :LANES], x[:, LANES:]]


def _pick_row(tile, h):
    row = lax.broadcasted_iota(jnp.int32, (tile.shape[0], 1), 0)
    return jnp.sum(jnp.where(row == h, tile, 0.0), axis=0, keepdims=True)


def _pick_lane(tile, h):
    return jnp.sum(jnp.where(_lane() == h, tile, 0.0), axis=1, keepdims=True)


def _row_halves(x):
    row = lax.broadcasted_iota(jnp.int32, (LANES, 1), 0)
    zero = jnp.zeros_like(x)
    return [jnp.where(row < HEAD_DIM, x, zero), jnp.where(row >= HEAD_DIM, x, zero)]


def _below_diagonal(s, lead=0):
    r = lax.broadcasted_iota(jnp.int32, s.shape, 0)
    c = lax.broadcasted_iota(jnp.int32, s.shape, 1)
    return jnp.where(c <= r + lead, s, -jnp.inf)


def _above_diagonal(s):
    r = lax.broadcasted_iota(jnp.int32, s.shape, 0)
    c = lax.broadcasted_iota(jnp.int32, s.shape, 1)
    return jnp.where(r <= c, s, -jnp.inf)


def _split_refs(refs, counts):
    out, at = [], 0
    for n in counts:
        out.append(refs[at:at + n])
        at += n
    return out


def flash_fwd(qt_arr, k_arr, vt_arr, f_cum, *, qoff, koff, voff, pair, scale, name, blk=512, side=None):
    t = k_arr.shape[0]
    tblk = qt_arr.shape[2]
    blk = max(min(blk, t), tblk)
    sub = blk // tblk
    nb = t // blk
    w = LANES if pair else 2 * LANES
    has_bias = f_cum is not None
    ins = [qt_arr, k_arr, vt_arr] + ([f_cum] if has_bias else [])

    def wide(ref, first):
        parts = [ref[first + u] for u in range(sub)]
        return parts[0] if sub == 1 else jnp.concatenate(parts, axis=1)
    s_ins, s_outs = (side.ins, side.out_shapes) if side else ([], [])

    def body(*refs):
        main, si, outs, so, sems = _split_refs(refs, [len(ins), len(s_ins), 2, len(s_outs), 2 if side else 0])
        qt_ref, k_ref, vt_ref = main[:3]
        f_ref = main[3] if has_bias else None
        o_ref, st_ref = outs
        g, i = pl.program_id(0), pl.program_id(1)
        step_id = g * nb + i
        if side:
            @pl.when(step_id == 0)
            def _():
                side.first(si, so, *sems)

            if side.mid is not None:
                @pl.when(step_id == (3 * PAIRS * nb) // 4)
                def _():
                    side.mid(si, so, *sems)

        qt = (wide(qt_ref, 0).astype(F32) * (scale * LOG2E)).astype(BF16)
        qts = _row_halves(qt) if pair else [qt[:LANES], qt[LANES:]]

        def step(j, carry, diagonal):
            rows = pl.ds(pl.multiple_of(j * blk, blk), blk)
            kk = k_ref[rows, :]
            ks = [kk, kk] if pair else _groups(kk)
            vt = wide(vt_ref, sub * j)
            out = []
            for n in range(2):
                m, l, acc = carry[n]
                s = jnp.dot(ks[n], qts[n], preferred_element_type=F32)
                if has_bias:
                    s = s - LOG2E * _pick_lane(f_ref[rows, :], 2 * g + n)
                if diagonal:
                    s = _above_diagonal(s)
                m_new = jnp.maximum(m, jnp.max(s, axis=0, keepdims=True))
                alpha = jnp.exp2(m - m_new)
                p = jnp.exp2(s - m_new)
                out.append((m_new, alpha * l + jnp.sum(p, axis=0, keepdims=True),
                            alpha * acc + jnp.dot(vt[n * HEAD_DIM:(n + 1) * HEAD_DIM], p.astype(BF16),
                                                  preferred_element_type=F32)))
            return tuple(out)

        def diagonal_in_halves(carry):
            h = tblk
            halves = [pl.ds(pl.multiple_of(i * blk, blk), h), pl.ds(pl.multiple_of(i * blk + h, h), h)]
            k_top, k_bot = ([x, x] if pair else _groups(x) for x in (k_ref[halves[0], :], k_ref[halves[1], :]))
            vt_top, vt_bot = vt_ref[sub * i], vt_ref[sub * i + 1]
            out = []
            for n in range(2):
                m, l, acc = carry[n]
                heads = slice(n * HEAD_DIM, (n + 1) * HEAD_DIM)
                s_top = jnp.dot(k_top[n], qts[n], preferred_element_type=F32)
                s_bot = jnp.dot(k_bot[n], qts[n][:, h:], preferred_element_type=F32)
                if has_bias:
                    s_top = s_top - LOG2E * _pick_lane(f_ref[halves[0], :], 2 * g + n)
                    s_bot = s_bot - LOG2E * _pick_lane(f_ref[halves[1], :], 2 * g + n)
                s_top, s_bot = _above_diagonal(s_top), _above_diagonal(s_bot)
                m_top = jnp.maximum(m, jnp.max(s_top, axis=0, keepdims=True))
                m_new = jnp.concatenate([m_top[:, :h], jnp.maximum(m_top[:, h:], jnp.max(s_bot, axis=0, keepdims=True))], axis=1)
                alpha = jnp.exp2(m - m_new)
                p_top = jnp.exp2(s_top - m_new)
                p_bot = jnp.exp2(s_bot - m_new[:, h:])
                late_l = jnp.concatenate([jnp.zeros((1, h), F32), jnp.sum(p_bot, axis=0, keepdims=True)], axis=1)
                late_acc = jnp.concatenate([jnp.zeros((HEAD_DIM, h), F32),
                                            jnp.dot(vt_bot[heads], p_bot.astype(BF16), preferred_element_type=F32)], axis=1)
                out.append((m_new, alpha * l + jnp.sum(p_top, axis=0, keepdims=True) + late_l,
                            alpha * acc + jnp.dot(vt_top[heads], p_top.astype(BF16), preferred_element_type=F32) + late_acc))
            return tuple(out)

        init = tuple((jnp.full((1, blk), -jnp.inf, F32), jnp.zeros((1, blk), F32), jnp.zeros((HEAD_DIM, blk), F32))
                     for _ in range(2))
        carry = lax.fori_loop(0, i, lambda j, c: step(j, c, False), init)
        (ma, la, acca), (mb, lb, accb) = diagonal_in_halves(carry) if sub == 2 else step(i, carry, True)
        o_ref[...] = jnp.concatenate([acca / la, accb / lb], axis=0).T
        row = lax.broadcasted_iota(jnp.int32, (LANES, 1), 0)
        st_ref[0] = jnp.where(row == 0, ma + jnp.log2(la), jnp.where(row == 1, mb + jnp.log2(lb), 0.0)).T
        if side:
            @pl.when(step_id == PAIRS * nb - 1)
            def _():
                side.last(si, so, *sems)

    in_specs = [pl.BlockSpec((sub, w, tblk), lambda g, i: (i, qoff + g, 0)), pl.BlockSpec((t, w), lambda g, i: (0, koff + g)),
                pl.BlockSpec((t // tblk, LANES, tblk), lambda g, i: (0, voff + g, 0))]
    if has_bias:
        in_specs.append(pl.BlockSpec((t, LANES), lambda g, i: (0, 0)))
    s_in_specs, s_out_specs = side.specs() if side else ([], [])
    return pl.pallas_call(
        body, name=name, grid=(PAIRS, nb), in_specs=in_specs + s_in_specs,
        out_specs=[pl.BlockSpec((blk, LANES), lambda g, i: (i, g)), pl.BlockSpec((1, blk, LANES), lambda g, i: (g, i, 0))]
        + s_out_specs,
        out_shape=[jax.ShapeDtypeStruct((t, PAIRS * LANES), F32), jax.ShapeDtypeStruct((PAIRS, t, LANES), F32)] + list(s_outs),
        scratch_shapes=side.sems() if side else [],
        compiler_params=_params(("arbitrary", "arbitrary")),
    )(*ins, *s_ins)


def mix_norm(fo, mo, g_fo, g_mo, *, name, bt=512):
    t, d = fo.shape
    bt = min(bt, t)

    def body(fo_ref, mo_ref, gf_ref, gm_ref, o_ref):
        for n, (x_ref, g_ref) in enumerate(((fo_ref, gf_ref), (mo_ref, gm_ref))):
            xv = x_ref[...]
            r = lax.rsqrt(jnp.mean(xv * xv, axis=-1, keepdims=True) + EPS)
            o_ref[:, n * d:(n + 1) * d] = (xv * r * g_ref[...]).astype(BF16)

    row = pl.BlockSpec((bt, d), lambda i: (i, 0))
    vec = pl.BlockSpec((1, d), lambda i: (0, 0))
    return pl.pallas_call(
        body, name=name, grid=(t // bt,), in_specs=[row, row, vec, vec],
        out_specs=pl.BlockSpec((bt, 2 * d), lambda i: (i, 0)),
        out_shape=jax.ShapeDtypeStruct((t, 2 * d), BF16),
        compiler_params=_params(("parallel",)),
    )(fo, mo, g_fo, g_mo)


def mix_norm_bwd(dmixed, fo, mo, g_fo, g_mo, st_f, st_m, *, name, bt=512, side=None):
    t, d = fo.shape
    bt = min(bt, t)

    def body(dm_ref, fo_ref, mo_ref, gf_ref, gm_ref, sf_ref, sm_ref, dfo_ref, dmo_ref, dgf_ref, dgm_ref, sfo_ref, smo_ref,
             dfot_ref, dmot_ref):
        @pl.when(pl.program_id(0) == 0)
        def _():
            dgf_ref[...] = jnp.zeros_like(dgf_ref)
            dgm_ref[...] = jnp.zeros_like(dgm_ref)

        groups = ((fo_ref, gf_ref, dfo_ref, dgf_ref, sf_ref, sfo_ref, dfot_ref),
                  (mo_ref, gm_ref, dmo_ref, dgm_ref, sm_ref, smo_ref, dmot_ref))
        for n, (x_ref, g_ref, dx_ref, dg_ref, st_ref, sto_ref, dxt_ref) in enumerate(groups):
            xv = x_ref[...]
            dhv = dm_ref[:, n * d:(n + 1) * d]
            r = lax.rsqrt(jnp.mean(xv * xv, axis=-1, keepdims=True) + EPS)
            u = dhv * g_ref[...]
            dxf = r * u - xv * (r * r * r * jnp.mean(u * xv, axis=-1, keepdims=True))
            dxb = dxf.astype(BF16)
            dx_ref[...] = dxb
            dxt_ref[0] = dxf.T.astype(BF16)
            dg_ref[...] += jnp.sum(dhv * (xv * r), axis=0, keepdims=True)
            prod = xv * dxb.astype(F32)
            for g in range(PAIRS):
                grp = prod[:, g * LANES:(g + 1) * LANES]
                da = jnp.sum(jnp.where(_lane() < HEAD_DIM, grp, 0.0), axis=1, keepdims=True)
                db = jnp.sum(jnp.where(_lane() >= HEAD_DIM, grp, 0.0), axis=1, keepdims=True)
                sto_ref[g] = jnp.where(_lane() == 2, da, jnp.where(_lane() == 3, db, st_ref[g]))

    row = pl.BlockSpec((bt, d), lambda i: (i, 0))
    vec = pl.BlockSpec((1, d), lambda i: (0, 0))
    stat = pl.BlockSpec((PAIRS, bt, LANES), lambda i: (0, i, 0))
    return gridded(
        body, name=name, grid=(t // bt,),
        in_specs=[pl.BlockSpec((bt, 2 * d), lambda i: (i, 0)), row, row, vec, vec, stat, stat],
        out_specs=[row, row, vec, vec, stat, stat] + [pl.BlockSpec((1, d, bt), lambda i: (i, 0, 0))] * 2,
        out_shape=[jax.ShapeDtypeStruct((t, d), BF16)] * 2 + [jax.ShapeDtypeStruct((1, d), F32)] * 2
        + [jax.ShapeDtypeStruct(st_f.shape, F32)] * 2 + [jax.ShapeDtypeStruct((t // bt, d, bt), BF16)] * 2,
        ins=[dmixed, fo, mo, g_fo, g_mo, st_f, st_m], semantics=("arbitrary",), side=side)


def flash_bwd(q_arr, qt_arr, k_arr, v_arr, do_arr, dot_arr, st, f_blocks, *, qoff, koff, voff, pair, scale, name, qblk=1024,
              side=None):
    t = q_arr.shape[0]
    blk = qt_arr.shape[2]
    qblk = max(min(qblk, t), blk)
    sub = qblk // blk
    nb, nbq = t // blk, t // qblk
    w = LANES if pair else 2 * LANES
    hw = w // 2
    has_bias = f_blocks is not None
    split = _halves if pair else _groups
    ins = [q_arr, qt_arr, k_arr, v_arr, do_arr, dot_arr, st] + ([f_blocks] if has_bias else [])
    n_out = 4 if has_bias else 3
    s_ins, s_outs = (side.ins, side.out_shapes) if side else ([], [])

    def wide(ref, first):
        parts = [ref[first + u] for u in range(sub)]
        return parts[0] if sub == 1 else jnp.concatenate(parts, axis=1)

    def body(*refs):
        main, si, outs, so, sems = _split_refs(refs, [len(ins), len(s_ins), n_out, len(s_outs), 2 if side else 0])
        q_ref, qt_ref, k_ref, v_ref, do_ref, dot_ref, st_ref = main[:7]
        dq_ref, dk_ref, dv_ref = outs[:3]
        g, j = pl.program_id(0), pl.program_id(1)
        step_id = g * nb + j
        if side:
            @pl.when(step_id == 0)
            def _():
                side.first(si, so, *sems)

        @pl.when(j == 0)
        def _():
            dq_ref[...] = jnp.zeros_like(dq_ref)

        kk, vv = k_ref[...], v_ref[...]
        ks = [kk, kk] if pair else _groups(kk)
        if has_bias:
            f_ref, df_ref = main[7], outs[3]
            fk = [LOG2E * _pick_row(f_ref[0], 2 * g + n) for n in range(2)]

            @pl.when(step_id == 0)
            def _():
                df_ref[...] = jnp.zeros_like(df_ref)

        def step(i, carry, diagonal):
            rows = pl.ds(pl.multiple_of(i * qblk, qblk), qblk)
            qs = split((q_ref[rows, :].astype(F32) * (scale * LOG2E)).astype(BF16))
            qt = (wide(qt_ref, sub * i).astype(F32) * (scale * LOG2E)).astype(BF16)
            dos = [h.astype(BF16) for h in _halves(do_ref[rows, :].astype(F32))]
            dot = wide(dot_ref, sub * i)
            stats = st_ref[0, rows, :]
            new, dqs, row_sums = [], [], []
            for n in range(2):
                dkt, dvt, dfk = carry[n]
                s = lax.dot_general(qs[n], ks[n], NT, preferred_element_type=F32)
                if has_bias:
                    s = s - fk[n]
                if diagonal:
                    s = _below_diagonal(s, i * qblk - j * blk)
                p = jnp.exp2(s - stats[:, n:n + 1])
                dp = lax.dot_general(dos[n], vv, NT, preferred_element_type=F32)
                ds = p * (dp - stats[:, 2 + n:3 + n])
                dsb = ds.astype(BF16)
                dvt = dvt + jnp.dot(dot[n * HEAD_DIM:(n + 1) * HEAD_DIM], p.astype(BF16), preferred_element_type=F32)
                dkt = dkt + jnp.dot(qt[n * hw:(n + 1) * hw], dsb, preferred_element_type=F32)
                dqs.append(jnp.dot(dsb, ks[n], preferred_element_type=F32))
                if has_bias:
                    dfk = dfk - jnp.sum(ds, axis=0, keepdims=True)
                    row_sums.append(jnp.sum(ds, axis=1, keepdims=True))
                new.append((dkt, dvt, dfk))
            dq = (jnp.where(_lane() < HEAD_DIM, dqs[0], dqs[1]) if pair else jnp.concatenate(dqs, axis=1)) * scale
            if has_bias:
                df_ref[rows, :] += jnp.where(_lane() == 2 * g, row_sums[0], jnp.where(_lane() == 2 * g + 1, row_sums[1], 0.0))
            dq_ref[rows, :] += dq
            return tuple(new)

        init = tuple((jnp.zeros((hw, blk), F32), jnp.zeros((HEAD_DIM, blk), F32), jnp.zeros((1, blk), F32)) for _ in range(2))
        first = j // sub
        carry = step(first, init, True)
        (dka, dva, dfa), (dkb, dvb, dfb) = lax.fori_loop(first + 1, nbq, lambda i, c: step(i, c, False), carry)
        dk_ref[...] = jnp.concatenate([dka, dkb], axis=0).T * LN2
        dv_ref[...] = jnp.concatenate([dva, dvb], axis=0).T
        if has_bias:
            row = lax.broadcasted_iota(jnp.int32, (LANES, 1), 0)
            by_head = jnp.where(row == 2 * g, dfa, jnp.where(row == 2 * g + 1, dfb, 0.0))
            df_ref[pl.ds(pl.multiple_of(j * blk, blk), blk), :] += by_head.T
        if side:
            @pl.when(step_id == PAIRS * nb - 1)
            def _():
                side.last(si, so, *sems)

    in_specs = [pl.BlockSpec((t, w), lambda g, j: (0, qoff + g)), pl.BlockSpec((nb, w, blk), lambda g, j: (0, qoff + g, 0)),
                pl.BlockSpec((blk, w), lambda g, j: (j, koff + g)), pl.BlockSpec((blk, LANES), lambda g, j: (j, voff + g)),
                pl.BlockSpec((t, LANES), lambda g, j: (0, g)), pl.BlockSpec((nb, LANES, blk), lambda g, j: (0, g, 0)),
                pl.BlockSpec((1, t, LANES), lambda g, j: (g, 0, 0))]
    out_specs = [pl.BlockSpec((t, w), lambda g, j: (0, g)), pl.BlockSpec((blk, w), lambda g, j: (j, g)),
                 pl.BlockSpec((blk, LANES), lambda g, j: (j, g))]
    out_shape = [jax.ShapeDtypeStruct((t, PAIRS * w), F32)] * 2 + [jax.ShapeDtypeStruct((t, PAIRS * LANES), F32)]
    if has_bias:
        in_specs.append(pl.BlockSpec((1, N_HEADS, blk), lambda g, j: (j, 0, 0)))
        out_specs.append(pl.BlockSpec((t, LANES), lambda g, j: (0, 0)))
        out_shape.append(jax.ShapeDtypeStruct((t, LANES), F32))
    s_in_specs, s_out_specs = side.specs() if side else ([], [])
    return pl.pallas_call(
        body, name=name, grid=(PAIRS, nb), in_specs=in_specs + s_in_specs, out_specs=out_specs + s_out_specs,
        out_shape=out_shape + list(s_outs), scratch_shapes=side.sems() if side else [],
        compiler_params=_params(("arbitrary", "arbitrary")),
    )(*ins, *s_ins)


def _adamw_math(w, g, m, v):
    nm = ADAM_B1 * m + (1.0 - ADAM_B1) * g
    nv = ADAM_B2 * v + (1.0 - ADAM_B2) * (g * g)
    m_hat = nm / (1.0 - ADAM_B1 ** ADAM_STEP)
    v_hat = nv / (1.0 - ADAM_B2 ** ADAM_STEP)
    return -ADAM_LR * (m_hat / (jnp.sqrt(v_hat) + ADAM_EPS) + ADAM_WD * w), nm, nv


def adamw(w, g, m, v, *, name):
    rws, cols = w.shape
    br = _row_block(rws)

    def body(w_ref, g_ref, m_ref, v_ref, d_ref, nm_ref, nv_ref):
        d_ref[...], nm_ref[...], nv_ref[...] = _adamw_math(w_ref[...], g_ref[...], m_ref[...], v_ref[...])

    blk = pl.BlockSpec((br, cols), lambda i: (i, 0))
    return pl.pallas_call(
        body, name=name, grid=(rws // br,), in_specs=[blk] * 4, out_specs=[blk] * 3,
        out_shape=[jax.ShapeDtypeStruct((rws, cols), F32)] * 3,
        compiler_params=_params(("parallel",)),
    )(w, g, m, v)


def adamw_halves(w, g_mine, g_other, m, v, core, *, name):
    _, k, n = w.shape
    br = _row_block(k // 2)
    nh = k // 2 // br

    def body(c_ref, w_ref, gm_ref, go_ref, m_ref, v_ref, g_out, d_ref, nm_ref, nv_ref):
        gv = jnp.where(pl.program_id(0) == c_ref[0], gm_ref[...], go_ref[...])
        g_out[0] = gv
        d_ref[0], nm_ref[0], nv_ref[0] = _adamw_math(w_ref[0], gv, m_ref[0], v_ref[0])

    full = pl.BlockSpec((1, br, n), lambda hb, i, c: (0, hb * nh + i, 0))
    half = pl.BlockSpec((br, n), lambda hb, i, c: (i, 0))
    return pl.pallas_call(
        body, name=name,
        grid_spec=pltpu.PrefetchScalarGridSpec(num_scalar_prefetch=1, grid=(2, nh), in_specs=[full, half, half, full, full],
                                               out_specs=[full] * 4),
        out_shape=[jax.ShapeDtypeStruct(w.shape, F32)] * 4,
        compiler_params=_params(("parallel", "parallel")),
    )(core, w, g_mine, g_other, m, v)


def adamw_halves_t(wt, gt_mine, gt_other, mt, vt, core, *, name, bc=128):
    n, k = wt.shape
    nh = k // 2 // bc

    def body(c_ref, w_ref, gm_ref, go_ref, m_ref, v_ref, g_out, d_ref, nm_ref, nv_ref):
        gv = jnp.where(pl.program_id(0) == c_ref[0], gm_ref[...], go_ref[...])
        g_out[...] = gv
        d_ref[...], nm_ref[...], nv_ref[...] = _adamw_math(w_ref[...], gv, m_ref[...], v_ref[...])

    full = pl.BlockSpec((n, bc), lambda hb, i, c: (0, hb * nh + i))
    half = pl.BlockSpec((n, bc), lambda hb, i, c: (0, i))
    return pl.pallas_call(
        body, name=name,
        grid_spec=pltpu.PrefetchScalarGridSpec(num_scalar_prefetch=1, grid=(2, nh), in_specs=[full, half, half, full, full],
                                               out_specs=[full] * 4),
        out_shape=[jax.ShapeDtypeStruct(wt.shape, F32)] * 4,
        compiler_params=_params(("parallel", "parallel")),
    )(core, wt, gt_mine, gt_other, mt, vt)


def add_pair(dw, recv, core, *, name):
    n4, k, n = dw.shape
    half = (1, k // 2, n) if split_axis(k) == 0 else (1, k, n // 2)
    mine = (lambda q, c: (q, c[0], 0)) if split_axis(k) == 0 else (lambda q, c: (q, 0, c[0]))

    def body(c_ref, a_ref, b_ref, o_ref):
        o_ref[...] = (a_ref[...] + b_ref[...].astype(F32)).astype(BF16)

    return pl.pallas_call(
        body, name=name,
        grid_spec=pltpu.PrefetchScalarGridSpec(
            num_scalar_prefetch=1, grid=(n4,),
            in_specs=[pl.BlockSpec(half, mine), pl.BlockSpec(half, lambda q, c: (q, 0, 0))],
            out_specs=pl.BlockSpec(half, lambda q, c: (q, 0, 0))),
        out_shape=jax.ShapeDtypeStruct((n4,) + half[1:], BF16),
        compiler_params=_params(("parallel",)),
    )(core, dw, recv)


def sum_chips(parts, *, name):
    n4, r, n = parts.shape
    if r % 16 == 0:
        br, bc = _row_block(r), n
    else:
        br, bc = r, LANES

    def body(p_ref, o_ref):
        acc = p_ref[0].astype(F32)
        for q in range(1, n4):
            acc = acc + p_ref[q].astype(F32)
        o_ref[...] = acc

    return pl.pallas_call(
        body, name=name, grid=(r // br, n // bc),
        in_specs=[pl.BlockSpec((n4, br, bc), lambda i, j: (0, i, j))], out_specs=pl.BlockSpec((br, bc), lambda i, j: (i, j)),
        out_shape=jax.ShapeDtypeStruct((r, n), F32),
        compiler_params=_params(("parallel", "parallel")),
    )(parts)


ANY = pl.BlockSpec(memory_space=pl.ANY)


def _place():
    x, y, c = lax.axis_index("x"), lax.axis_index("y"), lax.axis_index("c")
    chips = [(1 - x, y), (x, 1 - y), (1 - x, 1 - y)]
    return x, y, c, chips


def _copy(src, dst, send_sems, recv_sems, k, to):
    return pltpu.make_async_remote_copy(src_ref=src, dst_ref=dst, send_sem=send_sems.at[k], recv_sem=recv_sems.at[k],
                                        device_id=to, device_id_type=MESH)


def split_axis(rows):
    return 0 if rows % 32 == 0 else 1


def _half(ref, lead, hf):
    rows, cols = ref.shape[-2:]
    if split_axis(rows) == 0:
        at = (pl.ds(hf * (rows // 2), rows // 2), slice(None))
    else:
        at = (slice(None), pl.ds(hf * (cols // 2), cols // 2))
    return ref.at[at] if lead is None else ref.at[(lead,) + at]


def _gather_first(srcs, dsts, ssems, rsems):
    x, y, c, chips = _place()
    for ti, (s, d) in enumerate(zip(srcs, dsts)):
        for j, (cx, cy) in enumerate(chips):
            _copy(_half(s, None, c), _half(d, 2 * x + y, c), ssems, rsems, 3 * ti + j, (cx, cy, c)).start()


def _gather_mid(srcs, dsts, ssems, rsems):
    x, y, c, chips = _place()
    n1 = 3 * len(srcs)
    for ti, d in enumerate(dsts):
        for j, (cx, cy) in enumerate(chips):
            landed = _half(d, 2 * cx + cy, c)
            _copy(landed, landed, ssems, rsems, 3 * ti + j, (cx, cy, c)).wait_recv()
            _copy(landed, landed, ssems, rsems, n1 + 3 * ti + j, (x, y, 1 - c)).start()


def _gather_last(srcs, dsts, ssems, rsems):
    x, y, c, chips = _place()
    n1 = 3 * len(srcs)
    for ti, (s, d) in enumerate(zip(srcs, dsts)):
        for j, (cx, cy) in enumerate(chips):
            other = _half(d, 2 * cx + cy, 1 - c)
            _copy(other, other, ssems, rsems, n1 + 3 * ti + j, (x, y, 1 - c)).wait_recv()
        for j, (cx, cy) in enumerate(chips):
            mine = _half(s, None, c)
            _copy(mine, mine, ssems, rsems, 3 * ti + j, (cx, cy, c)).wait_send()
            _copy(mine, mine, ssems, rsems, n1 + 3 * ti + j, (x, y, 1 - c)).wait_send()


def gather_side(shards):
    return Side(shards, [jax.ShapeDtypeStruct((N_CHIPS,) + s.shape, s.dtype) for s in shards], 6 * len(shards),
                _gather_first, _gather_last, _gather_mid)


def _scatter_first(srcs, dsts, ssems, rsems):
    x, y, c, chips = _place()
    for ti, (s, d) in enumerate(zip(srcs, dsts)):
        for j, (cx, cy) in enumerate(chips):
            _copy(s.at[2 * cx + cy], d.at[2 * x + y], ssems, rsems, 3 * ti + j, (cx, cy, c)).start()


def _scatter_last(srcs, dsts, ssems, rsems):
    x, y, c, chips = _place()
    for ti, (s, d) in enumerate(zip(srcs, dsts)):
        for j, (cx, cy) in enumerate(chips):
            _copy(s.at[2 * cx + cy], d.at[2 * cx + cy], ssems, rsems, 3 * ti + j, (cx, cy, c)).wait_recv()
        for j, (cx, cy) in enumerate(chips):
            _copy(s.at[2 * cx + cy], d.at[2 * cx + cy], ssems, rsems, 3 * ti + j, (cx, cy, c)).wait_send()


def scatter_side(parts):
    return Side(parts, [jax.ShapeDtypeStruct(p.shape, p.dtype) for p in parts], 3 * len(parts), _scatter_first, _scatter_last)


def run_side(side, *, name):
    n_in, n_out = len(side.ins), len(side.out_shapes)

    def body(*refs):
        si, so, sems = _split_refs(refs, [n_in, n_out, 2])
        side.first(si, so, *sems)
        if side.mid is not None:
            side.mid(si, so, *sems)
        side.last(si, so, *sems)

    in_specs, out_specs = side.specs()
    return pl.pallas_call(body, name=name, in_specs=in_specs, out_specs=out_specs, out_shape=side.out_shapes,
                          scratch_shapes=side.sems())(*side.ins)


def _swap_first(srcs, dsts, ssems, rsems):
    x, y, c, _ = _place()
    for k, (s, d) in enumerate(zip(srcs, dsts)):
        _copy(s, d, ssems, rsems, k, (x, y, 1 - c)).start()


def _swap_last(srcs, dsts, ssems, rsems):
    x, y, c, _ = _place()
    for k, (s, d) in enumerate(zip(srcs, dsts)):
        _copy(s, d, ssems, rsems, k, (x, y, 1 - c)).wait()


def swap_side(xs):
    return Side(xs, [jax.ShapeDtypeStruct(a.shape, a.dtype) for a in xs], len(xs), _swap_first, _swap_last)


def allreduce_small(s):
    n_dev = 8

    def body(s_ref, out_ref, buf, send_sems, recv_sems):
        x, y, c, _ = _place()
        me = 4 * x + 2 * y + c
        buf[me] = s_ref[...]
        sends = []
        for k in range(1, n_dev):
            px = 1 - x if k & 4 else x
            py = 1 - y if k & 2 else y
            pc = 1 - c if k & 1 else c
            cp = _copy(s_ref, buf.at[me], send_sems, recv_sems, k - 1, (px, py, pc))
            cp.start()
            sends.append((cp, 4 * px + 2 * py + pc))
        for k, (cp, peer) in enumerate(sends):
            _copy(s_ref, buf.at[peer], send_sems, recv_sems, k, (x, y, c)).wait_recv()
        for cp, _ in sends:
            cp.wait_send()
        acc = buf[0]
        for d in range(1, n_dev):
            acc = acc + buf[d]
        out_ref[...] = acc

    vm = pl.BlockSpec(memory_space=pltpu.VMEM)
    return pl.pallas_call(
        body, name="allreduce_small", in_specs=[vm], out_specs=vm,
        out_shape=jax.ShapeDtypeStruct(s.shape, F32),
        scratch_shapes=[pltpu.VMEM((n_dev,) + s.shape, F32), pltpu.SemaphoreType.DMA((n_dev - 1,)),
                        pltpu.SemaphoreType.DMA((n_dev - 1,))],
    )(s)


def join_cols(sm):
    n4, k, n = sm.shape
    return sm.transpose(1, 0, 2).reshape(k, n4 * n)


def split_cols(full):
    k, n = full.shape
    return full.reshape(k, N_CHIPS, n // N_CHIPS).transpose(1, 0, 2)


def _pad_heads(w, width):
    lead = w.shape[:-1]
    w = w.reshape(lead + (N_HEADS, width))
    return jnp.pad(w, [(0, 0)] * len(lead) + [(0, 0), (0, LANES - width)]).reshape(lead + (HW,))


def _unpad_heads(w, width):
    lead = w.shape[:-1]
    return w.reshape(lead + (N_HEADS, LANES))[..., :width].reshape(lead + (N_HEADS * width,))


def split_w_in_t(w_in_t):
    d = w_in_t.shape[1]
    o_f = 3 * FW
    o_cq = o_f + N_HEADS
    o_ckv = o_cq + Q_RANK
    o_kr = o_ckv + KV_RANK

    def z(n):
        return jnp.zeros((n, d), w_in_t.dtype)

    small = jnp.concatenate([w_in_t[o_cq:o_ckv], w_in_t[o_ckv:o_kr], z(HEAD_DIM), w_in_t[o_kr:], z(LANES - HEAD_DIM - ROPE_DIM),
                             w_in_t[o_f:o_cq], z(LANES - N_HEADS), z(LANES)], axis=0)
    return w_in_t[:o_f], small


def join_w_in_t(d_qkv_t, d_small_t):
    kr = S_KR + HEAD_DIM
    return jnp.concatenate([d_qkv_t, d_small_t[S_F:S_F + N_HEADS], d_small_t[S_CQ:S_CKV], d_small_t[S_CKV:S_KR],
                            d_small_t[kr:kr + ROPE_DIM]], axis=0)


def rope_tables(pos):
    t = pos.shape[0]
    inv_freq = ROPE_THETA ** (-jnp.arange(0, ROPE_DIM, 2, dtype=F32) / ROPE_DIM)
    ang = pos.astype(F32)[:, None] * inv_freq
    cos, sin = jnp.cos(ang), jnp.sin(ang)
    half = ROPE_DIM // 2

    def z(n):
        return jnp.zeros((t, n), F32)

    tab_c = jnp.concatenate([jnp.ones((t, HEAD_DIM), F32), cos, cos, z(LANES - HEAD_DIM - ROPE_DIM)], axis=1)
    tab_a = jnp.concatenate([z(HEAD_DIM), -sin, z(half), z(LANES - HEAD_DIM - ROPE_DIM)], axis=1)
    tab_b = jnp.concatenate([z(HEAD_DIM), z(half), sin, z(LANES - HEAD_DIM - ROPE_DIM)], axis=1)
    return tab_c, tab_a, tab_b


def _pad_lanes(v, n):
    return jnp.pad(v, ((0, 0), (0, n - v.shape[1])))


ATTN_BLK = 512
ATTN_FWD_BLK = 1024
ATTN_BWD_QBLK = 1024


def local_step(xs, pos, tgt, gains, early_weights, late_weights, early_side=None, fwd_sides=(None, None), reduction=None):
    g_attn, b_forget, g_q, g_kv, g_fo, g_mo, g_mlp, g_fin = gains
    t = xs.shape[0]
    blk = min(ATTN_BLK, t)
    fox_scale = 1.0 / (HEAD_DIM ** 0.5)
    mla_scale = 1.0 / ((HEAD_DIM + ROPE_DIM) ** 0.5)

    h1, *gathered = rmsnorm(xs, g_attn, out_dtype=BF16, name="norm_attn", side=early_side)
    w_in_t, w_uq, w_ukv = early_weights(gathered)
    w_qkv_t, w_small_t = split_w_in_t(w_in_t)
    w_uq_p = _pad_heads(w_uq, HEAD_DIM + ROPE_DIM)
    kv = w_ukv.reshape(KV_RANK, N_HEADS, 2 * HEAD_DIM)
    w_ukv_p = jnp.concatenate([_pad_heads(kv[:, :, :HEAD_DIM].reshape(KV_RANK, FW), HEAD_DIM),
                               kv[:, :, HEAD_DIM:].reshape(KV_RANK, FW)], axis=1)
    b_f = _pad_lanes(b_forget, LANES)
    tab_c, tab_a, tab_b = rope_tables(pos)

    qkv, qkv_t = mm(h1, w_qkv_t, trans_b=True, out_dtypes=[BF16], t_blk=blk, name="proj_qkv")
    small, = mm(h1, w_small_t, trans_b=True, out_dtypes=[F32], name="proj_small")
    mq, mk, mv, lf, cqn, ckvn, mq_t, mv_t = mla_prep(small, g_q, g_kv, w_uq_p, w_ukv_p, tab_c, tab_a, tab_b, b_f,
                                                     name="mla_prep", bt=blk)
    f_cum = cumsum_rows(lf, reverse=False, name="gate_cumsum")
    f_blocks = f_cum[:, :N_HEADS].reshape(t // blk, blk, N_HEADS).transpose(0, 2, 1)
    fo, st_f, *gathered = flash_fwd(qkv_t, qkv, qkv_t, f_cum, qoff=0, koff=PAIRS, voff=2 * PAIRS, pair=True,
                                    scale=fox_scale, name="fox_fwd", blk=ATTN_FWD_BLK, side=fwd_sides[0])
    mo, st_m, *more = flash_fwd(mq_t, mk, mv_t, None, qoff=0, koff=0, voff=0, pair=False, scale=mla_scale, name="mla_fwd",
                                blk=ATTN_FWD_BLK, side=fwd_sides[1])
    w_o, w_up, w_down = late_weights(gathered + more)
    mixed = mix_norm(fo, mo, g_fo, g_mo, name="norm_mix")

    def inv_rms(v):
        return lax.rsqrt(jnp.mean(v * v, axis=-1, keepdims=True) + EPS)

    def residual_then_norm(acc, res, g):
        xn = acc + res
        return xn, xn * inv_rms(xn) * g

    def norm_bwd(dh, xn, res, g):
        r = inv_rms(xn)
        uu = dh * g
        return (r * uu - xn * (r * r * r * jnp.mean(uu * xn, axis=-1, keepdims=True)) + res,
                jnp.sum(dh * (xn * r), axis=0, keepdims=True))

    def norm_bwd2(dh, xn, res, g):
        dx, dg = norm_bwd(dh, xn, res, g)
        return dx, dx, dg

    def residual_then_loss(acc, res, target, g):
        xn = acc + res
        r = inv_rms(xn)
        xh = xn * r
        e = xh * g - target
        part = 0.5 * jnp.sum(jnp.mean(e * e, axis=-1, keepdims=True), axis=0, keepdims=True)
        dy = e * (1.0 / xn.shape[1])
        uu = dy * g
        dx = r * uu - xn * (r * r * r * jnp.mean(uu * xn, axis=-1, keepdims=True))
        return dx, dx, jnp.sum(dy * xh, axis=0, keepdims=True), part + jnp.zeros_like(g)

    x1, h2 = mm(mixed, w_o, extras=[xs], vecs=[g_mlp], epilogue=residual_then_norm, out_dtypes=[F32, BF16], name="out_proj")

    def relu2(uu):
        r = jnp.maximum(uu.astype(F32), 0.0)
        return (r * r).astype(BF16)

    u, = mm(h2, w_up, out_dtypes=[BF16], name="mlp_up")
    dx2, dx2b, dg_fin, loss_row = mm(u, w_down, a_pro=relu2, extras=[x1, tgt], vecs=[g_fin], epilogue=residual_then_loss,
                                     out_dtypes=[F32, BF16], n_sums=2, name="mlp_down_loss")
    loss = loss_row[:, :1]

    def relu2_grad(acc, uu):
        return (acc * (2.0 * jnp.maximum(uu.astype(F32), 0.0)),)

    du, = mm(dx2b, w_down, trans_b=True, extras=[u], epilogue=relu2_grad, out_dtypes=[BF16], name="mlp_down_bwd")
    dw_down = mm_tn(u, dx2b, a_pro=relu2, name="dw_down").reshape(N_CHIPS, -1, w_down.shape[1])
    dx1, dx1b, dg_mlp = mm(du, w_up, trans_b=True, extras=[x1, dx2], vecs=[g_mlp], epilogue=norm_bwd2,
                           out_dtypes=[F32, BF16], n_sums=1, name="mlp_up_bwd")
    dw_up = mm_tn(h2, du, name="dw_up", col_shards=N_CHIPS)

    dmixed, = mm(dx1b, w_o, trans_b=True, out_dtypes=[F32], name="out_proj_bwd")
    dw_o = mm_tn(mixed, dx1b, name="dw_o").reshape(N_CHIPS, -1, w_o.shape[1])
    late = (dw_o, dw_up, dw_down)
    red = reduction
    dfo, dmo, dg_fo, dg_mo, st_f, st_m, dfo_t, dmo_t, *got = mix_norm_bwd(
        dmixed, fo, mo, g_fo, g_mo, st_f, st_m, name="norm_mix_bwd", bt=blk, side=red.late_swap(late) if red else None)
    dfq, dfk, dfv, d_f, *got = flash_bwd(
        qkv, qkv_t, qkv, qkv, dfo, dfo_t, st_f, f_blocks, qoff=0, koff=PAIRS, voff=2 * PAIRS, pair=True,
        scale=fox_scale, name="fox_bwd", qblk=ATTN_BWD_QBLK, side=red.late_scatter(got) if red else None)
    dmq, dmk, dmv, *got = flash_bwd(mq, mq_t, mk, mv, dmo, dmo_t, st_m, None, qoff=0, koff=0, voff=0,
                                    pair=False, scale=mla_scale, name="mla_bwd", qblk=ATTN_BWD_QBLK,
                                    side=red.late_halves(got) if red else None)
    if red:
        red.late_done(got)
    dqkv = jnp.concatenate([dfq, dfk, dfv], axis=1).astype(BF16)
    dlf = cumsum_rows(d_f, reverse=True, name="gate_cumsum_bwd")
    dsmall, dq_u, dkv_u, dg_q, dg_kv, db_f = mla_prep_bwd(dmq, dmk, dmv, dlf, small, g_q, g_kv, w_uq_p, w_ukv_p,
                                                          tab_c, tab_a, tab_b, b_f, name="mla_prep_bwd")
    dw_uq_p = mm_tn(cqn, dq_u, name="dw_uq")
    dw_ukv_p = mm_tn(ckvn, dkv_u, name="dw_ukv")

    dw_in_t = join_w_in_t(mm_tn(dqkv, h1, name="dw_qkv"), mm_tn(dsmall, h1, name="dw_small"))
    dw_uq = _unpad_heads(dw_uq_p, HEAD_DIM + ROPE_DIM)
    dk_cols = _unpad_heads(dw_ukv_p[:, :HW], HEAD_DIM).reshape(KV_RANK, N_HEADS, HEAD_DIM)
    dv_cols = dw_ukv_p[:, HW:].reshape(KV_RANK, N_HEADS, HEAD_DIM)
    dw_ukv = jnp.concatenate([dk_cols, dv_cols], axis=2).reshape(KV_RANK, N_HEADS * 2 * HEAD_DIM)
    early = (dw_in_t.reshape(N_CHIPS, -1, dw_in_t.shape[1]), split_cols(dw_uq), split_cols(dw_ukv))
    grad_x, dg_attn, *got = mm([dqkv, dsmall], [w_qkv_t, w_small_t], extras=[xs, dx1], vecs=[g_attn],
                               epilogue=norm_bwd, out_dtypes=[F32], n_sums=1, name="proj_bwd",
                               side=red.early_scatter(early) if red else None)
    if red:
        red.early_done(got)
    d_gains = (dg_attn, db_f[:, :N_HEADS], dg_q, dg_kv, dg_fo, dg_mo, dg_mlp, dg_fin)
    return loss, grad_x, early, late, d_gains


class GradReduction:
    def __init__(self, core_id, chip):
        self.core_id, self.chip = core_id, chip
        self.core = core_id.reshape(1).astype(jnp.int32)
        self.grads, self.pairs, self.halves, self.others = {}, {}, {}, {}

    def _other_halves(self, grads):
        out = []
        for g in grads:
            axis = 1 + split_axis(g.shape[1])
            size = g.shape[axis] // 2
            out.append(lax.dynamic_slice_in_dim(g, (1 - self.core_id) * size, size, axis=axis).astype(BF16))
        return out

    def _add_pairs(self, group, recvs):
        self.pairs[group] = [add_pair(g, r, self.core, name="add_pair_%s_%d" % (group, n))
                             for n, (g, r) in enumerate(zip(self.grads[group], recvs))]
        return scatter_side(self.pairs[group])

    def _chip_sums(self, group, scattered):
        chip = self.chip
        with_mine = [lax.dynamic_update_index_in_dim(s, lax.dynamic_index_in_dim(p, chip, 0, keepdims=True), chip, 0)
                     for s, p in zip(scattered, self.pairs[group])]
        self.halves[group] = [sum_chips(s, name="sum_chips_%s_%d" % (group, n)) for n, s in enumerate(with_mine)]
        return self.halves[group]

    def late_swap(self, grads):
        self.grads["late"] = list(grads)
        return swap_side(self._other_halves(grads))

    def late_scatter(self, recvs):
        return self._add_pairs("late", recvs)

    def late_halves(self, scattered):
        return swap_side(self._chip_sums("late", scattered))

    def late_done(self, others):
        self.others["late"] = list(others)

    def early_scatter(self, grads):
        self.grads["early"] = list(grads)
        return self._add_pairs("early", run_side(swap_side(self._other_halves(grads)), name="swap_early_sends"))

    def early_done(self, scattered):
        self.others["early"] = list(run_side(swap_side(self._chip_sums("early", scattered)), name="swap_early_halves"))

def kernel(x, positions, attn_norm_g, w_in, b_forget, q_norm_g, w_uq, kv_norm_g, w_ukv, fox_out_g, mla_out_g, w_o, mlp_norm_g, w_up, w_down, final_norm_g, loss_target, m_attn_norm_g, m_w_in, m_b_forget, m_q_norm_g, m_w_uq, m_kv_norm_g, m_w_ukv, m_fox_out_g, m_mla_out_g, m_w_o, m_mlp_norm_g, m_w_up, m_w_down, m_final_norm_g, v_attn_norm_g, v_w_in, v_b_forget, v_q_norm_g, v_w_uq, v_kv_norm_g, v_w_ukv, v_fox_out_g, v_mla_out_g, v_w_o, v_mlp_norm_g, v_w_up, v_w_down, v_final_norm_g):
    core_id = lax.axis_index("c")
    core = core_id.reshape(1).astype(jnp.int32)
    chip = 2 * lax.axis_index("x") + lax.axis_index("y")
    big = [w_in, w_uq, w_ukv, w_o, w_up, w_down]
    big_m = [m_w_in, m_w_uq, m_w_ukv, m_w_o, m_w_up, m_w_down]
    big_v = [v_w_in, v_w_uq, v_w_ukv, v_w_o, v_w_up, v_w_down]
    n_early = 3

    def vec(a):
        return a.reshape(1, -1)

    small = [attn_norm_g, b_forget, q_norm_g, kv_norm_g, fox_out_g, mla_out_g, mlp_norm_g, final_norm_g]
    small_m = [m_attn_norm_g, m_b_forget, m_q_norm_g, m_kv_norm_g, m_fox_out_g, m_mla_out_g, m_mlp_norm_g, m_final_norm_g]
    small_v = [v_attn_norm_g, v_b_forget, v_q_norm_g, v_kv_norm_g, v_fox_out_g, v_mla_out_g, v_mlp_norm_g, v_final_norm_g]
    gains = [vec(a) for a in small]

    shards = [(w[0].T if n == 0 else w[0]).astype(BF16) for n, w in enumerate(big)]

    def with_own(gathered, mine):
        return [lax.dynamic_update_index_in_dim(g, s, chip, 0) for g, s in zip(gathered, mine)]

    def early_weights(gathered):
        g_in, g_uq, g_ukv = with_own(gathered, shards[:n_early])
        return g_in.reshape(-1, g_in.shape[2]), join_cols(g_uq), join_cols(g_ukv)

    def late_weights(gathered):
        g_o, g_up, g_down = with_own(gathered, shards[n_early:])
        return g_o.reshape(-1, g_o.shape[2]), join_cols(g_up), g_down.reshape(-1, g_down.shape[2])

    reduction = GradReduction(core_id, chip)
    loss, grad_x, _, _, d_small = local_step(
        x[0], positions[0], loss_target[0], gains, early_weights, late_weights, gather_side(shards[:n_early]),
        (gather_side(shards[n_early:-1]), gather_side(shards[-1:])), reduction)
    halves = reduction.halves["early"] + reduction.halves["late"]
    others = reduction.others["early"] + reduction.others["late"]

    def rows8(vs):
        return jnp.concatenate([_pad_lanes(vec(a).astype(F32), 1024) for a in vs], axis=0)

    with_loss = [jnp.concatenate([d, loss], axis=1) if n == 1 else d for n, d in enumerate(d_small)]
    g_small8 = allreduce_small(rows8(with_loss))

    outs_big = []
    for n, (w, gm, go, m, v) in enumerate(zip(big, halves, others, big_m, big_v)):
        _, k, cols = w.shape
        if n == 0:
            outs = adamw_halves_t(w[0].T, gm, go, m[0].T, v[0].T, core, name="adamw_%d" % n)
            outs_big.append([o.T[None] for o in outs])
        else:
            outs_big.append(adamw_halves(w, gm, go, m, v, core, name="adamw_%d" % n))
    d8, m8, v8 = adamw(rows8(small), g_small8, rows8(small_m), rows8(small_v), name="adamw_small")

    def unrows8(a8):
        return [a8[n, :s.size].reshape(s.shape) for n, s in enumerate(small)]

    loss_all = g_small8[1, N_HEADS]
    grads, deltas, new_m, new_v = [None] * 14, [None] * 14, [None] * 14, [None] * 14
    big_at = [1, 4, 6, 9, 11, 12]
    small_at = [0, 2, 3, 5, 7, 8, 10, 13]
    for n, at in enumerate(big_at):
        grads[at], deltas[at], new_m[at], new_v[at] = outs_big[n]
    for at, g, dd, mm_, vv in zip(small_at, unrows8(g_small8), unrows8(d8), unrows8(m8), unrows8(v8)):
        grads[at], deltas[at], new_m[at], new_v[at] = g, dd, mm_, vv
    return (loss_all, grad_x[None], *grads, *deltas, *new_m, *new_v)
```

```python
import jax
import jax.numpy as jnp
from jax import lax
from jax.experimental import pallas as pl
from jax.experimental.pallas import tpu as pltpu

F32 = jnp.float32
BF16 = jnp.bfloat16
MESH = pl.DeviceIdType.MESH

EPS = 1e-6
ROPE_THETA = 10000.0
N_HEADS = 8
PAIRS = N_HEADS // 2
HEAD_DIM = 64
ROPE_DIM = 32
LANES = 128
Q_RANK = 384
KV_RANK = 256
N_CHIPS = 4
ADAM_LR, ADAM_B1, ADAM_B2, ADAM_EPS, ADAM_WD, ADAM_STEP = 0.001, 0.9, 0.999, 1e-08, 0.01, 10
VMEM_LIMIT = 48 * 1024 * 1024
LOG2E = 1.4426950408889634
LN2 = 0.6931471805599453
NN = (((1,), (0,)), ((), ()))
NT = (((1,), (1,)), ((), ()))
TN = (((0,), (0,)), ((), ()))


def _params(sem=None):
    return pltpu.CompilerParams(dimension_semantics=sem, vmem_limit_bytes=VMEM_LIMIT)


def _fit(block, dim):
    if dim <= block:
        return dim
    return next(b for b in range(block - block % LANES, 0, -LANES) if dim % b == 0)


def _row_block(rows):
    return next(b for b in (256, 128, 64, 32, 16, 8) if rows % b == 0)


def gridded(body, *, name, grid, in_specs, out_specs, out_shape, ins, semantics, side=None):
    if side is None:
        return pl.pallas_call(body, name=name, grid=grid, in_specs=in_specs, out_specs=out_specs, out_shape=out_shape,
                              compiler_params=_params(semantics))(*ins)
    n_in, n_out = len(in_specs), len(out_specs)
    steps = 1
    for extent in grid:
        steps *= extent

    def riding(*refs):
        main_in, s_in, main_out, s_out, sems = _split_refs(refs, [n_in, len(side.ins), n_out, len(side.out_shapes), 2])
        step = 0
        for axis, extent in enumerate(grid):
            step = step * extent + pl.program_id(axis)

        @pl.when(step == 0)
        def _():
            side.first(s_in, s_out, *sems)

        body(*main_in, *main_out)

        @pl.when(step == steps - 1)
        def _():
            if side.mid is not None:
                side.mid(s_in, s_out, *sems)
            side.last(s_in, s_out, *sems)

    s_in_specs, s_out_specs = side.specs()
    return pl.pallas_call(
        riding, name=name, grid=grid, in_specs=list(in_specs) + s_in_specs, out_specs=list(out_specs) + s_out_specs,
        out_shape=list(out_shape) + side.out_shapes, scratch_shapes=side.sems(),
        compiler_params=_params(("arbitrary",) * len(grid)))(*ins, *side.ins)


def rmsnorm(x, g, *, out_dtype, name, bt=512, side=None):
    t, d = x.shape
    bt = min(bt, t)

    def body(x_ref, g_ref, o_ref):
        xv = x_ref[...].astype(F32)
        r = lax.rsqrt(jnp.mean(xv * xv, axis=-1, keepdims=True) + EPS)
        o_ref[...] = (xv * r * g_ref[...]).astype(o_ref.dtype)

    return gridded(
        body, name=name, grid=(t // bt,),
        in_specs=[pl.BlockSpec((bt, d), lambda i: (i, 0)), pl.BlockSpec((1, d), lambda i: (0, 0))],
        out_specs=[pl.BlockSpec((bt, d), lambda i: (i, 0))],
        out_shape=[jax.ShapeDtypeStruct((t, d), out_dtype)],
        ins=[x, g], semantics=("parallel",), side=side)


def mm(a, b, *, trans_b=False, a_pro=None, extras=(), vecs=(), epilogue=None, out_dtypes, n_sums=0, t_blk=None, name,
       bm=1024, bn=1024, side=None):
    a_list = list(a) if isinstance(a, (list, tuple)) else [a]
    b_list = list(b) if isinstance(b, (list, tuple)) else [b]
    m = a_list[0].shape[0]
    n = b_list[0].shape[0] if trans_b else b_list[0].shape[1]
    ks = [x.shape[1] for x in a_list]
    if sum(ks) > 2048:
        bm = bm // 2
    bm, bn = _fit(bm, m), _fit(bn, n)
    assert n_sums == 0 or bn == n
    n_ab, n_ex, n_vec, n_out = len(a_list), len(extras), len(vecs), len(out_dtypes)
    n_t = 0 if t_blk is None else 1

    def body(*refs):
        a_refs, b_refs, ex, vs, outs, t_outs, sums = _split_refs(refs, [n_ab, n_ab, n_ex, n_vec, n_out, n_t, n_sums])
        acc = None
        for a_ref, b_ref in zip(a_refs, b_refs):
            a_tile = a_ref[...] if a_pro is None else a_pro(a_ref[...])
            part = lax.dot_general(a_tile, b_ref[...], NT if trans_b else NN, preferred_element_type=F32)
            acc = part if acc is None else acc + part
        res = epilogue(acc, *[e[...] for e in ex], *[v[...] for v in vs]) if epilogue is not None else (acc,)
        for o, r in zip(outs, res[:n_out]):
            o[...] = r.astype(o.dtype)
        for t_ref in t_outs:
            for u in range(bm // t_blk):
                t_ref[u] = res[0][u * t_blk:(u + 1) * t_blk, :].T.astype(t_ref.dtype)
        if n_sums:
            @pl.when(pl.program_id(0) == 0)
            def _():
                for s_ref in sums:
                    s_ref[...] = jnp.zeros_like(s_ref)

            for s_ref, r in zip(sums, res[n_out:]):
                s_ref[...] += r

    tile = pl.BlockSpec((bm, bn), lambda i, j: (i, j))
    vec = pl.BlockSpec((1, bn), lambda i, j: (0, j))
    t_specs, t_shapes = [], []
    if t_blk is not None:
        t_specs = [pl.BlockSpec((bm // t_blk, bn, t_blk), lambda i, j: (i, j, 0))]
        t_shapes = [jax.ShapeDtypeStruct((m // t_blk, n, t_blk), out_dtypes[0])]
    a_specs = [pl.BlockSpec((bm, k), lambda i, j: (i, 0)) for k in ks]
    b_specs = [pl.BlockSpec((bn, k), lambda i, j: (j, 0)) if trans_b else pl.BlockSpec((k, bn), lambda i, j: (0, j)) for k in ks]
    return gridded(
        body, name=name, grid=(m // bm, n // bn),
        in_specs=a_specs + b_specs + [tile] * n_ex + [vec] * n_vec,
        out_specs=[tile] * n_out + t_specs + [vec] * n_sums,
        out_shape=[jax.ShapeDtypeStruct((m, n), dt) for dt in out_dtypes] + t_shapes + [jax.ShapeDtypeStruct((1, n), F32)] * n_sums,
        ins=[*a_list, *b_list, *extras, *vecs],
        semantics=("arbitrary", "arbitrary") if n_sums else ("parallel", "parallel"), side=side)


def mm_tn(a, b, *, a_pro=None, name, col_shards=1, bk=1024, bn=1024, bt=1024):
    t, k = a.shape
    n = b.shape[1]
    ns = n // col_shards
    bk, bn, bt = _fit(bk, k), _fit(bn, ns), _fit(bt, t)
    per = ns // bn

    def body(a_ref, b_ref, o_ref):
        @pl.when(pl.program_id(2) == 0)
        def _():
            o_ref[...] = jnp.zeros_like(o_ref)

        a_tile = a_ref[...] if a_pro is None else a_pro(a_ref[...])
        o_ref[...] += lax.dot_general(a_tile, b_ref[...], TN, preferred_element_type=F32)

    if col_shards == 1:
        out_spec = pl.BlockSpec((bk, bn), lambda i, j, s: (i, j))
        out_shape = jax.ShapeDtypeStruct((k, n), F32)
    else:
        out_spec = pl.BlockSpec((None, bk, bn), lambda i, j, s: (j // per, i, j % per))
        out_shape = jax.ShapeDtypeStruct((col_shards, k, ns), F32)
    return pl.pallas_call(
        body, name=name, grid=(k // bk, n // bn, t // bt),
        in_specs=[pl.BlockSpec((bt, bk), lambda i, j, s: (s, i)), pl.BlockSpec((bt, bn), lambda i, j, s: (s, j))],
        out_specs=out_spec, out_shape=out_shape,
        compiler_params=_params(("parallel", "parallel", "arbitrary")),
    )(a, b)


def _split3(x):
    hi = x.astype(BF16)
    r1 = x - hi.astype(F32)
    mid = r1.astype(BF16)
    lo = (r1 - mid.astype(F32)).astype(BF16)
    return hi, mid, lo


def cumsum_rows(x, *, reverse, name, bc=512):
    t, d = x.shape
    bc = min(bc, t)
    nb = t // bc

    def body(x_ref, o_ref, carry):
        @pl.when(pl.program_id(0) == 0)
        def _():
            carry[...] = jnp.zeros_like(carry)

        r = lax.broadcasted_iota(jnp.int32, (bc, bc), 0)
        c = lax.broadcasted_iota(jnp.int32, (bc, bc), 1)
        tri = jnp.where((r <= c) if reverse else (r >= c), 1.0, 0.0).astype(BF16)
        hi, mid, lo = _split3(x_ref[...])
        s = (lax.dot_general(tri, hi, NN, preferred_element_type=F32)
             + lax.dot_general(tri, mid, NN, preferred_element_type=F32)
             + lax.dot_general(tri, lo, NN, preferred_element_type=F32)) + carry[0:1, :]
        o_ref[...] = s
        carry[0:1, :] = s[0:1, :] if reverse else s[bc - 1:bc, :]

    imap = (lambda i: (nb - 1 - i, 0)) if reverse else (lambda i: (i, 0))
    return pl.pallas_call(
        body, name=name, grid=(nb,),
        in_specs=[pl.BlockSpec((bc, d), imap)], out_specs=pl.BlockSpec((bc, d), imap),
        out_shape=jax.ShapeDtypeStruct((t, d), F32),
        scratch_shapes=[pltpu.VMEM((8, d), F32)],
        compiler_params=_params(("arbitrary",)),
    )(x)


def _rope(x, c, a, b):
    return x * c + pltpu.roll(x, LANES - ROPE_DIM // 2, 1) * a + pltpu.roll(x, ROPE_DIM // 2, 1) * b


def _rope_bwd(d, c, a, b):
    return d * c + pltpu.roll(d * a, ROPE_DIM // 2, 1) + pltpu.roll(d * b, LANES - ROPE_DIM // 2, 1)


S_CQ, S_CKV, S_KR, S_F, S_END = 0, Q_RANK, Q_RANK + KV_RANK, Q_RANK + KV_RANK + LANES, 1024
HW = N_HEADS * LANES
FW = N_HEADS * HEAD_DIM


def mla_prep(small, g_q, g_kv, w_uq, w_ukv, tab_c, tab_a, tab_b, b_f, *, name, bt=512):
    t = small.shape[0]
    bt = min(bt, t)

    def body(s_ref, gq_ref, gkv_ref, wq_ref, wkv_ref, c_ref, a_ref, b_ref, bf_ref,
             mq_ref, mk_ref, mv_ref, lf_ref, cqn_ref, ckvn_ref, mqt_ref, mvt_ref):
        cq = s_ref[:, S_CQ:S_CKV]
        rq = lax.rsqrt(jnp.mean(cq * cq, axis=-1, keepdims=True) + EPS)
        cqn = (cq * rq * gq_ref[...]).astype(BF16)
        ckv = s_ref[:, S_CKV:S_KR]
        rkv = lax.rsqrt(jnp.mean(ckv * ckv, axis=-1, keepdims=True) + EPS)
        ckvn = (ckv * rkv * gkv_ref[...]).astype(BF16)
        cqn_ref[...] = cqn
        ckvn_ref[...] = ckvn
        tc, ta, tb = c_ref[...], a_ref[...], b_ref[...]
        q = jnp.dot(cqn, wq_ref[...], preferred_element_type=F32)
        kv = jnp.dot(ckvn, wkv_ref[...], preferred_element_type=F32)
        kr = _rope(s_ref[:, S_KR:S_F], tc, ta, tb)
        for h in range(N_HEADS):
            sl = slice(h * LANES, (h + 1) * LANES)
            roped = _rope(q[:, sl], tc, ta, tb)
            mq_ref[:, sl] = roped.astype(BF16)
            mqt_ref[0, sl, :] = roped.T.astype(BF16)
            mk_ref[:, sl] = (kv[:, sl] + kr).astype(BF16)
        mv_ref[...] = kv[:, HW:].astype(BF16)
        mvt_ref[0] = kv[:, HW:].T.astype(BF16)
        z = s_ref[:, S_F:S_END - LANES] + bf_ref[...]
        lf_ref[...] = jnp.minimum(z, 0.0) - jnp.log(1.0 + jnp.exp(-jnp.abs(z)))

    def row(w):
        return pl.BlockSpec((bt, w), lambda i: (i, 0))

    def full(arr):
        return pl.BlockSpec(arr.shape, lambda i: (0, 0))

    return pl.pallas_call(
        body, name=name, grid=(t // bt,),
        in_specs=[row(S_END), full(g_q), full(g_kv), full(w_uq), full(w_ukv), row(LANES), row(LANES), row(LANES), full(b_f)],
        out_specs=[row(HW), row(HW), row(FW), row(LANES), row(Q_RANK), row(KV_RANK),
                   pl.BlockSpec((1, HW, bt), lambda i: (i, 0, 0)), pl.BlockSpec((1, FW, bt), lambda i: (i, 0, 0))],
        out_shape=[jax.ShapeDtypeStruct((t, HW), BF16)] * 2 + [jax.ShapeDtypeStruct((t, FW), BF16), jax.ShapeDtypeStruct((t, LANES), F32),
                   jax.ShapeDtypeStruct((t, Q_RANK), BF16), jax.ShapeDtypeStruct((t, KV_RANK), BF16),
                   jax.ShapeDtypeStruct((t // bt, HW, bt), BF16), jax.ShapeDtypeStruct((t // bt, FW, bt), BF16)],
        compiler_params=_params(("parallel",)),
    )(small, g_q, g_kv, w_uq, w_ukv, tab_c, tab_a, tab_b, b_f)


def mla_prep_bwd(dmq, dmk, dmv, dlf, small, g_q, g_kv, w_uq, w_ukv, tab_c, tab_a, tab_b, b_f, *, name, bt=512):
    t = small.shape[0]
    bt = min(bt, t)

    def body(dmq_ref, dmk_ref, dmv_ref, dlf_ref, s_ref, gq_ref, gkv_ref, wq_ref, wkv_ref, c_ref, a_ref, b_ref, bf_ref,
             ds_ref, dq_ref, dkv_ref, dgq_ref, dgkv_ref, db_ref):
        tc, ta, tb = c_ref[...], a_ref[...], b_ref[...]
        lane = lax.broadcasted_iota(jnp.int32, (1, LANES), 1)
        dkr = jnp.zeros((bt, LANES), F32)
        for h in range(N_HEADS):
            sl = slice(h * LANES, (h + 1) * LANES)
            dq_ref[:, sl] = _rope_bwd(dmq_ref[:, sl], tc, ta, tb).astype(BF16)
            dkr = dkr + dmk_ref[:, sl]
        dkv_ref[:, :HW] = dmk_ref[...].astype(BF16)
        dkv_ref[:, HW:] = dmv_ref[...].astype(BF16)
        in_rope = (lane >= HEAD_DIM) & (lane < HEAD_DIM + ROPE_DIM)
        ds_ref[:, S_KR:S_F] = jnp.where(in_rope, _rope_bwd(dkr, tc, ta, tb), 0.0).astype(BF16)

        def norm_bwd(raw, g_ref, dn, dg_ref):
            r = lax.rsqrt(jnp.mean(raw * raw, axis=-1, keepdims=True) + EPS)
            u = dn * g_ref[...]
            dot = jnp.mean(u * raw, axis=-1, keepdims=True)
            dg_ref[...] += jnp.sum(dn * (raw * r), axis=0, keepdims=True)
            return r * u - raw * (r * r * r * dot)

        @pl.when(pl.program_id(0) == 0)
        def _():
            dgq_ref[...] = jnp.zeros_like(dgq_ref)
            dgkv_ref[...] = jnp.zeros_like(dgkv_ref)
            db_ref[...] = jnp.zeros_like(db_ref)

        dcqn = lax.dot_general(dq_ref[...], wq_ref[...], NT, preferred_element_type=F32)
        ds_ref[:, S_CQ:S_CKV] = norm_bwd(s_ref[:, S_CQ:S_CKV], gq_ref, dcqn, dgq_ref).astype(BF16)
        dckvn = lax.dot_general(dkv_ref[...], wkv_ref[...], NT, preferred_element_type=F32)
        ds_ref[:, S_CKV:S_KR] = norm_bwd(s_ref[:, S_CKV:S_KR], gkv_ref, dckvn, dgkv_ref).astype(BF16)
        z = s_ref[:, S_F:S_END - LANES] + bf_ref[...]
        dz = jnp.where(lane < N_HEADS, dlf_ref[...] / (1.0 + jnp.exp(z)), 0.0)
        db_ref[...] += jnp.sum(dz, axis=0, keepdims=True)
        ds_ref[:, S_F:S_END - LANES] = dz.astype(BF16)
        ds_ref[:, S_END - LANES:] = jnp.zeros((bt, LANES), BF16)

    def row(w):
        return pl.BlockSpec((bt, w), lambda i: (i, 0))

    def full(arr):
        return pl.BlockSpec(arr.shape, lambda i: (0, 0))

    def vec(w):
        return pl.BlockSpec((1, w), lambda i: (0, 0))

    return pl.pallas_call(
        body, name=name, grid=(t // bt,),
        in_specs=[row(HW), row(HW), row(FW), row(LANES), row(S_END), full(g_q), full(g_kv), full(w_uq), full(w_ukv),
                  row(LANES), row(LANES), row(LANES), full(b_f)],
        out_specs=[row(S_END), row(HW), row(HW + FW), vec(Q_RANK), vec(KV_RANK), vec(LANES)],
        out_shape=[jax.ShapeDtypeStruct((t, S_END), BF16), jax.ShapeDtypeStruct((t, HW), BF16),
                   jax.ShapeDtypeStruct((t, HW + FW), BF16), jax.ShapeDtypeStruct((1, Q_RANK), F32),
                   jax.ShapeDtypeStruct((1, KV_RANK), F32), jax.ShapeDtypeStruct((1, LANES), F32)],
        compiler_params=_params(("arbitrary",)),
    )(dmq, dmk, dmv, dlf, small, g_q, g_kv, w_uq, w_ukv, tab_c, tab_a, tab_b, b_f)


class Side:
    def __init__(self, ins, out_shapes, n_sems, first, last, mid=None):
        self.ins, self.out_shapes, self.n_sems = list(ins), list(out_shapes), n_sems
        self.first, self.mid, self.last = first, mid, last

    def specs(self):
        return [ANY] * len(self.ins), [ANY] * len(self.out_shapes)

    def sems(self):
        return [pltpu.SemaphoreType.DMA((self.n_sems,)), pltpu.SemaphoreType.DMA((self.n_sems,))]


def _lane():
    return lax.broadcasted_iota(jnp.int32, (1, LANES), 1)


def _halves(x):
    zero = jnp.zeros_like(x)
    return [jnp.where(_lane() < HEAD_DIM, x, zero), jnp.where(_lane() >= HEAD_DIM, x, zero)]


def _groups(x):
    return [x[:, :LANES], x[:, LANES:]]


def _pick_row(tile, h):
    row = lax.broadcasted_iota(jnp.int32, (tile.shape[0], 1), 0)
    return jnp.sum(jnp.where(row == h, tile, 0.0), axis=0, keepdims=True)


def _pick_lane(tile, h):
    return jnp.sum(jnp.where(_lane() == h, tile, 0.0), axis=1, keepdims=True)


def _row_halves(x):
    row = lax.broadcasted_iota(jnp.int32, (LANES, 1), 0)
    zero = jnp.zeros_like(x)
    return [jnp.where(row < HEAD_DIM, x, zero), jnp.where(row >= HEAD_DIM, x, zero)]


def _below_diagonal(s, lead=0):
    r = lax.broadcasted_iota(jnp.int32, s.shape, 0)
    c = lax.broadcasted_iota(jnp.int32, s.shape, 1)
    return jnp.where(c <= r + lead, s, -jnp.inf)


def _above_diagonal(s):
    r = lax.broadcasted_iota(jnp.int32, s.shape, 0)
    c = lax.broadcasted_iota(jnp.int32, s.shape, 1)
    return jnp.where(r <= c, s, -jnp.inf)


def _split_refs(refs, counts):
    out, at = [], 0
    for n in counts:
        out.append(refs[at:at + n])
        at += n
    return out


def flash_fwd(qt_arr, k_arr, vt_arr, f_cum, *, qoff, koff, voff, pair, scale, name, blk=512, side=None):
    t = k_arr.shape[0]
    tblk = qt_arr.shape[2]
    blk = max(min(blk, t), tblk)
    sub = blk // tblk
    nb = t // blk
    w = LANES if pair else 2 * LANES
    has_bias = f_cum is not None
    ins = [qt_arr, k_arr, vt_arr] + ([f_cum] if has_bias else [])

    def wide(ref, first):
        parts = [ref[first + u] for u in range(sub)]
        return parts[0] if sub == 1 else jnp.concatenate(parts, axis=1)
    s_ins, s_outs = (side.ins, side.out_shapes) if side else ([], [])

    def body(*refs):
        main, si, outs, so, sems = _split_refs(refs, [len(ins), len(s_ins), 2, len(s_outs), 2 if side else 0])
        qt_ref, k_ref, vt_ref = main[:3]
        f_ref = main[3] if has_bias else None
        o_ref, st_ref = outs
        g, i = pl.program_id(0), pl.program_id(1)
        step_id = g * nb + i
        if side:
            @pl.when(step_id == 0)
            def _():
                side.first(si, so, *sems)

            if side.mid is not None:
                @pl.when(step_id == (3 * PAIRS * nb) // 4)
                def _():
                    side.mid(si, so, *sems)

        qt = (wide(qt_ref, 0).astype(F32) * (scale * LOG2E)).astype(BF16)
        qts = _row_halves(qt) if pair else [qt[:LANES], qt[LANES:]]

        def step(j, carry, diagonal):
            rows = pl.ds(pl.multiple_of(j * blk, blk), blk)
            kk = k_ref[rows, :]
            ks = [kk, kk] if pair else _groups(kk)
            vt = wide(vt_ref, sub * j)
            out = []
            for n in range(2):
                m, l, acc = carry[n]
                s = jnp.dot(ks[n], qts[n], preferred_element_type=F32)
                if has_bias:
                    s = s - LOG2E * _pick_lane(f_ref[rows, :], 2 * g + n)
                if diagonal:
                    s = _above_diagonal(s)
                m_new = jnp.maximum(m, jnp.max(s, axis=0, keepdims=True))
                alpha = jnp.exp2(m - m_new)
                p = jnp.exp2(s - m_new)
                out.append((m_new, alpha * l + jnp.sum(p, axis=0, keepdims=True),
                            alpha * acc + jnp.dot(vt[n * HEAD_DIM:(n + 1) * HEAD_DIM], p.astype(BF16),
                                                  preferred_element_type=F32)))
            return tuple(out)

        def diagonal_in_halves(carry):
            h = tblk
            halves = [pl.ds(pl.multiple_of(i * blk, blk), h), pl.ds(pl.multiple_of(i * blk + h, h), h)]
            k_top, k_bot = ([x, x] if pair else _groups(x) for x in (k_ref[halves[0], :], k_ref[halves[1], :]))
            vt_top, vt_bot = vt_ref[sub * i], vt_ref[sub * i + 1]
            out = []
            for n in range(2):
                m, l, acc = carry[n]
                heads = slice(n * HEAD_DIM, (n + 1) * HEAD_DIM)
                s_top = jnp.dot(k_top[n], qts[n], preferred_element_type=F32)
                s_bot = jnp.dot(k_bot[n], qts[n][:, h:], preferred_element_type=F32)
                if has_bias:
                    s_top = s_top - LOG2E * _pick_lane(f_ref[halves[0], :], 2 * g + n)
                    s_bot = s_bot - LOG2E * _pick_lane(f_ref[halves[1], :], 2 * g + n)
                s_top, s_bot = _above_diagonal(s_top), _above_diagonal(s_bot)
                m_top = jnp.maximum(m, jnp.max(s_top, axis=0, keepdims=True))
                m_new = jnp.concatenate([m_top[:, :h], jnp.maximum(m_top[:, h:], jnp.max(s_bot, axis=0, keepdims=True))], axis=1)
                alpha = jnp.exp2(m - m_new)
                p_top = jnp.exp2(s_top - m_new)
                p_bot = jnp.exp2(s_bot - m_new[:, h:])
                late_l = jnp.concatenate([jnp.zeros((1, h), F32), jnp.sum(p_bot, axis=0, keepdims=True)], axis=1)
                late_acc = jnp.concatenate([jnp.zeros((HEAD_DIM, h), F32),
                                            jnp.dot(vt_bot[heads], p_bot.astype(BF16), preferred_element_type=F32)], axis=1)
                out.append((m_new, alpha * l + jnp.sum(p_top, axis=0, keepdims=True) + late_l,
                            alpha * acc + jnp.dot(vt_top[heads], p_top.astype(BF16), preferred_element_type=F32) + late_acc))
            return tuple(out)

        init = tuple((jnp.full((1, blk), -jnp.inf, F32), jnp.zeros((1, blk), F32), jnp.zeros((HEAD_DIM, blk), F32))
                     for _ in range(2))
        carry = lax.fori_loop(0, i, lambda j, c: step(j, c, False), init)
        (ma, la, acca), (mb, lb, accb) = diagonal_in_halves(carry) if sub == 2 else step(i, carry, True)
        o_ref[...] = jnp.concatenate([acca / la, accb / lb], axis=0).T
        row = lax.broadcasted_iota(jnp.int32, (LANES, 1), 0)
        st_ref[0] = jnp.where(row == 0, ma + jnp.log2(la), jnp.where(row == 1, mb + jnp.log2(lb), 0.0)).T
        if side:
            @pl.when(step_id == PAIRS * nb - 1)
            def _():
                side.last(si, so, *sems)

    in_specs = [pl.BlockSpec((sub, w, tblk), lambda g, i: (i, qoff + g, 0)), pl.BlockSpec((t, w), lambda g, i: (0, koff + g)),
                pl.BlockSpec((t // tblk, LANES, tblk), lambda g, i: (0, voff + g, 0))]
    if has_bias:
        in_specs.append(pl.BlockSpec((t, LANES), lambda g, i: (0, 0)))
    s_in_specs, s_out_specs = side.specs() if side else ([], [])
    return pl.pallas_call(
        body, name=name, grid=(PAIRS, nb), in_specs=in_specs + s_in_specs,
        out_specs=[pl.BlockSpec((blk, LANES), lambda g, i: (i, g)), pl.BlockSpec((1, blk, LANES), lambda g, i: (g, i, 0))]
        + s_out_specs,
        out_shape=[jax.ShapeDtypeStruct((t, PAIRS * LANES), F32), jax.ShapeDtypeStruct((PAIRS, t, LANES), F32)] + list(s_outs),
        scratch_shapes=side.sems() if side else [],
        compiler_params=_params(("arbitrary", "arbitrary")),
    )(*ins, *s_ins)


def mix_norm(fo, mo, g_fo, g_mo, *, name, bt=512):
    t, d = fo.shape
    bt = min(bt, t)

    def body(fo_ref, mo_ref, gf_ref, gm_ref, o_ref):
        for n, (x_ref, g_ref) in enumerate(((fo_ref, gf_ref), (mo_ref, gm_ref))):
            xv = x_ref[...]
            r = lax.rsqrt(jnp.mean(xv * xv, axis=-1, keepdims=True) + EPS)
            o_ref[:, n * d:(n + 1) * d] = (xv * r * g_ref[...]).astype(BF16)

    row = pl.BlockSpec((bt, d), lambda i: (i, 0))
    vec = pl.BlockSpec((1, d), lambda i: (0, 0))
    return pl.pallas_call(
        body, name=name, grid=(t // bt,), in_specs=[row, row, vec, vec],
        out_specs=pl.BlockSpec((bt, 2 * d), lambda i: (i, 0)),
        out_shape=jax.ShapeDtypeStruct((t, 2 * d), BF16),
        compiler_params=_params(("parallel",)),
    )(fo, mo, g_fo, g_mo)


def mix_norm_bwd(dx1b, w_o, fo, mo, g_fo, g_mo, st_f, st_m, *, name, bt=512, side=None):
    t, d = fo.shape
    bt = min(bt, t)

    def body(dx_in_ref, w_ref, fo_ref, mo_ref, gf_ref, gm_ref, sf_ref, sm_ref, dfo_ref, dmo_ref, dgf_ref, dgm_ref,
             sfo_ref, smo_ref, dfot_ref, dmot_ref):
        @pl.when(pl.program_id(0) == 0)
        def _():
            dgf_ref[...] = jnp.zeros_like(dgf_ref)
            dgm_ref[...] = jnp.zeros_like(dgm_ref)

        dmixed = lax.dot_general(dx_in_ref[...], w_ref[...], NT, preferred_element_type=F32)
        groups = ((fo_ref, gf_ref, dfo_ref, dgf_ref, sf_ref, sfo_ref, dfot_ref),
                  (mo_ref, gm_ref, dmo_ref, dgm_ref, sm_ref, smo_ref, dmot_ref))
        for n, (x_ref, g_ref, dx_ref, dg_ref, st_ref, sto_ref, dxt_ref) in enumerate(groups):
            xv = x_ref[...]
            dhv = dmixed[:, n * d:(n + 1) * d]
            r = lax.rsqrt(jnp.mean(xv * xv, axis=-1, keepdims=True) + EPS)
            u = dhv * g_ref[...]
            dxf = r * u - xv * (r * r * r * jnp.mean(u * xv, axis=-1, keepdims=True))
            dxb = dxf.astype(BF16)
            dx_ref[...] = dxb
            dxt_ref[0] = dxf.T.astype(BF16)
            dg_ref[...] += jnp.sum(dhv * (xv * r), axis=0, keepdims=True)
            prod = xv * dxb.astype(F32)
            for g in range(PAIRS):
                grp = prod[:, g * LANES:(g + 1) * LANES]
                da = jnp.sum(jnp.where(_lane() < HEAD_DIM, grp, 0.0), axis=1, keepdims=True)
                db = jnp.sum(jnp.where(_lane() >= HEAD_DIM, grp, 0.0), axis=1, keepdims=True)
                sto_ref[g] = jnp.where(_lane() == 2, da, jnp.where(_lane() == 3, db, st_ref[g]))

    row = pl.BlockSpec((bt, d), lambda i: (i, 0))
    vec = pl.BlockSpec((1, d), lambda i: (0, 0))
    stat = pl.BlockSpec((PAIRS, bt, LANES), lambda i: (0, i, 0))
    return gridded(
        body, name=name, grid=(t // bt,),
        in_specs=[pl.BlockSpec((bt, dx1b.shape[1]), lambda i: (i, 0)), pl.BlockSpec(w_o.shape, lambda i: (0, 0)),
                  row, row, vec, vec, stat, stat],
        out_specs=[row, row, vec, vec, stat, stat] + [pl.BlockSpec((1, d, bt), lambda i: (i, 0, 0))] * 2,
        out_shape=[jax.ShapeDtypeStruct((t, d), BF16)] * 2 + [jax.ShapeDtypeStruct((1, d), F32)] * 2
        + [jax.ShapeDtypeStruct(st_f.shape, F32)] * 2 + [jax.ShapeDtypeStruct((t // bt, d, bt), BF16)] * 2,
        ins=[dx1b, w_o, fo, mo, g_fo, g_mo, st_f, st_m], semantics=("arbitrary",), side=side)


def flash_bwd(q_arr, qt_arr, k_arr, v_arr, do_arr, dot_arr, st, f_blocks, *, qoff, koff, voff, pair, scale, name, qblk=1024,
              side=None):
    t = q_arr.shape[0]
    blk = qt_arr.shape[2]
    qblk = max(min(qblk, t), blk)
    sub = qblk // blk
    nb, nbq = t // blk, t // qblk
    w = LANES if pair else 2 * LANES
    hw = w // 2
    has_bias = f_blocks is not None
    split = _halves if pair else _groups
    ins = [q_arr, qt_arr, k_arr, v_arr, do_arr, dot_arr, st] + ([f_blocks] if has_bias else [])
    n_out = 4 if has_bias else 3
    s_ins, s_outs = (side.ins, side.out_shapes) if side else ([], [])

    def wide(ref, first):
        parts = [ref[first + u] for u in range(sub)]
        return parts[0] if sub == 1 else jnp.concatenate(parts, axis=1)

    def body(*refs):
        main, si, outs, so, sems = _split_refs(refs, [len(ins), len(s_ins), n_out, len(s_outs), 2 if side else 0])
        q_ref, qt_ref, k_ref, v_ref, do_ref, dot_ref, st_ref = main[:7]
        dq_ref, dk_ref, dv_ref = outs[:3]
        g, j = pl.program_id(0), pl.program_id(1)
        step_id = g * nb + j
        if side:
            @pl.when(step_id == 0)
            def _():
                side.first(si, so, *sems)

        @pl.when(j == 0)
        def _():
            dq_ref[...] = jnp.zeros_like(dq_ref)

        kk, vv = k_ref[...], v_ref[...]
        ks = [kk, kk] if pair else _groups(kk)
        if has_bias:
            f_ref, df_ref = main[7], outs[3]
            fk = [LOG2E * _pick_row(f_ref[0], 2 * g + n) for n in range(2)]

            @pl.when(step_id == 0)
            def _():
                df_ref[...] = jnp.zeros_like(df_ref)

        def step(i, carry, diagonal):
            rows = pl.ds(pl.multiple_of(i * qblk, qblk), qblk)
            qs = split((q_ref[rows, :].astype(F32) * (scale * LOG2E)).astype(BF16))
            qt = (wide(qt_ref, sub * i).astype(F32) * (scale * LOG2E)).astype(BF16)
            dos = [h.astype(BF16) for h in _halves(do_ref[rows, :].astype(F32))]
            dot = wide(dot_ref, sub * i)
            stats = st_ref[0, rows, :]
            new, dqs, row_sums = [], [], []
            for n in range(2):
                dkt, dvt, dfk = carry[n]
                s = lax.dot_general(qs[n], ks[n], NT, preferred_element_type=F32)
                if has_bias:
                    s = s - fk[n]
                if diagonal:
                    s = _below_diagonal(s, i * qblk - j * blk)
                p = jnp.exp2(s - stats[:, n:n + 1])
                dp = lax.dot_general(dos[n], vv, NT, preferred_element_type=F32)
                ds = p * (dp - stats[:, 2 + n:3 + n])
                dsb = ds.astype(BF16)
                dvt = dvt + jnp.dot(dot[n * HEAD_DIM:(n + 1) * HEAD_DIM], p.astype(BF16), preferred_element_type=F32)
                dkt = dkt + jnp.dot(qt[n * hw:(n + 1) * hw], dsb, preferred_element_type=F32)
                dqs.append(jnp.dot(dsb, ks[n], preferred_element_type=F32))
                if has_bias:
                    dfk = dfk - jnp.sum(ds, axis=0, keepdims=True)
                    row_sums.append(jnp.sum(ds, axis=1, keepdims=True))
                new.append((dkt, dvt, dfk))
            dq = (jnp.where(_lane() < HEAD_DIM, dqs[0], dqs[1]) if pair else jnp.concatenate(dqs, axis=1)) * scale
            if has_bias:
                df_ref[rows, :] += jnp.where(_lane() == 2 * g, row_sums[0], jnp.where(_lane() == 2 * g + 1, row_sums[1], 0.0))
            dq_ref[rows, :] += dq
            return tuple(new)

        init = tuple((jnp.zeros((hw, blk), F32), jnp.zeros((HEAD_DIM, blk), F32), jnp.zeros((1, blk), F32)) for _ in range(2))
        first = j // sub
        carry = step(first, init, True)
        (dka, dva, dfa), (dkb, dvb, dfb) = lax.fori_loop(first + 1, nbq, lambda i, c: step(i, c, False), carry)
        dk_ref[...] = jnp.concatenate([dka, dkb], axis=0).T * LN2
        dv_ref[...] = jnp.concatenate([dva, dvb], axis=0).T
        if has_bias:
            row = lax.broadcasted_iota(jnp.int32, (LANES, 1), 0)
            by_head = jnp.where(row == 2 * g, dfa, jnp.where(row == 2 * g + 1, dfb, 0.0))
            df_ref[pl.ds(pl.multiple_of(j * blk, blk), blk), :] += by_head.T
        if side:
            @pl.when(step_id == PAIRS * nb - 1)
            def _():
                side.last(si, so, *sems)

    in_specs = [pl.BlockSpec((t, w), lambda g, j: (0, qoff + g)), pl.BlockSpec((nb, w, blk), lambda g, j: (0, qoff + g, 0)),
                pl.BlockSpec((blk, w), lambda g, j: (j, koff + g)), pl.BlockSpec((blk, LANES), lambda g, j: (j, voff + g)),
                pl.BlockSpec((t, LANES), lambda g, j: (0, g)), pl.BlockSpec((nb, LANES, blk), lambda g, j: (0, g, 0)),
                pl.BlockSpec((1, t, LANES), lambda g, j: (g, 0, 0))]
    out_specs = [pl.BlockSpec((t, w), lambda g, j: (0, g)), pl.BlockSpec((blk, w), lambda g, j: (j, g)),
                 pl.BlockSpec((blk, LANES), lambda g, j: (j, g))]
    out_shape = [jax.ShapeDtypeStruct((t, PAIRS * w), F32)] * 2 + [jax.ShapeDtypeStruct((t, PAIRS * LANES), F32)]
    if has_bias:
        in_specs.append(pl.BlockSpec((1, N_HEADS, blk), lambda g, j: (j, 0, 0)))
        out_specs.append(pl.BlockSpec((t, LANES), lambda g, j: (0, 0)))
        out_shape.append(jax.ShapeDtypeStruct((t, LANES), F32))
    s_in_specs, s_out_specs = side.specs() if side else ([], [])
    return pl.pallas_call(
        body, name=name, grid=(PAIRS, nb), in_specs=in_specs + s_in_specs, out_specs=out_specs + s_out_specs,
        out_shape=out_shape + list(s_outs), scratch_shapes=side.sems() if side else [],
        compiler_params=_params(("arbitrary", "arbitrary")),
    )(*ins, *s_ins)


def _adamw_math(w, g, m, v):
    nm = ADAM_B1 * m + (1.0 - ADAM_B1) * g
    nv = ADAM_B2 * v + (1.0 - ADAM_B2) * (g * g)
    m_hat = nm / (1.0 - ADAM_B1 ** ADAM_STEP)
    v_hat = nv / (1.0 - ADAM_B2 ** ADAM_STEP)
    return -ADAM_LR * (m_hat / (jnp.sqrt(v_hat) + ADAM_EPS) + ADAM_WD * w), nm, nv


def adamw(w, g, m, v, *, name):
    rws, cols = w.shape
    br = _row_block(rws)

    def body(w_ref, g_ref, m_ref, v_ref, d_ref, nm_ref, nv_ref):
        d_ref[...], nm_ref[...], nv_ref[...] = _adamw_math(w_ref[...], g_ref[...], m_ref[...], v_ref[...])

    blk = pl.BlockSpec((br, cols), lambda i: (i, 0))
    return pl.pallas_call(
        body, name=name, grid=(rws // br,), in_specs=[blk] * 4, out_specs=[blk] * 3,
        out_shape=[jax.ShapeDtypeStruct((rws, cols), F32)] * 3,
        compiler_params=_params(("parallel",)),
    )(w, g, m, v)


def adamw_halves(w, g_mine, g_other, m, v, core, *, name):
    _, k, n = w.shape
    br = _row_block(k // 2)
    nh = k // 2 // br

    def body(c_ref, w_ref, gm_ref, go_ref, m_ref, v_ref, g_out, d_ref, nm_ref, nv_ref):
        gv = jnp.where(pl.program_id(0) == c_ref[0], gm_ref[...], go_ref[...])
        g_out[0] = gv
        d_ref[0], nm_ref[0], nv_ref[0] = _adamw_math(w_ref[0], gv, m_ref[0], v_ref[0])

    full = pl.BlockSpec((1, br, n), lambda hb, i, c: (0, hb * nh + i, 0))
    half = pl.BlockSpec((br, n), lambda hb, i, c: (i, 0))
    return pl.pallas_call(
        body, name=name,
        grid_spec=pltpu.PrefetchScalarGridSpec(num_scalar_prefetch=1, grid=(2, nh), in_specs=[full, half, half, full, full],
                                               out_specs=[full] * 4),
        out_shape=[jax.ShapeDtypeStruct(w.shape, F32)] * 4,
        compiler_params=_params(("parallel", "parallel")),
    )(core, w, g_mine, g_other, m, v)


def adamw_halves_t(wt, gt_mine, gt_other, mt, vt, core, *, name, bc=128):
    n, k = wt.shape
    nh = k // 2 // bc

    def body(c_ref, w_ref, gm_ref, go_ref, m_ref, v_ref, g_out, d_ref, nm_ref, nv_ref):
        gv = jnp.where(pl.program_id(0) == c_ref[0], gm_ref[...], go_ref[...])
        g_out[...] = gv
        d_ref[...], nm_ref[...], nv_ref[...] = _adamw_math(w_ref[...], gv, m_ref[...], v_ref[...])

    full = pl.BlockSpec((n, bc), lambda hb, i, c: (0, hb * nh + i))
    half = pl.BlockSpec((n, bc), lambda hb, i, c: (0, i))
    return pl.pallas_call(
        body, name=name,
        grid_spec=pltpu.PrefetchScalarGridSpec(num_scalar_prefetch=1, grid=(2, nh), in_specs=[full, half, half, full, full],
                                               out_specs=[full] * 4),
        out_shape=[jax.ShapeDtypeStruct(wt.shape, F32)] * 4,
        compiler_params=_params(("parallel", "parallel")),
    )(core, wt, gt_mine, gt_other, mt, vt)


def add_pair(dw, recv, core, *, name):
    n4, k, n = dw.shape
    half = (1, k // 2, n) if split_axis(k) == 0 else (1, k, n // 2)
    mine = (lambda q, c: (q, c[0], 0)) if split_axis(k) == 0 else (lambda q, c: (q, 0, c[0]))

    def body(c_ref, a_ref, b_ref, o_ref):
        o_ref[...] = (a_ref[...] + b_ref[...].astype(F32)).astype(BF16)

    return pl.pallas_call(
        body, name=name,
        grid_spec=pltpu.PrefetchScalarGridSpec(
            num_scalar_prefetch=1, grid=(n4,),
            in_specs=[pl.BlockSpec(half, mine), pl.BlockSpec(half, lambda q, c: (q, 0, 0))],
            out_specs=pl.BlockSpec(half, lambda q, c: (q, 0, 0))),
        out_shape=jax.ShapeDtypeStruct((n4,) + half[1:], BF16),
        compiler_params=_params(("parallel",)),
    )(core, dw, recv)


def sum_chips(parts, *, name):
    n4, r, n = parts.shape
    if r % 16 == 0:
        br, bc = _row_block(r), n
    else:
        br, bc = r, LANES

    def body(p_ref, o_ref):
        acc = p_ref[0].astype(F32)
        for q in range(1, n4):
            acc = acc + p_ref[q].astype(F32)
        o_ref[...] = acc

    return pl.pallas_call(
        body, name=name, grid=(r // br, n // bc),
        in_specs=[pl.BlockSpec((n4, br, bc), lambda i, j: (0, i, j))], out_specs=pl.BlockSpec((br, bc), lambda i, j: (i, j)),
        out_shape=jax.ShapeDtypeStruct((r, n), F32),
        compiler_params=_params(("parallel", "parallel")),
    )(parts)


ANY = pl.BlockSpec(memory_space=pl.ANY)


def _place():
    x, y, c = lax.axis_index("x"), lax.axis_index("y"), lax.axis_index("c")
    chips = [(1 - x, y), (x, 1 - y), (1 - x, 1 - y)]
    return x, y, c, chips


def _copy(src, dst, send_sems, recv_sems, k, to):
    return pltpu.make_async_remote_copy(src_ref=src, dst_ref=dst, send_sem=send_sems.at[k], recv_sem=recv_sems.at[k],
                                        device_id=to, device_id_type=MESH)


def split_axis(rows):
    return 0 if rows % 32 == 0 else 1


def _half(ref, lead, hf):
    rows, cols = ref.shape[-2:]
    if split_axis(rows) == 0:
        at = (pl.ds(hf * (rows // 2), rows // 2), slice(None))
    else:
        at = (slice(None), pl.ds(hf * (cols // 2), cols // 2))
    return ref.at[at] if lead is None else ref.at[(lead,) + at]


def _gather_first(srcs, dsts, ssems, rsems):
    x, y, c, chips = _place()
    for ti, (s, d) in enumerate(zip(srcs, dsts)):
        for j, (cx, cy) in enumerate(chips):
            _copy(_half(s, None, c), _half(d, 2 * x + y, c), ssems, rsems, 3 * ti + j, (cx, cy, c)).start()


def _gather_mid(srcs, dsts, ssems, rsems):
    x, y, c, chips = _place()
    n1 = 3 * len(srcs)
    for ti, d in enumerate(dsts):
        for j, (cx, cy) in enumerate(chips):
            landed = _half(d, 2 * cx + cy, c)
            _copy(landed, landed, ssems, rsems, 3 * ti + j, (cx, cy, c)).wait_recv()
            _copy(landed, landed, ssems, rsems, n1 + 3 * ti + j, (x, y, 1 - c)).start()


def _gather_last(srcs, dsts, ssems, rsems):
    x, y, c, chips = _place()
    n1 = 3 * len(srcs)
    for ti, (s, d) in enumerate(zip(srcs, dsts)):
        for j, (cx, cy) in enumerate(chips):
            other = _half(d, 2 * cx + cy, 1 - c)
            _copy(other, other, ssems, rsems, n1 + 3 * ti + j, (x, y, 1 - c)).wait_recv()
        for j, (cx, cy) in enumerate(chips):
            mine = _half(s, None, c)
            _copy(mine, mine, ssems, rsems, 3 * ti + j, (cx, cy, c)).wait_send()
            _copy(mine, mine, ssems, rsems, n1 + 3 * ti + j, (x, y, 1 - c)).wait_send()


def gather_side(shards):
    return Side(shards, [jax.ShapeDtypeStruct((N_CHIPS,) + s.shape, s.dtype) for s in shards], 6 * len(shards),
                _gather_first, _gather_last, _gather_mid)


def _scatter_first(srcs, dsts, ssems, rsems):
    x, y, c, chips = _place()
    for ti, (s, d) in enumerate(zip(srcs, dsts)):
        for j, (cx, cy) in enumerate(chips):
            _copy(s.at[2 * cx + cy], d.at[2 * x + y], ssems, rsems, 3 * ti + j, (cx, cy, c)).start()


def _scatter_last(srcs, dsts, ssems, rsems):
    x, y, c, chips = _place()
    for ti, (s, d) in enumerate(zip(srcs, dsts)):
        for j, (cx, cy) in enumerate(chips):
            _copy(s.at[2 * cx + cy], d.at[2 * cx + cy], ssems, rsems, 3 * ti + j, (cx, cy, c)).wait_recv()
        for j, (cx, cy) in enumerate(chips):
            _copy(s.at[2 * cx + cy], d.at[2 * cx + cy], ssems, rsems, 3 * ti + j, (cx, cy, c)).wait_send()


def scatter_side(parts):
    return Side(parts, [jax.ShapeDtypeStruct(p.shape, p.dtype) for p in parts], 3 * len(parts), _scatter_first, _scatter_last)


def run_side(side, *, name):
    n_in, n_out = len(side.ins), len(side.out_shapes)

    def body(*refs):
        si, so, sems = _split_refs(refs, [n_in, n_out, 2])
        side.first(si, so, *sems)
        if side.mid is not None:
            side.mid(si, so, *sems)
        side.last(si, so, *sems)

    in_specs, out_specs = side.specs()
    return pl.pallas_call(body, name=name, in_specs=in_specs, out_specs=out_specs, out_shape=side.out_shapes,
                          scratch_shapes=side.sems())(*side.ins)


def _swap_first(srcs, dsts, ssems, rsems):
    x, y, c, _ = _place()
    for k, (s, d) in enumerate(zip(srcs, dsts)):
        _copy(s, d, ssems, rsems, k, (x, y, 1 - c)).start()


def _swap_last(srcs, dsts, ssems, rsems):
    x, y, c, _ = _place()
    for k, (s, d) in enumerate(zip(srcs, dsts)):
        _copy(s, d, ssems, rsems, k, (x, y, 1 - c)).wait()


def swap_side(xs):
    return Side(xs, [jax.ShapeDtypeStruct(a.shape, a.dtype) for a in xs], len(xs), _swap_first, _swap_last)


def allreduce_small(s):
    n_dev = 8

    def body(s_ref, out_ref, buf, send_sems, recv_sems):
        x, y, c, _ = _place()
        me = 4 * x + 2 * y + c
        buf[me] = s_ref[...]
        sends = []
        for k in range(1, n_dev):
            px = 1 - x if k & 4 else x
            py = 1 - y if k & 2 else y
            pc = 1 - c if k & 1 else c
            cp = _copy(s_ref, buf.at[me], send_sems, recv_sems, k - 1, (px, py, pc))
            cp.start()
            sends.append((cp, 4 * px + 2 * py + pc))
        for k, (cp, peer) in enumerate(sends):
            _copy(s_ref, buf.at[peer], send_sems, recv_sems, k, (x, y, c)).wait_recv()
        for cp, _ in sends:
            cp.wait_send()
        acc = buf[0]
        for d in range(1, n_dev):
            acc = acc + buf[d]
        out_ref[...] = acc

    vm = pl.BlockSpec(memory_space=pltpu.VMEM)
    return pl.pallas_call(
        body, name="allreduce_small", in_specs=[vm], out_specs=vm,
        out_shape=jax.ShapeDtypeStruct(s.shape, F32),
        scratch_shapes=[pltpu.VMEM((n_dev,) + s.shape, F32), pltpu.SemaphoreType.DMA((n_dev - 1,)),
                        pltpu.SemaphoreType.DMA((n_dev - 1,))],
    )(s)


def join_cols(sm):
    n4, k, n = sm.shape
    return sm.transpose(1, 0, 2).reshape(k, n4 * n)


def split_cols(full):
    k, n = full.shape
    return full.reshape(k, N_CHIPS, n // N_CHIPS).transpose(1, 0, 2)


def _pad_heads(w, width):
    lead, heads = w.shape[:-1], w.shape[-1] // width
    w = w.reshape(lead + (heads, width))
    return jnp.pad(w, [(0, 0)] * len(lead) + [(0, 0), (0, LANES - width)]).reshape(lead + (heads * LANES,))


def _unpad_heads(w, width):
    lead, heads = w.shape[:-1], w.shape[-1] // LANES
    return w.reshape(lead + (heads, LANES))[..., :width].reshape(lead + (heads * width,))


def split_w_in_t(w_in_t):
    d = w_in_t.shape[1]
    o_f = 3 * FW
    o_cq = o_f + N_HEADS
    o_ckv = o_cq + Q_RANK
    o_kr = o_ckv + KV_RANK

    def z(n):
        return jnp.zeros((n, d), w_in_t.dtype)

    small = jnp.concatenate([w_in_t[o_cq:o_ckv], w_in_t[o_ckv:o_kr], z(HEAD_DIM), w_in_t[o_kr:], z(LANES - HEAD_DIM - ROPE_DIM),
                             w_in_t[o_f:o_cq], z(LANES - N_HEADS), z(LANES)], axis=0)
    return w_in_t[:o_f], small


def join_w_in_t(d_qkv_t, d_small_t):
    kr = S_KR + HEAD_DIM
    return jnp.concatenate([d_qkv_t, d_small_t[S_F:S_F + N_HEADS], d_small_t[S_CQ:S_CKV], d_small_t[S_CKV:S_KR],
                            d_small_t[kr:kr + ROPE_DIM]], axis=0)


def rope_tables(pos):
    t = pos.shape[0]
    inv_freq = ROPE_THETA ** (-jnp.arange(0, ROPE_DIM, 2, dtype=F32) / ROPE_DIM)
    ang = pos.astype(F32)[:, None] * inv_freq
    cos, sin = jnp.cos(ang), jnp.sin(ang)
    half = ROPE_DIM // 2

    def z(n):
        return jnp.zeros((t, n), F32)

    tab_c = jnp.concatenate([jnp.ones((t, HEAD_DIM), F32), cos, cos, z(LANES - HEAD_DIM - ROPE_DIM)], axis=1)
    tab_a = jnp.concatenate([z(HEAD_DIM), -sin, z(half), z(LANES - HEAD_DIM - ROPE_DIM)], axis=1)
    tab_b = jnp.concatenate([z(HEAD_DIM), z(half), sin, z(LANES - HEAD_DIM - ROPE_DIM)], axis=1)
    return tab_c, tab_a, tab_b


def _pad_lanes(v, n):
    return jnp.pad(v, ((0, 0), (0, n - v.shape[1])))


ATTN_BLK = 512
ATTN_FWD_BLK = 1024
ATTN_BWD_QBLK = 1024


def local_step(xs, pos, tgt, gains, early_weights, late_weights, early_side=None, fwd_sides=(None, None), reduction=None):
    g_attn, b_forget, g_q, g_kv, g_fo, g_mo, g_mlp, g_fin = gains
    t = xs.shape[0]
    blk = min(ATTN_BLK, t)
    fox_scale = 1.0 / (HEAD_DIM ** 0.5)
    mla_scale = 1.0 / ((HEAD_DIM + ROPE_DIM) ** 0.5)

    h1, *gathered = rmsnorm(xs, g_attn, out_dtype=BF16, name="norm_attn", side=early_side)
    w_in_t, w_uq_p, w_ukv = early_weights(gathered)
    w_qkv_t, w_small_t = split_w_in_t(w_in_t)
    kv = w_ukv.reshape(KV_RANK, N_HEADS, 2 * HEAD_DIM)
    w_ukv_p = jnp.concatenate([_pad_heads(kv[:, :, :HEAD_DIM].reshape(KV_RANK, FW), HEAD_DIM),
                               kv[:, :, HEAD_DIM:].reshape(KV_RANK, FW)], axis=1)
    b_f = _pad_lanes(b_forget, LANES)
    tab_c, tab_a, tab_b = rope_tables(pos)

    qkv, qkv_t = mm(h1, w_qkv_t, trans_b=True, out_dtypes=[BF16], t_blk=blk, name="proj_qkv")
    small, = mm(h1, w_small_t, trans_b=True, out_dtypes=[F32], name="proj_small")
    mq, mk, mv, lf, cqn, ckvn, mq_t, mv_t = mla_prep(small, g_q, g_kv, w_uq_p, w_ukv_p, tab_c, tab_a, tab_b, b_f,
                                                     name="mla_prep", bt=blk)
    f_cum = cumsum_rows(lf, reverse=False, name="gate_cumsum")
    f_blocks = f_cum[:, :N_HEADS].reshape(t // blk, blk, N_HEADS).transpose(0, 2, 1)
    fo, st_f, *gathered = flash_fwd(qkv_t, qkv, qkv_t, f_cum, qoff=0, koff=PAIRS, voff=2 * PAIRS, pair=True,
                                    scale=fox_scale, name="fox_fwd", blk=ATTN_FWD_BLK, side=fwd_sides[0])
    mo, st_m, *more = flash_fwd(mq_t, mk, mv_t, None, qoff=0, koff=0, voff=0, pair=False, scale=mla_scale, name="mla_fwd",
                                blk=ATTN_FWD_BLK, side=fwd_sides[1])
    w_o, w_up, w_down = late_weights(gathered + more)
    mixed = mix_norm(fo, mo, g_fo, g_mo, name="norm_mix")

    def inv_rms(v):
        return lax.rsqrt(jnp.mean(v * v, axis=-1, keepdims=True) + EPS)

    def residual_then_norm(acc, res, g):
        xn = acc + res
        return xn, xn * inv_rms(xn) * g

    def norm_bwd(dh, xn, res, g):
        r = inv_rms(xn)
        uu = dh * g
        return (r * uu - xn * (r * r * r * jnp.mean(uu * xn, axis=-1, keepdims=True)) + res,
                jnp.sum(dh * (xn * r), axis=0, keepdims=True))

    def norm_bwd2(dh, xn, res, g):
        dx, dg = norm_bwd(dh, xn, res, g)
        return dx, dx, dg

    def residual_then_loss(acc, res, target, g):
        xn = acc + res
        r = inv_rms(xn)
        xh = xn * r
        e = xh * g - target
        part = 0.5 * jnp.sum(jnp.mean(e * e, axis=-1, keepdims=True), axis=0, keepdims=True)
        dy = e * (1.0 / xn.shape[1])
        uu = dy * g
        dx = r * uu - xn * (r * r * r * jnp.mean(uu * xn, axis=-1, keepdims=True))
        return dx, dx, jnp.sum(dy * xh, axis=0, keepdims=True), part + jnp.zeros_like(g)

    x1, h2 = mm(mixed, w_o, extras=[xs], vecs=[g_mlp], epilogue=residual_then_norm, out_dtypes=[F32, BF16], name="out_proj")

    def relu2(uu):
        r = jnp.maximum(uu.astype(F32), 0.0)
        return (r * r).astype(BF16)

    u, = mm(h2, w_up, out_dtypes=[BF16], name="mlp_up")
    dx2, dx2b, dg_fin, loss_row = mm(u, w_down, a_pro=relu2, extras=[x1, tgt], vecs=[g_fin], epilogue=residual_then_loss,
                                     out_dtypes=[F32, BF16], n_sums=2, name="mlp_down_loss")
    loss = loss_row[:, :1]

    def relu2_grad(acc, uu):
        return (acc * (2.0 * jnp.maximum(uu.astype(F32), 0.0)),)

    du, = mm(dx2b, w_down, trans_b=True, extras=[u], epilogue=relu2_grad, out_dtypes=[BF16], name="mlp_down_bwd")
    dw_down = mm_tn(u, dx2b, a_pro=relu2, name="dw_down").reshape(N_CHIPS, -1, w_down.shape[1])
    dx1, dx1b, dg_mlp = mm(du, w_up, trans_b=True, extras=[x1, dx2], vecs=[g_mlp], epilogue=norm_bwd2,
                           out_dtypes=[F32, BF16], n_sums=1, name="mlp_up_bwd")
    dw_up = mm_tn(h2, du, name="dw_up", col_shards=N_CHIPS)

    dw_o = mm_tn(mixed, dx1b, name="dw_o").reshape(N_CHIPS, -1, w_o.shape[1])
    late = (dw_o, dw_up, dw_down)
    red = reduction
    dfo, dmo, dg_fo, dg_mo, st_f, st_m, dfo_t, dmo_t, *got = mix_norm_bwd(
        dx1b, w_o, fo, mo, g_fo, g_mo, st_f, st_m, name="out_proj_mix_bwd", bt=blk, side=red.late_swap(late) if red else None)
    dfq, dfk, dfv, d_f, *got = flash_bwd(
        qkv, qkv_t, qkv, qkv, dfo, dfo_t, st_f, f_blocks, qoff=0, koff=PAIRS, voff=2 * PAIRS, pair=True,
        scale=fox_scale, name="fox_bwd", qblk=ATTN_BWD_QBLK, side=red.late_scatter(got) if red else None)
    dmq, dmk, dmv, *got = flash_bwd(mq, mq_t, mk, mv, dmo, dmo_t, st_m, None, qoff=0, koff=0, voff=0,
                                    pair=False, scale=mla_scale, name="mla_bwd", qblk=ATTN_BWD_QBLK,
                                    side=red.late_halves(got) if red else None)
    if red:
        red.late_done(got)
    dqkv = jnp.concatenate([dfq, dfk, dfv], axis=1).astype(BF16)
    dlf = cumsum_rows(d_f, reverse=True, name="gate_cumsum_bwd")
    dsmall, dq_u, dkv_u, dg_q, dg_kv, db_f = mla_prep_bwd(dmq, dmk, dmv, dlf, small, g_q, g_kv, w_uq_p, w_ukv_p,
                                                          tab_c, tab_a, tab_b, b_f, name="mla_prep_bwd")
    dw_uq_p = mm_tn(cqn, dq_u, name="dw_uq")
    dw_ukv_p = mm_tn(ckvn, dkv_u, name="dw_ukv")

    dw_in_t = join_w_in_t(mm_tn(dqkv, h1, name="dw_qkv"), mm_tn(dsmall, h1, name="dw_small"))
    dk_cols = _unpad_heads(dw_ukv_p[:, :HW], HEAD_DIM).reshape(KV_RANK, N_HEADS, HEAD_DIM)
    dv_cols = dw_ukv_p[:, HW:].reshape(KV_RANK, N_HEADS, HEAD_DIM)
    dw_ukv = jnp.concatenate([dk_cols, dv_cols], axis=2).reshape(KV_RANK, N_HEADS * 2 * HEAD_DIM)
    early = (dw_in_t.reshape(N_CHIPS, -1, dw_in_t.shape[1]), split_cols(dw_uq_p), split_cols(dw_ukv))
    grad_x, dg_attn, *got = mm([dqkv, dsmall], [w_qkv_t, w_small_t], extras=[xs, dx1], vecs=[g_attn],
                               epilogue=norm_bwd, out_dtypes=[F32], n_sums=1, name="proj_bwd",
                               side=red.early_scatter(early) if red else None)
    if red:
        red.early_done(got)
    d_gains = (dg_attn, db_f[:, :N_HEADS], dg_q, dg_kv, dg_fo, dg_mo, dg_mlp, dg_fin)
    return loss, grad_x, early, late, d_gains


class GradReduction:
    def __init__(self, core_id, chip):
        self.core_id, self.chip = core_id, chip
        self.core = core_id.reshape(1).astype(jnp.int32)
        self.grads, self.pairs, self.halves, self.others = {}, {}, {}, {}

    def _other_halves(self, grads):
        out = []
        for g in grads:
            axis = 1 + split_axis(g.shape[1])
            size = g.shape[axis] // 2
            out.append(lax.dynamic_slice_in_dim(g, (1 - self.core_id) * size, size, axis=axis).astype(BF16))
        return out

    def _add_pairs(self, group, recvs):
        self.pairs[group] = [add_pair(g, r, self.core, name="add_pair_%s_%d" % (group, n))
                             for n, (g, r) in enumerate(zip(self.grads[group], recvs))]
        return scatter_side(self.pairs[group])

    def _chip_sums(self, group, scattered):
        chip = self.chip
        with_mine = [lax.dynamic_update_index_in_dim(s, lax.dynamic_index_in_dim(p, chip, 0, keepdims=True), chip, 0)
                     for s, p in zip(scattered, self.pairs[group])]
        self.halves[group] = [sum_chips(s, name="sum_chips_%s_%d" % (group, n)) for n, s in enumerate(with_mine)]
        return self.halves[group]

    def late_swap(self, grads):
        self.grads["late"] = list(grads)
        return swap_side(self._other_halves(grads))

    def late_scatter(self, recvs):
        return self._add_pairs("late", recvs)

    def late_halves(self, scattered):
        return swap_side(self._chip_sums("late", scattered))

    def late_done(self, others):
        self.others["late"] = list(others)

    def early_scatter(self, grads):
        self.grads["early"] = list(grads)
        return self._add_pairs("early", run_side(swap_side(self._other_halves(grads)), name="swap_early_sends"))

    def early_done(self, scattered):
        self.others["early"] = list(run_side(swap_side(self._chip_sums("early", scattered)), name="swap_early_halves"))

def kernel(x, positions, attn_norm_g, w_in, b_forget, q_norm_g, w_uq, kv_norm_g, w_ukv, fox_out_g, mla_out_g, w_o, mlp_norm_g, w_up, w_down, final_norm_g, loss_target, m_attn_norm_g, m_w_in, m_b_forget, m_q_norm_g, m_w_uq, m_kv_norm_g, m_w_ukv, m_fox_out_g, m_mla_out_g, m_w_o, m_mlp_norm_g, m_w_up, m_w_down, m_final_norm_g, v_attn_norm_g, v_w_in, v_b_forget, v_q_norm_g, v_w_uq, v_kv_norm_g, v_w_ukv, v_fox_out_g, v_mla_out_g, v_w_o, v_mlp_norm_g, v_w_up, v_w_down, v_final_norm_g):
    core_id = lax.axis_index("c")
    core = core_id.reshape(1).astype(jnp.int32)
    chip = 2 * lax.axis_index("x") + lax.axis_index("y")
    big = [w_in, w_uq, w_ukv, w_o, w_up, w_down]
    big_m = [m_w_in, m_w_uq, m_w_ukv, m_w_o, m_w_up, m_w_down]
    big_v = [v_w_in, v_w_uq, v_w_ukv, v_w_o, v_w_up, v_w_down]
    n_early = 3

    def vec(a):
        return a.reshape(1, -1)

    small = [attn_norm_g, b_forget, q_norm_g, kv_norm_g, fox_out_g, mla_out_g, mlp_norm_g, final_norm_g]
    small_m = [m_attn_norm_g, m_b_forget, m_q_norm_g, m_kv_norm_g, m_fox_out_g, m_mla_out_g, m_mlp_norm_g, m_final_norm_g]
    small_v = [v_attn_norm_g, v_b_forget, v_q_norm_g, v_kv_norm_g, v_fox_out_g, v_mla_out_g, v_mlp_norm_g, v_final_norm_g]
    gains = [vec(a) for a in small]

    views = [big[0][0].T, _pad_heads(big[1][0], HEAD_DIM + ROPE_DIM)] + [w[0] for w in big[2:]]
    shards = [v.astype(BF16) for v in views]

    def with_own(gathered, mine):
        return [lax.dynamic_update_index_in_dim(g, s, chip, 0) for g, s in zip(gathered, mine)]

    def early_weights(gathered):
        g_in, g_uq, g_ukv = with_own(gathered, shards[:n_early])
        return g_in.reshape(-1, g_in.shape[2]), join_cols(g_uq), join_cols(g_ukv)

    def late_weights(gathered):
        g_o, g_up, g_down = with_own(gathered, shards[n_early:])
        return g_o.reshape(-1, g_o.shape[2]), join_cols(g_up), g_down.reshape(-1, g_down.shape[2])

    reduction = GradReduction(core_id, chip)
    loss, grad_x, _, _, d_small = local_step(
        x[0], positions[0], loss_target[0], gains, early_weights, late_weights, gather_side(shards[:n_early]),
        (gather_side(shards[n_early:-1]), gather_side(shards[-1:])), reduction)
    halves = reduction.halves["early"] + reduction.halves["late"]
    others = reduction.others["early"] + reduction.others["late"]

    def rows8(vs):
        return jnp.concatenate([_pad_lanes(vec(a).astype(F32), 1024) for a in vs], axis=0)

    with_loss = [jnp.concatenate([d, loss], axis=1) if n == 1 else d for n, d in enumerate(d_small)]
    g_small8 = allreduce_small(rows8(with_loss))

    outs_big = []
    for n, (w, gm, go, m, v) in enumerate(zip(big, halves, others, big_m, big_v)):
        _, k, cols = w.shape
        if n == 0:
            outs = adamw_halves_t(w[0].T, gm, go, m[0].T, v[0].T, core, name="adamw_%d" % n)
            outs_big.append([o.T[None] for o in outs])
        else:
            if n == 1:
                gm, go = (_unpad_heads(gh, HEAD_DIM + ROPE_DIM) for gh in (gm, go))
            outs_big.append(adamw_halves(w, gm, go, m, v, core, name="adamw_%d" % n))
    d8, m8, v8 = adamw(rows8(small), g_small8, rows8(small_m), rows8(small_v), name="adamw_small")

    def unrows8(a8):
        return [a8[n, :s.size].reshape(s.shape) for n, s in enumerate(small)]

    loss_all = g_small8[1, N_HEADS]
    grads, deltas, new_m, new_v = [None] * 14, [None] * 14, [None] * 14, [None] * 14
    big_at = [1, 4, 6, 9, 11, 12]
    small_at = [0, 2, 3, 5, 7, 8, 10, 13]
    for n, at in enumerate(big_at):
        grads[at], deltas[at], new_m[at], new_v[at] = outs_big[n]
    for at, g, dd, mm_, vv in zip(small_at, unrows8(g_small8), unrows8(d8), unrows8(m8), unrows8(v8)):
        grads[at], deltas[at], new_m[at], new_v[at] = g, dd, mm_, vv
    return (loss_all, grad_x[None], *grads, *deltas, *new_m, *new_v)
```

```python
import jax
import jax.numpy as jnp
from jax import lax
from jax.experimental import pallas as pl
from jax.experimental.pallas import tpu as pltpu

F32 = jnp.float32
BF16 = jnp.bfloat16
MESH = pl.DeviceIdType.MESH

EPS = 1e-6
ROPE_THETA = 10000.0
N_HEADS = 8
PAIRS = N_HEADS // 2
HEAD_DIM = 64
ROPE_DIM = 32
LANES = 128
Q_RANK = 384
KV_RANK = 256
N_CHIPS = 4
ADAM_LR, ADAM_B1, ADAM_B2, ADAM_EPS, ADAM_WD, ADAM_STEP = 0.001, 0.9, 0.999, 1e-08, 0.01, 10
VMEM_LIMIT = 48 * 1024 * 1024
LOG2E = 1.4426950408889634
LN2 = 0.6931471805599453
NN = (((1,), (0,)), ((), ()))
NT = (((1,), (1,)), ((), ()))
TN = (((0,), (0,)), ((), ()))


def _params(sem=None):
    return pltpu.CompilerParams(dimension_semantics=sem, vmem_limit_bytes=VMEM_LIMIT)


def _fit(block, dim):
    if dim <= block:
        return dim
    return next(b for b in range(block - block % LANES, 0, -LANES) if dim % b == 0)


def _row_block(rows):
    return next(b for b in (256, 128, 64, 32, 16, 8) if rows % b == 0)


def gridded(body, *, name, grid, in_specs, out_specs, out_shape, ins, semantics, side=None):
    if side is None:
        return pl.pallas_call(body, name=name, grid=grid, in_specs=in_specs, out_specs=out_specs, out_shape=out_shape,
                              compiler_params=_params(semantics))(*ins)
    n_in, n_out = len(in_specs), len(out_specs)
    steps = 1
    for extent in grid:
        steps *= extent

    def riding(*refs):
        main_in, s_in, main_out, s_out, sems = _split_refs(refs, [n_in, len(side.ins), n_out, len(side.out_shapes), 2])
        step = 0
        for axis, extent in enumerate(grid):
            step = step * extent + pl.program_id(axis)

        @pl.when(step == 0)
        def _():
            side.first(s_in, s_out, *sems)

        body(*main_in, *main_out)

        @pl.when(step == steps - 1)
        def _():
            if side.mid is not None:
                side.mid(s_in, s_out, *sems)
            side.last(s_in, s_out, *sems)

    s_in_specs, s_out_specs = side.specs()
    return pl.pallas_call(
        riding, name=name, grid=grid, in_specs=list(in_specs) + s_in_specs, out_specs=list(out_specs) + s_out_specs,
        out_shape=list(out_shape) + side.out_shapes, scratch_shapes=side.sems(),
        compiler_params=_params(("arbitrary",) * len(grid)))(*ins, *side.ins)


def rmsnorm(x, g, *, out_dtype, name, bt=512, side=None):
    t, d = x.shape
    bt = min(bt, t)

    def body(x_ref, g_ref, o_ref):
        xv = x_ref[...].astype(F32)
        r = lax.rsqrt(jnp.mean(xv * xv, axis=-1, keepdims=True) + EPS)
        o_ref[...] = (xv * r * g_ref[...]).astype(o_ref.dtype)

    return gridded(
        body, name=name, grid=(t // bt,),
        in_specs=[pl.BlockSpec((bt, d), lambda i: (i, 0)), pl.BlockSpec((1, d), lambda i: (0, 0))],
        out_specs=[pl.BlockSpec((bt, d), lambda i: (i, 0))],
        out_shape=[jax.ShapeDtypeStruct((t, d), out_dtype)],
        ins=[x, g], semantics=("parallel",), side=side)


def mm(a, b, *, trans_b=False, a_pro=None, extras=(), vecs=(), epilogue=None, out_dtypes, n_sums=0, t_blk=None, name,
       bm=1024, bn=1024, side=None):
    a_list = list(a) if isinstance(a, (list, tuple)) else [a]
    b_list = list(b) if isinstance(b, (list, tuple)) else [b]
    m = a_list[0].shape[0]
    n = b_list[0].shape[0] if trans_b else b_list[0].shape[1]
    ks = [x.shape[1] for x in a_list]
    if sum(ks) > 2048:
        bm = bm // 2
    bm, bn = _fit(bm, m), _fit(bn, n)
    assert n_sums == 0 or bn == n
    n_ab, n_ex, n_vec, n_out = len(a_list), len(extras), len(vecs), len(out_dtypes)
    n_t = 0 if t_blk is None else 1

    def body(*refs):
        a_refs, b_refs, ex, vs, outs, t_outs, sums = _split_refs(refs, [n_ab, n_ab, n_ex, n_vec, n_out, n_t, n_sums])
        acc = None
        for a_ref, b_ref in zip(a_refs, b_refs):
            a_tile = a_ref[...] if a_pro is None else a_pro(a_ref[...])
            part = lax.dot_general(a_tile, b_ref[...], NT if trans_b else NN, preferred_element_type=F32)
            acc = part if acc is None else acc + part
        res = epilogue(acc, *[e[...] for e in ex], *[v[...] for v in vs]) if epilogue is not None else (acc,)
        for o, r in zip(outs, res[:n_out]):
            o[...] = r.astype(o.dtype)
        for t_ref in t_outs:
            for u in range(bm // t_blk):
                t_ref[u] = res[0][u * t_blk:(u + 1) * t_blk, :].T.astype(t_ref.dtype)
        if n_sums:
            @pl.when(pl.program_id(0) == 0)
            def _():
                for s_ref in sums:
                    s_ref[...] = jnp.zeros_like(s_ref)

            for s_ref, r in zip(sums, res[n_out:]):
                s_ref[...] += r

    tile = pl.BlockSpec((bm, bn), lambda i, j: (i, j))
    vec = pl.BlockSpec((1, bn), lambda i, j: (0, j))
    t_specs, t_shapes = [], []
    if t_blk is not None:
        t_specs = [pl.BlockSpec((bm // t_blk, bn, t_blk), lambda i, j: (i, j, 0))]
        t_shapes = [jax.ShapeDtypeStruct((m // t_blk, n, t_blk), out_dtypes[0])]
    a_specs = [pl.BlockSpec((bm, k), lambda i, j: (i, 0)) for k in ks]
    b_specs = [pl.BlockSpec((bn, k), lambda i, j: (j, 0)) if trans_b else pl.BlockSpec((k, bn), lambda i, j: (0, j)) for k in ks]
    return gridded(
        body, name=name, grid=(m // bm, n // bn),
        in_specs=a_specs + b_specs + [tile] * n_ex + [vec] * n_vec,
        out_specs=[tile] * n_out + t_specs + [vec] * n_sums,
        out_shape=[jax.ShapeDtypeStruct((m, n), dt) for dt in out_dtypes] + t_shapes + [jax.ShapeDtypeStruct((1, n), F32)] * n_sums,
        ins=[*a_list, *b_list, *extras, *vecs],
        semantics=("arbitrary", "arbitrary") if n_sums else ("parallel", "parallel"), side=side)


def mm_tn(a, b, *, a_pro=None, name, col_shards=1, bf16_copy=False, bk=1024, bn=1024, bt=1024):
    t, k = a.shape
    n = b.shape[1]
    ns = n // col_shards
    bk, bn, bt = _fit(bk, k), _fit(bn, ns), _fit(bt, t)
    per = ns // bn
    last = t // bt - 1

    def body(a_ref, b_ref, o_ref, *copy_ref):
        @pl.when(pl.program_id(2) == 0)
        def _():
            o_ref[...] = jnp.zeros_like(o_ref)

        a_tile = a_ref[...] if a_pro is None else a_pro(a_ref[...])
        o_ref[...] += lax.dot_general(a_tile, b_ref[...], TN, preferred_element_type=F32)
        if bf16_copy:
            @pl.when(pl.program_id(2) == last)
            def _():
                copy_ref[0][...] = o_ref[...].astype(BF16)

    if col_shards == 1:
        out_spec = pl.BlockSpec((bk, bn), lambda i, j, s: (i, j))
        shape = (k, n)
    else:
        out_spec = pl.BlockSpec((None, bk, bn), lambda i, j, s: (j // per, i, j % per))
        shape = (col_shards, k, ns)
    out_specs, out_shape = out_spec, jax.ShapeDtypeStruct(shape, F32)
    if bf16_copy:
        out_specs, out_shape = [out_spec, out_spec], [out_shape, jax.ShapeDtypeStruct(shape, BF16)]
    return pl.pallas_call(
        body, name=name, grid=(k // bk, n // bn, t // bt),
        in_specs=[pl.BlockSpec((bt, bk), lambda i, j, s: (s, i)), pl.BlockSpec((bt, bn), lambda i, j, s: (s, j))],
        out_specs=out_specs, out_shape=out_shape,
        compiler_params=_params(("parallel", "parallel", "arbitrary")),
    )(a, b)


def _split3(x):
    hi = x.astype(BF16)
    r1 = x - hi.astype(F32)
    mid = r1.astype(BF16)
    lo = (r1 - mid.astype(F32)).astype(BF16)
    return hi, mid, lo


def cumsum_rows(x, *, reverse, name, bc=512):
    t, d = x.shape
    bc = min(bc, t)
    nb = t // bc

    def body(x_ref, o_ref, carry):
        @pl.when(pl.program_id(0) == 0)
        def _():
            carry[...] = jnp.zeros_like(carry)

        r = lax.broadcasted_iota(jnp.int32, (bc, bc), 0)
        c = lax.broadcasted_iota(jnp.int32, (bc, bc), 1)
        tri = jnp.where((r <= c) if reverse else (r >= c), 1.0, 0.0).astype(BF16)
        hi, mid, lo = _split3(x_ref[...])
        s = (lax.dot_general(tri, hi, NN, preferred_element_type=F32)
             + lax.dot_general(tri, mid, NN, preferred_element_type=F32)
             + lax.dot_general(tri, lo, NN, preferred_element_type=F32)) + carry[0:1, :]
        o_ref[...] = s
        carry[0:1, :] = s[0:1, :] if reverse else s[bc - 1:bc, :]

    imap = (lambda i: (nb - 1 - i, 0)) if reverse else (lambda i: (i, 0))
    return pl.pallas_call(
        body, name=name, grid=(nb,),
        in_specs=[pl.BlockSpec((bc, d), imap)], out_specs=pl.BlockSpec((bc, d), imap),
        out_shape=jax.ShapeDtypeStruct((t, d), F32),
        scratch_shapes=[pltpu.VMEM((8, d), F32)],
        compiler_params=_params(("arbitrary",)),
    )(x)


def _rope(x, c, a, b):
    return x * c + pltpu.roll(x, LANES - ROPE_DIM // 2, 1) * a + pltpu.roll(x, ROPE_DIM // 2, 1) * b


def _rope_bwd(d, c, a, b):
    return d * c + pltpu.roll(d * a, ROPE_DIM // 2, 1) + pltpu.roll(d * b, LANES - ROPE_DIM // 2, 1)


S_CQ, S_CKV, S_KR, S_F, S_END = 0, Q_RANK, Q_RANK + KV_RANK, Q_RANK + KV_RANK + LANES, 1024
HW = N_HEADS * LANES
FW = N_HEADS * HEAD_DIM


def mla_prep(small, g_q, g_kv, w_uq, w_ukv, tab_c, tab_a, tab_b, b_f, *, name, bt=512):
    t = small.shape[0]
    bt = min(bt, t)

    def body(s_ref, gq_ref, gkv_ref, wq_ref, wkv_ref, c_ref, a_ref, b_ref, bf_ref,
             mq_ref, mk_ref, mv_ref, lf_ref, cqn_ref, ckvn_ref, mqt_ref, mvt_ref):
        cq = s_ref[:, S_CQ:S_CKV]
        rq = lax.rsqrt(jnp.mean(cq * cq, axis=-1, keepdims=True) + EPS)
        cqn = (cq * rq * gq_ref[...]).astype(BF16)
        ckv = s_ref[:, S_CKV:S_KR]
        rkv = lax.rsqrt(jnp.mean(ckv * ckv, axis=-1, keepdims=True) + EPS)
        ckvn = (ckv * rkv * gkv_ref[...]).astype(BF16)
        cqn_ref[...] = cqn
        ckvn_ref[...] = ckvn
        tc, ta, tb = c_ref[...], a_ref[...], b_ref[...]
        q = jnp.dot(cqn, wq_ref[...], preferred_element_type=F32)
        kv = jnp.dot(ckvn, wkv_ref[...], preferred_element_type=F32)
        kr = _rope(s_ref[:, S_KR:S_F], tc, ta, tb)
        for h in range(N_HEADS):
            sl = slice(h * LANES, (h + 1) * LANES)
            roped = _rope(q[:, sl], tc, ta, tb)
            mq_ref[:, sl] = roped.astype(BF16)
            mqt_ref[0, sl, :] = roped.T.astype(BF16)
            mk_ref[:, sl] = (kv[:, sl] + kr).astype(BF16)
        mv_ref[...] = kv[:, HW:].astype(BF16)
        mvt_ref[0] = kv[:, HW:].T.astype(BF16)
        z = s_ref[:, S_F:S_END - LANES] + bf_ref[...]
        lf_ref[...] = jnp.minimum(z, 0.0) - jnp.log(1.0 + jnp.exp(-jnp.abs(z)))

    def row(w):
        return pl.BlockSpec((bt, w), lambda i: (i, 0))

    def full(arr):
        return pl.BlockSpec(arr.shape, lambda i: (0, 0))

    return pl.pallas_call(
        body, name=name, grid=(t // bt,),
        in_specs=[row(S_END), full(g_q), full(g_kv), full(w_uq), full(w_ukv), row(LANES), row(LANES), row(LANES), full(b_f)],
        out_specs=[row(HW), row(HW), row(FW), row(LANES), row(Q_RANK), row(KV_RANK),
                   pl.BlockSpec((1, HW, bt), lambda i: (i, 0, 0)), pl.BlockSpec((1, FW, bt), lambda i: (i, 0, 0))],
        out_shape=[jax.ShapeDtypeStruct((t, HW), BF16)] * 2 + [jax.ShapeDtypeStruct((t, FW), BF16), jax.ShapeDtypeStruct((t, LANES), F32),
                   jax.ShapeDtypeStruct((t, Q_RANK), BF16), jax.ShapeDtypeStruct((t, KV_RANK), BF16),
                   jax.ShapeDtypeStruct((t // bt, HW, bt), BF16), jax.ShapeDtypeStruct((t // bt, FW, bt), BF16)],
        compiler_params=_params(("parallel",)),
    )(small, g_q, g_kv, w_uq, w_ukv, tab_c, tab_a, tab_b, b_f)


def mla_prep_bwd(dmq, dmk, dmv, dlf, small, g_q, g_kv, w_uq, w_ukv, tab_c, tab_a, tab_b, b_f, *, name, bt=512):
    t = small.shape[0]
    bt = min(bt, t)

    def body(dmq_ref, dmk_ref, dmv_ref, dlf_ref, s_ref, gq_ref, gkv_ref, wq_ref, wkv_ref, c_ref, a_ref, b_ref, bf_ref,
             ds_ref, dq_ref, dkv_ref, dgq_ref, dgkv_ref, db_ref):
        tc, ta, tb = c_ref[...], a_ref[...], b_ref[...]
        lane = lax.broadcasted_iota(jnp.int32, (1, LANES), 1)
        dkr = jnp.zeros((bt, LANES), F32)
        for h in range(N_HEADS):
            sl = slice(h * LANES, (h + 1) * LANES)
            dq_ref[:, sl] = _rope_bwd(dmq_ref[:, sl], tc, ta, tb).astype(BF16)
            dkr = dkr + dmk_ref[:, sl]
        dkv_ref[:, :HW] = dmk_ref[...].astype(BF16)
        dkv_ref[:, HW:] = dmv_ref[...].astype(BF16)
        in_rope = (lane >= HEAD_DIM) & (lane < HEAD_DIM + ROPE_DIM)
        ds_ref[:, S_KR:S_F] = jnp.where(in_rope, _rope_bwd(dkr, tc, ta, tb), 0.0).astype(BF16)

        def norm_bwd(raw, g_ref, dn, dg_ref):
            r = lax.rsqrt(jnp.mean(raw * raw, axis=-1, keepdims=True) + EPS)
            u = dn * g_ref[...]
            dot = jnp.mean(u * raw, axis=-1, keepdims=True)
            dg_ref[...] += jnp.sum(dn * (raw * r), axis=0, keepdims=True)
            return r * u - raw * (r * r * r * dot)

        @pl.when(pl.program_id(0) == 0)
        def _():
            dgq_ref[...] = jnp.zeros_like(dgq_ref)
            dgkv_ref[...] = jnp.zeros_like(dgkv_ref)
            db_ref[...] = jnp.zeros_like(db_ref)

        dcqn = lax.dot_general(dq_ref[...], wq_ref[...], NT, preferred_element_type=F32)
        ds_ref[:, S_CQ:S_CKV] = norm_bwd(s_ref[:, S_CQ:S_CKV], gq_ref, dcqn, dgq_ref).astype(BF16)
        dckvn = lax.dot_general(dkv_ref[...], wkv_ref[...], NT, preferred_element_type=F32)
        ds_ref[:, S_CKV:S_KR] = norm_bwd(s_ref[:, S_CKV:S_KR], gkv_ref, dckvn, dgkv_ref).astype(BF16)
        z = s_ref[:, S_F:S_END - LANES] + bf_ref[...]
        dz = jnp.where(lane < N_HEADS, dlf_ref[...] / (1.0 + jnp.exp(z)), 0.0)
        db_ref[...] += jnp.sum(dz, axis=0, keepdims=True)
        ds_ref[:, S_F:S_END - LANES] = dz.astype(BF16)
        ds_ref[:, S_END - LANES:] = jnp.zeros((bt, LANES), BF16)

    def row(w):
        return pl.BlockSpec((bt, w), lambda i: (i, 0))

    def full(arr):
        return pl.BlockSpec(arr.shape, lambda i: (0, 0))

    def vec(w):
        return pl.BlockSpec((1, w), lambda i: (0, 0))

    return pl.pallas_call(
        body, name=name, grid=(t // bt,),
        in_specs=[row(HW), row(HW), row(FW), row(LANES), row(S_END), full(g_q), full(g_kv), full(w_uq), full(w_ukv),
                  row(LANES), row(LANES), row(LANES), full(b_f)],
        out_specs=[row(S_END), row(HW), row(HW + FW), vec(Q_RANK), vec(KV_RANK), vec(LANES)],
        out_shape=[jax.ShapeDtypeStruct((t, S_END), BF16), jax.ShapeDtypeStruct((t, HW), BF16),
                   jax.ShapeDtypeStruct((t, HW + FW), BF16), jax.ShapeDtypeStruct((1, Q_RANK), F32),
                   jax.ShapeDtypeStruct((1, KV_RANK), F32), jax.ShapeDtypeStruct((1, LANES), F32)],
        compiler_params=_params(("arbitrary",)),
    )(dmq, dmk, dmv, dlf, small, g_q, g_kv, w_uq, w_ukv, tab_c, tab_a, tab_b, b_f)


class Side:
    def __init__(self, ins, out_shapes, n_sems, first, last, mid=None):
        self.ins, self.out_shapes, self.n_sems = list(ins), list(out_shapes), n_sems
        self.first, self.mid, self.last = first, mid, last

    def specs(self):
        return [ANY] * len(self.ins), [ANY] * len(self.out_shapes)

    def sems(self):
        return [pltpu.SemaphoreType.DMA((self.n_sems,)), pltpu.SemaphoreType.DMA((self.n_sems,))]


def _lane():
    return lax.broadcasted_iota(jnp.int32, (1, LANES), 1)


def _halves(x):
    zero = jnp.zeros_like(x)
    return [jnp.where(_lane() < HEAD_DIM, x, zero), jnp.where(_lane() >= HEAD_DIM, x, zero)]


def _groups(x):
    return [x[:, :LANES], x[:, LANES:]]


def _pick_row(tile, h):
    row = lax.broadcasted_iota(jnp.int32, (tile.shape[0], 1), 0)
    return jnp.sum(jnp.where(row == h, tile, 0.0), axis=0, keepdims=True)


def _pick_lane(tile, h):
    return jnp.sum(jnp.where(_lane() == h, tile, 0.0), axis=1, keepdims=True)


def _row_halves(x):
    row = lax.broadcasted_iota(jnp.int32, (LANES, 1), 0)
    zero = jnp.zeros_like(x)
    return [jnp.where(row < HEAD_DIM, x, zero), jnp.where(row >= HEAD_DIM, x, zero)]


def _below_diagonal(s, lead=0):
    r = lax.broadcasted_iota(jnp.int32, s.shape, 0)
    c = lax.broadcasted_iota(jnp.int32, s.shape, 1)
    return jnp.where(c <= r + lead, s, -jnp.inf)


def _above_diagonal(s):
    r = lax.broadcasted_iota(jnp.int32, s.shape, 0)
    c = lax.broadcasted_iota(jnp.int32, s.shape, 1)
    return jnp.where(r <= c, s, -jnp.inf)


def _split_refs(refs, counts):
    out, at = [], 0
    for n in counts:
        out.append(refs[at:at + n])
        at += n
    return out


def flash_fwd(qt_arr, k_arr, vt_arr, f_cum, *, qoff, koff, voff, pair, scale, name, blk=512, side=None):
    t = k_arr.shape[0]
    tblk = qt_arr.shape[2]
    blk = max(min(blk, t), tblk)
    sub = blk // tblk
    nb = t // blk
    w = LANES if pair else 2 * LANES
    has_bias = f_cum is not None
    ins = [qt_arr, k_arr, vt_arr] + ([f_cum] if has_bias else [])

    def wide(ref, first):
        parts = [ref[first + u] for u in range(sub)]
        return parts[0] if sub == 1 else jnp.concatenate(parts, axis=1)
    s_ins, s_outs = (side.ins, side.out_shapes) if side else ([], [])

    def body(*refs):
        main, si, outs, so, sems = _split_refs(refs, [len(ins), len(s_ins), 2, len(s_outs), 2 if side else 0])
        qt_ref, k_ref, vt_ref = main[:3]
        f_ref = main[3] if has_bias else None
        o_ref, st_ref = outs
        g, i = pl.program_id(0), pl.program_id(1)
        step_id = g * nb + i
        if side:
            @pl.when(step_id == 0)
            def _():
                side.first(si, so, *sems)

            if side.mid is not None:
                @pl.when(step_id == (3 * PAIRS * nb) // 4)
                def _():
                    side.mid(si, so, *sems)

        qt = (wide(qt_ref, 0).astype(F32) * (scale * LOG2E)).astype(BF16)
        qts = _row_halves(qt) if pair else [qt[:LANES], qt[LANES:]]

        def step(j, carry, diagonal):
            rows = pl.ds(pl.multiple_of(j * blk, blk), blk)
            kk = k_ref[rows, :]
            ks = [kk, kk] if pair else _groups(kk)
            vt = wide(vt_ref, sub * j)
            out = []
            for n in range(2):
                m, l, acc = carry[n]
                s = jnp.dot(ks[n], qts[n], preferred_element_type=F32)
                if has_bias:
                    s = s - LOG2E * _pick_lane(f_ref[rows, :], 2 * g + n)
                if diagonal:
                    s = _above_diagonal(s)
                m_new = jnp.maximum(m, jnp.max(s, axis=0, keepdims=True))
                alpha = jnp.exp2(m - m_new)
                p = jnp.exp2(s - m_new)
                out.append((m_new, alpha * l + jnp.sum(p, axis=0, keepdims=True),
                            alpha * acc + jnp.dot(vt[n * HEAD_DIM:(n + 1) * HEAD_DIM], p.astype(BF16),
                                                  preferred_element_type=F32)))
            return tuple(out)

        def diagonal_in_halves(carry):
            h = tblk
            halves = [pl.ds(pl.multiple_of(i * blk, blk), h), pl.ds(pl.multiple_of(i * blk + h, h), h)]
            k_top, k_bot = ([x, x] if pair else _groups(x) for x in (k_ref[halves[0], :], k_ref[halves[1], :]))
            vt_top, vt_bot = vt_ref[sub * i], vt_ref[sub * i + 1]
            out = []
            for n in range(2):
                m, l, acc = carry[n]
                heads = slice(n * HEAD_DIM, (n + 1) * HEAD_DIM)
                s_top = jnp.dot(k_top[n], qts[n], preferred_element_type=F32)
                s_bot = jnp.dot(k_bot[n], qts[n][:, h:], preferred_element_type=F32)
                if has_bias:
                    s_top = s_top - LOG2E * _pick_lane(f_ref[halves[0], :], 2 * g + n)
                    s_bot = s_bot - LOG2E * _pick_lane(f_ref[halves[1], :], 2 * g + n)
                s_top, s_bot = _above_diagonal(s_top), _above_diagonal(s_bot)
                m_top = jnp.maximum(m, jnp.max(s_top, axis=0, keepdims=True))
                m_new = jnp.concatenate([m_top[:, :h], jnp.maximum(m_top[:, h:], jnp.max(s_bot, axis=0, keepdims=True))], axis=1)
                alpha = jnp.exp2(m - m_new)
                p_top = jnp.exp2(s_top - m_new)
                p_bot = jnp.exp2(s_bot - m_new[:, h:])
                late_l = jnp.concatenate([jnp.zeros((1, h), F32), jnp.sum(p_bot, axis=0, keepdims=True)], axis=1)
                late_acc = jnp.concatenate([jnp.zeros((HEAD_DIM, h), F32),
                                            jnp.dot(vt_bot[heads], p_bot.astype(BF16), preferred_element_type=F32)], axis=1)
                out.append((m_new, alpha * l + jnp.sum(p_top, axis=0, keepdims=True) + late_l,
                            alpha * acc + jnp.dot(vt_top[heads], p_top.astype(BF16), preferred_element_type=F32) + late_acc))
            return tuple(out)

        init = tuple((jnp.full((1, blk), -jnp.inf, F32), jnp.zeros((1, blk), F32), jnp.zeros((HEAD_DIM, blk), F32))
                     for _ in range(2))
        carry = lax.fori_loop(0, i, lambda j, c: step(j, c, False), init)
        (ma, la, acca), (mb, lb, accb) = diagonal_in_halves(carry) if sub == 2 else step(i, carry, True)
        o_ref[...] = jnp.concatenate([acca / la, accb / lb], axis=0).T
        row = lax.broadcasted_iota(jnp.int32, (LANES, 1), 0)
        st_ref[0] = jnp.where(row == 0, ma + jnp.log2(la), jnp.where(row == 1, mb + jnp.log2(lb), 0.0)).T
        if side:
            @pl.when(step_id == PAIRS * nb - 1)
            def _():
                side.last(si, so, *sems)

    in_specs = [pl.BlockSpec((sub, w, tblk), lambda g, i: (i, qoff + g, 0)), pl.BlockSpec((t, w), lambda g, i: (0, koff + g)),
                pl.BlockSpec((t // tblk, LANES, tblk), lambda g, i: (0, voff + g, 0))]
    if has_bias:
        in_specs.append(pl.BlockSpec((t, LANES), lambda g, i: (0, 0)))
    s_in_specs, s_out_specs = side.specs() if side else ([], [])
    return pl.pallas_call(
        body, name=name, grid=(PAIRS, nb), in_specs=in_specs + s_in_specs,
        out_specs=[pl.BlockSpec((blk, LANES), lambda g, i: (i, g)), pl.BlockSpec((1, blk, LANES), lambda g, i: (g, i, 0))]
        + s_out_specs,
        out_shape=[jax.ShapeDtypeStruct((t, PAIRS * LANES), F32), jax.ShapeDtypeStruct((PAIRS, t, LANES), F32)] + list(s_outs),
        scratch_shapes=side.sems() if side else [],
        compiler_params=_params(("arbitrary", "arbitrary")),
    )(*ins, *s_ins)


def mix_norm(fo, mo, g_fo, g_mo, *, name, bt=512):
    t, d = fo.shape
    bt = min(bt, t)

    def body(fo_ref, mo_ref, gf_ref, gm_ref, o_ref):
        for n, (x_ref, g_ref) in enumerate(((fo_ref, gf_ref), (mo_ref, gm_ref))):
            xv = x_ref[...]
            r = lax.rsqrt(jnp.mean(xv * xv, axis=-1, keepdims=True) + EPS)
            o_ref[:, n * d:(n + 1) * d] = (xv * r * g_ref[...]).astype(BF16)

    row = pl.BlockSpec((bt, d), lambda i: (i, 0))
    vec = pl.BlockSpec((1, d), lambda i: (0, 0))
    return pl.pallas_call(
        body, name=name, grid=(t // bt,), in_specs=[row, row, vec, vec],
        out_specs=pl.BlockSpec((bt, 2 * d), lambda i: (i, 0)),
        out_shape=jax.ShapeDtypeStruct((t, 2 * d), BF16),
        compiler_params=_params(("parallel",)),
    )(fo, mo, g_fo, g_mo)


def mix_norm_bwd(dx1b, w_o, fo, mo, g_fo, g_mo, st_f, st_m, *, name, bt=512, side=None):
    t, d = fo.shape
    bt = min(bt, t)

    def body(dx_in_ref, w_ref, fo_ref, mo_ref, gf_ref, gm_ref, sf_ref, sm_ref, dfo_ref, dmo_ref, dgf_ref, dgm_ref,
             sfo_ref, smo_ref, dfot_ref, dmot_ref):
        @pl.when(pl.program_id(0) == 0)
        def _():
            dgf_ref[...] = jnp.zeros_like(dgf_ref)
            dgm_ref[...] = jnp.zeros_like(dgm_ref)

        dmixed = lax.dot_general(dx_in_ref[...], w_ref[...], NT, preferred_element_type=F32)
        groups = ((fo_ref, gf_ref, dfo_ref, dgf_ref, sf_ref, sfo_ref, dfot_ref),
                  (mo_ref, gm_ref, dmo_ref, dgm_ref, sm_ref, smo_ref, dmot_ref))
        for n, (x_ref, g_ref, dx_ref, dg_ref, st_ref, sto_ref, dxt_ref) in enumerate(groups):
            xv = x_ref[...]
            dhv = dmixed[:, n * d:(n + 1) * d]
            r = lax.rsqrt(jnp.mean(xv * xv, axis=-1, keepdims=True) + EPS)
            u = dhv * g_ref[...]
            dxf = r * u - xv * (r * r * r * jnp.mean(u * xv, axis=-1, keepdims=True))
            dxb = dxf.astype(BF16)
            dx_ref[...] = dxb
            dxt_ref[0] = dxf.T.astype(BF16)
            dg_ref[...] += jnp.sum(dhv * (xv * r), axis=0, keepdims=True)
            prod = xv * dxb.astype(F32)
            for g in range(PAIRS):
                grp = prod[:, g * LANES:(g + 1) * LANES]
                da = jnp.sum(jnp.where(_lane() < HEAD_DIM, grp, 0.0), axis=1, keepdims=True)
                db = jnp.sum(jnp.where(_lane() >= HEAD_DIM, grp, 0.0), axis=1, keepdims=True)
                sto_ref[g] = jnp.where(_lane() == 2, da, jnp.where(_lane() == 3, db, st_ref[g]))

    row = pl.BlockSpec((bt, d), lambda i: (i, 0))
    vec = pl.BlockSpec((1, d), lambda i: (0, 0))
    stat = pl.BlockSpec((PAIRS, bt, LANES), lambda i: (0, i, 0))
    return gridded(
        body, name=name, grid=(t // bt,),
        in_specs=[pl.BlockSpec((bt, dx1b.shape[1]), lambda i: (i, 0)), pl.BlockSpec(w_o.shape, lambda i: (0, 0)),
                  row, row, vec, vec, stat, stat],
        out_specs=[row, row, vec, vec, stat, stat] + [pl.BlockSpec((1, d, bt), lambda i: (i, 0, 0))] * 2,
        out_shape=[jax.ShapeDtypeStruct((t, d), BF16)] * 2 + [jax.ShapeDtypeStruct((1, d), F32)] * 2
        + [jax.ShapeDtypeStruct(st_f.shape, F32)] * 2 + [jax.ShapeDtypeStruct((t // bt, d, bt), BF16)] * 2,
        ins=[dx1b, w_o, fo, mo, g_fo, g_mo, st_f, st_m], semantics=("arbitrary",), side=side)


def flash_bwd(q_arr, qt_arr, k_arr, v_arr, do_arr, dot_arr, st, f_blocks, *, qoff, koff, voff, pair, scale, name, qblk=1024,
              side=None):
    t = q_arr.shape[0]
    blk = qt_arr.shape[2]
    qblk = max(min(qblk, t), blk)
    sub = qblk // blk
    nb, nbq = t // blk, t // qblk
    w = LANES if pair else 2 * LANES
    hw = w // 2
    has_bias = f_blocks is not None
    split = _halves if pair else _groups
    ins = [q_arr, qt_arr, k_arr, v_arr, do_arr, dot_arr, st] + ([f_blocks] if has_bias else [])
    n_out = 4 if has_bias else 3
    s_ins, s_outs = (side.ins, side.out_shapes) if side else ([], [])

    def wide(ref, first):
        parts = [ref[first + u] for u in range(sub)]
        return parts[0] if sub == 1 else jnp.concatenate(parts, axis=1)

    def body(*refs):
        main, si, outs, so, sems = _split_refs(refs, [len(ins), len(s_ins), n_out, len(s_outs), 2 if side else 0])
        q_ref, qt_ref, k_ref, v_ref, do_ref, dot_ref, st_ref = main[:7]
        dq_ref, dk_ref, dv_ref = outs[:3]
        g, j = pl.program_id(0), pl.program_id(1)
        step_id = g * nb + j
        if side:
            @pl.when(step_id == 0)
            def _():
                side.first(si, so, *sems)

        @pl.when(j == 0)
        def _():
            dq_ref[...] = jnp.zeros_like(dq_ref)

        kk, vv = k_ref[...], v_ref[...]
        ks = [kk, kk] if pair else _groups(kk)
        if has_bias:
            f_ref, df_ref = main[7], outs[3]
            fk = [LOG2E * _pick_row(f_ref[0], 2 * g + n) for n in range(2)]

            @pl.when(step_id == 0)
            def _():
                df_ref[...] = jnp.zeros_like(df_ref)

        def step(i, carry, diagonal):
            rows = pl.ds(pl.multiple_of(i * qblk, qblk), qblk)
            qs = split((q_ref[rows, :].astype(F32) * (scale * LOG2E)).astype(BF16))
            qt = (wide(qt_ref, sub * i).astype(F32) * (scale * LOG2E)).astype(BF16)
            dos = [h.astype(BF16) for h in _halves(do_ref[rows, :].astype(F32))]
            dot = wide(dot_ref, sub * i)
            stats = st_ref[0, rows, :]
            new, dqs, row_sums = [], [], []
            for n in range(2):
                dkt, dvt, dfk = carry[n]
                s = lax.dot_general(qs[n], ks[n], NT, preferred_element_type=F32)
                if has_bias:
                    s = s - fk[n]
                if diagonal:
                    s = _below_diagonal(s, i * qblk - j * blk)
                p = jnp.exp2(s - stats[:, n:n + 1])
                dp = lax.dot_general(dos[n], vv, NT, preferred_element_type=F32)
                ds = p * (dp - stats[:, 2 + n:3 + n])
                dsb = ds.astype(BF16)
                dvt = dvt + jnp.dot(dot[n * HEAD_DIM:(n + 1) * HEAD_DIM], p.astype(BF16), preferred_element_type=F32)
                dkt = dkt + jnp.dot(qt[n * hw:(n + 1) * hw], dsb, preferred_element_type=F32)
                dqs.append(jnp.dot(dsb, ks[n], preferred_element_type=F32))
                if has_bias:
                    dfk = dfk - jnp.sum(ds, axis=0, keepdims=True)
                    row_sums.append(jnp.sum(ds, axis=1, keepdims=True))
                new.append((dkt, dvt, dfk))
            dq = (jnp.where(_lane() < HEAD_DIM, dqs[0], dqs[1]) if pair else jnp.concatenate(dqs, axis=1)) * scale
            if has_bias:
                df_ref[rows, :] += jnp.where(_lane() == 2 * g, row_sums[0], jnp.where(_lane() == 2 * g + 1, row_sums[1], 0.0))
            dq_ref[rows, :] += dq
            return tuple(new)

        init = tuple((jnp.zeros((hw, blk), F32), jnp.zeros((HEAD_DIM, blk), F32), jnp.zeros((1, blk), F32)) for _ in range(2))
        first = j // sub
        carry = step(first, init, True)
        (dka, dva, dfa), (dkb, dvb, dfb) = lax.fori_loop(first + 1, nbq, lambda i, c: step(i, c, False), carry)
        dk_ref[...] = jnp.concatenate([dka, dkb], axis=0).T * LN2
        dv_ref[...] = jnp.concatenate([dva, dvb], axis=0).T
        if has_bias:
            row = lax.broadcasted_iota(jnp.int32, (LANES, 1), 0)
            by_head = jnp.where(row == 2 * g, dfa, jnp.where(row == 2 * g + 1, dfb, 0.0))
            df_ref[pl.ds(pl.multiple_of(j * blk, blk), blk), :] += by_head.T
        if side:
            @pl.when(step_id == PAIRS * nb - 1)
            def _():
                side.last(si, so, *sems)

    in_specs = [pl.BlockSpec((t, w), lambda g, j: (0, qoff + g)), pl.BlockSpec((nb, w, blk), lambda g, j: (0, qoff + g, 0)),
                pl.BlockSpec((blk, w), lambda g, j: (j, koff + g)), pl.BlockSpec((blk, LANES), lambda g, j: (j, voff + g)),
                pl.BlockSpec((t, LANES), lambda g, j: (0, g)), pl.BlockSpec((nb, LANES, blk), lambda g, j: (0, g, 0)),
                pl.BlockSpec((1, t, LANES), lambda g, j: (g, 0, 0))]
    out_specs = [pl.BlockSpec((t, w), lambda g, j: (0, g)), pl.BlockSpec((blk, w), lambda g, j: (j, g)),
                 pl.BlockSpec((blk, LANES), lambda g, j: (j, g))]
    out_shape = [jax.ShapeDtypeStruct((t, PAIRS * w), F32)] * 2 + [jax.ShapeDtypeStruct((t, PAIRS * LANES), F32)]
    if has_bias:
        in_specs.append(pl.BlockSpec((1, N_HEADS, blk), lambda g, j: (j, 0, 0)))
        out_specs.append(pl.BlockSpec((t, LANES), lambda g, j: (0, 0)))
        out_shape.append(jax.ShapeDtypeStruct((t, LANES), F32))
    s_in_specs, s_out_specs = side.specs() if side else ([], [])
    return pl.pallas_call(
        body, name=name, grid=(PAIRS, nb), in_specs=in_specs + s_in_specs, out_specs=out_specs + s_out_specs,
        out_shape=out_shape + list(s_outs), scratch_shapes=side.sems() if side else [],
        compiler_params=_params(("arbitrary", "arbitrary")),
    )(*ins, *s_ins)


def _adamw_math(w, g, m, v):
    nm = ADAM_B1 * m + (1.0 - ADAM_B1) * g
    nv = ADAM_B2 * v + (1.0 - ADAM_B2) * (g * g)
    m_hat = nm / (1.0 - ADAM_B1 ** ADAM_STEP)
    v_hat = nv / (1.0 - ADAM_B2 ** ADAM_STEP)
    return -ADAM_LR * (m_hat / (jnp.sqrt(v_hat) + ADAM_EPS) + ADAM_WD * w), nm, nv


def adamw(w, g, m, v, *, name):
    rws, cols = w.shape
    br = _row_block(rws)

    def body(w_ref, g_ref, m_ref, v_ref, d_ref, nm_ref, nv_ref):
        d_ref[...], nm_ref[...], nv_ref[...] = _adamw_math(w_ref[...], g_ref[...], m_ref[...], v_ref[...])

    blk = pl.BlockSpec((br, cols), lambda i: (i, 0))
    return pl.pallas_call(
        body, name=name, grid=(rws // br,), in_specs=[blk] * 4, out_specs=[blk] * 3,
        out_shape=[jax.ShapeDtypeStruct((rws, cols), F32)] * 3,
        compiler_params=_params(("parallel",)),
    )(w, g, m, v)


def adamw_halves(w, g_mine, g_other, m, v, core, *, name):
    _, k, n = w.shape
    br = _row_block(k // 2)
    nh = k // 2 // br

    def body(c_ref, w_ref, gm_ref, go_ref, m_ref, v_ref, g_out, d_ref, nm_ref, nv_ref):
        gv = jnp.where(pl.program_id(0) == c_ref[0], gm_ref[...], go_ref[...])
        g_out[0] = gv
        d_ref[0], nm_ref[0], nv_ref[0] = _adamw_math(w_ref[0], gv, m_ref[0], v_ref[0])

    full = pl.BlockSpec((1, br, n), lambda hb, i, c: (0, hb * nh + i, 0))
    half = pl.BlockSpec((br, n), lambda hb, i, c: (i, 0))
    return pl.pallas_call(
        body, name=name,
        grid_spec=pltpu.PrefetchScalarGridSpec(num_scalar_prefetch=1, grid=(2, nh), in_specs=[full, half, half, full, full],
                                               out_specs=[full] * 4),
        out_shape=[jax.ShapeDtypeStruct(w.shape, F32)] * 4,
        compiler_params=_params(("parallel", "parallel")),
    )(core, w, g_mine, g_other, m, v)


def adamw_halves_t(wt, gt_mine, gt_other, mt, vt, core, *, name, bc=128):
    n, k = wt.shape
    nh = k // 2 // bc

    def body(c_ref, w_ref, gm_ref, go_ref, m_ref, v_ref, g_out, d_ref, nm_ref, nv_ref):
        gv = jnp.where(pl.program_id(0) == c_ref[0], gm_ref[...], go_ref[...])
        g_out[...] = gv
        d_ref[...], nm_ref[...], nv_ref[...] = _adamw_math(w_ref[...], gv, m_ref[...], v_ref[...])

    full = pl.BlockSpec((n, bc), lambda hb, i, c: (0, hb * nh + i))
    half = pl.BlockSpec((n, bc), lambda hb, i, c: (0, i))
    return pl.pallas_call(
        body, name=name,
        grid_spec=pltpu.PrefetchScalarGridSpec(num_scalar_prefetch=1, grid=(2, nh), in_specs=[full, half, half, full, full],
                                               out_specs=[full] * 4),
        out_shape=[jax.ShapeDtypeStruct(wt.shape, F32)] * 4,
        compiler_params=_params(("parallel", "parallel")),
    )(core, wt, gt_mine, gt_other, mt, vt)


def add_pair(dw, recv, core, *, name):
    n4, k, n = dw.shape
    half = (1, k // 2, n) if split_axis(k) == 0 else (1, k, n // 2)
    mine = (lambda q, c: (q, c[0], 0)) if split_axis(k) == 0 else (lambda q, c: (q, 0, c[0]))

    def body(c_ref, a_ref, b_ref, o_ref):
        o_ref[...] = (a_ref[...] + b_ref[...].astype(F32)).astype(BF16)

    return pl.pallas_call(
        body, name=name,
        grid_spec=pltpu.PrefetchScalarGridSpec(
            num_scalar_prefetch=1, grid=(n4,),
            in_specs=[pl.BlockSpec(half, mine), pl.BlockSpec(half, lambda q, c: (q, 0, 0))],
            out_specs=pl.BlockSpec(half, lambda q, c: (q, 0, 0))),
        out_shape=jax.ShapeDtypeStruct((n4,) + half[1:], BF16),
        compiler_params=_params(("parallel",)),
    )(core, dw, recv)


def sum_chips(parts, *, name):
    n4, r, n = parts.shape
    if r % 16 == 0:
        br, bc = _row_block(r), n
    else:
        br, bc = r, LANES

    def body(p_ref, o_ref):
        acc = p_ref[0].astype(F32)
        for q in range(1, n4):
            acc = acc + p_ref[q].astype(F32)
        o_ref[...] = acc

    return pl.pallas_call(
        body, name=name, grid=(r // br, n // bc),
        in_specs=[pl.BlockSpec((n4, br, bc), lambda i, j: (0, i, j))], out_specs=pl.BlockSpec((br, bc), lambda i, j: (i, j)),
        out_shape=jax.ShapeDtypeStruct((r, n), F32),
        compiler_params=_params(("parallel", "parallel")),
    )(parts)


ANY = pl.BlockSpec(memory_space=pl.ANY)


def _place():
    x, y, c = lax.axis_index("x"), lax.axis_index("y"), lax.axis_index("c")
    chips = [(1 - x, y), (x, 1 - y), (1 - x, 1 - y)]
    return x, y, c, chips


def _copy(src, dst, send_sems, recv_sems, k, to):
    return pltpu.make_async_remote_copy(src_ref=src, dst_ref=dst, send_sem=send_sems.at[k], recv_sem=recv_sems.at[k],
                                        device_id=to, device_id_type=MESH)


def split_axis(rows):
    return 0 if rows % 32 == 0 else 1


def _half(ref, lead, hf):
    rows, cols = ref.shape[-2:]
    if split_axis(rows) == 0:
        at = (pl.ds(hf * (rows // 2), rows // 2), slice(None))
    else:
        at = (slice(None), pl.ds(hf * (cols // 2), cols // 2))
    return ref.at[at] if lead is None else ref.at[(lead,) + at]


def _gather_first(srcs, dsts, ssems, rsems):
    x, y, c, chips = _place()
    for ti, (s, d) in enumerate(zip(srcs, dsts)):
        for j, (cx, cy) in enumerate(chips):
            _copy(_half(s, None, c), _half(d, 2 * x + y, c), ssems, rsems, 3 * ti + j, (cx, cy, c)).start()


def _gather_mid(srcs, dsts, ssems, rsems):
    x, y, c, chips = _place()
    n1 = 3 * len(srcs)
    for ti, d in enumerate(dsts):
        for j, (cx, cy) in enumerate(chips):
            landed = _half(d, 2 * cx + cy, c)
            _copy(landed, landed, ssems, rsems, 3 * ti + j, (cx, cy, c)).wait_recv()
            _copy(landed, landed, ssems, rsems, n1 + 3 * ti + j, (x, y, 1 - c)).start()


def _gather_last(srcs, dsts, ssems, rsems):
    x, y, c, chips = _place()
    n1 = 3 * len(srcs)
    for ti, (s, d) in enumerate(zip(srcs, dsts)):
        for j, (cx, cy) in enumerate(chips):
            other = _half(d, 2 * cx + cy, 1 - c)
            _copy(other, other, ssems, rsems, n1 + 3 * ti + j, (x, y, 1 - c)).wait_recv()
        for j, (cx, cy) in enumerate(chips):
            mine = _half(s, None, c)
            _copy(mine, mine, ssems, rsems, 3 * ti + j, (cx, cy, c)).wait_send()
            _copy(mine, mine, ssems, rsems, n1 + 3 * ti + j, (x, y, 1 - c)).wait_send()


def gather_side(shards):
    return Side(shards, [jax.ShapeDtypeStruct((N_CHIPS,) + s.shape, s.dtype) for s in shards], 6 * len(shards),
                _gather_first, _gather_last, _gather_mid)


def _scatter_first(srcs, dsts, ssems, rsems):
    x, y, c, chips = _place()
    for ti, (s, d) in enumerate(zip(srcs, dsts)):
        for j, (cx, cy) in enumerate(chips):
            _copy(s.at[2 * cx + cy], d.at[2 * x + y], ssems, rsems, 3 * ti + j, (cx, cy, c)).start()


def _scatter_last(srcs, dsts, ssems, rsems):
    x, y, c, chips = _place()
    for ti, (s, d) in enumerate(zip(srcs, dsts)):
        for j, (cx, cy) in enumerate(chips):
            _copy(s.at[2 * cx + cy], d.at[2 * cx + cy], ssems, rsems, 3 * ti + j, (cx, cy, c)).wait_recv()
        for j, (cx, cy) in enumerate(chips):
            _copy(s.at[2 * cx + cy], d.at[2 * cx + cy], ssems, rsems, 3 * ti + j, (cx, cy, c)).wait_send()


def scatter_side(parts):
    return Side(parts, [jax.ShapeDtypeStruct(p.shape, p.dtype) for p in parts], 3 * len(parts), _scatter_first, _scatter_last)


def run_side(side, *, name):
    n_in, n_out = len(side.ins), len(side.out_shapes)

    def body(*refs):
        si, so, sems = _split_refs(refs, [n_in, n_out, 2])
        side.first(si, so, *sems)
        if side.mid is not None:
            side.mid(si, so, *sems)
        side.last(si, so, *sems)

    in_specs, out_specs = side.specs()
    return pl.pallas_call(body, name=name, in_specs=in_specs, out_specs=out_specs, out_shape=side.out_shapes,
                          scratch_shapes=side.sems())(*side.ins)


def _swap_first(srcs, dsts, ssems, rsems):
    x, y, c, _ = _place()
    for k, (s, d) in enumerate(zip(srcs, dsts)):
        _copy(s, d, ssems, rsems, k, (x, y, 1 - c)).start()


def _swap_last(srcs, dsts, ssems, rsems):
    x, y, c, _ = _place()
    for k, (s, d) in enumerate(zip(srcs, dsts)):
        _copy(s, d, ssems, rsems, k, (x, y, 1 - c)).wait()


def swap_side(xs):
    return Side(xs, [jax.ShapeDtypeStruct(a.shape, a.dtype) for a in xs], len(xs), _swap_first, _swap_last)


def _swap_halves(srcs, dsts, ssems, rsems):
    x, y, c, _ = _place()
    for k, (s, d) in enumerate(zip(srcs, dsts)):
        hk = s.shape[1] // 2
        yield _copy(s.at[:, pl.ds((1 - c) * hk, hk), :], d, ssems, rsems, k, (x, y, 1 - c))


def _swap_halves_first(srcs, dsts, ssems, rsems):
    for cp in _swap_halves(srcs, dsts, ssems, rsems):
        cp.start()


def _swap_halves_last(srcs, dsts, ssems, rsems):
    for cp in _swap_halves(srcs, dsts, ssems, rsems):
        cp.wait()


def swap_halves_side(xs):
    return Side(xs, [jax.ShapeDtypeStruct((a.shape[0], a.shape[1] // 2, a.shape[2]), a.dtype) for a in xs], len(xs),
                _swap_halves_first, _swap_halves_last)


def allreduce_small(s):
    n_dev = 8

    def body(s_ref, out_ref, buf, send_sems, recv_sems):
        x, y, c, _ = _place()
        me = 4 * x + 2 * y + c
        buf[me] = s_ref[...]
        sends = []
        for k in range(1, n_dev):
            px = 1 - x if k & 4 else x
            py = 1 - y if k & 2 else y
            pc = 1 - c if k & 1 else c
            cp = _copy(s_ref, buf.at[me], send_sems, recv_sems, k - 1, (px, py, pc))
            cp.start()
            sends.append((cp, 4 * px + 2 * py + pc))
        for k, (cp, peer) in enumerate(sends):
            _copy(s_ref, buf.at[peer], send_sems, recv_sems, k, (x, y, c)).wait_recv()
        for cp, _ in sends:
            cp.wait_send()
        acc = buf[0]
        for d in range(1, n_dev):
            acc = acc + buf[d]
        out_ref[...] = acc

    vm = pl.BlockSpec(memory_space=pltpu.VMEM)
    return pl.pallas_call(
        body, name="allreduce_small", in_specs=[vm], out_specs=vm,
        out_shape=jax.ShapeDtypeStruct(s.shape, F32),
        scratch_shapes=[pltpu.VMEM((n_dev,) + s.shape, F32), pltpu.SemaphoreType.DMA((n_dev - 1,)),
                        pltpu.SemaphoreType.DMA((n_dev - 1,))],
    )(s)


def join_cols(sm):
    n4, k, n = sm.shape
    return sm.transpose(1, 0, 2).reshape(k, n4 * n)


def split_cols(full):
    k, n = full.shape
    return full.reshape(k, N_CHIPS, n // N_CHIPS).transpose(1, 0, 2)


def _pad_heads(w, width):
    lead, heads = w.shape[:-1], w.shape[-1] // width
    w = w.reshape(lead + (heads, width))
    return jnp.pad(w, [(0, 0)] * len(lead) + [(0, 0), (0, LANES - width)]).reshape(lead + (heads * LANES,))


def _unpad_heads(w, width):
    lead, heads = w.shape[:-1], w.shape[-1] // LANES
    return w.reshape(lead + (heads, LANES))[..., :width].reshape(lead + (heads * width,))


O_F = 3 * FW
O_CQ = O_F + N_HEADS
O_CKV = O_CQ + Q_RANK
O_KR = O_CKV + KV_RANK
O_END = O_KR + ROPE_DIM


def _shard_rows(sm, a, b):
    r = sm.shape[1]
    out = []
    while a < b:
        q = a // r
        e = min(b, (q + 1) * r)
        out.append(sm[q, a - q * r:e - q * r])
        a = e
    return out


def split_w_in_t(w_sm):
    d = w_sm.shape[2]

    def z(n):
        return [jnp.zeros((n, d), w_sm.dtype)]

    small = (_shard_rows(w_sm, O_CQ, O_CKV) + _shard_rows(w_sm, O_CKV, O_KR) + z(HEAD_DIM) + _shard_rows(w_sm, O_KR, O_END)
             + z(LANES - HEAD_DIM - ROPE_DIM) + _shard_rows(w_sm, O_F, O_CQ) + z(LANES - N_HEADS) + z(LANES))
    return jnp.concatenate(_shard_rows(w_sm, 0, O_F), axis=0), jnp.concatenate(small, axis=0)


def join_w_in_t(d_qkv_t, d_small_t):
    kr = S_KR + HEAD_DIM
    segments = [(d_qkv_t, 0, 0, O_F), (d_small_t, S_F, O_F, N_HEADS), (d_small_t, S_CQ, O_CQ, Q_RANK),
                (d_small_t, S_CKV, O_CKV, KV_RANK), (d_small_t, kr, O_KR, ROPE_DIM)]
    r = O_END // N_CHIPS
    shards = []
    for q in range(N_CHIPS):
        pieces = []
        for src, s0, v0, n in segments:
            a, b = max(v0, q * r), min(v0 + n, (q + 1) * r)
            if a < b:
                pieces.append(src[s0 + a - v0:s0 + b - v0])
        shards.append(jnp.concatenate(pieces, axis=0))
    return jnp.stack(shards)


def rope_tables(pos):
    t = pos.shape[0]
    inv_freq = ROPE_THETA ** (-jnp.arange(0, ROPE_DIM, 2, dtype=F32) / ROPE_DIM)
    ang = pos.astype(F32)[:, None] * inv_freq
    cos, sin = jnp.cos(ang), jnp.sin(ang)
    half = ROPE_DIM // 2

    def z(n):
        return jnp.zeros((t, n), F32)

    tab_c = jnp.concatenate([jnp.ones((t, HEAD_DIM), F32), cos, cos, z(LANES - HEAD_DIM - ROPE_DIM)], axis=1)
    tab_a = jnp.concatenate([z(HEAD_DIM), -sin, z(half), z(LANES - HEAD_DIM - ROPE_DIM)], axis=1)
    tab_b = jnp.concatenate([z(HEAD_DIM), z(half), sin, z(LANES - HEAD_DIM - ROPE_DIM)], axis=1)
    return tab_c, tab_a, tab_b


def _pad_lanes(v, n):
    return jnp.pad(v, ((0, 0), (0, n - v.shape[1])))


ATTN_BLK = 512
ATTN_FWD_BLK = 1024
ATTN_BWD_QBLK = 1024


def local_step(xs, pos, tgt, gains, early_weights, late_weights, early_side=None, fwd_sides=(None, None), reduction=None):
    g_attn, b_forget, g_q, g_kv, g_fo, g_mo, g_mlp, g_fin = gains
    t = xs.shape[0]
    blk = min(ATTN_BLK, t)
    fox_scale = 1.0 / (HEAD_DIM ** 0.5)
    mla_scale = 1.0 / ((HEAD_DIM + ROPE_DIM) ** 0.5)

    h1, *gathered = rmsnorm(xs, g_attn, out_dtype=BF16, name="norm_attn", side=early_side)
    w_in_t, w_uq_p, w_ukv = early_weights(gathered)
    w_qkv_t, w_small_t = split_w_in_t(w_in_t)
    kv = w_ukv.reshape(KV_RANK, N_HEADS, 2 * HEAD_DIM)
    w_ukv_p = jnp.concatenate([_pad_heads(kv[:, :, :HEAD_DIM].reshape(KV_RANK, FW), HEAD_DIM),
                               kv[:, :, HEAD_DIM:].reshape(KV_RANK, FW)], axis=1)
    b_f = _pad_lanes(b_forget, LANES)
    tab_c, tab_a, tab_b = rope_tables(pos)

    qkv, qkv_t = mm(h1, w_qkv_t, trans_b=True, out_dtypes=[BF16], t_blk=blk, name="proj_qkv")
    small, = mm(h1, w_small_t, trans_b=True, out_dtypes=[F32], name="proj_small")
    mq, mk, mv, lf, cqn, ckvn, mq_t, mv_t = mla_prep(small, g_q, g_kv, w_uq_p, w_ukv_p, tab_c, tab_a, tab_b, b_f,
                                                     name="mla_prep", bt=blk)
    f_cum = cumsum_rows(lf, reverse=False, name="gate_cumsum")
    f_blocks = f_cum[:, :N_HEADS].reshape(t // blk, blk, N_HEADS).transpose(0, 2, 1)
    fo, st_f, *gathered = flash_fwd(qkv_t, qkv, qkv_t, f_cum, qoff=0, koff=PAIRS, voff=2 * PAIRS, pair=True,
                                    scale=fox_scale, name="fox_fwd", blk=ATTN_FWD_BLK, side=fwd_sides[0])
    mo, st_m, *more = flash_fwd(mq_t, mk, mv_t, None, qoff=0, koff=0, voff=0, pair=False, scale=mla_scale, name="mla_fwd",
                                blk=ATTN_FWD_BLK, side=fwd_sides[1])
    w_o, w_up, w_down = late_weights(gathered + more)
    mixed = mix_norm(fo, mo, g_fo, g_mo, name="norm_mix")

    def inv_rms(v):
        return lax.rsqrt(jnp.mean(v * v, axis=-1, keepdims=True) + EPS)

    def residual_then_norm(acc, res, g):
        xn = acc + res
        return xn, xn * inv_rms(xn) * g

    def norm_bwd(dh, xn, res, g):
        r = inv_rms(xn)
        uu = dh * g
        return (r * uu - xn * (r * r * r * jnp.mean(uu * xn, axis=-1, keepdims=True)) + res,
                jnp.sum(dh * (xn * r), axis=0, keepdims=True))

    def norm_bwd2(dh, xn, res, g):
        dx, dg = norm_bwd(dh, xn, res, g)
        return dx, dx, dg

    def residual_then_loss(acc, res, target, g):
        xn = acc + res
        r = inv_rms(xn)
        xh = xn * r
        e = xh * g - target
        part = 0.5 * jnp.sum(jnp.mean(e * e, axis=-1, keepdims=True), axis=0, keepdims=True)
        dy = e * (1.0 / xn.shape[1])
        uu = dy * g
        dx = r * uu - xn * (r * r * r * jnp.mean(uu * xn, axis=-1, keepdims=True))
        return dx, dx, jnp.sum(dy * xh, axis=0, keepdims=True), part + jnp.zeros_like(g)

    x1, h2 = mm(mixed, w_o, extras=[xs], vecs=[g_mlp], epilogue=residual_then_norm, out_dtypes=[F32, BF16], name="out_proj")

    def relu2(uu):
        r = jnp.maximum(uu.astype(F32), 0.0)
        return (r * r).astype(BF16)

    u, = mm(h2, w_up, out_dtypes=[BF16], name="mlp_up")
    dx2, dx2b, dg_fin, loss_row = mm(u, w_down, a_pro=relu2, extras=[x1, tgt], vecs=[g_fin], epilogue=residual_then_loss,
                                     out_dtypes=[F32, BF16], n_sums=2, name="mlp_down_loss")
    loss = loss_row[:, :1]

    def relu2_grad(acc, uu):
        return (acc * (2.0 * jnp.maximum(uu.astype(F32), 0.0)),)

    du, = mm(dx2b, w_down, trans_b=True, extras=[u], epilogue=relu2_grad, out_dtypes=[BF16], name="mlp_down_bwd")
    dw_down, dw_down_b = (g.reshape(N_CHIPS, -1, w_down.shape[1])
                          for g in mm_tn(u, dx2b, a_pro=relu2, name="dw_down", bf16_copy=True))
    dx1, dx1b, dg_mlp = mm(du, w_up, trans_b=True, extras=[x1, dx2], vecs=[g_mlp], epilogue=norm_bwd2,
                           out_dtypes=[F32, BF16], n_sums=1, name="mlp_up_bwd")
    dw_up, dw_up_b = mm_tn(h2, du, name="dw_up", col_shards=N_CHIPS, bf16_copy=True)

    dw_o, dw_o_b = (g.reshape(N_CHIPS, -1, w_o.shape[1]) for g in mm_tn(mixed, dx1b, name="dw_o", bf16_copy=True))
    late, late_b = (dw_o, dw_up, dw_down), (dw_o_b, dw_up_b, dw_down_b)
    red = reduction
    dfo, dmo, dg_fo, dg_mo, st_f, st_m, dfo_t, dmo_t, *got = mix_norm_bwd(
        dx1b, w_o, fo, mo, g_fo, g_mo, st_f, st_m, name="out_proj_mix_bwd", bt=blk,
        side=red.late_swap(late, late_b) if red else None)
    dfq, dfk, dfv, d_f, *got = flash_bwd(
        qkv, qkv_t, qkv, qkv, dfo, dfo_t, st_f, f_blocks, qoff=0, koff=PAIRS, voff=2 * PAIRS, pair=True,
        scale=fox_scale, name="fox_bwd", qblk=ATTN_BWD_QBLK, side=red.late_scatter(got) if red else None)
    dmq, dmk, dmv, *got = flash_bwd(mq, mq_t, mk, mv, dmo, dmo_t, st_m, None, qoff=0, koff=0, voff=0,
                                    pair=False, scale=mla_scale, name="mla_bwd", qblk=ATTN_BWD_QBLK,
                                    side=red.late_halves(got) if red else None)
    if red:
        red.late_done(got)
    dqkv = jnp.concatenate([dfq, dfk, dfv], axis=1).astype(BF16)
    dlf = cumsum_rows(d_f, reverse=True, name="gate_cumsum_bwd")
    dsmall, dq_u, dkv_u, dg_q, dg_kv, db_f = mla_prep_bwd(dmq, dmk, dmv, dlf, small, g_q, g_kv, w_uq_p, w_ukv_p,
                                                          tab_c, tab_a, tab_b, b_f, name="mla_prep_bwd")
    dw_uq_p = mm_tn(cqn, dq_u, name="dw_uq")
    dw_ukv_p = mm_tn(ckvn, dkv_u, name="dw_ukv")

    dw_in_t = join_w_in_t(mm_tn(dqkv, h1, name="dw_qkv"), mm_tn(dsmall, h1, name="dw_small"))
    dk_cols = _unpad_heads(dw_ukv_p[:, :HW], HEAD_DIM).reshape(KV_RANK, N_HEADS, HEAD_DIM)
    dv_cols = dw_ukv_p[:, HW:].reshape(KV_RANK, N_HEADS, HEAD_DIM)
    dw_ukv = jnp.concatenate([dk_cols, dv_cols], axis=2).reshape(KV_RANK, N_HEADS * 2 * HEAD_DIM)
    early = (dw_in_t, split_cols(dw_uq_p), split_cols(dw_ukv))
    grad_x, dg_attn, *got = mm([dqkv, dsmall], [w_qkv_t, w_small_t], extras=[xs, dx1], vecs=[g_attn],
                               epilogue=norm_bwd, out_dtypes=[F32], n_sums=1, name="proj_bwd",
                               side=red.early_scatter(early) if red else None)
    if red:
        red.early_done(got)
    d_gains = (dg_attn, db_f[:, :N_HEADS], dg_q, dg_kv, dg_fo, dg_mo, dg_mlp, dg_fin)
    return loss, grad_x, early, late, d_gains


class GradReduction:
    def __init__(self, core_id, chip):
        self.core_id, self.chip = core_id, chip
        self.core = core_id.reshape(1).astype(jnp.int32)
        self.grads, self.pairs, self.halves, self.others = {}, {}, {}, {}

    def _other_halves(self, grads):
        out = []
        for g in grads:
            axis = 1 + split_axis(g.shape[1])
            size = g.shape[axis] // 2
            out.append(lax.dynamic_slice_in_dim(g, (1 - self.core_id) * size, size, axis=axis).astype(BF16))
        return out

    def _add_pairs(self, group, recvs):
        self.pairs[group] = [add_pair(g, r, self.core, name="add_pair_%s_%d" % (group, n))
                             for n, (g, r) in enumerate(zip(self.grads[group], recvs))]
        return scatter_side(self.pairs[group])

    def _chip_sums(self, group, scattered):
        chip = self.chip
        with_mine = [lax.dynamic_update_index_in_dim(s, lax.dynamic_index_in_dim(p, chip, 0, keepdims=True), chip, 0)
                     for s, p in zip(scattered, self.pairs[group])]
        self.halves[group] = [sum_chips(s, name="sum_chips_%s_%d" % (group, n)) for n, s in enumerate(with_mine)]
        return self.halves[group]

    def late_swap(self, grads, bf16_copies):
        self.grads["late"] = list(grads)
        return swap_halves_side(list(bf16_copies))

    def late_scatter(self, recvs):
        return self._add_pairs("late", recvs)

    def late_halves(self, scattered):
        return swap_side(self._chip_sums("late", scattered))

    def late_done(self, others):
        self.others["late"] = list(others)

    def early_scatter(self, grads):
        self.grads["early"] = list(grads)
        return self._add_pairs("early", run_side(swap_side(self._other_halves(grads)), name="swap_early_sends"))

    def early_done(self, scattered):
        self.others["early"] = list(run_side(swap_side(self._chip_sums("early", scattered)), name="swap_early_halves"))

def kernel(x, positions, attn_norm_g, w_in, b_forget, q_norm_g, w_uq, kv_norm_g, w_ukv, fox_out_g, mla_out_g, w_o, mlp_norm_g, w_up, w_down, final_norm_g, loss_target, m_attn_norm_g, m_w_in, m_b_forget, m_q_norm_g, m_w_uq, m_kv_norm_g, m_w_ukv, m_fox_out_g, m_mla_out_g, m_w_o, m_mlp_norm_g, m_w_up, m_w_down, m_final_norm_g, v_attn_norm_g, v_w_in, v_b_forget, v_q_norm_g, v_w_uq, v_kv_norm_g, v_w_ukv, v_fox_out_g, v_mla_out_g, v_w_o, v_mlp_norm_g, v_w_up, v_w_down, v_final_norm_g):
    core_id = lax.axis_index("c")
    core = core_id.reshape(1).astype(jnp.int32)
    chip = 2 * lax.axis_index("x") + lax.axis_index("y")
    big = [w_in, w_uq, w_ukv, w_o, w_up, w_down]
    big_m = [m_w_in, m_w_uq, m_w_ukv, m_w_o, m_w_up, m_w_down]
    big_v = [v_w_in, v_w_uq, v_w_ukv, v_w_o, v_w_up, v_w_down]
    n_early = 3

    def vec(a):
        return a.reshape(1, -1)

    small = [attn_norm_g, b_forget, q_norm_g, kv_norm_g, fox_out_g, mla_out_g, mlp_norm_g, final_norm_g]
    small_m = [m_attn_norm_g, m_b_forget, m_q_norm_g, m_kv_norm_g, m_fox_out_g, m_mla_out_g, m_mlp_norm_g, m_final_norm_g]
    small_v = [v_attn_norm_g, v_b_forget, v_q_norm_g, v_kv_norm_g, v_fox_out_g, v_mla_out_g, v_mlp_norm_g, v_final_norm_g]
    gains = [vec(a) for a in small]

    views = [big[0][0].T, _pad_heads(big[1][0], HEAD_DIM + ROPE_DIM)] + [w[0] for w in big[2:]]
    shards = [v.astype(BF16) for v in views]

    def with_own(gathered, mine):
        return [lax.dynamic_update_index_in_dim(g, s, chip, 0) for g, s in zip(gathered, mine)]

    def early_weights(gathered):
        g_in, g_uq, g_ukv = with_own(gathered, shards[:n_early])
        return g_in, join_cols(g_uq), join_cols(g_ukv)

    def late_weights(gathered):
        g_o, g_up, g_down = with_own(gathered, shards[n_early:])
        return g_o.reshape(-1, g_o.shape[2]), join_cols(g_up), g_down.reshape(-1, g_down.shape[2])

    reduction = GradReduction(core_id, chip)
    loss, grad_x, _, _, d_small = local_step(
        x[0], positions[0], loss_target[0], gains, early_weights, late_weights, gather_side(shards[:n_early]),
        (gather_side(shards[n_early:-1]), gather_side(shards[-1:])), reduction)
    halves = reduction.halves["early"] + reduction.halves["late"]
    others = reduction.others["early"] + reduction.others["late"]

    def rows8(vs):
        return jnp.concatenate([_pad_lanes(vec(a).astype(F32), 1024) for a in vs], axis=0)

    with_loss = [jnp.concatenate([d, loss], axis=1) if n == 1 else d for n, d in enumerate(d_small)]
    g_small8 = allreduce_small(rows8(with_loss))

    outs_big = []
    for n, (w, gm, go, m, v) in enumerate(zip(big, halves, others, big_m, big_v)):
        _, k, cols = w.shape
        if n == 0:
            outs = adamw_halves_t(w[0].T, gm, go, m[0].T, v[0].T, core, name="adamw_%d" % n)
            outs_big.append([o.T[None] for o in outs])
        else:
            if n == 1:
                gm, go = (_unpad_heads(gh, HEAD_DIM + ROPE_DIM) for gh in (gm, go))
            outs_big.append(adamw_halves(w, gm, go, m, v, core, name="adamw_%d" % n))
    d8, m8, v8 = adamw(rows8(small), g_small8, rows8(small_m), rows8(small_v), name="adamw_small")

    def unrows8(a8):
        return [a8[n, :s.size].reshape(s.shape) for n, s in enumerate(small)]

    loss_all = g_small8[1, N_HEADS]
    grads, deltas, new_m, new_v = [None] * 14, [None] * 14, [None] * 14, [None] * 14
    big_at = [1, 4, 6, 9, 11, 12]
    small_at = [0, 2, 3, 5, 7, 8, 10, 13]
    for n, at in enumerate(big_at):
        grads[at], deltas[at], new_m[at], new_v[at] = outs_big[n]
    for at, g, dd, mm_, vv in zip(small_at, unrows8(g_small8), unrows8(d8), unrows8(m8), unrows8(v8)):
        grads[at], deltas[at], new_m[at], new_v[at] = g, dd, mm_, vv
    return (loss_all, grad_x[None], *grads, *deltas, *new_m, *new_v)
```

```python
import jax
import jax.numpy as jnp
from jax import lax
from jax.experimental import pallas as pl
from jax.experimental.pallas import tpu as pltpu

F32 = jnp.float32
BF16 = jnp.bfloat16
MESH = pl.DeviceIdType.MESH

EPS = 1e-6
ROPE_THETA = 10000.0
N_HEADS = 8
PAIRS = N_HEADS // 2
HEAD_DIM = 64
ROPE_DIM = 32
LANES = 128
Q_RANK = 384
KV_RANK = 256
N_CHIPS = 4
ADAM_LR, ADAM_B1, ADAM_B2, ADAM_EPS, ADAM_WD, ADAM_STEP = 0.001, 0.9, 0.999, 1e-08, 0.01, 10
VMEM_LIMIT = 48 * 1024 * 1024
LOG2E = 1.4426950408889634
LN2 = 0.6931471805599453
NN = (((1,), (0,)), ((), ()))
NT = (((1,), (1,)), ((), ()))
TN = (((0,), (0,)), ((), ()))


def _params(sem=None):
    return pltpu.CompilerParams(dimension_semantics=sem, vmem_limit_bytes=VMEM_LIMIT)


def _fit(block, dim):
    if dim <= block:
        return dim
    return next(b for b in range(block - block % LANES, 0, -LANES) if dim % b == 0)


def _row_block(rows):
    return next(b for b in (256, 128, 64, 32, 16, 8) if rows % b == 0)


def gridded(body, *, name, grid, in_specs, out_specs, out_shape, ins, semantics, side=None):
    if side is None:
        return pl.pallas_call(body, name=name, grid=grid, in_specs=in_specs, out_specs=out_specs, out_shape=out_shape,
                              compiler_params=_params(semantics))(*ins)
    n_in, n_out = len(in_specs), len(out_specs)
    steps = 1
    for extent in grid:
        steps *= extent

    def riding(*refs):
        main_in, s_in, main_out, s_out, sems = _split_refs(refs, [n_in, len(side.ins), n_out, len(side.out_shapes), 2])
        step = 0
        for axis, extent in enumerate(grid):
            step = step * extent + pl.program_id(axis)

        @pl.when(step == 0)
        def _():
            side.first(s_in, s_out, *sems)

        body(*main_in, *main_out)

        @pl.when(step == steps - 1)
        def _():
            if side.mid is not None:
                side.mid(s_in, s_out, *sems)
            side.last(s_in, s_out, *sems)

    s_in_specs, s_out_specs = side.specs()
    return pl.pallas_call(
        riding, name=name, grid=grid, in_specs=list(in_specs) + s_in_specs, out_specs=list(out_specs) + s_out_specs,
        out_shape=list(out_shape) + side.out_shapes, scratch_shapes=side.sems(),
        compiler_params=_params(("arbitrary",) * len(grid)))(*ins, *side.ins)


def rmsnorm(x, g, *, out_dtype, name, bt=512, side=None):
    t, d = x.shape
    bt = min(bt, t)

    def body(x_ref, g_ref, o_ref):
        xv = x_ref[...].astype(F32)
        r = lax.rsqrt(jnp.mean(xv * xv, axis=-1, keepdims=True) + EPS)
        o_ref[...] = (xv * r * g_ref[...]).astype(o_ref.dtype)

    return gridded(
        body, name=name, grid=(t // bt,),
        in_specs=[pl.BlockSpec((bt, d), lambda i: (i, 0)), pl.BlockSpec((1, d), lambda i: (0, 0))],
        out_specs=[pl.BlockSpec((bt, d), lambda i: (i, 0))],
        out_shape=[jax.ShapeDtypeStruct((t, d), out_dtype)],
        ins=[x, g], semantics=("parallel",), side=side)


def mm(a, b, *, trans_b=False, a_pro=None, extras=(), vecs=(), epilogue=None, out_dtypes, n_sums=0, t_blk=None, name,
       bm=1024, bn=1024, side=None):
    a_list = list(a) if isinstance(a, (list, tuple)) else [a]
    b_list = list(b) if isinstance(b, (list, tuple)) else [b]
    m = a_list[0].shape[0]
    n = b_list[0].shape[0] if trans_b else b_list[0].shape[1]
    ks = [x.shape[1] for x in a_list]
    if sum(ks) > 2048:
        bm = bm // 2
    bm, bn = _fit(bm, m), _fit(bn, n)
    assert n_sums == 0 or bn == n
    n_ab, n_ex, n_vec, n_out = len(a_list), len(extras), len(vecs), len(out_dtypes)
    n_t = 0 if t_blk is None else 1

    def body(*refs):
        a_refs, b_refs, ex, vs, outs, t_outs, sums = _split_refs(refs, [n_ab, n_ab, n_ex, n_vec, n_out, n_t, n_sums])
        acc = None
        for a_ref, b_ref in zip(a_refs, b_refs):
            a_tile = a_ref[...] if a_pro is None else a_pro(a_ref[...])
            part = lax.dot_general(a_tile, b_ref[...], NT if trans_b else NN, preferred_element_type=F32)
            acc = part if acc is None else acc + part
        res = epilogue(acc, *[e[...] for e in ex], *[v[...] for v in vs]) if epilogue is not None else (acc,)
        for o, r in zip(outs, res[:n_out]):
            o[...] = r.astype(o.dtype)
        for t_ref in t_outs:
            for u in range(bm // t_blk):
                t_ref[u] = res[0][u * t_blk:(u + 1) * t_blk, :].T.astype(t_ref.dtype)
        if n_sums:
            @pl.when(pl.program_id(0) == 0)
            def _():
                for s_ref in sums:
                    s_ref[...] = jnp.zeros_like(s_ref)

            for s_ref, r in zip(sums, res[n_out:]):
                s_ref[...] += r

    tile = pl.BlockSpec((bm, bn), lambda i, j: (i, j))
    vec = pl.BlockSpec((1, bn), lambda i, j: (0, j))
    t_specs, t_shapes = [], []
    if t_blk is not None:
        t_specs = [pl.BlockSpec((bm // t_blk, bn, t_blk), lambda i, j: (i, j, 0))]
        t_shapes = [jax.ShapeDtypeStruct((m // t_blk, n, t_blk), out_dtypes[0])]
    a_specs = [pl.BlockSpec((bm, k), lambda i, j: (i, 0)) for k in ks]
    b_specs = [pl.BlockSpec((bn, k), lambda i, j: (j, 0)) if trans_b else pl.BlockSpec((k, bn), lambda i, j: (0, j)) for k in ks]
    return gridded(
        body, name=name, grid=(m // bm, n // bn),
        in_specs=a_specs + b_specs + [tile] * n_ex + [vec] * n_vec,
        out_specs=[tile] * n_out + t_specs + [vec] * n_sums,
        out_shape=[jax.ShapeDtypeStruct((m, n), dt) for dt in out_dtypes] + t_shapes + [jax.ShapeDtypeStruct((1, n), F32)] * n_sums,
        ins=[*a_list, *b_list, *extras, *vecs],
        semantics=("arbitrary", "arbitrary") if n_sums else ("parallel", "parallel"), side=side)


def mm_tn(a, b, *, a_pro=None, name, col_shards=1, bf16_copy=False, bk=1024, bn=1024, bt=1024):
    t, k = a.shape
    n = b.shape[1]
    ns = n // col_shards
    bk, bn, bt = _fit(bk, k), _fit(bn, ns), _fit(bt, t)
    per = ns // bn
    last = t // bt - 1

    def body(a_ref, b_ref, o_ref, *copy_ref):
        @pl.when(pl.program_id(2) == 0)
        def _():
            o_ref[...] = jnp.zeros_like(o_ref)

        a_tile = a_ref[...] if a_pro is None else a_pro(a_ref[...])
        o_ref[...] += lax.dot_general(a_tile, b_ref[...], TN, preferred_element_type=F32)
        if bf16_copy:
            @pl.when(pl.program_id(2) == last)
            def _():
                copy_ref[0][...] = o_ref[...].astype(BF16)

    if col_shards == 1:
        out_spec = pl.BlockSpec((bk, bn), lambda i, j, s: (i, j))
        shape = (k, n)
    else:
        out_spec = pl.BlockSpec((None, bk, bn), lambda i, j, s: (j // per, i, j % per))
        shape = (col_shards, k, ns)
    out_specs, out_shape = out_spec, jax.ShapeDtypeStruct(shape, F32)
    if bf16_copy:
        out_specs, out_shape = [out_spec, out_spec], [out_shape, jax.ShapeDtypeStruct(shape, BF16)]
    return pl.pallas_call(
        body, name=name, grid=(k // bk, n // bn, t // bt),
        in_specs=[pl.BlockSpec((bt, bk), lambda i, j, s: (s, i)), pl.BlockSpec((bt, bn), lambda i, j, s: (s, j))],
        out_specs=out_specs, out_shape=out_shape,
        compiler_params=_params(("parallel", "parallel", "arbitrary")),
    )(a, b)


def _split3(x):
    hi = x.astype(BF16)
    r1 = x - hi.astype(F32)
    mid = r1.astype(BF16)
    lo = (r1 - mid.astype(F32)).astype(BF16)
    return hi, mid, lo


def cumsum_rows(x, *, reverse, name, bc=512):
    t, d = x.shape
    bc = min(bc, t)
    nb = t // bc

    def body(x_ref, o_ref, carry):
        @pl.when(pl.program_id(0) == 0)
        def _():
            carry[...] = jnp.zeros_like(carry)

        r = lax.broadcasted_iota(jnp.int32, (bc, bc), 0)
        c = lax.broadcasted_iota(jnp.int32, (bc, bc), 1)
        tri = jnp.where((r <= c) if reverse else (r >= c), 1.0, 0.0).astype(BF16)
        hi, mid, lo = _split3(x_ref[...])
        s = (lax.dot_general(tri, hi, NN, preferred_element_type=F32)
             + lax.dot_general(tri, mid, NN, preferred_element_type=F32)
             + lax.dot_general(tri, lo, NN, preferred_element_type=F32)) + carry[0:1, :]
        o_ref[...] = s
        carry[0:1, :] = s[0:1, :] if reverse else s[bc - 1:bc, :]

    imap = (lambda i: (nb - 1 - i, 0)) if reverse else (lambda i: (i, 0))
    return pl.pallas_call(
        body, name=name, grid=(nb,),
        in_specs=[pl.BlockSpec((bc, d), imap)], out_specs=pl.BlockSpec((bc, d), imap),
        out_shape=jax.ShapeDtypeStruct((t, d), F32),
        scratch_shapes=[pltpu.VMEM((8, d), F32)],
        compiler_params=_params(("arbitrary",)),
    )(x)


def _rope(x, c, a, b):
    return x * c + pltpu.roll(x, LANES - ROPE_DIM // 2, 1) * a + pltpu.roll(x, ROPE_DIM // 2, 1) * b


def _rope_bwd(d, c, a, b):
    return d * c + pltpu.roll(d * a, ROPE_DIM // 2, 1) + pltpu.roll(d * b, LANES - ROPE_DIM // 2, 1)


S_CQ, S_CKV, S_KR, S_F, S_END = 0, Q_RANK, Q_RANK + KV_RANK, Q_RANK + KV_RANK + LANES, 1024
HW = N_HEADS * LANES
FW = N_HEADS * HEAD_DIM


def mla_prep(small, g_q, g_kv, w_uq, w_ukv, tab_c, tab_a, tab_b, b_f, *, name, bt=512):
    t = small.shape[0]
    bt = min(bt, t)

    def body(s_ref, gq_ref, gkv_ref, wq_ref, wkv_ref, c_ref, a_ref, b_ref, bf_ref,
             mq_ref, mk_ref, mv_ref, lf_ref, cqn_ref, ckvn_ref, mqt_ref, mvt_ref):
        cq = s_ref[:, S_CQ:S_CKV]
        rq = lax.rsqrt(jnp.mean(cq * cq, axis=-1, keepdims=True) + EPS)
        cqn = (cq * rq * gq_ref[...]).astype(BF16)
        ckv = s_ref[:, S_CKV:S_KR]
        rkv = lax.rsqrt(jnp.mean(ckv * ckv, axis=-1, keepdims=True) + EPS)
        ckvn = (ckv * rkv * gkv_ref[...]).astype(BF16)
        cqn_ref[...] = cqn
        ckvn_ref[...] = ckvn
        tc, ta, tb = c_ref[...], a_ref[...], b_ref[...]
        q = jnp.dot(cqn, wq_ref[...], preferred_element_type=F32)
        kv = jnp.dot(ckvn, wkv_ref[...], preferred_element_type=F32)
        kr = _rope(s_ref[:, S_KR:S_F], tc, ta, tb)
        for h in range(N_HEADS):
            sl = slice(h * LANES, (h + 1) * LANES)
            roped = _rope(q[:, sl], tc, ta, tb)
            mq_ref[:, sl] = roped.astype(BF16)
            mqt_ref[0, sl, :] = roped.T.astype(BF16)
            mk_ref[:, sl] = (kv[:, sl] + kr).astype(BF16)
        mv_ref[...] = kv[:, HW:].astype(BF16)
        mvt_ref[0] = kv[:, HW:].T.astype(BF16)
        z = s_ref[:, S_F:S_END - LANES] + bf_ref[...]
        lf_ref[...] = jnp.minimum(z, 0.0) - jnp.log(1.0 + jnp.exp(-jnp.abs(z)))

    def row(w):
        return pl.BlockSpec((bt, w), lambda i: (i, 0))

    def full(arr):
        return pl.BlockSpec(arr.shape, lambda i: (0, 0))

    return pl.pallas_call(
        body, name=name, grid=(t // bt,),
        in_specs=[row(S_END), full(g_q), full(g_kv), full(w_uq), full(w_ukv), row(LANES), row(LANES), row(LANES), full(b_f)],
        out_specs=[row(HW), row(HW), row(FW), row(LANES), row(Q_RANK), row(KV_RANK),
                   pl.BlockSpec((1, HW, bt), lambda i: (i, 0, 0)), pl.BlockSpec((1, FW, bt), lambda i: (i, 0, 0))],
        out_shape=[jax.ShapeDtypeStruct((t, HW), BF16)] * 2 + [jax.ShapeDtypeStruct((t, FW), BF16), jax.ShapeDtypeStruct((t, LANES), F32),
                   jax.ShapeDtypeStruct((t, Q_RANK), BF16), jax.ShapeDtypeStruct((t, KV_RANK), BF16),
                   jax.ShapeDtypeStruct((t // bt, HW, bt), BF16), jax.ShapeDtypeStruct((t // bt, FW, bt), BF16)],
        compiler_params=_params(("parallel",)),
    )(small, g_q, g_kv, w_uq, w_ukv, tab_c, tab_a, tab_b, b_f)


def mla_prep_bwd(dmq, dmk, dmv, dlf, small, g_q, g_kv, w_uq, w_ukv, tab_c, tab_a, tab_b, b_f, *, name, bt=512):
    t = small.shape[0]
    bt = min(bt, t)

    def body(dmq_ref, dmk_ref, dmv_ref, dlf_ref, s_ref, gq_ref, gkv_ref, wq_ref, wkv_ref, c_ref, a_ref, b_ref, bf_ref,
             ds_ref, dq_ref, dkv_ref, dgq_ref, dgkv_ref, db_ref):
        tc, ta, tb = c_ref[...], a_ref[...], b_ref[...]
        lane = lax.broadcasted_iota(jnp.int32, (1, LANES), 1)
        dkr = jnp.zeros((bt, LANES), F32)
        for h in range(N_HEADS):
            sl = slice(h * LANES, (h + 1) * LANES)
            dq_ref[:, sl] = _rope_bwd(dmq_ref[:, sl], tc, ta, tb).astype(BF16)
            dkr = dkr + dmk_ref[:, sl]
        dkv_ref[:, :HW] = dmk_ref[...].astype(BF16)
        dkv_ref[:, HW:] = dmv_ref[...].astype(BF16)
        in_rope = (lane >= HEAD_DIM) & (lane < HEAD_DIM + ROPE_DIM)
        ds_ref[:, S_KR:S_F] = jnp.where(in_rope, _rope_bwd(dkr, tc, ta, tb), 0.0).astype(BF16)

        def norm_bwd(raw, g_ref, dn, dg_ref):
            r = lax.rsqrt(jnp.mean(raw * raw, axis=-1, keepdims=True) + EPS)
            u = dn * g_ref[...]
            dot = jnp.mean(u * raw, axis=-1, keepdims=True)
            dg_ref[...] += jnp.sum(dn * (raw * r), axis=0, keepdims=True)
            return r * u - raw * (r * r * r * dot)

        @pl.when(pl.program_id(0) == 0)
        def _():
            dgq_ref[...] = jnp.zeros_like(dgq_ref)
            dgkv_ref[...] = jnp.zeros_like(dgkv_ref)
            db_ref[...] = jnp.zeros_like(db_ref)

        dcqn = lax.dot_general(dq_ref[...], wq_ref[...], NT, preferred_element_type=F32)
        ds_ref[:, S_CQ:S_CKV] = norm_bwd(s_ref[:, S_CQ:S_CKV], gq_ref, dcqn, dgq_ref).astype(BF16)
        dckvn = lax.dot_general(dkv_ref[...], wkv_ref[...], NT, preferred_element_type=F32)
        ds_ref[:, S_CKV:S_KR] = norm_bwd(s_ref[:, S_CKV:S_KR], gkv_ref, dckvn, dgkv_ref).astype(BF16)
        z = s_ref[:, S_F:S_END - LANES] + bf_ref[...]
        dz = jnp.where(lane < N_HEADS, dlf_ref[...] / (1.0 + jnp.exp(z)), 0.0)
        db_ref[...] += jnp.sum(dz, axis=0, keepdims=True)
        ds_ref[:, S_F:S_END - LANES] = dz.astype(BF16)
        ds_ref[:, S_END - LANES:] = jnp.zeros((bt, LANES), BF16)

    def row(w):
        return pl.BlockSpec((bt, w), lambda i: (i, 0))

    def full(arr):
        return pl.BlockSpec(arr.shape, lambda i: (0, 0))

    def vec(w):
        return pl.BlockSpec((1, w), lambda i: (0, 0))

    return pl.pallas_call(
        body, name=name, grid=(t // bt,),
        in_specs=[row(HW), row(HW), row(FW), row(LANES), row(S_END), full(g_q), full(g_kv), full(w_uq), full(w_ukv),
                  row(LANES), row(LANES), row(LANES), full(b_f)],
        out_specs=[row(S_END), row(HW), row(HW + FW), vec(Q_RANK), vec(KV_RANK), vec(LANES)],
        out_shape=[jax.ShapeDtypeStruct((t, S_END), BF16), jax.ShapeDtypeStruct((t, HW), BF16),
                   jax.ShapeDtypeStruct((t, HW + FW), BF16), jax.ShapeDtypeStruct((1, Q_RANK), F32),
                   jax.ShapeDtypeStruct((1, KV_RANK), F32), jax.ShapeDtypeStruct((1, LANES), F32)],
        compiler_params=_params(("arbitrary",)),
    )(dmq, dmk, dmv, dlf, small, g_q, g_kv, w_uq, w_ukv, tab_c, tab_a, tab_b, b_f)


class Side:
    def __init__(self, ins, out_shapes, n_sems, first, last, mid=None):
        self.ins, self.out_shapes, self.n_sems = list(ins), list(out_shapes), n_sems
        self.first, self.mid, self.last = first, mid, last

    def specs(self):
        return [ANY] * len(self.ins), [ANY] * len(self.out_shapes)

    def sems(self):
        return [pltpu.SemaphoreType.DMA((self.n_sems,)), pltpu.SemaphoreType.DMA((self.n_sems,))]


def _lane():
    return lax.broadcasted_iota(jnp.int32, (1, LANES), 1)


def _halves(x):
    zero = jnp.zeros_like(x)
    return [jnp.where(_lane() < HEAD_DIM, x, zero), jnp.where(_lane() >= HEAD_DIM, x, zero)]


def _groups(x):
    return [x[:, :LANES], x[:, LANES:]]


def _pick_row(tile, h):
    row = lax.broadcasted_iota(jnp.int32, (tile.shape[0], 1), 0)
    return jnp.sum(jnp.where(row == h, tile, 0.0), axis=0, keepdims=True)


def _pick_lane(tile, h):
    return jnp.sum(jnp.where(_lane() == h, tile, 0.0), axis=1, keepdims=True)


def _row_halves(x):
    row = lax.broadcasted_iota(jnp.int32, (LANES, 1), 0)
    zero = jnp.zeros_like(x)
    return [jnp.where(row < HEAD_DIM, x, zero), jnp.where(row >= HEAD_DIM, x, zero)]


def _below_diagonal(s, lead=0):
    r = lax.broadcasted_iota(jnp.int32, s.shape, 0)
    c = lax.broadcasted_iota(jnp.int32, s.shape, 1)
    return jnp.where(c <= r + lead, s, -jnp.inf)


def _above_diagonal(s):
    r = lax.broadcasted_iota(jnp.int32, s.shape, 0)
    c = lax.broadcasted_iota(jnp.int32, s.shape, 1)
    return jnp.where(r <= c, s, -jnp.inf)


def _split_refs(refs, counts):
    out, at = [], 0
    for n in counts:
        out.append(refs[at:at + n])
        at += n
    return out


def flash_fwd(qt_arr, k_arr, vt_arr, f_cum, *, qoff, koff, voff, pair, scale, name, blk=512, side=None):
    t = k_arr.shape[0]
    tblk = qt_arr.shape[2]
    blk = max(min(blk, t), tblk)
    sub = blk // tblk
    nb = t // blk
    w = LANES if pair else 2 * LANES
    has_bias = f_cum is not None
    ins = [qt_arr, k_arr, vt_arr] + ([f_cum] if has_bias else [])

    def wide(ref, first):
        parts = [ref[first + u] for u in range(sub)]
        return parts[0] if sub == 1 else jnp.concatenate(parts, axis=1)
    s_ins, s_outs = (side.ins, side.out_shapes) if side else ([], [])

    def body(*refs):
        main, si, outs, so, sems = _split_refs(refs, [len(ins), len(s_ins), 2, len(s_outs), 2 if side else 0])
        qt_ref, k_ref, vt_ref = main[:3]
        f_ref = main[3] if has_bias else None
        o_ref, st_ref = outs
        g, i = pl.program_id(0), pl.program_id(1)
        step_id = g * nb + i
        if side:
            @pl.when(step_id == 0)
            def _():
                side.first(si, so, *sems)

            if side.mid is not None:
                @pl.when(step_id == (3 * PAIRS * nb) // 4)
                def _():
                    side.mid(si, so, *sems)

        qt = (wide(qt_ref, 0).astype(F32) * (scale * LOG2E)).astype(BF16)
        qts = _row_halves(qt) if pair else [qt[:LANES], qt[LANES:]]

        def with_ones(vt_rows):
            return jnp.concatenate([vt_rows, jnp.ones((ACC_ROWS - HEAD_DIM, vt_rows.shape[1]), BF16)], axis=0)

        def step(j, carry, diagonal):
            rows = pl.ds(pl.multiple_of(j * blk, blk), blk)
            kk = k_ref[rows, :]
            ks = [kk, kk] if pair else _groups(kk)
            vt = wide(vt_ref, sub * j)
            out = []
            for n in range(2):
                m, acc = carry[n]
                s = jnp.dot(ks[n], qts[n], preferred_element_type=F32)
                if has_bias:
                    s = s - LOG2E * _pick_lane(f_ref[rows, :], 2 * g + n)
                if diagonal:
                    s = _above_diagonal(s)
                m_new = jnp.maximum(m, jnp.max(s, axis=0, keepdims=True))
                p = jnp.exp2(s - m_new).astype(BF16)
                out.append((m_new, jnp.exp2(m - m_new) * acc
                            + jnp.dot(with_ones(vt[n * HEAD_DIM:(n + 1) * HEAD_DIM]), p, preferred_element_type=F32)))
            return tuple(out)

        def diagonal_in_halves(carry):
            h = tblk
            halves = [pl.ds(pl.multiple_of(i * blk, blk), h), pl.ds(pl.multiple_of(i * blk + h, h), h)]
            k_top, k_bot = ([x, x] if pair else _groups(x) for x in (k_ref[halves[0], :], k_ref[halves[1], :]))
            vt_top, vt_bot = vt_ref[sub * i], vt_ref[sub * i + 1]
            out = []
            for n in range(2):
                m, acc = carry[n]
                heads = slice(n * HEAD_DIM, (n + 1) * HEAD_DIM)
                s_top = jnp.dot(k_top[n], qts[n], preferred_element_type=F32)
                s_bot = jnp.dot(k_bot[n], qts[n][:, h:], preferred_element_type=F32)
                if has_bias:
                    s_top = s_top - LOG2E * _pick_lane(f_ref[halves[0], :], 2 * g + n)
                    s_bot = s_bot - LOG2E * _pick_lane(f_ref[halves[1], :], 2 * g + n)
                s_top, s_bot = _above_diagonal(s_top), _above_diagonal(s_bot)
                m_top = jnp.maximum(m, jnp.max(s_top, axis=0, keepdims=True))
                m_new = jnp.concatenate([m_top[:, :h], jnp.maximum(m_top[:, h:], jnp.max(s_bot, axis=0, keepdims=True))], axis=1)
                p_top = jnp.exp2(s_top - m_new).astype(BF16)
                p_bot = jnp.exp2(s_bot - m_new[:, h:]).astype(BF16)
                late = jnp.concatenate([jnp.zeros((ACC_ROWS, h), F32),
                                        jnp.dot(with_ones(vt_bot[heads]), p_bot, preferred_element_type=F32)], axis=1)
                out.append((m_new, jnp.exp2(m - m_new) * acc
                            + jnp.dot(with_ones(vt_top[heads]), p_top, preferred_element_type=F32) + late))
            return tuple(out)

        init = tuple((jnp.full((1, blk), -jnp.inf, F32), jnp.zeros((ACC_ROWS, blk), F32)) for _ in range(2))
        carry = lax.fori_loop(0, i, lambda j, c: step(j, c, False), init)
        (ma, acca), (mb, accb) = diagonal_in_halves(carry) if sub == 2 else step(i, carry, True)
        la, lb = acca[HEAD_DIM:HEAD_DIM + 1], accb[HEAD_DIM:HEAD_DIM + 1]
        o_ref[...] = jnp.concatenate([acca[:HEAD_DIM] / la, accb[:HEAD_DIM] / lb], axis=0).T
        row = lax.broadcasted_iota(jnp.int32, (LANES, 1), 0)
        st_ref[0] = jnp.where(row == 0, ma + jnp.log2(la), jnp.where(row == 1, mb + jnp.log2(lb), 0.0)).T
        if side:
            @pl.when(step_id == PAIRS * nb - 1)
            def _():
                side.last(si, so, *sems)

    in_specs = [pl.BlockSpec((sub, w, tblk), lambda g, i: (i, qoff + g, 0)), pl.BlockSpec((t, w), lambda g, i: (0, koff + g)),
                pl.BlockSpec((t // tblk, LANES, tblk), lambda g, i: (0, voff + g, 0))]
    if has_bias:
        in_specs.append(pl.BlockSpec((t, LANES), lambda g, i: (0, 0)))
    s_in_specs, s_out_specs = side.specs() if side else ([], [])
    return pl.pallas_call(
        body, name=name, grid=(PAIRS, nb), in_specs=in_specs + s_in_specs,
        out_specs=[pl.BlockSpec((blk, LANES), lambda g, i: (i, g)), pl.BlockSpec((1, blk, LANES), lambda g, i: (g, i, 0))]
        + s_out_specs,
        out_shape=[jax.ShapeDtypeStruct((t, PAIRS * LANES), F32), jax.ShapeDtypeStruct((PAIRS, t, LANES), F32)] + list(s_outs),
        scratch_shapes=side.sems() if side else [],
        compiler_params=_params(("arbitrary", "arbitrary")),
    )(*ins, *s_ins)


def mix_norm(fo, mo, g_fo, g_mo, *, name, bt=512):
    t, d = fo.shape
    bt = min(bt, t)

    def body(fo_ref, mo_ref, gf_ref, gm_ref, o_ref):
        for n, (x_ref, g_ref) in enumerate(((fo_ref, gf_ref), (mo_ref, gm_ref))):
            xv = x_ref[...]
            r = lax.rsqrt(jnp.mean(xv * xv, axis=-1, keepdims=True) + EPS)
            o_ref[:, n * d:(n + 1) * d] = (xv * r * g_ref[...]).astype(BF16)

    row = pl.BlockSpec((bt, d), lambda i: (i, 0))
    vec = pl.BlockSpec((1, d), lambda i: (0, 0))
    return pl.pallas_call(
        body, name=name, grid=(t // bt,), in_specs=[row, row, vec, vec],
        out_specs=pl.BlockSpec((bt, 2 * d), lambda i: (i, 0)),
        out_shape=jax.ShapeDtypeStruct((t, 2 * d), BF16),
        compiler_params=_params(("parallel",)),
    )(fo, mo, g_fo, g_mo)


def mix_norm_bwd(dx1b, w_o, fo, mo, g_fo, g_mo, st_f, st_m, *, name, bt=512, side=None):
    t, d = fo.shape
    bt = min(bt, t)

    def body(dx_in_ref, w_ref, fo_ref, mo_ref, gf_ref, gm_ref, sf_ref, sm_ref, dfo_ref, dmo_ref, dgf_ref, dgm_ref,
             sfo_ref, smo_ref, dfot_ref, dmot_ref):
        @pl.when(pl.program_id(0) == 0)
        def _():
            dgf_ref[...] = jnp.zeros_like(dgf_ref)
            dgm_ref[...] = jnp.zeros_like(dgm_ref)

        dmixed = lax.dot_general(dx_in_ref[...], w_ref[...], NT, preferred_element_type=F32)
        groups = ((fo_ref, gf_ref, dfo_ref, dgf_ref, sf_ref, sfo_ref, dfot_ref),
                  (mo_ref, gm_ref, dmo_ref, dgm_ref, sm_ref, smo_ref, dmot_ref))
        for n, (x_ref, g_ref, dx_ref, dg_ref, st_ref, sto_ref, dxt_ref) in enumerate(groups):
            xv = x_ref[...]
            dhv = dmixed[:, n * d:(n + 1) * d]
            r = lax.rsqrt(jnp.mean(xv * xv, axis=-1, keepdims=True) + EPS)
            u = dhv * g_ref[...]
            dxf = r * u - xv * (r * r * r * jnp.mean(u * xv, axis=-1, keepdims=True))
            dxb = dxf.astype(BF16)
            dx_ref[...] = dxb
            dxt_ref[0] = dxf.T.astype(BF16)
            dg_ref[...] += jnp.sum(dhv * (xv * r), axis=0, keepdims=True)
            prod = xv * dxb.astype(F32)
            for g in range(PAIRS):
                grp = prod[:, g * LANES:(g + 1) * LANES]
                da = jnp.sum(jnp.where(_lane() < HEAD_DIM, grp, 0.0), axis=1, keepdims=True)
                db = jnp.sum(jnp.where(_lane() >= HEAD_DIM, grp, 0.0), axis=1, keepdims=True)
                sto_ref[g] = jnp.where(_lane() == 2, da, jnp.where(_lane() == 3, db, st_ref[g]))

    row = pl.BlockSpec((bt, d), lambda i: (i, 0))
    vec = pl.BlockSpec((1, d), lambda i: (0, 0))
    stat = pl.BlockSpec((PAIRS, bt, LANES), lambda i: (0, i, 0))
    return gridded(
        body, name=name, grid=(t // bt,),
        in_specs=[pl.BlockSpec((bt, dx1b.shape[1]), lambda i: (i, 0)), pl.BlockSpec(w_o.shape, lambda i: (0, 0)),
                  row, row, vec, vec, stat, stat],
        out_specs=[row, row, vec, vec, stat, stat] + [pl.BlockSpec((1, d, bt), lambda i: (i, 0, 0))] * 2,
        out_shape=[jax.ShapeDtypeStruct((t, d), BF16)] * 2 + [jax.ShapeDtypeStruct((1, d), F32)] * 2
        + [jax.ShapeDtypeStruct(st_f.shape, F32)] * 2 + [jax.ShapeDtypeStruct((t // bt, d, bt), BF16)] * 2,
        ins=[dx1b, w_o, fo, mo, g_fo, g_mo, st_f, st_m], semantics=("arbitrary",), side=side)


def flash_bwd(q_arr, qt_arr, k_arr, v_arr, do_arr, dot_arr, st, f_blocks, *, qoff, koff, voff, pair, scale, name, qblk=1024,
              side=None):
    t = q_arr.shape[0]
    blk = qt_arr.shape[2]
    qblk = max(min(qblk, t), blk)
    sub = qblk // blk
    nb, nbq = t // blk, t // qblk
    w = LANES if pair else 2 * LANES
    hw = w // 2
    has_bias = f_blocks is not None
    split = _halves if pair else _groups
    ins = [q_arr, qt_arr, k_arr, v_arr, do_arr, dot_arr, st] + ([f_blocks] if has_bias else [])
    n_out = 4 if has_bias else 3
    s_ins, s_outs = (side.ins, side.out_shapes) if side else ([], [])

    def wide(ref, first):
        parts = [ref[first + u] for u in range(sub)]
        return parts[0] if sub == 1 else jnp.concatenate(parts, axis=1)

    def body(*refs):
        main, si, outs, so, sems = _split_refs(refs, [len(ins), len(s_ins), n_out, len(s_outs), 2 if side else 0])
        q_ref, qt_ref, k_ref, v_ref, do_ref, dot_ref, st_ref = main[:7]
        dq_ref, dk_ref, dv_ref = outs[:3]
        g, j = pl.program_id(0), pl.program_id(1)
        step_id = g * nb + j
        if side:
            @pl.when(step_id == 0)
            def _():
                side.first(si, so, *sems)

        @pl.when(j == 0)
        def _():
            dq_ref[...] = jnp.zeros_like(dq_ref)

        kk, vv = k_ref[...], v_ref[...]
        ks = [kk, kk] if pair else _groups(kk)
        if has_bias:
            f_ref, df_ref = main[7], outs[3]
            fk = [LOG2E * _pick_row(f_ref[0], 2 * g + n) for n in range(2)]

            @pl.when(step_id == 0)
            def _():
                df_ref[...] = jnp.zeros_like(df_ref)

        def step(i, carry, diagonal):
            rows = pl.ds(pl.multiple_of(i * qblk, qblk), qblk)
            qs = split((q_ref[rows, :].astype(F32) * (scale * LOG2E)).astype(BF16))
            qt = (wide(qt_ref, sub * i).astype(F32) * (scale * LOG2E)).astype(BF16)
            dos = [h.astype(BF16) for h in _halves(do_ref[rows, :].astype(F32))]
            dot = wide(dot_ref, sub * i)
            stats = st_ref[0, rows, :]
            new, dqs, row_sums = [], [], []
            for n in range(2):
                dkt, dvt, dfk = carry[n]
                s = lax.dot_general(qs[n], ks[n], NT, preferred_element_type=F32)
                if has_bias:
                    s = s - fk[n]
                if diagonal:
                    s = _below_diagonal(s, i * qblk - j * blk)
                p = jnp.exp2(s - stats[:, n:n + 1])
                dp = lax.dot_general(dos[n], vv, NT, preferred_element_type=F32)
                ds = p * (dp - stats[:, 2 + n:3 + n])
                dsb = ds.astype(BF16)
                dvt = dvt + jnp.dot(dot[n * HEAD_DIM:(n + 1) * HEAD_DIM], p.astype(BF16), preferred_element_type=F32)
                dkt = dkt + jnp.dot(qt[n * hw:(n + 1) * hw], dsb, preferred_element_type=F32)
                dqs.append(jnp.dot(dsb, ks[n], preferred_element_type=F32))
                if has_bias:
                    dfk = dfk - jnp.sum(ds, axis=0, keepdims=True)
                    row_sums.append(jnp.sum(ds, axis=1, keepdims=True))
                new.append((dkt, dvt, dfk))
            dq = (jnp.where(_lane() < HEAD_DIM, dqs[0], dqs[1]) if pair else jnp.concatenate(dqs, axis=1)) * scale
            if has_bias:
                df_ref[rows, :] += jnp.where(_lane() == 2 * g, row_sums[0], jnp.where(_lane() == 2 * g + 1, row_sums[1], 0.0))
            dq_ref[rows, :] += dq
            return tuple(new)

        init = tuple((jnp.zeros((hw, blk), F32), jnp.zeros((HEAD_DIM, blk), F32), jnp.zeros((1, blk), F32)) for _ in range(2))
        first = j // sub
        carry = step(first, init, True)
        (dka, dva, dfa), (dkb, dvb, dfb) = lax.fori_loop(first + 1, nbq, lambda i, c: step(i, c, False), carry)
        dk_ref[...] = jnp.concatenate([dka, dkb], axis=0).T * LN2
        dv_ref[...] = jnp.concatenate([dva, dvb], axis=0).T
        if has_bias:
            row = lax.broadcasted_iota(jnp.int32, (LANES, 1), 0)
            by_head = jnp.where(row == 2 * g, dfa, jnp.where(row == 2 * g + 1, dfb, 0.0))
            df_ref[pl.ds(pl.multiple_of(j * blk, blk), blk), :] += by_head.T
        if side:
            @pl.when(step_id == PAIRS * nb - 1)
            def _():
                side.last(si, so, *sems)

    in_specs = [pl.BlockSpec((t, w), lambda g, j: (0, qoff + g)), pl.BlockSpec((nb, w, blk), lambda g, j: (0, qoff + g, 0)),
                pl.BlockSpec((blk, w), lambda g, j: (j, koff + g)), pl.BlockSpec((blk, LANES), lambda g, j: (j, voff + g)),
                pl.BlockSpec((t, LANES), lambda g, j: (0, g)), pl.BlockSpec((nb, LANES, blk), lambda g, j: (0, g, 0)),
                pl.BlockSpec((1, t, LANES), lambda g, j: (g, 0, 0))]
    out_specs = [pl.BlockSpec((t, w), lambda g, j: (0, g)), pl.BlockSpec((blk, w), lambda g, j: (j, g)),
                 pl.BlockSpec((blk, LANES), lambda g, j: (j, g))]
    out_shape = [jax.ShapeDtypeStruct((t, PAIRS * w), F32)] * 2 + [jax.ShapeDtypeStruct((t, PAIRS * LANES), F32)]
    if has_bias:
        in_specs.append(pl.BlockSpec((1, N_HEADS, blk), lambda g, j: (j, 0, 0)))
        out_specs.append(pl.BlockSpec((t, LANES), lambda g, j: (0, 0)))
        out_shape.append(jax.ShapeDtypeStruct((t, LANES), F32))
    s_in_specs, s_out_specs = side.specs() if side else ([], [])
    return pl.pallas_call(
        body, name=name, grid=(PAIRS, nb), in_specs=in_specs + s_in_specs, out_specs=out_specs + s_out_specs,
        out_shape=out_shape + list(s_outs), scratch_shapes=side.sems() if side else [],
        compiler_params=_params(("arbitrary", "arbitrary")),
    )(*ins, *s_ins)


def _adamw_math(w, g, m, v):
    nm = ADAM_B1 * m + (1.0 - ADAM_B1) * g
    nv = ADAM_B2 * v + (1.0 - ADAM_B2) * (g * g)
    m_hat = nm / (1.0 - ADAM_B1 ** ADAM_STEP)
    v_hat = nv / (1.0 - ADAM_B2 ** ADAM_STEP)
    return -ADAM_LR * (m_hat / (jnp.sqrt(v_hat) + ADAM_EPS) + ADAM_WD * w), nm, nv


def adamw(w, g, m, v, *, name):
    rws, cols = w.shape
    br = _row_block(rws)

    def body(w_ref, g_ref, m_ref, v_ref, d_ref, nm_ref, nv_ref):
        d_ref[...], nm_ref[...], nv_ref[...] = _adamw_math(w_ref[...], g_ref[...], m_ref[...], v_ref[...])

    blk = pl.BlockSpec((br, cols), lambda i: (i, 0))
    return pl.pallas_call(
        body, name=name, grid=(rws // br,), in_specs=[blk] * 4, out_specs=[blk] * 3,
        out_shape=[jax.ShapeDtypeStruct((rws, cols), F32)] * 3,
        compiler_params=_params(("parallel",)),
    )(w, g, m, v)


def adamw_halves(w, g_mine, g_other, m, v, core, *, name):
    _, k, n = w.shape
    br = _row_block(k // 2)
    nh = k // 2 // br

    def body(c_ref, w_ref, gm_ref, go_ref, m_ref, v_ref, g_out, d_ref, nm_ref, nv_ref):
        gv = jnp.where(pl.program_id(0) == c_ref[0], gm_ref[...], go_ref[...])
        g_out[0] = gv
        d_ref[0], nm_ref[0], nv_ref[0] = _adamw_math(w_ref[0], gv, m_ref[0], v_ref[0])

    full = pl.BlockSpec((1, br, n), lambda hb, i, c: (0, hb * nh + i, 0))
    half = pl.BlockSpec((br, n), lambda hb, i, c: (i, 0))
    return pl.pallas_call(
        body, name=name,
        grid_spec=pltpu.PrefetchScalarGridSpec(num_scalar_prefetch=1, grid=(2, nh), in_specs=[full, half, half, full, full],
                                               out_specs=[full] * 4),
        out_shape=[jax.ShapeDtypeStruct(w.shape, F32)] * 4,
        compiler_params=_params(("parallel", "parallel")),
    )(core, w, g_mine, g_other, m, v)


def adamw_halves_t(wt, gt_mine, gt_other, mt, vt, core, *, name, bc=128):
    n, k = wt.shape
    nh = k // 2 // bc

    def body(c_ref, w_ref, gm_ref, go_ref, m_ref, v_ref, g_out, d_ref, nm_ref, nv_ref):
        gv = jnp.where(pl.program_id(0) == c_ref[0], gm_ref[...], go_ref[...])
        g_out[...] = gv
        d_ref[...], nm_ref[...], nv_ref[...] = _adamw_math(w_ref[...], gv, m_ref[...], v_ref[...])

    full = pl.BlockSpec((n, bc), lambda hb, i, c: (0, hb * nh + i))
    half = pl.BlockSpec((n, bc), lambda hb, i, c: (0, i))
    return pl.pallas_call(
        body, name=name,
        grid_spec=pltpu.PrefetchScalarGridSpec(num_scalar_prefetch=1, grid=(2, nh), in_specs=[full, half, half, full, full],
                                               out_specs=[full] * 4),
        out_shape=[jax.ShapeDtypeStruct(wt.shape, F32)] * 4,
        compiler_params=_params(("parallel", "parallel")),
    )(core, wt, gt_mine, gt_other, mt, vt)


def add_pair(dw, recv, core, *, name):
    n4, k, n = dw.shape
    half = (1, k // 2, n) if split_axis(k) == 0 else (1, k, n // 2)
    mine = (lambda q, c: (q, c[0], 0)) if split_axis(k) == 0 else (lambda q, c: (q, 0, c[0]))

    def body(c_ref, a_ref, b_ref, o_ref):
        o_ref[...] = (a_ref[...] + b_ref[...].astype(F32)).astype(BF16)

    return pl.pallas_call(
        body, name=name,
        grid_spec=pltpu.PrefetchScalarGridSpec(
            num_scalar_prefetch=1, grid=(n4,),
            in_specs=[pl.BlockSpec(half, mine), pl.BlockSpec(half, lambda q, c: (q, 0, 0))],
            out_specs=pl.BlockSpec(half, lambda q, c: (q, 0, 0))),
        out_shape=jax.ShapeDtypeStruct((n4,) + half[1:], BF16),
        compiler_params=_params(("parallel",)),
    )(core, dw, recv)


def sum_chips(parts, *, name):
    n4, r, n = parts.shape
    if r % 16 == 0:
        br, bc = _row_block(r), n
    else:
        br, bc = r, LANES

    def body(p_ref, o_ref):
        acc = p_ref[0].astype(F32)
        for q in range(1, n4):
            acc = acc + p_ref[q].astype(F32)
        o_ref[...] = acc

    return pl.pallas_call(
        body, name=name, grid=(r // br, n // bc),
        in_specs=[pl.BlockSpec((n4, br, bc), lambda i, j: (0, i, j))], out_specs=pl.BlockSpec((br, bc), lambda i, j: (i, j)),
        out_shape=jax.ShapeDtypeStruct((r, n), F32),
        compiler_params=_params(("parallel", "parallel")),
    )(parts)


ANY = pl.BlockSpec(memory_space=pl.ANY)


def _place():
    x, y, c = lax.axis_index("x"), lax.axis_index("y"), lax.axis_index("c")
    chips = [(1 - x, y), (x, 1 - y), (1 - x, 1 - y)]
    return x, y, c, chips


def _copy(src, dst, send_sems, recv_sems, k, to):
    return pltpu.make_async_remote_copy(src_ref=src, dst_ref=dst, send_sem=send_sems.at[k], recv_sem=recv_sems.at[k],
                                        device_id=to, device_id_type=MESH)


def split_axis(rows):
    return 0 if rows % 32 == 0 else 1


def _half(ref, lead, hf):
    rows, cols = ref.shape[-2:]
    if split_axis(rows) == 0:
        at = (pl.ds(hf * (rows // 2), rows // 2), slice(None))
    else:
        at = (slice(None), pl.ds(hf * (cols // 2), cols // 2))
    return ref.at[at] if lead is None else ref.at[(lead,) + at]


def _gather_first(srcs, dsts, ssems, rsems):
    x, y, c, chips = _place()
    for ti, (s, d) in enumerate(zip(srcs, dsts)):
        for j, (cx, cy) in enumerate(chips):
            _copy(_half(s, None, c), _half(d, 2 * x + y, c), ssems, rsems, 3 * ti + j, (cx, cy, c)).start()


def _gather_mid(srcs, dsts, ssems, rsems):
    x, y, c, chips = _place()
    n1 = 3 * len(srcs)
    for ti, d in enumerate(dsts):
        for j, (cx, cy) in enumerate(chips):
            landed = _half(d, 2 * cx + cy, c)
            _copy(landed, landed, ssems, rsems, 3 * ti + j, (cx, cy, c)).wait_recv()
            _copy(landed, landed, ssems, rsems, n1 + 3 * ti + j, (x, y, 1 - c)).start()


def _gather_last(srcs, dsts, ssems, rsems):
    x, y, c, chips = _place()
    n1 = 3 * len(srcs)
    for ti, (s, d) in enumerate(zip(srcs, dsts)):
        for j, (cx, cy) in enumerate(chips):
            other = _half(d, 2 * cx + cy, 1 - c)
            _copy(other, other, ssems, rsems, n1 + 3 * ti + j, (x, y, 1 - c)).wait_recv()
        for j, (cx, cy) in enumerate(chips):
            mine = _half(s, None, c)
            _copy(mine, mine, ssems, rsems, 3 * ti + j, (cx, cy, c)).wait_send()
            _copy(mine, mine, ssems, rsems, n1 + 3 * ti + j, (x, y, 1 - c)).wait_send()


def gather_side(shards):
    return Side(shards, [jax.ShapeDtypeStruct((N_CHIPS,) + s.shape, s.dtype) for s in shards], 6 * len(shards),
                _gather_first, _gather_last, _gather_mid)


def _scatter_first(srcs, dsts, ssems, rsems):
    x, y, c, chips = _place()
    for ti, (s, d) in enumerate(zip(srcs, dsts)):
        for j, (cx, cy) in enumerate(chips):
            _copy(s.at[2 * cx + cy], d.at[2 * x + y], ssems, rsems, 3 * ti + j, (cx, cy, c)).start()


def _scatter_last(srcs, dsts, ssems, rsems):
    x, y, c, chips = _place()
    for ti, (s, d) in enumerate(zip(srcs, dsts)):
        for j, (cx, cy) in enumerate(chips):
            _copy(s.at[2 * cx + cy], d.at[2 * cx + cy], ssems, rsems, 3 * ti + j, (cx, cy, c)).wait_recv()
        for j, (cx, cy) in enumerate(chips):
            _copy(s.at[2 * cx + cy], d.at[2 * cx + cy], ssems, rsems, 3 * ti + j, (cx, cy, c)).wait_send()


def scatter_side(parts):
    return Side(parts, [jax.ShapeDtypeStruct(p.shape, p.dtype) for p in parts], 3 * len(parts), _scatter_first, _scatter_last)


def run_side(side, *, name):
    n_in, n_out = len(side.ins), len(side.out_shapes)

    def body(*refs):
        si, so, sems = _split_refs(refs, [n_in, n_out, 2])
        side.first(si, so, *sems)
        if side.mid is not None:
            side.mid(si, so, *sems)
        side.last(si, so, *sems)

    in_specs, out_specs = side.specs()
    return pl.pallas_call(body, name=name, in_specs=in_specs, out_specs=out_specs, out_shape=side.out_shapes,
                          scratch_shapes=side.sems())(*side.ins)


def _swap_first(srcs, dsts, ssems, rsems):
    x, y, c, _ = _place()
    for k, (s, d) in enumerate(zip(srcs, dsts)):
        _copy(s, d, ssems, rsems, k, (x, y, 1 - c)).start()


def _swap_last(srcs, dsts, ssems, rsems):
    x, y, c, _ = _place()
    for k, (s, d) in enumerate(zip(srcs, dsts)):
        _copy(s, d, ssems, rsems, k, (x, y, 1 - c)).wait()


def swap_side(xs):
    return Side(xs, [jax.ShapeDtypeStruct(a.shape, a.dtype) for a in xs], len(xs), _swap_first, _swap_last)


def _swap_halves(srcs, dsts, ssems, rsems):
    x, y, c, _ = _place()
    for k, (s, d) in enumerate(zip(srcs, dsts)):
        hk = s.shape[1] // 2
        yield _copy(s.at[:, pl.ds((1 - c) * hk, hk), :], d, ssems, rsems, k, (x, y, 1 - c))


def _swap_halves_first(srcs, dsts, ssems, rsems):
    for cp in _swap_halves(srcs, dsts, ssems, rsems):
        cp.start()


def _swap_halves_last(srcs, dsts, ssems, rsems):
    for cp in _swap_halves(srcs, dsts, ssems, rsems):
        cp.wait()


def swap_halves_side(xs):
    return Side(xs, [jax.ShapeDtypeStruct((a.shape[0], a.shape[1] // 2, a.shape[2]), a.dtype) for a in xs], len(xs),
                _swap_halves_first, _swap_halves_last)


def allreduce_small(s):
    n_dev = 8

    def body(s_ref, out_ref, buf, send_sems, recv_sems):
        x, y, c, _ = _place()
        me = 4 * x + 2 * y + c
        buf[me] = s_ref[...]
        sends = []
        for k in range(1, n_dev):
            px = 1 - x if k & 4 else x
            py = 1 - y if k & 2 else y
            pc = 1 - c if k & 1 else c
            cp = _copy(s_ref, buf.at[me], send_sems, recv_sems, k - 1, (px, py, pc))
            cp.start()
            sends.append((cp, 4 * px + 2 * py + pc))
        for k, (cp, peer) in enumerate(sends):
            _copy(s_ref, buf.at[peer], send_sems, recv_sems, k, (x, y, c)).wait_recv()
        for cp, _ in sends:
            cp.wait_send()
        acc = buf[0]
        for d in range(1, n_dev):
            acc = acc + buf[d]
        out_ref[...] = acc

    vm = pl.BlockSpec(memory_space=pltpu.VMEM)
    return pl.pallas_call(
        body, name="allreduce_small", in_specs=[vm], out_specs=vm,
        out_shape=jax.ShapeDtypeStruct(s.shape, F32),
        scratch_shapes=[pltpu.VMEM((n_dev,) + s.shape, F32), pltpu.SemaphoreType.DMA((n_dev - 1,)),
                        pltpu.SemaphoreType.DMA((n_dev - 1,))],
    )(s)


def join_cols(sm):
    n4, k, n = sm.shape
    return sm.transpose(1, 0, 2).reshape(k, n4 * n)


def split_cols(full):
    k, n = full.shape
    return full.reshape(k, N_CHIPS, n // N_CHIPS).transpose(1, 0, 2)


def _pad_heads(w, width):
    lead, heads = w.shape[:-1], w.shape[-1] // width
    w = w.reshape(lead + (heads, width))
    return jnp.pad(w, [(0, 0)] * len(lead) + [(0, 0), (0, LANES - width)]).reshape(lead + (heads * LANES,))


def _unpad_heads(w, width):
    lead, heads = w.shape[:-1], w.shape[-1] // LANES
    return w.reshape(lead + (heads, LANES))[..., :width].reshape(lead + (heads * width,))


O_F = 3 * FW
O_CQ = O_F + N_HEADS
O_CKV = O_CQ + Q_RANK
O_KR = O_CKV + KV_RANK
O_END = O_KR + ROPE_DIM


def _shard_rows(sm, a, b):
    r = sm.shape[1]
    out = []
    while a < b:
        q = a // r
        e = min(b, (q + 1) * r)
        out.append(sm[q, a - q * r:e - q * r])
        a = e
    return out


def split_w_in_t(w_sm):
    d = w_sm.shape[2]

    def z(n):
        return [jnp.zeros((n, d), w_sm.dtype)]

    small = (_shard_rows(w_sm, O_CQ, O_CKV) + _shard_rows(w_sm, O_CKV, O_KR) + z(HEAD_DIM) + _shard_rows(w_sm, O_KR, O_END)
             + z(LANES - HEAD_DIM - ROPE_DIM) + _shard_rows(w_sm, O_F, O_CQ) + z(LANES - N_HEADS) + z(LANES))
    return jnp.concatenate(_shard_rows(w_sm, 0, O_F), axis=0), jnp.concatenate(small, axis=0)


def join_w_in_t(d_qkv_t, d_small_t):
    kr = S_KR + HEAD_DIM
    segments = [(d_qkv_t, 0, 0, O_F), (d_small_t, S_F, O_F, N_HEADS), (d_small_t, S_CQ, O_CQ, Q_RANK),
                (d_small_t, S_CKV, O_CKV, KV_RANK), (d_small_t, kr, O_KR, ROPE_DIM)]
    r = O_END // N_CHIPS
    shards = []
    for q in range(N_CHIPS):
        pieces = []
        for src, s0, v0, n in segments:
            a, b = max(v0, q * r), min(v0 + n, (q + 1) * r)
            if a < b:
                pieces.append(src[s0 + a - v0:s0 + b - v0])
        shards.append(jnp.concatenate(pieces, axis=0))
    return jnp.stack(shards)


def rope_tables(pos):
    t = pos.shape[0]
    inv_freq = ROPE_THETA ** (-jnp.arange(0, ROPE_DIM, 2, dtype=F32) / ROPE_DIM)
    ang = pos.astype(F32)[:, None] * inv_freq
    cos, sin = jnp.cos(ang), jnp.sin(ang)
    half = ROPE_DIM // 2

    def z(n):
        return jnp.zeros((t, n), F32)

    tab_c = jnp.concatenate([jnp.ones((t, HEAD_DIM), F32), cos, cos, z(LANES - HEAD_DIM - ROPE_DIM)], axis=1)
    tab_a = jnp.concatenate([z(HEAD_DIM), -sin, z(half), z(LANES - HEAD_DIM - ROPE_DIM)], axis=1)
    tab_b = jnp.concatenate([z(HEAD_DIM), z(half), sin, z(LANES - HEAD_DIM - ROPE_DIM)], axis=1)
    return tab_c, tab_a, tab_b


def _pad_lanes(v, n):
    return jnp.pad(v, ((0, 0), (0, n - v.shape[1])))


ATTN_BLK = 512
ATTN_FWD_BLK = 1024
ATTN_BWD_QBLK = 1024
ACC_ROWS = HEAD_DIM + 16


def local_step(xs, pos, tgt, gains, early_weights, late_weights, early_side=None, fwd_sides=(None, None), reduction=None):
    g_attn, b_forget, g_q, g_kv, g_fo, g_mo, g_mlp, g_fin = gains
    t = xs.shape[0]
    blk = min(ATTN_BLK, t)
    fox_scale = 1.0 / (HEAD_DIM ** 0.5)
    mla_scale = 1.0 / ((HEAD_DIM + ROPE_DIM) ** 0.5)

    h1, *gathered = rmsnorm(xs, g_attn, out_dtype=BF16, name="norm_attn", side=early_side)
    w_in_t, w_uq_p, w_ukv = early_weights(gathered)
    w_qkv_t, w_small_t = split_w_in_t(w_in_t)
    kv = w_ukv.reshape(KV_RANK, N_HEADS, 2 * HEAD_DIM)
    w_ukv_p = jnp.concatenate([_pad_heads(kv[:, :, :HEAD_DIM].reshape(KV_RANK, FW), HEAD_DIM),
                               kv[:, :, HEAD_DIM:].reshape(KV_RANK, FW)], axis=1)
    b_f = _pad_lanes(b_forget, LANES)
    tab_c, tab_a, tab_b = rope_tables(pos)

    qkv, qkv_t = mm(h1, w_qkv_t, trans_b=True, out_dtypes=[BF16], t_blk=blk, name="proj_qkv")
    small, = mm(h1, w_small_t, trans_b=True, out_dtypes=[F32], name="proj_small")
    mq, mk, mv, lf, cqn, ckvn, mq_t, mv_t = mla_prep(small, g_q, g_kv, w_uq_p, w_ukv_p, tab_c, tab_a, tab_b, b_f,
                                                     name="mla_prep", bt=blk)
    f_cum = cumsum_rows(lf, reverse=False, name="gate_cumsum")
    f_blocks = f_cum[:, :N_HEADS].reshape(t // blk, blk, N_HEADS).transpose(0, 2, 1)
    fo, st_f, *gathered = flash_fwd(qkv_t, qkv, qkv_t, f_cum, qoff=0, koff=PAIRS, voff=2 * PAIRS, pair=True,
                                    scale=fox_scale, name="fox_fwd", blk=ATTN_FWD_BLK, side=fwd_sides[0])
    mo, st_m, *more = flash_fwd(mq_t, mk, mv_t, None, qoff=0, koff=0, voff=0, pair=False, scale=mla_scale, name="mla_fwd",
                                blk=ATTN_FWD_BLK, side=fwd_sides[1])
    w_o, w_up, w_down = late_weights(gathered + more)
    mixed = mix_norm(fo, mo, g_fo, g_mo, name="norm_mix")

    def inv_rms(v):
        return lax.rsqrt(jnp.mean(v * v, axis=-1, keepdims=True) + EPS)

    def residual_then_norm(acc, res, g):
        xn = acc + res
        return xn, xn * inv_rms(xn) * g

    def norm_bwd(dh, xn, res, g):
        r = inv_rms(xn)
        uu = dh * g
        return (r * uu - xn * (r * r * r * jnp.mean(uu * xn, axis=-1, keepdims=True)) + res,
                jnp.sum(dh * (xn * r), axis=0, keepdims=True))

    def norm_bwd2(dh, xn, res, g):
        dx, dg = norm_bwd(dh, xn, res, g)
        return dx, dx, dg

    def residual_then_loss(acc, res, target, g):
        xn = acc + res
        r = inv_rms(xn)
        xh = xn * r
        e = xh * g - target
        part = 0.5 * jnp.sum(jnp.mean(e * e, axis=-1, keepdims=True), axis=0, keepdims=True)
        dy = e * (1.0 / xn.shape[1])
        uu = dy * g
        dx = r * uu - xn * (r * r * r * jnp.mean(uu * xn, axis=-1, keepdims=True))
        return dx, dx, jnp.sum(dy * xh, axis=0, keepdims=True), part + jnp.zeros_like(g)

    x1, h2 = mm(mixed, w_o, extras=[xs], vecs=[g_mlp], epilogue=residual_then_norm, out_dtypes=[F32, BF16], name="out_proj")

    def relu2(uu):
        r = jnp.maximum(uu.astype(F32), 0.0)
        return (r * r).astype(BF16)

    u, = mm(h2, w_up, out_dtypes=[BF16], name="mlp_up")
    dx2, dx2b, dg_fin, loss_row = mm(u, w_down, a_pro=relu2, extras=[x1, tgt], vecs=[g_fin], epilogue=residual_then_loss,
                                     out_dtypes=[F32, BF16], n_sums=2, name="mlp_down_loss")
    loss = loss_row[:, :1]

    def relu2_grad(acc, uu):
        return (acc * (2.0 * jnp.maximum(uu.astype(F32), 0.0)),)

    du, = mm(dx2b, w_down, trans_b=True, extras=[u], epilogue=relu2_grad, out_dtypes=[BF16], name="mlp_down_bwd")
    dw_down, dw_down_b = (g.reshape(N_CHIPS, -1, w_down.shape[1])
                          for g in mm_tn(u, dx2b, a_pro=relu2, name="dw_down", bf16_copy=True))
    dx1, dx1b, dg_mlp = mm(du, w_up, trans_b=True, extras=[x1, dx2], vecs=[g_mlp], epilogue=norm_bwd2,
                           out_dtypes=[F32, BF16], n_sums=1, name="mlp_up_bwd")
    dw_up, dw_up_b = mm_tn(h2, du, name="dw_up", col_shards=N_CHIPS, bf16_copy=True)

    dw_o, dw_o_b = (g.reshape(N_CHIPS, -1, w_o.shape[1]) for g in mm_tn(mixed, dx1b, name="dw_o", bf16_copy=True))
    late, late_b = (dw_o, dw_up, dw_down), (dw_o_b, dw_up_b, dw_down_b)
    red = reduction
    dfo, dmo, dg_fo, dg_mo, st_f, st_m, dfo_t, dmo_t, *got = mix_norm_bwd(
        dx1b, w_o, fo, mo, g_fo, g_mo, st_f, st_m, name="out_proj_mix_bwd", bt=blk,
        side=red.late_swap(late, late_b) if red else None)
    dfq, dfk, dfv, d_f, *got = flash_bwd(
        qkv, qkv_t, qkv, qkv, dfo, dfo_t, st_f, f_blocks, qoff=0, koff=PAIRS, voff=2 * PAIRS, pair=True,
        scale=fox_scale, name="fox_bwd", qblk=ATTN_BWD_QBLK, side=red.late_scatter(got) if red else None)
    dmq, dmk, dmv, *got = flash_bwd(mq, mq_t, mk, mv, dmo, dmo_t, st_m, None, qoff=0, koff=0, voff=0,
                                    pair=False, scale=mla_scale, name="mla_bwd", qblk=ATTN_BWD_QBLK,
                                    side=red.late_halves(got) if red else None)
    if red:
        red.late_done(got)
    dqkv = jnp.concatenate([dfq, dfk, dfv], axis=1).astype(BF16)
    dlf = cumsum_rows(d_f, reverse=True, name="gate_cumsum_bwd")
    dsmall, dq_u, dkv_u, dg_q, dg_kv, db_f = mla_prep_bwd(dmq, dmk, dmv, dlf, small, g_q, g_kv, w_uq_p, w_ukv_p,
                                                          tab_c, tab_a, tab_b, b_f, name="mla_prep_bwd")
    dw_uq_p = mm_tn(cqn, dq_u, name="dw_uq")
    dw_ukv_p = mm_tn(ckvn, dkv_u, name="dw_ukv")

    dw_in_t = join_w_in_t(mm_tn(dqkv, h1, name="dw_qkv"), mm_tn(dsmall, h1, name="dw_small"))
    dk_cols = _unpad_heads(dw_ukv_p[:, :HW], HEAD_DIM).reshape(KV_RANK, N_HEADS, HEAD_DIM)
    dv_cols = dw_ukv_p[:, HW:].reshape(KV_RANK, N_HEADS, HEAD_DIM)
    dw_ukv = jnp.concatenate([dk_cols, dv_cols], axis=2).reshape(KV_RANK, N_HEADS * 2 * HEAD_DIM)
    early = (dw_in_t, split_cols(dw_uq_p), split_cols(dw_ukv))
    grad_x, dg_attn, *got = mm([dqkv, dsmall], [w_qkv_t, w_small_t], extras=[xs, dx1], vecs=[g_attn],
                               epilogue=norm_bwd, out_dtypes=[F32], n_sums=1, name="proj_bwd",
                               side=red.early_scatter(early) if red else None)
    if red:
        red.early_done(got)
    d_gains = (dg_attn, db_f[:, :N_HEADS], dg_q, dg_kv, dg_fo, dg_mo, dg_mlp, dg_fin)
    return loss, grad_x, early, late, d_gains


class GradReduction:
    def __init__(self, core_id, chip):
        self.core_id, self.chip = core_id, chip
        self.core = core_id.reshape(1).astype(jnp.int32)
        self.grads, self.pairs, self.halves, self.others = {}, {}, {}, {}

    def _other_halves(self, grads):
        out = []
        for g in grads:
            axis = 1 + split_axis(g.shape[1])
            size = g.shape[axis] // 2
            out.append(lax.dynamic_slice_in_dim(g, (1 - self.core_id) * size, size, axis=axis).astype(BF16))
        return out

    def _add_pairs(self, group, recvs):
        self.pairs[group] = [add_pair(g, r, self.core, name="add_pair_%s_%d" % (group, n))
                             for n, (g, r) in enumerate(zip(self.grads[group], recvs))]
        return scatter_side(self.pairs[group])

    def _chip_sums(self, group, scattered):
        chip = self.chip
        with_mine = [lax.dynamic_update_index_in_dim(s, lax.dynamic_index_in_dim(p, chip, 0, keepdims=True), chip, 0)
                     for s, p in zip(scattered, self.pairs[group])]
        self.halves[group] = [sum_chips(s, name="sum_chips_%s_%d" % (group, n)) for n, s in enumerate(with_mine)]
        return self.halves[group]

    def late_swap(self, grads, bf16_copies):
        self.grads["late"] = list(grads)
        return swap_halves_side(list(bf16_copies))

    def late_scatter(self, recvs):
        return self._add_pairs("late", recvs)

    def late_halves(self, scattered):
        return swap_side(self._chip_sums("late", scattered))

    def late_done(self, others):
        self.others["late"] = list(others)

    def early_scatter(self, grads):
        self.grads["early"] = list(grads)
        return self._add_pairs("early", run_side(swap_side(self._other_halves(grads)), name="swap_early_sends"))

    def early_done(self, scattered):
        self.others["early"] = list(run_side(swap_side(self._chip_sums("early", scattered)), name="swap_early_halves"))

def kernel(x, positions, attn_norm_g, w_in, b_forget, q_norm_g, w_uq, kv_norm_g, w_ukv, fox_out_g, mla_out_g, w_o, mlp_norm_g, w_up, w_down, final_norm_g, loss_target, m_attn_norm_g, m_w_in, m_b_forget, m_q_norm_g, m_w_uq, m_kv_norm_g, m_w_ukv, m_fox_out_g, m_mla_out_g, m_w_o, m_mlp_norm_g, m_w_up, m_w_down, m_final_norm_g, v_attn_norm_g, v_w_in, v_b_forget, v_q_norm_g, v_w_uq, v_kv_norm_g, v_w_ukv, v_fox_out_g, v_mla_out_g, v_w_o, v_mlp_norm_g, v_w_up, v_w_down, v_final_norm_g):
    core_id = lax.axis_index("c")
    core = core_id.reshape(1).astype(jnp.int32)
    chip = 2 * lax.axis_index("x") + lax.axis_index("y")
    big = [w_in, w_uq, w_ukv, w_o, w_up, w_down]
    big_m = [m_w_in, m_w_uq, m_w_ukv, m_w_o, m_w_up, m_w_down]
    big_v = [v_w_in, v_w_uq, v_w_ukv, v_w_o, v_w_up, v_w_down]
    n_early = 3

    def vec(a):
        return a.reshape(1, -1)

    small = [attn_norm_g, b_forget, q_norm_g, kv_norm_g, fox_out_g, mla_out_g, mlp_norm_g, final_norm_g]
    small_m = [m_attn_norm_g, m_b_forget, m_q_norm_g, m_kv_norm_g, m_fox_out_g, m_mla_out_g, m_mlp_norm_g, m_final_norm_g]
    small_v = [v_attn_norm_g, v_b_forget, v_q_norm_g, v_kv_norm_g, v_fox_out_g, v_mla_out_g, v_mlp_norm_g, v_final_norm_g]
    gains = [vec(a) for a in small]

    views = [big[0][0].T, _pad_heads(big[1][0], HEAD_DIM + ROPE_DIM)] + [w[0] for w in big[2:]]
    shards = [v.astype(BF16) for v in views]

    def with_own(gathered, mine):
        return [lax.dynamic_update_index_in_dim(g, s, chip, 0) for g, s in zip(gathered, mine)]

    def early_weights(gathered):
        g_in, g_uq, g_ukv = with_own(gathered, shards[:n_early])
        return g_in, join_cols(g_uq), join_cols(g_ukv)

    def late_weights(gathered):
        g_o, g_up, g_down = with_own(gathered, shards[n_early:])
        return g_o.reshape(-1, g_o.shape[2]), join_cols(g_up), g_down.reshape(-1, g_down.shape[2])

    reduction = GradReduction(core_id, chip)
    loss, grad_x, _, _, d_small = local_step(
        x[0], positions[0], loss_target[0], gains, early_weights, late_weights, gather_side(shards[:n_early]),
        (gather_side(shards[n_early:-1]), gather_side(shards[-1:])), reduction)
    halves = reduction.halves["early"] + reduction.halves["late"]
    others = reduction.others["early"] + reduction.others["late"]

    def rows8(vs):
        return jnp.concatenate([_pad_lanes(vec(a).astype(F32), 1024) for a in vs], axis=0)

    with_loss = [jnp.concatenate([d, loss], axis=1) if n == 1 else d for n, d in enumerate(d_small)]
    g_small8 = allreduce_small(rows8(with_loss))

    outs_big = []
    for n, (w, gm, go, m, v) in enumerate(zip(big, halves, others, big_m, big_v)):
        _, k, cols = w.shape
        if n == 0:
            outs = adamw_halves_t(w[0].T, gm, go, m[0].T, v[0].T, core, name="adamw_%d" % n)
            outs_big.append([o.T[None] for o in outs])
        else:
            if n == 1:
                gm, go = (_unpad_heads(gh, HEAD_DIM + ROPE_DIM) for gh in (gm, go))
            outs_big.append(adamw_halves(w, gm, go, m, v, core, name="adamw_%d" % n))
    d8, m8, v8 = adamw(rows8(small), g_small8, rows8(small_m), rows8(small_v), name="adamw_small")

    def unrows8(a8):
        return [a8[n, :s.size].reshape(s.shape) for n, s in enumerate(small)]

    loss_all = g_small8[1, N_HEADS]
    grads, deltas, new_m, new_v = [None] * 14, [None] * 14, [None] * 14, [None] * 14
    big_at = [1, 4, 6, 9, 11, 12]
    small_at = [0, 2, 3, 5, 7, 8, 10, 13]
    for n, at in enumerate(big_at):
        grads[at], deltas[at], new_m[at], new_v[at] = outs_big[n]
    for at, g, dd, mm_, vv in zip(small_at, unrows8(g_small8), unrows8(d8), unrows8(m8), unrows8(v8)):
        grads[at], deltas[at], new_m[at], new_v[at] = g, dd, mm_, vv
    return (loss_all, grad_x[None], *grads, *deltas, *new_m, *new_v)
```

```python
import jax
import jax.numpy as jnp
from jax import lax
from jax.experimental import pallas as pl
from jax.experimental.pallas import tpu as pltpu

F32 = jnp.float32
BF16 = jnp.bfloat16
MESH = pl.DeviceIdType.MESH

EPS = 1e-6
ROPE_THETA = 10000.0
N_HEADS = 8
PAIRS = N_HEADS // 2
HEAD_DIM = 64
ROPE_DIM = 32
LANES = 128
Q_RANK = 384
KV_RANK = 256
N_CHIPS = 4
ADAM_LR, ADAM_B1, ADAM_B2, ADAM_EPS, ADAM_WD, ADAM_STEP = 0.001, 0.9, 0.999, 1e-08, 0.01, 10
VMEM_LIMIT = 48 * 1024 * 1024
LOG2E = 1.4426950408889634
LN2 = 0.6931471805599453
NN = (((1,), (0,)), ((), ()))
NT = (((1,), (1,)), ((), ()))
TN = (((0,), (0,)), ((), ()))


def _params(sem=None):
    return pltpu.CompilerParams(dimension_semantics=sem, vmem_limit_bytes=VMEM_LIMIT)


def _fit(block, dim):
    if dim <= block:
        return dim
    return next(b for b in range(block - block % LANES, 0, -LANES) if dim % b == 0)


def _row_block(rows):
    return next(b for b in (256, 128, 64, 32, 16, 8) if rows % b == 0)


def gridded(body, *, name, grid, in_specs, out_specs, out_shape, ins, semantics, side=None):
    if side is None:
        return pl.pallas_call(body, name=name, grid=grid, in_specs=in_specs, out_specs=out_specs, out_shape=out_shape,
                              compiler_params=_params(semantics))(*ins)
    n_in, n_out = len(in_specs), len(out_specs)
    steps = 1
    for extent in grid:
        steps *= extent

    def riding(*refs):
        main_in, s_in, main_out, s_out, sems = _split_refs(refs, [n_in, len(side.ins), n_out, len(side.out_shapes), 2])
        step = 0
        for axis, extent in enumerate(grid):
            step = step * extent + pl.program_id(axis)

        @pl.when(step == 0)
        def _():
            side.first(s_in, s_out, *sems)

        body(*main_in, *main_out)

        @pl.when(step == steps - 1)
        def _():
            if side.mid is not None:
                side.mid(s_in, s_out, *sems)
            side.last(s_in, s_out, *sems)

    s_in_specs, s_out_specs = side.specs()
    return pl.pallas_call(
        riding, name=name, grid=grid, in_specs=list(in_specs) + s_in_specs, out_specs=list(out_specs) + s_out_specs,
        out_shape=list(out_shape) + side.out_shapes, scratch_shapes=side.sems(),
        compiler_params=_params(("arbitrary",) * len(grid)))(*ins, *side.ins)


def rmsnorm(x, g, *, out_dtype, name, bt=512, side=None):
    t, d = x.shape
    bt = min(bt, t)

    def body(x_ref, g_ref, o_ref):
        xv = x_ref[...].astype(F32)
        r = lax.rsqrt(jnp.mean(xv * xv, axis=-1, keepdims=True) + EPS)
        o_ref[...] = (xv * r * g_ref[...]).astype(o_ref.dtype)

    return gridded(
        body, name=name, grid=(t // bt,),
        in_specs=[pl.BlockSpec((bt, d), lambda i: (i, 0)), pl.BlockSpec((1, d), lambda i: (0, 0))],
        out_specs=[pl.BlockSpec((bt, d), lambda i: (i, 0))],
        out_shape=[jax.ShapeDtypeStruct((t, d), out_dtype)],
        ins=[x, g], semantics=("parallel",), side=side)


def mm(a, b, *, trans_b=False, a_pro=None, extras=(), vecs=(), epilogue=None, out_dtypes, n_sums=0, t_blk=None, name,
       bm=1024, bn=1024, side=None):
    a_list = list(a) if isinstance(a, (list, tuple)) else [a]
    b_list = list(b) if isinstance(b, (list, tuple)) else [b]
    m = a_list[0].shape[0]
    n = b_list[0].shape[0] if trans_b else b_list[0].shape[1]
    ks = [x.shape[1] for x in a_list]
    if sum(ks) > 2048:
        bm = bm // 2
    bm, bn = _fit(bm, m), _fit(bn, n)
    assert n_sums == 0 or bn == n
    n_ab, n_ex, n_vec, n_out = len(a_list), len(extras), len(vecs), len(out_dtypes)
    n_t = 0 if t_blk is None else 1

    def body(*refs):
        a_refs, b_refs, ex, vs, outs, t_outs, sums = _split_refs(refs, [n_ab, n_ab, n_ex, n_vec, n_out, n_t, n_sums])
        acc = None
        for a_ref, b_ref in zip(a_refs, b_refs):
            a_tile = a_ref[...] if a_pro is None else a_pro(a_ref[...])
            part = lax.dot_general(a_tile, b_ref[...], NT if trans_b else NN, preferred_element_type=F32)
            acc = part if acc is None else acc + part
        res = epilogue(acc, *[e[...] for e in ex], *[v[...] for v in vs]) if epilogue is not None else (acc,)
        for o, r in zip(outs, res[:n_out]):
            o[...] = r.astype(o.dtype)
        for t_ref in t_outs:
            for u in range(bm // t_blk):
                t_ref[u] = res[0][u * t_blk:(u + 1) * t_blk, :].T.astype(t_ref.dtype)
        if n_sums:
            @pl.when(pl.program_id(0) == 0)
            def _():
                for s_ref in sums:
                    s_ref[...] = jnp.zeros_like(s_ref)

            for s_ref, r in zip(sums, res[n_out:]):
                s_ref[...] += r

    tile = pl.BlockSpec((bm, bn), lambda i, j: (i, j))
    vec = pl.BlockSpec((1, bn), lambda i, j: (0, j))
    t_specs, t_shapes = [], []
    if t_blk is not None:
        t_specs = [pl.BlockSpec((bm // t_blk, bn, t_blk), lambda i, j: (i, j, 0))]
        t_shapes = [jax.ShapeDtypeStruct((m // t_blk, n, t_blk), out_dtypes[0])]
    a_specs = [pl.BlockSpec((bm, k), lambda i, j: (i, 0)) for k in ks]
    b_specs = [pl.BlockSpec((bn, k), lambda i, j: (j, 0)) if trans_b else pl.BlockSpec((k, bn), lambda i, j: (0, j)) for k in ks]
    return gridded(
        body, name=name, grid=(m // bm, n // bn),
        in_specs=a_specs + b_specs + [tile] * n_ex + [vec] * n_vec,
        out_specs=[tile] * n_out + t_specs + [vec] * n_sums,
        out_shape=[jax.ShapeDtypeStruct((m, n), dt) for dt in out_dtypes] + t_shapes + [jax.ShapeDtypeStruct((1, n), F32)] * n_sums,
        ins=[*a_list, *b_list, *extras, *vecs],
        semantics=("arbitrary", "arbitrary") if n_sums else ("parallel", "parallel"), side=side)


def mm_tn(a, b, *, a_pro=None, name, col_shards=1, bf16_copy=False, bk=1024, bn=1024, bt=1024):
    t, k = a.shape
    n = b.shape[1]
    ns = n // col_shards
    bk, bn, bt = _fit(bk, k), _fit(bn, ns), _fit(bt, t)
    per = ns // bn
    last = t // bt - 1

    def body(a_ref, b_ref, o_ref, *copy_ref):
        @pl.when(pl.program_id(2) == 0)
        def _():
            o_ref[...] = jnp.zeros_like(o_ref)

        a_tile = a_ref[...] if a_pro is None else a_pro(a_ref[...])
        o_ref[...] += lax.dot_general(a_tile, b_ref[...], TN, preferred_element_type=F32)
        if bf16_copy:
            @pl.when(pl.program_id(2) == last)
            def _():
                copy_ref[0][...] = o_ref[...].astype(BF16)

    if col_shards == 1:
        out_spec = pl.BlockSpec((bk, bn), lambda i, j, s: (i, j))
        shape = (k, n)
    else:
        out_spec = pl.BlockSpec((None, bk, bn), lambda i, j, s: (j // per, i, j % per))
        shape = (col_shards, k, ns)
    out_specs, out_shape = out_spec, jax.ShapeDtypeStruct(shape, F32)
    if bf16_copy:
        out_specs, out_shape = [out_spec, out_spec], [out_shape, jax.ShapeDtypeStruct(shape, BF16)]
    return pl.pallas_call(
        body, name=name, grid=(k // bk, n // bn, t // bt),
        in_specs=[pl.BlockSpec((bt, bk), lambda i, j, s: (s, i)), pl.BlockSpec((bt, bn), lambda i, j, s: (s, j))],
        out_specs=out_specs, out_shape=out_shape,
        compiler_params=_params(("parallel", "parallel", "arbitrary")),
    )(a, b)


def _split3(x):
    hi = x.astype(BF16)
    r1 = x - hi.astype(F32)
    mid = r1.astype(BF16)
    lo = (r1 - mid.astype(F32)).astype(BF16)
    return hi, mid, lo


def cumsum_rows(x, *, reverse, name, bc=512):
    t, d = x.shape
    bc = min(bc, t)
    nb = t // bc

    def body(x_ref, o_ref, carry):
        @pl.when(pl.program_id(0) == 0)
        def _():
            carry[...] = jnp.zeros_like(carry)

        r = lax.broadcasted_iota(jnp.int32, (bc, bc), 0)
        c = lax.broadcasted_iota(jnp.int32, (bc, bc), 1)
        tri = jnp.where((r <= c) if reverse else (r >= c), 1.0, 0.0).astype(BF16)
        hi, mid, lo = _split3(x_ref[...])
        s = (lax.dot_general(tri, hi, NN, preferred_element_type=F32)
             + lax.dot_general(tri, mid, NN, preferred_element_type=F32)
             + lax.dot_general(tri, lo, NN, preferred_element_type=F32)) + carry[0:1, :]
        o_ref[...] = s
        carry[0:1, :] = s[0:1, :] if reverse else s[bc - 1:bc, :]

    imap = (lambda i: (nb - 1 - i, 0)) if reverse else (lambda i: (i, 0))
    return pl.pallas_call(
        body, name=name, grid=(nb,),
        in_specs=[pl.BlockSpec((bc, d), imap)], out_specs=pl.BlockSpec((bc, d), imap),
        out_shape=jax.ShapeDtypeStruct((t, d), F32),
        scratch_shapes=[pltpu.VMEM((8, d), F32)],
        compiler_params=_params(("arbitrary",)),
    )(x)


def _rope(x, c, a, b):
    return x * c + pltpu.roll(x, LANES - ROPE_DIM // 2, 1) * a + pltpu.roll(x, ROPE_DIM // 2, 1) * b


def _rope_bwd(d, c, a, b):
    return d * c + pltpu.roll(d * a, ROPE_DIM // 2, 1) + pltpu.roll(d * b, LANES - ROPE_DIM // 2, 1)


S_CQ, S_CKV, S_KR, S_F, S_END = 0, Q_RANK, Q_RANK + KV_RANK, Q_RANK + KV_RANK + LANES, 1024
HW = N_HEADS * LANES
FW = N_HEADS * HEAD_DIM


def mla_prep(small, g_q, g_kv, w_uq, w_ukv, tab_c, tab_a, tab_b, b_f, *, name, bt=512):
    t = small.shape[0]
    bt = min(bt, t)

    def body(s_ref, gq_ref, gkv_ref, wq_ref, wkv_ref, c_ref, a_ref, b_ref, bf_ref,
             mq_ref, mk_ref, mv_ref, lf_ref, cqn_ref, ckvn_ref, mqt_ref, mvt_ref):
        cq = s_ref[:, S_CQ:S_CKV]
        rq = lax.rsqrt(jnp.mean(cq * cq, axis=-1, keepdims=True) + EPS)
        cqn = (cq * rq * gq_ref[...]).astype(BF16)
        ckv = s_ref[:, S_CKV:S_KR]
        rkv = lax.rsqrt(jnp.mean(ckv * ckv, axis=-1, keepdims=True) + EPS)
        ckvn = (ckv * rkv * gkv_ref[...]).astype(BF16)
        cqn_ref[...] = cqn
        ckvn_ref[...] = ckvn
        tc, ta, tb = c_ref[...], a_ref[...], b_ref[...]
        q = jnp.dot(cqn, wq_ref[...], preferred_element_type=F32)
        kv = jnp.dot(ckvn, wkv_ref[...], preferred_element_type=F32)
        kr = _rope(s_ref[:, S_KR:S_F], tc, ta, tb)
        for h in range(N_HEADS):
            sl = slice(h * LANES, (h + 1) * LANES)
            roped = _rope(q[:, sl], tc, ta, tb)
            mq_ref[:, sl] = roped.astype(BF16)
            mqt_ref[0, sl, :] = roped.T.astype(BF16)
            mk_ref[:, sl] = (kv[:, sl] + kr).astype(BF16)
        mv_ref[...] = kv[:, HW:].astype(BF16)
        mvt_ref[0] = kv[:, HW:].T.astype(BF16)
        z = s_ref[:, S_F:S_END - LANES] + bf_ref[...]
        lf_ref[...] = jnp.minimum(z, 0.0) - jnp.log(1.0 + jnp.exp(-jnp.abs(z)))

    def row(w):
        return pl.BlockSpec((bt, w), lambda i: (i, 0))

    def full(arr):
        return pl.BlockSpec(arr.shape, lambda i: (0, 0))

    return pl.pallas_call(
        body, name=name, grid=(t // bt,),
        in_specs=[row(S_END), full(g_q), full(g_kv), full(w_uq), full(w_ukv), row(LANES), row(LANES), row(LANES), full(b_f)],
        out_specs=[row(HW), row(HW), row(FW), row(LANES), row(Q_RANK), row(KV_RANK),
                   pl.BlockSpec((1, HW, bt), lambda i: (i, 0, 0)), pl.BlockSpec((1, FW, bt), lambda i: (i, 0, 0))],
        out_shape=[jax.ShapeDtypeStruct((t, HW), BF16)] * 2 + [jax.ShapeDtypeStruct((t, FW), BF16), jax.ShapeDtypeStruct((t, LANES), F32),
                   jax.ShapeDtypeStruct((t, Q_RANK), BF16), jax.ShapeDtypeStruct((t, KV_RANK), BF16),
                   jax.ShapeDtypeStruct((t // bt, HW, bt), BF16), jax.ShapeDtypeStruct((t // bt, FW, bt), BF16)],
        compiler_params=_params(("parallel",)),
    )(small, g_q, g_kv, w_uq, w_ukv, tab_c, tab_a, tab_b, b_f)


def mla_prep_bwd(dmq, dmk, dmv, dlf, small, g_q, g_kv, w_uq, w_ukv, tab_c, tab_a, tab_b, b_f, *, name, bt=512):
    t = small.shape[0]
    bt = min(bt, t)

    def body(dmq_ref, dmk_ref, dmv_ref, dlf_ref, s_ref, gq_ref, gkv_ref, wq_ref, wkv_ref, c_ref, a_ref, b_ref, bf_ref,
             ds_ref, dq_ref, dkv_ref, dgq_ref, dgkv_ref, db_ref):
        tc, ta, tb = c_ref[...], a_ref[...], b_ref[...]
        lane = lax.broadcasted_iota(jnp.int32, (1, LANES), 1)
        dkr = jnp.zeros((bt, LANES), F32)
        for h in range(N_HEADS):
            sl = slice(h * LANES, (h + 1) * LANES)
            dq_ref[:, sl] = _rope_bwd(dmq_ref[:, sl], tc, ta, tb).astype(BF16)
            dkr = dkr + dmk_ref[:, sl]
        dkv_ref[:, :HW] = dmk_ref[...].astype(BF16)
        dkv_ref[:, HW:] = dmv_ref[...].astype(BF16)
        in_rope = (lane >= HEAD_DIM) & (lane < HEAD_DIM + ROPE_DIM)
        ds_ref[:, S_KR:S_F] = jnp.where(in_rope, _rope_bwd(dkr, tc, ta, tb), 0.0).astype(BF16)

        def norm_bwd(raw, g_ref, dn, dg_ref):
            r = lax.rsqrt(jnp.mean(raw * raw, axis=-1, keepdims=True) + EPS)
            u = dn * g_ref[...]
            dot = jnp.mean(u * raw, axis=-1, keepdims=True)
            dg_ref[...] += jnp.sum(dn * (raw * r), axis=0, keepdims=True)
            return r * u - raw * (r * r * r * dot)

        @pl.when(pl.program_id(0) == 0)
        def _():
            dgq_ref[...] = jnp.zeros_like(dgq_ref)
            dgkv_ref[...] = jnp.zeros_like(dgkv_ref)
            db_ref[...] = jnp.zeros_like(db_ref)

        dcqn = lax.dot_general(dq_ref[...], wq_ref[...], NT, preferred_element_type=F32)
        ds_ref[:, S_CQ:S_CKV] = norm_bwd(s_ref[:, S_CQ:S_CKV], gq_ref, dcqn, dgq_ref).astype(BF16)
        dckvn = lax.dot_general(dkv_ref[...], wkv_ref[...], NT, preferred_element_type=F32)
        ds_ref[:, S_CKV:S_KR] = norm_bwd(s_ref[:, S_CKV:S_KR], gkv_ref, dckvn, dgkv_ref).astype(BF16)
        z = s_ref[:, S_F:S_END - LANES] + bf_ref[...]
        dz = jnp.where(lane < N_HEADS, dlf_ref[...] / (1.0 + jnp.exp(z)), 0.0)
        db_ref[...] += jnp.sum(dz, axis=0, keepdims=True)
        ds_ref[:, S_F:S_END - LANES] = dz.astype(BF16)
        ds_ref[:, S_END - LANES:] = jnp.zeros((bt, LANES), BF16)

    def row(w):
        return pl.BlockSpec((bt, w), lambda i: (i, 0))

    def full(arr):
        return pl.BlockSpec(arr.shape, lambda i: (0, 0))

    def vec(w):
        return pl.BlockSpec((1, w), lambda i: (0, 0))

    return pl.pallas_call(
        body, name=name, grid=(t // bt,),
        in_specs=[row(HW), row(HW), row(FW), row(LANES), row(S_END), full(g_q), full(g_kv), full(w_uq), full(w_ukv),
                  row(LANES), row(LANES), row(LANES), full(b_f)],
        out_specs=[row(S_END), row(HW), row(HW + FW), vec(Q_RANK), vec(KV_RANK), vec(LANES)],
        out_shape=[jax.ShapeDtypeStruct((t, S_END), BF16), jax.ShapeDtypeStruct((t, HW), BF16),
                   jax.ShapeDtypeStruct((t, HW + FW), BF16), jax.ShapeDtypeStruct((1, Q_RANK), F32),
                   jax.ShapeDtypeStruct((1, KV_RANK), F32), jax.ShapeDtypeStruct((1, LANES), F32)],
        compiler_params=_params(("arbitrary",)),
    )(dmq, dmk, dmv, dlf, small, g_q, g_kv, w_uq, w_ukv, tab_c, tab_a, tab_b, b_f)


class Side:
    def __init__(self, ins, out_shapes, n_sems, first, last, mid=None):
        self.ins, self.out_shapes, self.n_sems = list(ins), list(out_shapes), n_sems
        self.first, self.mid, self.last = first, mid, last

    def specs(self):
        return [ANY] * len(self.ins), [ANY] * len(self.out_shapes)

    def sems(self):
        return [pltpu.SemaphoreType.DMA((self.n_sems,)), pltpu.SemaphoreType.DMA((self.n_sems,))]


def _lane():
    return lax.broadcasted_iota(jnp.int32, (1, LANES), 1)


def _halves(x):
    zero = jnp.zeros_like(x)
    return [jnp.where(_lane() < HEAD_DIM, x, zero), jnp.where(_lane() >= HEAD_DIM, x, zero)]


def _groups(x):
    return [x[:, :LANES], x[:, LANES:]]


def _pick_row(tile, h):
    row = lax.broadcasted_iota(jnp.int32, (tile.shape[0], 1), 0)
    return jnp.sum(jnp.where(row == h, tile, 0.0), axis=0, keepdims=True)


def _pick_lane(tile, h):
    return jnp.sum(jnp.where(_lane() == h, tile, 0.0), axis=1, keepdims=True)


def _row_halves(x):
    row = lax.broadcasted_iota(jnp.int32, (LANES, 1), 0)
    zero = jnp.zeros_like(x)
    return [jnp.where(row < HEAD_DIM, x, zero), jnp.where(row >= HEAD_DIM, x, zero)]


def _below_diagonal(s, lead=0):
    r = lax.broadcasted_iota(jnp.int32, s.shape, 0)
    c = lax.broadcasted_iota(jnp.int32, s.shape, 1)
    return jnp.where(c <= r + lead, s, -jnp.inf)


def _above_diagonal(s):
    r = lax.broadcasted_iota(jnp.int32, s.shape, 0)
    c = lax.broadcasted_iota(jnp.int32, s.shape, 1)
    return jnp.where(r <= c, s, -jnp.inf)


def _split_refs(refs, counts):
    out, at = [], 0
    for n in counts:
        out.append(refs[at:at + n])
        at += n
    return out


def flash_fwd(qt_arr, k_arr, vt_arr, f_cum, *, qoff, koff, voff, pair, scale, name, blk=512, side=None):
    t = k_arr.shape[0]
    tblk = qt_arr.shape[2]
    blk = max(min(blk, t), tblk)
    sub = blk // tblk
    nb = t // blk
    w = LANES if pair else 2 * LANES
    has_bias = f_cum is not None
    ins = [qt_arr, k_arr, vt_arr] + ([f_cum] if has_bias else [])

    def wide(ref, first):
        parts = [ref[first + u] for u in range(sub)]
        return parts[0] if sub == 1 else jnp.concatenate(parts, axis=1)
    s_ins, s_outs = (side.ins, side.out_shapes) if side else ([], [])

    def body(*refs):
        main, si, outs, so, sems = _split_refs(refs, [len(ins), len(s_ins), 2, len(s_outs), 2 if side else 0])
        qt_ref, k_ref, vt_ref = main[:3]
        f_ref = main[3] if has_bias else None
        o_ref, st_ref = outs
        g, i = pl.program_id(0), pl.program_id(1)
        step_id = g * nb + i
        if side:
            @pl.when(step_id == 0)
            def _():
                side.first(si, so, *sems)

            if side.mid is not None:
                @pl.when(step_id == (3 * PAIRS * nb) // 4)
                def _():
                    side.mid(si, so, *sems)

        qt = (wide(qt_ref, 0).astype(F32) * (scale * LOG2E)).astype(BF16)
        qts = _row_halves(qt) if pair else [qt[:LANES], qt[LANES:]]

        def with_ones(vt_rows):
            return jnp.concatenate([vt_rows, jnp.ones((ACC_ROWS - HEAD_DIM, vt_rows.shape[1]), BF16)], axis=0)

        def step(j, carry, diagonal):
            rows = pl.ds(pl.multiple_of(j * blk, blk), blk)
            kk = k_ref[rows, :]
            ks = [kk, kk] if pair else _groups(kk)
            vt = wide(vt_ref, sub * j)
            out = []
            for n in range(2):
                m, acc = carry[n]
                s = jnp.dot(ks[n], qts[n], preferred_element_type=F32)
                if has_bias:
                    s = s - LOG2E * _pick_lane(f_ref[rows, :], 2 * g + n)
                if diagonal:
                    s = _above_diagonal(s)
                m_new = jnp.maximum(m, jnp.max(s, axis=0, keepdims=True))
                p = jnp.exp2((s - m_new).astype(BF16))
                out.append((m_new, jnp.exp2(m - m_new) * acc
                            + jnp.dot(with_ones(vt[n * HEAD_DIM:(n + 1) * HEAD_DIM]), p, preferred_element_type=F32)))
            return tuple(out)

        def diagonal_in_halves(carry):
            h = tblk
            halves = [pl.ds(pl.multiple_of(i * blk, blk), h), pl.ds(pl.multiple_of(i * blk + h, h), h)]
            k_top, k_bot = ([x, x] if pair else _groups(x) for x in (k_ref[halves[0], :], k_ref[halves[1], :]))
            vt_top, vt_bot = vt_ref[sub * i], vt_ref[sub * i + 1]
            out = []
            for n in range(2):
                m, acc = carry[n]
                heads = slice(n * HEAD_DIM, (n + 1) * HEAD_DIM)
                s_top = jnp.dot(k_top[n], qts[n], preferred_element_type=F32)
                s_bot = jnp.dot(k_bot[n], qts[n][:, h:], preferred_element_type=F32)
                if has_bias:
                    s_top = s_top - LOG2E * _pick_lane(f_ref[halves[0], :], 2 * g + n)
                    s_bot = s_bot - LOG2E * _pick_lane(f_ref[halves[1], :], 2 * g + n)
                s_top, s_bot = _above_diagonal(s_top), _above_diagonal(s_bot)
                m_top = jnp.maximum(m, jnp.max(s_top, axis=0, keepdims=True))
                m_new = jnp.concatenate([m_top[:, :h], jnp.maximum(m_top[:, h:], jnp.max(s_bot, axis=0, keepdims=True))], axis=1)
                p_top = jnp.exp2((s_top - m_new).astype(BF16))
                p_bot = jnp.exp2((s_bot - m_new[:, h:]).astype(BF16))
                late = jnp.concatenate([jnp.zeros((ACC_ROWS, h), F32),
                                        jnp.dot(with_ones(vt_bot[heads]), p_bot, preferred_element_type=F32)], axis=1)
                out.append((m_new, jnp.exp2(m - m_new) * acc
                            + jnp.dot(with_ones(vt_top[heads]), p_top, preferred_element_type=F32) + late))
            return tuple(out)

        init = tuple((jnp.full((1, blk), -jnp.inf, F32), jnp.zeros((ACC_ROWS, blk), F32)) for _ in range(2))
        carry = lax.fori_loop(0, i, lambda j, c: step(j, c, False), init)
        (ma, acca), (mb, accb) = diagonal_in_halves(carry) if sub == 2 else step(i, carry, True)
        la, lb = acca[HEAD_DIM:HEAD_DIM + 1], accb[HEAD_DIM:HEAD_DIM + 1]
        o_ref[...] = jnp.concatenate([acca[:HEAD_DIM] / la, accb[:HEAD_DIM] / lb], axis=0).T
        row = lax.broadcasted_iota(jnp.int32, (LANES, 1), 0)
        st_ref[0] = jnp.where(row == 0, ma + jnp.log2(la), jnp.where(row == 1, mb + jnp.log2(lb), 0.0)).T
        if side:
            @pl.when(step_id == PAIRS * nb - 1)
            def _():
                side.last(si, so, *sems)

    in_specs = [pl.BlockSpec((sub, w, tblk), lambda g, i: (i, qoff + g, 0)), pl.BlockSpec((t, w), lambda g, i: (0, koff + g)),
                pl.BlockSpec((t // tblk, LANES, tblk), lambda g, i: (0, voff + g, 0))]
    if has_bias:
        in_specs.append(pl.BlockSpec((t, LANES), lambda g, i: (0, 0)))
    s_in_specs, s_out_specs = side.specs() if side else ([], [])
    return pl.pallas_call(
        body, name=name, grid=(PAIRS, nb), in_specs=in_specs + s_in_specs,
        out_specs=[pl.BlockSpec((blk, LANES), lambda g, i: (i, g)), pl.BlockSpec((1, blk, LANES), lambda g, i: (g, i, 0))]
        + s_out_specs,
        out_shape=[jax.ShapeDtypeStruct((t, PAIRS * LANES), F32), jax.ShapeDtypeStruct((PAIRS, t, LANES), F32)] + list(s_outs),
        scratch_shapes=side.sems() if side else [],
        compiler_params=_params(("arbitrary", "arbitrary")),
    )(*ins, *s_ins)


def mix_norm(fo, mo, g_fo, g_mo, *, name, bt=512):
    t, d = fo.shape
    bt = min(bt, t)

    def body(fo_ref, mo_ref, gf_ref, gm_ref, o_ref):
        for n, (x_ref, g_ref) in enumerate(((fo_ref, gf_ref), (mo_ref, gm_ref))):
            xv = x_ref[...]
            r = lax.rsqrt(jnp.mean(xv * xv, axis=-1, keepdims=True) + EPS)
            o_ref[:, n * d:(n + 1) * d] = (xv * r * g_ref[...]).astype(BF16)

    row = pl.BlockSpec((bt, d), lambda i: (i, 0))
    vec = pl.BlockSpec((1, d), lambda i: (0, 0))
    return pl.pallas_call(
        body, name=name, grid=(t // bt,), in_specs=[row, row, vec, vec],
        out_specs=pl.BlockSpec((bt, 2 * d), lambda i: (i, 0)),
        out_shape=jax.ShapeDtypeStruct((t, 2 * d), BF16),
        compiler_params=_params(("parallel",)),
    )(fo, mo, g_fo, g_mo)


def mix_norm_bwd(dx1b, w_o, fo, mo, g_fo, g_mo, st_f, st_m, *, name, bt=512, side=None):
    t, d = fo.shape
    bt = min(bt, t)

    def body(dx_in_ref, w_ref, fo_ref, mo_ref, gf_ref, gm_ref, sf_ref, sm_ref, dfo_ref, dmo_ref, dgf_ref, dgm_ref,
             sfo_ref, smo_ref, dfot_ref, dmot_ref):
        @pl.when(pl.program_id(0) == 0)
        def _():
            dgf_ref[...] = jnp.zeros_like(dgf_ref)
            dgm_ref[...] = jnp.zeros_like(dgm_ref)

        dmixed = lax.dot_general(dx_in_ref[...], w_ref[...], NT, preferred_element_type=F32)
        groups = ((fo_ref, gf_ref, dfo_ref, dgf_ref, sf_ref, sfo_ref, dfot_ref),
                  (mo_ref, gm_ref, dmo_ref, dgm_ref, sm_ref, smo_ref, dmot_ref))
        for n, (x_ref, g_ref, dx_ref, dg_ref, st_ref, sto_ref, dxt_ref) in enumerate(groups):
            xv = x_ref[...]
            dhv = dmixed[:, n * d:(n + 1) * d]
            r = lax.rsqrt(jnp.mean(xv * xv, axis=-1, keepdims=True) + EPS)
            u = dhv * g_ref[...]
            dxf = r * u - xv * (r * r * r * jnp.mean(u * xv, axis=-1, keepdims=True))
            dxb = dxf.astype(BF16)
            dx_ref[...] = dxb
            dxt_ref[0] = dxf.T.astype(BF16)
            dg_ref[...] += jnp.sum(dhv * (xv * r), axis=0, keepdims=True)
            prod = xv * dxb.astype(F32)
            for g in range(PAIRS):
                grp = prod[:, g * LANES:(g + 1) * LANES]
                da = jnp.sum(jnp.where(_lane() < HEAD_DIM, grp, 0.0), axis=1, keepdims=True)
                db = jnp.sum(jnp.where(_lane() >= HEAD_DIM, grp, 0.0), axis=1, keepdims=True)
                sto_ref[g] = jnp.where(_lane() == 2, da, jnp.where(_lane() == 3, db, st_ref[g]))

    row = pl.BlockSpec((bt, d), lambda i: (i, 0))
    vec = pl.BlockSpec((1, d), lambda i: (0, 0))
    stat = pl.BlockSpec((PAIRS, bt, LANES), lambda i: (0, i, 0))
    return gridded(
        body, name=name, grid=(t // bt,),
        in_specs=[pl.BlockSpec((bt, dx1b.shape[1]), lambda i: (i, 0)), pl.BlockSpec(w_o.shape, lambda i: (0, 0)),
                  row, row, vec, vec, stat, stat],
        out_specs=[row, row, vec, vec, stat, stat] + [pl.BlockSpec((1, d, bt), lambda i: (i, 0, 0))] * 2,
        out_shape=[jax.ShapeDtypeStruct((t, d), BF16)] * 2 + [jax.ShapeDtypeStruct((1, d), F32)] * 2
        + [jax.ShapeDtypeStruct(st_f.shape, F32)] * 2 + [jax.ShapeDtypeStruct((t // bt, d, bt), BF16)] * 2,
        ins=[dx1b, w_o, fo, mo, g_fo, g_mo, st_f, st_m], semantics=("arbitrary",), side=side)


def flash_bwd(q_arr, qt_arr, k_arr, v_arr, do_arr, dot_arr, st, f_blocks, *, qoff, koff, voff, pair, scale, name, qblk=1024,
              side=None):
    t = q_arr.shape[0]
    blk = qt_arr.shape[2]
    qblk = max(min(qblk, t), blk)
    sub = qblk // blk
    nb, nbq = t // blk, t // qblk
    w = LANES if pair else 2 * LANES
    hw = w // 2
    has_bias = f_blocks is not None
    split = _halves if pair else _groups
    ins = [q_arr, qt_arr, k_arr, v_arr, do_arr, dot_arr, st] + ([f_blocks] if has_bias else [])
    n_out = 4 if has_bias else 3
    s_ins, s_outs = (side.ins, side.out_shapes) if side else ([], [])

    def wide(ref, first):
        parts = [ref[first + u] for u in range(sub)]
        return parts[0] if sub == 1 else jnp.concatenate(parts, axis=1)

    def body(*refs):
        main, si, outs, so, sems = _split_refs(refs, [len(ins), len(s_ins), n_out, len(s_outs), 2 if side else 0])
        q_ref, qt_ref, k_ref, v_ref, do_ref, dot_ref, st_ref = main[:7]
        dq_ref, dk_ref, dv_ref = outs[:3]
        g, j = pl.program_id(0), pl.program_id(1)
        step_id = g * nb + j
        if side:
            @pl.when(step_id == 0)
            def _():
                side.first(si, so, *sems)

        @pl.when(j == 0)
        def _():
            dq_ref[...] = jnp.zeros_like(dq_ref)

        kk, vv = k_ref[...], v_ref[...]
        ks = [kk, kk] if pair else _groups(kk)
        if has_bias:
            f_ref, df_ref = main[7], outs[3]
            fk = [LOG2E * _pick_row(f_ref[0], 2 * g + n) for n in range(2)]

            @pl.when(step_id == 0)
            def _():
                df_ref[...] = jnp.zeros_like(df_ref)

        def step(i, carry, diagonal):
            rows = pl.ds(pl.multiple_of(i * qblk, qblk), qblk)
            qs = split((q_ref[rows, :].astype(F32) * (scale * LOG2E)).astype(BF16))
            qt = (wide(qt_ref, sub * i).astype(F32) * (scale * LOG2E)).astype(BF16)
            dos = [h.astype(BF16) for h in _halves(do_ref[rows, :].astype(F32))]
            dot = wide(dot_ref, sub * i)
            stats = st_ref[0, rows, :]
            new, dqs, row_sums = [], [], []
            for n in range(2):
                dkt, dvt, dfk = carry[n]
                s = lax.dot_general(qs[n], ks[n], NT, preferred_element_type=F32)
                if has_bias:
                    s = s - fk[n]
                if diagonal:
                    s = _below_diagonal(s, i * qblk - j * blk)
                p = jnp.exp2(s - stats[:, n:n + 1])
                dp = lax.dot_general(dos[n], vv, NT, preferred_element_type=F32)
                ds = p * (dp - stats[:, 2 + n:3 + n])
                dsb = ds.astype(BF16)
                dvt = dvt + jnp.dot(dot[n * HEAD_DIM:(n + 1) * HEAD_DIM], p.astype(BF16), preferred_element_type=F32)
                dkt = dkt + jnp.dot(qt[n * hw:(n + 1) * hw], dsb, preferred_element_type=F32)
                dqs.append(jnp.dot(dsb, ks[n], preferred_element_type=F32))
                if has_bias:
                    dfk = dfk - jnp.sum(ds, axis=0, keepdims=True)
                    row_sums.append(jnp.sum(ds, axis=1, keepdims=True))
                new.append((dkt, dvt, dfk))
            dq = (jnp.where(_lane() < HEAD_DIM, dqs[0], dqs[1]) if pair else jnp.concatenate(dqs, axis=1)) * scale
            if has_bias:
                df_ref[rows, :] += jnp.where(_lane() == 2 * g, row_sums[0], jnp.where(_lane() == 2 * g + 1, row_sums[1], 0.0))
            dq_ref[rows, :] += dq
            return tuple(new)

        init = tuple((jnp.zeros((hw, blk), F32), jnp.zeros((HEAD_DIM, blk), F32), jnp.zeros((1, blk), F32)) for _ in range(2))
        first = j // sub
        carry = step(first, init, True)
        (dka, dva, dfa), (dkb, dvb, dfb) = lax.fori_loop(first + 1, nbq, lambda i, c: step(i, c, False), carry)
        dk_ref[...] = jnp.concatenate([dka, dkb], axis=0).T * LN2
        dv_ref[...] = jnp.concatenate([dva, dvb], axis=0).T
        if has_bias:
            row = lax.broadcasted_iota(jnp.int32, (LANES, 1), 0)
            by_head = jnp.where(row == 2 * g, dfa, jnp.where(row == 2 * g + 1, dfb, 0.0))
            df_ref[pl.ds(pl.multiple_of(j * blk, blk), blk), :] += by_head.T
        if side:
            @pl.when(step_id == PAIRS * nb - 1)
            def _():
                side.last(si, so, *sems)

    in_specs = [pl.BlockSpec((t, w), lambda g, j: (0, qoff + g)), pl.BlockSpec((nb, w, blk), lambda g, j: (0, qoff + g, 0)),
                pl.BlockSpec((blk, w), lambda g, j: (j, koff + g)), pl.BlockSpec((blk, LANES), lambda g, j: (j, voff + g)),
                pl.BlockSpec((t, LANES), lambda g, j: (0, g)), pl.BlockSpec((nb, LANES, blk), lambda g, j: (0, g, 0)),
                pl.BlockSpec((1, t, LANES), lambda g, j: (g, 0, 0))]
    out_specs = [pl.BlockSpec((t, w), lambda g, j: (0, g)), pl.BlockSpec((blk, w), lambda g, j: (j, g)),
                 pl.BlockSpec((blk, LANES), lambda g, j: (j, g))]
    out_shape = [jax.ShapeDtypeStruct((t, PAIRS * w), F32)] * 2 + [jax.ShapeDtypeStruct((t, PAIRS * LANES), F32)]
    if has_bias:
        in_specs.append(pl.BlockSpec((1, N_HEADS, blk), lambda g, j: (j, 0, 0)))
        out_specs.append(pl.BlockSpec((t, LANES), lambda g, j: (0, 0)))
        out_shape.append(jax.ShapeDtypeStruct((t, LANES), F32))
    s_in_specs, s_out_specs = side.specs() if side else ([], [])
    return pl.pallas_call(
        body, name=name, grid=(PAIRS, nb), in_specs=in_specs + s_in_specs, out_specs=out_specs + s_out_specs,
        out_shape=out_shape + list(s_outs), scratch_shapes=side.sems() if side else [],
        compiler_params=_params(("arbitrary", "arbitrary")),
    )(*ins, *s_ins)


def _adamw_math(w, g, m, v):
    nm = ADAM_B1 * m + (1.0 - ADAM_B1) * g
    nv = ADAM_B2 * v + (1.0 - ADAM_B2) * (g * g)
    m_hat = nm / (1.0 - ADAM_B1 ** ADAM_STEP)
    v_hat = nv / (1.0 - ADAM_B2 ** ADAM_STEP)
    return -ADAM_LR * (m_hat / (jnp.sqrt(v_hat) + ADAM_EPS) + ADAM_WD * w), nm, nv


def adamw(w, g, m, v, *, name):
    rws, cols = w.shape
    br = _row_block(rws)

    def body(w_ref, g_ref, m_ref, v_ref, d_ref, nm_ref, nv_ref):
        d_ref[...], nm_ref[...], nv_ref[...] = _adamw_math(w_ref[...], g_ref[...], m_ref[...], v_ref[...])

    blk = pl.BlockSpec((br, cols), lambda i: (i, 0))
    return pl.pallas_call(
        body, name=name, grid=(rws // br,), in_specs=[blk] * 4, out_specs=[blk] * 3,
        out_shape=[jax.ShapeDtypeStruct((rws, cols), F32)] * 3,
        compiler_params=_params(("parallel",)),
    )(w, g, m, v)


def adamw_halves(w, g_mine, g_other, m, v, core, *, name):
    _, k, n = w.shape
    br = _row_block(k // 2)
    nh = k // 2 // br

    def body(c_ref, w_ref, gm_ref, go_ref, m_ref, v_ref, g_out, d_ref, nm_ref, nv_ref):
        gv = jnp.where(pl.program_id(0) == c_ref[0], gm_ref[...], go_ref[...])
        g_out[0] = gv
        d_ref[0], nm_ref[0], nv_ref[0] = _adamw_math(w_ref[0], gv, m_ref[0], v_ref[0])

    full = pl.BlockSpec((1, br, n), lambda hb, i, c: (0, hb * nh + i, 0))
    half = pl.BlockSpec((br, n), lambda hb, i, c: (i, 0))
    return pl.pallas_call(
        body, name=name,
        grid_spec=pltpu.PrefetchScalarGridSpec(num_scalar_prefetch=1, grid=(2, nh), in_specs=[full, half, half, full, full],
                                               out_specs=[full] * 4),
        out_shape=[jax.ShapeDtypeStruct(w.shape, F32)] * 4,
        compiler_params=_params(("parallel", "parallel")),
    )(core, w, g_mine, g_other, m, v)


def adamw_halves_t(wt, gt_mine, gt_other, mt, vt, core, *, name, bc=128):
    n, k = wt.shape
    nh = k // 2 // bc

    def body(c_ref, w_ref, gm_ref, go_ref, m_ref, v_ref, g_out, d_ref, nm_ref, nv_ref):
        gv = jnp.where(pl.program_id(0) == c_ref[0], gm_ref[...], go_ref[...])
        g_out[...] = gv
        d_ref[...], nm_ref[...], nv_ref[...] = _adamw_math(w_ref[...], gv, m_ref[...], v_ref[...])

    full = pl.BlockSpec((n, bc), lambda hb, i, c: (0, hb * nh + i))
    half = pl.BlockSpec((n, bc), lambda hb, i, c: (0, i))
    return pl.pallas_call(
        body, name=name,
        grid_spec=pltpu.PrefetchScalarGridSpec(num_scalar_prefetch=1, grid=(2, nh), in_specs=[full, half, half, full, full],
                                               out_specs=[full] * 4),
        out_shape=[jax.ShapeDtypeStruct(wt.shape, F32)] * 4,
        compiler_params=_params(("parallel", "parallel")),
    )(core, wt, gt_mine, gt_other, mt, vt)


def add_pair(dw, recv, core, *, name):
    n4, k, n = dw.shape
    half = (1, k // 2, n) if split_axis(k) == 0 else (1, k, n // 2)
    mine = (lambda q, c: (q, c[0], 0)) if split_axis(k) == 0 else (lambda q, c: (q, 0, c[0]))

    def body(c_ref, a_ref, b_ref, o_ref):
        o_ref[...] = (a_ref[...] + b_ref[...].astype(F32)).astype(BF16)

    return pl.pallas_call(
        body, name=name,
        grid_spec=pltpu.PrefetchScalarGridSpec(
            num_scalar_prefetch=1, grid=(n4,),
            in_specs=[pl.BlockSpec(half, mine), pl.BlockSpec(half, lambda q, c: (q, 0, 0))],
            out_specs=pl.BlockSpec(half, lambda q, c: (q, 0, 0))),
        out_shape=jax.ShapeDtypeStruct((n4,) + half[1:], BF16),
        compiler_params=_params(("parallel",)),
    )(core, dw, recv)


def sum_chips(parts, *, name):
    n4, r, n = parts.shape
    if r % 16 == 0:
        br, bc = _row_block(r), n
    else:
        br, bc = r, LANES

    def body(p_ref, o_ref):
        acc = p_ref[0].astype(F32)
        for q in range(1, n4):
            acc = acc + p_ref[q].astype(F32)
        o_ref[...] = acc

    return pl.pallas_call(
        body, name=name, grid=(r // br, n // bc),
        in_specs=[pl.BlockSpec((n4, br, bc), lambda i, j: (0, i, j))], out_specs=pl.BlockSpec((br, bc), lambda i, j: (i, j)),
        out_shape=jax.ShapeDtypeStruct((r, n), F32),
        compiler_params=_params(("parallel", "parallel")),
    )(parts)


ANY = pl.BlockSpec(memory_space=pl.ANY)


def _place():
    x, y, c = lax.axis_index("x"), lax.axis_index("y"), lax.axis_index("c")
    chips = [(1 - x, y), (x, 1 - y), (1 - x, 1 - y)]
    return x, y, c, chips


def _copy(src, dst, send_sems, recv_sems, k, to):
    return pltpu.make_async_remote_copy(src_ref=src, dst_ref=dst, send_sem=send_sems.at[k], recv_sem=recv_sems.at[k],
                                        device_id=to, device_id_type=MESH)


def split_axis(rows):
    return 0 if rows % 32 == 0 else 1


def _half(ref, lead, hf):
    rows, cols = ref.shape[-2:]
    if split_axis(rows) == 0:
        at = (pl.ds(hf * (rows // 2), rows // 2), slice(None))
    else:
        at = (slice(None), pl.ds(hf * (cols // 2), cols // 2))
    return ref.at[at] if lead is None else ref.at[(lead,) + at]


def _gather_first(srcs, dsts, ssems, rsems):
    x, y, c, chips = _place()
    for ti, (s, d) in enumerate(zip(srcs, dsts)):
        for j, (cx, cy) in enumerate(chips):
            _copy(_half(s, None, c), _half(d, 2 * x + y, c), ssems, rsems, 3 * ti + j, (cx, cy, c)).start()


def _gather_mid(srcs, dsts, ssems, rsems):
    x, y, c, chips = _place()
    n1 = 3 * len(srcs)
    for ti, d in enumerate(dsts):
        for j, (cx, cy) in enumerate(chips):
            landed = _half(d, 2 * cx + cy, c)
            _copy(landed, landed, ssems, rsems, 3 * ti + j, (cx, cy, c)).wait_recv()
            _copy(landed, landed, ssems, rsems, n1 + 3 * ti + j, (x, y, 1 - c)).start()


def _gather_last(srcs, dsts, ssems, rsems):
    x, y, c, chips = _place()
    n1 = 3 * len(srcs)
    for ti, (s, d) in enumerate(zip(srcs, dsts)):
        for j, (cx, cy) in enumerate(chips):
            other = _half(d, 2 * cx + cy, 1 - c)
            _copy(other, other, ssems, rsems, n1 + 3 * ti + j, (x, y, 1 - c)).wait_recv()
        for j, (cx, cy) in enumerate(chips):
            mine = _half(s, None, c)
            _copy(mine, mine, ssems, rsems, 3 * ti + j, (cx, cy, c)).wait_send()
            _copy(mine, mine, ssems, rsems, n1 + 3 * ti + j, (x, y, 1 - c)).wait_send()


def gather_side(shards):
    return Side(shards, [jax.ShapeDtypeStruct((N_CHIPS,) + s.shape, s.dtype) for s in shards], 6 * len(shards),
                _gather_first, _gather_last, _gather_mid)


def _scatter_first(srcs, dsts, ssems, rsems):
    x, y, c, chips = _place()
    for ti, (s, d) in enumerate(zip(srcs, dsts)):
        for j, (cx, cy) in enumerate(chips):
            _copy(s.at[2 * cx + cy], d.at[2 * x + y], ssems, rsems, 3 * ti + j, (cx, cy, c)).start()


def _scatter_last(srcs, dsts, ssems, rsems):
    x, y, c, chips = _place()
    for ti, (s, d) in enumerate(zip(srcs, dsts)):
        for j, (cx, cy) in enumerate(chips):
            _copy(s.at[2 * cx + cy], d.at[2 * cx + cy], ssems, rsems, 3 * ti + j, (cx, cy, c)).wait_recv()
        for j, (cx, cy) in enumerate(chips):
            _copy(s.at[2 * cx + cy], d.at[2 * cx + cy], ssems, rsems, 3 * ti + j, (cx, cy, c)).wait_send()


def scatter_side(parts):
    return Side(parts, [jax.ShapeDtypeStruct(p.shape, p.dtype) for p in parts], 3 * len(parts), _scatter_first, _scatter_last)


def run_side(side, *, name):
    n_in, n_out = len(side.ins), len(side.out_shapes)

    def body(*refs):
        si, so, sems = _split_refs(refs, [n_in, n_out, 2])
        side.first(si, so, *sems)
        if side.mid is not None:
            side.mid(si, so, *sems)
        side.last(si, so, *sems)

    in_specs, out_specs = side.specs()
    return pl.pallas_call(body, name=name, in_specs=in_specs, out_specs=out_specs, out_shape=side.out_shapes,
                          scratch_shapes=side.sems())(*side.ins)


def _swap_first(srcs, dsts, ssems, rsems):
    x, y, c, _ = _place()
    for k, (s, d) in enumerate(zip(srcs, dsts)):
        _copy(s, d, ssems, rsems, k, (x, y, 1 - c)).start()


def _swap_last(srcs, dsts, ssems, rsems):
    x, y, c, _ = _place()
    for k, (s, d) in enumerate(zip(srcs, dsts)):
        _copy(s, d, ssems, rsems, k, (x, y, 1 - c)).wait()


def swap_side(xs):
    return Side(xs, [jax.ShapeDtypeStruct(a.shape, a.dtype) for a in xs], len(xs), _swap_first, _swap_last)


def _swap_halves(srcs, dsts, ssems, rsems):
    x, y, c, _ = _place()
    for k, (s, d) in enumerate(zip(srcs, dsts)):
        hk = s.shape[1] // 2
        yield _copy(s.at[:, pl.ds((1 - c) * hk, hk), :], d, ssems, rsems, k, (x, y, 1 - c))


def _swap_halves_first(srcs, dsts, ssems, rsems):
    for cp in _swap_halves(srcs, dsts, ssems, rsems):
        cp.start()


def _swap_halves_last(srcs, dsts, ssems, rsems):
    for cp in _swap_halves(srcs, dsts, ssems, rsems):
        cp.wait()


def swap_halves_side(xs):
    return Side(xs, [jax.ShapeDtypeStruct((a.shape[0], a.shape[1] // 2, a.shape[2]), a.dtype) for a in xs], len(xs),
                _swap_halves_first, _swap_halves_last)


def allreduce_small(s):
    n_dev = 8

    def body(s_ref, out_ref, buf, send_sems, recv_sems):
        x, y, c, _ = _place()
        me = 4 * x + 2 * y + c
        buf[me] = s_ref[...]
        sends = []
        for k in range(1, n_dev):
            px = 1 - x if k & 4 else x
            py = 1 - y if k & 2 else y
            pc = 1 - c if k & 1 else c
            cp = _copy(s_ref, buf.at[me], send_sems, recv_sems, k - 1, (px, py, pc))
            cp.start()
            sends.append((cp, 4 * px + 2 * py + pc))
        for k, (cp, peer) in enumerate(sends):
            _copy(s_ref, buf.at[peer], send_sems, recv_sems, k, (x, y, c)).wait_recv()
        for cp, _ in sends:
            cp.wait_send()
        acc = buf[0]
        for d in range(1, n_dev):
            acc = acc + buf[d]
        out_ref[...] = acc

    vm = pl.BlockSpec(memory_space=pltpu.VMEM)
    return pl.pallas_call(
        body, name="allreduce_small", in_specs=[vm], out_specs=vm,
        out_shape=jax.ShapeDtypeStruct(s.shape, F32),
        scratch_shapes=[pltpu.VMEM((n_dev,) + s.shape, F32), pltpu.SemaphoreType.DMA((n_dev - 1,)),
                        pltpu.SemaphoreType.DMA((n_dev - 1,))],
    )(s)


def join_cols(sm):
    n4, k, n = sm.shape
    return sm.transpose(1, 0, 2).reshape(k, n4 * n)


def split_cols(full):
    k, n = full.shape
    return full.reshape(k, N_CHIPS, n // N_CHIPS).transpose(1, 0, 2)


def _pad_heads(w, width):
    lead, heads = w.shape[:-1], w.shape[-1] // width
    w = w.reshape(lead + (heads, width))
    return jnp.pad(w, [(0, 0)] * len(lead) + [(0, 0), (0, LANES - width)]).reshape(lead + (heads * LANES,))


def _unpad_heads(w, width):
    lead, heads = w.shape[:-1], w.shape[-1] // LANES
    return w.reshape(lead + (heads, LANES))[..., :width].reshape(lead + (heads * width,))


O_F = 3 * FW
O_CQ = O_F + N_HEADS
O_CKV = O_CQ + Q_RANK
O_KR = O_CKV + KV_RANK
O_END = O_KR + ROPE_DIM


def _shard_rows(sm, a, b):
    r = sm.shape[1]
    out = []
    while a < b:
        q = a // r
        e = min(b, (q + 1) * r)
        out.append(sm[q, a - q * r:e - q * r])
        a = e
    return out


def split_w_in_t(w_sm):
    d = w_sm.shape[2]

    def z(n):
        return [jnp.zeros((n, d), w_sm.dtype)]

    small = (_shard_rows(w_sm, O_CQ, O_CKV) + _shard_rows(w_sm, O_CKV, O_KR) + z(HEAD_DIM) + _shard_rows(w_sm, O_KR, O_END)
             + z(LANES - HEAD_DIM - ROPE_DIM) + _shard_rows(w_sm, O_F, O_CQ) + z(LANES - N_HEADS) + z(LANES))
    return jnp.concatenate(_shard_rows(w_sm, 0, O_F), axis=0), jnp.concatenate(small, axis=0)


def join_w_in_t(d_qkv_t, d_small_t):
    kr = S_KR + HEAD_DIM
    segments = [(d_qkv_t, 0, 0, O_F), (d_small_t, S_F, O_F, N_HEADS), (d_small_t, S_CQ, O_CQ, Q_RANK),
                (d_small_t, S_CKV, O_CKV, KV_RANK), (d_small_t, kr, O_KR, ROPE_DIM)]
    r = O_END // N_CHIPS
    shards = []
    for q in range(N_CHIPS):
        pieces = []
        for src, s0, v0, n in segments:
            a, b = max(v0, q * r), min(v0 + n, (q + 1) * r)
            if a < b:
                pieces.append(src[s0 + a - v0:s0 + b - v0])
        shards.append(jnp.concatenate(pieces, axis=0))
    return jnp.stack(shards)


def rope_tables(pos):
    t = pos.shape[0]
    inv_freq = ROPE_THETA ** (-jnp.arange(0, ROPE_DIM, 2, dtype=F32) / ROPE_DIM)
    ang = pos.astype(F32)[:, None] * inv_freq
    cos, sin = jnp.cos(ang), jnp.sin(ang)
    half = ROPE_DIM // 2

    def z(n):
        return jnp.zeros((t, n), F32)

    tab_c = jnp.concatenate([jnp.ones((t, HEAD_DIM), F32), cos, cos, z(LANES - HEAD_DIM - ROPE_DIM)], axis=1)
    tab_a = jnp.concatenate([z(HEAD_DIM), -sin, z(half), z(LANES - HEAD_DIM - ROPE_DIM)], axis=1)
    tab_b = jnp.concatenate([z(HEAD_DIM), z(half), sin, z(LANES - HEAD_DIM - ROPE_DIM)], axis=1)
    return tab_c, tab_a, tab_b


def _pad_lanes(v, n):
    return jnp.pad(v, ((0, 0), (0, n - v.shape[1])))


ATTN_BLK = 512
ATTN_FWD_BLK = 1024
ATTN_BWD_QBLK = 1024
ACC_ROWS = HEAD_DIM + 16


def local_step(xs, pos, tgt, gains, early_weights, late_weights, early_side=None, fwd_sides=(None, None), reduction=None):
    g_attn, b_forget, g_q, g_kv, g_fo, g_mo, g_mlp, g_fin = gains
    t = xs.shape[0]
    blk = min(ATTN_BLK, t)
    fox_scale = 1.0 / (HEAD_DIM ** 0.5)
    mla_scale = 1.0 / ((HEAD_DIM + ROPE_DIM) ** 0.5)

    h1, *gathered = rmsnorm(xs, g_attn, out_dtype=BF16, name="norm_attn", side=early_side)
    w_in_t, w_uq_p, w_ukv = early_weights(gathered)
    w_qkv_t, w_small_t = split_w_in_t(w_in_t)
    kv = w_ukv.reshape(KV_RANK, N_HEADS, 2 * HEAD_DIM)
    w_ukv_p = jnp.concatenate([_pad_heads(kv[:, :, :HEAD_DIM].reshape(KV_RANK, FW), HEAD_DIM),
                               kv[:, :, HEAD_DIM:].reshape(KV_RANK, FW)], axis=1)
    b_f = _pad_lanes(b_forget, LANES)
    tab_c, tab_a, tab_b = rope_tables(pos)

    qkv, qkv_t = mm(h1, w_qkv_t, trans_b=True, out_dtypes=[BF16], t_blk=blk, name="proj_qkv")
    small, = mm(h1, w_small_t, trans_b=True, out_dtypes=[F32], name="proj_small")
    mq, mk, mv, lf, cqn, ckvn, mq_t, mv_t = mla_prep(small, g_q, g_kv, w_uq_p, w_ukv_p, tab_c, tab_a, tab_b, b_f,
                                                     name="mla_prep", bt=blk)
    f_cum = cumsum_rows(lf, reverse=False, name="gate_cumsum")
    f_blocks = f_cum[:, :N_HEADS].reshape(t // blk, blk, N_HEADS).transpose(0, 2, 1)
    fo, st_f, *gathered = flash_fwd(qkv_t, qkv, qkv_t, f_cum, qoff=0, koff=PAIRS, voff=2 * PAIRS, pair=True,
                                    scale=fox_scale, name="fox_fwd", blk=ATTN_FWD_BLK, side=fwd_sides[0])
    mo, st_m, *more = flash_fwd(mq_t, mk, mv_t, None, qoff=0, koff=0, voff=0, pair=False, scale=mla_scale, name="mla_fwd",
                                blk=ATTN_FWD_BLK, side=fwd_sides[1])
    w_o, w_up, w_down = late_weights(gathered + more)
    mixed = mix_norm(fo, mo, g_fo, g_mo, name="norm_mix")

    def inv_rms(v):
        return lax.rsqrt(jnp.mean(v * v, axis=-1, keepdims=True) + EPS)

    def residual_then_norm(acc, res, g):
        xn = acc + res
        return xn, xn * inv_rms(xn) * g

    def norm_bwd(dh, xn, res, g):
        r = inv_rms(xn)
        uu = dh * g
        return (r * uu - xn * (r * r * r * jnp.mean(uu * xn, axis=-1, keepdims=True)) + res,
                jnp.sum(dh * (xn * r), axis=0, keepdims=True))

    def norm_bwd2(dh, xn, res, g):
        dx, dg = norm_bwd(dh, xn, res, g)
        return dx, dx, dg

    def residual_then_loss(acc, res, target, g):
        xn = acc + res
        r = inv_rms(xn)
        xh = xn * r
        e = xh * g - target
        part = 0.5 * jnp.sum(jnp.mean(e * e, axis=-1, keepdims=True), axis=0, keepdims=True)
        dy = e * (1.0 / xn.shape[1])
        uu = dy * g
        dx = r * uu - xn * (r * r * r * jnp.mean(uu * xn, axis=-1, keepdims=True))
        return dx, dx, jnp.sum(dy * xh, axis=0, keepdims=True), part + jnp.zeros_like(g)

    x1, h2 = mm(mixed, w_o, extras=[xs], vecs=[g_mlp], epilogue=residual_then_norm, out_dtypes=[F32, BF16], name="out_proj")

    def relu2(uu):
        r = jnp.maximum(uu.astype(F32), 0.0)
        return (r * r).astype(BF16)

    u, = mm(h2, w_up, out_dtypes=[BF16], name="mlp_up")
    dx2, dx2b, dg_fin, loss_row = mm(u, w_down, a_pro=relu2, extras=[x1, tgt], vecs=[g_fin], epilogue=residual_then_loss,
                                     out_dtypes=[F32, BF16], n_sums=2, name="mlp_down_loss")
    loss = loss_row[:, :1]

    def relu2_grad(acc, uu):
        return (acc * (2.0 * jnp.maximum(uu.astype(F32), 0.0)),)

    du, = mm(dx2b, w_down, trans_b=True, extras=[u], epilogue=relu2_grad, out_dtypes=[BF16], name="mlp_down_bwd")
    dw_down, dw_down_b = (g.reshape(N_CHIPS, -1, w_down.shape[1])
                          for g in mm_tn(u, dx2b, a_pro=relu2, name="dw_down", bf16_copy=True))
    dx1, dx1b, dg_mlp = mm(du, w_up, trans_b=True, extras=[x1, dx2], vecs=[g_mlp], epilogue=norm_bwd2,
                           out_dtypes=[F32, BF16], n_sums=1, name="mlp_up_bwd")
    dw_up, dw_up_b = mm_tn(h2, du, name="dw_up", col_shards=N_CHIPS, bf16_copy=True)

    dw_o, dw_o_b = (g.reshape(N_CHIPS, -1, w_o.shape[1]) for g in mm_tn(mixed, dx1b, name="dw_o", bf16_copy=True))
    late, late_b = (dw_o, dw_up, dw_down), (dw_o_b, dw_up_b, dw_down_b)
    red = reduction
    dfo, dmo, dg_fo, dg_mo, st_f, st_m, dfo_t, dmo_t, *got = mix_norm_bwd(
        dx1b, w_o, fo, mo, g_fo, g_mo, st_f, st_m, name="out_proj_mix_bwd", bt=blk,
        side=red.late_swap(late, late_b) if red else None)
    dfq, dfk, dfv, d_f, *got = flash_bwd(
        qkv, qkv_t, qkv, qkv, dfo, dfo_t, st_f, f_blocks, qoff=0, koff=PAIRS, voff=2 * PAIRS, pair=True,
        scale=fox_scale, name="fox_bwd", qblk=ATTN_BWD_QBLK, side=red.late_scatter(got) if red else None)
    dmq, dmk, dmv, *got = flash_bwd(mq, mq_t, mk, mv, dmo, dmo_t, st_m, None, qoff=0, koff=0, voff=0,
                                    pair=False, scale=mla_scale, name="mla_bwd", qblk=ATTN_BWD_QBLK,
                                    side=red.late_halves(got) if red else None)
    if red:
        red.late_done(got)
    dqkv = jnp.concatenate([dfq, dfk, dfv], axis=1).astype(BF16)
    dlf = cumsum_rows(d_f, reverse=True, name="gate_cumsum_bwd")
    dsmall, dq_u, dkv_u, dg_q, dg_kv, db_f = mla_prep_bwd(dmq, dmk, dmv, dlf, small, g_q, g_kv, w_uq_p, w_ukv_p,
                                                          tab_c, tab_a, tab_b, b_f, name="mla_prep_bwd")
    dw_uq_p = mm_tn(cqn, dq_u, name="dw_uq")
    dw_ukv_p = mm_tn(ckvn, dkv_u, name="dw_ukv")

    dw_in_t = join_w_in_t(mm_tn(dqkv, h1, name="dw_qkv"), mm_tn(dsmall, h1, name="dw_small"))
    dk_cols = _unpad_heads(dw_ukv_p[:, :HW], HEAD_DIM).reshape(KV_RANK, N_HEADS, HEAD_DIM)
    dv_cols = dw_ukv_p[:, HW:].reshape(KV_RANK, N_HEADS, HEAD_DIM)
    dw_ukv = jnp.concatenate([dk_cols, dv_cols], axis=2).reshape(KV_RANK, N_HEADS * 2 * HEAD_DIM)
    early = (dw_in_t, split_cols(dw_uq_p), split_cols(dw_ukv))
    grad_x, dg_attn, *got = mm([dqkv, dsmall], [w_qkv_t, w_small_t], extras=[xs, dx1], vecs=[g_attn],
                               epilogue=norm_bwd, out_dtypes=[F32], n_sums=1, name="proj_bwd",
                               side=red.early_scatter(early) if red else None)
    if red:
        red.early_done(got)
    d_gains = (dg_attn, db_f[:, :N_HEADS], dg_q, dg_kv, dg_fo, dg_mo, dg_mlp, dg_fin)
    return loss, grad_x, early, late, d_gains


class GradReduction:
    def __init__(self, core_id, chip):
        self.core_id, self.chip = core_id, chip
        self.core = core_id.reshape(1).astype(jnp.int32)
        self.grads, self.pairs, self.halves, self.others = {}, {}, {}, {}

    def _other_halves(self, grads):
        out = []
        for g in grads:
            axis = 1 + split_axis(g.shape[1])
            size = g.shape[axis] // 2
            out.append(lax.dynamic_slice_in_dim(g, (1 - self.core_id) * size, size, axis=axis).astype(BF16))
        return out

    def _add_pairs(self, group, recvs):
        self.pairs[group] = [add_pair(g, r, self.core, name="add_pair_%s_%d" % (group, n))
                             for n, (g, r) in enumerate(zip(self.grads[group], recvs))]
        return scatter_side(self.pairs[group])

    def _chip_sums(self, group, scattered):
        chip = self.chip
        with_mine = [lax.dynamic_update_index_in_dim(s, lax.dynamic_index_in_dim(p, chip, 0, keepdims=True), chip, 0)
                     for s, p in zip(scattered, self.pairs[group])]
        self.halves[group] = [sum_chips(s, name="sum_chips_%s_%d" % (group, n)) for n, s in enumerate(with_mine)]
        return self.halves[group]

    def late_swap(self, grads, bf16_copies):
        self.grads["late"] = list(grads)
        return swap_halves_side(list(bf16_copies))

    def late_scatter(self, recvs):
        return self._add_pairs("late", recvs)

    def late_halves(self, scattered):
        return swap_side(self._chip_sums("late", scattered))

    def late_done(self, others):
        self.others["late"] = list(others)

    def early_scatter(self, grads):
        self.grads["early"] = list(grads)
        return self._add_pairs("early", run_side(swap_side(self._other_halves(grads)), name="swap_early_sends"))

    def early_done(self, scattered):
        self.others["early"] = list(run_side(swap_side(self._chip_sums("early", scattered)), name="swap_early_halves"))

def kernel(x, positions, attn_norm_g, w_in, b_forget, q_norm_g, w_uq, kv_norm_g, w_ukv, fox_out_g, mla_out_g, w_o, mlp_norm_g, w_up, w_down, final_norm_g, loss_target, m_attn_norm_g, m_w_in, m_b_forget, m_q_norm_g, m_w_uq, m_kv_norm_g, m_w_ukv, m_fox_out_g, m_mla_out_g, m_w_o, m_mlp_norm_g, m_w_up, m_w_down, m_final_norm_g, v_attn_norm_g, v_w_in, v_b_forget, v_q_norm_g, v_w_uq, v_kv_norm_g, v_w_ukv, v_fox_out_g, v_mla_out_g, v_w_o, v_mlp_norm_g, v_w_up, v_w_down, v_final_norm_g):
    core_id = lax.axis_index("c")
    core = core_id.reshape(1).astype(jnp.int32)
    chip = 2 * lax.axis_index("x") + lax.axis_index("y")
    big = [w_in, w_uq, w_ukv, w_o, w_up, w_down]
    big_m = [m_w_in, m_w_uq, m_w_ukv, m_w_o, m_w_up, m_w_down]
    big_v = [v_w_in, v_w_uq, v_w_ukv, v_w_o, v_w_up, v_w_down]
    n_early = 3

    def vec(a):
        return a.reshape(1, -1)

    small = [attn_norm_g, b_forget, q_norm_g, kv_norm_g, fox_out_g, mla_out_g, mlp_norm_g, final_norm_g]
    small_m = [m_attn_norm_g, m_b_forget, m_q_norm_g, m_kv_norm_g, m_fox_out_g, m_mla_out_g, m_mlp_norm_g, m_final_norm_g]
    small_v = [v_attn_norm_g, v_b_forget, v_q_norm_g, v_kv_norm_g, v_fox_out_g, v_mla_out_g, v_mlp_norm_g, v_final_norm_g]
    gains = [vec(a) for a in small]

    views = [big[0][0].T, _pad_heads(big[1][0], HEAD_DIM + ROPE_DIM)] + [w[0] for w in big[2:]]
    shards = [v.astype(BF16) for v in views]

    def with_own(gathered, mine):
        return [lax.dynamic_update_index_in_dim(g, s, chip, 0) for g, s in zip(gathered, mine)]

    def early_weights(gathered):
        g_in, g_uq, g_ukv = with_own(gathered, shards[:n_early])
        return g_in, join_cols(g_uq), join_cols(g_ukv)

    def late_weights(gathered):
        g_o, g_up, g_down = with_own(gathered, shards[n_early:])
        return g_o.reshape(-1, g_o.shape[2]), join_cols(g_up), g_down.reshape(-1, g_down.shape[2])

    reduction = GradReduction(core_id, chip)
    loss, grad_x, _, _, d_small = local_step(
        x[0], positions[0], loss_target[0], gains, early_weights, late_weights, gather_side(shards[:n_early]),
        (gather_side(shards[n_early:-1]), gather_side(shards[-1:])), reduction)
    halves = reduction.halves["early"] + reduction.halves["late"]
    others = reduction.others["early"] + reduction.others["late"]

    def rows8(vs):
        return jnp.concatenate([_pad_lanes(vec(a).astype(F32), 1024) for a in vs], axis=0)

    with_loss = [jnp.concatenate([d, loss], axis=1) if n == 1 else d for n, d in enumerate(d_small)]
    g_small8 = allreduce_small(rows8(with_loss))

    outs_big = []
    for n, (w, gm, go, m, v) in enumerate(zip(big, halves, others, big_m, big_v)):
        _, k, cols = w.shape
        if n == 0:
            outs = adamw_halves_t(w[0].T, gm, go, m[0].T, v[0].T, core, name="adamw_%d" % n)
            outs_big.append([o.T[None] for o in outs])
        else:
            if n == 1:
                gm, go = (_unpad_heads(gh, HEAD_DIM + ROPE_DIM) for gh in (gm, go))
            outs_big.append(adamw_halves(w, gm, go, m, v, core, name="adamw_%d" % n))
    d8, m8, v8 = adamw(rows8(small), g_small8, rows8(small_m), rows8(small_v), name="adamw_small")

    def unrows8(a8):
        return [a8[n, :s.size].reshape(s.shape) for n, s in enumerate(small)]

    loss_all = g_small8[1, N_HEADS]
    grads, deltas, new_m, new_v = [None] * 14, [None] * 14, [None] * 14, [None] * 14
    big_at = [1, 4, 6, 9, 11, 12]
    small_at = [0, 2, 3, 5, 7, 8, 10, 13]
    for n, at in enumerate(big_at):
        grads[at], deltas[at], new_m[at], new_v[at] = outs_big[n]
    for at, g, dd, mm_, vv in zip(small_at, unrows8(g_small8), unrows8(d8), unrows8(m8), unrows8(v8)):
        grads[at], deltas[at], new_m[at], new_v[at] = g, dd, mm_, vv
    return (loss_all, grad_x[None], *grads, *deltas, *new_m, *new_v)
```

```python
import jax
import jax.numpy as jnp
from jax import lax
from jax.experimental import pallas as pl
from jax.experimental.pallas import tpu as pltpu

F32 = jnp.float32
BF16 = jnp.bfloat16
MESH = pl.DeviceIdType.MESH

EPS = 1e-6
ROPE_THETA = 10000.0
N_HEADS = 8
PAIRS = N_HEADS // 2
HEAD_DIM = 64
ROPE_DIM = 32
LANES = 128
Q_RANK = 384
KV_RANK = 256
N_CHIPS = 4
ADAM_LR, ADAM_B1, ADAM_B2, ADAM_EPS, ADAM_WD, ADAM_STEP = 0.001, 0.9, 0.999, 1e-08, 0.01, 10
VMEM_LIMIT = 48 * 1024 * 1024
LOG2E = 1.4426950408889634
LN2 = 0.6931471805599453
NN = (((1,), (0,)), ((), ()))
NT = (((1,), (1,)), ((), ()))
TN = (((0,), (0,)), ((), ()))


def _params(sem=None):
    return pltpu.CompilerParams(dimension_semantics=sem, vmem_limit_bytes=VMEM_LIMIT)


def _fit(block, dim):
    if dim <= block:
        return dim
    return next(b for b in range(block - block % LANES, 0, -LANES) if dim % b == 0)


def _row_block(rows):
    return next(b for b in (256, 128, 64, 32, 16, 8) if rows % b == 0)


def gridded(body, *, name, grid, in_specs, out_specs, out_shape, ins, semantics, side=None):
    if side is None:
        return pl.pallas_call(body, name=name, grid=grid, in_specs=in_specs, out_specs=out_specs, out_shape=out_shape,
                              compiler_params=_params(semantics))(*ins)
    n_in, n_out = len(in_specs), len(out_specs)
    steps = 1
    for extent in grid:
        steps *= extent

    def riding(*refs):
        main_in, s_in, main_out, s_out, sems = _split_refs(refs, [n_in, len(side.ins), n_out, len(side.out_shapes), 2])
        step = 0
        for axis, extent in enumerate(grid):
            step = step * extent + pl.program_id(axis)

        @pl.when(step == 0)
        def _():
            side.first(s_in, s_out, *sems)

        body(*main_in, *main_out)

        @pl.when(step == steps - 1)
        def _():
            if side.mid is not None:
                side.mid(s_in, s_out, *sems)
            side.last(s_in, s_out, *sems)

    s_in_specs, s_out_specs = side.specs()
    return pl.pallas_call(
        riding, name=name, grid=grid, in_specs=list(in_specs) + s_in_specs, out_specs=list(out_specs) + s_out_specs,
        out_shape=list(out_shape) + side.out_shapes, scratch_shapes=side.sems(),
        compiler_params=_params(("arbitrary",) * len(grid)))(*ins, *side.ins)


def rmsnorm(x, g, *, out_dtype, name, bt=512, side=None):
    t, d = x.shape
    bt = min(bt, t)

    def body(x_ref, g_ref, o_ref):
        xv = x_ref[...].astype(F32)
        r = lax.rsqrt(jnp.mean(xv * xv, axis=-1, keepdims=True) + EPS)
        o_ref[...] = (xv * r * g_ref[...]).astype(o_ref.dtype)

    return gridded(
        body, name=name, grid=(t // bt,),
        in_specs=[pl.BlockSpec((bt, d), lambda i: (i, 0)), pl.BlockSpec((1, d), lambda i: (0, 0))],
        out_specs=[pl.BlockSpec((bt, d), lambda i: (i, 0))],
        out_shape=[jax.ShapeDtypeStruct((t, d), out_dtype)],
        ins=[x, g], semantics=("parallel",), side=side)


def mm(a, b, *, trans_b=False, a_pro=None, extras=(), vecs=(), epilogue=None, out_dtypes, n_sums=0, t_blk=None, name,
       bm=1024, bn=1024, side=None):
    a_list = list(a) if isinstance(a, (list, tuple)) else [a]
    b_list = list(b) if isinstance(b, (list, tuple)) else [b]
    m = a_list[0].shape[0]
    n = b_list[0].shape[0] if trans_b else b_list[0].shape[1]
    ks = [x.shape[1] for x in a_list]
    if sum(ks) > 2048:
        bm = bm // 2
    bm, bn = _fit(bm, m), _fit(bn, n)
    assert n_sums == 0 or bn == n
    n_ab, n_ex, n_vec, n_out = len(a_list), len(extras), len(vecs), len(out_dtypes)
    n_t = 0 if t_blk is None else 1

    def body(*refs):
        a_refs, b_refs, ex, vs, outs, t_outs, sums = _split_refs(refs, [n_ab, n_ab, n_ex, n_vec, n_out, n_t, n_sums])
        acc = None
        for a_ref, b_ref in zip(a_refs, b_refs):
            a_tile = a_ref[...] if a_pro is None else a_pro(a_ref[...])
            part = lax.dot_general(a_tile, b_ref[...], NT if trans_b else NN, preferred_element_type=F32)
            acc = part if acc is None else acc + part
        res = epilogue(acc, *[e[...] for e in ex], *[v[...] for v in vs]) if epilogue is not None else (acc,)
        for o, r in zip(outs, res[:n_out]):
            o[...] = r.astype(o.dtype)
        for t_ref in t_outs:
            for u in range(bm // t_blk):
                t_ref[u] = res[0][u * t_blk:(u + 1) * t_blk, :].T.astype(t_ref.dtype)
        if n_sums:
            @pl.when(pl.program_id(0) == 0)
            def _():
                for s_ref in sums:
                    s_ref[...] = jnp.zeros_like(s_ref)

            for s_ref, r in zip(sums, res[n_out:]):
                s_ref[...] += r

    tile = pl.BlockSpec((bm, bn), lambda i, j: (i, j))
    vec = pl.BlockSpec((1, bn), lambda i, j: (0, j))
    t_specs, t_shapes = [], []
    if t_blk is not None:
        t_specs = [pl.BlockSpec((bm // t_blk, bn, t_blk), lambda i, j: (i, j, 0))]
        t_shapes = [jax.ShapeDtypeStruct((m // t_blk, n, t_blk), out_dtypes[0])]
    a_specs = [pl.BlockSpec((bm, k), lambda i, j: (i, 0)) for k in ks]
    b_specs = [pl.BlockSpec((bn, k), lambda i, j: (j, 0)) if trans_b else pl.BlockSpec((k, bn), lambda i, j: (0, j)) for k in ks]
    return gridded(
        body, name=name, grid=(m // bm, n // bn),
        in_specs=a_specs + b_specs + [tile] * n_ex + [vec] * n_vec,
        out_specs=[tile] * n_out + t_specs + [vec] * n_sums,
        out_shape=[jax.ShapeDtypeStruct((m, n), dt) for dt in out_dtypes] + t_shapes + [jax.ShapeDtypeStruct((1, n), F32)] * n_sums,
        ins=[*a_list, *b_list, *extras, *vecs],
        semantics=("arbitrary", "arbitrary") if n_sums else ("parallel", "parallel"), side=side)


def mm_tn(a, b, *, a_pro=None, name, col_shards=1, bf16_copy=False, bk=1024, bn=1024, bt=1024):
    t, k = a.shape
    n = b.shape[1]
    ns = n // col_shards
    bk, bn, bt = _fit(bk, k), _fit(bn, ns), _fit(bt, t)
    per = ns // bn
    last = t // bt - 1

    def body(a_ref, b_ref, o_ref, *copy_ref):
        @pl.when(pl.program_id(2) == 0)
        def _():
            o_ref[...] = jnp.zeros_like(o_ref)

        a_tile = a_ref[...] if a_pro is None else a_pro(a_ref[...])
        o_ref[...] += lax.dot_general(a_tile, b_ref[...], TN, preferred_element_type=F32)
        if bf16_copy:
            @pl.when(pl.program_id(2) == last)
            def _():
                copy_ref[0][...] = o_ref[...].astype(BF16)

    if col_shards == 1:
        out_spec = pl.BlockSpec((bk, bn), lambda i, j, s: (i, j))
        shape = (k, n)
    else:
        out_spec = pl.BlockSpec((None, bk, bn), lambda i, j, s: (j // per, i, j % per))
        shape = (col_shards, k, ns)
    out_specs, out_shape = out_spec, jax.ShapeDtypeStruct(shape, F32)
    if bf16_copy:
        out_specs, out_shape = [out_spec, out_spec], [out_shape, jax.ShapeDtypeStruct(shape, BF16)]
    return pl.pallas_call(
        body, name=name, grid=(k // bk, n // bn, t // bt),
        in_specs=[pl.BlockSpec((bt, bk), lambda i, j, s: (s, i)), pl.BlockSpec((bt, bn), lambda i, j, s: (s, j))],
        out_specs=out_specs, out_shape=out_shape,
        compiler_params=_params(("parallel", "parallel", "arbitrary")),
    )(a, b)


def _split3(x):
    hi = x.astype(BF16)
    r1 = x - hi.astype(F32)
    mid = r1.astype(BF16)
    lo = (r1 - mid.astype(F32)).astype(BF16)
    return hi, mid, lo


def cumsum_rows(x, *, reverse, name, bc=512):
    t, d = x.shape
    bc = min(bc, t)
    nb = t // bc

    def body(x_ref, o_ref, carry):
        @pl.when(pl.program_id(0) == 0)
        def _():
            carry[...] = jnp.zeros_like(carry)

        r = lax.broadcasted_iota(jnp.int32, (bc, bc), 0)
        c = lax.broadcasted_iota(jnp.int32, (bc, bc), 1)
        tri = jnp.where((r <= c) if reverse else (r >= c), 1.0, 0.0).astype(BF16)
        hi, mid, lo = _split3(x_ref[...])
        s = (lax.dot_general(tri, hi, NN, preferred_element_type=F32)
             + lax.dot_general(tri, mid, NN, preferred_element_type=F32)
             + lax.dot_general(tri, lo, NN, preferred_element_type=F32)) + carry[0:1, :]
        o_ref[...] = s
        carry[0:1, :] = s[0:1, :] if reverse else s[bc - 1:bc, :]

    imap = (lambda i: (nb - 1 - i, 0)) if reverse else (lambda i: (i, 0))
    return pl.pallas_call(
        body, name=name, grid=(nb,),
        in_specs=[pl.BlockSpec((bc, d), imap)], out_specs=pl.BlockSpec((bc, d), imap),
        out_shape=jax.ShapeDtypeStruct((t, d), F32),
        scratch_shapes=[pltpu.VMEM((8, d), F32)],
        compiler_params=_params(("arbitrary",)),
    )(x)


def _rope(x, c, a, b):
    return x * c + pltpu.roll(x, LANES - ROPE_DIM // 2, 1) * a + pltpu.roll(x, ROPE_DIM // 2, 1) * b


def _rope_bwd(d, c, a, b):
    return d * c + pltpu.roll(d * a, ROPE_DIM // 2, 1) + pltpu.roll(d * b, LANES - ROPE_DIM // 2, 1)


S_CQ, S_CKV, S_KR, S_F, S_END = 0, Q_RANK, Q_RANK + KV_RANK, Q_RANK + KV_RANK + LANES, 1024
HW = N_HEADS * LANES
FW = N_HEADS * HEAD_DIM


def mla_prep(small, g_q, g_kv, w_uq, w_ukv, tab_c, tab_a, tab_b, b_f, *, name, bt=512):
    t = small.shape[0]
    bt = min(bt, t)

    def body(s_ref, gq_ref, gkv_ref, wq_ref, wkv_ref, c_ref, a_ref, b_ref, bf_ref,
             mq_ref, mk_ref, mv_ref, lf_ref, cqn_ref, ckvn_ref, mqt_ref, mvt_ref):
        cq = s_ref[:, S_CQ:S_CKV]
        rq = lax.rsqrt(jnp.mean(cq * cq, axis=-1, keepdims=True) + EPS)
        cqn = (cq * rq * gq_ref[...]).astype(BF16)
        ckv = s_ref[:, S_CKV:S_KR]
        rkv = lax.rsqrt(jnp.mean(ckv * ckv, axis=-1, keepdims=True) + EPS)
        ckvn = (ckv * rkv * gkv_ref[...]).astype(BF16)
        cqn_ref[...] = cqn
        ckvn_ref[...] = ckvn
        tc, ta, tb = c_ref[...], a_ref[...], b_ref[...]
        q = jnp.dot(cqn, wq_ref[...], preferred_element_type=F32)
        kv = jnp.dot(ckvn, wkv_ref[...], preferred_element_type=F32)
        kr = _rope(s_ref[:, S_KR:S_F], tc, ta, tb)
        for h in range(N_HEADS):
            sl = slice(h * LANES, (h + 1) * LANES)
            roped = _rope(q[:, sl], tc, ta, tb)
            mq_ref[:, sl] = roped.astype(BF16)
            mqt_ref[0, sl, :] = roped.T.astype(BF16)
            mk_ref[:, sl] = (kv[:, sl] + kr).astype(BF16)
        mv_ref[...] = kv[:, HW:].astype(BF16)
        mvt_ref[0] = kv[:, HW:].T.astype(BF16)
        z = s_ref[:, S_F:S_END - LANES] + bf_ref[...]
        lf_ref[...] = jnp.minimum(z, 0.0) - jnp.log(1.0 + jnp.exp(-jnp.abs(z)))

    def row(w):
        return pl.BlockSpec((bt, w), lambda i: (i, 0))

    def full(arr):
        return pl.BlockSpec(arr.shape, lambda i: (0, 0))

    return pl.pallas_call(
        body, name=name, grid=(t // bt,),
        in_specs=[row(S_END), full(g_q), full(g_kv), full(w_uq), full(w_ukv), row(LANES), row(LANES), row(LANES), full(b_f)],
        out_specs=[row(HW), row(HW), row(FW), row(LANES), row(Q_RANK), row(KV_RANK),
                   pl.BlockSpec((1, HW, bt), lambda i: (i, 0, 0)), pl.BlockSpec((1, FW, bt), lambda i: (i, 0, 0))],
        out_shape=[jax.ShapeDtypeStruct((t, HW), BF16)] * 2 + [jax.ShapeDtypeStruct((t, FW), BF16), jax.ShapeDtypeStruct((t, LANES), F32),
                   jax.ShapeDtypeStruct((t, Q_RANK), BF16), jax.ShapeDtypeStruct((t, KV_RANK), BF16),
                   jax.ShapeDtypeStruct((t // bt, HW, bt), BF16), jax.ShapeDtypeStruct((t // bt, FW, bt), BF16)],
        compiler_params=_params(("parallel",)),
    )(small, g_q, g_kv, w_uq, w_ukv, tab_c, tab_a, tab_b, b_f)


def mla_prep_bwd(dmq, dmk, dmv, dlf, small, g_q, g_kv, w_uq, w_ukv, tab_c, tab_a, tab_b, b_f, *, name, bt=512):
    t = small.shape[0]
    bt = min(bt, t)

    def body(dmq_ref, dmk_ref, dmv_ref, dlf_ref, s_ref, gq_ref, gkv_ref, wq_ref, wkv_ref, c_ref, a_ref, b_ref, bf_ref,
             ds_ref, dq_ref, dkv_ref, dgq_ref, dgkv_ref, db_ref):
        tc, ta, tb = c_ref[...], a_ref[...], b_ref[...]
        lane = lax.broadcasted_iota(jnp.int32, (1, LANES), 1)
        dkr = jnp.zeros((bt, LANES), F32)
        for h in range(N_HEADS):
            sl = slice(h * LANES, (h + 1) * LANES)
            dq_ref[:, sl] = _rope_bwd(dmq_ref[:, sl], tc, ta, tb).astype(BF16)
            dkr = dkr + dmk_ref[:, sl]
        dkv_ref[:, :HW] = dmk_ref[...].astype(BF16)
        dkv_ref[:, HW:] = dmv_ref[...].astype(BF16)
        in_rope = (lane >= HEAD_DIM) & (lane < HEAD_DIM + ROPE_DIM)
        ds_ref[:, S_KR:S_F] = jnp.where(in_rope, _rope_bwd(dkr, tc, ta, tb), 0.0).astype(BF16)

        def norm_bwd(raw, g_ref, dn, dg_ref):
            r = lax.rsqrt(jnp.mean(raw * raw, axis=-1, keepdims=True) + EPS)
            u = dn * g_ref[...]
            dot = jnp.mean(u * raw, axis=-1, keepdims=True)
            dg_ref[...] += jnp.sum(dn * (raw * r), axis=0, keepdims=True)
            return r * u - raw * (r * r * r * dot)

        @pl.when(pl.program_id(0) == 0)
        def _():
            dgq_ref[...] = jnp.zeros_like(dgq_ref)
            dgkv_ref[...] = jnp.zeros_like(dgkv_ref)
            db_ref[...] = jnp.zeros_like(db_ref)

        dcqn = lax.dot_general(dq_ref[...], wq_ref[...], NT, preferred_element_type=F32)
        ds_ref[:, S_CQ:S_CKV] = norm_bwd(s_ref[:, S_CQ:S_CKV], gq_ref, dcqn, dgq_ref).astype(BF16)
        dckvn = lax.dot_general(dkv_ref[...], wkv_ref[...], NT, preferred_element_type=F32)
        ds_ref[:, S_CKV:S_KR] = norm_bwd(s_ref[:, S_CKV:S_KR], gkv_ref, dckvn, dgkv_ref).astype(BF16)
        z = s_ref[:, S_F:S_END - LANES] + bf_ref[...]
        dz = jnp.where(lane < N_HEADS, dlf_ref[...] / (1.0 + jnp.exp(z)), 0.0)
        db_ref[...] += jnp.sum(dz, axis=0, keepdims=True)
        ds_ref[:, S_F:S_END - LANES] = dz.astype(BF16)
        ds_ref[:, S_END - LANES:] = jnp.zeros((bt, LANES), BF16)

    def row(w):
        return pl.BlockSpec((bt, w), lambda i: (i, 0))

    def full(arr):
        return pl.BlockSpec(arr.shape, lambda i: (0, 0))

    def vec(w):
        return pl.BlockSpec((1, w), lambda i: (0, 0))

    return pl.pallas_call(
        body, name=name, grid=(t // bt,),
        in_specs=[row(HW), row(HW), row(FW), row(LANES), row(S_END), full(g_q), full(g_kv), full(w_uq), full(w_ukv),
                  row(LANES), row(LANES), row(LANES), full(b_f)],
        out_specs=[row(S_END), row(HW), row(HW + FW), vec(Q_RANK), vec(KV_RANK), vec(LANES)],
        out_shape=[jax.ShapeDtypeStruct((t, S_END), BF16), jax.ShapeDtypeStruct((t, HW), BF16),
                   jax.ShapeDtypeStruct((t, HW + FW), BF16), jax.ShapeDtypeStruct((1, Q_RANK), F32),
                   jax.ShapeDtypeStruct((1, KV_RANK), F32), jax.ShapeDtypeStruct((1, LANES), F32)],
        compiler_params=_params(("arbitrary",)),
    )(dmq, dmk, dmv, dlf, small, g_q, g_kv, w_uq, w_ukv, tab_c, tab_a, tab_b, b_f)


class Side:
    def __init__(self, ins, out_shapes, n_sems, first, last, mid=None):
        self.ins, self.out_shapes, self.n_sems = list(ins), list(out_shapes), n_sems
        self.first, self.mid, self.last = first, mid, last

    def specs(self):
        return [ANY] * len(self.ins), [ANY] * len(self.out_shapes)

    def sems(self):
        return [pltpu.SemaphoreType.DMA((self.n_sems,)), pltpu.SemaphoreType.DMA((self.n_sems,))]


def _lane():
    return lax.broadcasted_iota(jnp.int32, (1, LANES), 1)


def _halves(x):
    zero = jnp.zeros_like(x)
    return [jnp.where(_lane() < HEAD_DIM, x, zero), jnp.where(_lane() >= HEAD_DIM, x, zero)]


def _groups(x):
    return [x[:, :LANES], x[:, LANES:]]


def _pick_row(tile, h):
    row = lax.broadcasted_iota(jnp.int32, (tile.shape[0], 1), 0)
    return jnp.sum(jnp.where(row == h, tile, 0.0), axis=0, keepdims=True)


def _pick_lane(tile, h):
    return jnp.sum(jnp.where(_lane() == h, tile, 0.0), axis=1, keepdims=True)


def _row_halves(x):
    row = lax.broadcasted_iota(jnp.int32, (LANES, 1), 0)
    zero = jnp.zeros_like(x)
    return [jnp.where(row < HEAD_DIM, x, zero), jnp.where(row >= HEAD_DIM, x, zero)]


def _below_diagonal(s, lead=0):
    r = lax.broadcasted_iota(jnp.int32, s.shape, 0)
    c = lax.broadcasted_iota(jnp.int32, s.shape, 1)
    return jnp.where(c <= r + lead, s, -jnp.inf)


def _above_diagonal(s):
    r = lax.broadcasted_iota(jnp.int32, s.shape, 0)
    c = lax.broadcasted_iota(jnp.int32, s.shape, 1)
    return jnp.where(r <= c, s, -jnp.inf)


def _split_refs(refs, counts):
    out, at = [], 0
    for n in counts:
        out.append(refs[at:at + n])
        at += n
    return out


def flash_fwd(qt_arr, k_arr, vt_arr, f_cum, *, qoff, koff, voff, pair, scale, name, blk=512, pps=1, side=None):
    t = k_arr.shape[0]
    tblk = qt_arr.shape[2]
    blk = max(min(blk, t), tblk)
    sub = blk // tblk
    nb = t // blk
    nh = 2 * pps
    steps = PAIRS // pps * nb
    w = LANES if pair else 2 * LANES
    has_bias = f_cum is not None
    ins = [qt_arr, k_arr, vt_arr] + ([f_cum] if has_bias else [])

    def wide(ref, first):
        parts = [ref[first + u] for u in range(sub)]
        return parts[0] if sub == 1 else jnp.concatenate(parts, axis=1)
    s_ins, s_outs = (side.ins, side.out_shapes) if side else ([], [])

    def body(*refs):
        main, si, outs, so, sems = _split_refs(refs, [len(ins), len(s_ins), 2, len(s_outs), 2 if side else 0])
        qt_ref, k_ref, vt_ref = main[:3]
        f_ref = main[3] if has_bias else None
        o_ref, st_ref = outs
        g, i = pl.program_id(0), pl.program_id(1)
        step_id = g * nb + i
        if side:
            @pl.when(step_id == 0)
            def _():
                side.first(si, so, *sems)

            if side.mid is not None:
                @pl.when(step_id == (3 * steps) // 4)
                def _():
                    side.mid(si, so, *sems)

        qt = (wide(qt_ref, 0).astype(F32) * (scale * LOG2E)).astype(BF16)
        qts = []
        for u in range(pps):
            qu = qt[u * w:(u + 1) * w]
            qts += _row_halves(qu) if pair else [qu[:LANES], qu[LANES:]]

        def k_of(kk, hh):
            ku = kk[:, (hh // 2) * w:(hh // 2 + 1) * w]
            return ku if pair else ku[:, (hh % 2) * LANES:(hh % 2 + 1) * LANES]

        def with_ones(vt_rows):
            return jnp.concatenate([vt_rows, jnp.ones((ACC_ROWS - HEAD_DIM, vt_rows.shape[1]), BF16)], axis=0)

        def step(j, carry, diagonal):
            rows = pl.ds(pl.multiple_of(j * blk, blk), blk)
            kk = k_ref[rows, :]
            vt = wide(vt_ref, sub * j)
            out = []
            for n in range(nh):
                m, acc = carry[n]
                s = jnp.dot(k_of(kk, n), qts[n], preferred_element_type=F32)
                if has_bias:
                    s = s - LOG2E * _pick_lane(f_ref[rows, :], nh * g + n)
                if diagonal:
                    s = _above_diagonal(s)
                m_new = jnp.maximum(m, jnp.max(s, axis=0, keepdims=True))
                p = jnp.exp2(s - m_new).astype(BF16)
                out.append((m_new, jnp.exp2(m - m_new) * acc
                            + jnp.dot(with_ones(vt[n * HEAD_DIM:(n + 1) * HEAD_DIM]), p, preferred_element_type=F32)))
            return tuple(out)

        def diagonal_in_halves(carry):
            h = tblk
            halves = [pl.ds(pl.multiple_of(i * blk, blk), h), pl.ds(pl.multiple_of(i * blk + h, h), h)]
            k_top, k_bot = k_ref[halves[0], :], k_ref[halves[1], :]
            vt_top, vt_bot = vt_ref[sub * i], vt_ref[sub * i + 1]
            out = []
            for n in range(nh):
                m, acc = carry[n]
                heads = slice(n * HEAD_DIM, (n + 1) * HEAD_DIM)
                s_top = jnp.dot(k_of(k_top, n), qts[n], preferred_element_type=F32)
                s_bot = jnp.dot(k_of(k_bot, n), qts[n][:, h:], preferred_element_type=F32)
                if has_bias:
                    s_top = s_top - LOG2E * _pick_lane(f_ref[halves[0], :], nh * g + n)
                    s_bot = s_bot - LOG2E * _pick_lane(f_ref[halves[1], :], nh * g + n)
                s_top, s_bot = _above_diagonal(s_top), _above_diagonal(s_bot)
                m_top = jnp.maximum(m, jnp.max(s_top, axis=0, keepdims=True))
                m_new = jnp.concatenate([m_top[:, :h], jnp.maximum(m_top[:, h:], jnp.max(s_bot, axis=0, keepdims=True))], axis=1)
                p_top = jnp.exp2(s_top - m_new).astype(BF16)
                p_bot = jnp.exp2(s_bot - m_new[:, h:]).astype(BF16)
                late = jnp.concatenate([jnp.zeros((ACC_ROWS, h), F32),
                                        jnp.dot(with_ones(vt_bot[heads]), p_bot, preferred_element_type=F32)], axis=1)
                out.append((m_new, jnp.exp2(m - m_new) * acc
                            + jnp.dot(with_ones(vt_top[heads]), p_top, preferred_element_type=F32) + late))
            return tuple(out)

        init = tuple((jnp.full((1, blk), -jnp.inf, F32), jnp.zeros((ACC_ROWS, blk), F32)) for _ in range(nh))
        carry = lax.fori_loop(0, i, lambda j, c: step(j, c, False), init)
        done = diagonal_in_halves(carry) if sub == 2 else step(i, carry, True)
        sums = [acc[HEAD_DIM:HEAD_DIM + 1] for _, acc in done]
        o_ref[...] = jnp.concatenate([acc[:HEAD_DIM] / l for (_, acc), l in zip(done, sums)], axis=0).T
        row = lax.broadcasted_iota(jnp.int32, (LANES, 1), 0)
        for u in range(pps):
            lse = [done[2 * u + n][0] + jnp.log2(sums[2 * u + n]) for n in range(2)]
            st_ref[u] = jnp.where(row == 0, lse[0], jnp.where(row == 1, lse[1], 0.0)).T
        if side:
            @pl.when(step_id == steps - 1)
            def _():
                side.last(si, so, *sems)

    in_specs = [pl.BlockSpec((sub, pps * w, tblk), lambda g, i: (i, qoff // pps + g, 0)),
                pl.BlockSpec((t, pps * w), lambda g, i: (0, koff // pps + g)),
                pl.BlockSpec((t // tblk, pps * LANES, tblk), lambda g, i: (0, voff // pps + g, 0))]
    if has_bias:
        in_specs.append(pl.BlockSpec((t, LANES), lambda g, i: (0, 0)))
    s_in_specs, s_out_specs = side.specs() if side else ([], [])
    return pl.pallas_call(
        body, name=name, grid=(PAIRS // pps, nb), in_specs=in_specs + s_in_specs,
        out_specs=[pl.BlockSpec((blk, pps * LANES), lambda g, i: (i, g)), pl.BlockSpec((pps, blk, LANES), lambda g, i: (g, i, 0))]
        + s_out_specs,
        out_shape=[jax.ShapeDtypeStruct((t, PAIRS * LANES), F32), jax.ShapeDtypeStruct((PAIRS, t, LANES), F32)] + list(s_outs),
        scratch_shapes=side.sems() if side else [],
        compiler_params=_params(("arbitrary", "arbitrary")),
    )(*ins, *s_ins)


def mix_norm(fo, mo, g_fo, g_mo, *, name, bt=512):
    t, d = fo.shape
    bt = min(bt, t)

    def body(fo_ref, mo_ref, gf_ref, gm_ref, o_ref):
        for n, (x_ref, g_ref) in enumerate(((fo_ref, gf_ref), (mo_ref, gm_ref))):
            xv = x_ref[...]
            r = lax.rsqrt(jnp.mean(xv * xv, axis=-1, keepdims=True) + EPS)
            o_ref[:, n * d:(n + 1) * d] = (xv * r * g_ref[...]).astype(BF16)

    row = pl.BlockSpec((bt, d), lambda i: (i, 0))
    vec = pl.BlockSpec((1, d), lambda i: (0, 0))
    return pl.pallas_call(
        body, name=name, grid=(t // bt,), in_specs=[row, row, vec, vec],
        out_specs=pl.BlockSpec((bt, 2 * d), lambda i: (i, 0)),
        out_shape=jax.ShapeDtypeStruct((t, 2 * d), BF16),
        compiler_params=_params(("parallel",)),
    )(fo, mo, g_fo, g_mo)


def mix_norm_bwd(dx1b, w_o, fo, mo, g_fo, g_mo, st_f, st_m, *, name, bt=512, side=None):
    t, d = fo.shape
    bt = min(bt, t)

    def body(dx_in_ref, w_ref, fo_ref, mo_ref, gf_ref, gm_ref, sf_ref, sm_ref, dfo_ref, dmo_ref, dgf_ref, dgm_ref,
             sfo_ref, smo_ref, dfot_ref, dmot_ref):
        @pl.when(pl.program_id(0) == 0)
        def _():
            dgf_ref[...] = jnp.zeros_like(dgf_ref)
            dgm_ref[...] = jnp.zeros_like(dgm_ref)

        dmixed = lax.dot_general(dx_in_ref[...], w_ref[...], NT, preferred_element_type=F32)
        groups = ((fo_ref, gf_ref, dfo_ref, dgf_ref, sf_ref, sfo_ref, dfot_ref),
                  (mo_ref, gm_ref, dmo_ref, dgm_ref, sm_ref, smo_ref, dmot_ref))
        for n, (x_ref, g_ref, dx_ref, dg_ref, st_ref, sto_ref, dxt_ref) in enumerate(groups):
            xv = x_ref[...]
            dhv = dmixed[:, n * d:(n + 1) * d]
            r = lax.rsqrt(jnp.mean(xv * xv, axis=-1, keepdims=True) + EPS)
            u = dhv * g_ref[...]
            dxf = r * u - xv * (r * r * r * jnp.mean(u * xv, axis=-1, keepdims=True))
            dxb = dxf.astype(BF16)
            dx_ref[...] = dxb
            dxt_ref[0] = dxf.T.astype(BF16)
            dg_ref[...] += jnp.sum(dhv * (xv * r), axis=0, keepdims=True)
            prod = xv * dxb.astype(F32)
            for g in range(PAIRS):
                grp = prod[:, g * LANES:(g + 1) * LANES]
                da = jnp.sum(jnp.where(_lane() < HEAD_DIM, grp, 0.0), axis=1, keepdims=True)
                db = jnp.sum(jnp.where(_lane() >= HEAD_DIM, grp, 0.0), axis=1, keepdims=True)
                sto_ref[g] = jnp.where(_lane() == 2, da, jnp.where(_lane() == 3, db, st_ref[g]))

    row = pl.BlockSpec((bt, d), lambda i: (i, 0))
    vec = pl.BlockSpec((1, d), lambda i: (0, 0))
    stat = pl.BlockSpec((PAIRS, bt, LANES), lambda i: (0, i, 0))
    return gridded(
        body, name=name, grid=(t // bt,),
        in_specs=[pl.BlockSpec((bt, dx1b.shape[1]), lambda i: (i, 0)), pl.BlockSpec(w_o.shape, lambda i: (0, 0)),
                  row, row, vec, vec, stat, stat],
        out_specs=[row, row, vec, vec, stat, stat] + [pl.BlockSpec((1, d, bt), lambda i: (i, 0, 0))] * 2,
        out_shape=[jax.ShapeDtypeStruct((t, d), BF16)] * 2 + [jax.ShapeDtypeStruct((1, d), F32)] * 2
        + [jax.ShapeDtypeStruct(st_f.shape, F32)] * 2 + [jax.ShapeDtypeStruct((t // bt, d, bt), BF16)] * 2,
        ins=[dx1b, w_o, fo, mo, g_fo, g_mo, st_f, st_m], semantics=("arbitrary",), side=side)


def flash_bwd(q_arr, qt_arr, k_arr, v_arr, do_arr, dot_arr, st, f_blocks, *, qoff, koff, voff, pair, scale, name, qblk=1024,
              side=None):
    t = q_arr.shape[0]
    blk = qt_arr.shape[2]
    qblk = max(min(qblk, t), blk)
    sub = qblk // blk
    nb, nbq = t // blk, t // qblk
    w = LANES if pair else 2 * LANES
    hw = w // 2
    has_bias = f_blocks is not None
    split = _halves if pair else _groups
    ins = [q_arr, qt_arr, k_arr, v_arr, do_arr, dot_arr, st] + ([f_blocks] if has_bias else [])
    n_out = 4 if has_bias else 3
    s_ins, s_outs = (side.ins, side.out_shapes) if side else ([], [])

    def wide(ref, first, count):
        parts = [ref[first + u] for u in range(count)]
        return parts[0] if count == 1 else jnp.concatenate(parts, axis=1)

    def body(*refs):
        main, si, outs, so, sems = _split_refs(refs, [len(ins), len(s_ins), n_out, len(s_outs), 2 if side else 0])
        q_ref, qt_ref, k_ref, v_ref, do_ref, dot_ref, st_ref = main[:7]
        dq_ref, dk_ref, dv_ref = outs[:3]
        g, j = pl.program_id(0), pl.program_id(1)
        step_id = g * nb + j
        if side:
            @pl.when(step_id == 0)
            def _():
                side.first(si, so, *sems)

        @pl.when(j == 0)
        def _():
            dq_ref[...] = jnp.zeros_like(dq_ref)

        kk, vv = k_ref[...], v_ref[...]
        ks = [kk, kk] if pair else _groups(kk)
        if has_bias:
            f_ref, df_ref = main[7], outs[3]
            fk = [LOG2E * _pick_row(f_ref[0], 2 * g + n) for n in range(2)]

            @pl.when(step_id == 0)
            def _():
                df_ref[...] = jnp.zeros_like(df_ref)

        def step(tb, count, carry, diagonal):
            rows = pl.ds(pl.multiple_of(tb * blk, blk), count * blk)
            qs = split((q_ref[rows, :].astype(F32) * (scale * LOG2E)).astype(BF16))
            qt = (wide(qt_ref, tb, count).astype(F32) * (scale * LOG2E)).astype(BF16)
            dos = [h.astype(BF16) for h in _halves(do_ref[rows, :].astype(F32))]
            dot = wide(dot_ref, tb, count)
            stats = st_ref[0, rows, :]
            new, dqs, row_sums = [], [], []
            for n in range(2):
                dkt, dvt, dfk = carry[n]
                s = lax.dot_general(qs[n], ks[n], NT, preferred_element_type=F32)
                if has_bias:
                    s = s - fk[n]
                if diagonal:
                    s = _below_diagonal(s)
                p = jnp.exp2(s - stats[:, n:n + 1])
                dp = lax.dot_general(dos[n], vv, NT, preferred_element_type=F32)
                ds = p * (dp - stats[:, 2 + n:3 + n])
                dsb = ds.astype(BF16)
                dvt = dvt + jnp.dot(dot[n * HEAD_DIM:(n + 1) * HEAD_DIM], p.astype(BF16), preferred_element_type=F32)
                dkt = dkt + jnp.dot(qt[n * hw:(n + 1) * hw], dsb, preferred_element_type=F32)
                dqs.append(jnp.dot(dsb, ks[n], preferred_element_type=F32))
                if has_bias:
                    dfk = dfk - jnp.sum(ds, axis=0, keepdims=True)
                    row_sums.append(jnp.sum(ds, axis=1, keepdims=True))
                new.append((dkt, dvt, dfk))
            dq = (jnp.where(_lane() < HEAD_DIM, dqs[0], dqs[1]) if pair else jnp.concatenate(dqs, axis=1)) * scale
            if has_bias:
                df_ref[rows, :] += jnp.where(_lane() == 2 * g, row_sums[0], jnp.where(_lane() == 2 * g + 1, row_sums[1], 0.0))
            dq_ref[rows, :] += dq
            return tuple(new)

        init = tuple((jnp.zeros((hw, blk), F32), jnp.zeros((HEAD_DIM, blk), F32), jnp.zeros((1, blk), F32)) for _ in range(2))
        carry = step(j, 1, init, True)
        whole = j // sub + 1
        carry = lax.fori_loop(j + 1, whole * sub, lambda tb, c: step(tb, 1, c, False), carry)
        (dka, dva, dfa), (dkb, dvb, dfb) = lax.fori_loop(whole, nbq, lambda i, c: step(i * sub, sub, c, False), carry)
        dk_ref[...] = jnp.concatenate([dka, dkb], axis=0).T * LN2
        dv_ref[...] = jnp.concatenate([dva, dvb], axis=0).T
        if has_bias:
            row = lax.broadcasted_iota(jnp.int32, (LANES, 1), 0)
            by_head = jnp.where(row == 2 * g, dfa, jnp.where(row == 2 * g + 1, dfb, 0.0))
            df_ref[pl.ds(pl.multiple_of(j * blk, blk), blk), :] += by_head.T
        if side:
            @pl.when(step_id == PAIRS * nb - 1)
            def _():
                side.last(si, so, *sems)

    in_specs = [pl.BlockSpec((t, w), lambda g, j: (0, qoff + g)), pl.BlockSpec((nb, w, blk), lambda g, j: (0, qoff + g, 0)),
                pl.BlockSpec((blk, w), lambda g, j: (j, koff + g)), pl.BlockSpec((blk, LANES), lambda g, j: (j, voff + g)),
                pl.BlockSpec((t, LANES), lambda g, j: (0, g)), pl.BlockSpec((nb, LANES, blk), lambda g, j: (0, g, 0)),
                pl.BlockSpec((1, t, LANES), lambda g, j: (g, 0, 0))]
    out_specs = [pl.BlockSpec((t, w), lambda g, j: (0, g)), pl.BlockSpec((blk, w), lambda g, j: (j, g)),
                 pl.BlockSpec((blk, LANES), lambda g, j: (j, g))]
    out_shape = [jax.ShapeDtypeStruct((t, PAIRS * w), F32)] * 2 + [jax.ShapeDtypeStruct((t, PAIRS * LANES), F32)]
    if has_bias:
        in_specs.append(pl.BlockSpec((1, N_HEADS, blk), lambda g, j: (j, 0, 0)))
        out_specs.append(pl.BlockSpec((t, LANES), lambda g, j: (0, 0)))
        out_shape.append(jax.ShapeDtypeStruct((t, LANES), F32))
    s_in_specs, s_out_specs = side.specs() if side else ([], [])
    return pl.pallas_call(
        body, name=name, grid=(PAIRS, nb), in_specs=in_specs + s_in_specs, out_specs=out_specs + s_out_specs,
        out_shape=out_shape + list(s_outs), scratch_shapes=side.sems() if side else [],
        compiler_params=_params(("arbitrary", "arbitrary")),
    )(*ins, *s_ins)


def _adamw_math(w, g, m, v):
    nm = ADAM_B1 * m + (1.0 - ADAM_B1) * g
    nv = ADAM_B2 * v + (1.0 - ADAM_B2) * (g * g)
    m_hat = nm / (1.0 - ADAM_B1 ** ADAM_STEP)
    v_hat = nv / (1.0 - ADAM_B2 ** ADAM_STEP)
    return -ADAM_LR * (m_hat / (jnp.sqrt(v_hat) + ADAM_EPS) + ADAM_WD * w), nm, nv


def adamw(w, g, m, v, *, name):
    rws, cols = w.shape
    br = _row_block(rws)

    def body(w_ref, g_ref, m_ref, v_ref, d_ref, nm_ref, nv_ref):
        d_ref[...], nm_ref[...], nv_ref[...] = _adamw_math(w_ref[...], g_ref[...], m_ref[...], v_ref[...])

    blk = pl.BlockSpec((br, cols), lambda i: (i, 0))
    return pl.pallas_call(
        body, name=name, grid=(rws // br,), in_specs=[blk] * 4, out_specs=[blk] * 3,
        out_shape=[jax.ShapeDtypeStruct((rws, cols), F32)] * 3,
        compiler_params=_params(("parallel",)),
    )(w, g, m, v)


def adamw_halves(w, g_mine, g_other, m, v, core, *, name):
    _, k, n = w.shape
    br = _row_block(k // 2)
    nh = k // 2 // br

    def body(c_ref, w_ref, gm_ref, go_ref, m_ref, v_ref, g_out, d_ref, nm_ref, nv_ref):
        gv = jnp.where(pl.program_id(0) == c_ref[0], gm_ref[...], go_ref[...])
        g_out[0] = gv
        d_ref[0], nm_ref[0], nv_ref[0] = _adamw_math(w_ref[0], gv, m_ref[0], v_ref[0])

    full = pl.BlockSpec((1, br, n), lambda hb, i, c: (0, hb * nh + i, 0))
    half = pl.BlockSpec((br, n), lambda hb, i, c: (i, 0))
    return pl.pallas_call(
        body, name=name,
        grid_spec=pltpu.PrefetchScalarGridSpec(num_scalar_prefetch=1, grid=(2, nh), in_specs=[full, half, half, full, full],
                                               out_specs=[full] * 4),
        out_shape=[jax.ShapeDtypeStruct(w.shape, F32)] * 4,
        compiler_params=_params(("parallel", "parallel")),
    )(core, w, g_mine, g_other, m, v)


def adamw_halves_t(wt, gt_mine, gt_other, mt, vt, core, *, name, bc=128):
    n, k = wt.shape
    nh = k // 2 // bc

    def body(c_ref, w_ref, gm_ref, go_ref, m_ref, v_ref, g_out, d_ref, nm_ref, nv_ref):
        gv = jnp.where(pl.program_id(0) == c_ref[0], gm_ref[...], go_ref[...])
        g_out[...] = gv
        d_ref[...], nm_ref[...], nv_ref[...] = _adamw_math(w_ref[...], gv, m_ref[...], v_ref[...])

    full = pl.BlockSpec((n, bc), lambda hb, i, c: (0, hb * nh + i))
    half = pl.BlockSpec((n, bc), lambda hb, i, c: (0, i))
    return pl.pallas_call(
        body, name=name,
        grid_spec=pltpu.PrefetchScalarGridSpec(num_scalar_prefetch=1, grid=(2, nh), in_specs=[full, half, half, full, full],
                                               out_specs=[full] * 4),
        out_shape=[jax.ShapeDtypeStruct(wt.shape, F32)] * 4,
        compiler_params=_params(("parallel", "parallel")),
    )(core, wt, gt_mine, gt_other, mt, vt)


def add_pair(dw, recv, core, *, name):
    n4, k, n = dw.shape
    half = (1, k // 2, n) if split_axis(k) == 0 else (1, k, n // 2)
    mine = (lambda q, c: (q, c[0], 0)) if split_axis(k) == 0 else (lambda q, c: (q, 0, c[0]))

    def body(c_ref, a_ref, b_ref, o_ref):
        o_ref[...] = (a_ref[...] + b_ref[...].astype(F32)).astype(BF16)

    return pl.pallas_call(
        body, name=name,
        grid_spec=pltpu.PrefetchScalarGridSpec(
            num_scalar_prefetch=1, grid=(n4,),
            in_specs=[pl.BlockSpec(half, mine), pl.BlockSpec(half, lambda q, c: (q, 0, 0))],
            out_specs=pl.BlockSpec(half, lambda q, c: (q, 0, 0))),
        out_shape=jax.ShapeDtypeStruct((n4,) + half[1:], BF16),
        compiler_params=_params(("parallel",)),
    )(core, dw, recv)


def sum_chips(parts, *, name):
    n4, r, n = parts.shape
    if r % 16 == 0:
        br, bc = _row_block(r), n
    else:
        br, bc = r, LANES

    def body(p_ref, o_ref):
        acc = p_ref[0].astype(F32)
        for q in range(1, n4):
            acc = acc + p_ref[q].astype(F32)
        o_ref[...] = acc

    return pl.pallas_call(
        body, name=name, grid=(r // br, n // bc),
        in_specs=[pl.BlockSpec((n4, br, bc), lambda i, j: (0, i, j))], out_specs=pl.BlockSpec((br, bc), lambda i, j: (i, j)),
        out_shape=jax.ShapeDtypeStruct((r, n), F32),
        compiler_params=_params(("parallel", "parallel")),
    )(parts)


ANY = pl.BlockSpec(memory_space=pl.ANY)


def _place():
    x, y, c = lax.axis_index("x"), lax.axis_index("y"), lax.axis_index("c")
    chips = [(1 - x, y), (x, 1 - y), (1 - x, 1 - y)]
    return x, y, c, chips


def _copy(src, dst, send_sems, recv_sems, k, to):
    return pltpu.make_async_remote_copy(src_ref=src, dst_ref=dst, send_sem=send_sems.at[k], recv_sem=recv_sems.at[k],
                                        device_id=to, device_id_type=MESH)


def split_axis(rows):
    return 0 if rows % 32 == 0 else 1


def _half(ref, lead, hf):
    rows, cols = ref.shape[-2:]
    if split_axis(rows) == 0:
        at = (pl.ds(hf * (rows // 2), rows // 2), slice(None))
    else:
        at = (slice(None), pl.ds(hf * (cols // 2), cols // 2))
    return ref.at[at] if lead is None else ref.at[(lead,) + at]


def _gather_first(srcs, dsts, ssems, rsems):
    x, y, c, chips = _place()
    for ti, (s, d) in enumerate(zip(srcs, dsts)):
        for j, (cx, cy) in enumerate(chips):
            _copy(_half(s, None, c), _half(d, 2 * x + y, c), ssems, rsems, 3 * ti + j, (cx, cy, c)).start()


def _gather_mid(srcs, dsts, ssems, rsems):
    x, y, c, chips = _place()
    n1 = 3 * len(srcs)
    for ti, d in enumerate(dsts):
        for j, (cx, cy) in enumerate(chips):
            landed = _half(d, 2 * cx + cy, c)
            _copy(landed, landed, ssems, rsems, 3 * ti + j, (cx, cy, c)).wait_recv()
            _copy(landed, landed, ssems, rsems, n1 + 3 * ti + j, (x, y, 1 - c)).start()


def _gather_last(srcs, dsts, ssems, rsems):
    x, y, c, chips = _place()
    n1 = 3 * len(srcs)
    for ti, (s, d) in enumerate(zip(srcs, dsts)):
        for j, (cx, cy) in enumerate(chips):
            other = _half(d, 2 * cx + cy, 1 - c)
            _copy(other, other, ssems, rsems, n1 + 3 * ti + j, (x, y, 1 - c)).wait_recv()
        for j, (cx, cy) in enumerate(chips):
            mine = _half(s, None, c)
            _copy(mine, mine, ssems, rsems, 3 * ti + j, (cx, cy, c)).wait_send()
            _copy(mine, mine, ssems, rsems, n1 + 3 * ti + j, (x, y, 1 - c)).wait_send()


def gather_side(shards):
    return Side(shards, [jax.ShapeDtypeStruct((N_CHIPS,) + s.shape, s.dtype) for s in shards], 6 * len(shards),
                _gather_first, _gather_last, _gather_mid)


def _scatter_first(srcs, dsts, ssems, rsems):
    x, y, c, chips = _place()
    for ti, (s, d) in enumerate(zip(srcs, dsts)):
        for j, (cx, cy) in enumerate(chips):
            _copy(s.at[2 * cx + cy], d.at[2 * x + y], ssems, rsems, 3 * ti + j, (cx, cy, c)).start()


def _scatter_last(srcs, dsts, ssems, rsems):
    x, y, c, chips = _place()
    for ti, (s, d) in enumerate(zip(srcs, dsts)):
        for j, (cx, cy) in enumerate(chips):
            _copy(s.at[2 * cx + cy], d.at[2 * cx + cy], ssems, rsems, 3 * ti + j, (cx, cy, c)).wait_recv()
        for j, (cx, cy) in enumerate(chips):
            _copy(s.at[2 * cx + cy], d.at[2 * cx + cy], ssems, rsems, 3 * ti + j, (cx, cy, c)).wait_send()


def scatter_side(parts):
    return Side(parts, [jax.ShapeDtypeStruct(p.shape, p.dtype) for p in parts], 3 * len(parts), _scatter_first, _scatter_last)


def run_side(side, *, name):
    n_in, n_out = len(side.ins), len(side.out_shapes)

    def body(*refs):
        si, so, sems = _split_refs(refs, [n_in, n_out, 2])
        side.first(si, so, *sems)
        if side.mid is not None:
            side.mid(si, so, *sems)
        side.last(si, so, *sems)

    in_specs, out_specs = side.specs()
    return pl.pallas_call(body, name=name, in_specs=in_specs, out_specs=out_specs, out_shape=side.out_shapes,
                          scratch_shapes=side.sems())(*side.ins)


def _swap_first(srcs, dsts, ssems, rsems):
    x, y, c, _ = _place()
    for k, (s, d) in enumerate(zip(srcs, dsts)):
        _copy(s, d, ssems, rsems, k, (x, y, 1 - c)).start()


def _swap_last(srcs, dsts, ssems, rsems):
    x, y, c, _ = _place()
    for k, (s, d) in enumerate(zip(srcs, dsts)):
        _copy(s, d, ssems, rsems, k, (x, y, 1 - c)).wait()


def swap_side(xs):
    return Side(xs, [jax.ShapeDtypeStruct(a.shape, a.dtype) for a in xs], len(xs), _swap_first, _swap_last)


def _swap_halves(srcs, dsts, ssems, rsems):
    x, y, c, _ = _place()
    for k, (s, d) in enumerate(zip(srcs, dsts)):
        hk = s.shape[1] // 2
        yield _copy(s.at[:, pl.ds((1 - c) * hk, hk), :], d, ssems, rsems, k, (x, y, 1 - c))


def _swap_halves_first(srcs, dsts, ssems, rsems):
    for cp in _swap_halves(srcs, dsts, ssems, rsems):
        cp.start()


def _swap_halves_last(srcs, dsts, ssems, rsems):
    for cp in _swap_halves(srcs, dsts, ssems, rsems):
        cp.wait()


def swap_halves_side(xs):
    return Side(xs, [jax.ShapeDtypeStruct((a.shape[0], a.shape[1] // 2, a.shape[2]), a.dtype) for a in xs], len(xs),
                _swap_halves_first, _swap_halves_last)


def allreduce_small(s):
    n_dev = 8

    def body(s_ref, out_ref, buf, send_sems, recv_sems):
        x, y, c, _ = _place()
        me = 4 * x + 2 * y + c
        buf[me] = s_ref[...]
        sends = []
        for k in range(1, n_dev):
            px = 1 - x if k & 4 else x
            py = 1 - y if k & 2 else y
            pc = 1 - c if k & 1 else c
            cp = _copy(s_ref, buf.at[me], send_sems, recv_sems, k - 1, (px, py, pc))
            cp.start()
            sends.append((cp, 4 * px + 2 * py + pc))
        for k, (cp, peer) in enumerate(sends):
            _copy(s_ref, buf.at[peer], send_sems, recv_sems, k, (x, y, c)).wait_recv()
        for cp, _ in sends:
            cp.wait_send()
        acc = buf[0]
        for d in range(1, n_dev):
            acc = acc + buf[d]
        out_ref[...] = acc

    vm = pl.BlockSpec(memory_space=pltpu.VMEM)
    return pl.pallas_call(
        body, name="allreduce_small", in_specs=[vm], out_specs=vm,
        out_shape=jax.ShapeDtypeStruct(s.shape, F32),
        scratch_shapes=[pltpu.VMEM((n_dev,) + s.shape, F32), pltpu.SemaphoreType.DMA((n_dev - 1,)),
                        pltpu.SemaphoreType.DMA((n_dev - 1,))],
    )(s)


def join_cols(sm):
    n4, k, n = sm.shape
    return sm.transpose(1, 0, 2).reshape(k, n4 * n)


def split_cols(full):
    k, n = full.shape
    return full.reshape(k, N_CHIPS, n // N_CHIPS).transpose(1, 0, 2)


def _pad_heads(w, width):
    lead, heads = w.shape[:-1], w.shape[-1] // width
    w = w.reshape(lead + (heads, width))
    return jnp.pad(w, [(0, 0)] * len(lead) + [(0, 0), (0, LANES - width)]).reshape(lead + (heads * LANES,))


def _unpad_heads(w, width):
    lead, heads = w.shape[:-1], w.shape[-1] // LANES
    return w.reshape(lead + (heads, LANES))[..., :width].reshape(lead + (heads * width,))


O_F = 3 * FW
O_CQ = O_F + N_HEADS
O_CKV = O_CQ + Q_RANK
O_KR = O_CKV + KV_RANK
O_END = O_KR + ROPE_DIM


def _shard_rows(sm, a, b):
    r = sm.shape[1]
    out = []
    while a < b:
        q = a // r
        e = min(b, (q + 1) * r)
        out.append(sm[q, a - q * r:e - q * r])
        a = e
    return out


def split_w_in_t(w_sm):
    d = w_sm.shape[2]

    def z(n):
        return [jnp.zeros((n, d), w_sm.dtype)]

    small = (_shard_rows(w_sm, O_CQ, O_CKV) + _shard_rows(w_sm, O_CKV, O_KR) + z(HEAD_DIM) + _shard_rows(w_sm, O_KR, O_END)
             + z(LANES - HEAD_DIM - ROPE_DIM) + _shard_rows(w_sm, O_F, O_CQ) + z(LANES - N_HEADS) + z(LANES))
    return jnp.concatenate(_shard_rows(w_sm, 0, O_F), axis=0), jnp.concatenate(small, axis=0)


def join_w_in_t(d_qkv_t, d_small_t):
    kr = S_KR + HEAD_DIM
    segments = [(d_qkv_t, 0, 0, O_F), (d_small_t, S_F, O_F, N_HEADS), (d_small_t, S_CQ, O_CQ, Q_RANK),
                (d_small_t, S_CKV, O_CKV, KV_RANK), (d_small_t, kr, O_KR, ROPE_DIM)]
    r = O_END // N_CHIPS
    shards = []
    for q in range(N_CHIPS):
        pieces = []
        for src, s0, v0, n in segments:
            a, b = max(v0, q * r), min(v0 + n, (q + 1) * r)
            if a < b:
                pieces.append(src[s0 + a - v0:s0 + b - v0])
        shards.append(jnp.concatenate(pieces, axis=0))
    return jnp.stack(shards)


def rope_tables(pos):
    t = pos.shape[0]
    inv_freq = ROPE_THETA ** (-jnp.arange(0, ROPE_DIM, 2, dtype=F32) / ROPE_DIM)
    ang = pos.astype(F32)[:, None] * inv_freq
    cos, sin = jnp.cos(ang), jnp.sin(ang)
    half = ROPE_DIM // 2

    def z(n):
        return jnp.zeros((t, n), F32)

    tab_c = jnp.concatenate([jnp.ones((t, HEAD_DIM), F32), cos, cos, z(LANES - HEAD_DIM - ROPE_DIM)], axis=1)
    tab_a = jnp.concatenate([z(HEAD_DIM), -sin, z(half), z(LANES - HEAD_DIM - ROPE_DIM)], axis=1)
    tab_b = jnp.concatenate([z(HEAD_DIM), z(half), sin, z(LANES - HEAD_DIM - ROPE_DIM)], axis=1)
    return tab_c, tab_a, tab_b


def _pad_lanes(v, n):
    return jnp.pad(v, ((0, 0), (0, n - v.shape[1])))


ATTN_BLK = 512
ATTN_FWD_BLK = 1024
ATTN_BWD_QBLK = 1024
ATTN_FWD_PPS = 1
ACC_ROWS = HEAD_DIM + 16


def local_step(xs, pos, tgt, gains, early_weights, late_weights, early_side=None, fwd_sides=(None, None), reduction=None):
    g_attn, b_forget, g_q, g_kv, g_fo, g_mo, g_mlp, g_fin = gains
    t = xs.shape[0]
    blk = min(ATTN_BLK, t)
    fox_scale = 1.0 / (HEAD_DIM ** 0.5)
    mla_scale = 1.0 / ((HEAD_DIM + ROPE_DIM) ** 0.5)

    h1, *gathered = rmsnorm(xs, g_attn, out_dtype=BF16, name="norm_attn", side=early_side)
    w_in_t, w_uq_p, w_ukv = early_weights(gathered)
    w_qkv_t, w_small_t = split_w_in_t(w_in_t)
    kv = w_ukv.reshape(KV_RANK, N_HEADS, 2 * HEAD_DIM)
    w_ukv_p = jnp.concatenate([_pad_heads(kv[:, :, :HEAD_DIM].reshape(KV_RANK, FW), HEAD_DIM),
                               kv[:, :, HEAD_DIM:].reshape(KV_RANK, FW)], axis=1)
    b_f = _pad_lanes(b_forget, LANES)
    tab_c, tab_a, tab_b = rope_tables(pos)

    qkv, qkv_t = mm(h1, w_qkv_t, trans_b=True, out_dtypes=[BF16], t_blk=blk, name="proj_qkv")
    small, = mm(h1, w_small_t, trans_b=True, out_dtypes=[F32], name="proj_small")
    mq, mk, mv, lf, cqn, ckvn, mq_t, mv_t = mla_prep(small, g_q, g_kv, w_uq_p, w_ukv_p, tab_c, tab_a, tab_b, b_f,
                                                     name="mla_prep", bt=blk)
    f_cum = cumsum_rows(lf, reverse=False, name="gate_cumsum")
    f_blocks = f_cum[:, :N_HEADS].reshape(t // blk, blk, N_HEADS).transpose(0, 2, 1)
    fo, st_f, *gathered = flash_fwd(qkv_t, qkv, qkv_t, f_cum, qoff=0, koff=PAIRS, voff=2 * PAIRS, pair=True,
                                    scale=fox_scale, name="fox_fwd", blk=ATTN_FWD_BLK, pps=ATTN_FWD_PPS, side=fwd_sides[0])
    mo, st_m, *more = flash_fwd(mq_t, mk, mv_t, None, qoff=0, koff=0, voff=0, pair=False, scale=mla_scale, name="mla_fwd",
                                blk=ATTN_FWD_BLK, pps=ATTN_FWD_PPS, side=fwd_sides[1])
    w_o, w_up, w_down = late_weights(gathered + more)
    mixed = mix_norm(fo, mo, g_fo, g_mo, name="norm_mix")

    def inv_rms(v):
        return lax.rsqrt(jnp.mean(v * v, axis=-1, keepdims=True) + EPS)

    def residual_then_norm(acc, res, g):
        xn = acc + res
        return xn, xn * inv_rms(xn) * g

    def norm_bwd(dh, xn, res, g):
        r = inv_rms(xn)
        uu = dh * g
        return (r * uu - xn * (r * r * r * jnp.mean(uu * xn, axis=-1, keepdims=True)) + res,
                jnp.sum(dh * (xn * r), axis=0, keepdims=True))

    def norm_bwd2(dh, xn, res, g):
        dx, dg = norm_bwd(dh, xn, res, g)
        return dx, dx, dg

    def residual_then_loss(acc, res, target, g):
        xn = acc + res
        r = inv_rms(xn)
        xh = xn * r
        e = xh * g - target
        part = 0.5 * jnp.sum(jnp.mean(e * e, axis=-1, keepdims=True), axis=0, keepdims=True)
        dy = e * (1.0 / xn.shape[1])
        uu = dy * g
        dx = r * uu - xn * (r * r * r * jnp.mean(uu * xn, axis=-1, keepdims=True))
        return dx, dx, jnp.sum(dy * xh, axis=0, keepdims=True), part + jnp.zeros_like(g)

    x1, h2 = mm(mixed, w_o, extras=[xs], vecs=[g_mlp], epilogue=residual_then_norm, out_dtypes=[F32, BF16], name="out_proj")

    def relu2(uu):
        r = jnp.maximum(uu.astype(F32), 0.0)
        return (r * r).astype(BF16)

    u, = mm(h2, w_up, out_dtypes=[BF16], name="mlp_up")
    dx2, dx2b, dg_fin, loss_row = mm(u, w_down, a_pro=relu2, extras=[x1, tgt], vecs=[g_fin], epilogue=residual_then_loss,
                                     out_dtypes=[F32, BF16], n_sums=2, name="mlp_down_loss")
    loss = loss_row[:, :1]

    def relu2_grad(acc, uu):
        return (acc * (2.0 * jnp.maximum(uu.astype(F32), 0.0)),)

    du, = mm(dx2b, w_down, trans_b=True, extras=[u], epilogue=relu2_grad, out_dtypes=[BF16], name="mlp_down_bwd")
    dw_down, dw_down_b = (g.reshape(N_CHIPS, -1, w_down.shape[1])
                          for g in mm_tn(u, dx2b, a_pro=relu2, name="dw_down", bf16_copy=True))
    dx1, dx1b, dg_mlp = mm(du, w_up, trans_b=True, extras=[x1, dx2], vecs=[g_mlp], epilogue=norm_bwd2,
                           out_dtypes=[F32, BF16], n_sums=1, name="mlp_up_bwd")
    dw_up, dw_up_b = mm_tn(h2, du, name="dw_up", col_shards=N_CHIPS, bf16_copy=True)

    dw_o, dw_o_b = (g.reshape(N_CHIPS, -1, w_o.shape[1]) for g in mm_tn(mixed, dx1b, name="dw_o", bf16_copy=True))
    late, late_b = (dw_o, dw_up, dw_down), (dw_o_b, dw_up_b, dw_down_b)
    red = reduction
    dfo, dmo, dg_fo, dg_mo, st_f, st_m, dfo_t, dmo_t, *got = mix_norm_bwd(
        dx1b, w_o, fo, mo, g_fo, g_mo, st_f, st_m, name="out_proj_mix_bwd", bt=blk,
        side=red.late_swap(late, late_b) if red else None)
    dfq, dfk, dfv, d_f, *got = flash_bwd(
        qkv, qkv_t, qkv, qkv, dfo, dfo_t, st_f, f_blocks, qoff=0, koff=PAIRS, voff=2 * PAIRS, pair=True,
        scale=fox_scale, name="fox_bwd", qblk=ATTN_BWD_QBLK, side=red.late_scatter(got) if red else None)
    dmq, dmk, dmv, *got = flash_bwd(mq, mq_t, mk, mv, dmo, dmo_t, st_m, None, qoff=0, koff=0, voff=0,
                                    pair=False, scale=mla_scale, name="mla_bwd", qblk=ATTN_BWD_QBLK,
                                    side=red.late_halves(got) if red else None)
    if red:
        red.late_done(got)
    dqkv = jnp.concatenate([dfq, dfk, dfv], axis=1).astype(BF16)
    dlf = cumsum_rows(d_f, reverse=True, name="gate_cumsum_bwd")
    dsmall, dq_u, dkv_u, dg_q, dg_kv, db_f = mla_prep_bwd(dmq, dmk, dmv, dlf, small, g_q, g_kv, w_uq_p, w_ukv_p,
                                                          tab_c, tab_a, tab_b, b_f, name="mla_prep_bwd")
    dw_uq_p = mm_tn(cqn, dq_u, name="dw_uq")
    dw_ukv_p = mm_tn(ckvn, dkv_u, name="dw_ukv")

    dw_in_t = join_w_in_t(mm_tn(dqkv, h1, name="dw_qkv"), mm_tn(dsmall, h1, name="dw_small"))
    dk_cols = _unpad_heads(dw_ukv_p[:, :HW], HEAD_DIM).reshape(KV_RANK, N_HEADS, HEAD_DIM)
    dv_cols = dw_ukv_p[:, HW:].reshape(KV_RANK, N_HEADS, HEAD_DIM)
    dw_ukv = jnp.concatenate([dk_cols, dv_cols], axis=2).reshape(KV_RANK, N_HEADS * 2 * HEAD_DIM)
    early = (dw_in_t, split_cols(dw_uq_p), split_cols(dw_ukv))
    grad_x, dg_attn, *got = mm([dqkv, dsmall], [w_qkv_t, w_small_t], extras=[xs, dx1], vecs=[g_attn],
                               epilogue=norm_bwd, out_dtypes=[F32], n_sums=1, name="proj_bwd",
                               side=red.early_scatter(early) if red else None)
    if red:
        red.early_done(got)
    d_gains = (dg_attn, db_f[:, :N_HEADS], dg_q, dg_kv, dg_fo, dg_mo, dg_mlp, dg_fin)
    return loss, grad_x, early, late, d_gains


class GradReduction:
    def __init__(self, core_id, chip):
        self.core_id, self.chip = core_id, chip
        self.core = core_id.reshape(1).astype(jnp.int32)
        self.grads, self.pairs, self.halves, self.others = {}, {}, {}, {}

    def _other_halves(self, grads):
        out = []
        for g in grads:
            axis = 1 + split_axis(g.shape[1])
            size = g.shape[axis] // 2
            out.append(lax.dynamic_slice_in_dim(g, (1 - self.core_id) * size, size, axis=axis).astype(BF16))
        return out

    def _add_pairs(self, group, recvs):
        self.pairs[group] = [add_pair(g, r, self.core, name="add_pair_%s_%d" % (group, n))
                             for n, (g, r) in enumerate(zip(self.grads[group], recvs))]
        return scatter_side(self.pairs[group])

    def _chip_sums(self, group, scattered):
        chip = self.chip
        with_mine = [lax.dynamic_update_index_in_dim(s, lax.dynamic_index_in_dim(p, chip, 0, keepdims=True), chip, 0)
                     for s, p in zip(scattered, self.pairs[group])]
        self.halves[group] = [sum_chips(s, name="sum_chips_%s_%d" % (group, n)) for n, s in enumerate(with_mine)]
        return self.halves[group]

    def late_swap(self, grads, bf16_copies):
        self.grads["late"] = list(grads)
        return swap_halves_side(list(bf16_copies))

    def late_scatter(self, recvs):
        return self._add_pairs("late", recvs)

    def late_halves(self, scattered):
        return swap_side(self._chip_sums("late", scattered))

    def late_done(self, others):
        self.others["late"] = list(others)

    def early_scatter(self, grads):
        self.grads["early"] = list(grads)
        return self._add_pairs("early", run_side(swap_side(self._other_halves(grads)), name="swap_early_sends"))

    def early_done(self, scattered):
        self.others["early"] = list(run_side(swap_side(self._chip_sums("early", scattered)), name="swap_early_halves"))

def kernel(x, positions, attn_norm_g, w_in, b_forget, q_norm_g, w_uq, kv_norm_g, w_ukv, fox_out_g, mla_out_g, w_o, mlp_norm_g, w_up, w_down, final_norm_g, loss_target, m_attn_norm_g, m_w_in, m_b_forget, m_q_norm_g, m_w_uq, m_kv_norm_g, m_w_ukv, m_fox_out_g, m_mla_out_g, m_w_o, m_mlp_norm_g, m_w_up, m_w_down, m_final_norm_g, v_attn_norm_g, v_w_in, v_b_forget, v_q_norm_g, v_w_uq, v_kv_norm_g, v_w_ukv, v_fox_out_g, v_mla_out_g, v_w_o, v_mlp_norm_g, v_w_up, v_w_down, v_final_norm_g):
    core_id = lax.axis_index("c")
    core = core_id.reshape(1).astype(jnp.int32)
    chip = 2 * lax.axis_index("x") + lax.axis_index("y")
    big = [w_in, w_uq, w_ukv, w_o, w_up, w_down]
    big_m = [m_w_in, m_w_uq, m_w_ukv, m_w_o, m_w_up, m_w_down]
    big_v = [v_w_in, v_w_uq, v_w_ukv, v_w_o, v_w_up, v_w_down]
    n_early = 3

    def vec(a):
        return a.reshape(1, -1)

    small = [attn_norm_g, b_forget, q_norm_g, kv_norm_g, fox_out_g, mla_out_g, mlp_norm_g, final_norm_g]
    small_m = [m_attn_norm_g, m_b_forget, m_q_norm_g, m_kv_norm_g, m_fox_out_g, m_mla_out_g, m_mlp_norm_g, m_final_norm_g]
    small_v = [v_attn_norm_g, v_b_forget, v_q_norm_g, v_kv_norm_g, v_fox_out_g, v_mla_out_g, v_mlp_norm_g, v_final_norm_g]
    gains = [vec(a) for a in small]

    views = [big[0][0].T, _pad_heads(big[1][0], HEAD_DIM + ROPE_DIM)] + [w[0] for w in big[2:]]
    shards = [v.astype(BF16) for v in views]

    def with_own(gathered, mine):
        return [lax.dynamic_update_index_in_dim(g, s, chip, 0) for g, s in zip(gathered, mine)]

    def early_weights(gathered):
        g_in, g_uq, g_ukv = with_own(gathered, shards[:n_early])
        return g_in, join_cols(g_uq), join_cols(g_ukv)

    def late_weights(gathered):
        g_o, g_up, g_down = with_own(gathered, shards[n_early:])
        return g_o.reshape(-1, g_o.shape[2]), join_cols(g_up), g_down.reshape(-1, g_down.shape[2])

    reduction = GradReduction(core_id, chip)
    loss, grad_x, _, _, d_small = local_step(
        x[0], positions[0], loss_target[0], gains, early_weights, late_weights, gather_side(shards[:n_early]),
        (gather_side(shards[n_early:-1]), gather_side(shards[-1:])), reduction)
    halves = reduction.halves["early"] + reduction.halves["late"]
    others = reduction.others["early"] + reduction.others["late"]

    def rows8(vs):
        return jnp.concatenate([_pad_lanes(vec(a).astype(F32), 1024) for a in vs], axis=0)

    with_loss = [jnp.concatenate([d, loss], axis=1) if n == 1 else d for n, d in enumerate(d_small)]
    g_small8 = allreduce_small(rows8(with_loss))

    outs_big = []
    for n, (w, gm, go, m, v) in enumerate(zip(big, halves, others, big_m, big_v)):
        _, k, cols = w.shape
        if n == 0:
            outs = adamw_halves_t(w[0].T, gm, go, m[0].T, v[0].T, core, name="adamw_%d" % n)
            outs_big.append([o.T[None] for o in outs])
        else:
            if n == 1:
                gm, go = (_unpad_heads(gh, HEAD_DIM + ROPE_DIM) for gh in (gm, go))
            outs_big.append(adamw_halves(w, gm, go, m, v, core, name="adamw_%d" % n))
    d8, m8, v8 = adamw(rows8(small), g_small8, rows8(small_m), rows8(small_v), name="adamw_small")

    def unrows8(a8):
        return [a8[n, :s.size].reshape(s.shape) for n, s in enumerate(small)]

    loss_all = g_small8[1, N_HEADS]
    grads, deltas, new_m, new_v = [None] * 14, [None] * 14, [None] * 14, [None] * 14
    big_at = [1, 4, 6, 9, 11, 12]
    small_at = [0, 2, 3, 5, 7, 8, 10, 13]
    for n, at in enumerate(big_at):
        grads[at], deltas[at], new_m[at], new_v[at] = outs_big[n]
    for at, g, dd, mm_, vv in zip(small_at, unrows8(g_small8), unrows8(d8), unrows8(m8), unrows8(v8)):
        grads[at], deltas[at], new_m[at], new_v[at] = g, dd, mm_, vv
    return (loss_all, grad_x[None], *grads, *deltas, *new_m, *new_v)
```

```python
import jax
import jax.numpy as jnp
from jax import lax
from jax.experimental import pallas as pl
from jax.experimental.pallas import tpu as pltpu

F32 = jnp.float32
BF16 = jnp.bfloat16
MESH = pl.DeviceIdType.MESH

EPS = 1e-6
ROPE_THETA = 10000.0
N_HEADS = 8
PAIRS = N_HEADS // 2
HEAD_DIM = 64
ROPE_DIM = 32
LANES = 128
Q_RANK = 384
KV_RANK = 256
N_CHIPS = 4
ADAM_LR, ADAM_B1, ADAM_B2, ADAM_EPS, ADAM_WD, ADAM_STEP = 0.001, 0.9, 0.999, 1e-08, 0.01, 10
VMEM_LIMIT = 48 * 1024 * 1024
LOG2E = 1.4426950408889634
LN2 = 0.6931471805599453
NN = (((1,), (0,)), ((), ()))
NT = (((1,), (1,)), ((), ()))
TN = (((0,), (0,)), ((), ()))


def _params(sem=None):
    return pltpu.CompilerParams(dimension_semantics=sem, vmem_limit_bytes=VMEM_LIMIT)


def _fit(block, dim):
    if dim <= block:
        return dim
    return next(b for b in range(block - block % LANES, 0, -LANES) if dim % b == 0)


def _row_block(rows):
    return next(b for b in (256, 128, 64, 32, 16, 8) if rows % b == 0)


def gridded(body, *, name, grid, in_specs, out_specs, out_shape, ins, semantics, side=None):
    if side is None:
        return pl.pallas_call(body, name=name, grid=grid, in_specs=in_specs, out_specs=out_specs, out_shape=out_shape,
                              compiler_params=_params(semantics))(*ins)
    n_in, n_out = len(in_specs), len(out_specs)
    steps = 1
    for extent in grid:
        steps *= extent

    def riding(*refs):
        main_in, s_in, main_out, s_out, sems = _split_refs(refs, [n_in, len(side.ins), n_out, len(side.out_shapes), 2])
        step = 0
        for axis, extent in enumerate(grid):
            step = step * extent + pl.program_id(axis)

        @pl.when(step == 0)
        def _():
            side.first(s_in, s_out, *sems)

        body(*main_in, *main_out)

        @pl.when(step == steps - 1)
        def _():
            if side.mid is not None:
                side.mid(s_in, s_out, *sems)
            side.last(s_in, s_out, *sems)

    s_in_specs, s_out_specs = side.specs()
    return pl.pallas_call(
        riding, name=name, grid=grid, in_specs=list(in_specs) + s_in_specs, out_specs=list(out_specs) + s_out_specs,
        out_shape=list(out_shape) + side.out_shapes, scratch_shapes=side.sems(),
        compiler_params=_params(("arbitrary",) * len(grid)))(*ins, *side.ins)


def rmsnorm(x, g, *, out_dtype, name, bt=512, side=None):
    t, d = x.shape
    bt = min(bt, t)

    def body(x_ref, g_ref, o_ref):
        xv = x_ref[...].astype(F32)
        r = lax.rsqrt(jnp.mean(xv * xv, axis=-1, keepdims=True) + EPS)
        o_ref[...] = (xv * r * g_ref[...]).astype(o_ref.dtype)

    return gridded(
        body, name=name, grid=(t // bt,),
        in_specs=[pl.BlockSpec((bt, d), lambda i: (i, 0)), pl.BlockSpec((1, d), lambda i: (0, 0))],
        out_specs=[pl.BlockSpec((bt, d), lambda i: (i, 0))],
        out_shape=[jax.ShapeDtypeStruct((t, d), out_dtype)],
        ins=[x, g], semantics=("parallel",), side=side)


def mm(a, b, *, trans_b=False, a_pro=None, extras=(), vecs=(), epilogue=None, out_dtypes, n_sums=0, t_blk=None, name,
       bm=1024, bn=1024, side=None):
    a_list = list(a) if isinstance(a, (list, tuple)) else [a]
    b_list = list(b) if isinstance(b, (list, tuple)) else [b]
    m = a_list[0].shape[0]
    n = b_list[0].shape[0] if trans_b else b_list[0].shape[1]
    ks = [x.shape[1] for x in a_list]
    if sum(ks) > 2048:
        bm = bm // 2
    bm, bn = _fit(bm, m), _fit(bn, n)
    assert n_sums == 0 or bn == n
    n_ab, n_ex, n_vec, n_out = len(a_list), len(extras), len(vecs), len(out_dtypes)
    n_t = 0 if t_blk is None else 1

    def body(*refs):
        a_refs, b_refs, ex, vs, outs, t_outs, sums = _split_refs(refs, [n_ab, n_ab, n_ex, n_vec, n_out, n_t, n_sums])
        acc = None
        for a_ref, b_ref in zip(a_refs, b_refs):
            a_tile = a_ref[...] if a_pro is None else a_pro(a_ref[...])
            part = lax.dot_general(a_tile, b_ref[...], NT if trans_b else NN, preferred_element_type=F32)
            acc = part if acc is None else acc + part
        res = epilogue(acc, *[e[...] for e in ex], *[v[...] for v in vs]) if epilogue is not None else (acc,)
        for o, r in zip(outs, res[:n_out]):
            o[...] = r.astype(o.dtype)
        for t_ref in t_outs:
            for u in range(bm // t_blk):
                t_ref[u] = res[0][u * t_blk:(u + 1) * t_blk, :].T.astype(t_ref.dtype)
        if n_sums:
            @pl.when(pl.program_id(0) == 0)
            def _():
                for s_ref in sums:
                    s_ref[...] = jnp.zeros_like(s_ref)

            for s_ref, r in zip(sums, res[n_out:]):
                s_ref[...] += r

    tile = pl.BlockSpec((bm, bn), lambda i, j: (i, j))
    vec = pl.BlockSpec((1, bn), lambda i, j: (0, j))
    t_specs, t_shapes = [], []
    if t_blk is not None:
        t_specs = [pl.BlockSpec((bm // t_blk, bn, t_blk), lambda i, j: (i, j, 0))]
        t_shapes = [jax.ShapeDtypeStruct((m // t_blk, n, t_blk), out_dtypes[0])]
    a_specs = [pl.BlockSpec((bm, k), lambda i, j: (i, 0)) for k in ks]
    b_specs = [pl.BlockSpec((bn, k), lambda i, j: (j, 0)) if trans_b else pl.BlockSpec((k, bn), lambda i, j: (0, j)) for k in ks]
    return gridded(
        body, name=name, grid=(m // bm, n // bn),
        in_specs=a_specs + b_specs + [tile] * n_ex + [vec] * n_vec,
        out_specs=[tile] * n_out + t_specs + [vec] * n_sums,
        out_shape=[jax.ShapeDtypeStruct((m, n), dt) for dt in out_dtypes] + t_shapes + [jax.ShapeDtypeStruct((1, n), F32)] * n_sums,
        ins=[*a_list, *b_list, *extras, *vecs],
        semantics=("arbitrary", "arbitrary") if n_sums else ("parallel", "parallel"), side=side)


def mm_tn(a, b, *, a_pro=None, name, col_shards=1, bf16_copy=False, bk=1024, bn=1024, bt=1024):
    t, k = a.shape
    n = b.shape[1]
    ns = n // col_shards
    bk, bn, bt = _fit(bk, k), _fit(bn, ns), _fit(bt, t)
    per = ns // bn
    last = t // bt - 1

    def body(a_ref, b_ref, o_ref, *copy_ref):
        @pl.when(pl.program_id(2) == 0)
        def _():
            o_ref[...] = jnp.zeros_like(o_ref)

        a_tile = a_ref[...] if a_pro is None else a_pro(a_ref[...])
        o_ref[...] += lax.dot_general(a_tile, b_ref[...], TN, preferred_element_type=F32)
        if bf16_copy:
            @pl.when(pl.program_id(2) == last)
            def _():
                copy_ref[0][...] = o_ref[...].astype(BF16)

    if col_shards == 1:
        out_spec = pl.BlockSpec((bk, bn), lambda i, j, s: (i, j))
        shape = (k, n)
    else:
        out_spec = pl.BlockSpec((None, bk, bn), lambda i, j, s: (j // per, i, j % per))
        shape = (col_shards, k, ns)
    out_specs, out_shape = out_spec, jax.ShapeDtypeStruct(shape, F32)
    if bf16_copy:
        out_specs, out_shape = [out_spec, out_spec], [out_shape, jax.ShapeDtypeStruct(shape, BF16)]
    return pl.pallas_call(
        body, name=name, grid=(k // bk, n // bn, t // bt),
        in_specs=[pl.BlockSpec((bt, bk), lambda i, j, s: (s, i)), pl.BlockSpec((bt, bn), lambda i, j, s: (s, j))],
        out_specs=out_specs, out_shape=out_shape,
        compiler_params=_params(("parallel", "parallel", "arbitrary")),
    )(a, b)


def _split3(x):
    hi = x.astype(BF16)
    r1 = x - hi.astype(F32)
    mid = r1.astype(BF16)
    lo = (r1 - mid.astype(F32)).astype(BF16)
    return hi, mid, lo


def cumsum_rows(x, *, reverse, name, bc=512):
    t, d = x.shape
    bc = min(bc, t)
    nb = t // bc

    def body(x_ref, o_ref, carry):
        @pl.when(pl.program_id(0) == 0)
        def _():
            carry[...] = jnp.zeros_like(carry)

        r = lax.broadcasted_iota(jnp.int32, (bc, bc), 0)
        c = lax.broadcasted_iota(jnp.int32, (bc, bc), 1)
        tri = jnp.where((r <= c) if reverse else (r >= c), 1.0, 0.0).astype(BF16)
        hi, mid, lo = _split3(x_ref[...])
        s = (lax.dot_general(tri, hi, NN, preferred_element_type=F32)
             + lax.dot_general(tri, mid, NN, preferred_element_type=F32)
             + lax.dot_general(tri, lo, NN, preferred_element_type=F32)) + carry[0:1, :]
        o_ref[...] = s
        carry[0:1, :] = s[0:1, :] if reverse else s[bc - 1:bc, :]

    imap = (lambda i: (nb - 1 - i, 0)) if reverse else (lambda i: (i, 0))
    return pl.pallas_call(
        body, name=name, grid=(nb,),
        in_specs=[pl.BlockSpec((bc, d), imap)], out_specs=pl.BlockSpec((bc, d), imap),
        out_shape=jax.ShapeDtypeStruct((t, d), F32),
        scratch_shapes=[pltpu.VMEM((8, d), F32)],
        compiler_params=_params(("arbitrary",)),
    )(x)


def _rope(x, c, a, b):
    return x * c + pltpu.roll(x, LANES - ROPE_DIM // 2, 1) * a + pltpu.roll(x, ROPE_DIM // 2, 1) * b


def _rope_bwd(d, c, a, b):
    return d * c + pltpu.roll(d * a, ROPE_DIM // 2, 1) + pltpu.roll(d * b, LANES - ROPE_DIM // 2, 1)


S_CQ, S_CKV, S_KR, S_F, S_END = 0, Q_RANK, Q_RANK + KV_RANK, Q_RANK + KV_RANK + LANES, 1024
HW = N_HEADS * LANES
FW = N_HEADS * HEAD_DIM


def mla_prep(small, g_q, g_kv, w_uq, w_ukv, tab_c, tab_a, tab_b, b_f, *, name, bt=512):
    t = small.shape[0]
    bt = min(bt, t)

    def body(s_ref, gq_ref, gkv_ref, wq_ref, wkv_ref, c_ref, a_ref, b_ref, bf_ref,
             mq_ref, mk_ref, mv_ref, lf_ref, cqn_ref, ckvn_ref, mqt_ref, mvt_ref):
        cq = s_ref[:, S_CQ:S_CKV]
        rq = lax.rsqrt(jnp.mean(cq * cq, axis=-1, keepdims=True) + EPS)
        cqn = (cq * rq * gq_ref[...]).astype(BF16)
        ckv = s_ref[:, S_CKV:S_KR]
        rkv = lax.rsqrt(jnp.mean(ckv * ckv, axis=-1, keepdims=True) + EPS)
        ckvn = (ckv * rkv * gkv_ref[...]).astype(BF16)
        cqn_ref[...] = cqn
        ckvn_ref[...] = ckvn
        tc, ta, tb = c_ref[...], a_ref[...], b_ref[...]
        q = jnp.dot(cqn, wq_ref[...], preferred_element_type=F32)
        kv = jnp.dot(ckvn, wkv_ref[...], preferred_element_type=F32)
        kr = _rope(s_ref[:, S_KR:S_F], tc, ta, tb)
        for h in range(N_HEADS):
            sl = slice(h * LANES, (h + 1) * LANES)
            roped = _rope(q[:, sl], tc, ta, tb)
            mq_ref[:, sl] = roped.astype(BF16)
            mqt_ref[0, sl, :] = roped.T.astype(BF16)
            mk_ref[:, sl] = (kv[:, sl] + kr).astype(BF16)
        mv_ref[...] = kv[:, HW:].astype(BF16)
        mvt_ref[0] = kv[:, HW:].T.astype(BF16)
        z = s_ref[:, S_F:S_END - LANES] + bf_ref[...]
        lf_ref[...] = jnp.minimum(z, 0.0) - jnp.log(1.0 + jnp.exp(-jnp.abs(z)))

    def row(w):
        return pl.BlockSpec((bt, w), lambda i: (i, 0))

    def full(arr):
        return pl.BlockSpec(arr.shape, lambda i: (0, 0))

    return pl.pallas_call(
        body, name=name, grid=(t // bt,),
        in_specs=[row(S_END), full(g_q), full(g_kv), full(w_uq), full(w_ukv), row(LANES), row(LANES), row(LANES), full(b_f)],
        out_specs=[row(HW), row(HW), row(FW), row(LANES), row(Q_RANK), row(KV_RANK),
                   pl.BlockSpec((1, HW, bt), lambda i: (i, 0, 0)), pl.BlockSpec((1, FW, bt), lambda i: (i, 0, 0))],
        out_shape=[jax.ShapeDtypeStruct((t, HW), BF16)] * 2 + [jax.ShapeDtypeStruct((t, FW), BF16), jax.ShapeDtypeStruct((t, LANES), F32),
                   jax.ShapeDtypeStruct((t, Q_RANK), BF16), jax.ShapeDtypeStruct((t, KV_RANK), BF16),
                   jax.ShapeDtypeStruct((t // bt, HW, bt), BF16), jax.ShapeDtypeStruct((t // bt, FW, bt), BF16)],
        compiler_params=_params(("parallel",)),
    )(small, g_q, g_kv, w_uq, w_ukv, tab_c, tab_a, tab_b, b_f)


def mla_prep_bwd(dmq, dmk, dmv, dlf, small, g_q, g_kv, w_uq, w_ukv, tab_c, tab_a, tab_b, b_f, *, name, bt=512):
    t = small.shape[0]
    bt = min(bt, t)

    def body(dmq_ref, dmk_ref, dmv_ref, dlf_ref, s_ref, gq_ref, gkv_ref, wq_ref, wkv_ref, c_ref, a_ref, b_ref, bf_ref,
             ds_ref, dq_ref, dkv_ref, dgq_ref, dgkv_ref, db_ref):
        tc, ta, tb = c_ref[...], a_ref[...], b_ref[...]
        lane = lax.broadcasted_iota(jnp.int32, (1, LANES), 1)
        dkr = jnp.zeros((bt, LANES), F32)
        for h in range(N_HEADS):
            sl = slice(h * LANES, (h + 1) * LANES)
            dq_ref[:, sl] = _rope_bwd(dmq_ref[:, sl], tc, ta, tb).astype(BF16)
            dkr = dkr + dmk_ref[:, sl]
        dkv_ref[:, :HW] = dmk_ref[...].astype(BF16)
        dkv_ref[:, HW:] = dmv_ref[...].astype(BF16)
        in_rope = (lane >= HEAD_DIM) & (lane < HEAD_DIM + ROPE_DIM)
        ds_ref[:, S_KR:S_F] = jnp.where(in_rope, _rope_bwd(dkr, tc, ta, tb), 0.0).astype(BF16)

        def norm_bwd(raw, g_ref, dn, dg_ref):
            r = lax.rsqrt(jnp.mean(raw * raw, axis=-1, keepdims=True) + EPS)
            u = dn * g_ref[...]
            dot = jnp.mean(u * raw, axis=-1, keepdims=True)
            dg_ref[...] += jnp.sum(dn * (raw * r), axis=0, keepdims=True)
            return r * u - raw * (r * r * r * dot)

        @pl.when(pl.program_id(0) == 0)
        def _():
            dgq_ref[...] = jnp.zeros_like(dgq_ref)
            dgkv_ref[...] = jnp.zeros_like(dgkv_ref)
            db_ref[...] = jnp.zeros_like(db_ref)

        dcqn = lax.dot_general(dq_ref[...], wq_ref[...], NT, preferred_element_type=F32)
        ds_ref[:, S_CQ:S_CKV] = norm_bwd(s_ref[:, S_CQ:S_CKV], gq_ref, dcqn, dgq_ref).astype(BF16)
        dckvn = lax.dot_general(dkv_ref[...], wkv_ref[...], NT, preferred_element_type=F32)
        ds_ref[:, S_CKV:S_KR] = norm_bwd(s_ref[:, S_CKV:S_KR], gkv_ref, dckvn, dgkv_ref).astype(BF16)
        z = s_ref[:, S_F:S_END - LANES] + bf_ref[...]
        dz = jnp.where(lane < N_HEADS, dlf_ref[...] / (1.0 + jnp.exp(z)), 0.0)
        db_ref[...] += jnp.sum(dz, axis=0, keepdims=True)
        ds_ref[:, S_F:S_END - LANES] = dz.astype(BF16)
        ds_ref[:, S_END - LANES:] = jnp.zeros((bt, LANES), BF16)

    def row(w):
        return pl.BlockSpec((bt, w), lambda i: (i, 0))

    def full(arr):
        return pl.BlockSpec(arr.shape, lambda i: (0, 0))

    def vec(w):
        return pl.BlockSpec((1, w), lambda i: (0, 0))

    return pl.pallas_call(
        body, name=name, grid=(t // bt,),
        in_specs=[row(HW), row(HW), row(FW), row(LANES), row(S_END), full(g_q), full(g_kv), full(w_uq), full(w_ukv),
                  row(LANES), row(LANES), row(LANES), full(b_f)],
        out_specs=[row(S_END), row(HW), row(HW + FW), vec(Q_RANK), vec(KV_RANK), vec(LANES)],
        out_shape=[jax.ShapeDtypeStruct((t, S_END), BF16), jax.ShapeDtypeStruct((t, HW), BF16),
                   jax.ShapeDtypeStruct((t, HW + FW), BF16), jax.ShapeDtypeStruct((1, Q_RANK), F32),
                   jax.ShapeDtypeStruct((1, KV_RANK), F32), jax.ShapeDtypeStruct((1, LANES), F32)],
        compiler_params=_params(("arbitrary",)),
    )(dmq, dmk, dmv, dlf, small, g_q, g_kv, w_uq, w_ukv, tab_c, tab_a, tab_b, b_f)


class Side:
    def __init__(self, ins, out_shapes, n_sems, first, last, mid=None):
        self.ins, self.out_shapes, self.n_sems = list(ins), list(out_shapes), n_sems
        self.first, self.mid, self.last = first, mid, last

    def specs(self):
        return [ANY] * len(self.ins), [ANY] * len(self.out_shapes)

    def sems(self):
        return [pltpu.SemaphoreType.DMA((self.n_sems,)), pltpu.SemaphoreType.DMA((self.n_sems,))]


def _lane():
    return lax.broadcasted_iota(jnp.int32, (1, LANES), 1)


def _halves(x):
    zero = jnp.zeros_like(x)
    return [jnp.where(_lane() < HEAD_DIM, x, zero), jnp.where(_lane() >= HEAD_DIM, x, zero)]


def _groups(x):
    return [x[:, :LANES], x[:, LANES:]]


def _pick_row(tile, h):
    row = lax.broadcasted_iota(jnp.int32, (tile.shape[0], 1), 0)
    return jnp.sum(jnp.where(row == h, tile, 0.0), axis=0, keepdims=True)


def _pick_lane(tile, h):
    return jnp.sum(jnp.where(_lane() == h, tile, 0.0), axis=1, keepdims=True)


def _row_halves(x):
    row = lax.broadcasted_iota(jnp.int32, (LANES, 1), 0)
    zero = jnp.zeros_like(x)
    return [jnp.where(row < HEAD_DIM, x, zero), jnp.where(row >= HEAD_DIM, x, zero)]


def _below_diagonal(s):
    r = lax.broadcasted_iota(jnp.int32, s.shape, 0)
    c = lax.broadcasted_iota(jnp.int32, s.shape, 1)
    return jnp.where(c <= r, s, -jnp.inf)


def _above_diagonal(s):
    r = lax.broadcasted_iota(jnp.int32, s.shape, 0)
    c = lax.broadcasted_iota(jnp.int32, s.shape, 1)
    return jnp.where(r <= c, s, -jnp.inf)


def _split_refs(refs, counts):
    out, at = [], 0
    for n in counts:
        out.append(refs[at:at + n])
        at += n
    return out


def flash_fwd(qt_arr, k_arr, vt_arr, f_cum, *, qoff, koff, voff, pair, scale, name, blk=512, side=None):
    t = k_arr.shape[0]
    tblk = qt_arr.shape[2]
    blk = max(min(blk, t), tblk)
    sub = blk // tblk
    nb = t // blk
    steps = PAIRS * nb
    w = LANES if pair else 2 * LANES
    has_bias = f_cum is not None
    ins = [qt_arr, k_arr, vt_arr] + ([f_cum] if has_bias else [])

    def wide(ref, first):
        parts = [ref[first + u] for u in range(sub)]
        return parts[0] if sub == 1 else jnp.concatenate(parts, axis=1)
    s_ins, s_outs = (side.ins, side.out_shapes) if side else ([], [])

    def body(*refs):
        main, si, outs, so, sems = _split_refs(refs, [len(ins), len(s_ins), 2, len(s_outs), 2 if side else 0])
        qt_ref, k_ref, vt_ref = main[:3]
        f_ref = main[3] if has_bias else None
        o_ref, st_ref = outs
        g, i = pl.program_id(0), pl.program_id(1)
        step_id = g * nb + i
        if side:
            @pl.when(step_id == 0)
            def _():
                side.first(si, so, *sems)

            if side.mid is not None:
                @pl.when(step_id == (3 * steps) // 4)
                def _():
                    side.mid(si, so, *sems)

        qt = (wide(qt_ref, 0).astype(F32) * (scale * LOG2E)).astype(BF16)
        qts = _row_halves(qt) if pair else [qt[:LANES], qt[LANES:]]

        def k_of(kk, n):
            return kk if pair else kk[:, n * LANES:(n + 1) * LANES]

        def with_ones(vt_rows):
            return jnp.concatenate([vt_rows, jnp.ones((ACC_ROWS - HEAD_DIM, vt_rows.shape[1]), BF16)], axis=0)

        def step(j, carry, diagonal):
            rows = pl.ds(pl.multiple_of(j * blk, blk), blk)
            kk = k_ref[rows, :]
            vt = wide(vt_ref, sub * j)
            out = []
            for n in range(2):
                m, acc = carry[n]
                s = jnp.dot(k_of(kk, n), qts[n], preferred_element_type=F32)
                if has_bias:
                    s = s - LOG2E * _pick_lane(f_ref[rows, :], 2 * g + n)
                if diagonal:
                    s = _above_diagonal(s)
                m_new = jnp.maximum(m, jnp.max(s, axis=0, keepdims=True))
                p = jnp.exp2(s - m_new).astype(BF16)
                out.append((m_new, jnp.exp2(m - m_new) * acc
                            + jnp.dot(with_ones(vt[n * HEAD_DIM:(n + 1) * HEAD_DIM]), p, preferred_element_type=F32)))
            return tuple(out)

        def diagonal_in_halves(carry):
            h = tblk
            halves = [pl.ds(pl.multiple_of(i * blk, blk), h), pl.ds(pl.multiple_of(i * blk + h, h), h)]
            k_top, k_bot = k_ref[halves[0], :], k_ref[halves[1], :]
            vt_top, vt_bot = vt_ref[sub * i], vt_ref[sub * i + 1]
            out = []
            for n in range(2):
                m, acc = carry[n]
                heads = slice(n * HEAD_DIM, (n + 1) * HEAD_DIM)
                s_top = jnp.dot(k_of(k_top, n), qts[n], preferred_element_type=F32)
                s_bot = jnp.dot(k_of(k_bot, n), qts[n][:, h:], preferred_element_type=F32)
                if has_bias:
                    s_top = s_top - LOG2E * _pick_lane(f_ref[halves[0], :], 2 * g + n)
                    s_bot = s_bot - LOG2E * _pick_lane(f_ref[halves[1], :], 2 * g + n)
                s_top, s_bot = _above_diagonal(s_top), _above_diagonal(s_bot)
                m_top = jnp.maximum(m, jnp.max(s_top, axis=0, keepdims=True))
                m_new = jnp.concatenate([m_top[:, :h], jnp.maximum(m_top[:, h:], jnp.max(s_bot, axis=0, keepdims=True))], axis=1)
                p_top = jnp.exp2(s_top - m_new).astype(BF16)
                p_bot = jnp.exp2(s_bot - m_new[:, h:]).astype(BF16)
                late = jnp.concatenate([jnp.zeros((ACC_ROWS, h), F32),
                                        jnp.dot(with_ones(vt_bot[heads]), p_bot, preferred_element_type=F32)], axis=1)
                out.append((m_new, jnp.exp2(m - m_new) * acc
                            + jnp.dot(with_ones(vt_top[heads]), p_top, preferred_element_type=F32) + late))
            return tuple(out)

        init = tuple((jnp.full((1, blk), -jnp.inf, F32), jnp.zeros((ACC_ROWS, blk), F32)) for _ in range(2))
        carry = lax.fori_loop(0, i, lambda j, c: step(j, c, False), init)
        (ma, acca), (mb, accb) = diagonal_in_halves(carry) if sub == 2 else step(i, carry, True)
        la, lb = acca[HEAD_DIM:HEAD_DIM + 1], accb[HEAD_DIM:HEAD_DIM + 1]
        o_ref[...] = jnp.concatenate([acca[:HEAD_DIM] / la, accb[:HEAD_DIM] / lb], axis=0).T
        row = lax.broadcasted_iota(jnp.int32, (LANES, 1), 0)
        st_ref[0] = jnp.where(row == 0, ma + jnp.log2(la), jnp.where(row == 1, mb + jnp.log2(lb), 0.0)).T
        if side:
            @pl.when(step_id == steps - 1)
            def _():
                side.last(si, so, *sems)

    in_specs = [pl.BlockSpec((sub, w, tblk), lambda g, i: (i, qoff + g, 0)), pl.BlockSpec((t, w), lambda g, i: (0, koff + g)),
                pl.BlockSpec((t // tblk, LANES, tblk), lambda g, i: (0, voff + g, 0))]
    if has_bias:
        in_specs.append(pl.BlockSpec((t, LANES), lambda g, i: (0, 0)))
    s_in_specs, s_out_specs = side.specs() if side else ([], [])
    return pl.pallas_call(
        body, name=name, grid=(PAIRS, nb), in_specs=in_specs + s_in_specs,
        out_specs=[pl.BlockSpec((blk, LANES), lambda g, i: (i, g)), pl.BlockSpec((1, blk, LANES), lambda g, i: (g, i, 0))]
        + s_out_specs,
        out_shape=[jax.ShapeDtypeStruct((t, PAIRS * LANES), F32), jax.ShapeDtypeStruct((PAIRS, t, LANES), F32)] + list(s_outs),
        scratch_shapes=side.sems() if side else [],
        compiler_params=_params(("arbitrary", "arbitrary")),
    )(*ins, *s_ins)


def mix_norm(fo, mo, g_fo, g_mo, *, name, bt=512):
    t, d = fo.shape
    bt = min(bt, t)

    def body(fo_ref, mo_ref, gf_ref, gm_ref, o_ref):
        for n, (x_ref, g_ref) in enumerate(((fo_ref, gf_ref), (mo_ref, gm_ref))):
            xv = x_ref[...]
            r = lax.rsqrt(jnp.mean(xv * xv, axis=-1, keepdims=True) + EPS)
            o_ref[:, n * d:(n + 1) * d] = (xv * r * g_ref[...]).astype(BF16)

    row = pl.BlockSpec((bt, d), lambda i: (i, 0))
    vec = pl.BlockSpec((1, d), lambda i: (0, 0))
    return pl.pallas_call(
        body, name=name, grid=(t // bt,), in_specs=[row, row, vec, vec],
        out_specs=pl.BlockSpec((bt, 2 * d), lambda i: (i, 0)),
        out_shape=jax.ShapeDtypeStruct((t, 2 * d), BF16),
        compiler_params=_params(("parallel",)),
    )(fo, mo, g_fo, g_mo)


def mix_norm_bwd(dx1b, w_o, fo, mo, g_fo, g_mo, st_f, st_m, *, name, bt=512, side=None):
    t, d = fo.shape
    bt = min(bt, t)

    def body(dx_in_ref, w_ref, fo_ref, mo_ref, gf_ref, gm_ref, sf_ref, sm_ref, dfo_ref, dmo_ref, dgf_ref, dgm_ref,
             sfo_ref, smo_ref, dfot_ref, dmot_ref):
        @pl.when(pl.program_id(0) == 0)
        def _():
            dgf_ref[...] = jnp.zeros_like(dgf_ref)
            dgm_ref[...] = jnp.zeros_like(dgm_ref)

        dmixed = lax.dot_general(dx_in_ref[...], w_ref[...], NT, preferred_element_type=F32)
        groups = ((fo_ref, gf_ref, dfo_ref, dgf_ref, sf_ref, sfo_ref, dfot_ref),
                  (mo_ref, gm_ref, dmo_ref, dgm_ref, sm_ref, smo_ref, dmot_ref))
        for n, (x_ref, g_ref, dx_ref, dg_ref, st_ref, sto_ref, dxt_ref) in enumerate(groups):
            xv = x_ref[...]
            dhv = dmixed[:, n * d:(n + 1) * d]
            r = lax.rsqrt(jnp.mean(xv * xv, axis=-1, keepdims=True) + EPS)
            u = dhv * g_ref[...]
            dxf = r * u - xv * (r * r * r * jnp.mean(u * xv, axis=-1, keepdims=True))
            dxb = dxf.astype(BF16)
            dx_ref[...] = dxb
            dxt_ref[0] = dxf.T.astype(BF16)
            dg_ref[...] += jnp.sum(dhv * (xv * r), axis=0, keepdims=True)
            prod = xv * dxb.astype(F32)
            for g in range(PAIRS):
                grp = prod[:, g * LANES:(g + 1) * LANES]
                da = jnp.sum(jnp.where(_lane() < HEAD_DIM, grp, 0.0), axis=1, keepdims=True)
                db = jnp.sum(jnp.where(_lane() >= HEAD_DIM, grp, 0.0), axis=1, keepdims=True)
                sto_ref[g] = jnp.where(_lane() == 2, da, jnp.where(_lane() == 3, db, st_ref[g]))

    row = pl.BlockSpec((bt, d), lambda i: (i, 0))
    vec = pl.BlockSpec((1, d), lambda i: (0, 0))
    stat = pl.BlockSpec((PAIRS, bt, LANES), lambda i: (0, i, 0))
    return gridded(
        body, name=name, grid=(t // bt,),
        in_specs=[pl.BlockSpec((bt, dx1b.shape[1]), lambda i: (i, 0)), pl.BlockSpec(w_o.shape, lambda i: (0, 0)),
                  row, row, vec, vec, stat, stat],
        out_specs=[row, row, vec, vec, stat, stat] + [pl.BlockSpec((1, d, bt), lambda i: (i, 0, 0))] * 2,
        out_shape=[jax.ShapeDtypeStruct((t, d), BF16)] * 2 + [jax.ShapeDtypeStruct((1, d), F32)] * 2
        + [jax.ShapeDtypeStruct(st_f.shape, F32)] * 2 + [jax.ShapeDtypeStruct((t // bt, d, bt), BF16)] * 2,
        ins=[dx1b, w_o, fo, mo, g_fo, g_mo, st_f, st_m], semantics=("arbitrary",), side=side)


def flash_bwd(q_arr, qt_arr, k_arr, v_arr, do_arr, dot_arr, st, f_blocks, *, qoff, koff, voff, pair, scale, name, qblk=1024,
              side=None):
    t = q_arr.shape[0]
    blk = qt_arr.shape[2]
    qblk = max(min(qblk, t), blk)
    sub = qblk // blk
    nb, nbq = t // blk, t // qblk
    w = LANES if pair else 2 * LANES
    hw = w // 2
    has_bias = f_blocks is not None
    split = _halves if pair else _groups
    ins = [q_arr, qt_arr, k_arr, v_arr, do_arr, dot_arr, st] + ([f_blocks] if has_bias else [])
    n_out = 4 if has_bias else 3
    s_ins, s_outs = (side.ins, side.out_shapes) if side else ([], [])

    def wide(ref, first, count):
        parts = [ref[first + u] for u in range(count)]
        return parts[0] if count == 1 else jnp.concatenate(parts, axis=1)

    def body(*refs):
        main, si, outs, so, sems = _split_refs(refs, [len(ins), len(s_ins), n_out, len(s_outs), 2 if side else 0])
        q_ref, qt_ref, k_ref, v_ref, do_ref, dot_ref, st_ref = main[:7]
        dq_ref, dk_ref, dv_ref = outs[:3]
        g, j = pl.program_id(0), pl.program_id(1)
        step_id = g * nb + j
        if side:
            @pl.when(step_id == 0)
            def _():
                side.first(si, so, *sems)

        @pl.when(j == 0)
        def _():
            dq_ref[...] = jnp.zeros_like(dq_ref)

        kk, vv = k_ref[...], v_ref[...]
        ks = [kk, kk] if pair else _groups(kk)
        if has_bias:
            f_ref, df_ref = main[7], outs[3]
            fk = [LOG2E * _pick_row(f_ref[0], 2 * g + n) for n in range(2)]

            @pl.when(step_id == 0)
            def _():
                df_ref[...] = jnp.zeros_like(df_ref)

        def step(tb, count, carry, diagonal):
            rows = pl.ds(pl.multiple_of(tb * blk, blk), count * blk)
            qs = split((q_ref[rows, :].astype(F32) * (scale * LOG2E)).astype(BF16))
            qt = (wide(qt_ref, tb, count).astype(F32) * (scale * LOG2E)).astype(BF16)
            dos = [h.astype(BF16) for h in _halves(do_ref[rows, :].astype(F32))]
            dot = wide(dot_ref, tb, count)
            stats = st_ref[0, rows, :]
            new, dqs, row_sums = [], [], []
            for n in range(2):
                dkt, dvt, dfk = carry[n]
                s = lax.dot_general(qs[n], ks[n], NT, preferred_element_type=F32)
                if has_bias:
                    s = s - fk[n]
                if diagonal:
                    s = _below_diagonal(s)
                p = jnp.exp2(s - stats[:, n:n + 1])
                dp = lax.dot_general(dos[n], vv, NT, preferred_element_type=F32)
                ds = p * (dp - stats[:, 2 + n:3 + n])
                dsb = ds.astype(BF16)
                dvt = dvt + jnp.dot(dot[n * HEAD_DIM:(n + 1) * HEAD_DIM], p.astype(BF16), preferred_element_type=F32)
                dkt = dkt + jnp.dot(qt[n * hw:(n + 1) * hw], dsb, preferred_element_type=F32)
                dqs.append(jnp.dot(dsb, ks[n], preferred_element_type=F32))
                if has_bias:
                    dfk = dfk - jnp.sum(ds, axis=0, keepdims=True)
                    row_sums.append(jnp.sum(ds, axis=1, keepdims=True))
                new.append((dkt, dvt, dfk))
            dq = (jnp.where(_lane() < HEAD_DIM, dqs[0], dqs[1]) if pair else jnp.concatenate(dqs, axis=1)) * scale
            if has_bias:
                df_ref[rows, :] += jnp.where(_lane() == 2 * g, row_sums[0], jnp.where(_lane() == 2 * g + 1, row_sums[1], 0.0))
            dq_ref[rows, :] += dq
            return tuple(new)

        init = tuple((jnp.zeros((hw, blk), F32), jnp.zeros((HEAD_DIM, blk), F32), jnp.zeros((1, blk), F32)) for _ in range(2))
        carry = step(j, 1, init, True)
        whole = j // sub + 1
        carry = lax.fori_loop(j + 1, whole * sub, lambda tb, c: step(tb, 1, c, False), carry)
        (dka, dva, dfa), (dkb, dvb, dfb) = lax.fori_loop(whole, nbq, lambda i, c: step(i * sub, sub, c, False), carry)
        dk_ref[...] = jnp.concatenate([dka, dkb], axis=0).T * LN2
        dv_ref[...] = jnp.concatenate([dva, dvb], axis=0).T
        if has_bias:
            row = lax.broadcasted_iota(jnp.int32, (LANES, 1), 0)
            by_head = jnp.where(row == 2 * g, dfa, jnp.where(row == 2 * g + 1, dfb, 0.0))
            df_ref[pl.ds(pl.multiple_of(j * blk, blk), blk), :] += by_head.T
        if side:
            @pl.when(step_id == PAIRS * nb - 1)
            def _():
                side.last(si, so, *sems)

    in_specs = [pl.BlockSpec((t, w), lambda g, j: (0, qoff + g)), pl.BlockSpec((nb, w, blk), lambda g, j: (0, qoff + g, 0)),
                pl.BlockSpec((blk, w), lambda g, j: (j, koff + g)), pl.BlockSpec((blk, LANES), lambda g, j: (j, voff + g)),
                pl.BlockSpec((t, LANES), lambda g, j: (0, g)), pl.BlockSpec((nb, LANES, blk), lambda g, j: (0, g, 0)),
                pl.BlockSpec((1, t, LANES), lambda g, j: (g, 0, 0))]
    out_specs = [pl.BlockSpec((t, w), lambda g, j: (0, g)), pl.BlockSpec((blk, w), lambda g, j: (j, g)),
                 pl.BlockSpec((blk, LANES), lambda g, j: (j, g))]
    out_shape = [jax.ShapeDtypeStruct((t, PAIRS * w), F32)] * 2 + [jax.ShapeDtypeStruct((t, PAIRS * LANES), F32)]
    if has_bias:
        in_specs.append(pl.BlockSpec((1, N_HEADS, blk), lambda g, j: (j, 0, 0)))
        out_specs.append(pl.BlockSpec((t, LANES), lambda g, j: (0, 0)))
        out_shape.append(jax.ShapeDtypeStruct((t, LANES), F32))
    s_in_specs, s_out_specs = side.specs() if side else ([], [])
    return pl.pallas_call(
        body, name=name, grid=(PAIRS, nb), in_specs=in_specs + s_in_specs, out_specs=out_specs + s_out_specs,
        out_shape=out_shape + list(s_outs), scratch_shapes=side.sems() if side else [],
        compiler_params=_params(("arbitrary", "arbitrary")),
    )(*ins, *s_ins)


def _adamw_math(w, g, m, v):
    nm = ADAM_B1 * m + (1.0 - ADAM_B1) * g
    nv = ADAM_B2 * v + (1.0 - ADAM_B2) * (g * g)
    m_hat = nm / (1.0 - ADAM_B1 ** ADAM_STEP)
    v_hat = nv / (1.0 - ADAM_B2 ** ADAM_STEP)
    return -ADAM_LR * (m_hat / (jnp.sqrt(v_hat) + ADAM_EPS) + ADAM_WD * w), nm, nv


def adamw(w, g, m, v, *, name):
    rws, cols = w.shape
    br = _row_block(rws)

    def body(w_ref, g_ref, m_ref, v_ref, d_ref, nm_ref, nv_ref):
        d_ref[...], nm_ref[...], nv_ref[...] = _adamw_math(w_ref[...], g_ref[...], m_ref[...], v_ref[...])

    blk = pl.BlockSpec((br, cols), lambda i: (i, 0))
    return pl.pallas_call(
        body, name=name, grid=(rws // br,), in_specs=[blk] * 4, out_specs=[blk] * 3,
        out_shape=[jax.ShapeDtypeStruct((rws, cols), F32)] * 3,
        compiler_params=_params(("parallel",)),
    )(w, g, m, v)


def adamw_halves(w, g_mine, g_other, m, v, core, *, name):
    _, k, n = w.shape
    br = _row_block(k // 2)
    nh = k // 2 // br

    def body(c_ref, w_ref, gm_ref, go_ref, m_ref, v_ref, g_out, d_ref, nm_ref, nv_ref):
        gv = jnp.where(pl.program_id(0) == c_ref[0], gm_ref[...], go_ref[...])
        g_out[0] = gv
        d_ref[0], nm_ref[0], nv_ref[0] = _adamw_math(w_ref[0], gv, m_ref[0], v_ref[0])

    full = pl.BlockSpec((1, br, n), lambda hb, i, c: (0, hb * nh + i, 0))
    half = pl.BlockSpec((br, n), lambda hb, i, c: (i, 0))
    return pl.pallas_call(
        body, name=name,
        grid_spec=pltpu.PrefetchScalarGridSpec(num_scalar_prefetch=1, grid=(2, nh), in_specs=[full, half, half, full, full],
                                               out_specs=[full] * 4),
        out_shape=[jax.ShapeDtypeStruct(w.shape, F32)] * 4,
        compiler_params=_params(("parallel", "parallel")),
    )(core, w, g_mine, g_other, m, v)


def adamw_halves_t(wt, gt_mine, gt_other, mt, vt, core, *, name, bc=128):
    n, k = wt.shape
    nh = k // 2 // bc

    def body(c_ref, w_ref, gm_ref, go_ref, m_ref, v_ref, g_out, d_ref, nm_ref, nv_ref):
        gv = jnp.where(pl.program_id(0) == c_ref[0], gm_ref[...], go_ref[...])
        g_out[...] = gv
        d_ref[...], nm_ref[...], nv_ref[...] = _adamw_math(w_ref[...], gv, m_ref[...], v_ref[...])

    full = pl.BlockSpec((n, bc), lambda hb, i, c: (0, hb * nh + i))
    half = pl.BlockSpec((n, bc), lambda hb, i, c: (0, i))
    return pl.pallas_call(
        body, name=name,
        grid_spec=pltpu.PrefetchScalarGridSpec(num_scalar_prefetch=1, grid=(2, nh), in_specs=[full, half, half, full, full],
                                               out_specs=[full] * 4),
        out_shape=[jax.ShapeDtypeStruct(wt.shape, F32)] * 4,
        compiler_params=_params(("parallel", "parallel")),
    )(core, wt, gt_mine, gt_other, mt, vt)


def add_pair(dw, recv, core, *, name):
    n4, k, n = dw.shape
    half = (1, k // 2, n) if split_axis(k) == 0 else (1, k, n // 2)
    mine = (lambda q, c: (q, c[0], 0)) if split_axis(k) == 0 else (lambda q, c: (q, 0, c[0]))

    def body(c_ref, a_ref, b_ref, o_ref):
        o_ref[...] = (a_ref[...] + b_ref[...].astype(F32)).astype(BF16)

    return pl.pallas_call(
        body, name=name,
        grid_spec=pltpu.PrefetchScalarGridSpec(
            num_scalar_prefetch=1, grid=(n4,),
            in_specs=[pl.BlockSpec(half, mine), pl.BlockSpec(half, lambda q, c: (q, 0, 0))],
            out_specs=pl.BlockSpec(half, lambda q, c: (q, 0, 0))),
        out_shape=jax.ShapeDtypeStruct((n4,) + half[1:], BF16),
        compiler_params=_params(("parallel",)),
    )(core, dw, recv)


def sum_chips(parts, *, name):
    n4, r, n = parts.shape
    if r % 16 == 0:
        br, bc = _row_block(r), n
    else:
        br, bc = r, LANES

    def body(p_ref, o_ref):
        acc = p_ref[0].astype(F32)
        for q in range(1, n4):
            acc = acc + p_ref[q].astype(F32)
        o_ref[...] = acc

    return pl.pallas_call(
        body, name=name, grid=(r // br, n // bc),
        in_specs=[pl.BlockSpec((n4, br, bc), lambda i, j: (0, i, j))], out_specs=pl.BlockSpec((br, bc), lambda i, j: (i, j)),
        out_shape=jax.ShapeDtypeStruct((r, n), F32),
        compiler_params=_params(("parallel", "parallel")),
    )(parts)


ANY = pl.BlockSpec(memory_space=pl.ANY)


def _place():
    x, y, c = lax.axis_index("x"), lax.axis_index("y"), lax.axis_index("c")
    chips = [(1 - x, y), (x, 1 - y), (1 - x, 1 - y)]
    return x, y, c, chips


def _copy(src, dst, send_sems, recv_sems, k, to):
    return pltpu.make_async_remote_copy(src_ref=src, dst_ref=dst, send_sem=send_sems.at[k], recv_sem=recv_sems.at[k],
                                        device_id=to, device_id_type=MESH)


def split_axis(rows):
    return 0 if rows % 32 == 0 else 1


def _half(ref, lead, hf):
    rows, cols = ref.shape[-2:]
    if split_axis(rows) == 0:
        at = (pl.ds(hf * (rows // 2), rows // 2), slice(None))
    else:
        at = (slice(None), pl.ds(hf * (cols // 2), cols // 2))
    return ref.at[at] if lead is None else ref.at[(lead,) + at]


def _gather_first(srcs, dsts, ssems, rsems):
    x, y, c, chips = _place()
    for ti, (s, d) in enumerate(zip(srcs, dsts)):
        for j, (cx, cy) in enumerate(chips):
            _copy(_half(s, None, c), _half(d, 2 * x + y, c), ssems, rsems, 3 * ti + j, (cx, cy, c)).start()


def _gather_mid(srcs, dsts, ssems, rsems):
    x, y, c, chips = _place()
    n1 = 3 * len(srcs)
    for ti, d in enumerate(dsts):
        for j, (cx, cy) in enumerate(chips):
            landed = _half(d, 2 * cx + cy, c)
            _copy(landed, landed, ssems, rsems, 3 * ti + j, (cx, cy, c)).wait_recv()
            _copy(landed, landed, ssems, rsems, n1 + 3 * ti + j, (x, y, 1 - c)).start()


def _gather_last(srcs, dsts, ssems, rsems):
    x, y, c, chips = _place()
    n1 = 3 * len(srcs)
    for ti, (s, d) in enumerate(zip(srcs, dsts)):
        for j, (cx, cy) in enumerate(chips):
            other = _half(d, 2 * cx + cy, 1 - c)
            _copy(other, other, ssems, rsems, n1 + 3 * ti + j, (x, y, 1 - c)).wait_recv()
        for j, (cx, cy) in enumerate(chips):
            mine = _half(s, None, c)
            _copy(mine, mine, ssems, rsems, 3 * ti + j, (cx, cy, c)).wait_send()
            _copy(mine, mine, ssems, rsems, n1 + 3 * ti + j, (x, y, 1 - c)).wait_send()


def gather_side(shards):
    return Side(shards, [jax.ShapeDtypeStruct((N_CHIPS,) + s.shape, s.dtype) for s in shards], 6 * len(shards),
                _gather_first, _gather_last, _gather_mid)


def _scatter_first(srcs, dsts, ssems, rsems):
    x, y, c, chips = _place()
    for ti, (s, d) in enumerate(zip(srcs, dsts)):
        for j, (cx, cy) in enumerate(chips):
            _copy(s.at[2 * cx + cy], d.at[2 * x + y], ssems, rsems, 3 * ti + j, (cx, cy, c)).start()


def _scatter_last(srcs, dsts, ssems, rsems):
    x, y, c, chips = _place()
    for ti, (s, d) in enumerate(zip(srcs, dsts)):
        for j, (cx, cy) in enumerate(chips):
            _copy(s.at[2 * cx + cy], d.at[2 * cx + cy], ssems, rsems, 3 * ti + j, (cx, cy, c)).wait_recv()
        for j, (cx, cy) in enumerate(chips):
            _copy(s.at[2 * cx + cy], d.at[2 * cx + cy], ssems, rsems, 3 * ti + j, (cx, cy, c)).wait_send()


def scatter_side(parts):
    return Side(parts, [jax.ShapeDtypeStruct(p.shape, p.dtype) for p in parts], 3 * len(parts), _scatter_first, _scatter_last)


def run_side(side, *, name):
    n_in, n_out = len(side.ins), len(side.out_shapes)

    def body(*refs):
        si, so, sems = _split_refs(refs, [n_in, n_out, 2])
        side.first(si, so, *sems)
        if side.mid is not None:
            side.mid(si, so, *sems)
        side.last(si, so, *sems)

    in_specs, out_specs = side.specs()
    return pl.pallas_call(body, name=name, in_specs=in_specs, out_specs=out_specs, out_shape=side.out_shapes,
                          scratch_shapes=side.sems())(*side.ins)


def _swap_first(srcs, dsts, ssems, rsems):
    x, y, c, _ = _place()
    for k, (s, d) in enumerate(zip(srcs, dsts)):
        _copy(s, d, ssems, rsems, k, (x, y, 1 - c)).start()


def _swap_last(srcs, dsts, ssems, rsems):
    x, y, c, _ = _place()
    for k, (s, d) in enumerate(zip(srcs, dsts)):
        _copy(s, d, ssems, rsems, k, (x, y, 1 - c)).wait()


def swap_side(xs):
    return Side(xs, [jax.ShapeDtypeStruct(a.shape, a.dtype) for a in xs], len(xs), _swap_first, _swap_last)


def _swap_halves(srcs, dsts, ssems, rsems):
    x, y, c, _ = _place()
    for k, (s, d) in enumerate(zip(srcs, dsts)):
        hk = s.shape[1] // 2
        yield _copy(s.at[:, pl.ds((1 - c) * hk, hk), :], d, ssems, rsems, k, (x, y, 1 - c))


def _swap_halves_first(srcs, dsts, ssems, rsems):
    for cp in _swap_halves(srcs, dsts, ssems, rsems):
        cp.start()


def _swap_halves_last(srcs, dsts, ssems, rsems):
    for cp in _swap_halves(srcs, dsts, ssems, rsems):
        cp.wait()


def swap_halves_side(xs):
    return Side(xs, [jax.ShapeDtypeStruct((a.shape[0], a.shape[1] // 2, a.shape[2]), a.dtype) for a in xs], len(xs),
                _swap_halves_first, _swap_halves_last)


def allreduce_small(s):
    n_dev = 8

    def body(s_ref, out_ref, buf, send_sems, recv_sems):
        x, y, c, _ = _place()
        me = 4 * x + 2 * y + c
        buf[me] = s_ref[...]
        sends = []
        for k in range(1, n_dev):
            px = 1 - x if k & 4 else x
            py = 1 - y if k & 2 else y
            pc = 1 - c if k & 1 else c
            cp = _copy(s_ref, buf.at[me], send_sems, recv_sems, k - 1, (px, py, pc))
            cp.start()
            sends.append((cp, 4 * px + 2 * py + pc))
        for k, (cp, peer) in enumerate(sends):
            _copy(s_ref, buf.at[peer], send_sems, recv_sems, k, (x, y, c)).wait_recv()
        for cp, _ in sends:
            cp.wait_send()
        acc = buf[0]
        for d in range(1, n_dev):
            acc = acc + buf[d]
        out_ref[...] = acc

    vm = pl.BlockSpec(memory_space=pltpu.VMEM)
    return pl.pallas_call(
        body, name="allreduce_small", in_specs=[vm], out_specs=vm,
        out_shape=jax.ShapeDtypeStruct(s.shape, F32),
        scratch_shapes=[pltpu.VMEM((n_dev,) + s.shape, F32), pltpu.SemaphoreType.DMA((n_dev - 1,)),
                        pltpu.SemaphoreType.DMA((n_dev - 1,))],
    )(s)


def join_cols(sm):
    n4, k, n = sm.shape
    return sm.transpose(1, 0, 2).reshape(k, n4 * n)


def split_cols(full):
    k, n = full.shape
    return full.reshape(k, N_CHIPS, n // N_CHIPS).transpose(1, 0, 2)


def _pad_heads(w, width):
    lead, heads = w.shape[:-1], w.shape[-1] // width
    w = w.reshape(lead + (heads, width))
    return jnp.pad(w, [(0, 0)] * len(lead) + [(0, 0), (0, LANES - width)]).reshape(lead + (heads * LANES,))


def _unpad_heads(w, width):
    lead, heads = w.shape[:-1], w.shape[-1] // LANES
    return w.reshape(lead + (heads, LANES))[..., :width].reshape(lead + (heads * width,))


O_F = 3 * FW
O_CQ = O_F + N_HEADS
O_CKV = O_CQ + Q_RANK
O_KR = O_CKV + KV_RANK
O_END = O_KR + ROPE_DIM


def _shard_rows(sm, a, b):
    r = sm.shape[1]
    out = []
    while a < b:
        q = a // r
        e = min(b, (q + 1) * r)
        out.append(sm[q, a - q * r:e - q * r])
        a = e
    return out


def split_w_in_t(w_sm):
    d = w_sm.shape[2]

    def z(n):
        return [jnp.zeros((n, d), w_sm.dtype)]

    small = (_shard_rows(w_sm, O_CQ, O_CKV) + _shard_rows(w_sm, O_CKV, O_KR) + z(HEAD_DIM) + _shard_rows(w_sm, O_KR, O_END)
             + z(LANES - HEAD_DIM - ROPE_DIM) + _shard_rows(w_sm, O_F, O_CQ) + z(LANES - N_HEADS) + z(LANES))
    return jnp.concatenate(_shard_rows(w_sm, 0, O_F), axis=0), jnp.concatenate(small, axis=0)


def join_w_in_t(d_qkv_t, d_small_t):
    kr = S_KR + HEAD_DIM
    segments = [(d_qkv_t, 0, 0, O_F), (d_small_t, S_F, O_F, N_HEADS), (d_small_t, S_CQ, O_CQ, Q_RANK),
                (d_small_t, S_CKV, O_CKV, KV_RANK), (d_small_t, kr, O_KR, ROPE_DIM)]
    r = O_END // N_CHIPS
    shards = []
    for q in range(N_CHIPS):
        pieces = []
        for src, s0, v0, n in segments:
            a, b = max(v0, q * r), min(v0 + n, (q + 1) * r)
            if a < b:
                pieces.append(src[s0 + a - v0:s0 + b - v0])
        shards.append(jnp.concatenate(pieces, axis=0))
    return jnp.stack(shards)


def rope_tables(pos):
    t = pos.shape[0]
    inv_freq = ROPE_THETA ** (-jnp.arange(0, ROPE_DIM, 2, dtype=F32) / ROPE_DIM)
    ang = pos.astype(F32)[:, None] * inv_freq
    cos, sin = jnp.cos(ang), jnp.sin(ang)
    half = ROPE_DIM // 2

    def z(n):
        return jnp.zeros((t, n), F32)

    tab_c = jnp.concatenate([jnp.ones((t, HEAD_DIM), F32), cos, cos, z(LANES - HEAD_DIM - ROPE_DIM)], axis=1)
    tab_a = jnp.concatenate([z(HEAD_DIM), -sin, z(half), z(LANES - HEAD_DIM - ROPE_DIM)], axis=1)
    tab_b = jnp.concatenate([z(HEAD_DIM), z(half), sin, z(LANES - HEAD_DIM - ROPE_DIM)], axis=1)
    return tab_c, tab_a, tab_b


def _pad_lanes(v, n):
    return jnp.pad(v, ((0, 0), (0, n - v.shape[1])))


ATTN_BLK = 512
ATTN_FWD_BLK = 1024
ATTN_BWD_QBLK = 1024
ACC_ROWS = HEAD_DIM + 16


def local_step(xs, pos, tgt, gains, early_weights, late_weights, early_side=None, fwd_sides=(None, None), reduction=None):
    g_attn, b_forget, g_q, g_kv, g_fo, g_mo, g_mlp, g_fin = gains
    t = xs.shape[0]
    blk = min(ATTN_BLK, t)
    fox_scale = 1.0 / (HEAD_DIM ** 0.5)
    mla_scale = 1.0 / ((HEAD_DIM + ROPE_DIM) ** 0.5)

    h1, *gathered = rmsnorm(xs, g_attn, out_dtype=BF16, name="norm_attn", side=early_side)
    w_in_t, w_uq_p, w_ukv = early_weights(gathered)
    w_qkv_t, w_small_t = split_w_in_t(w_in_t)
    kv = w_ukv.reshape(KV_RANK, N_HEADS, 2 * HEAD_DIM)
    w_ukv_p = jnp.concatenate([_pad_heads(kv[:, :, :HEAD_DIM].reshape(KV_RANK, FW), HEAD_DIM),
                               kv[:, :, HEAD_DIM:].reshape(KV_RANK, FW)], axis=1)
    b_f = _pad_lanes(b_forget, LANES)
    tab_c, tab_a, tab_b = rope_tables(pos)

    qkv, qkv_t = mm(h1, w_qkv_t, trans_b=True, out_dtypes=[BF16], t_blk=blk, name="proj_qkv")
    small, = mm(h1, w_small_t, trans_b=True, out_dtypes=[F32], name="proj_small")
    mq, mk, mv, lf, cqn, ckvn, mq_t, mv_t = mla_prep(small, g_q, g_kv, w_uq_p, w_ukv_p, tab_c, tab_a, tab_b, b_f,
                                                     name="mla_prep", bt=blk)
    f_cum = cumsum_rows(lf, reverse=False, name="gate_cumsum")
    f_blocks = f_cum[:, :N_HEADS].reshape(t // blk, blk, N_HEADS).transpose(0, 2, 1)
    fo, st_f, *gathered = flash_fwd(qkv_t, qkv, qkv_t, f_cum, qoff=0, koff=PAIRS, voff=2 * PAIRS, pair=True,
                                    scale=fox_scale, name="fox_fwd", blk=ATTN_FWD_BLK, side=fwd_sides[0])
    mo, st_m, *more = flash_fwd(mq_t, mk, mv_t, None, qoff=0, koff=0, voff=0, pair=False, scale=mla_scale, name="mla_fwd",
                                blk=ATTN_FWD_BLK, side=fwd_sides[1])
    w_o, w_up, w_down = late_weights(gathered + more)
    mixed = mix_norm(fo, mo, g_fo, g_mo, name="norm_mix")

    def inv_rms(v):
        return lax.rsqrt(jnp.mean(v * v, axis=-1, keepdims=True) + EPS)

    def residual_then_norm(acc, res, g):
        xn = acc + res
        return xn, xn * inv_rms(xn) * g

    def norm_bwd(dh, xn, res, g):
        r = inv_rms(xn)
        uu = dh * g
        return (r * uu - xn * (r * r * r * jnp.mean(uu * xn, axis=-1, keepdims=True)) + res,
                jnp.sum(dh * (xn * r), axis=0, keepdims=True))

    def norm_bwd2(dh, xn, res, g):
        dx, dg = norm_bwd(dh, xn, res, g)
        return dx, dx, dg

    def residual_then_loss(acc, res, target, g):
        xn = acc + res
        r = inv_rms(xn)
        xh = xn * r
        e = xh * g - target
        part = 0.5 * jnp.sum(jnp.mean(e * e, axis=-1, keepdims=True), axis=0, keepdims=True)
        dy = e * (1.0 / xn.shape[1])
        uu = dy * g
        dx = r * uu - xn * (r * r * r * jnp.mean(uu * xn, axis=-1, keepdims=True))
        return dx, dx, jnp.sum(dy * xh, axis=0, keepdims=True), part + jnp.zeros_like(g)

    x1, h2 = mm(mixed, w_o, extras=[xs], vecs=[g_mlp], epilogue=residual_then_norm, out_dtypes=[F32, BF16], name="out_proj")

    def relu2(uu):
        r = jnp.maximum(uu.astype(F32), 0.0)
        return (r * r).astype(BF16)

    u, = mm(h2, w_up, out_dtypes=[BF16], name="mlp_up")
    dx2, dx2b, dg_fin, loss_row = mm(u, w_down, a_pro=relu2, extras=[x1, tgt], vecs=[g_fin], epilogue=residual_then_loss,
                                     out_dtypes=[F32, BF16], n_sums=2, name="mlp_down_loss")
    loss = loss_row[:, :1]

    def relu2_grad(acc, uu):
        return (acc * (2.0 * jnp.maximum(uu.astype(F32), 0.0)),)

    du, = mm(dx2b, w_down, trans_b=True, extras=[u], epilogue=relu2_grad, out_dtypes=[BF16], name="mlp_down_bwd")
    dw_down, dw_down_b = (g.reshape(N_CHIPS, -1, w_down.shape[1])
                          for g in mm_tn(u, dx2b, a_pro=relu2, name="dw_down", bf16_copy=True))
    dx1, dx1b, dg_mlp = mm(du, w_up, trans_b=True, extras=[x1, dx2], vecs=[g_mlp], epilogue=norm_bwd2,
                           out_dtypes=[F32, BF16], n_sums=1, name="mlp_up_bwd")
    dw_up, dw_up_b = mm_tn(h2, du, name="dw_up", col_shards=N_CHIPS, bf16_copy=True)

    dw_o, dw_o_b = (g.reshape(N_CHIPS, -1, w_o.shape[1]) for g in mm_tn(mixed, dx1b, name="dw_o", bf16_copy=True))
    late, late_b = (dw_o, dw_up, dw_down), (dw_o_b, dw_up_b, dw_down_b)
    red = reduction
    dfo, dmo, dg_fo, dg_mo, st_f, st_m, dfo_t, dmo_t, *got = mix_norm_bwd(
        dx1b, w_o, fo, mo, g_fo, g_mo, st_f, st_m, name="out_proj_mix_bwd", bt=blk,
        side=red.late_swap(late, late_b) if red else None)
    dfq, dfk, dfv, d_f, *got = flash_bwd(
        qkv, qkv_t, qkv, qkv, dfo, dfo_t, st_f, f_blocks, qoff=0, koff=PAIRS, voff=2 * PAIRS, pair=True,
        scale=fox_scale, name="fox_bwd", qblk=ATTN_BWD_QBLK, side=red.late_scatter(got) if red else None)
    dmq, dmk, dmv, *got = flash_bwd(mq, mq_t, mk, mv, dmo, dmo_t, st_m, None, qoff=0, koff=0, voff=0,
                                    pair=False, scale=mla_scale, name="mla_bwd", qblk=ATTN_BWD_QBLK,
                                    side=red.late_halves(got) if red else None)
    if red:
        red.late_done(got)
    dqkv = jnp.concatenate([dfq, dfk, dfv], axis=1).astype(BF16)
    dlf = cumsum_rows(d_f, reverse=True, name="gate_cumsum_bwd")
    dsmall, dq_u, dkv_u, dg_q, dg_kv, db_f = mla_prep_bwd(dmq, dmk, dmv, dlf, small, g_q, g_kv, w_uq_p, w_ukv_p,
                                                          tab_c, tab_a, tab_b, b_f, name="mla_prep_bwd")
    dw_uq_p = mm_tn(cqn, dq_u, name="dw_uq")
    dw_ukv_p = mm_tn(ckvn, dkv_u, name="dw_ukv")

    dw_in_t = join_w_in_t(mm_tn(dqkv, h1, name="dw_qkv"), mm_tn(dsmall, h1, name="dw_small"))
    dk_cols = _unpad_heads(dw_ukv_p[:, :HW], HEAD_DIM).reshape(KV_RANK, N_HEADS, HEAD_DIM)
    dv_cols = dw_ukv_p[:, HW:].reshape(KV_RANK, N_HEADS, HEAD_DIM)
    dw_ukv = jnp.concatenate([dk_cols, dv_cols], axis=2).reshape(KV_RANK, N_HEADS * 2 * HEAD_DIM)
    early = (dw_in_t, split_cols(dw_uq_p), split_cols(dw_ukv))
    grad_x, dg_attn, *got = mm([dqkv, dsmall], [w_qkv_t, w_small_t], extras=[xs, dx1], vecs=[g_attn],
                               epilogue=norm_bwd, out_dtypes=[F32], n_sums=1, name="proj_bwd",
                               side=red.early_scatter(early) if red else None)
    if red:
        red.early_done(got)
    d_gains = (dg_attn, db_f[:, :N_HEADS], dg_q, dg_kv, dg_fo, dg_mo, dg_mlp, dg_fin)
    return loss, grad_x, early, late, d_gains


class GradReduction:
    def __init__(self, core_id, chip):
        self.core_id, self.chip = core_id, chip
        self.core = core_id.reshape(1).astype(jnp.int32)
        self.grads, self.pairs, self.halves, self.others = {}, {}, {}, {}

    def _other_halves(self, grads):
        out = []
        for g in grads:
            axis = 1 + split_axis(g.shape[1])
            size = g.shape[axis] // 2
            out.append(lax.dynamic_slice_in_dim(g, (1 - self.core_id) * size, size, axis=axis).astype(BF16))
        return out

    def _add_pairs(self, group, recvs):
        self.pairs[group] = [add_pair(g, r, self.core, name="add_pair_%s_%d" % (group, n))
                             for n, (g, r) in enumerate(zip(self.grads[group], recvs))]
        return scatter_side(self.pairs[group])

    def _chip_sums(self, group, scattered):
        chip = self.chip
        with_mine = [lax.dynamic_update_index_in_dim(s, lax.dynamic_index_in_dim(p, chip, 0, keepdims=True), chip, 0)
                     for s, p in zip(scattered, self.pairs[group])]
        self.halves[group] = [sum_chips(s, name="sum_chips_%s_%d" % (group, n)) for n, s in enumerate(with_mine)]
        return self.halves[group]

    def late_swap(self, grads, bf16_copies):
        self.grads["late"] = list(grads)
        return swap_halves_side(list(bf16_copies))

    def late_scatter(self, recvs):
        return self._add_pairs("late", recvs)

    def late_halves(self, scattered):
        return swap_side(self._chip_sums("late", scattered))

    def late_done(self, others):
        self.others["late"] = list(others)

    def early_scatter(self, grads):
        self.grads["early"] = list(grads)
        return self._add_pairs("early", run_side(swap_side(self._other_halves(grads)), name="swap_early_sends"))

    def early_done(self, scattered):
        self.others["early"] = list(run_side(swap_side(self._chip_sums("early", scattered)), name="swap_early_halves"))

def kernel(x, positions, attn_norm_g, w_in, b_forget, q_norm_g, w_uq, kv_norm_g, w_ukv, fox_out_g, mla_out_g, w_o, mlp_norm_g, w_up, w_down, final_norm_g, loss_target, m_attn_norm_g, m_w_in, m_b_forget, m_q_norm_g, m_w_uq, m_kv_norm_g, m_w_ukv, m_fox_out_g, m_mla_out_g, m_w_o, m_mlp_norm_g, m_w_up, m_w_down, m_final_norm_g, v_attn_norm_g, v_w_in, v_b_forget, v_q_norm_g, v_w_uq, v_kv_norm_g, v_w_ukv, v_fox_out_g, v_mla_out_g, v_w_o, v_mlp_norm_g, v_w_up, v_w_down, v_final_norm_g):
    core_id = lax.axis_index("c")
    core = core_id.reshape(1).astype(jnp.int32)
    chip = 2 * lax.axis_index("x") + lax.axis_index("y")
    big = [w_in, w_uq, w_ukv, w_o, w_up, w_down]
    big_m = [m_w_in, m_w_uq, m_w_ukv, m_w_o, m_w_up, m_w_down]
    big_v = [v_w_in, v_w_uq, v_w_ukv, v_w_o, v_w_up, v_w_down]
    n_early = 3

    def vec(a):
        return a.reshape(1, -1)

    small = [attn_norm_g, b_forget, q_norm_g, kv_norm_g, fox_out_g, mla_out_g, mlp_norm_g, final_norm_g]
    small_m = [m_attn_norm_g, m_b_forget, m_q_norm_g, m_kv_norm_g, m_fox_out_g, m_mla_out_g, m_mlp_norm_g, m_final_norm_g]
    small_v = [v_attn_norm_g, v_b_forget, v_q_norm_g, v_kv_norm_g, v_fox_out_g, v_mla_out_g, v_mlp_norm_g, v_final_norm_g]
    gains = [vec(a) for a in small]

    views = [big[0][0].T, _pad_heads(big[1][0], HEAD_DIM + ROPE_DIM)] + [w[0] for w in big[2:]]
    shards = [v.astype(BF16) for v in views]

    def with_own(gathered, mine):
        return [lax.dynamic_update_index_in_dim(g, s, chip, 0) for g, s in zip(gathered, mine)]

    def early_weights(gathered):
        g_in, g_uq, g_ukv = with_own(gathered, shards[:n_early])
        return g_in, join_cols(g_uq), join_cols(g_ukv)

    def late_weights(gathered):
        g_o, g_up, g_down = with_own(gathered, shards[n_early:])
        return g_o.reshape(-1, g_o.shape[2]), join_cols(g_up), g_down.reshape(-1, g_down.shape[2])

    reduction = GradReduction(core_id, chip)
    loss, grad_x, _, _, d_small = local_step(
        x[0], positions[0], loss_target[0], gains, early_weights, late_weights, gather_side(shards[:n_early]),
        (gather_side(shards[n_early:-1]), gather_side(shards[-1:])), reduction)
    halves = reduction.halves["early"] + reduction.halves["late"]
    others = reduction.others["early"] + reduction.others["late"]

    def rows8(vs):
        return jnp.concatenate([_pad_lanes(vec(a).astype(F32), 1024) for a in vs], axis=0)

    with_loss = [jnp.concatenate([d, loss], axis=1) if n == 1 else d for n, d in enumerate(d_small)]
    g_small8 = allreduce_small(rows8(with_loss))

    outs_big = []
    for n, (w, gm, go, m, v) in enumerate(zip(big, halves, others, big_m, big_v)):
        if n == 0:
            outs = adamw_halves_t(w[0].T, gm, go, m[0].T, v[0].T, core, name="adamw_%d" % n)
            outs_big.append([o.T[None] for o in outs])
        else:
            if n == 1:
                gm, go = (_unpad_heads(gh, HEAD_DIM + ROPE_DIM) for gh in (gm, go))
            outs_big.append(adamw_halves(w, gm, go, m, v, core, name="adamw_%d" % n))
    d8, m8, v8 = adamw(rows8(small), g_small8, rows8(small_m), rows8(small_v), name="adamw_small")

    def unrows8(a8):
        return [a8[n, :s.size].reshape(s.shape) for n, s in enumerate(small)]

    loss_all = g_small8[1, N_HEADS]
    grads, deltas, new_m, new_v = [None] * 14, [None] * 14, [None] * 14, [None] * 14
    big_at = [1, 4, 6, 9, 11, 12]
    small_at = [0, 2, 3, 5, 7, 8, 10, 13]
    for n, at in enumerate(big_at):
        grads[at], deltas[at], new_m[at], new_v[at] = outs_big[n]
    for at, g, dd, mm_, vv in zip(small_at, unrows8(g_small8), unrows8(d8), unrows8(m8), unrows8(v8)):
        grads[at], deltas[at], new_m[at], new_v[at] = g, dd, mm_, vv
    return (loss_all, grad_x[None], *grads, *deltas, *new_m, *new_v)
```

```python
import jax
import jax.numpy as jnp
from jax import lax
from jax.experimental import pallas as pl
from jax.experimental.pallas import tpu as pltpu

F32 = jnp.float32
BF16 = jnp.bfloat16
MESH = pl.DeviceIdType.MESH

EPS = 1e-6
ROPE_THETA = 10000.0
N_HEADS = 8
PAIRS = N_HEADS // 2
HEAD_DIM = 64
ROPE_DIM = 32
LANES = 128
Q_RANK = 384
KV_RANK = 256
N_CHIPS = 4
ADAM_LR, ADAM_B1, ADAM_B2, ADAM_EPS, ADAM_WD, ADAM_STEP = 0.001, 0.9, 0.999, 1e-08, 0.01, 10
VMEM_LIMIT = 48 * 1024 * 1024
LOG2E = 1.4426950408889634
LN2 = 0.6931471805599453
NN = (((1,), (0,)), ((), ()))
NT = (((1,), (1,)), ((), ()))
TN = (((0,), (0,)), ((), ()))


def _params(sem=None):
    return pltpu.CompilerParams(dimension_semantics=sem, vmem_limit_bytes=VMEM_LIMIT)


def _fit(block, dim):
    if dim <= block:
        return dim
    return next(b for b in range(block - block % LANES, 0, -LANES) if dim % b == 0)


def _row_block(rows):
    return next(b for b in (256, 128, 64, 32, 16, 8) if rows % b == 0)


def gridded(body, *, name, grid, in_specs, out_specs, out_shape, ins, semantics, side=None):
    if side is None:
        return pl.pallas_call(body, name=name, grid=grid, in_specs=in_specs, out_specs=out_specs, out_shape=out_shape,
                              compiler_params=_params(semantics))(*ins)
    n_in, n_out = len(in_specs), len(out_specs)
    steps = 1
    for extent in grid:
        steps *= extent

    def riding(*refs):
        main_in, s_in, main_out, s_out, sems = _split_refs(refs, [n_in, len(side.ins), n_out, len(side.out_shapes), 2])
        step = 0
        for axis, extent in enumerate(grid):
            step = step * extent + pl.program_id(axis)

        @pl.when(step == 0)
        def _():
            side.first(s_in, s_out, *sems)

        body(*main_in, *main_out)

        @pl.when(step == steps - 1)
        def _():
            if side.mid is not None:
                side.mid(s_in, s_out, *sems)
            side.last(s_in, s_out, *sems)

    s_in_specs, s_out_specs = side.specs()
    return pl.pallas_call(
        riding, name=name, grid=grid, in_specs=list(in_specs) + s_in_specs, out_specs=list(out_specs) + s_out_specs,
        out_shape=list(out_shape) + side.out_shapes, scratch_shapes=side.sems(),
        compiler_params=_params(("arbitrary",) * len(grid)))(*ins, *side.ins)


def rmsnorm(x, g, *, out_dtype, name, bt=512, side=None):
    t, d = x.shape
    bt = min(bt, t)

    def body(x_ref, g_ref, o_ref):
        xv = x_ref[...].astype(F32)
        r = lax.rsqrt(jnp.mean(xv * xv, axis=-1, keepdims=True) + EPS)
        o_ref[...] = (xv * r * g_ref[...]).astype(o_ref.dtype)

    return gridded(
        body, name=name, grid=(t // bt,),
        in_specs=[pl.BlockSpec((bt, d), lambda i: (i, 0)), pl.BlockSpec((1, d), lambda i: (0, 0))],
        out_specs=[pl.BlockSpec((bt, d), lambda i: (i, 0))],
        out_shape=[jax.ShapeDtypeStruct((t, d), out_dtype)],
        ins=[x, g], semantics=("parallel",), side=side)


def mm(a, b, *, trans_b=False, a_pro=None, extras=(), vecs=(), epilogue=None, out_dtypes, n_sums=0, t_blk=None, name,
       bm=1024, bn=1024, side=None):
    a_list = list(a) if isinstance(a, (list, tuple)) else [a]
    b_list = list(b) if isinstance(b, (list, tuple)) else [b]
    m = a_list[0].shape[0]
    n = b_list[0].shape[0] if trans_b else b_list[0].shape[1]
    ks = [x.shape[1] for x in a_list]
    if sum(ks) > 2048:
        bm = bm // 2
    bm, bn = _fit(bm, m), _fit(bn, n)
    assert n_sums == 0 or bn == n
    n_ab, n_ex, n_vec, n_out = len(a_list), len(extras), len(vecs), len(out_dtypes)
    n_t = 0 if t_blk is None else 1

    def body(*refs):
        a_refs, b_refs, ex, vs, outs, t_outs, sums = _split_refs(refs, [n_ab, n_ab, n_ex, n_vec, n_out, n_t, n_sums])
        acc = None
        for a_ref, b_ref in zip(a_refs, b_refs):
            a_tile = a_ref[...] if a_pro is None else a_pro(a_ref[...])
            part = lax.dot_general(a_tile, b_ref[...], NT if trans_b else NN, preferred_element_type=F32)
            acc = part if acc is None else acc + part
        res = epilogue(acc, *[e[...] for e in ex], *[v[...] for v in vs]) if epilogue is not None else (acc,)
        for o, r in zip(outs, res[:n_out]):
            o[...] = r.astype(o.dtype)
        for t_ref in t_outs:
            for u in range(bm // t_blk):
                t_ref[u] = res[0][u * t_blk:(u + 1) * t_blk, :].T.astype(t_ref.dtype)
        if n_sums:
            @pl.when(pl.program_id(0) == 0)
            def _():
                for s_ref in sums:
                    s_ref[...] = jnp.zeros_like(s_ref)

            for s_ref, r in zip(sums, res[n_out:]):
                s_ref[...] += r

    tile = pl.BlockSpec((bm, bn), lambda i, j: (i, j))
    vec = pl.BlockSpec((1, bn), lambda i, j: (0, j))
    t_specs, t_shapes = [], []
    if t_blk is not None:
        t_specs = [pl.BlockSpec((bm // t_blk, bn, t_blk), lambda i, j: (i, j, 0))]
        t_shapes = [jax.ShapeDtypeStruct((m // t_blk, n, t_blk), out_dtypes[0])]
    a_specs = [pl.BlockSpec((bm, k), lambda i, j: (i, 0)) for k in ks]
    b_specs = [pl.BlockSpec((bn, k), lambda i, j: (j, 0)) if trans_b else pl.BlockSpec((k, bn), lambda i, j: (0, j)) for k in ks]
    return gridded(
        body, name=name, grid=(m // bm, n // bn),
        in_specs=a_specs + b_specs + [tile] * n_ex + [vec] * n_vec,
        out_specs=[tile] * n_out + t_specs + [vec] * n_sums,
        out_shape=[jax.ShapeDtypeStruct((m, n), dt) for dt in out_dtypes] + t_shapes + [jax.ShapeDtypeStruct((1, n), F32)] * n_sums,
        ins=[*a_list, *b_list, *extras, *vecs],
        semantics=("arbitrary", "arbitrary") if n_sums else ("parallel", "parallel"), side=side)


def mm_tn(a, b, *, a_pro=None, name, col_shards=1, bf16_copy=False, bk=1024, bn=1024, bt=1024):
    t, k = a.shape
    n = b.shape[1]
    ns = n // col_shards
    bk, bn, bt = _fit(bk, k), _fit(bn, ns), _fit(bt, t)
    per = ns // bn
    last = t // bt - 1

    def body(a_ref, b_ref, o_ref, *copy_ref):
        @pl.when(pl.program_id(2) == 0)
        def _():
            o_ref[...] = jnp.zeros_like(o_ref)

        a_tile = a_ref[...] if a_pro is None else a_pro(a_ref[...])
        o_ref[...] += lax.dot_general(a_tile, b_ref[...], TN, preferred_element_type=F32)
        if bf16_copy:
            @pl.when(pl.program_id(2) == last)
            def _():
                copy_ref[0][...] = o_ref[...].astype(BF16)

    if col_shards == 1:
        out_spec = pl.BlockSpec((bk, bn), lambda i, j, s: (i, j))
        shape = (k, n)
    else:
        out_spec = pl.BlockSpec((None, bk, bn), lambda i, j, s: (j // per, i, j % per))
        shape = (col_shards, k, ns)
    out_specs, out_shape = out_spec, jax.ShapeDtypeStruct(shape, F32)
    if bf16_copy:
        out_specs, out_shape = [out_spec, out_spec], [out_shape, jax.ShapeDtypeStruct(shape, BF16)]
    return pl.pallas_call(
        body, name=name, grid=(k // bk, n // bn, t // bt),
        in_specs=[pl.BlockSpec((bt, bk), lambda i, j, s: (s, i)), pl.BlockSpec((bt, bn), lambda i, j, s: (s, j))],
        out_specs=out_specs, out_shape=out_shape,
        compiler_params=_params(("parallel", "parallel", "arbitrary")),
    )(a, b)


def _split3(x):
    hi = x.astype(BF16)
    r1 = x - hi.astype(F32)
    mid = r1.astype(BF16)
    lo = (r1 - mid.astype(F32)).astype(BF16)
    return hi, mid, lo


def cumsum_rows(x, *, reverse, name, bc=512):
    t, d = x.shape
    bc = min(bc, t)
    nb = t // bc

    def body(x_ref, o_ref, carry):
        @pl.when(pl.program_id(0) == 0)
        def _():
            carry[...] = jnp.zeros_like(carry)

        r = lax.broadcasted_iota(jnp.int32, (bc, bc), 0)
        c = lax.broadcasted_iota(jnp.int32, (bc, bc), 1)
        tri = jnp.where((r <= c) if reverse else (r >= c), 1.0, 0.0).astype(BF16)
        hi, mid, lo = _split3(x_ref[...])
        s = (lax.dot_general(tri, hi, NN, preferred_element_type=F32)
             + lax.dot_general(tri, mid, NN, preferred_element_type=F32)
             + lax.dot_general(tri, lo, NN, preferred_element_type=F32)) + carry[0:1, :]
        o_ref[...] = s
        carry[0:1, :] = s[0:1, :] if reverse else s[bc - 1:bc, :]

    imap = (lambda i: (nb - 1 - i, 0)) if reverse else (lambda i: (i, 0))
    return pl.pallas_call(
        body, name=name, grid=(nb,),
        in_specs=[pl.BlockSpec((bc, d), imap)], out_specs=pl.BlockSpec((bc, d), imap),
        out_shape=jax.ShapeDtypeStruct((t, d), F32),
        scratch_shapes=[pltpu.VMEM((8, d), F32)],
        compiler_params=_params(("arbitrary",)),
    )(x)


def _rope(x, c, a, b):
    return x * c + pltpu.roll(x, LANES - ROPE_DIM // 2, 1) * a + pltpu.roll(x, ROPE_DIM // 2, 1) * b


def _rope_bwd(d, c, a, b):
    return d * c + pltpu.roll(d * a, ROPE_DIM // 2, 1) + pltpu.roll(d * b, LANES - ROPE_DIM // 2, 1)


S_CQ, S_CKV, S_KR, S_F, S_END = 0, Q_RANK, Q_RANK + KV_RANK, Q_RANK + KV_RANK + LANES, 1024
HW = N_HEADS * LANES
FW = N_HEADS * HEAD_DIM


def mla_prep(small, g_q, g_kv, w_uq, w_ukv, tab_c, tab_a, tab_b, b_f, *, name, bt=512):
    t = small.shape[0]
    bt = min(bt, t)

    def body(s_ref, gq_ref, gkv_ref, wq_ref, wkv_ref, c_ref, a_ref, b_ref, bf_ref,
             mq_ref, mk_ref, mv_ref, lf_ref, cqn_ref, ckvn_ref, mqt_ref, mvt_ref):
        cq = s_ref[:, S_CQ:S_CKV]
        rq = lax.rsqrt(jnp.mean(cq * cq, axis=-1, keepdims=True) + EPS)
        cqn = (cq * rq * gq_ref[...]).astype(BF16)
        ckv = s_ref[:, S_CKV:S_KR]
        rkv = lax.rsqrt(jnp.mean(ckv * ckv, axis=-1, keepdims=True) + EPS)
        ckvn = (ckv * rkv * gkv_ref[...]).astype(BF16)
        cqn_ref[...] = cqn
        ckvn_ref[...] = ckvn
        tc, ta, tb = c_ref[...], a_ref[...], b_ref[...]
        q = jnp.dot(cqn, wq_ref[...], preferred_element_type=F32)
        kv = jnp.dot(ckvn, wkv_ref[...], preferred_element_type=F32)
        kr = _rope(s_ref[:, S_KR:S_F], tc, ta, tb)
        for h in range(N_HEADS):
            sl = slice(h * LANES, (h + 1) * LANES)
            roped = _rope(q[:, sl], tc, ta, tb)
            mq_ref[:, sl] = roped.astype(BF16)
            mqt_ref[0, sl, :] = roped.T.astype(BF16)
            mk_ref[:, sl] = (kv[:, sl] + kr).astype(BF16)
        mv_ref[...] = kv[:, HW:].astype(BF16)
        mvt_ref[0] = kv[:, HW:].T.astype(BF16)
        z = s_ref[:, S_F:S_END - LANES] + bf_ref[...]
        lf_ref[...] = jnp.minimum(z, 0.0) - jnp.log(1.0 + jnp.exp(-jnp.abs(z)))

    def row(w):
        return pl.BlockSpec((bt, w), lambda i: (i, 0))

    def full(arr):
        return pl.BlockSpec(arr.shape, lambda i: (0, 0))

    return pl.pallas_call(
        body, name=name, grid=(t // bt,),
        in_specs=[row(S_END), full(g_q), full(g_kv), full(w_uq), full(w_ukv), row(LANES), row(LANES), row(LANES), full(b_f)],
        out_specs=[row(HW), row(HW), row(FW), row(LANES), row(Q_RANK), row(KV_RANK),
                   pl.BlockSpec((1, HW, bt), lambda i: (i, 0, 0)), pl.BlockSpec((1, FW, bt), lambda i: (i, 0, 0))],
        out_shape=[jax.ShapeDtypeStruct((t, HW), BF16)] * 2 + [jax.ShapeDtypeStruct((t, FW), BF16), jax.ShapeDtypeStruct((t, LANES), F32),
                   jax.ShapeDtypeStruct((t, Q_RANK), BF16), jax.ShapeDtypeStruct((t, KV_RANK), BF16),
                   jax.ShapeDtypeStruct((t // bt, HW, bt), BF16), jax.ShapeDtypeStruct((t // bt, FW, bt), BF16)],
        compiler_params=_params(("parallel",)),
    )(small, g_q, g_kv, w_uq, w_ukv, tab_c, tab_a, tab_b, b_f)


def mla_prep_bwd(dmq, dmk, dmv, dlf, small, g_q, g_kv, w_uq, w_ukv, tab_c, tab_a, tab_b, b_f, *, name, bt=512):
    t = small.shape[0]
    bt = min(bt, t)

    def body(dmq_ref, dmk_ref, dmv_ref, dlf_ref, s_ref, gq_ref, gkv_ref, wq_ref, wkv_ref, c_ref, a_ref, b_ref, bf_ref,
             ds_ref, dq_ref, dkv_ref, dgq_ref, dgkv_ref, db_ref):
        tc, ta, tb = c_ref[...], a_ref[...], b_ref[...]
        lane = lax.broadcasted_iota(jnp.int32, (1, LANES), 1)
        dkr = jnp.zeros((bt, LANES), F32)
        for h in range(N_HEADS):
            sl = slice(h * LANES, (h + 1) * LANES)
            dq_ref[:, sl] = _rope_bwd(dmq_ref[:, sl], tc, ta, tb).astype(BF16)
            dkr = dkr + dmk_ref[:, sl]
        dkv_ref[:, :HW] = dmk_ref[...].astype(BF16)
        dkv_ref[:, HW:] = dmv_ref[...].astype(BF16)
        in_rope = (lane >= HEAD_DIM) & (lane < HEAD_DIM + ROPE_DIM)
        ds_ref[:, S_KR:S_F] = jnp.where(in_rope, _rope_bwd(dkr, tc, ta, tb), 0.0).astype(BF16)

        def norm_bwd(raw, g_ref, dn, dg_ref):
            r = lax.rsqrt(jnp.mean(raw * raw, axis=-1, keepdims=True) + EPS)
            u = dn * g_ref[...]
            dot = jnp.mean(u * raw, axis=-1, keepdims=True)
            dg_ref[...] += jnp.sum(dn * (raw * r), axis=0, keepdims=True)
            return r * u - raw * (r * r * r * dot)

        @pl.when(pl.program_id(0) == 0)
        def _():
            dgq_ref[...] = jnp.zeros_like(dgq_ref)
            dgkv_ref[...] = jnp.zeros_like(dgkv_ref)
            db_ref[...] = jnp.zeros_like(db_ref)

        dcqn = lax.dot_general(dq_ref[...], wq_ref[...], NT, preferred_element_type=F32)
        ds_ref[:, S_CQ:S_CKV] = norm_bwd(s_ref[:, S_CQ:S_CKV], gq_ref, dcqn, dgq_ref).astype(BF16)
        dckvn = lax.dot_general(dkv_ref[...], wkv_ref[...], NT, preferred_element_type=F32)
        ds_ref[:, S_CKV:S_KR] = norm_bwd(s_ref[:, S_CKV:S_KR], gkv_ref, dckvn, dgkv_ref).astype(BF16)
        z = s_ref[:, S_F:S_END - LANES] + bf_ref[...]
        dz = jnp.where(lane < N_HEADS, dlf_ref[...] / (1.0 + jnp.exp(z)), 0.0)
        db_ref[...] += jnp.sum(dz, axis=0, keepdims=True)
        ds_ref[:, S_F:S_END - LANES] = dz.astype(BF16)
        ds_ref[:, S_END - LANES:] = jnp.zeros((bt, LANES), BF16)

    def row(w):
        return pl.BlockSpec((bt, w), lambda i: (i, 0))

    def full(arr):
        return pl.BlockSpec(arr.shape, lambda i: (0, 0))

    def vec(w):
        return pl.BlockSpec((1, w), lambda i: (0, 0))

    return pl.pallas_call(
        body, name=name, grid=(t // bt,),
        in_specs=[row(HW), row(HW), row(FW), row(LANES), row(S_END), full(g_q), full(g_kv), full(w_uq), full(w_ukv),
                  row(LANES), row(LANES), row(LANES), full(b_f)],
        out_specs=[row(S_END), row(HW), row(HW + FW), vec(Q_RANK), vec(KV_RANK), vec(LANES)],
        out_shape=[jax.ShapeDtypeStruct((t, S_END), BF16), jax.ShapeDtypeStruct((t, HW), BF16),
                   jax.ShapeDtypeStruct((t, HW + FW), BF16), jax.ShapeDtypeStruct((1, Q_RANK), F32),
                   jax.ShapeDtypeStruct((1, KV_RANK), F32), jax.ShapeDtypeStruct((1, LANES), F32)],
        compiler_params=_params(("arbitrary",)),
    )(dmq, dmk, dmv, dlf, small, g_q, g_kv, w_uq, w_ukv, tab_c, tab_a, tab_b, b_f)


class Side:
    def __init__(self, ins, out_shapes, n_sems, first, last, mid=None):
        self.ins, self.out_shapes, self.n_sems = list(ins), list(out_shapes), n_sems
        self.first, self.mid, self.last = first, mid, last

    def specs(self):
        return [ANY] * len(self.ins), [ANY] * len(self.out_shapes)

    def sems(self):
        return [pltpu.SemaphoreType.DMA((self.n_sems,)), pltpu.SemaphoreType.DMA((self.n_sems,))]


def _lane():
    return lax.broadcasted_iota(jnp.int32, (1, LANES), 1)


def _halves(x):
    zero = jnp.zeros_like(x)
    return [jnp.where(_lane() < HEAD_DIM, x, zero), jnp.where(_lane() >= HEAD_DIM, x, zero)]


def _groups(x):
    return [x[:, :LANES], x[:, LANES:]]


def _pick_row(tile, h):
    row = lax.broadcasted_iota(jnp.int32, (tile.shape[0], 1), 0)
    return jnp.sum(jnp.where(row == h, tile, 0.0), axis=0, keepdims=True)


def _pick_lane(tile, h):
    return jnp.sum(jnp.where(_lane() == h, tile, 0.0), axis=1, keepdims=True)


def _row_halves(x):
    row = lax.broadcasted_iota(jnp.int32, (LANES, 1), 0)
    zero = jnp.zeros_like(x)
    return [jnp.where(row < HEAD_DIM, x, zero), jnp.where(row >= HEAD_DIM, x, zero)]


def _below_diagonal(s):
    r = lax.broadcasted_iota(jnp.int32, s.shape, 0)
    c = lax.broadcasted_iota(jnp.int32, s.shape, 1)
    return jnp.where(c <= r, s, -jnp.inf)


def _above_diagonal(s):
    r = lax.broadcasted_iota(jnp.int32, s.shape, 0)
    c = lax.broadcasted_iota(jnp.int32, s.shape, 1)
    return jnp.where(r <= c, s, -jnp.inf)


def _split_refs(refs, counts):
    out, at = [], 0
    for n in counts:
        out.append(refs[at:at + n])
        at += n
    return out


def flash_fwd(qt_arr, k_arr, vt_arr, f_cum, *, qoff, koff, voff, pair, scale, name, blk=512, side=None):
    t = k_arr.shape[0]
    tblk = qt_arr.shape[2]
    blk = max(min(blk, t), tblk)
    sub = blk // tblk
    nb = t // blk
    steps = PAIRS * nb
    w = LANES if pair else 2 * LANES
    has_bias = f_cum is not None
    ins = [qt_arr, k_arr, vt_arr] + ([f_cum] if has_bias else [])

    def wide(ref, first):
        parts = [ref[first + u] for u in range(sub)]
        return parts[0] if sub == 1 else jnp.concatenate(parts, axis=1)
    s_ins, s_outs = (side.ins, side.out_shapes) if side else ([], [])

    def body(*refs):
        main, si, outs, so, sems = _split_refs(refs, [len(ins), len(s_ins), 2, len(s_outs), 2 if side else 0])
        qt_ref, k_ref, vt_ref = main[:3]
        f_ref = main[3] if has_bias else None
        o_ref, st_ref = outs
        g, i = pl.program_id(0), pl.program_id(1)
        step_id = g * nb + i
        if side:
            @pl.when(step_id == 0)
            def _():
                side.first(si, so, *sems)

            if side.mid is not None:
                @pl.when(step_id == (3 * steps) // 4)
                def _():
                    side.mid(si, so, *sems)

        qt = (wide(qt_ref, 0).astype(F32) * (scale * LOG2E)).astype(BF16)
        qts = _row_halves(qt) if pair else [qt[:LANES], qt[LANES:]]

        def k_of(kk, n):
            return kk if pair else kk[:, n * LANES:(n + 1) * LANES]

        def with_ones(vt_rows):
            return jnp.concatenate([vt_rows, jnp.ones((ACC_ROWS - HEAD_DIM, vt_rows.shape[1]), BF16)], axis=0)

        def step(j, carry, diagonal):
            rows = pl.ds(pl.multiple_of(j * blk, blk), blk)
            kk = k_ref[rows, :]
            vt = wide(vt_ref, sub * j)
            out = []
            for n in range(2):
                m, acc = carry[n]
                s = jnp.dot(k_of(kk, n), qts[n], preferred_element_type=F32)
                if has_bias:
                    s = s - LOG2E * _pick_lane(f_ref[rows, :], 2 * g + n)
                if diagonal:
                    s = _above_diagonal(s)
                m_new = jnp.maximum(m, jnp.max(s, axis=0, keepdims=True))
                p = jnp.exp2(s - m_new).astype(BF16)
                out.append((m_new, jnp.exp2(m - m_new) * acc
                            + jnp.dot(with_ones(vt[n * HEAD_DIM:(n + 1) * HEAD_DIM]), p, preferred_element_type=F32)))
            return tuple(out)

        def diagonal_in_halves(carry):
            h = tblk
            halves = [pl.ds(pl.multiple_of(i * blk, blk), h), pl.ds(pl.multiple_of(i * blk + h, h), h)]
            k_top, k_bot = k_ref[halves[0], :], k_ref[halves[1], :]
            vt_top, vt_bot = vt_ref[sub * i], vt_ref[sub * i + 1]
            out = []
            for n in range(2):
                m, acc = carry[n]
                heads = slice(n * HEAD_DIM, (n + 1) * HEAD_DIM)
                s_top = jnp.dot(k_of(k_top, n), qts[n], preferred_element_type=F32)
                s_bot = jnp.dot(k_of(k_bot, n), qts[n][:, h:], preferred_element_type=F32)
                if has_bias:
                    s_top = s_top - LOG2E * _pick_lane(f_ref[halves[0], :], 2 * g + n)
                    s_bot = s_bot - LOG2E * _pick_lane(f_ref[halves[1], :], 2 * g + n)
                s_top, s_bot = _above_diagonal(s_top), _above_diagonal(s_bot)
                m_top = jnp.maximum(m, jnp.max(s_top, axis=0, keepdims=True))
                m_new = jnp.concatenate([m_top[:, :h], jnp.maximum(m_top[:, h:], jnp.max(s_bot, axis=0, keepdims=True))], axis=1)
                p_top = jnp.exp2(s_top - m_new).astype(BF16)
                p_bot = jnp.exp2(s_bot - m_new[:, h:]).astype(BF16)
                late = jnp.concatenate([jnp.zeros((ACC_ROWS, h), F32),
                                        jnp.dot(with_ones(vt_bot[heads]), p_bot, preferred_element_type=F32)], axis=1)
                out.append((m_new, jnp.exp2(m - m_new) * acc
                            + jnp.dot(with_ones(vt_top[heads]), p_top, preferred_element_type=F32) + late))
            return tuple(out)

        init = tuple((jnp.full((1, blk), -jnp.inf, F32), jnp.zeros((ACC_ROWS, blk), F32)) for _ in range(2))
        carry = lax.fori_loop(0, i, lambda j, c: step(j, c, False), init)
        (ma, acca), (mb, accb) = diagonal_in_halves(carry) if sub == 2 else step(i, carry, True)
        la, lb = acca[HEAD_DIM:HEAD_DIM + 1], accb[HEAD_DIM:HEAD_DIM + 1]
        o_ref[...] = jnp.concatenate([acca[:HEAD_DIM] / la, accb[:HEAD_DIM] / lb], axis=0).T
        row = lax.broadcasted_iota(jnp.int32, (LANES, 1), 0)
        st_ref[0] = jnp.where(row == 0, ma + jnp.log2(la), jnp.where(row == 1, mb + jnp.log2(lb), 0.0)).T
        if side:
            @pl.when(step_id == steps - 1)
            def _():
                side.last(si, so, *sems)

    in_specs = [pl.BlockSpec((sub, w, tblk), lambda g, i: (i, qoff + g, 0)), pl.BlockSpec((t, w), lambda g, i: (0, koff + g)),
                pl.BlockSpec((t // tblk, LANES, tblk), lambda g, i: (0, voff + g, 0))]
    if has_bias:
        in_specs.append(pl.BlockSpec((t, LANES), lambda g, i: (0, 0)))
    s_in_specs, s_out_specs = side.specs() if side else ([], [])
    return pl.pallas_call(
        body, name=name, grid=(PAIRS, nb), in_specs=in_specs + s_in_specs,
        out_specs=[pl.BlockSpec((blk, LANES), lambda g, i: (i, g)), pl.BlockSpec((1, blk, LANES), lambda g, i: (g, i, 0))]
        + s_out_specs,
        out_shape=[jax.ShapeDtypeStruct((t, PAIRS * LANES), F32), jax.ShapeDtypeStruct((PAIRS, t, LANES), F32)] + list(s_outs),
        scratch_shapes=side.sems() if side else [],
        compiler_params=_params(("arbitrary", "arbitrary")),
    )(*ins, *s_ins)


def mix_norm(fo, mo, g_fo, g_mo, *, name, bt=512):
    t, d = fo.shape
    bt = min(bt, t)

    def body(fo_ref, mo_ref, gf_ref, gm_ref, o_ref):
        for n, (x_ref, g_ref) in enumerate(((fo_ref, gf_ref), (mo_ref, gm_ref))):
            xv = x_ref[...]
            r = lax.rsqrt(jnp.mean(xv * xv, axis=-1, keepdims=True) + EPS)
            o_ref[:, n * d:(n + 1) * d] = (xv * r * g_ref[...]).astype(BF16)

    row = pl.BlockSpec((bt, d), lambda i: (i, 0))
    vec = pl.BlockSpec((1, d), lambda i: (0, 0))
    return pl.pallas_call(
        body, name=name, grid=(t // bt,), in_specs=[row, row, vec, vec],
        out_specs=pl.BlockSpec((bt, 2 * d), lambda i: (i, 0)),
        out_shape=jax.ShapeDtypeStruct((t, 2 * d), BF16),
        compiler_params=_params(("parallel",)),
    )(fo, mo, g_fo, g_mo)


def mix_norm_bwd(dx1b, w_o, fo, mo, g_fo, g_mo, st_f, st_m, *, name, bt=512, side=None):
    t, d = fo.shape
    bt = min(bt, t)

    def body(dx_in_ref, w_ref, fo_ref, mo_ref, gf_ref, gm_ref, sf_ref, sm_ref, dfo_ref, dmo_ref, dgf_ref, dgm_ref,
             sfo_ref, smo_ref, dfot_ref, dmot_ref):
        @pl.when(pl.program_id(0) == 0)
        def _():
            dgf_ref[...] = jnp.zeros_like(dgf_ref)
            dgm_ref[...] = jnp.zeros_like(dgm_ref)

        dmixed = lax.dot_general(dx_in_ref[...], w_ref[...], NT, preferred_element_type=F32)
        groups = ((fo_ref, gf_ref, dfo_ref, dgf_ref, sf_ref, sfo_ref, dfot_ref),
                  (mo_ref, gm_ref, dmo_ref, dgm_ref, sm_ref, smo_ref, dmot_ref))
        for n, (x_ref, g_ref, dx_ref, dg_ref, st_ref, sto_ref, dxt_ref) in enumerate(groups):
            xv = x_ref[...]
            dhv = dmixed[:, n * d:(n + 1) * d]
            r = lax.rsqrt(jnp.mean(xv * xv, axis=-1, keepdims=True) + EPS)
            u = dhv * g_ref[...]
            dxf = r * u - xv * (r * r * r * jnp.mean(u * xv, axis=-1, keepdims=True))
            dxb = dxf.astype(BF16)
            dx_ref[...] = dxb
            dxt_ref[0] = dxf.T.astype(BF16)
            dg_ref[...] += jnp.sum(dhv * (xv * r), axis=0, keepdims=True)
            prod = xv * dxb.astype(F32)
            for g in range(PAIRS):
                grp = prod[:, g * LANES:(g + 1) * LANES]
                da = jnp.sum(jnp.where(_lane() < HEAD_DIM, grp, 0.0), axis=1, keepdims=True)
                db = jnp.sum(jnp.where(_lane() >= HEAD_DIM, grp, 0.0), axis=1, keepdims=True)
                sto_ref[g] = jnp.where(_lane() == 2, da, jnp.where(_lane() == 3, db, st_ref[g]))

    row = pl.BlockSpec((bt, d), lambda i: (i, 0))
    vec = pl.BlockSpec((1, d), lambda i: (0, 0))
    stat = pl.BlockSpec((PAIRS, bt, LANES), lambda i: (0, i, 0))
    return gridded(
        body, name=name, grid=(t // bt,),
        in_specs=[pl.BlockSpec((bt, dx1b.shape[1]), lambda i: (i, 0)), pl.BlockSpec(w_o.shape, lambda i: (0, 0)),
                  row, row, vec, vec, stat, stat],
        out_specs=[row, row, vec, vec, stat, stat] + [pl.BlockSpec((1, d, bt), lambda i: (i, 0, 0))] * 2,
        out_shape=[jax.ShapeDtypeStruct((t, d), BF16)] * 2 + [jax.ShapeDtypeStruct((1, d), F32)] * 2
        + [jax.ShapeDtypeStruct(st_f.shape, F32)] * 2 + [jax.ShapeDtypeStruct((t // bt, d, bt), BF16)] * 2,
        ins=[dx1b, w_o, fo, mo, g_fo, g_mo, st_f, st_m], semantics=("arbitrary",), side=side)


def flash_bwd(q_arr, qt_arr, k_arr, v_arr, do_arr, dot_arr, st, f_blocks, *, qoff, koff, voff, pair, scale, name, qblk=1024,
              side=None):
    t = q_arr.shape[0]
    blk = qt_arr.shape[2]
    qblk = max(min(qblk, t), blk)
    sub = qblk // blk
    nb, nbq = t // blk, t // qblk
    w = LANES if pair else 2 * LANES
    hw = w // 2
    has_bias = f_blocks is not None
    split = _halves if pair else _groups
    ins = [q_arr, qt_arr, k_arr, v_arr, do_arr, dot_arr, st] + ([f_blocks] if has_bias else [])
    n_out = 4 if has_bias else 3
    s_ins, s_outs = (side.ins, side.out_shapes) if side else ([], [])

    def wide(ref, first, count):
        parts = [ref[first + u] for u in range(count)]
        return parts[0] if count == 1 else jnp.concatenate(parts, axis=1)

    def body(*refs):
        main, si, outs, so, sems = _split_refs(refs, [len(ins), len(s_ins), n_out, len(s_outs), 2 if side else 0])
        q_ref, qt_ref, k_ref, v_ref, do_ref, dot_ref, st_ref = main[:7]
        dq_ref, dk_ref, dv_ref = outs[:3]
        g, j = pl.program_id(0), pl.program_id(1)
        step_id = g * nb + j
        if side:
            @pl.when(step_id == 0)
            def _():
                side.first(si, so, *sems)

        @pl.when(j == 0)
        def _():
            dq_ref[...] = jnp.zeros_like(dq_ref)

        kk, vv = k_ref[...], v_ref[...]
        ks = [kk, kk] if pair else _groups(kk)
        if has_bias:
            f_ref, df_ref = main[7], outs[3]
            fk = [LOG2E * _pick_row(f_ref[0], 2 * g + n) for n in range(2)]

            @pl.when(step_id == 0)
            def _():
                df_ref[...] = jnp.zeros_like(df_ref)

        def step(tb, count, carry, diagonal):
            rows = pl.ds(pl.multiple_of(tb * blk, blk), count * blk)
            qs = split((q_ref[rows, :].astype(F32) * (scale * LOG2E)).astype(BF16))
            qt = (wide(qt_ref, tb, count).astype(F32) * (scale * LOG2E)).astype(BF16)
            dos = [h.astype(BF16) for h in _halves(do_ref[rows, :].astype(F32))]
            dot = wide(dot_ref, tb, count)
            stats = st_ref[0, rows, :]
            new, dqs, row_sums = [], [], []
            for n in range(2):
                dkt, dvt, dfk = carry[n]
                s = lax.dot_general(qs[n], ks[n], NT, preferred_element_type=F32)
                if has_bias:
                    s = s - fk[n]
                if diagonal:
                    s = _below_diagonal(s)
                p = jnp.exp2(s - stats[:, n:n + 1])
                dp = lax.dot_general(dos[n], vv, NT, preferred_element_type=F32)
                ds = p * (dp - stats[:, 2 + n:3 + n])
                dsb = ds.astype(BF16)
                dvt = dvt + jnp.dot(dot[n * HEAD_DIM:(n + 1) * HEAD_DIM], p.astype(BF16), preferred_element_type=F32)
                dkt = dkt + jnp.dot(qt[n * hw:(n + 1) * hw], dsb, preferred_element_type=F32)
                dqs.append(jnp.dot(dsb, ks[n], preferred_element_type=F32))
                if has_bias:
                    dfk = dfk - jnp.sum(ds, axis=0, keepdims=True)
                    row_sums.append(jnp.sum(ds, axis=1, keepdims=True))
                new.append((dkt, dvt, dfk))
            dq = (jnp.where(_lane() < HEAD_DIM, dqs[0], dqs[1]) if pair else jnp.concatenate(dqs, axis=1)) * scale
            if has_bias:
                df_ref[rows, :] += jnp.where(_lane() == 2 * g, row_sums[0], jnp.where(_lane() == 2 * g + 1, row_sums[1], 0.0))
            dq_ref[rows, :] += dq
            return tuple(new)

        init = tuple((jnp.zeros((hw, blk), F32), jnp.zeros((HEAD_DIM, blk), F32), jnp.zeros((1, blk), F32)) for _ in range(2))
        carry = step(j, 1, init, True)
        whole = j // sub + 1
        carry = lax.fori_loop(j + 1, whole * sub, lambda tb, c: step(tb, 1, c, False), carry)
        (dka, dva, dfa), (dkb, dvb, dfb) = lax.fori_loop(whole, nbq, lambda i, c: step(i * sub, sub, c, False), carry)
        dk_ref[...] = jnp.concatenate([dka, dkb], axis=0).T * LN2
        dv_ref[...] = jnp.concatenate([dva, dvb], axis=0).T
        if has_bias:
            row = lax.broadcasted_iota(jnp.int32, (LANES, 1), 0)
            by_head = jnp.where(row == 2 * g, dfa, jnp.where(row == 2 * g + 1, dfb, 0.0))
            df_ref[pl.ds(pl.multiple_of(j * blk, blk), blk), :] += by_head.T
        if side:
            @pl.when(step_id == PAIRS * nb - 1)
            def _():
                side.last(si, so, *sems)

    in_specs = [pl.BlockSpec((t, w), lambda g, j: (0, qoff + g)), pl.BlockSpec((nb, w, blk), lambda g, j: (0, qoff + g, 0)),
                pl.BlockSpec((blk, w), lambda g, j: (j, koff + g)), pl.BlockSpec((blk, LANES), lambda g, j: (j, voff + g)),
                pl.BlockSpec((t, LANES), lambda g, j: (0, g)), pl.BlockSpec((nb, LANES, blk), lambda g, j: (0, g, 0)),
                pl.BlockSpec((1, t, LANES), lambda g, j: (g, 0, 0))]
    out_specs = [pl.BlockSpec((t, w), lambda g, j: (0, g)), pl.BlockSpec((blk, w), lambda g, j: (j, g)),
                 pl.BlockSpec((blk, LANES), lambda g, j: (j, g))]
    out_shape = [jax.ShapeDtypeStruct((t, PAIRS * w), F32)] * 2 + [jax.ShapeDtypeStruct((t, PAIRS * LANES), F32)]
    if has_bias:
        in_specs.append(pl.BlockSpec((1, N_HEADS, blk), lambda g, j: (j, 0, 0)))
        out_specs.append(pl.BlockSpec((t, LANES), lambda g, j: (0, 0)))
        out_shape.append(jax.ShapeDtypeStruct((t, LANES), F32))
    s_in_specs, s_out_specs = side.specs() if side else ([], [])
    return pl.pallas_call(
        body, name=name, grid=(PAIRS, nb), in_specs=in_specs + s_in_specs, out_specs=out_specs + s_out_specs,
        out_shape=out_shape + list(s_outs), scratch_shapes=side.sems() if side else [],
        compiler_params=_params(("arbitrary", "arbitrary")),
    )(*ins, *s_ins)


def _adamw_math(w, g, m, v):
    nm = ADAM_B1 * m + (1.0 - ADAM_B1) * g
    nv = ADAM_B2 * v + (1.0 - ADAM_B2) * (g * g)
    m_hat = nm / (1.0 - ADAM_B1 ** ADAM_STEP)
    v_hat = nv / (1.0 - ADAM_B2 ** ADAM_STEP)
    return -ADAM_LR * (m_hat / (jnp.sqrt(v_hat) + ADAM_EPS) + ADAM_WD * w), nm, nv


def adamw(w, g, m, v, *, name):
    rws, cols = w.shape
    br = _row_block(rws)

    def body(w_ref, g_ref, m_ref, v_ref, d_ref, nm_ref, nv_ref):
        d_ref[...], nm_ref[...], nv_ref[...] = _adamw_math(w_ref[...], g_ref[...], m_ref[...], v_ref[...])

    blk = pl.BlockSpec((br, cols), lambda i: (i, 0))
    return pl.pallas_call(
        body, name=name, grid=(rws // br,), in_specs=[blk] * 4, out_specs=[blk] * 3,
        out_shape=[jax.ShapeDtypeStruct((rws, cols), F32)] * 3,
        compiler_params=_params(("parallel",)),
    )(w, g, m, v)


def adamw_halves(w, g_mine, g_other, m, v, core, *, name):
    _, k, n = w.shape
    br = _row_block(k // 2)
    nh = k // 2 // br

    def body(c_ref, w_ref, gm_ref, go_ref, m_ref, v_ref, g_out, d_ref, nm_ref, nv_ref):
        gv = jnp.where(pl.program_id(0) == c_ref[0], gm_ref[...], go_ref[...])
        g_out[0] = gv
        d_ref[0], nm_ref[0], nv_ref[0] = _adamw_math(w_ref[0], gv, m_ref[0], v_ref[0])

    full = pl.BlockSpec((1, br, n), lambda hb, i, c: (0, hb * nh + i, 0))
    half = pl.BlockSpec((br, n), lambda hb, i, c: (i, 0))
    return pl.pallas_call(
        body, name=name,
        grid_spec=pltpu.PrefetchScalarGridSpec(num_scalar_prefetch=1, grid=(2, nh), in_specs=[full, half, half, full, full],
                                               out_specs=[full] * 4),
        out_shape=[jax.ShapeDtypeStruct(w.shape, F32)] * 4,
        compiler_params=_params(("parallel", "parallel")),
    )(core, w, g_mine, g_other, m, v)


def adamw_halves_t(wt, gt_mine, gt_other, mt, vt, core, *, name, bc=128):
    n, k = wt.shape
    nh = k // 2 // bc

    def body(c_ref, w_ref, gm_ref, go_ref, m_ref, v_ref, g_out, d_ref, nm_ref, nv_ref):
        gv = jnp.where(pl.program_id(0) == c_ref[0], gm_ref[...], go_ref[...])
        g_out[...] = gv
        d_ref[...], nm_ref[...], nv_ref[...] = _adamw_math(w_ref[...], gv, m_ref[...], v_ref[...])

    full = pl.BlockSpec((n, bc), lambda hb, i, c: (0, hb * nh + i))
    half = pl.BlockSpec((n, bc), lambda hb, i, c: (0, i))
    return pl.pallas_call(
        body, name=name,
        grid_spec=pltpu.PrefetchScalarGridSpec(num_scalar_prefetch=1, grid=(2, nh), in_specs=[full, half, half, full, full],
                                               out_specs=[full] * 4),
        out_shape=[jax.ShapeDtypeStruct(wt.shape, F32)] * 4,
        compiler_params=_params(("parallel", "parallel")),
    )(core, wt, gt_mine, gt_other, mt, vt)


def add_pair(dw, recv, core, *, name):
    n4, k, n = dw.shape
    half = (1, k // 2, n) if split_axis(k) == 0 else (1, k, n // 2)
    mine = (lambda q, c: (q, c[0], 0)) if split_axis(k) == 0 else (lambda q, c: (q, 0, c[0]))

    def body(c_ref, a_ref, b_ref, o_ref):
        o_ref[...] = (a_ref[...] + b_ref[...].astype(F32)).astype(BF16)

    return pl.pallas_call(
        body, name=name,
        grid_spec=pltpu.PrefetchScalarGridSpec(
            num_scalar_prefetch=1, grid=(n4,),
            in_specs=[pl.BlockSpec(half, mine), pl.BlockSpec(half, lambda q, c: (q, 0, 0))],
            out_specs=pl.BlockSpec(half, lambda q, c: (q, 0, 0))),
        out_shape=jax.ShapeDtypeStruct((n4,) + half[1:], BF16),
        compiler_params=_params(("parallel",)),
    )(core, dw, recv)


def sum_chips(parts, *, name):
    n4, r, n = parts.shape
    if r % 16 == 0:
        br, bc = _row_block(r), n
    else:
        br, bc = r, LANES

    def body(p_ref, o_ref):
        acc = p_ref[0].astype(F32)
        for q in range(1, n4):
            acc = acc + p_ref[q].astype(F32)
        o_ref[...] = acc

    return pl.pallas_call(
        body, name=name, grid=(r // br, n // bc),
        in_specs=[pl.BlockSpec((n4, br, bc), lambda i, j: (0, i, j))], out_specs=pl.BlockSpec((br, bc), lambda i, j: (i, j)),
        out_shape=jax.ShapeDtypeStruct((r, n), F32),
        compiler_params=_params(("parallel", "parallel")),
    )(parts)


ANY = pl.BlockSpec(memory_space=pl.ANY)


def _place():
    x, y, c = lax.axis_index("x"), lax.axis_index("y"), lax.axis_index("c")
    chips = [(1 - x, y), (x, 1 - y), (1 - x, 1 - y)]
    return x, y, c, chips


def _copy(src, dst, send_sems, recv_sems, k, to):
    return pltpu.make_async_remote_copy(src_ref=src, dst_ref=dst, send_sem=send_sems.at[k], recv_sem=recv_sems.at[k],
                                        device_id=to, device_id_type=MESH)


def split_axis(rows):
    return 0 if rows % 32 == 0 else 1


def _half(ref, lead, hf):
    rows, cols = ref.shape[-2:]
    if split_axis(rows) == 0:
        at = (pl.ds(hf * (rows // 2), rows // 2), slice(None))
    else:
        at = (slice(None), pl.ds(hf * (cols // 2), cols // 2))
    return ref.at[at] if lead is None else ref.at[(lead,) + at]


def _gather_first(srcs, dsts, ssems, rsems):
    x, y, c, chips = _place()
    for ti, (s, d) in enumerate(zip(srcs, dsts)):
        for j, (cx, cy) in enumerate(chips):
            _copy(_half(s, None, c), _half(d, 2 * x + y, c), ssems, rsems, 3 * ti + j, (cx, cy, c)).start()


def _gather_mid(srcs, dsts, ssems, rsems):
    x, y, c, chips = _place()
    n1 = 3 * len(srcs)
    for ti, d in enumerate(dsts):
        for j, (cx, cy) in enumerate(chips):
            landed = _half(d, 2 * cx + cy, c)
            _copy(landed, landed, ssems, rsems, 3 * ti + j, (cx, cy, c)).wait_recv()
            _copy(landed, landed, ssems, rsems, n1 + 3 * ti + j, (x, y, 1 - c)).start()


def _gather_last(srcs, dsts, ssems, rsems):
    x, y, c, chips = _place()
    n1 = 3 * len(srcs)
    for ti, (s, d) in enumerate(zip(srcs, dsts)):
        for j, (cx, cy) in enumerate(chips):
            other = _half(d, 2 * cx + cy, 1 - c)
            _copy(other, other, ssems, rsems, n1 + 3 * ti + j, (x, y, 1 - c)).wait_recv()
        for j, (cx, cy) in enumerate(chips):
            mine = _half(s, None, c)
            _copy(mine, mine, ssems, rsems, 3 * ti + j, (cx, cy, c)).wait_send()
            _copy(mine, mine, ssems, rsems, n1 + 3 * ti + j, (x, y, 1 - c)).wait_send()


def gather_side(shards):
    return Side(shards, [jax.ShapeDtypeStruct((N_CHIPS,) + s.shape, s.dtype) for s in shards], 6 * len(shards),
                _gather_first, _gather_last, _gather_mid)


def _scatter_first(srcs, dsts, ssems, rsems):
    x, y, c, chips = _place()
    for ti, (s, d) in enumerate(zip(srcs, dsts)):
        for j, (cx, cy) in enumerate(chips):
            _copy(s.at[2 * cx + cy], d.at[2 * x + y], ssems, rsems, 3 * ti + j, (cx, cy, c)).start()


def _scatter_last(srcs, dsts, ssems, rsems):
    x, y, c, chips = _place()
    for ti, (s, d) in enumerate(zip(srcs, dsts)):
        for j, (cx, cy) in enumerate(chips):
            _copy(s.at[2 * cx + cy], d.at[2 * cx + cy], ssems, rsems, 3 * ti + j, (cx, cy, c)).wait_recv()
        for j, (cx, cy) in enumerate(chips):
            _copy(s.at[2 * cx + cy], d.at[2 * cx + cy], ssems, rsems, 3 * ti + j, (cx, cy, c)).wait_send()


def scatter_side(parts):
    return Side(parts, [jax.ShapeDtypeStruct(p.shape, p.dtype) for p in parts], 3 * len(parts), _scatter_first, _scatter_last)


def run_side(side, *, name):
    n_in, n_out = len(side.ins), len(side.out_shapes)

    def body(*refs):
        si, so, sems = _split_refs(refs, [n_in, n_out, 2])
        side.first(si, so, *sems)
        if side.mid is not None:
            side.mid(si, so, *sems)
        side.last(si, so, *sems)

    in_specs, out_specs = side.specs()
    return pl.pallas_call(body, name=name, in_specs=in_specs, out_specs=out_specs, out_shape=side.out_shapes,
                          scratch_shapes=side.sems())(*side.ins)


def _swap_first(srcs, dsts, ssems, rsems):
    x, y, c, _ = _place()
    for k, (s, d) in enumerate(zip(srcs, dsts)):
        _copy(s, d, ssems, rsems, k, (x, y, 1 - c)).start()


def _swap_last(srcs, dsts, ssems, rsems):
    x, y, c, _ = _place()
    for k, (s, d) in enumerate(zip(srcs, dsts)):
        _copy(s, d, ssems, rsems, k, (x, y, 1 - c)).wait()


def swap_side(xs):
    return Side(xs, [jax.ShapeDtypeStruct(a.shape, a.dtype) for a in xs], len(xs), _swap_first, _swap_last)


def _swap_halves(srcs, dsts, ssems, rsems):
    x, y, c, _ = _place()
    for k, (s, d) in enumerate(zip(srcs, dsts)):
        hk = s.shape[1] // 2
        yield _copy(s.at[:, pl.ds((1 - c) * hk, hk), :], d, ssems, rsems, k, (x, y, 1 - c))


def _swap_halves_first(srcs, dsts, ssems, rsems):
    for cp in _swap_halves(srcs, dsts, ssems, rsems):
        cp.start()


def _swap_halves_last(srcs, dsts, ssems, rsems):
    for cp in _swap_halves(srcs, dsts, ssems, rsems):
        cp.wait()


def swap_halves_side(xs):
    return Side(xs, [jax.ShapeDtypeStruct((a.shape[0], a.shape[1] // 2, a.shape[2]), a.dtype) for a in xs], len(xs),
                _swap_halves_first, _swap_halves_last)


def allreduce_small(s):
    n_dev = 8

    def body(s_ref, out_ref, buf, send_sems, recv_sems):
        x, y, c, _ = _place()
        me = 4 * x + 2 * y + c
        buf[me] = s_ref[...]
        sends = []
        for k in range(1, n_dev):
            px = 1 - x if k & 4 else x
            py = 1 - y if k & 2 else y
            pc = 1 - c if k & 1 else c
            cp = _copy(s_ref, buf.at[me], send_sems, recv_sems, k - 1, (px, py, pc))
            cp.start()
            sends.append((cp, 4 * px + 2 * py + pc))
        for k, (cp, peer) in enumerate(sends):
            _copy(s_ref, buf.at[peer], send_sems, recv_sems, k, (x, y, c)).wait_recv()
        for cp, _ in sends:
            cp.wait_send()
        acc = buf[0]
        for d in range(1, n_dev):
            acc = acc + buf[d]
        out_ref[...] = acc

    vm = pl.BlockSpec(memory_space=pltpu.VMEM)
    return pl.pallas_call(
        body, name="allreduce_small", in_specs=[vm], out_specs=vm,
        out_shape=jax.ShapeDtypeStruct(s.shape, F32),
        scratch_shapes=[pltpu.VMEM((n_dev,) + s.shape, F32), pltpu.SemaphoreType.DMA((n_dev - 1,)),
                        pltpu.SemaphoreType.DMA((n_dev - 1,))],
    )(s)


def join_cols(sm):
    n4, k, n = sm.shape
    return sm.transpose(1, 0, 2).reshape(k, n4 * n)


def split_cols(full):
    k, n = full.shape
    return full.reshape(k, N_CHIPS, n // N_CHIPS).transpose(1, 0, 2)


def _pad_heads(w, width):
    lead, heads = w.shape[:-1], w.shape[-1] // width
    w = w.reshape(lead + (heads, width))
    return jnp.pad(w, [(0, 0)] * len(lead) + [(0, 0), (0, LANES - width)]).reshape(lead + (heads * LANES,))


def _unpad_heads(w, width):
    lead, heads = w.shape[:-1], w.shape[-1] // LANES
    return w.reshape(lead + (heads, LANES))[..., :width].reshape(lead + (heads * width,))


O_F = 3 * FW
O_CQ = O_F + N_HEADS
O_CKV = O_CQ + Q_RANK
O_KR = O_CKV + KV_RANK
O_END = O_KR + ROPE_DIM


def _shard_rows(sm, a, b):
    r = sm.shape[1]
    out = []
    while a < b:
        q = a // r
        e = min(b, (q + 1) * r)
        out.append(sm[q, a - q * r:e - q * r])
        a = e
    return out


def split_w_in_t(w_sm):
    d = w_sm.shape[2]

    def z(n):
        return [jnp.zeros((n, d), w_sm.dtype)]

    small = (_shard_rows(w_sm, O_CQ, O_CKV) + _shard_rows(w_sm, O_CKV, O_KR) + z(HEAD_DIM) + _shard_rows(w_sm, O_KR, O_END)
             + z(LANES - HEAD_DIM - ROPE_DIM) + _shard_rows(w_sm, O_F, O_CQ) + z(LANES - N_HEADS) + z(LANES))
    return jnp.concatenate(_shard_rows(w_sm, 0, O_F), axis=0), jnp.concatenate(small, axis=0)


def join_w_in_t(d_qkv_t, d_small_t):
    kr = S_KR + HEAD_DIM
    segments = [(d_qkv_t, 0, 0, O_F), (d_small_t, S_F, O_F, N_HEADS), (d_small_t, S_CQ, O_CQ, Q_RANK),
                (d_small_t, S_CKV, O_CKV, KV_RANK), (d_small_t, kr, O_KR, ROPE_DIM)]
    r = O_END // N_CHIPS
    shards = []
    for q in range(N_CHIPS):
        pieces = []
        for src, s0, v0, n in segments:
            a, b = max(v0, q * r), min(v0 + n, (q + 1) * r)
            if a < b:
                pieces.append(src[s0 + a - v0:s0 + b - v0])
        shards.append(jnp.concatenate(pieces, axis=0))
    return jnp.stack(shards)


def rope_tables(pos):
    t = pos.shape[0]
    inv_freq = ROPE_THETA ** (-jnp.arange(0, ROPE_DIM, 2, dtype=F32) / ROPE_DIM)
    ang = pos.astype(F32)[:, None] * inv_freq
    cos, sin = jnp.cos(ang), jnp.sin(ang)
    half = ROPE_DIM // 2

    def z(n):
        return jnp.zeros((t, n), F32)

    tab_c = jnp.concatenate([jnp.ones((t, HEAD_DIM), F32), cos, cos, z(LANES - HEAD_DIM - ROPE_DIM)], axis=1)
    tab_a = jnp.concatenate([z(HEAD_DIM), -sin, z(half), z(LANES - HEAD_DIM - ROPE_DIM)], axis=1)
    tab_b = jnp.concatenate([z(HEAD_DIM), z(half), sin, z(LANES - HEAD_DIM - ROPE_DIM)], axis=1)
    return tab_c, tab_a, tab_b


def _pad_lanes(v, n):
    return jnp.pad(v, ((0, 0), (0, n - v.shape[1])))


ATTN_BLK = 512
ATTN_FWD_BLK = 1024
ATTN_BWD_QBLK = 1024
ACC_ROWS = HEAD_DIM + 16


def local_step(xs, pos, tgt, gains, early_weights, late_weights, early_side=None, fwd_sides=(None, None), reduction=None):
    g_attn, b_forget, g_q, g_kv, g_fo, g_mo, g_mlp, g_fin = gains
    t = xs.shape[0]
    blk = min(ATTN_BLK, t)
    fox_scale = 1.0 / (HEAD_DIM ** 0.5)
    mla_scale = 1.0 / ((HEAD_DIM + ROPE_DIM) ** 0.5)

    h1, *gathered = rmsnorm(xs, g_attn, out_dtype=BF16, name="norm_attn", side=early_side)
    w_in_t, w_uq_p, w_ukv = early_weights(gathered)
    w_qkv_t, w_small_t = split_w_in_t(w_in_t)
    kv = w_ukv.reshape(KV_RANK, N_HEADS, 2 * HEAD_DIM)
    w_ukv_p = jnp.concatenate([_pad_heads(kv[:, :, :HEAD_DIM].reshape(KV_RANK, FW), HEAD_DIM),
                               kv[:, :, HEAD_DIM:].reshape(KV_RANK, FW)], axis=1)
    b_f = _pad_lanes(b_forget, LANES)
    tab_c, tab_a, tab_b = rope_tables(pos)

    qkv, qkv_t = mm(h1, w_qkv_t, trans_b=True, out_dtypes=[BF16], t_blk=blk, name="proj_qkv")
    small, = mm(h1, w_small_t, trans_b=True, out_dtypes=[F32], name="proj_small")
    mq, mk, mv, lf, cqn, ckvn, mq_t, mv_t = mla_prep(small, g_q, g_kv, w_uq_p, w_ukv_p, tab_c, tab_a, tab_b, b_f,
                                                     name="mla_prep", bt=blk)
    f_cum = cumsum_rows(lf, reverse=False, name="gate_cumsum")
    f_blocks = f_cum[:, :N_HEADS].reshape(t // blk, blk, N_HEADS).transpose(0, 2, 1)
    fo, st_f, *gathered = flash_fwd(qkv_t, qkv, qkv_t, f_cum, qoff=0, koff=PAIRS, voff=2 * PAIRS, pair=True,
                                    scale=fox_scale, name="fox_fwd", blk=ATTN_FWD_BLK, side=fwd_sides[0])
    mo, st_m, *more = flash_fwd(mq_t, mk, mv_t, None, qoff=0, koff=0, voff=0, pair=False, scale=mla_scale, name="mla_fwd",
                                blk=ATTN_FWD_BLK, side=fwd_sides[1])
    w_o, w_up, w_down = late_weights(gathered + more)
    mixed = mix_norm(fo, mo, g_fo, g_mo, name="norm_mix")

    def inv_rms(v):
        return lax.rsqrt(jnp.mean(v * v, axis=-1, keepdims=True) + EPS)

    def residual_then_norm(acc, res, g):
        xn = acc + res
        return xn, xn * inv_rms(xn) * g

    def norm_bwd(dh, xn, res, g):
        r = inv_rms(xn)
        uu = dh * g
        return (r * uu - xn * (r * r * r * jnp.mean(uu * xn, axis=-1, keepdims=True)) + res,
                jnp.sum(dh * (xn * r), axis=0, keepdims=True))

    def norm_bwd2(dh, xn, res, g):
        dx, dg = norm_bwd(dh, xn, res, g)
        return dx, dx, dg

    def residual_then_loss(acc, res, target, g):
        xn = acc + res
        r = inv_rms(xn)
        xh = xn * r
        e = xh * g - target
        part = 0.5 * jnp.sum(jnp.mean(e * e, axis=-1, keepdims=True), axis=0, keepdims=True)
        dy = e * (1.0 / xn.shape[1])
        uu = dy * g
        dx = r * uu - xn * (r * r * r * jnp.mean(uu * xn, axis=-1, keepdims=True))
        return dx, dx, jnp.sum(dy * xh, axis=0, keepdims=True), part + jnp.zeros_like(g)

    x1, h2 = mm(mixed, w_o, extras=[xs], vecs=[g_mlp], epilogue=residual_then_norm, out_dtypes=[F32, BF16], name="out_proj")

    def relu2(uu):
        r = jnp.maximum(uu.astype(F32), 0.0)
        return (r * r).astype(BF16)

    u, = mm(h2, w_up, out_dtypes=[BF16], name="mlp_up", bn=2048)
    dx2, dx2b, dg_fin, loss_row = mm(u, w_down, a_pro=relu2, extras=[x1, tgt], vecs=[g_fin], epilogue=residual_then_loss,
                                     out_dtypes=[F32, BF16], n_sums=2, name="mlp_down_loss")
    loss = loss_row[:, :1]

    def relu2_grad(acc, uu):
        return (acc * (2.0 * jnp.maximum(uu.astype(F32), 0.0)),)

    du, = mm(dx2b, w_down, trans_b=True, extras=[u], epilogue=relu2_grad, out_dtypes=[BF16], name="mlp_down_bwd", bn=2048)
    dw_down, dw_down_b = (g.reshape(N_CHIPS, -1, w_down.shape[1])
                          for g in mm_tn(u, dx2b, a_pro=relu2, name="dw_down", bf16_copy=True))
    dx1, dx1b, dg_mlp = mm(du, w_up, trans_b=True, extras=[x1, dx2], vecs=[g_mlp], epilogue=norm_bwd2,
                           out_dtypes=[F32, BF16], n_sums=1, name="mlp_up_bwd")
    dw_up, dw_up_b = mm_tn(h2, du, name="dw_up", col_shards=N_CHIPS, bf16_copy=True)

    dw_o, dw_o_b = (g.reshape(N_CHIPS, -1, w_o.shape[1]) for g in mm_tn(mixed, dx1b, name="dw_o", bf16_copy=True))
    late, late_b = (dw_o, dw_up, dw_down), (dw_o_b, dw_up_b, dw_down_b)
    red = reduction
    dfo, dmo, dg_fo, dg_mo, st_f, st_m, dfo_t, dmo_t, *got = mix_norm_bwd(
        dx1b, w_o, fo, mo, g_fo, g_mo, st_f, st_m, name="out_proj_mix_bwd", bt=blk,
        side=red.late_swap(late, late_b) if red else None)
    dfq, dfk, dfv, d_f, *got = flash_bwd(
        qkv, qkv_t, qkv, qkv, dfo, dfo_t, st_f, f_blocks, qoff=0, koff=PAIRS, voff=2 * PAIRS, pair=True,
        scale=fox_scale, name="fox_bwd", qblk=ATTN_BWD_QBLK, side=red.late_scatter(got) if red else None)
    dmq, dmk, dmv, *got = flash_bwd(mq, mq_t, mk, mv, dmo, dmo_t, st_m, None, qoff=0, koff=0, voff=0,
                                    pair=False, scale=mla_scale, name="mla_bwd", qblk=ATTN_BWD_QBLK,
                                    side=red.late_halves(got) if red else None)
    if red:
        red.late_done(got)
    dqkv = jnp.concatenate([dfq, dfk, dfv], axis=1).astype(BF16)
    dlf = cumsum_rows(d_f, reverse=True, name="gate_cumsum_bwd")
    dsmall, dq_u, dkv_u, dg_q, dg_kv, db_f = mla_prep_bwd(dmq, dmk, dmv, dlf, small, g_q, g_kv, w_uq_p, w_ukv_p,
                                                          tab_c, tab_a, tab_b, b_f, name="mla_prep_bwd")
    dw_uq_p = mm_tn(cqn, dq_u, name="dw_uq")
    dw_ukv_p = mm_tn(ckvn, dkv_u, name="dw_ukv")

    dw_in_t = join_w_in_t(mm_tn(dqkv, h1, name="dw_qkv"), mm_tn(dsmall, h1, name="dw_small"))
    dk_cols = _unpad_heads(dw_ukv_p[:, :HW], HEAD_DIM).reshape(KV_RANK, N_HEADS, HEAD_DIM)
    dv_cols = dw_ukv_p[:, HW:].reshape(KV_RANK, N_HEADS, HEAD_DIM)
    dw_ukv = jnp.concatenate([dk_cols, dv_cols], axis=2).reshape(KV_RANK, N_HEADS * 2 * HEAD_DIM)
    early = (dw_in_t, split_cols(dw_uq_p), split_cols(dw_ukv))
    grad_x, dg_attn, *got = mm([dqkv, dsmall], [w_qkv_t, w_small_t], extras=[xs, dx1], vecs=[g_attn],
                               epilogue=norm_bwd, out_dtypes=[F32], n_sums=1, name="proj_bwd",
                               side=red.early_scatter(early) if red else None)
    if red:
        red.early_done(got)
    d_gains = (dg_attn, db_f[:, :N_HEADS], dg_q, dg_kv, dg_fo, dg_mo, dg_mlp, dg_fin)
    return loss, grad_x, early, late, d_gains


class GradReduction:
    def __init__(self, core_id, chip):
        self.core_id, self.chip = core_id, chip
        self.core = core_id.reshape(1).astype(jnp.int32)
        self.grads, self.pairs, self.halves, self.others = {}, {}, {}, {}

    def _other_halves(self, grads):
        out = []
        for g in grads:
            axis = 1 + split_axis(g.shape[1])
            size = g.shape[axis] // 2
            out.append(lax.dynamic_slice_in_dim(g, (1 - self.core_id) * size, size, axis=axis).astype(BF16))
        return out

    def _add_pairs(self, group, recvs):
        self.pairs[group] = [add_pair(g, r, self.core, name="add_pair_%s_%d" % (group, n))
                             for n, (g, r) in enumerate(zip(self.grads[group], recvs))]
        return scatter_side(self.pairs[group])

    def _chip_sums(self, group, scattered):
        chip = self.chip
        with_mine = [lax.dynamic_update_index_in_dim(s, lax.dynamic_index_in_dim(p, chip, 0, keepdims=True), chip, 0)
                     for s, p in zip(scattered, self.pairs[group])]
        self.halves[group] = [sum_chips(s, name="sum_chips_%s_%d" % (group, n)) for n, s in enumerate(with_mine)]
        return self.halves[group]

    def late_swap(self, grads, bf16_copies):
        self.grads["late"] = list(grads)
        return swap_halves_side(list(bf16_copies))

    def late_scatter(self, recvs):
        return self._add_pairs("late", recvs)

    def late_halves(self, scattered):
        return swap_side(self._chip_sums("late", scattered))

    def late_done(self, others):
        self.others["late"] = list(others)

    def early_scatter(self, grads):
        self.grads["early"] = list(grads)
        return self._add_pairs("early", run_side(swap_side(self._other_halves(grads)), name="swap_early_sends"))

    def early_done(self, scattered):
        self.others["early"] = list(run_side(swap_side(self._chip_sums("early", scattered)), name="swap_early_halves"))

def kernel(x, positions, attn_norm_g, w_in, b_forget, q_norm_g, w_uq, kv_norm_g, w_ukv, fox_out_g, mla_out_g, w_o, mlp_norm_g, w_up, w_down, final_norm_g, loss_target, m_attn_norm_g, m_w_in, m_b_forget, m_q_norm_g, m_w_uq, m_kv_norm_g, m_w_ukv, m_fox_out_g, m_mla_out_g, m_w_o, m_mlp_norm_g, m_w_up, m_w_down, m_final_norm_g, v_attn_norm_g, v_w_in, v_b_forget, v_q_norm_g, v_w_uq, v_kv_norm_g, v_w_ukv, v_fox_out_g, v_mla_out_g, v_w_o, v_mlp_norm_g, v_w_up, v_w_down, v_final_norm_g):
    core_id = lax.axis_index("c")
    core = core_id.reshape(1).astype(jnp.int32)
    chip = 2 * lax.axis_index("x") + lax.axis_index("y")
    big = [w_in, w_uq, w_ukv, w_o, w_up, w_down]
    big_m = [m_w_in, m_w_uq, m_w_ukv, m_w_o, m_w_up, m_w_down]
    big_v = [v_w_in, v_w_uq, v_w_ukv, v_w_o, v_w_up, v_w_down]
    n_early = 3

    def vec(a):
        return a.reshape(1, -1)

    small = [attn_norm_g, b_forget, q_norm_g, kv_norm_g, fox_out_g, mla_out_g, mlp_norm_g, final_norm_g]
    small_m = [m_attn_norm_g, m_b_forget, m_q_norm_g, m_kv_norm_g, m_fox_out_g, m_mla_out_g, m_mlp_norm_g, m_final_norm_g]
    small_v = [v_attn_norm_g, v_b_forget, v_q_norm_g, v_kv_norm_g, v_fox_out_g, v_mla_out_g, v_mlp_norm_g, v_final_norm_g]
    gains = [vec(a) for a in small]

    views = [big[0][0].T, _pad_heads(big[1][0], HEAD_DIM + ROPE_DIM)] + [w[0] for w in big[2:]]
    shards = [v.astype(BF16) for v in views]

    def with_own(gathered, mine):
        return [lax.dynamic_update_index_in_dim(g, s, chip, 0) for g, s in zip(gathered, mine)]

    def early_weights(gathered):
        g_in, g_uq, g_ukv = with_own(gathered, shards[:n_early])
        return g_in, join_cols(g_uq), join_cols(g_ukv)

    def late_weights(gathered):
        g_o, g_up, g_down = with_own(gathered, shards[n_early:])
        return g_o.reshape(-1, g_o.shape[2]), join_cols(g_up), g_down.reshape(-1, g_down.shape[2])

    reduction = GradReduction(core_id, chip)
    loss, grad_x, _, _, d_small = local_step(
        x[0], positions[0], loss_target[0], gains, early_weights, late_weights, gather_side(shards[:n_early]),
        (gather_side(shards[n_early:-1]), gather_side(shards[-1:])), reduction)
    halves = reduction.halves["early"] + reduction.halves["late"]
    others = reduction.others["early"] + reduction.others["late"]

    def rows8(vs):
        return jnp.concatenate([_pad_lanes(vec(a).astype(F32), 1024) for a in vs], axis=0)

    with_loss = [jnp.concatenate([d, loss], axis=1) if n == 1 else d for n, d in enumerate(d_small)]
    g_small8 = allreduce_small(rows8(with_loss))

    outs_big = []
    for n, (w, gm, go, m, v) in enumerate(zip(big, halves, others, big_m, big_v)):
        if n == 0:
            outs = adamw_halves_t(w[0].T, gm, go, m[0].T, v[0].T, core, name="adamw_%d" % n)
            outs_big.append([o.T[None] for o in outs])
        else:
            if n == 1:
                gm, go = (_unpad_heads(gh, HEAD_DIM + ROPE_DIM) for gh in (gm, go))
            outs_big.append(adamw_halves(w, gm, go, m, v, core, name="adamw_%d" % n))
    d8, m8, v8 = adamw(rows8(small), g_small8, rows8(small_m), rows8(small_v), name="adamw_small")

    def unrows8(a8):
        return [a8[n, :s.size].reshape(s.shape) for n, s in enumerate(small)]

    loss_all = g_small8[1, N_HEADS]
    grads, deltas, new_m, new_v = [None] * 14, [None] * 14, [None] * 14, [None] * 14
    big_at = [1, 4, 6, 9, 11, 12]
    small_at = [0, 2, 3, 5, 7, 8, 10, 13]
    for n, at in enumerate(big_at):
        grads[at], deltas[at], new_m[at], new_v[at] = outs_big[n]
    for at, g, dd, mm_, vv in zip(small_at, unrows8(g_small8), unrows8(d8), unrows8(m8), unrows8(v8)):
        grads[at], deltas[at], new_m[at], new_v[at] = g, dd, mm_, vv
    return (loss_all, grad_x[None], *grads, *deltas, *new_m, *new_v)
```

```python
import jax
import jax.numpy as jnp
from jax import lax
from jax.experimental import pallas as pl
from jax.experimental.pallas import tpu as pltpu

F32 = jnp.float32
BF16 = jnp.bfloat16
MESH = pl.DeviceIdType.MESH

EPS = 1e-6
ROPE_THETA = 10000.0
N_HEADS = 8
PAIRS = N_HEADS // 2
HEAD_DIM = 64
ROPE_DIM = 32
LANES = 128
Q_RANK = 384
KV_RANK = 256
N_CHIPS = 4
ADAM_LR, ADAM_B1, ADAM_B2, ADAM_EPS, ADAM_WD, ADAM_STEP = 0.001, 0.9, 0.999, 1e-08, 0.01, 10
VMEM_LIMIT = 48 * 1024 * 1024
LOG2E = 1.4426950408889634
LN2 = 0.6931471805599453
NN = (((1,), (0,)), ((), ()))
NT = (((1,), (1,)), ((), ()))
TN = (((0,), (0,)), ((), ()))


def _params(sem=None):
    return pltpu.CompilerParams(dimension_semantics=sem, vmem_limit_bytes=VMEM_LIMIT)


def _fit(block, dim):
    if dim <= block:
        return dim
    return next(b for b in range(block - block % LANES, 0, -LANES) if dim % b == 0)


def _row_block(rows):
    return next(b for b in (256, 128, 64, 32, 16, 8) if rows % b == 0)


def gridded(body, *, name, grid, in_specs, out_specs, out_shape, ins, semantics, side=None):
    if side is None:
        return pl.pallas_call(body, name=name, grid=grid, in_specs=in_specs, out_specs=out_specs, out_shape=out_shape,
                              compiler_params=_params(semantics))(*ins)
    n_in, n_out = len(in_specs), len(out_specs)
    steps = 1
    for extent in grid:
        steps *= extent

    def riding(*refs):
        main_in, s_in, main_out, s_out, sems = _split_refs(refs, [n_in, len(side.ins), n_out, len(side.out_shapes), 2])
        step = 0
        for axis, extent in enumerate(grid):
            step = step * extent + pl.program_id(axis)

        @pl.when(step == 0)
        def _():
            side.first(s_in, s_out, *sems)

        body(*main_in, *main_out)

        @pl.when(step == steps - 1)
        def _():
            if side.mid is not None:
                side.mid(s_in, s_out, *sems)
            side.last(s_in, s_out, *sems)

    s_in_specs, s_out_specs = side.specs()
    return pl.pallas_call(
        riding, name=name, grid=grid, in_specs=list(in_specs) + s_in_specs, out_specs=list(out_specs) + s_out_specs,
        out_shape=list(out_shape) + side.out_shapes, scratch_shapes=side.sems(),
        compiler_params=_params(("arbitrary",) * len(grid)))(*ins, *side.ins)


def rmsnorm(x, g, *, out_dtype, name, bt=512, side=None):
    t, d = x.shape
    bt = min(bt, t)

    def body(x_ref, g_ref, o_ref):
        xv = x_ref[...].astype(F32)
        r = lax.rsqrt(jnp.mean(xv * xv, axis=-1, keepdims=True) + EPS)
        o_ref[...] = (xv * r * g_ref[...]).astype(o_ref.dtype)

    return gridded(
        body, name=name, grid=(t // bt,),
        in_specs=[pl.BlockSpec((bt, d), lambda i: (i, 0)), pl.BlockSpec((1, d), lambda i: (0, 0))],
        out_specs=[pl.BlockSpec((bt, d), lambda i: (i, 0))],
        out_shape=[jax.ShapeDtypeStruct((t, d), out_dtype)],
        ins=[x, g], semantics=("parallel",), side=side)


def mm(a, b, *, trans_b=False, a_pro=None, extras=(), vecs=(), epilogue=None, out_dtypes, n_sums=0, t_blk=None, name,
       bm=1024, bn=1024, side=None):
    a_list = list(a) if isinstance(a, (list, tuple)) else [a]
    b_list = list(b) if isinstance(b, (list, tuple)) else [b]
    m = a_list[0].shape[0]
    n = b_list[0].shape[0] if trans_b else b_list[0].shape[1]
    ks = [x.shape[1] for x in a_list]
    if sum(ks) > 2048:
        bm = bm // 2
    bm, bn = _fit(bm, m), _fit(bn, n)
    assert n_sums == 0 or bn == n
    n_ab, n_ex, n_vec, n_out = len(a_list), len(extras), len(vecs), len(out_dtypes)
    n_t = 0 if t_blk is None else 1

    def body(*refs):
        a_refs, b_refs, ex, vs, outs, t_outs, sums = _split_refs(refs, [n_ab, n_ab, n_ex, n_vec, n_out, n_t, n_sums])
        acc = None
        for a_ref, b_ref in zip(a_refs, b_refs):
            a_tile = a_ref[...] if a_pro is None else a_pro(a_ref[...])
            part = lax.dot_general(a_tile, b_ref[...], NT if trans_b else NN, preferred_element_type=F32)
            acc = part if acc is None else acc + part
        res = epilogue(acc, *[e[...] for e in ex], *[v[...] for v in vs]) if epilogue is not None else (acc,)
        for o, r in zip(outs, res[:n_out]):
            o[...] = r.astype(o.dtype)
        for t_ref in t_outs:
            for u in range(bm // t_blk):
                t_ref[u] = res[0][u * t_blk:(u + 1) * t_blk, :].T.astype(t_ref.dtype)
        if n_sums:
            @pl.when(pl.program_id(0) == 0)
            def _():
                for s_ref in sums:
                    s_ref[...] = jnp.zeros_like(s_ref)

            for s_ref, r in zip(sums, res[n_out:]):
                s_ref[...] += r

    tile = pl.BlockSpec((bm, bn), lambda i, j: (i, j))
    vec = pl.BlockSpec((1, bn), lambda i, j: (0, j))
    t_specs, t_shapes = [], []
    if t_blk is not None:
        t_specs = [pl.BlockSpec((bm // t_blk, bn, t_blk), lambda i, j: (i, j, 0))]
        t_shapes = [jax.ShapeDtypeStruct((m // t_blk, n, t_blk), out_dtypes[0])]
    a_specs = [pl.BlockSpec((bm, k), lambda i, j: (i, 0)) for k in ks]
    b_specs = [pl.BlockSpec((bn, k), lambda i, j: (j, 0)) if trans_b else pl.BlockSpec((k, bn), lambda i, j: (0, j)) for k in ks]
    return gridded(
        body, name=name, grid=(m // bm, n // bn),
        in_specs=a_specs + b_specs + [tile] * n_ex + [vec] * n_vec,
        out_specs=[tile] * n_out + t_specs + [vec] * n_sums,
        out_shape=[jax.ShapeDtypeStruct((m, n), dt) for dt in out_dtypes] + t_shapes + [jax.ShapeDtypeStruct((1, n), F32)] * n_sums,
        ins=[*a_list, *b_list, *extras, *vecs],
        semantics=("arbitrary", "arbitrary") if n_sums else ("parallel", "parallel"), side=side)


def mm_tn(a, b, *, a_pro=None, name, col_shards=1, bf16_copy=False, bk=1024, bn=1024, bt=2048):
    t, k = a.shape
    n = b.shape[1]
    ns = n // col_shards
    bk, bn, bt = _fit(bk, k), _fit(bn, ns), _fit(bt, t)
    per = ns // bn
    last = t // bt - 1

    def body(a_ref, b_ref, o_ref, *copy_ref):
        @pl.when(pl.program_id(2) == 0)
        def _():
            o_ref[...] = jnp.zeros_like(o_ref)

        a_tile = a_ref[...] if a_pro is None else a_pro(a_ref[...])
        o_ref[...] += lax.dot_general(a_tile, b_ref[...], TN, preferred_element_type=F32)
        if bf16_copy:
            @pl.when(pl.program_id(2) == last)
            def _():
                copy_ref[0][...] = o_ref[...].astype(BF16)

    if col_shards == 1:
        out_spec = pl.BlockSpec((bk, bn), lambda i, j, s: (i, j))
        shape = (k, n)
    else:
        out_spec = pl.BlockSpec((None, bk, bn), lambda i, j, s: (j // per, i, j % per))
        shape = (col_shards, k, ns)
    out_specs, out_shape = out_spec, jax.ShapeDtypeStruct(shape, F32)
    if bf16_copy:
        out_specs, out_shape = [out_spec, out_spec], [out_shape, jax.ShapeDtypeStruct(shape, BF16)]
    return pl.pallas_call(
        body, name=name, grid=(k // bk, n // bn, t // bt),
        in_specs=[pl.BlockSpec((bt, bk), lambda i, j, s: (s, i)), pl.BlockSpec((bt, bn), lambda i, j, s: (s, j))],
        out_specs=out_specs, out_shape=out_shape,
        compiler_params=_params(("parallel", "parallel", "arbitrary")),
    )(a, b)


def _split3(x):
    hi = x.astype(BF16)
    r1 = x - hi.astype(F32)
    mid = r1.astype(BF16)
    lo = (r1 - mid.astype(F32)).astype(BF16)
    return hi, mid, lo


def cumsum_rows(x, *, reverse, name, bc=512):
    t, d = x.shape
    bc = min(bc, t)
    nb = t // bc

    def body(x_ref, o_ref, carry):
        @pl.when(pl.program_id(0) == 0)
        def _():
            carry[...] = jnp.zeros_like(carry)

        r = lax.broadcasted_iota(jnp.int32, (bc, bc), 0)
        c = lax.broadcasted_iota(jnp.int32, (bc, bc), 1)
        tri = jnp.where((r <= c) if reverse else (r >= c), 1.0, 0.0).astype(BF16)
        hi, mid, lo = _split3(x_ref[...])
        s = (lax.dot_general(tri, hi, NN, preferred_element_type=F32)
             + lax.dot_general(tri, mid, NN, preferred_element_type=F32)
             + lax.dot_general(tri, lo, NN, preferred_element_type=F32)) + carry[0:1, :]
        o_ref[...] = s
        carry[0:1, :] = s[0:1, :] if reverse else s[bc - 1:bc, :]

    imap = (lambda i: (nb - 1 - i, 0)) if reverse else (lambda i: (i, 0))
    return pl.pallas_call(
        body, name=name, grid=(nb,),
        in_specs=[pl.BlockSpec((bc, d), imap)], out_specs=pl.BlockSpec((bc, d), imap),
        out_shape=jax.ShapeDtypeStruct((t, d), F32),
        scratch_shapes=[pltpu.VMEM((8, d), F32)],
        compiler_params=_params(("arbitrary",)),
    )(x)


def _rope(x, c, a, b):
    return x * c + pltpu.roll(x, LANES - ROPE_DIM // 2, 1) * a + pltpu.roll(x, ROPE_DIM // 2, 1) * b


def _rope_bwd(d, c, a, b):
    return d * c + pltpu.roll(d * a, ROPE_DIM // 2, 1) + pltpu.roll(d * b, LANES - ROPE_DIM // 2, 1)


S_CQ, S_CKV, S_KR, S_F, S_END = 0, Q_RANK, Q_RANK + KV_RANK, Q_RANK + KV_RANK + LANES, 1024
HW = N_HEADS * LANES
FW = N_HEADS * HEAD_DIM


def mla_prep(small, g_q, g_kv, w_uq, w_ukv, tab_c, tab_a, tab_b, b_f, *, name, bt=512):
    t = small.shape[0]
    bt = min(bt, t)

    def body(s_ref, gq_ref, gkv_ref, wq_ref, wkv_ref, c_ref, a_ref, b_ref, bf_ref,
             mq_ref, mk_ref, mv_ref, lf_ref, cqn_ref, ckvn_ref, mqt_ref, mvt_ref):
        cq = s_ref[:, S_CQ:S_CKV]
        rq = lax.rsqrt(jnp.mean(cq * cq, axis=-1, keepdims=True) + EPS)
        cqn = (cq * rq * gq_ref[...]).astype(BF16)
        ckv = s_ref[:, S_CKV:S_KR]
        rkv = lax.rsqrt(jnp.mean(ckv * ckv, axis=-1, keepdims=True) + EPS)
        ckvn = (ckv * rkv * gkv_ref[...]).astype(BF16)
        cqn_ref[...] = cqn
        ckvn_ref[...] = ckvn
        tc, ta, tb = c_ref[...], a_ref[...], b_ref[...]
        q = jnp.dot(cqn, wq_ref[...], preferred_element_type=F32)
        kv = jnp.dot(ckvn, wkv_ref[...], preferred_element_type=F32)
        kr = _rope(s_ref[:, S_KR:S_F], tc, ta, tb)
        for h in range(N_HEADS):
            sl = slice(h * LANES, (h + 1) * LANES)
            roped = _rope(q[:, sl], tc, ta, tb)
            mq_ref[:, sl] = roped.astype(BF16)
            mqt_ref[0, sl, :] = roped.T.astype(BF16)
            mk_ref[:, sl] = (kv[:, sl] + kr).astype(BF16)
        mv_ref[...] = kv[:, HW:].astype(BF16)
        mvt_ref[0] = kv[:, HW:].T.astype(BF16)
        z = s_ref[:, S_F:S_END - LANES] + bf_ref[...]
        lf_ref[...] = jnp.minimum(z, 0.0) - jnp.log(1.0 + jnp.exp(-jnp.abs(z)))

    def row(w):
        return pl.BlockSpec((bt, w), lambda i: (i, 0))

    def full(arr):
        return pl.BlockSpec(arr.shape, lambda i: (0, 0))

    return pl.pallas_call(
        body, name=name, grid=(t // bt,),
        in_specs=[row(S_END), full(g_q), full(g_kv), full(w_uq), full(w_ukv), row(LANES), row(LANES), row(LANES), full(b_f)],
        out_specs=[row(HW), row(HW), row(FW), row(LANES), row(Q_RANK), row(KV_RANK),
                   pl.BlockSpec((1, HW, bt), lambda i: (i, 0, 0)), pl.BlockSpec((1, FW, bt), lambda i: (i, 0, 0))],
        out_shape=[jax.ShapeDtypeStruct((t, HW), BF16)] * 2 + [jax.ShapeDtypeStruct((t, FW), BF16), jax.ShapeDtypeStruct((t, LANES), F32),
                   jax.ShapeDtypeStruct((t, Q_RANK), BF16), jax.ShapeDtypeStruct((t, KV_RANK), BF16),
                   jax.ShapeDtypeStruct((t // bt, HW, bt), BF16), jax.ShapeDtypeStruct((t // bt, FW, bt), BF16)],
        compiler_params=_params(("parallel",)),
    )(small, g_q, g_kv, w_uq, w_ukv, tab_c, tab_a, tab_b, b_f)


def mla_prep_bwd(dmq, dmk, dmv, dlf, small, g_q, g_kv, w_uq, w_ukv, tab_c, tab_a, tab_b, b_f, *, name, bt=512):
    t = small.shape[0]
    bt = min(bt, t)

    def body(dmq_ref, dmk_ref, dmv_ref, dlf_ref, s_ref, gq_ref, gkv_ref, wq_ref, wkv_ref, c_ref, a_ref, b_ref, bf_ref,
             ds_ref, dq_ref, dkv_ref, dgq_ref, dgkv_ref, db_ref):
        tc, ta, tb = c_ref[...], a_ref[...], b_ref[...]
        lane = lax.broadcasted_iota(jnp.int32, (1, LANES), 1)
        dkr = jnp.zeros((bt, LANES), F32)
        for h in range(N_HEADS):
            sl = slice(h * LANES, (h + 1) * LANES)
            dq_ref[:, sl] = _rope_bwd(dmq_ref[:, sl], tc, ta, tb).astype(BF16)
            dkr = dkr + dmk_ref[:, sl]
        dkv_ref[:, :HW] = dmk_ref[...].astype(BF16)
        dkv_ref[:, HW:] = dmv_ref[...].astype(BF16)
        in_rope = (lane >= HEAD_DIM) & (lane < HEAD_DIM + ROPE_DIM)
        ds_ref[:, S_KR:S_F] = jnp.where(in_rope, _rope_bwd(dkr, tc, ta, tb), 0.0).astype(BF16)

        def norm_bwd(raw, g_ref, dn, dg_ref):
            r = lax.rsqrt(jnp.mean(raw * raw, axis=-1, keepdims=True) + EPS)
            u = dn * g_ref[...]
            dot = jnp.mean(u * raw, axis=-1, keepdims=True)
            dg_ref[...] += jnp.sum(dn * (raw * r), axis=0, keepdims=True)
            return r * u - raw * (r * r * r * dot)

        @pl.when(pl.program_id(0) == 0)
        def _():
            dgq_ref[...] = jnp.zeros_like(dgq_ref)
            dgkv_ref[...] = jnp.zeros_like(dgkv_ref)
            db_ref[...] = jnp.zeros_like(db_ref)

        dcqn = lax.dot_general(dq_ref[...], wq_ref[...], NT, preferred_element_type=F32)
        ds_ref[:, S_CQ:S_CKV] = norm_bwd(s_ref[:, S_CQ:S_CKV], gq_ref, dcqn, dgq_ref).astype(BF16)
        dckvn = lax.dot_general(dkv_ref[...], wkv_ref[...], NT, preferred_element_type=F32)
        ds_ref[:, S_CKV:S_KR] = norm_bwd(s_ref[:, S_CKV:S_KR], gkv_ref, dckvn, dgkv_ref).astype(BF16)
        z = s_ref[:, S_F:S_END - LANES] + bf_ref[...]
        dz = jnp.where(lane < N_HEADS, dlf_ref[...] / (1.0 + jnp.exp(z)), 0.0)
        db_ref[...] += jnp.sum(dz, axis=0, keepdims=True)
        ds_ref[:, S_F:S_END - LANES] = dz.astype(BF16)
        ds_ref[:, S_END - LANES:] = jnp.zeros((bt, LANES), BF16)

    def row(w):
        return pl.BlockSpec((bt, w), lambda i: (i, 0))

    def full(arr):
        return pl.BlockSpec(arr.shape, lambda i: (0, 0))

    def vec(w):
        return pl.BlockSpec((1, w), lambda i: (0, 0))

    return pl.pallas_call(
        body, name=name, grid=(t // bt,),
        in_specs=[row(HW), row(HW), row(FW), row(LANES), row(S_END), full(g_q), full(g_kv), full(w_uq), full(w_ukv),
                  row(LANES), row(LANES), row(LANES), full(b_f)],
        out_specs=[row(S_END), row(HW), row(HW + FW), vec(Q_RANK), vec(KV_RANK), vec(LANES)],
        out_shape=[jax.ShapeDtypeStruct((t, S_END), BF16), jax.ShapeDtypeStruct((t, HW), BF16),
                   jax.ShapeDtypeStruct((t, HW + FW), BF16), jax.ShapeDtypeStruct((1, Q_RANK), F32),
                   jax.ShapeDtypeStruct((1, KV_RANK), F32), jax.ShapeDtypeStruct((1, LANES), F32)],
        compiler_params=_params(("arbitrary",)),
    )(dmq, dmk, dmv, dlf, small, g_q, g_kv, w_uq, w_ukv, tab_c, tab_a, tab_b, b_f)


class Side:
    def __init__(self, ins, out_shapes, n_sems, first, last, mid=None):
        self.ins, self.out_shapes, self.n_sems = list(ins), list(out_shapes), n_sems
        self.first, self.mid, self.last = first, mid, last

    def specs(self):
        return [ANY] * len(self.ins), [ANY] * len(self.out_shapes)

    def sems(self):
        return [pltpu.SemaphoreType.DMA((self.n_sems,)), pltpu.SemaphoreType.DMA((self.n_sems,))]


def _lane():
    return lax.broadcasted_iota(jnp.int32, (1, LANES), 1)


def _halves(x):
    zero = jnp.zeros_like(x)
    return [jnp.where(_lane() < HEAD_DIM, x, zero), jnp.where(_lane() >= HEAD_DIM, x, zero)]


def _groups(x):
    return [x[:, :LANES], x[:, LANES:]]


def _pick_row(tile, h):
    row = lax.broadcasted_iota(jnp.int32, (tile.shape[0], 1), 0)
    return jnp.sum(jnp.where(row == h, tile, 0.0), axis=0, keepdims=True)


def _pick_lane(tile, h):
    return jnp.sum(jnp.where(_lane() == h, tile, 0.0), axis=1, keepdims=True)


def _row_halves(x):
    row = lax.broadcasted_iota(jnp.int32, (LANES, 1), 0)
    zero = jnp.zeros_like(x)
    return [jnp.where(row < HEAD_DIM, x, zero), jnp.where(row >= HEAD_DIM, x, zero)]


def _below_diagonal(s):
    r = lax.broadcasted_iota(jnp.int32, s.shape, 0)
    c = lax.broadcasted_iota(jnp.int32, s.shape, 1)
    return jnp.where(c <= r, s, -jnp.inf)


def _above_diagonal(s):
    r = lax.broadcasted_iota(jnp.int32, s.shape, 0)
    c = lax.broadcasted_iota(jnp.int32, s.shape, 1)
    return jnp.where(r <= c, s, -jnp.inf)


def _split_refs(refs, counts):
    out, at = [], 0
    for n in counts:
        out.append(refs[at:at + n])
        at += n
    return out


def flash_fwd(qt_arr, k_arr, vt_arr, f_cum, *, qoff, koff, voff, pair, scale, name, blk=512, side=None):
    t = k_arr.shape[0]
    tblk = qt_arr.shape[2]
    blk = max(min(blk, t), tblk)
    sub = blk // tblk
    nb = t // blk
    steps = PAIRS * nb
    w = LANES if pair else 2 * LANES
    has_bias = f_cum is not None
    ins = [qt_arr, k_arr, vt_arr] + ([f_cum] if has_bias else [])

    def wide(ref, first):
        parts = [ref[first + u] for u in range(sub)]
        return parts[0] if sub == 1 else jnp.concatenate(parts, axis=1)
    s_ins, s_outs = (side.ins, side.out_shapes) if side else ([], [])

    def body(*refs):
        main, si, outs, so, sems = _split_refs(refs, [len(ins), len(s_ins), 2, len(s_outs), 2 if side else 0])
        qt_ref, k_ref, vt_ref = main[:3]
        f_ref = main[3] if has_bias else None
        o_ref, st_ref = outs
        g, i = pl.program_id(0), pl.program_id(1)
        step_id = g * nb + i
        if side:
            @pl.when(step_id == 0)
            def _():
                side.first(si, so, *sems)

            if side.mid is not None:
                @pl.when(step_id == (3 * steps) // 4)
                def _():
                    side.mid(si, so, *sems)

        qt = (wide(qt_ref, 0).astype(F32) * (scale * LOG2E)).astype(BF16)
        qts = _row_halves(qt) if pair else [qt[:LANES], qt[LANES:]]

        def k_of(kk, n):
            return kk if pair else kk[:, n * LANES:(n + 1) * LANES]

        def with_ones(vt_rows):
            return jnp.concatenate([vt_rows, jnp.ones((ACC_ROWS - HEAD_DIM, vt_rows.shape[1]), BF16)], axis=0)

        def step(j, carry, diagonal):
            rows = pl.ds(pl.multiple_of(j * blk, blk), blk)
            kk = k_ref[rows, :]
            vt = wide(vt_ref, sub * j)
            out = []
            for n in range(2):
                m, acc = carry[n]
                s = jnp.dot(k_of(kk, n), qts[n], preferred_element_type=F32)
                if has_bias:
                    s = s - LOG2E * _pick_lane(f_ref[rows, :], 2 * g + n)
                if diagonal:
                    s = _above_diagonal(s)
                m_new = jnp.maximum(m, jnp.max(s, axis=0, keepdims=True))
                p = jnp.exp2(s - m_new).astype(BF16)
                out.append((m_new, jnp.exp2(m - m_new) * acc
                            + jnp.dot(with_ones(vt[n * HEAD_DIM:(n + 1) * HEAD_DIM]), p, preferred_element_type=F32)))
            return tuple(out)

        def diagonal_in_halves(carry):
            h = tblk
            halves = [pl.ds(pl.multiple_of(i * blk, blk), h), pl.ds(pl.multiple_of(i * blk + h, h), h)]
            k_top, k_bot = k_ref[halves[0], :], k_ref[halves[1], :]
            vt_top, vt_bot = vt_ref[sub * i], vt_ref[sub * i + 1]
            out = []
            for n in range(2):
                m, acc = carry[n]
                heads = slice(n * HEAD_DIM, (n + 1) * HEAD_DIM)
                s_top = jnp.dot(k_of(k_top, n), qts[n], preferred_element_type=F32)
                s_bot = jnp.dot(k_of(k_bot, n), qts[n][:, h:], preferred_element_type=F32)
                if has_bias:
                    s_top = s_top - LOG2E * _pick_lane(f_ref[halves[0], :], 2 * g + n)
                    s_bot = s_bot - LOG2E * _pick_lane(f_ref[halves[1], :], 2 * g + n)
                s_top, s_bot = _above_diagonal(s_top), _above_diagonal(s_bot)
                m_top = jnp.maximum(m, jnp.max(s_top, axis=0, keepdims=True))
                m_new = jnp.concatenate([m_top[:, :h], jnp.maximum(m_top[:, h:], jnp.max(s_bot, axis=0, keepdims=True))], axis=1)
                p_top = jnp.exp2(s_top - m_new).astype(BF16)
                p_bot = jnp.exp2(s_bot - m_new[:, h:]).astype(BF16)
                late = jnp.concatenate([jnp.zeros((ACC_ROWS, h), F32),
                                        jnp.dot(with_ones(vt_bot[heads]), p_bot, preferred_element_type=F32)], axis=1)
                out.append((m_new, jnp.exp2(m - m_new) * acc
                            + jnp.dot(with_ones(vt_top[heads]), p_top, preferred_element_type=F32) + late))
            return tuple(out)

        init = tuple((jnp.full((1, blk), -jnp.inf, F32), jnp.zeros((ACC_ROWS, blk), F32)) for _ in range(2))
        carry = lax.fori_loop(0, i, lambda j, c: step(j, c, False), init)
        (ma, acca), (mb, accb) = diagonal_in_halves(carry) if sub == 2 else step(i, carry, True)
        la, lb = acca[HEAD_DIM:HEAD_DIM + 1], accb[HEAD_DIM:HEAD_DIM + 1]
        o_ref[...] = jnp.concatenate([acca[:HEAD_DIM] / la, accb[:HEAD_DIM] / lb], axis=0).T
        row = lax.broadcasted_iota(jnp.int32, (LANES, 1), 0)
        st_ref[0] = jnp.where(row == 0, ma + jnp.log2(la), jnp.where(row == 1, mb + jnp.log2(lb), 0.0)).T
        if side:
            @pl.when(step_id == steps - 1)
            def _():
                side.last(si, so, *sems)

    in_specs = [pl.BlockSpec((sub, w, tblk), lambda g, i: (i, qoff + g, 0)), pl.BlockSpec((t, w), lambda g, i: (0, koff + g)),
                pl.BlockSpec((t // tblk, LANES, tblk), lambda g, i: (0, voff + g, 0))]
    if has_bias:
        in_specs.append(pl.BlockSpec((t, LANES), lambda g, i: (0, 0)))
    s_in_specs, s_out_specs = side.specs() if side else ([], [])
    return pl.pallas_call(
        body, name=name, grid=(PAIRS, nb), in_specs=in_specs + s_in_specs,
        out_specs=[pl.BlockSpec((blk, LANES), lambda g, i: (i, g)), pl.BlockSpec((1, blk, LANES), lambda g, i: (g, i, 0))]
        + s_out_specs,
        out_shape=[jax.ShapeDtypeStruct((t, PAIRS * LANES), F32), jax.ShapeDtypeStruct((PAIRS, t, LANES), F32)] + list(s_outs),
        scratch_shapes=side.sems() if side else [],
        compiler_params=_params(("arbitrary", "arbitrary")),
    )(*ins, *s_ins)


def mix_norm(fo, mo, g_fo, g_mo, *, name, bt=512):
    t, d = fo.shape
    bt = min(bt, t)

    def body(fo_ref, mo_ref, gf_ref, gm_ref, o_ref):
        for n, (x_ref, g_ref) in enumerate(((fo_ref, gf_ref), (mo_ref, gm_ref))):
            xv = x_ref[...]
            r = lax.rsqrt(jnp.mean(xv * xv, axis=-1, keepdims=True) + EPS)
            o_ref[:, n * d:(n + 1) * d] = (xv * r * g_ref[...]).astype(BF16)

    row = pl.BlockSpec((bt, d), lambda i: (i, 0))
    vec = pl.BlockSpec((1, d), lambda i: (0, 0))
    return pl.pallas_call(
        body, name=name, grid=(t // bt,), in_specs=[row, row, vec, vec],
        out_specs=pl.BlockSpec((bt, 2 * d), lambda i: (i, 0)),
        out_shape=jax.ShapeDtypeStruct((t, 2 * d), BF16),
        compiler_params=_params(("parallel",)),
    )(fo, mo, g_fo, g_mo)


def mix_norm_bwd(dx1b, w_o, fo, mo, g_fo, g_mo, st_f, st_m, *, name, bt=512, side=None):
    t, d = fo.shape
    bt = min(bt, t)

    def body(dx_in_ref, w_ref, fo_ref, mo_ref, gf_ref, gm_ref, sf_ref, sm_ref, dfo_ref, dmo_ref, dgf_ref, dgm_ref,
             sfo_ref, smo_ref, dfot_ref, dmot_ref):
        @pl.when(pl.program_id(0) == 0)
        def _():
            dgf_ref[...] = jnp.zeros_like(dgf_ref)
            dgm_ref[...] = jnp.zeros_like(dgm_ref)

        dmixed = lax.dot_general(dx_in_ref[...], w_ref[...], NT, preferred_element_type=F32)
        groups = ((fo_ref, gf_ref, dfo_ref, dgf_ref, sf_ref, sfo_ref, dfot_ref),
                  (mo_ref, gm_ref, dmo_ref, dgm_ref, sm_ref, smo_ref, dmot_ref))
        for n, (x_ref, g_ref, dx_ref, dg_ref, st_ref, sto_ref, dxt_ref) in enumerate(groups):
            xv = x_ref[...]
            dhv = dmixed[:, n * d:(n + 1) * d]
            r = lax.rsqrt(jnp.mean(xv * xv, axis=-1, keepdims=True) + EPS)
            u = dhv * g_ref[...]
            dxf = r * u - xv * (r * r * r * jnp.mean(u * xv, axis=-1, keepdims=True))
            dxb = dxf.astype(BF16)
            dx_ref[...] = dxb
            dxt_ref[0] = dxf.T.astype(BF16)
            dg_ref[...] += jnp.sum(dhv * (xv * r), axis=0, keepdims=True)
            prod = xv * dxb.astype(F32)
            for g in range(PAIRS):
                grp = prod[:, g * LANES:(g + 1) * LANES]
                da = jnp.sum(jnp.where(_lane() < HEAD_DIM, grp, 0.0), axis=1, keepdims=True)
                db = jnp.sum(jnp.where(_lane() >= HEAD_DIM, grp, 0.0), axis=1, keepdims=True)
                sto_ref[g] = jnp.where(_lane() == 2, da, jnp.where(_lane() == 3, db, st_ref[g]))

    row = pl.BlockSpec((bt, d), lambda i: (i, 0))
    vec = pl.BlockSpec((1, d), lambda i: (0, 0))
    stat = pl.BlockSpec((PAIRS, bt, LANES), lambda i: (0, i, 0))
    return gridded(
        body, name=name, grid=(t // bt,),
        in_specs=[pl.BlockSpec((bt, dx1b.shape[1]), lambda i: (i, 0)), pl.BlockSpec(w_o.shape, lambda i: (0, 0)),
                  row, row, vec, vec, stat, stat],
        out_specs=[row, row, vec, vec, stat, stat] + [pl.BlockSpec((1, d, bt), lambda i: (i, 0, 0))] * 2,
        out_shape=[jax.ShapeDtypeStruct((t, d), BF16)] * 2 + [jax.ShapeDtypeStruct((1, d), F32)] * 2
        + [jax.ShapeDtypeStruct(st_f.shape, F32)] * 2 + [jax.ShapeDtypeStruct((t // bt, d, bt), BF16)] * 2,
        ins=[dx1b, w_o, fo, mo, g_fo, g_mo, st_f, st_m], semantics=("arbitrary",), side=side)


def flash_bwd(q_arr, qt_arr, k_arr, v_arr, do_arr, dot_arr, st, f_blocks, *, qoff, koff, voff, pair, scale, name, qblk=1024,
              side=None):
    t = q_arr.shape[0]
    blk = qt_arr.shape[2]
    qblk = max(min(qblk, t), blk)
    sub = qblk // blk
    nb, nbq = t // blk, t // qblk
    w = LANES if pair else 2 * LANES
    hw = w // 2
    has_bias = f_blocks is not None
    split = _halves if pair else _groups
    ins = [q_arr, qt_arr, k_arr, v_arr, do_arr, dot_arr, st] + ([f_blocks] if has_bias else [])
    n_out = 4 if has_bias else 3
    s_ins, s_outs = (side.ins, side.out_shapes) if side else ([], [])

    def wide(ref, first, count):
        parts = [ref[first + u] for u in range(count)]
        return parts[0] if count == 1 else jnp.concatenate(parts, axis=1)

    def body(*refs):
        main, si, outs, so, sems = _split_refs(refs, [len(ins), len(s_ins), n_out, len(s_outs), 2 if side else 0])
        q_ref, qt_ref, k_ref, v_ref, do_ref, dot_ref, st_ref = main[:7]
        dq_ref, dk_ref, dv_ref = outs[:3]
        g, j = pl.program_id(0), pl.program_id(1)
        step_id = g * nb + j
        if side:
            @pl.when(step_id == 0)
            def _():
                side.first(si, so, *sems)

        @pl.when(j == 0)
        def _():
            dq_ref[...] = jnp.zeros_like(dq_ref)

        kk, vv = k_ref[...], v_ref[...]
        ks = [kk, kk] if pair else _groups(kk)
        if has_bias:
            f_ref, df_ref = main[7], outs[3]
            fk = [LOG2E * _pick_row(f_ref[0], 2 * g + n) for n in range(2)]

            @pl.when(step_id == 0)
            def _():
                df_ref[...] = jnp.zeros_like(df_ref)

        def step(tb, count, carry, diagonal):
            rows = pl.ds(pl.multiple_of(tb * blk, blk), count * blk)
            qs = split((q_ref[rows, :].astype(F32) * (scale * LOG2E)).astype(BF16))
            qt = (wide(qt_ref, tb, count).astype(F32) * (scale * LOG2E)).astype(BF16)
            dos = [h.astype(BF16) for h in _halves(do_ref[rows, :].astype(F32))]
            dot = wide(dot_ref, tb, count)
            stats = st_ref[0, rows, :]
            new, dqs, row_sums = [], [], []
            for n in range(2):
                dkt, dvt, dfk = carry[n]
                s = lax.dot_general(qs[n], ks[n], NT, preferred_element_type=F32)
                if has_bias:
                    s = s - fk[n]
                if diagonal:
                    s = _below_diagonal(s)
                p = jnp.exp2(s - stats[:, n:n + 1])
                dp = lax.dot_general(dos[n], vv, NT, preferred_element_type=F32)
                ds = p * (dp - stats[:, 2 + n:3 + n])
                dsb = ds.astype(BF16)
                dvt = dvt + jnp.dot(dot[n * HEAD_DIM:(n + 1) * HEAD_DIM], p.astype(BF16), preferred_element_type=F32)
                dkt = dkt + jnp.dot(qt[n * hw:(n + 1) * hw], dsb, preferred_element_type=F32)
                dqs.append(jnp.dot(dsb, ks[n], preferred_element_type=F32))
                if has_bias:
                    dfk = dfk - jnp.sum(ds, axis=0, keepdims=True)
                    row_sums.append(jnp.sum(ds, axis=1, keepdims=True))
                new.append((dkt, dvt, dfk))
            dq = (jnp.where(_lane() < HEAD_DIM, dqs[0], dqs[1]) if pair else jnp.concatenate(dqs, axis=1)) * scale
            if has_bias:
                df_ref[rows, :] += jnp.where(_lane() == 2 * g, row_sums[0], jnp.where(_lane() == 2 * g + 1, row_sums[1], 0.0))
            dq_ref[rows, :] += dq
            return tuple(new)

        init = tuple((jnp.zeros((hw, blk), F32), jnp.zeros((HEAD_DIM, blk), F32), jnp.zeros((1, blk), F32)) for _ in range(2))
        carry = step(j, 1, init, True)
        whole = j // sub + 1
        carry = lax.fori_loop(j + 1, whole * sub, lambda tb, c: step(tb, 1, c, False), carry)
        (dka, dva, dfa), (dkb, dvb, dfb) = lax.fori_loop(whole, nbq, lambda i, c: step(i * sub, sub, c, False), carry)
        dk_ref[...] = jnp.concatenate([dka, dkb], axis=0).T * LN2
        dv_ref[...] = jnp.concatenate([dva, dvb], axis=0).T
        if has_bias:
            row = lax.broadcasted_iota(jnp.int32, (LANES, 1), 0)
            by_head = jnp.where(row == 2 * g, dfa, jnp.where(row == 2 * g + 1, dfb, 0.0))
            df_ref[pl.ds(pl.multiple_of(j * blk, blk), blk), :] += by_head.T
        if side:
            @pl.when(step_id == PAIRS * nb - 1)
            def _():
                side.last(si, so, *sems)

    in_specs = [pl.BlockSpec((t, w), lambda g, j: (0, qoff + g)), pl.BlockSpec((nb, w, blk), lambda g, j: (0, qoff + g, 0)),
                pl.BlockSpec((blk, w), lambda g, j: (j, koff + g)), pl.BlockSpec((blk, LANES), lambda g, j: (j, voff + g)),
                pl.BlockSpec((t, LANES), lambda g, j: (0, g)), pl.BlockSpec((nb, LANES, blk), lambda g, j: (0, g, 0)),
                pl.BlockSpec((1, t, LANES), lambda g, j: (g, 0, 0))]
    out_specs = [pl.BlockSpec((t, w), lambda g, j: (0, g)), pl.BlockSpec((blk, w), lambda g, j: (j, g)),
                 pl.BlockSpec((blk, LANES), lambda g, j: (j, g))]
    out_shape = [jax.ShapeDtypeStruct((t, PAIRS * w), F32)] * 2 + [jax.ShapeDtypeStruct((t, PAIRS * LANES), F32)]
    if has_bias:
        in_specs.append(pl.BlockSpec((1, N_HEADS, blk), lambda g, j: (j, 0, 0)))
        out_specs.append(pl.BlockSpec((t, LANES), lambda g, j: (0, 0)))
        out_shape.append(jax.ShapeDtypeStruct((t, LANES), F32))
    s_in_specs, s_out_specs = side.specs() if side else ([], [])
    return pl.pallas_call(
        body, name=name, grid=(PAIRS, nb), in_specs=in_specs + s_in_specs, out_specs=out_specs + s_out_specs,
        out_shape=out_shape + list(s_outs), scratch_shapes=side.sems() if side else [],
        compiler_params=_params(("arbitrary", "arbitrary")),
    )(*ins, *s_ins)


def _adamw_math(w, g, m, v):
    nm = ADAM_B1 * m + (1.0 - ADAM_B1) * g
    nv = ADAM_B2 * v + (1.0 - ADAM_B2) * (g * g)
    m_hat = nm / (1.0 - ADAM_B1 ** ADAM_STEP)
    v_hat = nv / (1.0 - ADAM_B2 ** ADAM_STEP)
    return -ADAM_LR * (m_hat / (jnp.sqrt(v_hat) + ADAM_EPS) + ADAM_WD * w), nm, nv


def adamw(w, g, m, v, *, name):
    rws, cols = w.shape
    br = _row_block(rws)

    def body(w_ref, g_ref, m_ref, v_ref, d_ref, nm_ref, nv_ref):
        d_ref[...], nm_ref[...], nv_ref[...] = _adamw_math(w_ref[...], g_ref[...], m_ref[...], v_ref[...])

    blk = pl.BlockSpec((br, cols), lambda i: (i, 0))
    return pl.pallas_call(
        body, name=name, grid=(rws // br,), in_specs=[blk] * 4, out_specs=[blk] * 3,
        out_shape=[jax.ShapeDtypeStruct((rws, cols), F32)] * 3,
        compiler_params=_params(("parallel",)),
    )(w, g, m, v)


def adamw_halves(w, g_mine, g_other, m, v, core, *, name):
    _, k, n = w.shape
    br = _row_block(k // 2)
    nh = k // 2 // br

    def body(c_ref, w_ref, gm_ref, go_ref, m_ref, v_ref, g_out, d_ref, nm_ref, nv_ref):
        gv = jnp.where(pl.program_id(0) == c_ref[0], gm_ref[...], go_ref[...])
        g_out[0] = gv
        d_ref[0], nm_ref[0], nv_ref[0] = _adamw_math(w_ref[0], gv, m_ref[0], v_ref[0])

    full = pl.BlockSpec((1, br, n), lambda hb, i, c: (0, hb * nh + i, 0))
    half = pl.BlockSpec((br, n), lambda hb, i, c: (i, 0))
    return pl.pallas_call(
        body, name=name,
        grid_spec=pltpu.PrefetchScalarGridSpec(num_scalar_prefetch=1, grid=(2, nh), in_specs=[full, half, half, full, full],
                                               out_specs=[full] * 4),
        out_shape=[jax.ShapeDtypeStruct(w.shape, F32)] * 4,
        compiler_params=_params(("parallel", "parallel")),
    )(core, w, g_mine, g_other, m, v)


def adamw_halves_t(wt, gt_mine, gt_other, mt, vt, core, *, name, bc=128):
    n, k = wt.shape
    nh = k // 2 // bc

    def body(c_ref, w_ref, gm_ref, go_ref, m_ref, v_ref, g_out, d_ref, nm_ref, nv_ref):
        gv = jnp.where(pl.program_id(0) == c_ref[0], gm_ref[...], go_ref[...])
        g_out[...] = gv
        d_ref[...], nm_ref[...], nv_ref[...] = _adamw_math(w_ref[...], gv, m_ref[...], v_ref[...])

    full = pl.BlockSpec((n, bc), lambda hb, i, c: (0, hb * nh + i))
    half = pl.BlockSpec((n, bc), lambda hb, i, c: (0, i))
    return pl.pallas_call(
        body, name=name,
        grid_spec=pltpu.PrefetchScalarGridSpec(num_scalar_prefetch=1, grid=(2, nh), in_specs=[full, half, half, full, full],
                                               out_specs=[full] * 4),
        out_shape=[jax.ShapeDtypeStruct(wt.shape, F32)] * 4,
        compiler_params=_params(("parallel", "parallel")),
    )(core, wt, gt_mine, gt_other, mt, vt)


def add_pair(dw, recv, core, *, name):
    n4, k, n = dw.shape
    half = (1, k // 2, n) if split_axis(k) == 0 else (1, k, n // 2)
    mine = (lambda q, c: (q, c[0], 0)) if split_axis(k) == 0 else (lambda q, c: (q, 0, c[0]))

    def body(c_ref, a_ref, b_ref, o_ref):
        o_ref[...] = (a_ref[...] + b_ref[...].astype(F32)).astype(BF16)

    return pl.pallas_call(
        body, name=name,
        grid_spec=pltpu.PrefetchScalarGridSpec(
            num_scalar_prefetch=1, grid=(n4,),
            in_specs=[pl.BlockSpec(half, mine), pl.BlockSpec(half, lambda q, c: (q, 0, 0))],
            out_specs=pl.BlockSpec(half, lambda q, c: (q, 0, 0))),
        out_shape=jax.ShapeDtypeStruct((n4,) + half[1:], BF16),
        compiler_params=_params(("parallel",)),
    )(core, dw, recv)


def sum_chips(parts, *, name):
    n4, r, n = parts.shape
    if r % 16 == 0:
        br, bc = _row_block(r), n
    else:
        br, bc = r, LANES

    def body(p_ref, o_ref):
        acc = p_ref[0].astype(F32)
        for q in range(1, n4):
            acc = acc + p_ref[q].astype(F32)
        o_ref[...] = acc

    return pl.pallas_call(
        body, name=name, grid=(r // br, n // bc),
        in_specs=[pl.BlockSpec((n4, br, bc), lambda i, j: (0, i, j))], out_specs=pl.BlockSpec((br, bc), lambda i, j: (i, j)),
        out_shape=jax.ShapeDtypeStruct((r, n), F32),
        compiler_params=_params(("parallel", "parallel")),
    )(parts)


ANY = pl.BlockSpec(memory_space=pl.ANY)


def _place():
    x, y, c = lax.axis_index("x"), lax.axis_index("y"), lax.axis_index("c")
    chips = [(1 - x, y), (x, 1 - y), (1 - x, 1 - y)]
    return x, y, c, chips


def _copy(src, dst, send_sems, recv_sems, k, to):
    return pltpu.make_async_remote_copy(src_ref=src, dst_ref=dst, send_sem=send_sems.at[k], recv_sem=recv_sems.at[k],
                                        device_id=to, device_id_type=MESH)


def split_axis(rows):
    return 0 if rows % 32 == 0 else 1


def _half(ref, lead, hf):
    rows, cols = ref.shape[-2:]
    if split_axis(rows) == 0:
        at = (pl.ds(hf * (rows // 2), rows // 2), slice(None))
    else:
        at = (slice(None), pl.ds(hf * (cols // 2), cols // 2))
    return ref.at[at] if lead is None else ref.at[(lead,) + at]


def _gather_first(srcs, dsts, ssems, rsems):
    x, y, c, chips = _place()
    for ti, (s, d) in enumerate(zip(srcs, dsts)):
        for j, (cx, cy) in enumerate(chips):
            _copy(_half(s, None, c), _half(d, 2 * x + y, c), ssems, rsems, 3 * ti + j, (cx, cy, c)).start()


def _gather_mid(srcs, dsts, ssems, rsems):
    x, y, c, chips = _place()
    n1 = 3 * len(srcs)
    for ti, d in enumerate(dsts):
        for j, (cx, cy) in enumerate(chips):
            landed = _half(d, 2 * cx + cy, c)
            _copy(landed, landed, ssems, rsems, 3 * ti + j, (cx, cy, c)).wait_recv()
            _copy(landed, landed, ssems, rsems, n1 + 3 * ti + j, (x, y, 1 - c)).start()


def _gather_last(srcs, dsts, ssems, rsems):
    x, y, c, chips = _place()
    n1 = 3 * len(srcs)
    for ti, (s, d) in enumerate(zip(srcs, dsts)):
        for j, (cx, cy) in enumerate(chips):
            other = _half(d, 2 * cx + cy, 1 - c)
            _copy(other, other, ssems, rsems, n1 + 3 * ti + j, (x, y, 1 - c)).wait_recv()
        for j, (cx, cy) in enumerate(chips):
            mine = _half(s, None, c)
            _copy(mine, mine, ssems, rsems, 3 * ti + j, (cx, cy, c)).wait_send()
            _copy(mine, mine, ssems, rsems, n1 + 3 * ti + j, (x, y, 1 - c)).wait_send()


def gather_side(shards):
    return Side(shards, [jax.ShapeDtypeStruct((N_CHIPS,) + s.shape, s.dtype) for s in shards], 6 * len(shards),
                _gather_first, _gather_last, _gather_mid)


def _scatter_first(srcs, dsts, ssems, rsems):
    x, y, c, chips = _place()
    for ti, (s, d) in enumerate(zip(srcs, dsts)):
        for j, (cx, cy) in enumerate(chips):
            _copy(s.at[2 * cx + cy], d.at[2 * x + y], ssems, rsems, 3 * ti + j, (cx, cy, c)).start()


def _scatter_last(srcs, dsts, ssems, rsems):
    x, y, c, chips = _place()
    for ti, (s, d) in enumerate(zip(srcs, dsts)):
        for j, (cx, cy) in enumerate(chips):
            _copy(s.at[2 * cx + cy], d.at[2 * cx + cy], ssems, rsems, 3 * ti + j, (cx, cy, c)).wait_recv()
        for j, (cx, cy) in enumerate(chips):
            _copy(s.at[2 * cx + cy], d.at[2 * cx + cy], ssems, rsems, 3 * ti + j, (cx, cy, c)).wait_send()


def scatter_side(parts):
    return Side(parts, [jax.ShapeDtypeStruct(p.shape, p.dtype) for p in parts], 3 * len(parts), _scatter_first, _scatter_last)


def run_side(side, *, name):
    n_in, n_out = len(side.ins), len(side.out_shapes)

    def body(*refs):
        si, so, sems = _split_refs(refs, [n_in, n_out, 2])
        side.first(si, so, *sems)
        if side.mid is not None:
            side.mid(si, so, *sems)
        side.last(si, so, *sems)

    in_specs, out_specs = side.specs()
    return pl.pallas_call(body, name=name, in_specs=in_specs, out_specs=out_specs, out_shape=side.out_shapes,
                          scratch_shapes=side.sems())(*side.ins)


def _swap_first(srcs, dsts, ssems, rsems):
    x, y, c, _ = _place()
    for k, (s, d) in enumerate(zip(srcs, dsts)):
        _copy(s, d, ssems, rsems, k, (x, y, 1 - c)).start()


def _swap_last(srcs, dsts, ssems, rsems):
    x, y, c, _ = _place()
    for k, (s, d) in enumerate(zip(srcs, dsts)):
        _copy(s, d, ssems, rsems, k, (x, y, 1 - c)).wait()


def swap_side(xs):
    return Side(xs, [jax.ShapeDtypeStruct(a.shape, a.dtype) for a in xs], len(xs), _swap_first, _swap_last)


def _swap_halves(srcs, dsts, ssems, rsems):
    x, y, c, _ = _place()
    for k, (s, d) in enumerate(zip(srcs, dsts)):
        hk = s.shape[1] // 2
        yield _copy(s.at[:, pl.ds((1 - c) * hk, hk), :], d, ssems, rsems, k, (x, y, 1 - c))


def _swap_halves_first(srcs, dsts, ssems, rsems):
    for cp in _swap_halves(srcs, dsts, ssems, rsems):
        cp.start()


def _swap_halves_last(srcs, dsts, ssems, rsems):
    for cp in _swap_halves(srcs, dsts, ssems, rsems):
        cp.wait()


def swap_halves_side(xs):
    return Side(xs, [jax.ShapeDtypeStruct((a.shape[0], a.shape[1] // 2, a.shape[2]), a.dtype) for a in xs], len(xs),
                _swap_halves_first, _swap_halves_last)


def allreduce_small(s):
    n_dev = 8

    def body(s_ref, out_ref, buf, send_sems, recv_sems):
        x, y, c, _ = _place()
        me = 4 * x + 2 * y + c
        buf[me] = s_ref[...]
        sends = []
        for k in range(1, n_dev):
            px = 1 - x if k & 4 else x
            py = 1 - y if k & 2 else y
            pc = 1 - c if k & 1 else c
            cp = _copy(s_ref, buf.at[me], send_sems, recv_sems, k - 1, (px, py, pc))
            cp.start()
            sends.append((cp, 4 * px + 2 * py + pc))
        for k, (cp, peer) in enumerate(sends):
            _copy(s_ref, buf.at[peer], send_sems, recv_sems, k, (x, y, c)).wait_recv()
        for cp, _ in sends:
            cp.wait_send()
        acc = buf[0]
        for d in range(1, n_dev):
            acc = acc + buf[d]
        out_ref[...] = acc

    vm = pl.BlockSpec(memory_space=pltpu.VMEM)
    return pl.pallas_call(
        body, name="allreduce_small", in_specs=[vm], out_specs=vm,
        out_shape=jax.ShapeDtypeStruct(s.shape, F32),
        scratch_shapes=[pltpu.VMEM((n_dev,) + s.shape, F32), pltpu.SemaphoreType.DMA((n_dev - 1,)),
                        pltpu.SemaphoreType.DMA((n_dev - 1,))],
    )(s)


def join_cols(sm):
    n4, k, n = sm.shape
    return sm.transpose(1, 0, 2).reshape(k, n4 * n)


def split_cols(full):
    k, n = full.shape
    return full.reshape(k, N_CHIPS, n // N_CHIPS).transpose(1, 0, 2)


def _pad_heads(w, width):
    lead, heads = w.shape[:-1], w.shape[-1] // width
    w = w.reshape(lead + (heads, width))
    return jnp.pad(w, [(0, 0)] * len(lead) + [(0, 0), (0, LANES - width)]).reshape(lead + (heads * LANES,))


def _unpad_heads(w, width):
    lead, heads = w.shape[:-1], w.shape[-1] // LANES
    return w.reshape(lead + (heads, LANES))[..., :width].reshape(lead + (heads * width,))


O_F = 3 * FW
O_CQ = O_F + N_HEADS
O_CKV = O_CQ + Q_RANK
O_KR = O_CKV + KV_RANK
O_END = O_KR + ROPE_DIM


def _shard_rows(sm, a, b):
    r = sm.shape[1]
    out = []
    while a < b:
        q = a // r
        e = min(b, (q + 1) * r)
        out.append(sm[q, a - q * r:e - q * r])
        a = e
    return out


def split_w_in_t(w_sm):
    d = w_sm.shape[2]

    def z(n):
        return [jnp.zeros((n, d), w_sm.dtype)]

    small = (_shard_rows(w_sm, O_CQ, O_CKV) + _shard_rows(w_sm, O_CKV, O_KR) + z(HEAD_DIM) + _shard_rows(w_sm, O_KR, O_END)
             + z(LANES - HEAD_DIM - ROPE_DIM) + _shard_rows(w_sm, O_F, O_CQ) + z(LANES - N_HEADS) + z(LANES))
    return jnp.concatenate(_shard_rows(w_sm, 0, O_F), axis=0), jnp.concatenate(small, axis=0)


def join_w_in_t(d_qkv_t, d_small_t):
    kr = S_KR + HEAD_DIM
    segments = [(d_qkv_t, 0, 0, O_F), (d_small_t, S_F, O_F, N_HEADS), (d_small_t, S_CQ, O_CQ, Q_RANK),
                (d_small_t, S_CKV, O_CKV, KV_RANK), (d_small_t, kr, O_KR, ROPE_DIM)]
    r = O_END // N_CHIPS
    shards = []
    for q in range(N_CHIPS):
        pieces = []
        for src, s0, v0, n in segments:
            a, b = max(v0, q * r), min(v0 + n, (q + 1) * r)
            if a < b:
                pieces.append(src[s0 + a - v0:s0 + b - v0])
        shards.append(jnp.concatenate(pieces, axis=0))
    return jnp.stack(shards)


def rope_tables(pos):
    t = pos.shape[0]
    inv_freq = ROPE_THETA ** (-jnp.arange(0, ROPE_DIM, 2, dtype=F32) / ROPE_DIM)
    ang = pos.astype(F32)[:, None] * inv_freq
    cos, sin = jnp.cos(ang), jnp.sin(ang)
    half = ROPE_DIM // 2

    def z(n):
        return jnp.zeros((t, n), F32)

    tab_c = jnp.concatenate([jnp.ones((t, HEAD_DIM), F32), cos, cos, z(LANES - HEAD_DIM - ROPE_DIM)], axis=1)
    tab_a = jnp.concatenate([z(HEAD_DIM), -sin, z(half), z(LANES - HEAD_DIM - ROPE_DIM)], axis=1)
    tab_b = jnp.concatenate([z(HEAD_DIM), z(half), sin, z(LANES - HEAD_DIM - ROPE_DIM)], axis=1)
    return tab_c, tab_a, tab_b


def _pad_lanes(v, n):
    return jnp.pad(v, ((0, 0), (0, n - v.shape[1])))


ATTN_BLK = 512
ATTN_FWD_BLK = 1024
ATTN_BWD_QBLK = 1024
ACC_ROWS = HEAD_DIM + 16


def local_step(xs, pos, tgt, gains, early_weights, late_weights, early_side=None, fwd_sides=(None, None), reduction=None):
    g_attn, b_forget, g_q, g_kv, g_fo, g_mo, g_mlp, g_fin = gains
    t = xs.shape[0]
    blk = min(ATTN_BLK, t)
    fox_scale = 1.0 / (HEAD_DIM ** 0.5)
    mla_scale = 1.0 / ((HEAD_DIM + ROPE_DIM) ** 0.5)

    h1, *gathered = rmsnorm(xs, g_attn, out_dtype=BF16, name="norm_attn", side=early_side)
    w_in_t, w_uq_p, w_ukv = early_weights(gathered)
    w_qkv_t, w_small_t = split_w_in_t(w_in_t)
    kv = w_ukv.reshape(KV_RANK, N_HEADS, 2 * HEAD_DIM)
    w_ukv_p = jnp.concatenate([_pad_heads(kv[:, :, :HEAD_DIM].reshape(KV_RANK, FW), HEAD_DIM),
                               kv[:, :, HEAD_DIM:].reshape(KV_RANK, FW)], axis=1)
    b_f = _pad_lanes(b_forget, LANES)
    tab_c, tab_a, tab_b = rope_tables(pos)

    qkv, qkv_t = mm(h1, w_qkv_t, trans_b=True, out_dtypes=[BF16], t_blk=blk, name="proj_qkv", bn=1536)
    small, = mm(h1, w_small_t, trans_b=True, out_dtypes=[F32], name="proj_small")
    mq, mk, mv, lf, cqn, ckvn, mq_t, mv_t = mla_prep(small, g_q, g_kv, w_uq_p, w_ukv_p, tab_c, tab_a, tab_b, b_f,
                                                     name="mla_prep", bt=blk)
    f_cum = cumsum_rows(lf, reverse=False, name="gate_cumsum")
    f_blocks = f_cum[:, :N_HEADS].reshape(t // blk, blk, N_HEADS).transpose(0, 2, 1)
    fo, st_f, *gathered = flash_fwd(qkv_t, qkv, qkv_t, f_cum, qoff=0, koff=PAIRS, voff=2 * PAIRS, pair=True,
                                    scale=fox_scale, name="fox_fwd", blk=ATTN_FWD_BLK, side=fwd_sides[0])
    mo, st_m, *more = flash_fwd(mq_t, mk, mv_t, None, qoff=0, koff=0, voff=0, pair=False, scale=mla_scale, name="mla_fwd",
                                blk=ATTN_FWD_BLK, side=fwd_sides[1])
    w_o, w_up, w_down = late_weights(gathered + more)
    mixed = mix_norm(fo, mo, g_fo, g_mo, name="norm_mix")

    def inv_rms(v):
        return lax.rsqrt(jnp.mean(v * v, axis=-1, keepdims=True) + EPS)

    def residual_then_norm(acc, res, g):
        xn = acc + res
        return xn, xn * inv_rms(xn) * g

    def norm_bwd(dh, xn, res, g):
        r = inv_rms(xn)
        uu = dh * g
        return (r * uu - xn * (r * r * r * jnp.mean(uu * xn, axis=-1, keepdims=True)) + res,
                jnp.sum(dh * (xn * r), axis=0, keepdims=True))

    def norm_bwd2(dh, xn, res, g):
        dx, dg = norm_bwd(dh, xn, res, g)
        return dx, dx, dg

    def residual_then_loss(acc, res, target, g):
        xn = acc + res
        r = inv_rms(xn)
        xh = xn * r
        e = xh * g - target
        part = 0.5 * jnp.sum(jnp.mean(e * e, axis=-1, keepdims=True), axis=0, keepdims=True)
        dy = e * (1.0 / xn.shape[1])
        uu = dy * g
        dx = r * uu - xn * (r * r * r * jnp.mean(uu * xn, axis=-1, keepdims=True))
        return dx, dx, jnp.sum(dy * xh, axis=0, keepdims=True), part + jnp.zeros_like(g)

    x1, h2 = mm(mixed, w_o, extras=[xs], vecs=[g_mlp], epilogue=residual_then_norm, out_dtypes=[F32, BF16], name="out_proj")

    def relu2(uu):
        r = jnp.maximum(uu.astype(F32), 0.0)
        return (r * r).astype(BF16)

    u, = mm(h2, w_up, out_dtypes=[BF16], name="mlp_up", bn=2048)
    dx2, dx2b, dg_fin, loss_row = mm(u, w_down, a_pro=relu2, extras=[x1, tgt], vecs=[g_fin], epilogue=residual_then_loss,
                                     out_dtypes=[F32, BF16], n_sums=2, name="mlp_down_loss")
    loss = loss_row[:, :1]

    def relu2_grad(acc, uu):
        return (acc * (2.0 * jnp.maximum(uu.astype(F32), 0.0)),)

    du, = mm(dx2b, w_down, trans_b=True, extras=[u], epilogue=relu2_grad, out_dtypes=[BF16], name="mlp_down_bwd", bn=2048)
    dw_down, dw_down_b = (g.reshape(N_CHIPS, -1, w_down.shape[1])
                          for g in mm_tn(u, dx2b, a_pro=relu2, name="dw_down", bf16_copy=True))
    dx1, dx1b, dg_mlp = mm(du, w_up, trans_b=True, extras=[x1, dx2], vecs=[g_mlp], epilogue=norm_bwd2,
                           out_dtypes=[F32, BF16], n_sums=1, name="mlp_up_bwd")
    dw_up, dw_up_b = mm_tn(h2, du, name="dw_up", col_shards=N_CHIPS, bf16_copy=True)

    dw_o, dw_o_b = (g.reshape(N_CHIPS, -1, w_o.shape[1]) for g in mm_tn(mixed, dx1b, name="dw_o", bf16_copy=True))
    late, late_b = (dw_o, dw_up, dw_down), (dw_o_b, dw_up_b, dw_down_b)
    red = reduction
    dfo, dmo, dg_fo, dg_mo, st_f, st_m, dfo_t, dmo_t, *got = mix_norm_bwd(
        dx1b, w_o, fo, mo, g_fo, g_mo, st_f, st_m, name="out_proj_mix_bwd", bt=blk,
        side=red.late_swap(late, late_b) if red else None)
    dfq, dfk, dfv, d_f, *got = flash_bwd(
        qkv, qkv_t, qkv, qkv, dfo, dfo_t, st_f, f_blocks, qoff=0, koff=PAIRS, voff=2 * PAIRS, pair=True,
        scale=fox_scale, name="fox_bwd", qblk=ATTN_BWD_QBLK, side=red.late_scatter(got) if red else None)
    dmq, dmk, dmv, *got = flash_bwd(mq, mq_t, mk, mv, dmo, dmo_t, st_m, None, qoff=0, koff=0, voff=0,
                                    pair=False, scale=mla_scale, name="mla_bwd", qblk=ATTN_BWD_QBLK,
                                    side=red.late_halves(got) if red else None)
    if red:
        red.late_done(got)
    dqkv = jnp.concatenate([dfq, dfk, dfv], axis=1).astype(BF16)
    dlf = cumsum_rows(d_f, reverse=True, name="gate_cumsum_bwd")
    dsmall, dq_u, dkv_u, dg_q, dg_kv, db_f = mla_prep_bwd(dmq, dmk, dmv, dlf, small, g_q, g_kv, w_uq_p, w_ukv_p,
                                                          tab_c, tab_a, tab_b, b_f, name="mla_prep_bwd")
    dw_uq_p = mm_tn(cqn, dq_u, name="dw_uq")
    dw_ukv_p = mm_tn(ckvn, dkv_u, name="dw_ukv")

    dw_in_t = join_w_in_t(mm_tn(dqkv, h1, name="dw_qkv"), mm_tn(dsmall, h1, name="dw_small"))
    dk_cols = _unpad_heads(dw_ukv_p[:, :HW], HEAD_DIM).reshape(KV_RANK, N_HEADS, HEAD_DIM)
    dv_cols = dw_ukv_p[:, HW:].reshape(KV_RANK, N_HEADS, HEAD_DIM)
    dw_ukv = jnp.concatenate([dk_cols, dv_cols], axis=2).reshape(KV_RANK, N_HEADS * 2 * HEAD_DIM)
    early = (dw_in_t, split_cols(dw_uq_p), split_cols(dw_ukv))
    grad_x, dg_attn, *got = mm([dqkv, dsmall], [w_qkv_t, w_small_t], extras=[xs, dx1], vecs=[g_attn],
                               epilogue=norm_bwd, out_dtypes=[F32], n_sums=1, name="proj_bwd",
                               side=red.early_scatter(early) if red else None)
    if red:
        red.early_done(got)
    d_gains = (dg_attn, db_f[:, :N_HEADS], dg_q, dg_kv, dg_fo, dg_mo, dg_mlp, dg_fin)
    return loss, grad_x, early, late, d_gains


class GradReduction:
    def __init__(self, core_id, chip):
        self.core_id, self.chip = core_id, chip
        self.core = core_id.reshape(1).astype(jnp.int32)
        self.grads, self.pairs, self.halves, self.others = {}, {}, {}, {}

    def _other_halves(self, grads):
        out = []
        for g in grads:
            axis = 1 + split_axis(g.shape[1])
            size = g.shape[axis] // 2
            out.append(lax.dynamic_slice_in_dim(g, (1 - self.core_id) * size, size, axis=axis).astype(BF16))
        return out

    def _add_pairs(self, group, recvs):
        self.pairs[group] = [add_pair(g, r, self.core, name="add_pair_%s_%d" % (group, n))
                             for n, (g, r) in enumerate(zip(self.grads[group], recvs))]
        return scatter_side(self.pairs[group])

    def _chip_sums(self, group, scattered):
        chip = self.chip
        with_mine = [lax.dynamic_update_index_in_dim(s, lax.dynamic_index_in_dim(p, chip, 0, keepdims=True), chip, 0)
                     for s, p in zip(scattered, self.pairs[group])]
        self.halves[group] = [sum_chips(s, name="sum_chips_%s_%d" % (group, n)) for n, s in enumerate(with_mine)]
        return self.halves[group]

    def late_swap(self, grads, bf16_copies):
        self.grads["late"] = list(grads)
        return swap_halves_side(list(bf16_copies))

    def late_scatter(self, recvs):
        return self._add_pairs("late", recvs)

    def late_halves(self, scattered):
        return swap_side(self._chip_sums("late", scattered))

    def late_done(self, others):
        self.others["late"] = list(others)

    def early_scatter(self, grads):
        self.grads["early"] = list(grads)
        return self._add_pairs("early", run_side(swap_side(self._other_halves(grads)), name="swap_early_sends"))

    def early_done(self, scattered):
        self.others["early"] = list(run_side(swap_side(self._chip_sums("early", scattered)), name="swap_early_halves"))

def kernel(x, positions, attn_norm_g, w_in, b_forget, q_norm_g, w_uq, kv_norm_g, w_ukv, fox_out_g, mla_out_g, w_o, mlp_norm_g, w_up, w_down, final_norm_g, loss_target, m_attn_norm_g, m_w_in, m_b_forget, m_q_norm_g, m_w_uq, m_kv_norm_g, m_w_ukv, m_fox_out_g, m_mla_out_g, m_w_o, m_mlp_norm_g, m_w_up, m_w_down, m_final_norm_g, v_attn_norm_g, v_w_in, v_b_forget, v_q_norm_g, v_w_uq, v_kv_norm_g, v_w_ukv, v_fox_out_g, v_mla_out_g, v_w_o, v_mlp_norm_g, v_w_up, v_w_down, v_final_norm_g):
    core_id = lax.axis_index("c")
    core = core_id.reshape(1).astype(jnp.int32)
    chip = 2 * lax.axis_index("x") + lax.axis_index("y")
    big = [w_in, w_uq, w_ukv, w_o, w_up, w_down]
    big_m = [m_w_in, m_w_uq, m_w_ukv, m_w_o, m_w_up, m_w_down]
    big_v = [v_w_in, v_w_uq, v_w_ukv, v_w_o, v_w_up, v_w_down]
    n_early = 3

    def vec(a):
        return a.reshape(1, -1)

    small = [attn_norm_g, b_forget, q_norm_g, kv_norm_g, fox_out_g, mla_out_g, mlp_norm_g, final_norm_g]
    small_m = [m_attn_norm_g, m_b_forget, m_q_norm_g, m_kv_norm_g, m_fox_out_g, m_mla_out_g, m_mlp_norm_g, m_final_norm_g]
    small_v = [v_attn_norm_g, v_b_forget, v_q_norm_g, v_kv_norm_g, v_fox_out_g, v_mla_out_g, v_mlp_norm_g, v_final_norm_g]
    gains = [vec(a) for a in small]

    views = [big[0][0].T, _pad_heads(big[1][0], HEAD_DIM + ROPE_DIM)] + [w[0] for w in big[2:]]
    shards = [v.astype(BF16) for v in views]

    def with_own(gathered, mine):
        return [lax.dynamic_update_index_in_dim(g, s, chip, 0) for g, s in zip(gathered, mine)]

    def early_weights(gathered):
        g_in, g_uq, g_ukv = with_own(gathered, shards[:n_early])
        return g_in, join_cols(g_uq), join_cols(g_ukv)

    def late_weights(gathered):
        g_o, g_up, g_down = with_own(gathered, shards[n_early:])
        return g_o.reshape(-1, g_o.shape[2]), join_cols(g_up), g_down.reshape(-1, g_down.shape[2])

    reduction = GradReduction(core_id, chip)
    loss, grad_x, _, _, d_small = local_step(
        x[0], positions[0], loss_target[0], gains, early_weights, late_weights, gather_side(shards[:n_early]),
        (gather_side(shards[n_early:-1]), gather_side(shards[-1:])), reduction)
    halves = reduction.halves["early"] + reduction.halves["late"]
    others = reduction.others["early"] + reduction.others["late"]

    def rows8(vs):
        return jnp.concatenate([_pad_lanes(vec(a).astype(F32), 1024) for a in vs], axis=0)

    with_loss = [jnp.concatenate([d, loss], axis=1) if n == 1 else d for n, d in enumerate(d_small)]
    g_small8 = allreduce_small(rows8(with_loss))

    outs_big = []
    for n, (w, gm, go, m, v) in enumerate(zip(big, halves, others, big_m, big_v)):
        if n == 0:
            outs = adamw_halves_t(w[0].T, gm, go, m[0].T, v[0].T, core, name="adamw_%d" % n)
            outs_big.append([o.T[None] for o in outs])
        else:
            if n == 1:
                gm, go = (_unpad_heads(gh, HEAD_DIM + ROPE_DIM) for gh in (gm, go))
            outs_big.append(adamw_halves(w, gm, go, m, v, core, name="adamw_%d" % n))
    d8, m8, v8 = adamw(rows8(small), g_small8, rows8(small_m), rows8(small_v), name="adamw_small")

    def unrows8(a8):
        return [a8[n, :s.size].reshape(s.shape) for n, s in enumerate(small)]

    loss_all = g_small8[1, N_HEADS]
    grads, deltas, new_m, new_v = [None] * 14, [None] * 14, [None] * 14, [None] * 14
    big_at = [1, 4, 6, 9, 11, 12]
    small_at = [0, 2, 3, 5, 7, 8, 10, 13]
    for n, at in enumerate(big_at):
        grads[at], deltas[at], new_m[at], new_v[at] = outs_big[n]
    for at, g, dd, mm_, vv in zip(small_at, unrows8(g_small8), unrows8(d8), unrows8(m8), unrows8(v8)):
        grads[at], deltas[at], new_m[at], new_v[at] = g, dd, mm_, vv
    return (loss_all, grad_x[None], *grads, *deltas, *new_m, *new_v)
```

```python
import jax
import jax.numpy as jnp
from jax import lax
from jax.experimental import pallas as pl
from jax.experimental.pallas import tpu as pltpu

F32 = jnp.float32
BF16 = jnp.bfloat16
MESH = pl.DeviceIdType.MESH

EPS = 1e-6
ROPE_THETA = 10000.0
N_HEADS = 8
PAIRS = N_HEADS // 2
HEAD_DIM = 64
ROPE_DIM = 32
LANES = 128
Q_RANK = 384
KV_RANK = 256
N_CHIPS = 4
ADAM_LR, ADAM_B1, ADAM_B2, ADAM_EPS, ADAM_WD, ADAM_STEP = 0.001, 0.9, 0.999, 1e-08, 0.01, 10
VMEM_LIMIT = 48 * 1024 * 1024
LOG2E = 1.4426950408889634
LN2 = 0.6931471805599453
NN = (((1,), (0,)), ((), ()))
NT = (((1,), (1,)), ((), ()))
TN = (((0,), (0,)), ((), ()))


def _params(sem=None):
    return pltpu.CompilerParams(dimension_semantics=sem, vmem_limit_bytes=VMEM_LIMIT)


def _fit(block, dim):
    if dim <= block:
        return dim
    return next(b for b in range(block - block % LANES, 0, -LANES) if dim % b == 0)


def _row_block(rows):
    return next(b for b in (256, 128, 64, 32, 16, 8) if rows % b == 0)


def gridded(body, *, name, grid, in_specs, out_specs, out_shape, ins, semantics, side=None):
    if side is None:
        return pl.pallas_call(body, name=name, grid=grid, in_specs=in_specs, out_specs=out_specs, out_shape=out_shape,
                              compiler_params=_params(semantics))(*ins)
    n_in, n_out = len(in_specs), len(out_specs)
    steps = 1
    for extent in grid:
        steps *= extent

    def riding(*refs):
        main_in, s_in, main_out, s_out, sems = _split_refs(refs, [n_in, len(side.ins), n_out, len(side.out_shapes), 2])
        step = 0
        for axis, extent in enumerate(grid):
            step = step * extent + pl.program_id(axis)

        @pl.when(step == 0)
        def _():
            side.first(s_in, s_out, *sems)

        body(*main_in, *main_out)

        @pl.when(step == steps - 1)
        def _():
            if side.mid is not None:
                side.mid(s_in, s_out, *sems)
            side.last(s_in, s_out, *sems)

    s_in_specs, s_out_specs = side.specs()
    return pl.pallas_call(
        riding, name=name, grid=grid, in_specs=list(in_specs) + s_in_specs, out_specs=list(out_specs) + s_out_specs,
        out_shape=list(out_shape) + side.out_shapes, scratch_shapes=side.sems(),
        compiler_params=_params(("arbitrary",) * len(grid)))(*ins, *side.ins)


def rmsnorm(x, g, *, out_dtype, name, bt=512, side=None):
    t, d = x.shape
    bt = min(bt, t)

    def body(x_ref, g_ref, o_ref):
        xv = x_ref[...].astype(F32)
        r = lax.rsqrt(jnp.mean(xv * xv, axis=-1, keepdims=True) + EPS)
        o_ref[...] = (xv * r * g_ref[...]).astype(o_ref.dtype)

    return gridded(
        body, name=name, grid=(t // bt,),
        in_specs=[pl.BlockSpec((bt, d), lambda i: (i, 0)), pl.BlockSpec((1, d), lambda i: (0, 0))],
        out_specs=[pl.BlockSpec((bt, d), lambda i: (i, 0))],
        out_shape=[jax.ShapeDtypeStruct((t, d), out_dtype)],
        ins=[x, g], semantics=("parallel",), side=side)


def mm(a, b, *, trans_b=False, a_pro=None, extras=(), vecs=(), epilogue=None, out_dtypes, n_sums=0, t_blk=None, name,
       bm=1024, bn=1024, side=None):
    a_list = list(a) if isinstance(a, (list, tuple)) else [a]
    b_list = list(b) if isinstance(b, (list, tuple)) else [b]
    m = a_list[0].shape[0]
    n = b_list[0].shape[0] if trans_b else b_list[0].shape[1]
    ks = [x.shape[1] for x in a_list]
    if sum(ks) > 2048:
        bm = bm // 2
    bm, bn = _fit(bm, m), _fit(bn, n)
    assert n_sums == 0 or bn == n
    n_ab, n_ex, n_vec, n_out = len(a_list), len(extras), len(vecs), len(out_dtypes)
    n_t = 0 if t_blk is None else 1

    def body(*refs):
        a_refs, b_refs, ex, vs, outs, t_outs, sums = _split_refs(refs, [n_ab, n_ab, n_ex, n_vec, n_out, n_t, n_sums])
        acc = None
        for a_ref, b_ref in zip(a_refs, b_refs):
            a_tile = a_ref[...] if a_pro is None else a_pro(a_ref[...])
            part = lax.dot_general(a_tile, b_ref[...], NT if trans_b else NN, preferred_element_type=F32)
            acc = part if acc is None else acc + part
        res = epilogue(acc, *[e[...] for e in ex], *[v[...] for v in vs]) if epilogue is not None else (acc,)
        for o, r in zip(outs, res[:n_out]):
            o[...] = r.astype(o.dtype)
        for t_ref in t_outs:
            for u in range(bm // t_blk):
                t_ref[u] = res[0][u * t_blk:(u + 1) * t_blk, :].T.astype(t_ref.dtype)
        if n_sums:
            @pl.when(pl.program_id(0) == 0)
            def _():
                for s_ref in sums:
                    s_ref[...] = jnp.zeros_like(s_ref)

            for s_ref, r in zip(sums, res[n_out:]):
                s_ref[...] += r

    tile = pl.BlockSpec((bm, bn), lambda i, j: (i, j))
    vec = pl.BlockSpec((1, bn), lambda i, j: (0, j))
    t_specs, t_shapes = [], []
    if t_blk is not None:
        t_specs = [pl.BlockSpec((bm // t_blk, bn, t_blk), lambda i, j: (i, j, 0))]
        t_shapes = [jax.ShapeDtypeStruct((m // t_blk, n, t_blk), out_dtypes[0])]
    a_specs = [pl.BlockSpec((bm, k), lambda i, j: (i, 0)) for k in ks]
    b_specs = [pl.BlockSpec((bn, k), lambda i, j: (j, 0)) if trans_b else pl.BlockSpec((k, bn), lambda i, j: (0, j)) for k in ks]
    return gridded(
        body, name=name, grid=(m // bm, n // bn),
        in_specs=a_specs + b_specs + [tile] * n_ex + [vec] * n_vec,
        out_specs=[tile] * n_out + t_specs + [vec] * n_sums,
        out_shape=[jax.ShapeDtypeStruct((m, n), dt) for dt in out_dtypes] + t_shapes + [jax.ShapeDtypeStruct((1, n), F32)] * n_sums,
        ins=[*a_list, *b_list, *extras, *vecs],
        semantics=("arbitrary", "arbitrary") if n_sums else ("parallel", "parallel"), side=side)


def mm_tn(a, b, *, a_pro=None, name, col_shards=1, bf16_copy=False, bk=1024, bn=1024, bt=2048):
    t, k = a.shape
    n = b.shape[1]
    ns = n // col_shards
    bk, bn, bt = _fit(bk, k), _fit(bn, ns), _fit(bt, t)
    per = ns // bn
    last = t // bt - 1

    def body(a_ref, b_ref, o_ref, *copy_ref):
        @pl.when(pl.program_id(2) == 0)
        def _():
            o_ref[...] = jnp.zeros_like(o_ref)

        a_tile = a_ref[...] if a_pro is None else a_pro(a_ref[...])
        o_ref[...] += lax.dot_general(a_tile, b_ref[...], TN, preferred_element_type=F32)
        if bf16_copy:
            @pl.when(pl.program_id(2) == last)
            def _():
                copy_ref[0][...] = o_ref[...].astype(BF16)

    if col_shards == 1:
        out_spec = pl.BlockSpec((bk, bn), lambda i, j, s: (i, j))
        shape = (k, n)
    else:
        out_spec = pl.BlockSpec((None, bk, bn), lambda i, j, s: (j // per, i, j % per))
        shape = (col_shards, k, ns)
    out_specs, out_shape = out_spec, jax.ShapeDtypeStruct(shape, F32)
    if bf16_copy:
        out_specs, out_shape = [out_spec, out_spec], [out_shape, jax.ShapeDtypeStruct(shape, BF16)]
    return pl.pallas_call(
        body, name=name, grid=(k // bk, n // bn, t // bt),
        in_specs=[pl.BlockSpec((bt, bk), lambda i, j, s: (s, i)), pl.BlockSpec((bt, bn), lambda i, j, s: (s, j))],
        out_specs=out_specs, out_shape=out_shape,
        compiler_params=_params(("parallel", "parallel", "arbitrary")),
    )(a, b)


def _split3(x):
    hi = x.astype(BF16)
    r1 = x - hi.astype(F32)
    mid = r1.astype(BF16)
    lo = (r1 - mid.astype(F32)).astype(BF16)
    return hi, mid, lo


def cumsum_rows(x, *, reverse, name, bc=512):
    t, d = x.shape
    bc = min(bc, t)
    nb = t // bc

    def body(x_ref, o_ref, carry):
        @pl.when(pl.program_id(0) == 0)
        def _():
            carry[...] = jnp.zeros_like(carry)

        r = lax.broadcasted_iota(jnp.int32, (bc, bc), 0)
        c = lax.broadcasted_iota(jnp.int32, (bc, bc), 1)
        tri = jnp.where((r <= c) if reverse else (r >= c), 1.0, 0.0).astype(BF16)
        hi, mid, lo = _split3(x_ref[...])
        s = (lax.dot_general(tri, hi, NN, preferred_element_type=F32)
             + lax.dot_general(tri, mid, NN, preferred_element_type=F32)
             + lax.dot_general(tri, lo, NN, preferred_element_type=F32)) + carry[0:1, :]
        o_ref[...] = s
        carry[0:1, :] = s[0:1, :] if reverse else s[bc - 1:bc, :]

    imap = (lambda i: (nb - 1 - i, 0)) if reverse else (lambda i: (i, 0))
    return pl.pallas_call(
        body, name=name, grid=(nb,),
        in_specs=[pl.BlockSpec((bc, d), imap)], out_specs=pl.BlockSpec((bc, d), imap),
        out_shape=jax.ShapeDtypeStruct((t, d), F32),
        scratch_shapes=[pltpu.VMEM((8, d), F32)],
        compiler_params=_params(("arbitrary",)),
    )(x)


def _rope(x, c, a, b):
    return x * c + pltpu.roll(x, LANES - ROPE_DIM // 2, 1) * a + pltpu.roll(x, ROPE_DIM // 2, 1) * b


def _rope_bwd(d, c, a, b):
    return d * c + pltpu.roll(d * a, ROPE_DIM // 2, 1) + pltpu.roll(d * b, LANES - ROPE_DIM // 2, 1)


S_CQ, S_CKV, S_KR, S_F, S_END = 0, Q_RANK, Q_RANK + KV_RANK, Q_RANK + KV_RANK + LANES, 1024
HW = N_HEADS * LANES
FW = N_HEADS * HEAD_DIM


def mla_prep(h1, w_small_t, g_q, g_kv, w_uq, w_ukv, tab_c, tab_a, tab_b, b_f, *, name, bt=512):
    t = h1.shape[0]
    bt = min(bt, t)

    def body(h_ref, ws_ref, gq_ref, gkv_ref, wq_ref, wkv_ref, c_ref, a_ref, b_ref, bf_ref,
             mq_ref, mk_ref, mv_ref, lf_ref, cqn_ref, ckvn_ref, mqt_ref, mvt_ref, s_ref):
        s_ref[...] = lax.dot_general(h_ref[...], ws_ref[...], NT, preferred_element_type=F32)
        cq = s_ref[:, S_CQ:S_CKV]
        rq = lax.rsqrt(jnp.mean(cq * cq, axis=-1, keepdims=True) + EPS)
        cqn = (cq * rq * gq_ref[...]).astype(BF16)
        ckv = s_ref[:, S_CKV:S_KR]
        rkv = lax.rsqrt(jnp.mean(ckv * ckv, axis=-1, keepdims=True) + EPS)
        ckvn = (ckv * rkv * gkv_ref[...]).astype(BF16)
        cqn_ref[...] = cqn
        ckvn_ref[...] = ckvn
        tc, ta, tb = c_ref[...], a_ref[...], b_ref[...]
        q = jnp.dot(cqn, wq_ref[...], preferred_element_type=F32)
        kv = jnp.dot(ckvn, wkv_ref[...], preferred_element_type=F32)
        kr = _rope(s_ref[:, S_KR:S_F], tc, ta, tb)
        for h in range(N_HEADS):
            sl = slice(h * LANES, (h + 1) * LANES)
            roped = _rope(q[:, sl], tc, ta, tb)
            mq_ref[:, sl] = roped.astype(BF16)
            mqt_ref[0, sl, :] = roped.T.astype(BF16)
            mk_ref[:, sl] = (kv[:, sl] + kr).astype(BF16)
        mv_ref[...] = kv[:, HW:].astype(BF16)
        mvt_ref[0] = kv[:, HW:].T.astype(BF16)
        z = s_ref[:, S_F:S_END - LANES] + bf_ref[...]
        lf_ref[...] = jnp.minimum(z, 0.0) - jnp.log(1.0 + jnp.exp(-jnp.abs(z)))

    def row(w):
        return pl.BlockSpec((bt, w), lambda i: (i, 0))

    def full(arr):
        return pl.BlockSpec(arr.shape, lambda i: (0, 0))

    return pl.pallas_call(
        body, name=name, grid=(t // bt,),
        in_specs=[row(h1.shape[1]), full(w_small_t), full(g_q), full(g_kv), full(w_uq), full(w_ukv), row(LANES), row(LANES),
                  row(LANES), full(b_f)],
        out_specs=[row(HW), row(HW), row(FW), row(LANES), row(Q_RANK), row(KV_RANK),
                   pl.BlockSpec((1, HW, bt), lambda i: (i, 0, 0)), pl.BlockSpec((1, FW, bt), lambda i: (i, 0, 0)), row(S_END)],
        out_shape=[jax.ShapeDtypeStruct((t, HW), BF16)] * 2 + [jax.ShapeDtypeStruct((t, FW), BF16), jax.ShapeDtypeStruct((t, LANES), F32),
                   jax.ShapeDtypeStruct((t, Q_RANK), BF16), jax.ShapeDtypeStruct((t, KV_RANK), BF16),
                   jax.ShapeDtypeStruct((t // bt, HW, bt), BF16), jax.ShapeDtypeStruct((t // bt, FW, bt), BF16),
                   jax.ShapeDtypeStruct((t, S_END), F32)],
        compiler_params=_params(("parallel",)),
    )(h1, w_small_t, g_q, g_kv, w_uq, w_ukv, tab_c, tab_a, tab_b, b_f)


def mla_prep_bwd(dmq, dmk, dmv, dlf, small, g_q, g_kv, w_uq, w_ukv, tab_c, tab_a, tab_b, b_f, *, name, bt=512):
    t = small.shape[0]
    bt = min(bt, t)

    def body(dmq_ref, dmk_ref, dmv_ref, dlf_ref, s_ref, gq_ref, gkv_ref, wq_ref, wkv_ref, c_ref, a_ref, b_ref, bf_ref,
             ds_ref, dq_ref, dkv_ref, dgq_ref, dgkv_ref, db_ref):
        tc, ta, tb = c_ref[...], a_ref[...], b_ref[...]
        lane = lax.broadcasted_iota(jnp.int32, (1, LANES), 1)
        dkr = jnp.zeros((bt, LANES), F32)
        for h in range(N_HEADS):
            sl = slice(h * LANES, (h + 1) * LANES)
            dq_ref[:, sl] = _rope_bwd(dmq_ref[:, sl], tc, ta, tb).astype(BF16)
            dkr = dkr + dmk_ref[:, sl]
        dkv_ref[:, :HW] = dmk_ref[...].astype(BF16)
        dkv_ref[:, HW:] = dmv_ref[...].astype(BF16)
        in_rope = (lane >= HEAD_DIM) & (lane < HEAD_DIM + ROPE_DIM)
        ds_ref[:, S_KR:S_F] = jnp.where(in_rope, _rope_bwd(dkr, tc, ta, tb), 0.0).astype(BF16)

        def norm_bwd(raw, g_ref, dn, dg_ref):
            r = lax.rsqrt(jnp.mean(raw * raw, axis=-1, keepdims=True) + EPS)
            u = dn * g_ref[...]
            dot = jnp.mean(u * raw, axis=-1, keepdims=True)
            dg_ref[...] += jnp.sum(dn * (raw * r), axis=0, keepdims=True)
            return r * u - raw * (r * r * r * dot)

        @pl.when(pl.program_id(0) == 0)
        def _():
            dgq_ref[...] = jnp.zeros_like(dgq_ref)
            dgkv_ref[...] = jnp.zeros_like(dgkv_ref)
            db_ref[...] = jnp.zeros_like(db_ref)

        dcqn = lax.dot_general(dq_ref[...], wq_ref[...], NT, preferred_element_type=F32)
        ds_ref[:, S_CQ:S_CKV] = norm_bwd(s_ref[:, S_CQ:S_CKV], gq_ref, dcqn, dgq_ref).astype(BF16)
        dckvn = lax.dot_general(dkv_ref[...], wkv_ref[...], NT, preferred_element_type=F32)
        ds_ref[:, S_CKV:S_KR] = norm_bwd(s_ref[:, S_CKV:S_KR], gkv_ref, dckvn, dgkv_ref).astype(BF16)
        z = s_ref[:, S_F:S_END - LANES] + bf_ref[...]
        dz = jnp.where(lane < N_HEADS, dlf_ref[...] / (1.0 + jnp.exp(z)), 0.0)
        db_ref[...] += jnp.sum(dz, axis=0, keepdims=True)
        ds_ref[:, S_F:S_END - LANES] = dz.astype(BF16)
        ds_ref[:, S_END - LANES:] = jnp.zeros((bt, LANES), BF16)

    def row(w):
        return pl.BlockSpec((bt, w), lambda i: (i, 0))

    def full(arr):
        return pl.BlockSpec(arr.shape, lambda i: (0, 0))

    def vec(w):
        return pl.BlockSpec((1, w), lambda i: (0, 0))

    return pl.pallas_call(
        body, name=name, grid=(t // bt,),
        in_specs=[row(HW), row(HW), row(FW), row(LANES), row(S_END), full(g_q), full(g_kv), full(w_uq), full(w_ukv),
                  row(LANES), row(LANES), row(LANES), full(b_f)],
        out_specs=[row(S_END), row(HW), row(HW + FW), vec(Q_RANK), vec(KV_RANK), vec(LANES)],
        out_shape=[jax.ShapeDtypeStruct((t, S_END), BF16), jax.ShapeDtypeStruct((t, HW), BF16),
                   jax.ShapeDtypeStruct((t, HW + FW), BF16), jax.ShapeDtypeStruct((1, Q_RANK), F32),
                   jax.ShapeDtypeStruct((1, KV_RANK), F32), jax.ShapeDtypeStruct((1, LANES), F32)],
        compiler_params=_params(("arbitrary",)),
    )(dmq, dmk, dmv, dlf, small, g_q, g_kv, w_uq, w_ukv, tab_c, tab_a, tab_b, b_f)


class Side:
    def __init__(self, ins, out_shapes, n_sems, first, last, mid=None):
        self.ins, self.out_shapes, self.n_sems = list(ins), list(out_shapes), n_sems
        self.first, self.mid, self.last = first, mid, last

    def specs(self):
        return [ANY] * len(self.ins), [ANY] * len(self.out_shapes)

    def sems(self):
        return [pltpu.SemaphoreType.DMA((self.n_sems,)), pltpu.SemaphoreType.DMA((self.n_sems,))]


def _lane():
    return lax.broadcasted_iota(jnp.int32, (1, LANES), 1)


def _halves(x):
    zero = jnp.zeros_like(x)
    return [jnp.where(_lane() < HEAD_DIM, x, zero), jnp.where(_lane() >= HEAD_DIM, x, zero)]


def _groups(x):
    return [x[:, :LANES], x[:, LANES:]]


def _pick_row(tile, h):
    row = lax.broadcasted_iota(jnp.int32, (tile.shape[0], 1), 0)
    return jnp.sum(jnp.where(row == h, tile, 0.0), axis=0, keepdims=True)


def _pick_lane(tile, h):
    return jnp.sum(jnp.where(_lane() == h, tile, 0.0), axis=1, keepdims=True)


def _row_halves(x):
    row = lax.broadcasted_iota(jnp.int32, (LANES, 1), 0)
    zero = jnp.zeros_like(x)
    return [jnp.where(row < HEAD_DIM, x, zero), jnp.where(row >= HEAD_DIM, x, zero)]


def _below_diagonal(s):
    r = lax.broadcasted_iota(jnp.int32, s.shape, 0)
    c = lax.broadcasted_iota(jnp.int32, s.shape, 1)
    return jnp.where(c <= r, s, -jnp.inf)


def _above_diagonal(s):
    r = lax.broadcasted_iota(jnp.int32, s.shape, 0)
    c = lax.broadcasted_iota(jnp.int32, s.shape, 1)
    return jnp.where(r <= c, s, -jnp.inf)


def _split_refs(refs, counts):
    out, at = [], 0
    for n in counts:
        out.append(refs[at:at + n])
        at += n
    return out


def flash_fwd(qt_arr, k_arr, vt_arr, f_cum, *, qoff, koff, voff, pair, scale, name, blk=512, side=None):
    t = k_arr.shape[0]
    tblk = qt_arr.shape[2]
    blk = max(min(blk, t), tblk)
    sub = blk // tblk
    nb = t // blk
    steps = PAIRS * nb
    w = LANES if pair else 2 * LANES
    has_bias = f_cum is not None
    ins = [qt_arr, k_arr, vt_arr] + ([f_cum] if has_bias else [])

    def wide(ref, first):
        parts = [ref[first + u] for u in range(sub)]
        return parts[0] if sub == 1 else jnp.concatenate(parts, axis=1)
    s_ins, s_outs = (side.ins, side.out_shapes) if side else ([], [])

    def body(*refs):
        main, si, outs, so, sems = _split_refs(refs, [len(ins), len(s_ins), 2, len(s_outs), 2 if side else 0])
        qt_ref, k_ref, vt_ref = main[:3]
        f_ref = main[3] if has_bias else None
        o_ref, st_ref = outs
        g, i = pl.program_id(0), pl.program_id(1)
        step_id = g * nb + i
        if side:
            @pl.when(step_id == 0)
            def _():
                side.first(si, so, *sems)

            if side.mid is not None:
                @pl.when(step_id == (3 * steps) // 4)
                def _():
                    side.mid(si, so, *sems)

        qt = (wide(qt_ref, 0).astype(F32) * (scale * LOG2E)).astype(BF16)
        qts = _row_halves(qt) if pair else [qt[:LANES], qt[LANES:]]

        def k_of(kk, n):
            return kk if pair else kk[:, n * LANES:(n + 1) * LANES]

        def with_ones(vt_rows):
            return jnp.concatenate([vt_rows, jnp.ones((ACC_ROWS - HEAD_DIM, vt_rows.shape[1]), BF16)], axis=0)

        def step(j, carry, diagonal):
            rows = pl.ds(pl.multiple_of(j * blk, blk), blk)
            kk = k_ref[rows, :]
            vt = wide(vt_ref, sub * j)
            out = []
            for n in range(2):
                m, acc = carry[n]
                s = jnp.dot(k_of(kk, n), qts[n], preferred_element_type=F32)
                if has_bias:
                    s = s - LOG2E * _pick_lane(f_ref[rows, :], 2 * g + n)
                if diagonal:
                    s = _above_diagonal(s)
                m_new = jnp.maximum(m, jnp.max(s, axis=0, keepdims=True))
                p = jnp.exp2(s - m_new).astype(BF16)
                out.append((m_new, jnp.exp2(m - m_new) * acc
                            + jnp.dot(with_ones(vt[n * HEAD_DIM:(n + 1) * HEAD_DIM]), p, preferred_element_type=F32)))
            return tuple(out)

        def diagonal_in_halves(carry):
            h = tblk
            halves = [pl.ds(pl.multiple_of(i * blk, blk), h), pl.ds(pl.multiple_of(i * blk + h, h), h)]
            k_top, k_bot = k_ref[halves[0], :], k_ref[halves[1], :]
            vt_top, vt_bot = vt_ref[sub * i], vt_ref[sub * i + 1]
            out = []
            for n in range(2):
                m, acc = carry[n]
                heads = slice(n * HEAD_DIM, (n + 1) * HEAD_DIM)
                s_top = jnp.dot(k_of(k_top, n), qts[n], preferred_element_type=F32)
                s_bot = jnp.dot(k_of(k_bot, n), qts[n][:, h:], preferred_element_type=F32)
                if has_bias:
                    s_top = s_top - LOG2E * _pick_lane(f_ref[halves[0], :], 2 * g + n)
                    s_bot = s_bot - LOG2E * _pick_lane(f_ref[halves[1], :], 2 * g + n)
                s_top, s_bot = _above_diagonal(s_top), _above_diagonal(s_bot)
                m_top = jnp.maximum(m, jnp.max(s_top, axis=0, keepdims=True))
                m_new = jnp.concatenate([m_top[:, :h], jnp.maximum(m_top[:, h:], jnp.max(s_bot, axis=0, keepdims=True))], axis=1)
                p_top = jnp.exp2(s_top - m_new).astype(BF16)
                p_bot = jnp.exp2(s_bot - m_new[:, h:]).astype(BF16)
                late = jnp.concatenate([jnp.zeros((ACC_ROWS, h), F32),
                                        jnp.dot(with_ones(vt_bot[heads]), p_bot, preferred_element_type=F32)], axis=1)
                out.append((m_new, jnp.exp2(m - m_new) * acc
                            + jnp.dot(with_ones(vt_top[heads]), p_top, preferred_element_type=F32) + late))
            return tuple(out)

        init = tuple((jnp.full((1, blk), -jnp.inf, F32), jnp.zeros((ACC_ROWS, blk), F32)) for _ in range(2))
        carry = lax.fori_loop(0, i, lambda j, c: step(j, c, False), init)
        (ma, acca), (mb, accb) = diagonal_in_halves(carry) if sub == 2 else step(i, carry, True)
        la, lb = acca[HEAD_DIM:HEAD_DIM + 1], accb[HEAD_DIM:HEAD_DIM + 1]
        o_ref[...] = jnp.concatenate([acca[:HEAD_DIM] / la, accb[:HEAD_DIM] / lb], axis=0).T
        row = lax.broadcasted_iota(jnp.int32, (LANES, 1), 0)
        st_ref[0] = jnp.where(row == 0, ma + jnp.log2(la), jnp.where(row == 1, mb + jnp.log2(lb), 0.0)).T
        if side:
            @pl.when(step_id == steps - 1)
            def _():
                side.last(si, so, *sems)

    in_specs = [pl.BlockSpec((sub, w, tblk), lambda g, i: (i, qoff + g, 0)), pl.BlockSpec((t, w), lambda g, i: (0, koff + g)),
                pl.BlockSpec((t // tblk, LANES, tblk), lambda g, i: (0, voff + g, 0))]
    if has_bias:
        in_specs.append(pl.BlockSpec((t, LANES), lambda g, i: (0, 0)))
    s_in_specs, s_out_specs = side.specs() if side else ([], [])
    return pl.pallas_call(
        body, name=name, grid=(PAIRS, nb), in_specs=in_specs + s_in_specs,
        out_specs=[pl.BlockSpec((blk, LANES), lambda g, i: (i, g)), pl.BlockSpec((1, blk, LANES), lambda g, i: (g, i, 0))]
        + s_out_specs,
        out_shape=[jax.ShapeDtypeStruct((t, PAIRS * LANES), F32), jax.ShapeDtypeStruct((PAIRS, t, LANES), F32)] + list(s_outs),
        scratch_shapes=side.sems() if side else [],
        compiler_params=_params(("arbitrary", "arbitrary")),
    )(*ins, *s_ins)


def mix_norm(fo, mo, g_fo, g_mo, *, name, bt=512):
    t, d = fo.shape
    bt = min(bt, t)

    def body(fo_ref, mo_ref, gf_ref, gm_ref, o_ref):
        for n, (x_ref, g_ref) in enumerate(((fo_ref, gf_ref), (mo_ref, gm_ref))):
            xv = x_ref[...]
            r = lax.rsqrt(jnp.mean(xv * xv, axis=-1, keepdims=True) + EPS)
            o_ref[:, n * d:(n + 1) * d] = (xv * r * g_ref[...]).astype(BF16)

    row = pl.BlockSpec((bt, d), lambda i: (i, 0))
    vec = pl.BlockSpec((1, d), lambda i: (0, 0))
    return pl.pallas_call(
        body, name=name, grid=(t // bt,), in_specs=[row, row, vec, vec],
        out_specs=pl.BlockSpec((bt, 2 * d), lambda i: (i, 0)),
        out_shape=jax.ShapeDtypeStruct((t, 2 * d), BF16),
        compiler_params=_params(("parallel",)),
    )(fo, mo, g_fo, g_mo)


def mix_norm_bwd(dx1b, w_o, fo, mo, g_fo, g_mo, st_f, st_m, *, name, bt=512, side=None):
    t, d = fo.shape
    bt = min(bt, t)

    def body(dx_in_ref, w_ref, fo_ref, mo_ref, gf_ref, gm_ref, sf_ref, sm_ref, dfo_ref, dmo_ref, dgf_ref, dgm_ref,
             sfo_ref, smo_ref, dfot_ref, dmot_ref):
        @pl.when(pl.program_id(0) == 0)
        def _():
            dgf_ref[...] = jnp.zeros_like(dgf_ref)
            dgm_ref[...] = jnp.zeros_like(dgm_ref)

        dmixed = lax.dot_general(dx_in_ref[...], w_ref[...], NT, preferred_element_type=F32)
        groups = ((fo_ref, gf_ref, dfo_ref, dgf_ref, sf_ref, sfo_ref, dfot_ref),
                  (mo_ref, gm_ref, dmo_ref, dgm_ref, sm_ref, smo_ref, dmot_ref))
        for n, (x_ref, g_ref, dx_ref, dg_ref, st_ref, sto_ref, dxt_ref) in enumerate(groups):
            xv = x_ref[...]
            dhv = dmixed[:, n * d:(n + 1) * d]
            r = lax.rsqrt(jnp.mean(xv * xv, axis=-1, keepdims=True) + EPS)
            u = dhv * g_ref[...]
            dxf = r * u - xv * (r * r * r * jnp.mean(u * xv, axis=-1, keepdims=True))
            dxb = dxf.astype(BF16)
            dx_ref[...] = dxb
            dxt_ref[0] = dxf.T.astype(BF16)
            dg_ref[...] += jnp.sum(dhv * (xv * r), axis=0, keepdims=True)
            prod = xv * dxb.astype(F32)
            for g in range(PAIRS):
                grp = prod[:, g * LANES:(g + 1) * LANES]
                da = jnp.sum(jnp.where(_lane() < HEAD_DIM, grp, 0.0), axis=1, keepdims=True)
                db = jnp.sum(jnp.where(_lane() >= HEAD_DIM, grp, 0.0), axis=1, keepdims=True)
                sto_ref[g] = jnp.where(_lane() == 2, da, jnp.where(_lane() == 3, db, st_ref[g]))

    row = pl.BlockSpec((bt, d), lambda i: (i, 0))
    vec = pl.BlockSpec((1, d), lambda i: (0, 0))
    stat = pl.BlockSpec((PAIRS, bt, LANES), lambda i: (0, i, 0))
    return gridded(
        body, name=name, grid=(t // bt,),
        in_specs=[pl.BlockSpec((bt, dx1b.shape[1]), lambda i: (i, 0)), pl.BlockSpec(w_o.shape, lambda i: (0, 0)),
                  row, row, vec, vec, stat, stat],
        out_specs=[row, row, vec, vec, stat, stat] + [pl.BlockSpec((1, d, bt), lambda i: (i, 0, 0))] * 2,
        out_shape=[jax.ShapeDtypeStruct((t, d), BF16)] * 2 + [jax.ShapeDtypeStruct((1, d), F32)] * 2
        + [jax.ShapeDtypeStruct(st_f.shape, F32)] * 2 + [jax.ShapeDtypeStruct((t // bt, d, bt), BF16)] * 2,
        ins=[dx1b, w_o, fo, mo, g_fo, g_mo, st_f, st_m], semantics=("arbitrary",), side=side)


def flash_bwd(q_arr, qt_arr, k_arr, v_arr, do_arr, dot_arr, st, f_blocks, *, qoff, koff, voff, pair, scale, name, qblk=1024,
              side=None):
    t = q_arr.shape[0]
    blk = qt_arr.shape[2]
    qblk = max(min(qblk, t), blk)
    sub = qblk // blk
    nb, nbq = t // blk, t // qblk
    w = LANES if pair else 2 * LANES
    hw = w // 2
    has_bias = f_blocks is not None
    split = _halves if pair else _groups
    ins = [q_arr, qt_arr, k_arr, v_arr, do_arr, dot_arr, st] + ([f_blocks] if has_bias else [])
    n_out = 4 if has_bias else 3
    s_ins, s_outs = (side.ins, side.out_shapes) if side else ([], [])

    def wide(ref, first, count):
        parts = [ref[first + u] for u in range(count)]
        return parts[0] if count == 1 else jnp.concatenate(parts, axis=1)

    def body(*refs):
        main, si, outs, so, sems = _split_refs(refs, [len(ins), len(s_ins), n_out, len(s_outs), 2 if side else 0])
        q_ref, qt_ref, k_ref, v_ref, do_ref, dot_ref, st_ref = main[:7]
        dq_ref, dk_ref, dv_ref = outs[:3]
        g, j = pl.program_id(0), pl.program_id(1)
        step_id = g * nb + j
        if side:
            @pl.when(step_id == 0)
            def _():
                side.first(si, so, *sems)

        @pl.when(j == 0)
        def _():
            dq_ref[...] = jnp.zeros_like(dq_ref)

        kk, vv = k_ref[...], v_ref[...]
        ks = [kk, kk] if pair else _groups(kk)
        if has_bias:
            f_ref, df_ref = main[7], outs[3]
            fk = [LOG2E * _pick_row(f_ref[0], 2 * g + n) for n in range(2)]

            @pl.when(step_id == 0)
            def _():
                df_ref[...] = jnp.zeros_like(df_ref)

        def step(tb, count, carry, diagonal):
            rows = pl.ds(pl.multiple_of(tb * blk, blk), count * blk)
            qs = split((q_ref[rows, :].astype(F32) * (scale * LOG2E)).astype(BF16))
            qt = (wide(qt_ref, tb, count).astype(F32) * (scale * LOG2E)).astype(BF16)
            dos = [h.astype(BF16) for h in _halves(do_ref[rows, :].astype(F32))]
            dot = wide(dot_ref, tb, count)
            stats = st_ref[0, rows, :]
            new, dqs, row_sums = [], [], []
            for n in range(2):
                dkt, dvt, dfk = carry[n]
                s = lax.dot_general(qs[n], ks[n], NT, preferred_element_type=F32)
                if has_bias:
                    s = s - fk[n]
                if diagonal:
                    s = _below_diagonal(s)
                p = jnp.exp2(s - stats[:, n:n + 1])
                dp = lax.dot_general(dos[n], vv, NT, preferred_element_type=F32)
                ds = p * (dp - stats[:, 2 + n:3 + n])
                dsb = ds.astype(BF16)
                dvt = dvt + jnp.dot(dot[n * HEAD_DIM:(n + 1) * HEAD_DIM], p.astype(BF16), preferred_element_type=F32)
                dkt = dkt + jnp.dot(qt[n * hw:(n + 1) * hw], dsb, preferred_element_type=F32)
                dqs.append(jnp.dot(dsb, ks[n], preferred_element_type=F32))
                if has_bias:
                    dfk = dfk - jnp.sum(ds, axis=0, keepdims=True)
                    row_sums.append(jnp.sum(ds, axis=1, keepdims=True))
                new.append((dkt, dvt, dfk))
            dq = (jnp.where(_lane() < HEAD_DIM, dqs[0], dqs[1]) if pair else jnp.concatenate(dqs, axis=1)) * scale
            if has_bias:
                df_ref[rows, :] += jnp.where(_lane() == 2 * g, row_sums[0], jnp.where(_lane() == 2 * g + 1, row_sums[1], 0.0))
            dq_ref[rows, :] += dq
            return tuple(new)

        init = tuple((jnp.zeros((hw, blk), F32), jnp.zeros((HEAD_DIM, blk), F32), jnp.zeros((1, blk), F32)) for _ in range(2))
        carry = step(j, 1, init, True)
        whole = j // sub + 1
        carry = lax.fori_loop(j + 1, whole * sub, lambda tb, c: step(tb, 1, c, False), carry)
        (dka, dva, dfa), (dkb, dvb, dfb) = lax.fori_loop(whole, nbq, lambda i, c: step(i * sub, sub, c, False), carry)
        dk_ref[...] = jnp.concatenate([dka, dkb], axis=0).T * LN2
        dv_ref[...] = jnp.concatenate([dva, dvb], axis=0).T
        if has_bias:
            row = lax.broadcasted_iota(jnp.int32, (LANES, 1), 0)
            by_head = jnp.where(row == 2 * g, dfa, jnp.where(row == 2 * g + 1, dfb, 0.0))
            df_ref[pl.ds(pl.multiple_of(j * blk, blk), blk), :] += by_head.T
        if side:
            @pl.when(step_id == PAIRS * nb - 1)
            def _():
                side.last(si, so, *sems)

    in_specs = [pl.BlockSpec((t, w), lambda g, j: (0, qoff + g)), pl.BlockSpec((nb, w, blk), lambda g, j: (0, qoff + g, 0)),
                pl.BlockSpec((blk, w), lambda g, j: (j, koff + g)), pl.BlockSpec((blk, LANES), lambda g, j: (j, voff + g)),
                pl.BlockSpec((t, LANES), lambda g, j: (0, g)), pl.BlockSpec((nb, LANES, blk), lambda g, j: (0, g, 0)),
                pl.BlockSpec((1, t, LANES), lambda g, j: (g, 0, 0))]
    out_specs = [pl.BlockSpec((t, w), lambda g, j: (0, g)), pl.BlockSpec((blk, w), lambda g, j: (j, g)),
                 pl.BlockSpec((blk, LANES), lambda g, j: (j, g))]
    out_shape = [jax.ShapeDtypeStruct((t, PAIRS * w), F32)] * 2 + [jax.ShapeDtypeStruct((t, PAIRS * LANES), F32)]
    if has_bias:
        in_specs.append(pl.BlockSpec((1, N_HEADS, blk), lambda g, j: (j, 0, 0)))
        out_specs.append(pl.BlockSpec((t, LANES), lambda g, j: (0, 0)))
        out_shape.append(jax.ShapeDtypeStruct((t, LANES), F32))
    s_in_specs, s_out_specs = side.specs() if side else ([], [])
    return pl.pallas_call(
        body, name=name, grid=(PAIRS, nb), in_specs=in_specs + s_in_specs, out_specs=out_specs + s_out_specs,
        out_shape=out_shape + list(s_outs), scratch_shapes=side.sems() if side else [],
        compiler_params=_params(("arbitrary", "arbitrary")),
    )(*ins, *s_ins)


def _adamw_math(w, g, m, v):
    nm = ADAM_B1 * m + (1.0 - ADAM_B1) * g
    nv = ADAM_B2 * v + (1.0 - ADAM_B2) * (g * g)
    m_hat = nm / (1.0 - ADAM_B1 ** ADAM_STEP)
    v_hat = nv / (1.0 - ADAM_B2 ** ADAM_STEP)
    return -ADAM_LR * (m_hat / (jnp.sqrt(v_hat) + ADAM_EPS) + ADAM_WD * w), nm, nv


def adamw(w, g, m, v, *, name):
    rws, cols = w.shape
    br = _row_block(rws)

    def body(w_ref, g_ref, m_ref, v_ref, d_ref, nm_ref, nv_ref):
        d_ref[...], nm_ref[...], nv_ref[...] = _adamw_math(w_ref[...], g_ref[...], m_ref[...], v_ref[...])

    blk = pl.BlockSpec((br, cols), lambda i: (i, 0))
    return pl.pallas_call(
        body, name=name, grid=(rws // br,), in_specs=[blk] * 4, out_specs=[blk] * 3,
        out_shape=[jax.ShapeDtypeStruct((rws, cols), F32)] * 3,
        compiler_params=_params(("parallel",)),
    )(w, g, m, v)


def adamw_halves(w, g_mine, g_other, m, v, core, *, name):
    _, k, n = w.shape
    br = _row_block(k // 2)
    nh = k // 2 // br

    def body(c_ref, w_ref, gm_ref, go_ref, m_ref, v_ref, g_out, d_ref, nm_ref, nv_ref):
        gv = jnp.where(pl.program_id(0) == c_ref[0], gm_ref[...], go_ref[...])
        g_out[0] = gv
        d_ref[0], nm_ref[0], nv_ref[0] = _adamw_math(w_ref[0], gv, m_ref[0], v_ref[0])

    full = pl.BlockSpec((1, br, n), lambda hb, i, c: (0, hb * nh + i, 0))
    half = pl.BlockSpec((br, n), lambda hb, i, c: (i, 0))
    return pl.pallas_call(
        body, name=name,
        grid_spec=pltpu.PrefetchScalarGridSpec(num_scalar_prefetch=1, grid=(2, nh), in_specs=[full, half, half, full, full],
                                               out_specs=[full] * 4),
        out_shape=[jax.ShapeDtypeStruct(w.shape, F32)] * 4,
        compiler_params=_params(("parallel", "parallel")),
    )(core, w, g_mine, g_other, m, v)


def adamw_halves_t(wt, gt_mine, gt_other, mt, vt, core, *, name, bc=128):
    n, k = wt.shape
    nh = k // 2 // bc

    def body(c_ref, w_ref, gm_ref, go_ref, m_ref, v_ref, g_out, d_ref, nm_ref, nv_ref):
        gv = jnp.where(pl.program_id(0) == c_ref[0], gm_ref[...], go_ref[...])
        g_out[...] = gv
        d_ref[...], nm_ref[...], nv_ref[...] = _adamw_math(w_ref[...], gv, m_ref[...], v_ref[...])

    full = pl.BlockSpec((n, bc), lambda hb, i, c: (0, hb * nh + i))
    half = pl.BlockSpec((n, bc), lambda hb, i, c: (0, i))
    return pl.pallas_call(
        body, name=name,
        grid_spec=pltpu.PrefetchScalarGridSpec(num_scalar_prefetch=1, grid=(2, nh), in_specs=[full, half, half, full, full],
                                               out_specs=[full] * 4),
        out_shape=[jax.ShapeDtypeStruct(wt.shape, F32)] * 4,
        compiler_params=_params(("parallel", "parallel")),
    )(core, wt, gt_mine, gt_other, mt, vt)


def add_pair(dw, recv, core, *, name):
    n4, k, n = dw.shape
    half = (1, k // 2, n) if split_axis(k) == 0 else (1, k, n // 2)
    mine = (lambda q, c: (q, c[0], 0)) if split_axis(k) == 0 else (lambda q, c: (q, 0, c[0]))

    def body(c_ref, a_ref, b_ref, o_ref):
        o_ref[...] = (a_ref[...] + b_ref[...].astype(F32)).astype(BF16)

    return pl.pallas_call(
        body, name=name,
        grid_spec=pltpu.PrefetchScalarGridSpec(
            num_scalar_prefetch=1, grid=(n4,),
            in_specs=[pl.BlockSpec(half, mine), pl.BlockSpec(half, lambda q, c: (q, 0, 0))],
            out_specs=pl.BlockSpec(half, lambda q, c: (q, 0, 0))),
        out_shape=jax.ShapeDtypeStruct((n4,) + half[1:], BF16),
        compiler_params=_params(("parallel",)),
    )(core, dw, recv)


def sum_chips(parts, *, name):
    n4, r, n = parts.shape
    if r % 16 == 0:
        br, bc = _row_block(r), n
    else:
        br, bc = r, LANES

    def body(p_ref, o_ref):
        acc = p_ref[0].astype(F32)
        for q in range(1, n4):
            acc = acc + p_ref[q].astype(F32)
        o_ref[...] = acc

    return pl.pallas_call(
        body, name=name, grid=(r // br, n // bc),
        in_specs=[pl.BlockSpec((n4, br, bc), lambda i, j: (0, i, j))], out_specs=pl.BlockSpec((br, bc), lambda i, j: (i, j)),
        out_shape=jax.ShapeDtypeStruct((r, n), F32),
        compiler_params=_params(("parallel", "parallel")),
    )(parts)


ANY = pl.BlockSpec(memory_space=pl.ANY)


def _place():
    x, y, c = lax.axis_index("x"), lax.axis_index("y"), lax.axis_index("c")
    chips = [(1 - x, y), (x, 1 - y), (1 - x, 1 - y)]
    return x, y, c, chips


def _copy(src, dst, send_sems, recv_sems, k, to):
    return pltpu.make_async_remote_copy(src_ref=src, dst_ref=dst, send_sem=send_sems.at[k], recv_sem=recv_sems.at[k],
                                        device_id=to, device_id_type=MESH)


def split_axis(rows):
    return 0 if rows % 32 == 0 else 1


def _half(ref, lead, hf):
    rows, cols = ref.shape[-2:]
    if split_axis(rows) == 0:
        at = (pl.ds(hf * (rows // 2), rows // 2), slice(None))
    else:
        at = (slice(None), pl.ds(hf * (cols // 2), cols // 2))
    return ref.at[at] if lead is None else ref.at[(lead,) + at]


def _gather_first(srcs, dsts, ssems, rsems):
    x, y, c, chips = _place()
    for ti, (s, d) in enumerate(zip(srcs, dsts)):
        for j, (cx, cy) in enumerate(chips):
            _copy(_half(s, None, c), _half(d, 2 * x + y, c), ssems, rsems, 3 * ti + j, (cx, cy, c)).start()


def _gather_mid(srcs, dsts, ssems, rsems):
    x, y, c, chips = _place()
    n1 = 3 * len(srcs)
    for ti, d in enumerate(dsts):
        for j, (cx, cy) in enumerate(chips):
            landed = _half(d, 2 * cx + cy, c)
            _copy(landed, landed, ssems, rsems, 3 * ti + j, (cx, cy, c)).wait_recv()
            _copy(landed, landed, ssems, rsems, n1 + 3 * ti + j, (x, y, 1 - c)).start()


def _gather_last(srcs, dsts, ssems, rsems):
    x, y, c, chips = _place()
    n1 = 3 * len(srcs)
    for ti, (s, d) in enumerate(zip(srcs, dsts)):
        for j, (cx, cy) in enumerate(chips):
            other = _half(d, 2 * cx + cy, 1 - c)
            _copy(other, other, ssems, rsems, n1 + 3 * ti + j, (x, y, 1 - c)).wait_recv()
        for j, (cx, cy) in enumerate(chips):
            mine = _half(s, None, c)
            _copy(mine, mine, ssems, rsems, 3 * ti + j, (cx, cy, c)).wait_send()
            _copy(mine, mine, ssems, rsems, n1 + 3 * ti + j, (x, y, 1 - c)).wait_send()


def gather_side(shards):
    return Side(shards, [jax.ShapeDtypeStruct((N_CHIPS,) + s.shape, s.dtype) for s in shards], 6 * len(shards),
                _gather_first, _gather_last, _gather_mid)


def _scatter_first(srcs, dsts, ssems, rsems):
    x, y, c, chips = _place()
    for ti, (s, d) in enumerate(zip(srcs, dsts)):
        for j, (cx, cy) in enumerate(chips):
            _copy(s.at[2 * cx + cy], d.at[2 * x + y], ssems, rsems, 3 * ti + j, (cx, cy, c)).start()


def _scatter_last(srcs, dsts, ssems, rsems):
    x, y, c, chips = _place()
    for ti, (s, d) in enumerate(zip(srcs, dsts)):
        for j, (cx, cy) in enumerate(chips):
            _copy(s.at[2 * cx + cy], d.at[2 * cx + cy], ssems, rsems, 3 * ti + j, (cx, cy, c)).wait_recv()
        for j, (cx, cy) in enumerate(chips):
            _copy(s.at[2 * cx + cy], d.at[2 * cx + cy], ssems, rsems, 3 * ti + j, (cx, cy, c)).wait_send()


def scatter_side(parts):
    return Side(parts, [jax.ShapeDtypeStruct(p.shape, p.dtype) for p in parts], 3 * len(parts), _scatter_first, _scatter_last)


def run_side(side, *, name):
    n_in, n_out = len(side.ins), len(side.out_shapes)

    def body(*refs):
        si, so, sems = _split_refs(refs, [n_in, n_out, 2])
        side.first(si, so, *sems)
        if side.mid is not None:
            side.mid(si, so, *sems)
        side.last(si, so, *sems)

    in_specs, out_specs = side.specs()
    return pl.pallas_call(body, name=name, in_specs=in_specs, out_specs=out_specs, out_shape=side.out_shapes,
                          scratch_shapes=side.sems())(*side.ins)


def _swap_first(srcs, dsts, ssems, rsems):
    x, y, c, _ = _place()
    for k, (s, d) in enumerate(zip(srcs, dsts)):
        _copy(s, d, ssems, rsems, k, (x, y, 1 - c)).start()


def _swap_last(srcs, dsts, ssems, rsems):
    x, y, c, _ = _place()
    for k, (s, d) in enumerate(zip(srcs, dsts)):
        _copy(s, d, ssems, rsems, k, (x, y, 1 - c)).wait()


def swap_side(xs):
    return Side(xs, [jax.ShapeDtypeStruct(a.shape, a.dtype) for a in xs], len(xs), _swap_first, _swap_last)


def _swap_halves(srcs, dsts, ssems, rsems):
    x, y, c, _ = _place()
    for k, (s, d) in enumerate(zip(srcs, dsts)):
        hk = s.shape[1] // 2
        yield _copy(s.at[:, pl.ds((1 - c) * hk, hk), :], d, ssems, rsems, k, (x, y, 1 - c))


def _swap_halves_first(srcs, dsts, ssems, rsems):
    for cp in _swap_halves(srcs, dsts, ssems, rsems):
        cp.start()


def _swap_halves_last(srcs, dsts, ssems, rsems):
    for cp in _swap_halves(srcs, dsts, ssems, rsems):
        cp.wait()


def swap_halves_side(xs):
    return Side(xs, [jax.ShapeDtypeStruct((a.shape[0], a.shape[1] // 2, a.shape[2]), a.dtype) for a in xs], len(xs),
                _swap_halves_first, _swap_halves_last)


def allreduce_small(s):
    n_dev = 8

    def body(s_ref, out_ref, buf, send_sems, recv_sems):
        x, y, c, _ = _place()
        me = 4 * x + 2 * y + c
        buf[me] = s_ref[...]
        sends = []
        for k in range(1, n_dev):
            px = 1 - x if k & 4 else x
            py = 1 - y if k & 2 else y
            pc = 1 - c if k & 1 else c
            cp = _copy(s_ref, buf.at[me], send_sems, recv_sems, k - 1, (px, py, pc))
            cp.start()
            sends.append((cp, 4 * px + 2 * py + pc))
        for k, (cp, peer) in enumerate(sends):
            _copy(s_ref, buf.at[peer], send_sems, recv_sems, k, (x, y, c)).wait_recv()
        for cp, _ in sends:
            cp.wait_send()
        acc = buf[0]
        for d in range(1, n_dev):
            acc = acc + buf[d]
        out_ref[...] = acc

    vm = pl.BlockSpec(memory_space=pltpu.VMEM)
    return pl.pallas_call(
        body, name="allreduce_small", in_specs=[vm], out_specs=vm,
        out_shape=jax.ShapeDtypeStruct(s.shape, F32),
        scratch_shapes=[pltpu.VMEM((n_dev,) + s.shape, F32), pltpu.SemaphoreType.DMA((n_dev - 1,)),
                        pltpu.SemaphoreType.DMA((n_dev - 1,))],
    )(s)


def join_cols(sm):
    n4, k, n = sm.shape
    return sm.transpose(1, 0, 2).reshape(k, n4 * n)


def split_cols(full):
    k, n = full.shape
    return full.reshape(k, N_CHIPS, n // N_CHIPS).transpose(1, 0, 2)


def _pad_heads(w, width):
    lead, heads = w.shape[:-1], w.shape[-1] // width
    w = w.reshape(lead + (heads, width))
    return jnp.pad(w, [(0, 0)] * len(lead) + [(0, 0), (0, LANES - width)]).reshape(lead + (heads * LANES,))


def _unpad_heads(w, width):
    lead, heads = w.shape[:-1], w.shape[-1] // LANES
    return w.reshape(lead + (heads, LANES))[..., :width].reshape(lead + (heads * width,))


O_F = 3 * FW
O_CQ = O_F + N_HEADS
O_CKV = O_CQ + Q_RANK
O_KR = O_CKV + KV_RANK
O_END = O_KR + ROPE_DIM


def _shard_rows(sm, a, b):
    r = sm.shape[1]
    out = []
    while a < b:
        q = a // r
        e = min(b, (q + 1) * r)
        out.append(sm[q, a - q * r:e - q * r])
        a = e
    return out


def split_w_in_t(w_sm):
    d = w_sm.shape[2]

    def z(n):
        return [jnp.zeros((n, d), w_sm.dtype)]

    small = (_shard_rows(w_sm, O_CQ, O_CKV) + _shard_rows(w_sm, O_CKV, O_KR) + z(HEAD_DIM) + _shard_rows(w_sm, O_KR, O_END)
             + z(LANES - HEAD_DIM - ROPE_DIM) + _shard_rows(w_sm, O_F, O_CQ) + z(LANES - N_HEADS) + z(LANES))
    return jnp.concatenate(_shard_rows(w_sm, 0, O_F), axis=0), jnp.concatenate(small, axis=0)


def join_w_in_t(d_qkv_t, d_small_t):
    kr = S_KR + HEAD_DIM
    segments = [(d_qkv_t, 0, 0, O_F), (d_small_t, S_F, O_F, N_HEADS), (d_small_t, S_CQ, O_CQ, Q_RANK),
                (d_small_t, S_CKV, O_CKV, KV_RANK), (d_small_t, kr, O_KR, ROPE_DIM)]
    r = O_END // N_CHIPS
    shards = []
    for q in range(N_CHIPS):
        pieces = []
        for src, s0, v0, n in segments:
            a, b = max(v0, q * r), min(v0 + n, (q + 1) * r)
            if a < b:
                pieces.append(src[s0 + a - v0:s0 + b - v0])
        shards.append(jnp.concatenate(pieces, axis=0))
    return jnp.stack(shards)


def rope_tables(pos):
    t = pos.shape[0]
    inv_freq = ROPE_THETA ** (-jnp.arange(0, ROPE_DIM, 2, dtype=F32) / ROPE_DIM)
    ang = pos.astype(F32)[:, None] * inv_freq
    cos, sin = jnp.cos(ang), jnp.sin(ang)
    half = ROPE_DIM // 2

    def z(n):
        return jnp.zeros((t, n), F32)

    tab_c = jnp.concatenate([jnp.ones((t, HEAD_DIM), F32), cos, cos, z(LANES - HEAD_DIM - ROPE_DIM)], axis=1)
    tab_a = jnp.concatenate([z(HEAD_DIM), -sin, z(half), z(LANES - HEAD_DIM - ROPE_DIM)], axis=1)
    tab_b = jnp.concatenate([z(HEAD_DIM), z(half), sin, z(LANES - HEAD_DIM - ROPE_DIM)], axis=1)
    return tab_c, tab_a, tab_b


def _pad_lanes(v, n):
    return jnp.pad(v, ((0, 0), (0, n - v.shape[1])))


ATTN_BLK = 512
ATTN_FWD_BLK = 1024
ATTN_BWD_QBLK = 1024
ACC_ROWS = HEAD_DIM + 16


def local_step(xs, pos, tgt, gains, early_weights, late_weights, early_side=None, fwd_sides=(None, None), reduction=None):
    g_attn, b_forget, g_q, g_kv, g_fo, g_mo, g_mlp, g_fin = gains
    t = xs.shape[0]
    blk = min(ATTN_BLK, t)
    fox_scale = 1.0 / (HEAD_DIM ** 0.5)
    mla_scale = 1.0 / ((HEAD_DIM + ROPE_DIM) ** 0.5)

    h1, *gathered = rmsnorm(xs, g_attn, out_dtype=BF16, name="norm_attn", side=early_side)
    w_in_t, w_uq_p, w_ukv = early_weights(gathered)
    w_qkv_t, w_small_t = split_w_in_t(w_in_t)
    kv = w_ukv.reshape(KV_RANK, N_HEADS, 2 * HEAD_DIM)
    w_ukv_p = jnp.concatenate([_pad_heads(kv[:, :, :HEAD_DIM].reshape(KV_RANK, FW), HEAD_DIM),
                               kv[:, :, HEAD_DIM:].reshape(KV_RANK, FW)], axis=1)
    b_f = _pad_lanes(b_forget, LANES)
    tab_c, tab_a, tab_b = rope_tables(pos)

    qkv, qkv_t = mm(h1, w_qkv_t, trans_b=True, out_dtypes=[BF16], t_blk=blk, name="proj_qkv", bn=1536)
    mq, mk, mv, lf, cqn, ckvn, mq_t, mv_t, small = mla_prep(h1, w_small_t, g_q, g_kv, w_uq_p, w_ukv_p, tab_c, tab_a, tab_b, b_f,
                                                            name="proj_small_mla_prep", bt=blk)
    f_cum = cumsum_rows(lf, reverse=False, name="gate_cumsum")
    f_blocks = f_cum[:, :N_HEADS].reshape(t // blk, blk, N_HEADS).transpose(0, 2, 1)
    fo, st_f, *gathered = flash_fwd(qkv_t, qkv, qkv_t, f_cum, qoff=0, koff=PAIRS, voff=2 * PAIRS, pair=True,
                                    scale=fox_scale, name="fox_fwd", blk=ATTN_FWD_BLK, side=fwd_sides[0])
    mo, st_m, *more = flash_fwd(mq_t, mk, mv_t, None, qoff=0, koff=0, voff=0, pair=False, scale=mla_scale, name="mla_fwd",
                                blk=ATTN_FWD_BLK, side=fwd_sides[1])
    w_o, w_up, w_down = late_weights(gathered + more)
    mixed = mix_norm(fo, mo, g_fo, g_mo, name="norm_mix")

    def inv_rms(v):
        return lax.rsqrt(jnp.mean(v * v, axis=-1, keepdims=True) + EPS)

    def residual_then_norm(acc, res, g):
        xn = acc + res
        return xn, xn * inv_rms(xn) * g

    def norm_bwd(dh, xn, res, g):
        r = inv_rms(xn)
        uu = dh * g
        return (r * uu - xn * (r * r * r * jnp.mean(uu * xn, axis=-1, keepdims=True)) + res,
                jnp.sum(dh * (xn * r), axis=0, keepdims=True))

    def norm_bwd2(dh, xn, res, g):
        dx, dg = norm_bwd(dh, xn, res, g)
        return dx, dx, dg

    def residual_then_loss(acc, res, target, g):
        xn = acc + res
        r = inv_rms(xn)
        xh = xn * r
        e = xh * g - target
        part = 0.5 * jnp.sum(jnp.mean(e * e, axis=-1, keepdims=True), axis=0, keepdims=True)
        dy = e * (1.0 / xn.shape[1])
        uu = dy * g
        dx = r * uu - xn * (r * r * r * jnp.mean(uu * xn, axis=-1, keepdims=True))
        return dx, dx, jnp.sum(dy * xh, axis=0, keepdims=True), part + jnp.zeros_like(g)

    x1, h2 = mm(mixed, w_o, extras=[xs], vecs=[g_mlp], epilogue=residual_then_norm, out_dtypes=[F32, BF16], name="out_proj")

    def relu2(uu):
        r = jnp.maximum(uu.astype(F32), 0.0)
        return (r * r).astype(BF16)

    u, = mm(h2, w_up, out_dtypes=[BF16], name="mlp_up", bn=2048)
    dx2, dx2b, dg_fin, loss_row = mm(u, w_down, a_pro=relu2, extras=[x1, tgt], vecs=[g_fin], epilogue=residual_then_loss,
                                     out_dtypes=[F32, BF16], n_sums=2, name="mlp_down_loss")
    loss = loss_row[:, :1]

    def relu2_grad(acc, uu):
        return (acc * (2.0 * jnp.maximum(uu.astype(F32), 0.0)),)

    du, = mm(dx2b, w_down, trans_b=True, extras=[u], epilogue=relu2_grad, out_dtypes=[BF16], name="mlp_down_bwd", bn=2048)
    dw_down, dw_down_b = (g.reshape(N_CHIPS, -1, w_down.shape[1])
                          for g in mm_tn(u, dx2b, a_pro=relu2, name="dw_down", bf16_copy=True))
    dx1, dx1b, dg_mlp = mm(du, w_up, trans_b=True, extras=[x1, dx2], vecs=[g_mlp], epilogue=norm_bwd2,
                           out_dtypes=[F32, BF16], n_sums=1, name="mlp_up_bwd")
    dw_up, dw_up_b = mm_tn(h2, du, name="dw_up", col_shards=N_CHIPS, bf16_copy=True)

    dw_o, dw_o_b = (g.reshape(N_CHIPS, -1, w_o.shape[1]) for g in mm_tn(mixed, dx1b, name="dw_o", bf16_copy=True))
    late, late_b = (dw_o, dw_up, dw_down), (dw_o_b, dw_up_b, dw_down_b)
    red = reduction
    dfo, dmo, dg_fo, dg_mo, st_f, st_m, dfo_t, dmo_t, *got = mix_norm_bwd(
        dx1b, w_o, fo, mo, g_fo, g_mo, st_f, st_m, name="out_proj_mix_bwd", bt=blk,
        side=red.late_swap(late, late_b) if red else None)
    dfq, dfk, dfv, d_f, *got = flash_bwd(
        qkv, qkv_t, qkv, qkv, dfo, dfo_t, st_f, f_blocks, qoff=0, koff=PAIRS, voff=2 * PAIRS, pair=True,
        scale=fox_scale, name="fox_bwd", qblk=ATTN_BWD_QBLK, side=red.late_scatter(got) if red else None)
    dmq, dmk, dmv, *got = flash_bwd(mq, mq_t, mk, mv, dmo, dmo_t, st_m, None, qoff=0, koff=0, voff=0,
                                    pair=False, scale=mla_scale, name="mla_bwd", qblk=ATTN_BWD_QBLK,
                                    side=red.late_halves(got) if red else None)
    if red:
        red.late_done(got)
    dqkv = jnp.concatenate([dfq, dfk, dfv], axis=1).astype(BF16)
    dlf = cumsum_rows(d_f, reverse=True, name="gate_cumsum_bwd")
    dsmall, dq_u, dkv_u, dg_q, dg_kv, db_f = mla_prep_bwd(dmq, dmk, dmv, dlf, small, g_q, g_kv, w_uq_p, w_ukv_p,
                                                          tab_c, tab_a, tab_b, b_f, name="mla_prep_bwd")
    dw_uq_p = mm_tn(cqn, dq_u, name="dw_uq")
    dw_ukv_p = mm_tn(ckvn, dkv_u, name="dw_ukv")

    dw_in_t = join_w_in_t(mm_tn(dqkv, h1, name="dw_qkv"), mm_tn(dsmall, h1, name="dw_small"))
    dk_cols = _unpad_heads(dw_ukv_p[:, :HW], HEAD_DIM).reshape(KV_RANK, N_HEADS, HEAD_DIM)
    dv_cols = dw_ukv_p[:, HW:].reshape(KV_RANK, N_HEADS, HEAD_DIM)
    dw_ukv = jnp.concatenate([dk_cols, dv_cols], axis=2).reshape(KV_RANK, N_HEADS * 2 * HEAD_DIM)
    early = (dw_in_t, split_cols(dw_uq_p), split_cols(dw_ukv))
    grad_x, dg_attn, *got = mm([dqkv, dsmall], [w_qkv_t, w_small_t], extras=[xs, dx1], vecs=[g_attn],
                               epilogue=norm_bwd, out_dtypes=[F32], n_sums=1, name="proj_bwd",
                               side=red.early_scatter(early) if red else None)
    if red:
        red.early_done(got)
    d_gains = (dg_attn, db_f[:, :N_HEADS], dg_q, dg_kv, dg_fo, dg_mo, dg_mlp, dg_fin)
    return loss, grad_x, early, late, d_gains


class GradReduction:
    def __init__(self, core_id, chip):
        self.core_id, self.chip = core_id, chip
        self.core = core_id.reshape(1).astype(jnp.int32)
        self.grads, self.pairs, self.halves, self.others = {}, {}, {}, {}

    def _other_halves(self, grads):
        out = []
        for g in grads:
            axis = 1 + split_axis(g.shape[1])
            size = g.shape[axis] // 2
            out.append(lax.dynamic_slice_in_dim(g, (1 - self.core_id) * size, size, axis=axis).astype(BF16))
        return out

    def _add_pairs(self, group, recvs):
        self.pairs[group] = [add_pair(g, r, self.core, name="add_pair_%s_%d" % (group, n))
                             for n, (g, r) in enumerate(zip(self.grads[group], recvs))]
        return scatter_side(self.pairs[group])

    def _chip_sums(self, group, scattered):
        chip = self.chip
        with_mine = [lax.dynamic_update_index_in_dim(s, lax.dynamic_index_in_dim(p, chip, 0, keepdims=True), chip, 0)
                     for s, p in zip(scattered, self.pairs[group])]
        self.halves[group] = [sum_chips(s, name="sum_chips_%s_%d" % (group, n)) for n, s in enumerate(with_mine)]
        return self.halves[group]

    def late_swap(self, grads, bf16_copies):
        self.grads["late"] = list(grads)
        return swap_halves_side(list(bf16_copies))

    def late_scatter(self, recvs):
        return self._add_pairs("late", recvs)

    def late_halves(self, scattered):
        return swap_side(self._chip_sums("late", scattered))

    def late_done(self, others):
        self.others["late"] = list(others)

    def early_scatter(self, grads):
        self.grads["early"] = list(grads)
        return self._add_pairs("early", run_side(swap_side(self._other_halves(grads)), name="swap_early_sends"))

    def early_done(self, scattered):
        self.others["early"] = list(run_side(swap_side(self._chip_sums("early", scattered)), name="swap_early_halves"))

def kernel(x, positions, attn_norm_g, w_in, b_forget, q_norm_g, w_uq, kv_norm_g, w_ukv, fox_out_g, mla_out_g, w_o, mlp_norm_g, w_up, w_down, final_norm_g, loss_target, m_attn_norm_g, m_w_in, m_b_forget, m_q_norm_g, m_w_uq, m_kv_norm_g, m_w_ukv, m_fox_out_g, m_mla_out_g, m_w_o, m_mlp_norm_g, m_w_up, m_w_down, m_final_norm_g, v_attn_norm_g, v_w_in, v_b_forget, v_q_norm_g, v_w_uq, v_kv_norm_g, v_w_ukv, v_fox_out_g, v_mla_out_g, v_w_o, v_mlp_norm_g, v_w_up, v_w_down, v_final_norm_g):
    core_id = lax.axis_index("c")
    core = core_id.reshape(1).astype(jnp.int32)
    chip = 2 * lax.axis_index("x") + lax.axis_index("y")
    big = [w_in, w_uq, w_ukv, w_o, w_up, w_down]
    big_m = [m_w_in, m_w_uq, m_w_ukv, m_w_o, m_w_up, m_w_down]
    big_v = [v_w_in, v_w_uq, v_w_ukv, v_w_o, v_w_up, v_w_down]
    n_early = 3

    def vec(a):
        return a.reshape(1, -1)

    small = [attn_norm_g, b_forget, q_norm_g, kv_norm_g, fox_out_g, mla_out_g, mlp_norm_g, final_norm_g]
    small_m = [m_attn_norm_g, m_b_forget, m_q_norm_g, m_kv_norm_g, m_fox_out_g, m_mla_out_g, m_mlp_norm_g, m_final_norm_g]
    small_v = [v_attn_norm_g, v_b_forget, v_q_norm_g, v_kv_norm_g, v_fox_out_g, v_mla_out_g, v_mlp_norm_g, v_final_norm_g]
    gains = [vec(a) for a in small]

    views = [big[0][0].T, _pad_heads(big[1][0], HEAD_DIM + ROPE_DIM)] + [w[0] for w in big[2:]]
    shards = [v.astype(BF16) for v in views]

    def with_own(gathered, mine):
        return [lax.dynamic_update_index_in_dim(g, s, chip, 0) for g, s in zip(gathered, mine)]

    def early_weights(gathered):
        g_in, g_uq, g_ukv = with_own(gathered, shards[:n_early])
        return g_in, join_cols(g_uq), join_cols(g_ukv)

    def late_weights(gathered):
        g_o, g_up, g_down = with_own(gathered, shards[n_early:])
        return g_o.reshape(-1, g_o.shape[2]), join_cols(g_up), g_down.reshape(-1, g_down.shape[2])

    reduction = GradReduction(core_id, chip)
    loss, grad_x, _, _, d_small = local_step(
        x[0], positions[0], loss_target[0], gains, early_weights, late_weights, gather_side(shards[:n_early]),
        (gather_side(shards[n_early:-1]), gather_side(shards[-1:])), reduction)
    halves = reduction.halves["early"] + reduction.halves["late"]
    others = reduction.others["early"] + reduction.others["late"]

    def rows8(vs):
        return jnp.concatenate([_pad_lanes(vec(a).astype(F32), 1024) for a in vs], axis=0)

    with_loss = [jnp.concatenate([d, loss], axis=1) if n == 1 else d for n, d in enumerate(d_small)]
    g_small8 = allreduce_small(rows8(with_loss))

    outs_big = []
    for n, (w, gm, go, m, v) in enumerate(zip(big, halves, others, big_m, big_v)):
        if n == 0:
            outs = adamw_halves_t(w[0].T, gm, go, m[0].T, v[0].T, core, name="adamw_%d" % n)
            outs_big.append([o.T[None] for o in outs])
        else:
            if n == 1:
                gm, go = (_unpad_heads(gh, HEAD_DIM + ROPE_DIM) for gh in (gm, go))
            outs_big.append(adamw_halves(w, gm, go, m, v, core, name="adamw_%d" % n))
    d8, m8, v8 = adamw(rows8(small), g_small8, rows8(small_m), rows8(small_v), name="adamw_small")

    def unrows8(a8):
        return [a8[n, :s.size].reshape(s.shape) for n, s in enumerate(small)]

    loss_all = g_small8[1, N_HEADS]
    grads, deltas, new_m, new_v = [None] * 14, [None] * 14, [None] * 14, [None] * 14
    big_at = [1, 4, 6, 9, 11, 12]
    small_at = [0, 2, 3, 5, 7, 8, 10, 13]
    for n, at in enumerate(big_at):
        grads[at], deltas[at], new_m[at], new_v[at] = outs_big[n]
    for at, g, dd, mm_, vv in zip(small_at, unrows8(g_small8), unrows8(d8), unrows8(m8), unrows8(v8)):
        grads[at], deltas[at], new_m[at], new_v[at] = g, dd, mm_, vv
    return (loss_all, grad_x[None], *grads, *deltas, *new_m, *new_v)
```

```python
import jax
import jax.numpy as jnp
from jax import lax
from jax.experimental import pallas as pl
from jax.experimental.pallas import tpu as pltpu

F32 = jnp.float32
BF16 = jnp.bfloat16
MESH = pl.DeviceIdType.MESH

EPS = 1e-6
ROPE_THETA = 10000.0
N_HEADS = 8
PAIRS = N_HEADS // 2
HEAD_DIM = 64
ROPE_DIM = 32
LANES = 128
Q_RANK = 384
KV_RANK = 256
N_CHIPS = 4
ADAM_LR, ADAM_B1, ADAM_B2, ADAM_EPS, ADAM_WD, ADAM_STEP = 0.001, 0.9, 0.999, 1e-08, 0.01, 10
VMEM_LIMIT = 48 * 1024 * 1024
LOG2E = 1.4426950408889634
LN2 = 0.6931471805599453
NN = (((1,), (0,)), ((), ()))
NT = (((1,), (1,)), ((), ()))
TN = (((0,), (0,)), ((), ()))


def _params(sem=None):
    return pltpu.CompilerParams(dimension_semantics=sem, vmem_limit_bytes=VMEM_LIMIT)


def _fit(block, dim):
    if dim <= block:
        return dim
    return next(b for b in range(block - block % LANES, 0, -LANES) if dim % b == 0)


def _row_block(rows):
    return next(b for b in (256, 128, 64, 32, 16, 8) if rows % b == 0)


def gridded(body, *, name, grid, in_specs, out_specs, out_shape, ins, semantics, side=None):
    if side is None:
        return pl.pallas_call(body, name=name, grid=grid, in_specs=in_specs, out_specs=out_specs, out_shape=out_shape,
                              compiler_params=_params(semantics))(*ins)
    n_in, n_out = len(in_specs), len(out_specs)
    steps = 1
    for extent in grid:
        steps *= extent

    def riding(*refs):
        main_in, s_in, main_out, s_out, sems = _split_refs(refs, [n_in, len(side.ins), n_out, len(side.out_shapes), 2])
        step = 0
        for axis, extent in enumerate(grid):
            step = step * extent + pl.program_id(axis)

        @pl.when(step == 0)
        def _():
            side.first(s_in, s_out, *sems)

        body(*main_in, *main_out)

        @pl.when(step == steps - 1)
        def _():
            if side.mid is not None:
                side.mid(s_in, s_out, *sems)
            side.last(s_in, s_out, *sems)

    s_in_specs, s_out_specs = side.specs()
    return pl.pallas_call(
        riding, name=name, grid=grid, in_specs=list(in_specs) + s_in_specs, out_specs=list(out_specs) + s_out_specs,
        out_shape=list(out_shape) + side.out_shapes, scratch_shapes=side.sems(),
        compiler_params=_params(("arbitrary",) * len(grid)))(*ins, *side.ins)


def rmsnorm(x, g, *, out_dtype, name, bt=512, side=None):
    t, d = x.shape
    bt = min(bt, t)

    def body(x_ref, g_ref, o_ref):
        xv = x_ref[...].astype(F32)
        r = lax.rsqrt(jnp.mean(xv * xv, axis=-1, keepdims=True) + EPS)
        o_ref[...] = (xv * r * g_ref[...]).astype(o_ref.dtype)

    return gridded(
        body, name=name, grid=(t // bt,),
        in_specs=[pl.BlockSpec((bt, d), lambda i: (i, 0)), pl.BlockSpec((1, d), lambda i: (0, 0))],
        out_specs=[pl.BlockSpec((bt, d), lambda i: (i, 0))],
        out_shape=[jax.ShapeDtypeStruct((t, d), out_dtype)],
        ins=[x, g], semantics=("parallel",), side=side)


def mm(a, b, *, trans_b=False, a_pro=None, extras=(), vecs=(), epilogue=None, out_dtypes, n_sums=0, t_blk=None, name,
       bm=1024, bn=1024, side=None):
    a_list = list(a) if isinstance(a, (list, tuple)) else [a]
    b_list = list(b) if isinstance(b, (list, tuple)) else [b]
    m = a_list[0].shape[0]
    n = b_list[0].shape[0] if trans_b else b_list[0].shape[1]
    ks = [x.shape[1] for x in a_list]
    if sum(ks) > 2048:
        bm = bm // 2
    bm, bn = _fit(bm, m), _fit(bn, n)
    assert n_sums == 0 or bn == n
    n_ab, n_ex, n_vec, n_out = len(a_list), len(extras), len(vecs), len(out_dtypes)
    n_t = 0 if t_blk is None else 1

    def body(*refs):
        a_refs, b_refs, ex, vs, outs, t_outs, sums = _split_refs(refs, [n_ab, n_ab, n_ex, n_vec, n_out, n_t, n_sums])
        acc = None
        for a_ref, b_ref in zip(a_refs, b_refs):
            a_tile = a_ref[...] if a_pro is None else a_pro(a_ref[...])
            part = lax.dot_general(a_tile, b_ref[...], NT if trans_b else NN, preferred_element_type=F32)
            acc = part if acc is None else acc + part
        res = epilogue(acc, *[e[...] for e in ex], *[v[...] for v in vs]) if epilogue is not None else (acc,)
        for o, r in zip(outs, res[:n_out]):
            o[...] = r.astype(o.dtype)
        for t_ref in t_outs:
            for u in range(bm // t_blk):
                t_ref[u] = res[0][u * t_blk:(u + 1) * t_blk, :].T.astype(t_ref.dtype)
        if n_sums:
            @pl.when(pl.program_id(0) == 0)
            def _():
                for s_ref in sums:
                    s_ref[...] = jnp.zeros_like(s_ref)

            for s_ref, r in zip(sums, res[n_out:]):
                s_ref[...] += r

    tile = pl.BlockSpec((bm, bn), lambda i, j: (i, j))
    vec = pl.BlockSpec((1, bn), lambda i, j: (0, j))
    t_specs, t_shapes = [], []
    if t_blk is not None:
        t_specs = [pl.BlockSpec((bm // t_blk, bn, t_blk), lambda i, j: (i, j, 0))]
        t_shapes = [jax.ShapeDtypeStruct((m // t_blk, n, t_blk), out_dtypes[0])]
    a_specs = [pl.BlockSpec((bm, k), lambda i, j: (i, 0)) for k in ks]
    b_specs = [pl.BlockSpec((bn, k), lambda i, j: (j, 0)) if trans_b else pl.BlockSpec((k, bn), lambda i, j: (0, j)) for k in ks]
    return gridded(
        body, name=name, grid=(m // bm, n // bn),
        in_specs=a_specs + b_specs + [tile] * n_ex + [vec] * n_vec,
        out_specs=[tile] * n_out + t_specs + [vec] * n_sums,
        out_shape=[jax.ShapeDtypeStruct((m, n), dt) for dt in out_dtypes] + t_shapes + [jax.ShapeDtypeStruct((1, n), F32)] * n_sums,
        ins=[*a_list, *b_list, *extras, *vecs],
        semantics=("arbitrary", "arbitrary") if n_sums else ("parallel", "parallel"), side=side)


def mm_tn(a, b, *, a_pro=None, name, col_shards=1, bf16_copy=False, bk=1024, bn=1024, bt=2048):
    t, k = a.shape
    n = b.shape[1]
    ns = n // col_shards
    bk, bn, bt = _fit(bk, k), _fit(bn, ns), _fit(bt, t)
    per = ns // bn
    last = t // bt - 1

    def body(a_ref, b_ref, o_ref, *copy_ref):
        @pl.when(pl.program_id(2) == 0)
        def _():
            o_ref[...] = jnp.zeros_like(o_ref)

        a_tile = a_ref[...] if a_pro is None else a_pro(a_ref[...])
        o_ref[...] += lax.dot_general(a_tile, b_ref[...], TN, preferred_element_type=F32)
        if bf16_copy:
            @pl.when(pl.program_id(2) == last)
            def _():
                copy_ref[0][...] = o_ref[...].astype(BF16)

    if col_shards == 1:
        out_spec = pl.BlockSpec((bk, bn), lambda i, j, s: (i, j))
        shape = (k, n)
    else:
        out_spec = pl.BlockSpec((None, bk, bn), lambda i, j, s: (j // per, i, j % per))
        shape = (col_shards, k, ns)
    out_specs, out_shape = out_spec, jax.ShapeDtypeStruct(shape, F32)
    if bf16_copy:
        out_specs, out_shape = [out_spec, out_spec], [out_shape, jax.ShapeDtypeStruct(shape, BF16)]
    return pl.pallas_call(
        body, name=name, grid=(k // bk, n // bn, t // bt),
        in_specs=[pl.BlockSpec((bt, bk), lambda i, j, s: (s, i)), pl.BlockSpec((bt, bn), lambda i, j, s: (s, j))],
        out_specs=out_specs, out_shape=out_shape,
        compiler_params=_params(("parallel", "parallel", "arbitrary")),
    )(a, b)


def _split3(x):
    hi = x.astype(BF16)
    r1 = x - hi.astype(F32)
    mid = r1.astype(BF16)
    lo = (r1 - mid.astype(F32)).astype(BF16)
    return hi, mid, lo


def cumsum_rows(x, *, reverse, name, bc=512):
    t, d = x.shape
    bc = min(bc, t)
    nb = t // bc

    def body(x_ref, o_ref, carry):
        @pl.when(pl.program_id(0) == 0)
        def _():
            carry[...] = jnp.zeros_like(carry)

        r = lax.broadcasted_iota(jnp.int32, (bc, bc), 0)
        c = lax.broadcasted_iota(jnp.int32, (bc, bc), 1)
        tri = jnp.where((r <= c) if reverse else (r >= c), 1.0, 0.0).astype(BF16)
        hi, mid, lo = _split3(x_ref[...])
        s = (lax.dot_general(tri, hi, NN, preferred_element_type=F32)
             + lax.dot_general(tri, mid, NN, preferred_element_type=F32)
             + lax.dot_general(tri, lo, NN, preferred_element_type=F32)) + carry[0:1, :]
        o_ref[...] = s
        carry[0:1, :] = s[0:1, :] if reverse else s[bc - 1:bc, :]

    imap = (lambda i: (nb - 1 - i, 0)) if reverse else (lambda i: (i, 0))
    return pl.pallas_call(
        body, name=name, grid=(nb,),
        in_specs=[pl.BlockSpec((bc, d), imap)], out_specs=pl.BlockSpec((bc, d), imap),
        out_shape=jax.ShapeDtypeStruct((t, d), F32),
        scratch_shapes=[pltpu.VMEM((8, d), F32)],
        compiler_params=_params(("arbitrary",)),
    )(x)


def _rope(x, c, a, b):
    return x * c + pltpu.roll(x, LANES - ROPE_DIM // 2, 1) * a + pltpu.roll(x, ROPE_DIM // 2, 1) * b


def _rope_bwd(d, c, a, b):
    return d * c + pltpu.roll(d * a, ROPE_DIM // 2, 1) + pltpu.roll(d * b, LANES - ROPE_DIM // 2, 1)


S_CQ, S_CKV, S_KR, S_F, S_END = 0, Q_RANK, Q_RANK + KV_RANK, Q_RANK + KV_RANK + LANES, 1024
HW = N_HEADS * LANES
FW = N_HEADS * HEAD_DIM


def mla_prep(small, g_q, g_kv, w_uq, w_ukv, tab_c, tab_a, tab_b, b_f, *, name, bt=512):
    t = small.shape[0]
    bt = min(bt, t)

    def body(s_ref, gq_ref, gkv_ref, wq_ref, wkv_ref, c_ref, a_ref, b_ref, bf_ref,
             mq_ref, mk_ref, mv_ref, lf_ref, cqn_ref, ckvn_ref, mqt_ref, mvt_ref):
        cq = s_ref[:, S_CQ:S_CKV]
        rq = lax.rsqrt(jnp.mean(cq * cq, axis=-1, keepdims=True) + EPS)
        cqn = (cq * rq * gq_ref[...]).astype(BF16)
        ckv = s_ref[:, S_CKV:S_KR]
        rkv = lax.rsqrt(jnp.mean(ckv * ckv, axis=-1, keepdims=True) + EPS)
        ckvn = (ckv * rkv * gkv_ref[...]).astype(BF16)
        cqn_ref[...] = cqn
        ckvn_ref[...] = ckvn
        tc, ta, tb = c_ref[...], a_ref[...], b_ref[...]
        q = jnp.dot(cqn, wq_ref[...], preferred_element_type=F32)
        kv = jnp.dot(ckvn, wkv_ref[...], preferred_element_type=F32)
        kr = _rope(s_ref[:, S_KR:S_F], tc, ta, tb)
        for h in range(N_HEADS):
            sl = slice(h * LANES, (h + 1) * LANES)
            roped = _rope(q[:, sl], tc, ta, tb)
            mq_ref[:, sl] = roped.astype(BF16)
            mqt_ref[0, sl, :] = roped.T.astype(BF16)
            mk_ref[:, sl] = (kv[:, sl] + kr).astype(BF16)
        mv_ref[...] = kv[:, HW:].astype(BF16)
        mvt_ref[0] = kv[:, HW:].T.astype(BF16)
        z = s_ref[:, S_F:S_END - LANES] + bf_ref[...]
        lf_ref[...] = jnp.minimum(z, 0.0) - jnp.log(1.0 + jnp.exp(-jnp.abs(z)))

    def row(w):
        return pl.BlockSpec((bt, w), lambda i: (i, 0))

    def full(arr):
        return pl.BlockSpec(arr.shape, lambda i: (0, 0))

    return pl.pallas_call(
        body, name=name, grid=(t // bt,),
        in_specs=[row(S_END), full(g_q), full(g_kv), full(w_uq), full(w_ukv), row(LANES), row(LANES), row(LANES), full(b_f)],
        out_specs=[row(HW), row(HW), row(FW), row(LANES), row(Q_RANK), row(KV_RANK),
                   pl.BlockSpec((1, HW, bt), lambda i: (i, 0, 0)), pl.BlockSpec((1, FW, bt), lambda i: (i, 0, 0))],
        out_shape=[jax.ShapeDtypeStruct((t, HW), BF16)] * 2 + [jax.ShapeDtypeStruct((t, FW), BF16), jax.ShapeDtypeStruct((t, LANES), F32),
                   jax.ShapeDtypeStruct((t, Q_RANK), BF16), jax.ShapeDtypeStruct((t, KV_RANK), BF16),
                   jax.ShapeDtypeStruct((t // bt, HW, bt), BF16), jax.ShapeDtypeStruct((t // bt, FW, bt), BF16)],
        compiler_params=_params(("parallel",)),
    )(small, g_q, g_kv, w_uq, w_ukv, tab_c, tab_a, tab_b, b_f)


def mla_prep_bwd(dmq, dmk, dmv, dlf, small, g_q, g_kv, w_uq, w_ukv, tab_c, tab_a, tab_b, b_f, *, name, bt=512):
    t = small.shape[0]
    bt = min(bt, t)

    def body(dmq_ref, dmk_ref, dmv_ref, dlf_ref, s_ref, gq_ref, gkv_ref, wq_ref, wkv_ref, c_ref, a_ref, b_ref, bf_ref,
             ds_ref, dq_ref, dkv_ref, dgq_ref, dgkv_ref, db_ref):
        tc, ta, tb = c_ref[...], a_ref[...], b_ref[...]
        lane = lax.broadcasted_iota(jnp.int32, (1, LANES), 1)
        dkr = jnp.zeros((bt, LANES), F32)
        for h in range(N_HEADS):
            sl = slice(h * LANES, (h + 1) * LANES)
            dq_ref[:, sl] = _rope_bwd(dmq_ref[:, sl], tc, ta, tb).astype(BF16)
            dkr = dkr + dmk_ref[:, sl]
        dkv_ref[:, :HW] = dmk_ref[...].astype(BF16)
        dkv_ref[:, HW:] = dmv_ref[...].astype(BF16)
        in_rope = (lane >= HEAD_DIM) & (lane < HEAD_DIM + ROPE_DIM)
        ds_ref[:, S_KR:S_F] = jnp.where(in_rope, _rope_bwd(dkr, tc, ta, tb), 0.0).astype(BF16)

        def norm_bwd(raw, g_ref, dn, dg_ref):
            r = lax.rsqrt(jnp.mean(raw * raw, axis=-1, keepdims=True) + EPS)
            u = dn * g_ref[...]
            dot = jnp.mean(u * raw, axis=-1, keepdims=True)
            dg_ref[...] += jnp.sum(dn * (raw * r), axis=0, keepdims=True)
            return r * u - raw * (r * r * r * dot)

        @pl.when(pl.program_id(0) == 0)
        def _():
            dgq_ref[...] = jnp.zeros_like(dgq_ref)
            dgkv_ref[...] = jnp.zeros_like(dgkv_ref)
            db_ref[...] = jnp.zeros_like(db_ref)

        dcqn = lax.dot_general(dq_ref[...], wq_ref[...], NT, preferred_element_type=F32)
        ds_ref[:, S_CQ:S_CKV] = norm_bwd(s_ref[:, S_CQ:S_CKV], gq_ref, dcqn, dgq_ref).astype(BF16)
        dckvn = lax.dot_general(dkv_ref[...], wkv_ref[...], NT, preferred_element_type=F32)
        ds_ref[:, S_CKV:S_KR] = norm_bwd(s_ref[:, S_CKV:S_KR], gkv_ref, dckvn, dgkv_ref).astype(BF16)
        z = s_ref[:, S_F:S_END - LANES] + bf_ref[...]
        dz = jnp.where(lane < N_HEADS, dlf_ref[...] / (1.0 + jnp.exp(z)), 0.0)
        db_ref[...] += jnp.sum(dz, axis=0, keepdims=True)
        ds_ref[:, S_F:S_END - LANES] = dz.astype(BF16)
        ds_ref[:, S_END - LANES:] = jnp.zeros((bt, LANES), BF16)

    def row(w):
        return pl.BlockSpec((bt, w), lambda i: (i, 0))

    def full(arr):
        return pl.BlockSpec(arr.shape, lambda i: (0, 0))

    def vec(w):
        return pl.BlockSpec((1, w), lambda i: (0, 0))

    return pl.pallas_call(
        body, name=name, grid=(t // bt,),
        in_specs=[row(HW), row(HW), row(FW), row(LANES), row(S_END), full(g_q), full(g_kv), full(w_uq), full(w_ukv),
                  row(LANES), row(LANES), row(LANES), full(b_f)],
        out_specs=[row(S_END), row(HW), row(HW + FW), vec(Q_RANK), vec(KV_RANK), vec(LANES)],
        out_shape=[jax.ShapeDtypeStruct((t, S_END), BF16), jax.ShapeDtypeStruct((t, HW), BF16),
                   jax.ShapeDtypeStruct((t, HW + FW), BF16), jax.ShapeDtypeStruct((1, Q_RANK), F32),
                   jax.ShapeDtypeStruct((1, KV_RANK), F32), jax.ShapeDtypeStruct((1, LANES), F32)],
        compiler_params=_params(("arbitrary",)),
    )(dmq, dmk, dmv, dlf, small, g_q, g_kv, w_uq, w_ukv, tab_c, tab_a, tab_b, b_f)


class Side:
    def __init__(self, ins, out_shapes, n_sems, first, last, mid=None):
        self.ins, self.out_shapes, self.n_sems = list(ins), list(out_shapes), n_sems
        self.first, self.mid, self.last = first, mid, last

    def specs(self):
        return [ANY] * len(self.ins), [ANY] * len(self.out_shapes)

    def sems(self):
        return [pltpu.SemaphoreType.DMA((self.n_sems,)), pltpu.SemaphoreType.DMA((self.n_sems,))]


def _lane():
    return lax.broadcasted_iota(jnp.int32, (1, LANES), 1)


def _halves(x):
    zero = jnp.zeros_like(x)
    return [jnp.where(_lane() < HEAD_DIM, x, zero), jnp.where(_lane() >= HEAD_DIM, x, zero)]


def _groups(x):
    return [x[:, :LANES], x[:, LANES:]]


def _pick_row(tile, h):
    row = lax.broadcasted_iota(jnp.int32, (tile.shape[0], 1), 0)
    return jnp.sum(jnp.where(row == h, tile, 0.0), axis=0, keepdims=True)


def _pick_lane(tile, h):
    return jnp.sum(jnp.where(_lane() == h, tile, 0.0), axis=1, keepdims=True)


def _row_halves(x):
    row = lax.broadcasted_iota(jnp.int32, (LANES, 1), 0)
    zero = jnp.zeros_like(x)
    return [jnp.where(row < HEAD_DIM, x, zero), jnp.where(row >= HEAD_DIM, x, zero)]


def _below_diagonal(s):
    r = lax.broadcasted_iota(jnp.int32, s.shape, 0)
    c = lax.broadcasted_iota(jnp.int32, s.shape, 1)
    return jnp.where(c <= r, s, -jnp.inf)


def _above_diagonal(s):
    r = lax.broadcasted_iota(jnp.int32, s.shape, 0)
    c = lax.broadcasted_iota(jnp.int32, s.shape, 1)
    return jnp.where(r <= c, s, -jnp.inf)


def _split_refs(refs, counts):
    out, at = [], 0
    for n in counts:
        out.append(refs[at:at + n])
        at += n
    return out


def flash_fwd(qt_arr, k_arr, vt_arr, f_cum, *, qoff, koff, voff, pair, scale, name, blk=512, side=None):
    t = k_arr.shape[0]
    tblk = qt_arr.shape[2]
    blk = max(min(blk, t), tblk)
    sub = blk // tblk
    nb = t // blk
    steps = PAIRS * nb
    w = LANES if pair else 2 * LANES
    has_bias = f_cum is not None
    ins = [qt_arr, k_arr, vt_arr] + ([f_cum] if has_bias else [])

    def wide(ref, first):
        parts = [ref[first + u] for u in range(sub)]
        return parts[0] if sub == 1 else jnp.concatenate(parts, axis=1)
    s_ins, s_outs = (side.ins, side.out_shapes) if side else ([], [])

    def body(*refs):
        main, si, outs, so, sems = _split_refs(refs, [len(ins), len(s_ins), 2, len(s_outs), 2 if side else 0])
        qt_ref, k_ref, vt_ref = main[:3]
        f_ref = main[3] if has_bias else None
        o_ref, st_ref = outs
        g, i = pl.program_id(0), pl.program_id(1)
        step_id = g * nb + i
        if side:
            @pl.when(step_id == 0)
            def _():
                side.first(si, so, *sems)

            if side.mid is not None:
                @pl.when(step_id == (3 * steps) // 4)
                def _():
                    side.mid(si, so, *sems)

        qt = (wide(qt_ref, 0).astype(F32) * (scale * LOG2E)).astype(BF16)
        qts = _row_halves(qt) if pair else [qt[:LANES], qt[LANES:]]

        def k_of(kk, n):
            return kk if pair else kk[:, n * LANES:(n + 1) * LANES]

        def with_ones(vt_rows):
            return jnp.concatenate([vt_rows, jnp.ones((ACC_ROWS - HEAD_DIM, vt_rows.shape[1]), BF16)], axis=0)

        def step(j, carry, diagonal):
            rows = pl.ds(pl.multiple_of(j * blk, blk), blk)
            kk = k_ref[rows, :]
            vt = wide(vt_ref, sub * j)
            out = []
            for n in range(2):
                m, acc = carry[n]
                s = jnp.dot(k_of(kk, n), qts[n], preferred_element_type=F32)
                if has_bias:
                    s = s - LOG2E * _pick_lane(f_ref[rows, :], 2 * g + n)
                if diagonal:
                    s = _above_diagonal(s)
                m_new = jnp.maximum(m, jnp.max(s, axis=0, keepdims=True))
                p = jnp.exp2(s - m_new).astype(BF16)
                out.append((m_new, jnp.exp2(m - m_new) * acc
                            + jnp.dot(with_ones(vt[n * HEAD_DIM:(n + 1) * HEAD_DIM]), p, preferred_element_type=F32)))
            return tuple(out)

        def diagonal_in_halves(carry):
            h = tblk
            halves = [pl.ds(pl.multiple_of(i * blk, blk), h), pl.ds(pl.multiple_of(i * blk + h, h), h)]
            k_top, k_bot = k_ref[halves[0], :], k_ref[halves[1], :]
            vt_top, vt_bot = vt_ref[sub * i], vt_ref[sub * i + 1]
            out = []
            for n in range(2):
                m, acc = carry[n]
                heads = slice(n * HEAD_DIM, (n + 1) * HEAD_DIM)
                s_top = jnp.dot(k_of(k_top, n), qts[n], preferred_element_type=F32)
                s_bot = jnp.dot(k_of(k_bot, n), qts[n][:, h:], preferred_element_type=F32)
                if has_bias:
                    s_top = s_top - LOG2E * _pick_lane(f_ref[halves[0], :], 2 * g + n)
                    s_bot = s_bot - LOG2E * _pick_lane(f_ref[halves[1], :], 2 * g + n)
                s_top, s_bot = _above_diagonal(s_top), _above_diagonal(s_bot)
                m_top = jnp.maximum(m, jnp.max(s_top, axis=0, keepdims=True))
                m_new = jnp.concatenate([m_top[:, :h], jnp.maximum(m_top[:, h:], jnp.max(s_bot, axis=0, keepdims=True))], axis=1)
                p_top = jnp.exp2(s_top - m_new).astype(BF16)
                p_bot = jnp.exp2(s_bot - m_new[:, h:]).astype(BF16)
                late = jnp.concatenate([jnp.zeros((ACC_ROWS, h), F32),
                                        jnp.dot(with_ones(vt_bot[heads]), p_bot, preferred_element_type=F32)], axis=1)
                out.append((m_new, jnp.exp2(m - m_new) * acc
                            + jnp.dot(with_ones(vt_top[heads]), p_top, preferred_element_type=F32) + late))
            return tuple(out)

        init = tuple((jnp.full((1, blk), -jnp.inf, F32), jnp.zeros((ACC_ROWS, blk), F32)) for _ in range(2))
        carry = lax.fori_loop(0, i, lambda j, c: step(j, c, False), init)
        (ma, acca), (mb, accb) = diagonal_in_halves(carry) if sub == 2 else step(i, carry, True)
        la, lb = acca[HEAD_DIM:HEAD_DIM + 1], accb[HEAD_DIM:HEAD_DIM + 1]
        o_ref[...] = jnp.concatenate([acca[:HEAD_DIM] / la, accb[:HEAD_DIM] / lb], axis=0).T
        row = lax.broadcasted_iota(jnp.int32, (LANES, 1), 0)
        st_ref[0] = jnp.where(row == 0, ma + jnp.log2(la), jnp.where(row == 1, mb + jnp.log2(lb), 0.0)).T
        if side:
            @pl.when(step_id == steps - 1)
            def _():
                side.last(si, so, *sems)

    in_specs = [pl.BlockSpec((sub, w, tblk), lambda g, i: (i, qoff + g, 0)), pl.BlockSpec((t, w), lambda g, i: (0, koff + g)),
                pl.BlockSpec((t // tblk, LANES, tblk), lambda g, i: (0, voff + g, 0))]
    if has_bias:
        in_specs.append(pl.BlockSpec((t, LANES), lambda g, i: (0, 0)))
    s_in_specs, s_out_specs = side.specs() if side else ([], [])
    return pl.pallas_call(
        body, name=name, grid=(PAIRS, nb), in_specs=in_specs + s_in_specs,
        out_specs=[pl.BlockSpec((blk, LANES), lambda g, i: (i, g)), pl.BlockSpec((1, blk, LANES), lambda g, i: (g, i, 0))]
        + s_out_specs,
        out_shape=[jax.ShapeDtypeStruct((t, PAIRS * LANES), F32), jax.ShapeDtypeStruct((PAIRS, t, LANES), F32)] + list(s_outs),
        scratch_shapes=side.sems() if side else [],
        compiler_params=_params(("arbitrary", "arbitrary")),
    )(*ins, *s_ins)


def mix_norm(fo, mo, g_fo, g_mo, *, name, bt=512):
    t, d = fo.shape
    bt = min(bt, t)

    def body(fo_ref, mo_ref, gf_ref, gm_ref, o_ref):
        for n, (x_ref, g_ref) in enumerate(((fo_ref, gf_ref), (mo_ref, gm_ref))):
            xv = x_ref[...]
            r = lax.rsqrt(jnp.mean(xv * xv, axis=-1, keepdims=True) + EPS)
            o_ref[:, n * d:(n + 1) * d] = (xv * r * g_ref[...]).astype(BF16)

    row = pl.BlockSpec((bt, d), lambda i: (i, 0))
    vec = pl.BlockSpec((1, d), lambda i: (0, 0))
    return pl.pallas_call(
        body, name=name, grid=(t // bt,), in_specs=[row, row, vec, vec],
        out_specs=pl.BlockSpec((bt, 2 * d), lambda i: (i, 0)),
        out_shape=jax.ShapeDtypeStruct((t, 2 * d), BF16),
        compiler_params=_params(("parallel",)),
    )(fo, mo, g_fo, g_mo)


def mix_norm_bwd(dx1b, w_o, fo, mo, g_fo, g_mo, st_f, st_m, *, name, bt=512, side=None):
    t, d = fo.shape
    bt = min(bt, t)

    def body(dx_in_ref, w_ref, fo_ref, mo_ref, gf_ref, gm_ref, sf_ref, sm_ref, dfo_ref, dmo_ref, dgf_ref, dgm_ref,
             sfo_ref, smo_ref, dfot_ref, dmot_ref):
        @pl.when(pl.program_id(0) == 0)
        def _():
            dgf_ref[...] = jnp.zeros_like(dgf_ref)
            dgm_ref[...] = jnp.zeros_like(dgm_ref)

        dmixed = lax.dot_general(dx_in_ref[...], w_ref[...], NT, preferred_element_type=F32)
        groups = ((fo_ref, gf_ref, dfo_ref, dgf_ref, sf_ref, sfo_ref, dfot_ref),
                  (mo_ref, gm_ref, dmo_ref, dgm_ref, sm_ref, smo_ref, dmot_ref))
        for n, (x_ref, g_ref, dx_ref, dg_ref, st_ref, sto_ref, dxt_ref) in enumerate(groups):
            xv = x_ref[...]
            dhv = dmixed[:, n * d:(n + 1) * d]
            r = lax.rsqrt(jnp.mean(xv * xv, axis=-1, keepdims=True) + EPS)
            u = dhv * g_ref[...]
            dxf = r * u - xv * (r * r * r * jnp.mean(u * xv, axis=-1, keepdims=True))
            dxb = dxf.astype(BF16)
            dx_ref[...] = dxb
            dxt_ref[0] = dxf.T.astype(BF16)
            dg_ref[...] += jnp.sum(dhv * (xv * r), axis=0, keepdims=True)
            prod = xv * dxb.astype(F32)
            for g in range(PAIRS):
                grp = prod[:, g * LANES:(g + 1) * LANES]
                da = jnp.sum(jnp.where(_lane() < HEAD_DIM, grp, 0.0), axis=1, keepdims=True)
                db = jnp.sum(jnp.where(_lane() >= HEAD_DIM, grp, 0.0), axis=1, keepdims=True)
                sto_ref[g] = jnp.where(_lane() == 2, da, jnp.where(_lane() == 3, db, st_ref[g]))

    row = pl.BlockSpec((bt, d), lambda i: (i, 0))
    vec = pl.BlockSpec((1, d), lambda i: (0, 0))
    stat = pl.BlockSpec((PAIRS, bt, LANES), lambda i: (0, i, 0))
    return gridded(
        body, name=name, grid=(t // bt,),
        in_specs=[pl.BlockSpec((bt, dx1b.shape[1]), lambda i: (i, 0)), pl.BlockSpec(w_o.shape, lambda i: (0, 0)),
                  row, row, vec, vec, stat, stat],
        out_specs=[row, row, vec, vec, stat, stat] + [pl.BlockSpec((1, d, bt), lambda i: (i, 0, 0))] * 2,
        out_shape=[jax.ShapeDtypeStruct((t, d), BF16)] * 2 + [jax.ShapeDtypeStruct((1, d), F32)] * 2
        + [jax.ShapeDtypeStruct(st_f.shape, F32)] * 2 + [jax.ShapeDtypeStruct((t // bt, d, bt), BF16)] * 2,
        ins=[dx1b, w_o, fo, mo, g_fo, g_mo, st_f, st_m], semantics=("arbitrary",), side=side)


def flash_bwd(q_arr, qt_arr, k_arr, v_arr, do_arr, dot_arr, st, f_blocks, *, qoff, koff, voff, pair, scale, name, qblk=1024,
              side=None):
    t = q_arr.shape[0]
    blk = qt_arr.shape[2]
    qblk = max(min(qblk, t), blk)
    sub = qblk // blk
    nb, nbq = t // blk, t // qblk
    w = LANES if pair else 2 * LANES
    hw = w // 2
    has_bias = f_blocks is not None
    split = _halves if pair else _groups
    ins = [q_arr, qt_arr, k_arr, v_arr, do_arr, dot_arr, st] + ([f_blocks] if has_bias else [])
    n_out = 4 if has_bias else 3
    s_ins, s_outs = (side.ins, side.out_shapes) if side else ([], [])

    def wide(ref, first, count):
        parts = [ref[first + u] for u in range(count)]
        return parts[0] if count == 1 else jnp.concatenate(parts, axis=1)

    def body(*refs):
        main, si, outs, so, sems = _split_refs(refs, [len(ins), len(s_ins), n_out, len(s_outs), 2 if side else 0])
        q_ref, qt_ref, k_ref, v_ref, do_ref, dot_ref, st_ref = main[:7]
        dq_ref, dk_ref, dv_ref = outs[:3]
        g, j = pl.program_id(0), pl.program_id(1)
        step_id = g * nb + j
        if side:
            @pl.when(step_id == 0)
            def _():
                side.first(si, so, *sems)

        @pl.when(j == 0)
        def _():
            dq_ref[...] = jnp.zeros_like(dq_ref)

        kk, vv = k_ref[...], v_ref[...]
        ks = [kk, kk] if pair else _groups(kk)
        if has_bias:
            f_ref, df_ref = main[7], outs[3]
            fk = [LOG2E * _pick_row(f_ref[0], 2 * g + n) for n in range(2)]

            @pl.when(step_id == 0)
            def _():
                df_ref[...] = jnp.zeros_like(df_ref)

        def step(tb, count, carry, diagonal):
            rows = pl.ds(pl.multiple_of(tb * blk, blk), count * blk)
            qs = split((q_ref[rows, :].astype(F32) * (scale * LOG2E)).astype(BF16))
            qt = (wide(qt_ref, tb, count).astype(F32) * (scale * LOG2E)).astype(BF16)
            dos = [h.astype(BF16) for h in _halves(do_ref[rows, :].astype(F32))]
            dot = wide(dot_ref, tb, count)
            stats = st_ref[0, rows, :]
            new, dqs, row_sums = [], [], []
            for n in range(2):
                dkt, dvt, dfk = carry[n]
                s = lax.dot_general(qs[n], ks[n], NT, preferred_element_type=F32)
                if has_bias:
                    s = s - fk[n]
                if diagonal:
                    s = _below_diagonal(s)
                p = jnp.exp2(s - stats[:, n:n + 1])
                dp = lax.dot_general(dos[n], vv, NT, preferred_element_type=F32)
                ds = p * (dp - stats[:, 2 + n:3 + n])
                dsb = ds.astype(BF16)
                dvt = dvt + jnp.dot(dot[n * HEAD_DIM:(n + 1) * HEAD_DIM], p.astype(BF16), preferred_element_type=F32)
                dkt = dkt + jnp.dot(qt[n * hw:(n + 1) * hw], dsb, preferred_element_type=F32)
                dqs.append(jnp.dot(dsb, ks[n], preferred_element_type=F32))
                if has_bias:
                    dfk = dfk - jnp.sum(ds, axis=0, keepdims=True)
                    row_sums.append(jnp.sum(ds, axis=1, keepdims=True))
                new.append((dkt, dvt, dfk))
            dq = (jnp.where(_lane() < HEAD_DIM, dqs[0], dqs[1]) if pair else jnp.concatenate(dqs, axis=1)) * scale
            if has_bias:
                df_ref[rows, :] += jnp.where(_lane() == 2 * g, row_sums[0], jnp.where(_lane() == 2 * g + 1, row_sums[1], 0.0))
            dq_ref[rows, :] += dq
            return tuple(new)

        init = tuple((jnp.zeros((hw, blk), F32), jnp.zeros((HEAD_DIM, blk), F32), jnp.zeros((1, blk), F32)) for _ in range(2))
        carry = step(j, 1, init, True)
        whole = j // sub + 1
        carry = lax.fori_loop(j + 1, whole * sub, lambda tb, c: step(tb, 1, c, False), carry)
        (dka, dva, dfa), (dkb, dvb, dfb) = lax.fori_loop(whole, nbq, lambda i, c: step(i * sub, sub, c, False), carry)
        dk_ref[...] = (jnp.concatenate([dka, dkb], axis=0).T * LN2).astype(BF16)
        dv_ref[...] = jnp.concatenate([dva, dvb], axis=0).T.astype(BF16)
        if has_bias:
            row = lax.broadcasted_iota(jnp.int32, (LANES, 1), 0)
            by_head = jnp.where(row == 2 * g, dfa, jnp.where(row == 2 * g + 1, dfb, 0.0))
            df_ref[pl.ds(pl.multiple_of(j * blk, blk), blk), :] += by_head.T
        if side:
            @pl.when(step_id == PAIRS * nb - 1)
            def _():
                side.last(si, so, *sems)

    in_specs = [pl.BlockSpec((t, w), lambda g, j: (0, qoff + g)), pl.BlockSpec((nb, w, blk), lambda g, j: (0, qoff + g, 0)),
                pl.BlockSpec((blk, w), lambda g, j: (j, koff + g)), pl.BlockSpec((blk, LANES), lambda g, j: (j, voff + g)),
                pl.BlockSpec((t, LANES), lambda g, j: (0, g)), pl.BlockSpec((nb, LANES, blk), lambda g, j: (0, g, 0)),
                pl.BlockSpec((1, t, LANES), lambda g, j: (g, 0, 0))]
    out_specs = [pl.BlockSpec((t, w), lambda g, j: (0, g)), pl.BlockSpec((blk, w), lambda g, j: (j, g)),
                 pl.BlockSpec((blk, LANES), lambda g, j: (j, g))]
    out_shape = [jax.ShapeDtypeStruct((t, PAIRS * w), F32), jax.ShapeDtypeStruct((t, PAIRS * w), BF16),
                 jax.ShapeDtypeStruct((t, PAIRS * LANES), BF16)]
    if has_bias:
        in_specs.append(pl.BlockSpec((1, N_HEADS, blk), lambda g, j: (j, 0, 0)))
        out_specs.append(pl.BlockSpec((t, LANES), lambda g, j: (0, 0)))
        out_shape.append(jax.ShapeDtypeStruct((t, LANES), F32))
    s_in_specs, s_out_specs = side.specs() if side else ([], [])
    return pl.pallas_call(
        body, name=name, grid=(PAIRS, nb), in_specs=in_specs + s_in_specs, out_specs=out_specs + s_out_specs,
        out_shape=out_shape + list(s_outs), scratch_shapes=side.sems() if side else [],
        compiler_params=_params(("arbitrary", "arbitrary")),
    )(*ins, *s_ins)


def _adamw_math(w, g, m, v):
    nm = ADAM_B1 * m + (1.0 - ADAM_B1) * g
    nv = ADAM_B2 * v + (1.0 - ADAM_B2) * (g * g)
    m_hat = nm / (1.0 - ADAM_B1 ** ADAM_STEP)
    v_hat = nv / (1.0 - ADAM_B2 ** ADAM_STEP)
    return -ADAM_LR * (m_hat / (jnp.sqrt(v_hat) + ADAM_EPS) + ADAM_WD * w), nm, nv


def adamw(w, g, m, v, *, name):
    rws, cols = w.shape
    br = _row_block(rws)

    def body(w_ref, g_ref, m_ref, v_ref, d_ref, nm_ref, nv_ref):
        d_ref[...], nm_ref[...], nv_ref[...] = _adamw_math(w_ref[...], g_ref[...], m_ref[...], v_ref[...])

    blk = pl.BlockSpec((br, cols), lambda i: (i, 0))
    return pl.pallas_call(
        body, name=name, grid=(rws // br,), in_specs=[blk] * 4, out_specs=[blk] * 3,
        out_shape=[jax.ShapeDtypeStruct((rws, cols), F32)] * 3,
        compiler_params=_params(("parallel",)),
    )(w, g, m, v)


def adamw_halves(w, g_mine, g_other, m, v, core, *, name):
    _, k, n = w.shape
    br = _row_block(k // 2)
    nh = k // 2 // br

    def body(c_ref, w_ref, gm_ref, go_ref, m_ref, v_ref, g_out, d_ref, nm_ref, nv_ref):
        gv = jnp.where(pl.program_id(0) == c_ref[0], gm_ref[...], go_ref[...])
        g_out[0] = gv
        d_ref[0], nm_ref[0], nv_ref[0] = _adamw_math(w_ref[0], gv, m_ref[0], v_ref[0])

    full = pl.BlockSpec((1, br, n), lambda hb, i, c: (0, hb * nh + i, 0))
    half = pl.BlockSpec((br, n), lambda hb, i, c: (i, 0))
    return pl.pallas_call(
        body, name=name,
        grid_spec=pltpu.PrefetchScalarGridSpec(num_scalar_prefetch=1, grid=(2, nh), in_specs=[full, half, half, full, full],
                                               out_specs=[full] * 4),
        out_shape=[jax.ShapeDtypeStruct(w.shape, F32)] * 4,
        compiler_params=_params(("parallel", "parallel")),
    )(core, w, g_mine, g_other, m, v)


def adamw_halves_t(wt, gt_mine, gt_other, mt, vt, core, *, name, bc=128):
    n, k = wt.shape
    nh = k // 2 // bc

    def body(c_ref, w_ref, gm_ref, go_ref, m_ref, v_ref, g_out, d_ref, nm_ref, nv_ref):
        gv = jnp.where(pl.program_id(0) == c_ref[0], gm_ref[...], go_ref[...])
        g_out[...] = gv
        d_ref[...], nm_ref[...], nv_ref[...] = _adamw_math(w_ref[...], gv, m_ref[...], v_ref[...])

    full = pl.BlockSpec((n, bc), lambda hb, i, c: (0, hb * nh + i))
    half = pl.BlockSpec((n, bc), lambda hb, i, c: (0, i))
    return pl.pallas_call(
        body, name=name,
        grid_spec=pltpu.PrefetchScalarGridSpec(num_scalar_prefetch=1, grid=(2, nh), in_specs=[full, half, half, full, full],
                                               out_specs=[full] * 4),
        out_shape=[jax.ShapeDtypeStruct(wt.shape, F32)] * 4,
        compiler_params=_params(("parallel", "parallel")),
    )(core, wt, gt_mine, gt_other, mt, vt)


def add_pair(dw, recv, core, *, name):
    n4, k, n = dw.shape
    half = (1, k // 2, n) if split_axis(k) == 0 else (1, k, n // 2)
    mine = (lambda q, c: (q, c[0], 0)) if split_axis(k) == 0 else (lambda q, c: (q, 0, c[0]))

    def body(c_ref, a_ref, b_ref, o_ref):
        o_ref[...] = (a_ref[...] + b_ref[...].astype(F32)).astype(BF16)

    return pl.pallas_call(
        body, name=name,
        grid_spec=pltpu.PrefetchScalarGridSpec(
            num_scalar_prefetch=1, grid=(n4,),
            in_specs=[pl.BlockSpec(half, mine), pl.BlockSpec(half, lambda q, c: (q, 0, 0))],
            out_specs=pl.BlockSpec(half, lambda q, c: (q, 0, 0))),
        out_shape=jax.ShapeDtypeStruct((n4,) + half[1:], BF16),
        compiler_params=_params(("parallel",)),
    )(core, dw, recv)


def sum_chips(parts, *, name):
    n4, r, n = parts.shape
    if r % 16 == 0:
        br, bc = _row_block(r), n
    else:
        br, bc = r, LANES

    def body(p_ref, o_ref):
        acc = p_ref[0].astype(F32)
        for q in range(1, n4):
            acc = acc + p_ref[q].astype(F32)
        o_ref[...] = acc

    return pl.pallas_call(
        body, name=name, grid=(r // br, n // bc),
        in_specs=[pl.BlockSpec((n4, br, bc), lambda i, j: (0, i, j))], out_specs=pl.BlockSpec((br, bc), lambda i, j: (i, j)),
        out_shape=jax.ShapeDtypeStruct((r, n), F32),
        compiler_params=_params(("parallel", "parallel")),
    )(parts)


ANY = pl.BlockSpec(memory_space=pl.ANY)


def _place():
    x, y, c = lax.axis_index("x"), lax.axis_index("y"), lax.axis_index("c")
    chips = [(1 - x, y), (x, 1 - y), (1 - x, 1 - y)]
    return x, y, c, chips


def _copy(src, dst, send_sems, recv_sems, k, to):
    return pltpu.make_async_remote_copy(src_ref=src, dst_ref=dst, send_sem=send_sems.at[k], recv_sem=recv_sems.at[k],
                                        device_id=to, device_id_type=MESH)


def split_axis(rows):
    return 0 if rows % 32 == 0 else 1


def _half(ref, lead, hf):
    rows, cols = ref.shape[-2:]
    if split_axis(rows) == 0:
        at = (pl.ds(hf * (rows // 2), rows // 2), slice(None))
    else:
        at = (slice(None), pl.ds(hf * (cols // 2), cols // 2))
    return ref.at[at] if lead is None else ref.at[(lead,) + at]


def _gather_first(srcs, dsts, ssems, rsems):
    x, y, c, chips = _place()
    for ti, (s, d) in enumerate(zip(srcs, dsts)):
        for j, (cx, cy) in enumerate(chips):
            _copy(_half(s, None, c), _half(d, 2 * x + y, c), ssems, rsems, 3 * ti + j, (cx, cy, c)).start()


def _gather_mid(srcs, dsts, ssems, rsems):
    x, y, c, chips = _place()
    n1 = 3 * len(srcs)
    for ti, d in enumerate(dsts):
        for j, (cx, cy) in enumerate(chips):
            landed = _half(d, 2 * cx + cy, c)
            _copy(landed, landed, ssems, rsems, 3 * ti + j, (cx, cy, c)).wait_recv()
            _copy(landed, landed, ssems, rsems, n1 + 3 * ti + j, (x, y, 1 - c)).start()


def _gather_last(srcs, dsts, ssems, rsems):
    x, y, c, chips = _place()
    n1 = 3 * len(srcs)
    for ti, (s, d) in enumerate(zip(srcs, dsts)):
        for j, (cx, cy) in enumerate(chips):
            other = _half(d, 2 * cx + cy, 1 - c)
            _copy(other, other, ssems, rsems, n1 + 3 * ti + j, (x, y, 1 - c)).wait_recv()
        for j, (cx, cy) in enumerate(chips):
            mine = _half(s, None, c)
            _copy(mine, mine, ssems, rsems, 3 * ti + j, (cx, cy, c)).wait_send()
            _copy(mine, mine, ssems, rsems, n1 + 3 * ti + j, (x, y, 1 - c)).wait_send()


def gather_side(shards):
    return Side(shards, [jax.ShapeDtypeStruct((N_CHIPS,) + s.shape, s.dtype) for s in shards], 6 * len(shards),
                _gather_first, _gather_last, _gather_mid)


def _scatter_first(srcs, dsts, ssems, rsems):
    x, y, c, chips = _place()
    for ti, (s, d) in enumerate(zip(srcs, dsts)):
        for j, (cx, cy) in enumerate(chips):
            _copy(s.at[2 * cx + cy], d.at[2 * x + y], ssems, rsems, 3 * ti + j, (cx, cy, c)).start()


def _scatter_last(srcs, dsts, ssems, rsems):
    x, y, c, chips = _place()
    for ti, (s, d) in enumerate(zip(srcs, dsts)):
        for j, (cx, cy) in enumerate(chips):
            _copy(s.at[2 * cx + cy], d.at[2 * cx + cy], ssems, rsems, 3 * ti + j, (cx, cy, c)).wait_recv()
        for j, (cx, cy) in enumerate(chips):
            _copy(s.at[2 * cx + cy], d.at[2 * cx + cy], ssems, rsems, 3 * ti + j, (cx, cy, c)).wait_send()


def scatter_side(parts):
    return Side(parts, [jax.ShapeDtypeStruct(p.shape, p.dtype) for p in parts], 3 * len(parts), _scatter_first, _scatter_last)


def run_side(side, *, name):
    n_in, n_out = len(side.ins), len(side.out_shapes)

    def body(*refs):
        si, so, sems = _split_refs(refs, [n_in, n_out, 2])
        side.first(si, so, *sems)
        if side.mid is not None:
            side.mid(si, so, *sems)
        side.last(si, so, *sems)

    in_specs, out_specs = side.specs()
    return pl.pallas_call(body, name=name, in_specs=in_specs, out_specs=out_specs, out_shape=side.out_shapes,
                          scratch_shapes=side.sems())(*side.ins)


def _swap_first(srcs, dsts, ssems, rsems):
    x, y, c, _ = _place()
    for k, (s, d) in enumerate(zip(srcs, dsts)):
        _copy(s, d, ssems, rsems, k, (x, y, 1 - c)).start()


def _swap_last(srcs, dsts, ssems, rsems):
    x, y, c, _ = _place()
    for k, (s, d) in enumerate(zip(srcs, dsts)):
        _copy(s, d, ssems, rsems, k, (x, y, 1 - c)).wait()


def swap_side(xs):
    return Side(xs, [jax.ShapeDtypeStruct(a.shape, a.dtype) for a in xs], len(xs), _swap_first, _swap_last)


def _swap_halves(srcs, dsts, ssems, rsems):
    x, y, c, _ = _place()
    for k, (s, d) in enumerate(zip(srcs, dsts)):
        hk = s.shape[1] // 2
        yield _copy(s.at[:, pl.ds((1 - c) * hk, hk), :], d, ssems, rsems, k, (x, y, 1 - c))


def _swap_halves_first(srcs, dsts, ssems, rsems):
    for cp in _swap_halves(srcs, dsts, ssems, rsems):
        cp.start()


def _swap_halves_last(srcs, dsts, ssems, rsems):
    for cp in _swap_halves(srcs, dsts, ssems, rsems):
        cp.wait()


def swap_halves_side(xs):
    return Side(xs, [jax.ShapeDtypeStruct((a.shape[0], a.shape[1] // 2, a.shape[2]), a.dtype) for a in xs], len(xs),
                _swap_halves_first, _swap_halves_last)


def allreduce_small(s):
    n_dev = 8

    def body(s_ref, out_ref, buf, send_sems, recv_sems):
        x, y, c, _ = _place()
        me = 4 * x + 2 * y + c
        buf[me] = s_ref[...]
        sends = []
        for k in range(1, n_dev):
            px = 1 - x if k & 4 else x
            py = 1 - y if k & 2 else y
            pc = 1 - c if k & 1 else c
            cp = _copy(s_ref, buf.at[me], send_sems, recv_sems, k - 1, (px, py, pc))
            cp.start()
            sends.append((cp, 4 * px + 2 * py + pc))
        for k, (cp, peer) in enumerate(sends):
            _copy(s_ref, buf.at[peer], send_sems, recv_sems, k, (x, y, c)).wait_recv()
        for cp, _ in sends:
            cp.wait_send()
        acc = buf[0]
        for d in range(1, n_dev):
            acc = acc + buf[d]
        out_ref[...] = acc

    vm = pl.BlockSpec(memory_space=pltpu.VMEM)
    return pl.pallas_call(
        body, name="allreduce_small", in_specs=[vm], out_specs=vm,
        out_shape=jax.ShapeDtypeStruct(s.shape, F32),
        scratch_shapes=[pltpu.VMEM((n_dev,) + s.shape, F32), pltpu.SemaphoreType.DMA((n_dev - 1,)),
                        pltpu.SemaphoreType.DMA((n_dev - 1,))],
    )(s)


def join_cols(sm):
    n4, k, n = sm.shape
    return sm.transpose(1, 0, 2).reshape(k, n4 * n)


def split_cols(full):
    k, n = full.shape
    return full.reshape(k, N_CHIPS, n // N_CHIPS).transpose(1, 0, 2)


def _pad_heads(w, width):
    lead, heads = w.shape[:-1], w.shape[-1] // width
    w = w.reshape(lead + (heads, width))
    return jnp.pad(w, [(0, 0)] * len(lead) + [(0, 0), (0, LANES - width)]).reshape(lead + (heads * LANES,))


def _unpad_heads(w, width):
    lead, heads = w.shape[:-1], w.shape[-1] // LANES
    return w.reshape(lead + (heads, LANES))[..., :width].reshape(lead + (heads * width,))


O_F = 3 * FW
O_CQ = O_F + N_HEADS
O_CKV = O_CQ + Q_RANK
O_KR = O_CKV + KV_RANK
O_END = O_KR + ROPE_DIM


def _shard_rows(sm, a, b):
    r = sm.shape[1]
    out = []
    while a < b:
        q = a // r
        e = min(b, (q + 1) * r)
        out.append(sm[q, a - q * r:e - q * r])
        a = e
    return out


def split_w_in_t(w_sm):
    d = w_sm.shape[2]

    def z(n):
        return [jnp.zeros((n, d), w_sm.dtype)]

    small = (_shard_rows(w_sm, O_CQ, O_CKV) + _shard_rows(w_sm, O_CKV, O_KR) + z(HEAD_DIM) + _shard_rows(w_sm, O_KR, O_END)
             + z(LANES - HEAD_DIM - ROPE_DIM) + _shard_rows(w_sm, O_F, O_CQ) + z(LANES - N_HEADS) + z(LANES))
    return jnp.concatenate(_shard_rows(w_sm, 0, O_F), axis=0), jnp.concatenate(small, axis=0)


def join_w_in_t(d_qkv_t, d_small_t):
    kr = S_KR + HEAD_DIM
    segments = [(part, 0, n * FW, FW) for n, part in enumerate(d_qkv_t)]
    segments += [(d_small_t, S_F, O_F, N_HEADS), (d_small_t, S_CQ, O_CQ, Q_RANK),
                 (d_small_t, S_CKV, O_CKV, KV_RANK), (d_small_t, kr, O_KR, ROPE_DIM)]
    r = O_END // N_CHIPS
    shards = []
    for q in range(N_CHIPS):
        pieces = []
        for src, s0, v0, n in segments:
            a, b = max(v0, q * r), min(v0 + n, (q + 1) * r)
            if a < b:
                pieces.append(src[s0 + a - v0:s0 + b - v0])
        shards.append(jnp.concatenate(pieces, axis=0))
    return jnp.stack(shards)


def rope_tables(pos):
    t = pos.shape[0]
    inv_freq = ROPE_THETA ** (-jnp.arange(0, ROPE_DIM, 2, dtype=F32) / ROPE_DIM)
    ang = pos.astype(F32)[:, None] * inv_freq
    cos, sin = jnp.cos(ang), jnp.sin(ang)
    half = ROPE_DIM // 2

    def z(n):
        return jnp.zeros((t, n), F32)

    tab_c = jnp.concatenate([jnp.ones((t, HEAD_DIM), F32), cos, cos, z(LANES - HEAD_DIM - ROPE_DIM)], axis=1)
    tab_a = jnp.concatenate([z(HEAD_DIM), -sin, z(half), z(LANES - HEAD_DIM - ROPE_DIM)], axis=1)
    tab_b = jnp.concatenate([z(HEAD_DIM), z(half), sin, z(LANES - HEAD_DIM - ROPE_DIM)], axis=1)
    return tab_c, tab_a, tab_b


def _pad_lanes(v, n):
    return jnp.pad(v, ((0, 0), (0, n - v.shape[1])))


ATTN_BLK = 512
ATTN_FWD_BLK = 1024
ATTN_BWD_QBLK = 1024
ACC_ROWS = HEAD_DIM + 16


def local_step(xs, pos, tgt, gains, early_weights, late_weights, early_side=None, fwd_sides=(None, None), reduction=None):
    g_attn, b_forget, g_q, g_kv, g_fo, g_mo, g_mlp, g_fin = gains
    t = xs.shape[0]
    blk = min(ATTN_BLK, t)
    fox_scale = 1.0 / (HEAD_DIM ** 0.5)
    mla_scale = 1.0 / ((HEAD_DIM + ROPE_DIM) ** 0.5)

    h1, *gathered = rmsnorm(xs, g_attn, out_dtype=BF16, name="norm_attn", side=early_side)
    w_in_t, w_uq_p, w_ukv = early_weights(gathered)
    w_qkv_t, w_small_t = split_w_in_t(w_in_t)
    kv = w_ukv.reshape(KV_RANK, N_HEADS, 2 * HEAD_DIM)
    w_ukv_p = jnp.concatenate([_pad_heads(kv[:, :, :HEAD_DIM].reshape(KV_RANK, FW), HEAD_DIM),
                               kv[:, :, HEAD_DIM:].reshape(KV_RANK, FW)], axis=1)
    b_f = _pad_lanes(b_forget, LANES)
    tab_c, tab_a, tab_b = rope_tables(pos)

    qkv, qkv_t = mm(h1, w_qkv_t, trans_b=True, out_dtypes=[BF16], t_blk=blk, name="proj_qkv", bn=1536)
    small, = mm(h1, w_small_t, trans_b=True, out_dtypes=[F32], name="proj_small")
    mq, mk, mv, lf, cqn, ckvn, mq_t, mv_t = mla_prep(small, g_q, g_kv, w_uq_p, w_ukv_p, tab_c, tab_a, tab_b, b_f,
                                                     name="mla_prep", bt=blk)
    f_cum = cumsum_rows(lf, reverse=False, name="gate_cumsum")
    f_blocks = f_cum[:, :N_HEADS].reshape(t // blk, blk, N_HEADS).transpose(0, 2, 1)
    fo, st_f, *gathered = flash_fwd(qkv_t, qkv, qkv_t, f_cum, qoff=0, koff=PAIRS, voff=2 * PAIRS, pair=True,
                                    scale=fox_scale, name="fox_fwd", blk=ATTN_FWD_BLK, side=fwd_sides[0])
    mo, st_m, *more = flash_fwd(mq_t, mk, mv_t, None, qoff=0, koff=0, voff=0, pair=False, scale=mla_scale, name="mla_fwd",
                                blk=ATTN_FWD_BLK, side=fwd_sides[1])
    w_o, w_up, w_down = late_weights(gathered + more)
    mixed = mix_norm(fo, mo, g_fo, g_mo, name="norm_mix")

    def inv_rms(v):
        return lax.rsqrt(jnp.mean(v * v, axis=-1, keepdims=True) + EPS)

    def residual_then_norm(acc, res, g):
        xn = acc + res
        return xn, xn * inv_rms(xn) * g

    def norm_bwd(dh, xn, res, g):
        r = inv_rms(xn)
        uu = dh * g
        return (r * uu - xn * (r * r * r * jnp.mean(uu * xn, axis=-1, keepdims=True)) + res,
                jnp.sum(dh * (xn * r), axis=0, keepdims=True))

    def norm_bwd2(dh, xn, res, g):
        dx, dg = norm_bwd(dh, xn, res, g)
        return dx, dx, dg

    def residual_then_loss(acc, res, target, g):
        xn = acc + res
        r = inv_rms(xn)
        xh = xn * r
        e = xh * g - target
        part = 0.5 * jnp.sum(jnp.mean(e * e, axis=-1, keepdims=True), axis=0, keepdims=True)
        dy = e * (1.0 / xn.shape[1])
        uu = dy * g
        dx = r * uu - xn * (r * r * r * jnp.mean(uu * xn, axis=-1, keepdims=True))
        return dx, dx, jnp.sum(dy * xh, axis=0, keepdims=True), part + jnp.zeros_like(g)

    x1, h2 = mm(mixed, w_o, extras=[xs], vecs=[g_mlp], epilogue=residual_then_norm, out_dtypes=[F32, BF16], name="out_proj")

    def relu2(uu):
        r = jnp.maximum(uu.astype(F32), 0.0)
        return (r * r).astype(BF16)

    u, = mm(h2, w_up, out_dtypes=[BF16], name="mlp_up", bn=2048)
    dx2, dx2b, dg_fin, loss_row = mm(u, w_down, a_pro=relu2, extras=[x1, tgt], vecs=[g_fin], epilogue=residual_then_loss,
                                     out_dtypes=[F32, BF16], n_sums=2, name="mlp_down_loss")
    loss = loss_row[:, :1]

    def relu2_grad(acc, uu):
        return (acc * (2.0 * jnp.maximum(uu.astype(F32), 0.0)),)

    du, = mm(dx2b, w_down, trans_b=True, extras=[u], epilogue=relu2_grad, out_dtypes=[BF16], name="mlp_down_bwd", bn=2048)
    dw_down, dw_down_b = (g.reshape(N_CHIPS, -1, w_down.shape[1])
                          for g in mm_tn(u, dx2b, a_pro=relu2, name="dw_down", bf16_copy=True))
    dx1, dx1b, dg_mlp = mm(du, w_up, trans_b=True, extras=[x1, dx2], vecs=[g_mlp], epilogue=norm_bwd2,
                           out_dtypes=[F32, BF16], n_sums=1, name="mlp_up_bwd")
    dw_up, dw_up_b = mm_tn(h2, du, name="dw_up", col_shards=N_CHIPS, bf16_copy=True)

    dw_o, dw_o_b = (g.reshape(N_CHIPS, -1, w_o.shape[1]) for g in mm_tn(mixed, dx1b, name="dw_o", bf16_copy=True))
    late, late_b = (dw_o, dw_up, dw_down), (dw_o_b, dw_up_b, dw_down_b)
    red = reduction
    dfo, dmo, dg_fo, dg_mo, st_f, st_m, dfo_t, dmo_t, *got = mix_norm_bwd(
        dx1b, w_o, fo, mo, g_fo, g_mo, st_f, st_m, name="out_proj_mix_bwd", bt=blk,
        side=red.late_swap(late, late_b) if red else None)
    dfq, dfk, dfv, d_f, *got = flash_bwd(
        qkv, qkv_t, qkv, qkv, dfo, dfo_t, st_f, f_blocks, qoff=0, koff=PAIRS, voff=2 * PAIRS, pair=True,
        scale=fox_scale, name="fox_bwd", qblk=ATTN_BWD_QBLK, side=red.late_scatter(got) if red else None)
    dmq, dmk, dmv, *got = flash_bwd(mq, mq_t, mk, mv, dmo, dmo_t, st_m, None, qoff=0, koff=0, voff=0,
                                    pair=False, scale=mla_scale, name="mla_bwd", qblk=ATTN_BWD_QBLK,
                                    side=red.late_halves(got) if red else None)
    if red:
        red.late_done(got)
    dlf = cumsum_rows(d_f, reverse=True, name="gate_cumsum_bwd")
    dsmall, dq_u, dkv_u, dg_q, dg_kv, db_f = mla_prep_bwd(dmq, dmk, dmv, dlf, small, g_q, g_kv, w_uq_p, w_ukv_p,
                                                          tab_c, tab_a, tab_b, b_f, name="mla_prep_bwd")
    dw_uq_p = mm_tn(cqn, dq_u, name="dw_uq")
    dw_ukv_p = mm_tn(ckvn, dkv_u, name="dw_ukv")

    def to_bf16(tile):
        return tile.astype(BF16)

    dqkv = [dfq, dfk, dfv]
    dw_in_t = join_w_in_t([mm_tn(part, h1, a_pro=to_bf16, name="dw_" + nm) for part, nm in zip(dqkv, "qkv")],
                          mm_tn(dsmall, h1, name="dw_small"))
    dk_cols = _unpad_heads(dw_ukv_p[:, :HW], HEAD_DIM).reshape(KV_RANK, N_HEADS, HEAD_DIM)
    dv_cols = dw_ukv_p[:, HW:].reshape(KV_RANK, N_HEADS, HEAD_DIM)
    dw_ukv = jnp.concatenate([dk_cols, dv_cols], axis=2).reshape(KV_RANK, N_HEADS * 2 * HEAD_DIM)
    early = (dw_in_t, split_cols(dw_uq_p), split_cols(dw_ukv))
    w_parts = [w_qkv_t[n * FW:(n + 1) * FW] for n in range(3)]
    grad_x, dg_attn, *got = mm(dqkv + [dsmall], w_parts + [w_small_t], a_pro=to_bf16, extras=[xs, dx1], vecs=[g_attn],
                               epilogue=norm_bwd, out_dtypes=[F32], n_sums=1, name="proj_bwd",
                               side=red.early_scatter(early) if red else None)
    if red:
        red.early_done(got)
    d_gains = (dg_attn, db_f[:, :N_HEADS], dg_q, dg_kv, dg_fo, dg_mo, dg_mlp, dg_fin)
    return loss, grad_x, early, late, d_gains


class GradReduction:
    def __init__(self, core_id, chip):
        self.core_id, self.chip = core_id, chip
        self.core = core_id.reshape(1).astype(jnp.int32)
        self.grads, self.pairs, self.halves, self.others = {}, {}, {}, {}

    def _other_halves(self, grads):
        out = []
        for g in grads:
            axis = 1 + split_axis(g.shape[1])
            size = g.shape[axis] // 2
            out.append(lax.dynamic_slice_in_dim(g, (1 - self.core_id) * size, size, axis=axis).astype(BF16))
        return out

    def _add_pairs(self, group, recvs):
        self.pairs[group] = [add_pair(g, r, self.core, name="add_pair_%s_%d" % (group, n))
                             for n, (g, r) in enumerate(zip(self.grads[group], recvs))]
        return scatter_side(self.pairs[group])

    def _chip_sums(self, group, scattered):
        chip = self.chip
        with_mine = [lax.dynamic_update_index_in_dim(s, lax.dynamic_index_in_dim(p, chip, 0, keepdims=True), chip, 0)
                     for s, p in zip(scattered, self.pairs[group])]
        self.halves[group] = [sum_chips(s, name="sum_chips_%s_%d" % (group, n)) for n, s in enumerate(with_mine)]
        return self.halves[group]

    def late_swap(self, grads, bf16_copies):
        self.grads["late"] = list(grads)
        return swap_halves_side(list(bf16_copies))

    def late_scatter(self, recvs):
        return self._add_pairs("late", recvs)

    def late_halves(self, scattered):
        return swap_side(self._chip_sums("late", scattered))

    def late_done(self, others):
        self.others["late"] = list(others)

    def early_scatter(self, grads):
        self.grads["early"] = list(grads)
        return self._add_pairs("early", run_side(swap_side(self._other_halves(grads)), name="swap_early_sends"))

    def early_done(self, scattered):
        self.others["early"] = list(run_side(swap_side(self._chip_sums("early", scattered)), name="swap_early_halves"))

def kernel(x, positions, attn_norm_g, w_in, b_forget, q_norm_g, w_uq, kv_norm_g, w_ukv, fox_out_g, mla_out_g, w_o, mlp_norm_g, w_up, w_down, final_norm_g, loss_target, m_attn_norm_g, m_w_in, m_b_forget, m_q_norm_g, m_w_uq, m_kv_norm_g, m_w_ukv, m_fox_out_g, m_mla_out_g, m_w_o, m_mlp_norm_g, m_w_up, m_w_down, m_final_norm_g, v_attn_norm_g, v_w_in, v_b_forget, v_q_norm_g, v_w_uq, v_kv_norm_g, v_w_ukv, v_fox_out_g, v_mla_out_g, v_w_o, v_mlp_norm_g, v_w_up, v_w_down, v_final_norm_g):
    core_id = lax.axis_index("c")
    core = core_id.reshape(1).astype(jnp.int32)
    chip = 2 * lax.axis_index("x") + lax.axis_index("y")
    big = [w_in, w_uq, w_ukv, w_o, w_up, w_down]
    big_m = [m_w_in, m_w_uq, m_w_ukv, m_w_o, m_w_up, m_w_down]
    big_v = [v_w_in, v_w_uq, v_w_ukv, v_w_o, v_w_up, v_w_down]
    n_early = 3

    def vec(a):
        return a.reshape(1, -1)

    small = [attn_norm_g, b_forget, q_norm_g, kv_norm_g, fox_out_g, mla_out_g, mlp_norm_g, final_norm_g]
    small_m = [m_attn_norm_g, m_b_forget, m_q_norm_g, m_kv_norm_g, m_fox_out_g, m_mla_out_g, m_mlp_norm_g, m_final_norm_g]
    small_v = [v_attn_norm_g, v_b_forget, v_q_norm_g, v_kv_norm_g, v_fox_out_g, v_mla_out_g, v_mlp_norm_g, v_final_norm_g]
    gains = [vec(a) for a in small]

    views = [big[0][0].T, _pad_heads(big[1][0], HEAD_DIM + ROPE_DIM)] + [w[0] for w in big[2:]]
    shards = [v.astype(BF16) for v in views]

    def with_own(gathered, mine):
        return [lax.dynamic_update_index_in_dim(g, s, chip, 0) for g, s in zip(gathered, mine)]

    def early_weights(gathered):
        g_in, g_uq, g_ukv = with_own(gathered, shards[:n_early])
        return g_in, join_cols(g_uq), join_cols(g_ukv)

    def late_weights(gathered):
        g_o, g_up, g_down = with_own(gathered, shards[n_early:])
        return g_o.reshape(-1, g_o.shape[2]), join_cols(g_up), g_down.reshape(-1, g_down.shape[2])

    reduction = GradReduction(core_id, chip)
    loss, grad_x, _, _, d_small = local_step(
        x[0], positions[0], loss_target[0], gains, early_weights, late_weights, gather_side(shards[:n_early]),
        (gather_side(shards[n_early:-1]), gather_side(shards[-1:])), reduction)
    halves = reduction.halves["early"] + reduction.halves["late"]
    others = reduction.others["early"] + reduction.others["late"]

    def rows8(vs):
        return jnp.concatenate([_pad_lanes(vec(a).astype(F32), 1024) for a in vs], axis=0)

    with_loss = [jnp.concatenate([d, loss], axis=1) if n == 1 else d for n, d in enumerate(d_small)]
    g_small8 = allreduce_small(rows8(with_loss))

    outs_big = []
    for n, (w, gm, go, m, v) in enumerate(zip(big, halves, others, big_m, big_v)):
        if n == 0:
            outs = adamw_halves_t(w[0].T, gm, go, m[0].T, v[0].T, core, name="adamw_%d" % n)
            outs_big.append([o.T[None] for o in outs])
        else:
            if n == 1:
                gm, go = (_unpad_heads(gh, HEAD_DIM + ROPE_DIM) for gh in (gm, go))
            outs_big.append(adamw_halves(w, gm, go, m, v, core, name="adamw_%d" % n))
    d8, m8, v8 = adamw(rows8(small), g_small8, rows8(small_m), rows8(small_v), name="adamw_small")

    def unrows8(a8):
        return [a8[n, :s.size].reshape(s.shape) for n, s in enumerate(small)]

    loss_all = g_small8[1, N_HEADS]
    grads, deltas, new_m, new_v = [None] * 14, [None] * 14, [None] * 14, [None] * 14
    big_at = [1, 4, 6, 9, 11, 12]
    small_at = [0, 2, 3, 5, 7, 8, 10, 13]
    for n, at in enumerate(big_at):
        grads[at], deltas[at], new_m[at], new_v[at] = outs_big[n]
    for at, g, dd, mm_, vv in zip(small_at, unrows8(g_small8), unrows8(d8), unrows8(m8), unrows8(v8)):
        grads[at], deltas[at], new_m[at], new_v[at] = g, dd, mm_, vv
    return (loss_all, grad_x[None], *grads, *deltas, *new_m, *new_v)
```

```python
import jax
import jax.numpy as jnp
from jax import lax
from jax.experimental import pallas as pl
from jax.experimental.pallas import tpu as pltpu

F32 = jnp.float32
BF16 = jnp.bfloat16
MESH = pl.DeviceIdType.MESH

EPS = 1e-6
ROPE_THETA = 10000.0
N_HEADS = 8
PAIRS = N_HEADS // 2
HEAD_DIM = 64
ROPE_DIM = 32
LANES = 128
Q_RANK = 384
KV_RANK = 256
N_CHIPS = 4
ADAM_LR, ADAM_B1, ADAM_B2, ADAM_EPS, ADAM_WD, ADAM_STEP = 0.001, 0.9, 0.999, 1e-08, 0.01, 10
VMEM_LIMIT = 48 * 1024 * 1024
LOG2E = 1.4426950408889634
LN2 = 0.6931471805599453
NN = (((1,), (0,)), ((), ()))
NT = (((1,), (1,)), ((), ()))
TN = (((0,), (0,)), ((), ()))


def _params(sem=None):
    return pltpu.CompilerParams(dimension_semantics=sem, vmem_limit_bytes=VMEM_LIMIT)


def _fit(block, dim):
    if dim <= block:
        return dim
    return next(b for b in range(block - block % LANES, 0, -LANES) if dim % b == 0)


def _row_block(rows):
    return next(b for b in (256, 128, 64, 32, 16, 8) if rows % b == 0)


def gridded(body, *, name, grid, in_specs, out_specs, out_shape, ins, semantics, side=None):
    if side is None:
        return pl.pallas_call(body, name=name, grid=grid, in_specs=in_specs, out_specs=out_specs, out_shape=out_shape,
                              compiler_params=_params(semantics))(*ins)
    n_in, n_out = len(in_specs), len(out_specs)
    steps = 1
    for extent in grid:
        steps *= extent

    def riding(*refs):
        main_in, s_in, main_out, s_out, sems = _split_refs(refs, [n_in, len(side.ins), n_out, len(side.out_shapes), 2])
        step = 0
        for axis, extent in enumerate(grid):
            step = step * extent + pl.program_id(axis)

        @pl.when(step == 0)
        def _():
            side.first(s_in, s_out, *sems)

        body(*main_in, *main_out)

        @pl.when(step == steps - 1)
        def _():
            if side.mid is not None:
                side.mid(s_in, s_out, *sems)
            side.last(s_in, s_out, *sems)

    s_in_specs, s_out_specs = side.specs()
    return pl.pallas_call(
        riding, name=name, grid=grid, in_specs=list(in_specs) + s_in_specs, out_specs=list(out_specs) + s_out_specs,
        out_shape=list(out_shape) + side.out_shapes, scratch_shapes=side.sems(),
        compiler_params=_params(("arbitrary",) * len(grid)))(*ins, *side.ins)


def rmsnorm(x, g, *, out_dtype, name, bt=512, side=None):
    t, d = x.shape
    bt = min(bt, t)

    def body(x_ref, g_ref, o_ref):
        xv = x_ref[...].astype(F32)
        r = lax.rsqrt(jnp.mean(xv * xv, axis=-1, keepdims=True) + EPS)
        o_ref[...] = (xv * r * g_ref[...]).astype(o_ref.dtype)

    return gridded(
        body, name=name, grid=(t // bt,),
        in_specs=[pl.BlockSpec((bt, d), lambda i: (i, 0)), pl.BlockSpec((1, d), lambda i: (0, 0))],
        out_specs=[pl.BlockSpec((bt, d), lambda i: (i, 0))],
        out_shape=[jax.ShapeDtypeStruct((t, d), out_dtype)],
        ins=[x, g], semantics=("parallel",), side=side)


def mm(a, b, *, trans_b=False, a_pro=None, extras=(), vecs=(), epilogue=None, out_dtypes, n_sums=0, t_blk=None, name,
       bm=1024, bn=1024, side=None):
    a_list = list(a) if isinstance(a, (list, tuple)) else [a]
    b_list = list(b) if isinstance(b, (list, tuple)) else [b]
    m = a_list[0].shape[0]
    n = b_list[0].shape[0] if trans_b else b_list[0].shape[1]
    ks = [x.shape[1] for x in a_list]
    if sum(ks) > 2048:
        bm = bm // 2
    bm, bn = _fit(bm, m), _fit(bn, n)
    assert n_sums == 0 or bn == n
    n_ab, n_ex, n_vec, n_out = len(a_list), len(extras), len(vecs), len(out_dtypes)
    n_t = 0 if t_blk is None else 1

    def body(*refs):
        a_refs, b_refs, ex, vs, outs, t_outs, sums = _split_refs(refs, [n_ab, n_ab, n_ex, n_vec, n_out, n_t, n_sums])
        acc = None
        for a_ref, b_ref in zip(a_refs, b_refs):
            a_tile = a_ref[...] if a_pro is None else a_pro(a_ref[...])
            part = lax.dot_general(a_tile, b_ref[...], NT if trans_b else NN, preferred_element_type=F32)
            acc = part if acc is None else acc + part
        res = epilogue(acc, *[e[...] for e in ex], *[v[...] for v in vs]) if epilogue is not None else (acc,)
        for o, r in zip(outs, res[:n_out]):
            o[...] = r.astype(o.dtype)
        for t_ref in t_outs:
            for u in range(bm // t_blk):
                t_ref[u] = res[0][u * t_blk:(u + 1) * t_blk, :].T.astype(t_ref.dtype)
        if n_sums:
            @pl.when(pl.program_id(0) == 0)
            def _():
                for s_ref in sums:
                    s_ref[...] = jnp.zeros_like(s_ref)

            for s_ref, r in zip(sums, res[n_out:]):
                s_ref[...] += r

    tile = pl.BlockSpec((bm, bn), lambda i, j: (i, j))
    vec = pl.BlockSpec((1, bn), lambda i, j: (0, j))
    t_specs, t_shapes = [], []
    if t_blk is not None:
        t_specs = [pl.BlockSpec((bm // t_blk, bn, t_blk), lambda i, j: (i, j, 0))]
        t_shapes = [jax.ShapeDtypeStruct((m // t_blk, n, t_blk), out_dtypes[0])]
    a_specs = [pl.BlockSpec((bm, k), lambda i, j: (i, 0)) for k in ks]
    b_specs = [pl.BlockSpec((bn, k), lambda i, j: (j, 0)) if trans_b else pl.BlockSpec((k, bn), lambda i, j: (0, j)) for k in ks]
    return gridded(
        body, name=name, grid=(m // bm, n // bn),
        in_specs=a_specs + b_specs + [tile] * n_ex + [vec] * n_vec,
        out_specs=[tile] * n_out + t_specs + [vec] * n_sums,
        out_shape=[jax.ShapeDtypeStruct((m, n), dt) for dt in out_dtypes] + t_shapes + [jax.ShapeDtypeStruct((1, n), F32)] * n_sums,
        ins=[*a_list, *b_list, *extras, *vecs],
        semantics=("arbitrary", "arbitrary") if n_sums else ("parallel", "parallel"), side=side)


def mm_tn(a, b, *, a_pro=None, name, col_shards=1, bf16_copy=False, bk=1024, bn=1024, bt=2048):
    t, k = a.shape
    n = b.shape[1]
    ns = n // col_shards
    bk, bn, bt = _fit(bk, k), _fit(bn, ns), _fit(bt, t)
    per = ns // bn
    last = t // bt - 1

    def body(a_ref, b_ref, o_ref, *copy_ref):
        @pl.when(pl.program_id(2) == 0)
        def _():
            o_ref[...] = jnp.zeros_like(o_ref)

        a_tile = a_ref[...] if a_pro is None else a_pro(a_ref[...])
        o_ref[...] += lax.dot_general(a_tile, b_ref[...], TN, preferred_element_type=F32)
        if bf16_copy:
            @pl.when(pl.program_id(2) == last)
            def _():
                copy_ref[0][...] = o_ref[...].astype(BF16)

    if col_shards == 1:
        out_spec = pl.BlockSpec((bk, bn), lambda i, j, s: (i, j))
        shape = (k, n)
    else:
        out_spec = pl.BlockSpec((None, bk, bn), lambda i, j, s: (j // per, i, j % per))
        shape = (col_shards, k, ns)
    out_specs, out_shape = out_spec, jax.ShapeDtypeStruct(shape, F32)
    if bf16_copy:
        out_specs, out_shape = [out_spec, out_spec], [out_shape, jax.ShapeDtypeStruct(shape, BF16)]
    return pl.pallas_call(
        body, name=name, grid=(k // bk, n // bn, t // bt),
        in_specs=[pl.BlockSpec((bt, bk), lambda i, j, s: (s, i)), pl.BlockSpec((bt, bn), lambda i, j, s: (s, j))],
        out_specs=out_specs, out_shape=out_shape,
        compiler_params=_params(("parallel", "parallel", "arbitrary")),
    )(a, b)


def _split3(x):
    hi = x.astype(BF16)
    r1 = x - hi.astype(F32)
    mid = r1.astype(BF16)
    lo = (r1 - mid.astype(F32)).astype(BF16)
    return hi, mid, lo


def cumsum_rows(x, *, reverse, name, bc=512):
    t, d = x.shape
    bc = min(bc, t)
    nb = t // bc

    def body(x_ref, o_ref, carry):
        @pl.when(pl.program_id(0) == 0)
        def _():
            carry[...] = jnp.zeros_like(carry)

        r = lax.broadcasted_iota(jnp.int32, (bc, bc), 0)
        c = lax.broadcasted_iota(jnp.int32, (bc, bc), 1)
        tri = jnp.where((r <= c) if reverse else (r >= c), 1.0, 0.0).astype(BF16)
        hi, mid, lo = _split3(x_ref[...])
        s = (lax.dot_general(tri, hi, NN, preferred_element_type=F32)
             + lax.dot_general(tri, mid, NN, preferred_element_type=F32)
             + lax.dot_general(tri, lo, NN, preferred_element_type=F32)) + carry[0:1, :]
        o_ref[...] = s
        carry[0:1, :] = s[0:1, :] if reverse else s[bc - 1:bc, :]

    imap = (lambda i: (nb - 1 - i, 0)) if reverse else (lambda i: (i, 0))
    return pl.pallas_call(
        body, name=name, grid=(nb,),
        in_specs=[pl.BlockSpec((bc, d), imap)], out_specs=pl.BlockSpec((bc, d), imap),
        out_shape=jax.ShapeDtypeStruct((t, d), F32),
        scratch_shapes=[pltpu.VMEM((8, d), F32)],
        compiler_params=_params(("arbitrary",)),
    )(x)


def _rope(x, c, a, b):
    return x * c + pltpu.roll(x, LANES - ROPE_DIM // 2, 1) * a + pltpu.roll(x, ROPE_DIM // 2, 1) * b


def _rope_bwd(d, c, a, b):
    return d * c + pltpu.roll(d * a, ROPE_DIM // 2, 1) + pltpu.roll(d * b, LANES - ROPE_DIM // 2, 1)


S_CQ, S_CKV, S_KR, S_F, S_END = 0, Q_RANK, Q_RANK + KV_RANK, Q_RANK + KV_RANK + LANES, 1024
HW = N_HEADS * LANES
FW = N_HEADS * HEAD_DIM


def _rope_tables(pos_col, inv_row):
    ang = pos_col * inv_row
    cos, sin = jnp.cos(ang), jnp.sin(ang)
    lane = lax.broadcasted_iota(jnp.int32, (1, LANES), 1)
    first = (lane >= HEAD_DIM) & (lane < HEAD_DIM + ROPE_DIM // 2)
    second = (lane >= HEAD_DIM + ROPE_DIM // 2) & (lane < HEAD_DIM + ROPE_DIM)
    return (jnp.where(lane < HEAD_DIM, 1.0, jnp.where(first | second, cos, 0.0)), jnp.where(first, -sin, 0.0),
            jnp.where(second, sin, 0.0))


def mla_prep(small, g_q, g_kv, w_uq, w_ukv, pos_col, inv_row, b_f, *, name, bt=512):
    t = small.shape[0]
    bt = min(bt, t)

    def body(s_ref, gq_ref, gkv_ref, wq_ref, wkv_ref, pos_ref, inv_ref, bf_ref,
             mq_ref, mk_ref, mv_ref, lf_ref, cqn_ref, ckvn_ref, mqt_ref, mvt_ref):
        cq = s_ref[:, S_CQ:S_CKV]
        rq = lax.rsqrt(jnp.mean(cq * cq, axis=-1, keepdims=True) + EPS)
        cqn = (cq * rq * gq_ref[...]).astype(BF16)
        ckv = s_ref[:, S_CKV:S_KR]
        rkv = lax.rsqrt(jnp.mean(ckv * ckv, axis=-1, keepdims=True) + EPS)
        ckvn = (ckv * rkv * gkv_ref[...]).astype(BF16)
        cqn_ref[...] = cqn
        ckvn_ref[...] = ckvn
        tc, ta, tb = _rope_tables(pos_ref[...], inv_ref[...])
        q = jnp.dot(cqn, wq_ref[...], preferred_element_type=F32)
        kv = jnp.dot(ckvn, wkv_ref[...], preferred_element_type=F32)
        kr = _rope(s_ref[:, S_KR:S_F], tc, ta, tb)
        for h in range(N_HEADS):
            sl = slice(h * LANES, (h + 1) * LANES)
            roped = _rope(q[:, sl], tc, ta, tb)
            mq_ref[:, sl] = roped.astype(BF16)
            mqt_ref[0, sl, :] = roped.T.astype(BF16)
            mk_ref[:, sl] = (kv[:, sl] + kr).astype(BF16)
        mv_ref[...] = kv[:, HW:].astype(BF16)
        mvt_ref[0] = kv[:, HW:].T.astype(BF16)
        z = s_ref[:, S_F:S_END - LANES] + bf_ref[...]
        lf_ref[...] = jnp.minimum(z, 0.0) - jnp.log(1.0 + jnp.exp(-jnp.abs(z)))

    def row(w):
        return pl.BlockSpec((bt, w), lambda i: (i, 0))

    def full(arr):
        return pl.BlockSpec(arr.shape, lambda i: (0, 0))

    return pl.pallas_call(
        body, name=name, grid=(t // bt,),
        in_specs=[row(S_END), full(g_q), full(g_kv), full(w_uq), full(w_ukv), row(1), full(inv_row), full(b_f)],
        out_specs=[row(HW), row(HW), row(FW), row(LANES), row(Q_RANK), row(KV_RANK),
                   pl.BlockSpec((1, HW, bt), lambda i: (i, 0, 0)), pl.BlockSpec((1, FW, bt), lambda i: (i, 0, 0))],
        out_shape=[jax.ShapeDtypeStruct((t, HW), BF16)] * 2 + [jax.ShapeDtypeStruct((t, FW), BF16), jax.ShapeDtypeStruct((t, LANES), F32),
                   jax.ShapeDtypeStruct((t, Q_RANK), BF16), jax.ShapeDtypeStruct((t, KV_RANK), BF16),
                   jax.ShapeDtypeStruct((t // bt, HW, bt), BF16), jax.ShapeDtypeStruct((t // bt, FW, bt), BF16)],
        compiler_params=_params(("parallel",)),
    )(small, g_q, g_kv, w_uq, w_ukv, pos_col, inv_row, b_f)


def mla_prep_bwd(dmq, dmk, dmv, dlf, small, g_q, g_kv, w_uq, w_ukv, pos_col, inv_row, b_f, *, name, bt=512):
    t = small.shape[0]
    bt = min(bt, t)

    def body(dmq_ref, dmk_ref, dmv_ref, dlf_ref, s_ref, gq_ref, gkv_ref, wq_ref, wkv_ref, pos_ref, inv_ref, bf_ref,
             ds_ref, dq_ref, dkv_ref, dgq_ref, dgkv_ref, db_ref):
        tc, ta, tb = _rope_tables(pos_ref[...], inv_ref[...])
        lane = lax.broadcasted_iota(jnp.int32, (1, LANES), 1)
        dkr = jnp.zeros((bt, LANES), F32)
        for h in range(N_HEADS):
            sl = slice(h * LANES, (h + 1) * LANES)
            dq_ref[:, sl] = _rope_bwd(dmq_ref[:, sl], tc, ta, tb).astype(BF16)
            dkr = dkr + dmk_ref[:, sl]
        dkv_ref[:, :HW] = dmk_ref[...].astype(BF16)
        dkv_ref[:, HW:] = dmv_ref[...].astype(BF16)
        in_rope = (lane >= HEAD_DIM) & (lane < HEAD_DIM + ROPE_DIM)
        ds_ref[:, S_KR:S_F] = jnp.where(in_rope, _rope_bwd(dkr, tc, ta, tb), 0.0).astype(BF16)

        def norm_bwd(raw, g_ref, dn, dg_ref):
            r = lax.rsqrt(jnp.mean(raw * raw, axis=-1, keepdims=True) + EPS)
            u = dn * g_ref[...]
            dot = jnp.mean(u * raw, axis=-1, keepdims=True)
            dg_ref[...] += jnp.sum(dn * (raw * r), axis=0, keepdims=True)
            return r * u - raw * (r * r * r * dot)

        @pl.when(pl.program_id(0) == 0)
        def _():
            dgq_ref[...] = jnp.zeros_like(dgq_ref)
            dgkv_ref[...] = jnp.zeros_like(dgkv_ref)
            db_ref[...] = jnp.zeros_like(db_ref)

        dcqn = lax.dot_general(dq_ref[...], wq_ref[...], NT, preferred_element_type=F32)
        ds_ref[:, S_CQ:S_CKV] = norm_bwd(s_ref[:, S_CQ:S_CKV], gq_ref, dcqn, dgq_ref).astype(BF16)
        dckvn = lax.dot_general(dkv_ref[...], wkv_ref[...], NT, preferred_element_type=F32)
        ds_ref[:, S_CKV:S_KR] = norm_bwd(s_ref[:, S_CKV:S_KR], gkv_ref, dckvn, dgkv_ref).astype(BF16)
        z = s_ref[:, S_F:S_END - LANES] + bf_ref[...]
        dz = jnp.where(lane < N_HEADS, dlf_ref[...] / (1.0 + jnp.exp(z)), 0.0)
        db_ref[...] += jnp.sum(dz, axis=0, keepdims=True)
        ds_ref[:, S_F:S_END - LANES] = dz.astype(BF16)
        ds_ref[:, S_END - LANES:] = jnp.zeros((bt, LANES), BF16)

    def row(w):
        return pl.BlockSpec((bt, w), lambda i: (i, 0))

    def full(arr):
        return pl.BlockSpec(arr.shape, lambda i: (0, 0))

    def vec(w):
        return pl.BlockSpec((1, w), lambda i: (0, 0))

    return pl.pallas_call(
        body, name=name, grid=(t // bt,),
        in_specs=[row(HW), row(HW), row(FW), row(LANES), row(S_END), full(g_q), full(g_kv), full(w_uq), full(w_ukv),
                  row(1), full(inv_row), full(b_f)],
        out_specs=[row(S_END), row(HW), row(HW + FW), vec(Q_RANK), vec(KV_RANK), vec(LANES)],
        out_shape=[jax.ShapeDtypeStruct((t, S_END), BF16), jax.ShapeDtypeStruct((t, HW), BF16),
                   jax.ShapeDtypeStruct((t, HW + FW), BF16), jax.ShapeDtypeStruct((1, Q_RANK), F32),
                   jax.ShapeDtypeStruct((1, KV_RANK), F32), jax.ShapeDtypeStruct((1, LANES), F32)],
        compiler_params=_params(("arbitrary",)),
    )(dmq, dmk, dmv, dlf, small, g_q, g_kv, w_uq, w_ukv, pos_col, inv_row, b_f)


class Side:
    def __init__(self, ins, out_shapes, n_sems, first, last, mid=None):
        self.ins, self.out_shapes, self.n_sems = list(ins), list(out_shapes), n_sems
        self.first, self.mid, self.last = first, mid, last

    def specs(self):
        return [ANY] * len(self.ins), [ANY] * len(self.out_shapes)

    def sems(self):
        return [pltpu.SemaphoreType.DMA((self.n_sems,)), pltpu.SemaphoreType.DMA((self.n_sems,))]


def _lane():
    return lax.broadcasted_iota(jnp.int32, (1, LANES), 1)


def _halves(x):
    zero = jnp.zeros_like(x)
    return [jnp.where(_lane() < HEAD_DIM, x, zero), jnp.where(_lane() >= HEAD_DIM, x, zero)]


def _groups(x):
    return [x[:, :LANES], x[:, LANES:]]


def _pick_row(tile, h):
    row = lax.broadcasted_iota(jnp.int32, (tile.shape[0], 1), 0)
    return jnp.sum(jnp.where(row == h, tile, 0.0), axis=0, keepdims=True)


def _pick_lane(tile, h):
    return jnp.sum(jnp.where(_lane() == h, tile, 0.0), axis=1, keepdims=True)


def _row_halves(x):
    row = lax.broadcasted_iota(jnp.int32, (LANES, 1), 0)
    zero = jnp.zeros_like(x)
    return [jnp.where(row < HEAD_DIM, x, zero), jnp.where(row >= HEAD_DIM, x, zero)]


def _below_diagonal(s):
    r = lax.broadcasted_iota(jnp.int32, s.shape, 0)
    c = lax.broadcasted_iota(jnp.int32, s.shape, 1)
    return jnp.where(c <= r, s, -jnp.inf)


def _above_diagonal(s):
    r = lax.broadcasted_iota(jnp.int32, s.shape, 0)
    c = lax.broadcasted_iota(jnp.int32, s.shape, 1)
    return jnp.where(r <= c, s, -jnp.inf)


def _split_refs(refs, counts):
    out, at = [], 0
    for n in counts:
        out.append(refs[at:at + n])
        at += n
    return out


def flash_fwd(qt_arr, k_arr, vt_arr, f_cum, *, qoff, koff, voff, pair, scale, name, blk=512, side=None):
    t = k_arr.shape[0]
    tblk = qt_arr.shape[2]
    blk = max(min(blk, t), tblk)
    sub = blk // tblk
    nb = t // blk
    steps = PAIRS * nb
    w = LANES if pair else 2 * LANES
    has_bias = f_cum is not None
    ins = [qt_arr, k_arr, vt_arr] + ([f_cum] if has_bias else [])

    def wide(ref, first):
        parts = [ref[first + u] for u in range(sub)]
        return parts[0] if sub == 1 else jnp.concatenate(parts, axis=1)
    s_ins, s_outs = (side.ins, side.out_shapes) if side else ([], [])

    def body(*refs):
        main, si, outs, so, sems = _split_refs(refs, [len(ins), len(s_ins), 2, len(s_outs), 2 if side else 0])
        qt_ref, k_ref, vt_ref = main[:3]
        f_ref = main[3] if has_bias else None
        o_ref, st_ref = outs
        g, i = pl.program_id(0), pl.program_id(1)
        step_id = g * nb + i
        if side:
            @pl.when(step_id == 0)
            def _():
                side.first(si, so, *sems)

            if side.mid is not None:
                @pl.when(step_id == (3 * steps) // 4)
                def _():
                    side.mid(si, so, *sems)

        qt = (wide(qt_ref, 0).astype(F32) * (scale * LOG2E)).astype(BF16)
        qts = _row_halves(qt) if pair else [qt[:LANES], qt[LANES:]]

        def k_of(kk, n):
            return kk if pair else kk[:, n * LANES:(n + 1) * LANES]

        def with_ones(vt_rows):
            return jnp.concatenate([vt_rows, jnp.ones((ACC_ROWS - HEAD_DIM, vt_rows.shape[1]), BF16)], axis=0)

        def step(j, carry, diagonal):
            rows = pl.ds(pl.multiple_of(j * blk, blk), blk)
            kk = k_ref[rows, :]
            vt = wide(vt_ref, sub * j)
            out = []
            for n in range(2):
                m, acc = carry[n]
                s = jnp.dot(k_of(kk, n), qts[n], preferred_element_type=F32)
                if has_bias:
                    s = s - LOG2E * _pick_lane(f_ref[rows, :], 2 * g + n)
                if diagonal:
                    s = _above_diagonal(s)
                m_new = jnp.maximum(m, jnp.max(s, axis=0, keepdims=True))
                p = jnp.exp2(s - m_new).astype(BF16)
                out.append((m_new, jnp.exp2(m - m_new) * acc
                            + jnp.dot(with_ones(vt[n * HEAD_DIM:(n + 1) * HEAD_DIM]), p, preferred_element_type=F32)))
            return tuple(out)

        def diagonal_in_halves(carry):
            h = tblk
            halves = [pl.ds(pl.multiple_of(i * blk, blk), h), pl.ds(pl.multiple_of(i * blk + h, h), h)]
            k_top, k_bot = k_ref[halves[0], :], k_ref[halves[1], :]
            vt_top, vt_bot = vt_ref[sub * i], vt_ref[sub * i + 1]
            out = []
            for n in range(2):
                m, acc = carry[n]
                heads = slice(n * HEAD_DIM, (n + 1) * HEAD_DIM)
                s_top = jnp.dot(k_of(k_top, n), qts[n], preferred_element_type=F32)
                s_bot = jnp.dot(k_of(k_bot, n), qts[n][:, h:], preferred_element_type=F32)
                if has_bias:
                    s_top = s_top - LOG2E * _pick_lane(f_ref[halves[0], :], 2 * g + n)
                    s_bot = s_bot - LOG2E * _pick_lane(f_ref[halves[1], :], 2 * g + n)
                s_top, s_bot = _above_diagonal(s_top), _above_diagonal(s_bot)
                m_top = jnp.maximum(m, jnp.max(s_top, axis=0, keepdims=True))
                m_new = jnp.concatenate([m_top[:, :h], jnp.maximum(m_top[:, h:], jnp.max(s_bot, axis=0, keepdims=True))], axis=1)
                p_top = jnp.exp2(s_top - m_new).astype(BF16)
                p_bot = jnp.exp2(s_bot - m_new[:, h:]).astype(BF16)
                late = jnp.concatenate([jnp.zeros((ACC_ROWS, h), F32),
                                        jnp.dot(with_ones(vt_bot[heads]), p_bot, preferred_element_type=F32)], axis=1)
                out.append((m_new, jnp.exp2(m - m_new) * acc
                            + jnp.dot(with_ones(vt_top[heads]), p_top, preferred_element_type=F32) + late))
            return tuple(out)

        init = tuple((jnp.full((1, blk), -jnp.inf, F32), jnp.zeros((ACC_ROWS, blk), F32)) for _ in range(2))
        carry = lax.fori_loop(0, i, lambda j, c: step(j, c, False), init)
        (ma, acca), (mb, accb) = diagonal_in_halves(carry) if sub == 2 else step(i, carry, True)
        la, lb = acca[HEAD_DIM:HEAD_DIM + 1], accb[HEAD_DIM:HEAD_DIM + 1]
        o_ref[...] = jnp.concatenate([acca[:HEAD_DIM] / la, accb[:HEAD_DIM] / lb], axis=0).T
        row = lax.broadcasted_iota(jnp.int32, (LANES, 1), 0)
        st_ref[0] = jnp.where(row == 0, ma + jnp.log2(la), jnp.where(row == 1, mb + jnp.log2(lb), 0.0)).T
        if side:
            @pl.when(step_id == steps - 1)
            def _():
                side.last(si, so, *sems)

    in_specs = [pl.BlockSpec((sub, w, tblk), lambda g, i: (i, qoff + g, 0)), pl.BlockSpec((t, w), lambda g, i: (0, koff + g)),
                pl.BlockSpec((t // tblk, LANES, tblk), lambda g, i: (0, voff + g, 0))]
    if has_bias:
        in_specs.append(pl.BlockSpec((t, LANES), lambda g, i: (0, 0)))
    s_in_specs, s_out_specs = side.specs() if side else ([], [])
    return pl.pallas_call(
        body, name=name, grid=(PAIRS, nb), in_specs=in_specs + s_in_specs,
        out_specs=[pl.BlockSpec((blk, LANES), lambda g, i: (i, g)), pl.BlockSpec((1, blk, LANES), lambda g, i: (g, i, 0))]
        + s_out_specs,
        out_shape=[jax.ShapeDtypeStruct((t, PAIRS * LANES), F32), jax.ShapeDtypeStruct((PAIRS, t, LANES), F32)] + list(s_outs),
        scratch_shapes=side.sems() if side else [],
        compiler_params=_params(("arbitrary", "arbitrary")),
    )(*ins, *s_ins)


def mix_norm(fo, mo, g_fo, g_mo, *, name, bt=512):
    t, d = fo.shape
    bt = min(bt, t)

    def body(fo_ref, mo_ref, gf_ref, gm_ref, o_ref):
        for n, (x_ref, g_ref) in enumerate(((fo_ref, gf_ref), (mo_ref, gm_ref))):
            xv = x_ref[...]
            r = lax.rsqrt(jnp.mean(xv * xv, axis=-1, keepdims=True) + EPS)
            o_ref[:, n * d:(n + 1) * d] = (xv * r * g_ref[...]).astype(BF16)

    row = pl.BlockSpec((bt, d), lambda i: (i, 0))
    vec = pl.BlockSpec((1, d), lambda i: (0, 0))
    return pl.pallas_call(
        body, name=name, grid=(t // bt,), in_specs=[row, row, vec, vec],
        out_specs=pl.BlockSpec((bt, 2 * d), lambda i: (i, 0)),
        out_shape=jax.ShapeDtypeStruct((t, 2 * d), BF16),
        compiler_params=_params(("parallel",)),
    )(fo, mo, g_fo, g_mo)


def mix_norm_bwd(dx1b, w_o, fo, mo, g_fo, g_mo, st_f, st_m, *, name, bt=512, side=None):
    t, d = fo.shape
    bt = min(bt, t)

    def body(dx_in_ref, w_ref, fo_ref, mo_ref, gf_ref, gm_ref, sf_ref, sm_ref, dfo_ref, dmo_ref, dgf_ref, dgm_ref,
             sfo_ref, smo_ref, dfot_ref, dmot_ref):
        @pl.when(pl.program_id(0) == 0)
        def _():
            dgf_ref[...] = jnp.zeros_like(dgf_ref)
            dgm_ref[...] = jnp.zeros_like(dgm_ref)

        dmixed = lax.dot_general(dx_in_ref[...], w_ref[...], NT, preferred_element_type=F32)
        groups = ((fo_ref, gf_ref, dfo_ref, dgf_ref, sf_ref, sfo_ref, dfot_ref),
                  (mo_ref, gm_ref, dmo_ref, dgm_ref, sm_ref, smo_ref, dmot_ref))
        for n, (x_ref, g_ref, dx_ref, dg_ref, st_ref, sto_ref, dxt_ref) in enumerate(groups):
            xv = x_ref[...]
            dhv = dmixed[:, n * d:(n + 1) * d]
            r = lax.rsqrt(jnp.mean(xv * xv, axis=-1, keepdims=True) + EPS)
            u = dhv * g_ref[...]
            dxf = r * u - xv * (r * r * r * jnp.mean(u * xv, axis=-1, keepdims=True))
            dxb = dxf.astype(BF16)
            dx_ref[...] = dxb
            dxt_ref[0] = dxf.T.astype(BF16)
            dg_ref[...] += jnp.sum(dhv * (xv * r), axis=0, keepdims=True)
            prod = xv * dxb.astype(F32)
            for g in range(PAIRS):
                grp = prod[:, g * LANES:(g + 1) * LANES]
                da = jnp.sum(jnp.where(_lane() < HEAD_DIM, grp, 0.0), axis=1, keepdims=True)
                db = jnp.sum(jnp.where(_lane() >= HEAD_DIM, grp, 0.0), axis=1, keepdims=True)
                sto_ref[g] = jnp.where(_lane() == 2, da, jnp.where(_lane() == 3, db, st_ref[g]))

    row = pl.BlockSpec((bt, d), lambda i: (i, 0))
    vec = pl.BlockSpec((1, d), lambda i: (0, 0))
    stat = pl.BlockSpec((PAIRS, bt, LANES), lambda i: (0, i, 0))
    return gridded(
        body, name=name, grid=(t // bt,),
        in_specs=[pl.BlockSpec((bt, dx1b.shape[1]), lambda i: (i, 0)), pl.BlockSpec(w_o.shape, lambda i: (0, 0)),
                  row, row, vec, vec, stat, stat],
        out_specs=[row, row, vec, vec, stat, stat] + [pl.BlockSpec((1, d, bt), lambda i: (i, 0, 0))] * 2,
        out_shape=[jax.ShapeDtypeStruct((t, d), BF16)] * 2 + [jax.ShapeDtypeStruct((1, d), F32)] * 2
        + [jax.ShapeDtypeStruct(st_f.shape, F32)] * 2 + [jax.ShapeDtypeStruct((t // bt, d, bt), BF16)] * 2,
        ins=[dx1b, w_o, fo, mo, g_fo, g_mo, st_f, st_m], semantics=("arbitrary",), side=side)


def flash_bwd(q_arr, qt_arr, k_arr, v_arr, do_arr, dot_arr, st, f_blocks, *, qoff, koff, voff, pair, scale, name, qblk=1024,
              side=None):
    t = q_arr.shape[0]
    blk = qt_arr.shape[2]
    qblk = max(min(qblk, t), blk)
    sub = qblk // blk
    nb, nbq = t // blk, t // qblk
    w = LANES if pair else 2 * LANES
    hw = w // 2
    has_bias = f_blocks is not None
    split = _halves if pair else _groups
    ins = [q_arr, qt_arr, k_arr, v_arr, do_arr, dot_arr, st] + ([f_blocks] if has_bias else [])
    n_out = 4 if has_bias else 3
    s_ins, s_outs = (side.ins, side.out_shapes) if side else ([], [])

    def wide(ref, first, count):
        parts = [ref[first + u] for u in range(count)]
        return parts[0] if count == 1 else jnp.concatenate(parts, axis=1)

    def body(*refs):
        main, si, outs, so, sems = _split_refs(refs, [len(ins), len(s_ins), n_out, len(s_outs), 2 if side else 0])
        q_ref, qt_ref, k_ref, v_ref, do_ref, dot_ref, st_ref = main[:7]
        dq_ref, dk_ref, dv_ref = outs[:3]
        g, j = pl.program_id(0), pl.program_id(1)
        step_id = g * nb + j
        if side:
            @pl.when(step_id == 0)
            def _():
                side.first(si, so, *sems)

        @pl.when(j == 0)
        def _():
            dq_ref[...] = jnp.zeros_like(dq_ref)

        kk, vv = k_ref[...], v_ref[...]
        ks = [kk, kk] if pair else _groups(kk)
        if has_bias:
            f_ref, df_ref = main[7], outs[3]
            fk = [LOG2E * _pick_row(f_ref[0], 2 * g + n) for n in range(2)]

            @pl.when(step_id == 0)
            def _():
                df_ref[...] = jnp.zeros_like(df_ref)

        def step(tb, count, carry, diagonal):
            rows = pl.ds(pl.multiple_of(tb * blk, blk), count * blk)
            qs = split((q_ref[rows, :].astype(F32) * (scale * LOG2E)).astype(BF16))
            qt = (wide(qt_ref, tb, count).astype(F32) * (scale * LOG2E)).astype(BF16)
            dos = [h.astype(BF16) for h in _halves(do_ref[rows, :].astype(F32))]
            dot = wide(dot_ref, tb, count)
            stats = st_ref[0, rows, :]
            new, dqs, row_sums = [], [], []
            for n in range(2):
                dkt, dvt, dfk = carry[n]
                s = lax.dot_general(qs[n], ks[n], NT, preferred_element_type=F32)
                if has_bias:
                    s = s - fk[n]
                if diagonal:
                    s = _below_diagonal(s)
                p = jnp.exp2(s - stats[:, n:n + 1])
                dp = lax.dot_general(dos[n], vv, NT, preferred_element_type=F32)
                ds = p * (dp - stats[:, 2 + n:3 + n])
                dsb = ds.astype(BF16)
                dvt = dvt + jnp.dot(dot[n * HEAD_DIM:(n + 1) * HEAD_DIM], p.astype(BF16), preferred_element_type=F32)
                dkt = dkt + jnp.dot(qt[n * hw:(n + 1) * hw], dsb, preferred_element_type=F32)
                dqs.append(jnp.dot(dsb, ks[n], preferred_element_type=F32))
                if has_bias:
                    dfk = dfk - jnp.sum(ds, axis=0, keepdims=True)
                    row_sums.append(jnp.sum(ds, axis=1, keepdims=True))
                new.append((dkt, dvt, dfk))
            dq = (jnp.where(_lane() < HEAD_DIM, dqs[0], dqs[1]) if pair else jnp.concatenate(dqs, axis=1)) * scale
            if has_bias:
                df_ref[rows, :] += jnp.where(_lane() == 2 * g, row_sums[0], jnp.where(_lane() == 2 * g + 1, row_sums[1], 0.0))
            dq_ref[rows, :] += dq
            return tuple(new)

        init = tuple((jnp.zeros((hw, blk), F32), jnp.zeros((HEAD_DIM, blk), F32), jnp.zeros((1, blk), F32)) for _ in range(2))
        carry = step(j, 1, init, True)
        whole = j // sub + 1
        carry = lax.fori_loop(j + 1, whole * sub, lambda tb, c: step(tb, 1, c, False), carry)
        (dka, dva, dfa), (dkb, dvb, dfb) = lax.fori_loop(whole, nbq, lambda i, c: step(i * sub, sub, c, False), carry)
        dk_ref[...] = (jnp.concatenate([dka, dkb], axis=0).T * LN2).astype(BF16)
        dv_ref[...] = jnp.concatenate([dva, dvb], axis=0).T.astype(BF16)
        if has_bias:
            row = lax.broadcasted_iota(jnp.int32, (LANES, 1), 0)
            by_head = jnp.where(row == 2 * g, dfa, jnp.where(row == 2 * g + 1, dfb, 0.0))
            df_ref[pl.ds(pl.multiple_of(j * blk, blk), blk), :] += by_head.T
        if side:
            @pl.when(step_id == PAIRS * nb - 1)
            def _():
                side.last(si, so, *sems)

    in_specs = [pl.BlockSpec((t, w), lambda g, j: (0, qoff + g)), pl.BlockSpec((nb, w, blk), lambda g, j: (0, qoff + g, 0)),
                pl.BlockSpec((blk, w), lambda g, j: (j, koff + g)), pl.BlockSpec((blk, LANES), lambda g, j: (j, voff + g)),
                pl.BlockSpec((t, LANES), lambda g, j: (0, g)), pl.BlockSpec((nb, LANES, blk), lambda g, j: (0, g, 0)),
                pl.BlockSpec((1, t, LANES), lambda g, j: (g, 0, 0))]
    out_specs = [pl.BlockSpec((t, w), lambda g, j: (0, g)), pl.BlockSpec((blk, w), lambda g, j: (j, g)),
                 pl.BlockSpec((blk, LANES), lambda g, j: (j, g))]
    out_shape = [jax.ShapeDtypeStruct((t, PAIRS * w), F32), jax.ShapeDtypeStruct((t, PAIRS * w), BF16),
                 jax.ShapeDtypeStruct((t, PAIRS * LANES), BF16)]
    if has_bias:
        in_specs.append(pl.BlockSpec((1, N_HEADS, blk), lambda g, j: (j, 0, 0)))
        out_specs.append(pl.BlockSpec((t, LANES), lambda g, j: (0, 0)))
        out_shape.append(jax.ShapeDtypeStruct((t, LANES), F32))
    s_in_specs, s_out_specs = side.specs() if side else ([], [])
    return pl.pallas_call(
        body, name=name, grid=(PAIRS, nb), in_specs=in_specs + s_in_specs, out_specs=out_specs + s_out_specs,
        out_shape=out_shape + list(s_outs), scratch_shapes=side.sems() if side else [],
        compiler_params=_params(("arbitrary", "arbitrary")),
    )(*ins, *s_ins)


def _adamw_math(w, g, m, v):
    nm = ADAM_B1 * m + (1.0 - ADAM_B1) * g
    nv = ADAM_B2 * v + (1.0 - ADAM_B2) * (g * g)
    m_hat = nm / (1.0 - ADAM_B1 ** ADAM_STEP)
    v_hat = nv / (1.0 - ADAM_B2 ** ADAM_STEP)
    return -ADAM_LR * (m_hat / (jnp.sqrt(v_hat) + ADAM_EPS) + ADAM_WD * w), nm, nv


def adamw(w, g, m, v, *, name):
    rws, cols = w.shape
    br = _row_block(rws)

    def body(w_ref, g_ref, m_ref, v_ref, d_ref, nm_ref, nv_ref):
        d_ref[...], nm_ref[...], nv_ref[...] = _adamw_math(w_ref[...], g_ref[...], m_ref[...], v_ref[...])

    blk = pl.BlockSpec((br, cols), lambda i: (i, 0))
    return pl.pallas_call(
        body, name=name, grid=(rws // br,), in_specs=[blk] * 4, out_specs=[blk] * 3,
        out_shape=[jax.ShapeDtypeStruct((rws, cols), F32)] * 3,
        compiler_params=_params(("parallel",)),
    )(w, g, m, v)


def adamw_halves(w, g_mine, g_other, m, v, core, *, name):
    _, k, n = w.shape
    br = _row_block(k // 2)
    nh = k // 2 // br

    def body(c_ref, w_ref, gm_ref, go_ref, m_ref, v_ref, g_out, d_ref, nm_ref, nv_ref):
        gv = jnp.where(pl.program_id(0) == c_ref[0], gm_ref[...], go_ref[...])
        g_out[0] = gv
        d_ref[0], nm_ref[0], nv_ref[0] = _adamw_math(w_ref[0], gv, m_ref[0], v_ref[0])

    full = pl.BlockSpec((1, br, n), lambda hb, i, c: (0, hb * nh + i, 0))
    half = pl.BlockSpec((br, n), lambda hb, i, c: (i, 0))
    return pl.pallas_call(
        body, name=name,
        grid_spec=pltpu.PrefetchScalarGridSpec(num_scalar_prefetch=1, grid=(2, nh), in_specs=[full, half, half, full, full],
                                               out_specs=[full] * 4),
        out_shape=[jax.ShapeDtypeStruct(w.shape, F32)] * 4,
        compiler_params=_params(("parallel", "parallel")),
    )(core, w, g_mine, g_other, m, v)


def adamw_halves_t(wt, gt_mine, gt_other, mt, vt, core, *, name, bc=128):
    n, k = wt.shape
    nh = k // 2 // bc

    def body(c_ref, w_ref, gm_ref, go_ref, m_ref, v_ref, g_out, d_ref, nm_ref, nv_ref):
        gv = jnp.where(pl.program_id(0) == c_ref[0], gm_ref[...], go_ref[...])
        g_out[...] = gv
        d_ref[...], nm_ref[...], nv_ref[...] = _adamw_math(w_ref[...], gv, m_ref[...], v_ref[...])

    full = pl.BlockSpec((n, bc), lambda hb, i, c: (0, hb * nh + i))
    half = pl.BlockSpec((n, bc), lambda hb, i, c: (0, i))
    return pl.pallas_call(
        body, name=name,
        grid_spec=pltpu.PrefetchScalarGridSpec(num_scalar_prefetch=1, grid=(2, nh), in_specs=[full, half, half, full, full],
                                               out_specs=[full] * 4),
        out_shape=[jax.ShapeDtypeStruct(wt.shape, F32)] * 4,
        compiler_params=_params(("parallel", "parallel")),
    )(core, wt, gt_mine, gt_other, mt, vt)


def add_pair(dw, recv, core, *, name):
    n4, k, n = dw.shape
    half = (1, k // 2, n) if split_axis(k) == 0 else (1, k, n // 2)
    mine = (lambda q, c: (q, c[0], 0)) if split_axis(k) == 0 else (lambda q, c: (q, 0, c[0]))

    def body(c_ref, a_ref, b_ref, o_ref):
        o_ref[...] = (a_ref[...] + b_ref[...].astype(F32)).astype(BF16)

    return pl.pallas_call(
        body, name=name,
        grid_spec=pltpu.PrefetchScalarGridSpec(
            num_scalar_prefetch=1, grid=(n4,),
            in_specs=[pl.BlockSpec(half, mine), pl.BlockSpec(half, lambda q, c: (q, 0, 0))],
            out_specs=pl.BlockSpec(half, lambda q, c: (q, 0, 0))),
        out_shape=jax.ShapeDtypeStruct((n4,) + half[1:], BF16),
        compiler_params=_params(("parallel",)),
    )(core, dw, recv)


def sum_chips(parts, *, name):
    n4, r, n = parts.shape
    if r % 16 == 0:
        br, bc = _row_block(r), n
    else:
        br, bc = r, LANES

    def body(p_ref, o_ref):
        acc = p_ref[0].astype(F32)
        for q in range(1, n4):
            acc = acc + p_ref[q].astype(F32)
        o_ref[...] = acc

    return pl.pallas_call(
        body, name=name, grid=(r // br, n // bc),
        in_specs=[pl.BlockSpec((n4, br, bc), lambda i, j: (0, i, j))], out_specs=pl.BlockSpec((br, bc), lambda i, j: (i, j)),
        out_shape=jax.ShapeDtypeStruct((r, n), F32),
        compiler_params=_params(("parallel", "parallel")),
    )(parts)


ANY = pl.BlockSpec(memory_space=pl.ANY)


def _place():
    x, y, c = lax.axis_index("x"), lax.axis_index("y"), lax.axis_index("c")
    chips = [(1 - x, y), (x, 1 - y), (1 - x, 1 - y)]
    return x, y, c, chips


def _copy(src, dst, send_sems, recv_sems, k, to):
    return pltpu.make_async_remote_copy(src_ref=src, dst_ref=dst, send_sem=send_sems.at[k], recv_sem=recv_sems.at[k],
                                        device_id=to, device_id_type=MESH)


def split_axis(rows):
    return 0 if rows % 32 == 0 else 1


def _half(ref, lead, hf):
    rows, cols = ref.shape[-2:]
    if split_axis(rows) == 0:
        at = (pl.ds(hf * (rows // 2), rows // 2), slice(None))
    else:
        at = (slice(None), pl.ds(hf * (cols // 2), cols // 2))
    return ref.at[at] if lead is None else ref.at[(lead,) + at]


def _gather_first(srcs, dsts, ssems, rsems):
    x, y, c, chips = _place()
    for ti, (s, d) in enumerate(zip(srcs, dsts)):
        for j, (cx, cy) in enumerate(chips):
            _copy(_half(s, None, c), _half(d, 2 * x + y, c), ssems, rsems, 3 * ti + j, (cx, cy, c)).start()


def _gather_mid(srcs, dsts, ssems, rsems):
    x, y, c, chips = _place()
    n1 = 3 * len(srcs)
    for ti, d in enumerate(dsts):
        for j, (cx, cy) in enumerate(chips):
            landed = _half(d, 2 * cx + cy, c)
            _copy(landed, landed, ssems, rsems, 3 * ti + j, (cx, cy, c)).wait_recv()
            _copy(landed, landed, ssems, rsems, n1 + 3 * ti + j, (x, y, 1 - c)).start()


def _gather_last(srcs, dsts, ssems, rsems):
    x, y, c, chips = _place()
    n1 = 3 * len(srcs)
    for ti, (s, d) in enumerate(zip(srcs, dsts)):
        for j, (cx, cy) in enumerate(chips):
            other = _half(d, 2 * cx + cy, 1 - c)
            _copy(other, other, ssems, rsems, n1 + 3 * ti + j, (x, y, 1 - c)).wait_recv()
        for j, (cx, cy) in enumerate(chips):
            mine = _half(s, None, c)
            _copy(mine, mine, ssems, rsems, 3 * ti + j, (cx, cy, c)).wait_send()
            _copy(mine, mine, ssems, rsems, n1 + 3 * ti + j, (x, y, 1 - c)).wait_send()


def gather_side(shards):
    return Side(shards, [jax.ShapeDtypeStruct((N_CHIPS,) + s.shape, s.dtype) for s in shards], 6 * len(shards),
                _gather_first, _gather_last, _gather_mid)


def _scatter_first(srcs, dsts, ssems, rsems):
    x, y, c, chips = _place()
    for ti, (s, d) in enumerate(zip(srcs, dsts)):
        for j, (cx, cy) in enumerate(chips):
            _copy(s.at[2 * cx + cy], d.at[2 * x + y], ssems, rsems, 3 * ti + j, (cx, cy, c)).start()


def _scatter_last(srcs, dsts, ssems, rsems):
    x, y, c, chips = _place()
    for ti, (s, d) in enumerate(zip(srcs, dsts)):
        for j, (cx, cy) in enumerate(chips):
            _copy(s.at[2 * cx + cy], d.at[2 * cx + cy], ssems, rsems, 3 * ti + j, (cx, cy, c)).wait_recv()
        for j, (cx, cy) in enumerate(chips):
            _copy(s.at[2 * cx + cy], d.at[2 * cx + cy], ssems, rsems, 3 * ti + j, (cx, cy, c)).wait_send()


def scatter_side(parts):
    return Side(parts, [jax.ShapeDtypeStruct(p.shape, p.dtype) for p in parts], 3 * len(parts), _scatter_first, _scatter_last)


def run_side(side, *, name):
    n_in, n_out = len(side.ins), len(side.out_shapes)

    def body(*refs):
        si, so, sems = _split_refs(refs, [n_in, n_out, 2])
        side.first(si, so, *sems)
        if side.mid is not None:
            side.mid(si, so, *sems)
        side.last(si, so, *sems)

    in_specs, out_specs = side.specs()
    return pl.pallas_call(body, name=name, in_specs=in_specs, out_specs=out_specs, out_shape=side.out_shapes,
                          scratch_shapes=side.sems())(*side.ins)


def _swap_first(srcs, dsts, ssems, rsems):
    x, y, c, _ = _place()
    for k, (s, d) in enumerate(zip(srcs, dsts)):
        _copy(s, d, ssems, rsems, k, (x, y, 1 - c)).start()


def _swap_last(srcs, dsts, ssems, rsems):
    x, y, c, _ = _place()
    for k, (s, d) in enumerate(zip(srcs, dsts)):
        _copy(s, d, ssems, rsems, k, (x, y, 1 - c)).wait()


def swap_side(xs):
    return Side(xs, [jax.ShapeDtypeStruct(a.shape, a.dtype) for a in xs], len(xs), _swap_first, _swap_last)


def _swap_halves(srcs, dsts, ssems, rsems):
    x, y, c, _ = _place()
    for k, (s, d) in enumerate(zip(srcs, dsts)):
        hk = s.shape[1] // 2
        yield _copy(s.at[:, pl.ds((1 - c) * hk, hk), :], d, ssems, rsems, k, (x, y, 1 - c))


def _swap_halves_first(srcs, dsts, ssems, rsems):
    for cp in _swap_halves(srcs, dsts, ssems, rsems):
        cp.start()


def _swap_halves_last(srcs, dsts, ssems, rsems):
    for cp in _swap_halves(srcs, dsts, ssems, rsems):
        cp.wait()


def swap_halves_side(xs):
    return Side(xs, [jax.ShapeDtypeStruct((a.shape[0], a.shape[1] // 2, a.shape[2]), a.dtype) for a in xs], len(xs),
                _swap_halves_first, _swap_halves_last)


def allreduce_small(s):
    n_dev = 8

    def body(s_ref, out_ref, buf, send_sems, recv_sems):
        x, y, c, _ = _place()
        me = 4 * x + 2 * y + c
        buf[me] = s_ref[...]
        sends = []
        for k in range(1, n_dev):
            px = 1 - x if k & 4 else x
            py = 1 - y if k & 2 else y
            pc = 1 - c if k & 1 else c
            cp = _copy(s_ref, buf.at[me], send_sems, recv_sems, k - 1, (px, py, pc))
            cp.start()
            sends.append((cp, 4 * px + 2 * py + pc))
        for k, (cp, peer) in enumerate(sends):
            _copy(s_ref, buf.at[peer], send_sems, recv_sems, k, (x, y, c)).wait_recv()
        for cp, _ in sends:
            cp.wait_send()
        acc = buf[0]
        for d in range(1, n_dev):
            acc = acc + buf[d]
        out_ref[...] = acc

    vm = pl.BlockSpec(memory_space=pltpu.VMEM)
    return pl.pallas_call(
        body, name="allreduce_small", in_specs=[vm], out_specs=vm,
        out_shape=jax.ShapeDtypeStruct(s.shape, F32),
        scratch_shapes=[pltpu.VMEM((n_dev,) + s.shape, F32), pltpu.SemaphoreType.DMA((n_dev - 1,)),
                        pltpu.SemaphoreType.DMA((n_dev - 1,))],
    )(s)


def join_cols(sm):
    n4, k, n = sm.shape
    return sm.transpose(1, 0, 2).reshape(k, n4 * n)


def split_cols(full):
    k, n = full.shape
    return full.reshape(k, N_CHIPS, n // N_CHIPS).transpose(1, 0, 2)


def _pad_heads(w, width):
    lead, heads = w.shape[:-1], w.shape[-1] // width
    w = w.reshape(lead + (heads, width))
    return jnp.pad(w, [(0, 0)] * len(lead) + [(0, 0), (0, LANES - width)]).reshape(lead + (heads * LANES,))


def _unpad_heads(w, width):
    lead, heads = w.shape[:-1], w.shape[-1] // LANES
    return w.reshape(lead + (heads, LANES))[..., :width].reshape(lead + (heads * width,))


O_F = 3 * FW
O_CQ = O_F + N_HEADS
O_CKV = O_CQ + Q_RANK
O_KR = O_CKV + KV_RANK
O_END = O_KR + ROPE_DIM


def _shard_rows(sm, a, b):
    r = sm.shape[1]
    out = []
    while a < b:
        q = a // r
        e = min(b, (q + 1) * r)
        out.append(sm[q, a - q * r:e - q * r])
        a = e
    return out


def split_w_in_t(w_sm):
    d = w_sm.shape[2]

    def z(n):
        return [jnp.zeros((n, d), w_sm.dtype)]

    small = (_shard_rows(w_sm, O_CQ, O_CKV) + _shard_rows(w_sm, O_CKV, O_KR) + z(HEAD_DIM) + _shard_rows(w_sm, O_KR, O_END)
             + z(LANES - HEAD_DIM - ROPE_DIM) + _shard_rows(w_sm, O_F, O_CQ) + z(LANES - N_HEADS) + z(LANES))
    return jnp.concatenate(_shard_rows(w_sm, 0, O_F), axis=0), jnp.concatenate(small, axis=0)


def join_w_in_t(d_qkv_t, d_small_t):
    kr = S_KR + HEAD_DIM
    segments = [(part, 0, n * FW, FW) for n, part in enumerate(d_qkv_t)]
    segments += [(d_small_t, S_F, O_F, N_HEADS), (d_small_t, S_CQ, O_CQ, Q_RANK),
                 (d_small_t, S_CKV, O_CKV, KV_RANK), (d_small_t, kr, O_KR, ROPE_DIM)]
    r = O_END // N_CHIPS
    shards = []
    for q in range(N_CHIPS):
        pieces = []
        for src, s0, v0, n in segments:
            a, b = max(v0, q * r), min(v0 + n, (q + 1) * r)
            if a < b:
                pieces.append(src[s0 + a - v0:s0 + b - v0])
        shards.append(jnp.concatenate(pieces, axis=0))
    return jnp.stack(shards)


def rope_inputs(pos):
    inv_freq = ROPE_THETA ** (-jnp.arange(0, ROPE_DIM, 2, dtype=F32) / ROPE_DIM)
    inv_row = jnp.concatenate([jnp.zeros((HEAD_DIM,), F32), inv_freq, inv_freq, jnp.zeros((LANES - HEAD_DIM - ROPE_DIM,), F32)])
    return pos.astype(F32)[:, None], inv_row[None, :]


def _pad_lanes(v, n):
    return jnp.pad(v, ((0, 0), (0, n - v.shape[1])))


ATTN_BLK = 512
ATTN_FWD_BLK = 1024
ATTN_BWD_QBLK = 1024
ACC_ROWS = HEAD_DIM + 16


def local_step(xs, pos, tgt, gains, early_weights, late_weights, early_side=None, fwd_sides=(None, None), reduction=None):
    g_attn, b_forget, g_q, g_kv, g_fo, g_mo, g_mlp, g_fin = gains
    t = xs.shape[0]
    blk = min(ATTN_BLK, t)
    fox_scale = 1.0 / (HEAD_DIM ** 0.5)
    mla_scale = 1.0 / ((HEAD_DIM + ROPE_DIM) ** 0.5)

    h1, *gathered = rmsnorm(xs, g_attn, out_dtype=BF16, name="norm_attn", side=early_side)
    w_in_t, w_uq_p, w_ukv = early_weights(gathered)
    w_qkv_t, w_small_t = split_w_in_t(w_in_t)
    kv = w_ukv.reshape(KV_RANK, N_HEADS, 2 * HEAD_DIM)
    w_ukv_p = jnp.concatenate([_pad_heads(kv[:, :, :HEAD_DIM].reshape(KV_RANK, FW), HEAD_DIM),
                               kv[:, :, HEAD_DIM:].reshape(KV_RANK, FW)], axis=1)
    b_f = _pad_lanes(b_forget, LANES)
    pos_col, inv_row = rope_inputs(pos)

    qkv, qkv_t = mm(h1, w_qkv_t, trans_b=True, out_dtypes=[BF16], t_blk=blk, name="proj_qkv", bn=1536)
    small, = mm(h1, w_small_t, trans_b=True, out_dtypes=[F32], name="proj_small")
    mq, mk, mv, lf, cqn, ckvn, mq_t, mv_t = mla_prep(small, g_q, g_kv, w_uq_p, w_ukv_p, pos_col, inv_row, b_f,
                                                     name="mla_prep", bt=blk)
    f_cum = cumsum_rows(lf, reverse=False, name="gate_cumsum")
    f_blocks = f_cum[:, :N_HEADS].reshape(t // blk, blk, N_HEADS).transpose(0, 2, 1)
    fo, st_f, *gathered = flash_fwd(qkv_t, qkv, qkv_t, f_cum, qoff=0, koff=PAIRS, voff=2 * PAIRS, pair=True,
                                    scale=fox_scale, name="fox_fwd", blk=ATTN_FWD_BLK, side=fwd_sides[0])
    mo, st_m, *more = flash_fwd(mq_t, mk, mv_t, None, qoff=0, koff=0, voff=0, pair=False, scale=mla_scale, name="mla_fwd",
                                blk=ATTN_FWD_BLK, side=fwd_sides[1])
    w_o, w_up, w_down = late_weights(gathered + more)
    mixed = mix_norm(fo, mo, g_fo, g_mo, name="norm_mix")

    def inv_rms(v):
        return lax.rsqrt(jnp.mean(v * v, axis=-1, keepdims=True) + EPS)

    def residual_then_norm(acc, res, g):
        xn = acc + res
        return xn, xn * inv_rms(xn) * g

    def norm_bwd(dh, xn, res, g):
        r = inv_rms(xn)
        uu = dh * g
        return (r * uu - xn * (r * r * r * jnp.mean(uu * xn, axis=-1, keepdims=True)) + res,
                jnp.sum(dh * (xn * r), axis=0, keepdims=True))

    def norm_bwd2(dh, xn, res, g):
        dx, dg = norm_bwd(dh, xn, res, g)
        return dx, dx, dg

    def residual_then_loss(acc, res, target, g):
        xn = acc + res
        r = inv_rms(xn)
        xh = xn * r
        e = xh * g - target
        part = 0.5 * jnp.sum(jnp.mean(e * e, axis=-1, keepdims=True), axis=0, keepdims=True)
        dy = e * (1.0 / xn.shape[1])
        uu = dy * g
        dx = r * uu - xn * (r * r * r * jnp.mean(uu * xn, axis=-1, keepdims=True))
        return dx, dx, jnp.sum(dy * xh, axis=0, keepdims=True), part + jnp.zeros_like(g)

    x1, h2 = mm(mixed, w_o, extras=[xs], vecs=[g_mlp], epilogue=residual_then_norm, out_dtypes=[F32, BF16], name="out_proj")

    def relu2(uu):
        r = jnp.maximum(uu.astype(F32), 0.0)
        return (r * r).astype(BF16)

    u, = mm(h2, w_up, out_dtypes=[BF16], name="mlp_up", bn=2048)
    dx2, dx2b, dg_fin, loss_row = mm(u, w_down, a_pro=relu2, extras=[x1, tgt], vecs=[g_fin], epilogue=residual_then_loss,
                                     out_dtypes=[F32, BF16], n_sums=2, name="mlp_down_loss")
    loss = loss_row[:, :1]

    def relu2_grad(acc, uu):
        return (acc * (2.0 * jnp.maximum(uu.astype(F32), 0.0)),)

    du, = mm(dx2b, w_down, trans_b=True, extras=[u], epilogue=relu2_grad, out_dtypes=[BF16], name="mlp_down_bwd", bn=2048)
    dw_down, dw_down_b = (g.reshape(N_CHIPS, -1, w_down.shape[1])
                          for g in mm_tn(u, dx2b, a_pro=relu2, name="dw_down", bf16_copy=True))
    dx1, dx1b, dg_mlp = mm(du, w_up, trans_b=True, extras=[x1, dx2], vecs=[g_mlp], epilogue=norm_bwd2,
                           out_dtypes=[F32, BF16], n_sums=1, name="mlp_up_bwd")
    dw_up, dw_up_b = mm_tn(h2, du, name="dw_up", col_shards=N_CHIPS, bf16_copy=True)

    dw_o, dw_o_b = (g.reshape(N_CHIPS, -1, w_o.shape[1]) for g in mm_tn(mixed, dx1b, name="dw_o", bf16_copy=True))
    late, late_b = (dw_o, dw_up, dw_down), (dw_o_b, dw_up_b, dw_down_b)
    red = reduction
    dfo, dmo, dg_fo, dg_mo, st_f, st_m, dfo_t, dmo_t, *got = mix_norm_bwd(
        dx1b, w_o, fo, mo, g_fo, g_mo, st_f, st_m, name="out_proj_mix_bwd", bt=blk,
        side=red.late_swap(late, late_b) if red else None)
    dfq, dfk, dfv, d_f, *got = flash_bwd(
        qkv, qkv_t, qkv, qkv, dfo, dfo_t, st_f, f_blocks, qoff=0, koff=PAIRS, voff=2 * PAIRS, pair=True,
        scale=fox_scale, name="fox_bwd", qblk=ATTN_BWD_QBLK, side=red.late_scatter(got) if red else None)
    dmq, dmk, dmv, *got = flash_bwd(mq, mq_t, mk, mv, dmo, dmo_t, st_m, None, qoff=0, koff=0, voff=0,
                                    pair=False, scale=mla_scale, name="mla_bwd", qblk=ATTN_BWD_QBLK,
                                    side=red.late_halves(got) if red else None)
    if red:
        red.late_done(got)
    dlf = cumsum_rows(d_f, reverse=True, name="gate_cumsum_bwd")
    dsmall, dq_u, dkv_u, dg_q, dg_kv, db_f = mla_prep_bwd(dmq, dmk, dmv, dlf, small, g_q, g_kv, w_uq_p, w_ukv_p,
                                                          pos_col, inv_row, b_f, name="mla_prep_bwd")
    dw_uq_p = mm_tn(cqn, dq_u, name="dw_uq")
    dw_ukv_p = mm_tn(ckvn, dkv_u, name="dw_ukv")

    def to_bf16(tile):
        return tile.astype(BF16)

    dqkv = [dfq, dfk, dfv]
    dw_in_t = join_w_in_t([mm_tn(part, h1, a_pro=to_bf16, name="dw_" + nm) for part, nm in zip(dqkv, "qkv")],
                          mm_tn(dsmall, h1, name="dw_small"))
    dk_cols = _unpad_heads(dw_ukv_p[:, :HW], HEAD_DIM).reshape(KV_RANK, N_HEADS, HEAD_DIM)
    dv_cols = dw_ukv_p[:, HW:].reshape(KV_RANK, N_HEADS, HEAD_DIM)
    dw_ukv = jnp.concatenate([dk_cols, dv_cols], axis=2).reshape(KV_RANK, N_HEADS * 2 * HEAD_DIM)
    early = (dw_in_t, split_cols(dw_uq_p), split_cols(dw_ukv))
    w_parts = [w_qkv_t[n * FW:(n + 1) * FW] for n in range(3)]
    grad_x, dg_attn, *got = mm(dqkv + [dsmall], w_parts + [w_small_t], a_pro=to_bf16, extras=[xs, dx1], vecs=[g_attn],
                               epilogue=norm_bwd, out_dtypes=[F32], n_sums=1, name="proj_bwd",
                               side=red.early_scatter(early) if red else None)
    if red:
        red.early_done(got)
    d_gains = (dg_attn, db_f[:, :N_HEADS], dg_q, dg_kv, dg_fo, dg_mo, dg_mlp, dg_fin)
    return loss, grad_x, early, late, d_gains


class GradReduction:
    def __init__(self, core_id, chip):
        self.core_id, self.chip = core_id, chip
        self.core = core_id.reshape(1).astype(jnp.int32)
        self.grads, self.pairs, self.halves, self.others = {}, {}, {}, {}

    def _other_halves(self, grads):
        out = []
        for g in grads:
            axis = 1 + split_axis(g.shape[1])
            size = g.shape[axis] // 2
            out.append(lax.dynamic_slice_in_dim(g, (1 - self.core_id) * size, size, axis=axis).astype(BF16))
        return out

    def _add_pairs(self, group, recvs):
        self.pairs[group] = [add_pair(g, r, self.core, name="add_pair_%s_%d" % (group, n))
                             for n, (g, r) in enumerate(zip(self.grads[group], recvs))]
        return scatter_side(self.pairs[group])

    def _chip_sums(self, group, scattered):
        chip = self.chip
        with_mine = [lax.dynamic_update_index_in_dim(s, lax.dynamic_index_in_dim(p, chip, 0, keepdims=True), chip, 0)
                     for s, p in zip(scattered, self.pairs[group])]
        self.halves[group] = [sum_chips(s, name="sum_chips_%s_%d" % (group, n)) for n, s in enumerate(with_mine)]
        return self.halves[group]

    def late_swap(self, grads, bf16_copies):
        self.grads["late"] = list(grads)
        return swap_halves_side(list(bf16_copies))

    def late_scatter(self, recvs):
        return self._add_pairs("late", recvs)

    def late_halves(self, scattered):
        return swap_side(self._chip_sums("late", scattered))

    def late_done(self, others):
        self.others["late"] = list(others)

    def early_scatter(self, grads):
        self.grads["early"] = list(grads)
        return self._add_pairs("early", run_side(swap_side(self._other_halves(grads)), name="swap_early_sends"))

    def early_done(self, scattered):
        self.others["early"] = list(run_side(swap_side(self._chip_sums("early", scattered)), name="swap_early_halves"))

def kernel(x, positions, attn_norm_g, w_in, b_forget, q_norm_g, w_uq, kv_norm_g, w_ukv, fox_out_g, mla_out_g, w_o, mlp_norm_g, w_up, w_down, final_norm_g, loss_target, m_attn_norm_g, m_w_in, m_b_forget, m_q_norm_g, m_w_uq, m_kv_norm_g, m_w_ukv, m_fox_out_g, m_mla_out_g, m_w_o, m_mlp_norm_g, m_w_up, m_w_down, m_final_norm_g, v_attn_norm_g, v_w_in, v_b_forget, v_q_norm_g, v_w_uq, v_kv_norm_g, v_w_ukv, v_fox_out_g, v_mla_out_g, v_w_o, v_mlp_norm_g, v_w_up, v_w_down, v_final_norm_g):
    core_id = lax.axis_index("c")
    core = core_id.reshape(1).astype(jnp.int32)
    chip = 2 * lax.axis_index("x") + lax.axis_index("y")
    big = [w_in, w_uq, w_ukv, w_o, w_up, w_down]
    big_m = [m_w_in, m_w_uq, m_w_ukv, m_w_o, m_w_up, m_w_down]
    big_v = [v_w_in, v_w_uq, v_w_ukv, v_w_o, v_w_up, v_w_down]
    n_early = 3

    def vec(a):
        return a.reshape(1, -1)

    small = [attn_norm_g, b_forget, q_norm_g, kv_norm_g, fox_out_g, mla_out_g, mlp_norm_g, final_norm_g]
    small_m = [m_attn_norm_g, m_b_forget, m_q_norm_g, m_kv_norm_g, m_fox_out_g, m_mla_out_g, m_mlp_norm_g, m_final_norm_g]
    small_v = [v_attn_norm_g, v_b_forget, v_q_norm_g, v_kv_norm_g, v_fox_out_g, v_mla_out_g, v_mlp_norm_g, v_final_norm_g]
    gains = [vec(a) for a in small]

    views = [big[0][0].T, _pad_heads(big[1][0], HEAD_DIM + ROPE_DIM)] + [w[0] for w in big[2:]]
    shards = [v.astype(BF16) for v in views]

    def with_own(gathered, mine):
        return [lax.dynamic_update_index_in_dim(g, s, chip, 0) for g, s in zip(gathered, mine)]

    def early_weights(gathered):
        g_in, g_uq, g_ukv = with_own(gathered, shards[:n_early])
        return g_in, join_cols(g_uq), join_cols(g_ukv)

    def late_weights(gathered):
        g_o, g_up, g_down = with_own(gathered, shards[n_early:])
        return g_o.reshape(-1, g_o.shape[2]), join_cols(g_up), g_down.reshape(-1, g_down.shape[2])

    reduction = GradReduction(core_id, chip)
    loss, grad_x, _, _, d_small = local_step(
        x[0], positions[0], loss_target[0], gains, early_weights, late_weights, gather_side(shards[:n_early]),
        (gather_side(shards[n_early:-1]), gather_side(shards[-1:])), reduction)
    halves = reduction.halves["early"] + reduction.halves["late"]
    others = reduction.others["early"] + reduction.others["late"]

    def rows8(vs):
        return jnp.concatenate([_pad_lanes(vec(a).astype(F32), 1024) for a in vs], axis=0)

    with_loss = [jnp.concatenate([d, loss], axis=1) if n == 1 else d for n, d in enumerate(d_small)]
    g_small8 = allreduce_small(rows8(with_loss))

    outs_big = []
    for n, (w, gm, go, m, v) in enumerate(zip(big, halves, others, big_m, big_v)):
        if n == 0:
            outs = adamw_halves_t(w[0].T, gm, go, m[0].T, v[0].T, core, name="adamw_%d" % n)
            outs_big.append([o.T[None] for o in outs])
        else:
            if n == 1:
                gm, go = (_unpad_heads(gh, HEAD_DIM + ROPE_DIM) for gh in (gm, go))
            outs_big.append(adamw_halves(w, gm, go, m, v, core, name="adamw_%d" % n))
    d8, m8, v8 = adamw(rows8(small), g_small8, rows8(small_m), rows8(small_v), name="adamw_small")

    def unrows8(a8):
        return [a8[n, :s.size].reshape(s.shape) for n, s in enumerate(small)]

    loss_all = g_small8[1, N_HEADS]
    grads, deltas, new_m, new_v = [None] * 14, [None] * 14, [None] * 14, [None] * 14
    big_at = [1, 4, 6, 9, 11, 12]
    small_at = [0, 2, 3, 5, 7, 8, 10, 13]
    for n, at in enumerate(big_at):
        grads[at], deltas[at], new_m[at], new_v[at] = outs_big[n]
    for at, g, dd, mm_, vv in zip(small_at, unrows8(g_small8), unrows8(d8), unrows8(m8), unrows8(v8)):
        grads[at], deltas[at], new_m[at], new_v[at] = g, dd, mm_, vv
    return (loss_all, grad_x[None], *grads, *deltas, *new_m, *new_v)
```

```python
import jax
import jax.numpy as jnp
from jax import lax
from jax.experimental import pallas as pl
from jax.experimental.pallas import tpu as pltpu

F32 = jnp.float32
BF16 = jnp.bfloat16
MESH = pl.DeviceIdType.MESH

EPS = 1e-6
ROPE_THETA = 10000.0
N_HEADS = 8
PAIRS = N_HEADS // 2
HEAD_DIM = 64
ROPE_DIM = 32
LANES = 128
Q_RANK = 384
KV_RANK = 256
N_CHIPS = 4
ADAM_LR, ADAM_B1, ADAM_B2, ADAM_EPS, ADAM_WD, ADAM_STEP = 0.001, 0.9, 0.999, 1e-08, 0.01, 10
VMEM_LIMIT = 48 * 1024 * 1024
LOG2E = 1.4426950408889634
LN2 = 0.6931471805599453
NN = (((1,), (0,)), ((), ()))
NT = (((1,), (1,)), ((), ()))
TN = (((0,), (0,)), ((), ()))


def _params(sem=None):
    return pltpu.CompilerParams(dimension_semantics=sem, vmem_limit_bytes=VMEM_LIMIT)


def _fit(block, dim):
    if dim <= block:
        return dim
    return next(b for b in range(block - block % LANES, 0, -LANES) if dim % b == 0)


def _row_block(rows):
    return next(b for b in (256, 128, 64, 32, 16, 8) if rows % b == 0)


def gridded(body, *, name, grid, in_specs, out_specs, out_shape, ins, semantics, side=None):
    if side is None:
        return pl.pallas_call(body, name=name, grid=grid, in_specs=in_specs, out_specs=out_specs, out_shape=out_shape,
                              compiler_params=_params(semantics))(*ins)
    n_in, n_out = len(in_specs), len(out_specs)
    steps = 1
    for extent in grid:
        steps *= extent

    def riding(*refs):
        main_in, s_in, main_out, s_out, sems = _split_refs(refs, [n_in, len(side.ins), n_out, len(side.out_shapes), 2])
        step = 0
        for axis, extent in enumerate(grid):
            step = step * extent + pl.program_id(axis)

        @pl.when(step == 0)
        def _():
            side.first(s_in, s_out, *sems)

        body(*main_in, *main_out)

        @pl.when(step == steps - 1)
        def _():
            if side.mid is not None:
                side.mid(s_in, s_out, *sems)
            side.last(s_in, s_out, *sems)

    s_in_specs, s_out_specs = side.specs()
    return pl.pallas_call(
        riding, name=name, grid=grid, in_specs=list(in_specs) + s_in_specs, out_specs=list(out_specs) + s_out_specs,
        out_shape=list(out_shape) + side.out_shapes, scratch_shapes=side.sems(),
        compiler_params=_params(("arbitrary",) * len(grid)))(*ins, *side.ins)


def rmsnorm(x, g, *, out_dtype, name, bt=512, side=None):
    t, d = x.shape
    bt = min(bt, t)

    def body(x_ref, g_ref, o_ref):
        xv = x_ref[...].astype(F32)
        r = lax.rsqrt(jnp.mean(xv * xv, axis=-1, keepdims=True) + EPS)
        o_ref[...] = (xv * r * g_ref[...]).astype(o_ref.dtype)

    return gridded(
        body, name=name, grid=(t // bt,),
        in_specs=[pl.BlockSpec((bt, d), lambda i: (i, 0)), pl.BlockSpec((1, d), lambda i: (0, 0))],
        out_specs=[pl.BlockSpec((bt, d), lambda i: (i, 0))],
        out_shape=[jax.ShapeDtypeStruct((t, d), out_dtype)],
        ins=[x, g], semantics=("parallel",), side=side)


def mm(a, b, *, trans_b=False, a_pro=None, extras=(), vecs=(), epilogue=None, out_dtypes, n_sums=0, t_blk=None, name,
       bm=1024, bn=1024, side=None):
    a_list = list(a) if isinstance(a, (list, tuple)) else [a]
    b_list = list(b) if isinstance(b, (list, tuple)) else [b]
    m = a_list[0].shape[0]
    n = b_list[0].shape[0] if trans_b else b_list[0].shape[1]
    ks = [x.shape[1] for x in a_list]
    if sum(ks) > 2048:
        bm = bm // 2
    bm, bn = _fit(bm, m), _fit(bn, n)
    assert n_sums == 0 or bn == n
    n_ab, n_ex, n_vec, n_out = len(a_list), len(extras), len(vecs), len(out_dtypes)
    n_t = 0 if t_blk is None else 1

    def body(*refs):
        a_refs, b_refs, ex, vs, outs, t_outs, sums = _split_refs(refs, [n_ab, n_ab, n_ex, n_vec, n_out, n_t, n_sums])
        acc = None
        for a_ref, b_ref in zip(a_refs, b_refs):
            a_tile = a_ref[...] if a_pro is None else a_pro(a_ref[...])
            part = lax.dot_general(a_tile, b_ref[...], NT if trans_b else NN, preferred_element_type=F32)
            acc = part if acc is None else acc + part
        res = epilogue(acc, *[e[...] for e in ex], *[v[...] for v in vs]) if epilogue is not None else (acc,)
        for o, r in zip(outs, res[:n_out]):
            o[...] = r.astype(o.dtype)
        for t_ref in t_outs:
            for u in range(bm // t_blk):
                t_ref[u] = res[0][u * t_blk:(u + 1) * t_blk, :].T.astype(t_ref.dtype)
        if n_sums:
            @pl.when(pl.program_id(0) == 0)
            def _():
                for s_ref in sums:
                    s_ref[...] = jnp.zeros_like(s_ref)

            for s_ref, r in zip(sums, res[n_out:]):
                s_ref[...] += r

    tile = pl.BlockSpec((bm, bn), lambda i, j: (i, j))
    vec = pl.BlockSpec((1, bn), lambda i, j: (0, j))
    t_specs, t_shapes = [], []
    if t_blk is not None:
        t_specs = [pl.BlockSpec((bm // t_blk, bn, t_blk), lambda i, j: (i, j, 0))]
        t_shapes = [jax.ShapeDtypeStruct((m // t_blk, n, t_blk), out_dtypes[0])]
    a_specs = [pl.BlockSpec((bm, k), lambda i, j: (i, 0)) for k in ks]
    b_specs = [pl.BlockSpec((bn, k), lambda i, j: (j, 0)) if trans_b else pl.BlockSpec((k, bn), lambda i, j: (0, j)) for k in ks]
    return gridded(
        body, name=name, grid=(m // bm, n // bn),
        in_specs=a_specs + b_specs + [tile] * n_ex + [vec] * n_vec,
        out_specs=[tile] * n_out + t_specs + [vec] * n_sums,
        out_shape=[jax.ShapeDtypeStruct((m, n), dt) for dt in out_dtypes] + t_shapes + [jax.ShapeDtypeStruct((1, n), F32)] * n_sums,
        ins=[*a_list, *b_list, *extras, *vecs],
        semantics=("arbitrary", "arbitrary") if n_sums else ("parallel", "parallel"), side=side)


def mm_tn(a, b, *, a_pro=None, name, col_shards=1, bf16_copy=False, bk=1024, bn=1024, bt=2048):
    t, k = a.shape
    n = b.shape[1]
    ns = n // col_shards
    bk, bn, bt = _fit(bk, k), _fit(bn, ns), _fit(bt, t)
    per = ns // bn
    last = t // bt - 1

    def body(a_ref, b_ref, o_ref, *copy_ref):
        @pl.when(pl.program_id(2) == 0)
        def _():
            o_ref[...] = jnp.zeros_like(o_ref)

        a_tile = a_ref[...] if a_pro is None else a_pro(a_ref[...])
        o_ref[...] += lax.dot_general(a_tile, b_ref[...], TN, preferred_element_type=F32)
        if bf16_copy:
            @pl.when(pl.program_id(2) == last)
            def _():
                copy_ref[0][...] = o_ref[...].astype(BF16)

    if col_shards == 1:
        out_spec = pl.BlockSpec((bk, bn), lambda i, j, s: (i, j))
        shape = (k, n)
    else:
        out_spec = pl.BlockSpec((None, bk, bn), lambda i, j, s: (j // per, i, j % per))
        shape = (col_shards, k, ns)
    out_specs, out_shape = out_spec, jax.ShapeDtypeStruct(shape, F32)
    if bf16_copy:
        out_specs, out_shape = [out_spec, out_spec], [out_shape, jax.ShapeDtypeStruct(shape, BF16)]
    return pl.pallas_call(
        body, name=name, grid=(k // bk, n // bn, t // bt),
        in_specs=[pl.BlockSpec((bt, bk), lambda i, j, s: (s, i)), pl.BlockSpec((bt, bn), lambda i, j, s: (s, j))],
        out_specs=out_specs, out_shape=out_shape,
        compiler_params=_params(("parallel", "parallel", "arbitrary")),
    )(a, b)


def _split3(x):
    hi = x.astype(BF16)
    r1 = x - hi.astype(F32)
    mid = r1.astype(BF16)
    lo = (r1 - mid.astype(F32)).astype(BF16)
    return hi, mid, lo


def cumsum_rows(x, *, reverse, name, bc=512):
    t, d = x.shape
    bc = min(bc, t)
    nb = t // bc

    def body(x_ref, o_ref, carry):
        @pl.when(pl.program_id(0) == 0)
        def _():
            carry[...] = jnp.zeros_like(carry)

        r = lax.broadcasted_iota(jnp.int32, (bc, bc), 0)
        c = lax.broadcasted_iota(jnp.int32, (bc, bc), 1)
        tri = jnp.where((r <= c) if reverse else (r >= c), 1.0, 0.0).astype(BF16)
        hi, mid, lo = _split3(x_ref[...])
        s = (lax.dot_general(tri, hi, NN, preferred_element_type=F32)
             + lax.dot_general(tri, mid, NN, preferred_element_type=F32)
             + lax.dot_general(tri, lo, NN, preferred_element_type=F32)) + carry[0:1, :]
        o_ref[...] = s
        carry[0:1, :] = s[0:1, :] if reverse else s[bc - 1:bc, :]

    imap = (lambda i: (nb - 1 - i, 0)) if reverse else (lambda i: (i, 0))
    return pl.pallas_call(
        body, name=name, grid=(nb,),
        in_specs=[pl.BlockSpec((bc, d), imap)], out_specs=pl.BlockSpec((bc, d), imap),
        out_shape=jax.ShapeDtypeStruct((t, d), F32),
        scratch_shapes=[pltpu.VMEM((8, d), F32)],
        compiler_params=_params(("arbitrary",)),
    )(x)


def _rope(x, c, a, b):
    return x * c + pltpu.roll(x, LANES - ROPE_DIM // 2, 1) * a + pltpu.roll(x, ROPE_DIM // 2, 1) * b


def _rope_bwd(d, c, a, b):
    return d * c + pltpu.roll(d * a, ROPE_DIM // 2, 1) + pltpu.roll(d * b, LANES - ROPE_DIM // 2, 1)


S_CQ, S_CKV, S_KR, S_F, S_END = 0, Q_RANK, Q_RANK + KV_RANK, Q_RANK + KV_RANK + LANES, 1024
HW = N_HEADS * LANES
FW = N_HEADS * HEAD_DIM


def _rope_tables(pos_col, inv_row):
    ang = pos_col * inv_row
    cos, sin = jnp.cos(ang), jnp.sin(ang)
    lane = lax.broadcasted_iota(jnp.int32, (1, LANES), 1)
    first = (lane >= HEAD_DIM) & (lane < HEAD_DIM + ROPE_DIM // 2)
    second = (lane >= HEAD_DIM + ROPE_DIM // 2) & (lane < HEAD_DIM + ROPE_DIM)
    return (jnp.where(lane < HEAD_DIM, 1.0, jnp.where(first | second, cos, 0.0)), jnp.where(first, -sin, 0.0),
            jnp.where(second, sin, 0.0))


def mla_prep(small, g_q, g_kv, w_uq, w_ukv, pos_col, inv_row, b_f, *, name, bt=512):
    t = small.shape[0]
    bt = min(bt, t)

    def body(s_ref, gq_ref, gkv_ref, wq_ref, wkv_ref, pos_ref, inv_ref, bf_ref,
             mq_ref, mk_ref, mv_ref, lf_ref, cqn_ref, ckvn_ref, mqt_ref, mvt_ref):
        cq = s_ref[:, S_CQ:S_CKV]
        rq = lax.rsqrt(jnp.mean(cq * cq, axis=-1, keepdims=True) + EPS)
        cqn = (cq * rq * gq_ref[...]).astype(BF16)
        ckv = s_ref[:, S_CKV:S_KR]
        rkv = lax.rsqrt(jnp.mean(ckv * ckv, axis=-1, keepdims=True) + EPS)
        ckvn = (ckv * rkv * gkv_ref[...]).astype(BF16)
        cqn_ref[...] = cqn
        ckvn_ref[...] = ckvn
        tc, ta, tb = _rope_tables(pos_ref[...], inv_ref[...])
        q = jnp.dot(cqn, wq_ref[...], preferred_element_type=F32)
        kv = jnp.dot(ckvn, wkv_ref[...], preferred_element_type=F32)
        kr = _rope(s_ref[:, S_KR:S_F], tc, ta, tb)
        for h in range(N_HEADS):
            sl = slice(h * LANES, (h + 1) * LANES)
            roped = _rope(q[:, sl], tc, ta, tb)
            mq_ref[:, sl] = roped.astype(BF16)
            mqt_ref[0, sl, :] = roped.T.astype(BF16)
            mk_ref[:, sl] = (kv[:, sl] + kr).astype(BF16)
        mv_ref[...] = kv[:, HW:].astype(BF16)
        mvt_ref[0] = kv[:, HW:].T.astype(BF16)
        z = s_ref[:, S_F:S_END - LANES] + bf_ref[...]
        lf_ref[...] = jnp.minimum(z, 0.0) - jnp.log(1.0 + jnp.exp(-jnp.abs(z)))

    def row(w):
        return pl.BlockSpec((bt, w), lambda i: (i, 0))

    def full(arr):
        return pl.BlockSpec(arr.shape, lambda i: (0, 0))

    return pl.pallas_call(
        body, name=name, grid=(t // bt,),
        in_specs=[row(S_END), full(g_q), full(g_kv), full(w_uq), full(w_ukv), row(1), full(inv_row), full(b_f)],
        out_specs=[row(HW), row(HW), row(FW), row(LANES), row(Q_RANK), row(KV_RANK),
                   pl.BlockSpec((1, HW, bt), lambda i: (i, 0, 0)), pl.BlockSpec((1, FW, bt), lambda i: (i, 0, 0))],
        out_shape=[jax.ShapeDtypeStruct((t, HW), BF16)] * 2 + [jax.ShapeDtypeStruct((t, FW), BF16), jax.ShapeDtypeStruct((t, LANES), F32),
                   jax.ShapeDtypeStruct((t, Q_RANK), BF16), jax.ShapeDtypeStruct((t, KV_RANK), BF16),
                   jax.ShapeDtypeStruct((t // bt, HW, bt), BF16), jax.ShapeDtypeStruct((t // bt, FW, bt), BF16)],
        compiler_params=_params(("parallel",)),
    )(small, g_q, g_kv, w_uq, w_ukv, pos_col, inv_row, b_f)


def mla_prep_bwd(dmq, dmk, dmv, dlf, small, g_q, g_kv, w_uq, w_ukv, pos_col, inv_row, b_f, *, name, bt=512):
    t = small.shape[0]
    bt = min(bt, t)

    def body(dmq_ref, dmk_ref, dmv_ref, dlf_ref, s_ref, gq_ref, gkv_ref, wq_ref, wkv_ref, pos_ref, inv_ref, bf_ref,
             ds_ref, dq_ref, dkv_ref, dgq_ref, dgkv_ref, db_ref):
        tc, ta, tb = _rope_tables(pos_ref[...], inv_ref[...])
        lane = lax.broadcasted_iota(jnp.int32, (1, LANES), 1)
        dkr = jnp.zeros((bt, LANES), F32)
        for h in range(N_HEADS):
            sl = slice(h * LANES, (h + 1) * LANES)
            dq_ref[:, sl] = _rope_bwd(dmq_ref[:, sl], tc, ta, tb).astype(BF16)
            dkr = dkr + dmk_ref[:, sl]
        dkv_ref[:, :HW] = dmk_ref[...].astype(BF16)
        dkv_ref[:, HW:] = dmv_ref[...].astype(BF16)
        in_rope = (lane >= HEAD_DIM) & (lane < HEAD_DIM + ROPE_DIM)
        ds_ref[:, S_KR:S_F] = jnp.where(in_rope, _rope_bwd(dkr, tc, ta, tb), 0.0).astype(BF16)

        def norm_bwd(raw, g_ref, dn, dg_ref):
            r = lax.rsqrt(jnp.mean(raw * raw, axis=-1, keepdims=True) + EPS)
            u = dn * g_ref[...]
            dot = jnp.mean(u * raw, axis=-1, keepdims=True)
            dg_ref[...] += jnp.sum(dn * (raw * r), axis=0, keepdims=True)
            return r * u - raw * (r * r * r * dot)

        @pl.when(pl.program_id(0) == 0)
        def _():
            dgq_ref[...] = jnp.zeros_like(dgq_ref)
            dgkv_ref[...] = jnp.zeros_like(dgkv_ref)
            db_ref[...] = jnp.zeros_like(db_ref)

        dcqn = lax.dot_general(dq_ref[...], wq_ref[...], NT, preferred_element_type=F32)
        ds_ref[:, S_CQ:S_CKV] = norm_bwd(s_ref[:, S_CQ:S_CKV], gq_ref, dcqn, dgq_ref).astype(BF16)
        dckvn = lax.dot_general(dkv_ref[...], wkv_ref[...], NT, preferred_element_type=F32)
        ds_ref[:, S_CKV:S_KR] = norm_bwd(s_ref[:, S_CKV:S_KR], gkv_ref, dckvn, dgkv_ref).astype(BF16)
        z = s_ref[:, S_F:S_END - LANES] + bf_ref[...]
        dz = jnp.where(lane < N_HEADS, dlf_ref[...] / (1.0 + jnp.exp(z)), 0.0)
        db_ref[...] += jnp.sum(dz, axis=0, keepdims=True)
        ds_ref[:, S_F:S_END - LANES] = dz.astype(BF16)
        ds_ref[:, S_END - LANES:] = jnp.zeros((bt, LANES), BF16)

    def row(w):
        return pl.BlockSpec((bt, w), lambda i: (i, 0))

    def full(arr):
        return pl.BlockSpec(arr.shape, lambda i: (0, 0))

    def vec(w):
        return pl.BlockSpec((1, w), lambda i: (0, 0))

    return pl.pallas_call(
        body, name=name, grid=(t // bt,),
        in_specs=[row(HW), row(HW), row(FW), row(LANES), row(S_END), full(g_q), full(g_kv), full(w_uq), full(w_ukv),
                  row(1), full(inv_row), full(b_f)],
        out_specs=[row(S_END), row(HW), row(HW + FW), vec(Q_RANK), vec(KV_RANK), vec(LANES)],
        out_shape=[jax.ShapeDtypeStruct((t, S_END), BF16), jax.ShapeDtypeStruct((t, HW), BF16),
                   jax.ShapeDtypeStruct((t, HW + FW), BF16), jax.ShapeDtypeStruct((1, Q_RANK), F32),
                   jax.ShapeDtypeStruct((1, KV_RANK), F32), jax.ShapeDtypeStruct((1, LANES), F32)],
        compiler_params=_params(("arbitrary",)),
    )(dmq, dmk, dmv, dlf, small, g_q, g_kv, w_uq, w_ukv, pos_col, inv_row, b_f)


class Side:
    def __init__(self, ins, out_shapes, n_sems, first, last, mid=None):
        self.ins, self.out_shapes, self.n_sems = list(ins), list(out_shapes), n_sems
        self.first, self.mid, self.last = first, mid, last

    def specs(self):
        return [ANY] * len(self.ins), [ANY] * len(self.out_shapes)

    def sems(self):
        return [pltpu.SemaphoreType.DMA((self.n_sems,)), pltpu.SemaphoreType.DMA((self.n_sems,))]


def _lane():
    return lax.broadcasted_iota(jnp.int32, (1, LANES), 1)


def _halves(x):
    zero = jnp.zeros_like(x)
    return [jnp.where(_lane() < HEAD_DIM, x, zero), jnp.where(_lane() >= HEAD_DIM, x, zero)]


def _groups(x):
    return [x[:, :LANES], x[:, LANES:]]


def _pick_row(tile, h):
    row = lax.broadcasted_iota(jnp.int32, (tile.shape[0], 1), 0)
    return jnp.sum(jnp.where(row == h, tile, 0.0), axis=0, keepdims=True)


def _pick_lane(tile, h):
    return jnp.sum(jnp.where(_lane() == h, tile, 0.0), axis=1, keepdims=True)


def _row_halves(x):
    row = lax.broadcasted_iota(jnp.int32, (LANES, 1), 0)
    zero = jnp.zeros_like(x)
    return [jnp.where(row < HEAD_DIM, x, zero), jnp.where(row >= HEAD_DIM, x, zero)]


def _below_diagonal(s):
    r = lax.broadcasted_iota(jnp.int32, s.shape, 0)
    c = lax.broadcasted_iota(jnp.int32, s.shape, 1)
    return jnp.where(c <= r, s, -jnp.inf)


def _above_diagonal(s):
    r = lax.broadcasted_iota(jnp.int32, s.shape, 0)
    c = lax.broadcasted_iota(jnp.int32, s.shape, 1)
    return jnp.where(r <= c, s, -jnp.inf)


def _split_refs(refs, counts):
    out, at = [], 0
    for n in counts:
        out.append(refs[at:at + n])
        at += n
    return out


def flash_fwd(qt_arr, k_arr, vt_arr, f_cum, *, qoff, koff, voff, pair, scale, name, blk=512, side=None):
    t = k_arr.shape[0]
    tblk = qt_arr.shape[2]
    blk = max(min(blk, t), tblk)
    sub = blk // tblk
    nb = t // blk
    steps = PAIRS * nb
    w = LANES if pair else 2 * LANES
    has_bias = f_cum is not None
    ins = [qt_arr, k_arr, vt_arr] + ([f_cum] if has_bias else [])

    def wide(ref, first):
        parts = [ref[first + u] for u in range(sub)]
        return parts[0] if sub == 1 else jnp.concatenate(parts, axis=1)
    s_ins, s_outs = (side.ins, side.out_shapes) if side else ([], [])

    def body(*refs):
        main, si, outs, so, sems = _split_refs(refs, [len(ins), len(s_ins), 2, len(s_outs), 2 if side else 0])
        qt_ref, k_ref, vt_ref = main[:3]
        f_ref = main[3] if has_bias else None
        o_ref, st_ref = outs
        g, i = pl.program_id(0), pl.program_id(1)
        step_id = g * nb + i
        if side:
            @pl.when(step_id == 0)
            def _():
                side.first(si, so, *sems)

            if side.mid is not None:
                @pl.when(step_id == (3 * steps) // 4)
                def _():
                    side.mid(si, so, *sems)

        qt = (wide(qt_ref, 0).astype(F32) * (scale * LOG2E)).astype(BF16)
        qts = _row_halves(qt) if pair else [qt[:LANES], qt[LANES:]]

        def k_of(kk, n):
            return kk if pair else kk[:, n * LANES:(n + 1) * LANES]

        def with_ones(vt_rows):
            return jnp.concatenate([vt_rows, jnp.ones((ACC_ROWS - HEAD_DIM, vt_rows.shape[1]), BF16)], axis=0)

        def step(j, carry, diagonal):
            rows = pl.ds(pl.multiple_of(j * blk, blk), blk)
            kk = k_ref[rows, :]
            vt = wide(vt_ref, sub * j)
            out = []
            for n in range(2):
                m, acc = carry[n]
                s = jnp.dot(k_of(kk, n), qts[n], preferred_element_type=F32)
                if has_bias:
                    s = s - LOG2E * _pick_lane(f_ref[rows, :], 2 * g + n)
                if diagonal:
                    s = _above_diagonal(s)
                m_new = jnp.maximum(m, jnp.max(s, axis=0, keepdims=True))
                p = jnp.exp2(s - m_new).astype(BF16)
                out.append((m_new, jnp.exp2(m - m_new) * acc
                            + jnp.dot(with_ones(vt[n * HEAD_DIM:(n + 1) * HEAD_DIM]), p, preferred_element_type=F32)))
            return tuple(out)

        def diagonal_in_halves(carry):
            h = tblk
            halves = [pl.ds(pl.multiple_of(i * blk, blk), h), pl.ds(pl.multiple_of(i * blk + h, h), h)]
            k_top, k_bot = k_ref[halves[0], :], k_ref[halves[1], :]
            vt_top, vt_bot = vt_ref[sub * i], vt_ref[sub * i + 1]
            out = []
            for n in range(2):
                m, acc = carry[n]
                heads = slice(n * HEAD_DIM, (n + 1) * HEAD_DIM)
                s_top = jnp.dot(k_of(k_top, n), qts[n], preferred_element_type=F32)
                s_bot = jnp.dot(k_of(k_bot, n), qts[n][:, h:], preferred_element_type=F32)
                if has_bias:
                    s_top = s_top - LOG2E * _pick_lane(f_ref[halves[0], :], 2 * g + n)
                    s_bot = s_bot - LOG2E * _pick_lane(f_ref[halves[1], :], 2 * g + n)
                s_top, s_bot = _above_diagonal(s_top), _above_diagonal(s_bot)
                m_top = jnp.maximum(m, jnp.max(s_top, axis=0, keepdims=True))
                m_new = jnp.concatenate([m_top[:, :h], jnp.maximum(m_top[:, h:], jnp.max(s_bot, axis=0, keepdims=True))], axis=1)
                p_top = jnp.exp2(s_top - m_new).astype(BF16)
                p_bot = jnp.exp2(s_bot - m_new[:, h:]).astype(BF16)
                late = jnp.concatenate([jnp.zeros((ACC_ROWS, h), F32),
                                        jnp.dot(with_ones(vt_bot[heads]), p_bot, preferred_element_type=F32)], axis=1)
                out.append((m_new, jnp.exp2(m - m_new) * acc
                            + jnp.dot(with_ones(vt_top[heads]), p_top, preferred_element_type=F32) + late))
            return tuple(out)

        init = tuple((jnp.full((1, blk), -jnp.inf, F32), jnp.zeros((ACC_ROWS, blk), F32)) for _ in range(2))
        carry = lax.fori_loop(0, i, lambda j, c: step(j, c, False), init)
        (ma, acca), (mb, accb) = diagonal_in_halves(carry) if sub == 2 else step(i, carry, True)
        la, lb = acca[HEAD_DIM:HEAD_DIM + 1], accb[HEAD_DIM:HEAD_DIM + 1]
        o_ref[...] = jnp.concatenate([acca[:HEAD_DIM] / la, accb[:HEAD_DIM] / lb], axis=0).T
        row = lax.broadcasted_iota(jnp.int32, (LANES, 1), 0)
        st_ref[0] = jnp.where(row == 0, ma + jnp.log2(la), jnp.where(row == 1, mb + jnp.log2(lb), 0.0)).T
        if side:
            @pl.when(step_id == steps - 1)
            def _():
                side.last(si, so, *sems)

    in_specs = [pl.BlockSpec((sub, w, tblk), lambda g, i: (i, qoff + g, 0)), pl.BlockSpec((t, w), lambda g, i: (0, koff + g)),
                pl.BlockSpec((t // tblk, LANES, tblk), lambda g, i: (0, voff + g, 0))]
    if has_bias:
        in_specs.append(pl.BlockSpec((t, LANES), lambda g, i: (0, 0)))
    s_in_specs, s_out_specs = side.specs() if side else ([], [])
    return pl.pallas_call(
        body, name=name, grid=(PAIRS, nb), in_specs=in_specs + s_in_specs,
        out_specs=[pl.BlockSpec((blk, LANES), lambda g, i: (i, g)), pl.BlockSpec((1, blk, LANES), lambda g, i: (g, i, 0))]
        + s_out_specs,
        out_shape=[jax.ShapeDtypeStruct((t, PAIRS * LANES), F32), jax.ShapeDtypeStruct((PAIRS, t, LANES), F32)] + list(s_outs),
        scratch_shapes=side.sems() if side else [],
        compiler_params=_params(("arbitrary", "arbitrary")),
    )(*ins, *s_ins)


def mix_norm(fo, mo, g_fo, g_mo, *, name, bt=512):
    t, d = fo.shape
    bt = min(bt, t)

    def body(fo_ref, mo_ref, gf_ref, gm_ref, o_ref):
        for n, (x_ref, g_ref) in enumerate(((fo_ref, gf_ref), (mo_ref, gm_ref))):
            xv = x_ref[...]
            r = lax.rsqrt(jnp.mean(xv * xv, axis=-1, keepdims=True) + EPS)
            o_ref[:, n * d:(n + 1) * d] = (xv * r * g_ref[...]).astype(BF16)

    row = pl.BlockSpec((bt, d), lambda i: (i, 0))
    vec = pl.BlockSpec((1, d), lambda i: (0, 0))
    return pl.pallas_call(
        body, name=name, grid=(t // bt,), in_specs=[row, row, vec, vec],
        out_specs=pl.BlockSpec((bt, 2 * d), lambda i: (i, 0)),
        out_shape=jax.ShapeDtypeStruct((t, 2 * d), BF16),
        compiler_params=_params(("parallel",)),
    )(fo, mo, g_fo, g_mo)


def mix_norm_bwd(dx1b, w_o, fo, mo, g_fo, g_mo, st_f, st_m, *, name, bt=512, side=None):
    t, d = fo.shape
    bt = min(bt, t)

    def body(dx_in_ref, w_ref, fo_ref, mo_ref, gf_ref, gm_ref, sf_ref, sm_ref, dfo_ref, dmo_ref, dgf_ref, dgm_ref,
             sfo_ref, smo_ref, dfot_ref, dmot_ref):
        @pl.when(pl.program_id(0) == 0)
        def _():
            dgf_ref[...] = jnp.zeros_like(dgf_ref)
            dgm_ref[...] = jnp.zeros_like(dgm_ref)

        dmixed = lax.dot_general(dx_in_ref[...], w_ref[...], NT, preferred_element_type=F32)
        groups = ((fo_ref, gf_ref, dfo_ref, dgf_ref, sf_ref, sfo_ref, dfot_ref),
                  (mo_ref, gm_ref, dmo_ref, dgm_ref, sm_ref, smo_ref, dmot_ref))
        for n, (x_ref, g_ref, dx_ref, dg_ref, st_ref, sto_ref, dxt_ref) in enumerate(groups):
            xv = x_ref[...]
            dhv = dmixed[:, n * d:(n + 1) * d]
            r = lax.rsqrt(jnp.mean(xv * xv, axis=-1, keepdims=True) + EPS)
            u = dhv * g_ref[...]
            dxf = r * u - xv * (r * r * r * jnp.mean(u * xv, axis=-1, keepdims=True))
            dxb = dxf.astype(BF16)
            dx_ref[...] = dxb
            dxt_ref[0] = dxf.T.astype(BF16)
            dg_ref[...] += jnp.sum(dhv * (xv * r), axis=0, keepdims=True)
            prod = xv * dxb.astype(F32)
            for g in range(PAIRS):
                grp = prod[:, g * LANES:(g + 1) * LANES]
                da = jnp.sum(jnp.where(_lane() < HEAD_DIM, grp, 0.0), axis=1, keepdims=True)
                db = jnp.sum(jnp.where(_lane() >= HEAD_DIM, grp, 0.0), axis=1, keepdims=True)
                sto_ref[g] = jnp.where(_lane() == 2, da, jnp.where(_lane() == 3, db, st_ref[g]))

    row = pl.BlockSpec((bt, d), lambda i: (i, 0))
    vec = pl.BlockSpec((1, d), lambda i: (0, 0))
    stat = pl.BlockSpec((PAIRS, bt, LANES), lambda i: (0, i, 0))
    return gridded(
        body, name=name, grid=(t // bt,),
        in_specs=[pl.BlockSpec((bt, dx1b.shape[1]), lambda i: (i, 0)), pl.BlockSpec(w_o.shape, lambda i: (0, 0)),
                  row, row, vec, vec, stat, stat],
        out_specs=[row, row, vec, vec, stat, stat] + [pl.BlockSpec((1, d, bt), lambda i: (i, 0, 0))] * 2,
        out_shape=[jax.ShapeDtypeStruct((t, d), BF16)] * 2 + [jax.ShapeDtypeStruct((1, d), F32)] * 2
        + [jax.ShapeDtypeStruct(st_f.shape, F32)] * 2 + [jax.ShapeDtypeStruct((t // bt, d, bt), BF16)] * 2,
        ins=[dx1b, w_o, fo, mo, g_fo, g_mo, st_f, st_m], semantics=("arbitrary",), side=side)


def flash_bwd(q_arr, qt_arr, k_arr, v_arr, do_arr, dot_arr, st, f_blocks, *, qoff, koff, voff, pair, scale, name, qblk=1024,
              side=None):
    t = q_arr.shape[0]
    blk = qt_arr.shape[2]
    qblk = max(min(qblk, t), blk)
    sub = qblk // blk
    nb, nbq = t // blk, t // qblk
    w = LANES if pair else 2 * LANES
    hw = w // 2
    has_bias = f_blocks is not None
    split = _halves if pair else _groups
    ins = [q_arr, qt_arr, k_arr, v_arr, do_arr, dot_arr, st] + ([f_blocks] if has_bias else [])
    n_out = 4 if has_bias else 3
    s_ins, s_outs = (side.ins, side.out_shapes) if side else ([], [])

    def wide(ref, first, count):
        parts = [ref[first + u] for u in range(count)]
        return parts[0] if count == 1 else jnp.concatenate(parts, axis=1)

    def body(*refs):
        main, si, outs, so, sems = _split_refs(refs, [len(ins), len(s_ins), n_out, len(s_outs), 2 if side else 0])
        q_ref, qt_ref, k_ref, v_ref, do_ref, dot_ref, st_ref = main[:7]
        dq_ref, dk_ref, dv_ref = outs[:3]
        g, j = pl.program_id(0), pl.program_id(1)
        step_id = g * nb + j
        if side:
            @pl.when(step_id == 0)
            def _():
                side.first(si, so, *sems)

        @pl.when(j == 0)
        def _():
            dq_ref[...] = jnp.zeros_like(dq_ref)

        kk, vv = k_ref[...], v_ref[...]
        ks = [kk, kk] if pair else _groups(kk)
        if has_bias:
            f_ref, df_ref = main[7], outs[3]
            fk = [LOG2E * _pick_row(f_ref[0], 2 * g + n) for n in range(2)]

            @pl.when(step_id == 0)
            def _():
                df_ref[...] = jnp.zeros_like(df_ref)

        def step(tb, count, carry, diagonal):
            rows = pl.ds(pl.multiple_of(tb * blk, blk), count * blk)
            qs = split((q_ref[rows, :].astype(F32) * (scale * LOG2E)).astype(BF16))
            qt = (wide(qt_ref, tb, count).astype(F32) * (scale * LOG2E)).astype(BF16)
            dos = [h.astype(BF16) for h in _halves(do_ref[rows, :].astype(F32))]
            dot = wide(dot_ref, tb, count)
            stats = st_ref[0, rows, :]
            new, dqs, row_sums = [], [], []
            for n in range(2):
                dkt, dvt, dfk = carry[n]
                s = lax.dot_general(qs[n], ks[n], NT, preferred_element_type=F32)
                if has_bias:
                    s = s - fk[n]
                if diagonal:
                    s = _below_diagonal(s)
                p = jnp.exp2(s - stats[:, n:n + 1])
                dp = lax.dot_general(dos[n], vv, NT, preferred_element_type=F32)
                ds = p * (dp - stats[:, 2 + n:3 + n])
                dsb = ds.astype(BF16)
                dvt = dvt + jnp.dot(dot[n * HEAD_DIM:(n + 1) * HEAD_DIM], p.astype(BF16), preferred_element_type=F32)
                dkt = dkt + jnp.dot(qt[n * hw:(n + 1) * hw], dsb, preferred_element_type=F32)
                dqs.append(jnp.dot(dsb, ks[n], preferred_element_type=F32))
                if has_bias:
                    dfk = dfk - jnp.sum(ds, axis=0, keepdims=True)
                    row_sums.append(jnp.sum(ds, axis=1, keepdims=True))
                new.append((dkt, dvt, dfk))
            dq = (jnp.where(_lane() < HEAD_DIM, dqs[0], dqs[1]) if pair else jnp.concatenate(dqs, axis=1)) * scale
            if has_bias:
                df_ref[rows, :] += jnp.where(_lane() == 2 * g, row_sums[0], jnp.where(_lane() == 2 * g + 1, row_sums[1], 0.0))
            dq_ref[rows, :] += dq
            return tuple(new)

        init = tuple((jnp.zeros((hw, blk), F32), jnp.zeros((HEAD_DIM, blk), F32), jnp.zeros((1, blk), F32)) for _ in range(2))
        carry = step(j, 1, init, True)
        whole = j // sub + 1
        carry = lax.fori_loop(j + 1, whole * sub, lambda tb, c: step(tb, 1, c, False), carry)
        (dka, dva, dfa), (dkb, dvb, dfb) = lax.fori_loop(whole, nbq, lambda i, c: step(i * sub, sub, c, False), carry)
        dk_ref[...] = (jnp.concatenate([dka, dkb], axis=0).T * LN2).astype(BF16)
        dv_ref[...] = jnp.concatenate([dva, dvb], axis=0).T.astype(BF16)
        if has_bias:
            row = lax.broadcasted_iota(jnp.int32, (LANES, 1), 0)
            by_head = jnp.where(row == 2 * g, dfa, jnp.where(row == 2 * g + 1, dfb, 0.0))
            df_ref[pl.ds(pl.multiple_of(j * blk, blk), blk), :] += by_head.T
        if side:
            @pl.when(step_id == PAIRS * nb - 1)
            def _():
                side.last(si, so, *sems)

    in_specs = [pl.BlockSpec((t, w), lambda g, j: (0, qoff + g)), pl.BlockSpec((nb, w, blk), lambda g, j: (0, qoff + g, 0)),
                pl.BlockSpec((blk, w), lambda g, j: (j, koff + g)), pl.BlockSpec((blk, LANES), lambda g, j: (j, voff + g)),
                pl.BlockSpec((t, LANES), lambda g, j: (0, g)), pl.BlockSpec((nb, LANES, blk), lambda g, j: (0, g, 0)),
                pl.BlockSpec((1, t, LANES), lambda g, j: (g, 0, 0))]
    out_specs = [pl.BlockSpec((t, w), lambda g, j: (0, g)), pl.BlockSpec((blk, w), lambda g, j: (j, g)),
                 pl.BlockSpec((blk, LANES), lambda g, j: (j, g))]
    out_shape = [jax.ShapeDtypeStruct((t, PAIRS * w), F32), jax.ShapeDtypeStruct((t, PAIRS * w), BF16),
                 jax.ShapeDtypeStruct((t, PAIRS * LANES), BF16)]
    if has_bias:
        in_specs.append(pl.BlockSpec((1, N_HEADS, blk), lambda g, j: (j, 0, 0)))
        out_specs.append(pl.BlockSpec((t, LANES), lambda g, j: (0, 0)))
        out_shape.append(jax.ShapeDtypeStruct((t, LANES), F32))
    s_in_specs, s_out_specs = side.specs() if side else ([], [])
    return pl.pallas_call(
        body, name=name, grid=(PAIRS, nb), in_specs=in_specs + s_in_specs, out_specs=out_specs + s_out_specs,
        out_shape=out_shape + list(s_outs), scratch_shapes=side.sems() if side else [],
        compiler_params=_params(("arbitrary", "arbitrary")),
    )(*ins, *s_ins)


def _adamw_math(w, g, m, v):
    nm = ADAM_B1 * m + (1.0 - ADAM_B1) * g
    nv = ADAM_B2 * v + (1.0 - ADAM_B2) * (g * g)
    m_hat = nm / (1.0 - ADAM_B1 ** ADAM_STEP)
    v_hat = nv / (1.0 - ADAM_B2 ** ADAM_STEP)
    return -ADAM_LR * (m_hat / (jnp.sqrt(v_hat) + ADAM_EPS) + ADAM_WD * w), nm, nv


def adamw(w, g, m, v, *, name):
    rws, cols = w.shape
    br = _row_block(rws)

    def body(w_ref, g_ref, m_ref, v_ref, d_ref, nm_ref, nv_ref):
        d_ref[...], nm_ref[...], nv_ref[...] = _adamw_math(w_ref[...], g_ref[...], m_ref[...], v_ref[...])

    blk = pl.BlockSpec((br, cols), lambda i: (i, 0))
    return pl.pallas_call(
        body, name=name, grid=(rws // br,), in_specs=[blk] * 4, out_specs=[blk] * 3,
        out_shape=[jax.ShapeDtypeStruct((rws, cols), F32)] * 3,
        compiler_params=_params(("parallel",)),
    )(w, g, m, v)


def adamw_halves(w, g_mine, g_other, m, v, core, *, name):
    _, k, n = w.shape
    br = _row_block(k // 2)
    nh = k // 2 // br

    def body(c_ref, w_ref, gm_ref, go_ref, m_ref, v_ref, g_out, d_ref, nm_ref, nv_ref):
        gv = jnp.where(pl.program_id(0) == c_ref[0], gm_ref[...], go_ref[...])
        g_out[0] = gv
        d_ref[0], nm_ref[0], nv_ref[0] = _adamw_math(w_ref[0], gv, m_ref[0], v_ref[0])

    full = pl.BlockSpec((1, br, n), lambda hb, i, c: (0, hb * nh + i, 0))
    half = pl.BlockSpec((br, n), lambda hb, i, c: (i, 0))
    return pl.pallas_call(
        body, name=name,
        grid_spec=pltpu.PrefetchScalarGridSpec(num_scalar_prefetch=1, grid=(2, nh), in_specs=[full, half, half, full, full],
                                               out_specs=[full] * 4),
        out_shape=[jax.ShapeDtypeStruct(w.shape, F32)] * 4,
        compiler_params=_params(("parallel", "parallel")),
    )(core, w, g_mine, g_other, m, v)


def adamw_halves_t(wt, mt, vt, gt_mine, gt_other, core, *, name, bc=128):
    n, _, k = wt.shape
    nh = k // 2 // bc

    def body(c_ref, w_ref, m_ref, v_ref, gm_ref, go_ref, g_out, d_ref, nm_ref, nv_ref):
        gv = jnp.where(pl.program_id(0) == c_ref[0], gm_ref[...], go_ref[...])
        g_out[:, 0, :] = gv
        d_ref[:, 0, :], nm_ref[:, 0, :], nv_ref[:, 0, :] = _adamw_math(w_ref[:, 0, :], gv, m_ref[:, 0, :], v_ref[:, 0, :])

    full = pl.BlockSpec((n, 1, bc), lambda hb, i, c: (0, 0, hb * nh + i))
    half = pl.BlockSpec((n, bc), lambda hb, i, c: (0, i))
    return pl.pallas_call(
        body, name=name,
        grid_spec=pltpu.PrefetchScalarGridSpec(num_scalar_prefetch=1, grid=(2, nh), in_specs=[full, full, full, half, half],
                                               out_specs=[full] * 4),
        out_shape=[jax.ShapeDtypeStruct(wt.shape, F32)] * 4,
        compiler_params=_params(("parallel", "parallel")),
    )(core, wt, mt, vt, gt_mine, gt_other)


def add_pair(dw, recv, core, *, name):
    n4, k, n = dw.shape
    half = (1, k // 2, n) if split_axis(k) == 0 else (1, k, n // 2)
    mine = (lambda q, c: (q, c[0], 0)) if split_axis(k) == 0 else (lambda q, c: (q, 0, c[0]))

    def body(c_ref, a_ref, b_ref, o_ref):
        o_ref[...] = (a_ref[...] + b_ref[...].astype(F32)).astype(BF16)

    return pl.pallas_call(
        body, name=name,
        grid_spec=pltpu.PrefetchScalarGridSpec(
            num_scalar_prefetch=1, grid=(n4,),
            in_specs=[pl.BlockSpec(half, mine), pl.BlockSpec(half, lambda q, c: (q, 0, 0))],
            out_specs=pl.BlockSpec(half, lambda q, c: (q, 0, 0))),
        out_shape=jax.ShapeDtypeStruct((n4,) + half[1:], BF16),
        compiler_params=_params(("parallel",)),
    )(core, dw, recv)


def sum_chips(parts, *, name):
    n4, r, n = parts.shape
    if r % 16 == 0:
        br, bc = _row_block(r), n
    else:
        br, bc = r, LANES

    def body(p_ref, o_ref):
        acc = p_ref[0].astype(F32)
        for q in range(1, n4):
            acc = acc + p_ref[q].astype(F32)
        o_ref[...] = acc

    return pl.pallas_call(
        body, name=name, grid=(r // br, n // bc),
        in_specs=[pl.BlockSpec((n4, br, bc), lambda i, j: (0, i, j))], out_specs=pl.BlockSpec((br, bc), lambda i, j: (i, j)),
        out_shape=jax.ShapeDtypeStruct((r, n), F32),
        compiler_params=_params(("parallel", "parallel")),
    )(parts)


ANY = pl.BlockSpec(memory_space=pl.ANY)


def _place():
    x, y, c = lax.axis_index("x"), lax.axis_index("y"), lax.axis_index("c")
    chips = [(1 - x, y), (x, 1 - y), (1 - x, 1 - y)]
    return x, y, c, chips


def _copy(src, dst, send_sems, recv_sems, k, to):
    return pltpu.make_async_remote_copy(src_ref=src, dst_ref=dst, send_sem=send_sems.at[k], recv_sem=recv_sems.at[k],
                                        device_id=to, device_id_type=MESH)


def split_axis(rows):
    return 0 if rows % 32 == 0 else 1


def _half(ref, lead, hf):
    rows, cols = ref.shape[-2:]
    if split_axis(rows) == 0:
        at = (pl.ds(hf * (rows // 2), rows // 2), slice(None))
    else:
        at = (slice(None), pl.ds(hf * (cols // 2), cols // 2))
    return ref.at[at] if lead is None else ref.at[(lead,) + at]


def _gather_first(srcs, dsts, ssems, rsems):
    x, y, c, chips = _place()
    for ti, (s, d) in enumerate(zip(srcs, dsts)):
        for j, (cx, cy) in enumerate(chips):
            _copy(_half(s, None, c), _half(d, 2 * x + y, c), ssems, rsems, 3 * ti + j, (cx, cy, c)).start()


def _gather_mid(srcs, dsts, ssems, rsems):
    x, y, c, chips = _place()
    n1 = 3 * len(srcs)
    for ti, d in enumerate(dsts):
        for j, (cx, cy) in enumerate(chips):
            landed = _half(d, 2 * cx + cy, c)
            _copy(landed, landed, ssems, rsems, 3 * ti + j, (cx, cy, c)).wait_recv()
            _copy(landed, landed, ssems, rsems, n1 + 3 * ti + j, (x, y, 1 - c)).start()


def _gather_last(srcs, dsts, ssems, rsems):
    x, y, c, chips = _place()
    n1 = 3 * len(srcs)
    for ti, (s, d) in enumerate(zip(srcs, dsts)):
        for j, (cx, cy) in enumerate(chips):
            other = _half(d, 2 * cx + cy, 1 - c)
            _copy(other, other, ssems, rsems, n1 + 3 * ti + j, (x, y, 1 - c)).wait_recv()
        for j, (cx, cy) in enumerate(chips):
            mine = _half(s, None, c)
            _copy(mine, mine, ssems, rsems, 3 * ti + j, (cx, cy, c)).wait_send()
            _copy(mine, mine, ssems, rsems, n1 + 3 * ti + j, (x, y, 1 - c)).wait_send()


def gather_side(shards):
    return Side(shards, [jax.ShapeDtypeStruct((N_CHIPS,) + s.shape, s.dtype) for s in shards], 6 * len(shards),
                _gather_first, _gather_last, _gather_mid)


def _scatter_first(srcs, dsts, ssems, rsems):
    x, y, c, chips = _place()
    for ti, (s, d) in enumerate(zip(srcs, dsts)):
        for j, (cx, cy) in enumerate(chips):
            _copy(s.at[2 * cx + cy], d.at[2 * x + y], ssems, rsems, 3 * ti + j, (cx, cy, c)).start()


def _scatter_last(srcs, dsts, ssems, rsems):
    x, y, c, chips = _place()
    for ti, (s, d) in enumerate(zip(srcs, dsts)):
        for j, (cx, cy) in enumerate(chips):
            _copy(s.at[2 * cx + cy], d.at[2 * cx + cy], ssems, rsems, 3 * ti + j, (cx, cy, c)).wait_recv()
        for j, (cx, cy) in enumerate(chips):
            _copy(s.at[2 * cx + cy], d.at[2 * cx + cy], ssems, rsems, 3 * ti + j, (cx, cy, c)).wait_send()


def scatter_side(parts):
    return Side(parts, [jax.ShapeDtypeStruct(p.shape, p.dtype) for p in parts], 3 * len(parts), _scatter_first, _scatter_last)


def run_side(side, *, name):
    n_in, n_out = len(side.ins), len(side.out_shapes)

    def body(*refs):
        si, so, sems = _split_refs(refs, [n_in, n_out, 2])
        side.first(si, so, *sems)
        if side.mid is not None:
            side.mid(si, so, *sems)
        side.last(si, so, *sems)

    in_specs, out_specs = side.specs()
    return pl.pallas_call(body, name=name, in_specs=in_specs, out_specs=out_specs, out_shape=side.out_shapes,
                          scratch_shapes=side.sems())(*side.ins)


def _swap_first(srcs, dsts, ssems, rsems):
    x, y, c, _ = _place()
    for k, (s, d) in enumerate(zip(srcs, dsts)):
        _copy(s, d, ssems, rsems, k, (x, y, 1 - c)).start()


def _swap_last(srcs, dsts, ssems, rsems):
    x, y, c, _ = _place()
    for k, (s, d) in enumerate(zip(srcs, dsts)):
        _copy(s, d, ssems, rsems, k, (x, y, 1 - c)).wait()


def swap_side(xs):
    return Side(xs, [jax.ShapeDtypeStruct(a.shape, a.dtype) for a in xs], len(xs), _swap_first, _swap_last)


def _swap_halves(srcs, dsts, ssems, rsems):
    x, y, c, _ = _place()
    for k, (s, d) in enumerate(zip(srcs, dsts)):
        hk = s.shape[1] // 2
        yield _copy(s.at[:, pl.ds((1 - c) * hk, hk), :], d, ssems, rsems, k, (x, y, 1 - c))


def _swap_halves_first(srcs, dsts, ssems, rsems):
    for cp in _swap_halves(srcs, dsts, ssems, rsems):
        cp.start()


def _swap_halves_last(srcs, dsts, ssems, rsems):
    for cp in _swap_halves(srcs, dsts, ssems, rsems):
        cp.wait()


def swap_halves_side(xs):
    return Side(xs, [jax.ShapeDtypeStruct((a.shape[0], a.shape[1] // 2, a.shape[2]), a.dtype) for a in xs], len(xs),
                _swap_halves_first, _swap_halves_last)


def allreduce_small(s):
    n_dev = 8

    def body(s_ref, out_ref, buf, send_sems, recv_sems):
        x, y, c, _ = _place()
        me = 4 * x + 2 * y + c
        buf[me] = s_ref[...]
        sends = []
        for k in range(1, n_dev):
            px = 1 - x if k & 4 else x
            py = 1 - y if k & 2 else y
            pc = 1 - c if k & 1 else c
            cp = _copy(s_ref, buf.at[me], send_sems, recv_sems, k - 1, (px, py, pc))
            cp.start()
            sends.append((cp, 4 * px + 2 * py + pc))
        for k, (cp, peer) in enumerate(sends):
            _copy(s_ref, buf.at[peer], send_sems, recv_sems, k, (x, y, c)).wait_recv()
        for cp, _ in sends:
            cp.wait_send()
        acc = buf[0]
        for d in range(1, n_dev):
            acc = acc + buf[d]
        out_ref[...] = acc

    vm = pl.BlockSpec(memory_space=pltpu.VMEM)
    return pl.pallas_call(
        body, name="allreduce_small", in_specs=[vm], out_specs=vm,
        out_shape=jax.ShapeDtypeStruct(s.shape, F32),
        scratch_shapes=[pltpu.VMEM((n_dev,) + s.shape, F32), pltpu.SemaphoreType.DMA((n_dev - 1,)),
                        pltpu.SemaphoreType.DMA((n_dev - 1,))],
    )(s)


def join_cols(sm):
    n4, k, n = sm.shape
    return sm.transpose(1, 0, 2).reshape(k, n4 * n)


def split_cols(full):
    k, n = full.shape
    return full.reshape(k, N_CHIPS, n // N_CHIPS).transpose(1, 0, 2)


def _pad_heads(w, width):
    lead, heads = w.shape[:-1], w.shape[-1] // width
    w = w.reshape(lead + (heads, width))
    return jnp.pad(w, [(0, 0)] * len(lead) + [(0, 0), (0, LANES - width)]).reshape(lead + (heads * LANES,))


def _unpad_heads(w, width):
    lead, heads = w.shape[:-1], w.shape[-1] // LANES
    return w.reshape(lead + (heads, LANES))[..., :width].reshape(lead + (heads * width,))


O_F = 3 * FW
O_CQ = O_F + N_HEADS
O_CKV = O_CQ + Q_RANK
O_KR = O_CKV + KV_RANK
O_END = O_KR + ROPE_DIM


def _shard_rows(sm, a, b):
    r = sm.shape[1]
    out = []
    while a < b:
        q = a // r
        e = min(b, (q + 1) * r)
        out.append(sm[q, a - q * r:e - q * r])
        a = e
    return out


def split_w_in_t(w_sm):
    d = w_sm.shape[2]

    def z(n):
        return [jnp.zeros((n, d), w_sm.dtype)]

    small = (_shard_rows(w_sm, O_CQ, O_CKV) + _shard_rows(w_sm, O_CKV, O_KR) + z(HEAD_DIM) + _shard_rows(w_sm, O_KR, O_END)
             + z(LANES - HEAD_DIM - ROPE_DIM) + _shard_rows(w_sm, O_F, O_CQ) + z(LANES - N_HEADS) + z(LANES))
    return jnp.concatenate(_shard_rows(w_sm, 0, O_F), axis=0), jnp.concatenate(small, axis=0)


def join_w_in_t(d_qkv_t, d_small_t):
    kr = S_KR + HEAD_DIM
    segments = [(part, 0, n * FW, FW) for n, part in enumerate(d_qkv_t)]
    segments += [(d_small_t, S_F, O_F, N_HEADS), (d_small_t, S_CQ, O_CQ, Q_RANK),
                 (d_small_t, S_CKV, O_CKV, KV_RANK), (d_small_t, kr, O_KR, ROPE_DIM)]
    r = O_END // N_CHIPS
    shards = []
    for q in range(N_CHIPS):
        pieces = []
        for src, s0, v0, n in segments:
            a, b = max(v0, q * r), min(v0 + n, (q + 1) * r)
            if a < b:
                pieces.append(src[s0 + a - v0:s0 + b - v0])
        shards.append(jnp.concatenate(pieces, axis=0))
    return jnp.stack(shards)


def rope_inputs(pos):
    inv_freq = ROPE_THETA ** (-jnp.arange(0, ROPE_DIM, 2, dtype=F32) / ROPE_DIM)
    inv_row = jnp.concatenate([jnp.zeros((HEAD_DIM,), F32), inv_freq, inv_freq, jnp.zeros((LANES - HEAD_DIM - ROPE_DIM,), F32)])
    return pos.astype(F32)[:, None], inv_row[None, :]


def _pad_lanes(v, n):
    return jnp.pad(v, ((0, 0), (0, n - v.shape[1])))


ATTN_BLK = 512
ATTN_FWD_BLK = 1024
ATTN_BWD_QBLK = 1024
ACC_ROWS = HEAD_DIM + 16


def local_step(xs, pos, tgt, gains, early_weights, late_weights, early_side=None, fwd_sides=(None, None), reduction=None):
    g_attn, b_forget, g_q, g_kv, g_fo, g_mo, g_mlp, g_fin = gains
    t = xs.shape[0]
    blk = min(ATTN_BLK, t)
    fox_scale = 1.0 / (HEAD_DIM ** 0.5)
    mla_scale = 1.0 / ((HEAD_DIM + ROPE_DIM) ** 0.5)

    h1, *gathered = rmsnorm(xs, g_attn, out_dtype=BF16, name="norm_attn", side=early_side)
    w_in_t, w_uq_p, w_ukv = early_weights(gathered)
    w_qkv_t, w_small_t = split_w_in_t(w_in_t)
    kv = w_ukv.reshape(KV_RANK, N_HEADS, 2 * HEAD_DIM)
    w_ukv_p = jnp.concatenate([_pad_heads(kv[:, :, :HEAD_DIM].reshape(KV_RANK, FW), HEAD_DIM),
                               kv[:, :, HEAD_DIM:].reshape(KV_RANK, FW)], axis=1)
    b_f = _pad_lanes(b_forget, LANES)
    pos_col, inv_row = rope_inputs(pos)

    qkv, qkv_t = mm(h1, w_qkv_t, trans_b=True, out_dtypes=[BF16], t_blk=blk, name="proj_qkv", bn=1536)
    small, = mm(h1, w_small_t, trans_b=True, out_dtypes=[F32], name="proj_small")
    mq, mk, mv, lf, cqn, ckvn, mq_t, mv_t = mla_prep(small, g_q, g_kv, w_uq_p, w_ukv_p, pos_col, inv_row, b_f,
                                                     name="mla_prep", bt=blk)
    f_cum = cumsum_rows(lf, reverse=False, name="gate_cumsum")
    f_blocks = f_cum[:, :N_HEADS].reshape(t // blk, blk, N_HEADS).transpose(0, 2, 1)
    fo, st_f, *gathered = flash_fwd(qkv_t, qkv, qkv_t, f_cum, qoff=0, koff=PAIRS, voff=2 * PAIRS, pair=True,
                                    scale=fox_scale, name="fox_fwd", blk=ATTN_FWD_BLK, side=fwd_sides[0])
    mo, st_m, *more = flash_fwd(mq_t, mk, mv_t, None, qoff=0, koff=0, voff=0, pair=False, scale=mla_scale, name="mla_fwd",
                                blk=ATTN_FWD_BLK, side=fwd_sides[1])
    w_o, w_up, w_down = late_weights(gathered + more)
    mixed = mix_norm(fo, mo, g_fo, g_mo, name="norm_mix")

    def inv_rms(v):
        return lax.rsqrt(jnp.mean(v * v, axis=-1, keepdims=True) + EPS)

    def residual_then_norm(acc, res, g):
        xn = acc + res
        return xn, xn * inv_rms(xn) * g

    def norm_bwd(dh, xn, res, g):
        r = inv_rms(xn)
        uu = dh * g
        return (r * uu - xn * (r * r * r * jnp.mean(uu * xn, axis=-1, keepdims=True)) + res,
                jnp.sum(dh * (xn * r), axis=0, keepdims=True))

    def norm_bwd2(dh, xn, res, g):
        dx, dg = norm_bwd(dh, xn, res, g)
        return dx, dx, dg

    def residual_then_loss(acc, res, target, g):
        xn = acc + res
        r = inv_rms(xn)
        xh = xn * r
        e = xh * g - target
        part = 0.5 * jnp.sum(jnp.mean(e * e, axis=-1, keepdims=True), axis=0, keepdims=True)
        dy = e * (1.0 / xn.shape[1])
        uu = dy * g
        dx = r * uu - xn * (r * r * r * jnp.mean(uu * xn, axis=-1, keepdims=True))
        return dx, dx, jnp.sum(dy * xh, axis=0, keepdims=True), part + jnp.zeros_like(g)

    x1, h2 = mm(mixed, w_o, extras=[xs], vecs=[g_mlp], epilogue=residual_then_norm, out_dtypes=[F32, BF16], name="out_proj")

    def relu2(uu):
        r = jnp.maximum(uu.astype(F32), 0.0)
        return (r * r).astype(BF16)

    u, = mm(h2, w_up, out_dtypes=[BF16], name="mlp_up", bn=2048)
    dx2, dx2b, dg_fin, loss_row = mm(u, w_down, a_pro=relu2, extras=[x1, tgt], vecs=[g_fin], epilogue=residual_then_loss,
                                     out_dtypes=[F32, BF16], n_sums=2, name="mlp_down_loss")
    loss = loss_row[:, :1]

    def relu2_grad(acc, uu):
        return (acc * (2.0 * jnp.maximum(uu.astype(F32), 0.0)),)

    du, = mm(dx2b, w_down, trans_b=True, extras=[u], epilogue=relu2_grad, out_dtypes=[BF16], name="mlp_down_bwd", bn=2048)
    dw_down, dw_down_b = (g.reshape(N_CHIPS, -1, w_down.shape[1])
                          for g in mm_tn(u, dx2b, a_pro=relu2, name="dw_down", bf16_copy=True))
    dx1, dx1b, dg_mlp = mm(du, w_up, trans_b=True, extras=[x1, dx2], vecs=[g_mlp], epilogue=norm_bwd2,
                           out_dtypes=[F32, BF16], n_sums=1, name="mlp_up_bwd")
    dw_up, dw_up_b = mm_tn(h2, du, name="dw_up", col_shards=N_CHIPS, bf16_copy=True)

    dw_o, dw_o_b = (g.reshape(N_CHIPS, -1, w_o.shape[1]) for g in mm_tn(mixed, dx1b, name="dw_o", bf16_copy=True))
    late, late_b = (dw_o, dw_up, dw_down), (dw_o_b, dw_up_b, dw_down_b)
    red = reduction
    dfo, dmo, dg_fo, dg_mo, st_f, st_m, dfo_t, dmo_t, *got = mix_norm_bwd(
        dx1b, w_o, fo, mo, g_fo, g_mo, st_f, st_m, name="out_proj_mix_bwd", bt=blk,
        side=red.late_swap(late, late_b) if red else None)
    dfq, dfk, dfv, d_f, *got = flash_bwd(
        qkv, qkv_t, qkv, qkv, dfo, dfo_t, st_f, f_blocks, qoff=0, koff=PAIRS, voff=2 * PAIRS, pair=True,
        scale=fox_scale, name="fox_bwd", qblk=ATTN_BWD_QBLK, side=red.late_scatter(got) if red else None)
    dmq, dmk, dmv, *got = flash_bwd(mq, mq_t, mk, mv, dmo, dmo_t, st_m, None, qoff=0, koff=0, voff=0,
                                    pair=False, scale=mla_scale, name="mla_bwd", qblk=ATTN_BWD_QBLK,
                                    side=red.late_halves(got) if red else None)
    if red:
        red.late_done(got)
    dlf = cumsum_rows(d_f, reverse=True, name="gate_cumsum_bwd")
    dsmall, dq_u, dkv_u, dg_q, dg_kv, db_f = mla_prep_bwd(dmq, dmk, dmv, dlf, small, g_q, g_kv, w_uq_p, w_ukv_p,
                                                          pos_col, inv_row, b_f, name="mla_prep_bwd")
    dw_uq_p = mm_tn(cqn, dq_u, name="dw_uq")
    dw_ukv_p = mm_tn(ckvn, dkv_u, name="dw_ukv")

    def to_bf16(tile):
        return tile.astype(BF16)

    dqkv = [dfq, dfk, dfv]
    dw_in_t = join_w_in_t([mm_tn(part, h1, a_pro=to_bf16, name="dw_" + nm) for part, nm in zip(dqkv, "qkv")],
                          mm_tn(dsmall, h1, name="dw_small"))
    dk_cols = _unpad_heads(dw_ukv_p[:, :HW], HEAD_DIM).reshape(KV_RANK, N_HEADS, HEAD_DIM)
    dv_cols = dw_ukv_p[:, HW:].reshape(KV_RANK, N_HEADS, HEAD_DIM)
    dw_ukv = jnp.concatenate([dk_cols, dv_cols], axis=2).reshape(KV_RANK, N_HEADS * 2 * HEAD_DIM)
    early = (dw_in_t, split_cols(dw_uq_p), split_cols(dw_ukv))
    w_parts = [w_qkv_t[n * FW:(n + 1) * FW] for n in range(3)]
    grad_x, dg_attn, *got = mm(dqkv + [dsmall], w_parts + [w_small_t], a_pro=to_bf16, extras=[xs, dx1], vecs=[g_attn],
                               epilogue=norm_bwd, out_dtypes=[F32], n_sums=1, name="proj_bwd",
                               side=red.early_scatter(early) if red else None)
    if red:
        red.early_done(got)
    d_gains = (dg_attn, db_f[:, :N_HEADS], dg_q, dg_kv, dg_fo, dg_mo, dg_mlp, dg_fin)
    return loss, grad_x, early, late, d_gains


class GradReduction:
    def __init__(self, core_id, chip):
        self.core_id, self.chip = core_id, chip
        self.core = core_id.reshape(1).astype(jnp.int32)
        self.grads, self.pairs, self.halves, self.others = {}, {}, {}, {}

    def _other_halves(self, grads):
        out = []
        for g in grads:
            axis = 1 + split_axis(g.shape[1])
            size = g.shape[axis] // 2
            out.append(lax.dynamic_slice_in_dim(g, (1 - self.core_id) * size, size, axis=axis).astype(BF16))
        return out

    def _add_pairs(self, group, recvs):
        self.pairs[group] = [add_pair(g, r, self.core, name="add_pair_%s_%d" % (group, n))
                             for n, (g, r) in enumerate(zip(self.grads[group], recvs))]
        return scatter_side(self.pairs[group])

    def _chip_sums(self, group, scattered):
        chip = self.chip
        with_mine = [lax.dynamic_update_index_in_dim(s, lax.dynamic_index_in_dim(p, chip, 0, keepdims=True), chip, 0)
                     for s, p in zip(scattered, self.pairs[group])]
        self.halves[group] = [sum_chips(s, name="sum_chips_%s_%d" % (group, n)) for n, s in enumerate(with_mine)]
        return self.halves[group]

    def late_swap(self, grads, bf16_copies):
        self.grads["late"] = list(grads)
        return swap_halves_side(list(bf16_copies))

    def late_scatter(self, recvs):
        return self._add_pairs("late", recvs)

    def late_halves(self, scattered):
        return swap_side(self._chip_sums("late", scattered))

    def late_done(self, others):
        self.others["late"] = list(others)

    def early_scatter(self, grads):
        self.grads["early"] = list(grads)
        return self._add_pairs("early", run_side(swap_side(self._other_halves(grads)), name="swap_early_sends"))

    def early_done(self, scattered):
        self.others["early"] = list(run_side(swap_side(self._chip_sums("early", scattered)), name="swap_early_halves"))

def kernel(x, positions, attn_norm_g, w_in, b_forget, q_norm_g, w_uq, kv_norm_g, w_ukv, fox_out_g, mla_out_g, w_o, mlp_norm_g, w_up, w_down, final_norm_g, loss_target, m_attn_norm_g, m_w_in, m_b_forget, m_q_norm_g, m_w_uq, m_kv_norm_g, m_w_ukv, m_fox_out_g, m_mla_out_g, m_w_o, m_mlp_norm_g, m_w_up, m_w_down, m_final_norm_g, v_attn_norm_g, v_w_in, v_b_forget, v_q_norm_g, v_w_uq, v_kv_norm_g, v_w_ukv, v_fox_out_g, v_mla_out_g, v_w_o, v_mlp_norm_g, v_w_up, v_w_down, v_final_norm_g):
    core_id = lax.axis_index("c")
    core = core_id.reshape(1).astype(jnp.int32)
    chip = 2 * lax.axis_index("x") + lax.axis_index("y")
    big = [w_in, w_uq, w_ukv, w_o, w_up, w_down]
    big_m = [m_w_in, m_w_uq, m_w_ukv, m_w_o, m_w_up, m_w_down]
    big_v = [v_w_in, v_w_uq, v_w_ukv, v_w_o, v_w_up, v_w_down]
    n_early = 3

    def vec(a):
        return a.reshape(1, -1)

    small = [attn_norm_g, b_forget, q_norm_g, kv_norm_g, fox_out_g, mla_out_g, mlp_norm_g, final_norm_g]
    small_m = [m_attn_norm_g, m_b_forget, m_q_norm_g, m_kv_norm_g, m_fox_out_g, m_mla_out_g, m_mlp_norm_g, m_final_norm_g]
    small_v = [v_attn_norm_g, v_b_forget, v_q_norm_g, v_kv_norm_g, v_fox_out_g, v_mla_out_g, v_mlp_norm_g, v_final_norm_g]
    gains = [vec(a) for a in small]

    views = [big[0][0].T, _pad_heads(big[1][0], HEAD_DIM + ROPE_DIM)] + [w[0] for w in big[2:]]
    shards = [v.astype(BF16) for v in views]

    def with_own(gathered, mine):
        return [lax.dynamic_update_index_in_dim(g, s, chip, 0) for g, s in zip(gathered, mine)]

    def early_weights(gathered):
        g_in, g_uq, g_ukv = with_own(gathered, shards[:n_early])
        return g_in, join_cols(g_uq), join_cols(g_ukv)

    def late_weights(gathered):
        g_o, g_up, g_down = with_own(gathered, shards[n_early:])
        return g_o.reshape(-1, g_o.shape[2]), join_cols(g_up), g_down.reshape(-1, g_down.shape[2])

    reduction = GradReduction(core_id, chip)
    loss, grad_x, _, _, d_small = local_step(
        x[0], positions[0], loss_target[0], gains, early_weights, late_weights, gather_side(shards[:n_early]),
        (gather_side(shards[n_early:-1]), gather_side(shards[-1:])), reduction)
    halves = reduction.halves["early"] + reduction.halves["late"]
    others = reduction.others["early"] + reduction.others["late"]

    def rows8(vs):
        return jnp.concatenate([_pad_lanes(vec(a).astype(F32), 1024) for a in vs], axis=0)

    with_loss = [jnp.concatenate([d, loss], axis=1) if n == 1 else d for n, d in enumerate(d_small)]
    g_small8 = allreduce_small(rows8(with_loss))

    outs_big = []
    for n, (w, gm, go, m, v) in enumerate(zip(big, halves, others, big_m, big_v)):
        if n == 0:
            outs = adamw_halves_t(*(jnp.transpose(a, (2, 0, 1)) for a in (w, m, v)), gm, go, core, name="adamw_%d" % n)
            outs_big.append([jnp.transpose(o, (1, 2, 0)) for o in outs])
        else:
            if n == 1:
                gm, go = (_unpad_heads(gh, HEAD_DIM + ROPE_DIM) for gh in (gm, go))
            outs_big.append(adamw_halves(w, gm, go, m, v, core, name="adamw_%d" % n))
    d8, m8, v8 = adamw(rows8(small), g_small8, rows8(small_m), rows8(small_v), name="adamw_small")

    def unrows8(a8):
        return [a8[n, :s.size].reshape(s.shape) for n, s in enumerate(small)]

    loss_all = g_small8[1, N_HEADS]
    grads, deltas, new_m, new_v = [None] * 14, [None] * 14, [None] * 14, [None] * 14
    big_at = [1, 4, 6, 9, 11, 12]
    small_at = [0, 2, 3, 5, 7, 8, 10, 13]
    for n, at in enumerate(big_at):
        grads[at], deltas[at], new_m[at], new_v[at] = outs_big[n]
    for at, g, dd, mm_, vv in zip(small_at, unrows8(g_small8), unrows8(d8), unrows8(m8), unrows8(v8)):
        grads[at], deltas[at], new_m[at], new_v[at] = g, dd, mm_, vv
    return (loss_all, grad_x[None], *grads, *deltas, *new_m, *new_v)
```

```python
import jax
import jax.numpy as jnp
from jax import lax
from jax.experimental import pallas as pl
from jax.experimental.pallas import tpu as pltpu

F32 = jnp.float32
BF16 = jnp.bfloat16
MESH = pl.DeviceIdType.MESH

EPS = 1e-6
ROPE_THETA = 10000.0
N_HEADS = 8
PAIRS = N_HEADS // 2
HEAD_DIM = 64
ROPE_DIM = 32
LANES = 128
Q_RANK = 384
KV_RANK = 256
N_CHIPS = 4
ADAM_LR, ADAM_B1, ADAM_B2, ADAM_EPS, ADAM_WD, ADAM_STEP = 0.001, 0.9, 0.999, 1e-08, 0.01, 10
VMEM_LIMIT = 48 * 1024 * 1024
LOG2E = 1.4426950408889634
LN2 = 0.6931471805599453
NN = (((1,), (0,)), ((), ()))
NT = (((1,), (1,)), ((), ()))
TN = (((0,), (0,)), ((), ()))


def _params(sem=None):
    return pltpu.CompilerParams(dimension_semantics=sem, vmem_limit_bytes=VMEM_LIMIT)


def _fit(block, dim):
    if dim <= block:
        return dim
    return next(b for b in range(block - block % LANES, 0, -LANES) if dim % b == 0)


def _row_block(rows):
    return next(b for b in (256, 128, 64, 32, 16, 8) if rows % b == 0)


def gridded(body, *, name, grid, in_specs, out_specs, out_shape, ins, semantics, side=None):
    if side is None:
        return pl.pallas_call(body, name=name, grid=grid, in_specs=in_specs, out_specs=out_specs, out_shape=out_shape,
                              compiler_params=_params(semantics))(*ins)
    n_in, n_out = len(in_specs), len(out_specs)
    steps = 1
    for extent in grid:
        steps *= extent

    def riding(*refs):
        main_in, s_in, main_out, s_out, sems = _split_refs(refs, [n_in, len(side.ins), n_out, len(side.out_shapes), 2])
        step = 0
        for axis, extent in enumerate(grid):
            step = step * extent + pl.program_id(axis)

        @pl.when(step == 0)
        def _():
            side.first(s_in, s_out, *sems)

        body(*main_in, *main_out)

        @pl.when(step == steps - 1)
        def _():
            if side.mid is not None:
                side.mid(s_in, s_out, *sems)
            side.last(s_in, s_out, *sems)

    s_in_specs, s_out_specs = side.specs()
    return pl.pallas_call(
        riding, name=name, grid=grid, in_specs=list(in_specs) + s_in_specs, out_specs=list(out_specs) + s_out_specs,
        out_shape=list(out_shape) + side.out_shapes, scratch_shapes=side.sems(),
        compiler_params=_params(("arbitrary",) * len(grid)))(*ins, *side.ins)


def rmsnorm(x, g, *, out_dtype, name, bt=512, side=None):
    t, d = x.shape
    bt = min(bt, t)

    def body(x_ref, g_ref, o_ref):
        xv = x_ref[...].astype(F32)
        r = lax.rsqrt(jnp.mean(xv * xv, axis=-1, keepdims=True) + EPS)
        o_ref[...] = (xv * r * g_ref[...]).astype(o_ref.dtype)

    return gridded(
        body, name=name, grid=(t // bt,),
        in_specs=[pl.BlockSpec((bt, d), lambda i: (i, 0)), pl.BlockSpec((1, d), lambda i: (0, 0))],
        out_specs=[pl.BlockSpec((bt, d), lambda i: (i, 0))],
        out_shape=[jax.ShapeDtypeStruct((t, d), out_dtype)],
        ins=[x, g], semantics=("parallel",), side=side)


def mm(a, b, *, trans_b=False, a_pro=None, extras=(), vecs=(), epilogue=None, out_dtypes, n_sums=0, t_blk=None, name,
       bm=1024, bn=1024, side=None):
    a_list = list(a) if isinstance(a, (list, tuple)) else [a]
    b_list = list(b) if isinstance(b, (list, tuple)) else [b]
    m = a_list[0].shape[0]
    n = b_list[0].shape[0] if trans_b else b_list[0].shape[1]
    ks = [x.shape[1] for x in a_list]
    if sum(ks) > 2048:
        bm = bm // 2
    bm, bn = _fit(bm, m), _fit(bn, n)
    assert n_sums == 0 or bn == n
    n_ab, n_ex, n_vec, n_out = len(a_list), len(extras), len(vecs), len(out_dtypes)
    n_t = 0 if t_blk is None else 1

    def body(*refs):
        a_refs, b_refs, ex, vs, outs, t_outs, sums = _split_refs(refs, [n_ab, n_ab, n_ex, n_vec, n_out, n_t, n_sums])
        acc = None
        for a_ref, b_ref in zip(a_refs, b_refs):
            a_tile = a_ref[...] if a_pro is None else a_pro(a_ref[...])
            part = lax.dot_general(a_tile, b_ref[...], NT if trans_b else NN, preferred_element_type=F32)
            acc = part if acc is None else acc + part
        res = epilogue(acc, *[e[...] for e in ex], *[v[...] for v in vs]) if epilogue is not None else (acc,)
        for o, r in zip(outs, res[:n_out]):
            o[...] = r.astype(o.dtype)
        for t_ref in t_outs:
            for u in range(bm // t_blk):
                t_ref[u] = res[0][u * t_blk:(u + 1) * t_blk, :].T.astype(t_ref.dtype)
        if n_sums:
            @pl.when(pl.program_id(0) == 0)
            def _():
                for s_ref in sums:
                    s_ref[...] = jnp.zeros_like(s_ref)

            for s_ref, r in zip(sums, res[n_out:]):
                s_ref[...] += r

    tile = pl.BlockSpec((bm, bn), lambda i, j: (i, j))
    vec = pl.BlockSpec((1, bn), lambda i, j: (0, j))
    t_specs, t_shapes = [], []
    if t_blk is not None:
        t_specs = [pl.BlockSpec((bm // t_blk, bn, t_blk), lambda i, j: (i, j, 0))]
        t_shapes = [jax.ShapeDtypeStruct((m // t_blk, n, t_blk), out_dtypes[0])]
    a_specs = [pl.BlockSpec((bm, k), lambda i, j: (i, 0)) for k in ks]
    b_specs = [pl.BlockSpec((bn, k), lambda i, j: (j, 0)) if trans_b else pl.BlockSpec((k, bn), lambda i, j: (0, j)) for k in ks]
    return gridded(
        body, name=name, grid=(m // bm, n // bn),
        in_specs=a_specs + b_specs + [tile] * n_ex + [vec] * n_vec,
        out_specs=[tile] * n_out + t_specs + [vec] * n_sums,
        out_shape=[jax.ShapeDtypeStruct((m, n), dt) for dt in out_dtypes] + t_shapes + [jax.ShapeDtypeStruct((1, n), F32)] * n_sums,
        ins=[*a_list, *b_list, *extras, *vecs],
        semantics=("arbitrary", "arbitrary") if n_sums else ("parallel", "parallel"), side=side)


def mm_tn(a, b, *, a_pro=None, name, col_shards=1, bf16_copy=False, bk=1024, bn=1024, bt=2048):
    t, k = a.shape
    n = b.shape[1]
    ns = n // col_shards
    bk, bn, bt = _fit(bk, k), _fit(bn, ns), _fit(bt, t)
    per = ns // bn
    last = t // bt - 1

    def body(a_ref, b_ref, o_ref, *copy_ref):
        @pl.when(pl.program_id(2) == 0)
        def _():
            o_ref[...] = jnp.zeros_like(o_ref)

        a_tile = a_ref[...] if a_pro is None else a_pro(a_ref[...])
        o_ref[...] += lax.dot_general(a_tile, b_ref[...], TN, preferred_element_type=F32)
        if bf16_copy:
            @pl.when(pl.program_id(2) == last)
            def _():
                copy_ref[0][...] = o_ref[...].astype(BF16)

    if col_shards == 1:
        out_spec = pl.BlockSpec((bk, bn), lambda i, j, s: (i, j))
        shape = (k, n)
    else:
        out_spec = pl.BlockSpec((None, bk, bn), lambda i, j, s: (j // per, i, j % per))
        shape = (col_shards, k, ns)
    out_specs, out_shape = out_spec, jax.ShapeDtypeStruct(shape, F32)
    if bf16_copy:
        out_specs, out_shape = [out_spec, out_spec], [out_shape, jax.ShapeDtypeStruct(shape, BF16)]
    return pl.pallas_call(
        body, name=name, grid=(k // bk, n // bn, t // bt),
        in_specs=[pl.BlockSpec((bt, bk), lambda i, j, s: (s, i)), pl.BlockSpec((bt, bn), lambda i, j, s: (s, j))],
        out_specs=out_specs, out_shape=out_shape,
        compiler_params=_params(("parallel", "parallel", "arbitrary")),
    )(a, b)


def _split3(x):
    hi = x.astype(BF16)
    r1 = x - hi.astype(F32)
    mid = r1.astype(BF16)
    lo = (r1 - mid.astype(F32)).astype(BF16)
    return hi, mid, lo


def cumsum_rows(x, *, reverse, name, bc=512):
    t, d = x.shape
    bc = min(bc, t)
    nb = t // bc

    def body(x_ref, o_ref, carry):
        @pl.when(pl.program_id(0) == 0)
        def _():
            carry[...] = jnp.zeros_like(carry)

        r = lax.broadcasted_iota(jnp.int32, (bc, bc), 0)
        c = lax.broadcasted_iota(jnp.int32, (bc, bc), 1)
        tri = jnp.where((r <= c) if reverse else (r >= c), 1.0, 0.0).astype(BF16)
        hi, mid, lo = _split3(x_ref[...])
        s = (lax.dot_general(tri, hi, NN, preferred_element_type=F32)
             + lax.dot_general(tri, mid, NN, preferred_element_type=F32)
             + lax.dot_general(tri, lo, NN, preferred_element_type=F32)) + carry[0:1, :]
        o_ref[...] = s
        carry[0:1, :] = s[0:1, :] if reverse else s[bc - 1:bc, :]

    imap = (lambda i: (nb - 1 - i, 0)) if reverse else (lambda i: (i, 0))
    return pl.pallas_call(
        body, name=name, grid=(nb,),
        in_specs=[pl.BlockSpec((bc, d), imap)], out_specs=pl.BlockSpec((bc, d), imap),
        out_shape=jax.ShapeDtypeStruct((t, d), F32),
        scratch_shapes=[pltpu.VMEM((8, d), F32)],
        compiler_params=_params(("arbitrary",)),
    )(x)


def _rope(x, c, a, b):
    return x * c + pltpu.roll(x, LANES - ROPE_DIM // 2, 1) * a + pltpu.roll(x, ROPE_DIM // 2, 1) * b


def _rope_bwd(d, c, a, b):
    return d * c + pltpu.roll(d * a, ROPE_DIM // 2, 1) + pltpu.roll(d * b, LANES - ROPE_DIM // 2, 1)


S_CQ, S_CKV, S_KR, S_F, S_END = 0, Q_RANK, Q_RANK + KV_RANK, Q_RANK + KV_RANK + LANES, 1024
HW = N_HEADS * LANES
FW = N_HEADS * HEAD_DIM


def _rope_tables(pos_col, inv_row):
    ang = pos_col * inv_row
    cos, sin = jnp.cos(ang), jnp.sin(ang)
    lane = lax.broadcasted_iota(jnp.int32, (1, LANES), 1)
    first = (lane >= HEAD_DIM) & (lane < HEAD_DIM + ROPE_DIM // 2)
    second = (lane >= HEAD_DIM + ROPE_DIM // 2) & (lane < HEAD_DIM + ROPE_DIM)
    return (jnp.where(lane < HEAD_DIM, 1.0, jnp.where(first | second, cos, 0.0)), jnp.where(first, -sin, 0.0),
            jnp.where(second, sin, 0.0))


def mla_prep(small, g_q, g_kv, w_uq, w_ukv, pos_col, inv_row, b_f, *, name, bt=512):
    t = small.shape[0]
    bt = min(bt, t)

    def body(s_ref, gq_ref, gkv_ref, wq_ref, wkv_ref, pos_ref, inv_ref, bf_ref,
             mq_ref, mk_ref, mv_ref, lf_ref, cqn_ref, ckvn_ref, mqt_ref, mvt_ref):
        cq = s_ref[:, S_CQ:S_CKV]
        rq = lax.rsqrt(jnp.mean(cq * cq, axis=-1, keepdims=True) + EPS)
        cqn = (cq * rq * gq_ref[...]).astype(BF16)
        ckv = s_ref[:, S_CKV:S_KR]
        rkv = lax.rsqrt(jnp.mean(ckv * ckv, axis=-1, keepdims=True) + EPS)
        ckvn = (ckv * rkv * gkv_ref[...]).astype(BF16)
        cqn_ref[...] = cqn
        ckvn_ref[...] = ckvn
        tc, ta, tb = _rope_tables(pos_ref[...], inv_ref[...])
        q = jnp.dot(cqn, wq_ref[...], preferred_element_type=F32)
        kv = jnp.dot(ckvn, wkv_ref[...], preferred_element_type=F32)
        kr = _rope(s_ref[:, S_KR:S_F], tc, ta, tb)
        for h in range(N_HEADS):
            sl = slice(h * LANES, (h + 1) * LANES)
            roped = _rope(q[:, sl], tc, ta, tb)
            mq_ref[:, sl] = roped.astype(BF16)
            mqt_ref[0, sl, :] = roped.T.astype(BF16)
            mk_ref[:, sl] = (kv[:, sl] + kr).astype(BF16)
        mv_ref[...] = kv[:, HW:].astype(BF16)
        mvt_ref[0] = kv[:, HW:].T.astype(BF16)
        z = s_ref[:, S_F:S_END - LANES] + bf_ref[...]
        lf_ref[...] = jnp.minimum(z, 0.0) - jnp.log(1.0 + jnp.exp(-jnp.abs(z)))

    def row(w):
        return pl.BlockSpec((bt, w), lambda i: (i, 0))

    def full(arr):
        return pl.BlockSpec(arr.shape, lambda i: (0, 0))

    return pl.pallas_call(
        body, name=name, grid=(t // bt,),
        in_specs=[row(S_END), full(g_q), full(g_kv), full(w_uq), full(w_ukv), row(1), full(inv_row), full(b_f)],
        out_specs=[row(HW), row(HW), row(FW), row(LANES), row(Q_RANK), row(KV_RANK),
                   pl.BlockSpec((1, HW, bt), lambda i: (i, 0, 0)), pl.BlockSpec((1, FW, bt), lambda i: (i, 0, 0))],
        out_shape=[jax.ShapeDtypeStruct((t, HW), BF16)] * 2 + [jax.ShapeDtypeStruct((t, FW), BF16), jax.ShapeDtypeStruct((t, LANES), F32),
                   jax.ShapeDtypeStruct((t, Q_RANK), BF16), jax.ShapeDtypeStruct((t, KV_RANK), BF16),
                   jax.ShapeDtypeStruct((t // bt, HW, bt), BF16), jax.ShapeDtypeStruct((t // bt, FW, bt), BF16)],
        compiler_params=_params(("parallel",)),
    )(small, g_q, g_kv, w_uq, w_ukv, pos_col, inv_row, b_f)


def mla_prep_bwd(dmq, dmk, dmv, dlf, small, g_q, g_kv, w_uq, w_ukv, pos_col, inv_row, b_f, *, name, bt=512):
    t = small.shape[0]
    bt = min(bt, t)

    def body(dmq_ref, dmk_ref, dmv_ref, dlf_ref, s_ref, gq_ref, gkv_ref, wq_ref, wkv_ref, pos_ref, inv_ref, bf_ref,
             ds_ref, dq_ref, dkv_ref, dgq_ref, dgkv_ref, db_ref):
        tc, ta, tb = _rope_tables(pos_ref[...], inv_ref[...])
        lane = lax.broadcasted_iota(jnp.int32, (1, LANES), 1)
        dkr = jnp.zeros((bt, LANES), F32)
        for h in range(N_HEADS):
            sl = slice(h * LANES, (h + 1) * LANES)
            dq_ref[:, sl] = _rope_bwd(dmq_ref[:, sl], tc, ta, tb).astype(BF16)
            dkr = dkr + dmk_ref[:, sl]
        dkv_ref[:, :HW] = dmk_ref[...].astype(BF16)
        dkv_ref[:, HW:] = dmv_ref[...].astype(BF16)
        in_rope = (lane >= HEAD_DIM) & (lane < HEAD_DIM + ROPE_DIM)
        ds_ref[:, S_KR:S_F] = jnp.where(in_rope, _rope_bwd(dkr, tc, ta, tb), 0.0).astype(BF16)

        def norm_bwd(raw, g_ref, dn, dg_ref):
            r = lax.rsqrt(jnp.mean(raw * raw, axis=-1, keepdims=True) + EPS)
            u = dn * g_ref[...]
            dot = jnp.mean(u * raw, axis=-1, keepdims=True)
            dg_ref[...] += jnp.sum(dn * (raw * r), axis=0, keepdims=True)
            return r * u - raw * (r * r * r * dot)

        @pl.when(pl.program_id(0) == 0)
        def _():
            dgq_ref[...] = jnp.zeros_like(dgq_ref)
            dgkv_ref[...] = jnp.zeros_like(dgkv_ref)
            db_ref[...] = jnp.zeros_like(db_ref)

        dcqn = lax.dot_general(dq_ref[...], wq_ref[...], NT, preferred_element_type=F32)
        ds_ref[:, S_CQ:S_CKV] = norm_bwd(s_ref[:, S_CQ:S_CKV], gq_ref, dcqn, dgq_ref).astype(BF16)
        dckvn = lax.dot_general(dkv_ref[...], wkv_ref[...], NT, preferred_element_type=F32)
        ds_ref[:, S_CKV:S_KR] = norm_bwd(s_ref[:, S_CKV:S_KR], gkv_ref, dckvn, dgkv_ref).astype(BF16)
        z = s_ref[:, S_F:S_END - LANES] + bf_ref[...]
        dz = jnp.where(lane < N_HEADS, dlf_ref[...] / (1.0 + jnp.exp(z)), 0.0)
        db_ref[...] += jnp.sum(dz, axis=0, keepdims=True)
        ds_ref[:, S_F:S_END - LANES] = dz.astype(BF16)
        ds_ref[:, S_END - LANES:] = jnp.zeros((bt, LANES), BF16)

    def row(w):
        return pl.BlockSpec((bt, w), lambda i: (i, 0))

    def full(arr):
        return pl.BlockSpec(arr.shape, lambda i: (0, 0))

    def vec(w):
        return pl.BlockSpec((1, w), lambda i: (0, 0))

    return pl.pallas_call(
        body, name=name, grid=(t // bt,),
        in_specs=[row(HW), row(HW), row(FW), row(LANES), row(S_END), full(g_q), full(g_kv), full(w_uq), full(w_ukv),
                  row(1), full(inv_row), full(b_f)],
        out_specs=[row(S_END), row(HW), row(HW + FW), vec(Q_RANK), vec(KV_RANK), vec(LANES)],
        out_shape=[jax.ShapeDtypeStruct((t, S_END), BF16), jax.ShapeDtypeStruct((t, HW), BF16),
                   jax.ShapeDtypeStruct((t, HW + FW), BF16), jax.ShapeDtypeStruct((1, Q_RANK), F32),
                   jax.ShapeDtypeStruct((1, KV_RANK), F32), jax.ShapeDtypeStruct((1, LANES), F32)],
        compiler_params=_params(("arbitrary",)),
    )(dmq, dmk, dmv, dlf, small, g_q, g_kv, w_uq, w_ukv, pos_col, inv_row, b_f)


class Side:
    def __init__(self, ins, out_shapes, n_sems, first, last, mid=None):
        self.ins, self.out_shapes, self.n_sems = list(ins), list(out_shapes), n_sems
        self.first, self.mid, self.last = first, mid, last

    def specs(self):
        return [ANY] * len(self.ins), [ANY] * len(self.out_shapes)

    def sems(self):
        return [pltpu.SemaphoreType.DMA((self.n_sems,)), pltpu.SemaphoreType.DMA((self.n_sems,))]


def _lane():
    return lax.broadcasted_iota(jnp.int32, (1, LANES), 1)


def _halves(x):
    zero = jnp.zeros_like(x)
    return [jnp.where(_lane() < HEAD_DIM, x, zero), jnp.where(_lane() >= HEAD_DIM, x, zero)]


def _groups(x):
    return [x[:, :LANES], x[:, LANES:]]


def _pick_row(tile, h):
    row = lax.broadcasted_iota(jnp.int32, (tile.shape[0], 1), 0)
    return jnp.sum(jnp.where(row == h, tile, 0.0), axis=0, keepdims=True)


def _pick_lane(tile, h):
    return jnp.sum(jnp.where(_lane() == h, tile, 0.0), axis=1, keepdims=True)


def _row_halves(x):
    row = lax.broadcasted_iota(jnp.int32, (LANES, 1), 0)
    zero = jnp.zeros_like(x)
    return [jnp.where(row < HEAD_DIM, x, zero), jnp.where(row >= HEAD_DIM, x, zero)]


def _below_diagonal(s):
    r = lax.broadcasted_iota(jnp.int32, s.shape, 0)
    c = lax.broadcasted_iota(jnp.int32, s.shape, 1)
    return jnp.where(c <= r, s, -jnp.inf)


def _above_diagonal(s):
    r = lax.broadcasted_iota(jnp.int32, s.shape, 0)
    c = lax.broadcasted_iota(jnp.int32, s.shape, 1)
    return jnp.where(r <= c, s, -jnp.inf)


def _split_refs(refs, counts):
    out, at = [], 0
    for n in counts:
        out.append(refs[at:at + n])
        at += n
    return out


def flash_fwd(qt_arr, k_arr, vt_arr, f_cum, *, qoff, koff, voff, pair, scale, name, blk=512, side=None):
    t = k_arr.shape[0]
    tblk = qt_arr.shape[2]
    blk = max(min(blk, t), tblk)
    sub = blk // tblk
    nb = t // blk
    steps = PAIRS * nb
    w = LANES if pair else 2 * LANES
    has_bias = f_cum is not None
    ins = [qt_arr, k_arr, vt_arr] + ([f_cum] if has_bias else [])

    def wide(ref, first):
        parts = [ref[first + u] for u in range(sub)]
        return parts[0] if sub == 1 else jnp.concatenate(parts, axis=1)
    s_ins, s_outs = (side.ins, side.out_shapes) if side else ([], [])

    def body(*refs):
        main, si, outs, so, sems = _split_refs(refs, [len(ins), len(s_ins), 2, len(s_outs), 2 if side else 0])
        qt_ref, k_ref, vt_ref = main[:3]
        f_ref = main[3] if has_bias else None
        o_ref, st_ref = outs
        g, i = pl.program_id(0), pl.program_id(1)
        step_id = g * nb + i
        if side:
            @pl.when(step_id == 0)
            def _():
                side.first(si, so, *sems)

            if side.mid is not None:
                @pl.when(step_id == (3 * steps) // 4)
                def _():
                    side.mid(si, so, *sems)

        qt = (wide(qt_ref, 0).astype(F32) * (scale * LOG2E)).astype(BF16)
        qts = _row_halves(qt) if pair else [qt[:LANES], qt[LANES:]]

        def k_of(kk, n):
            return kk if pair else kk[:, n * LANES:(n + 1) * LANES]

        def with_ones(vt_rows):
            return jnp.concatenate([vt_rows, jnp.ones((ACC_ROWS - HEAD_DIM, vt_rows.shape[1]), BF16)], axis=0)

        def step(j, carry, diagonal):
            rows = pl.ds(pl.multiple_of(j * blk, blk), blk)
            kk = k_ref[rows, :]
            vt = wide(vt_ref, sub * j)
            out = []
            for n in range(2):
                m, acc = carry[n]
                s = jnp.dot(k_of(kk, n), qts[n], preferred_element_type=F32)
                if has_bias:
                    s = s - LOG2E * _pick_lane(f_ref[rows, :], 2 * g + n)
                if diagonal:
                    s = _above_diagonal(s)
                m_new = jnp.maximum(m, jnp.max(s, axis=0, keepdims=True))
                p = jnp.exp2(s - m_new).astype(BF16)
                out.append((m_new, jnp.exp2(m - m_new) * acc
                            + jnp.dot(with_ones(vt[n * HEAD_DIM:(n + 1) * HEAD_DIM]), p, preferred_element_type=F32)))
            return tuple(out)

        def diagonal_in_halves(carry):
            h = tblk
            halves = [pl.ds(pl.multiple_of(i * blk, blk), h), pl.ds(pl.multiple_of(i * blk + h, h), h)]
            k_top, k_bot = k_ref[halves[0], :], k_ref[halves[1], :]
            vt_top, vt_bot = vt_ref[sub * i], vt_ref[sub * i + 1]
            out = []
            for n in range(2):
                m, acc = carry[n]
                heads = slice(n * HEAD_DIM, (n + 1) * HEAD_DIM)
                s_top = jnp.dot(k_of(k_top, n), qts[n], preferred_element_type=F32)
                s_bot = jnp.dot(k_of(k_bot, n), qts[n][:, h:], preferred_element_type=F32)
                if has_bias:
                    s_top = s_top - LOG2E * _pick_lane(f_ref[halves[0], :], 2 * g + n)
                    s_bot = s_bot - LOG2E * _pick_lane(f_ref[halves[1], :], 2 * g + n)
                s_top, s_bot = _above_diagonal(s_top), _above_diagonal(s_bot)
                m_top = jnp.maximum(m, jnp.max(s_top, axis=0, keepdims=True))
                m_new = jnp.concatenate([m_top[:, :h], jnp.maximum(m_top[:, h:], jnp.max(s_bot, axis=0, keepdims=True))], axis=1)
                p_top = jnp.exp2(s_top - m_new).astype(BF16)
                p_bot = jnp.exp2(s_bot - m_new[:, h:]).astype(BF16)
                late = jnp.concatenate([jnp.zeros((ACC_ROWS, h), F32),
                                        jnp.dot(with_ones(vt_bot[heads]), p_bot, preferred_element_type=F32)], axis=1)
                out.append((m_new, jnp.exp2(m - m_new) * acc
                            + jnp.dot(with_ones(vt_top[heads]), p_top, preferred_element_type=F32) + late))
            return tuple(out)

        init = tuple((jnp.full((1, blk), -jnp.inf, F32), jnp.zeros((ACC_ROWS, blk), F32)) for _ in range(2))
        carry = lax.fori_loop(0, i, lambda j, c: step(j, c, False), init)
        (ma, acca), (mb, accb) = diagonal_in_halves(carry) if sub == 2 else step(i, carry, True)
        la, lb = acca[HEAD_DIM:HEAD_DIM + 1], accb[HEAD_DIM:HEAD_DIM + 1]
        o_ref[...] = jnp.concatenate([acca[:HEAD_DIM] / la, accb[:HEAD_DIM] / lb], axis=0).T
        row = lax.broadcasted_iota(jnp.int32, (LANES, 1), 0)
        st_ref[0] = jnp.where(row == 0, ma + jnp.log2(la), jnp.where(row == 1, mb + jnp.log2(lb), 0.0)).T
        if side:
            @pl.when(step_id == steps - 1)
            def _():
                side.last(si, so, *sems)

    in_specs = [pl.BlockSpec((sub, w, tblk), lambda g, i: (i, qoff + g, 0)), pl.BlockSpec((t, w), lambda g, i: (0, koff + g)),
                pl.BlockSpec((t // tblk, LANES, tblk), lambda g, i: (0, voff + g, 0))]
    if has_bias:
        in_specs.append(pl.BlockSpec((t, LANES), lambda g, i: (0, 0)))
    s_in_specs, s_out_specs = side.specs() if side else ([], [])
    return pl.pallas_call(
        body, name=name, grid=(PAIRS, nb), in_specs=in_specs + s_in_specs,
        out_specs=[pl.BlockSpec((blk, LANES), lambda g, i: (i, g)), pl.BlockSpec((1, blk, LANES), lambda g, i: (g, i, 0))]
        + s_out_specs,
        out_shape=[jax.ShapeDtypeStruct((t, PAIRS * LANES), F32), jax.ShapeDtypeStruct((PAIRS, t, LANES), F32)] + list(s_outs),
        scratch_shapes=side.sems() if side else [],
        compiler_params=_params(("arbitrary", "arbitrary")),
    )(*ins, *s_ins)


def mix_norm(fo, mo, g_fo, g_mo, *, name, bt=512):
    t, d = fo.shape
    bt = min(bt, t)

    def body(fo_ref, mo_ref, gf_ref, gm_ref, o_ref):
        for n, (x_ref, g_ref) in enumerate(((fo_ref, gf_ref), (mo_ref, gm_ref))):
            xv = x_ref[...]
            r = lax.rsqrt(jnp.mean(xv * xv, axis=-1, keepdims=True) + EPS)
            o_ref[:, n * d:(n + 1) * d] = (xv * r * g_ref[...]).astype(BF16)

    row = pl.BlockSpec((bt, d), lambda i: (i, 0))
    vec = pl.BlockSpec((1, d), lambda i: (0, 0))
    return pl.pallas_call(
        body, name=name, grid=(t // bt,), in_specs=[row, row, vec, vec],
        out_specs=pl.BlockSpec((bt, 2 * d), lambda i: (i, 0)),
        out_shape=jax.ShapeDtypeStruct((t, 2 * d), BF16),
        compiler_params=_params(("parallel",)),
    )(fo, mo, g_fo, g_mo)


def mix_norm_bwd(dx1b, w_o, fo, mo, g_fo, g_mo, st_f, st_m, *, name, bt=512, side=None):
    t, d = fo.shape
    bt = min(bt, t)

    def body(dx_in_ref, w_ref, fo_ref, mo_ref, gf_ref, gm_ref, sf_ref, sm_ref, dfo_ref, dmo_ref, dgf_ref, dgm_ref,
             sfo_ref, smo_ref, dfot_ref, dmot_ref):
        @pl.when(pl.program_id(0) == 0)
        def _():
            dgf_ref[...] = jnp.zeros_like(dgf_ref)
            dgm_ref[...] = jnp.zeros_like(dgm_ref)

        dmixed = lax.dot_general(dx_in_ref[...], w_ref[...], NT, preferred_element_type=F32)
        groups = ((fo_ref, gf_ref, dfo_ref, dgf_ref, sf_ref, sfo_ref, dfot_ref),
                  (mo_ref, gm_ref, dmo_ref, dgm_ref, sm_ref, smo_ref, dmot_ref))
        for n, (x_ref, g_ref, dx_ref, dg_ref, st_ref, sto_ref, dxt_ref) in enumerate(groups):
            xv = x_ref[...]
            dhv = dmixed[:, n * d:(n + 1) * d]
            r = lax.rsqrt(jnp.mean(xv * xv, axis=-1, keepdims=True) + EPS)
            u = dhv * g_ref[...]
            dxf = r * u - xv * (r * r * r * jnp.mean(u * xv, axis=-1, keepdims=True))
            dxb = dxf.astype(BF16)
            dx_ref[...] = dxb
            dxt_ref[0] = dxf.T.astype(BF16)
            dg_ref[...] += jnp.sum(dhv * (xv * r), axis=0, keepdims=True)
            prod = xv * dxb.astype(F32)
            for g in range(PAIRS):
                grp = prod[:, g * LANES:(g + 1) * LANES]
                da = jnp.sum(jnp.where(_lane() < HEAD_DIM, grp, 0.0), axis=1, keepdims=True)
                db = jnp.sum(jnp.where(_lane() >= HEAD_DIM, grp, 0.0), axis=1, keepdims=True)
                sto_ref[g] = jnp.where(_lane() == 2, da, jnp.where(_lane() == 3, db, st_ref[g]))

    row = pl.BlockSpec((bt, d), lambda i: (i, 0))
    vec = pl.BlockSpec((1, d), lambda i: (0, 0))
    stat = pl.BlockSpec((PAIRS, bt, LANES), lambda i: (0, i, 0))
    return gridded(
        body, name=name, grid=(t // bt,),
        in_specs=[pl.BlockSpec((bt, dx1b.shape[1]), lambda i: (i, 0)), pl.BlockSpec(w_o.shape, lambda i: (0, 0)),
                  row, row, vec, vec, stat, stat],
        out_specs=[row, row, vec, vec, stat, stat] + [pl.BlockSpec((1, d, bt), lambda i: (i, 0, 0))] * 2,
        out_shape=[jax.ShapeDtypeStruct((t, d), BF16)] * 2 + [jax.ShapeDtypeStruct((1, d), F32)] * 2
        + [jax.ShapeDtypeStruct(st_f.shape, F32)] * 2 + [jax.ShapeDtypeStruct((t // bt, d, bt), BF16)] * 2,
        ins=[dx1b, w_o, fo, mo, g_fo, g_mo, st_f, st_m], semantics=("arbitrary",), side=side)


def flash_bwd(q_arr, qt_arr, k_arr, v_arr, do_arr, dot_arr, st, f_blocks, *, qoff, koff, voff, pair, scale, name, qblk=1024,
              side=None):
    t = q_arr.shape[0]
    blk = qt_arr.shape[2]
    qblk = max(min(qblk, t), blk)
    sub = qblk // blk
    nb, nbq = t // blk, t // qblk
    w = LANES if pair else 2 * LANES
    hw = w // 2
    has_bias = f_blocks is not None
    split = _halves if pair else _groups
    ins = [q_arr, qt_arr, k_arr, v_arr, do_arr, dot_arr, st] + ([f_blocks] if has_bias else [])
    n_out = 4 if has_bias else 3
    s_ins, s_outs = (side.ins, side.out_shapes) if side else ([], [])

    def wide(ref, first, count):
        parts = [ref[first + u] for u in range(count)]
        return parts[0] if count == 1 else jnp.concatenate(parts, axis=1)

    def body(*refs):
        main, si, outs, so, sems = _split_refs(refs, [len(ins), len(s_ins), n_out, len(s_outs), 2 if side else 0])
        q_ref, qt_ref, k_ref, v_ref, do_ref, dot_ref, st_ref = main[:7]
        dq_ref, dk_ref, dv_ref = outs[:3]
        g, j = pl.program_id(0), pl.program_id(1)
        step_id = g * nb + j
        if side:
            @pl.when(step_id == 0)
            def _():
                side.first(si, so, *sems)

        @pl.when(j == 0)
        def _():
            dq_ref[...] = jnp.zeros_like(dq_ref)

        kk, vv = k_ref[...], v_ref[...]
        ks = [kk, kk] if pair else _groups(kk)
        if has_bias:
            f_ref, df_ref = main[7], outs[3]
            fk = [LOG2E * _pick_row(f_ref[0], 2 * g + n) for n in range(2)]

            @pl.when(step_id == 0)
            def _():
                df_ref[...] = jnp.zeros_like(df_ref)

        def step(tb, count, carry, diagonal):
            rows = pl.ds(pl.multiple_of(tb * blk, blk), count * blk)
            qs = split((q_ref[rows, :].astype(F32) * (scale * LOG2E)).astype(BF16))
            qt = (wide(qt_ref, tb, count).astype(F32) * (scale * LOG2E)).astype(BF16)
            dos = [h.astype(BF16) for h in _halves(do_ref[rows, :].astype(F32))]
            dot = wide(dot_ref, tb, count)
            stats = st_ref[0, rows, :]
            new, dqs, row_sums = [], [], []
            for n in range(2):
                dkt, dvt, dfk = carry[n]
                s = lax.dot_general(qs[n], ks[n], NT, preferred_element_type=F32)
                if has_bias:
                    s = s - fk[n]
                if diagonal:
                    s = _below_diagonal(s)
                p = jnp.exp2(s - stats[:, n:n + 1])
                dp = lax.dot_general(dos[n], vv, NT, preferred_element_type=F32)
                ds = p * (dp - stats[:, 2 + n:3 + n])
                dsb = ds.astype(BF16)
                dvt = dvt + jnp.dot(dot[n * HEAD_DIM:(n + 1) * HEAD_DIM], p.astype(BF16), preferred_element_type=F32)
                dkt = dkt + jnp.dot(qt[n * hw:(n + 1) * hw], dsb, preferred_element_type=F32)
                dqs.append(jnp.dot(dsb, ks[n], preferred_element_type=F32))
                if has_bias:
                    dfk = dfk - jnp.sum(ds, axis=0, keepdims=True)
                    row_sums.append(jnp.sum(ds, axis=1, keepdims=True))
                new.append((dkt, dvt, dfk))
            dq = (jnp.where(_lane() < HEAD_DIM, dqs[0], dqs[1]) if pair else jnp.concatenate(dqs, axis=1)) * scale
            if has_bias:
                df_ref[rows, :] += jnp.where(_lane() == 2 * g, row_sums[0], jnp.where(_lane() == 2 * g + 1, row_sums[1], 0.0))
            dq_ref[rows, :] += dq
            return tuple(new)

        init = tuple((jnp.zeros((hw, blk), F32), jnp.zeros((HEAD_DIM, blk), F32), jnp.zeros((1, blk), F32)) for _ in range(2))
        carry = step(j, 1, init, True)
        whole = j // sub + 1
        carry = lax.fori_loop(j + 1, whole * sub, lambda tb, c: step(tb, 1, c, False), carry)
        (dka, dva, dfa), (dkb, dvb, dfb) = lax.fori_loop(whole, nbq, lambda i, c: step(i * sub, sub, c, False), carry)
        dk_ref[...] = (jnp.concatenate([dka, dkb], axis=0).T * LN2).astype(BF16)
        dv_ref[...] = jnp.concatenate([dva, dvb], axis=0).T.astype(BF16)
        if has_bias:
            row = lax.broadcasted_iota(jnp.int32, (LANES, 1), 0)
            by_head = jnp.where(row == 2 * g, dfa, jnp.where(row == 2 * g + 1, dfb, 0.0))
            df_ref[pl.ds(pl.multiple_of(j * blk, blk), blk), :] += by_head.T
        if side:
            @pl.when(step_id == PAIRS * nb - 1)
            def _():
                side.last(si, so, *sems)

    in_specs = [pl.BlockSpec((t, w), lambda g, j: (0, qoff + g)), pl.BlockSpec((nb, w, blk), lambda g, j: (0, qoff + g, 0)),
                pl.BlockSpec((blk, w), lambda g, j: (j, koff + g)), pl.BlockSpec((blk, LANES), lambda g, j: (j, voff + g)),
                pl.BlockSpec((t, LANES), lambda g, j: (0, g)), pl.BlockSpec((nb, LANES, blk), lambda g, j: (0, g, 0)),
                pl.BlockSpec((1, t, LANES), lambda g, j: (g, 0, 0))]
    out_specs = [pl.BlockSpec((t, w), lambda g, j: (0, g)), pl.BlockSpec((blk, w), lambda g, j: (j, g)),
                 pl.BlockSpec((blk, LANES), lambda g, j: (j, g))]
    out_shape = [jax.ShapeDtypeStruct((t, PAIRS * w), F32), jax.ShapeDtypeStruct((t, PAIRS * w), BF16),
                 jax.ShapeDtypeStruct((t, PAIRS * LANES), BF16)]
    if has_bias:
        in_specs.append(pl.BlockSpec((1, N_HEADS, blk), lambda g, j: (j, 0, 0)))
        out_specs.append(pl.BlockSpec((t, LANES), lambda g, j: (0, 0)))
        out_shape.append(jax.ShapeDtypeStruct((t, LANES), F32))
    s_in_specs, s_out_specs = side.specs() if side else ([], [])
    return pl.pallas_call(
        body, name=name, grid=(PAIRS, nb), in_specs=in_specs + s_in_specs, out_specs=out_specs + s_out_specs,
        out_shape=out_shape + list(s_outs), scratch_shapes=side.sems() if side else [],
        compiler_params=_params(("arbitrary", "arbitrary")),
    )(*ins, *s_ins)


def _adamw_math(w, g, m, v):
    nm = ADAM_B1 * m + (1.0 - ADAM_B1) * g
    nv = ADAM_B2 * v + (1.0 - ADAM_B2) * (g * g)
    m_hat = nm / (1.0 - ADAM_B1 ** ADAM_STEP)
    v_hat = nv / (1.0 - ADAM_B2 ** ADAM_STEP)
    return -ADAM_LR * (m_hat / (jnp.sqrt(v_hat) + ADAM_EPS) + ADAM_WD * w), nm, nv


def adamw_vectors(ws, g_rows, ms, vs, *, name):
    k = len(ws)

    def body(g_ref, *refs):
        w_refs, m_refs, v_refs, outs = _split_refs(refs, [k, k, k, 4 * k])
        for i in range(k):
            g = g_ref[i:i + 1, :w_refs[i].shape[1]]
            outs[4 * i][...] = g
            outs[4 * i + 1][...], outs[4 * i + 2][...], outs[4 * i + 3][...] = _adamw_math(
                w_refs[i][...], g, m_refs[i][...], v_refs[i][...])

    flat = pl.pallas_call(
        body, name=name, out_shape=[jax.ShapeDtypeStruct(w.shape, F32) for w in ws for _ in range(4)],
        compiler_params=_params(),
    )(g_rows, *ws, *ms, *vs)
    return [flat[4 * i:4 * i + 4] for i in range(k)]


def adamw_halves(w, g_mine, g_other, m, v, core, *, name):
    _, k, n = w.shape
    br = _row_block(k // 2)
    nh = k // 2 // br

    def body(c_ref, w_ref, gm_ref, go_ref, m_ref, v_ref, g_out, d_ref, nm_ref, nv_ref):
        gv = jnp.where(pl.program_id(0) == c_ref[0], gm_ref[...], go_ref[...])
        g_out[0] = gv
        d_ref[0], nm_ref[0], nv_ref[0] = _adamw_math(w_ref[0], gv, m_ref[0], v_ref[0])

    full = pl.BlockSpec((1, br, n), lambda hb, i, c: (0, hb * nh + i, 0))
    half = pl.BlockSpec((br, n), lambda hb, i, c: (i, 0))
    return pl.pallas_call(
        body, name=name,
        grid_spec=pltpu.PrefetchScalarGridSpec(num_scalar_prefetch=1, grid=(2, nh), in_specs=[full, half, half, full, full],
                                               out_specs=[full] * 4),
        out_shape=[jax.ShapeDtypeStruct(w.shape, F32)] * 4,
        compiler_params=_params(("parallel", "parallel")),
    )(core, w, g_mine, g_other, m, v)


def adamw_halves_t(wt, mt, vt, gt_mine, gt_other, core, *, name, bc=128):
    n, _, k = wt.shape
    nh = k // 2 // bc

    def body(c_ref, w_ref, m_ref, v_ref, gm_ref, go_ref, g_out, d_ref, nm_ref, nv_ref):
        gv = jnp.where(pl.program_id(0) == c_ref[0], gm_ref[...], go_ref[...])
        g_out[:, 0, :] = gv
        d_ref[:, 0, :], nm_ref[:, 0, :], nv_ref[:, 0, :] = _adamw_math(w_ref[:, 0, :], gv, m_ref[:, 0, :], v_ref[:, 0, :])

    full = pl.BlockSpec((n, 1, bc), lambda hb, i, c: (0, 0, hb * nh + i))
    half = pl.BlockSpec((n, bc), lambda hb, i, c: (0, i))
    return pl.pallas_call(
        body, name=name,
        grid_spec=pltpu.PrefetchScalarGridSpec(num_scalar_prefetch=1, grid=(2, nh), in_specs=[full, full, full, half, half],
                                               out_specs=[full] * 4),
        out_shape=[jax.ShapeDtypeStruct(wt.shape, F32)] * 4,
        compiler_params=_params(("parallel", "parallel")),
    )(core, wt, mt, vt, gt_mine, gt_other)


def add_pair(dw, recv, core, *, name):
    n4, k, n = dw.shape
    half = (1, k // 2, n) if split_axis(k) == 0 else (1, k, n // 2)
    mine = (lambda q, c: (q, c[0], 0)) if split_axis(k) == 0 else (lambda q, c: (q, 0, c[0]))

    def body(c_ref, a_ref, b_ref, o_ref):
        o_ref[...] = (a_ref[...] + b_ref[...].astype(F32)).astype(BF16)

    return pl.pallas_call(
        body, name=name,
        grid_spec=pltpu.PrefetchScalarGridSpec(
            num_scalar_prefetch=1, grid=(n4,),
            in_specs=[pl.BlockSpec(half, mine), pl.BlockSpec(half, lambda q, c: (q, 0, 0))],
            out_specs=pl.BlockSpec(half, lambda q, c: (q, 0, 0))),
        out_shape=jax.ShapeDtypeStruct((n4,) + half[1:], BF16),
        compiler_params=_params(("parallel",)),
    )(core, dw, recv)


def sum_chips(parts, *, name):
    n4, r, n = parts.shape
    if r % 16 == 0:
        br, bc = _row_block(r), n
    else:
        br, bc = r, LANES

    def body(p_ref, o_ref):
        acc = p_ref[0].astype(F32)
        for q in range(1, n4):
            acc = acc + p_ref[q].astype(F32)
        o_ref[...] = acc

    return pl.pallas_call(
        body, name=name, grid=(r // br, n // bc),
        in_specs=[pl.BlockSpec((n4, br, bc), lambda i, j: (0, i, j))], out_specs=pl.BlockSpec((br, bc), lambda i, j: (i, j)),
        out_shape=jax.ShapeDtypeStruct((r, n), F32),
        compiler_params=_params(("parallel", "parallel")),
    )(parts)


ANY = pl.BlockSpec(memory_space=pl.ANY)


def _place():
    x, y, c = lax.axis_index("x"), lax.axis_index("y"), lax.axis_index("c")
    chips = [(1 - x, y), (x, 1 - y), (1 - x, 1 - y)]
    return x, y, c, chips


def _copy(src, dst, send_sems, recv_sems, k, to):
    return pltpu.make_async_remote_copy(src_ref=src, dst_ref=dst, send_sem=send_sems.at[k], recv_sem=recv_sems.at[k],
                                        device_id=to, device_id_type=MESH)


def split_axis(rows):
    return 0 if rows % 32 == 0 else 1


def _half(ref, lead, hf):
    rows, cols = ref.shape[-2:]
    if split_axis(rows) == 0:
        at = (pl.ds(hf * (rows // 2), rows // 2), slice(None))
    else:
        at = (slice(None), pl.ds(hf * (cols // 2), cols // 2))
    return ref.at[at] if lead is None else ref.at[(lead,) + at]


def _gather_first(srcs, dsts, ssems, rsems):
    x, y, c, chips = _place()
    for ti, (s, d) in enumerate(zip(srcs, dsts)):
        for j, (cx, cy) in enumerate(chips):
            _copy(_half(s, None, c), _half(d, 2 * x + y, c), ssems, rsems, 3 * ti + j, (cx, cy, c)).start()


def _gather_mid(srcs, dsts, ssems, rsems):
    x, y, c, chips = _place()
    n1 = 3 * len(srcs)
    for ti, d in enumerate(dsts):
        for j, (cx, cy) in enumerate(chips):
            landed = _half(d, 2 * cx + cy, c)
            _copy(landed, landed, ssems, rsems, 3 * ti + j, (cx, cy, c)).wait_recv()
            _copy(landed, landed, ssems, rsems, n1 + 3 * ti + j, (x, y, 1 - c)).start()


def _gather_last(srcs, dsts, ssems, rsems):
    x, y, c, chips = _place()
    n1 = 3 * len(srcs)
    for ti, (s, d) in enumerate(zip(srcs, dsts)):
        for j, (cx, cy) in enumerate(chips):
            other = _half(d, 2 * cx + cy, 1 - c)
            _copy(other, other, ssems, rsems, n1 + 3 * ti + j, (x, y, 1 - c)).wait_recv()
        for j, (cx, cy) in enumerate(chips):
            mine = _half(s, None, c)
            _copy(mine, mine, ssems, rsems, 3 * ti + j, (cx, cy, c)).wait_send()
            _copy(mine, mine, ssems, rsems, n1 + 3 * ti + j, (x, y, 1 - c)).wait_send()


def gather_side(shards):
    return Side(shards, [jax.ShapeDtypeStruct((N_CHIPS,) + s.shape, s.dtype) for s in shards], 6 * len(shards),
                _gather_first, _gather_last, _gather_mid)


def _scatter_first(srcs, dsts, ssems, rsems):
    x, y, c, chips = _place()
    for ti, (s, d) in enumerate(zip(srcs, dsts)):
        for j, (cx, cy) in enumerate(chips):
            _copy(s.at[2 * cx + cy], d.at[2 * x + y], ssems, rsems, 3 * ti + j, (cx, cy, c)).start()


def _scatter_last(srcs, dsts, ssems, rsems):
    x, y, c, chips = _place()
    for ti, (s, d) in enumerate(zip(srcs, dsts)):
        for j, (cx, cy) in enumerate(chips):
            _copy(s.at[2 * cx + cy], d.at[2 * cx + cy], ssems, rsems, 3 * ti + j, (cx, cy, c)).wait_recv()
        for j, (cx, cy) in enumerate(chips):
            _copy(s.at[2 * cx + cy], d.at[2 * cx + cy], ssems, rsems, 3 * ti + j, (cx, cy, c)).wait_send()


def scatter_side(parts):
    return Side(parts, [jax.ShapeDtypeStruct(p.shape, p.dtype) for p in parts], 3 * len(parts), _scatter_first, _scatter_last)


def run_side(side, *, name):
    n_in, n_out = len(side.ins), len(side.out_shapes)

    def body(*refs):
        si, so, sems = _split_refs(refs, [n_in, n_out, 2])
        side.first(si, so, *sems)
        if side.mid is not None:
            side.mid(si, so, *sems)
        side.last(si, so, *sems)

    in_specs, out_specs = side.specs()
    return pl.pallas_call(body, name=name, in_specs=in_specs, out_specs=out_specs, out_shape=side.out_shapes,
                          scratch_shapes=side.sems())(*side.ins)


def _swap_first(srcs, dsts, ssems, rsems):
    x, y, c, _ = _place()
    for k, (s, d) in enumerate(zip(srcs, dsts)):
        _copy(s, d, ssems, rsems, k, (x, y, 1 - c)).start()


def _swap_last(srcs, dsts, ssems, rsems):
    x, y, c, _ = _place()
    for k, (s, d) in enumerate(zip(srcs, dsts)):
        _copy(s, d, ssems, rsems, k, (x, y, 1 - c)).wait()


def swap_side(xs):
    return Side(xs, [jax.ShapeDtypeStruct(a.shape, a.dtype) for a in xs], len(xs), _swap_first, _swap_last)


def _swap_halves(srcs, dsts, ssems, rsems):
    x, y, c, _ = _place()
    for k, (s, d) in enumerate(zip(srcs, dsts)):
        hk = s.shape[1] // 2
        yield _copy(s.at[:, pl.ds((1 - c) * hk, hk), :], d, ssems, rsems, k, (x, y, 1 - c))


def _swap_halves_first(srcs, dsts, ssems, rsems):
    for cp in _swap_halves(srcs, dsts, ssems, rsems):
        cp.start()


def _swap_halves_last(srcs, dsts, ssems, rsems):
    for cp in _swap_halves(srcs, dsts, ssems, rsems):
        cp.wait()


def swap_halves_side(xs):
    return Side(xs, [jax.ShapeDtypeStruct((a.shape[0], a.shape[1] // 2, a.shape[2]), a.dtype) for a in xs], len(xs),
                _swap_halves_first, _swap_halves_last)


def allreduce_small(s):
    n_dev = 8

    def body(s_ref, out_ref, buf, send_sems, recv_sems):
        x, y, c, _ = _place()
        me = 4 * x + 2 * y + c
        buf[me] = s_ref[...]
        sends = []
        for k in range(1, n_dev):
            px = 1 - x if k & 4 else x
            py = 1 - y if k & 2 else y
            pc = 1 - c if k & 1 else c
            cp = _copy(s_ref, buf.at[me], send_sems, recv_sems, k - 1, (px, py, pc))
            cp.start()
            sends.append((cp, 4 * px + 2 * py + pc))
        for k, (cp, peer) in enumerate(sends):
            _copy(s_ref, buf.at[peer], send_sems, recv_sems, k, (x, y, c)).wait_recv()
        for cp, _ in sends:
            cp.wait_send()
        acc = buf[0]
        for d in range(1, n_dev):
            acc = acc + buf[d]
        out_ref[...] = acc

    vm = pl.BlockSpec(memory_space=pltpu.VMEM)
    return pl.pallas_call(
        body, name="allreduce_small", in_specs=[vm], out_specs=vm,
        out_shape=jax.ShapeDtypeStruct(s.shape, F32),
        scratch_shapes=[pltpu.VMEM((n_dev,) + s.shape, F32), pltpu.SemaphoreType.DMA((n_dev - 1,)),
                        pltpu.SemaphoreType.DMA((n_dev - 1,))],
    )(s)


def join_cols(sm):
    n4, k, n = sm.shape
    return sm.transpose(1, 0, 2).reshape(k, n4 * n)


def split_cols(full):
    k, n = full.shape
    return full.reshape(k, N_CHIPS, n // N_CHIPS).transpose(1, 0, 2)


def _pad_heads(w, width):
    lead, heads = w.shape[:-1], w.shape[-1] // width
    w = w.reshape(lead + (heads, width))
    return jnp.pad(w, [(0, 0)] * len(lead) + [(0, 0), (0, LANES - width)]).reshape(lead + (heads * LANES,))


def _unpad_heads(w, width):
    lead, heads = w.shape[:-1], w.shape[-1] // LANES
    return w.reshape(lead + (heads, LANES))[..., :width].reshape(lead + (heads * width,))


O_F = 3 * FW
O_CQ = O_F + N_HEADS
O_CKV = O_CQ + Q_RANK
O_KR = O_CKV + KV_RANK
O_END = O_KR + ROPE_DIM


def _shard_rows(sm, a, b):
    r = sm.shape[1]
    out = []
    while a < b:
        q = a // r
        e = min(b, (q + 1) * r)
        out.append(sm[q, a - q * r:e - q * r])
        a = e
    return out


def split_w_in_t(w_sm):
    d = w_sm.shape[2]

    def z(n):
        return [jnp.zeros((n, d), w_sm.dtype)]

    small = (_shard_rows(w_sm, O_CQ, O_CKV) + _shard_rows(w_sm, O_CKV, O_KR) + z(HEAD_DIM) + _shard_rows(w_sm, O_KR, O_END)
             + z(LANES - HEAD_DIM - ROPE_DIM) + _shard_rows(w_sm, O_F, O_CQ) + z(LANES - N_HEADS) + z(LANES))
    return jnp.concatenate(_shard_rows(w_sm, 0, O_F), axis=0), jnp.concatenate(small, axis=0)


def join_w_in_t(d_qkv_t, d_small_t):
    kr = S_KR + HEAD_DIM
    segments = [(part, 0, n * FW, FW) for n, part in enumerate(d_qkv_t)]
    segments += [(d_small_t, S_F, O_F, N_HEADS), (d_small_t, S_CQ, O_CQ, Q_RANK),
                 (d_small_t, S_CKV, O_CKV, KV_RANK), (d_small_t, kr, O_KR, ROPE_DIM)]
    r = O_END // N_CHIPS
    shards = []
    for q in range(N_CHIPS):
        pieces = []
        for src, s0, v0, n in segments:
            a, b = max(v0, q * r), min(v0 + n, (q + 1) * r)
            if a < b:
                pieces.append(src[s0 + a - v0:s0 + b - v0])
        shards.append(jnp.concatenate(pieces, axis=0))
    return jnp.stack(shards)


def rope_inputs(pos):
    inv_freq = ROPE_THETA ** (-jnp.arange(0, ROPE_DIM, 2, dtype=F32) / ROPE_DIM)
    inv_row = jnp.concatenate([jnp.zeros((HEAD_DIM,), F32), inv_freq, inv_freq, jnp.zeros((LANES - HEAD_DIM - ROPE_DIM,), F32)])
    return pos.astype(F32)[:, None], inv_row[None, :]


def _pad_lanes(v, n):
    return jnp.pad(v, ((0, 0), (0, n - v.shape[1])))


ATTN_BLK = 512
ATTN_FWD_BLK = 1024
ATTN_BWD_QBLK = 1024
ACC_ROWS = HEAD_DIM + 16


def local_step(xs, pos, tgt, gains, early_weights, late_weights, early_side=None, fwd_sides=(None, None), reduction=None):
    g_attn, b_forget, g_q, g_kv, g_fo, g_mo, g_mlp, g_fin = gains
    t = xs.shape[0]
    blk = min(ATTN_BLK, t)
    fox_scale = 1.0 / (HEAD_DIM ** 0.5)
    mla_scale = 1.0 / ((HEAD_DIM + ROPE_DIM) ** 0.5)

    h1, *gathered = rmsnorm(xs, g_attn, out_dtype=BF16, name="norm_attn", side=early_side)
    w_in_t, w_uq_p, w_ukv = early_weights(gathered)
    w_qkv_t, w_small_t = split_w_in_t(w_in_t)
    kv = w_ukv.reshape(KV_RANK, N_HEADS, 2 * HEAD_DIM)
    w_ukv_p = jnp.concatenate([_pad_heads(kv[:, :, :HEAD_DIM].reshape(KV_RANK, FW), HEAD_DIM),
                               kv[:, :, HEAD_DIM:].reshape(KV_RANK, FW)], axis=1)
    b_f = _pad_lanes(b_forget, LANES)
    pos_col, inv_row = rope_inputs(pos)

    qkv, qkv_t = mm(h1, w_qkv_t, trans_b=True, out_dtypes=[BF16], t_blk=blk, name="proj_qkv", bn=1536)
    small, = mm(h1, w_small_t, trans_b=True, out_dtypes=[F32], name="proj_small")
    mq, mk, mv, lf, cqn, ckvn, mq_t, mv_t = mla_prep(small, g_q, g_kv, w_uq_p, w_ukv_p, pos_col, inv_row, b_f,
                                                     name="mla_prep", bt=blk)
    f_cum = cumsum_rows(lf, reverse=False, name="gate_cumsum")
    f_blocks = f_cum[:, :N_HEADS].reshape(t // blk, blk, N_HEADS).transpose(0, 2, 1)
    fo, st_f, *gathered = flash_fwd(qkv_t, qkv, qkv_t, f_cum, qoff=0, koff=PAIRS, voff=2 * PAIRS, pair=True,
                                    scale=fox_scale, name="fox_fwd", blk=ATTN_FWD_BLK, side=fwd_sides[0])
    mo, st_m, *more = flash_fwd(mq_t, mk, mv_t, None, qoff=0, koff=0, voff=0, pair=False, scale=mla_scale, name="mla_fwd",
                                blk=ATTN_FWD_BLK, side=fwd_sides[1])
    w_o, w_up, w_down = late_weights(gathered + more)
    mixed = mix_norm(fo, mo, g_fo, g_mo, name="norm_mix")

    def inv_rms(v):
        return lax.rsqrt(jnp.mean(v * v, axis=-1, keepdims=True) + EPS)

    def residual_then_norm(acc, res, g):
        xn = acc + res
        return xn, xn * inv_rms(xn) * g

    def norm_bwd(dh, xn, res, g):
        r = inv_rms(xn)
        uu = dh * g
        return (r * uu - xn * (r * r * r * jnp.mean(uu * xn, axis=-1, keepdims=True)) + res,
                jnp.sum(dh * (xn * r), axis=0, keepdims=True))

    def norm_bwd2(dh, xn, res, g):
        dx, dg = norm_bwd(dh, xn, res, g)
        return dx, dx, dg

    def residual_then_loss(acc, res, target, g):
        xn = acc + res
        r = inv_rms(xn)
        xh = xn * r
        e = xh * g - target
        part = 0.5 * jnp.sum(jnp.mean(e * e, axis=-1, keepdims=True), axis=0, keepdims=True)
        dy = e * (1.0 / xn.shape[1])
        uu = dy * g
        dx = r * uu - xn * (r * r * r * jnp.mean(uu * xn, axis=-1, keepdims=True))
        return dx, dx, jnp.sum(dy * xh, axis=0, keepdims=True), part + jnp.zeros_like(g)

    x1, h2 = mm(mixed, w_o, extras=[xs], vecs=[g_mlp], epilogue=residual_then_norm, out_dtypes=[F32, BF16], name="out_proj")

    def relu2(uu):
        r = jnp.maximum(uu.astype(F32), 0.0)
        return (r * r).astype(BF16)

    u, = mm(h2, w_up, out_dtypes=[BF16], name="mlp_up", bn=2048)
    dx2, dx2b, dg_fin, loss_row = mm(u, w_down, a_pro=relu2, extras=[x1, tgt], vecs=[g_fin], epilogue=residual_then_loss,
                                     out_dtypes=[F32, BF16], n_sums=2, name="mlp_down_loss")
    loss = loss_row[:, :1]

    def relu2_grad(acc, uu):
        return (acc * (2.0 * jnp.maximum(uu.astype(F32), 0.0)),)

    du, = mm(dx2b, w_down, trans_b=True, extras=[u], epilogue=relu2_grad, out_dtypes=[BF16], name="mlp_down_bwd", bn=2048)
    dw_down, dw_down_b = (g.reshape(N_CHIPS, -1, w_down.shape[1])
                          for g in mm_tn(u, dx2b, a_pro=relu2, name="dw_down", bf16_copy=True))
    dx1, dx1b, dg_mlp = mm(du, w_up, trans_b=True, extras=[x1, dx2], vecs=[g_mlp], epilogue=norm_bwd2,
                           out_dtypes=[F32, BF16], n_sums=1, name="mlp_up_bwd")
    dw_up, dw_up_b = mm_tn(h2, du, name="dw_up", col_shards=N_CHIPS, bf16_copy=True)

    dw_o, dw_o_b = (g.reshape(N_CHIPS, -1, w_o.shape[1]) for g in mm_tn(mixed, dx1b, name="dw_o", bf16_copy=True))
    late, late_b = (dw_o, dw_up, dw_down), (dw_o_b, dw_up_b, dw_down_b)
    red = reduction
    dfo, dmo, dg_fo, dg_mo, st_f, st_m, dfo_t, dmo_t, *got = mix_norm_bwd(
        dx1b, w_o, fo, mo, g_fo, g_mo, st_f, st_m, name="out_proj_mix_bwd", bt=blk,
        side=red.late_swap(late, late_b) if red else None)
    dfq, dfk, dfv, d_f, *got = flash_bwd(
        qkv, qkv_t, qkv, qkv, dfo, dfo_t, st_f, f_blocks, qoff=0, koff=PAIRS, voff=2 * PAIRS, pair=True,
        scale=fox_scale, name="fox_bwd", qblk=ATTN_BWD_QBLK, side=red.late_scatter(got) if red else None)
    dmq, dmk, dmv, *got = flash_bwd(mq, mq_t, mk, mv, dmo, dmo_t, st_m, None, qoff=0, koff=0, voff=0,
                                    pair=False, scale=mla_scale, name="mla_bwd", qblk=ATTN_BWD_QBLK,
                                    side=red.late_halves(got) if red else None)
    if red:
        red.late_done(got)
    dlf = cumsum_rows(d_f, reverse=True, name="gate_cumsum_bwd")
    dsmall, dq_u, dkv_u, dg_q, dg_kv, db_f = mla_prep_bwd(dmq, dmk, dmv, dlf, small, g_q, g_kv, w_uq_p, w_ukv_p,
                                                          pos_col, inv_row, b_f, name="mla_prep_bwd")
    dw_uq_p = mm_tn(cqn, dq_u, name="dw_uq")
    dw_ukv_p = mm_tn(ckvn, dkv_u, name="dw_ukv")

    def to_bf16(tile):
        return tile.astype(BF16)

    dqkv = [dfq, dfk, dfv]
    dw_in_t = join_w_in_t([mm_tn(part, h1, a_pro=to_bf16, name="dw_" + nm) for part, nm in zip(dqkv, "qkv")],
                          mm_tn(dsmall, h1, name="dw_small"))
    dk_cols = _unpad_heads(dw_ukv_p[:, :HW], HEAD_DIM).reshape(KV_RANK, N_HEADS, HEAD_DIM)
    dv_cols = dw_ukv_p[:, HW:].reshape(KV_RANK, N_HEADS, HEAD_DIM)
    dw_ukv = jnp.concatenate([dk_cols, dv_cols], axis=2).reshape(KV_RANK, N_HEADS * 2 * HEAD_DIM)
    early = (dw_in_t, split_cols(dw_uq_p), split_cols(dw_ukv))
    w_parts = [w_qkv_t[n * FW:(n + 1) * FW] for n in range(3)]
    grad_x, dg_attn, *got = mm(dqkv + [dsmall], w_parts + [w_small_t], a_pro=to_bf16, extras=[xs, dx1], vecs=[g_attn],
                               epilogue=norm_bwd, out_dtypes=[F32], n_sums=1, name="proj_bwd",
                               side=red.early_scatter(early) if red else None)
    if red:
        red.early_done(got)
    d_gains = (dg_attn, db_f[:, :N_HEADS], dg_q, dg_kv, dg_fo, dg_mo, dg_mlp, dg_fin)
    return loss, grad_x, early, late, d_gains


class GradReduction:
    def __init__(self, core_id, chip):
        self.core_id, self.chip = core_id, chip
        self.core = core_id.reshape(1).astype(jnp.int32)
        self.grads, self.pairs, self.halves, self.others = {}, {}, {}, {}

    def _other_halves(self, grads):
        out = []
        for g in grads:
            axis = 1 + split_axis(g.shape[1])
            size = g.shape[axis] // 2
            out.append(lax.dynamic_slice_in_dim(g, (1 - self.core_id) * size, size, axis=axis).astype(BF16))
        return out

    def _add_pairs(self, group, recvs):
        self.pairs[group] = [add_pair(g, r, self.core, name="add_pair_%s_%d" % (group, n))
                             for n, (g, r) in enumerate(zip(self.grads[group], recvs))]
        return scatter_side(self.pairs[group])

    def _chip_sums(self, group, scattered):
        chip = self.chip
        with_mine = [lax.dynamic_update_index_in_dim(s, lax.dynamic_index_in_dim(p, chip, 0, keepdims=True), chip, 0)
                     for s, p in zip(scattered, self.pairs[group])]
        self.halves[group] = [sum_chips(s, name="sum_chips_%s_%d" % (group, n)) for n, s in enumerate(with_mine)]
        return self.halves[group]

    def late_swap(self, grads, bf16_copies):
        self.grads["late"] = list(grads)
        return swap_halves_side(list(bf16_copies))

    def late_scatter(self, recvs):
        return self._add_pairs("late", recvs)

    def late_halves(self, scattered):
        return swap_side(self._chip_sums("late", scattered))

    def late_done(self, others):
        self.others["late"] = list(others)

    def early_scatter(self, grads):
        self.grads["early"] = list(grads)
        return self._add_pairs("early", run_side(swap_side(self._other_halves(grads)), name="swap_early_sends"))

    def early_done(self, scattered):
        self.others["early"] = list(run_side(swap_side(self._chip_sums("early", scattered)), name="swap_early_halves"))

def kernel(x, positions, attn_norm_g, w_in, b_forget, q_norm_g, w_uq, kv_norm_g, w_ukv, fox_out_g, mla_out_g, w_o, mlp_norm_g, w_up, w_down, final_norm_g, loss_target, m_attn_norm_g, m_w_in, m_b_forget, m_q_norm_g, m_w_uq, m_kv_norm_g, m_w_ukv, m_fox_out_g, m_mla_out_g, m_w_o, m_mlp_norm_g, m_w_up, m_w_down, m_final_norm_g, v_attn_norm_g, v_w_in, v_b_forget, v_q_norm_g, v_w_uq, v_kv_norm_g, v_w_ukv, v_fox_out_g, v_mla_out_g, v_w_o, v_mlp_norm_g, v_w_up, v_w_down, v_final_norm_g):
    core_id = lax.axis_index("c")
    core = core_id.reshape(1).astype(jnp.int32)
    chip = 2 * lax.axis_index("x") + lax.axis_index("y")
    big = [w_in, w_uq, w_ukv, w_o, w_up, w_down]
    big_m = [m_w_in, m_w_uq, m_w_ukv, m_w_o, m_w_up, m_w_down]
    big_v = [v_w_in, v_w_uq, v_w_ukv, v_w_o, v_w_up, v_w_down]
    n_early = 3

    def vec(a):
        return a.reshape(1, -1)

    small = [attn_norm_g, b_forget, q_norm_g, kv_norm_g, fox_out_g, mla_out_g, mlp_norm_g, final_norm_g]
    small_m = [m_attn_norm_g, m_b_forget, m_q_norm_g, m_kv_norm_g, m_fox_out_g, m_mla_out_g, m_mlp_norm_g, m_final_norm_g]
    small_v = [v_attn_norm_g, v_b_forget, v_q_norm_g, v_kv_norm_g, v_fox_out_g, v_mla_out_g, v_mlp_norm_g, v_final_norm_g]
    gains = [vec(a) for a in small]

    views = [big[0][0].T, _pad_heads(big[1][0], HEAD_DIM + ROPE_DIM)] + [w[0] for w in big[2:]]
    shards = [v.astype(BF16) for v in views]

    def with_own(gathered, mine):
        return [lax.dynamic_update_index_in_dim(g, s, chip, 0) for g, s in zip(gathered, mine)]

    def early_weights(gathered):
        g_in, g_uq, g_ukv = with_own(gathered, shards[:n_early])
        return g_in, join_cols(g_uq), join_cols(g_ukv)

    def late_weights(gathered):
        g_o, g_up, g_down = with_own(gathered, shards[n_early:])
        return g_o.reshape(-1, g_o.shape[2]), join_cols(g_up), g_down.reshape(-1, g_down.shape[2])

    reduction = GradReduction(core_id, chip)
    loss, grad_x, _, _, d_small = local_step(
        x[0], positions[0], loss_target[0], gains, early_weights, late_weights, gather_side(shards[:n_early]),
        (gather_side(shards[n_early:-1]), gather_side(shards[-1:])), reduction)
    halves = reduction.halves["early"] + reduction.halves["late"]
    others = reduction.others["early"] + reduction.others["late"]

    def rows8(vs):
        return jnp.concatenate([_pad_lanes(vec(a).astype(F32), 1024) for a in vs], axis=0)

    with_loss = [jnp.concatenate([d, loss], axis=1) if n == 1 else d for n, d in enumerate(d_small)]
    g_small8 = allreduce_small(rows8(with_loss))

    outs_big = []
    for n, (w, gm, go, m, v) in enumerate(zip(big, halves, others, big_m, big_v)):
        if n == 0:
            outs = adamw_halves_t(*(jnp.transpose(a, (2, 0, 1)) for a in (w, m, v)), gm, go, core, name="adamw_%d" % n)
            outs_big.append([jnp.transpose(o, (1, 2, 0)) for o in outs])
        else:
            if n == 1:
                gm, go = (_unpad_heads(gh, HEAD_DIM + ROPE_DIM) for gh in (gm, go))
            outs_big.append(adamw_halves(w, gm, go, m, v, core, name="adamw_%d" % n))
    outs_small = adamw_vectors(gains, g_small8, [vec(a) for a in small_m], [vec(a) for a in small_v], name="adamw_small")

    loss_all = g_small8[1, N_HEADS]
    grads, deltas, new_m, new_v = [None] * 14, [None] * 14, [None] * 14, [None] * 14
    big_at = [1, 4, 6, 9, 11, 12]
    small_at = [0, 2, 3, 5, 7, 8, 10, 13]
    for n, at in enumerate(big_at):
        grads[at], deltas[at], new_m[at], new_v[at] = outs_big[n]
    for at, s, outs in zip(small_at, small, outs_small):
        grads[at], deltas[at], new_m[at], new_v[at] = (o.reshape(s.shape) for o in outs)
    return (loss_all, grad_x[None], *grads, *deltas, *new_m, *new_v)
```

```python
import jax
import jax.numpy as jnp
from jax import lax
from jax.experimental import pallas as pl
from jax.experimental.pallas import tpu as pltpu

F32 = jnp.float32
BF16 = jnp.bfloat16
MESH = pl.DeviceIdType.MESH

EPS = 1e-6
ROPE_THETA = 10000.0
N_HEADS = 8
PAIRS = N_HEADS // 2
HEAD_DIM = 64
ROPE_DIM = 32
LANES = 128
Q_RANK = 384
KV_RANK = 256
N_CHIPS = 4
ADAM_LR, ADAM_B1, ADAM_B2, ADAM_EPS, ADAM_WD, ADAM_STEP = 0.001, 0.9, 0.999, 1e-08, 0.01, 10
VMEM_LIMIT = 48 * 1024 * 1024
LOG2E = 1.4426950408889634
LN2 = 0.6931471805599453
NN = (((1,), (0,)), ((), ()))
NT = (((1,), (1,)), ((), ()))
TN = (((0,), (0,)), ((), ()))


def _params(sem=None):
    return pltpu.CompilerParams(dimension_semantics=sem, vmem_limit_bytes=VMEM_LIMIT)


def _fit(block, dim):
    if dim <= block:
        return dim
    return next(b for b in range(block - block % LANES, 0, -LANES) if dim % b == 0)


def _row_block(rows):
    return next(b for b in (256, 128, 64, 32, 16, 8) if rows % b == 0)


def gridded(body, *, name, grid, in_specs, out_specs, out_shape, ins, semantics, side=None):
    if side is None:
        return pl.pallas_call(body, name=name, grid=grid, in_specs=in_specs, out_specs=out_specs, out_shape=out_shape,
                              compiler_params=_params(semantics))(*ins)
    n_in, n_out = len(in_specs), len(out_specs)
    steps = 1
    for extent in grid:
        steps *= extent

    def riding(*refs):
        main_in, s_in, main_out, s_out, sems = _split_refs(refs, [n_in, len(side.ins), n_out, len(side.out_shapes), 2])
        step = 0
        for axis, extent in enumerate(grid):
            step = step * extent + pl.program_id(axis)

        @pl.when(step == 0)
        def _():
            side.first(s_in, s_out, *sems)

        body(*main_in, *main_out)

        @pl.when(step == steps - 1)
        def _():
            if side.mid is not None:
                side.mid(s_in, s_out, *sems)
            side.last(s_in, s_out, *sems)

    s_in_specs, s_out_specs = side.specs()
    return pl.pallas_call(
        riding, name=name, grid=grid, in_specs=list(in_specs) + s_in_specs, out_specs=list(out_specs) + s_out_specs,
        out_shape=list(out_shape) + side.out_shapes, scratch_shapes=side.sems(),
        compiler_params=_params(("arbitrary",) * len(grid)))(*ins, *side.ins)


def rmsnorm(x, g, *, out_dtype, name, bt=512, side=None):
    t, d = x.shape
    bt = min(bt, t)

    def body(x_ref, g_ref, o_ref):
        xv = x_ref[...].astype(F32)
        r = lax.rsqrt(jnp.mean(xv * xv, axis=-1, keepdims=True) + EPS)
        o_ref[...] = (xv * r * g_ref[...]).astype(o_ref.dtype)

    return gridded(
        body, name=name, grid=(t // bt,),
        in_specs=[pl.BlockSpec((bt, d), lambda i: (i, 0)), pl.BlockSpec((1, d), lambda i: (0, 0))],
        out_specs=[pl.BlockSpec((bt, d), lambda i: (i, 0))],
        out_shape=[jax.ShapeDtypeStruct((t, d), out_dtype)],
        ins=[x, g], semantics=("parallel",), side=side)


def mm(a, b, *, trans_b=False, a_pro=None, extras=(), vecs=(), epilogue=None, out_dtypes, n_sums=0, t_blk=None, name,
       bm=1024, bn=1024, side=None):
    a_list = list(a) if isinstance(a, (list, tuple)) else [a]
    b_list = list(b) if isinstance(b, (list, tuple)) else [b]
    m = a_list[0].shape[0]
    n = b_list[0].shape[0] if trans_b else b_list[0].shape[1]
    ks = [x.shape[1] for x in a_list]
    if sum(ks) > 2048:
        bm = bm // 2
    bm, bn = _fit(bm, m), _fit(bn, n)
    assert n_sums == 0 or bn == n
    n_ab, n_ex, n_vec, n_out = len(a_list), len(extras), len(vecs), len(out_dtypes)
    n_t = 0 if t_blk is None else 1

    def body(*refs):
        a_refs, b_refs, ex, vs, outs, t_outs, sums = _split_refs(refs, [n_ab, n_ab, n_ex, n_vec, n_out, n_t, n_sums])
        acc = None
        for a_ref, b_ref in zip(a_refs, b_refs):
            a_tile = a_ref[...] if a_pro is None else a_pro(a_ref[...])
            part = lax.dot_general(a_tile, b_ref[...], NT if trans_b else NN, preferred_element_type=F32)
            acc = part if acc is None else acc + part
        res = epilogue(acc, *[e[...] for e in ex], *[v[...] for v in vs]) if epilogue is not None else (acc,)
        for o, r in zip(outs, res[:n_out]):
            o[...] = r.astype(o.dtype)
        for t_ref in t_outs:
            for u in range(bm // t_blk):
                t_ref[u] = res[0][u * t_blk:(u + 1) * t_blk, :].T.astype(t_ref.dtype)
        if n_sums:
            @pl.when(pl.program_id(0) == 0)
            def _():
                for s_ref in sums:
                    s_ref[...] = jnp.zeros_like(s_ref)

            for s_ref, r in zip(sums, res[n_out:]):
                s_ref[...] += r

    tile = pl.BlockSpec((bm, bn), lambda i, j: (i, j))
    vec = pl.BlockSpec((1, bn), lambda i, j: (0, j))
    t_specs, t_shapes = [], []
    if t_blk is not None:
        t_specs = [pl.BlockSpec((bm // t_blk, bn, t_blk), lambda i, j: (i, j, 0))]
        t_shapes = [jax.ShapeDtypeStruct((m // t_blk, n, t_blk), out_dtypes[0])]
    a_specs = [pl.BlockSpec((bm, k), lambda i, j: (i, 0)) for k in ks]
    b_specs = [pl.BlockSpec((bn, k), lambda i, j: (j, 0)) if trans_b else pl.BlockSpec((k, bn), lambda i, j: (0, j)) for k in ks]
    return gridded(
        body, name=name, grid=(m // bm, n // bn),
        in_specs=a_specs + b_specs + [tile] * n_ex + [vec] * n_vec,
        out_specs=[tile] * n_out + t_specs + [vec] * n_sums,
        out_shape=[jax.ShapeDtypeStruct((m, n), dt) for dt in out_dtypes] + t_shapes + [jax.ShapeDtypeStruct((1, n), F32)] * n_sums,
        ins=[*a_list, *b_list, *extras, *vecs],
        semantics=("arbitrary", "arbitrary") if n_sums else ("parallel", "parallel"), side=side)


def mm_tn(a, b, *, a_pro=None, name, col_shards=1, bf16_copy=False, bk=1024, bn=1024, bt=2048):
    t, k = a.shape
    n = b.shape[1]
    ns = n // col_shards
    bk, bn, bt = _fit(bk, k), _fit(bn, ns), _fit(bt, t)
    per = ns // bn
    last = t // bt - 1

    def body(a_ref, b_ref, o_ref, *copy_ref):
        @pl.when(pl.program_id(2) == 0)
        def _():
            o_ref[...] = jnp.zeros_like(o_ref)

        a_tile = a_ref[...] if a_pro is None else a_pro(a_ref[...])
        o_ref[...] += lax.dot_general(a_tile, b_ref[...], TN, preferred_element_type=F32)
        if bf16_copy:
            @pl.when(pl.program_id(2) == last)
            def _():
                copy_ref[0][...] = o_ref[...].astype(BF16)

    if col_shards == 1:
        out_spec = pl.BlockSpec((bk, bn), lambda i, j, s: (i, j))
        shape = (k, n)
    else:
        out_spec = pl.BlockSpec((None, bk, bn), lambda i, j, s: (j // per, i, j % per))
        shape = (col_shards, k, ns)
    out_specs, out_shape = out_spec, jax.ShapeDtypeStruct(shape, F32)
    if bf16_copy:
        out_specs, out_shape = [out_spec, out_spec], [out_shape, jax.ShapeDtypeStruct(shape, BF16)]
    return pl.pallas_call(
        body, name=name, grid=(k // bk, n // bn, t // bt),
        in_specs=[pl.BlockSpec((bt, bk), lambda i, j, s: (s, i)), pl.BlockSpec((bt, bn), lambda i, j, s: (s, j))],
        out_specs=out_specs, out_shape=out_shape,
        compiler_params=_params(("parallel", "parallel", "arbitrary")),
    )(a, b)


def _split3(x):
    hi = x.astype(BF16)
    r1 = x - hi.astype(F32)
    mid = r1.astype(BF16)
    lo = (r1 - mid.astype(F32)).astype(BF16)
    return hi, mid, lo


def cumsum_rows(x, *, reverse, name, bc=512):
    t, d = x.shape
    bc = min(bc, t)
    nb = t // bc

    def body(x_ref, o_ref, carry):
        @pl.when(pl.program_id(0) == 0)
        def _():
            carry[...] = jnp.zeros_like(carry)

        r = lax.broadcasted_iota(jnp.int32, (bc, bc), 0)
        c = lax.broadcasted_iota(jnp.int32, (bc, bc), 1)
        tri = jnp.where((r <= c) if reverse else (r >= c), 1.0, 0.0).astype(BF16)
        hi, mid, lo = _split3(x_ref[...])
        s = (lax.dot_general(tri, hi, NN, preferred_element_type=F32)
             + lax.dot_general(tri, mid, NN, preferred_element_type=F32)
             + lax.dot_general(tri, lo, NN, preferred_element_type=F32)) + carry[0:1, :]
        o_ref[...] = s
        carry[0:1, :] = s[0:1, :] if reverse else s[bc - 1:bc, :]

    imap = (lambda i: (nb - 1 - i, 0)) if reverse else (lambda i: (i, 0))
    return pl.pallas_call(
        body, name=name, grid=(nb,),
        in_specs=[pl.BlockSpec((bc, d), imap)], out_specs=pl.BlockSpec((bc, d), imap),
        out_shape=jax.ShapeDtypeStruct((t, d), F32),
        scratch_shapes=[pltpu.VMEM((8, d), F32)],
        compiler_params=_params(("arbitrary",)),
    )(x)


def _rope(x, c, a, b):
    return x * c + pltpu.roll(x, LANES - ROPE_DIM // 2, 1) * a + pltpu.roll(x, ROPE_DIM // 2, 1) * b


def _rope_bwd(d, c, a, b):
    return d * c + pltpu.roll(d * a, ROPE_DIM // 2, 1) + pltpu.roll(d * b, LANES - ROPE_DIM // 2, 1)


S_CQ, S_CKV, S_KR, S_F, S_END = 0, Q_RANK, Q_RANK + KV_RANK, Q_RANK + KV_RANK + LANES, 1024
HW = N_HEADS * LANES
FW = N_HEADS * HEAD_DIM


def _rope_tables(pos_col, inv_row):
    ang = pos_col * inv_row
    cos, sin = jnp.cos(ang), jnp.sin(ang)
    lane = lax.broadcasted_iota(jnp.int32, (1, LANES), 1)
    first = (lane >= HEAD_DIM) & (lane < HEAD_DIM + ROPE_DIM // 2)
    second = (lane >= HEAD_DIM + ROPE_DIM // 2) & (lane < HEAD_DIM + ROPE_DIM)
    return (jnp.where(lane < HEAD_DIM, 1.0, jnp.where(first | second, cos, 0.0)), jnp.where(first, -sin, 0.0),
            jnp.where(second, sin, 0.0))


def mla_prep(small, g_q, g_kv, w_uq, w_ukv, pos_col, inv_row, b_f, *, name, bt=512):
    t = small.shape[0]
    bt = min(bt, t)

    def body(s_ref, gq_ref, gkv_ref, wq_ref, wkv_ref, pos_ref, inv_ref, bf_ref,
             mq_ref, mk_ref, mv_ref, lf_ref, cqn_ref, ckvn_ref, mqt_ref, mvt_ref):
        cq = s_ref[:, S_CQ:S_CKV]
        rq = lax.rsqrt(jnp.mean(cq * cq, axis=-1, keepdims=True) + EPS)
        cqn = (cq * rq * gq_ref[...]).astype(BF16)
        ckv = s_ref[:, S_CKV:S_KR]
        rkv = lax.rsqrt(jnp.mean(ckv * ckv, axis=-1, keepdims=True) + EPS)
        ckvn = (ckv * rkv * gkv_ref[...]).astype(BF16)
        cqn_ref[...] = cqn
        ckvn_ref[...] = ckvn
        tc, ta, tb = _rope_tables(pos_ref[...], inv_ref[...])
        q = jnp.dot(cqn, wq_ref[...], preferred_element_type=F32)
        kv = jnp.dot(ckvn, wkv_ref[...], preferred_element_type=F32)
        kr = _rope(s_ref[:, S_KR:S_F], tc, ta, tb)
        for h in range(N_HEADS):
            sl = slice(h * LANES, (h + 1) * LANES)
            roped = _rope(q[:, sl], tc, ta, tb)
            mq_ref[:, sl] = roped.astype(BF16)
            mqt_ref[0, sl, :] = roped.T.astype(BF16)
            mk_ref[:, sl] = (kv[:, sl] + kr).astype(BF16)
        mv_ref[...] = kv[:, HW:].astype(BF16)
        mvt_ref[0] = kv[:, HW:].T.astype(BF16)
        z = s_ref[:, S_F:S_END - LANES] + bf_ref[...]
        lf_ref[...] = jnp.minimum(z, 0.0) - jnp.log(1.0 + jnp.exp(-jnp.abs(z)))

    def row(w):
        return pl.BlockSpec((bt, w), lambda i: (i, 0))

    def full(arr):
        return pl.BlockSpec(arr.shape, lambda i: (0, 0))

    return pl.pallas_call(
        body, name=name, grid=(t // bt,),
        in_specs=[row(S_END), full(g_q), full(g_kv), full(w_uq), full(w_ukv), row(1), full(inv_row), full(b_f)],
        out_specs=[row(HW), row(HW), row(FW), row(LANES), row(Q_RANK), row(KV_RANK),
                   pl.BlockSpec((1, HW, bt), lambda i: (i, 0, 0)), pl.BlockSpec((1, FW, bt), lambda i: (i, 0, 0))],
        out_shape=[jax.ShapeDtypeStruct((t, HW), BF16)] * 2 + [jax.ShapeDtypeStruct((t, FW), BF16), jax.ShapeDtypeStruct((t, LANES), F32),
                   jax.ShapeDtypeStruct((t, Q_RANK), BF16), jax.ShapeDtypeStruct((t, KV_RANK), BF16),
                   jax.ShapeDtypeStruct((t // bt, HW, bt), BF16), jax.ShapeDtypeStruct((t // bt, FW, bt), BF16)],
        compiler_params=_params(("parallel",)),
    )(small, g_q, g_kv, w_uq, w_ukv, pos_col, inv_row, b_f)


def mla_prep_bwd(dmq, dmk, dmv, dlf, small, g_q, g_kv, w_uq, w_ukv, pos_col, inv_row, b_f, *, name, bt=512):
    t = small.shape[0]
    bt = min(bt, t)

    def body(dmq_ref, dmk_ref, dmv_ref, dlf_ref, s_ref, gq_ref, gkv_ref, wq_ref, wkv_ref, pos_ref, inv_ref, bf_ref,
             ds_ref, dq_ref, dkv_ref, dgq_ref, dgkv_ref, db_ref):
        tc, ta, tb = _rope_tables(pos_ref[...], inv_ref[...])
        lane = lax.broadcasted_iota(jnp.int32, (1, LANES), 1)
        dkr = jnp.zeros((bt, LANES), F32)
        for h in range(N_HEADS):
            sl = slice(h * LANES, (h + 1) * LANES)
            dq_ref[:, sl] = _rope_bwd(dmq_ref[:, sl], tc, ta, tb).astype(BF16)
            dkr = dkr + dmk_ref[:, sl]
        dkv_ref[:, :HW] = dmk_ref[...].astype(BF16)
        dkv_ref[:, HW:] = dmv_ref[...].astype(BF16)
        in_rope = (lane >= HEAD_DIM) & (lane < HEAD_DIM + ROPE_DIM)
        ds_ref[:, S_KR:S_F] = jnp.where(in_rope, _rope_bwd(dkr, tc, ta, tb), 0.0).astype(BF16)

        def norm_bwd(raw, g_ref, dn, dg_ref):
            r = lax.rsqrt(jnp.mean(raw * raw, axis=-1, keepdims=True) + EPS)
            u = dn * g_ref[...]
            dot = jnp.mean(u * raw, axis=-1, keepdims=True)
            dg_ref[...] += jnp.sum(dn * (raw * r), axis=0, keepdims=True)
            return r * u - raw * (r * r * r * dot)

        @pl.when(pl.program_id(0) == 0)
        def _():
            dgq_ref[...] = jnp.zeros_like(dgq_ref)
            dgkv_ref[...] = jnp.zeros_like(dgkv_ref)
            db_ref[...] = jnp.zeros_like(db_ref)

        dcqn = lax.dot_general(dq_ref[...], wq_ref[...], NT, preferred_element_type=F32)
        ds_ref[:, S_CQ:S_CKV] = norm_bwd(s_ref[:, S_CQ:S_CKV], gq_ref, dcqn, dgq_ref).astype(BF16)
        dckvn = lax.dot_general(dkv_ref[...], wkv_ref[...], NT, preferred_element_type=F32)
        ds_ref[:, S_CKV:S_KR] = norm_bwd(s_ref[:, S_CKV:S_KR], gkv_ref, dckvn, dgkv_ref).astype(BF16)
        z = s_ref[:, S_F:S_END - LANES] + bf_ref[...]
        dz = jnp.where(lane < N_HEADS, dlf_ref[...] / (1.0 + jnp.exp(z)), 0.0)
        db_ref[...] += jnp.sum(dz, axis=0, keepdims=True)
        ds_ref[:, S_F:S_END - LANES] = dz.astype(BF16)
        ds_ref[:, S_END - LANES:] = jnp.zeros((bt, LANES), BF16)

    def row(w):
        return pl.BlockSpec((bt, w), lambda i: (i, 0))

    def full(arr):
        return pl.BlockSpec(arr.shape, lambda i: (0, 0))

    def vec(w):
        return pl.BlockSpec((1, w), lambda i: (0, 0))

    return pl.pallas_call(
        body, name=name, grid=(t // bt,),
        in_specs=[row(HW), row(HW), row(FW), row(LANES), row(S_END), full(g_q), full(g_kv), full(w_uq), full(w_ukv),
                  row(1), full(inv_row), full(b_f)],
        out_specs=[row(S_END), row(HW), row(HW + FW), vec(Q_RANK), vec(KV_RANK), vec(LANES)],
        out_shape=[jax.ShapeDtypeStruct((t, S_END), BF16), jax.ShapeDtypeStruct((t, HW), BF16),
                   jax.ShapeDtypeStruct((t, HW + FW), BF16), jax.ShapeDtypeStruct((1, Q_RANK), F32),
                   jax.ShapeDtypeStruct((1, KV_RANK), F32), jax.ShapeDtypeStruct((1, LANES), F32)],
        compiler_params=_params(("arbitrary",)),
    )(dmq, dmk, dmv, dlf, small, g_q, g_kv, w_uq, w_ukv, pos_col, inv_row, b_f)


class Side:
    def __init__(self, ins, out_shapes, n_sems, first, last, mid=None):
        self.ins, self.out_shapes, self.n_sems = list(ins), list(out_shapes), n_sems
        self.first, self.mid, self.last = first, mid, last

    def specs(self):
        return [ANY] * len(self.ins), [ANY] * len(self.out_shapes)

    def sems(self):
        return [pltpu.SemaphoreType.DMA((self.n_sems,)), pltpu.SemaphoreType.DMA((self.n_sems,))]


def _lane():
    return lax.broadcasted_iota(jnp.int32, (1, LANES), 1)


def _halves(x):
    zero = jnp.zeros_like(x)
    return [jnp.where(_lane() < HEAD_DIM, x, zero), jnp.where(_lane() >= HEAD_DIM, x, zero)]


def _groups(x):
    return [x[:, :LANES], x[:, LANES:]]


def _pick_row(tile, h):
    row = lax.broadcasted_iota(jnp.int32, (tile.shape[0], 1), 0)
    return jnp.sum(jnp.where(row == h, tile, 0.0), axis=0, keepdims=True)


def _pick_lane(tile, h):
    return jnp.sum(jnp.where(_lane() == h, tile, 0.0), axis=1, keepdims=True)


def _row_halves(x):
    row = lax.broadcasted_iota(jnp.int32, (LANES, 1), 0)
    zero = jnp.zeros_like(x)
    return [jnp.where(row < HEAD_DIM, x, zero), jnp.where(row >= HEAD_DIM, x, zero)]


def _below_diagonal(s):
    r = lax.broadcasted_iota(jnp.int32, s.shape, 0)
    c = lax.broadcasted_iota(jnp.int32, s.shape, 1)
    return jnp.where(c <= r, s, -jnp.inf)


def _above_diagonal(s):
    r = lax.broadcasted_iota(jnp.int32, s.shape, 0)
    c = lax.broadcasted_iota(jnp.int32, s.shape, 1)
    return jnp.where(r <= c, s, -jnp.inf)


def _split_refs(refs, counts):
    out, at = [], 0
    for n in counts:
        out.append(refs[at:at + n])
        at += n
    return out


def flash_fwd(qt_arr, k_arr, vt_arr, f_cum, *, qoff, koff, voff, pair, scale, name, blk=512, side=None):
    t = k_arr.shape[0]
    tblk = qt_arr.shape[2]
    blk = max(min(blk, t), tblk)
    sub = blk // tblk
    nb = t // blk
    steps = PAIRS * nb
    w = LANES if pair else 2 * LANES
    has_bias = f_cum is not None
    ins = [qt_arr, k_arr, vt_arr] + ([f_cum] if has_bias else [])

    def wide(ref, first):
        parts = [ref[first + u] for u in range(sub)]
        return parts[0] if sub == 1 else jnp.concatenate(parts, axis=1)
    s_ins, s_outs = (side.ins, side.out_shapes) if side else ([], [])

    def body(*refs):
        main, si, outs, so, sems = _split_refs(refs, [len(ins), len(s_ins), 2, len(s_outs), 2 if side else 0])
        qt_ref, k_ref, vt_ref = main[:3]
        f_ref = main[3] if has_bias else None
        o_ref, st_ref = outs
        g, i = pl.program_id(0), pl.program_id(1)
        step_id = g * nb + i
        if side:
            @pl.when(step_id == 0)
            def _():
                side.first(si, so, *sems)

            if side.mid is not None:
                @pl.when(step_id == (3 * steps) // 4)
                def _():
                    side.mid(si, so, *sems)

        qt = (wide(qt_ref, 0).astype(F32) * (scale * LOG2E)).astype(BF16)
        qts = _row_halves(qt) if pair else [qt[:LANES], qt[LANES:]]

        def k_of(kk, n):
            return kk if pair else kk[:, n * LANES:(n + 1) * LANES]

        def with_ones(vt_rows):
            return jnp.concatenate([vt_rows, jnp.ones((ACC_ROWS - HEAD_DIM, vt_rows.shape[1]), BF16)], axis=0)

        def step(j, carry, diagonal):
            rows = pl.ds(pl.multiple_of(j * blk, blk), blk)
            kk = k_ref[rows, :]
            vt = wide(vt_ref, sub * j)
            out = []
            for n in range(2):
                m, acc = carry[n]
                s = jnp.dot(k_of(kk, n), qts[n], preferred_element_type=F32)
                if has_bias:
                    s = s - LOG2E * _pick_lane(f_ref[rows, :], 2 * g + n)
                if diagonal:
                    s = _above_diagonal(s)
                m_new = jnp.maximum(m, jnp.max(s, axis=0, keepdims=True))
                p = jnp.exp2(s - m_new).astype(BF16)
                out.append((m_new, jnp.exp2(m - m_new) * acc
                            + jnp.dot(with_ones(vt[n * HEAD_DIM:(n + 1) * HEAD_DIM]), p, preferred_element_type=F32)))
            return tuple(out)

        def diagonal_in_halves(carry):
            h = tblk
            halves = [pl.ds(pl.multiple_of(i * blk, blk), h), pl.ds(pl.multiple_of(i * blk + h, h), h)]
            k_top, k_bot = k_ref[halves[0], :], k_ref[halves[1], :]
            vt_top, vt_bot = vt_ref[sub * i], vt_ref[sub * i + 1]
            out = []
            for n in range(2):
                m, acc = carry[n]
                heads = slice(n * HEAD_DIM, (n + 1) * HEAD_DIM)
                s_top = jnp.dot(k_of(k_top, n), qts[n], preferred_element_type=F32)
                s_bot = jnp.dot(k_of(k_bot, n), qts[n][:, h:], preferred_element_type=F32)
                if has_bias:
                    s_top = s_top - LOG2E * _pick_lane(f_ref[halves[0], :], 2 * g + n)
                    s_bot = s_bot - LOG2E * _pick_lane(f_ref[halves[1], :], 2 * g + n)
                s_top, s_bot = _above_diagonal(s_top), _above_diagonal(s_bot)
                m_top = jnp.maximum(m, jnp.max(s_top, axis=0, keepdims=True))
                m_new = jnp.concatenate([m_top[:, :h], jnp.maximum(m_top[:, h:], jnp.max(s_bot, axis=0, keepdims=True))], axis=1)
                p_top = jnp.exp2(s_top - m_new).astype(BF16)
                p_bot = jnp.exp2(s_bot - m_new[:, h:]).astype(BF16)
                late = jnp.concatenate([jnp.zeros((ACC_ROWS, h), F32),
                                        jnp.dot(with_ones(vt_bot[heads]), p_bot, preferred_element_type=F32)], axis=1)
                out.append((m_new, jnp.exp2(m - m_new) * acc
                            + jnp.dot(with_ones(vt_top[heads]), p_top, preferred_element_type=F32) + late))
            return tuple(out)

        init = tuple((jnp.full((1, blk), -jnp.inf, F32), jnp.zeros((ACC_ROWS, blk), F32)) for _ in range(2))
        carry = lax.fori_loop(0, i, lambda j, c: step(j, c, False), init)
        (ma, acca), (mb, accb) = diagonal_in_halves(carry) if sub == 2 else step(i, carry, True)
        la, lb = acca[HEAD_DIM:HEAD_DIM + 1], accb[HEAD_DIM:HEAD_DIM + 1]
        o_ref[...] = jnp.concatenate([acca[:HEAD_DIM] / la, accb[:HEAD_DIM] / lb], axis=0).T
        row = lax.broadcasted_iota(jnp.int32, (LANES, 1), 0)
        st_ref[0] = jnp.where(row == 0, ma + jnp.log2(la), jnp.where(row == 1, mb + jnp.log2(lb), 0.0)).T
        if side:
            @pl.when(step_id == steps - 1)
            def _():
                side.last(si, so, *sems)

    in_specs = [pl.BlockSpec((sub, w, tblk), lambda g, i: (i, qoff + g, 0)), pl.BlockSpec((t, w), lambda g, i: (0, koff + g)),
                pl.BlockSpec((t // tblk, LANES, tblk), lambda g, i: (0, voff + g, 0))]
    if has_bias:
        in_specs.append(pl.BlockSpec((t, LANES), lambda g, i: (0, 0)))
    s_in_specs, s_out_specs = side.specs() if side else ([], [])
    return pl.pallas_call(
        body, name=name, grid=(PAIRS, nb), in_specs=in_specs + s_in_specs,
        out_specs=[pl.BlockSpec((blk, LANES), lambda g, i: (i, g)), pl.BlockSpec((1, blk, LANES), lambda g, i: (g, i, 0))]
        + s_out_specs,
        out_shape=[jax.ShapeDtypeStruct((t, PAIRS * LANES), F32), jax.ShapeDtypeStruct((PAIRS, t, LANES), F32)] + list(s_outs),
        scratch_shapes=side.sems() if side else [],
        compiler_params=_params(("arbitrary", "arbitrary")),
    )(*ins, *s_ins)


def mix_norm(fo, mo, g_fo, g_mo, *, name, bt=512):
    t, d = fo.shape
    bt = min(bt, t)

    def body(fo_ref, mo_ref, gf_ref, gm_ref, o_ref):
        for n, (x_ref, g_ref) in enumerate(((fo_ref, gf_ref), (mo_ref, gm_ref))):
            xv = x_ref[...]
            r = lax.rsqrt(jnp.mean(xv * xv, axis=-1, keepdims=True) + EPS)
            o_ref[:, n * d:(n + 1) * d] = (xv * r * g_ref[...]).astype(BF16)

    row = pl.BlockSpec((bt, d), lambda i: (i, 0))
    vec = pl.BlockSpec((1, d), lambda i: (0, 0))
    return pl.pallas_call(
        body, name=name, grid=(t // bt,), in_specs=[row, row, vec, vec],
        out_specs=pl.BlockSpec((bt, 2 * d), lambda i: (i, 0)),
        out_shape=jax.ShapeDtypeStruct((t, 2 * d), BF16),
        compiler_params=_params(("parallel",)),
    )(fo, mo, g_fo, g_mo)


def mix_norm_bwd(dx1b, w_o, fo, mo, g_fo, g_mo, st_f, st_m, *, name, bt=512, side=None):
    t, d = fo.shape
    bt = min(bt, t)

    def body(dx_in_ref, w_ref, fo_ref, mo_ref, gf_ref, gm_ref, sf_ref, sm_ref, dfo_ref, dmo_ref, dgf_ref, dgm_ref,
             sfo_ref, smo_ref, dfot_ref, dmot_ref):
        @pl.when(pl.program_id(0) == 0)
        def _():
            dgf_ref[...] = jnp.zeros_like(dgf_ref)
            dgm_ref[...] = jnp.zeros_like(dgm_ref)

        dmixed = lax.dot_general(dx_in_ref[...], w_ref[...], NT, preferred_element_type=F32)
        groups = ((fo_ref, gf_ref, dfo_ref, dgf_ref, sf_ref, sfo_ref, dfot_ref),
                  (mo_ref, gm_ref, dmo_ref, dgm_ref, sm_ref, smo_ref, dmot_ref))
        for n, (x_ref, g_ref, dx_ref, dg_ref, st_ref, sto_ref, dxt_ref) in enumerate(groups):
            xv = x_ref[...]
            dhv = dmixed[:, n * d:(n + 1) * d]
            r = lax.rsqrt(jnp.mean(xv * xv, axis=-1, keepdims=True) + EPS)
            u = dhv * g_ref[...]
            dxf = r * u - xv * (r * r * r * jnp.mean(u * xv, axis=-1, keepdims=True))
            dxb = dxf.astype(BF16)
            dx_ref[...] = dxb
            dxt_ref[0] = dxf.T.astype(BF16)
            dg_ref[...] += jnp.sum(dhv * (xv * r), axis=0, keepdims=True)
            prod = xv * dxb.astype(F32)
            for g in range(PAIRS):
                grp = prod[:, g * LANES:(g + 1) * LANES]
                da = jnp.sum(jnp.where(_lane() < HEAD_DIM, grp, 0.0), axis=1, keepdims=True)
                db = jnp.sum(jnp.where(_lane() >= HEAD_DIM, grp, 0.0), axis=1, keepdims=True)
                sto_ref[g] = jnp.where(_lane() == 2, da, jnp.where(_lane() == 3, db, st_ref[g]))

    row = pl.BlockSpec((bt, d), lambda i: (i, 0))
    vec = pl.BlockSpec((1, d), lambda i: (0, 0))
    stat = pl.BlockSpec((PAIRS, bt, LANES), lambda i: (0, i, 0))
    return gridded(
        body, name=name, grid=(t // bt,),
        in_specs=[pl.BlockSpec((bt, dx1b.shape[1]), lambda i: (i, 0)), pl.BlockSpec(w_o.shape, lambda i: (0, 0)),
                  row, row, vec, vec, stat, stat],
        out_specs=[row, row, vec, vec, stat, stat] + [pl.BlockSpec((1, d, bt), lambda i: (i, 0, 0))] * 2,
        out_shape=[jax.ShapeDtypeStruct((t, d), BF16)] * 2 + [jax.ShapeDtypeStruct((1, d), F32)] * 2
        + [jax.ShapeDtypeStruct(st_f.shape, F32)] * 2 + [jax.ShapeDtypeStruct((t // bt, d, bt), BF16)] * 2,
        ins=[dx1b, w_o, fo, mo, g_fo, g_mo, st_f, st_m], semantics=("arbitrary",), side=side)


def flash_bwd(q_arr, qt_arr, k_arr, v_arr, do_arr, dot_arr, st, f_blocks, *, qoff, koff, voff, pair, scale, name, qblk=1024,
              side=None):
    t = q_arr.shape[0]
    blk = qt_arr.shape[2]
    qblk = max(min(qblk, t), blk)
    sub = qblk // blk
    nb, nbq = t // blk, t // qblk
    w = LANES if pair else 2 * LANES
    hw = w // 2
    has_bias = f_blocks is not None
    split = _halves if pair else _groups
    ins = [q_arr, qt_arr, k_arr, v_arr, do_arr, dot_arr, st] + ([f_blocks] if has_bias else [])
    n_out = 4 if has_bias else 3
    s_ins, s_outs = (side.ins, side.out_shapes) if side else ([], [])

    def wide(ref, first, count):
        parts = [ref[first + u] for u in range(count)]
        return parts[0] if count == 1 else jnp.concatenate(parts, axis=1)

    def body(*refs):
        main, si, outs, so, sems = _split_refs(refs, [len(ins), len(s_ins), n_out, len(s_outs), 2 if side else 0])
        q_ref, qt_ref, k_ref, v_ref, do_ref, dot_ref, st_ref = main[:7]
        dq_ref, dk_ref, dv_ref = outs[:3]
        g, j = pl.program_id(0), pl.program_id(1)
        step_id = g * nb + j
        if side:
            @pl.when(step_id == 0)
            def _():
                side.first(si, so, *sems)

        @pl.when(j == 0)
        def _():
            dq_ref[...] = jnp.zeros_like(dq_ref)

        kk, vv = k_ref[...], v_ref[...]
        ks = [kk, kk] if pair else _groups(kk)
        if has_bias:
            f_ref, df_ref = main[7], outs[3]
            fk = [LOG2E * _pick_row(f_ref[0], 2 * g + n) for n in range(2)]

            @pl.when(step_id == 0)
            def _():
                df_ref[...] = jnp.zeros_like(df_ref)

        def step(tb, count, carry, diagonal):
            rows = pl.ds(pl.multiple_of(tb * blk, blk), count * blk)
            qs = split((q_ref[rows, :].astype(F32) * (scale * LOG2E)).astype(BF16))
            qt = (wide(qt_ref, tb, count).astype(F32) * (scale * LOG2E)).astype(BF16)
            dos = [h.astype(BF16) for h in _halves(do_ref[rows, :].astype(F32))]
            dot = wide(dot_ref, tb, count)
            stats = st_ref[0, rows, :]
            new, dqs, row_sums = [], [], []
            for n in range(2):
                dkt, dvt, dfk = carry[n]
                s = lax.dot_general(qs[n], ks[n], NT, preferred_element_type=F32)
                if has_bias:
                    s = s - fk[n]
                if diagonal:
                    s = _below_diagonal(s)
                p = jnp.exp2(s - stats[:, n:n + 1])
                dp = lax.dot_general(dos[n], vv, NT, preferred_element_type=F32)
                ds = p * (dp - stats[:, 2 + n:3 + n])
                dsb = ds.astype(BF16)
                dvt = dvt + jnp.dot(dot[n * HEAD_DIM:(n + 1) * HEAD_DIM], p.astype(BF16), preferred_element_type=F32)
                dkt = dkt + jnp.dot(qt[n * hw:(n + 1) * hw], dsb, preferred_element_type=F32)
                dqs.append(jnp.dot(dsb, ks[n], preferred_element_type=F32))
                if has_bias:
                    dfk = dfk - jnp.sum(ds, axis=0, keepdims=True)
                    row_sums.append(jnp.sum(ds, axis=1, keepdims=True))
                new.append((dkt, dvt, dfk))
            dq = (jnp.where(_lane() < HEAD_DIM, dqs[0], dqs[1]) if pair else jnp.concatenate(dqs, axis=1)) * scale
            if has_bias:
                df_ref[rows, :] += jnp.where(_lane() == 2 * g, row_sums[0], jnp.where(_lane() == 2 * g + 1, row_sums[1], 0.0))
            dq_ref[rows, :] += dq
            return tuple(new)

        init = tuple((jnp.zeros((hw, blk), F32), jnp.zeros((HEAD_DIM, blk), F32), jnp.zeros((1, blk), F32)) for _ in range(2))
        carry = step(j, 1, init, True)
        whole = j // sub + 1
        carry = lax.fori_loop(j + 1, whole * sub, lambda tb, c: step(tb, 1, c, False), carry)
        (dka, dva, dfa), (dkb, dvb, dfb) = lax.fori_loop(whole, nbq, lambda i, c: step(i * sub, sub, c, False), carry)
        dk_ref[...] = (jnp.concatenate([dka, dkb], axis=0).T * LN2).astype(BF16)
        dv_ref[...] = jnp.concatenate([dva, dvb], axis=0).T.astype(BF16)
        if has_bias:
            row = lax.broadcasted_iota(jnp.int32, (LANES, 1), 0)
            by_head = jnp.where(row == 2 * g, dfa, jnp.where(row == 2 * g + 1, dfb, 0.0))
            df_ref[pl.ds(pl.multiple_of(j * blk, blk), blk), :] += by_head.T
        if side:
            @pl.when(step_id == PAIRS * nb - 1)
            def _():
                side.last(si, so, *sems)

    in_specs = [pl.BlockSpec((t, w), lambda g, j: (0, qoff + g)), pl.BlockSpec((nb, w, blk), lambda g, j: (0, qoff + g, 0)),
                pl.BlockSpec((blk, w), lambda g, j: (j, koff + g)), pl.BlockSpec((blk, LANES), lambda g, j: (j, voff + g)),
                pl.BlockSpec((t, LANES), lambda g, j: (0, g)), pl.BlockSpec((nb, LANES, blk), lambda g, j: (0, g, 0)),
                pl.BlockSpec((1, t, LANES), lambda g, j: (g, 0, 0))]
    out_specs = [pl.BlockSpec((t, w), lambda g, j: (0, g)), pl.BlockSpec((blk, w), lambda g, j: (j, g)),
                 pl.BlockSpec((blk, LANES), lambda g, j: (j, g))]
    out_shape = [jax.ShapeDtypeStruct((t, PAIRS * w), F32), jax.ShapeDtypeStruct((t, PAIRS * w), BF16),
                 jax.ShapeDtypeStruct((t, PAIRS * LANES), BF16)]
    if has_bias:
        in_specs.append(pl.BlockSpec((1, N_HEADS, blk), lambda g, j: (j, 0, 0)))
        out_specs.append(pl.BlockSpec((t, LANES), lambda g, j: (0, 0)))
        out_shape.append(jax.ShapeDtypeStruct((t, LANES), F32))
    s_in_specs, s_out_specs = side.specs() if side else ([], [])
    return pl.pallas_call(
        body, name=name, grid=(PAIRS, nb), in_specs=in_specs + s_in_specs, out_specs=out_specs + s_out_specs,
        out_shape=out_shape + list(s_outs), scratch_shapes=side.sems() if side else [],
        compiler_params=_params(("arbitrary", "arbitrary")),
    )(*ins, *s_ins)


def _adamw_math(w, g, m, v):
    nm = ADAM_B1 * m + (1.0 - ADAM_B1) * g
    nv = ADAM_B2 * v + (1.0 - ADAM_B2) * (g * g)
    m_hat = nm / (1.0 - ADAM_B1 ** ADAM_STEP)
    v_hat = nv / (1.0 - ADAM_B2 ** ADAM_STEP)
    return -ADAM_LR * (m_hat / (jnp.sqrt(v_hat) + ADAM_EPS) + ADAM_WD * w), nm, nv


def adamw_vectors(ws, g_rows, ms, vs, *, name):
    k = len(ws)

    def body(g_ref, *refs):
        w_refs, m_refs, v_refs, outs = _split_refs(refs, [k, k, k, 4 * k])
        for i in range(k):
            g = g_ref[i:i + 1, :w_refs[i].shape[1]]
            outs[4 * i][...] = g
            outs[4 * i + 1][...], outs[4 * i + 2][...], outs[4 * i + 3][...] = _adamw_math(
                w_refs[i][...], g, m_refs[i][...], v_refs[i][...])

    flat = pl.pallas_call(
        body, name=name, out_shape=[jax.ShapeDtypeStruct(w.shape, F32) for w in ws for _ in range(4)],
        compiler_params=_params(),
    )(g_rows, *ws, *ms, *vs)
    return [flat[4 * i:4 * i + 4] for i in range(k)]


def adamw_halves(w, g_mine, g_other, m, v, core, *, name):
    _, k, n = w.shape
    br = _row_block(k // 2)
    nh = k // 2 // br

    def body(c_ref, w_ref, gm_ref, go_ref, m_ref, v_ref, g_out, d_ref, nm_ref, nv_ref):
        gv = jnp.where(pl.program_id(0) == c_ref[0], gm_ref[...], go_ref[...])
        g_out[0] = gv
        d_ref[0], nm_ref[0], nv_ref[0] = _adamw_math(w_ref[0], gv, m_ref[0], v_ref[0])

    full = pl.BlockSpec((1, br, n), lambda hb, i, c: (0, hb * nh + i, 0))
    half = pl.BlockSpec((br, n), lambda hb, i, c: (i, 0))
    return pl.pallas_call(
        body, name=name,
        grid_spec=pltpu.PrefetchScalarGridSpec(num_scalar_prefetch=1, grid=(2, nh), in_specs=[full, half, half, full, full],
                                               out_specs=[full] * 4),
        out_shape=[jax.ShapeDtypeStruct(w.shape, F32)] * 4,
        compiler_params=_params(("parallel", "parallel")),
    )(core, w, g_mine, g_other, m, v)


def adamw_halves_t(wt, mt, vt, gt_mine, gt_other, core, *, name, bc=128):
    n, _, k = wt.shape
    nh = k // 2 // bc

    def body(c_ref, w_ref, m_ref, v_ref, gm_ref, go_ref, g_out, d_ref, nm_ref, nv_ref):
        gv = jnp.where(pl.program_id(0) == c_ref[0], gm_ref[...], go_ref[...])
        g_out[:, 0, :] = gv
        d_ref[:, 0, :], nm_ref[:, 0, :], nv_ref[:, 0, :] = _adamw_math(w_ref[:, 0, :], gv, m_ref[:, 0, :], v_ref[:, 0, :])

    full = pl.BlockSpec((n, 1, bc), lambda hb, i, c: (0, 0, hb * nh + i))
    half = pl.BlockSpec((n, bc), lambda hb, i, c: (0, i))
    return pl.pallas_call(
        body, name=name,
        grid_spec=pltpu.PrefetchScalarGridSpec(num_scalar_prefetch=1, grid=(2, nh), in_specs=[full, full, full, half, half],
                                               out_specs=[full] * 4),
        out_shape=[jax.ShapeDtypeStruct(wt.shape, F32)] * 4,
        compiler_params=_params(("parallel", "parallel")),
    )(core, wt, mt, vt, gt_mine, gt_other)


def add_pair(dw, recv, core, *, name):
    n4, k, n = dw.shape
    half = (1, k // 2, n) if split_axis(k) == 0 else (1, k, n // 2)
    mine = (lambda q, c: (q, c[0], 0)) if split_axis(k) == 0 else (lambda q, c: (q, 0, c[0]))

    def body(c_ref, a_ref, b_ref, o_ref):
        o_ref[...] = (a_ref[...] + b_ref[...].astype(F32)).astype(BF16)

    return pl.pallas_call(
        body, name=name,
        grid_spec=pltpu.PrefetchScalarGridSpec(
            num_scalar_prefetch=1, grid=(n4,),
            in_specs=[pl.BlockSpec(half, mine), pl.BlockSpec(half, lambda q, c: (q, 0, 0))],
            out_specs=pl.BlockSpec(half, lambda q, c: (q, 0, 0))),
        out_shape=jax.ShapeDtypeStruct((n4,) + half[1:], BF16),
        compiler_params=_params(("parallel",)),
    )(core, dw, recv)


def sum_chips(parts, *, name):
    n4, r, n = parts.shape
    if r % 16 == 0:
        br, bc = _row_block(r), n
    else:
        br, bc = r, LANES

    def body(p_ref, o_ref):
        acc = p_ref[0].astype(F32)
        for q in range(1, n4):
            acc = acc + p_ref[q].astype(F32)
        o_ref[...] = acc

    return pl.pallas_call(
        body, name=name, grid=(r // br, n // bc),
        in_specs=[pl.BlockSpec((n4, br, bc), lambda i, j: (0, i, j))], out_specs=pl.BlockSpec((br, bc), lambda i, j: (i, j)),
        out_shape=jax.ShapeDtypeStruct((r, n), F32),
        compiler_params=_params(("parallel", "parallel")),
    )(parts)


ANY = pl.BlockSpec(memory_space=pl.ANY)


def _place():
    x, y, c = lax.axis_index("x"), lax.axis_index("y"), lax.axis_index("c")
    chips = [(1 - x, y), (x, 1 - y), (1 - x, 1 - y)]
    return x, y, c, chips


def _copy(src, dst, send_sems, recv_sems, k, to):
    return pltpu.make_async_remote_copy(src_ref=src, dst_ref=dst, send_sem=send_sems.at[k], recv_sem=recv_sems.at[k],
                                        device_id=to, device_id_type=MESH)


def split_axis(rows):
    return 0 if rows % 32 == 0 else 1


def _half(ref, lead, hf):
    rows, cols = ref.shape[-2:]
    if split_axis(rows) == 0:
        at = (pl.ds(hf * (rows // 2), rows // 2), slice(None))
    else:
        at = (slice(None), pl.ds(hf * (cols // 2), cols // 2))
    return ref.at[at] if lead is None else ref.at[(lead,) + at]


def _gather_first(srcs, dsts, ssems, rsems):
    x, y, c, chips = _place()
    for ti, (s, d) in enumerate(zip(srcs, dsts)):
        for j, (cx, cy) in enumerate(chips):
            _copy(_half(s, None, c), _half(d, 2 * x + y, c), ssems, rsems, 3 * ti + j, (cx, cy, c)).start()


def _gather_mid(srcs, dsts, ssems, rsems):
    x, y, c, chips = _place()
    n1 = 3 * len(srcs)
    for ti, d in enumerate(dsts):
        for j, (cx, cy) in enumerate(chips):
            landed = _half(d, 2 * cx + cy, c)
            _copy(landed, landed, ssems, rsems, 3 * ti + j, (cx, cy, c)).wait_recv()
            _copy(landed, landed, ssems, rsems, n1 + 3 * ti + j, (x, y, 1 - c)).start()


def _gather_last(srcs, dsts, ssems, rsems):
    x, y, c, chips = _place()
    n1 = 3 * len(srcs)
    for ti, (s, d) in enumerate(zip(srcs, dsts)):
        for j, (cx, cy) in enumerate(chips):
            other = _half(d, 2 * cx + cy, 1 - c)
            _copy(other, other, ssems, rsems, n1 + 3 * ti + j, (x, y, 1 - c)).wait_recv()
        for j, (cx, cy) in enumerate(chips):
            mine = _half(s, None, c)
            _copy(mine, mine, ssems, rsems, 3 * ti + j, (cx, cy, c)).wait_send()
            _copy(mine, mine, ssems, rsems, n1 + 3 * ti + j, (x, y, 1 - c)).wait_send()


def gather_side(shards):
    return Side(shards, [jax.ShapeDtypeStruct((N_CHIPS,) + s.shape, s.dtype) for s in shards], 6 * len(shards),
                _gather_first, _gather_last, _gather_mid)


def _scatter_first(srcs, dsts, ssems, rsems):
    x, y, c, chips = _place()
    for ti, (s, d) in enumerate(zip(srcs, dsts)):
        for j, (cx, cy) in enumerate(chips):
            _copy(s.at[2 * cx + cy], d.at[2 * x + y], ssems, rsems, 3 * ti + j, (cx, cy, c)).start()


def _scatter_last(srcs, dsts, ssems, rsems):
    x, y, c, chips = _place()
    for ti, (s, d) in enumerate(zip(srcs, dsts)):
        for j, (cx, cy) in enumerate(chips):
            _copy(s.at[2 * cx + cy], d.at[2 * cx + cy], ssems, rsems, 3 * ti + j, (cx, cy, c)).wait_recv()
        for j, (cx, cy) in enumerate(chips):
            _copy(s.at[2 * cx + cy], d.at[2 * cx + cy], ssems, rsems, 3 * ti + j, (cx, cy, c)).wait_send()


def scatter_side(parts):
    return Side(parts, [jax.ShapeDtypeStruct(p.shape, p.dtype) for p in parts], 3 * len(parts), _scatter_first, _scatter_last)


def run_side(side, *, name):
    n_in, n_out = len(side.ins), len(side.out_shapes)

    def body(*refs):
        si, so, sems = _split_refs(refs, [n_in, n_out, 2])
        side.first(si, so, *sems)
        if side.mid is not None:
            side.mid(si, so, *sems)
        side.last(si, so, *sems)

    in_specs, out_specs = side.specs()
    return pl.pallas_call(body, name=name, in_specs=in_specs, out_specs=out_specs, out_shape=side.out_shapes,
                          scratch_shapes=side.sems())(*side.ins)


def _swap_first(srcs, dsts, ssems, rsems):
    x, y, c, _ = _place()
    for k, (s, d) in enumerate(zip(srcs, dsts)):
        _copy(s, d, ssems, rsems, k, (x, y, 1 - c)).start()


def _swap_last(srcs, dsts, ssems, rsems):
    x, y, c, _ = _place()
    for k, (s, d) in enumerate(zip(srcs, dsts)):
        _copy(s, d, ssems, rsems, k, (x, y, 1 - c)).wait()


def swap_side(xs):
    return Side(xs, [jax.ShapeDtypeStruct(a.shape, a.dtype) for a in xs], len(xs), _swap_first, _swap_last)


def _swap_halves(srcs, dsts, ssems, rsems):
    x, y, c, _ = _place()
    for k, (s, d) in enumerate(zip(srcs, dsts)):
        hk = s.shape[1] // 2
        yield _copy(s.at[:, pl.ds((1 - c) * hk, hk), :], d, ssems, rsems, k, (x, y, 1 - c))


def _swap_halves_first(srcs, dsts, ssems, rsems):
    for cp in _swap_halves(srcs, dsts, ssems, rsems):
        cp.start()


def _swap_halves_last(srcs, dsts, ssems, rsems):
    for cp in _swap_halves(srcs, dsts, ssems, rsems):
        cp.wait()


def swap_halves_side(xs):
    return Side(xs, [jax.ShapeDtypeStruct((a.shape[0], a.shape[1] // 2, a.shape[2]), a.dtype) for a in xs], len(xs),
                _swap_halves_first, _swap_halves_last)


def allreduce_small(s):
    n_dev = 8

    def body(s_ref, out_ref, buf, send_sems, recv_sems):
        x, y, c, _ = _place()
        me = 4 * x + 2 * y + c
        buf[me] = s_ref[...]
        sends = []
        for k in range(1, n_dev):
            px = 1 - x if k & 4 else x
            py = 1 - y if k & 2 else y
            pc = 1 - c if k & 1 else c
            cp = _copy(s_ref, buf.at[me], send_sems, recv_sems, k - 1, (px, py, pc))
            cp.start()
            sends.append((cp, 4 * px + 2 * py + pc))
        for k, (cp, peer) in enumerate(sends):
            _copy(s_ref, buf.at[peer], send_sems, recv_sems, k, (x, y, c)).wait_recv()
        for cp, _ in sends:
            cp.wait_send()
        acc = buf[0]
        for d in range(1, n_dev):
            acc = acc + buf[d]
        out_ref[...] = acc

    vm = pl.BlockSpec(memory_space=pltpu.VMEM)
    return pl.pallas_call(
        body, name="allreduce_small", in_specs=[vm], out_specs=vm,
        out_shape=jax.ShapeDtypeStruct(s.shape, F32),
        scratch_shapes=[pltpu.VMEM((n_dev,) + s.shape, F32), pltpu.SemaphoreType.DMA((n_dev - 1,)),
                        pltpu.SemaphoreType.DMA((n_dev - 1,))],
    )(s)


def join_cols(sm):
    n4, k, n = sm.shape
    return sm.transpose(1, 0, 2).reshape(k, n4 * n)


def split_cols(full):
    k, n = full.shape
    return full.reshape(k, N_CHIPS, n // N_CHIPS).transpose(1, 0, 2)


def _pad_heads(w, width):
    lead, heads = w.shape[:-1], w.shape[-1] // width
    w = w.reshape(lead + (heads, width))
    return jnp.pad(w, [(0, 0)] * len(lead) + [(0, 0), (0, LANES - width)]).reshape(lead + (heads * LANES,))


def _unpad_heads(w, width):
    lead, heads = w.shape[:-1], w.shape[-1] // LANES
    return w.reshape(lead + (heads, LANES))[..., :width].reshape(lead + (heads * width,))


O_F = 3 * FW
O_CQ = O_F + N_HEADS
O_CKV = O_CQ + Q_RANK
O_KR = O_CKV + KV_RANK
O_END = O_KR + ROPE_DIM


def _shard_rows(sm, a, b):
    r = sm.shape[1]
    out = []
    while a < b:
        q = a // r
        e = min(b, (q + 1) * r)
        out.append(sm[q, a - q * r:e - q * r])
        a = e
    return out


def split_w_in_t(w_sm):
    n4, r, d = w_sm.shape
    segments = [(0, O_F, 0, 0), (O_F, N_HEADS, 1, S_F), (O_CQ, Q_RANK, 1, S_CQ), (O_CKV, KV_RANK, 1, S_CKV),
                (O_KR, ROPE_DIM, 1, S_KR + HEAD_DIM)]
    moves = []
    for v0, n, dst, d0 in segments:
        for q in range(n4):
            a, b = max(v0, q * r), min(v0 + n, (q + 1) * r)
            if a < b:
                assert a % 2 == 0 and b % 2 == 0 and (d0 + a - v0) % 2 == 0
                moves.append((q, (a - q * r) // 2, (b - a) // 2, dst, (d0 + a - v0) // 2))

    def body(w_ref, qkv_ref, small_ref, src, dst_qkv, dst_small):
        for q in range(n4):
            src[q] = pltpu.bitcast(w_ref[q], jnp.uint32)
        dst_small[...] = jnp.zeros(dst_small.shape, jnp.uint32)
        for q, a, n, dst, b in moves:
            (dst_qkv, dst_small)[dst][pl.ds(b, n), :] = src[q, pl.ds(a, n), :]
        qkv_ref[...] = pltpu.bitcast(dst_qkv[...], w_sm.dtype)
        small_ref[...] = pltpu.bitcast(dst_small[...], w_sm.dtype)

    return pl.pallas_call(
        body, name="split_w_in",
        out_shape=[jax.ShapeDtypeStruct((O_F, d), w_sm.dtype), jax.ShapeDtypeStruct((S_END, d), w_sm.dtype)],
        scratch_shapes=[pltpu.VMEM((n4, r // 2, d), jnp.uint32), pltpu.VMEM((O_F // 2, d), jnp.uint32),
                        pltpu.VMEM((S_END // 2, d), jnp.uint32)],
        compiler_params=_params(),
    )(w_sm)


def join_w_in_t(d_qkv_t, d_small_t):
    kr = S_KR + HEAD_DIM
    segments = [(part, 0, n * FW, FW) for n, part in enumerate(d_qkv_t)]
    segments += [(d_small_t, S_F, O_F, N_HEADS), (d_small_t, S_CQ, O_CQ, Q_RANK),
                 (d_small_t, S_CKV, O_CKV, KV_RANK), (d_small_t, kr, O_KR, ROPE_DIM)]
    r = O_END // N_CHIPS
    shards = []
    for q in range(N_CHIPS):
        pieces = []
        for src, s0, v0, n in segments:
            a, b = max(v0, q * r), min(v0 + n, (q + 1) * r)
            if a < b:
                pieces.append(src[s0 + a - v0:s0 + b - v0])
        shards.append(jnp.concatenate(pieces, axis=0))
    return jnp.stack(shards)


def rope_inputs(pos):
    inv_freq = ROPE_THETA ** (-jnp.arange(0, ROPE_DIM, 2, dtype=F32) / ROPE_DIM)
    inv_row = jnp.concatenate([jnp.zeros((HEAD_DIM,), F32), inv_freq, inv_freq, jnp.zeros((LANES - HEAD_DIM - ROPE_DIM,), F32)])
    return pos.astype(F32)[:, None], inv_row[None, :]


def _pad_lanes(v, n):
    return jnp.pad(v, ((0, 0), (0, n - v.shape[1])))


ATTN_BLK = 512
ATTN_FWD_BLK = 1024
ATTN_BWD_QBLK = 1024
ACC_ROWS = HEAD_DIM + 16


def local_step(xs, pos, tgt, gains, early_weights, late_weights, early_side=None, fwd_sides=(None, None), reduction=None):
    g_attn, b_forget, g_q, g_kv, g_fo, g_mo, g_mlp, g_fin = gains
    t = xs.shape[0]
    blk = min(ATTN_BLK, t)
    fox_scale = 1.0 / (HEAD_DIM ** 0.5)
    mla_scale = 1.0 / ((HEAD_DIM + ROPE_DIM) ** 0.5)

    h1, *gathered = rmsnorm(xs, g_attn, out_dtype=BF16, name="norm_attn", side=early_side)
    w_in_t, w_uq_p, w_ukv = early_weights(gathered)
    w_qkv_t, w_small_t = split_w_in_t(w_in_t)
    kv = w_ukv.reshape(KV_RANK, N_HEADS, 2 * HEAD_DIM)
    w_ukv_p = jnp.concatenate([_pad_heads(kv[:, :, :HEAD_DIM].reshape(KV_RANK, FW), HEAD_DIM),
                               kv[:, :, HEAD_DIM:].reshape(KV_RANK, FW)], axis=1)
    b_f = _pad_lanes(b_forget, LANES)
    pos_col, inv_row = rope_inputs(pos)

    qkv, qkv_t = mm(h1, w_qkv_t, trans_b=True, out_dtypes=[BF16], t_blk=blk, name="proj_qkv", bn=1536)
    small, = mm(h1, w_small_t, trans_b=True, out_dtypes=[F32], name="proj_small")
    mq, mk, mv, lf, cqn, ckvn, mq_t, mv_t = mla_prep(small, g_q, g_kv, w_uq_p, w_ukv_p, pos_col, inv_row, b_f,
                                                     name="mla_prep", bt=blk)
    f_cum = cumsum_rows(lf, reverse=False, name="gate_cumsum")
    f_blocks = f_cum[:, :N_HEADS].reshape(t // blk, blk, N_HEADS).transpose(0, 2, 1)
    fo, st_f, *gathered = flash_fwd(qkv_t, qkv, qkv_t, f_cum, qoff=0, koff=PAIRS, voff=2 * PAIRS, pair=True,
                                    scale=fox_scale, name="fox_fwd", blk=ATTN_FWD_BLK, side=fwd_sides[0])
    mo, st_m, *more = flash_fwd(mq_t, mk, mv_t, None, qoff=0, koff=0, voff=0, pair=False, scale=mla_scale, name="mla_fwd",
                                blk=ATTN_FWD_BLK, side=fwd_sides[1])
    w_o, w_up, w_down = late_weights(gathered + more)
    mixed = mix_norm(fo, mo, g_fo, g_mo, name="norm_mix")

    def inv_rms(v):
        return lax.rsqrt(jnp.mean(v * v, axis=-1, keepdims=True) + EPS)

    def residual_then_norm(acc, res, g):
        xn = acc + res
        return xn, xn * inv_rms(xn) * g

    def norm_bwd(dh, xn, res, g):
        r = inv_rms(xn)
        uu = dh * g
        return (r * uu - xn * (r * r * r * jnp.mean(uu * xn, axis=-1, keepdims=True)) + res,
                jnp.sum(dh * (xn * r), axis=0, keepdims=True))

    def norm_bwd2(dh, xn, res, g):
        dx, dg = norm_bwd(dh, xn, res, g)
        return dx, dx, dg

    def residual_then_loss(acc, res, target, g):
        xn = acc + res
        r = inv_rms(xn)
        xh = xn * r
        e = xh * g - target
        part = 0.5 * jnp.sum(jnp.mean(e * e, axis=-1, keepdims=True), axis=0, keepdims=True)
        dy = e * (1.0 / xn.shape[1])
        uu = dy * g
        dx = r * uu - xn * (r * r * r * jnp.mean(uu * xn, axis=-1, keepdims=True))
        return dx, dx, jnp.sum(dy * xh, axis=0, keepdims=True), part + jnp.zeros_like(g)

    x1, h2 = mm(mixed, w_o, extras=[xs], vecs=[g_mlp], epilogue=residual_then_norm, out_dtypes=[F32, BF16], name="out_proj")

    def relu2(uu):
        r = jnp.maximum(uu.astype(F32), 0.0)
        return (r * r).astype(BF16)

    u, = mm(h2, w_up, out_dtypes=[BF16], name="mlp_up", bn=2048)
    dx2, dx2b, dg_fin, loss_row = mm(u, w_down, a_pro=relu2, extras=[x1, tgt], vecs=[g_fin], epilogue=residual_then_loss,
                                     out_dtypes=[F32, BF16], n_sums=2, name="mlp_down_loss")
    loss = loss_row[:, :1]

    def relu2_grad(acc, uu):
        return (acc * (2.0 * jnp.maximum(uu.astype(F32), 0.0)),)

    du, = mm(dx2b, w_down, trans_b=True, extras=[u], epilogue=relu2_grad, out_dtypes=[BF16], name="mlp_down_bwd", bn=2048)
    dw_down, dw_down_b = (g.reshape(N_CHIPS, -1, w_down.shape[1])
                          for g in mm_tn(u, dx2b, a_pro=relu2, name="dw_down", bf16_copy=True))
    dx1, dx1b, dg_mlp = mm(du, w_up, trans_b=True, extras=[x1, dx2], vecs=[g_mlp], epilogue=norm_bwd2,
                           out_dtypes=[F32, BF16], n_sums=1, name="mlp_up_bwd")
    dw_up, dw_up_b = mm_tn(h2, du, name="dw_up", col_shards=N_CHIPS, bf16_copy=True)

    dw_o, dw_o_b = (g.reshape(N_CHIPS, -1, w_o.shape[1]) for g in mm_tn(mixed, dx1b, name="dw_o", bf16_copy=True))
    late, late_b = (dw_o, dw_up, dw_down), (dw_o_b, dw_up_b, dw_down_b)
    red = reduction
    dfo, dmo, dg_fo, dg_mo, st_f, st_m, dfo_t, dmo_t, *got = mix_norm_bwd(
        dx1b, w_o, fo, mo, g_fo, g_mo, st_f, st_m, name="out_proj_mix_bwd", bt=blk,
        side=red.late_swap(late, late_b) if red else None)
    dfq, dfk, dfv, d_f, *got = flash_bwd(
        qkv, qkv_t, qkv, qkv, dfo, dfo_t, st_f, f_blocks, qoff=0, koff=PAIRS, voff=2 * PAIRS, pair=True,
        scale=fox_scale, name="fox_bwd", qblk=ATTN_BWD_QBLK, side=red.late_scatter(got) if red else None)
    dmq, dmk, dmv, *got = flash_bwd(mq, mq_t, mk, mv, dmo, dmo_t, st_m, None, qoff=0, koff=0, voff=0,
                                    pair=False, scale=mla_scale, name="mla_bwd", qblk=ATTN_BWD_QBLK,
                                    side=red.late_halves(got) if red else None)
    if red:
        red.late_done(got)
    dlf = cumsum_rows(d_f, reverse=True, name="gate_cumsum_bwd")
    dsmall, dq_u, dkv_u, dg_q, dg_kv, db_f = mla_prep_bwd(dmq, dmk, dmv, dlf, small, g_q, g_kv, w_uq_p, w_ukv_p,
                                                          pos_col, inv_row, b_f, name="mla_prep_bwd")
    dw_uq_p = mm_tn(cqn, dq_u, name="dw_uq")
    dw_ukv_p = mm_tn(ckvn, dkv_u, name="dw_ukv")

    def to_bf16(tile):
        return tile.astype(BF16)

    dqkv = [dfq, dfk, dfv]
    dw_in_t = join_w_in_t([mm_tn(part, h1, a_pro=to_bf16, name="dw_" + nm) for part, nm in zip(dqkv, "qkv")],
                          mm_tn(dsmall, h1, name="dw_small"))
    dk_cols = _unpad_heads(dw_ukv_p[:, :HW], HEAD_DIM).reshape(KV_RANK, N_HEADS, HEAD_DIM)
    dv_cols = dw_ukv_p[:, HW:].reshape(KV_RANK, N_HEADS, HEAD_DIM)
    dw_ukv = jnp.concatenate([dk_cols, dv_cols], axis=2).reshape(KV_RANK, N_HEADS * 2 * HEAD_DIM)
    early = (dw_in_t, split_cols(dw_uq_p), split_cols(dw_ukv))
    w_parts = [w_qkv_t[n * FW:(n + 1) * FW] for n in range(3)]
    grad_x, dg_attn, *got = mm(dqkv + [dsmall], w_parts + [w_small_t], a_pro=to_bf16, extras=[xs, dx1], vecs=[g_attn],
                               epilogue=norm_bwd, out_dtypes=[F32], n_sums=1, name="proj_bwd",
                               side=red.early_scatter(early) if red else None)
    if red:
        red.early_done(got)
    d_gains = (dg_attn, db_f[:, :N_HEADS], dg_q, dg_kv, dg_fo, dg_mo, dg_mlp, dg_fin)
    return loss, grad_x, early, late, d_gains


class GradReduction:
    def __init__(self, core_id, chip):
        self.core_id, self.chip = core_id, chip
        self.core = core_id.reshape(1).astype(jnp.int32)
        self.grads, self.pairs, self.halves, self.others = {}, {}, {}, {}

    def _other_halves(self, grads):
        out = []
        for g in grads:
            axis = 1 + split_axis(g.shape[1])
            size = g.shape[axis] // 2
            out.append(lax.dynamic_slice_in_dim(g, (1 - self.core_id) * size, size, axis=axis).astype(BF16))
        return out

    def _add_pairs(self, group, recvs):
        self.pairs[group] = [add_pair(g, r, self.core, name="add_pair_%s_%d" % (group, n))
                             for n, (g, r) in enumerate(zip(self.grads[group], recvs))]
        return scatter_side(self.pairs[group])

    def _chip_sums(self, group, scattered):
        chip = self.chip
        with_mine = [lax.dynamic_update_index_in_dim(s, lax.dynamic_index_in_dim(p, chip, 0, keepdims=True), chip, 0)
                     for s, p in zip(scattered, self.pairs[group])]
        self.halves[group] = [sum_chips(s, name="sum_chips_%s_%d" % (group, n)) for n, s in enumerate(with_mine)]
        return self.halves[group]

    def late_swap(self, grads, bf16_copies):
        self.grads["late"] = list(grads)
        return swap_halves_side(list(bf16_copies))

    def late_scatter(self, recvs):
        return self._add_pairs("late", recvs)

    def late_halves(self, scattered):
        return swap_side(self._chip_sums("late", scattered))

    def late_done(self, others):
        self.others["late"] = list(others)

    def early_scatter(self, grads):
        self.grads["early"] = list(grads)
        return self._add_pairs("early", run_side(swap_side(self._other_halves(grads)), name="swap_early_sends"))

    def early_done(self, scattered):
        self.others["early"] = list(run_side(swap_side(self._chip_sums("early", scattered)), name="swap_early_halves"))

def kernel(x, positions, attn_norm_g, w_in, b_forget, q_norm_g, w_uq, kv_norm_g, w_ukv, fox_out_g, mla_out_g, w_o, mlp_norm_g, w_up, w_down, final_norm_g, loss_target, m_attn_norm_g, m_w_in, m_b_forget, m_q_norm_g, m_w_uq, m_kv_norm_g, m_w_ukv, m_fox_out_g, m_mla_out_g, m_w_o, m_mlp_norm_g, m_w_up, m_w_down, m_final_norm_g, v_attn_norm_g, v_w_in, v_b_forget, v_q_norm_g, v_w_uq, v_kv_norm_g, v_w_ukv, v_fox_out_g, v_mla_out_g, v_w_o, v_mlp_norm_g, v_w_up, v_w_down, v_final_norm_g):
    core_id = lax.axis_index("c")
    core = core_id.reshape(1).astype(jnp.int32)
    chip = 2 * lax.axis_index("x") + lax.axis_index("y")
    big = [w_in, w_uq, w_ukv, w_o, w_up, w_down]
    big_m = [m_w_in, m_w_uq, m_w_ukv, m_w_o, m_w_up, m_w_down]
    big_v = [v_w_in, v_w_uq, v_w_ukv, v_w_o, v_w_up, v_w_down]
    n_early = 3

    def vec(a):
        return a.reshape(1, -1)

    small = [attn_norm_g, b_forget, q_norm_g, kv_norm_g, fox_out_g, mla_out_g, mlp_norm_g, final_norm_g]
    small_m = [m_attn_norm_g, m_b_forget, m_q_norm_g, m_kv_norm_g, m_fox_out_g, m_mla_out_g, m_mlp_norm_g, m_final_norm_g]
    small_v = [v_attn_norm_g, v_b_forget, v_q_norm_g, v_kv_norm_g, v_fox_out_g, v_mla_out_g, v_mlp_norm_g, v_final_norm_g]
    gains = [vec(a) for a in small]

    views = [big[0][0].T, _pad_heads(big[1][0], HEAD_DIM + ROPE_DIM)] + [w[0] for w in big[2:]]
    shards = [v.astype(BF16) for v in views]

    def with_own(gathered, mine):
        return [lax.dynamic_update_index_in_dim(g, s, chip, 0) for g, s in zip(gathered, mine)]

    def early_weights(gathered):
        g_in, g_uq, g_ukv = with_own(gathered, shards[:n_early])
        return g_in, join_cols(g_uq), join_cols(g_ukv)

    def late_weights(gathered):
        g_o, g_up, g_down = with_own(gathered, shards[n_early:])
        return g_o.reshape(-1, g_o.shape[2]), join_cols(g_up), g_down.reshape(-1, g_down.shape[2])

    reduction = GradReduction(core_id, chip)
    loss, grad_x, _, _, d_small = local_step(
        x[0], positions[0], loss_target[0], gains, early_weights, late_weights, gather_side(shards[:n_early]),
        (gather_side(shards[n_early:-1]), gather_side(shards[-1:])), reduction)
    halves = reduction.halves["early"] + reduction.halves["late"]
    others = reduction.others["early"] + reduction.others["late"]

    def rows8(vs):
        return jnp.concatenate([_pad_lanes(vec(a).astype(F32), 1024) for a in vs], axis=0)

    with_loss = [jnp.concatenate([d, loss], axis=1) if n == 1 else d for n, d in enumerate(d_small)]
    g_small8 = allreduce_small(rows8(with_loss))

    outs_big = []
    for n, (w, gm, go, m, v) in enumerate(zip(big, halves, others, big_m, big_v)):
        if n == 0:
            outs = adamw_halves_t(*(jnp.transpose(a, (2, 0, 1)) for a in (w, m, v)), gm, go, core, name="adamw_%d" % n)
            outs_big.append([jnp.transpose(o, (1, 2, 0)) for o in outs])
        else:
            if n == 1:
                gm, go = (_unpad_heads(gh, HEAD_DIM + ROPE_DIM) for gh in (gm, go))
            outs_big.append(adamw_halves(w, gm, go, m, v, core, name="adamw_%d" % n))
    outs_small = adamw_vectors(gains, g_small8, [vec(a) for a in small_m], [vec(a) for a in small_v], name="adamw_small")

    loss_all = g_small8[1, N_HEADS]
    grads, deltas, new_m, new_v = [None] * 14, [None] * 14, [None] * 14, [None] * 14
    big_at = [1, 4, 6, 9, 11, 12]
    small_at = [0, 2, 3, 5, 7, 8, 10, 13]
    for n, at in enumerate(big_at):
        grads[at], deltas[at], new_m[at], new_v[at] = outs_big[n]
    for at, s, outs in zip(small_at, small, outs_small):
        grads[at], deltas[at], new_m[at], new_v[at] = (o.reshape(s.shape) for o in outs)
    return (loss_all, grad_x[None], *grads, *deltas, *new_m, *new_v)
```

```python
import jax
import jax.numpy as jnp
from jax import lax
from jax.experimental import pallas as pl
from jax.experimental.pallas import tpu as pltpu

F32 = jnp.float32
BF16 = jnp.bfloat16
MESH = pl.DeviceIdType.MESH

EPS = 1e-6
ROPE_THETA = 10000.0
N_HEADS = 8
PAIRS = N_HEADS // 2
HEAD_DIM = 64
ROPE_DIM = 32
LANES = 128
Q_RANK = 384
KV_RANK = 256
N_CHIPS = 4
ADAM_LR, ADAM_B1, ADAM_B2, ADAM_EPS, ADAM_WD, ADAM_STEP = 0.001, 0.9, 0.999, 1e-08, 0.01, 10
VMEM_LIMIT = 48 * 1024 * 1024
LOG2E = 1.4426950408889634
LN2 = 0.6931471805599453
NN = (((1,), (0,)), ((), ()))
NT = (((1,), (1,)), ((), ()))
TN = (((0,), (0,)), ((), ()))


def _params(sem=None):
    return pltpu.CompilerParams(dimension_semantics=sem, vmem_limit_bytes=VMEM_LIMIT)


def _fit(block, dim):
    if dim <= block:
        return dim
    return next(b for b in range(block - block % LANES, 0, -LANES) if dim % b == 0)


def _row_block(rows):
    return next(b for b in (256, 128, 64, 32, 16, 8) if rows % b == 0)


def gridded(body, *, name, grid, in_specs, out_specs, out_shape, ins, semantics, side=None):
    if side is None:
        return pl.pallas_call(body, name=name, grid=grid, in_specs=in_specs, out_specs=out_specs, out_shape=out_shape,
                              compiler_params=_params(semantics))(*ins)
    n_in, n_out = len(in_specs), len(out_specs)
    steps = 1
    for extent in grid:
        steps *= extent

    def riding(*refs):
        main_in, s_in, main_out, s_out, sems = _split_refs(refs, [n_in, len(side.ins), n_out, len(side.out_shapes), 2])
        step = 0
        for axis, extent in enumerate(grid):
            step = step * extent + pl.program_id(axis)

        @pl.when(step == 0)
        def _():
            side.first(s_in, s_out, *sems)

        body(*main_in, *main_out)

        @pl.when(step == steps - 1)
        def _():
            if side.mid is not None:
                side.mid(s_in, s_out, *sems)
            side.last(s_in, s_out, *sems)

    s_in_specs, s_out_specs = side.specs()
    return pl.pallas_call(
        riding, name=name, grid=grid, in_specs=list(in_specs) + s_in_specs, out_specs=list(out_specs) + s_out_specs,
        out_shape=list(out_shape) + side.out_shapes, scratch_shapes=side.sems(),
        compiler_params=_params(("arbitrary",) * len(grid)))(*ins, *side.ins)


def rmsnorm(x, g, *, out_dtype, name, bt=512, side=None):
    t, d = x.shape
    bt = min(bt, t)

    def body(x_ref, g_ref, o_ref):
        xv = x_ref[...].astype(F32)
        r = lax.rsqrt(jnp.mean(xv * xv, axis=-1, keepdims=True) + EPS)
        o_ref[...] = (xv * r * g_ref[...]).astype(o_ref.dtype)

    return gridded(
        body, name=name, grid=(t // bt,),
        in_specs=[pl.BlockSpec((bt, d), lambda i: (i, 0)), pl.BlockSpec((1, d), lambda i: (0, 0))],
        out_specs=[pl.BlockSpec((bt, d), lambda i: (i, 0))],
        out_shape=[jax.ShapeDtypeStruct((t, d), out_dtype)],
        ins=[x, g], semantics=("parallel",), side=side)


def mm(a, b, *, trans_b=False, a_pro=None, extras=(), vecs=(), epilogue=None, out_dtypes, n_sums=0, t_blk=None, name,
       bm=1024, bn=1024, side=None):
    a_list = list(a) if isinstance(a, (list, tuple)) else [a]
    b_list = list(b) if isinstance(b, (list, tuple)) else [b]
    m = a_list[0].shape[0]
    n = b_list[0].shape[0] if trans_b else b_list[0].shape[1]
    ks = [x.shape[1] for x in a_list]
    if sum(ks) > 2048:
        bm = bm // 2
    bm, bn = _fit(bm, m), _fit(bn, n)
    assert n_sums == 0 or bn == n
    n_ab, n_ex, n_vec, n_out = len(a_list), len(extras), len(vecs), len(out_dtypes)
    n_t = 0 if t_blk is None else 1

    def body(*refs):
        a_refs, b_refs, ex, vs, outs, t_outs, sums = _split_refs(refs, [n_ab, n_ab, n_ex, n_vec, n_out, n_t, n_sums])
        acc = None
        for a_ref, b_ref in zip(a_refs, b_refs):
            a_tile = a_ref[...] if a_pro is None else a_pro(a_ref[...])
            part = lax.dot_general(a_tile, b_ref[...], NT if trans_b else NN, preferred_element_type=F32)
            acc = part if acc is None else acc + part
        res = epilogue(acc, *[e[...] for e in ex], *[v[...] for v in vs]) if epilogue is not None else (acc,)
        for o, r in zip(outs, res[:n_out]):
            o[...] = r.astype(o.dtype)
        for t_ref in t_outs:
            for u in range(bm // t_blk):
                t_ref[u] = res[0][u * t_blk:(u + 1) * t_blk, :].T.astype(t_ref.dtype)
        if n_sums:
            @pl.when(pl.program_id(0) == 0)
            def _():
                for s_ref in sums:
                    s_ref[...] = jnp.zeros_like(s_ref)

            for s_ref, r in zip(sums, res[n_out:]):
                s_ref[...] += r

    tile = pl.BlockSpec((bm, bn), lambda i, j: (i, j))
    vec = pl.BlockSpec((1, bn), lambda i, j: (0, j))
    t_specs, t_shapes = [], []
    if t_blk is not None:
        t_specs = [pl.BlockSpec((bm // t_blk, bn, t_blk), lambda i, j: (i, j, 0))]
        t_shapes = [jax.ShapeDtypeStruct((m // t_blk, n, t_blk), out_dtypes[0])]
    a_specs = [pl.BlockSpec((bm, k), lambda i, j: (i, 0)) for k in ks]
    b_specs = [pl.BlockSpec((bn, k), lambda i, j: (j, 0)) if trans_b else pl.BlockSpec((k, bn), lambda i, j: (0, j)) for k in ks]
    return gridded(
        body, name=name, grid=(m // bm, n // bn),
        in_specs=a_specs + b_specs + [tile] * n_ex + [vec] * n_vec,
        out_specs=[tile] * n_out + t_specs + [vec] * n_sums,
        out_shape=[jax.ShapeDtypeStruct((m, n), dt) for dt in out_dtypes] + t_shapes + [jax.ShapeDtypeStruct((1, n), F32)] * n_sums,
        ins=[*a_list, *b_list, *extras, *vecs],
        semantics=("arbitrary", "arbitrary") if n_sums else ("parallel", "parallel"), side=side)


def mm_tn(a, b, *, a_pro=None, name, col_shards=1, bf16_copy=False, bk=1024, bn=1024, bt=2048):
    t, k = a.shape
    n = b.shape[1]
    ns = n // col_shards
    bk, bn, bt = _fit(bk, k), _fit(bn, ns), _fit(bt, t)
    per = ns // bn
    last = t // bt - 1

    def body(a_ref, b_ref, o_ref, *copy_ref):
        @pl.when(pl.program_id(2) == 0)
        def _():
            o_ref[...] = jnp.zeros_like(o_ref)

        a_tile = a_ref[...] if a_pro is None else a_pro(a_ref[...])
        o_ref[...] += lax.dot_general(a_tile, b_ref[...], TN, preferred_element_type=F32)
        if bf16_copy:
            @pl.when(pl.program_id(2) == last)
            def _():
                copy_ref[0][...] = o_ref[...].astype(BF16)

    if col_shards == 1:
        out_spec = pl.BlockSpec((bk, bn), lambda i, j, s: (i, j))
        shape = (k, n)
    else:
        out_spec = pl.BlockSpec((None, bk, bn), lambda i, j, s: (j // per, i, j % per))
        shape = (col_shards, k, ns)
    out_specs, out_shape = out_spec, jax.ShapeDtypeStruct(shape, F32)
    if bf16_copy:
        out_specs, out_shape = [out_spec, out_spec], [out_shape, jax.ShapeDtypeStruct(shape, BF16)]
    return pl.pallas_call(
        body, name=name, grid=(k // bk, n // bn, t // bt),
        in_specs=[pl.BlockSpec((bt, bk), lambda i, j, s: (s, i)), pl.BlockSpec((bt, bn), lambda i, j, s: (s, j))],
        out_specs=out_specs, out_shape=out_shape,
        compiler_params=_params(("parallel", "parallel", "arbitrary")),
    )(a, b)


def _split3(x):
    hi = x.astype(BF16)
    r1 = x - hi.astype(F32)
    mid = r1.astype(BF16)
    lo = (r1 - mid.astype(F32)).astype(BF16)
    return hi, mid, lo


def cumsum_rows(x, *, reverse, name, bc=512):
    t, d = x.shape
    bc = min(bc, t)
    nb = t // bc

    def body(x_ref, o_ref, carry):
        @pl.when(pl.program_id(0) == 0)
        def _():
            carry[...] = jnp.zeros_like(carry)

        r = lax.broadcasted_iota(jnp.int32, (bc, bc), 0)
        c = lax.broadcasted_iota(jnp.int32, (bc, bc), 1)
        tri = jnp.where((r <= c) if reverse else (r >= c), 1.0, 0.0).astype(BF16)
        hi, mid, lo = _split3(x_ref[...])
        s = (lax.dot_general(tri, hi, NN, preferred_element_type=F32)
             + lax.dot_general(tri, mid, NN, preferred_element_type=F32)
             + lax.dot_general(tri, lo, NN, preferred_element_type=F32)) + carry[0:1, :]
        o_ref[...] = s
        carry[0:1, :] = s[0:1, :] if reverse else s[bc - 1:bc, :]

    imap = (lambda i: (nb - 1 - i, 0)) if reverse else (lambda i: (i, 0))
    return pl.pallas_call(
        body, name=name, grid=(nb,),
        in_specs=[pl.BlockSpec((bc, d), imap)], out_specs=pl.BlockSpec((bc, d), imap),
        out_shape=jax.ShapeDtypeStruct((t, d), F32),
        scratch_shapes=[pltpu.VMEM((8, d), F32)],
        compiler_params=_params(("arbitrary",)),
    )(x)


def _rope(x, c, a, b):
    return x * c + pltpu.roll(x, LANES - ROPE_DIM // 2, 1) * a + pltpu.roll(x, ROPE_DIM // 2, 1) * b


def _rope_bwd(d, c, a, b):
    return d * c + pltpu.roll(d * a, ROPE_DIM // 2, 1) + pltpu.roll(d * b, LANES - ROPE_DIM // 2, 1)


S_CQ, S_CKV, S_KR, S_F, S_END = 0, Q_RANK, Q_RANK + KV_RANK, Q_RANK + KV_RANK + LANES, 1024
HW = N_HEADS * LANES
FW = N_HEADS * HEAD_DIM


def _rope_tables(pos_col, inv_row):
    ang = pos_col * inv_row
    cos, sin = jnp.cos(ang), jnp.sin(ang)
    lane = lax.broadcasted_iota(jnp.int32, (1, LANES), 1)
    first = (lane >= HEAD_DIM) & (lane < HEAD_DIM + ROPE_DIM // 2)
    second = (lane >= HEAD_DIM + ROPE_DIM // 2) & (lane < HEAD_DIM + ROPE_DIM)
    return (jnp.where(lane < HEAD_DIM, 1.0, jnp.where(first | second, cos, 0.0)), jnp.where(first, -sin, 0.0),
            jnp.where(second, sin, 0.0))


def mla_prep(small, g_q, g_kv, w_uq, w_ukv, pos_col, inv_row, b_f, *, name, bt=512):
    t = small.shape[0]
    bt = min(bt, t)

    def body(s_ref, gq_ref, gkv_ref, wq_ref, wkv_ref, pos_ref, inv_ref, bf_ref,
             mq_ref, mk_ref, mv_ref, lf_ref, cqn_ref, ckvn_ref, mqt_ref, mvt_ref):
        cq = s_ref[:, S_CQ:S_CKV]
        rq = lax.rsqrt(jnp.mean(cq * cq, axis=-1, keepdims=True) + EPS)
        cqn = (cq * rq * gq_ref[...]).astype(BF16)
        ckv = s_ref[:, S_CKV:S_KR]
        rkv = lax.rsqrt(jnp.mean(ckv * ckv, axis=-1, keepdims=True) + EPS)
        ckvn = (ckv * rkv * gkv_ref[...]).astype(BF16)
        cqn_ref[...] = cqn
        ckvn_ref[...] = ckvn
        tc, ta, tb = _rope_tables(pos_ref[...], inv_ref[...])
        q = jnp.dot(cqn, wq_ref[...], preferred_element_type=F32)
        kv = jnp.dot(ckvn, wkv_ref[...], preferred_element_type=F32)
        kr = _rope(s_ref[:, S_KR:S_F], tc, ta, tb)
        for h in range(N_HEADS):
            sl = slice(h * LANES, (h + 1) * LANES)
            roped = _rope(q[:, sl], tc, ta, tb)
            mq_ref[:, sl] = roped.astype(BF16)
            mqt_ref[0, sl, :] = roped.T.astype(BF16)
            mk_ref[:, sl] = (kv[:, sl] + kr).astype(BF16)
        mv_ref[...] = kv[:, HW:].astype(BF16)
        mvt_ref[0] = kv[:, HW:].T.astype(BF16)
        z = s_ref[:, S_F:S_END - LANES] + bf_ref[...]
        lf_ref[...] = jnp.minimum(z, 0.0) - jnp.log(1.0 + jnp.exp(-jnp.abs(z)))

    def row(w):
        return pl.BlockSpec((bt, w), lambda i: (i, 0))

    def full(arr):
        return pl.BlockSpec(arr.shape, lambda i: (0, 0))

    return pl.pallas_call(
        body, name=name, grid=(t // bt,),
        in_specs=[row(S_END), full(g_q), full(g_kv), full(w_uq), full(w_ukv), row(1), full(inv_row), full(b_f)],
        out_specs=[row(HW), row(HW), row(FW), row(LANES), row(Q_RANK), row(KV_RANK),
                   pl.BlockSpec((1, HW, bt), lambda i: (i, 0, 0)), pl.BlockSpec((1, FW, bt), lambda i: (i, 0, 0))],
        out_shape=[jax.ShapeDtypeStruct((t, HW), BF16)] * 2 + [jax.ShapeDtypeStruct((t, FW), BF16), jax.ShapeDtypeStruct((t, LANES), F32),
                   jax.ShapeDtypeStruct((t, Q_RANK), BF16), jax.ShapeDtypeStruct((t, KV_RANK), BF16),
                   jax.ShapeDtypeStruct((t // bt, HW, bt), BF16), jax.ShapeDtypeStruct((t // bt, FW, bt), BF16)],
        compiler_params=_params(("parallel",)),
    )(small, g_q, g_kv, w_uq, w_ukv, pos_col, inv_row, b_f)


def mla_prep_bwd(dmq, dmk, dmv, dlf, small, g_q, g_kv, w_uq, w_ukv, pos_col, inv_row, b_f, *, name, bt=512):
    t = small.shape[0]
    bt = min(bt, t)

    def body(dmq_ref, dmk_ref, dmv_ref, dlf_ref, s_ref, gq_ref, gkv_ref, wq_ref, wkv_ref, pos_ref, inv_ref, bf_ref,
             ds_ref, dq_ref, dkv_ref, dgq_ref, dgkv_ref, db_ref):
        tc, ta, tb = _rope_tables(pos_ref[...], inv_ref[...])
        lane = lax.broadcasted_iota(jnp.int32, (1, LANES), 1)
        dkr = jnp.zeros((bt, LANES), F32)
        for h in range(N_HEADS):
            sl = slice(h * LANES, (h + 1) * LANES)
            dq_ref[:, sl] = _rope_bwd(dmq_ref[:, sl], tc, ta, tb).astype(BF16)
            dkr = dkr + dmk_ref[:, sl]
        dkv_ref[:, :HW] = dmk_ref[...].astype(BF16)
        dkv_ref[:, HW:] = dmv_ref[...].astype(BF16)
        in_rope = (lane >= HEAD_DIM) & (lane < HEAD_DIM + ROPE_DIM)
        ds_ref[:, S_KR:S_F] = jnp.where(in_rope, _rope_bwd(dkr, tc, ta, tb), 0.0).astype(BF16)

        def norm_bwd(raw, g_ref, dn, dg_ref):
            r = lax.rsqrt(jnp.mean(raw * raw, axis=-1, keepdims=True) + EPS)
            u = dn * g_ref[...]
            dot = jnp.mean(u * raw, axis=-1, keepdims=True)
            dg_ref[...] += jnp.sum(dn * (raw * r), axis=0, keepdims=True)
            return r * u - raw * (r * r * r * dot)

        @pl.when(pl.program_id(0) == 0)
        def _():
            dgq_ref[...] = jnp.zeros_like(dgq_ref)
            dgkv_ref[...] = jnp.zeros_like(dgkv_ref)
            db_ref[...] = jnp.zeros_like(db_ref)

        dcqn = lax.dot_general(dq_ref[...], wq_ref[...], NT, preferred_element_type=F32)
        ds_ref[:, S_CQ:S_CKV] = norm_bwd(s_ref[:, S_CQ:S_CKV], gq_ref, dcqn, dgq_ref).astype(BF16)
        dckvn = lax.dot_general(dkv_ref[...], wkv_ref[...], NT, preferred_element_type=F32)
        ds_ref[:, S_CKV:S_KR] = norm_bwd(s_ref[:, S_CKV:S_KR], gkv_ref, dckvn, dgkv_ref).astype(BF16)
        z = s_ref[:, S_F:S_END - LANES] + bf_ref[...]
        dz = jnp.where(lane < N_HEADS, dlf_ref[...] / (1.0 + jnp.exp(z)), 0.0)
        db_ref[...] += jnp.sum(dz, axis=0, keepdims=True)
        ds_ref[:, S_F:S_END - LANES] = dz.astype(BF16)
        ds_ref[:, S_END - LANES:] = jnp.zeros((bt, LANES), BF16)

    def row(w):
        return pl.BlockSpec((bt, w), lambda i: (i, 0))

    def full(arr):
        return pl.BlockSpec(arr.shape, lambda i: (0, 0))

    def vec(w):
        return pl.BlockSpec((1, w), lambda i: (0, 0))

    return pl.pallas_call(
        body, name=name, grid=(t // bt,),
        in_specs=[row(HW), row(HW), row(FW), row(LANES), row(S_END), full(g_q), full(g_kv), full(w_uq), full(w_ukv),
                  row(1), full(inv_row), full(b_f)],
        out_specs=[row(S_END), row(HW), row(HW + FW), vec(Q_RANK), vec(KV_RANK), vec(LANES)],
        out_shape=[jax.ShapeDtypeStruct((t, S_END), BF16), jax.ShapeDtypeStruct((t, HW), BF16),
                   jax.ShapeDtypeStruct((t, HW + FW), BF16), jax.ShapeDtypeStruct((1, Q_RANK), F32),
                   jax.ShapeDtypeStruct((1, KV_RANK), F32), jax.ShapeDtypeStruct((1, LANES), F32)],
        compiler_params=_params(("arbitrary",)),
    )(dmq, dmk, dmv, dlf, small, g_q, g_kv, w_uq, w_ukv, pos_col, inv_row, b_f)


class Side:
    def __init__(self, ins, out_shapes, n_sems, first, last, mid=None):
        self.ins, self.out_shapes, self.n_sems = list(ins), list(out_shapes), n_sems
        self.first, self.mid, self.last = first, mid, last

    def specs(self):
        return [ANY] * len(self.ins), [ANY] * len(self.out_shapes)

    def sems(self):
        return [pltpu.SemaphoreType.DMA((self.n_sems,)), pltpu.SemaphoreType.DMA((self.n_sems,))]


def _lane():
    return lax.broadcasted_iota(jnp.int32, (1, LANES), 1)


def _halves(x):
    zero = jnp.zeros_like(x)
    return [jnp.where(_lane() < HEAD_DIM, x, zero), jnp.where(_lane() >= HEAD_DIM, x, zero)]


def _groups(x):
    return [x[:, :LANES], x[:, LANES:]]


def _pick_row(tile, h):
    row = lax.broadcasted_iota(jnp.int32, (tile.shape[0], 1), 0)
    return jnp.sum(jnp.where(row == h, tile, 0.0), axis=0, keepdims=True)


def _pick_lane(tile, h):
    return jnp.sum(jnp.where(_lane() == h, tile, 0.0), axis=1, keepdims=True)


def _row_halves(x):
    row = lax.broadcasted_iota(jnp.int32, (LANES, 1), 0)
    zero = jnp.zeros_like(x)
    return [jnp.where(row < HEAD_DIM, x, zero), jnp.where(row >= HEAD_DIM, x, zero)]


def _below_diagonal(s):
    r = lax.broadcasted_iota(jnp.int32, s.shape, 0)
    c = lax.broadcasted_iota(jnp.int32, s.shape, 1)
    return jnp.where(c <= r, s, -jnp.inf)


def _above_diagonal(s):
    r = lax.broadcasted_iota(jnp.int32, s.shape, 0)
    c = lax.broadcasted_iota(jnp.int32, s.shape, 1)
    return jnp.where(r <= c, s, -jnp.inf)


def _split_refs(refs, counts):
    out, at = [], 0
    for n in counts:
        out.append(refs[at:at + n])
        at += n
    return out


def flash_fwd(qt_arr, k_arr, vt_arr, f_cum, *, qoff, koff, voff, pair, scale, name, blk=512, side=None):
    t = k_arr.shape[0]
    tblk = qt_arr.shape[2]
    blk = max(min(blk, t), tblk)
    sub = blk // tblk
    nb = t // blk
    steps = PAIRS * nb
    w = LANES if pair else 2 * LANES
    has_bias = f_cum is not None
    ins = [qt_arr, k_arr, vt_arr] + ([f_cum] if has_bias else [])

    def wide(ref, first):
        parts = [ref[first + u] for u in range(sub)]
        return parts[0] if sub == 1 else jnp.concatenate(parts, axis=1)
    s_ins, s_outs = (side.ins, side.out_shapes) if side else ([], [])

    def body(*refs):
        main, si, outs, so, sems = _split_refs(refs, [len(ins), len(s_ins), 2, len(s_outs), 2 if side else 0])
        qt_ref, k_ref, vt_ref = main[:3]
        f_ref = main[3] if has_bias else None
        o_ref, st_ref = outs
        g, i = pl.program_id(0), pl.program_id(1)
        step_id = g * nb + i
        if side:
            @pl.when(step_id == 0)
            def _():
                side.first(si, so, *sems)

            if side.mid is not None:
                @pl.when(step_id == (3 * steps) // 4)
                def _():
                    side.mid(si, so, *sems)

        qt = (wide(qt_ref, 0).astype(F32) * (scale * LOG2E)).astype(BF16)
        qts = _row_halves(qt) if pair else [qt[:LANES], qt[LANES:]]

        def k_of(kk, n):
            return kk if pair else kk[:, n * LANES:(n + 1) * LANES]

        def with_ones(vt_rows):
            return jnp.concatenate([vt_rows, jnp.ones((ACC_ROWS - HEAD_DIM, vt_rows.shape[1]), BF16)], axis=0)

        def step(j, carry, diagonal):
            rows = pl.ds(pl.multiple_of(j * blk, blk), blk)
            kk = k_ref[rows, :]
            vt = wide(vt_ref, sub * j)
            out = []
            for n in range(2):
                m, acc = carry[n]
                s = jnp.dot(k_of(kk, n), qts[n], preferred_element_type=F32)
                if has_bias:
                    s = s - LOG2E * _pick_lane(f_ref[rows, :], 2 * g + n)
                if diagonal:
                    s = _above_diagonal(s)
                m_new = jnp.maximum(m, jnp.max(s, axis=0, keepdims=True))
                p = jnp.exp2(s - m_new).astype(BF16)
                out.append((m_new, jnp.exp2(m - m_new) * acc
                            + jnp.dot(with_ones(vt[n * HEAD_DIM:(n + 1) * HEAD_DIM]), p, preferred_element_type=F32)))
            return tuple(out)

        def diagonal_in_halves(carry):
            h = tblk
            halves = [pl.ds(pl.multiple_of(i * blk, blk), h), pl.ds(pl.multiple_of(i * blk + h, h), h)]
            k_top, k_bot = k_ref[halves[0], :], k_ref[halves[1], :]
            vt_top, vt_bot = vt_ref[sub * i], vt_ref[sub * i + 1]
            out = []
            for n in range(2):
                m, acc = carry[n]
                heads = slice(n * HEAD_DIM, (n + 1) * HEAD_DIM)
                s_top = jnp.dot(k_of(k_top, n), qts[n], preferred_element_type=F32)
                s_bot = jnp.dot(k_of(k_bot, n), qts[n][:, h:], preferred_element_type=F32)
                if has_bias:
                    s_top = s_top - LOG2E * _pick_lane(f_ref[halves[0], :], 2 * g + n)
                    s_bot = s_bot - LOG2E * _pick_lane(f_ref[halves[1], :], 2 * g + n)
                s_top, s_bot = _above_diagonal(s_top), _above_diagonal(s_bot)
                m_top = jnp.maximum(m, jnp.max(s_top, axis=0, keepdims=True))
                m_new = jnp.concatenate([m_top[:, :h], jnp.maximum(m_top[:, h:], jnp.max(s_bot, axis=0, keepdims=True))], axis=1)
                p_top = jnp.exp2(s_top - m_new).astype(BF16)
                p_bot = jnp.exp2(s_bot - m_new[:, h:]).astype(BF16)
                late = jnp.concatenate([jnp.zeros((ACC_ROWS, h), F32),
                                        jnp.dot(with_ones(vt_bot[heads]), p_bot, preferred_element_type=F32)], axis=1)
                out.append((m_new, jnp.exp2(m - m_new) * acc
                            + jnp.dot(with_ones(vt_top[heads]), p_top, preferred_element_type=F32) + late))
            return tuple(out)

        init = tuple((jnp.full((1, blk), -jnp.inf, F32), jnp.zeros((ACC_ROWS, blk), F32)) for _ in range(2))
        carry = lax.fori_loop(0, i, lambda j, c: step(j, c, False), init)
        (ma, acca), (mb, accb) = diagonal_in_halves(carry) if sub == 2 else step(i, carry, True)
        la, lb = acca[HEAD_DIM:HEAD_DIM + 1], accb[HEAD_DIM:HEAD_DIM + 1]
        o_ref[...] = jnp.concatenate([acca[:HEAD_DIM] / la, accb[:HEAD_DIM] / lb], axis=0).T
        row = lax.broadcasted_iota(jnp.int32, (LANES, 1), 0)
        st_ref[0] = jnp.where(row == 0, ma + jnp.log2(la), jnp.where(row == 1, mb + jnp.log2(lb), 0.0)).T
        if side:
            @pl.when(step_id == steps - 1)
            def _():
                side.last(si, so, *sems)

    in_specs = [pl.BlockSpec((sub, w, tblk), lambda g, i: (i, qoff + g, 0)), pl.BlockSpec((t, w), lambda g, i: (0, koff + g)),
                pl.BlockSpec((t // tblk, LANES, tblk), lambda g, i: (0, voff + g, 0))]
    if has_bias:
        in_specs.append(pl.BlockSpec((t, LANES), lambda g, i: (0, 0)))
    s_in_specs, s_out_specs = side.specs() if side else ([], [])
    return pl.pallas_call(
        body, name=name, grid=(PAIRS, nb), in_specs=in_specs + s_in_specs,
        out_specs=[pl.BlockSpec((blk, LANES), lambda g, i: (i, g)), pl.BlockSpec((1, blk, LANES), lambda g, i: (g, i, 0))]
        + s_out_specs,
        out_shape=[jax.ShapeDtypeStruct((t, PAIRS * LANES), F32), jax.ShapeDtypeStruct((PAIRS, t, LANES), F32)] + list(s_outs),
        scratch_shapes=side.sems() if side else [],
        compiler_params=_params(("arbitrary", "arbitrary")),
    )(*ins, *s_ins)


def mix_norm(fo, mo, g_fo, g_mo, *, name, bt=512):
    t, d = fo.shape
    bt = min(bt, t)

    def body(fo_ref, mo_ref, gf_ref, gm_ref, o_ref):
        for n, (x_ref, g_ref) in enumerate(((fo_ref, gf_ref), (mo_ref, gm_ref))):
            xv = x_ref[...]
            r = lax.rsqrt(jnp.mean(xv * xv, axis=-1, keepdims=True) + EPS)
            o_ref[:, n * d:(n + 1) * d] = (xv * r * g_ref[...]).astype(BF16)

    row = pl.BlockSpec((bt, d), lambda i: (i, 0))
    vec = pl.BlockSpec((1, d), lambda i: (0, 0))
    return pl.pallas_call(
        body, name=name, grid=(t // bt,), in_specs=[row, row, vec, vec],
        out_specs=pl.BlockSpec((bt, 2 * d), lambda i: (i, 0)),
        out_shape=jax.ShapeDtypeStruct((t, 2 * d), BF16),
        compiler_params=_params(("parallel",)),
    )(fo, mo, g_fo, g_mo)


def mix_norm_bwd(dx1b, w_o, fo, mo, g_fo, g_mo, st_f, st_m, *, name, bt=512, side=None):
    t, d = fo.shape
    bt = min(bt, t)

    def body(dx_in_ref, w_ref, fo_ref, mo_ref, gf_ref, gm_ref, sf_ref, sm_ref, dfo_ref, dmo_ref, dgf_ref, dgm_ref,
             sfo_ref, smo_ref, dfot_ref, dmot_ref):
        @pl.when(pl.program_id(0) == 0)
        def _():
            dgf_ref[...] = jnp.zeros_like(dgf_ref)
            dgm_ref[...] = jnp.zeros_like(dgm_ref)

        dmixed = lax.dot_general(dx_in_ref[...], w_ref[...], NT, preferred_element_type=F32)
        groups = ((fo_ref, gf_ref, dfo_ref, dgf_ref, sf_ref, sfo_ref, dfot_ref),
                  (mo_ref, gm_ref, dmo_ref, dgm_ref, sm_ref, smo_ref, dmot_ref))
        for n, (x_ref, g_ref, dx_ref, dg_ref, st_ref, sto_ref, dxt_ref) in enumerate(groups):
            xv = x_ref[...]
            dhv = dmixed[:, n * d:(n + 1) * d]
            r = lax.rsqrt(jnp.mean(xv * xv, axis=-1, keepdims=True) + EPS)
            u = dhv * g_ref[...]
            dxf = r * u - xv * (r * r * r * jnp.mean(u * xv, axis=-1, keepdims=True))
            dxb = dxf.astype(BF16)
            dx_ref[...] = dxb
            dxt_ref[0] = dxf.T.astype(BF16)
            dg_ref[...] += jnp.sum(dhv * (xv * r), axis=0, keepdims=True)
            prod = xv * dxb.astype(F32)
            for g in range(PAIRS):
                grp = prod[:, g * LANES:(g + 1) * LANES]
                da = jnp.sum(jnp.where(_lane() < HEAD_DIM, grp, 0.0), axis=1, keepdims=True)
                db = jnp.sum(jnp.where(_lane() >= HEAD_DIM, grp, 0.0), axis=1, keepdims=True)
                sto_ref[g] = jnp.where(_lane() == 2, da, jnp.where(_lane() == 3, db, st_ref[g]))

    row = pl.BlockSpec((bt, d), lambda i: (i, 0))
    vec = pl.BlockSpec((1, d), lambda i: (0, 0))
    stat = pl.BlockSpec((PAIRS, bt, LANES), lambda i: (0, i, 0))
    return gridded(
        body, name=name, grid=(t // bt,),
        in_specs=[pl.BlockSpec((bt, dx1b.shape[1]), lambda i: (i, 0)), pl.BlockSpec(w_o.shape, lambda i: (0, 0)),
                  row, row, vec, vec, stat, stat],
        out_specs=[row, row, vec, vec, stat, stat] + [pl.BlockSpec((1, d, bt), lambda i: (i, 0, 0))] * 2,
        out_shape=[jax.ShapeDtypeStruct((t, d), BF16)] * 2 + [jax.ShapeDtypeStruct((1, d), F32)] * 2
        + [jax.ShapeDtypeStruct(st_f.shape, F32)] * 2 + [jax.ShapeDtypeStruct((t // bt, d, bt), BF16)] * 2,
        ins=[dx1b, w_o, fo, mo, g_fo, g_mo, st_f, st_m], semantics=("arbitrary",), side=side)


def flash_bwd(q_arr, qt_arr, k_arr, v_arr, do_arr, dot_arr, st, f_blocks, *, qoff, koff, voff, pair, scale, name, qblk=1024,
              side=None):
    t = q_arr.shape[0]
    blk = qt_arr.shape[2]
    qblk = max(min(qblk, t), blk)
    sub = qblk // blk
    nb, nbq = t // blk, t // qblk
    w = LANES if pair else 2 * LANES
    hw = w // 2
    has_bias = f_blocks is not None
    split = _halves if pair else _groups
    ins = [q_arr, qt_arr, k_arr, v_arr, do_arr, dot_arr, st] + ([f_blocks] if has_bias else [])
    n_out = 4 if has_bias else 3
    s_ins, s_outs = (side.ins, side.out_shapes) if side else ([], [])

    def wide(ref, first, count):
        parts = [ref[first + u] for u in range(count)]
        return parts[0] if count == 1 else jnp.concatenate(parts, axis=1)

    def body(*refs):
        main, si, outs, so, sems = _split_refs(refs, [len(ins), len(s_ins), n_out, len(s_outs), 2 if side else 0])
        q_ref, qt_ref, k_ref, v_ref, do_ref, dot_ref, st_ref = main[:7]
        dq_ref, dk_ref, dv_ref = outs[:3]
        g, j = pl.program_id(0), pl.program_id(1)
        step_id = g * nb + j
        if side:
            @pl.when(step_id == 0)
            def _():
                side.first(si, so, *sems)

        @pl.when(j == 0)
        def _():
            dq_ref[...] = jnp.zeros_like(dq_ref)

        kk, vv = k_ref[...], v_ref[...]
        ks = [kk, kk] if pair else _groups(kk)
        if has_bias:
            f_ref, df_ref = main[7], outs[3]
            fk = [LOG2E * _pick_row(f_ref[0], 2 * g + n) for n in range(2)]

            @pl.when(step_id == 0)
            def _():
                df_ref[...] = jnp.zeros_like(df_ref)

        def step(tb, count, carry, diagonal):
            rows = pl.ds(pl.multiple_of(tb * blk, blk), count * blk)
            qs = split((q_ref[rows, :].astype(F32) * (scale * LOG2E)).astype(BF16))
            qt = (wide(qt_ref, tb, count).astype(F32) * (scale * LOG2E)).astype(BF16)
            dos = [h.astype(BF16) for h in _halves(do_ref[rows, :].astype(F32))]
            dot = wide(dot_ref, tb, count)
            stats = st_ref[0, rows, :]
            new, dqs, row_sums = [], [], []
            for n in range(2):
                dkt, dvt, dfk = carry[n]
                s = lax.dot_general(qs[n], ks[n], NT, preferred_element_type=F32)
                if has_bias:
                    s = s - fk[n]
                if diagonal:
                    s = _below_diagonal(s)
                p = jnp.exp2(s - stats[:, n:n + 1])
                dp = lax.dot_general(dos[n], vv, NT, preferred_element_type=F32)
                ds = p * (dp - stats[:, 2 + n:3 + n])
                dsb = ds.astype(BF16)
                dvt = dvt + jnp.dot(dot[n * HEAD_DIM:(n + 1) * HEAD_DIM], p.astype(BF16), preferred_element_type=F32)
                dkt = dkt + jnp.dot(qt[n * hw:(n + 1) * hw], dsb, preferred_element_type=F32)
                dqs.append(jnp.dot(dsb, ks[n], preferred_element_type=F32))
                if has_bias:
                    dfk = dfk - jnp.sum(ds, axis=0, keepdims=True)
                    row_sums.append(jnp.sum(ds, axis=1, keepdims=True))
                new.append((dkt, dvt, dfk))
            dq = (jnp.where(_lane() < HEAD_DIM, dqs[0], dqs[1]) if pair else jnp.concatenate(dqs, axis=1)) * scale
            if has_bias:
                df_ref[rows, :] += jnp.where(_lane() == 2 * g, row_sums[0], jnp.where(_lane() == 2 * g + 1, row_sums[1], 0.0))
            dq_ref[rows, :] += dq
            return tuple(new)

        init = tuple((jnp.zeros((hw, blk), F32), jnp.zeros((HEAD_DIM, blk), F32), jnp.zeros((1, blk), F32)) for _ in range(2))
        carry = step(j, 1, init, True)
        whole = j // sub + 1
        carry = lax.fori_loop(j + 1, whole * sub, lambda tb, c: step(tb, 1, c, False), carry)
        (dka, dva, dfa), (dkb, dvb, dfb) = lax.fori_loop(whole, nbq, lambda i, c: step(i * sub, sub, c, False), carry)
        dk_ref[...] = (jnp.concatenate([dka, dkb], axis=0).T * LN2).astype(BF16)
        dv_ref[...] = jnp.concatenate([dva, dvb], axis=0).T.astype(BF16)
        if has_bias:
            row = lax.broadcasted_iota(jnp.int32, (LANES, 1), 0)
            by_head = jnp.where(row == 2 * g, dfa, jnp.where(row == 2 * g + 1, dfb, 0.0))
            df_ref[pl.ds(pl.multiple_of(j * blk, blk), blk), :] += by_head.T
        if side:
            @pl.when(step_id == PAIRS * nb - 1)
            def _():
                side.last(si, so, *sems)

    in_specs = [pl.BlockSpec((t, w), lambda g, j: (0, qoff + g)), pl.BlockSpec((nb, w, blk), lambda g, j: (0, qoff + g, 0)),
                pl.BlockSpec((blk, w), lambda g, j: (j, koff + g)), pl.BlockSpec((blk, LANES), lambda g, j: (j, voff + g)),
                pl.BlockSpec((t, LANES), lambda g, j: (0, g)), pl.BlockSpec((nb, LANES, blk), lambda g, j: (0, g, 0)),
                pl.BlockSpec((1, t, LANES), lambda g, j: (g, 0, 0))]
    out_specs = [pl.BlockSpec((t, w), lambda g, j: (0, g)), pl.BlockSpec((blk, w), lambda g, j: (j, g)),
                 pl.BlockSpec((blk, LANES), lambda g, j: (j, g))]
    out_shape = [jax.ShapeDtypeStruct((t, PAIRS * w), F32), jax.ShapeDtypeStruct((t, PAIRS * w), BF16),
                 jax.ShapeDtypeStruct((t, PAIRS * LANES), BF16)]
    if has_bias:
        in_specs.append(pl.BlockSpec((1, N_HEADS, blk), lambda g, j: (j, 0, 0)))
        out_specs.append(pl.BlockSpec((t, LANES), lambda g, j: (0, 0)))
        out_shape.append(jax.ShapeDtypeStruct((t, LANES), F32))
    s_in_specs, s_out_specs = side.specs() if side else ([], [])
    return pl.pallas_call(
        body, name=name, grid=(PAIRS, nb), in_specs=in_specs + s_in_specs, out_specs=out_specs + s_out_specs,
        out_shape=out_shape + list(s_outs), scratch_shapes=side.sems() if side else [],
        compiler_params=_params(("arbitrary", "arbitrary")),
    )(*ins, *s_ins)


def _adamw_math(w, g, m, v):
    nm = ADAM_B1 * m + (1.0 - ADAM_B1) * g
    nv = ADAM_B2 * v + (1.0 - ADAM_B2) * (g * g)
    m_hat = nm / (1.0 - ADAM_B1 ** ADAM_STEP)
    v_hat = nv / (1.0 - ADAM_B2 ** ADAM_STEP)
    return -ADAM_LR * (m_hat / (jnp.sqrt(v_hat) + ADAM_EPS) + ADAM_WD * w), nm, nv


def adamw_vectors(ws, g_rows, ms, vs, *, name):
    k = len(ws)

    def body(g_ref, *refs):
        w_refs, m_refs, v_refs, outs = _split_refs(refs, [k, k, k, 4 * k])
        for i in range(k):
            g = g_ref[i:i + 1, :w_refs[i].shape[1]]
            outs[4 * i][...] = g
            outs[4 * i + 1][...], outs[4 * i + 2][...], outs[4 * i + 3][...] = _adamw_math(
                w_refs[i][...], g, m_refs[i][...], v_refs[i][...])

    flat = pl.pallas_call(
        body, name=name, out_shape=[jax.ShapeDtypeStruct(w.shape, F32) for w in ws for _ in range(4)],
        compiler_params=_params(),
    )(g_rows, *ws, *ms, *vs)
    return [flat[4 * i:4 * i + 4] for i in range(k)]


def adamw_halves(w, g_mine, g_other, m, v, core, *, name):
    _, k, n = w.shape
    br = _row_block(k // 2)
    nh = k // 2 // br

    def body(c_ref, w_ref, gm_ref, go_ref, m_ref, v_ref, g_out, d_ref, nm_ref, nv_ref):
        gv = jnp.where(pl.program_id(0) == c_ref[0], gm_ref[...], go_ref[...])
        g_out[0] = gv
        d_ref[0], nm_ref[0], nv_ref[0] = _adamw_math(w_ref[0], gv, m_ref[0], v_ref[0])

    full = pl.BlockSpec((1, br, n), lambda hb, i, c: (0, hb * nh + i, 0))
    half = pl.BlockSpec((br, n), lambda hb, i, c: (i, 0))
    return pl.pallas_call(
        body, name=name,
        grid_spec=pltpu.PrefetchScalarGridSpec(num_scalar_prefetch=1, grid=(2, nh), in_specs=[full, half, half, full, full],
                                               out_specs=[full] * 4),
        out_shape=[jax.ShapeDtypeStruct(w.shape, F32)] * 4,
        compiler_params=_params(("parallel", "parallel")),
    )(core, w, g_mine, g_other, m, v)


def adamw_halves_t(wt, mt, vt, gt_mine, gt_other, core, *, name, bc=128):
    n, _, k = wt.shape
    nh = k // 2 // bc

    def body(c_ref, w_ref, m_ref, v_ref, gm_ref, go_ref, g_out, d_ref, nm_ref, nv_ref):
        gv = jnp.where(pl.program_id(0) == c_ref[0], gm_ref[...], go_ref[...])
        g_out[:, 0, :] = gv
        d_ref[:, 0, :], nm_ref[:, 0, :], nv_ref[:, 0, :] = _adamw_math(w_ref[:, 0, :], gv, m_ref[:, 0, :], v_ref[:, 0, :])

    full = pl.BlockSpec((n, 1, bc), lambda hb, i, c: (0, 0, hb * nh + i))
    half = pl.BlockSpec((n, bc), lambda hb, i, c: (0, i))
    return pl.pallas_call(
        body, name=name,
        grid_spec=pltpu.PrefetchScalarGridSpec(num_scalar_prefetch=1, grid=(2, nh), in_specs=[full, full, full, half, half],
                                               out_specs=[full] * 4),
        out_shape=[jax.ShapeDtypeStruct(wt.shape, F32)] * 4,
        compiler_params=_params(("parallel", "parallel")),
    )(core, wt, mt, vt, gt_mine, gt_other)


def add_pair(dw, recv, core, *, name):
    n4, k, n = dw.shape
    half = (1, k // 2, n) if split_axis(k) == 0 else (1, k, n // 2)
    mine = (lambda q, c: (q, c[0], 0)) if split_axis(k) == 0 else (lambda q, c: (q, 0, c[0]))

    def body(c_ref, a_ref, b_ref, o_ref):
        o_ref[...] = (a_ref[...] + b_ref[...].astype(F32)).astype(BF16)

    return pl.pallas_call(
        body, name=name,
        grid_spec=pltpu.PrefetchScalarGridSpec(
            num_scalar_prefetch=1, grid=(n4,),
            in_specs=[pl.BlockSpec(half, mine), pl.BlockSpec(half, lambda q, c: (q, 0, 0))],
            out_specs=pl.BlockSpec(half, lambda q, c: (q, 0, 0))),
        out_shape=jax.ShapeDtypeStruct((n4,) + half[1:], BF16),
        compiler_params=_params(("parallel",)),
    )(core, dw, recv)


def sum_chips(parts, *, name):
    n4, r, n = parts.shape
    if r % 16 == 0:
        br, bc = _row_block(r), n
    else:
        br, bc = r, LANES

    def body(p_ref, o_ref):
        acc = p_ref[0].astype(F32)
        for q in range(1, n4):
            acc = acc + p_ref[q].astype(F32)
        o_ref[...] = acc

    return pl.pallas_call(
        body, name=name, grid=(r // br, n // bc),
        in_specs=[pl.BlockSpec((n4, br, bc), lambda i, j: (0, i, j))], out_specs=pl.BlockSpec((br, bc), lambda i, j: (i, j)),
        out_shape=jax.ShapeDtypeStruct((r, n), F32),
        compiler_params=_params(("parallel", "parallel")),
    )(parts)


ANY = pl.BlockSpec(memory_space=pl.ANY)


def _place():
    x, y, c = lax.axis_index("x"), lax.axis_index("y"), lax.axis_index("c")
    chips = [(1 - x, y), (x, 1 - y), (1 - x, 1 - y)]
    return x, y, c, chips


def _copy(src, dst, send_sems, recv_sems, k, to):
    return pltpu.make_async_remote_copy(src_ref=src, dst_ref=dst, send_sem=send_sems.at[k], recv_sem=recv_sems.at[k],
                                        device_id=to, device_id_type=MESH)


def split_axis(rows):
    return 0 if rows % 32 == 0 else 1


def _half(ref, lead, hf):
    rows, cols = ref.shape[-2:]
    if split_axis(rows) == 0:
        at = (pl.ds(hf * (rows // 2), rows // 2), slice(None))
    else:
        at = (slice(None), pl.ds(hf * (cols // 2), cols // 2))
    return ref.at[at] if lead is None else ref.at[(lead,) + at]


def _gather_first(srcs, dsts, ssems, rsems):
    x, y, c, chips = _place()
    for ti, (s, d) in enumerate(zip(srcs, dsts)):
        for j, (cx, cy) in enumerate(chips):
            _copy(_half(s, None, c), _half(d, 2 * x + y, c), ssems, rsems, 3 * ti + j, (cx, cy, c)).start()


def _gather_mid(srcs, dsts, ssems, rsems):
    x, y, c, chips = _place()
    n1 = 3 * len(srcs)
    for ti, d in enumerate(dsts):
        for j, (cx, cy) in enumerate(chips):
            landed = _half(d, 2 * cx + cy, c)
            _copy(landed, landed, ssems, rsems, 3 * ti + j, (cx, cy, c)).wait_recv()
            _copy(landed, landed, ssems, rsems, n1 + 3 * ti + j, (x, y, 1 - c)).start()


def _gather_last(srcs, dsts, ssems, rsems):
    x, y, c, chips = _place()
    n1 = 3 * len(srcs)
    for ti, (s, d) in enumerate(zip(srcs, dsts)):
        for j, (cx, cy) in enumerate(chips):
            other = _half(d, 2 * cx + cy, 1 - c)
            _copy(other, other, ssems, rsems, n1 + 3 * ti + j, (x, y, 1 - c)).wait_recv()
        for j, (cx, cy) in enumerate(chips):
            mine = _half(s, None, c)
            _copy(mine, mine, ssems, rsems, 3 * ti + j, (cx, cy, c)).wait_send()
            _copy(mine, mine, ssems, rsems, n1 + 3 * ti + j, (x, y, 1 - c)).wait_send()


def gather_side(shards):
    return Side(shards, [jax.ShapeDtypeStruct((N_CHIPS,) + s.shape, s.dtype) for s in shards], 6 * len(shards),
                _gather_first, _gather_last, _gather_mid)


def _scatter_first(srcs, dsts, ssems, rsems):
    x, y, c, chips = _place()
    for ti, (s, d) in enumerate(zip(srcs, dsts)):
        for j, (cx, cy) in enumerate(chips):
            _copy(s.at[2 * cx + cy], d.at[2 * x + y], ssems, rsems, 3 * ti + j, (cx, cy, c)).start()


def _scatter_last(srcs, dsts, ssems, rsems):
    x, y, c, chips = _place()
    for ti, (s, d) in enumerate(zip(srcs, dsts)):
        for j, (cx, cy) in enumerate(chips):
            _copy(s.at[2 * cx + cy], d.at[2 * cx + cy], ssems, rsems, 3 * ti + j, (cx, cy, c)).wait_recv()
        for j, (cx, cy) in enumerate(chips):
            _copy(s.at[2 * cx + cy], d.at[2 * cx + cy], ssems, rsems, 3 * ti + j, (cx, cy, c)).wait_send()


def scatter_side(parts):
    return Side(parts, [jax.ShapeDtypeStruct(p.shape, p.dtype) for p in parts], 3 * len(parts), _scatter_first, _scatter_last)


def run_side(side, *, name):
    n_in, n_out = len(side.ins), len(side.out_shapes)

    def body(*refs):
        si, so, sems = _split_refs(refs, [n_in, n_out, 2])
        side.first(si, so, *sems)
        if side.mid is not None:
            side.mid(si, so, *sems)
        side.last(si, so, *sems)

    in_specs, out_specs = side.specs()
    return pl.pallas_call(body, name=name, in_specs=in_specs, out_specs=out_specs, out_shape=side.out_shapes,
                          scratch_shapes=side.sems())(*side.ins)


def _swap_first(srcs, dsts, ssems, rsems):
    x, y, c, _ = _place()
    for k, (s, d) in enumerate(zip(srcs, dsts)):
        _copy(s, d, ssems, rsems, k, (x, y, 1 - c)).start()


def _swap_last(srcs, dsts, ssems, rsems):
    x, y, c, _ = _place()
    for k, (s, d) in enumerate(zip(srcs, dsts)):
        _copy(s, d, ssems, rsems, k, (x, y, 1 - c)).wait()


def swap_side(xs):
    return Side(xs, [jax.ShapeDtypeStruct(a.shape, a.dtype) for a in xs], len(xs), _swap_first, _swap_last)


def _swap_halves(srcs, dsts, ssems, rsems):
    x, y, c, _ = _place()
    for k, (s, d) in enumerate(zip(srcs, dsts)):
        hk = s.shape[1] // 2
        yield _copy(s.at[:, pl.ds((1 - c) * hk, hk), :], d, ssems, rsems, k, (x, y, 1 - c))


def _swap_halves_first(srcs, dsts, ssems, rsems):
    for cp in _swap_halves(srcs, dsts, ssems, rsems):
        cp.start()


def _swap_halves_last(srcs, dsts, ssems, rsems):
    for cp in _swap_halves(srcs, dsts, ssems, rsems):
        cp.wait()


def swap_halves_side(xs):
    return Side(xs, [jax.ShapeDtypeStruct((a.shape[0], a.shape[1] // 2, a.shape[2]), a.dtype) for a in xs], len(xs),
                _swap_halves_first, _swap_halves_last)


def allreduce_small(s):
    n_dev = 8

    def body(s_ref, out_ref, buf, send_sems, recv_sems):
        x, y, c, _ = _place()
        me = 4 * x + 2 * y + c
        buf[me] = s_ref[...]
        sends = []
        for k in range(1, n_dev):
            px = 1 - x if k & 4 else x
            py = 1 - y if k & 2 else y
            pc = 1 - c if k & 1 else c
            cp = _copy(s_ref, buf.at[me], send_sems, recv_sems, k - 1, (px, py, pc))
            cp.start()
            sends.append((cp, 4 * px + 2 * py + pc))
        for k, (cp, peer) in enumerate(sends):
            _copy(s_ref, buf.at[peer], send_sems, recv_sems, k, (x, y, c)).wait_recv()
        for cp, _ in sends:
            cp.wait_send()
        acc = buf[0]
        for d in range(1, n_dev):
            acc = acc + buf[d]
        out_ref[...] = acc

    vm = pl.BlockSpec(memory_space=pltpu.VMEM)
    return pl.pallas_call(
        body, name="allreduce_small", in_specs=[vm], out_specs=vm,
        out_shape=jax.ShapeDtypeStruct(s.shape, F32),
        scratch_shapes=[pltpu.VMEM((n_dev,) + s.shape, F32), pltpu.SemaphoreType.DMA((n_dev - 1,)),
                        pltpu.SemaphoreType.DMA((n_dev - 1,))],
    )(s)


def join_cols(sm):
    n4, k, n = sm.shape
    return sm.transpose(1, 0, 2).reshape(k, n4 * n)


def split_cols(full):
    k, n = full.shape
    return full.reshape(k, N_CHIPS, n // N_CHIPS).transpose(1, 0, 2)


def _pad_heads(w, width):
    lead, heads = w.shape[:-1], w.shape[-1] // width
    w = w.reshape(lead + (heads, width))
    return jnp.pad(w, [(0, 0)] * len(lead) + [(0, 0), (0, LANES - width)]).reshape(lead + (heads * LANES,))


def _unpad_heads(w, width):
    lead, heads = w.shape[:-1], w.shape[-1] // LANES
    return w.reshape(lead + (heads, LANES))[..., :width].reshape(lead + (heads * width,))


O_F = 3 * FW
O_CQ = O_F + N_HEADS
O_CKV = O_CQ + Q_RANK
O_KR = O_CKV + KV_RANK
O_END = O_KR + ROPE_DIM


def split_w_in_t(w_sm):
    n4, r, d = w_sm.shape
    segments = [(0, O_F, 0, 0), (O_F, N_HEADS, 1, S_F), (O_CQ, Q_RANK, 1, S_CQ), (O_CKV, KV_RANK, 1, S_CKV),
                (O_KR, ROPE_DIM, 1, S_KR + HEAD_DIM)]
    moves = []
    for v0, n, dst, d0 in segments:
        for q in range(n4):
            a, b = max(v0, q * r), min(v0 + n, (q + 1) * r)
            if a < b:
                assert a % 2 == 0 and b % 2 == 0 and (d0 + a - v0) % 2 == 0
                moves.append((q, (a - q * r) // 2, (b - a) // 2, dst, (d0 + a - v0) // 2))

    def body(w_ref, qkv_ref, small_ref, src, dst_qkv, dst_small):
        for q in range(n4):
            src[q] = pltpu.bitcast(w_ref[q], jnp.uint32)
        dst_small[...] = jnp.zeros(dst_small.shape, jnp.uint32)
        for q, a, n, dst, b in moves:
            (dst_qkv, dst_small)[dst][pl.ds(b, n), :] = src[q, pl.ds(a, n), :]
        qkv_ref[...] = pltpu.bitcast(dst_qkv[...], w_sm.dtype)
        small_ref[...] = pltpu.bitcast(dst_small[...], w_sm.dtype)

    return pl.pallas_call(
        body, name="split_w_in",
        out_shape=[jax.ShapeDtypeStruct((O_F, d), w_sm.dtype), jax.ShapeDtypeStruct((S_END, d), w_sm.dtype)],
        scratch_shapes=[pltpu.VMEM((n4, r // 2, d), jnp.uint32), pltpu.VMEM((O_F // 2, d), jnp.uint32),
                        pltpu.VMEM((S_END // 2, d), jnp.uint32)],
        compiler_params=_params(),
    )(w_sm)


def join_w_in_t(d_qkv_t, d_small_t):
    small = len(d_qkv_t)
    segments = [(n, 0, n * FW, FW) for n in range(small)]
    segments += [(small, S_F, O_F, N_HEADS), (small, S_CQ, O_CQ, Q_RANK), (small, S_CKV, O_CKV, KV_RANK),
                 (small, S_KR + HEAD_DIM, O_KR, ROPE_DIM)]
    r, d = O_END // N_CHIPS, d_small_t.shape[1]

    def body(*refs):
        o_ref = refs[-1]
        for src, s0, v0, n in segments:
            for q in range(N_CHIPS):
                a, b = max(v0, q * r), min(v0 + n, (q + 1) * r)
                if a < b:
                    o_ref[q, pl.ds(a - q * r, b - a), :] = refs[src][pl.ds(s0 + a - v0, b - a), :]

    return pl.pallas_call(
        body, name="join_w_in", out_shape=jax.ShapeDtypeStruct((N_CHIPS, r, d), F32), compiler_params=_params(),
    )(*d_qkv_t, d_small_t)


def rope_inputs(pos):
    inv_freq = ROPE_THETA ** (-jnp.arange(0, ROPE_DIM, 2, dtype=F32) / ROPE_DIM)
    inv_row = jnp.concatenate([jnp.zeros((HEAD_DIM,), F32), inv_freq, inv_freq, jnp.zeros((LANES - HEAD_DIM - ROPE_DIM,), F32)])
    return pos.astype(F32)[:, None], inv_row[None, :]


def _pad_lanes(v, n):
    return jnp.pad(v, ((0, 0), (0, n - v.shape[1])))


ATTN_BLK = 512
ATTN_FWD_BLK = 1024
ATTN_BWD_QBLK = 1024
ACC_ROWS = HEAD_DIM + 16


def local_step(xs, pos, tgt, gains, early_weights, late_weights, early_side=None, fwd_sides=(None, None), reduction=None):
    g_attn, b_forget, g_q, g_kv, g_fo, g_mo, g_mlp, g_fin = gains
    t = xs.shape[0]
    blk = min(ATTN_BLK, t)
    fox_scale = 1.0 / (HEAD_DIM ** 0.5)
    mla_scale = 1.0 / ((HEAD_DIM + ROPE_DIM) ** 0.5)

    h1, *gathered = rmsnorm(xs, g_attn, out_dtype=BF16, name="norm_attn", side=early_side)
    w_in_t, w_uq_p, w_ukv = early_weights(gathered)
    w_qkv_t, w_small_t = split_w_in_t(w_in_t)
    kv = w_ukv.reshape(KV_RANK, N_HEADS, 2 * HEAD_DIM)
    w_ukv_p = jnp.concatenate([_pad_heads(kv[:, :, :HEAD_DIM].reshape(KV_RANK, FW), HEAD_DIM),
                               kv[:, :, HEAD_DIM:].reshape(KV_RANK, FW)], axis=1)
    b_f = _pad_lanes(b_forget, LANES)
    pos_col, inv_row = rope_inputs(pos)

    qkv, qkv_t = mm(h1, w_qkv_t, trans_b=True, out_dtypes=[BF16], t_blk=blk, name="proj_qkv", bn=1536)
    small, = mm(h1, w_small_t, trans_b=True, out_dtypes=[F32], name="proj_small")
    mq, mk, mv, lf, cqn, ckvn, mq_t, mv_t = mla_prep(small, g_q, g_kv, w_uq_p, w_ukv_p, pos_col, inv_row, b_f,
                                                     name="mla_prep", bt=blk)
    f_cum = cumsum_rows(lf, reverse=False, name="gate_cumsum")
    f_blocks = f_cum[:, :N_HEADS].reshape(t // blk, blk, N_HEADS).transpose(0, 2, 1)
    fo, st_f, *gathered = flash_fwd(qkv_t, qkv, qkv_t, f_cum, qoff=0, koff=PAIRS, voff=2 * PAIRS, pair=True,
                                    scale=fox_scale, name="fox_fwd", blk=ATTN_FWD_BLK, side=fwd_sides[0])
    mo, st_m, *more = flash_fwd(mq_t, mk, mv_t, None, qoff=0, koff=0, voff=0, pair=False, scale=mla_scale, name="mla_fwd",
                                blk=ATTN_FWD_BLK, side=fwd_sides[1])
    w_o, w_up, w_down = late_weights(gathered + more)
    mixed = mix_norm(fo, mo, g_fo, g_mo, name="norm_mix")

    def inv_rms(v):
        return lax.rsqrt(jnp.mean(v * v, axis=-1, keepdims=True) + EPS)

    def residual_then_norm(acc, res, g):
        xn = acc + res
        return xn, xn * inv_rms(xn) * g

    def norm_bwd(dh, xn, res, g):
        r = inv_rms(xn)
        uu = dh * g
        return (r * uu - xn * (r * r * r * jnp.mean(uu * xn, axis=-1, keepdims=True)) + res,
                jnp.sum(dh * (xn * r), axis=0, keepdims=True))

    def norm_bwd2(dh, xn, res, g):
        dx, dg = norm_bwd(dh, xn, res, g)
        return dx, dx, dg

    def residual_then_loss(acc, res, target, g):
        xn = acc + res
        r = inv_rms(xn)
        xh = xn * r
        e = xh * g - target
        part = 0.5 * jnp.sum(jnp.mean(e * e, axis=-1, keepdims=True), axis=0, keepdims=True)
        dy = e * (1.0 / xn.shape[1])
        uu = dy * g
        dx = r * uu - xn * (r * r * r * jnp.mean(uu * xn, axis=-1, keepdims=True))
        return dx, dx, jnp.sum(dy * xh, axis=0, keepdims=True), part + jnp.zeros_like(g)

    x1, h2 = mm(mixed, w_o, extras=[xs], vecs=[g_mlp], epilogue=residual_then_norm, out_dtypes=[F32, BF16], name="out_proj")

    def relu2(uu):
        r = jnp.maximum(uu.astype(F32), 0.0)
        return (r * r).astype(BF16)

    u, = mm(h2, w_up, out_dtypes=[BF16], name="mlp_up", bn=2048)
    dx2, dx2b, dg_fin, loss_row = mm(u, w_down, a_pro=relu2, extras=[x1, tgt], vecs=[g_fin], epilogue=residual_then_loss,
                                     out_dtypes=[F32, BF16], n_sums=2, name="mlp_down_loss")
    loss = loss_row[:, :1]

    def relu2_grad(acc, uu):
        return (acc * (2.0 * jnp.maximum(uu.astype(F32), 0.0)),)

    du, = mm(dx2b, w_down, trans_b=True, extras=[u], epilogue=relu2_grad, out_dtypes=[BF16], name="mlp_down_bwd", bn=2048)
    dw_down, dw_down_b = (g.reshape(N_CHIPS, -1, w_down.shape[1])
                          for g in mm_tn(u, dx2b, a_pro=relu2, name="dw_down", bf16_copy=True))
    dx1, dx1b, dg_mlp = mm(du, w_up, trans_b=True, extras=[x1, dx2], vecs=[g_mlp], epilogue=norm_bwd2,
                           out_dtypes=[F32, BF16], n_sums=1, name="mlp_up_bwd")
    dw_up, dw_up_b = mm_tn(h2, du, name="dw_up", col_shards=N_CHIPS, bf16_copy=True)

    dw_o, dw_o_b = (g.reshape(N_CHIPS, -1, w_o.shape[1]) for g in mm_tn(mixed, dx1b, name="dw_o", bf16_copy=True))
    late, late_b = (dw_o, dw_up, dw_down), (dw_o_b, dw_up_b, dw_down_b)
    red = reduction
    dfo, dmo, dg_fo, dg_mo, st_f, st_m, dfo_t, dmo_t, *got = mix_norm_bwd(
        dx1b, w_o, fo, mo, g_fo, g_mo, st_f, st_m, name="out_proj_mix_bwd", bt=blk,
        side=red.late_swap(late, late_b) if red else None)
    dfq, dfk, dfv, d_f, *got = flash_bwd(
        qkv, qkv_t, qkv, qkv, dfo, dfo_t, st_f, f_blocks, qoff=0, koff=PAIRS, voff=2 * PAIRS, pair=True,
        scale=fox_scale, name="fox_bwd", qblk=ATTN_BWD_QBLK, side=red.late_scatter(got) if red else None)
    dmq, dmk, dmv, *got = flash_bwd(mq, mq_t, mk, mv, dmo, dmo_t, st_m, None, qoff=0, koff=0, voff=0,
                                    pair=False, scale=mla_scale, name="mla_bwd", qblk=ATTN_BWD_QBLK,
                                    side=red.late_halves(got) if red else None)
    if red:
        red.late_done(got)
    dlf = cumsum_rows(d_f, reverse=True, name="gate_cumsum_bwd")
    dsmall, dq_u, dkv_u, dg_q, dg_kv, db_f = mla_prep_bwd(dmq, dmk, dmv, dlf, small, g_q, g_kv, w_uq_p, w_ukv_p,
                                                          pos_col, inv_row, b_f, name="mla_prep_bwd")
    dw_uq_p = mm_tn(cqn, dq_u, name="dw_uq")
    dw_ukv_p = mm_tn(ckvn, dkv_u, name="dw_ukv")

    def to_bf16(tile):
        return tile.astype(BF16)

    dqkv = [dfq, dfk, dfv]
    dw_in_t = join_w_in_t([mm_tn(part, h1, a_pro=to_bf16, name="dw_" + nm) for part, nm in zip(dqkv, "qkv")],
                          mm_tn(dsmall, h1, name="dw_small"))
    dk_cols = _unpad_heads(dw_ukv_p[:, :HW], HEAD_DIM).reshape(KV_RANK, N_HEADS, HEAD_DIM)
    dv_cols = dw_ukv_p[:, HW:].reshape(KV_RANK, N_HEADS, HEAD_DIM)
    dw_ukv = jnp.concatenate([dk_cols, dv_cols], axis=2).reshape(KV_RANK, N_HEADS * 2 * HEAD_DIM)
    early = (dw_in_t, split_cols(dw_uq_p), split_cols(dw_ukv))
    w_parts = [w_qkv_t[n * FW:(n + 1) * FW] for n in range(3)]
    grad_x, dg_attn, *got = mm(dqkv + [dsmall], w_parts + [w_small_t], a_pro=to_bf16, extras=[xs, dx1], vecs=[g_attn],
                               epilogue=norm_bwd, out_dtypes=[F32], n_sums=1, name="proj_bwd",
                               side=red.early_scatter(early) if red else None)
    if red:
        red.early_done(got)
    d_gains = (dg_attn, db_f[:, :N_HEADS], dg_q, dg_kv, dg_fo, dg_mo, dg_mlp, dg_fin)
    return loss, grad_x, early, late, d_gains


class GradReduction:
    def __init__(self, core_id, chip):
        self.core_id, self.chip = core_id, chip
        self.core = core_id.reshape(1).astype(jnp.int32)
        self.grads, self.pairs, self.halves, self.others = {}, {}, {}, {}

    def _other_halves(self, grads):
        out = []
        for g in grads:
            axis = 1 + split_axis(g.shape[1])
            size = g.shape[axis] // 2
            out.append(lax.dynamic_slice_in_dim(g, (1 - self.core_id) * size, size, axis=axis).astype(BF16))
        return out

    def _add_pairs(self, group, recvs):
        self.pairs[group] = [add_pair(g, r, self.core, name="add_pair_%s_%d" % (group, n))
                             for n, (g, r) in enumerate(zip(self.grads[group], recvs))]
        return scatter_side(self.pairs[group])

    def _chip_sums(self, group, scattered):
        chip = self.chip
        with_mine = [lax.dynamic_update_index_in_dim(s, lax.dynamic_index_in_dim(p, chip, 0, keepdims=True), chip, 0)
                     for s, p in zip(scattered, self.pairs[group])]
        self.halves[group] = [sum_chips(s, name="sum_chips_%s_%d" % (group, n)) for n, s in enumerate(with_mine)]
        return self.halves[group]

    def late_swap(self, grads, bf16_copies):
        self.grads["late"] = list(grads)
        return swap_halves_side(list(bf16_copies))

    def late_scatter(self, recvs):
        return self._add_pairs("late", recvs)

    def late_halves(self, scattered):
        return swap_side(self._chip_sums("late", scattered))

    def late_done(self, others):
        self.others["late"] = list(others)

    def early_scatter(self, grads):
        self.grads["early"] = list(grads)
        return self._add_pairs("early", run_side(swap_side(self._other_halves(grads)), name="swap_early_sends"))

    def early_done(self, scattered):
        self.others["early"] = list(run_side(swap_side(self._chip_sums("early", scattered)), name="swap_early_halves"))

def kernel(x, positions, attn_norm_g, w_in, b_forget, q_norm_g, w_uq, kv_norm_g, w_ukv, fox_out_g, mla_out_g, w_o, mlp_norm_g, w_up, w_down, final_norm_g, loss_target, m_attn_norm_g, m_w_in, m_b_forget, m_q_norm_g, m_w_uq, m_kv_norm_g, m_w_ukv, m_fox_out_g, m_mla_out_g, m_w_o, m_mlp_norm_g, m_w_up, m_w_down, m_final_norm_g, v_attn_norm_g, v_w_in, v_b_forget, v_q_norm_g, v_w_uq, v_kv_norm_g, v_w_ukv, v_fox_out_g, v_mla_out_g, v_w_o, v_mlp_norm_g, v_w_up, v_w_down, v_final_norm_g):
    core_id = lax.axis_index("c")
    core = core_id.reshape(1).astype(jnp.int32)
    chip = 2 * lax.axis_index("x") + lax.axis_index("y")
    big = [w_in, w_uq, w_ukv, w_o, w_up, w_down]
    big_m = [m_w_in, m_w_uq, m_w_ukv, m_w_o, m_w_up, m_w_down]
    big_v = [v_w_in, v_w_uq, v_w_ukv, v_w_o, v_w_up, v_w_down]
    n_early = 3

    def vec(a):
        return a.reshape(1, -1)

    small = [attn_norm_g, b_forget, q_norm_g, kv_norm_g, fox_out_g, mla_out_g, mlp_norm_g, final_norm_g]
    small_m = [m_attn_norm_g, m_b_forget, m_q_norm_g, m_kv_norm_g, m_fox_out_g, m_mla_out_g, m_mlp_norm_g, m_final_norm_g]
    small_v = [v_attn_norm_g, v_b_forget, v_q_norm_g, v_kv_norm_g, v_fox_out_g, v_mla_out_g, v_mlp_norm_g, v_final_norm_g]
    gains = [vec(a) for a in small]

    views = [big[0][0].T, _pad_heads(big[1][0], HEAD_DIM + ROPE_DIM)] + [w[0] for w in big[2:]]
    shards = [v.astype(BF16) for v in views]

    def with_own(gathered, mine):
        return [lax.dynamic_update_index_in_dim(g, s, chip, 0) for g, s in zip(gathered, mine)]

    def early_weights(gathered):
        g_in, g_uq, g_ukv = with_own(gathered, shards[:n_early])
        return g_in, join_cols(g_uq), join_cols(g_ukv)

    def late_weights(gathered):
        g_o, g_up, g_down = with_own(gathered, shards[n_early:])
        return g_o.reshape(-1, g_o.shape[2]), join_cols(g_up), g_down.reshape(-1, g_down.shape[2])

    reduction = GradReduction(core_id, chip)
    loss, grad_x, _, _, d_small = local_step(
        x[0], positions[0], loss_target[0], gains, early_weights, late_weights, gather_side(shards[:n_early]),
        (gather_side(shards[n_early:-1]), gather_side(shards[-1:])), reduction)
    halves = reduction.halves["early"] + reduction.halves["late"]
    others = reduction.others["early"] + reduction.others["late"]

    def rows8(vs):
        return jnp.concatenate([_pad_lanes(vec(a).astype(F32), 1024) for a in vs], axis=0)

    with_loss = [jnp.concatenate([d, loss], axis=1) if n == 1 else d for n, d in enumerate(d_small)]
    g_small8 = allreduce_small(rows8(with_loss))

    outs_big = []
    for n, (w, gm, go, m, v) in enumerate(zip(big, halves, others, big_m, big_v)):
        if n == 0:
            outs = adamw_halves_t(*(jnp.transpose(a, (2, 0, 1)) for a in (w, m, v)), gm, go, core, name="adamw_%d" % n)
            outs_big.append([jnp.transpose(o, (1, 2, 0)) for o in outs])
        else:
            if n == 1:
                gm, go = (_unpad_heads(gh, HEAD_DIM + ROPE_DIM) for gh in (gm, go))
            outs_big.append(adamw_halves(w, gm, go, m, v, core, name="adamw_%d" % n))
    outs_small = adamw_vectors(gains, g_small8, [vec(a) for a in small_m], [vec(a) for a in small_v], name="adamw_small")

    loss_all = g_small8[1, N_HEADS]
    grads, deltas, new_m, new_v = [None] * 14, [None] * 14, [None] * 14, [None] * 14
    big_at = [1, 4, 6, 9, 11, 12]
    small_at = [0, 2, 3, 5, 7, 8, 10, 13]
    for n, at in enumerate(big_at):
        grads[at], deltas[at], new_m[at], new_v[at] = outs_big[n]
    for at, s, outs in zip(small_at, small, outs_small):
        grads[at], deltas[at], new_m[at], new_v[at] = (o.reshape(s.shape) for o in outs)
    return (loss_all, grad_x[None], *grads, *deltas, *new_m, *new_v)
```

```python
import jax
import jax.numpy as jnp
from jax import lax
from jax.experimental import pallas as pl
from jax.experimental.pallas import tpu as pltpu

F32 = jnp.float32
BF16 = jnp.bfloat16
MESH = pl.DeviceIdType.MESH

EPS = 1e-6
ROPE_THETA = 10000.0
N_HEADS = 8
PAIRS = N_HEADS // 2
HEAD_DIM = 64
ROPE_DIM = 32
LANES = 128
Q_RANK = 384
KV_RANK = 256
N_CHIPS = 4
ADAM_LR, ADAM_B1, ADAM_B2, ADAM_EPS, ADAM_WD, ADAM_STEP = 0.001, 0.9, 0.999, 1e-08, 0.01, 10
VMEM_LIMIT = 48 * 1024 * 1024
LOG2E = 1.4426950408889634
LN2 = 0.6931471805599453
NN = (((1,), (0,)), ((), ()))
NT = (((1,), (1,)), ((), ()))
TN = (((0,), (0,)), ((), ()))


def _params(sem=None):
    return pltpu.CompilerParams(dimension_semantics=sem, vmem_limit_bytes=VMEM_LIMIT)


def _fit(block, dim):
    if dim <= block:
        return dim
    return next(b for b in range(block - block % LANES, 0, -LANES) if dim % b == 0)


def _row_block(rows):
    return next(b for b in (256, 128, 64, 32, 16, 8) if rows % b == 0)


def gridded(body, *, name, grid, in_specs, out_specs, out_shape, ins, semantics, side=None):
    if side is None:
        return pl.pallas_call(body, name=name, grid=grid, in_specs=in_specs, out_specs=out_specs, out_shape=out_shape,
                              compiler_params=_params(semantics))(*ins)
    n_in, n_out = len(in_specs), len(out_specs)
    steps = 1
    for extent in grid:
        steps *= extent

    def riding(*refs):
        main_in, s_in, main_out, s_out, sems = _split_refs(refs, [n_in, len(side.ins), n_out, len(side.out_shapes), 2])
        step = 0
        for axis, extent in enumerate(grid):
            step = step * extent + pl.program_id(axis)

        @pl.when(step == 0)
        def _():
            side.first(s_in, s_out, *sems)

        body(*main_in, *main_out)

        @pl.when(step == steps - 1)
        def _():
            if side.mid is not None:
                side.mid(s_in, s_out, *sems)
            side.last(s_in, s_out, *sems)

    s_in_specs, s_out_specs = side.specs()
    return pl.pallas_call(
        riding, name=name, grid=grid, in_specs=list(in_specs) + s_in_specs, out_specs=list(out_specs) + s_out_specs,
        out_shape=list(out_shape) + side.out_shapes, scratch_shapes=side.sems(),
        compiler_params=_params(("arbitrary",) * len(grid)))(*ins, *side.ins)


def rmsnorm(x, g, *, out_dtype, name, bt=512, side=None):
    t, d = x.shape
    bt = min(bt, t)

    def body(x_ref, g_ref, o_ref):
        xv = x_ref[...].astype(F32)
        r = lax.rsqrt(jnp.mean(xv * xv, axis=-1, keepdims=True) + EPS)
        o_ref[...] = (xv * r * g_ref[...]).astype(o_ref.dtype)

    return gridded(
        body, name=name, grid=(t // bt,),
        in_specs=[pl.BlockSpec((bt, d), lambda i: (i, 0)), pl.BlockSpec((1, d), lambda i: (0, 0))],
        out_specs=[pl.BlockSpec((bt, d), lambda i: (i, 0))],
        out_shape=[jax.ShapeDtypeStruct((t, d), out_dtype)],
        ins=[x, g], semantics=("parallel",), side=side)


def mm(a, b, *, trans_b=False, a_pro=None, extras=(), vecs=(), epilogue=None, out_dtypes, n_sums=0, t_blk=None, name,
       bm=1024, bn=1024, side=None):
    a_list = list(a) if isinstance(a, (list, tuple)) else [a]
    b_list = list(b) if isinstance(b, (list, tuple)) else [b]
    m = a_list[0].shape[0]
    n = b_list[0].shape[0] if trans_b else b_list[0].shape[1]
    ks = [x.shape[1] for x in a_list]
    if sum(ks) > 2048:
        bm = bm // 2
    bm, bn = _fit(bm, m), _fit(bn, n)
    assert n_sums == 0 or bn == n
    n_ab, n_ex, n_vec, n_out = len(a_list), len(extras), len(vecs), len(out_dtypes)
    n_t = 0 if t_blk is None else 1

    def body(*refs):
        a_refs, b_refs, ex, vs, outs, t_outs, sums = _split_refs(refs, [n_ab, n_ab, n_ex, n_vec, n_out, n_t, n_sums])
        acc = None
        for a_ref, b_ref in zip(a_refs, b_refs):
            a_tile = a_ref[...] if a_pro is None else a_pro(a_ref[...])
            part = lax.dot_general(a_tile, b_ref[...], NT if trans_b else NN, preferred_element_type=F32)
            acc = part if acc is None else acc + part
        res = epilogue(acc, *[e[...] for e in ex], *[v[...] for v in vs]) if epilogue is not None else (acc,)
        for o, r in zip(outs, res[:n_out]):
            o[...] = r.astype(o.dtype)
        for t_ref in t_outs:
            for u in range(bm // t_blk):
                t_ref[u] = res[0][u * t_blk:(u + 1) * t_blk, :].T.astype(t_ref.dtype)
        if n_sums:
            @pl.when(pl.program_id(0) == 0)
            def _():
                for s_ref in sums:
                    s_ref[...] = jnp.zeros_like(s_ref)

            for s_ref, r in zip(sums, res[n_out:]):
                s_ref[...] += r

    tile = pl.BlockSpec((bm, bn), lambda i, j: (i, j))
    vec = pl.BlockSpec((1, bn), lambda i, j: (0, j))
    t_specs, t_shapes = [], []
    if t_blk is not None:
        t_specs = [pl.BlockSpec((bm // t_blk, bn, t_blk), lambda i, j: (i, j, 0))]
        t_shapes = [jax.ShapeDtypeStruct((m // t_blk, n, t_blk), out_dtypes[0])]
    a_specs = [pl.BlockSpec((bm, k), lambda i, j: (i, 0)) for k in ks]
    b_specs = [pl.BlockSpec((bn, k), lambda i, j: (j, 0)) if trans_b else pl.BlockSpec((k, bn), lambda i, j: (0, j)) for k in ks]
    return gridded(
        body, name=name, grid=(m // bm, n // bn),
        in_specs=a_specs + b_specs + [tile] * n_ex + [vec] * n_vec,
        out_specs=[tile] * n_out + t_specs + [vec] * n_sums,
        out_shape=[jax.ShapeDtypeStruct((m, n), dt) for dt in out_dtypes] + t_shapes + [jax.ShapeDtypeStruct((1, n), F32)] * n_sums,
        ins=[*a_list, *b_list, *extras, *vecs],
        semantics=("arbitrary", "arbitrary") if n_sums else ("parallel", "parallel"), side=side)


def mm_tn(a, b, *, a_pro=None, name, col_shards=1, bf16_copy=False, bk=1024, bn=1024, bt=2048):
    t, k = a.shape
    n = b.shape[1]
    ns = n // col_shards
    bk, bn, bt = _fit(bk, k), _fit(bn, ns), _fit(bt, t)
    per = ns // bn
    last = t // bt - 1

    def body(a_ref, b_ref, o_ref, *copy_ref):
        @pl.when(pl.program_id(2) == 0)
        def _():
            o_ref[...] = jnp.zeros_like(o_ref)

        a_tile = a_ref[...] if a_pro is None else a_pro(a_ref[...])
        o_ref[...] += lax.dot_general(a_tile, b_ref[...], TN, preferred_element_type=F32)
        if bf16_copy:
            @pl.when(pl.program_id(2) == last)
            def _():
                copy_ref[0][...] = o_ref[...].astype(BF16)

    if col_shards == 1:
        out_spec = pl.BlockSpec((bk, bn), lambda i, j, s: (i, j))
        shape = (k, n)
    else:
        out_spec = pl.BlockSpec((None, bk, bn), lambda i, j, s: (j // per, i, j % per))
        shape = (col_shards, k, ns)
    out_specs, out_shape = out_spec, jax.ShapeDtypeStruct(shape, F32)
    if bf16_copy:
        out_specs, out_shape = [out_spec, out_spec], [out_shape, jax.ShapeDtypeStruct(shape, BF16)]
    return pl.pallas_call(
        body, name=name, grid=(k // bk, n // bn, t // bt),
        in_specs=[pl.BlockSpec((bt, bk), lambda i, j, s: (s, i)), pl.BlockSpec((bt, bn), lambda i, j, s: (s, j))],
        out_specs=out_specs, out_shape=out_shape,
        compiler_params=_params(("parallel", "parallel", "arbitrary")),
    )(a, b)


def _split3(x):
    hi = x.astype(BF16)
    r1 = x - hi.astype(F32)
    mid = r1.astype(BF16)
    lo = (r1 - mid.astype(F32)).astype(BF16)
    return hi, mid, lo


def cumsum_rows(x, *, reverse, name, bc=512):
    t, d = x.shape
    bc = min(bc, t)
    nb = t // bc

    def body(x_ref, o_ref, carry):
        @pl.when(pl.program_id(0) == 0)
        def _():
            carry[...] = jnp.zeros_like(carry)

        r = lax.broadcasted_iota(jnp.int32, (bc, bc), 0)
        c = lax.broadcasted_iota(jnp.int32, (bc, bc), 1)
        tri = jnp.where((r <= c) if reverse else (r >= c), 1.0, 0.0).astype(BF16)
        hi, mid, lo = _split3(x_ref[...])
        s = (lax.dot_general(tri, hi, NN, preferred_element_type=F32)
             + lax.dot_general(tri, mid, NN, preferred_element_type=F32)
             + lax.dot_general(tri, lo, NN, preferred_element_type=F32)) + carry[0:1, :]
        o_ref[...] = s
        carry[0:1, :] = s[0:1, :] if reverse else s[bc - 1:bc, :]

    imap = (lambda i: (nb - 1 - i, 0)) if reverse else (lambda i: (i, 0))
    return pl.pallas_call(
        body, name=name, grid=(nb,),
        in_specs=[pl.BlockSpec((bc, d), imap)], out_specs=pl.BlockSpec((bc, d), imap),
        out_shape=jax.ShapeDtypeStruct((t, d), F32),
        scratch_shapes=[pltpu.VMEM((8, d), F32)],
        compiler_params=_params(("arbitrary",)),
    )(x)


def _rope(x, c, a, b):
    return x * c + pltpu.roll(x, LANES - ROPE_DIM // 2, 1) * a + pltpu.roll(x, ROPE_DIM // 2, 1) * b


def _rope_bwd(d, c, a, b):
    return d * c + pltpu.roll(d * a, ROPE_DIM // 2, 1) + pltpu.roll(d * b, LANES - ROPE_DIM // 2, 1)


S_CQ, S_CKV, S_KR, S_F, S_END = 0, Q_RANK, Q_RANK + KV_RANK, Q_RANK + KV_RANK + LANES, 1024
HW = N_HEADS * LANES
FW = N_HEADS * HEAD_DIM


def _rope_tables(pos_col, inv_row):
    ang = pos_col * inv_row
    cos, sin = jnp.cos(ang), jnp.sin(ang)
    lane = lax.broadcasted_iota(jnp.int32, (1, LANES), 1)
    first = (lane >= HEAD_DIM) & (lane < HEAD_DIM + ROPE_DIM // 2)
    second = (lane >= HEAD_DIM + ROPE_DIM // 2) & (lane < HEAD_DIM + ROPE_DIM)
    return (jnp.where(lane < HEAD_DIM, 1.0, jnp.where(first | second, cos, 0.0)), jnp.where(first, -sin, 0.0),
            jnp.where(second, sin, 0.0))


def mla_prep(small, g_q, g_kv, w_uq, w_ukv, pos_col, inv_row, b_f, *, name, bt=512):
    t = small.shape[0]
    bt = min(bt, t)

    def body(s_ref, gq_ref, gkv_ref, wq_ref, wkv_ref, pos_ref, inv_ref, bf_ref,
             mq_ref, mk_ref, mv_ref, lf_ref, cqn_ref, ckvn_ref, mqt_ref, mvt_ref):
        cq = s_ref[:, S_CQ:S_CKV]
        rq = lax.rsqrt(jnp.mean(cq * cq, axis=-1, keepdims=True) + EPS)
        cqn = (cq * rq * gq_ref[...]).astype(BF16)
        ckv = s_ref[:, S_CKV:S_KR]
        rkv = lax.rsqrt(jnp.mean(ckv * ckv, axis=-1, keepdims=True) + EPS)
        ckvn = (ckv * rkv * gkv_ref[...]).astype(BF16)
        cqn_ref[...] = cqn
        ckvn_ref[...] = ckvn
        tc, ta, tb = _rope_tables(pos_ref[...], inv_ref[...])
        q = jnp.dot(cqn, wq_ref[...], preferred_element_type=F32)
        kv = jnp.dot(ckvn, wkv_ref[...], preferred_element_type=F32)
        kr = _rope(s_ref[:, S_KR:S_F], tc, ta, tb)
        for h in range(N_HEADS):
            sl = slice(h * LANES, (h + 1) * LANES)
            roped = _rope(q[:, sl], tc, ta, tb)
            mq_ref[:, sl] = roped.astype(BF16)
            mqt_ref[0, sl, :] = roped.T.astype(BF16)
            mk_ref[:, sl] = (kv[:, sl] + kr).astype(BF16)
        mv_ref[...] = kv[:, HW:].astype(BF16)
        mvt_ref[0] = kv[:, HW:].T.astype(BF16)
        z = s_ref[:, S_F:S_END - LANES] + bf_ref[...]
        lf_ref[...] = jnp.minimum(z, 0.0) - jnp.log(1.0 + jnp.exp(-jnp.abs(z)))

    def row(w):
        return pl.BlockSpec((bt, w), lambda i: (i, 0))

    def full(arr):
        return pl.BlockSpec(arr.shape, lambda i: (0, 0))

    return pl.pallas_call(
        body, name=name, grid=(t // bt,),
        in_specs=[row(S_END), full(g_q), full(g_kv), full(w_uq), full(w_ukv), row(1), full(inv_row), full(b_f)],
        out_specs=[row(HW), row(HW), row(FW), row(LANES), row(Q_RANK), row(KV_RANK),
                   pl.BlockSpec((1, HW, bt), lambda i: (i, 0, 0)), pl.BlockSpec((1, FW, bt), lambda i: (i, 0, 0))],
        out_shape=[jax.ShapeDtypeStruct((t, HW), BF16)] * 2 + [jax.ShapeDtypeStruct((t, FW), BF16), jax.ShapeDtypeStruct((t, LANES), F32),
                   jax.ShapeDtypeStruct((t, Q_RANK), BF16), jax.ShapeDtypeStruct((t, KV_RANK), BF16),
                   jax.ShapeDtypeStruct((t // bt, HW, bt), BF16), jax.ShapeDtypeStruct((t // bt, FW, bt), BF16)],
        compiler_params=_params(("parallel",)),
    )(small, g_q, g_kv, w_uq, w_ukv, pos_col, inv_row, b_f)


def mla_prep_bwd(dmq, dmk, dmv, dlf, small, g_q, g_kv, w_uq, w_ukv, pos_col, inv_row, b_f, *, name, bt=512):
    t = small.shape[0]
    bt = min(bt, t)

    def body(dmq_ref, dmk_ref, dmv_ref, dlf_ref, s_ref, gq_ref, gkv_ref, wq_ref, wkv_ref, pos_ref, inv_ref, bf_ref,
             ds_ref, dq_ref, dkv_ref, dgq_ref, dgkv_ref, db_ref):
        tc, ta, tb = _rope_tables(pos_ref[...], inv_ref[...])
        lane = lax.broadcasted_iota(jnp.int32, (1, LANES), 1)
        dkr = jnp.zeros((bt, LANES), F32)
        for h in range(N_HEADS):
            sl = slice(h * LANES, (h + 1) * LANES)
            dq_ref[:, sl] = _rope_bwd(dmq_ref[:, sl], tc, ta, tb).astype(BF16)
            dkr = dkr + dmk_ref[:, sl]
        dkv_ref[:, :HW] = dmk_ref[...].astype(BF16)
        dkv_ref[:, HW:] = dmv_ref[...].astype(BF16)
        in_rope = (lane >= HEAD_DIM) & (lane < HEAD_DIM + ROPE_DIM)
        ds_ref[:, S_KR:S_F] = jnp.where(in_rope, _rope_bwd(dkr, tc, ta, tb), 0.0).astype(BF16)

        def norm_bwd(raw, g_ref, dn, dg_ref):
            r = lax.rsqrt(jnp.mean(raw * raw, axis=-1, keepdims=True) + EPS)
            u = dn * g_ref[...]
            dot = jnp.mean(u * raw, axis=-1, keepdims=True)
            dg_ref[...] += jnp.sum(dn * (raw * r), axis=0, keepdims=True)
            return r * u - raw * (r * r * r * dot)

        @pl.when(pl.program_id(0) == 0)
        def _():
            dgq_ref[...] = jnp.zeros_like(dgq_ref)
            dgkv_ref[...] = jnp.zeros_like(dgkv_ref)
            db_ref[...] = jnp.zeros_like(db_ref)

        dcqn = lax.dot_general(dq_ref[...], wq_ref[...], NT, preferred_element_type=F32)
        ds_ref[:, S_CQ:S_CKV] = norm_bwd(s_ref[:, S_CQ:S_CKV], gq_ref, dcqn, dgq_ref).astype(BF16)
        dckvn = lax.dot_general(dkv_ref[...], wkv_ref[...], NT, preferred_element_type=F32)
        ds_ref[:, S_CKV:S_KR] = norm_bwd(s_ref[:, S_CKV:S_KR], gkv_ref, dckvn, dgkv_ref).astype(BF16)
        z = s_ref[:, S_F:S_END - LANES] + bf_ref[...]
        dz = jnp.where(lane < N_HEADS, dlf_ref[...] / (1.0 + jnp.exp(z)), 0.0)
        db_ref[...] += jnp.sum(dz, axis=0, keepdims=True)
        ds_ref[:, S_F:S_END - LANES] = dz.astype(BF16)
        ds_ref[:, S_END - LANES:] = jnp.zeros((bt, LANES), BF16)

    def row(w):
        return pl.BlockSpec((bt, w), lambda i: (i, 0))

    def full(arr):
        return pl.BlockSpec(arr.shape, lambda i: (0, 0))

    def vec(w):
        return pl.BlockSpec((1, w), lambda i: (0, 0))

    return pl.pallas_call(
        body, name=name, grid=(t // bt,),
        in_specs=[row(HW), row(HW), row(FW), row(LANES), row(S_END), full(g_q), full(g_kv), full(w_uq), full(w_ukv),
                  row(1), full(inv_row), full(b_f)],
        out_specs=[row(S_END), row(HW), row(HW + FW), vec(Q_RANK), vec(KV_RANK), vec(LANES)],
        out_shape=[jax.ShapeDtypeStruct((t, S_END), BF16), jax.ShapeDtypeStruct((t, HW), BF16),
                   jax.ShapeDtypeStruct((t, HW + FW), BF16), jax.ShapeDtypeStruct((1, Q_RANK), F32),
                   jax.ShapeDtypeStruct((1, KV_RANK), F32), jax.ShapeDtypeStruct((1, LANES), F32)],
        compiler_params=_params(("arbitrary",)),
    )(dmq, dmk, dmv, dlf, small, g_q, g_kv, w_uq, w_ukv, pos_col, inv_row, b_f)


class Side:
    def __init__(self, ins, out_shapes, n_sems, first, last, mid=None):
        self.ins, self.out_shapes, self.n_sems = list(ins), list(out_shapes), n_sems
        self.first, self.mid, self.last = first, mid, last

    def specs(self):
        return [ANY] * len(self.ins), [ANY] * len(self.out_shapes)

    def sems(self):
        return [pltpu.SemaphoreType.DMA((self.n_sems,)), pltpu.SemaphoreType.DMA((self.n_sems,))]


def _lane():
    return lax.broadcasted_iota(jnp.int32, (1, LANES), 1)


def _halves(x):
    zero = jnp.zeros_like(x)
    return [jnp.where(_lane() < HEAD_DIM, x, zero), jnp.where(_lane() >= HEAD_DIM, x, zero)]


def _groups(x):
    return [x[:, :LANES], x[:, LANES:]]


def _pick_row(tile, h):
    row = lax.broadcasted_iota(jnp.int32, (tile.shape[0], 1), 0)
    return jnp.sum(jnp.where(row == h, tile, 0.0), axis=0, keepdims=True)


def _pick_lane(tile, h):
    return jnp.sum(jnp.where(_lane() == h, tile, 0.0), axis=1, keepdims=True)


def _row_halves(x):
    row = lax.broadcasted_iota(jnp.int32, (LANES, 1), 0)
    zero = jnp.zeros_like(x)
    return [jnp.where(row < HEAD_DIM, x, zero), jnp.where(row >= HEAD_DIM, x, zero)]


def _below_diagonal(s):
    r = lax.broadcasted_iota(jnp.int32, s.shape, 0)
    c = lax.broadcasted_iota(jnp.int32, s.shape, 1)
    return jnp.where(c <= r, s, -jnp.inf)


def _above_diagonal(s):
    r = lax.broadcasted_iota(jnp.int32, s.shape, 0)
    c = lax.broadcasted_iota(jnp.int32, s.shape, 1)
    return jnp.where(r <= c, s, -jnp.inf)


def _split_refs(refs, counts):
    out, at = [], 0
    for n in counts:
        out.append(refs[at:at + n])
        at += n
    return out


def flash_fwd(qt_arr, k_arr, vt_arr, f_cum, *, qoff, koff, voff, pair, scale, name, blk=512, side=None):
    t = k_arr.shape[0]
    tblk = qt_arr.shape[2]
    blk = max(min(blk, t), tblk)
    sub = blk // tblk
    nb = t // blk
    steps = PAIRS * nb
    w = LANES if pair else 2 * LANES
    has_bias = f_cum is not None
    ins = [qt_arr, k_arr, vt_arr] + ([f_cum] if has_bias else [])

    def wide(ref, first):
        parts = [ref[first + u] for u in range(sub)]
        return parts[0] if sub == 1 else jnp.concatenate(parts, axis=1)
    s_ins, s_outs = (side.ins, side.out_shapes) if side else ([], [])

    def body(*refs):
        main, si, outs, so, sems = _split_refs(refs, [len(ins), len(s_ins), 2, len(s_outs), 2 if side else 0])
        qt_ref, k_ref, vt_ref = main[:3]
        f_ref = main[3] if has_bias else None
        o_ref, st_ref = outs
        g, i = pl.program_id(0), pl.program_id(1)
        step_id = g * nb + i
        if side:
            @pl.when(step_id == 0)
            def _():
                side.first(si, so, *sems)

            if side.mid is not None:
                @pl.when(step_id == (3 * steps) // 4)
                def _():
                    side.mid(si, so, *sems)

        qt = (wide(qt_ref, 0).astype(F32) * (scale * LOG2E)).astype(BF16)
        qts = _row_halves(qt) if pair else [qt[:LANES], qt[LANES:]]

        def k_of(kk, n):
            return kk if pair else kk[:, n * LANES:(n + 1) * LANES]

        def with_ones(vt_rows):
            return jnp.concatenate([vt_rows, jnp.ones((ACC_ROWS - HEAD_DIM, vt_rows.shape[1]), BF16)], axis=0)

        def step(j, carry, diagonal):
            rows = pl.ds(pl.multiple_of(j * blk, blk), blk)
            kk = k_ref[rows, :]
            vt = wide(vt_ref, sub * j)
            out = []
            for n in range(2):
                m, acc = carry[n]
                s = jnp.dot(k_of(kk, n), qts[n], preferred_element_type=F32)
                if has_bias:
                    s = s - LOG2E * _pick_lane(f_ref[rows, :], 2 * g + n)
                if diagonal:
                    s = _above_diagonal(s)
                m_new = jnp.maximum(m, jnp.max(s, axis=0, keepdims=True))
                p = jnp.exp2(s - m_new).astype(BF16)
                out.append((m_new, jnp.exp2(m - m_new) * acc
                            + jnp.dot(with_ones(vt[n * HEAD_DIM:(n + 1) * HEAD_DIM]), p, preferred_element_type=F32)))
            return tuple(out)

        def diagonal_in_halves(carry):
            h = tblk
            halves = [pl.ds(pl.multiple_of(i * blk, blk), h), pl.ds(pl.multiple_of(i * blk + h, h), h)]
            k_top, k_bot = k_ref[halves[0], :], k_ref[halves[1], :]
            vt_top, vt_bot = vt_ref[sub * i], vt_ref[sub * i + 1]
            out = []
            for n in range(2):
                m, acc = carry[n]
                heads = slice(n * HEAD_DIM, (n + 1) * HEAD_DIM)
                s_top = jnp.dot(k_of(k_top, n), qts[n], preferred_element_type=F32)
                s_bot = jnp.dot(k_of(k_bot, n), qts[n][:, h:], preferred_element_type=F32)
                if has_bias:
                    s_top = s_top - LOG2E * _pick_lane(f_ref[halves[0], :], 2 * g + n)
                    s_bot = s_bot - LOG2E * _pick_lane(f_ref[halves[1], :], 2 * g + n)
                s_top, s_bot = _above_diagonal(s_top), _above_diagonal(s_bot)
                m_top = jnp.maximum(m, jnp.max(s_top, axis=0, keepdims=True))
                m_new = jnp.concatenate([m_top[:, :h], jnp.maximum(m_top[:, h:], jnp.max(s_bot, axis=0, keepdims=True))], axis=1)
                p_top = jnp.exp2(s_top - m_new).astype(BF16)
                p_bot = jnp.exp2(s_bot - m_new[:, h:]).astype(BF16)
                late = jnp.concatenate([jnp.zeros((ACC_ROWS, h), F32),
                                        jnp.dot(with_ones(vt_bot[heads]), p_bot, preferred_element_type=F32)], axis=1)
                out.append((m_new, jnp.exp2(m - m_new) * acc
                            + jnp.dot(with_ones(vt_top[heads]), p_top, preferred_element_type=F32) + late))
            return tuple(out)

        init = tuple((jnp.full((1, blk), -jnp.inf, F32), jnp.zeros((ACC_ROWS, blk), F32)) for _ in range(2))
        carry = lax.fori_loop(0, i, lambda j, c: step(j, c, False), init)
        (ma, acca), (mb, accb) = diagonal_in_halves(carry) if sub == 2 else step(i, carry, True)
        la, lb = acca[HEAD_DIM:HEAD_DIM + 1], accb[HEAD_DIM:HEAD_DIM + 1]
        o_ref[...] = jnp.concatenate([acca[:HEAD_DIM] / la, accb[:HEAD_DIM] / lb], axis=0).T
        row = lax.broadcasted_iota(jnp.int32, (LANES, 1), 0)
        st_ref[0] = jnp.where(row == 0, ma + jnp.log2(la), jnp.where(row == 1, mb + jnp.log2(lb), 0.0)).T
        if side:
            @pl.when(step_id == steps - 1)
            def _():
                side.last(si, so, *sems)

    in_specs = [pl.BlockSpec((sub, w, tblk), lambda g, i: (i, qoff + g, 0)), pl.BlockSpec((t, w), lambda g, i: (0, koff + g)),
                pl.BlockSpec((t // tblk, LANES, tblk), lambda g, i: (0, voff + g, 0))]
    if has_bias:
        in_specs.append(pl.BlockSpec((t, LANES), lambda g, i: (0, 0)))
    s_in_specs, s_out_specs = side.specs() if side else ([], [])
    return pl.pallas_call(
        body, name=name, grid=(PAIRS, nb), in_specs=in_specs + s_in_specs,
        out_specs=[pl.BlockSpec((blk, LANES), lambda g, i: (i, g)), pl.BlockSpec((1, blk, LANES), lambda g, i: (g, i, 0))]
        + s_out_specs,
        out_shape=[jax.ShapeDtypeStruct((t, PAIRS * LANES), F32), jax.ShapeDtypeStruct((PAIRS, t, LANES), F32)] + list(s_outs),
        scratch_shapes=side.sems() if side else [],
        compiler_params=_params(("arbitrary", "arbitrary")),
    )(*ins, *s_ins)


def mix_norm(fo, mo, g_fo, g_mo, *, name, bt=512):
    t, d = fo.shape
    bt = min(bt, t)

    def body(fo_ref, mo_ref, gf_ref, gm_ref, o_ref):
        for n, (x_ref, g_ref) in enumerate(((fo_ref, gf_ref), (mo_ref, gm_ref))):
            xv = x_ref[...]
            r = lax.rsqrt(jnp.mean(xv * xv, axis=-1, keepdims=True) + EPS)
            o_ref[:, n * d:(n + 1) * d] = (xv * r * g_ref[...]).astype(BF16)

    row = pl.BlockSpec((bt, d), lambda i: (i, 0))
    vec = pl.BlockSpec((1, d), lambda i: (0, 0))
    return pl.pallas_call(
        body, name=name, grid=(t // bt,), in_specs=[row, row, vec, vec],
        out_specs=pl.BlockSpec((bt, 2 * d), lambda i: (i, 0)),
        out_shape=jax.ShapeDtypeStruct((t, 2 * d), BF16),
        compiler_params=_params(("parallel",)),
    )(fo, mo, g_fo, g_mo)


def mix_norm_bwd(dx1b, w_o, fo, mo, g_fo, g_mo, st_f, st_m, *, name, bt=512, side=None):
    t, d = fo.shape
    bt = min(bt, t)

    def body(dx_in_ref, w_ref, fo_ref, mo_ref, gf_ref, gm_ref, sf_ref, sm_ref, dfo_ref, dmo_ref, dgf_ref, dgm_ref,
             sfo_ref, smo_ref, dfot_ref, dmot_ref):
        @pl.when(pl.program_id(0) == 0)
        def _():
            dgf_ref[...] = jnp.zeros_like(dgf_ref)
            dgm_ref[...] = jnp.zeros_like(dgm_ref)

        dmixed = lax.dot_general(dx_in_ref[...], w_ref[...], NT, preferred_element_type=F32)
        groups = ((fo_ref, gf_ref, dfo_ref, dgf_ref, sf_ref, sfo_ref, dfot_ref),
                  (mo_ref, gm_ref, dmo_ref, dgm_ref, sm_ref, smo_ref, dmot_ref))
        for n, (x_ref, g_ref, dx_ref, dg_ref, st_ref, sto_ref, dxt_ref) in enumerate(groups):
            xv = x_ref[...]
            dhv = dmixed[:, n * d:(n + 1) * d]
            r = lax.rsqrt(jnp.mean(xv * xv, axis=-1, keepdims=True) + EPS)
            u = dhv * g_ref[...]
            dxf = r * u - xv * (r * r * r * jnp.mean(u * xv, axis=-1, keepdims=True))
            dxb = dxf.astype(BF16)
            dx_ref[...] = dxb
            dxt_ref[0] = dxf.T.astype(BF16)
            dg_ref[...] += jnp.sum(dhv * (xv * r), axis=0, keepdims=True)
            prod = xv * dxb.astype(F32)
            for g in range(PAIRS):
                grp = prod[:, g * LANES:(g + 1) * LANES]
                da = jnp.sum(jnp.where(_lane() < HEAD_DIM, grp, 0.0), axis=1, keepdims=True)
                db = jnp.sum(jnp.where(_lane() >= HEAD_DIM, grp, 0.0), axis=1, keepdims=True)
                sto_ref[g] = jnp.where(_lane() == 2, da, jnp.where(_lane() == 3, db, st_ref[g]))

    row = pl.BlockSpec((bt, d), lambda i: (i, 0))
    vec = pl.BlockSpec((1, d), lambda i: (0, 0))
    stat = pl.BlockSpec((PAIRS, bt, LANES), lambda i: (0, i, 0))
    return gridded(
        body, name=name, grid=(t // bt,),
        in_specs=[pl.BlockSpec((bt, dx1b.shape[1]), lambda i: (i, 0)), pl.BlockSpec(w_o.shape, lambda i: (0, 0)),
                  row, row, vec, vec, stat, stat],
        out_specs=[row, row, vec, vec, stat, stat] + [pl.BlockSpec((1, d, bt), lambda i: (i, 0, 0))] * 2,
        out_shape=[jax.ShapeDtypeStruct((t, d), BF16)] * 2 + [jax.ShapeDtypeStruct((1, d), F32)] * 2
        + [jax.ShapeDtypeStruct(st_f.shape, F32)] * 2 + [jax.ShapeDtypeStruct((t // bt, d, bt), BF16)] * 2,
        ins=[dx1b, w_o, fo, mo, g_fo, g_mo, st_f, st_m], semantics=("arbitrary",), side=side)


def flash_bwd(q_arr, qt_arr, k_arr, v_arr, do_arr, dot_arr, st, f_blocks, *, qoff, koff, voff, pair, scale, name, qblk=1024,
              side=None):
    t = q_arr.shape[0]
    blk = qt_arr.shape[2]
    qblk = max(min(qblk, t), blk)
    sub = qblk // blk
    nb, nbq = t // blk, t // qblk
    w = LANES if pair else 2 * LANES
    hw = w // 2
    has_bias = f_blocks is not None
    split = _halves if pair else _groups
    ins = [q_arr, qt_arr, k_arr, v_arr, do_arr, dot_arr, st] + ([f_blocks] if has_bias else [])
    n_out = 4 if has_bias else 3
    s_ins, s_outs = (side.ins, side.out_shapes) if side else ([], [])

    def wide(ref, first, count):
        parts = [ref[first + u] for u in range(count)]
        return parts[0] if count == 1 else jnp.concatenate(parts, axis=1)

    def body(*refs):
        main, si, outs, so, sems = _split_refs(refs, [len(ins), len(s_ins), n_out, len(s_outs), 2 if side else 0])
        q_ref, qt_ref, k_ref, v_ref, do_ref, dot_ref, st_ref = main[:7]
        dq_ref, dk_ref, dv_ref = outs[:3]
        g, j = pl.program_id(0), pl.program_id(1)
        step_id = g * nb + j
        if side:
            @pl.when(step_id == 0)
            def _():
                side.first(si, so, *sems)

        @pl.when(j == 0)
        def _():
            dq_ref[...] = jnp.zeros_like(dq_ref)

        kk, vv = k_ref[...], v_ref[...]
        ks = [kk, kk] if pair else _groups(kk)
        if has_bias:
            f_ref, df_ref = main[7], outs[3]
            fk = [LOG2E * _pick_row(f_ref[0], 2 * g + n) for n in range(2)]

            @pl.when(step_id == 0)
            def _():
                df_ref[...] = jnp.zeros_like(df_ref)

        def step(tb, count, carry, diagonal):
            rows = pl.ds(pl.multiple_of(tb * blk, blk), count * blk)
            qs = split((q_ref[rows, :].astype(F32) * (scale * LOG2E)).astype(BF16))
            qt = (wide(qt_ref, tb, count).astype(F32) * (scale * LOG2E)).astype(BF16)
            dos = [h.astype(BF16) for h in _halves(do_ref[rows, :].astype(F32))]
            dot = wide(dot_ref, tb, count)
            stats = st_ref[0, rows, :]
            new, dqs, row_sums = [], [], []
            for n in range(2):
                dkt, dvt, dfk = carry[n]
                s = lax.dot_general(qs[n], ks[n], NT, preferred_element_type=F32)
                if has_bias:
                    s = s - fk[n]
                if diagonal:
                    s = _below_diagonal(s)
                p = jnp.exp2(s - stats[:, n:n + 1])
                dp = lax.dot_general(dos[n], vv, NT, preferred_element_type=F32)
                ds = p * (dp - stats[:, 2 + n:3 + n])
                dsb = ds.astype(BF16)
                dvt = dvt + jnp.dot(dot[n * HEAD_DIM:(n + 1) * HEAD_DIM], p.astype(BF16), preferred_element_type=F32)
                dkt = dkt + jnp.dot(qt[n * hw:(n + 1) * hw], dsb, preferred_element_type=F32)
                dqs.append(jnp.dot(dsb, ks[n], preferred_element_type=F32))
                if has_bias:
                    dfk = dfk - jnp.sum(ds, axis=0, keepdims=True)
                    row_sums.append(jnp.sum(ds, axis=1, keepdims=True))
                new.append((dkt, dvt, dfk))
            dq = (jnp.where(_lane() < HEAD_DIM, dqs[0], dqs[1]) if pair else jnp.concatenate(dqs, axis=1)) * scale
            if has_bias:
                df_ref[rows, :] += jnp.where(_lane() == 2 * g, row_sums[0], jnp.where(_lane() == 2 * g + 1, row_sums[1], 0.0))
            dq_ref[rows, :] += dq
            return tuple(new)

        init = tuple((jnp.zeros((hw, blk), F32), jnp.zeros((HEAD_DIM, blk), F32), jnp.zeros((1, blk), F32)) for _ in range(2))
        carry = step(j, 1, init, True)
        whole = j // sub + 1
        carry = lax.fori_loop(j + 1, whole * sub, lambda tb, c: step(tb, 1, c, False), carry)
        (dka, dva, dfa), (dkb, dvb, dfb) = lax.fori_loop(whole, nbq, lambda i, c: step(i * sub, sub, c, False), carry)
        dk_ref[...] = (jnp.concatenate([dka, dkb], axis=0).T * LN2).astype(BF16)
        dv_ref[...] = jnp.concatenate([dva, dvb], axis=0).T.astype(BF16)
        if has_bias:
            row = lax.broadcasted_iota(jnp.int32, (LANES, 1), 0)
            by_head = jnp.where(row == 2 * g, dfa, jnp.where(row == 2 * g + 1, dfb, 0.0))
            df_ref[pl.ds(pl.multiple_of(j * blk, blk), blk), :] += by_head.T
        if side:
            @pl.when(step_id == PAIRS * nb - 1)
            def _():
                side.last(si, so, *sems)

    in_specs = [pl.BlockSpec((t, w), lambda g, j: (0, qoff + g)), pl.BlockSpec((nb, w, blk), lambda g, j: (0, qoff + g, 0)),
                pl.BlockSpec((blk, w), lambda g, j: (j, koff + g)), pl.BlockSpec((blk, LANES), lambda g, j: (j, voff + g)),
                pl.BlockSpec((t, LANES), lambda g, j: (0, g)), pl.BlockSpec((nb, LANES, blk), lambda g, j: (0, g, 0)),
                pl.BlockSpec((1, t, LANES), lambda g, j: (g, 0, 0))]
    out_specs = [pl.BlockSpec((t, w), lambda g, j: (0, g)), pl.BlockSpec((blk, w), lambda g, j: (j, g)),
                 pl.BlockSpec((blk, LANES), lambda g, j: (j, g))]
    out_shape = [jax.ShapeDtypeStruct((t, PAIRS * w), F32), jax.ShapeDtypeStruct((t, PAIRS * w), BF16),
                 jax.ShapeDtypeStruct((t, PAIRS * LANES), BF16)]
    if has_bias:
        in_specs.append(pl.BlockSpec((1, N_HEADS, blk), lambda g, j: (j, 0, 0)))
        out_specs.append(pl.BlockSpec((t, LANES), lambda g, j: (0, 0)))
        out_shape.append(jax.ShapeDtypeStruct((t, LANES), F32))
    s_in_specs, s_out_specs = side.specs() if side else ([], [])
    return pl.pallas_call(
        body, name=name, grid=(PAIRS, nb), in_specs=in_specs + s_in_specs, out_specs=out_specs + s_out_specs,
        out_shape=out_shape + list(s_outs), scratch_shapes=side.sems() if side else [],
        compiler_params=_params(("arbitrary", "arbitrary")),
    )(*ins, *s_ins)


def _adamw_math(w, g, m, v):
    nm = ADAM_B1 * m + (1.0 - ADAM_B1) * g
    nv = ADAM_B2 * v + (1.0 - ADAM_B2) * (g * g)
    m_hat = nm / (1.0 - ADAM_B1 ** ADAM_STEP)
    v_hat = nv / (1.0 - ADAM_B2 ** ADAM_STEP)
    return -ADAM_LR * (m_hat / (jnp.sqrt(v_hat) + ADAM_EPS) + ADAM_WD * w), nm, nv


def adamw_vectors(ws, g_rows, ms, vs, *, name):
    k = len(ws)

    def body(g_ref, *refs):
        w_refs, m_refs, v_refs, outs = _split_refs(refs, [k, k, k, 4 * k])
        for i in range(k):
            g = g_ref[i:i + 1, :w_refs[i].shape[1]]
            outs[4 * i][...] = g
            outs[4 * i + 1][...], outs[4 * i + 2][...], outs[4 * i + 3][...] = _adamw_math(
                w_refs[i][...], g, m_refs[i][...], v_refs[i][...])

    flat = pl.pallas_call(
        body, name=name, out_shape=[jax.ShapeDtypeStruct(w.shape, F32) for w in ws for _ in range(4)],
        compiler_params=_params(),
    )(g_rows, *ws, *ms, *vs)
    return [flat[4 * i:4 * i + 4] for i in range(k)]


def adamw_whole(w, g, m, v, *, name):
    def body(w_ref, g_ref, m_ref, v_ref, d_ref, nm_ref, nv_ref):
        d_ref[...], nm_ref[...], nv_ref[...] = _adamw_math(w_ref[...], g_ref[...], m_ref[...], v_ref[...])

    return pl.pallas_call(body, name=name, out_shape=[jax.ShapeDtypeStruct(w.shape, F32)] * 3,
                          compiler_params=_params())(w, g, m, v)


def adamw_halves(w, g_mine, g_other, m, v, core, *, name):
    _, k, n = w.shape
    br = _row_block(k // 2)
    nh = k // 2 // br

    def body(c_ref, w_ref, gm_ref, go_ref, m_ref, v_ref, g_out, d_ref, nm_ref, nv_ref):
        gv = jnp.where(pl.program_id(0) == c_ref[0], gm_ref[...], go_ref[...])
        g_out[0] = gv
        d_ref[0], nm_ref[0], nv_ref[0] = _adamw_math(w_ref[0], gv, m_ref[0], v_ref[0])

    full = pl.BlockSpec((1, br, n), lambda hb, i, c: (0, hb * nh + i, 0))
    half = pl.BlockSpec((br, n), lambda hb, i, c: (i, 0))
    return pl.pallas_call(
        body, name=name,
        grid_spec=pltpu.PrefetchScalarGridSpec(num_scalar_prefetch=1, grid=(2, nh), in_specs=[full, half, half, full, full],
                                               out_specs=[full] * 4),
        out_shape=[jax.ShapeDtypeStruct(w.shape, F32)] * 4,
        compiler_params=_params(("parallel", "parallel")),
    )(core, w, g_mine, g_other, m, v)


def adamw_halves_t(wt, mt, vt, gt_mine, gt_other, core, *, name, bc=128):
    n, _, k = wt.shape
    nh = k // 2 // bc

    def body(c_ref, w_ref, m_ref, v_ref, gm_ref, go_ref, g_out, d_ref, nm_ref, nv_ref):
        gv = jnp.where(pl.program_id(0) == c_ref[0], gm_ref[...], go_ref[...])
        g_out[:, 0, :] = gv
        d_ref[:, 0, :], nm_ref[:, 0, :], nv_ref[:, 0, :] = _adamw_math(w_ref[:, 0, :], gv, m_ref[:, 0, :], v_ref[:, 0, :])

    full = pl.BlockSpec((n, 1, bc), lambda hb, i, c: (0, 0, hb * nh + i))
    half = pl.BlockSpec((n, bc), lambda hb, i, c: (0, i))
    return pl.pallas_call(
        body, name=name,
        grid_spec=pltpu.PrefetchScalarGridSpec(num_scalar_prefetch=1, grid=(2, nh), in_specs=[full, full, full, half, half],
                                               out_specs=[full] * 4),
        out_shape=[jax.ShapeDtypeStruct(wt.shape, F32)] * 4,
        compiler_params=_params(("parallel", "parallel")),
    )(core, wt, mt, vt, gt_mine, gt_other)


def add_pair(dw, recv, core, *, name):
    n4, k, n = dw.shape
    half = (1, k // 2, n) if split_axis(k) == 0 else (1, k, n // 2)
    mine = (lambda q, c: (q, c[0], 0)) if split_axis(k) == 0 else (lambda q, c: (q, 0, c[0]))

    def body(c_ref, a_ref, b_ref, o_ref):
        o_ref[...] = (a_ref[...] + b_ref[...].astype(F32)).astype(BF16)

    return pl.pallas_call(
        body, name=name,
        grid_spec=pltpu.PrefetchScalarGridSpec(
            num_scalar_prefetch=1, grid=(n4,),
            in_specs=[pl.BlockSpec(half, mine), pl.BlockSpec(half, lambda q, c: (q, 0, 0))],
            out_specs=pl.BlockSpec(half, lambda q, c: (q, 0, 0))),
        out_shape=jax.ShapeDtypeStruct((n4,) + half[1:], BF16),
        compiler_params=_params(("parallel",)),
    )(core, dw, recv)


def sum_chips(parts, *, name):
    n4, r, n = parts.shape
    if r % 16 == 0:
        br, bc = _row_block(r), n
    else:
        br, bc = r, LANES

    def body(p_ref, o_ref):
        acc = p_ref[0].astype(F32)
        for q in range(1, n4):
            acc = acc + p_ref[q].astype(F32)
        o_ref[...] = acc

    return pl.pallas_call(
        body, name=name, grid=(r // br, n // bc),
        in_specs=[pl.BlockSpec((n4, br, bc), lambda i, j: (0, i, j))], out_specs=pl.BlockSpec((br, bc), lambda i, j: (i, j)),
        out_shape=jax.ShapeDtypeStruct((r, n), F32),
        compiler_params=_params(("parallel", "parallel")),
    )(parts)


ANY = pl.BlockSpec(memory_space=pl.ANY)


def _place():
    x, y, c = lax.axis_index("x"), lax.axis_index("y"), lax.axis_index("c")
    chips = [(1 - x, y), (x, 1 - y), (1 - x, 1 - y)]
    return x, y, c, chips


def _copy(src, dst, send_sems, recv_sems, k, to):
    return pltpu.make_async_remote_copy(src_ref=src, dst_ref=dst, send_sem=send_sems.at[k], recv_sem=recv_sems.at[k],
                                        device_id=to, device_id_type=MESH)


def split_axis(rows):
    return 0 if rows % 32 == 0 else 1


def _half(ref, lead, hf):
    rows, cols = ref.shape[-2:]
    if split_axis(rows) == 0:
        at = (pl.ds(hf * (rows // 2), rows // 2), slice(None))
    else:
        at = (slice(None), pl.ds(hf * (cols // 2), cols // 2))
    return ref.at[at] if lead is None else ref.at[(lead,) + at]


def _gather_first(srcs, dsts, ssems, rsems):
    x, y, c, chips = _place()
    for ti, (s, d) in enumerate(zip(srcs, dsts)):
        for j, (cx, cy) in enumerate(chips):
            _copy(_half(s, None, c), _half(d, 2 * x + y, c), ssems, rsems, 3 * ti + j, (cx, cy, c)).start()


def _gather_mid(srcs, dsts, ssems, rsems):
    x, y, c, chips = _place()
    n1 = 3 * len(srcs)
    for ti, d in enumerate(dsts):
        for j, (cx, cy) in enumerate(chips):
            landed = _half(d, 2 * cx + cy, c)
            _copy(landed, landed, ssems, rsems, 3 * ti + j, (cx, cy, c)).wait_recv()
            _copy(landed, landed, ssems, rsems, n1 + 3 * ti + j, (x, y, 1 - c)).start()


def _gather_last(srcs, dsts, ssems, rsems):
    x, y, c, chips = _place()
    n1 = 3 * len(srcs)
    for ti, (s, d) in enumerate(zip(srcs, dsts)):
        for j, (cx, cy) in enumerate(chips):
            other = _half(d, 2 * cx + cy, 1 - c)
            _copy(other, other, ssems, rsems, n1 + 3 * ti + j, (x, y, 1 - c)).wait_recv()
        for j, (cx, cy) in enumerate(chips):
            mine = _half(s, None, c)
            _copy(mine, mine, ssems, rsems, 3 * ti + j, (cx, cy, c)).wait_send()
            _copy(mine, mine, ssems, rsems, n1 + 3 * ti + j, (x, y, 1 - c)).wait_send()


def gather_side(shards):
    return Side(shards, [jax.ShapeDtypeStruct((N_CHIPS,) + s.shape, s.dtype) for s in shards], 6 * len(shards),
                _gather_first, _gather_last, _gather_mid)


def _scatter_first(srcs, dsts, ssems, rsems):
    x, y, c, chips = _place()
    for ti, (s, d) in enumerate(zip(srcs, dsts)):
        for j, (cx, cy) in enumerate(chips):
            _copy(s.at[2 * cx + cy], d.at[2 * x + y], ssems, rsems, 3 * ti + j, (cx, cy, c)).start()


def _scatter_last(srcs, dsts, ssems, rsems):
    x, y, c, chips = _place()
    for ti, (s, d) in enumerate(zip(srcs, dsts)):
        for j, (cx, cy) in enumerate(chips):
            _copy(s.at[2 * cx + cy], d.at[2 * cx + cy], ssems, rsems, 3 * ti + j, (cx, cy, c)).wait_recv()
        for j, (cx, cy) in enumerate(chips):
            _copy(s.at[2 * cx + cy], d.at[2 * cx + cy], ssems, rsems, 3 * ti + j, (cx, cy, c)).wait_send()


def scatter_side(parts):
    return Side(parts, [jax.ShapeDtypeStruct(p.shape, p.dtype) for p in parts], 3 * len(parts), _scatter_first, _scatter_last)


def run_side(side, *, name):
    n_in, n_out = len(side.ins), len(side.out_shapes)

    def body(*refs):
        si, so, sems = _split_refs(refs, [n_in, n_out, 2])
        side.first(si, so, *sems)
        if side.mid is not None:
            side.mid(si, so, *sems)
        side.last(si, so, *sems)

    in_specs, out_specs = side.specs()
    return pl.pallas_call(body, name=name, in_specs=in_specs, out_specs=out_specs, out_shape=side.out_shapes,
                          scratch_shapes=side.sems())(*side.ins)


def _swap_first(srcs, dsts, ssems, rsems):
    x, y, c, _ = _place()
    for k, (s, d) in enumerate(zip(srcs, dsts)):
        _copy(s, d, ssems, rsems, k, (x, y, 1 - c)).start()


def _swap_last(srcs, dsts, ssems, rsems):
    x, y, c, _ = _place()
    for k, (s, d) in enumerate(zip(srcs, dsts)):
        _copy(s, d, ssems, rsems, k, (x, y, 1 - c)).wait()


def swap_side(xs):
    return Side(xs, [jax.ShapeDtypeStruct(a.shape, a.dtype) for a in xs], len(xs), _swap_first, _swap_last)


def _swap_halves(srcs, dsts, ssems, rsems):
    x, y, c, _ = _place()
    for k, (s, d) in enumerate(zip(srcs, dsts)):
        hk = s.shape[1] // 2
        yield _copy(s.at[:, pl.ds((1 - c) * hk, hk), :], d, ssems, rsems, k, (x, y, 1 - c))


def _swap_halves_first(srcs, dsts, ssems, rsems):
    for cp in _swap_halves(srcs, dsts, ssems, rsems):
        cp.start()


def _swap_halves_last(srcs, dsts, ssems, rsems):
    for cp in _swap_halves(srcs, dsts, ssems, rsems):
        cp.wait()


def swap_halves_side(xs):
    return Side(xs, [jax.ShapeDtypeStruct((a.shape[0], a.shape[1] // 2, a.shape[2]), a.dtype) for a in xs], len(xs),
                _swap_halves_first, _swap_halves_last)


def allreduce_small(s):
    n_dev = 8

    def body(s_ref, out_ref, buf, send_sems, recv_sems):
        x, y, c, _ = _place()
        me = 4 * x + 2 * y + c
        buf[me] = s_ref[...]
        sends = []
        for k in range(1, n_dev):
            px = 1 - x if k & 4 else x
            py = 1 - y if k & 2 else y
            pc = 1 - c if k & 1 else c
            cp = _copy(s_ref, buf.at[me], send_sems, recv_sems, k - 1, (px, py, pc))
            cp.start()
            sends.append((cp, 4 * px + 2 * py + pc))
        for k, (cp, peer) in enumerate(sends):
            _copy(s_ref, buf.at[peer], send_sems, recv_sems, k, (x, y, c)).wait_recv()
        for cp, _ in sends:
            cp.wait_send()
        acc = buf[0]
        for d in range(1, n_dev):
            acc = acc + buf[d]
        out_ref[...] = acc

    vm = pl.BlockSpec(memory_space=pltpu.VMEM)
    return pl.pallas_call(
        body, name="allreduce_small", in_specs=[vm], out_specs=vm,
        out_shape=jax.ShapeDtypeStruct(s.shape, F32),
        scratch_shapes=[pltpu.VMEM((n_dev,) + s.shape, F32), pltpu.SemaphoreType.DMA((n_dev - 1,)),
                        pltpu.SemaphoreType.DMA((n_dev - 1,))],
    )(s)


def join_cols(sm):
    n4, k, n = sm.shape
    return sm.transpose(1, 0, 2).reshape(k, n4 * n)


def split_cols(full):
    k, n = full.shape
    return full.reshape(k, N_CHIPS, n // N_CHIPS).transpose(1, 0, 2)


def _pad_heads(w, width):
    lead, heads = w.shape[:-1], w.shape[-1] // width
    w = w.reshape(lead + (heads, width))
    return jnp.pad(w, [(0, 0)] * len(lead) + [(0, 0), (0, LANES - width)]).reshape(lead + (heads * LANES,))


def _unpad_heads(w, width):
    lead, heads = w.shape[:-1], w.shape[-1] // LANES
    return w.reshape(lead + (heads, LANES))[..., :width].reshape(lead + (heads * width,))


O_F = 3 * FW
O_CQ = O_F + N_HEADS
O_CKV = O_CQ + Q_RANK
O_KR = O_CKV + KV_RANK
O_END = O_KR + ROPE_DIM


def split_w_in_t(w_sm):
    n4, r, d = w_sm.shape
    segments = [(0, O_F, 0, 0), (O_F, N_HEADS, 1, S_F), (O_CQ, Q_RANK, 1, S_CQ), (O_CKV, KV_RANK, 1, S_CKV),
                (O_KR, ROPE_DIM, 1, S_KR + HEAD_DIM)]
    moves = []
    for v0, n, dst, d0 in segments:
        for q in range(n4):
            a, b = max(v0, q * r), min(v0 + n, (q + 1) * r)
            if a < b:
                assert a % 2 == 0 and b % 2 == 0 and (d0 + a - v0) % 2 == 0
                moves.append((q, (a - q * r) // 2, (b - a) // 2, dst, (d0 + a - v0) // 2))

    def body(w_ref, qkv_ref, small_ref, src, dst_qkv, dst_small):
        for q in range(n4):
            src[q] = pltpu.bitcast(w_ref[q], jnp.uint32)
        dst_small[...] = jnp.zeros(dst_small.shape, jnp.uint32)
        for q, a, n, dst, b in moves:
            (dst_qkv, dst_small)[dst][pl.ds(b, n), :] = src[q, pl.ds(a, n), :]
        qkv_ref[...] = pltpu.bitcast(dst_qkv[...], w_sm.dtype)
        small_ref[...] = pltpu.bitcast(dst_small[...], w_sm.dtype)

    return pl.pallas_call(
        body, name="split_w_in",
        out_shape=[jax.ShapeDtypeStruct((O_F, d), w_sm.dtype), jax.ShapeDtypeStruct((S_END, d), w_sm.dtype)],
        scratch_shapes=[pltpu.VMEM((n4, r // 2, d), jnp.uint32), pltpu.VMEM((O_F // 2, d), jnp.uint32),
                        pltpu.VMEM((S_END // 2, d), jnp.uint32)],
        compiler_params=_params(),
    )(w_sm)


def join_w_in_t(d_qkv_t, d_small_t):
    small = len(d_qkv_t)
    segments = [(n, 0, n * FW, FW) for n in range(small)]
    segments += [(small, S_F, O_F, N_HEADS), (small, S_CQ, O_CQ, Q_RANK), (small, S_CKV, O_CKV, KV_RANK),
                 (small, S_KR + HEAD_DIM, O_KR, ROPE_DIM)]
    r, d = O_END // N_CHIPS, d_small_t.shape[1]

    def body(*refs):
        o_ref = refs[-1]
        for src, s0, v0, n in segments:
            for q in range(N_CHIPS):
                a, b = max(v0, q * r), min(v0 + n, (q + 1) * r)
                if a < b:
                    o_ref[q, pl.ds(a - q * r, b - a), :] = refs[src][pl.ds(s0 + a - v0, b - a), :]

    return pl.pallas_call(
        body, name="join_w_in", out_shape=jax.ShapeDtypeStruct((N_CHIPS, r, d), F32), compiler_params=_params(),
    )(*d_qkv_t, d_small_t)


def rope_inputs(pos):
    inv_freq = ROPE_THETA ** (-jnp.arange(0, ROPE_DIM, 2, dtype=F32) / ROPE_DIM)
    inv_row = jnp.concatenate([jnp.zeros((HEAD_DIM,), F32), inv_freq, inv_freq, jnp.zeros((LANES - HEAD_DIM - ROPE_DIM,), F32)])
    return pos.astype(F32)[:, None], inv_row[None, :]


def _pad_lanes(v, n):
    return jnp.pad(v, ((0, 0), (0, n - v.shape[1])))


ATTN_BLK = 512
ATTN_FWD_BLK = 1024
ATTN_BWD_QBLK = 1024
ACC_ROWS = HEAD_DIM + 16


def local_step(xs, pos, tgt, gains, early_weights, late_weights, early_side=None, fwd_sides=(None, None), reduction=None):
    g_attn, b_forget, g_q, g_kv, g_fo, g_mo, g_mlp, g_fin = gains
    t = xs.shape[0]
    blk = min(ATTN_BLK, t)
    fox_scale = 1.0 / (HEAD_DIM ** 0.5)
    mla_scale = 1.0 / ((HEAD_DIM + ROPE_DIM) ** 0.5)

    h1, *gathered = rmsnorm(xs, g_attn, out_dtype=BF16, name="norm_attn", side=early_side)
    w_in_t, w_uq_p, w_ukv = early_weights(gathered)
    w_qkv_t, w_small_t = split_w_in_t(w_in_t)
    kv = w_ukv.reshape(KV_RANK, N_HEADS, 2 * HEAD_DIM)
    w_ukv_p = jnp.concatenate([_pad_heads(kv[:, :, :HEAD_DIM].reshape(KV_RANK, FW), HEAD_DIM),
                               kv[:, :, HEAD_DIM:].reshape(KV_RANK, FW)], axis=1)
    b_f = _pad_lanes(b_forget, LANES)
    pos_col, inv_row = rope_inputs(pos)

    qkv, qkv_t = mm(h1, w_qkv_t, trans_b=True, out_dtypes=[BF16], t_blk=blk, name="proj_qkv", bn=1536)
    small, = mm(h1, w_small_t, trans_b=True, out_dtypes=[F32], name="proj_small")
    mq, mk, mv, lf, cqn, ckvn, mq_t, mv_t = mla_prep(small, g_q, g_kv, w_uq_p, w_ukv_p, pos_col, inv_row, b_f,
                                                     name="mla_prep", bt=blk)
    f_cum = cumsum_rows(lf, reverse=False, name="gate_cumsum")
    f_blocks = f_cum[:, :N_HEADS].reshape(t // blk, blk, N_HEADS).transpose(0, 2, 1)
    fo, st_f, *gathered = flash_fwd(qkv_t, qkv, qkv_t, f_cum, qoff=0, koff=PAIRS, voff=2 * PAIRS, pair=True,
                                    scale=fox_scale, name="fox_fwd", blk=ATTN_FWD_BLK, side=fwd_sides[0])
    mo, st_m, *more = flash_fwd(mq_t, mk, mv_t, None, qoff=0, koff=0, voff=0, pair=False, scale=mla_scale, name="mla_fwd",
                                blk=ATTN_FWD_BLK, side=fwd_sides[1])
    w_o, w_up, w_down = late_weights(gathered + more)
    mixed = mix_norm(fo, mo, g_fo, g_mo, name="norm_mix")

    def inv_rms(v):
        return lax.rsqrt(jnp.mean(v * v, axis=-1, keepdims=True) + EPS)

    def residual_then_norm(acc, res, g):
        xn = acc + res
        return xn, xn * inv_rms(xn) * g

    def norm_bwd(dh, xn, res, g):
        r = inv_rms(xn)
        uu = dh * g
        return (r * uu - xn * (r * r * r * jnp.mean(uu * xn, axis=-1, keepdims=True)) + res,
                jnp.sum(dh * (xn * r), axis=0, keepdims=True))

    def norm_bwd2(dh, xn, res, g):
        dx, dg = norm_bwd(dh, xn, res, g)
        return dx, dx, dg

    def residual_then_loss(acc, res, target, g):
        xn = acc + res
        r = inv_rms(xn)
        xh = xn * r
        e = xh * g - target
        part = 0.5 * jnp.sum(jnp.mean(e * e, axis=-1, keepdims=True), axis=0, keepdims=True)
        dy = e * (1.0 / xn.shape[1])
        uu = dy * g
        dx = r * uu - xn * (r * r * r * jnp.mean(uu * xn, axis=-1, keepdims=True))
        return dx, dx, jnp.sum(dy * xh, axis=0, keepdims=True), part + jnp.zeros_like(g)

    x1, h2 = mm(mixed, w_o, extras=[xs], vecs=[g_mlp], epilogue=residual_then_norm, out_dtypes=[F32, BF16], name="out_proj")

    def relu2(uu):
        r = jnp.maximum(uu.astype(F32), 0.0)
        return (r * r).astype(BF16)

    u, = mm(h2, w_up, out_dtypes=[BF16], name="mlp_up", bn=2048)
    dx2, dx2b, dg_fin, loss_row = mm(u, w_down, a_pro=relu2, extras=[x1, tgt], vecs=[g_fin], epilogue=residual_then_loss,
                                     out_dtypes=[F32, BF16], n_sums=2, name="mlp_down_loss")
    loss = loss_row[:, :1]

    def relu2_grad(acc, uu):
        return (acc * (2.0 * jnp.maximum(uu.astype(F32), 0.0)),)

    du, = mm(dx2b, w_down, trans_b=True, extras=[u], epilogue=relu2_grad, out_dtypes=[BF16], name="mlp_down_bwd", bn=2048)
    dw_down, dw_down_b = (g.reshape(N_CHIPS, -1, w_down.shape[1])
                          for g in mm_tn(u, dx2b, a_pro=relu2, name="dw_down", bf16_copy=True))
    dx1, dx1b, dg_mlp = mm(du, w_up, trans_b=True, extras=[x1, dx2], vecs=[g_mlp], epilogue=norm_bwd2,
                           out_dtypes=[F32, BF16], n_sums=1, name="mlp_up_bwd")
    dw_up, dw_up_b = mm_tn(h2, du, name="dw_up", col_shards=N_CHIPS, bf16_copy=True)

    dw_o, dw_o_b = (g.reshape(N_CHIPS, -1, w_o.shape[1]) for g in mm_tn(mixed, dx1b, name="dw_o", bf16_copy=True))
    late, late_b = (dw_o, dw_up, dw_down), (dw_o_b, dw_up_b, dw_down_b)
    red = reduction
    dfo, dmo, dg_fo, dg_mo, st_f, st_m, dfo_t, dmo_t, *got = mix_norm_bwd(
        dx1b, w_o, fo, mo, g_fo, g_mo, st_f, st_m, name="out_proj_mix_bwd", bt=blk,
        side=red.late_swap(late, late_b) if red else None)
    dfq, dfk, dfv, d_f, *got = flash_bwd(
        qkv, qkv_t, qkv, qkv, dfo, dfo_t, st_f, f_blocks, qoff=0, koff=PAIRS, voff=2 * PAIRS, pair=True,
        scale=fox_scale, name="fox_bwd", qblk=ATTN_BWD_QBLK, side=red.late_scatter(got) if red else None)
    dmq, dmk, dmv, *got = flash_bwd(mq, mq_t, mk, mv, dmo, dmo_t, st_m, None, qoff=0, koff=0, voff=0,
                                    pair=False, scale=mla_scale, name="mla_bwd", qblk=ATTN_BWD_QBLK,
                                    side=red.late_halves(got) if red else None)
    if red:
        red.late_done(got)
    dlf = cumsum_rows(d_f, reverse=True, name="gate_cumsum_bwd")
    dsmall, dq_u, dkv_u, dg_q, dg_kv, db_f = mla_prep_bwd(dmq, dmk, dmv, dlf, small, g_q, g_kv, w_uq_p, w_ukv_p,
                                                          pos_col, inv_row, b_f, name="mla_prep_bwd")
    dw_uq_p = mm_tn(cqn, dq_u, name="dw_uq")
    dw_ukv_p = mm_tn(ckvn, dkv_u, name="dw_ukv")

    def to_bf16(tile):
        return tile.astype(BF16)

    dqkv = [dfq, dfk, dfv]
    dw_in_t = join_w_in_t([mm_tn(part, h1, a_pro=to_bf16, name="dw_" + nm) for part, nm in zip(dqkv, "qkv")],
                          mm_tn(dsmall, h1, name="dw_small"))
    dk_cols = _unpad_heads(dw_ukv_p[:, :HW], HEAD_DIM).reshape(KV_RANK, N_HEADS, HEAD_DIM)
    dv_cols = dw_ukv_p[:, HW:].reshape(KV_RANK, N_HEADS, HEAD_DIM)
    dw_ukv = jnp.concatenate([dk_cols, dv_cols], axis=2).reshape(KV_RANK, N_HEADS * 2 * HEAD_DIM)
    early = (dw_in_t, split_cols(dw_uq_p), split_cols(dw_ukv))
    w_parts = [w_qkv_t[n * FW:(n + 1) * FW] for n in range(3)]
    grad_x, dg_attn, *got = mm(dqkv + [dsmall], w_parts + [w_small_t], a_pro=to_bf16, extras=[xs, dx1], vecs=[g_attn],
                               epilogue=norm_bwd, out_dtypes=[F32], n_sums=1, name="proj_bwd",
                               side=red.early_scatter(early) if red else None)
    if red:
        red.early_done(got)
    d_gains = (dg_attn, db_f[:, :N_HEADS], dg_q, dg_kv, dg_fo, dg_mo, dg_mlp, dg_fin)
    return loss, grad_x, early, late, d_gains


class GradReduction:
    def __init__(self, core_id, chip):
        self.core_id, self.chip = core_id, chip
        self.core = core_id.reshape(1).astype(jnp.int32)
        self.grads, self.pairs, self.halves, self.others = {}, {}, {}, {}

    def _other_halves(self, grads):
        out = []
        for g in grads:
            axis = 1 + split_axis(g.shape[1])
            size = g.shape[axis] // 2
            out.append(lax.dynamic_slice_in_dim(g, (1 - self.core_id) * size, size, axis=axis).astype(BF16))
        return out

    def _add_pairs(self, group, recvs):
        self.pairs[group] = [add_pair(g, r, self.core, name="add_pair_%s_%d" % (group, n))
                             for n, (g, r) in enumerate(zip(self.grads[group], recvs))]
        return scatter_side(self.pairs[group])

    def _chip_sums(self, group, scattered):
        chip = self.chip
        with_mine = [lax.dynamic_update_index_in_dim(s, lax.dynamic_index_in_dim(p, chip, 0, keepdims=True), chip, 0)
                     for s, p in zip(scattered, self.pairs[group])]
        self.halves[group] = [sum_chips(s, name="sum_chips_%s_%d" % (group, n)) for n, s in enumerate(with_mine)]
        return self.halves[group]

    def late_swap(self, grads, bf16_copies):
        self.grads["late"] = list(grads)
        return swap_halves_side(list(bf16_copies))

    def late_scatter(self, recvs):
        return self._add_pairs("late", recvs)

    def late_halves(self, scattered):
        return swap_side(self._chip_sums("late", scattered))

    def late_done(self, others):
        self.others["late"] = list(others)

    def early_scatter(self, grads):
        self.grads["early"] = list(grads)
        return self._add_pairs("early", run_side(swap_side(self._other_halves(grads)), name="swap_early_sends"))

    def early_done(self, scattered):
        self.others["early"] = list(run_side(swap_side(self._chip_sums("early", scattered)), name="swap_early_halves"))

def kernel(x, positions, attn_norm_g, w_in, b_forget, q_norm_g, w_uq, kv_norm_g, w_ukv, fox_out_g, mla_out_g, w_o, mlp_norm_g, w_up, w_down, final_norm_g, loss_target, m_attn_norm_g, m_w_in, m_b_forget, m_q_norm_g, m_w_uq, m_kv_norm_g, m_w_ukv, m_fox_out_g, m_mla_out_g, m_w_o, m_mlp_norm_g, m_w_up, m_w_down, m_final_norm_g, v_attn_norm_g, v_w_in, v_b_forget, v_q_norm_g, v_w_uq, v_kv_norm_g, v_w_ukv, v_fox_out_g, v_mla_out_g, v_w_o, v_mlp_norm_g, v_w_up, v_w_down, v_final_norm_g):
    core_id = lax.axis_index("c")
    core = core_id.reshape(1).astype(jnp.int32)
    chip = 2 * lax.axis_index("x") + lax.axis_index("y")
    big = [w_in, w_uq, w_ukv, w_o, w_up, w_down]
    big_m = [m_w_in, m_w_uq, m_w_ukv, m_w_o, m_w_up, m_w_down]
    big_v = [v_w_in, v_w_uq, v_w_ukv, v_w_o, v_w_up, v_w_down]
    n_early = 3

    def vec(a):
        return a.reshape(1, -1)

    small = [attn_norm_g, b_forget, q_norm_g, kv_norm_g, fox_out_g, mla_out_g, mlp_norm_g, final_norm_g]
    small_m = [m_attn_norm_g, m_b_forget, m_q_norm_g, m_kv_norm_g, m_fox_out_g, m_mla_out_g, m_mlp_norm_g, m_final_norm_g]
    small_v = [v_attn_norm_g, v_b_forget, v_q_norm_g, v_kv_norm_g, v_fox_out_g, v_mla_out_g, v_mlp_norm_g, v_final_norm_g]
    gains = [vec(a) for a in small]

    views = [big[0][0].T, _pad_heads(big[1][0], HEAD_DIM + ROPE_DIM)] + [w[0] for w in big[2:]]
    shards = [v.astype(BF16) for v in views]

    def with_own(gathered, mine):
        return [lax.dynamic_update_index_in_dim(g, s, chip, 0) for g, s in zip(gathered, mine)]

    def early_weights(gathered):
        g_in, g_uq, g_ukv = with_own(gathered, shards[:n_early])
        return g_in, join_cols(g_uq), join_cols(g_ukv)

    def late_weights(gathered):
        g_o, g_up, g_down = with_own(gathered, shards[n_early:])
        return g_o.reshape(-1, g_o.shape[2]), join_cols(g_up), g_down.reshape(-1, g_down.shape[2])

    reduction = GradReduction(core_id, chip)
    loss, grad_x, _, _, d_small = local_step(
        x[0], positions[0], loss_target[0], gains, early_weights, late_weights, gather_side(shards[:n_early]),
        (gather_side(shards[n_early:-1]), gather_side(shards[-1:])), reduction)
    halves = reduction.halves["early"] + reduction.halves["late"]
    others = reduction.others["early"] + reduction.others["late"]

    def rows8(vs):
        return jnp.concatenate([_pad_lanes(vec(a).astype(F32), 1024) for a in vs], axis=0)

    with_loss = [jnp.concatenate([d, loss], axis=1) if n == 1 else d for n, d in enumerate(d_small)]
    g_small8 = allreduce_small(rows8(with_loss))

    outs_big = []
    for n, (w, gm, go, m, v) in enumerate(zip(big, halves, others, big_m, big_v)):
        if n == 0:
            outs = adamw_halves_t(*(jnp.transpose(a, (2, 0, 1)) for a in (w, m, v)), gm, go, core, name="adamw_%d" % n)
            outs_big.append([jnp.transpose(o, (1, 2, 0)) for o in outs])
        elif n == 1:
            gm, go = (_unpad_heads(gh, HEAD_DIM + ROPE_DIM) for gh in (gm, go))
            g_t = jnp.where(core[0] == 0, jnp.concatenate([gm, go], axis=0), jnp.concatenate([go, gm], axis=0)).T
            outs = (g_t,) + tuple(adamw_whole(w[0].T, g_t, m[0].T, v[0].T, name="adamw_%d" % n))
            outs_big.append([o.T[None] for o in outs])
        else:
            outs_big.append(adamw_halves(w, gm, go, m, v, core, name="adamw_%d" % n))
    outs_small = adamw_vectors(gains, g_small8, [vec(a) for a in small_m], [vec(a) for a in small_v], name="adamw_small")

    loss_all = g_small8[1, N_HEADS]
    grads, deltas, new_m, new_v = [None] * 14, [None] * 14, [None] * 14, [None] * 14
    big_at = [1, 4, 6, 9, 11, 12]
    small_at = [0, 2, 3, 5, 7, 8, 10, 13]
    for n, at in enumerate(big_at):
        grads[at], deltas[at], new_m[at], new_v[at] = outs_big[n]
    for at, s, outs in zip(small_at, small, outs_small):
        grads[at], deltas[at], new_m[at], new_v[at] = (o.reshape(s.shape) for o in outs)
    return (loss_all, grad_x[None], *grads, *deltas, *new_m, *new_v)
```

```python
import jax
import jax.numpy as jnp
from jax import lax
from jax.experimental import pallas as pl
from jax.experimental.pallas import tpu as pltpu

F32 = jnp.float32
BF16 = jnp.bfloat16
MESH = pl.DeviceIdType.MESH

EPS = 1e-6
ROPE_THETA = 10000.0
N_HEADS = 8
PAIRS = N_HEADS // 2
HEAD_DIM = 64
ROPE_DIM = 32
LANES = 128
Q_RANK = 384
KV_RANK = 256
N_CHIPS = 4
ADAM_LR, ADAM_B1, ADAM_B2, ADAM_EPS, ADAM_WD, ADAM_STEP = 0.001, 0.9, 0.999, 1e-08, 0.01, 10
VMEM_LIMIT = 48 * 1024 * 1024
ADAMW_BLOCK = 64 * 1024
LOG2E = 1.4426950408889634
LN2 = 0.6931471805599453
NN = (((1,), (0,)), ((), ()))
NT = (((1,), (1,)), ((), ()))
TN = (((0,), (0,)), ((), ()))


def _params(sem=None):
    return pltpu.CompilerParams(dimension_semantics=sem, vmem_limit_bytes=VMEM_LIMIT)


def _fit(block, dim):
    if dim <= block:
        return dim
    return next(b for b in range(block - block % LANES, 0, -LANES) if dim % b == 0)


def _row_block(rows):
    return next(b for b in (256, 128, 64, 32, 16, 8) if rows % b == 0)


def gridded(body, *, name, grid, in_specs, out_specs, out_shape, ins, semantics, side=None):
    if side is None:
        return pl.pallas_call(body, name=name, grid=grid, in_specs=in_specs, out_specs=out_specs, out_shape=out_shape,
                              compiler_params=_params(semantics))(*ins)
    n_in, n_out = len(in_specs), len(out_specs)
    steps = 1
    for extent in grid:
        steps *= extent

    def riding(*refs):
        main_in, s_in, main_out, s_out, sems = _split_refs(refs, [n_in, len(side.ins), n_out, len(side.out_shapes), 2])
        step = 0
        for axis, extent in enumerate(grid):
            step = step * extent + pl.program_id(axis)

        @pl.when(step == 0)
        def _():
            side.first(s_in, s_out, *sems)

        body(*main_in, *main_out)

        @pl.when(step == steps - 1)
        def _():
            if side.mid is not None:
                side.mid(s_in, s_out, *sems)
            side.last(s_in, s_out, *sems)

    s_in_specs, s_out_specs = side.specs()
    return pl.pallas_call(
        riding, name=name, grid=grid, in_specs=list(in_specs) + s_in_specs, out_specs=list(out_specs) + s_out_specs,
        out_shape=list(out_shape) + side.out_shapes, scratch_shapes=side.sems(),
        compiler_params=_params(("arbitrary",) * len(grid)))(*ins, *side.ins)


def rmsnorm(x, g, *, out_dtype, name, bt=512, side=None):
    t, d = x.shape
    bt = min(bt, t)

    def body(x_ref, g_ref, o_ref):
        xv = x_ref[...].astype(F32)
        r = lax.rsqrt(jnp.mean(xv * xv, axis=-1, keepdims=True) + EPS)
        o_ref[...] = (xv * r * g_ref[...]).astype(o_ref.dtype)

    return gridded(
        body, name=name, grid=(t // bt,),
        in_specs=[pl.BlockSpec((bt, d), lambda i: (i, 0)), pl.BlockSpec((1, d), lambda i: (0, 0))],
        out_specs=[pl.BlockSpec((bt, d), lambda i: (i, 0))],
        out_shape=[jax.ShapeDtypeStruct((t, d), out_dtype)],
        ins=[x, g], semantics=("parallel",), side=side)


def mm(a, b, *, trans_b=False, a_pro=None, extras=(), vecs=(), epilogue=None, out_dtypes, n_sums=0, t_blk=None, name,
       bm=1024, bn=1024, side=None):
    a_list = list(a) if isinstance(a, (list, tuple)) else [a]
    b_list = list(b) if isinstance(b, (list, tuple)) else [b]
    m = a_list[0].shape[0]
    n = b_list[0].shape[0] if trans_b else b_list[0].shape[1]
    ks = [x.shape[1] for x in a_list]
    if sum(ks) > 2048:
        bm = bm // 2
    bm, bn = _fit(bm, m), _fit(bn, n)
    assert n_sums == 0 or bn == n
    n_ab, n_ex, n_vec, n_out = len(a_list), len(extras), len(vecs), len(out_dtypes)
    n_t = 0 if t_blk is None else 1

    def body(*refs):
        a_refs, b_refs, ex, vs, outs, t_outs, sums = _split_refs(refs, [n_ab, n_ab, n_ex, n_vec, n_out, n_t, n_sums])
        acc = None
        for a_ref, b_ref in zip(a_refs, b_refs):
            a_tile = a_ref[...] if a_pro is None else a_pro(a_ref[...])
            part = lax.dot_general(a_tile, b_ref[...], NT if trans_b else NN, preferred_element_type=F32)
            acc = part if acc is None else acc + part
        res = epilogue(acc, *[e[...] for e in ex], *[v[...] for v in vs]) if epilogue is not None else (acc,)
        for o, r in zip(outs, res[:n_out]):
            o[...] = r.astype(o.dtype)
        for t_ref in t_outs:
            for u in range(bm // t_blk):
                t_ref[u] = res[0][u * t_blk:(u + 1) * t_blk, :].T.astype(t_ref.dtype)
        if n_sums:
            @pl.when(pl.program_id(0) == 0)
            def _():
                for s_ref in sums:
                    s_ref[...] = jnp.zeros_like(s_ref)

            for s_ref, r in zip(sums, res[n_out:]):
                s_ref[...] += r

    tile = pl.BlockSpec((bm, bn), lambda i, j: (i, j))
    vec = pl.BlockSpec((1, bn), lambda i, j: (0, j))
    t_specs, t_shapes = [], []
    if t_blk is not None:
        t_specs = [pl.BlockSpec((bm // t_blk, bn, t_blk), lambda i, j: (i, j, 0))]
        t_shapes = [jax.ShapeDtypeStruct((m // t_blk, n, t_blk), out_dtypes[0])]
    a_specs = [pl.BlockSpec((bm, k), lambda i, j: (i, 0)) for k in ks]
    b_specs = [pl.BlockSpec((bn, k), lambda i, j: (j, 0)) if trans_b else pl.BlockSpec((k, bn), lambda i, j: (0, j)) for k in ks]
    return gridded(
        body, name=name, grid=(m // bm, n // bn),
        in_specs=a_specs + b_specs + [tile] * n_ex + [vec] * n_vec,
        out_specs=[tile] * n_out + t_specs + [vec] * n_sums,
        out_shape=[jax.ShapeDtypeStruct((m, n), dt) for dt in out_dtypes] + t_shapes + [jax.ShapeDtypeStruct((1, n), F32)] * n_sums,
        ins=[*a_list, *b_list, *extras, *vecs],
        semantics=("arbitrary", "arbitrary") if n_sums else ("parallel", "parallel"), side=side)


def mm_tn(a, b, *, a_pro=None, name, col_shards=1, bf16_copy=False, bk=1024, bn=1024, bt=2048):
    t, k = a.shape
    n = b.shape[1]
    ns = n // col_shards
    bk, bn, bt = _fit(bk, k), _fit(bn, ns), _fit(bt, t)
    per = ns // bn
    last = t // bt - 1

    def body(a_ref, b_ref, o_ref, *copy_ref):
        @pl.when(pl.program_id(2) == 0)
        def _():
            o_ref[...] = jnp.zeros_like(o_ref)

        a_tile = a_ref[...] if a_pro is None else a_pro(a_ref[...])
        o_ref[...] += lax.dot_general(a_tile, b_ref[...], TN, preferred_element_type=F32)
        if bf16_copy:
            @pl.when(pl.program_id(2) == last)
            def _():
                copy_ref[0][...] = o_ref[...].astype(BF16)

    if col_shards == 1:
        out_spec = pl.BlockSpec((bk, bn), lambda i, j, s: (i, j))
        shape = (k, n)
    else:
        out_spec = pl.BlockSpec((None, bk, bn), lambda i, j, s: (j // per, i, j % per))
        shape = (col_shards, k, ns)
    out_specs, out_shape = out_spec, jax.ShapeDtypeStruct(shape, F32)
    if bf16_copy:
        out_specs, out_shape = [out_spec, out_spec], [out_shape, jax.ShapeDtypeStruct(shape, BF16)]
    return pl.pallas_call(
        body, name=name, grid=(k // bk, n // bn, t // bt),
        in_specs=[pl.BlockSpec((bt, bk), lambda i, j, s: (s, i)), pl.BlockSpec((bt, bn), lambda i, j, s: (s, j))],
        out_specs=out_specs, out_shape=out_shape,
        compiler_params=_params(("parallel", "parallel", "arbitrary")),
    )(a, b)


def _split3(x):
    hi = x.astype(BF16)
    r1 = x - hi.astype(F32)
    mid = r1.astype(BF16)
    lo = (r1 - mid.astype(F32)).astype(BF16)
    return hi, mid, lo


def cumsum_rows(x, *, reverse, name, bc=512):
    t, d = x.shape
    bc = min(bc, t)
    nb = t // bc

    def body(x_ref, o_ref, carry):
        @pl.when(pl.program_id(0) == 0)
        def _():
            carry[...] = jnp.zeros_like(carry)

        r = lax.broadcasted_iota(jnp.int32, (bc, bc), 0)
        c = lax.broadcasted_iota(jnp.int32, (bc, bc), 1)
        tri = jnp.where((r <= c) if reverse else (r >= c), 1.0, 0.0).astype(BF16)
        hi, mid, lo = _split3(x_ref[...])
        s = (lax.dot_general(tri, hi, NN, preferred_element_type=F32)
             + lax.dot_general(tri, mid, NN, preferred_element_type=F32)
             + lax.dot_general(tri, lo, NN, preferred_element_type=F32)) + carry[0:1, :]
        o_ref[...] = s
        carry[0:1, :] = s[0:1, :] if reverse else s[bc - 1:bc, :]

    imap = (lambda i: (nb - 1 - i, 0)) if reverse else (lambda i: (i, 0))
    return pl.pallas_call(
        body, name=name, grid=(nb,),
        in_specs=[pl.BlockSpec((bc, d), imap)], out_specs=pl.BlockSpec((bc, d), imap),
        out_shape=jax.ShapeDtypeStruct((t, d), F32),
        scratch_shapes=[pltpu.VMEM((8, d), F32)],
        compiler_params=_params(("arbitrary",)),
    )(x)


def _rope(x, c, a, b):
    return x * c + pltpu.roll(x, LANES - ROPE_DIM // 2, 1) * a + pltpu.roll(x, ROPE_DIM // 2, 1) * b


def _rope_bwd(d, c, a, b):
    return d * c + pltpu.roll(d * a, ROPE_DIM // 2, 1) + pltpu.roll(d * b, LANES - ROPE_DIM // 2, 1)


S_CQ, S_CKV, S_KR, S_F, S_END = 0, Q_RANK, Q_RANK + KV_RANK, Q_RANK + KV_RANK + LANES, 1024
HW = N_HEADS * LANES
FW = N_HEADS * HEAD_DIM


def _rope_tables(pos_col, inv_row):
    ang = pos_col * inv_row
    cos, sin = jnp.cos(ang), jnp.sin(ang)
    lane = lax.broadcasted_iota(jnp.int32, (1, LANES), 1)
    first = (lane >= HEAD_DIM) & (lane < HEAD_DIM + ROPE_DIM // 2)
    second = (lane >= HEAD_DIM + ROPE_DIM // 2) & (lane < HEAD_DIM + ROPE_DIM)
    return (jnp.where(lane < HEAD_DIM, 1.0, jnp.where(first | second, cos, 0.0)), jnp.where(first, -sin, 0.0),
            jnp.where(second, sin, 0.0))


def mla_prep(small, g_q, g_kv, w_uq, w_ukv, pos_col, inv_row, b_f, *, name, bt=512):
    t = small.shape[0]
    bt = min(bt, t)

    def body(s_ref, gq_ref, gkv_ref, wq_ref, wkv_ref, pos_ref, inv_ref, bf_ref,
             mq_ref, mk_ref, mv_ref, lf_ref, cqn_ref, ckvn_ref, mqt_ref, mvt_ref):
        cq = s_ref[:, S_CQ:S_CKV]
        rq = lax.rsqrt(jnp.mean(cq * cq, axis=-1, keepdims=True) + EPS)
        cqn = (cq * rq * gq_ref[...]).astype(BF16)
        ckv = s_ref[:, S_CKV:S_KR]
        rkv = lax.rsqrt(jnp.mean(ckv * ckv, axis=-1, keepdims=True) + EPS)
        ckvn = (ckv * rkv * gkv_ref[...]).astype(BF16)
        cqn_ref[...] = cqn
        ckvn_ref[...] = ckvn
        tc, ta, tb = _rope_tables(pos_ref[...], inv_ref[...])
        q = jnp.dot(cqn, wq_ref[...], preferred_element_type=F32)
        kv = jnp.dot(ckvn, wkv_ref[...], preferred_element_type=F32)
        kr = _rope(s_ref[:, S_KR:S_F], tc, ta, tb)
        for h in range(N_HEADS):
            sl = slice(h * LANES, (h + 1) * LANES)
            roped = _rope(q[:, sl], tc, ta, tb)
            mq_ref[:, sl] = roped.astype(BF16)
            mqt_ref[0, sl, :] = roped.T.astype(BF16)
            mk_ref[:, sl] = (kv[:, sl] + kr).astype(BF16)
        mv_ref[...] = kv[:, HW:].astype(BF16)
        mvt_ref[0] = kv[:, HW:].T.astype(BF16)
        z = s_ref[:, S_F:S_END - LANES] + bf_ref[...]
        lf_ref[...] = jnp.minimum(z, 0.0) - jnp.log(1.0 + jnp.exp(-jnp.abs(z)))

    def row(w):
        return pl.BlockSpec((bt, w), lambda i: (i, 0))

    def full(arr):
        return pl.BlockSpec(arr.shape, lambda i: (0, 0))

    return pl.pallas_call(
        body, name=name, grid=(t // bt,),
        in_specs=[row(S_END), full(g_q), full(g_kv), full(w_uq), full(w_ukv), row(1), full(inv_row), full(b_f)],
        out_specs=[row(HW), row(HW), row(FW), row(LANES), row(Q_RANK), row(KV_RANK),
                   pl.BlockSpec((1, HW, bt), lambda i: (i, 0, 0)), pl.BlockSpec((1, FW, bt), lambda i: (i, 0, 0))],
        out_shape=[jax.ShapeDtypeStruct((t, HW), BF16)] * 2 + [jax.ShapeDtypeStruct((t, FW), BF16), jax.ShapeDtypeStruct((t, LANES), F32),
                   jax.ShapeDtypeStruct((t, Q_RANK), BF16), jax.ShapeDtypeStruct((t, KV_RANK), BF16),
                   jax.ShapeDtypeStruct((t // bt, HW, bt), BF16), jax.ShapeDtypeStruct((t // bt, FW, bt), BF16)],
        compiler_params=_params(("parallel",)),
    )(small, g_q, g_kv, w_uq, w_ukv, pos_col, inv_row, b_f)


def mla_prep_bwd(dmq, dmk, dmv, dlf, small, g_q, g_kv, w_uq, w_ukv, pos_col, inv_row, b_f, *, name, bt=512):
    t = small.shape[0]
    bt = min(bt, t)

    def body(dmq_ref, dmk_ref, dmv_ref, dlf_ref, s_ref, gq_ref, gkv_ref, wq_ref, wkv_ref, pos_ref, inv_ref, bf_ref,
             ds_ref, dq_ref, dkv_ref, dgq_ref, dgkv_ref, db_ref):
        tc, ta, tb = _rope_tables(pos_ref[...], inv_ref[...])
        lane = lax.broadcasted_iota(jnp.int32, (1, LANES), 1)
        dkr = jnp.zeros((bt, LANES), F32)
        for h in range(N_HEADS):
            sl = slice(h * LANES, (h + 1) * LANES)
            dq_ref[:, sl] = _rope_bwd(dmq_ref[:, sl], tc, ta, tb).astype(BF16)
            dkr = dkr + dmk_ref[:, sl]
        dkv_ref[:, :HW] = dmk_ref[...].astype(BF16)
        dkv_ref[:, HW:] = dmv_ref[...].astype(BF16)
        in_rope = (lane >= HEAD_DIM) & (lane < HEAD_DIM + ROPE_DIM)
        ds_ref[:, S_KR:S_F] = jnp.where(in_rope, _rope_bwd(dkr, tc, ta, tb), 0.0).astype(BF16)

        def norm_bwd(raw, g_ref, dn, dg_ref):
            r = lax.rsqrt(jnp.mean(raw * raw, axis=-1, keepdims=True) + EPS)
            u = dn * g_ref[...]
            dot = jnp.mean(u * raw, axis=-1, keepdims=True)
            dg_ref[...] += jnp.sum(dn * (raw * r), axis=0, keepdims=True)
            return r * u - raw * (r * r * r * dot)

        @pl.when(pl.program_id(0) == 0)
        def _():
            dgq_ref[...] = jnp.zeros_like(dgq_ref)
            dgkv_ref[...] = jnp.zeros_like(dgkv_ref)
            db_ref[...] = jnp.zeros_like(db_ref)

        dcqn = lax.dot_general(dq_ref[...], wq_ref[...], NT, preferred_element_type=F32)
        ds_ref[:, S_CQ:S_CKV] = norm_bwd(s_ref[:, S_CQ:S_CKV], gq_ref, dcqn, dgq_ref).astype(BF16)
        dckvn = lax.dot_general(dkv_ref[...], wkv_ref[...], NT, preferred_element_type=F32)
        ds_ref[:, S_CKV:S_KR] = norm_bwd(s_ref[:, S_CKV:S_KR], gkv_ref, dckvn, dgkv_ref).astype(BF16)
        z = s_ref[:, S_F:S_END - LANES] + bf_ref[...]
        dz = jnp.where(lane < N_HEADS, dlf_ref[...] / (1.0 + jnp.exp(z)), 0.0)
        db_ref[...] += jnp.sum(dz, axis=0, keepdims=True)
        ds_ref[:, S_F:S_END - LANES] = dz.astype(BF16)
        ds_ref[:, S_END - LANES:] = jnp.zeros((bt, LANES), BF16)

    def row(w):
        return pl.BlockSpec((bt, w), lambda i: (i, 0))

    def full(arr):
        return pl.BlockSpec(arr.shape, lambda i: (0, 0))

    def vec(w):
        return pl.BlockSpec((1, w), lambda i: (0, 0))

    return pl.pallas_call(
        body, name=name, grid=(t // bt,),
        in_specs=[row(HW), row(HW), row(FW), row(LANES), row(S_END), full(g_q), full(g_kv), full(w_uq), full(w_ukv),
                  row(1), full(inv_row), full(b_f)],
        out_specs=[row(S_END), row(HW), row(HW + FW), vec(Q_RANK), vec(KV_RANK), vec(LANES)],
        out_shape=[jax.ShapeDtypeStruct((t, S_END), BF16), jax.ShapeDtypeStruct((t, HW), BF16),
                   jax.ShapeDtypeStruct((t, HW + FW), BF16), jax.ShapeDtypeStruct((1, Q_RANK), F32),
                   jax.ShapeDtypeStruct((1, KV_RANK), F32), jax.ShapeDtypeStruct((1, LANES), F32)],
        compiler_params=_params(("arbitrary",)),
    )(dmq, dmk, dmv, dlf, small, g_q, g_kv, w_uq, w_ukv, pos_col, inv_row, b_f)


class Side:
    def __init__(self, ins, out_shapes, n_sems, first, last, mid=None):
        self.ins, self.out_shapes, self.n_sems = list(ins), list(out_shapes), n_sems
        self.first, self.mid, self.last = first, mid, last

    def specs(self):
        return [ANY] * len(self.ins), [ANY] * len(self.out_shapes)

    def sems(self):
        return [pltpu.SemaphoreType.DMA((self.n_sems,)), pltpu.SemaphoreType.DMA((self.n_sems,))]


def _lane():
    return lax.broadcasted_iota(jnp.int32, (1, LANES), 1)


def _halves(x):
    zero = jnp.zeros_like(x)
    return [jnp.where(_lane() < HEAD_DIM, x, zero), jnp.where(_lane() >= HEAD_DIM, x, zero)]


def _groups(x):
    return [x[:, :LANES], x[:, LANES:]]


def _pick_row(tile, h):
    row = lax.broadcasted_iota(jnp.int32, (tile.shape[0], 1), 0)
    return jnp.sum(jnp.where(row == h, tile, 0.0), axis=0, keepdims=True)


def _pick_lane(tile, h):
    return jnp.sum(jnp.where(_lane() == h, tile, 0.0), axis=1, keepdims=True)


def _row_halves(x):
    row = lax.broadcasted_iota(jnp.int32, (LANES, 1), 0)
    zero = jnp.zeros_like(x)
    return [jnp.where(row < HEAD_DIM, x, zero), jnp.where(row >= HEAD_DIM, x, zero)]


def _below_diagonal(s):
    r = lax.broadcasted_iota(jnp.int32, s.shape, 0)
    c = lax.broadcasted_iota(jnp.int32, s.shape, 1)
    return jnp.where(c <= r, s, -jnp.inf)


def _above_diagonal(s):
    r = lax.broadcasted_iota(jnp.int32, s.shape, 0)
    c = lax.broadcasted_iota(jnp.int32, s.shape, 1)
    return jnp.where(r <= c, s, -jnp.inf)


def _split_refs(refs, counts):
    out, at = [], 0
    for n in counts:
        out.append(refs[at:at + n])
        at += n
    return out


def flash_fwd(qt_arr, k_arr, vt_arr, f_cum, *, qoff, koff, voff, pair, scale, name, blk=512, side=None):
    t = k_arr.shape[0]
    tblk = qt_arr.shape[2]
    blk = max(min(blk, t), tblk)
    sub = blk // tblk
    nb = t // blk
    steps = PAIRS * nb
    w = LANES if pair else 2 * LANES
    has_bias = f_cum is not None
    ins = [qt_arr, k_arr, vt_arr] + ([f_cum] if has_bias else [])

    def wide(ref, first):
        parts = [ref[first + u] for u in range(sub)]
        return parts[0] if sub == 1 else jnp.concatenate(parts, axis=1)
    s_ins, s_outs = (side.ins, side.out_shapes) if side else ([], [])

    def body(*refs):
        main, si, outs, so, sems = _split_refs(refs, [len(ins), len(s_ins), 2, len(s_outs), 2 if side else 0])
        qt_ref, k_ref, vt_ref = main[:3]
        f_ref = main[3] if has_bias else None
        o_ref, st_ref = outs
        g, i = pl.program_id(0), pl.program_id(1)
        step_id = g * nb + i
        if side:
            @pl.when(step_id == 0)
            def _():
                side.first(si, so, *sems)

            if side.mid is not None:
                @pl.when(step_id == (3 * steps) // 4)
                def _():
                    side.mid(si, so, *sems)

        qt = (wide(qt_ref, 0).astype(F32) * (scale * LOG2E)).astype(BF16)
        qts = _row_halves(qt) if pair else [qt[:LANES], qt[LANES:]]

        def k_of(kk, n):
            return kk if pair else kk[:, n * LANES:(n + 1) * LANES]

        def with_ones(vt_rows):
            return jnp.concatenate([vt_rows, jnp.ones((ACC_ROWS - HEAD_DIM, vt_rows.shape[1]), BF16)], axis=0)

        def step(j, carry, diagonal):
            rows = pl.ds(pl.multiple_of(j * blk, blk), blk)
            kk = k_ref[rows, :]
            vt = wide(vt_ref, sub * j)
            out = []
            for n in range(2):
                m, acc = carry[n]
                s = jnp.dot(k_of(kk, n), qts[n], preferred_element_type=F32)
                if has_bias:
                    s = s - LOG2E * _pick_lane(f_ref[rows, :], 2 * g + n)
                if diagonal:
                    s = _above_diagonal(s)
                m_new = jnp.maximum(m, jnp.max(s, axis=0, keepdims=True))
                p = jnp.exp2(s - m_new).astype(BF16)
                out.append((m_new, jnp.exp2(m - m_new) * acc
                            + jnp.dot(with_ones(vt[n * HEAD_DIM:(n + 1) * HEAD_DIM]), p, preferred_element_type=F32)))
            return tuple(out)

        def diagonal_in_halves(carry):
            h = tblk
            halves = [pl.ds(pl.multiple_of(i * blk, blk), h), pl.ds(pl.multiple_of(i * blk + h, h), h)]
            k_top, k_bot = k_ref[halves[0], :], k_ref[halves[1], :]
            vt_top, vt_bot = vt_ref[sub * i], vt_ref[sub * i + 1]
            out = []
            for n in range(2):
                m, acc = carry[n]
                heads = slice(n * HEAD_DIM, (n + 1) * HEAD_DIM)
                s_top = jnp.dot(k_of(k_top, n), qts[n], preferred_element_type=F32)
                s_bot = jnp.dot(k_of(k_bot, n), qts[n][:, h:], preferred_element_type=F32)
                if has_bias:
                    s_top = s_top - LOG2E * _pick_lane(f_ref[halves[0], :], 2 * g + n)
                    s_bot = s_bot - LOG2E * _pick_lane(f_ref[halves[1], :], 2 * g + n)
                s_top, s_bot = _above_diagonal(s_top), _above_diagonal(s_bot)
                m_top = jnp.maximum(m, jnp.max(s_top, axis=0, keepdims=True))
                m_new = jnp.concatenate([m_top[:, :h], jnp.maximum(m_top[:, h:], jnp.max(s_bot, axis=0, keepdims=True))], axis=1)
                p_top = jnp.exp2(s_top - m_new).astype(BF16)
                p_bot = jnp.exp2(s_bot - m_new[:, h:]).astype(BF16)
                late = jnp.concatenate([jnp.zeros((ACC_ROWS, h), F32),
                                        jnp.dot(with_ones(vt_bot[heads]), p_bot, preferred_element_type=F32)], axis=1)
                out.append((m_new, jnp.exp2(m - m_new) * acc
                            + jnp.dot(with_ones(vt_top[heads]), p_top, preferred_element_type=F32) + late))
            return tuple(out)

        init = tuple((jnp.full((1, blk), -jnp.inf, F32), jnp.zeros((ACC_ROWS, blk), F32)) for _ in range(2))
        carry = lax.fori_loop(0, i, lambda j, c: step(j, c, False), init)
        (ma, acca), (mb, accb) = diagonal_in_halves(carry) if sub == 2 else step(i, carry, True)
        la, lb = acca[HEAD_DIM:HEAD_DIM + 1], accb[HEAD_DIM:HEAD_DIM + 1]
        o_ref[...] = jnp.concatenate([acca[:HEAD_DIM] / la, accb[:HEAD_DIM] / lb], axis=0).T
        row = lax.broadcasted_iota(jnp.int32, (LANES, 1), 0)
        st_ref[0] = jnp.where(row == 0, ma + jnp.log2(la), jnp.where(row == 1, mb + jnp.log2(lb), 0.0)).T
        if side:
            @pl.when(step_id == steps - 1)
            def _():
                side.last(si, so, *sems)

    in_specs = [pl.BlockSpec((sub, w, tblk), lambda g, i: (i, qoff + g, 0)), pl.BlockSpec((t, w), lambda g, i: (0, koff + g)),
                pl.BlockSpec((t // tblk, LANES, tblk), lambda g, i: (0, voff + g, 0))]
    if has_bias:
        in_specs.append(pl.BlockSpec((t, LANES), lambda g, i: (0, 0)))
    s_in_specs, s_out_specs = side.specs() if side else ([], [])
    return pl.pallas_call(
        body, name=name, grid=(PAIRS, nb), in_specs=in_specs + s_in_specs,
        out_specs=[pl.BlockSpec((blk, LANES), lambda g, i: (i, g)), pl.BlockSpec((1, blk, LANES), lambda g, i: (g, i, 0))]
        + s_out_specs,
        out_shape=[jax.ShapeDtypeStruct((t, PAIRS * LANES), F32), jax.ShapeDtypeStruct((PAIRS, t, LANES), F32)] + list(s_outs),
        scratch_shapes=side.sems() if side else [],
        compiler_params=_params(("arbitrary", "arbitrary")),
    )(*ins, *s_ins)


def mix_norm(fo, mo, g_fo, g_mo, *, name, bt=512):
    t, d = fo.shape
    bt = min(bt, t)

    def body(fo_ref, mo_ref, gf_ref, gm_ref, o_ref):
        for n, (x_ref, g_ref) in enumerate(((fo_ref, gf_ref), (mo_ref, gm_ref))):
            xv = x_ref[...]
            r = lax.rsqrt(jnp.mean(xv * xv, axis=-1, keepdims=True) + EPS)
            o_ref[:, n * d:(n + 1) * d] = (xv * r * g_ref[...]).astype(BF16)

    row = pl.BlockSpec((bt, d), lambda i: (i, 0))
    vec = pl.BlockSpec((1, d), lambda i: (0, 0))
    return pl.pallas_call(
        body, name=name, grid=(t // bt,), in_specs=[row, row, vec, vec],
        out_specs=pl.BlockSpec((bt, 2 * d), lambda i: (i, 0)),
        out_shape=jax.ShapeDtypeStruct((t, 2 * d), BF16),
        compiler_params=_params(("parallel",)),
    )(fo, mo, g_fo, g_mo)


def mix_norm_bwd(dx1b, w_o, fo, mo, g_fo, g_mo, st_f, st_m, *, name, bt=512, side=None):
    t, d = fo.shape
    bt = min(bt, t)

    def body(dx_in_ref, w_ref, fo_ref, mo_ref, gf_ref, gm_ref, sf_ref, sm_ref, dfo_ref, dmo_ref, dgf_ref, dgm_ref,
             sfo_ref, smo_ref, dfot_ref, dmot_ref):
        @pl.when(pl.program_id(0) == 0)
        def _():
            dgf_ref[...] = jnp.zeros_like(dgf_ref)
            dgm_ref[...] = jnp.zeros_like(dgm_ref)

        dmixed = lax.dot_general(dx_in_ref[...], w_ref[...], NT, preferred_element_type=F32)
        groups = ((fo_ref, gf_ref, dfo_ref, dgf_ref, sf_ref, sfo_ref, dfot_ref),
                  (mo_ref, gm_ref, dmo_ref, dgm_ref, sm_ref, smo_ref, dmot_ref))
        for n, (x_ref, g_ref, dx_ref, dg_ref, st_ref, sto_ref, dxt_ref) in enumerate(groups):
            xv = x_ref[...]
            dhv = dmixed[:, n * d:(n + 1) * d]
            r = lax.rsqrt(jnp.mean(xv * xv, axis=-1, keepdims=True) + EPS)
            u = dhv * g_ref[...]
            dxf = r * u - xv * (r * r * r * jnp.mean(u * xv, axis=-1, keepdims=True))
            dxb = dxf.astype(BF16)
            dx_ref[...] = dxb
            dxt_ref[0] = dxf.T.astype(BF16)
            dg_ref[...] += jnp.sum(dhv * (xv * r), axis=0, keepdims=True)
            prod = xv * dxb.astype(F32)
            for g in range(PAIRS):
                grp = prod[:, g * LANES:(g + 1) * LANES]
                da = jnp.sum(jnp.where(_lane() < HEAD_DIM, grp, 0.0), axis=1, keepdims=True)
                db = jnp.sum(jnp.where(_lane() >= HEAD_DIM, grp, 0.0), axis=1, keepdims=True)
                sto_ref[g] = jnp.where(_lane() == 2, da, jnp.where(_lane() == 3, db, st_ref[g]))

    row = pl.BlockSpec((bt, d), lambda i: (i, 0))
    vec = pl.BlockSpec((1, d), lambda i: (0, 0))
    stat = pl.BlockSpec((PAIRS, bt, LANES), lambda i: (0, i, 0))
    return gridded(
        body, name=name, grid=(t // bt,),
        in_specs=[pl.BlockSpec((bt, dx1b.shape[1]), lambda i: (i, 0)), pl.BlockSpec(w_o.shape, lambda i: (0, 0)),
                  row, row, vec, vec, stat, stat],
        out_specs=[row, row, vec, vec, stat, stat] + [pl.BlockSpec((1, d, bt), lambda i: (i, 0, 0))] * 2,
        out_shape=[jax.ShapeDtypeStruct((t, d), BF16)] * 2 + [jax.ShapeDtypeStruct((1, d), F32)] * 2
        + [jax.ShapeDtypeStruct(st_f.shape, F32)] * 2 + [jax.ShapeDtypeStruct((t // bt, d, bt), BF16)] * 2,
        ins=[dx1b, w_o, fo, mo, g_fo, g_mo, st_f, st_m], semantics=("arbitrary",), side=side)


def flash_bwd(q_arr, qt_arr, k_arr, v_arr, do_arr, dot_arr, st, f_blocks, *, qoff, koff, voff, pair, scale, name, qblk=1024,
              side=None):
    t = q_arr.shape[0]
    blk = qt_arr.shape[2]
    qblk = max(min(qblk, t), blk)
    sub = qblk // blk
    nb, nbq = t // blk, t // qblk
    w = LANES if pair else 2 * LANES
    hw = w // 2
    has_bias = f_blocks is not None
    split = _halves if pair else _groups
    ins = [q_arr, qt_arr, k_arr, v_arr, do_arr, dot_arr, st] + ([f_blocks] if has_bias else [])
    n_out = 4 if has_bias else 3
    s_ins, s_outs = (side.ins, side.out_shapes) if side else ([], [])

    def wide(ref, first, count):
        parts = [ref[first + u] for u in range(count)]
        return parts[0] if count == 1 else jnp.concatenate(parts, axis=1)

    def body(*refs):
        main, si, outs, so, sems = _split_refs(refs, [len(ins), len(s_ins), n_out, len(s_outs), 2 if side else 0])
        q_ref, qt_ref, k_ref, v_ref, do_ref, dot_ref, st_ref = main[:7]
        dq_ref, dk_ref, dv_ref = outs[:3]
        g, j = pl.program_id(0), pl.program_id(1)
        step_id = g * nb + j
        if side:
            @pl.when(step_id == 0)
            def _():
                side.first(si, so, *sems)

        @pl.when(j == 0)
        def _():
            dq_ref[...] = jnp.zeros_like(dq_ref)

        kk, vv = k_ref[...], v_ref[...]
        ks = [kk, kk] if pair else _groups(kk)
        if has_bias:
            f_ref, df_ref = main[7], outs[3]
            fk = [LOG2E * _pick_row(f_ref[0], 2 * g + n) for n in range(2)]

            @pl.when(step_id == 0)
            def _():
                df_ref[...] = jnp.zeros_like(df_ref)

        def step(tb, count, carry, diagonal):
            rows = pl.ds(pl.multiple_of(tb * blk, blk), count * blk)
            qs = split((q_ref[rows, :].astype(F32) * (scale * LOG2E)).astype(BF16))
            qt = (wide(qt_ref, tb, count).astype(F32) * (scale * LOG2E)).astype(BF16)
            dos = [h.astype(BF16) for h in _halves(do_ref[rows, :].astype(F32))]
            dot = wide(dot_ref, tb, count)
            stats = st_ref[0, rows, :]
            new, dqs, row_sums = [], [], []
            for n in range(2):
                dkt, dvt, dfk = carry[n]
                s = lax.dot_general(qs[n], ks[n], NT, preferred_element_type=F32)
                if has_bias:
                    s = s - fk[n]
                if diagonal:
                    s = _below_diagonal(s)
                p = jnp.exp2(s - stats[:, n:n + 1])
                dp = lax.dot_general(dos[n], vv, NT, preferred_element_type=F32)
                ds = p * (dp - stats[:, 2 + n:3 + n])
                dsb = ds.astype(BF16)
                dvt = dvt + jnp.dot(dot[n * HEAD_DIM:(n + 1) * HEAD_DIM], p.astype(BF16), preferred_element_type=F32)
                dkt = dkt + jnp.dot(qt[n * hw:(n + 1) * hw], dsb, preferred_element_type=F32)
                dqs.append(jnp.dot(dsb, ks[n], preferred_element_type=F32))
                if has_bias:
                    dfk = dfk - jnp.sum(ds, axis=0, keepdims=True)
                    row_sums.append(jnp.sum(ds, axis=1, keepdims=True))
                new.append((dkt, dvt, dfk))
            dq = (jnp.where(_lane() < HEAD_DIM, dqs[0], dqs[1]) if pair else jnp.concatenate(dqs, axis=1)) * scale
            if has_bias:
                df_ref[rows, :] += jnp.where(_lane() == 2 * g, row_sums[0], jnp.where(_lane() == 2 * g + 1, row_sums[1], 0.0))
            dq_ref[rows, :] += dq
            return tuple(new)

        init = tuple((jnp.zeros((hw, blk), F32), jnp.zeros((HEAD_DIM, blk), F32), jnp.zeros((1, blk), F32)) for _ in range(2))
        carry = step(j, 1, init, True)
        whole = j // sub + 1
        carry = lax.fori_loop(j + 1, whole * sub, lambda tb, c: step(tb, 1, c, False), carry)
        (dka, dva, dfa), (dkb, dvb, dfb) = lax.fori_loop(whole, nbq, lambda i, c: step(i * sub, sub, c, False), carry)
        dk_ref[...] = (jnp.concatenate([dka, dkb], axis=0).T * LN2).astype(BF16)
        dv_ref[...] = jnp.concatenate([dva, dvb], axis=0).T.astype(BF16)
        if has_bias:
            row = lax.broadcasted_iota(jnp.int32, (LANES, 1), 0)
            by_head = jnp.where(row == 2 * g, dfa, jnp.where(row == 2 * g + 1, dfb, 0.0))
            df_ref[pl.ds(pl.multiple_of(j * blk, blk), blk), :] += by_head.T
        if side:
            @pl.when(step_id == PAIRS * nb - 1)
            def _():
                side.last(si, so, *sems)

    in_specs = [pl.BlockSpec((t, w), lambda g, j: (0, qoff + g)), pl.BlockSpec((nb, w, blk), lambda g, j: (0, qoff + g, 0)),
                pl.BlockSpec((blk, w), lambda g, j: (j, koff + g)), pl.BlockSpec((blk, LANES), lambda g, j: (j, voff + g)),
                pl.BlockSpec((t, LANES), lambda g, j: (0, g)), pl.BlockSpec((nb, LANES, blk), lambda g, j: (0, g, 0)),
                pl.BlockSpec((1, t, LANES), lambda g, j: (g, 0, 0))]
    out_specs = [pl.BlockSpec((t, w), lambda g, j: (0, g)), pl.BlockSpec((blk, w), lambda g, j: (j, g)),
                 pl.BlockSpec((blk, LANES), lambda g, j: (j, g))]
    out_shape = [jax.ShapeDtypeStruct((t, PAIRS * w), F32), jax.ShapeDtypeStruct((t, PAIRS * w), BF16),
                 jax.ShapeDtypeStruct((t, PAIRS * LANES), BF16)]
    if has_bias:
        in_specs.append(pl.BlockSpec((1, N_HEADS, blk), lambda g, j: (j, 0, 0)))
        out_specs.append(pl.BlockSpec((t, LANES), lambda g, j: (0, 0)))
        out_shape.append(jax.ShapeDtypeStruct((t, LANES), F32))
    s_in_specs, s_out_specs = side.specs() if side else ([], [])
    return pl.pallas_call(
        body, name=name, grid=(PAIRS, nb), in_specs=in_specs + s_in_specs, out_specs=out_specs + s_out_specs,
        out_shape=out_shape + list(s_outs), scratch_shapes=side.sems() if side else [],
        compiler_params=_params(("arbitrary", "arbitrary")),
    )(*ins, *s_ins)


def _adamw_math(w, g, m, v):
    nm = ADAM_B1 * m + (1.0 - ADAM_B1) * g
    nv = ADAM_B2 * v + (1.0 - ADAM_B2) * (g * g)
    m_hat = nm / (1.0 - ADAM_B1 ** ADAM_STEP)
    v_hat = nv / (1.0 - ADAM_B2 ** ADAM_STEP)
    return -ADAM_LR * (m_hat / (jnp.sqrt(v_hat) + ADAM_EPS) + ADAM_WD * w), nm, nv


def adamw_vectors(ws, g_rows, ms, vs, *, name):
    k = len(ws)

    def body(g_ref, *refs):
        w_refs, m_refs, v_refs, outs = _split_refs(refs, [k, k, k, 4 * k])
        for i in range(k):
            g = g_ref[i:i + 1, :w_refs[i].shape[1]]
            outs[4 * i][...] = g
            outs[4 * i + 1][...], outs[4 * i + 2][...], outs[4 * i + 3][...] = _adamw_math(
                w_refs[i][...], g, m_refs[i][...], v_refs[i][...])

    flat = pl.pallas_call(
        body, name=name, out_shape=[jax.ShapeDtypeStruct(w.shape, F32) for w in ws for _ in range(4)],
        compiler_params=_params(),
    )(g_rows, *ws, *ms, *vs)
    return [flat[4 * i:4 * i + 4] for i in range(k)]


def adamw_whole(w, g, m, v, *, name):
    def body(w_ref, g_ref, m_ref, v_ref, d_ref, nm_ref, nv_ref):
        d_ref[...], nm_ref[...], nv_ref[...] = _adamw_math(w_ref[...], g_ref[...], m_ref[...], v_ref[...])

    return pl.pallas_call(body, name=name, out_shape=[jax.ShapeDtypeStruct(w.shape, F32)] * 3,
                          compiler_params=_params())(w, g, m, v)


def adamw_halves(w, g_mine, g_other, m, v, core, *, name):
    _, k, n = w.shape
    br = min(_row_block(k // 2), max(8, ADAMW_BLOCK // n))
    nh = k // 2 // br

    def body(c_ref, w_ref, gm_ref, go_ref, m_ref, v_ref, g_out, d_ref, nm_ref, nv_ref):
        gv = jnp.where(pl.program_id(0) == c_ref[0], gm_ref[...], go_ref[...])
        g_out[0] = gv
        d_ref[0], nm_ref[0], nv_ref[0] = _adamw_math(w_ref[0], gv, m_ref[0], v_ref[0])

    full = pl.BlockSpec((1, br, n), lambda hb, i, c: (0, hb * nh + i, 0))
    half = pl.BlockSpec((br, n), lambda hb, i, c: (i, 0))
    return pl.pallas_call(
        body, name=name,
        grid_spec=pltpu.PrefetchScalarGridSpec(num_scalar_prefetch=1, grid=(2, nh), in_specs=[full, half, half, full, full],
                                               out_specs=[full] * 4),
        out_shape=[jax.ShapeDtypeStruct(w.shape, F32)] * 4,
        compiler_params=_params(("parallel", "parallel")),
    )(core, w, g_mine, g_other, m, v)


def adamw_halves_t(wt, mt, vt, gt_mine, gt_other, core, *, name, bc=128):
    n, _, k = wt.shape
    nh = k // 2 // bc

    def body(c_ref, w_ref, m_ref, v_ref, gm_ref, go_ref, g_out, d_ref, nm_ref, nv_ref):
        gv = jnp.where(pl.program_id(0) == c_ref[0], gm_ref[...], go_ref[...])
        g_out[:, 0, :] = gv
        d_ref[:, 0, :], nm_ref[:, 0, :], nv_ref[:, 0, :] = _adamw_math(w_ref[:, 0, :], gv, m_ref[:, 0, :], v_ref[:, 0, :])

    full = pl.BlockSpec((n, 1, bc), lambda hb, i, c: (0, 0, hb * nh + i))
    half = pl.BlockSpec((n, bc), lambda hb, i, c: (0, i))
    return pl.pallas_call(
        body, name=name,
        grid_spec=pltpu.PrefetchScalarGridSpec(num_scalar_prefetch=1, grid=(2, nh), in_specs=[full, full, full, half, half],
                                               out_specs=[full] * 4),
        out_shape=[jax.ShapeDtypeStruct(wt.shape, F32)] * 4,
        compiler_params=_params(("parallel", "parallel")),
    )(core, wt, mt, vt, gt_mine, gt_other)


def add_pair(dw, recv, core, *, name):
    n4, k, n = dw.shape
    half = (1, k // 2, n) if split_axis(k) == 0 else (1, k, n // 2)
    mine = (lambda q, c: (q, c[0], 0)) if split_axis(k) == 0 else (lambda q, c: (q, 0, c[0]))

    def body(c_ref, a_ref, b_ref, o_ref):
        o_ref[...] = (a_ref[...] + b_ref[...].astype(F32)).astype(BF16)

    return pl.pallas_call(
        body, name=name,
        grid_spec=pltpu.PrefetchScalarGridSpec(
            num_scalar_prefetch=1, grid=(n4,),
            in_specs=[pl.BlockSpec(half, mine), pl.BlockSpec(half, lambda q, c: (q, 0, 0))],
            out_specs=pl.BlockSpec(half, lambda q, c: (q, 0, 0))),
        out_shape=jax.ShapeDtypeStruct((n4,) + half[1:], BF16),
        compiler_params=_params(("parallel",)),
    )(core, dw, recv)


def sum_chips(parts, *, name):
    n4, r, n = parts.shape
    if r % 16 == 0:
        br, bc = _row_block(r), n
    else:
        br, bc = r, LANES

    def body(p_ref, o_ref):
        acc = p_ref[0].astype(F32)
        for q in range(1, n4):
            acc = acc + p_ref[q].astype(F32)
        o_ref[...] = acc

    return pl.pallas_call(
        body, name=name, grid=(r // br, n // bc),
        in_specs=[pl.BlockSpec((n4, br, bc), lambda i, j: (0, i, j))], out_specs=pl.BlockSpec((br, bc), lambda i, j: (i, j)),
        out_shape=jax.ShapeDtypeStruct((r, n), F32),
        compiler_params=_params(("parallel", "parallel")),
    )(parts)


ANY = pl.BlockSpec(memory_space=pl.ANY)


def _place():
    x, y, c = lax.axis_index("x"), lax.axis_index("y"), lax.axis_index("c")
    chips = [(1 - x, y), (x, 1 - y), (1 - x, 1 - y)]
    return x, y, c, chips


def _copy(src, dst, send_sems, recv_sems, k, to):
    return pltpu.make_async_remote_copy(src_ref=src, dst_ref=dst, send_sem=send_sems.at[k], recv_sem=recv_sems.at[k],
                                        device_id=to, device_id_type=MESH)


def split_axis(rows):
    return 0 if rows % 32 == 0 else 1


def _half(ref, lead, hf):
    rows, cols = ref.shape[-2:]
    if split_axis(rows) == 0:
        at = (pl.ds(hf * (rows // 2), rows // 2), slice(None))
    else:
        at = (slice(None), pl.ds(hf * (cols // 2), cols // 2))
    return ref.at[at] if lead is None else ref.at[(lead,) + at]


def _gather_first(srcs, dsts, ssems, rsems):
    x, y, c, chips = _place()
    for ti, (s, d) in enumerate(zip(srcs, dsts)):
        for j, (cx, cy) in enumerate(chips):
            _copy(_half(s, None, c), _half(d, 2 * x + y, c), ssems, rsems, 3 * ti + j, (cx, cy, c)).start()


def _gather_mid(srcs, dsts, ssems, rsems):
    x, y, c, chips = _place()
    n1 = 3 * len(srcs)
    for ti, d in enumerate(dsts):
        for j, (cx, cy) in enumerate(chips):
            landed = _half(d, 2 * cx + cy, c)
            _copy(landed, landed, ssems, rsems, 3 * ti + j, (cx, cy, c)).wait_recv()
            _copy(landed, landed, ssems, rsems, n1 + 3 * ti + j, (x, y, 1 - c)).start()


def _gather_last(srcs, dsts, ssems, rsems):
    x, y, c, chips = _place()
    n1 = 3 * len(srcs)
    for ti, (s, d) in enumerate(zip(srcs, dsts)):
        for j, (cx, cy) in enumerate(chips):
            other = _half(d, 2 * cx + cy, 1 - c)
            _copy(other, other, ssems, rsems, n1 + 3 * ti + j, (x, y, 1 - c)).wait_recv()
        for j, (cx, cy) in enumerate(chips):
            mine = _half(s, None, c)
            _copy(mine, mine, ssems, rsems, 3 * ti + j, (cx, cy, c)).wait_send()
            _copy(mine, mine, ssems, rsems, n1 + 3 * ti + j, (x, y, 1 - c)).wait_send()


def gather_side(shards):
    return Side(shards, [jax.ShapeDtypeStruct((N_CHIPS,) + s.shape, s.dtype) for s in shards], 6 * len(shards),
                _gather_first, _gather_last, _gather_mid)


def _scatter_first(srcs, dsts, ssems, rsems):
    x, y, c, chips = _place()
    for ti, (s, d) in enumerate(zip(srcs, dsts)):
        for j, (cx, cy) in enumerate(chips):
            _copy(s.at[2 * cx + cy], d.at[2 * x + y], ssems, rsems, 3 * ti + j, (cx, cy, c)).start()


def _scatter_last(srcs, dsts, ssems, rsems):
    x, y, c, chips = _place()
    for ti, (s, d) in enumerate(zip(srcs, dsts)):
        for j, (cx, cy) in enumerate(chips):
            _copy(s.at[2 * cx + cy], d.at[2 * cx + cy], ssems, rsems, 3 * ti + j, (cx, cy, c)).wait_recv()
        for j, (cx, cy) in enumerate(chips):
            _copy(s.at[2 * cx + cy], d.at[2 * cx + cy], ssems, rsems, 3 * ti + j, (cx, cy, c)).wait_send()


def scatter_side(parts):
    return Side(parts, [jax.ShapeDtypeStruct(p.shape, p.dtype) for p in parts], 3 * len(parts), _scatter_first, _scatter_last)


def run_side(side, *, name):
    n_in, n_out = len(side.ins), len(side.out_shapes)

    def body(*refs):
        si, so, sems = _split_refs(refs, [n_in, n_out, 2])
        side.first(si, so, *sems)
        if side.mid is not None:
            side.mid(si, so, *sems)
        side.last(si, so, *sems)

    in_specs, out_specs = side.specs()
    return pl.pallas_call(body, name=name, in_specs=in_specs, out_specs=out_specs, out_shape=side.out_shapes,
                          scratch_shapes=side.sems())(*side.ins)


def _swap_first(srcs, dsts, ssems, rsems):
    x, y, c, _ = _place()
    for k, (s, d) in enumerate(zip(srcs, dsts)):
        _copy(s, d, ssems, rsems, k, (x, y, 1 - c)).start()


def _swap_last(srcs, dsts, ssems, rsems):
    x, y, c, _ = _place()
    for k, (s, d) in enumerate(zip(srcs, dsts)):
        _copy(s, d, ssems, rsems, k, (x, y, 1 - c)).wait()


def swap_side(xs):
    return Side(xs, [jax.ShapeDtypeStruct(a.shape, a.dtype) for a in xs], len(xs), _swap_first, _swap_last)


def _swap_halves(srcs, dsts, ssems, rsems):
    x, y, c, _ = _place()
    for k, (s, d) in enumerate(zip(srcs, dsts)):
        hk = s.shape[1] // 2
        yield _copy(s.at[:, pl.ds((1 - c) * hk, hk), :], d, ssems, rsems, k, (x, y, 1 - c))


def _swap_halves_first(srcs, dsts, ssems, rsems):
    for cp in _swap_halves(srcs, dsts, ssems, rsems):
        cp.start()


def _swap_halves_last(srcs, dsts, ssems, rsems):
    for cp in _swap_halves(srcs, dsts, ssems, rsems):
        cp.wait()


def swap_halves_side(xs):
    return Side(xs, [jax.ShapeDtypeStruct((a.shape[0], a.shape[1] // 2, a.shape[2]), a.dtype) for a in xs], len(xs),
                _swap_halves_first, _swap_halves_last)


def allreduce_small(s):
    n_dev = 8

    def body(s_ref, out_ref, buf, send_sems, recv_sems):
        x, y, c, _ = _place()
        me = 4 * x + 2 * y + c
        buf[me] = s_ref[...]
        sends = []
        for k in range(1, n_dev):
            px = 1 - x if k & 4 else x
            py = 1 - y if k & 2 else y
            pc = 1 - c if k & 1 else c
            cp = _copy(s_ref, buf.at[me], send_sems, recv_sems, k - 1, (px, py, pc))
            cp.start()
            sends.append((cp, 4 * px + 2 * py + pc))
        for k, (cp, peer) in enumerate(sends):
            _copy(s_ref, buf.at[peer], send_sems, recv_sems, k, (x, y, c)).wait_recv()
        for cp, _ in sends:
            cp.wait_send()
        acc = buf[0]
        for d in range(1, n_dev):
            acc = acc + buf[d]
        out_ref[...] = acc

    vm = pl.BlockSpec(memory_space=pltpu.VMEM)
    return pl.pallas_call(
        body, name="allreduce_small", in_specs=[vm], out_specs=vm,
        out_shape=jax.ShapeDtypeStruct(s.shape, F32),
        scratch_shapes=[pltpu.VMEM((n_dev,) + s.shape, F32), pltpu.SemaphoreType.DMA((n_dev - 1,)),
                        pltpu.SemaphoreType.DMA((n_dev - 1,))],
    )(s)


def join_cols(sm):
    n4, k, n = sm.shape
    return sm.transpose(1, 0, 2).reshape(k, n4 * n)


def split_cols(full):
    k, n = full.shape
    return full.reshape(k, N_CHIPS, n // N_CHIPS).transpose(1, 0, 2)


def _pad_heads(w, width):
    lead, heads = w.shape[:-1], w.shape[-1] // width
    w = w.reshape(lead + (heads, width))
    return jnp.pad(w, [(0, 0)] * len(lead) + [(0, 0), (0, LANES - width)]).reshape(lead + (heads * LANES,))


def _unpad_heads(w, width):
    lead, heads = w.shape[:-1], w.shape[-1] // LANES
    return w.reshape(lead + (heads, LANES))[..., :width].reshape(lead + (heads * width,))


O_F = 3 * FW
O_CQ = O_F + N_HEADS
O_CKV = O_CQ + Q_RANK
O_KR = O_CKV + KV_RANK
O_END = O_KR + ROPE_DIM


def split_w_in_t(w_sm):
    n4, r, d = w_sm.shape
    segments = [(0, O_F, 0, 0), (O_F, N_HEADS, 1, S_F), (O_CQ, Q_RANK, 1, S_CQ), (O_CKV, KV_RANK, 1, S_CKV),
                (O_KR, ROPE_DIM, 1, S_KR + HEAD_DIM)]
    moves = []
    for v0, n, dst, d0 in segments:
        for q in range(n4):
            a, b = max(v0, q * r), min(v0 + n, (q + 1) * r)
            if a < b:
                assert a % 2 == 0 and b % 2 == 0 and (d0 + a - v0) % 2 == 0
                moves.append((q, (a - q * r) // 2, (b - a) // 2, dst, (d0 + a - v0) // 2))

    def body(w_ref, qkv_ref, small_ref, src, dst_qkv, dst_small):
        for q in range(n4):
            src[q] = pltpu.bitcast(w_ref[q], jnp.uint32)
        dst_small[...] = jnp.zeros(dst_small.shape, jnp.uint32)
        for q, a, n, dst, b in moves:
            (dst_qkv, dst_small)[dst][pl.ds(b, n), :] = src[q, pl.ds(a, n), :]
        qkv_ref[...] = pltpu.bitcast(dst_qkv[...], w_sm.dtype)
        small_ref[...] = pltpu.bitcast(dst_small[...], w_sm.dtype)

    return pl.pallas_call(
        body, name="split_w_in",
        out_shape=[jax.ShapeDtypeStruct((O_F, d), w_sm.dtype), jax.ShapeDtypeStruct((S_END, d), w_sm.dtype)],
        scratch_shapes=[pltpu.VMEM((n4, r // 2, d), jnp.uint32), pltpu.VMEM((O_F // 2, d), jnp.uint32),
                        pltpu.VMEM((S_END // 2, d), jnp.uint32)],
        compiler_params=_params(),
    )(w_sm)


def join_w_in_t(d_qkv_t, d_small_t):
    small = len(d_qkv_t)
    segments = [(n, 0, n * FW, FW) for n in range(small)]
    segments += [(small, S_F, O_F, N_HEADS), (small, S_CQ, O_CQ, Q_RANK), (small, S_CKV, O_CKV, KV_RANK),
                 (small, S_KR + HEAD_DIM, O_KR, ROPE_DIM)]
    r, d = O_END // N_CHIPS, d_small_t.shape[1]

    def body(*refs):
        o_ref = refs[-1]
        for src, s0, v0, n in segments:
            for q in range(N_CHIPS):
                a, b = max(v0, q * r), min(v0 + n, (q + 1) * r)
                if a < b:
                    o_ref[q, pl.ds(a - q * r, b - a), :] = refs[src][pl.ds(s0 + a - v0, b - a), :]

    return pl.pallas_call(
        body, name="join_w_in", out_shape=jax.ShapeDtypeStruct((N_CHIPS, r, d), F32), compiler_params=_params(),
    )(*d_qkv_t, d_small_t)


def rope_inputs(pos):
    inv_freq = ROPE_THETA ** (-jnp.arange(0, ROPE_DIM, 2, dtype=F32) / ROPE_DIM)
    inv_row = jnp.concatenate([jnp.zeros((HEAD_DIM,), F32), inv_freq, inv_freq, jnp.zeros((LANES - HEAD_DIM - ROPE_DIM,), F32)])
    return pos.astype(F32)[:, None], inv_row[None, :]


def _pad_lanes(v, n):
    return jnp.pad(v, ((0, 0), (0, n - v.shape[1])))


ATTN_BLK = 512
ATTN_FWD_BLK = 1024
ATTN_BWD_QBLK = 1024
ACC_ROWS = HEAD_DIM + 16


def local_step(xs, pos, tgt, gains, early_weights, late_weights, early_side=None, fwd_sides=(None, None), reduction=None):
    g_attn, b_forget, g_q, g_kv, g_fo, g_mo, g_mlp, g_fin = gains
    t = xs.shape[0]
    blk = min(ATTN_BLK, t)
    fox_scale = 1.0 / (HEAD_DIM ** 0.5)
    mla_scale = 1.0 / ((HEAD_DIM + ROPE_DIM) ** 0.5)

    h1, *gathered = rmsnorm(xs, g_attn, out_dtype=BF16, name="norm_attn", side=early_side)
    w_in_t, w_uq_p, w_ukv = early_weights(gathered)
    w_qkv_t, w_small_t = split_w_in_t(w_in_t)
    kv = w_ukv.reshape(KV_RANK, N_HEADS, 2 * HEAD_DIM)
    w_ukv_p = jnp.concatenate([_pad_heads(kv[:, :, :HEAD_DIM].reshape(KV_RANK, FW), HEAD_DIM),
                               kv[:, :, HEAD_DIM:].reshape(KV_RANK, FW)], axis=1)
    b_f = _pad_lanes(b_forget, LANES)
    pos_col, inv_row = rope_inputs(pos)

    qkv, qkv_t = mm(h1, w_qkv_t, trans_b=True, out_dtypes=[BF16], t_blk=blk, name="proj_qkv", bn=1536)
    small, = mm(h1, w_small_t, trans_b=True, out_dtypes=[F32], name="proj_small")
    mq, mk, mv, lf, cqn, ckvn, mq_t, mv_t = mla_prep(small, g_q, g_kv, w_uq_p, w_ukv_p, pos_col, inv_row, b_f,
                                                     name="mla_prep", bt=blk)
    f_cum = cumsum_rows(lf, reverse=False, name="gate_cumsum")
    f_blocks = f_cum[:, :N_HEADS].reshape(t // blk, blk, N_HEADS).transpose(0, 2, 1)
    fo, st_f, *gathered = flash_fwd(qkv_t, qkv, qkv_t, f_cum, qoff=0, koff=PAIRS, voff=2 * PAIRS, pair=True,
                                    scale=fox_scale, name="fox_fwd", blk=ATTN_FWD_BLK, side=fwd_sides[0])
    mo, st_m, *more = flash_fwd(mq_t, mk, mv_t, None, qoff=0, koff=0, voff=0, pair=False, scale=mla_scale, name="mla_fwd",
                                blk=ATTN_FWD_BLK, side=fwd_sides[1])
    w_o, w_up, w_down = late_weights(gathered + more)
    mixed = mix_norm(fo, mo, g_fo, g_mo, name="norm_mix")

    def inv_rms(v):
        return lax.rsqrt(jnp.mean(v * v, axis=-1, keepdims=True) + EPS)

    def residual_then_norm(acc, res, g):
        xn = acc + res
        return xn, xn * inv_rms(xn) * g

    def norm_bwd(dh, xn, res, g):
        r = inv_rms(xn)
        uu = dh * g
        return (r * uu - xn * (r * r * r * jnp.mean(uu * xn, axis=-1, keepdims=True)) + res,
                jnp.sum(dh * (xn * r), axis=0, keepdims=True))

    def norm_bwd2(dh, xn, res, g):
        dx, dg = norm_bwd(dh, xn, res, g)
        return dx, dx, dg

    def residual_then_loss(acc, res, target, g):
        xn = acc + res
        r = inv_rms(xn)
        xh = xn * r
        e = xh * g - target
        part = 0.5 * jnp.sum(jnp.mean(e * e, axis=-1, keepdims=True), axis=0, keepdims=True)
        dy = e * (1.0 / xn.shape[1])
        uu = dy * g
        dx = r * uu - xn * (r * r * r * jnp.mean(uu * xn, axis=-1, keepdims=True))
        return dx, dx, jnp.sum(dy * xh, axis=0, keepdims=True), part + jnp.zeros_like(g)

    x1, h2 = mm(mixed, w_o, extras=[xs], vecs=[g_mlp], epilogue=residual_then_norm, out_dtypes=[F32, BF16], name="out_proj")

    def relu2(uu):
        r = jnp.maximum(uu.astype(F32), 0.0)
        return (r * r).astype(BF16)

    u, = mm(h2, w_up, out_dtypes=[BF16], name="mlp_up", bn=2048)
    dx2, dx2b, dg_fin, loss_row = mm(u, w_down, a_pro=relu2, extras=[x1, tgt], vecs=[g_fin], epilogue=residual_then_loss,
                                     out_dtypes=[F32, BF16], n_sums=2, name="mlp_down_loss")
    loss = loss_row[:, :1]

    def relu2_grad(acc, uu):
        return (acc * (2.0 * jnp.maximum(uu.astype(F32), 0.0)),)

    du, = mm(dx2b, w_down, trans_b=True, extras=[u], epilogue=relu2_grad, out_dtypes=[BF16], name="mlp_down_bwd", bn=2048)
    dw_down, dw_down_b = (g.reshape(N_CHIPS, -1, w_down.shape[1])
                          for g in mm_tn(u, dx2b, a_pro=relu2, name="dw_down", bf16_copy=True))
    dx1, dx1b, dg_mlp = mm(du, w_up, trans_b=True, extras=[x1, dx2], vecs=[g_mlp], epilogue=norm_bwd2,
                           out_dtypes=[F32, BF16], n_sums=1, name="mlp_up_bwd")
    dw_up, dw_up_b = mm_tn(h2, du, name="dw_up", col_shards=N_CHIPS, bf16_copy=True)

    dw_o, dw_o_b = (g.reshape(N_CHIPS, -1, w_o.shape[1]) for g in mm_tn(mixed, dx1b, name="dw_o", bf16_copy=True))
    late, late_b = (dw_o, dw_up, dw_down), (dw_o_b, dw_up_b, dw_down_b)
    red = reduction
    dfo, dmo, dg_fo, dg_mo, st_f, st_m, dfo_t, dmo_t, *got = mix_norm_bwd(
        dx1b, w_o, fo, mo, g_fo, g_mo, st_f, st_m, name="out_proj_mix_bwd", bt=blk,
        side=red.late_swap(late, late_b) if red else None)
    dfq, dfk, dfv, d_f, *got = flash_bwd(
        qkv, qkv_t, qkv, qkv, dfo, dfo_t, st_f, f_blocks, qoff=0, koff=PAIRS, voff=2 * PAIRS, pair=True,
        scale=fox_scale, name="fox_bwd", qblk=ATTN_BWD_QBLK, side=red.late_scatter(got) if red else None)
    dmq, dmk, dmv, *got = flash_bwd(mq, mq_t, mk, mv, dmo, dmo_t, st_m, None, qoff=0, koff=0, voff=0,
                                    pair=False, scale=mla_scale, name="mla_bwd", qblk=ATTN_BWD_QBLK,
                                    side=red.late_halves(got) if red else None)
    if red:
        red.late_done(got)
    dlf = cumsum_rows(d_f, reverse=True, name="gate_cumsum_bwd")
    dsmall, dq_u, dkv_u, dg_q, dg_kv, db_f = mla_prep_bwd(dmq, dmk, dmv, dlf, small, g_q, g_kv, w_uq_p, w_ukv_p,
                                                          pos_col, inv_row, b_f, name="mla_prep_bwd")
    dw_uq_p = mm_tn(cqn, dq_u, name="dw_uq")
    dw_ukv_p = mm_tn(ckvn, dkv_u, name="dw_ukv")

    def to_bf16(tile):
        return tile.astype(BF16)

    dqkv = [dfq, dfk, dfv]
    dw_in_t = join_w_in_t([mm_tn(part, h1, a_pro=to_bf16, name="dw_" + nm) for part, nm in zip(dqkv, "qkv")],
                          mm_tn(dsmall, h1, name="dw_small"))
    dk_cols = _unpad_heads(dw_ukv_p[:, :HW], HEAD_DIM).reshape(KV_RANK, N_HEADS, HEAD_DIM)
    dv_cols = dw_ukv_p[:, HW:].reshape(KV_RANK, N_HEADS, HEAD_DIM)
    dw_ukv = jnp.concatenate([dk_cols, dv_cols], axis=2).reshape(KV_RANK, N_HEADS * 2 * HEAD_DIM)
    early = (dw_in_t, split_cols(dw_uq_p), split_cols(dw_ukv))
    w_parts = [w_qkv_t[n * FW:(n + 1) * FW] for n in range(3)]
    grad_x, dg_attn, *got = mm(dqkv + [dsmall], w_parts + [w_small_t], a_pro=to_bf16, extras=[xs, dx1], vecs=[g_attn],
                               epilogue=norm_bwd, out_dtypes=[F32], n_sums=1, name="proj_bwd",
                               side=red.early_scatter(early) if red else None)
    if red:
        red.early_done(got)
    d_gains = (dg_attn, db_f[:, :N_HEADS], dg_q, dg_kv, dg_fo, dg_mo, dg_mlp, dg_fin)
    return loss, grad_x, early, late, d_gains


class GradReduction:
    def __init__(self, core_id, chip):
        self.core_id, self.chip = core_id, chip
        self.core = core_id.reshape(1).astype(jnp.int32)
        self.grads, self.pairs, self.halves, self.others = {}, {}, {}, {}

    def _other_halves(self, grads):
        out = []
        for g in grads:
            axis = 1 + split_axis(g.shape[1])
            size = g.shape[axis] // 2
            out.append(lax.dynamic_slice_in_dim(g, (1 - self.core_id) * size, size, axis=axis).astype(BF16))
        return out

    def _add_pairs(self, group, recvs):
        self.pairs[group] = [add_pair(g, r, self.core, name="add_pair_%s_%d" % (group, n))
                             for n, (g, r) in enumerate(zip(self.grads[group], recvs))]
        return scatter_side(self.pairs[group])

    def _chip_sums(self, group, scattered):
        chip = self.chip
        with_mine = [lax.dynamic_update_index_in_dim(s, lax.dynamic_index_in_dim(p, chip, 0, keepdims=True), chip, 0)
                     for s, p in zip(scattered, self.pairs[group])]
        self.halves[group] = [sum_chips(s, name="sum_chips_%s_%d" % (group, n)) for n, s in enumerate(with_mine)]
        return self.halves[group]

    def late_swap(self, grads, bf16_copies):
        self.grads["late"] = list(grads)
        return swap_halves_side(list(bf16_copies))

    def late_scatter(self, recvs):
        return self._add_pairs("late", recvs)

    def late_halves(self, scattered):
        return swap_side(self._chip_sums("late", scattered))

    def late_done(self, others):
        self.others["late"] = list(others)

    def early_scatter(self, grads):
        self.grads["early"] = list(grads)
        return self._add_pairs("early", run_side(swap_side(self._other_halves(grads)), name="swap_early_sends"))

    def early_done(self, scattered):
        self.others["early"] = list(run_side(swap_side(self._chip_sums("early", scattered)), name="swap_early_halves"))

def kernel(x, positions, attn_norm_g, w_in, b_forget, q_norm_g, w_uq, kv_norm_g, w_ukv, fox_out_g, mla_out_g, w_o, mlp_norm_g, w_up, w_down, final_norm_g, loss_target, m_attn_norm_g, m_w_in, m_b_forget, m_q_norm_g, m_w_uq, m_kv_norm_g, m_w_ukv, m_fox_out_g, m_mla_out_g, m_w_o, m_mlp_norm_g, m_w_up, m_w_down, m_final_norm_g, v_attn_norm_g, v_w_in, v_b_forget, v_q_norm_g, v_w_uq, v_kv_norm_g, v_w_ukv, v_fox_out_g, v_mla_out_g, v_w_o, v_mlp_norm_g, v_w_up, v_w_down, v_final_norm_g):
    core_id = lax.axis_index("c")
    core = core_id.reshape(1).astype(jnp.int32)
    chip = 2 * lax.axis_index("x") + lax.axis_index("y")
    big = [w_in, w_uq, w_ukv, w_o, w_up, w_down]
    big_m = [m_w_in, m_w_uq, m_w_ukv, m_w_o, m_w_up, m_w_down]
    big_v = [v_w_in, v_w_uq, v_w_ukv, v_w_o, v_w_up, v_w_down]
    n_early = 3

    def vec(a):
        return a.reshape(1, -1)

    small = [attn_norm_g, b_forget, q_norm_g, kv_norm_g, fox_out_g, mla_out_g, mlp_norm_g, final_norm_g]
    small_m = [m_attn_norm_g, m_b_forget, m_q_norm_g, m_kv_norm_g, m_fox_out_g, m_mla_out_g, m_mlp_norm_g, m_final_norm_g]
    small_v = [v_attn_norm_g, v_b_forget, v_q_norm_g, v_kv_norm_g, v_fox_out_g, v_mla_out_g, v_mlp_norm_g, v_final_norm_g]
    gains = [vec(a) for a in small]

    views = [big[0][0].T, _pad_heads(big[1][0], HEAD_DIM + ROPE_DIM)] + [w[0] for w in big[2:]]
    shards = [v.astype(BF16) for v in views]

    def with_own(gathered, mine):
        return [lax.dynamic_update_index_in_dim(g, s, chip, 0) for g, s in zip(gathered, mine)]

    def early_weights(gathered):
        g_in, g_uq, g_ukv = with_own(gathered, shards[:n_early])
        return g_in, join_cols(g_uq), join_cols(g_ukv)

    def late_weights(gathered):
        g_o, g_up, g_down = with_own(gathered, shards[n_early:])
        return g_o.reshape(-1, g_o.shape[2]), join_cols(g_up), g_down.reshape(-1, g_down.shape[2])

    reduction = GradReduction(core_id, chip)
    loss, grad_x, _, _, d_small = local_step(
        x[0], positions[0], loss_target[0], gains, early_weights, late_weights, gather_side(shards[:n_early]),
        (gather_side(shards[n_early:-1]), gather_side(shards[-1:])), reduction)
    halves = reduction.halves["early"] + reduction.halves["late"]
    others = reduction.others["early"] + reduction.others["late"]

    def rows8(vs):
        return jnp.concatenate([_pad_lanes(vec(a).astype(F32), 1024) for a in vs], axis=0)

    with_loss = [jnp.concatenate([d, loss], axis=1) if n == 1 else d for n, d in enumerate(d_small)]
    g_small8 = allreduce_small(rows8(with_loss))

    outs_big = []
    for n, (w, gm, go, m, v) in enumerate(zip(big, halves, others, big_m, big_v)):
        if n == 0:
            outs = adamw_halves_t(*(jnp.transpose(a, (2, 0, 1)) for a in (w, m, v)), gm, go, core, name="adamw_%d" % n)
            outs_big.append([jnp.transpose(o, (1, 2, 0)) for o in outs])
        elif n == 1:
            gm, go = (_unpad_heads(gh, HEAD_DIM + ROPE_DIM) for gh in (gm, go))
            g_t = jnp.where(core[0] == 0, jnp.concatenate([gm, go], axis=0), jnp.concatenate([go, gm], axis=0)).T
            outs = (g_t,) + tuple(adamw_whole(w[0].T, g_t, m[0].T, v[0].T, name="adamw_%d" % n))
            outs_big.append([o.T[None] for o in outs])
        else:
            outs_big.append(adamw_halves(w, gm, go, m, v, core, name="adamw_%d" % n))
    outs_small = adamw_vectors(gains, g_small8, [vec(a) for a in small_m], [vec(a) for a in small_v], name="adamw_small")

    loss_all = g_small8[1, N_HEADS]
    grads, deltas, new_m, new_v = [None] * 14, [None] * 14, [None] * 14, [None] * 14
    big_at = [1, 4, 6, 9, 11, 12]
    small_at = [0, 2, 3, 5, 7, 8, 10, 13]
    for n, at in enumerate(big_at):
        grads[at], deltas[at], new_m[at], new_v[at] = outs_big[n]
    for at, s, outs in zip(small_at, small, outs_small):
        grads[at], deltas[at], new_m[at], new_v[at] = (o.reshape(s.shape) for o in outs)
    return (loss_all, grad_x[None], *grads, *deltas, *new_m, *new_v)
```

```python
import jax
import jax.numpy as jnp
from jax import lax
from jax.experimental import pallas as pl
from jax.experimental.pallas import tpu as pltpu

F32 = jnp.float32
BF16 = jnp.bfloat16
MESH = pl.DeviceIdType.MESH

EPS = 1e-6
ROPE_THETA = 10000.0
N_HEADS = 8
PAIRS = N_HEADS // 2
HEAD_DIM = 64
ROPE_DIM = 32
LANES = 128
Q_RANK = 384
KV_RANK = 256
N_CHIPS = 4
ADAM_LR, ADAM_B1, ADAM_B2, ADAM_EPS, ADAM_WD, ADAM_STEP = 0.001, 0.9, 0.999, 1e-08, 0.01, 10
VMEM_LIMIT = 48 * 1024 * 1024
ADAMW_BLOCK = 256 * 1024
LOG2E = 1.4426950408889634
LN2 = 0.6931471805599453
NN = (((1,), (0,)), ((), ()))
NT = (((1,), (1,)), ((), ()))
TN = (((0,), (0,)), ((), ()))


def _params(sem=None):
    return pltpu.CompilerParams(dimension_semantics=sem, vmem_limit_bytes=VMEM_LIMIT)


def _fit(block, dim):
    if dim <= block:
        return dim
    return next(b for b in range(block - block % LANES, 0, -LANES) if dim % b == 0)


def _row_block(rows):
    return next(b for b in (256, 128, 64, 32, 16, 8) if rows % b == 0)


def gridded(body, *, name, grid, in_specs, out_specs, out_shape, ins, semantics, side=None):
    if side is None:
        return pl.pallas_call(body, name=name, grid=grid, in_specs=in_specs, out_specs=out_specs, out_shape=out_shape,
                              compiler_params=_params(semantics))(*ins)
    n_in, n_out = len(in_specs), len(out_specs)
    steps = 1
    for extent in grid:
        steps *= extent

    def riding(*refs):
        main_in, s_in, main_out, s_out, sems = _split_refs(refs, [n_in, len(side.ins), n_out, len(side.out_shapes), 2])
        step = 0
        for axis, extent in enumerate(grid):
            step = step * extent + pl.program_id(axis)

        @pl.when(step == 0)
        def _():
            side.first(s_in, s_out, *sems)

        body(*main_in, *main_out)

        @pl.when(step == steps - 1)
        def _():
            if side.mid is not None:
                side.mid(s_in, s_out, *sems)
            side.last(s_in, s_out, *sems)

    s_in_specs, s_out_specs = side.specs()
    return pl.pallas_call(
        riding, name=name, grid=grid, in_specs=list(in_specs) + s_in_specs, out_specs=list(out_specs) + s_out_specs,
        out_shape=list(out_shape) + side.out_shapes, scratch_shapes=side.sems(),
        compiler_params=_params(("arbitrary",) * len(grid)))(*ins, *side.ins)


def rmsnorm(x, g, *, out_dtype, name, bt=512, side=None):
    t, d = x.shape
    bt = min(bt, t)

    def body(x_ref, g_ref, o_ref):
        xv = x_ref[...].astype(F32)
        r = lax.rsqrt(jnp.mean(xv * xv, axis=-1, keepdims=True) + EPS)
        o_ref[...] = (xv * r * g_ref[...]).astype(o_ref.dtype)

    return gridded(
        body, name=name, grid=(t // bt,),
        in_specs=[pl.BlockSpec((bt, d), lambda i: (i, 0)), pl.BlockSpec((1, d), lambda i: (0, 0))],
        out_specs=[pl.BlockSpec((bt, d), lambda i: (i, 0))],
        out_shape=[jax.ShapeDtypeStruct((t, d), out_dtype)],
        ins=[x, g], semantics=("parallel",), side=side)


def mm(a, b, *, trans_b=False, a_pro=None, extras=(), vecs=(), epilogue=None, out_dtypes, n_sums=0, t_blk=None, name,
       bm=1024, bn=1024, side=None):
    a_list = list(a) if isinstance(a, (list, tuple)) else [a]
    b_list = list(b) if isinstance(b, (list, tuple)) else [b]
    m = a_list[0].shape[0]
    n = b_list[0].shape[0] if trans_b else b_list[0].shape[1]
    ks = [x.shape[1] for x in a_list]
    if sum(ks) > 2048:
        bm = bm // 2
    bm, bn = _fit(bm, m), _fit(bn, n)
    assert n_sums == 0 or bn == n
    n_ab, n_ex, n_vec, n_out = len(a_list), len(extras), len(vecs), len(out_dtypes)
    n_t = 0 if t_blk is None else 1

    def body(*refs):
        a_refs, b_refs, ex, vs, outs, t_outs, sums = _split_refs(refs, [n_ab, n_ab, n_ex, n_vec, n_out, n_t, n_sums])
        acc = None
        for a_ref, b_ref in zip(a_refs, b_refs):
            a_tile = a_ref[...] if a_pro is None else a_pro(a_ref[...])
            part = lax.dot_general(a_tile, b_ref[...], NT if trans_b else NN, preferred_element_type=F32)
            acc = part if acc is None else acc + part
        res = epilogue(acc, *[e[...] for e in ex], *[v[...] for v in vs]) if epilogue is not None else (acc,)
        for o, r in zip(outs, res[:n_out]):
            o[...] = r.astype(o.dtype)
        for t_ref in t_outs:
            for u in range(bm // t_blk):
                t_ref[u] = res[0][u * t_blk:(u + 1) * t_blk, :].T.astype(t_ref.dtype)
        if n_sums:
            @pl.when(pl.program_id(0) == 0)
            def _():
                for s_ref in sums:
                    s_ref[...] = jnp.zeros_like(s_ref)

            for s_ref, r in zip(sums, res[n_out:]):
                s_ref[...] += r

    tile = pl.BlockSpec((bm, bn), lambda i, j: (i, j))
    vec = pl.BlockSpec((1, bn), lambda i, j: (0, j))
    t_specs, t_shapes = [], []
    if t_blk is not None:
        t_specs = [pl.BlockSpec((bm // t_blk, bn, t_blk), lambda i, j: (i, j, 0))]
        t_shapes = [jax.ShapeDtypeStruct((m // t_blk, n, t_blk), out_dtypes[0])]
    a_specs = [pl.BlockSpec((bm, k), lambda i, j: (i, 0)) for k in ks]
    b_specs = [pl.BlockSpec((bn, k), lambda i, j: (j, 0)) if trans_b else pl.BlockSpec((k, bn), lambda i, j: (0, j)) for k in ks]
    return gridded(
        body, name=name, grid=(m // bm, n // bn),
        in_specs=a_specs + b_specs + [tile] * n_ex + [vec] * n_vec,
        out_specs=[tile] * n_out + t_specs + [vec] * n_sums,
        out_shape=[jax.ShapeDtypeStruct((m, n), dt) for dt in out_dtypes] + t_shapes + [jax.ShapeDtypeStruct((1, n), F32)] * n_sums,
        ins=[*a_list, *b_list, *extras, *vecs],
        semantics=("arbitrary", "arbitrary") if n_sums else ("parallel", "parallel"), side=side)


def mm_tn(a, b, *, a_pro=None, name, col_shards=1, bf16_copy=False, bk=1024, bn=1024, bt=2048):
    t, k = a.shape
    n = b.shape[1]
    ns = n // col_shards
    bk, bn, bt = _fit(bk, k), _fit(bn, ns), _fit(bt, t)
    per = ns // bn
    last = t // bt - 1

    def body(a_ref, b_ref, o_ref, *copy_ref):
        @pl.when(pl.program_id(2) == 0)
        def _():
            o_ref[...] = jnp.zeros_like(o_ref)

        a_tile = a_ref[...] if a_pro is None else a_pro(a_ref[...])
        o_ref[...] += lax.dot_general(a_tile, b_ref[...], TN, preferred_element_type=F32)
        if bf16_copy:
            @pl.when(pl.program_id(2) == last)
            def _():
                copy_ref[0][...] = o_ref[...].astype(BF16)

    if col_shards == 1:
        out_spec = pl.BlockSpec((bk, bn), lambda i, j, s: (i, j))
        shape = (k, n)
    else:
        out_spec = pl.BlockSpec((None, bk, bn), lambda i, j, s: (j // per, i, j % per))
        shape = (col_shards, k, ns)
    out_specs, out_shape = out_spec, jax.ShapeDtypeStruct(shape, F32)
    if bf16_copy:
        out_specs, out_shape = [out_spec, out_spec], [out_shape, jax.ShapeDtypeStruct(shape, BF16)]
    return pl.pallas_call(
        body, name=name, grid=(k // bk, n // bn, t // bt),
        in_specs=[pl.BlockSpec((bt, bk), lambda i, j, s: (s, i)), pl.BlockSpec((bt, bn), lambda i, j, s: (s, j))],
        out_specs=out_specs, out_shape=out_shape,
        compiler_params=_params(("parallel", "parallel", "arbitrary")),
    )(a, b)


def _split3(x):
    hi = x.astype(BF16)
    r1 = x - hi.astype(F32)
    mid = r1.astype(BF16)
    lo = (r1 - mid.astype(F32)).astype(BF16)
    return hi, mid, lo


def cumsum_rows(x, *, reverse, name, bc=512):
    t, d = x.shape
    bc = min(bc, t)
    nb = t // bc

    def body(x_ref, o_ref, carry):
        @pl.when(pl.program_id(0) == 0)
        def _():
            carry[...] = jnp.zeros_like(carry)

        r = lax.broadcasted_iota(jnp.int32, (bc, bc), 0)
        c = lax.broadcasted_iota(jnp.int32, (bc, bc), 1)
        tri = jnp.where((r <= c) if reverse else (r >= c), 1.0, 0.0).astype(BF16)
        hi, mid, lo = _split3(x_ref[...])
        s = (lax.dot_general(tri, hi, NN, preferred_element_type=F32)
             + lax.dot_general(tri, mid, NN, preferred_element_type=F32)
             + lax.dot_general(tri, lo, NN, preferred_element_type=F32)) + carry[0:1, :]
        o_ref[...] = s
        carry[0:1, :] = s[0:1, :] if reverse else s[bc - 1:bc, :]

    imap = (lambda i: (nb - 1 - i, 0)) if reverse else (lambda i: (i, 0))
    return pl.pallas_call(
        body, name=name, grid=(nb,),
        in_specs=[pl.BlockSpec((bc, d), imap)], out_specs=pl.BlockSpec((bc, d), imap),
        out_shape=jax.ShapeDtypeStruct((t, d), F32),
        scratch_shapes=[pltpu.VMEM((8, d), F32)],
        compiler_params=_params(("arbitrary",)),
    )(x)


def _rope(x, c, a, b):
    return x * c + pltpu.roll(x, LANES - ROPE_DIM // 2, 1) * a + pltpu.roll(x, ROPE_DIM // 2, 1) * b


def _rope_bwd(d, c, a, b):
    return d * c + pltpu.roll(d * a, ROPE_DIM // 2, 1) + pltpu.roll(d * b, LANES - ROPE_DIM // 2, 1)


S_CQ, S_CKV, S_KR, S_F, S_END = 0, Q_RANK, Q_RANK + KV_RANK, Q_RANK + KV_RANK + LANES, 1024
HW = N_HEADS * LANES
FW = N_HEADS * HEAD_DIM


def _rope_tables(pos_col, inv_row):
    ang = pos_col * inv_row
    cos, sin = jnp.cos(ang), jnp.sin(ang)
    lane = lax.broadcasted_iota(jnp.int32, (1, LANES), 1)
    first = (lane >= HEAD_DIM) & (lane < HEAD_DIM + ROPE_DIM // 2)
    second = (lane >= HEAD_DIM + ROPE_DIM // 2) & (lane < HEAD_DIM + ROPE_DIM)
    return (jnp.where(lane < HEAD_DIM, 1.0, jnp.where(first | second, cos, 0.0)), jnp.where(first, -sin, 0.0),
            jnp.where(second, sin, 0.0))


def mla_prep(small, g_q, g_kv, w_uq, w_ukv, pos_col, inv_row, b_f, *, name, bt=512):
    t = small.shape[0]
    bt = min(bt, t)

    def body(s_ref, gq_ref, gkv_ref, wq_ref, wkv_ref, pos_ref, inv_ref, bf_ref,
             mq_ref, mk_ref, mv_ref, lf_ref, cqn_ref, ckvn_ref, mqt_ref, mvt_ref):
        cq = s_ref[:, S_CQ:S_CKV]
        rq = lax.rsqrt(jnp.mean(cq * cq, axis=-1, keepdims=True) + EPS)
        cqn = (cq * rq * gq_ref[...]).astype(BF16)
        ckv = s_ref[:, S_CKV:S_KR]
        rkv = lax.rsqrt(jnp.mean(ckv * ckv, axis=-1, keepdims=True) + EPS)
        ckvn = (ckv * rkv * gkv_ref[...]).astype(BF16)
        cqn_ref[...] = cqn
        ckvn_ref[...] = ckvn
        tc, ta, tb = _rope_tables(pos_ref[...], inv_ref[...])
        q = jnp.dot(cqn, wq_ref[...], preferred_element_type=F32)
        kv = jnp.dot(ckvn, wkv_ref[...], preferred_element_type=F32)
        kr = _rope(s_ref[:, S_KR:S_F], tc, ta, tb)
        for h in range(N_HEADS):
            sl = slice(h * LANES, (h + 1) * LANES)
            roped = _rope(q[:, sl], tc, ta, tb)
            mq_ref[:, sl] = roped.astype(BF16)
            mqt_ref[0, sl, :] = roped.T.astype(BF16)
            mk_ref[:, sl] = (kv[:, sl] + kr).astype(BF16)
        mv_ref[...] = kv[:, HW:].astype(BF16)
        mvt_ref[0] = kv[:, HW:].T.astype(BF16)
        z = s_ref[:, S_F:S_END - LANES] + bf_ref[...]
        lf_ref[...] = jnp.minimum(z, 0.0) - jnp.log(1.0 + jnp.exp(-jnp.abs(z)))

    def row(w):
        return pl.BlockSpec((bt, w), lambda i: (i, 0))

    def full(arr):
        return pl.BlockSpec(arr.shape, lambda i: (0, 0))

    return pl.pallas_call(
        body, name=name, grid=(t // bt,),
        in_specs=[row(S_END), full(g_q), full(g_kv), full(w_uq), full(w_ukv), row(1), full(inv_row), full(b_f)],
        out_specs=[row(HW), row(HW), row(FW), row(LANES), row(Q_RANK), row(KV_RANK),
                   pl.BlockSpec((1, HW, bt), lambda i: (i, 0, 0)), pl.BlockSpec((1, FW, bt), lambda i: (i, 0, 0))],
        out_shape=[jax.ShapeDtypeStruct((t, HW), BF16)] * 2 + [jax.ShapeDtypeStruct((t, FW), BF16), jax.ShapeDtypeStruct((t, LANES), F32),
                   jax.ShapeDtypeStruct((t, Q_RANK), BF16), jax.ShapeDtypeStruct((t, KV_RANK), BF16),
                   jax.ShapeDtypeStruct((t // bt, HW, bt), BF16), jax.ShapeDtypeStruct((t // bt, FW, bt), BF16)],
        compiler_params=_params(("parallel",)),
    )(small, g_q, g_kv, w_uq, w_ukv, pos_col, inv_row, b_f)


def mla_prep_bwd(dmq, dmk, dmv, dlf, small, g_q, g_kv, w_uq, w_ukv, pos_col, inv_row, b_f, *, name, bt=512):
    t = small.shape[0]
    bt = min(bt, t)

    def body(dmq_ref, dmk_ref, dmv_ref, dlf_ref, s_ref, gq_ref, gkv_ref, wq_ref, wkv_ref, pos_ref, inv_ref, bf_ref,
             ds_ref, dq_ref, dkv_ref, dgq_ref, dgkv_ref, db_ref):
        tc, ta, tb = _rope_tables(pos_ref[...], inv_ref[...])
        lane = lax.broadcasted_iota(jnp.int32, (1, LANES), 1)
        dkr = jnp.zeros((bt, LANES), F32)
        for h in range(N_HEADS):
            sl = slice(h * LANES, (h + 1) * LANES)
            dq_ref[:, sl] = _rope_bwd(dmq_ref[:, sl], tc, ta, tb).astype(BF16)
            dkr = dkr + dmk_ref[:, sl]
        dkv_ref[:, :HW] = dmk_ref[...].astype(BF16)
        dkv_ref[:, HW:] = dmv_ref[...].astype(BF16)
        in_rope = (lane >= HEAD_DIM) & (lane < HEAD_DIM + ROPE_DIM)
        ds_ref[:, S_KR:S_F] = jnp.where(in_rope, _rope_bwd(dkr, tc, ta, tb), 0.0).astype(BF16)

        def norm_bwd(raw, g_ref, dn, dg_ref):
            r = lax.rsqrt(jnp.mean(raw * raw, axis=-1, keepdims=True) + EPS)
            u = dn * g_ref[...]
            dot = jnp.mean(u * raw, axis=-1, keepdims=True)
            dg_ref[...] += jnp.sum(dn * (raw * r), axis=0, keepdims=True)
            return r * u - raw * (r * r * r * dot)

        @pl.when(pl.program_id(0) == 0)
        def _():
            dgq_ref[...] = jnp.zeros_like(dgq_ref)
            dgkv_ref[...] = jnp.zeros_like(dgkv_ref)
            db_ref[...] = jnp.zeros_like(db_ref)

        dcqn = lax.dot_general(dq_ref[...], wq_ref[...], NT, preferred_element_type=F32)
        ds_ref[:, S_CQ:S_CKV] = norm_bwd(s_ref[:, S_CQ:S_CKV], gq_ref, dcqn, dgq_ref).astype(BF16)
        dckvn = lax.dot_general(dkv_ref[...], wkv_ref[...], NT, preferred_element_type=F32)
        ds_ref[:, S_CKV:S_KR] = norm_bwd(s_ref[:, S_CKV:S_KR], gkv_ref, dckvn, dgkv_ref).astype(BF16)
        z = s_ref[:, S_F:S_END - LANES] + bf_ref[...]
        dz = jnp.where(lane < N_HEADS, dlf_ref[...] / (1.0 + jnp.exp(z)), 0.0)
        db_ref[...] += jnp.sum(dz, axis=0, keepdims=True)
        ds_ref[:, S_F:S_END - LANES] = dz.astype(BF16)
        ds_ref[:, S_END - LANES:] = jnp.zeros((bt, LANES), BF16)

    def row(w):
        return pl.BlockSpec((bt, w), lambda i: (i, 0))

    def full(arr):
        return pl.BlockSpec(arr.shape, lambda i: (0, 0))

    def vec(w):
        return pl.BlockSpec((1, w), lambda i: (0, 0))

    return pl.pallas_call(
        body, name=name, grid=(t // bt,),
        in_specs=[row(HW), row(HW), row(FW), row(LANES), row(S_END), full(g_q), full(g_kv), full(w_uq), full(w_ukv),
                  row(1), full(inv_row), full(b_f)],
        out_specs=[row(S_END), row(HW), row(HW + FW), vec(Q_RANK), vec(KV_RANK), vec(LANES)],
        out_shape=[jax.ShapeDtypeStruct((t, S_END), BF16), jax.ShapeDtypeStruct((t, HW), BF16),
                   jax.ShapeDtypeStruct((t, HW + FW), BF16), jax.ShapeDtypeStruct((1, Q_RANK), F32),
                   jax.ShapeDtypeStruct((1, KV_RANK), F32), jax.ShapeDtypeStruct((1, LANES), F32)],
        compiler_params=_params(("arbitrary",)),
    )(dmq, dmk, dmv, dlf, small, g_q, g_kv, w_uq, w_ukv, pos_col, inv_row, b_f)


class Side:
    def __init__(self, ins, out_shapes, n_sems, first, last, mid=None):
        self.ins, self.out_shapes, self.n_sems = list(ins), list(out_shapes), n_sems
        self.first, self.mid, self.last = first, mid, last

    def specs(self):
        return [ANY] * len(self.ins), [ANY] * len(self.out_shapes)

    def sems(self):
        return [pltpu.SemaphoreType.DMA((self.n_sems,)), pltpu.SemaphoreType.DMA((self.n_sems,))]


def _lane():
    return lax.broadcasted_iota(jnp.int32, (1, LANES), 1)


def _halves(x):
    zero = jnp.zeros_like(x)
    return [jnp.where(_lane() < HEAD_DIM, x, zero), jnp.where(_lane() >= HEAD_DIM, x, zero)]


def _groups(x):
    return [x[:, :LANES], x[:, LANES:]]


def _pick_row(tile, h):
    row = lax.broadcasted_iota(jnp.int32, (tile.shape[0], 1), 0)
    return jnp.sum(jnp.where(row == h, tile, 0.0), axis=0, keepdims=True)


def _pick_lane(tile, h):
    return jnp.sum(jnp.where(_lane() == h, tile, 0.0), axis=1, keepdims=True)


def _row_halves(x):
    row = lax.broadcasted_iota(jnp.int32, (LANES, 1), 0)
    zero = jnp.zeros_like(x)
    return [jnp.where(row < HEAD_DIM, x, zero), jnp.where(row >= HEAD_DIM, x, zero)]


def _below_diagonal(s):
    r = lax.broadcasted_iota(jnp.int32, s.shape, 0)
    c = lax.broadcasted_iota(jnp.int32, s.shape, 1)
    return jnp.where(c <= r, s, -jnp.inf)


def _above_diagonal(s):
    r = lax.broadcasted_iota(jnp.int32, s.shape, 0)
    c = lax.broadcasted_iota(jnp.int32, s.shape, 1)
    return jnp.where(r <= c, s, -jnp.inf)


def _split_refs(refs, counts):
    out, at = [], 0
    for n in counts:
        out.append(refs[at:at + n])
        at += n
    return out


def flash_fwd(qt_arr, k_arr, vt_arr, f_cum, *, qoff, koff, voff, pair, scale, name, blk=512, side=None):
    t = k_arr.shape[0]
    tblk = qt_arr.shape[2]
    blk = max(min(blk, t), tblk)
    sub = blk // tblk
    nb = t // blk
    steps = PAIRS * nb
    w = LANES if pair else 2 * LANES
    has_bias = f_cum is not None
    ins = [qt_arr, k_arr, vt_arr] + ([f_cum] if has_bias else [])

    def wide(ref, first):
        parts = [ref[first + u] for u in range(sub)]
        return parts[0] if sub == 1 else jnp.concatenate(parts, axis=1)
    s_ins, s_outs = (side.ins, side.out_shapes) if side else ([], [])

    def body(*refs):
        main, si, outs, so, sems = _split_refs(refs, [len(ins), len(s_ins), 2, len(s_outs), 2 if side else 0])
        qt_ref, k_ref, vt_ref = main[:3]
        f_ref = main[3] if has_bias else None
        o_ref, st_ref = outs
        g, i = pl.program_id(0), pl.program_id(1)
        step_id = g * nb + i
        if side:
            @pl.when(step_id == 0)
            def _():
                side.first(si, so, *sems)

            if side.mid is not None:
                @pl.when(step_id == (3 * steps) // 4)
                def _():
                    side.mid(si, so, *sems)

        qt = (wide(qt_ref, 0).astype(F32) * (scale * LOG2E)).astype(BF16)
        qts = _row_halves(qt) if pair else [qt[:LANES], qt[LANES:]]

        def k_of(kk, n):
            return kk if pair else kk[:, n * LANES:(n + 1) * LANES]

        def with_ones(vt_rows):
            return jnp.concatenate([vt_rows, jnp.ones((ACC_ROWS - HEAD_DIM, vt_rows.shape[1]), BF16)], axis=0)

        def step(j, carry, diagonal):
            rows = pl.ds(pl.multiple_of(j * blk, blk), blk)
            kk = k_ref[rows, :]
            vt = wide(vt_ref, sub * j)
            out = []
            for n in range(2):
                m, acc = carry[n]
                s = jnp.dot(k_of(kk, n), qts[n], preferred_element_type=F32)
                if has_bias:
                    s = s - LOG2E * _pick_lane(f_ref[rows, :], 2 * g + n)
                if diagonal:
                    s = _above_diagonal(s)
                m_new = jnp.maximum(m, jnp.max(s, axis=0, keepdims=True))
                p = jnp.exp2(s - m_new).astype(BF16)
                out.append((m_new, jnp.exp2(m - m_new) * acc
                            + jnp.dot(with_ones(vt[n * HEAD_DIM:(n + 1) * HEAD_DIM]), p, preferred_element_type=F32)))
            return tuple(out)

        def diagonal_in_halves(carry):
            h = tblk
            halves = [pl.ds(pl.multiple_of(i * blk, blk), h), pl.ds(pl.multiple_of(i * blk + h, h), h)]
            k_top, k_bot = k_ref[halves[0], :], k_ref[halves[1], :]
            vt_top, vt_bot = vt_ref[sub * i], vt_ref[sub * i + 1]
            out = []
            for n in range(2):
                m, acc = carry[n]
                heads = slice(n * HEAD_DIM, (n + 1) * HEAD_DIM)
                s_top = jnp.dot(k_of(k_top, n), qts[n], preferred_element_type=F32)
                s_bot = jnp.dot(k_of(k_bot, n), qts[n][:, h:], preferred_element_type=F32)
                if has_bias:
                    s_top = s_top - LOG2E * _pick_lane(f_ref[halves[0], :], 2 * g + n)
                    s_bot = s_bot - LOG2E * _pick_lane(f_ref[halves[1], :], 2 * g + n)
                s_top, s_bot = _above_diagonal(s_top), _above_diagonal(s_bot)
                m_top = jnp.maximum(m, jnp.max(s_top, axis=0, keepdims=True))
                m_new = jnp.concatenate([m_top[:, :h], jnp.maximum(m_top[:, h:], jnp.max(s_bot, axis=0, keepdims=True))], axis=1)
                p_top = jnp.exp2(s_top - m_new).astype(BF16)
                p_bot = jnp.exp2(s_bot - m_new[:, h:]).astype(BF16)
                late = jnp.concatenate([jnp.zeros((ACC_ROWS, h), F32),
                                        jnp.dot(with_ones(vt_bot[heads]), p_bot, preferred_element_type=F32)], axis=1)
                out.append((m_new, jnp.exp2(m - m_new) * acc
                            + jnp.dot(with_ones(vt_top[heads]), p_top, preferred_element_type=F32) + late))
            return tuple(out)

        init = tuple((jnp.full((1, blk), -jnp.inf, F32), jnp.zeros((ACC_ROWS, blk), F32)) for _ in range(2))
        carry = lax.fori_loop(0, i, lambda j, c: step(j, c, False), init)
        (ma, acca), (mb, accb) = diagonal_in_halves(carry) if sub == 2 else step(i, carry, True)
        la, lb = acca[HEAD_DIM:HEAD_DIM + 1], accb[HEAD_DIM:HEAD_DIM + 1]
        o_ref[...] = jnp.concatenate([acca[:HEAD_DIM] / la, accb[:HEAD_DIM] / lb], axis=0).T
        row = lax.broadcasted_iota(jnp.int32, (LANES, 1), 0)
        st_ref[0] = jnp.where(row == 0, ma + jnp.log2(la), jnp.where(row == 1, mb + jnp.log2(lb), 0.0)).T
        if side:
            @pl.when(step_id == steps - 1)
            def _():
                side.last(si, so, *sems)

    in_specs = [pl.BlockSpec((sub, w, tblk), lambda g, i: (i, qoff + g, 0)), pl.BlockSpec((t, w), lambda g, i: (0, koff + g)),
                pl.BlockSpec((t // tblk, LANES, tblk), lambda g, i: (0, voff + g, 0))]
    if has_bias:
        in_specs.append(pl.BlockSpec((t, LANES), lambda g, i: (0, 0)))
    s_in_specs, s_out_specs = side.specs() if side else ([], [])
    return pl.pallas_call(
        body, name=name, grid=(PAIRS, nb), in_specs=in_specs + s_in_specs,
        out_specs=[pl.BlockSpec((blk, LANES), lambda g, i: (i, g)), pl.BlockSpec((1, blk, LANES), lambda g, i: (g, i, 0))]
        + s_out_specs,
        out_shape=[jax.ShapeDtypeStruct((t, PAIRS * LANES), F32), jax.ShapeDtypeStruct((PAIRS, t, LANES), F32)] + list(s_outs),
        scratch_shapes=side.sems() if side else [],
        compiler_params=_params(("arbitrary", "arbitrary")),
    )(*ins, *s_ins)


def mix_norm(fo, mo, g_fo, g_mo, *, name, bt=512):
    t, d = fo.shape
    bt = min(bt, t)

    def body(fo_ref, mo_ref, gf_ref, gm_ref, o_ref):
        for n, (x_ref, g_ref) in enumerate(((fo_ref, gf_ref), (mo_ref, gm_ref))):
            xv = x_ref[...]
            r = lax.rsqrt(jnp.mean(xv * xv, axis=-1, keepdims=True) + EPS)
            o_ref[:, n * d:(n + 1) * d] = (xv * r * g_ref[...]).astype(BF16)

    row = pl.BlockSpec((bt, d), lambda i: (i, 0))
    vec = pl.BlockSpec((1, d), lambda i: (0, 0))
    return pl.pallas_call(
        body, name=name, grid=(t // bt,), in_specs=[row, row, vec, vec],
        out_specs=pl.BlockSpec((bt, 2 * d), lambda i: (i, 0)),
        out_shape=jax.ShapeDtypeStruct((t, 2 * d), BF16),
        compiler_params=_params(("parallel",)),
    )(fo, mo, g_fo, g_mo)


def mix_norm_bwd(dx1b, w_o, fo, mo, g_fo, g_mo, st_f, st_m, *, name, bt=512, side=None):
    t, d = fo.shape
    bt = min(bt, t)

    def body(dx_in_ref, w_ref, fo_ref, mo_ref, gf_ref, gm_ref, sf_ref, sm_ref, dfo_ref, dmo_ref, dgf_ref, dgm_ref,
             sfo_ref, smo_ref, dfot_ref, dmot_ref):
        @pl.when(pl.program_id(0) == 0)
        def _():
            dgf_ref[...] = jnp.zeros_like(dgf_ref)
            dgm_ref[...] = jnp.zeros_like(dgm_ref)

        dmixed = lax.dot_general(dx_in_ref[...], w_ref[...], NT, preferred_element_type=F32)
        groups = ((fo_ref, gf_ref, dfo_ref, dgf_ref, sf_ref, sfo_ref, dfot_ref),
                  (mo_ref, gm_ref, dmo_ref, dgm_ref, sm_ref, smo_ref, dmot_ref))
        for n, (x_ref, g_ref, dx_ref, dg_ref, st_ref, sto_ref, dxt_ref) in enumerate(groups):
            xv = x_ref[...]
            dhv = dmixed[:, n * d:(n + 1) * d]
            r = lax.rsqrt(jnp.mean(xv * xv, axis=-1, keepdims=True) + EPS)
            u = dhv * g_ref[...]
            dxf = r * u - xv * (r * r * r * jnp.mean(u * xv, axis=-1, keepdims=True))
            dxb = dxf.astype(BF16)
            dx_ref[...] = dxb
            dxt_ref[0] = dxf.T.astype(BF16)
            dg_ref[...] += jnp.sum(dhv * (xv * r), axis=0, keepdims=True)
            prod = xv * dxb.astype(F32)
            for g in range(PAIRS):
                grp = prod[:, g * LANES:(g + 1) * LANES]
                da = jnp.sum(jnp.where(_lane() < HEAD_DIM, grp, 0.0), axis=1, keepdims=True)
                db = jnp.sum(jnp.where(_lane() >= HEAD_DIM, grp, 0.0), axis=1, keepdims=True)
                sto_ref[g] = jnp.where(_lane() == 2, da, jnp.where(_lane() == 3, db, st_ref[g]))

    row = pl.BlockSpec((bt, d), lambda i: (i, 0))
    vec = pl.BlockSpec((1, d), lambda i: (0, 0))
    stat = pl.BlockSpec((PAIRS, bt, LANES), lambda i: (0, i, 0))
    return gridded(
        body, name=name, grid=(t // bt,),
        in_specs=[pl.BlockSpec((bt, dx1b.shape[1]), lambda i: (i, 0)), pl.BlockSpec(w_o.shape, lambda i: (0, 0)),
                  row, row, vec, vec, stat, stat],
        out_specs=[row, row, vec, vec, stat, stat] + [pl.BlockSpec((1, d, bt), lambda i: (i, 0, 0))] * 2,
        out_shape=[jax.ShapeDtypeStruct((t, d), BF16)] * 2 + [jax.ShapeDtypeStruct((1, d), F32)] * 2
        + [jax.ShapeDtypeStruct(st_f.shape, F32)] * 2 + [jax.ShapeDtypeStruct((t // bt, d, bt), BF16)] * 2,
        ins=[dx1b, w_o, fo, mo, g_fo, g_mo, st_f, st_m], semantics=("arbitrary",), side=side)


def flash_bwd(q_arr, qt_arr, k_arr, v_arr, do_arr, dot_arr, st, f_blocks, *, qoff, koff, voff, pair, scale, name, qblk=1024,
              side=None):
    t = q_arr.shape[0]
    blk = qt_arr.shape[2]
    qblk = max(min(qblk, t), blk)
    sub = qblk // blk
    nb, nbq = t // blk, t // qblk
    w = LANES if pair else 2 * LANES
    hw = w // 2
    has_bias = f_blocks is not None
    split = _halves if pair else _groups
    ins = [q_arr, qt_arr, k_arr, v_arr, do_arr, dot_arr, st] + ([f_blocks] if has_bias else [])
    n_out = 4 if has_bias else 3
    s_ins, s_outs = (side.ins, side.out_shapes) if side else ([], [])

    def wide(ref, first, count):
        parts = [ref[first + u] for u in range(count)]
        return parts[0] if count == 1 else jnp.concatenate(parts, axis=1)

    def body(*refs):
        main, si, outs, so, sems = _split_refs(refs, [len(ins), len(s_ins), n_out, len(s_outs), 2 if side else 0])
        q_ref, qt_ref, k_ref, v_ref, do_ref, dot_ref, st_ref = main[:7]
        dq_ref, dk_ref, dv_ref = outs[:3]
        g, j = pl.program_id(0), pl.program_id(1)
        step_id = g * nb + j
        if side:
            @pl.when(step_id == 0)
            def _():
                side.first(si, so, *sems)

        @pl.when(j == 0)
        def _():
            dq_ref[...] = jnp.zeros_like(dq_ref)

        kk, vv = k_ref[...], v_ref[...]
        ks = [kk, kk] if pair else _groups(kk)
        if has_bias:
            f_ref, df_ref = main[7], outs[3]
            fk = [LOG2E * _pick_row(f_ref[0], 2 * g + n) for n in range(2)]

            @pl.when(step_id == 0)
            def _():
                df_ref[...] = jnp.zeros_like(df_ref)

        def step(tb, count, carry, diagonal):
            rows = pl.ds(pl.multiple_of(tb * blk, blk), count * blk)
            qs = split((q_ref[rows, :].astype(F32) * (scale * LOG2E)).astype(BF16))
            qt = (wide(qt_ref, tb, count).astype(F32) * (scale * LOG2E)).astype(BF16)
            dos = [h.astype(BF16) for h in _halves(do_ref[rows, :].astype(F32))]
            dot = wide(dot_ref, tb, count)
            stats = st_ref[0, rows, :]
            new, dqs, row_sums = [], [], []
            for n in range(2):
                dkt, dvt, dfk = carry[n]
                s = lax.dot_general(qs[n], ks[n], NT, preferred_element_type=F32)
                if has_bias:
                    s = s - fk[n]
                if diagonal:
                    s = _below_diagonal(s)
                p = jnp.exp2(s - stats[:, n:n + 1])
                dp = lax.dot_general(dos[n], vv, NT, preferred_element_type=F32)
                ds = p * (dp - stats[:, 2 + n:3 + n])
                dsb = ds.astype(BF16)
                dvt = dvt + jnp.dot(dot[n * HEAD_DIM:(n + 1) * HEAD_DIM], p.astype(BF16), preferred_element_type=F32)
                dkt = dkt + jnp.dot(qt[n * hw:(n + 1) * hw], dsb, preferred_element_type=F32)
                dqs.append(jnp.dot(dsb, ks[n], preferred_element_type=F32))
                if has_bias:
                    dfk = dfk - jnp.sum(ds, axis=0, keepdims=True)
                    row_sums.append(jnp.sum(ds, axis=1, keepdims=True))
                new.append((dkt, dvt, dfk))
            dq = (jnp.where(_lane() < HEAD_DIM, dqs[0], dqs[1]) if pair else jnp.concatenate(dqs, axis=1)) * scale
            if has_bias:
                df_ref[rows, :] += jnp.where(_lane() == 2 * g, row_sums[0], jnp.where(_lane() == 2 * g + 1, row_sums[1], 0.0))
            dq_ref[rows, :] += dq
            return tuple(new)

        init = tuple((jnp.zeros((hw, blk), F32), jnp.zeros((HEAD_DIM, blk), F32), jnp.zeros((1, blk), F32)) for _ in range(2))
        carry = step(j, 1, init, True)
        whole = j // sub + 1
        carry = lax.fori_loop(j + 1, whole * sub, lambda tb, c: step(tb, 1, c, False), carry)
        (dka, dva, dfa), (dkb, dvb, dfb) = lax.fori_loop(whole, nbq, lambda i, c: step(i * sub, sub, c, False), carry)
        dk_ref[...] = (jnp.concatenate([dka, dkb], axis=0).T * LN2).astype(BF16)
        dv_ref[...] = jnp.concatenate([dva, dvb], axis=0).T.astype(BF16)
        if has_bias:
            row = lax.broadcasted_iota(jnp.int32, (LANES, 1), 0)
            by_head = jnp.where(row == 2 * g, dfa, jnp.where(row == 2 * g + 1, dfb, 0.0))
            df_ref[pl.ds(pl.multiple_of(j * blk, blk), blk), :] += by_head.T
        if side:
            @pl.when(step_id == PAIRS * nb - 1)
            def _():
                side.last(si, so, *sems)

    in_specs = [pl.BlockSpec((t, w), lambda g, j: (0, qoff + g)), pl.BlockSpec((nb, w, blk), lambda g, j: (0, qoff + g, 0)),
                pl.BlockSpec((blk, w), lambda g, j: (j, koff + g)), pl.BlockSpec((blk, LANES), lambda g, j: (j, voff + g)),
                pl.BlockSpec((t, LANES), lambda g, j: (0, g)), pl.BlockSpec((nb, LANES, blk), lambda g, j: (0, g, 0)),
                pl.BlockSpec((1, t, LANES), lambda g, j: (g, 0, 0))]
    out_specs = [pl.BlockSpec((t, w), lambda g, j: (0, g)), pl.BlockSpec((blk, w), lambda g, j: (j, g)),
                 pl.BlockSpec((blk, LANES), lambda g, j: (j, g))]
    out_shape = [jax.ShapeDtypeStruct((t, PAIRS * w), F32), jax.ShapeDtypeStruct((t, PAIRS * w), BF16),
                 jax.ShapeDtypeStruct((t, PAIRS * LANES), BF16)]
    if has_bias:
        in_specs.append(pl.BlockSpec((1, N_HEADS, blk), lambda g, j: (j, 0, 0)))
        out_specs.append(pl.BlockSpec((t, LANES), lambda g, j: (0, 0)))
        out_shape.append(jax.ShapeDtypeStruct((t, LANES), F32))
    s_in_specs, s_out_specs = side.specs() if side else ([], [])
    return pl.pallas_call(
        body, name=name, grid=(PAIRS, nb), in_specs=in_specs + s_in_specs, out_specs=out_specs + s_out_specs,
        out_shape=out_shape + list(s_outs), scratch_shapes=side.sems() if side else [],
        compiler_params=_params(("arbitrary", "arbitrary")),
    )(*ins, *s_ins)


def _adamw_math(w, g, m, v):
    nm = ADAM_B1 * m + (1.0 - ADAM_B1) * g
    nv = ADAM_B2 * v + (1.0 - ADAM_B2) * (g * g)
    m_hat = nm / (1.0 - ADAM_B1 ** ADAM_STEP)
    v_hat = nv / (1.0 - ADAM_B2 ** ADAM_STEP)
    return -ADAM_LR * (m_hat / (jnp.sqrt(v_hat) + ADAM_EPS) + ADAM_WD * w), nm, nv


def adamw_vectors(ws, g_rows, ms, vs, *, name):
    k = len(ws)

    def body(g_ref, *refs):
        w_refs, m_refs, v_refs, outs = _split_refs(refs, [k, k, k, 4 * k])
        for i in range(k):
            g = g_ref[i:i + 1, :w_refs[i].shape[1]]
            outs[4 * i][...] = g
            outs[4 * i + 1][...], outs[4 * i + 2][...], outs[4 * i + 3][...] = _adamw_math(
                w_refs[i][...], g, m_refs[i][...], v_refs[i][...])

    flat = pl.pallas_call(
        body, name=name, out_shape=[jax.ShapeDtypeStruct(w.shape, F32) for w in ws for _ in range(4)],
        compiler_params=_params(),
    )(g_rows, *ws, *ms, *vs)
    return [flat[4 * i:4 * i + 4] for i in range(k)]


def adamw_whole(w, g, m, v, *, name):
    def body(w_ref, g_ref, m_ref, v_ref, d_ref, nm_ref, nv_ref):
        d_ref[...], nm_ref[...], nv_ref[...] = _adamw_math(w_ref[...], g_ref[...], m_ref[...], v_ref[...])

    return pl.pallas_call(body, name=name, out_shape=[jax.ShapeDtypeStruct(w.shape, F32)] * 3,
                          compiler_params=_params())(w, g, m, v)


def adamw_halves(w, g_mine, g_other, m, v, core, *, name):
    _, k, n = w.shape
    br = min(k // 2, max(8, ADAMW_BLOCK // n))
    nh = k // 2 // br

    def body(c_ref, w_ref, gm_ref, go_ref, m_ref, v_ref, g_out, d_ref, nm_ref, nv_ref):
        gv = jnp.where(pl.program_id(0) == c_ref[0], gm_ref[...], go_ref[...])
        g_out[0] = gv
        d_ref[0], nm_ref[0], nv_ref[0] = _adamw_math(w_ref[0], gv, m_ref[0], v_ref[0])

    full = pl.BlockSpec((1, br, n), lambda hb, i, c: (0, hb * nh + i, 0))
    half = pl.BlockSpec((br, n), lambda hb, i, c: (i, 0))
    return pl.pallas_call(
        body, name=name,
        grid_spec=pltpu.PrefetchScalarGridSpec(num_scalar_prefetch=1, grid=(2, nh), in_specs=[full, half, half, full, full],
                                               out_specs=[full] * 4),
        out_shape=[jax.ShapeDtypeStruct(w.shape, F32)] * 4,
        compiler_params=_params(("parallel", "parallel")),
    )(core, w, g_mine, g_other, m, v)


def adamw_halves_t(wt, mt, vt, gt_mine, gt_other, core, *, name, bc=256):
    n, _, k = wt.shape
    nh = k // 2 // bc

    def body(c_ref, w_ref, m_ref, v_ref, gm_ref, go_ref, g_out, d_ref, nm_ref, nv_ref):
        gv = jnp.where(pl.program_id(0) == c_ref[0], gm_ref[...], go_ref[...])
        g_out[:, 0, :] = gv
        d_ref[:, 0, :], nm_ref[:, 0, :], nv_ref[:, 0, :] = _adamw_math(w_ref[:, 0, :], gv, m_ref[:, 0, :], v_ref[:, 0, :])

    full = pl.BlockSpec((n, 1, bc), lambda hb, i, c: (0, 0, hb * nh + i))
    half = pl.BlockSpec((n, bc), lambda hb, i, c: (0, i))
    return pl.pallas_call(
        body, name=name,
        grid_spec=pltpu.PrefetchScalarGridSpec(num_scalar_prefetch=1, grid=(2, nh), in_specs=[full, full, full, half, half],
                                               out_specs=[full] * 4),
        out_shape=[jax.ShapeDtypeStruct(wt.shape, F32)] * 4,
        compiler_params=_params(("parallel", "parallel")),
    )(core, wt, mt, vt, gt_mine, gt_other)


def add_pair(dw, recv, core, *, name):
    n4, k, n = dw.shape
    half = (1, k // 2, n) if split_axis(k) == 0 else (1, k, n // 2)
    mine = (lambda q, c: (q, c[0], 0)) if split_axis(k) == 0 else (lambda q, c: (q, 0, c[0]))

    def body(c_ref, a_ref, b_ref, o_ref):
        o_ref[...] = (a_ref[...] + b_ref[...].astype(F32)).astype(BF16)

    return pl.pallas_call(
        body, name=name,
        grid_spec=pltpu.PrefetchScalarGridSpec(
            num_scalar_prefetch=1, grid=(n4,),
            in_specs=[pl.BlockSpec(half, mine), pl.BlockSpec(half, lambda q, c: (q, 0, 0))],
            out_specs=pl.BlockSpec(half, lambda q, c: (q, 0, 0))),
        out_shape=jax.ShapeDtypeStruct((n4,) + half[1:], BF16),
        compiler_params=_params(("parallel",)),
    )(core, dw, recv)


def sum_chips(parts, *, name):
    n4, r, n = parts.shape
    if r % 16 == 0:
        br, bc = _row_block(r), n
    else:
        br, bc = r, LANES

    def body(p_ref, o_ref):
        acc = p_ref[0].astype(F32)
        for q in range(1, n4):
            acc = acc + p_ref[q].astype(F32)
        o_ref[...] = acc

    return pl.pallas_call(
        body, name=name, grid=(r // br, n // bc),
        in_specs=[pl.BlockSpec((n4, br, bc), lambda i, j: (0, i, j))], out_specs=pl.BlockSpec((br, bc), lambda i, j: (i, j)),
        out_shape=jax.ShapeDtypeStruct((r, n), F32),
        compiler_params=_params(("parallel", "parallel")),
    )(parts)


ANY = pl.BlockSpec(memory_space=pl.ANY)


def _place():
    x, y, c = lax.axis_index("x"), lax.axis_index("y"), lax.axis_index("c")
    chips = [(1 - x, y), (x, 1 - y), (1 - x, 1 - y)]
    return x, y, c, chips


def _copy(src, dst, send_sems, recv_sems, k, to):
    return pltpu.make_async_remote_copy(src_ref=src, dst_ref=dst, send_sem=send_sems.at[k], recv_sem=recv_sems.at[k],
                                        device_id=to, device_id_type=MESH)


def split_axis(rows):
    return 0 if rows % 32 == 0 else 1


def _half(ref, lead, hf):
    rows, cols = ref.shape[-2:]
    if split_axis(rows) == 0:
        at = (pl.ds(hf * (rows // 2), rows // 2), slice(None))
    else:
        at = (slice(None), pl.ds(hf * (cols // 2), cols // 2))
    return ref.at[at] if lead is None else ref.at[(lead,) + at]


def _gather_first(srcs, dsts, ssems, rsems):
    x, y, c, chips = _place()
    for ti, (s, d) in enumerate(zip(srcs, dsts)):
        for j, (cx, cy) in enumerate(chips):
            _copy(_half(s, None, c), _half(d, 2 * x + y, c), ssems, rsems, 3 * ti + j, (cx, cy, c)).start()


def _gather_mid(srcs, dsts, ssems, rsems):
    x, y, c, chips = _place()
    n1 = 3 * len(srcs)
    for ti, d in enumerate(dsts):
        for j, (cx, cy) in enumerate(chips):
            landed = _half(d, 2 * cx + cy, c)
            _copy(landed, landed, ssems, rsems, 3 * ti + j, (cx, cy, c)).wait_recv()
            _copy(landed, landed, ssems, rsems, n1 + 3 * ti + j, (x, y, 1 - c)).start()


def _gather_last(srcs, dsts, ssems, rsems):
    x, y, c, chips = _place()
    n1 = 3 * len(srcs)
    for ti, (s, d) in enumerate(zip(srcs, dsts)):
        for j, (cx, cy) in enumerate(chips):
            other = _half(d, 2 * cx + cy, 1 - c)
            _copy(other, other, ssems, rsems, n1 + 3 * ti + j, (x, y, 1 - c)).wait_recv()
        for j, (cx, cy) in enumerate(chips):
            mine = _half(s, None, c)
            _copy(mine, mine, ssems, rsems, 3 * ti + j, (cx, cy, c)).wait_send()
            _copy(mine, mine, ssems, rsems, n1 + 3 * ti + j, (x, y, 1 - c)).wait_send()


def gather_side(shards):
    return Side(shards, [jax.ShapeDtypeStruct((N_CHIPS,) + s.shape, s.dtype) for s in shards], 6 * len(shards),
                _gather_first, _gather_last, _gather_mid)


def _scatter_first(srcs, dsts, ssems, rsems):
    x, y, c, chips = _place()
    for ti, (s, d) in enumerate(zip(srcs, dsts)):
        for j, (cx, cy) in enumerate(chips):
            _copy(s.at[2 * cx + cy], d.at[2 * x + y], ssems, rsems, 3 * ti + j, (cx, cy, c)).start()


def _scatter_last(srcs, dsts, ssems, rsems):
    x, y, c, chips = _place()
    for ti, (s, d) in enumerate(zip(srcs, dsts)):
        for j, (cx, cy) in enumerate(chips):
            _copy(s.at[2 * cx + cy], d.at[2 * cx + cy], ssems, rsems, 3 * ti + j, (cx, cy, c)).wait_recv()
        for j, (cx, cy) in enumerate(chips):
            _copy(s.at[2 * cx + cy], d.at[2 * cx + cy], ssems, rsems, 3 * ti + j, (cx, cy, c)).wait_send()


def scatter_side(parts):
    return Side(parts, [jax.ShapeDtypeStruct(p.shape, p.dtype) for p in parts], 3 * len(parts), _scatter_first, _scatter_last)


def run_side(side, *, name):
    n_in, n_out = len(side.ins), len(side.out_shapes)

    def body(*refs):
        si, so, sems = _split_refs(refs, [n_in, n_out, 2])
        side.first(si, so, *sems)
        if side.mid is not None:
            side.mid(si, so, *sems)
        side.last(si, so, *sems)

    in_specs, out_specs = side.specs()
    return pl.pallas_call(body, name=name, in_specs=in_specs, out_specs=out_specs, out_shape=side.out_shapes,
                          scratch_shapes=side.sems())(*side.ins)


def _swap_first(srcs, dsts, ssems, rsems):
    x, y, c, _ = _place()
    for k, (s, d) in enumerate(zip(srcs, dsts)):
        _copy(s, d, ssems, rsems, k, (x, y, 1 - c)).start()


def _swap_last(srcs, dsts, ssems, rsems):
    x, y, c, _ = _place()
    for k, (s, d) in enumerate(zip(srcs, dsts)):
        _copy(s, d, ssems, rsems, k, (x, y, 1 - c)).wait()


def swap_side(xs):
    return Side(xs, [jax.ShapeDtypeStruct(a.shape, a.dtype) for a in xs], len(xs), _swap_first, _swap_last)


def _swap_halves(srcs, dsts, ssems, rsems):
    x, y, c, _ = _place()
    for k, (s, d) in enumerate(zip(srcs, dsts)):
        hk = s.shape[1] // 2
        yield _copy(s.at[:, pl.ds((1 - c) * hk, hk), :], d, ssems, rsems, k, (x, y, 1 - c))


def _swap_halves_first(srcs, dsts, ssems, rsems):
    for cp in _swap_halves(srcs, dsts, ssems, rsems):
        cp.start()


def _swap_halves_last(srcs, dsts, ssems, rsems):
    for cp in _swap_halves(srcs, dsts, ssems, rsems):
        cp.wait()


def swap_halves_side(xs):
    return Side(xs, [jax.ShapeDtypeStruct((a.shape[0], a.shape[1] // 2, a.shape[2]), a.dtype) for a in xs], len(xs),
                _swap_halves_first, _swap_halves_last)


def allreduce_small(s):
    n_dev = 8

    def body(s_ref, out_ref, buf, send_sems, recv_sems):
        x, y, c, _ = _place()
        me = 4 * x + 2 * y + c
        buf[me] = s_ref[...]
        sends = []
        for k in range(1, n_dev):
            px = 1 - x if k & 4 else x
            py = 1 - y if k & 2 else y
            pc = 1 - c if k & 1 else c
            cp = _copy(s_ref, buf.at[me], send_sems, recv_sems, k - 1, (px, py, pc))
            cp.start()
            sends.append((cp, 4 * px + 2 * py + pc))
        for k, (cp, peer) in enumerate(sends):
            _copy(s_ref, buf.at[peer], send_sems, recv_sems, k, (x, y, c)).wait_recv()
        for cp, _ in sends:
            cp.wait_send()
        acc = buf[0]
        for d in range(1, n_dev):
            acc = acc + buf[d]
        out_ref[...] = acc

    vm = pl.BlockSpec(memory_space=pltpu.VMEM)
    return pl.pallas_call(
        body, name="allreduce_small", in_specs=[vm], out_specs=vm,
        out_shape=jax.ShapeDtypeStruct(s.shape, F32),
        scratch_shapes=[pltpu.VMEM((n_dev,) + s.shape, F32), pltpu.SemaphoreType.DMA((n_dev - 1,)),
                        pltpu.SemaphoreType.DMA((n_dev - 1,))],
    )(s)


def join_cols(sm):
    n4, k, n = sm.shape
    return sm.transpose(1, 0, 2).reshape(k, n4 * n)


def split_cols(full):
    k, n = full.shape
    return full.reshape(k, N_CHIPS, n // N_CHIPS).transpose(1, 0, 2)


def _pad_heads(w, width):
    lead, heads = w.shape[:-1], w.shape[-1] // width
    w = w.reshape(lead + (heads, width))
    return jnp.pad(w, [(0, 0)] * len(lead) + [(0, 0), (0, LANES - width)]).reshape(lead + (heads * LANES,))


def _unpad_heads(w, width):
    lead, heads = w.shape[:-1], w.shape[-1] // LANES
    return w.reshape(lead + (heads, LANES))[..., :width].reshape(lead + (heads * width,))


O_F = 3 * FW
O_CQ = O_F + N_HEADS
O_CKV = O_CQ + Q_RANK
O_KR = O_CKV + KV_RANK
O_END = O_KR + ROPE_DIM


def split_w_in_t(w_sm):
    n4, r, d = w_sm.shape
    segments = [(0, O_F, 0, 0), (O_F, N_HEADS, 1, S_F), (O_CQ, Q_RANK, 1, S_CQ), (O_CKV, KV_RANK, 1, S_CKV),
                (O_KR, ROPE_DIM, 1, S_KR + HEAD_DIM)]
    moves = []
    for v0, n, dst, d0 in segments:
        for q in range(n4):
            a, b = max(v0, q * r), min(v0 + n, (q + 1) * r)
            if a < b:
                assert a % 2 == 0 and b % 2 == 0 and (d0 + a - v0) % 2 == 0
                moves.append((q, (a - q * r) // 2, (b - a) // 2, dst, (d0 + a - v0) // 2))

    def body(w_ref, qkv_ref, small_ref, src, dst_qkv, dst_small):
        for q in range(n4):
            src[q] = pltpu.bitcast(w_ref[q], jnp.uint32)
        dst_small[...] = jnp.zeros(dst_small.shape, jnp.uint32)
        for q, a, n, dst, b in moves:
            (dst_qkv, dst_small)[dst][pl.ds(b, n), :] = src[q, pl.ds(a, n), :]
        qkv_ref[...] = pltpu.bitcast(dst_qkv[...], w_sm.dtype)
        small_ref[...] = pltpu.bitcast(dst_small[...], w_sm.dtype)

    return pl.pallas_call(
        body, name="split_w_in",
        out_shape=[jax.ShapeDtypeStruct((O_F, d), w_sm.dtype), jax.ShapeDtypeStruct((S_END, d), w_sm.dtype)],
        scratch_shapes=[pltpu.VMEM((n4, r // 2, d), jnp.uint32), pltpu.VMEM((O_F // 2, d), jnp.uint32),
                        pltpu.VMEM((S_END // 2, d), jnp.uint32)],
        compiler_params=_params(),
    )(w_sm)


def join_w_in_t(d_qkv_t, d_small_t):
    small = len(d_qkv_t)
    segments = [(n, 0, n * FW, FW) for n in range(small)]
    segments += [(small, S_F, O_F, N_HEADS), (small, S_CQ, O_CQ, Q_RANK), (small, S_CKV, O_CKV, KV_RANK),
                 (small, S_KR + HEAD_DIM, O_KR, ROPE_DIM)]
    r, d = O_END // N_CHIPS, d_small_t.shape[1]

    def body(*refs):
        o_ref = refs[-1]
        for src, s0, v0, n in segments:
            for q in range(N_CHIPS):
                a, b = max(v0, q * r), min(v0 + n, (q + 1) * r)
                if a < b:
                    o_ref[q, pl.ds(a - q * r, b - a), :] = refs[src][pl.ds(s0 + a - v0, b - a), :]

    return pl.pallas_call(
        body, name="join_w_in", out_shape=jax.ShapeDtypeStruct((N_CHIPS, r, d), F32), compiler_params=_params(),
    )(*d_qkv_t, d_small_t)


def rope_inputs(pos):
    inv_freq = ROPE_THETA ** (-jnp.arange(0, ROPE_DIM, 2, dtype=F32) / ROPE_DIM)
    inv_row = jnp.concatenate([jnp.zeros((HEAD_DIM,), F32), inv_freq, inv_freq, jnp.zeros((LANES - HEAD_DIM - ROPE_DIM,), F32)])
    return pos.astype(F32)[:, None], inv_row[None, :]


def _pad_lanes(v, n):
    return jnp.pad(v, ((0, 0), (0, n - v.shape[1])))


ATTN_BLK = 512
ATTN_FWD_BLK = 1024
ATTN_BWD_QBLK = 1024
ACC_ROWS = HEAD_DIM + 16


def local_step(xs, pos, tgt, gains, early_weights, late_weights, early_side=None, fwd_sides=(None, None), reduction=None):
    g_attn, b_forget, g_q, g_kv, g_fo, g_mo, g_mlp, g_fin = gains
    t = xs.shape[0]
    blk = min(ATTN_BLK, t)
    fox_scale = 1.0 / (HEAD_DIM ** 0.5)
    mla_scale = 1.0 / ((HEAD_DIM + ROPE_DIM) ** 0.5)

    h1, *gathered = rmsnorm(xs, g_attn, out_dtype=BF16, name="norm_attn", side=early_side)
    w_in_t, w_uq_p, w_ukv = early_weights(gathered)
    w_qkv_t, w_small_t = split_w_in_t(w_in_t)
    kv = w_ukv.reshape(KV_RANK, N_HEADS, 2 * HEAD_DIM)
    w_ukv_p = jnp.concatenate([_pad_heads(kv[:, :, :HEAD_DIM].reshape(KV_RANK, FW), HEAD_DIM),
                               kv[:, :, HEAD_DIM:].reshape(KV_RANK, FW)], axis=1)
    b_f = _pad_lanes(b_forget, LANES)
    pos_col, inv_row = rope_inputs(pos)

    qkv, qkv_t = mm(h1, w_qkv_t, trans_b=True, out_dtypes=[BF16], t_blk=blk, name="proj_qkv", bn=1536)
    small, = mm(h1, w_small_t, trans_b=True, out_dtypes=[F32], name="proj_small")
    mq, mk, mv, lf, cqn, ckvn, mq_t, mv_t = mla_prep(small, g_q, g_kv, w_uq_p, w_ukv_p, pos_col, inv_row, b_f,
                                                     name="mla_prep", bt=blk)
    f_cum = cumsum_rows(lf, reverse=False, name="gate_cumsum")
    f_blocks = f_cum[:, :N_HEADS].reshape(t // blk, blk, N_HEADS).transpose(0, 2, 1)
    fo, st_f, *gathered = flash_fwd(qkv_t, qkv, qkv_t, f_cum, qoff=0, koff=PAIRS, voff=2 * PAIRS, pair=True,
                                    scale=fox_scale, name="fox_fwd", blk=ATTN_FWD_BLK, side=fwd_sides[0])
    mo, st_m, *more = flash_fwd(mq_t, mk, mv_t, None, qoff=0, koff=0, voff=0, pair=False, scale=mla_scale, name="mla_fwd",
                                blk=ATTN_FWD_BLK, side=fwd_sides[1])
    w_o, w_up, w_down = late_weights(gathered + more)
    mixed = mix_norm(fo, mo, g_fo, g_mo, name="norm_mix")

    def inv_rms(v):
        return lax.rsqrt(jnp.mean(v * v, axis=-1, keepdims=True) + EPS)

    def residual_then_norm(acc, res, g):
        xn = acc + res
        return xn, xn * inv_rms(xn) * g

    def norm_bwd(dh, xn, res, g):
        r = inv_rms(xn)
        uu = dh * g
        return (r * uu - xn * (r * r * r * jnp.mean(uu * xn, axis=-1, keepdims=True)) + res,
                jnp.sum(dh * (xn * r), axis=0, keepdims=True))

    def norm_bwd2(dh, xn, res, g):
        dx, dg = norm_bwd(dh, xn, res, g)
        return dx, dx, dg

    def residual_then_loss(acc, res, target, g):
        xn = acc + res
        r = inv_rms(xn)
        xh = xn * r
        e = xh * g - target
        part = 0.5 * jnp.sum(jnp.mean(e * e, axis=-1, keepdims=True), axis=0, keepdims=True)
        dy = e * (1.0 / xn.shape[1])
        uu = dy * g
        dx = r * uu - xn * (r * r * r * jnp.mean(uu * xn, axis=-1, keepdims=True))
        return dx, dx, jnp.sum(dy * xh, axis=0, keepdims=True), part + jnp.zeros_like(g)

    x1, h2 = mm(mixed, w_o, extras=[xs], vecs=[g_mlp], epilogue=residual_then_norm, out_dtypes=[F32, BF16], name="out_proj")

    def relu2(uu):
        r = jnp.maximum(uu.astype(F32), 0.0)
        return (r * r).astype(BF16)

    u, = mm(h2, w_up, out_dtypes=[BF16], name="mlp_up", bn=2048)
    dx2, dx2b, dg_fin, loss_row = mm(u, w_down, a_pro=relu2, extras=[x1, tgt], vecs=[g_fin], epilogue=residual_then_loss,
                                     out_dtypes=[F32, BF16], n_sums=2, name="mlp_down_loss")
    loss = loss_row[:, :1]

    def relu2_grad(acc, uu):
        return (acc * (2.0 * jnp.maximum(uu.astype(F32), 0.0)),)

    du, = mm(dx2b, w_down, trans_b=True, extras=[u], epilogue=relu2_grad, out_dtypes=[BF16], name="mlp_down_bwd", bn=2048)
    dw_down, dw_down_b = (g.reshape(N_CHIPS, -1, w_down.shape[1])
                          for g in mm_tn(u, dx2b, a_pro=relu2, name="dw_down", bf16_copy=True))
    dx1, dx1b, dg_mlp = mm(du, w_up, trans_b=True, extras=[x1, dx2], vecs=[g_mlp], epilogue=norm_bwd2,
                           out_dtypes=[F32, BF16], n_sums=1, name="mlp_up_bwd")
    dw_up, dw_up_b = mm_tn(h2, du, name="dw_up", col_shards=N_CHIPS, bf16_copy=True)

    dw_o, dw_o_b = (g.reshape(N_CHIPS, -1, w_o.shape[1]) for g in mm_tn(mixed, dx1b, name="dw_o", bf16_copy=True))
    late, late_b = (dw_o, dw_up, dw_down), (dw_o_b, dw_up_b, dw_down_b)
    red = reduction
    dfo, dmo, dg_fo, dg_mo, st_f, st_m, dfo_t, dmo_t, *got = mix_norm_bwd(
        dx1b, w_o, fo, mo, g_fo, g_mo, st_f, st_m, name="out_proj_mix_bwd", bt=blk,
        side=red.late_swap(late, late_b) if red else None)
    dfq, dfk, dfv, d_f, *got = flash_bwd(
        qkv, qkv_t, qkv, qkv, dfo, dfo_t, st_f, f_blocks, qoff=0, koff=PAIRS, voff=2 * PAIRS, pair=True,
        scale=fox_scale, name="fox_bwd", qblk=ATTN_BWD_QBLK, side=red.late_scatter(got) if red else None)
    dmq, dmk, dmv, *got = flash_bwd(mq, mq_t, mk, mv, dmo, dmo_t, st_m, None, qoff=0, koff=0, voff=0,
                                    pair=False, scale=mla_scale, name="mla_bwd", qblk=ATTN_BWD_QBLK,
                                    side=red.late_halves(got) if red else None)
    if red:
        red.late_done(got)
    dlf = cumsum_rows(d_f, reverse=True, name="gate_cumsum_bwd")
    dsmall, dq_u, dkv_u, dg_q, dg_kv, db_f = mla_prep_bwd(dmq, dmk, dmv, dlf, small, g_q, g_kv, w_uq_p, w_ukv_p,
                                                          pos_col, inv_row, b_f, name="mla_prep_bwd")
    dw_uq_p = mm_tn(cqn, dq_u, name="dw_uq")
    dw_ukv_p = mm_tn(ckvn, dkv_u, name="dw_ukv")

    def to_bf16(tile):
        return tile.astype(BF16)

    dqkv = [dfq, dfk, dfv]
    dw_in_t = join_w_in_t([mm_tn(part, h1, a_pro=to_bf16, name="dw_" + nm) for part, nm in zip(dqkv, "qkv")],
                          mm_tn(dsmall, h1, name="dw_small"))
    dk_cols = _unpad_heads(dw_ukv_p[:, :HW], HEAD_DIM).reshape(KV_RANK, N_HEADS, HEAD_DIM)
    dv_cols = dw_ukv_p[:, HW:].reshape(KV_RANK, N_HEADS, HEAD_DIM)
    dw_ukv = jnp.concatenate([dk_cols, dv_cols], axis=2).reshape(KV_RANK, N_HEADS * 2 * HEAD_DIM)
    early = (dw_in_t, split_cols(dw_uq_p), split_cols(dw_ukv))
    w_parts = [w_qkv_t[n * FW:(n + 1) * FW] for n in range(3)]
    grad_x, dg_attn, *got = mm(dqkv + [dsmall], w_parts + [w_small_t], a_pro=to_bf16, extras=[xs, dx1], vecs=[g_attn],
                               epilogue=norm_bwd, out_dtypes=[F32], n_sums=1, name="proj_bwd",
                               side=red.early_scatter(early) if red else None)
    if red:
        red.early_done(got)
    d_gains = (dg_attn, db_f[:, :N_HEADS], dg_q, dg_kv, dg_fo, dg_mo, dg_mlp, dg_fin)
    return loss, grad_x, early, late, d_gains


class GradReduction:
    def __init__(self, core_id, chip):
        self.core_id, self.chip = core_id, chip
        self.core = core_id.reshape(1).astype(jnp.int32)
        self.grads, self.pairs, self.halves, self.others = {}, {}, {}, {}

    def _other_halves(self, grads):
        out = []
        for g in grads:
            axis = 1 + split_axis(g.shape[1])
            size = g.shape[axis] // 2
            out.append(lax.dynamic_slice_in_dim(g, (1 - self.core_id) * size, size, axis=axis).astype(BF16))
        return out

    def _add_pairs(self, group, recvs):
        self.pairs[group] = [add_pair(g, r, self.core, name="add_pair_%s_%d" % (group, n))
                             for n, (g, r) in enumerate(zip(self.grads[group], recvs))]
        return scatter_side(self.pairs[group])

    def _chip_sums(self, group, scattered):
        chip = self.chip
        with_mine = [lax.dynamic_update_index_in_dim(s, lax.dynamic_index_in_dim(p, chip, 0, keepdims=True), chip, 0)
                     for s, p in zip(scattered, self.pairs[group])]
        self.halves[group] = [sum_chips(s, name="sum_chips_%s_%d" % (group, n)) for n, s in enumerate(with_mine)]
        return self.halves[group]

    def late_swap(self, grads, bf16_copies):
        self.grads["late"] = list(grads)
        return swap_halves_side(list(bf16_copies))

    def late_scatter(self, recvs):
        return self._add_pairs("late", recvs)

    def late_halves(self, scattered):
        return swap_side(self._chip_sums("late", scattered))

    def late_done(self, others):
        self.others["late"] = list(others)

    def early_scatter(self, grads):
        self.grads["early"] = list(grads)
        return self._add_pairs("early", run_side(swap_side(self._other_halves(grads)), name="swap_early_sends"))

    def early_done(self, scattered):
        self.others["early"] = list(run_side(swap_side(self._chip_sums("early", scattered)), name="swap_early_halves"))

def kernel(x, positions, attn_norm_g, w_in, b_forget, q_norm_g, w_uq, kv_norm_g, w_ukv, fox_out_g, mla_out_g, w_o, mlp_norm_g, w_up, w_down, final_norm_g, loss_target, m_attn_norm_g, m_w_in, m_b_forget, m_q_norm_g, m_w_uq, m_kv_norm_g, m_w_ukv, m_fox_out_g, m_mla_out_g, m_w_o, m_mlp_norm_g, m_w_up, m_w_down, m_final_norm_g, v_attn_norm_g, v_w_in, v_b_forget, v_q_norm_g, v_w_uq, v_kv_norm_g, v_w_ukv, v_fox_out_g, v_mla_out_g, v_w_o, v_mlp_norm_g, v_w_up, v_w_down, v_final_norm_g):
    core_id = lax.axis_index("c")
    core = core_id.reshape(1).astype(jnp.int32)
    chip = 2 * lax.axis_index("x") + lax.axis_index("y")
    big = [w_in, w_uq, w_ukv, w_o, w_up, w_down]
    big_m = [m_w_in, m_w_uq, m_w_ukv, m_w_o, m_w_up, m_w_down]
    big_v = [v_w_in, v_w_uq, v_w_ukv, v_w_o, v_w_up, v_w_down]
    n_early = 3

    def vec(a):
        return a.reshape(1, -1)

    small = [attn_norm_g, b_forget, q_norm_g, kv_norm_g, fox_out_g, mla_out_g, mlp_norm_g, final_norm_g]
    small_m = [m_attn_norm_g, m_b_forget, m_q_norm_g, m_kv_norm_g, m_fox_out_g, m_mla_out_g, m_mlp_norm_g, m_final_norm_g]
    small_v = [v_attn_norm_g, v_b_forget, v_q_norm_g, v_kv_norm_g, v_fox_out_g, v_mla_out_g, v_mlp_norm_g, v_final_norm_g]
    gains = [vec(a) for a in small]

    views = [big[0][0].T, _pad_heads(big[1][0], HEAD_DIM + ROPE_DIM)] + [w[0] for w in big[2:]]
    shards = [v.astype(BF16) for v in views]

    def with_own(gathered, mine):
        return [lax.dynamic_update_index_in_dim(g, s, chip, 0) for g, s in zip(gathered, mine)]

    def early_weights(gathered):
        g_in, g_uq, g_ukv = with_own(gathered, shards[:n_early])
        return g_in, join_cols(g_uq), join_cols(g_ukv)

    def late_weights(gathered):
        g_o, g_up, g_down = with_own(gathered, shards[n_early:])
        return g_o.reshape(-1, g_o.shape[2]), join_cols(g_up), g_down.reshape(-1, g_down.shape[2])

    reduction = GradReduction(core_id, chip)
    loss, grad_x, _, _, d_small = local_step(
        x[0], positions[0], loss_target[0], gains, early_weights, late_weights, gather_side(shards[:n_early]),
        (gather_side(shards[n_early:-1]), gather_side(shards[-1:])), reduction)
    halves = reduction.halves["early"] + reduction.halves["late"]
    others = reduction.others["early"] + reduction.others["late"]

    def rows8(vs):
        return jnp.concatenate([_pad_lanes(vec(a).astype(F32), 1024) for a in vs], axis=0)

    with_loss = [jnp.concatenate([d, loss], axis=1) if n == 1 else d for n, d in enumerate(d_small)]
    g_small8 = allreduce_small(rows8(with_loss))

    outs_big = []
    for n, (w, gm, go, m, v) in enumerate(zip(big, halves, others, big_m, big_v)):
        if n == 0:
            outs = adamw_halves_t(*(jnp.transpose(a, (2, 0, 1)) for a in (w, m, v)), gm, go, core, name="adamw_%d" % n)
            outs_big.append([jnp.transpose(o, (1, 2, 0)) for o in outs])
        elif n == 1:
            gm, go = (_unpad_heads(gh, HEAD_DIM + ROPE_DIM) for gh in (gm, go))
            g_t = jnp.where(core[0] == 0, jnp.concatenate([gm, go], axis=0), jnp.concatenate([go, gm], axis=0)).T
            outs = (g_t,) + tuple(adamw_whole(w[0].T, g_t, m[0].T, v[0].T, name="adamw_%d" % n))
            outs_big.append([o.T[None] for o in outs])
        else:
            outs_big.append(adamw_halves(w, gm, go, m, v, core, name="adamw_%d" % n))
    outs_small = adamw_vectors(gains, g_small8, [vec(a) for a in small_m], [vec(a) for a in small_v], name="adamw_small")

    loss_all = g_small8[1, N_HEADS]
    grads, deltas, new_m, new_v = [None] * 14, [None] * 14, [None] * 14, [None] * 14
    big_at = [1, 4, 6, 9, 11, 12]
    small_at = [0, 2, 3, 5, 7, 8, 10, 13]
    for n, at in enumerate(big_at):
        grads[at], deltas[at], new_m[at], new_v[at] = outs_big[n]
    for at, s, outs in zip(small_at, small, outs_small):
        grads[at], deltas[at], new_m[at], new_v[at] = (o.reshape(s.shape) for o in outs)
    return (loss_all, grad_x[None], *grads, *deltas, *new_m, *new_v)
```

```python
import jax
import jax.numpy as jnp
from jax import lax
from jax.experimental import pallas as pl
from jax.experimental.pallas import tpu as pltpu

F32 = jnp.float32
BF16 = jnp.bfloat16
MESH = pl.DeviceIdType.MESH

EPS = 1e-6
ROPE_THETA = 10000.0
N_HEADS = 8
PAIRS = N_HEADS // 2
HEAD_DIM = 64
ROPE_DIM = 32
LANES = 128
Q_RANK = 384
KV_RANK = 256
N_CHIPS = 4
ADAM_LR, ADAM_B1, ADAM_B2, ADAM_EPS, ADAM_WD, ADAM_STEP = 0.001, 0.9, 0.999, 1e-08, 0.01, 10
VMEM_LIMIT = 48 * 1024 * 1024
ADAMW_BLOCK = 256 * 1024
LOG2E = 1.4426950408889634
LN2 = 0.6931471805599453
NN = (((1,), (0,)), ((), ()))
NT = (((1,), (1,)), ((), ()))
TN = (((0,), (0,)), ((), ()))


def _params(sem=None):
    return pltpu.CompilerParams(dimension_semantics=sem, vmem_limit_bytes=VMEM_LIMIT)


def _fit(block, dim):
    if dim <= block:
        return dim
    return next(b for b in range(block - block % LANES, 0, -LANES) if dim % b == 0)


def _row_block(rows):
    return next(b for b in (256, 128, 64, 32, 16, 8) if rows % b == 0)


def gridded(body, *, name, grid, in_specs, out_specs, out_shape, ins, semantics, side=None):
    if side is None:
        return pl.pallas_call(body, name=name, grid=grid, in_specs=in_specs, out_specs=out_specs, out_shape=out_shape,
                              compiler_params=_params(semantics))(*ins)
    n_in, n_out = len(in_specs), len(out_specs)
    steps = 1
    for extent in grid:
        steps *= extent

    def riding(*refs):
        main_in, s_in, main_out, s_out, sems = _split_refs(refs, [n_in, len(side.ins), n_out, len(side.out_shapes), 2])
        step = 0
        for axis, extent in enumerate(grid):
            step = step * extent + pl.program_id(axis)

        @pl.when(step == 0)
        def _():
            side.first(s_in, s_out, *sems)

        body(*main_in, *main_out)

        @pl.when(step == steps - 1)
        def _():
            if side.mid is not None:
                side.mid(s_in, s_out, *sems)
            side.last(s_in, s_out, *sems)

    s_in_specs, s_out_specs = side.specs()
    return pl.pallas_call(
        riding, name=name, grid=grid, in_specs=list(in_specs) + s_in_specs, out_specs=list(out_specs) + s_out_specs,
        out_shape=list(out_shape) + side.out_shapes, scratch_shapes=side.sems(),
        compiler_params=_params(("arbitrary",) * len(grid)))(*ins, *side.ins)


def rmsnorm(x, g, *, out_dtype, name, bt=512, side=None):
    t, d = x.shape
    bt = min(bt, t)

    def body(x_ref, g_ref, o_ref):
        xv = x_ref[...].astype(F32)
        r = lax.rsqrt(jnp.mean(xv * xv, axis=-1, keepdims=True) + EPS)
        o_ref[...] = (xv * r * g_ref[...]).astype(o_ref.dtype)

    return gridded(
        body, name=name, grid=(t // bt,),
        in_specs=[pl.BlockSpec((bt, d), lambda i: (i, 0)), pl.BlockSpec((1, d), lambda i: (0, 0))],
        out_specs=[pl.BlockSpec((bt, d), lambda i: (i, 0))],
        out_shape=[jax.ShapeDtypeStruct((t, d), out_dtype)],
        ins=[x, g], semantics=("parallel",), side=side)


def mm(a, b, *, trans_b=False, a_pro=None, extras=(), vecs=(), epilogue=None, out_dtypes, n_sums=0, t_blk=None, name,
       bm=1024, bn=1024, side=None):
    a_list = list(a) if isinstance(a, (list, tuple)) else [a]
    b_list = list(b) if isinstance(b, (list, tuple)) else [b]
    m = a_list[0].shape[0]
    n = b_list[0].shape[0] if trans_b else b_list[0].shape[1]
    ks = [x.shape[1] for x in a_list]
    if sum(ks) > 2048:
        bm = bm // 2
    bm, bn = _fit(bm, m), _fit(bn, n)
    assert n_sums == 0 or bn == n
    n_ab, n_ex, n_vec, n_out = len(a_list), len(extras), len(vecs), len(out_dtypes)
    n_t = 0 if t_blk is None else 1

    def body(*refs):
        a_refs, b_refs, ex, vs, outs, t_outs, sums = _split_refs(refs, [n_ab, n_ab, n_ex, n_vec, n_out, n_t, n_sums])
        acc = None
        for a_ref, b_ref in zip(a_refs, b_refs):
            a_tile = a_ref[...] if a_pro is None else a_pro(a_ref[...])
            part = lax.dot_general(a_tile, b_ref[...], NT if trans_b else NN, preferred_element_type=F32)
            acc = part if acc is None else acc + part
        res = epilogue(acc, *[e[...] for e in ex], *[v[...] for v in vs]) if epilogue is not None else (acc,)
        for o, r in zip(outs, res[:n_out]):
            o[...] = r.astype(o.dtype)
        for t_ref in t_outs:
            for u in range(bm // t_blk):
                t_ref[u] = res[0][u * t_blk:(u + 1) * t_blk, :].T.astype(t_ref.dtype)
        if n_sums:
            @pl.when(pl.program_id(0) == 0)
            def _():
                for s_ref in sums:
                    s_ref[...] = jnp.zeros_like(s_ref)

            for s_ref, r in zip(sums, res[n_out:]):
                s_ref[...] += r

    tile = pl.BlockSpec((bm, bn), lambda i, j: (i, j))
    vec = pl.BlockSpec((1, bn), lambda i, j: (0, j))
    t_specs, t_shapes = [], []
    if t_blk is not None:
        t_specs = [pl.BlockSpec((bm // t_blk, bn, t_blk), lambda i, j: (i, j, 0))]
        t_shapes = [jax.ShapeDtypeStruct((m // t_blk, n, t_blk), out_dtypes[0])]
    a_specs = [pl.BlockSpec((bm, k), lambda i, j: (i, 0)) for k in ks]
    b_specs = [pl.BlockSpec((bn, k), lambda i, j: (j, 0)) if trans_b else pl.BlockSpec((k, bn), lambda i, j: (0, j)) for k in ks]
    return gridded(
        body, name=name, grid=(m // bm, n // bn),
        in_specs=a_specs + b_specs + [tile] * n_ex + [vec] * n_vec,
        out_specs=[tile] * n_out + t_specs + [vec] * n_sums,
        out_shape=[jax.ShapeDtypeStruct((m, n), dt) for dt in out_dtypes] + t_shapes + [jax.ShapeDtypeStruct((1, n), F32)] * n_sums,
        ins=[*a_list, *b_list, *extras, *vecs],
        semantics=("arbitrary", "arbitrary") if n_sums else ("parallel", "parallel"), side=side)


def mm_tn(a, b, *, a_pro=None, name, col_shards=1, bf16_copy=False, bk=1024, bn=1024, bt=2048):
    t, k = a.shape
    n = b.shape[1]
    ns = n // col_shards
    bk, bn, bt = _fit(bk, k), _fit(bn, ns), _fit(bt, t)
    per = ns // bn
    last = t // bt - 1

    def body(a_ref, b_ref, o_ref, *copy_ref):
        @pl.when(pl.program_id(2) == 0)
        def _():
            o_ref[...] = jnp.zeros_like(o_ref)

        a_tile = a_ref[...] if a_pro is None else a_pro(a_ref[...])
        o_ref[...] += lax.dot_general(a_tile, b_ref[...], TN, preferred_element_type=F32)
        if bf16_copy:
            @pl.when(pl.program_id(2) == last)
            def _():
                copy_ref[0][...] = o_ref[...].astype(BF16)

    if col_shards == 1:
        out_spec = pl.BlockSpec((bk, bn), lambda i, j, s: (i, j))
        shape = (k, n)
    else:
        out_spec = pl.BlockSpec((None, bk, bn), lambda i, j, s: (j // per, i, j % per))
        shape = (col_shards, k, ns)
    out_specs, out_shape = out_spec, jax.ShapeDtypeStruct(shape, F32)
    if bf16_copy:
        out_specs, out_shape = [out_spec, out_spec], [out_shape, jax.ShapeDtypeStruct(shape, BF16)]
    return pl.pallas_call(
        body, name=name, grid=(k // bk, n // bn, t // bt),
        in_specs=[pl.BlockSpec((bt, bk), lambda i, j, s: (s, i)), pl.BlockSpec((bt, bn), lambda i, j, s: (s, j))],
        out_specs=out_specs, out_shape=out_shape,
        compiler_params=_params(("parallel", "parallel", "arbitrary")),
    )(a, b)


def _split3(x):
    hi = x.astype(BF16)
    r1 = x - hi.astype(F32)
    mid = r1.astype(BF16)
    lo = (r1 - mid.astype(F32)).astype(BF16)
    return hi, mid, lo


def cumsum_rows(x, *, reverse, name, bc=512):
    t, d = x.shape
    bc = min(bc, t)
    nb = t // bc

    def body(x_ref, o_ref, carry):
        @pl.when(pl.program_id(0) == 0)
        def _():
            carry[...] = jnp.zeros_like(carry)

        r = lax.broadcasted_iota(jnp.int32, (bc, bc), 0)
        c = lax.broadcasted_iota(jnp.int32, (bc, bc), 1)
        tri = jnp.where((r <= c) if reverse else (r >= c), 1.0, 0.0).astype(BF16)
        hi, mid, lo = _split3(x_ref[...])
        s = (lax.dot_general(tri, hi, NN, preferred_element_type=F32)
             + lax.dot_general(tri, mid, NN, preferred_element_type=F32)
             + lax.dot_general(tri, lo, NN, preferred_element_type=F32)) + carry[0:1, :]
        o_ref[...] = s
        carry[0:1, :] = s[0:1, :] if reverse else s[bc - 1:bc, :]

    imap = (lambda i: (nb - 1 - i, 0)) if reverse else (lambda i: (i, 0))
    return pl.pallas_call(
        body, name=name, grid=(nb,),
        in_specs=[pl.BlockSpec((bc, d), imap)], out_specs=pl.BlockSpec((bc, d), imap),
        out_shape=jax.ShapeDtypeStruct((t, d), F32),
        scratch_shapes=[pltpu.VMEM((8, d), F32)],
        compiler_params=_params(("arbitrary",)),
    )(x)


def _rope(x, c, a, b):
    return x * c + pltpu.roll(x, LANES - ROPE_DIM // 2, 1) * a + pltpu.roll(x, ROPE_DIM // 2, 1) * b


def _rope_bwd(d, c, a, b):
    return d * c + pltpu.roll(d * a, ROPE_DIM // 2, 1) + pltpu.roll(d * b, LANES - ROPE_DIM // 2, 1)


S_CQ, S_CKV, S_KR, S_F, S_END = 0, Q_RANK, Q_RANK + KV_RANK, Q_RANK + KV_RANK + LANES, 1024
HW = N_HEADS * LANES
FW = N_HEADS * HEAD_DIM


def _rope_tables(pos_col, inv_row):
    ang = pos_col * inv_row
    cos, sin = jnp.cos(ang), jnp.sin(ang)
    lane = lax.broadcasted_iota(jnp.int32, (1, LANES), 1)
    first = (lane >= HEAD_DIM) & (lane < HEAD_DIM + ROPE_DIM // 2)
    second = (lane >= HEAD_DIM + ROPE_DIM // 2) & (lane < HEAD_DIM + ROPE_DIM)
    return (jnp.where(lane < HEAD_DIM, 1.0, jnp.where(first | second, cos, 0.0)), jnp.where(first, -sin, 0.0),
            jnp.where(second, sin, 0.0))


def mla_prep(small, g_q, g_kv, w_uq, w_ukv, pos_col, inv_row, b_f, *, name, bt=512):
    t = small.shape[0]
    bt = min(bt, t)

    def body(s_ref, gq_ref, gkv_ref, wq_ref, wkv_ref, pos_ref, inv_ref, bf_ref,
             mq_ref, mk_ref, mv_ref, lf_ref, cqn_ref, ckvn_ref, mqt_ref, mvt_ref):
        cq = s_ref[:, S_CQ:S_CKV]
        rq = lax.rsqrt(jnp.mean(cq * cq, axis=-1, keepdims=True) + EPS)
        cqn = (cq * rq * gq_ref[...]).astype(BF16)
        ckv = s_ref[:, S_CKV:S_KR]
        rkv = lax.rsqrt(jnp.mean(ckv * ckv, axis=-1, keepdims=True) + EPS)
        ckvn = (ckv * rkv * gkv_ref[...]).astype(BF16)
        cqn_ref[...] = cqn
        ckvn_ref[...] = ckvn
        tc, ta, tb = _rope_tables(pos_ref[...], inv_ref[...])
        q = jnp.dot(cqn, wq_ref[...], preferred_element_type=F32)
        kv = jnp.dot(ckvn, wkv_ref[...], preferred_element_type=F32)
        kr = _rope(s_ref[:, S_KR:S_F], tc, ta, tb)
        for h in range(N_HEADS):
            sl = slice(h * LANES, (h + 1) * LANES)
            roped = _rope(q[:, sl], tc, ta, tb)
            mq_ref[:, sl] = roped.astype(BF16)
            mqt_ref[0, sl, :] = roped.T.astype(BF16)
            mk_ref[:, sl] = (kv[:, sl] + kr).astype(BF16)
        mv_ref[...] = kv[:, HW:].astype(BF16)
        mvt_ref[0] = kv[:, HW:].T.astype(BF16)
        z = s_ref[:, S_F:S_END - LANES] + bf_ref[...]
        lf_ref[...] = jnp.minimum(z, 0.0) - jnp.log(1.0 + jnp.exp(-jnp.abs(z)))

    def row(w):
        return pl.BlockSpec((bt, w), lambda i: (i, 0))

    def full(arr):
        return pl.BlockSpec(arr.shape, lambda i: (0, 0))

    return pl.pallas_call(
        body, name=name, grid=(t // bt,),
        in_specs=[row(S_END), full(g_q), full(g_kv), full(w_uq), full(w_ukv), row(1), full(inv_row), full(b_f)],
        out_specs=[row(HW), row(HW), row(FW), row(LANES), row(Q_RANK), row(KV_RANK),
                   pl.BlockSpec((1, HW, bt), lambda i: (i, 0, 0)), pl.BlockSpec((1, FW, bt), lambda i: (i, 0, 0))],
        out_shape=[jax.ShapeDtypeStruct((t, HW), BF16)] * 2 + [jax.ShapeDtypeStruct((t, FW), BF16), jax.ShapeDtypeStruct((t, LANES), F32),
                   jax.ShapeDtypeStruct((t, Q_RANK), BF16), jax.ShapeDtypeStruct((t, KV_RANK), BF16),
                   jax.ShapeDtypeStruct((t // bt, HW, bt), BF16), jax.ShapeDtypeStruct((t // bt, FW, bt), BF16)],
        compiler_params=_params(("parallel",)),
    )(small, g_q, g_kv, w_uq, w_ukv, pos_col, inv_row, b_f)


def mla_prep_bwd(dmq, dmk, dmv, dlf, small, g_q, g_kv, w_uq, w_ukv, pos_col, inv_row, b_f, *, name, bt=512):
    t = small.shape[0]
    bt = min(bt, t)

    def body(dmq_ref, dmk_ref, dmv_ref, dlf_ref, s_ref, gq_ref, gkv_ref, wq_ref, wkv_ref, pos_ref, inv_ref, bf_ref,
             ds_ref, dq_ref, dkv_ref, dgq_ref, dgkv_ref, db_ref):
        tc, ta, tb = _rope_tables(pos_ref[...], inv_ref[...])
        lane = lax.broadcasted_iota(jnp.int32, (1, LANES), 1)
        dkr = jnp.zeros((bt, LANES), F32)
        for h in range(N_HEADS):
            sl = slice(h * LANES, (h + 1) * LANES)
            dq_ref[:, sl] = _rope_bwd(dmq_ref[:, sl], tc, ta, tb).astype(BF16)
            dkr = dkr + dmk_ref[:, sl]
        dkv_ref[:, :HW] = dmk_ref[...].astype(BF16)
        dkv_ref[:, HW:] = dmv_ref[...].astype(BF16)
        in_rope = (lane >= HEAD_DIM) & (lane < HEAD_DIM + ROPE_DIM)
        ds_ref[:, S_KR:S_F] = jnp.where(in_rope, _rope_bwd(dkr, tc, ta, tb), 0.0).astype(BF16)

        def norm_bwd(raw, g_ref, dn, dg_ref):
            r = lax.rsqrt(jnp.mean(raw * raw, axis=-1, keepdims=True) + EPS)
            u = dn * g_ref[...]
            dot = jnp.mean(u * raw, axis=-1, keepdims=True)
            dg_ref[...] += jnp.sum(dn * (raw * r), axis=0, keepdims=True)
            return r * u - raw * (r * r * r * dot)

        @pl.when(pl.program_id(0) == 0)
        def _():
            dgq_ref[...] = jnp.zeros_like(dgq_ref)
            dgkv_ref[...] = jnp.zeros_like(dgkv_ref)
            db_ref[...] = jnp.zeros_like(db_ref)

        dcqn = lax.dot_general(dq_ref[...], wq_ref[...], NT, preferred_element_type=F32)
        ds_ref[:, S_CQ:S_CKV] = norm_bwd(s_ref[:, S_CQ:S_CKV], gq_ref, dcqn, dgq_ref).astype(BF16)
        dckvn = lax.dot_general(dkv_ref[...], wkv_ref[...], NT, preferred_element_type=F32)
        ds_ref[:, S_CKV:S_KR] = norm_bwd(s_ref[:, S_CKV:S_KR], gkv_ref, dckvn, dgkv_ref).astype(BF16)
        z = s_ref[:, S_F:S_END - LANES] + bf_ref[...]
        dz = jnp.where(lane < N_HEADS, dlf_ref[...] / (1.0 + jnp.exp(z)), 0.0)
        db_ref[...] += jnp.sum(dz, axis=0, keepdims=True)
        ds_ref[:, S_F:S_END - LANES] = dz.astype(BF16)
        ds_ref[:, S_END - LANES:] = jnp.zeros((bt, LANES), BF16)

    def row(w):
        return pl.BlockSpec((bt, w), lambda i: (i, 0))

    def full(arr):
        return pl.BlockSpec(arr.shape, lambda i: (0, 0))

    def vec(w):
        return pl.BlockSpec((1, w), lambda i: (0, 0))

    return pl.pallas_call(
        body, name=name, grid=(t // bt,),
        in_specs=[row(HW), row(HW), row(FW), row(LANES), row(S_END), full(g_q), full(g_kv), full(w_uq), full(w_ukv),
                  row(1), full(inv_row), full(b_f)],
        out_specs=[row(S_END), row(HW), row(HW + FW), vec(Q_RANK), vec(KV_RANK), vec(LANES)],
        out_shape=[jax.ShapeDtypeStruct((t, S_END), BF16), jax.ShapeDtypeStruct((t, HW), BF16),
                   jax.ShapeDtypeStruct((t, HW + FW), BF16), jax.ShapeDtypeStruct((1, Q_RANK), F32),
                   jax.ShapeDtypeStruct((1, KV_RANK), F32), jax.ShapeDtypeStruct((1, LANES), F32)],
        compiler_params=_params(("arbitrary",)),
    )(dmq, dmk, dmv, dlf, small, g_q, g_kv, w_uq, w_ukv, pos_col, inv_row, b_f)


class Side:
    def __init__(self, ins, out_shapes, n_sems, first, last, mid=None):
        self.ins, self.out_shapes, self.n_sems = list(ins), list(out_shapes), n_sems
        self.first, self.mid, self.last = first, mid, last

    def specs(self):
        return [ANY] * len(self.ins), [ANY] * len(self.out_shapes)

    def sems(self):
        return [pltpu.SemaphoreType.DMA((self.n_sems,)), pltpu.SemaphoreType.DMA((self.n_sems,))]


def _lane():
    return lax.broadcasted_iota(jnp.int32, (1, LANES), 1)


def _halves(x):
    zero = jnp.zeros_like(x)
    return [jnp.where(_lane() < HEAD_DIM, x, zero), jnp.where(_lane() >= HEAD_DIM, x, zero)]


def _groups(x):
    return [x[:, :LANES], x[:, LANES:]]


def _pick_row(tile, h):
    row = lax.broadcasted_iota(jnp.int32, (tile.shape[0], 1), 0)
    return jnp.sum(jnp.where(row == h, tile, 0.0), axis=0, keepdims=True)


def _pick_lane(tile, h):
    return jnp.sum(jnp.where(_lane() == h, tile, 0.0), axis=1, keepdims=True)


def _row_halves(x):
    row = lax.broadcasted_iota(jnp.int32, (LANES, 1), 0)
    zero = jnp.zeros_like(x)
    return [jnp.where(row < HEAD_DIM, x, zero), jnp.where(row >= HEAD_DIM, x, zero)]


def _below_diagonal(s):
    r = lax.broadcasted_iota(jnp.int32, s.shape, 0)
    c = lax.broadcasted_iota(jnp.int32, s.shape, 1)
    return jnp.where(c <= r, s, -jnp.inf)


def _above_diagonal(s):
    r = lax.broadcasted_iota(jnp.int32, s.shape, 0)
    c = lax.broadcasted_iota(jnp.int32, s.shape, 1)
    return jnp.where(r <= c, s, -jnp.inf)


def _split_refs(refs, counts):
    out, at = [], 0
    for n in counts:
        out.append(refs[at:at + n])
        at += n
    return out


def flash_fwd(qt_arr, k_arr, vt_arr, f_cum, *, qoff, koff, voff, pair, scale, name, blk=512, side=None):
    t = k_arr.shape[0]
    tblk = qt_arr.shape[2]
    blk = max(min(blk, t), tblk)
    sub = blk // tblk
    nb = t // blk
    steps = PAIRS * nb
    w = LANES if pair else 2 * LANES
    has_bias = f_cum is not None
    ins = [qt_arr, k_arr, vt_arr] + ([f_cum] if has_bias else [])

    def wide(ref, first):
        parts = [ref[first + u] for u in range(sub)]
        return parts[0] if sub == 1 else jnp.concatenate(parts, axis=1)
    s_ins, s_outs = (side.ins, side.out_shapes) if side else ([], [])

    def body(*refs):
        main, si, outs, so, sems = _split_refs(refs, [len(ins), len(s_ins), 2, len(s_outs), 2 if side else 0])
        qt_ref, k_ref, vt_ref = main[:3]
        f_ref = main[3] if has_bias else None
        o_ref, st_ref = outs
        g, i = pl.program_id(0), pl.program_id(1)
        step_id = g * nb + i
        if side:
            @pl.when(step_id == 0)
            def _():
                side.first(si, so, *sems)

            if side.mid is not None:
                @pl.when(step_id == (3 * steps) // 4)
                def _():
                    side.mid(si, so, *sems)

        qt = (wide(qt_ref, 0).astype(F32) * (scale * LOG2E)).astype(BF16)
        qts = _row_halves(qt) if pair else [qt[:LANES], qt[LANES:]]

        def k_of(kk, n):
            return kk if pair else kk[:, n * LANES:(n + 1) * LANES]

        def with_ones(vt_rows):
            return jnp.concatenate([vt_rows, jnp.ones((ACC_ROWS - HEAD_DIM, vt_rows.shape[1]), BF16)], axis=0)

        def step(j, carry, diagonal):
            rows = pl.ds(pl.multiple_of(j * blk, blk), blk)
            kk = k_ref[rows, :]
            vt = wide(vt_ref, sub * j)
            out = []
            for n in range(2):
                m, acc = carry[n]
                s = jnp.dot(k_of(kk, n), qts[n], preferred_element_type=F32)
                if has_bias:
                    s = s - LOG2E * _pick_lane(f_ref[rows, :], 2 * g + n)
                if diagonal:
                    s = _above_diagonal(s)
                m_new = jnp.maximum(m, jnp.max(s, axis=0, keepdims=True))
                p = jnp.exp2(s - m_new).astype(BF16)
                out.append((m_new, jnp.exp2(m - m_new) * acc
                            + jnp.dot(with_ones(vt[n * HEAD_DIM:(n + 1) * HEAD_DIM]), p, preferred_element_type=F32)))
            return tuple(out)

        def diagonal_in_halves(carry):
            h = tblk
            halves = [pl.ds(pl.multiple_of(i * blk, blk), h), pl.ds(pl.multiple_of(i * blk + h, h), h)]
            k_top, k_bot = k_ref[halves[0], :], k_ref[halves[1], :]
            vt_top, vt_bot = vt_ref[sub * i], vt_ref[sub * i + 1]
            out = []
            for n in range(2):
                m, acc = carry[n]
                heads = slice(n * HEAD_DIM, (n + 1) * HEAD_DIM)
                s_top = jnp.dot(k_of(k_top, n), qts[n], preferred_element_type=F32)
                s_bot = jnp.dot(k_of(k_bot, n), qts[n][:, h:], preferred_element_type=F32)
                if has_bias:
                    s_top = s_top - LOG2E * _pick_lane(f_ref[halves[0], :], 2 * g + n)
                    s_bot = s_bot - LOG2E * _pick_lane(f_ref[halves[1], :], 2 * g + n)
                s_top, s_bot = _above_diagonal(s_top), _above_diagonal(s_bot)
                m_top = jnp.maximum(m, jnp.max(s_top, axis=0, keepdims=True))
                m_new = jnp.concatenate([m_top[:, :h], jnp.maximum(m_top[:, h:], jnp.max(s_bot, axis=0, keepdims=True))], axis=1)
                p_top = jnp.exp2(s_top - m_new).astype(BF16)
                p_bot = jnp.exp2(s_bot - m_new[:, h:]).astype(BF16)
                late = jnp.concatenate([jnp.zeros((ACC_ROWS, h), F32),
                                        jnp.dot(with_ones(vt_bot[heads]), p_bot, preferred_element_type=F32)], axis=1)
                out.append((m_new, jnp.exp2(m - m_new) * acc
                            + jnp.dot(with_ones(vt_top[heads]), p_top, preferred_element_type=F32) + late))
            return tuple(out)

        init = tuple((jnp.full((1, blk), -jnp.inf, F32), jnp.zeros((ACC_ROWS, blk), F32)) for _ in range(2))
        carry = lax.fori_loop(0, i, lambda j, c: step(j, c, False), init)
        (ma, acca), (mb, accb) = diagonal_in_halves(carry) if sub == 2 else step(i, carry, True)
        la, lb = acca[HEAD_DIM:HEAD_DIM + 1], accb[HEAD_DIM:HEAD_DIM + 1]
        o_ref[...] = jnp.concatenate([acca[:HEAD_DIM] / la, accb[:HEAD_DIM] / lb], axis=0).T
        row = lax.broadcasted_iota(jnp.int32, (LANES, 1), 0)
        st_ref[0] = jnp.where(row == 0, ma + jnp.log2(la), jnp.where(row == 1, mb + jnp.log2(lb), 0.0)).T
        if side:
            @pl.when(step_id == steps - 1)
            def _():
                side.last(si, so, *sems)

    in_specs = [pl.BlockSpec((sub, w, tblk), lambda g, i: (i, qoff + g, 0)), pl.BlockSpec((t, w), lambda g, i: (0, koff + g)),
                pl.BlockSpec((t // tblk, LANES, tblk), lambda g, i: (0, voff + g, 0))]
    if has_bias:
        in_specs.append(pl.BlockSpec((t, LANES), lambda g, i: (0, 0)))
    s_in_specs, s_out_specs = side.specs() if side else ([], [])
    return pl.pallas_call(
        body, name=name, grid=(PAIRS, nb), in_specs=in_specs + s_in_specs,
        out_specs=[pl.BlockSpec((blk, LANES), lambda g, i: (i, g)), pl.BlockSpec((1, blk, LANES), lambda g, i: (g, i, 0))]
        + s_out_specs,
        out_shape=[jax.ShapeDtypeStruct((t, PAIRS * LANES), F32), jax.ShapeDtypeStruct((PAIRS, t, LANES), F32)] + list(s_outs),
        scratch_shapes=side.sems() if side else [],
        compiler_params=_params(("arbitrary", "arbitrary")),
    )(*ins, *s_ins)


def mix_norm(fo, mo, g_fo, g_mo, *, name, bt=512):
    t, d = fo.shape
    bt = min(bt, t)

    def body(fo_ref, mo_ref, gf_ref, gm_ref, o_ref):
        for n, (x_ref, g_ref) in enumerate(((fo_ref, gf_ref), (mo_ref, gm_ref))):
            xv = x_ref[...]
            r = lax.rsqrt(jnp.mean(xv * xv, axis=-1, keepdims=True) + EPS)
            o_ref[:, n * d:(n + 1) * d] = (xv * r * g_ref[...]).astype(BF16)

    row = pl.BlockSpec((bt, d), lambda i: (i, 0))
    vec = pl.BlockSpec((1, d), lambda i: (0, 0))
    return pl.pallas_call(
        body, name=name, grid=(t // bt,), in_specs=[row, row, vec, vec],
        out_specs=pl.BlockSpec((bt, 2 * d), lambda i: (i, 0)),
        out_shape=jax.ShapeDtypeStruct((t, 2 * d), BF16),
        compiler_params=_params(("parallel",)),
    )(fo, mo, g_fo, g_mo)


def mix_norm_bwd(dx1b, w_o, fo, mo, g_fo, g_mo, st_f, st_m, *, name, bt=512, side=None):
    t, d = fo.shape
    bt = min(bt, t)

    def body(dx_in_ref, w_ref, fo_ref, mo_ref, gf_ref, gm_ref, sf_ref, sm_ref, dfo_ref, dmo_ref, dgf_ref, dgm_ref,
             sfo_ref, smo_ref, dfot_ref, dmot_ref):
        @pl.when(pl.program_id(0) == 0)
        def _():
            dgf_ref[...] = jnp.zeros_like(dgf_ref)
            dgm_ref[...] = jnp.zeros_like(dgm_ref)

        dmixed = lax.dot_general(dx_in_ref[...], w_ref[...], NT, preferred_element_type=F32)
        groups = ((fo_ref, gf_ref, dfo_ref, dgf_ref, sf_ref, sfo_ref, dfot_ref),
                  (mo_ref, gm_ref, dmo_ref, dgm_ref, sm_ref, smo_ref, dmot_ref))
        for n, (x_ref, g_ref, dx_ref, dg_ref, st_ref, sto_ref, dxt_ref) in enumerate(groups):
            xv = x_ref[...]
            dhv = dmixed[:, n * d:(n + 1) * d]
            r = lax.rsqrt(jnp.mean(xv * xv, axis=-1, keepdims=True) + EPS)
            u = dhv * g_ref[...]
            dxf = r * u - xv * (r * r * r * jnp.mean(u * xv, axis=-1, keepdims=True))
            dxb = dxf.astype(BF16)
            dx_ref[...] = dxb
            dxt_ref[0] = dxf.T.astype(BF16)
            dg_ref[...] += jnp.sum(dhv * (xv * r), axis=0, keepdims=True)
            prod = xv * dxb.astype(F32)
            for g in range(PAIRS):
                grp = prod[:, g * LANES:(g + 1) * LANES]
                da = jnp.sum(jnp.where(_lane() < HEAD_DIM, grp, 0.0), axis=1, keepdims=True)
                db = jnp.sum(jnp.where(_lane() >= HEAD_DIM, grp, 0.0), axis=1, keepdims=True)
                sto_ref[g] = jnp.where(_lane() == 2, da, jnp.where(_lane() == 3, db, st_ref[g]))

    row = pl.BlockSpec((bt, d), lambda i: (i, 0))
    vec = pl.BlockSpec((1, d), lambda i: (0, 0))
    stat = pl.BlockSpec((PAIRS, bt, LANES), lambda i: (0, i, 0))
    return gridded(
        body, name=name, grid=(t // bt,),
        in_specs=[pl.BlockSpec((bt, dx1b.shape[1]), lambda i: (i, 0)), pl.BlockSpec(w_o.shape, lambda i: (0, 0)),
                  row, row, vec, vec, stat, stat],
        out_specs=[row, row, vec, vec, stat, stat] + [pl.BlockSpec((1, d, bt), lambda i: (i, 0, 0))] * 2,
        out_shape=[jax.ShapeDtypeStruct((t, d), BF16)] * 2 + [jax.ShapeDtypeStruct((1, d), F32)] * 2
        + [jax.ShapeDtypeStruct(st_f.shape, F32)] * 2 + [jax.ShapeDtypeStruct((t // bt, d, bt), BF16)] * 2,
        ins=[dx1b, w_o, fo, mo, g_fo, g_mo, st_f, st_m], semantics=("arbitrary",), side=side)


def flash_bwd(q_arr, qt_arr, k_arr, v_arr, do_arr, dot_arr, st, f_blocks, *, qoff, koff, voff, pair, scale, name, qblk=1024,
              side=None):
    t = q_arr.shape[0]
    blk = qt_arr.shape[2]
    qblk = max(min(qblk, t), blk)
    sub = qblk // blk
    nb, nbq = t // blk, t // qblk
    w = LANES if pair else 2 * LANES
    hw = w // 2
    has_bias = f_blocks is not None
    split = _halves if pair else _groups
    ins = [q_arr, qt_arr, k_arr, v_arr, do_arr, dot_arr, st] + ([f_blocks] if has_bias else [])
    n_out = 4 if has_bias else 3
    s_ins, s_outs = (side.ins, side.out_shapes) if side else ([], [])

    def wide(ref, first, count):
        parts = [ref[first + u] for u in range(count)]
        return parts[0] if count == 1 else jnp.concatenate(parts, axis=1)

    def body(*refs):
        main, si, outs, so, sems = _split_refs(refs, [len(ins), len(s_ins), n_out, len(s_outs), 2 if side else 0])
        q_ref, qt_ref, k_ref, v_ref, do_ref, dot_ref, st_ref = main[:7]
        dq_ref, dk_ref, dv_ref = outs[:3]
        g, j = pl.program_id(0), pl.program_id(1)
        step_id = g * nb + j
        if side:
            @pl.when(step_id == 0)
            def _():
                side.first(si, so, *sems)

        @pl.when(j == 0)
        def _():
            dq_ref[...] = jnp.zeros_like(dq_ref)

        kk, vv = k_ref[...], v_ref[...]
        ks = [kk, kk] if pair else _groups(kk)
        if has_bias:
            f_ref, df_ref = main[7], outs[3]
            fk = [LOG2E * _pick_row(f_ref[0], 2 * g + n) for n in range(2)]

            @pl.when(step_id == 0)
            def _():
                df_ref[...] = jnp.zeros_like(df_ref)

        def step(tb, count, carry, diagonal):
            rows = pl.ds(pl.multiple_of(tb * blk, blk), count * blk)
            qs = split((q_ref[rows, :].astype(F32) * (scale * LOG2E)).astype(BF16))
            qt = (wide(qt_ref, tb, count).astype(F32) * (scale * LOG2E)).astype(BF16)
            dos = [h.astype(BF16) for h in _halves(do_ref[rows, :].astype(F32))]
            dot = wide(dot_ref, tb, count)
            stats = st_ref[0, rows, :]
            new, dqs, row_sums = [], [], []
            for n in range(2):
                dkt, dvt, dfk = carry[n]
                s = lax.dot_general(qs[n], ks[n], NT, preferred_element_type=F32)
                if has_bias:
                    s = s - fk[n]
                if diagonal:
                    s = _below_diagonal(s)
                p = jnp.exp2(s - stats[:, n:n + 1])
                dp = lax.dot_general(dos[n], vv, NT, preferred_element_type=F32)
                ds = p * (dp - stats[:, 2 + n:3 + n])
                dsb = ds.astype(BF16)
                dvt = dvt + jnp.dot(dot[n * HEAD_DIM:(n + 1) * HEAD_DIM], p.astype(BF16), preferred_element_type=F32)
                dkt = dkt + jnp.dot(qt[n * hw:(n + 1) * hw], dsb, preferred_element_type=F32)
                dqs.append(jnp.dot(dsb, ks[n], preferred_element_type=F32))
                if has_bias:
                    dfk = dfk - jnp.sum(ds, axis=0, keepdims=True)
                    row_sums.append(jnp.sum(ds, axis=1, keepdims=True))
                new.append((dkt, dvt, dfk))
            dq = (jnp.where(_lane() < HEAD_DIM, dqs[0], dqs[1]) if pair else jnp.concatenate(dqs, axis=1)) * scale
            if has_bias:
                df_ref[rows, :] += jnp.where(_lane() == 2 * g, row_sums[0], jnp.where(_lane() == 2 * g + 1, row_sums[1], 0.0))
            dq_ref[rows, :] += dq
            return tuple(new)

        init = tuple((jnp.zeros((hw, blk), F32), jnp.zeros((HEAD_DIM, blk), F32), jnp.zeros((1, blk), F32)) for _ in range(2))
        carry = step(j, 1, init, True)
        whole = j // sub + 1
        carry = lax.fori_loop(j + 1, whole * sub, lambda tb, c: step(tb, 1, c, False), carry)
        (dka, dva, dfa), (dkb, dvb, dfb) = lax.fori_loop(whole, nbq, lambda i, c: step(i * sub, sub, c, False), carry)
        dk_ref[...] = (jnp.concatenate([dka, dkb], axis=0).T * LN2).astype(BF16)
        dv_ref[...] = jnp.concatenate([dva, dvb], axis=0).T.astype(BF16)
        if has_bias:
            row = lax.broadcasted_iota(jnp.int32, (LANES, 1), 0)
            by_head = jnp.where(row == 2 * g, dfa, jnp.where(row == 2 * g + 1, dfb, 0.0))
            df_ref[pl.ds(pl.multiple_of(j * blk, blk), blk), :] += by_head.T
        if side:
            @pl.when(step_id == PAIRS * nb - 1)
            def _():
                side.last(si, so, *sems)

    in_specs = [pl.BlockSpec((t, w), lambda g, j: (0, qoff + g)), pl.BlockSpec((nb, w, blk), lambda g, j: (0, qoff + g, 0)),
                pl.BlockSpec((blk, w), lambda g, j: (j, koff + g)), pl.BlockSpec((blk, LANES), lambda g, j: (j, voff + g)),
                pl.BlockSpec((t, LANES), lambda g, j: (0, g)), pl.BlockSpec((nb, LANES, blk), lambda g, j: (0, g, 0)),
                pl.BlockSpec((1, t, LANES), lambda g, j: (g, 0, 0))]
    out_specs = [pl.BlockSpec((t, w), lambda g, j: (0, g)), pl.BlockSpec((blk, w), lambda g, j: (j, g)),
                 pl.BlockSpec((blk, LANES), lambda g, j: (j, g))]
    out_shape = [jax.ShapeDtypeStruct((t, PAIRS * w), F32), jax.ShapeDtypeStruct((t, PAIRS * w), BF16),
                 jax.ShapeDtypeStruct((t, PAIRS * LANES), BF16)]
    if has_bias:
        in_specs.append(pl.BlockSpec((1, N_HEADS, blk), lambda g, j: (j, 0, 0)))
        out_specs.append(pl.BlockSpec((t, LANES), lambda g, j: (0, 0)))
        out_shape.append(jax.ShapeDtypeStruct((t, LANES), F32))
    s_in_specs, s_out_specs = side.specs() if side else ([], [])
    return pl.pallas_call(
        body, name=name, grid=(PAIRS, nb), in_specs=in_specs + s_in_specs, out_specs=out_specs + s_out_specs,
        out_shape=out_shape + list(s_outs), scratch_shapes=side.sems() if side else [],
        compiler_params=_params(("arbitrary", "arbitrary")),
    )(*ins, *s_ins)


def _adamw_math(w, g, m, v):
    nm = ADAM_B1 * m + (1.0 - ADAM_B1) * g
    nv = ADAM_B2 * v + (1.0 - ADAM_B2) * (g * g)
    m_hat = nm / (1.0 - ADAM_B1 ** ADAM_STEP)
    v_hat = nv / (1.0 - ADAM_B2 ** ADAM_STEP)
    return -ADAM_LR * (m_hat / (jnp.sqrt(v_hat) + ADAM_EPS) + ADAM_WD * w), nm, nv


def adamw_vectors(ws, g_rows, ms, vs, *, name):
    k = len(ws)

    def body(g_ref, *refs):
        w_refs, m_refs, v_refs, outs = _split_refs(refs, [k, k, k, 4 * k])
        for i in range(k):
            g = g_ref[i:i + 1, :w_refs[i].shape[1]]
            outs[4 * i][...] = g
            outs[4 * i + 1][...], outs[4 * i + 2][...], outs[4 * i + 3][...] = _adamw_math(
                w_refs[i][...], g, m_refs[i][...], v_refs[i][...])

    flat = pl.pallas_call(
        body, name=name, out_shape=[jax.ShapeDtypeStruct(w.shape, F32) for w in ws for _ in range(4)],
        compiler_params=_params(),
    )(g_rows, *ws, *ms, *vs)
    return [flat[4 * i:4 * i + 4] for i in range(k)]


def adamw_whole(w, g, m, v, *, name):
    def body(w_ref, g_ref, m_ref, v_ref, d_ref, nm_ref, nv_ref):
        d_ref[...], nm_ref[...], nv_ref[...] = _adamw_math(w_ref[...], g_ref[...], m_ref[...], v_ref[...])

    return pl.pallas_call(body, name=name, out_shape=[jax.ShapeDtypeStruct(w.shape, F32)] * 3,
                          compiler_params=_params())(w, g, m, v)


def adamw_halves(w, g_mine, g_other, m, v, core, *, name):
    _, k, n = w.shape
    br = min(k // 2, max(8, ADAMW_BLOCK // n))
    nh = k // 2 // br

    def body(c_ref, w_ref, gm_ref, go_ref, m_ref, v_ref, g_out, d_ref, nm_ref, nv_ref):
        gv = jnp.where(pl.program_id(0) == c_ref[0], gm_ref[...], go_ref[...])
        g_out[0] = gv
        d_ref[0], nm_ref[0], nv_ref[0] = _adamw_math(w_ref[0], gv, m_ref[0], v_ref[0])

    full = pl.BlockSpec((1, br, n), lambda hb, i, c: (0, hb * nh + i, 0))
    half = pl.BlockSpec((br, n), lambda hb, i, c: (i, 0))
    return pl.pallas_call(
        body, name=name,
        grid_spec=pltpu.PrefetchScalarGridSpec(num_scalar_prefetch=1, grid=(2, nh), in_specs=[full, half, half, full, full],
                                               out_specs=[full] * 4),
        out_shape=[jax.ShapeDtypeStruct(w.shape, F32)] * 4,
        compiler_params=_params(("parallel", "parallel")),
    )(core, w, g_mine, g_other, m, v)


def adamw_halves_t(wt, mt, vt, gt_mine, gt_other, core, *, name, bc=512):
    n, _, k = wt.shape
    nh = k // 2 // bc

    def body(c_ref, w_ref, m_ref, v_ref, gm_ref, go_ref, g_out, d_ref, nm_ref, nv_ref):
        gv = jnp.where(pl.program_id(0) == c_ref[0], gm_ref[...], go_ref[...])
        g_out[:, 0, :] = gv
        d_ref[:, 0, :], nm_ref[:, 0, :], nv_ref[:, 0, :] = _adamw_math(w_ref[:, 0, :], gv, m_ref[:, 0, :], v_ref[:, 0, :])

    full = pl.BlockSpec((n, 1, bc), lambda hb, i, c: (0, 0, hb * nh + i))
    half = pl.BlockSpec((n, bc), lambda hb, i, c: (0, i))
    return pl.pallas_call(
        body, name=name,
        grid_spec=pltpu.PrefetchScalarGridSpec(num_scalar_prefetch=1, grid=(2, nh), in_specs=[full, full, full, half, half],
                                               out_specs=[full] * 4),
        out_shape=[jax.ShapeDtypeStruct(wt.shape, F32)] * 4,
        compiler_params=_params(("parallel", "parallel")),
    )(core, wt, mt, vt, gt_mine, gt_other)


def add_pair(dw, recv, core, *, name):
    n4, k, n = dw.shape
    half = (1, k // 2, n) if split_axis(k) == 0 else (1, k, n // 2)
    mine = (lambda q, c: (q, c[0], 0)) if split_axis(k) == 0 else (lambda q, c: (q, 0, c[0]))

    def body(c_ref, a_ref, b_ref, o_ref):
        o_ref[...] = (a_ref[...] + b_ref[...].astype(F32)).astype(BF16)

    return pl.pallas_call(
        body, name=name,
        grid_spec=pltpu.PrefetchScalarGridSpec(
            num_scalar_prefetch=1, grid=(n4,),
            in_specs=[pl.BlockSpec(half, mine), pl.BlockSpec(half, lambda q, c: (q, 0, 0))],
            out_specs=pl.BlockSpec(half, lambda q, c: (q, 0, 0))),
        out_shape=jax.ShapeDtypeStruct((n4,) + half[1:], BF16),
        compiler_params=_params(("parallel",)),
    )(core, dw, recv)


def sum_chips(parts, *, name):
    n4, r, n = parts.shape
    if r % 16 == 0:
        br, bc = _row_block(r), n
    else:
        br, bc = r, LANES

    def body(p_ref, o_ref):
        acc = p_ref[0].astype(F32)
        for q in range(1, n4):
            acc = acc + p_ref[q].astype(F32)
        o_ref[...] = acc

    return pl.pallas_call(
        body, name=name, grid=(r // br, n // bc),
        in_specs=[pl.BlockSpec((n4, br, bc), lambda i, j: (0, i, j))], out_specs=pl.BlockSpec((br, bc), lambda i, j: (i, j)),
        out_shape=jax.ShapeDtypeStruct((r, n), F32),
        compiler_params=_params(("parallel", "parallel")),
    )(parts)


ANY = pl.BlockSpec(memory_space=pl.ANY)


def _place():
    x, y, c = lax.axis_index("x"), lax.axis_index("y"), lax.axis_index("c")
    chips = [(1 - x, y), (x, 1 - y), (1 - x, 1 - y)]
    return x, y, c, chips


def _copy(src, dst, send_sems, recv_sems, k, to):
    return pltpu.make_async_remote_copy(src_ref=src, dst_ref=dst, send_sem=send_sems.at[k], recv_sem=recv_sems.at[k],
                                        device_id=to, device_id_type=MESH)


def split_axis(rows):
    return 0 if rows % 32 == 0 else 1


def _half(ref, lead, hf):
    rows, cols = ref.shape[-2:]
    if split_axis(rows) == 0:
        at = (pl.ds(hf * (rows // 2), rows // 2), slice(None))
    else:
        at = (slice(None), pl.ds(hf * (cols // 2), cols // 2))
    return ref.at[at] if lead is None else ref.at[(lead,) + at]


def _gather_first(srcs, dsts, ssems, rsems):
    x, y, c, chips = _place()
    for ti, (s, d) in enumerate(zip(srcs, dsts)):
        for j, (cx, cy) in enumerate(chips):
            _copy(_half(s, None, c), _half(d, 2 * x + y, c), ssems, rsems, 3 * ti + j, (cx, cy, c)).start()


def _gather_mid(srcs, dsts, ssems, rsems):
    x, y, c, chips = _place()
    n1 = 3 * len(srcs)
    for ti, d in enumerate(dsts):
        for j, (cx, cy) in enumerate(chips):
            landed = _half(d, 2 * cx + cy, c)
            _copy(landed, landed, ssems, rsems, 3 * ti + j, (cx, cy, c)).wait_recv()
            _copy(landed, landed, ssems, rsems, n1 + 3 * ti + j, (x, y, 1 - c)).start()


def _gather_last(srcs, dsts, ssems, rsems):
    x, y, c, chips = _place()
    n1 = 3 * len(srcs)
    for ti, (s, d) in enumerate(zip(srcs, dsts)):
        for j, (cx, cy) in enumerate(chips):
            other = _half(d, 2 * cx + cy, 1 - c)
            _copy(other, other, ssems, rsems, n1 + 3 * ti + j, (x, y, 1 - c)).wait_recv()
        for j, (cx, cy) in enumerate(chips):
            mine = _half(s, None, c)
            _copy(mine, mine, ssems, rsems, 3 * ti + j, (cx, cy, c)).wait_send()
            _copy(mine, mine, ssems, rsems, n1 + 3 * ti + j, (x, y, 1 - c)).wait_send()


def gather_side(shards):
    return Side(shards, [jax.ShapeDtypeStruct((N_CHIPS,) + s.shape, s.dtype) for s in shards], 6 * len(shards),
                _gather_first, _gather_last, _gather_mid)


def _scatter_first(srcs, dsts, ssems, rsems):
    x, y, c, chips = _place()
    for ti, (s, d) in enumerate(zip(srcs, dsts)):
        for j, (cx, cy) in enumerate(chips):
            _copy(s.at[2 * cx + cy], d.at[2 * x + y], ssems, rsems, 3 * ti + j, (cx, cy, c)).start()


def _scatter_last(srcs, dsts, ssems, rsems):
    x, y, c, chips = _place()
    for ti, (s, d) in enumerate(zip(srcs, dsts)):
        for j, (cx, cy) in enumerate(chips):
            _copy(s.at[2 * cx + cy], d.at[2 * cx + cy], ssems, rsems, 3 * ti + j, (cx, cy, c)).wait_recv()
        for j, (cx, cy) in enumerate(chips):
            _copy(s.at[2 * cx + cy], d.at[2 * cx + cy], ssems, rsems, 3 * ti + j, (cx, cy, c)).wait_send()


def scatter_side(parts):
    return Side(parts, [jax.ShapeDtypeStruct(p.shape, p.dtype) for p in parts], 3 * len(parts), _scatter_first, _scatter_last)


def run_side(side, *, name):
    n_in, n_out = len(side.ins), len(side.out_shapes)

    def body(*refs):
        si, so, sems = _split_refs(refs, [n_in, n_out, 2])
        side.first(si, so, *sems)
        if side.mid is not None:
            side.mid(si, so, *sems)
        side.last(si, so, *sems)

    in_specs, out_specs = side.specs()
    return pl.pallas_call(body, name=name, in_specs=in_specs, out_specs=out_specs, out_shape=side.out_shapes,
                          scratch_shapes=side.sems())(*side.ins)


def _swap_first(srcs, dsts, ssems, rsems):
    x, y, c, _ = _place()
    for k, (s, d) in enumerate(zip(srcs, dsts)):
        _copy(s, d, ssems, rsems, k, (x, y, 1 - c)).start()


def _swap_last(srcs, dsts, ssems, rsems):
    x, y, c, _ = _place()
    for k, (s, d) in enumerate(zip(srcs, dsts)):
        _copy(s, d, ssems, rsems, k, (x, y, 1 - c)).wait()


def swap_side(xs):
    return Side(xs, [jax.ShapeDtypeStruct(a.shape, a.dtype) for a in xs], len(xs), _swap_first, _swap_last)


def _swap_halves(srcs, dsts, ssems, rsems):
    x, y, c, _ = _place()
    for k, (s, d) in enumerate(zip(srcs, dsts)):
        hk = s.shape[1] // 2
        yield _copy(s.at[:, pl.ds((1 - c) * hk, hk), :], d, ssems, rsems, k, (x, y, 1 - c))


def _swap_halves_first(srcs, dsts, ssems, rsems):
    for cp in _swap_halves(srcs, dsts, ssems, rsems):
        cp.start()


def _swap_halves_last(srcs, dsts, ssems, rsems):
    for cp in _swap_halves(srcs, dsts, ssems, rsems):
        cp.wait()


def swap_halves_side(xs):
    return Side(xs, [jax.ShapeDtypeStruct((a.shape[0], a.shape[1] // 2, a.shape[2]), a.dtype) for a in xs], len(xs),
                _swap_halves_first, _swap_halves_last)


def allreduce_small(s):
    n_dev = 8

    def body(s_ref, out_ref, buf, send_sems, recv_sems):
        x, y, c, _ = _place()
        me = 4 * x + 2 * y + c
        buf[me] = s_ref[...]
        sends = []
        for k in range(1, n_dev):
            px = 1 - x if k & 4 else x
            py = 1 - y if k & 2 else y
            pc = 1 - c if k & 1 else c
            cp = _copy(s_ref, buf.at[me], send_sems, recv_sems, k - 1, (px, py, pc))
            cp.start()
            sends.append((cp, 4 * px + 2 * py + pc))
        for k, (cp, peer) in enumerate(sends):
            _copy(s_ref, buf.at[peer], send_sems, recv_sems, k, (x, y, c)).wait_recv()
        for cp, _ in sends:
            cp.wait_send()
        acc = buf[0]
        for d in range(1, n_dev):
            acc = acc + buf[d]
        out_ref[...] = acc

    vm = pl.BlockSpec(memory_space=pltpu.VMEM)
    return pl.pallas_call(
        body, name="allreduce_small", in_specs=[vm], out_specs=vm,
        out_shape=jax.ShapeDtypeStruct(s.shape, F32),
        scratch_shapes=[pltpu.VMEM((n_dev,) + s.shape, F32), pltpu.SemaphoreType.DMA((n_dev - 1,)),
                        pltpu.SemaphoreType.DMA((n_dev - 1,))],
    )(s)


def join_cols(sm):
    n4, k, n = sm.shape
    return sm.transpose(1, 0, 2).reshape(k, n4 * n)


def split_cols(full):
    k, n = full.shape
    return full.reshape(k, N_CHIPS, n // N_CHIPS).transpose(1, 0, 2)


def _pad_heads(w, width):
    lead, heads = w.shape[:-1], w.shape[-1] // width
    w = w.reshape(lead + (heads, width))
    return jnp.pad(w, [(0, 0)] * len(lead) + [(0, 0), (0, LANES - width)]).reshape(lead + (heads * LANES,))


def _unpad_heads(w, width):
    lead, heads = w.shape[:-1], w.shape[-1] // LANES
    return w.reshape(lead + (heads, LANES))[..., :width].reshape(lead + (heads * width,))


O_F = 3 * FW
O_CQ = O_F + N_HEADS
O_CKV = O_CQ + Q_RANK
O_KR = O_CKV + KV_RANK
O_END = O_KR + ROPE_DIM


def split_w_in_t(w_sm):
    n4, r, d = w_sm.shape
    segments = [(0, O_F, 0, 0), (O_F, N_HEADS, 1, S_F), (O_CQ, Q_RANK, 1, S_CQ), (O_CKV, KV_RANK, 1, S_CKV),
                (O_KR, ROPE_DIM, 1, S_KR + HEAD_DIM)]
    moves = []
    for v0, n, dst, d0 in segments:
        for q in range(n4):
            a, b = max(v0, q * r), min(v0 + n, (q + 1) * r)
            if a < b:
                assert a % 2 == 0 and b % 2 == 0 and (d0 + a - v0) % 2 == 0
                moves.append((q, (a - q * r) // 2, (b - a) // 2, dst, (d0 + a - v0) // 2))

    def body(w_ref, qkv_ref, small_ref, src, dst_qkv, dst_small):
        for q in range(n4):
            src[q] = pltpu.bitcast(w_ref[q], jnp.uint32)
        dst_small[...] = jnp.zeros(dst_small.shape, jnp.uint32)
        for q, a, n, dst, b in moves:
            (dst_qkv, dst_small)[dst][pl.ds(b, n), :] = src[q, pl.ds(a, n), :]
        qkv_ref[...] = pltpu.bitcast(dst_qkv[...], w_sm.dtype)
        small_ref[...] = pltpu.bitcast(dst_small[...], w_sm.dtype)

    return pl.pallas_call(
        body, name="split_w_in",
        out_shape=[jax.ShapeDtypeStruct((O_F, d), w_sm.dtype), jax.ShapeDtypeStruct((S_END, d), w_sm.dtype)],
        scratch_shapes=[pltpu.VMEM((n4, r // 2, d), jnp.uint32), pltpu.VMEM((O_F // 2, d), jnp.uint32),
                        pltpu.VMEM((S_END // 2, d), jnp.uint32)],
        compiler_params=_params(),
    )(w_sm)


def join_w_in_t(d_qkv_t, d_small_t):
    small = len(d_qkv_t)
    segments = [(n, 0, n * FW, FW) for n in range(small)]
    segments += [(small, S_F, O_F, N_HEADS), (small, S_CQ, O_CQ, Q_RANK), (small, S_CKV, O_CKV, KV_RANK),
                 (small, S_KR + HEAD_DIM, O_KR, ROPE_DIM)]
    r, d = O_END // N_CHIPS, d_small_t.shape[1]

    def body(*refs):
        o_ref = refs[-1]
        for src, s0, v0, n in segments:
            for q in range(N_CHIPS):
                a, b = max(v0, q * r), min(v0 + n, (q + 1) * r)
                if a < b:
                    o_ref[q, pl.ds(a - q * r, b - a), :] = refs[src][pl.ds(s0 + a - v0, b - a), :]

    return pl.pallas_call(
        body, name="join_w_in", out_shape=jax.ShapeDtypeStruct((N_CHIPS, r, d), F32), compiler_params=_params(),
    )(*d_qkv_t, d_small_t)


def rope_inputs(pos):
    inv_freq = ROPE_THETA ** (-jnp.arange(0, ROPE_DIM, 2, dtype=F32) / ROPE_DIM)
    inv_row = jnp.concatenate([jnp.zeros((HEAD_DIM,), F32), inv_freq, inv_freq, jnp.zeros((LANES - HEAD_DIM - ROPE_DIM,), F32)])
    return pos.astype(F32)[:, None], inv_row[None, :]


def _pad_lanes(v, n):
    return jnp.pad(v, ((0, 0), (0, n - v.shape[1])))


ATTN_BLK = 512
ATTN_FWD_BLK = 1024
ATTN_BWD_QBLK = 1024
ACC_ROWS = HEAD_DIM + 16


def local_step(xs, pos, tgt, gains, early_weights, late_weights, early_side=None, fwd_sides=(None, None), reduction=None):
    g_attn, b_forget, g_q, g_kv, g_fo, g_mo, g_mlp, g_fin = gains
    t = xs.shape[0]
    blk = min(ATTN_BLK, t)
    fox_scale = 1.0 / (HEAD_DIM ** 0.5)
    mla_scale = 1.0 / ((HEAD_DIM + ROPE_DIM) ** 0.5)

    h1, *gathered = rmsnorm(xs, g_attn, out_dtype=BF16, name="norm_attn", side=early_side)
    w_in_t, w_uq_p, w_ukv = early_weights(gathered)
    w_qkv_t, w_small_t = split_w_in_t(w_in_t)
    kv = w_ukv.reshape(KV_RANK, N_HEADS, 2 * HEAD_DIM)
    w_ukv_p = jnp.concatenate([_pad_heads(kv[:, :, :HEAD_DIM].reshape(KV_RANK, FW), HEAD_DIM),
                               kv[:, :, HEAD_DIM:].reshape(KV_RANK, FW)], axis=1)
    b_f = _pad_lanes(b_forget, LANES)
    pos_col, inv_row = rope_inputs(pos)

    qkv, qkv_t = mm(h1, w_qkv_t, trans_b=True, out_dtypes=[BF16], t_blk=blk, name="proj_qkv", bn=1536)
    small, = mm(h1, w_small_t, trans_b=True, out_dtypes=[F32], name="proj_small")
    mq, mk, mv, lf, cqn, ckvn, mq_t, mv_t = mla_prep(small, g_q, g_kv, w_uq_p, w_ukv_p, pos_col, inv_row, b_f,
                                                     name="mla_prep", bt=blk)
    f_cum = cumsum_rows(lf, reverse=False, name="gate_cumsum")
    f_blocks = f_cum[:, :N_HEADS].reshape(t // blk, blk, N_HEADS).transpose(0, 2, 1)
    fo, st_f, *gathered = flash_fwd(qkv_t, qkv, qkv_t, f_cum, qoff=0, koff=PAIRS, voff=2 * PAIRS, pair=True,
                                    scale=fox_scale, name="fox_fwd", blk=ATTN_FWD_BLK, side=fwd_sides[0])
    mo, st_m, *more = flash_fwd(mq_t, mk, mv_t, None, qoff=0, koff=0, voff=0, pair=False, scale=mla_scale, name="mla_fwd",
                                blk=ATTN_FWD_BLK, side=fwd_sides[1])
    w_o, w_up, w_down = late_weights(gathered + more)
    mixed = mix_norm(fo, mo, g_fo, g_mo, name="norm_mix")

    def inv_rms(v):
        return lax.rsqrt(jnp.mean(v * v, axis=-1, keepdims=True) + EPS)

    def residual_then_norm(acc, res, g):
        xn = acc + res
        return xn, xn * inv_rms(xn) * g

    def norm_bwd(dh, xn, res, g):
        r = inv_rms(xn)
        uu = dh * g
        return (r * uu - xn * (r * r * r * jnp.mean(uu * xn, axis=-1, keepdims=True)) + res,
                jnp.sum(dh * (xn * r), axis=0, keepdims=True))

    def norm_bwd2(dh, xn, res, g):
        dx, dg = norm_bwd(dh, xn, res, g)
        return dx, dx, dg

    def residual_then_loss(acc, res, target, g):
        xn = acc + res
        r = inv_rms(xn)
        xh = xn * r
        e = xh * g - target
        part = 0.5 * jnp.sum(jnp.mean(e * e, axis=-1, keepdims=True), axis=0, keepdims=True)
        dy = e * (1.0 / xn.shape[1])
        uu = dy * g
        dx = r * uu - xn * (r * r * r * jnp.mean(uu * xn, axis=-1, keepdims=True))
        return dx, dx, jnp.sum(dy * xh, axis=0, keepdims=True), part + jnp.zeros_like(g)

    x1, h2 = mm(mixed, w_o, extras=[xs], vecs=[g_mlp], epilogue=residual_then_norm, out_dtypes=[F32, BF16], name="out_proj")

    def relu2(uu):
        r = jnp.maximum(uu.astype(F32), 0.0)
        return (r * r).astype(BF16)

    u, = mm(h2, w_up, out_dtypes=[BF16], name="mlp_up", bn=2048)
    dx2, dx2b, dg_fin, loss_row = mm(u, w_down, a_pro=relu2, extras=[x1, tgt], vecs=[g_fin], epilogue=residual_then_loss,
                                     out_dtypes=[F32, BF16], n_sums=2, name="mlp_down_loss")
    loss = loss_row[:, :1]

    def relu2_grad(acc, uu):
        return (acc * (2.0 * jnp.maximum(uu.astype(F32), 0.0)),)

    du, = mm(dx2b, w_down, trans_b=True, extras=[u], epilogue=relu2_grad, out_dtypes=[BF16], name="mlp_down_bwd", bn=2048)
    dw_down, dw_down_b = (g.reshape(N_CHIPS, -1, w_down.shape[1])
                          for g in mm_tn(u, dx2b, a_pro=relu2, name="dw_down", bf16_copy=True))
    dx1, dx1b, dg_mlp = mm(du, w_up, trans_b=True, extras=[x1, dx2], vecs=[g_mlp], epilogue=norm_bwd2,
                           out_dtypes=[F32, BF16], n_sums=1, name="mlp_up_bwd")
    dw_up, dw_up_b = mm_tn(h2, du, name="dw_up", col_shards=N_CHIPS, bf16_copy=True)

    dw_o, dw_o_b = (g.reshape(N_CHIPS, -1, w_o.shape[1]) for g in mm_tn(mixed, dx1b, name="dw_o", bf16_copy=True))
    late, late_b = (dw_o, dw_up, dw_down), (dw_o_b, dw_up_b, dw_down_b)
    red = reduction
    dfo, dmo, dg_fo, dg_mo, st_f, st_m, dfo_t, dmo_t, *got = mix_norm_bwd(
        dx1b, w_o, fo, mo, g_fo, g_mo, st_f, st_m, name="out_proj_mix_bwd", bt=blk,
        side=red.late_swap(late, late_b) if red else None)
    dfq, dfk, dfv, d_f, *got = flash_bwd(
        qkv, qkv_t, qkv, qkv, dfo, dfo_t, st_f, f_blocks, qoff=0, koff=PAIRS, voff=2 * PAIRS, pair=True,
        scale=fox_scale, name="fox_bwd", qblk=ATTN_BWD_QBLK, side=red.late_scatter(got) if red else None)
    dmq, dmk, dmv, *got = flash_bwd(mq, mq_t, mk, mv, dmo, dmo_t, st_m, None, qoff=0, koff=0, voff=0,
                                    pair=False, scale=mla_scale, name="mla_bwd", qblk=ATTN_BWD_QBLK,
                                    side=red.late_halves(got) if red else None)
    if red:
        red.late_done(got)
    dlf = cumsum_rows(d_f, reverse=True, name="gate_cumsum_bwd")
    dsmall, dq_u, dkv_u, dg_q, dg_kv, db_f = mla_prep_bwd(dmq, dmk, dmv, dlf, small, g_q, g_kv, w_uq_p, w_ukv_p,
                                                          pos_col, inv_row, b_f, name="mla_prep_bwd")
    dw_uq_p = mm_tn(cqn, dq_u, name="dw_uq")
    dw_ukv_p = mm_tn(ckvn, dkv_u, name="dw_ukv")

    def to_bf16(tile):
        return tile.astype(BF16)

    dqkv = [dfq, dfk, dfv]
    dw_in_t = join_w_in_t([mm_tn(part, h1, a_pro=to_bf16, name="dw_" + nm) for part, nm in zip(dqkv, "qkv")],
                          mm_tn(dsmall, h1, name="dw_small"))
    dk_cols = _unpad_heads(dw_ukv_p[:, :HW], HEAD_DIM).reshape(KV_RANK, N_HEADS, HEAD_DIM)
    dv_cols = dw_ukv_p[:, HW:].reshape(KV_RANK, N_HEADS, HEAD_DIM)
    dw_ukv = jnp.concatenate([dk_cols, dv_cols], axis=2).reshape(KV_RANK, N_HEADS * 2 * HEAD_DIM)
    early = (dw_in_t, split_cols(dw_uq_p), split_cols(dw_ukv))
    w_parts = [w_qkv_t[n * FW:(n + 1) * FW] for n in range(3)]
    grad_x, dg_attn, *got = mm(dqkv + [dsmall], w_parts + [w_small_t], a_pro=to_bf16, extras=[xs, dx1], vecs=[g_attn],
                               epilogue=norm_bwd, out_dtypes=[F32], n_sums=1, name="proj_bwd",
                               side=red.early_scatter(early) if red else None)
    if red:
        red.early_done(got)
    d_gains = (dg_attn, db_f[:, :N_HEADS], dg_q, dg_kv, dg_fo, dg_mo, dg_mlp, dg_fin)
    return loss, grad_x, early, late, d_gains


class GradReduction:
    def __init__(self, core_id, chip):
        self.core_id, self.chip = core_id, chip
        self.core = core_id.reshape(1).astype(jnp.int32)
        self.grads, self.pairs, self.halves, self.others = {}, {}, {}, {}

    def _other_halves(self, grads):
        out = []
        for g in grads:
            axis = 1 + split_axis(g.shape[1])
            size = g.shape[axis] // 2
            out.append(lax.dynamic_slice_in_dim(g, (1 - self.core_id) * size, size, axis=axis).astype(BF16))
        return out

    def _add_pairs(self, group, recvs):
        self.pairs[group] = [add_pair(g, r, self.core, name="add_pair_%s_%d" % (group, n))
                             for n, (g, r) in enumerate(zip(self.grads[group], recvs))]
        return scatter_side(self.pairs[group])

    def _chip_sums(self, group, scattered):
        chip = self.chip
        with_mine = [lax.dynamic_update_index_in_dim(s, lax.dynamic_index_in_dim(p, chip, 0, keepdims=True), chip, 0)
                     for s, p in zip(scattered, self.pairs[group])]
        self.halves[group] = [sum_chips(s, name="sum_chips_%s_%d" % (group, n)) for n, s in enumerate(with_mine)]
        return self.halves[group]

    def late_swap(self, grads, bf16_copies):
        self.grads["late"] = list(grads)
        return swap_halves_side(list(bf16_copies))

    def late_scatter(self, recvs):
        return self._add_pairs("late", recvs)

    def late_halves(self, scattered):
        return swap_side(self._chip_sums("late", scattered))

    def late_done(self, others):
        self.others["late"] = list(others)

    def early_scatter(self, grads):
        self.grads["early"] = list(grads)
        return self._add_pairs("early", run_side(swap_side(self._other_halves(grads)), name="swap_early_sends"))

    def early_done(self, scattered):
        self.others["early"] = list(run_side(swap_side(self._chip_sums("early", scattered)), name="swap_early_halves"))

def kernel(x, positions, attn_norm_g, w_in, b_forget, q_norm_g, w_uq, kv_norm_g, w_ukv, fox_out_g, mla_out_g, w_o, mlp_norm_g, w_up, w_down, final_norm_g, loss_target, m_attn_norm_g, m_w_in, m_b_forget, m_q_norm_g, m_w_uq, m_kv_norm_g, m_w_ukv, m_fox_out_g, m_mla_out_g, m_w_o, m_mlp_norm_g, m_w_up, m_w_down, m_final_norm_g, v_attn_norm_g, v_w_in, v_b_forget, v_q_norm_g, v_w_uq, v_kv_norm_g, v_w_ukv, v_fox_out_g, v_mla_out_g, v_w_o, v_mlp_norm_g, v_w_up, v_w_down, v_final_norm_g):
    core_id = lax.axis_index("c")
    core = core_id.reshape(1).astype(jnp.int32)
    chip = 2 * lax.axis_index("x") + lax.axis_index("y")
    big = [w_in, w_uq, w_ukv, w_o, w_up, w_down]
    big_m = [m_w_in, m_w_uq, m_w_ukv, m_w_o, m_w_up, m_w_down]
    big_v = [v_w_in, v_w_uq, v_w_ukv, v_w_o, v_w_up, v_w_down]
    n_early = 3

    def vec(a):
        return a.reshape(1, -1)

    small = [attn_norm_g, b_forget, q_norm_g, kv_norm_g, fox_out_g, mla_out_g, mlp_norm_g, final_norm_g]
    small_m = [m_attn_norm_g, m_b_forget, m_q_norm_g, m_kv_norm_g, m_fox_out_g, m_mla_out_g, m_mlp_norm_g, m_final_norm_g]
    small_v = [v_attn_norm_g, v_b_forget, v_q_norm_g, v_kv_norm_g, v_fox_out_g, v_mla_out_g, v_mlp_norm_g, v_final_norm_g]
    gains = [vec(a) for a in small]

    views = [big[0][0].T, _pad_heads(big[1][0], HEAD_DIM + ROPE_DIM)] + [w[0] for w in big[2:]]
    shards = [v.astype(BF16) for v in views]

    def with_own(gathered, mine):
        return [lax.dynamic_update_index_in_dim(g, s, chip, 0) for g, s in zip(gathered, mine)]

    def early_weights(gathered):
        g_in, g_uq, g_ukv = with_own(gathered, shards[:n_early])
        return g_in, join_cols(g_uq), join_cols(g_ukv)

    def late_weights(gathered):
        g_o, g_up, g_down = with_own(gathered, shards[n_early:])
        return g_o.reshape(-1, g_o.shape[2]), join_cols(g_up), g_down.reshape(-1, g_down.shape[2])

    reduction = GradReduction(core_id, chip)
    loss, grad_x, _, _, d_small = local_step(
        x[0], positions[0], loss_target[0], gains, early_weights, late_weights, gather_side(shards[:n_early]),
        (gather_side(shards[n_early:-1]), gather_side(shards[-1:])), reduction)
    halves = reduction.halves["early"] + reduction.halves["late"]
    others = reduction.others["early"] + reduction.others["late"]

    def rows8(vs):
        return jnp.concatenate([_pad_lanes(vec(a).astype(F32), 1024) for a in vs], axis=0)

    with_loss = [jnp.concatenate([d, loss], axis=1) if n == 1 else d for n, d in enumerate(d_small)]
    g_small8 = allreduce_small(rows8(with_loss))

    outs_big = []
    for n, (w, gm, go, m, v) in enumerate(zip(big, halves, others, big_m, big_v)):
        if n == 0:
            outs = adamw_halves_t(*(jnp.transpose(a, (2, 0, 1)) for a in (w, m, v)), gm, go, core, name="adamw_%d" % n)
            outs_big.append([jnp.transpose(o, (1, 2, 0)) for o in outs])
        elif n == 1:
            gm, go = (_unpad_heads(gh, HEAD_DIM + ROPE_DIM) for gh in (gm, go))
            g_t = jnp.where(core[0] == 0, jnp.concatenate([gm, go], axis=0), jnp.concatenate([go, gm], axis=0)).T
            outs = (g_t,) + tuple(adamw_whole(w[0].T, g_t, m[0].T, v[0].T, name="adamw_%d" % n))
            outs_big.append([o.T[None] for o in outs])
        else:
            outs_big.append(adamw_halves(w, gm, go, m, v, core, name="adamw_%d" % n))
    outs_small = adamw_vectors(gains, g_small8, [vec(a) for a in small_m], [vec(a) for a in small_v], name="adamw_small")

    loss_all = g_small8[1, N_HEADS]
    grads, deltas, new_m, new_v = [None] * 14, [None] * 14, [None] * 14, [None] * 14
    big_at = [1, 4, 6, 9, 11, 12]
    small_at = [0, 2, 3, 5, 7, 8, 10, 13]
    for n, at in enumerate(big_at):
        grads[at], deltas[at], new_m[at], new_v[at] = outs_big[n]
    for at, s, outs in zip(small_at, small, outs_small):
        grads[at], deltas[at], new_m[at], new_v[at] = (o.reshape(s.shape) for o in outs)
    return (loss_all, grad_x[None], *grads, *deltas, *new_m, *new_v)
```

```python
import jax
import jax.numpy as jnp
from jax import lax
from jax.experimental import pallas as pl
from jax.experimental.pallas import tpu as pltpu

F32 = jnp.float32
BF16 = jnp.bfloat16
MESH = pl.DeviceIdType.MESH

EPS = 1e-6
ROPE_THETA = 10000.0
N_HEADS = 8
PAIRS = N_HEADS // 2
HEAD_DIM = 64
ROPE_DIM = 32
LANES = 128
Q_RANK = 384
KV_RANK = 256
N_CHIPS = 4
ADAM_LR, ADAM_B1, ADAM_B2, ADAM_EPS, ADAM_WD, ADAM_STEP = 0.001, 0.9, 0.999, 1e-08, 0.01, 10
VMEM_LIMIT = 48 * 1024 * 1024
ADAMW_BLOCK = 256 * 1024
LOG2E = 1.4426950408889634
LN2 = 0.6931471805599453
NN = (((1,), (0,)), ((), ()))
NT = (((1,), (1,)), ((), ()))
TN = (((0,), (0,)), ((), ()))


def _params(sem=None):
    return pltpu.CompilerParams(dimension_semantics=sem, vmem_limit_bytes=VMEM_LIMIT)


def _fit(block, dim):
    if dim <= block:
        return dim
    return next(b for b in range(block - block % LANES, 0, -LANES) if dim % b == 0)


def _row_block(rows):
    return next(b for b in (256, 128, 64, 32, 16, 8) if rows % b == 0)


def gridded(body, *, name, grid, in_specs, out_specs, out_shape, ins, semantics, side=None):
    if side is None:
        return pl.pallas_call(body, name=name, grid=grid, in_specs=in_specs, out_specs=out_specs, out_shape=out_shape,
                              compiler_params=_params(semantics))(*ins)
    n_in, n_out = len(in_specs), len(out_specs)
    steps = 1
    for extent in grid:
        steps *= extent

    def riding(*refs):
        main_in, s_in, main_out, s_out, sems = _split_refs(refs, [n_in, len(side.ins), n_out, len(side.out_shapes), 2])
        step = 0
        for axis, extent in enumerate(grid):
            step = step * extent + pl.program_id(axis)

        @pl.when(step == 0)
        def _():
            side.first(s_in, s_out, *sems)

        body(*main_in, *main_out)

        @pl.when(step == steps - 1)
        def _():
            if side.mid is not None:
                side.mid(s_in, s_out, *sems)
            side.last(s_in, s_out, *sems)

    s_in_specs, s_out_specs = side.specs()
    return pl.pallas_call(
        riding, name=name, grid=grid, in_specs=list(in_specs) + s_in_specs, out_specs=list(out_specs) + s_out_specs,
        out_shape=list(out_shape) + side.out_shapes, scratch_shapes=side.sems(),
        compiler_params=_params(("arbitrary",) * len(grid)))(*ins, *side.ins)


def rmsnorm(x, g, *, out_dtype, name, bt=512, side=None):
    t, d = x.shape
    bt = min(bt, t)

    def body(x_ref, g_ref, o_ref):
        xv = x_ref[...].astype(F32)
        r = lax.rsqrt(jnp.mean(xv * xv, axis=-1, keepdims=True) + EPS)
        o_ref[...] = (xv * r * g_ref[...]).astype(o_ref.dtype)

    return gridded(
        body, name=name, grid=(t // bt,),
        in_specs=[pl.BlockSpec((bt, d), lambda i: (i, 0)), pl.BlockSpec((1, d), lambda i: (0, 0))],
        out_specs=[pl.BlockSpec((bt, d), lambda i: (i, 0))],
        out_shape=[jax.ShapeDtypeStruct((t, d), out_dtype)],
        ins=[x, g], semantics=("parallel",), side=side)


def mm(a, b, *, trans_b=False, a_pro=None, extras=(), vecs=(), epilogue=None, out_dtypes, n_sums=0, t_blk=None, name,
       bm=1024, bn=1024, side=None):
    a_list = list(a) if isinstance(a, (list, tuple)) else [a]
    b_list = list(b) if isinstance(b, (list, tuple)) else [b]
    m = a_list[0].shape[0]
    n = b_list[0].shape[0] if trans_b else b_list[0].shape[1]
    ks = [x.shape[1] for x in a_list]
    if sum(ks) > 2048:
        bm = bm // 2
    bm, bn = _fit(bm, m), _fit(bn, n)
    assert n_sums == 0 or bn == n
    n_ab, n_ex, n_vec, n_out = len(a_list), len(extras), len(vecs), len(out_dtypes)
    n_t = 0 if t_blk is None else 1

    def body(*refs):
        a_refs, b_refs, ex, vs, outs, t_outs, sums = _split_refs(refs, [n_ab, n_ab, n_ex, n_vec, n_out, n_t, n_sums])
        acc = None
        for a_ref, b_ref in zip(a_refs, b_refs):
            a_tile = a_ref[...] if a_pro is None else a_pro(a_ref[...])
            part = lax.dot_general(a_tile, b_ref[...], NT if trans_b else NN, preferred_element_type=F32)
            acc = part if acc is None else acc + part
        res = epilogue(acc, *[e[...] for e in ex], *[v[...] for v in vs]) if epilogue is not None else (acc,)
        for o, r in zip(outs, res[:n_out]):
            o[...] = r.astype(o.dtype)
        for t_ref in t_outs:
            for u in range(bm // t_blk):
                t_ref[u] = res[0][u * t_blk:(u + 1) * t_blk, :].T.astype(t_ref.dtype)
        if n_sums:
            @pl.when(pl.program_id(0) == 0)
            def _():
                for s_ref in sums:
                    s_ref[...] = jnp.zeros_like(s_ref)

            for s_ref, r in zip(sums, res[n_out:]):
                s_ref[...] += r

    tile = pl.BlockSpec((bm, bn), lambda i, j: (i, j))
    vec = pl.BlockSpec((1, bn), lambda i, j: (0, j))
    t_specs, t_shapes = [], []
    if t_blk is not None:
        t_specs = [pl.BlockSpec((bm // t_blk, bn, t_blk), lambda i, j: (i, j, 0))]
        t_shapes = [jax.ShapeDtypeStruct((m // t_blk, n, t_blk), out_dtypes[0])]
    a_specs = [pl.BlockSpec((bm, k), lambda i, j: (i, 0)) for k in ks]
    b_specs = [pl.BlockSpec((bn, k), lambda i, j: (j, 0)) if trans_b else pl.BlockSpec((k, bn), lambda i, j: (0, j)) for k in ks]
    return gridded(
        body, name=name, grid=(m // bm, n // bn),
        in_specs=a_specs + b_specs + [tile] * n_ex + [vec] * n_vec,
        out_specs=[tile] * n_out + t_specs + [vec] * n_sums,
        out_shape=[jax.ShapeDtypeStruct((m, n), dt) for dt in out_dtypes] + t_shapes + [jax.ShapeDtypeStruct((1, n), F32)] * n_sums,
        ins=[*a_list, *b_list, *extras, *vecs],
        semantics=("arbitrary", "arbitrary") if n_sums else ("parallel", "parallel"), side=side)


def mm_tn(a, b, *, a_pro=None, name, col_shards=1, bf16_copy=False, bk=1024, bn=1024, bt=2048):
    t, k = a.shape
    n = b.shape[1]
    ns = n // col_shards
    bk, bn, bt = _fit(bk, k), _fit(bn, ns), _fit(bt, t)
    per = ns // bn
    last = t // bt - 1

    def body(a_ref, b_ref, o_ref, *copy_ref):
        @pl.when(pl.program_id(2) == 0)
        def _():
            o_ref[...] = jnp.zeros_like(o_ref)

        a_tile = a_ref[...] if a_pro is None else a_pro(a_ref[...])
        o_ref[...] += lax.dot_general(a_tile, b_ref[...], TN, preferred_element_type=F32)
        if bf16_copy:
            @pl.when(pl.program_id(2) == last)
            def _():
                copy_ref[0][...] = o_ref[...].astype(BF16)

    if col_shards == 1:
        out_spec = pl.BlockSpec((bk, bn), lambda i, j, s: (i, j))
        shape = (k, n)
    else:
        out_spec = pl.BlockSpec((None, bk, bn), lambda i, j, s: (j // per, i, j % per))
        shape = (col_shards, k, ns)
    out_specs, out_shape = out_spec, jax.ShapeDtypeStruct(shape, F32)
    if bf16_copy:
        out_specs, out_shape = [out_spec, out_spec], [out_shape, jax.ShapeDtypeStruct(shape, BF16)]
    return pl.pallas_call(
        body, name=name, grid=(k // bk, n // bn, t // bt),
        in_specs=[pl.BlockSpec((bt, bk), lambda i, j, s: (s, i)), pl.BlockSpec((bt, bn), lambda i, j, s: (s, j))],
        out_specs=out_specs, out_shape=out_shape,
        compiler_params=_params(("parallel", "parallel", "arbitrary")),
    )(a, b)


def _split3(x):
    hi = x.astype(BF16)
    r1 = x - hi.astype(F32)
    mid = r1.astype(BF16)
    lo = (r1 - mid.astype(F32)).astype(BF16)
    return hi, mid, lo


def cumsum_rows(x, *, reverse, name, bc=512):
    t, d = x.shape
    bc = min(bc, t)
    nb = t // bc

    def body(x_ref, o_ref, carry):
        @pl.when(pl.program_id(0) == 0)
        def _():
            carry[...] = jnp.zeros_like(carry)

        r = lax.broadcasted_iota(jnp.int32, (bc, bc), 0)
        c = lax.broadcasted_iota(jnp.int32, (bc, bc), 1)
        tri = jnp.where((r <= c) if reverse else (r >= c), 1.0, 0.0).astype(BF16)
        hi, mid, lo = _split3(x_ref[...])
        s = (lax.dot_general(tri, hi, NN, preferred_element_type=F32)
             + lax.dot_general(tri, mid, NN, preferred_element_type=F32)
             + lax.dot_general(tri, lo, NN, preferred_element_type=F32)) + carry[0:1, :]
        o_ref[...] = s
        carry[0:1, :] = s[0:1, :] if reverse else s[bc - 1:bc, :]

    imap = (lambda i: (nb - 1 - i, 0)) if reverse else (lambda i: (i, 0))
    return pl.pallas_call(
        body, name=name, grid=(nb,),
        in_specs=[pl.BlockSpec((bc, d), imap)], out_specs=pl.BlockSpec((bc, d), imap),
        out_shape=jax.ShapeDtypeStruct((t, d), F32),
        scratch_shapes=[pltpu.VMEM((8, d), F32)],
        compiler_params=_params(("arbitrary",)),
    )(x)


def _rope(x, c, a, b):
    return x * c + pltpu.roll(x, LANES - ROPE_DIM // 2, 1) * a + pltpu.roll(x, ROPE_DIM // 2, 1) * b


def _rope_bwd(d, c, a, b):
    return d * c + pltpu.roll(d * a, ROPE_DIM // 2, 1) + pltpu.roll(d * b, LANES - ROPE_DIM // 2, 1)


S_CQ, S_CKV, S_KR, S_F, S_END = 0, Q_RANK, Q_RANK + KV_RANK, Q_RANK + KV_RANK + LANES, 1024
HW = N_HEADS * LANES
FW = N_HEADS * HEAD_DIM


def _rope_tables(pos_col, inv_row):
    ang = pos_col * inv_row
    cos, sin = jnp.cos(ang), jnp.sin(ang)
    lane = lax.broadcasted_iota(jnp.int32, (1, LANES), 1)
    first = (lane >= HEAD_DIM) & (lane < HEAD_DIM + ROPE_DIM // 2)
    second = (lane >= HEAD_DIM + ROPE_DIM // 2) & (lane < HEAD_DIM + ROPE_DIM)
    return (jnp.where(lane < HEAD_DIM, 1.0, jnp.where(first | second, cos, 0.0)), jnp.where(first, -sin, 0.0),
            jnp.where(second, sin, 0.0))


def mla_prep(small, g_q, g_kv, w_uq, w_ukv, pos_col, inv_row, b_f, *, name, bt=512):
    t = small.shape[0]
    bt = min(bt, t)

    def body(s_ref, gq_ref, gkv_ref, wq_ref, wkv_ref, pos_ref, inv_ref, bf_ref,
             mq_ref, mk_ref, mv_ref, lf_ref, cqn_ref, ckvn_ref, mqt_ref, mvt_ref):
        cq = s_ref[:, S_CQ:S_CKV]
        rq = lax.rsqrt(jnp.mean(cq * cq, axis=-1, keepdims=True) + EPS)
        cqn = (cq * rq * gq_ref[...]).astype(BF16)
        ckv = s_ref[:, S_CKV:S_KR]
        rkv = lax.rsqrt(jnp.mean(ckv * ckv, axis=-1, keepdims=True) + EPS)
        ckvn = (ckv * rkv * gkv_ref[...]).astype(BF16)
        cqn_ref[...] = cqn
        ckvn_ref[...] = ckvn
        tc, ta, tb = _rope_tables(pos_ref[...], inv_ref[...])
        q = jnp.dot(cqn, wq_ref[...], preferred_element_type=F32)
        kv = jnp.dot(ckvn, wkv_ref[...], preferred_element_type=F32)
        kr = _rope(s_ref[:, S_KR:S_F], tc, ta, tb)
        for h in range(N_HEADS):
            sl = slice(h * LANES, (h + 1) * LANES)
            roped = _rope(q[:, sl], tc, ta, tb)
            mq_ref[:, sl] = roped.astype(BF16)
            mqt_ref[0, sl, :] = roped.T.astype(BF16)
            mk_ref[:, sl] = (kv[:, sl] + kr).astype(BF16)
        mv_ref[...] = kv[:, HW:].astype(BF16)
        mvt_ref[0] = kv[:, HW:].T.astype(BF16)
        z = s_ref[:, S_F:S_END - LANES] + bf_ref[...]
        lf_ref[...] = jnp.minimum(z, 0.0) - jnp.log(1.0 + jnp.exp(-jnp.abs(z)))

    def row(w):
        return pl.BlockSpec((bt, w), lambda i: (i, 0))

    def full(arr):
        return pl.BlockSpec(arr.shape, lambda i: (0, 0))

    return pl.pallas_call(
        body, name=name, grid=(t // bt,),
        in_specs=[row(S_END), full(g_q), full(g_kv), full(w_uq), full(w_ukv), row(1), full(inv_row), full(b_f)],
        out_specs=[row(HW), row(HW), row(FW), row(LANES), row(Q_RANK), row(KV_RANK),
                   pl.BlockSpec((1, HW, bt), lambda i: (i, 0, 0)), pl.BlockSpec((1, FW, bt), lambda i: (i, 0, 0))],
        out_shape=[jax.ShapeDtypeStruct((t, HW), BF16)] * 2 + [jax.ShapeDtypeStruct((t, FW), BF16), jax.ShapeDtypeStruct((t, LANES), F32),
                   jax.ShapeDtypeStruct((t, Q_RANK), BF16), jax.ShapeDtypeStruct((t, KV_RANK), BF16),
                   jax.ShapeDtypeStruct((t // bt, HW, bt), BF16), jax.ShapeDtypeStruct((t // bt, FW, bt), BF16)],
        compiler_params=_params(("parallel",)),
    )(small, g_q, g_kv, w_uq, w_ukv, pos_col, inv_row, b_f)


def mla_prep_bwd(dmq, dmk, dmv, dlf, small, g_q, g_kv, w_uq, w_ukv, pos_col, inv_row, b_f, *, name, bt=1024):
    t = small.shape[0]
    bt = min(bt, t)

    def body(dmq_ref, dmk_ref, dmv_ref, dlf_ref, s_ref, gq_ref, gkv_ref, wq_ref, wkv_ref, pos_ref, inv_ref, bf_ref,
             ds_ref, dq_ref, dkv_ref, dgq_ref, dgkv_ref, db_ref):
        tc, ta, tb = _rope_tables(pos_ref[...], inv_ref[...])
        lane = lax.broadcasted_iota(jnp.int32, (1, LANES), 1)
        dkr = jnp.zeros((bt, LANES), F32)
        for h in range(N_HEADS):
            sl = slice(h * LANES, (h + 1) * LANES)
            dq_ref[:, sl] = _rope_bwd(dmq_ref[:, sl], tc, ta, tb).astype(BF16)
            dkr = dkr + dmk_ref[:, sl]
        dkv_ref[:, :HW] = dmk_ref[...].astype(BF16)
        dkv_ref[:, HW:] = dmv_ref[...].astype(BF16)
        in_rope = (lane >= HEAD_DIM) & (lane < HEAD_DIM + ROPE_DIM)
        ds_ref[:, S_KR:S_F] = jnp.where(in_rope, _rope_bwd(dkr, tc, ta, tb), 0.0).astype(BF16)

        def norm_bwd(raw, g_ref, dn, dg_ref):
            r = lax.rsqrt(jnp.mean(raw * raw, axis=-1, keepdims=True) + EPS)
            u = dn * g_ref[...]
            dot = jnp.mean(u * raw, axis=-1, keepdims=True)
            dg_ref[...] += jnp.sum(dn * (raw * r), axis=0, keepdims=True)
            return r * u - raw * (r * r * r * dot)

        @pl.when(pl.program_id(0) == 0)
        def _():
            dgq_ref[...] = jnp.zeros_like(dgq_ref)
            dgkv_ref[...] = jnp.zeros_like(dgkv_ref)
            db_ref[...] = jnp.zeros_like(db_ref)

        dcqn = lax.dot_general(dq_ref[...], wq_ref[...], NT, preferred_element_type=F32)
        ds_ref[:, S_CQ:S_CKV] = norm_bwd(s_ref[:, S_CQ:S_CKV], gq_ref, dcqn, dgq_ref).astype(BF16)
        dckvn = lax.dot_general(dkv_ref[...], wkv_ref[...], NT, preferred_element_type=F32)
        ds_ref[:, S_CKV:S_KR] = norm_bwd(s_ref[:, S_CKV:S_KR], gkv_ref, dckvn, dgkv_ref).astype(BF16)
        z = s_ref[:, S_F:S_END - LANES] + bf_ref[...]
        dz = jnp.where(lane < N_HEADS, dlf_ref[...] / (1.0 + jnp.exp(z)), 0.0)
        db_ref[...] += jnp.sum(dz, axis=0, keepdims=True)
        ds_ref[:, S_F:S_END - LANES] = dz.astype(BF16)
        ds_ref[:, S_END - LANES:] = jnp.zeros((bt, LANES), BF16)

    def row(w):
        return pl.BlockSpec((bt, w), lambda i: (i, 0))

    def full(arr):
        return pl.BlockSpec(arr.shape, lambda i: (0, 0))

    def vec(w):
        return pl.BlockSpec((1, w), lambda i: (0, 0))

    return pl.pallas_call(
        body, name=name, grid=(t // bt,),
        in_specs=[row(HW), row(HW), row(FW), row(LANES), row(S_END), full(g_q), full(g_kv), full(w_uq), full(w_ukv),
                  row(1), full(inv_row), full(b_f)],
        out_specs=[row(S_END), row(HW), row(HW + FW), vec(Q_RANK), vec(KV_RANK), vec(LANES)],
        out_shape=[jax.ShapeDtypeStruct((t, S_END), BF16), jax.ShapeDtypeStruct((t, HW), BF16),
                   jax.ShapeDtypeStruct((t, HW + FW), BF16), jax.ShapeDtypeStruct((1, Q_RANK), F32),
                   jax.ShapeDtypeStruct((1, KV_RANK), F32), jax.ShapeDtypeStruct((1, LANES), F32)],
        compiler_params=_params(("arbitrary",)),
    )(dmq, dmk, dmv, dlf, small, g_q, g_kv, w_uq, w_ukv, pos_col, inv_row, b_f)


class Side:
    def __init__(self, ins, out_shapes, n_sems, first, last, mid=None):
        self.ins, self.out_shapes, self.n_sems = list(ins), list(out_shapes), n_sems
        self.first, self.mid, self.last = first, mid, last

    def specs(self):
        return [ANY] * len(self.ins), [ANY] * len(self.out_shapes)

    def sems(self):
        return [pltpu.SemaphoreType.DMA((self.n_sems,)), pltpu.SemaphoreType.DMA((self.n_sems,))]


def _lane():
    return lax.broadcasted_iota(jnp.int32, (1, LANES), 1)


def _halves(x):
    zero = jnp.zeros_like(x)
    return [jnp.where(_lane() < HEAD_DIM, x, zero), jnp.where(_lane() >= HEAD_DIM, x, zero)]


def _groups(x):
    return [x[:, :LANES], x[:, LANES:]]


def _pick_row(tile, h):
    row = lax.broadcasted_iota(jnp.int32, (tile.shape[0], 1), 0)
    return jnp.sum(jnp.where(row == h, tile, 0.0), axis=0, keepdims=True)


def _pick_lane(tile, h):
    return jnp.sum(jnp.where(_lane() == h, tile, 0.0), axis=1, keepdims=True)


def _row_halves(x):
    row = lax.broadcasted_iota(jnp.int32, (LANES, 1), 0)
    zero = jnp.zeros_like(x)
    return [jnp.where(row < HEAD_DIM, x, zero), jnp.where(row >= HEAD_DIM, x, zero)]


def _below_diagonal(s):
    r = lax.broadcasted_iota(jnp.int32, s.shape, 0)
    c = lax.broadcasted_iota(jnp.int32, s.shape, 1)
    return jnp.where(c <= r, s, -jnp.inf)


def _above_diagonal(s):
    r = lax.broadcasted_iota(jnp.int32, s.shape, 0)
    c = lax.broadcasted_iota(jnp.int32, s.shape, 1)
    return jnp.where(r <= c, s, -jnp.inf)


def _split_refs(refs, counts):
    out, at = [], 0
    for n in counts:
        out.append(refs[at:at + n])
        at += n
    return out


def flash_fwd(qt_arr, k_arr, vt_arr, f_cum, *, qoff, koff, voff, pair, scale, name, blk=512, side=None):
    t = k_arr.shape[0]
    tblk = qt_arr.shape[2]
    blk = max(min(blk, t), tblk)
    sub = blk // tblk
    nb = t // blk
    steps = PAIRS * nb
    w = LANES if pair else 2 * LANES
    has_bias = f_cum is not None
    ins = [qt_arr, k_arr, vt_arr] + ([f_cum] if has_bias else [])

    def wide(ref, first):
        parts = [ref[first + u] for u in range(sub)]
        return parts[0] if sub == 1 else jnp.concatenate(parts, axis=1)
    s_ins, s_outs = (side.ins, side.out_shapes) if side else ([], [])

    def body(*refs):
        main, si, outs, so, sems = _split_refs(refs, [len(ins), len(s_ins), 2, len(s_outs), 2 if side else 0])
        qt_ref, k_ref, vt_ref = main[:3]
        f_ref = main[3] if has_bias else None
        o_ref, st_ref = outs
        g, i = pl.program_id(0), pl.program_id(1)
        step_id = g * nb + i
        if side:
            @pl.when(step_id == 0)
            def _():
                side.first(si, so, *sems)

            if side.mid is not None:
                @pl.when(step_id == (3 * steps) // 4)
                def _():
                    side.mid(si, so, *sems)

        qt = (wide(qt_ref, 0).astype(F32) * (scale * LOG2E)).astype(BF16)
        qts = _row_halves(qt) if pair else [qt[:LANES], qt[LANES:]]

        def k_of(kk, n):
            return kk if pair else kk[:, n * LANES:(n + 1) * LANES]

        def with_ones(vt_rows):
            return jnp.concatenate([vt_rows, jnp.ones((ACC_ROWS - HEAD_DIM, vt_rows.shape[1]), BF16)], axis=0)

        def step(j, carry, diagonal):
            rows = pl.ds(pl.multiple_of(j * blk, blk), blk)
            kk = k_ref[rows, :]
            vt = wide(vt_ref, sub * j)
            out = []
            for n in range(2):
                m, acc = carry[n]
                s = jnp.dot(k_of(kk, n), qts[n], preferred_element_type=F32)
                if has_bias:
                    s = s - LOG2E * _pick_lane(f_ref[rows, :], 2 * g + n)
                if diagonal:
                    s = _above_diagonal(s)
                m_new = jnp.maximum(m, jnp.max(s, axis=0, keepdims=True))
                p = jnp.exp2(s - m_new).astype(BF16)
                out.append((m_new, jnp.exp2(m - m_new) * acc
                            + jnp.dot(with_ones(vt[n * HEAD_DIM:(n + 1) * HEAD_DIM]), p, preferred_element_type=F32)))
            return tuple(out)

        def diagonal_in_halves(carry):
            h = tblk
            halves = [pl.ds(pl.multiple_of(i * blk, blk), h), pl.ds(pl.multiple_of(i * blk + h, h), h)]
            k_top, k_bot = k_ref[halves[0], :], k_ref[halves[1], :]
            vt_top, vt_bot = vt_ref[sub * i], vt_ref[sub * i + 1]
            out = []
            for n in range(2):
                m, acc = carry[n]
                heads = slice(n * HEAD_DIM, (n + 1) * HEAD_DIM)
                s_top = jnp.dot(k_of(k_top, n), qts[n], preferred_element_type=F32)
                s_bot = jnp.dot(k_of(k_bot, n), qts[n][:, h:], preferred_element_type=F32)
                if has_bias:
                    s_top = s_top - LOG2E * _pick_lane(f_ref[halves[0], :], 2 * g + n)
                    s_bot = s_bot - LOG2E * _pick_lane(f_ref[halves[1], :], 2 * g + n)
                s_top, s_bot = _above_diagonal(s_top), _above_diagonal(s_bot)
                m_top = jnp.maximum(m, jnp.max(s_top, axis=0, keepdims=True))
                m_new = jnp.concatenate([m_top[:, :h], jnp.maximum(m_top[:, h:], jnp.max(s_bot, axis=0, keepdims=True))], axis=1)
                p_top = jnp.exp2(s_top - m_new).astype(BF16)
                p_bot = jnp.exp2(s_bot - m_new[:, h:]).astype(BF16)
                late = jnp.concatenate([jnp.zeros((ACC_ROWS, h), F32),
                                        jnp.dot(with_ones(vt_bot[heads]), p_bot, preferred_element_type=F32)], axis=1)
                out.append((m_new, jnp.exp2(m - m_new) * acc
                            + jnp.dot(with_ones(vt_top[heads]), p_top, preferred_element_type=F32) + late))
            return tuple(out)

        init = tuple((jnp.full((1, blk), -jnp.inf, F32), jnp.zeros((ACC_ROWS, blk), F32)) for _ in range(2))
        carry = lax.fori_loop(0, i, lambda j, c: step(j, c, False), init)
        (ma, acca), (mb, accb) = diagonal_in_halves(carry) if sub == 2 else step(i, carry, True)
        la, lb = acca[HEAD_DIM:HEAD_DIM + 1], accb[HEAD_DIM:HEAD_DIM + 1]
        o_ref[...] = jnp.concatenate([acca[:HEAD_DIM] / la, accb[:HEAD_DIM] / lb], axis=0).T
        row = lax.broadcasted_iota(jnp.int32, (LANES, 1), 0)
        st_ref[0] = jnp.where(row == 0, ma + jnp.log2(la), jnp.where(row == 1, mb + jnp.log2(lb), 0.0)).T
        if side:
            @pl.when(step_id == steps - 1)
            def _():
                side.last(si, so, *sems)

    in_specs = [pl.BlockSpec((sub, w, tblk), lambda g, i: (i, qoff + g, 0)), pl.BlockSpec((t, w), lambda g, i: (0, koff + g)),
                pl.BlockSpec((t // tblk, LANES, tblk), lambda g, i: (0, voff + g, 0))]
    if has_bias:
        in_specs.append(pl.BlockSpec((t, LANES), lambda g, i: (0, 0)))
    s_in_specs, s_out_specs = side.specs() if side else ([], [])
    return pl.pallas_call(
        body, name=name, grid=(PAIRS, nb), in_specs=in_specs + s_in_specs,
        out_specs=[pl.BlockSpec((blk, LANES), lambda g, i: (i, g)), pl.BlockSpec((1, blk, LANES), lambda g, i: (g, i, 0))]
        + s_out_specs,
        out_shape=[jax.ShapeDtypeStruct((t, PAIRS * LANES), F32), jax.ShapeDtypeStruct((PAIRS, t, LANES), F32)] + list(s_outs),
        scratch_shapes=side.sems() if side else [],
        compiler_params=_params(("arbitrary", "arbitrary")),
    )(*ins, *s_ins)


def mix_norm(fo, mo, g_fo, g_mo, *, name, bt=512):
    t, d = fo.shape
    bt = min(bt, t)

    def body(fo_ref, mo_ref, gf_ref, gm_ref, o_ref):
        for n, (x_ref, g_ref) in enumerate(((fo_ref, gf_ref), (mo_ref, gm_ref))):
            xv = x_ref[...]
            r = lax.rsqrt(jnp.mean(xv * xv, axis=-1, keepdims=True) + EPS)
            o_ref[:, n * d:(n + 1) * d] = (xv * r * g_ref[...]).astype(BF16)

    row = pl.BlockSpec((bt, d), lambda i: (i, 0))
    vec = pl.BlockSpec((1, d), lambda i: (0, 0))
    return pl.pallas_call(
        body, name=name, grid=(t // bt,), in_specs=[row, row, vec, vec],
        out_specs=pl.BlockSpec((bt, 2 * d), lambda i: (i, 0)),
        out_shape=jax.ShapeDtypeStruct((t, 2 * d), BF16),
        compiler_params=_params(("parallel",)),
    )(fo, mo, g_fo, g_mo)


def mix_norm_bwd(dx1b, w_o, fo, mo, g_fo, g_mo, st_f, st_m, *, name, bt=512, side=None):
    t, d = fo.shape
    bt = min(bt, t)

    def body(dx_in_ref, w_ref, fo_ref, mo_ref, gf_ref, gm_ref, sf_ref, sm_ref, dfo_ref, dmo_ref, dgf_ref, dgm_ref,
             sfo_ref, smo_ref, dfot_ref, dmot_ref):
        @pl.when(pl.program_id(0) == 0)
        def _():
            dgf_ref[...] = jnp.zeros_like(dgf_ref)
            dgm_ref[...] = jnp.zeros_like(dgm_ref)

        dmixed = lax.dot_general(dx_in_ref[...], w_ref[...], NT, preferred_element_type=F32)
        groups = ((fo_ref, gf_ref, dfo_ref, dgf_ref, sf_ref, sfo_ref, dfot_ref),
                  (mo_ref, gm_ref, dmo_ref, dgm_ref, sm_ref, smo_ref, dmot_ref))
        for n, (x_ref, g_ref, dx_ref, dg_ref, st_ref, sto_ref, dxt_ref) in enumerate(groups):
            xv = x_ref[...]
            dhv = dmixed[:, n * d:(n + 1) * d]
            r = lax.rsqrt(jnp.mean(xv * xv, axis=-1, keepdims=True) + EPS)
            u = dhv * g_ref[...]
            dxf = r * u - xv * (r * r * r * jnp.mean(u * xv, axis=-1, keepdims=True))
            dxb = dxf.astype(BF16)
            dx_ref[...] = dxb
            dxt_ref[0] = dxf.T.astype(BF16)
            dg_ref[...] += jnp.sum(dhv * (xv * r), axis=0, keepdims=True)
            prod = xv * dxb.astype(F32)
            for g in range(PAIRS):
                grp = prod[:, g * LANES:(g + 1) * LANES]
                da = jnp.sum(jnp.where(_lane() < HEAD_DIM, grp, 0.0), axis=1, keepdims=True)
                db = jnp.sum(jnp.where(_lane() >= HEAD_DIM, grp, 0.0), axis=1, keepdims=True)
                sto_ref[g] = jnp.where(_lane() == 2, da, jnp.where(_lane() == 3, db, st_ref[g]))

    row = pl.BlockSpec((bt, d), lambda i: (i, 0))
    vec = pl.BlockSpec((1, d), lambda i: (0, 0))
    stat = pl.BlockSpec((PAIRS, bt, LANES), lambda i: (0, i, 0))
    return gridded(
        body, name=name, grid=(t // bt,),
        in_specs=[pl.BlockSpec((bt, dx1b.shape[1]), lambda i: (i, 0)), pl.BlockSpec(w_o.shape, lambda i: (0, 0)),
                  row, row, vec, vec, stat, stat],
        out_specs=[row, row, vec, vec, stat, stat] + [pl.BlockSpec((1, d, bt), lambda i: (i, 0, 0))] * 2,
        out_shape=[jax.ShapeDtypeStruct((t, d), BF16)] * 2 + [jax.ShapeDtypeStruct((1, d), F32)] * 2
        + [jax.ShapeDtypeStruct(st_f.shape, F32)] * 2 + [jax.ShapeDtypeStruct((t // bt, d, bt), BF16)] * 2,
        ins=[dx1b, w_o, fo, mo, g_fo, g_mo, st_f, st_m], semantics=("arbitrary",), side=side)


def flash_bwd(q_arr, qt_arr, k_arr, v_arr, do_arr, dot_arr, st, f_blocks, *, qoff, koff, voff, pair, scale, name, qblk=1024,
              side=None):
    t = q_arr.shape[0]
    blk = qt_arr.shape[2]
    qblk = max(min(qblk, t), blk)
    sub = qblk // blk
    nb, nbq = t // blk, t // qblk
    w = LANES if pair else 2 * LANES
    hw = w // 2
    has_bias = f_blocks is not None
    split = _halves if pair else _groups
    ins = [q_arr, qt_arr, k_arr, v_arr, do_arr, dot_arr, st] + ([f_blocks] if has_bias else [])
    n_out = 4 if has_bias else 3
    s_ins, s_outs = (side.ins, side.out_shapes) if side else ([], [])

    def wide(ref, first, count):
        parts = [ref[first + u] for u in range(count)]
        return parts[0] if count == 1 else jnp.concatenate(parts, axis=1)

    def body(*refs):
        main, si, outs, so, sems = _split_refs(refs, [len(ins), len(s_ins), n_out, len(s_outs), 2 if side else 0])
        q_ref, qt_ref, k_ref, v_ref, do_ref, dot_ref, st_ref = main[:7]
        dq_ref, dk_ref, dv_ref = outs[:3]
        g, j = pl.program_id(0), pl.program_id(1)
        step_id = g * nb + j
        if side:
            @pl.when(step_id == 0)
            def _():
                side.first(si, so, *sems)

        @pl.when(j == 0)
        def _():
            dq_ref[...] = jnp.zeros_like(dq_ref)

        kk, vv = k_ref[...], v_ref[...]
        ks = [kk, kk] if pair else _groups(kk)
        if has_bias:
            f_ref, df_ref = main[7], outs[3]
            fk = [LOG2E * _pick_row(f_ref[0], 2 * g + n) for n in range(2)]

            @pl.when(step_id == 0)
            def _():
                df_ref[...] = jnp.zeros_like(df_ref)

        def step(tb, count, carry, diagonal):
            rows = pl.ds(pl.multiple_of(tb * blk, blk), count * blk)
            qs = split((q_ref[rows, :].astype(F32) * (scale * LOG2E)).astype(BF16))
            qt = (wide(qt_ref, tb, count).astype(F32) * (scale * LOG2E)).astype(BF16)
            dos = [h.astype(BF16) for h in _halves(do_ref[rows, :].astype(F32))]
            dot = wide(dot_ref, tb, count)
            stats = st_ref[0, rows, :]
            new, dqs, row_sums = [], [], []
            for n in range(2):
                dkt, dvt, dfk = carry[n]
                s = lax.dot_general(qs[n], ks[n], NT, preferred_element_type=F32)
                if has_bias:
                    s = s - fk[n]
                if diagonal:
                    s = _below_diagonal(s)
                p = jnp.exp2(s - stats[:, n:n + 1])
                dp = lax.dot_general(dos[n], vv, NT, preferred_element_type=F32)
                ds = p * (dp - stats[:, 2 + n:3 + n])
                dsb = ds.astype(BF16)
                dvt = dvt + jnp.dot(dot[n * HEAD_DIM:(n + 1) * HEAD_DIM], p.astype(BF16), preferred_element_type=F32)
                dkt = dkt + jnp.dot(qt[n * hw:(n + 1) * hw], dsb, preferred_element_type=F32)
                dqs.append(jnp.dot(dsb, ks[n], preferred_element_type=F32))
                if has_bias:
                    dfk = dfk - jnp.sum(ds, axis=0, keepdims=True)
                    row_sums.append(jnp.sum(ds, axis=1, keepdims=True))
                new.append((dkt, dvt, dfk))
            dq = (jnp.where(_lane() < HEAD_DIM, dqs[0], dqs[1]) if pair else jnp.concatenate(dqs, axis=1)) * scale
            if has_bias:
                df_ref[rows, :] += jnp.where(_lane() == 2 * g, row_sums[0], jnp.where(_lane() == 2 * g + 1, row_sums[1], 0.0))
            dq_ref[rows, :] += dq
            return tuple(new)

        init = tuple((jnp.zeros((hw, blk), F32), jnp.zeros((HEAD_DIM, blk), F32), jnp.zeros((1, blk), F32)) for _ in range(2))
        carry = step(j, 1, init, True)
        whole = j // sub + 1
        carry = lax.fori_loop(j + 1, whole * sub, lambda tb, c: step(tb, 1, c, False), carry)
        (dka, dva, dfa), (dkb, dvb, dfb) = lax.fori_loop(whole, nbq, lambda i, c: step(i * sub, sub, c, False), carry)
        dk_ref[...] = (jnp.concatenate([dka, dkb], axis=0).T * LN2).astype(BF16)
        dv_ref[...] = jnp.concatenate([dva, dvb], axis=0).T.astype(BF16)
        if has_bias:
            row = lax.broadcasted_iota(jnp.int32, (LANES, 1), 0)
            by_head = jnp.where(row == 2 * g, dfa, jnp.where(row == 2 * g + 1, dfb, 0.0))
            df_ref[pl.ds(pl.multiple_of(j * blk, blk), blk), :] += by_head.T
        if side:
            @pl.when(step_id == PAIRS * nb - 1)
            def _():
                side.last(si, so, *sems)

    in_specs = [pl.BlockSpec((t, w), lambda g, j: (0, qoff + g)), pl.BlockSpec((nb, w, blk), lambda g, j: (0, qoff + g, 0)),
                pl.BlockSpec((blk, w), lambda g, j: (j, koff + g)), pl.BlockSpec((blk, LANES), lambda g, j: (j, voff + g)),
                pl.BlockSpec((t, LANES), lambda g, j: (0, g)), pl.BlockSpec((nb, LANES, blk), lambda g, j: (0, g, 0)),
                pl.BlockSpec((1, t, LANES), lambda g, j: (g, 0, 0))]
    out_specs = [pl.BlockSpec((t, w), lambda g, j: (0, g)), pl.BlockSpec((blk, w), lambda g, j: (j, g)),
                 pl.BlockSpec((blk, LANES), lambda g, j: (j, g))]
    out_shape = [jax.ShapeDtypeStruct((t, PAIRS * w), F32), jax.ShapeDtypeStruct((t, PAIRS * w), BF16),
                 jax.ShapeDtypeStruct((t, PAIRS * LANES), BF16)]
    if has_bias:
        in_specs.append(pl.BlockSpec((1, N_HEADS, blk), lambda g, j: (j, 0, 0)))
        out_specs.append(pl.BlockSpec((t, LANES), lambda g, j: (0, 0)))
        out_shape.append(jax.ShapeDtypeStruct((t, LANES), F32))
    s_in_specs, s_out_specs = side.specs() if side else ([], [])
    return pl.pallas_call(
        body, name=name, grid=(PAIRS, nb), in_specs=in_specs + s_in_specs, out_specs=out_specs + s_out_specs,
        out_shape=out_shape + list(s_outs), scratch_shapes=side.sems() if side else [],
        compiler_params=_params(("arbitrary", "arbitrary")),
    )(*ins, *s_ins)


def _adamw_math(w, g, m, v):
    nm = ADAM_B1 * m + (1.0 - ADAM_B1) * g
    nv = ADAM_B2 * v + (1.0 - ADAM_B2) * (g * g)
    m_hat = nm / (1.0 - ADAM_B1 ** ADAM_STEP)
    v_hat = nv / (1.0 - ADAM_B2 ** ADAM_STEP)
    return -ADAM_LR * (m_hat / (jnp.sqrt(v_hat) + ADAM_EPS) + ADAM_WD * w), nm, nv


def adamw_vectors(ws, g_rows, ms, vs, *, name):
    k = len(ws)

    def body(g_ref, *refs):
        w_refs, m_refs, v_refs, outs = _split_refs(refs, [k, k, k, 4 * k])
        for i in range(k):
            g = g_ref[i:i + 1, :w_refs[i].shape[1]]
            outs[4 * i][...] = g
            outs[4 * i + 1][...], outs[4 * i + 2][...], outs[4 * i + 3][...] = _adamw_math(
                w_refs[i][...], g, m_refs[i][...], v_refs[i][...])

    flat = pl.pallas_call(
        body, name=name, out_shape=[jax.ShapeDtypeStruct(w.shape, F32) for w in ws for _ in range(4)],
        compiler_params=_params(),
    )(g_rows, *ws, *ms, *vs)
    return [flat[4 * i:4 * i + 4] for i in range(k)]


def adamw_whole(w, g, m, v, *, name):
    def body(w_ref, g_ref, m_ref, v_ref, d_ref, nm_ref, nv_ref):
        d_ref[...], nm_ref[...], nv_ref[...] = _adamw_math(w_ref[...], g_ref[...], m_ref[...], v_ref[...])

    return pl.pallas_call(body, name=name, out_shape=[jax.ShapeDtypeStruct(w.shape, F32)] * 3,
                          compiler_params=_params())(w, g, m, v)


def adamw_halves(w, g_mine, g_other, m, v, core, *, name):
    _, k, n = w.shape
    br = min(k // 2, max(8, ADAMW_BLOCK // n))
    nh = k // 2 // br

    def body(c_ref, w_ref, gm_ref, go_ref, m_ref, v_ref, g_out, d_ref, nm_ref, nv_ref):
        gv = jnp.where(pl.program_id(0) == c_ref[0], gm_ref[...], go_ref[...])
        g_out[0] = gv
        d_ref[0], nm_ref[0], nv_ref[0] = _adamw_math(w_ref[0], gv, m_ref[0], v_ref[0])

    full = pl.BlockSpec((1, br, n), lambda hb, i, c: (0, hb * nh + i, 0))
    half = pl.BlockSpec((br, n), lambda hb, i, c: (i, 0))
    return pl.pallas_call(
        body, name=name,
        grid_spec=pltpu.PrefetchScalarGridSpec(num_scalar_prefetch=1, grid=(2, nh), in_specs=[full, half, half, full, full],
                                               out_specs=[full] * 4),
        out_shape=[jax.ShapeDtypeStruct(w.shape, F32)] * 4,
        compiler_params=_params(("parallel", "parallel")),
    )(core, w, g_mine, g_other, m, v)


def adamw_halves_t(wt, mt, vt, gt_mine, gt_other, core, *, name, bc=512):
    n, _, k = wt.shape
    nh = k // 2 // bc

    def body(c_ref, w_ref, m_ref, v_ref, gm_ref, go_ref, g_out, d_ref, nm_ref, nv_ref):
        gv = jnp.where(pl.program_id(0) == c_ref[0], gm_ref[...], go_ref[...])
        g_out[:, 0, :] = gv
        d_ref[:, 0, :], nm_ref[:, 0, :], nv_ref[:, 0, :] = _adamw_math(w_ref[:, 0, :], gv, m_ref[:, 0, :], v_ref[:, 0, :])

    full = pl.BlockSpec((n, 1, bc), lambda hb, i, c: (0, 0, hb * nh + i))
    half = pl.BlockSpec((n, bc), lambda hb, i, c: (0, i))
    return pl.pallas_call(
        body, name=name,
        grid_spec=pltpu.PrefetchScalarGridSpec(num_scalar_prefetch=1, grid=(2, nh), in_specs=[full, full, full, half, half],
                                               out_specs=[full] * 4),
        out_shape=[jax.ShapeDtypeStruct(wt.shape, F32)] * 4,
        compiler_params=_params(("parallel", "parallel")),
    )(core, wt, mt, vt, gt_mine, gt_other)


def add_pair(dw, recv, core, *, name):
    n4, k, n = dw.shape
    half = (1, k // 2, n) if split_axis(k) == 0 else (1, k, n // 2)
    mine = (lambda q, c: (q, c[0], 0)) if split_axis(k) == 0 else (lambda q, c: (q, 0, c[0]))

    def body(c_ref, a_ref, b_ref, o_ref):
        o_ref[...] = (a_ref[...] + b_ref[...].astype(F32)).astype(BF16)

    return pl.pallas_call(
        body, name=name,
        grid_spec=pltpu.PrefetchScalarGridSpec(
            num_scalar_prefetch=1, grid=(n4,),
            in_specs=[pl.BlockSpec(half, mine), pl.BlockSpec(half, lambda q, c: (q, 0, 0))],
            out_specs=pl.BlockSpec(half, lambda q, c: (q, 0, 0))),
        out_shape=jax.ShapeDtypeStruct((n4,) + half[1:], BF16),
        compiler_params=_params(("parallel",)),
    )(core, dw, recv)


def sum_chips(parts, *, name):
    n4, r, n = parts.shape
    if r % 16 == 0:
        br, bc = _row_block(r), n
    else:
        br, bc = r, LANES

    def body(p_ref, o_ref):
        acc = p_ref[0].astype(F32)
        for q in range(1, n4):
            acc = acc + p_ref[q].astype(F32)
        o_ref[...] = acc

    return pl.pallas_call(
        body, name=name, grid=(r // br, n // bc),
        in_specs=[pl.BlockSpec((n4, br, bc), lambda i, j: (0, i, j))], out_specs=pl.BlockSpec((br, bc), lambda i, j: (i, j)),
        out_shape=jax.ShapeDtypeStruct((r, n), F32),
        compiler_params=_params(("parallel", "parallel")),
    )(parts)


ANY = pl.BlockSpec(memory_space=pl.ANY)


def _place():
    x, y, c = lax.axis_index("x"), lax.axis_index("y"), lax.axis_index("c")
    chips = [(1 - x, y), (x, 1 - y), (1 - x, 1 - y)]
    return x, y, c, chips


def _copy(src, dst, send_sems, recv_sems, k, to):
    return pltpu.make_async_remote_copy(src_ref=src, dst_ref=dst, send_sem=send_sems.at[k], recv_sem=recv_sems.at[k],
                                        device_id=to, device_id_type=MESH)


def split_axis(rows):
    return 0 if rows % 32 == 0 else 1


def _half(ref, lead, hf):
    rows, cols = ref.shape[-2:]
    if split_axis(rows) == 0:
        at = (pl.ds(hf * (rows // 2), rows // 2), slice(None))
    else:
        at = (slice(None), pl.ds(hf * (cols // 2), cols // 2))
    return ref.at[at] if lead is None else ref.at[(lead,) + at]


def _gather_first(srcs, dsts, ssems, rsems):
    x, y, c, chips = _place()
    for ti, (s, d) in enumerate(zip(srcs, dsts)):
        for j, (cx, cy) in enumerate(chips):
            _copy(_half(s, None, c), _half(d, 2 * x + y, c), ssems, rsems, 3 * ti + j, (cx, cy, c)).start()


def _gather_mid(srcs, dsts, ssems, rsems):
    x, y, c, chips = _place()
    n1 = 3 * len(srcs)
    for ti, d in enumerate(dsts):
        for j, (cx, cy) in enumerate(chips):
            landed = _half(d, 2 * cx + cy, c)
            _copy(landed, landed, ssems, rsems, 3 * ti + j, (cx, cy, c)).wait_recv()
            _copy(landed, landed, ssems, rsems, n1 + 3 * ti + j, (x, y, 1 - c)).start()


def _gather_last(srcs, dsts, ssems, rsems):
    x, y, c, chips = _place()
    n1 = 3 * len(srcs)
    for ti, (s, d) in enumerate(zip(srcs, dsts)):
        for j, (cx, cy) in enumerate(chips):
            other = _half(d, 2 * cx + cy, 1 - c)
            _copy(other, other, ssems, rsems, n1 + 3 * ti + j, (x, y, 1 - c)).wait_recv()
        for j, (cx, cy) in enumerate(chips):
            mine = _half(s, None, c)
            _copy(mine, mine, ssems, rsems, 3 * ti + j, (cx, cy, c)).wait_send()
            _copy(mine, mine, ssems, rsems, n1 + 3 * ti + j, (x, y, 1 - c)).wait_send()


def gather_side(shards):
    return Side(shards, [jax.ShapeDtypeStruct((N_CHIPS,) + s.shape, s.dtype) for s in shards], 6 * len(shards),
                _gather_first, _gather_last, _gather_mid)


def _scatter_first(srcs, dsts, ssems, rsems):
    x, y, c, chips = _place()
    for ti, (s, d) in enumerate(zip(srcs, dsts)):
        for j, (cx, cy) in enumerate(chips):
            _copy(s.at[2 * cx + cy], d.at[2 * x + y], ssems, rsems, 3 * ti + j, (cx, cy, c)).start()


def _scatter_last(srcs, dsts, ssems, rsems):
    x, y, c, chips = _place()
    for ti, (s, d) in enumerate(zip(srcs, dsts)):
        for j, (cx, cy) in enumerate(chips):
            _copy(s.at[2 * cx + cy], d.at[2 * cx + cy], ssems, rsems, 3 * ti + j, (cx, cy, c)).wait_recv()
        for j, (cx, cy) in enumerate(chips):
            _copy(s.at[2 * cx + cy], d.at[2 * cx + cy], ssems, rsems, 3 * ti + j, (cx, cy, c)).wait_send()


def scatter_side(parts):
    return Side(parts, [jax.ShapeDtypeStruct(p.shape, p.dtype) for p in parts], 3 * len(parts), _scatter_first, _scatter_last)


def run_side(side, *, name):
    n_in, n_out = len(side.ins), len(side.out_shapes)

    def body(*refs):
        si, so, sems = _split_refs(refs, [n_in, n_out, 2])
        side.first(si, so, *sems)
        if side.mid is not None:
            side.mid(si, so, *sems)
        side.last(si, so, *sems)

    in_specs, out_specs = side.specs()
    return pl.pallas_call(body, name=name, in_specs=in_specs, out_specs=out_specs, out_shape=side.out_shapes,
                          scratch_shapes=side.sems())(*side.ins)


def _swap_first(srcs, dsts, ssems, rsems):
    x, y, c, _ = _place()
    for k, (s, d) in enumerate(zip(srcs, dsts)):
        _copy(s, d, ssems, rsems, k, (x, y, 1 - c)).start()


def _swap_last(srcs, dsts, ssems, rsems):
    x, y, c, _ = _place()
    for k, (s, d) in enumerate(zip(srcs, dsts)):
        _copy(s, d, ssems, rsems, k, (x, y, 1 - c)).wait()


def swap_side(xs):
    return Side(xs, [jax.ShapeDtypeStruct(a.shape, a.dtype) for a in xs], len(xs), _swap_first, _swap_last)


def _swap_halves(srcs, dsts, ssems, rsems):
    x, y, c, _ = _place()
    for k, (s, d) in enumerate(zip(srcs, dsts)):
        hk = s.shape[1] // 2
        yield _copy(s.at[:, pl.ds((1 - c) * hk, hk), :], d, ssems, rsems, k, (x, y, 1 - c))


def _swap_halves_first(srcs, dsts, ssems, rsems):
    for cp in _swap_halves(srcs, dsts, ssems, rsems):
        cp.start()


def _swap_halves_last(srcs, dsts, ssems, rsems):
    for cp in _swap_halves(srcs, dsts, ssems, rsems):
        cp.wait()


def swap_halves_side(xs):
    return Side(xs, [jax.ShapeDtypeStruct((a.shape[0], a.shape[1] // 2, a.shape[2]), a.dtype) for a in xs], len(xs),
                _swap_halves_first, _swap_halves_last)


def allreduce_small(s):
    n_dev = 8

    def body(s_ref, out_ref, buf, send_sems, recv_sems):
        x, y, c, _ = _place()
        me = 4 * x + 2 * y + c
        buf[me] = s_ref[...]
        sends = []
        for k in range(1, n_dev):
            px = 1 - x if k & 4 else x
            py = 1 - y if k & 2 else y
            pc = 1 - c if k & 1 else c
            cp = _copy(s_ref, buf.at[me], send_sems, recv_sems, k - 1, (px, py, pc))
            cp.start()
            sends.append((cp, 4 * px + 2 * py + pc))
        for k, (cp, peer) in enumerate(sends):
            _copy(s_ref, buf.at[peer], send_sems, recv_sems, k, (x, y, c)).wait_recv()
        for cp, _ in sends:
            cp.wait_send()
        acc = buf[0]
        for d in range(1, n_dev):
            acc = acc + buf[d]
        out_ref[...] = acc

    vm = pl.BlockSpec(memory_space=pltpu.VMEM)
    return pl.pallas_call(
        body, name="allreduce_small", in_specs=[vm], out_specs=vm,
        out_shape=jax.ShapeDtypeStruct(s.shape, F32),
        scratch_shapes=[pltpu.VMEM((n_dev,) + s.shape, F32), pltpu.SemaphoreType.DMA((n_dev - 1,)),
                        pltpu.SemaphoreType.DMA((n_dev - 1,))],
    )(s)


def join_cols(sm):
    n4, k, n = sm.shape
    return sm.transpose(1, 0, 2).reshape(k, n4 * n)


def split_cols(full):
    k, n = full.shape
    return full.reshape(k, N_CHIPS, n // N_CHIPS).transpose(1, 0, 2)


def _pad_heads(w, width):
    lead, heads = w.shape[:-1], w.shape[-1] // width
    w = w.reshape(lead + (heads, width))
    return jnp.pad(w, [(0, 0)] * len(lead) + [(0, 0), (0, LANES - width)]).reshape(lead + (heads * LANES,))


def _unpad_heads(w, width):
    lead, heads = w.shape[:-1], w.shape[-1] // LANES
    return w.reshape(lead + (heads, LANES))[..., :width].reshape(lead + (heads * width,))


O_F = 3 * FW
O_CQ = O_F + N_HEADS
O_CKV = O_CQ + Q_RANK
O_KR = O_CKV + KV_RANK
O_END = O_KR + ROPE_DIM


def split_w_in_t(w_sm):
    n4, r, d = w_sm.shape
    segments = [(0, O_F, 0, 0), (O_F, N_HEADS, 1, S_F), (O_CQ, Q_RANK, 1, S_CQ), (O_CKV, KV_RANK, 1, S_CKV),
                (O_KR, ROPE_DIM, 1, S_KR + HEAD_DIM)]
    moves = []
    for v0, n, dst, d0 in segments:
        for q in range(n4):
            a, b = max(v0, q * r), min(v0 + n, (q + 1) * r)
            if a < b:
                assert a % 2 == 0 and b % 2 == 0 and (d0 + a - v0) % 2 == 0
                moves.append((q, (a - q * r) // 2, (b - a) // 2, dst, (d0 + a - v0) // 2))

    def body(w_ref, qkv_ref, small_ref, src, dst_qkv, dst_small):
        for q in range(n4):
            src[q] = pltpu.bitcast(w_ref[q], jnp.uint32)
        dst_small[...] = jnp.zeros(dst_small.shape, jnp.uint32)
        for q, a, n, dst, b in moves:
            (dst_qkv, dst_small)[dst][pl.ds(b, n), :] = src[q, pl.ds(a, n), :]
        qkv_ref[...] = pltpu.bitcast(dst_qkv[...], w_sm.dtype)
        small_ref[...] = pltpu.bitcast(dst_small[...], w_sm.dtype)

    return pl.pallas_call(
        body, name="split_w_in",
        out_shape=[jax.ShapeDtypeStruct((O_F, d), w_sm.dtype), jax.ShapeDtypeStruct((S_END, d), w_sm.dtype)],
        scratch_shapes=[pltpu.VMEM((n4, r // 2, d), jnp.uint32), pltpu.VMEM((O_F // 2, d), jnp.uint32),
                        pltpu.VMEM((S_END // 2, d), jnp.uint32)],
        compiler_params=_params(),
    )(w_sm)


def join_w_in_t(d_qkv_t, d_small_t):
    small = len(d_qkv_t)
    segments = [(n, 0, n * FW, FW) for n in range(small)]
    segments += [(small, S_F, O_F, N_HEADS), (small, S_CQ, O_CQ, Q_RANK), (small, S_CKV, O_CKV, KV_RANK),
                 (small, S_KR + HEAD_DIM, O_KR, ROPE_DIM)]
    r, d = O_END // N_CHIPS, d_small_t.shape[1]

    def body(*refs):
        o_ref = refs[-1]
        for src, s0, v0, n in segments:
            for q in range(N_CHIPS):
                a, b = max(v0, q * r), min(v0 + n, (q + 1) * r)
                if a < b:
                    o_ref[q, pl.ds(a - q * r, b - a), :] = refs[src][pl.ds(s0 + a - v0, b - a), :]

    return pl.pallas_call(
        body, name="join_w_in", out_shape=jax.ShapeDtypeStruct((N_CHIPS, r, d), F32), compiler_params=_params(),
    )(*d_qkv_t, d_small_t)


def rope_inputs(pos):
    inv_freq = ROPE_THETA ** (-jnp.arange(0, ROPE_DIM, 2, dtype=F32) / ROPE_DIM)
    inv_row = jnp.concatenate([jnp.zeros((HEAD_DIM,), F32), inv_freq, inv_freq, jnp.zeros((LANES - HEAD_DIM - ROPE_DIM,), F32)])
    return pos.astype(F32)[:, None], inv_row[None, :]


def _pad_lanes(v, n):
    return jnp.pad(v, ((0, 0), (0, n - v.shape[1])))


ATTN_BLK = 512
ATTN_FWD_BLK = 1024
ATTN_BWD_QBLK = 1024
ACC_ROWS = HEAD_DIM + 16


def local_step(xs, pos, tgt, gains, early_weights, late_weights, early_side=None, fwd_sides=(None, None), reduction=None):
    g_attn, b_forget, g_q, g_kv, g_fo, g_mo, g_mlp, g_fin = gains
    t = xs.shape[0]
    blk = min(ATTN_BLK, t)
    fox_scale = 1.0 / (HEAD_DIM ** 0.5)
    mla_scale = 1.0 / ((HEAD_DIM + ROPE_DIM) ** 0.5)

    h1, *gathered = rmsnorm(xs, g_attn, out_dtype=BF16, name="norm_attn", side=early_side)
    w_in_t, w_uq_p, w_ukv = early_weights(gathered)
    w_qkv_t, w_small_t = split_w_in_t(w_in_t)
    kv = w_ukv.reshape(KV_RANK, N_HEADS, 2 * HEAD_DIM)
    w_ukv_p = jnp.concatenate([_pad_heads(kv[:, :, :HEAD_DIM].reshape(KV_RANK, FW), HEAD_DIM),
                               kv[:, :, HEAD_DIM:].reshape(KV_RANK, FW)], axis=1)
    b_f = _pad_lanes(b_forget, LANES)
    pos_col, inv_row = rope_inputs(pos)

    qkv, qkv_t = mm(h1, w_qkv_t, trans_b=True, out_dtypes=[BF16], t_blk=blk, name="proj_qkv", bn=1536)
    small, = mm(h1, w_small_t, trans_b=True, out_dtypes=[F32], name="proj_small")
    mq, mk, mv, lf, cqn, ckvn, mq_t, mv_t = mla_prep(small, g_q, g_kv, w_uq_p, w_ukv_p, pos_col, inv_row, b_f,
                                                     name="mla_prep", bt=blk)
    f_cum = cumsum_rows(lf, reverse=False, name="gate_cumsum")
    f_blocks = f_cum[:, :N_HEADS].reshape(t // blk, blk, N_HEADS).transpose(0, 2, 1)
    fo, st_f, *gathered = flash_fwd(qkv_t, qkv, qkv_t, f_cum, qoff=0, koff=PAIRS, voff=2 * PAIRS, pair=True,
                                    scale=fox_scale, name="fox_fwd", blk=ATTN_FWD_BLK, side=fwd_sides[0])
    mo, st_m, *more = flash_fwd(mq_t, mk, mv_t, None, qoff=0, koff=0, voff=0, pair=False, scale=mla_scale, name="mla_fwd",
                                blk=ATTN_FWD_BLK, side=fwd_sides[1])
    w_o, w_up, w_down = late_weights(gathered + more)
    mixed = mix_norm(fo, mo, g_fo, g_mo, name="norm_mix")

    def inv_rms(v):
        return lax.rsqrt(jnp.mean(v * v, axis=-1, keepdims=True) + EPS)

    def residual_then_norm(acc, res, g):
        xn = acc + res
        return xn, xn * inv_rms(xn) * g

    def norm_bwd(dh, xn, res, g):
        r = inv_rms(xn)
        uu = dh * g
        return (r * uu - xn * (r * r * r * jnp.mean(uu * xn, axis=-1, keepdims=True)) + res,
                jnp.sum(dh * (xn * r), axis=0, keepdims=True))

    def norm_bwd2(dh, xn, res, g):
        dx, dg = norm_bwd(dh, xn, res, g)
        return dx, dx, dg

    def residual_then_loss(acc, res, target, g):
        xn = acc + res
        r = inv_rms(xn)
        xh = xn * r
        e = xh * g - target
        part = 0.5 * jnp.sum(jnp.mean(e * e, axis=-1, keepdims=True), axis=0, keepdims=True)
        dy = e * (1.0 / xn.shape[1])
        uu = dy * g
        dx = r * uu - xn * (r * r * r * jnp.mean(uu * xn, axis=-1, keepdims=True))
        return dx, dx, jnp.sum(dy * xh, axis=0, keepdims=True), part + jnp.zeros_like(g)

    x1, h2 = mm(mixed, w_o, extras=[xs], vecs=[g_mlp], epilogue=residual_then_norm, out_dtypes=[F32, BF16], name="out_proj")

    def relu2(uu):
        r = jnp.maximum(uu.astype(F32), 0.0)
        return (r * r).astype(BF16)

    u, = mm(h2, w_up, out_dtypes=[BF16], name="mlp_up", bn=2048)
    dx2, dx2b, dg_fin, loss_row = mm(u, w_down, a_pro=relu2, extras=[x1, tgt], vecs=[g_fin], epilogue=residual_then_loss,
                                     out_dtypes=[F32, BF16], n_sums=2, name="mlp_down_loss")
    loss = loss_row[:, :1]

    def relu2_grad(acc, uu):
        return (acc * (2.0 * jnp.maximum(uu.astype(F32), 0.0)),)

    du, = mm(dx2b, w_down, trans_b=True, extras=[u], epilogue=relu2_grad, out_dtypes=[BF16], name="mlp_down_bwd", bn=2048)
    dw_down, dw_down_b = (g.reshape(N_CHIPS, -1, w_down.shape[1])
                          for g in mm_tn(u, dx2b, a_pro=relu2, name="dw_down", bf16_copy=True))
    dx1, dx1b, dg_mlp = mm(du, w_up, trans_b=True, extras=[x1, dx2], vecs=[g_mlp], epilogue=norm_bwd2,
                           out_dtypes=[F32, BF16], n_sums=1, name="mlp_up_bwd")
    dw_up, dw_up_b = mm_tn(h2, du, name="dw_up", col_shards=N_CHIPS, bf16_copy=True)

    dw_o, dw_o_b = (g.reshape(N_CHIPS, -1, w_o.shape[1]) for g in mm_tn(mixed, dx1b, name="dw_o", bf16_copy=True))
    late, late_b = (dw_o, dw_up, dw_down), (dw_o_b, dw_up_b, dw_down_b)
    red = reduction
    dfo, dmo, dg_fo, dg_mo, st_f, st_m, dfo_t, dmo_t, *got = mix_norm_bwd(
        dx1b, w_o, fo, mo, g_fo, g_mo, st_f, st_m, name="out_proj_mix_bwd", bt=blk,
        side=red.late_swap(late, late_b) if red else None)
    dfq, dfk, dfv, d_f, *got = flash_bwd(
        qkv, qkv_t, qkv, qkv, dfo, dfo_t, st_f, f_blocks, qoff=0, koff=PAIRS, voff=2 * PAIRS, pair=True,
        scale=fox_scale, name="fox_bwd", qblk=ATTN_BWD_QBLK, side=red.late_scatter(got) if red else None)
    dmq, dmk, dmv, *got = flash_bwd(mq, mq_t, mk, mv, dmo, dmo_t, st_m, None, qoff=0, koff=0, voff=0,
                                    pair=False, scale=mla_scale, name="mla_bwd", qblk=ATTN_BWD_QBLK,
                                    side=red.late_halves(got) if red else None)
    if red:
        red.late_done(got)
    dlf = cumsum_rows(d_f, reverse=True, name="gate_cumsum_bwd")
    dsmall, dq_u, dkv_u, dg_q, dg_kv, db_f = mla_prep_bwd(dmq, dmk, dmv, dlf, small, g_q, g_kv, w_uq_p, w_ukv_p,
                                                          pos_col, inv_row, b_f, name="mla_prep_bwd")
    dw_uq_p = mm_tn(cqn, dq_u, name="dw_uq")
    dw_ukv_p = mm_tn(ckvn, dkv_u, name="dw_ukv")

    def to_bf16(tile):
        return tile.astype(BF16)

    dqkv = [dfq, dfk, dfv]
    dw_in_t = join_w_in_t([mm_tn(part, h1, a_pro=to_bf16, name="dw_" + nm) for part, nm in zip(dqkv, "qkv")],
                          mm_tn(dsmall, h1, name="dw_small"))
    dk_cols = _unpad_heads(dw_ukv_p[:, :HW], HEAD_DIM).reshape(KV_RANK, N_HEADS, HEAD_DIM)
    dv_cols = dw_ukv_p[:, HW:].reshape(KV_RANK, N_HEADS, HEAD_DIM)
    dw_ukv = jnp.concatenate([dk_cols, dv_cols], axis=2).reshape(KV_RANK, N_HEADS * 2 * HEAD_DIM)
    early = (dw_in_t, split_cols(dw_uq_p), split_cols(dw_ukv))
    w_parts = [w_qkv_t[n * FW:(n + 1) * FW] for n in range(3)]
    grad_x, dg_attn, *got = mm(dqkv + [dsmall], w_parts + [w_small_t], a_pro=to_bf16, extras=[xs, dx1], vecs=[g_attn],
                               epilogue=norm_bwd, out_dtypes=[F32], n_sums=1, name="proj_bwd",
                               side=red.early_scatter(early) if red else None)
    if red:
        red.early_done(got)
    d_gains = (dg_attn, db_f[:, :N_HEADS], dg_q, dg_kv, dg_fo, dg_mo, dg_mlp, dg_fin)
    return loss, grad_x, early, late, d_gains


class GradReduction:
    def __init__(self, core_id, chip):
        self.core_id, self.chip = core_id, chip
        self.core = core_id.reshape(1).astype(jnp.int32)
        self.grads, self.pairs, self.halves, self.others = {}, {}, {}, {}

    def _other_halves(self, grads):
        out = []
        for g in grads:
            axis = 1 + split_axis(g.shape[1])
            size = g.shape[axis] // 2
            out.append(lax.dynamic_slice_in_dim(g, (1 - self.core_id) * size, size, axis=axis).astype(BF16))
        return out

    def _add_pairs(self, group, recvs):
        self.pairs[group] = [add_pair(g, r, self.core, name="add_pair_%s_%d" % (group, n))
                             for n, (g, r) in enumerate(zip(self.grads[group], recvs))]
        return scatter_side(self.pairs[group])

    def _chip_sums(self, group, scattered):
        chip = self.chip
        with_mine = [lax.dynamic_update_index_in_dim(s, lax.dynamic_index_in_dim(p, chip, 0, keepdims=True), chip, 0)
                     for s, p in zip(scattered, self.pairs[group])]
        self.halves[group] = [sum_chips(s, name="sum_chips_%s_%d" % (group, n)) for n, s in enumerate(with_mine)]
        return self.halves[group]

    def late_swap(self, grads, bf16_copies):
        self.grads["late"] = list(grads)
        return swap_halves_side(list(bf16_copies))

    def late_scatter(self, recvs):
        return self._add_pairs("late", recvs)

    def late_halves(self, scattered):
        return swap_side(self._chip_sums("late", scattered))

    def late_done(self, others):
        self.others["late"] = list(others)

    def early_scatter(self, grads):
        self.grads["early"] = list(grads)
        return self._add_pairs("early", run_side(swap_side(self._other_halves(grads)), name="swap_early_sends"))

    def early_done(self, scattered):
        self.others["early"] = list(run_side(swap_side(self._chip_sums("early", scattered)), name="swap_early_halves"))

def kernel(x, positions, attn_norm_g, w_in, b_forget, q_norm_g, w_uq, kv_norm_g, w_ukv, fox_out_g, mla_out_g, w_o, mlp_norm_g, w_up, w_down, final_norm_g, loss_target, m_attn_norm_g, m_w_in, m_b_forget, m_q_norm_g, m_w_uq, m_kv_norm_g, m_w_ukv, m_fox_out_g, m_mla_out_g, m_w_o, m_mlp_norm_g, m_w_up, m_w_down, m_final_norm_g, v_attn_norm_g, v_w_in, v_b_forget, v_q_norm_g, v_w_uq, v_kv_norm_g, v_w_ukv, v_fox_out_g, v_mla_out_g, v_w_o, v_mlp_norm_g, v_w_up, v_w_down, v_final_norm_g):
    core_id = lax.axis_index("c")
    core = core_id.reshape(1).astype(jnp.int32)
    chip = 2 * lax.axis_index("x") + lax.axis_index("y")
    big = [w_in, w_uq, w_ukv, w_o, w_up, w_down]
    big_m = [m_w_in, m_w_uq, m_w_ukv, m_w_o, m_w_up, m_w_down]
    big_v = [v_w_in, v_w_uq, v_w_ukv, v_w_o, v_w_up, v_w_down]
    n_early = 3

    def vec(a):
        return a.reshape(1, -1)

    small = [attn_norm_g, b_forget, q_norm_g, kv_norm_g, fox_out_g, mla_out_g, mlp_norm_g, final_norm_g]
    small_m = [m_attn_norm_g, m_b_forget, m_q_norm_g, m_kv_norm_g, m_fox_out_g, m_mla_out_g, m_mlp_norm_g, m_final_norm_g]
    small_v = [v_attn_norm_g, v_b_forget, v_q_norm_g, v_kv_norm_g, v_fox_out_g, v_mla_out_g, v_mlp_norm_g, v_final_norm_g]
    gains = [vec(a) for a in small]

    views = [big[0][0].T, _pad_heads(big[1][0], HEAD_DIM + ROPE_DIM)] + [w[0] for w in big[2:]]
    shards = [v.astype(BF16) for v in views]

    def with_own(gathered, mine):
        return [lax.dynamic_update_index_in_dim(g, s, chip, 0) for g, s in zip(gathered, mine)]

    def early_weights(gathered):
        g_in, g_uq, g_ukv = with_own(gathered, shards[:n_early])
        return g_in, join_cols(g_uq), join_cols(g_ukv)

    def late_weights(gathered):
        g_o, g_up, g_down = with_own(gathered, shards[n_early:])
        return g_o.reshape(-1, g_o.shape[2]), join_cols(g_up), g_down.reshape(-1, g_down.shape[2])

    reduction = GradReduction(core_id, chip)
    loss, grad_x, _, _, d_small = local_step(
        x[0], positions[0], loss_target[0], gains, early_weights, late_weights, gather_side(shards[:n_early]),
        (gather_side(shards[n_early:-1]), gather_side(shards[-1:])), reduction)
    halves = reduction.halves["early"] + reduction.halves["late"]
    others = reduction.others["early"] + reduction.others["late"]

    def rows8(vs):
        return jnp.concatenate([_pad_lanes(vec(a).astype(F32), 1024) for a in vs], axis=0)

    with_loss = [jnp.concatenate([d, loss], axis=1) if n == 1 else d for n, d in enumerate(d_small)]
    g_small8 = allreduce_small(rows8(with_loss))

    outs_big = []
    for n, (w, gm, go, m, v) in enumerate(zip(big, halves, others, big_m, big_v)):
        if n == 0:
            outs = adamw_halves_t(*(jnp.transpose(a, (2, 0, 1)) for a in (w, m, v)), gm, go, core, name="adamw_%d" % n)
            outs_big.append([jnp.transpose(o, (1, 2, 0)) for o in outs])
        elif n == 1:
            gm, go = (_unpad_heads(gh, HEAD_DIM + ROPE_DIM) for gh in (gm, go))
            g_t = jnp.where(core[0] == 0, jnp.concatenate([gm, go], axis=0), jnp.concatenate([go, gm], axis=0)).T
            outs = (g_t,) + tuple(adamw_whole(w[0].T, g_t, m[0].T, v[0].T, name="adamw_%d" % n))
            outs_big.append([o.T[None] for o in outs])
        else:
            outs_big.append(adamw_halves(w, gm, go, m, v, core, name="adamw_%d" % n))
    outs_small = adamw_vectors(gains, g_small8, [vec(a) for a in small_m], [vec(a) for a in small_v], name="adamw_small")

    loss_all = g_small8[1, N_HEADS]
    grads, deltas, new_m, new_v = [None] * 14, [None] * 14, [None] * 14, [None] * 14
    big_at = [1, 4, 6, 9, 11, 12]
    small_at = [0, 2, 3, 5, 7, 8, 10, 13]
    for n, at in enumerate(big_at):
        grads[at], deltas[at], new_m[at], new_v[at] = outs_big[n]
    for at, s, outs in zip(small_at, small, outs_small):
        grads[at], deltas[at], new_m[at], new_v[at] = (o.reshape(s.shape) for o in outs)
    return (loss_all, grad_x[None], *grads, *deltas, *new_m, *new_v)
```

```python
import jax
import jax.numpy as jnp
from jax import lax
from jax.experimental import pallas as pl
from jax.experimental.pallas import tpu as pltpu

F32 = jnp.float32
BF16 = jnp.bfloat16
MESH = pl.DeviceIdType.MESH

EPS = 1e-6
ROPE_THETA = 10000.0
N_HEADS = 8
PAIRS = N_HEADS // 2
HEAD_DIM = 64
ROPE_DIM = 32
LANES = 128
Q_RANK = 384
KV_RANK = 256
N_CHIPS = 4
ADAM_LR, ADAM_B1, ADAM_B2, ADAM_EPS, ADAM_WD, ADAM_STEP = 0.001, 0.9, 0.999, 1e-08, 0.01, 10
VMEM_LIMIT = 48 * 1024 * 1024
ADAMW_BLOCK = 256 * 1024
LOG2E = 1.4426950408889634
LN2 = 0.6931471805599453
NN = (((1,), (0,)), ((), ()))
NT = (((1,), (1,)), ((), ()))
TN = (((0,), (0,)), ((), ()))


def _params(sem=None):
    return pltpu.CompilerParams(dimension_semantics=sem, vmem_limit_bytes=VMEM_LIMIT)


def _fit(block, dim):
    if dim <= block:
        return dim
    return next(b for b in range(block - block % LANES, 0, -LANES) if dim % b == 0)


def _row_block(rows):
    return next(b for b in (256, 128, 64, 32, 16, 8) if rows % b == 0)


def gridded(body, *, name, grid, in_specs, out_specs, out_shape, ins, semantics, side=None):
    if side is None:
        return pl.pallas_call(body, name=name, grid=grid, in_specs=in_specs, out_specs=out_specs, out_shape=out_shape,
                              compiler_params=_params(semantics))(*ins)
    n_in, n_out = len(in_specs), len(out_specs)
    steps = 1
    for extent in grid:
        steps *= extent

    def riding(*refs):
        main_in, s_in, main_out, s_out, sems = _split_refs(refs, [n_in, len(side.ins), n_out, len(side.out_shapes), 2])
        step = 0
        for axis, extent in enumerate(grid):
            step = step * extent + pl.program_id(axis)

        @pl.when(step == 0)
        def _():
            side.first(s_in, s_out, *sems)

        body(*main_in, *main_out)

        @pl.when(step == steps - 1)
        def _():
            if side.mid is not None:
                side.mid(s_in, s_out, *sems)
            side.last(s_in, s_out, *sems)

    s_in_specs, s_out_specs = side.specs()
    return pl.pallas_call(
        riding, name=name, grid=grid, in_specs=list(in_specs) + s_in_specs, out_specs=list(out_specs) + s_out_specs,
        out_shape=list(out_shape) + side.out_shapes, scratch_shapes=side.sems(),
        compiler_params=_params(("arbitrary",) * len(grid)))(*ins, *side.ins)


def rmsnorm(x, g, *, out_dtype, name, bt=512, side=None):
    t, d = x.shape
    bt = min(bt, t)

    def body(x_ref, g_ref, o_ref):
        xv = x_ref[...].astype(F32)
        r = lax.rsqrt(jnp.mean(xv * xv, axis=-1, keepdims=True) + EPS)
        o_ref[...] = (xv * r * g_ref[...]).astype(o_ref.dtype)

    return gridded(
        body, name=name, grid=(t // bt,),
        in_specs=[pl.BlockSpec((bt, d), lambda i: (i, 0)), pl.BlockSpec((1, d), lambda i: (0, 0))],
        out_specs=[pl.BlockSpec((bt, d), lambda i: (i, 0))],
        out_shape=[jax.ShapeDtypeStruct((t, d), out_dtype)],
        ins=[x, g], semantics=("parallel",), side=side)


def mm(a, b, *, trans_b=False, a_pro=None, extras=(), vecs=(), epilogue=None, out_dtypes, n_sums=0, t_blk=None, name,
       bm=1024, bn=1024, side=None):
    a_list = list(a) if isinstance(a, (list, tuple)) else [a]
    b_list = list(b) if isinstance(b, (list, tuple)) else [b]
    m = a_list[0].shape[0]
    n = b_list[0].shape[0] if trans_b else b_list[0].shape[1]
    ks = [x.shape[1] for x in a_list]
    if sum(ks) > 2048:
        bm = bm // 2
    bm, bn = _fit(bm, m), _fit(bn, n)
    assert n_sums == 0 or bn == n
    n_ab, n_ex, n_vec, n_out = len(a_list), len(extras), len(vecs), len(out_dtypes)
    n_t = 0 if t_blk is None else 1

    def body(*refs):
        a_refs, b_refs, ex, vs, outs, t_outs, sums = _split_refs(refs, [n_ab, n_ab, n_ex, n_vec, n_out, n_t, n_sums])
        acc = None
        for a_ref, b_ref in zip(a_refs, b_refs):
            a_tile = a_ref[...] if a_pro is None else a_pro(a_ref[...])
            part = lax.dot_general(a_tile, b_ref[...], NT if trans_b else NN, preferred_element_type=F32)
            acc = part if acc is None else acc + part
        res = epilogue(acc, *[e[...] for e in ex], *[v[...] for v in vs]) if epilogue is not None else (acc,)
        for o, r in zip(outs, res[:n_out]):
            o[...] = r.astype(o.dtype)
        for t_ref in t_outs:
            for u in range(bm // t_blk):
                t_ref[u] = res[0][u * t_blk:(u + 1) * t_blk, :].T.astype(t_ref.dtype)
        if n_sums:
            @pl.when(pl.program_id(0) == 0)
            def _():
                for s_ref in sums:
                    s_ref[...] = jnp.zeros_like(s_ref)

            for s_ref, r in zip(sums, res[n_out:]):
                s_ref[...] += r

    tile = pl.BlockSpec((bm, bn), lambda i, j: (i, j))
    vec = pl.BlockSpec((1, bn), lambda i, j: (0, j))
    t_specs, t_shapes = [], []
    if t_blk is not None:
        t_specs = [pl.BlockSpec((bm // t_blk, bn, t_blk), lambda i, j: (i, j, 0))]
        t_shapes = [jax.ShapeDtypeStruct((m // t_blk, n, t_blk), out_dtypes[0])]
    a_specs = [pl.BlockSpec((bm, k), lambda i, j: (i, 0)) for k in ks]
    b_specs = [pl.BlockSpec((bn, k), lambda i, j: (j, 0)) if trans_b else pl.BlockSpec((k, bn), lambda i, j: (0, j)) for k in ks]
    return gridded(
        body, name=name, grid=(m // bm, n // bn),
        in_specs=a_specs + b_specs + [tile] * n_ex + [vec] * n_vec,
        out_specs=[tile] * n_out + t_specs + [vec] * n_sums,
        out_shape=[jax.ShapeDtypeStruct((m, n), dt) for dt in out_dtypes] + t_shapes + [jax.ShapeDtypeStruct((1, n), F32)] * n_sums,
        ins=[*a_list, *b_list, *extras, *vecs],
        semantics=("arbitrary", "arbitrary") if n_sums else ("parallel", "parallel"), side=side)


def mm_tn(a, b, *, a_pro=None, name, col_shards=1, bf16_copy=False, bk=1024, bn=1024, bt=2048):
    t, k = a.shape
    n = b.shape[1]
    ns = n // col_shards
    bk, bn, bt = _fit(bk, k), _fit(bn, ns), _fit(bt, t)
    per = ns // bn
    last = t // bt - 1

    def body(a_ref, b_ref, o_ref, *copy_ref):
        @pl.when(pl.program_id(2) == 0)
        def _():
            o_ref[...] = jnp.zeros_like(o_ref)

        a_tile = a_ref[...] if a_pro is None else a_pro(a_ref[...])
        o_ref[...] += lax.dot_general(a_tile, b_ref[...], TN, preferred_element_type=F32)
        if bf16_copy:
            @pl.when(pl.program_id(2) == last)
            def _():
                copy_ref[0][...] = o_ref[...].astype(BF16)

    if col_shards == 1:
        out_spec = pl.BlockSpec((bk, bn), lambda i, j, s: (i, j))
        shape = (k, n)
    else:
        out_spec = pl.BlockSpec((None, bk, bn), lambda i, j, s: (j // per, i, j % per))
        shape = (col_shards, k, ns)
    out_specs, out_shape = out_spec, jax.ShapeDtypeStruct(shape, F32)
    if bf16_copy:
        out_specs, out_shape = [out_spec, out_spec], [out_shape, jax.ShapeDtypeStruct(shape, BF16)]
    return pl.pallas_call(
        body, name=name, grid=(k // bk, n // bn, t // bt),
        in_specs=[pl.BlockSpec((bt, bk), lambda i, j, s: (s, i)), pl.BlockSpec((bt, bn), lambda i, j, s: (s, j))],
        out_specs=out_specs, out_shape=out_shape,
        compiler_params=_params(("parallel", "parallel", "arbitrary")),
    )(a, b)


def _split3(x):
    hi = x.astype(BF16)
    r1 = x - hi.astype(F32)
    mid = r1.astype(BF16)
    lo = (r1 - mid.astype(F32)).astype(BF16)
    return hi, mid, lo


def cumsum_rows(x, *, reverse, name, bc=512):
    t, d = x.shape
    bc = min(bc, t)
    nb = t // bc

    def body(x_ref, o_ref, carry):
        @pl.when(pl.program_id(0) == 0)
        def _():
            carry[...] = jnp.zeros_like(carry)

        r = lax.broadcasted_iota(jnp.int32, (bc, bc), 0)
        c = lax.broadcasted_iota(jnp.int32, (bc, bc), 1)
        tri = jnp.where((r <= c) if reverse else (r >= c), 1.0, 0.0).astype(BF16)
        hi, mid, lo = _split3(x_ref[...])
        s = (lax.dot_general(tri, hi, NN, preferred_element_type=F32)
             + lax.dot_general(tri, mid, NN, preferred_element_type=F32)
             + lax.dot_general(tri, lo, NN, preferred_element_type=F32)) + carry[0:1, :]
        o_ref[...] = s
        carry[0:1, :] = s[0:1, :] if reverse else s[bc - 1:bc, :]

    imap = (lambda i: (nb - 1 - i, 0)) if reverse else (lambda i: (i, 0))
    return pl.pallas_call(
        body, name=name, grid=(nb,),
        in_specs=[pl.BlockSpec((bc, d), imap)], out_specs=pl.BlockSpec((bc, d), imap),
        out_shape=jax.ShapeDtypeStruct((t, d), F32),
        scratch_shapes=[pltpu.VMEM((8, d), F32)],
        compiler_params=_params(("arbitrary",)),
    )(x)


def _rope(x, c, a, b):
    return x * c + pltpu.roll(x, LANES - ROPE_DIM // 2, 1) * a + pltpu.roll(x, ROPE_DIM // 2, 1) * b


def _rope_bwd(d, c, a, b):
    return d * c + pltpu.roll(d * a, ROPE_DIM // 2, 1) + pltpu.roll(d * b, LANES - ROPE_DIM // 2, 1)


S_CQ, S_CKV, S_KR, S_F, S_END = 0, Q_RANK, Q_RANK + KV_RANK, Q_RANK + KV_RANK + LANES, 1024
HW = N_HEADS * LANES
FW = N_HEADS * HEAD_DIM


def _rope_tables(pos_col, inv_row):
    ang = pos_col * inv_row
    cos, sin = jnp.cos(ang), jnp.sin(ang)
    lane = lax.broadcasted_iota(jnp.int32, (1, LANES), 1)
    first = (lane >= HEAD_DIM) & (lane < HEAD_DIM + ROPE_DIM // 2)
    second = (lane >= HEAD_DIM + ROPE_DIM // 2) & (lane < HEAD_DIM + ROPE_DIM)
    return (jnp.where(lane < HEAD_DIM, 1.0, jnp.where(first | second, cos, 0.0)), jnp.where(first, -sin, 0.0),
            jnp.where(second, sin, 0.0))


def mla_prep(small, g_q, g_kv, w_uq, w_ukv, pos_col, inv_row, b_f, *, name, bt=512):
    t = small.shape[0]
    bt = min(bt, t)

    def body(s_ref, gq_ref, gkv_ref, wq_ref, wkv_ref, pos_ref, inv_ref, bf_ref,
             mq_ref, mk_ref, mv_ref, lf_ref, cqn_ref, ckvn_ref, mqt_ref, mvt_ref):
        cq = s_ref[:, S_CQ:S_CKV]
        rq = lax.rsqrt(jnp.mean(cq * cq, axis=-1, keepdims=True) + EPS)
        cqn = (cq * rq * gq_ref[...]).astype(BF16)
        ckv = s_ref[:, S_CKV:S_KR]
        rkv = lax.rsqrt(jnp.mean(ckv * ckv, axis=-1, keepdims=True) + EPS)
        ckvn = (ckv * rkv * gkv_ref[...]).astype(BF16)
        cqn_ref[...] = cqn
        ckvn_ref[...] = ckvn
        tc, ta, tb = _rope_tables(pos_ref[...], inv_ref[...])
        q = jnp.dot(cqn, wq_ref[...], preferred_element_type=F32)
        kv = jnp.dot(ckvn, wkv_ref[...], preferred_element_type=F32)
        kr = _rope(s_ref[:, S_KR:S_F], tc, ta, tb)
        for h in range(N_HEADS):
            sl = slice(h * LANES, (h + 1) * LANES)
            roped = _rope(q[:, sl], tc, ta, tb)
            mq_ref[:, sl] = roped.astype(BF16)
            mqt_ref[0, sl, :] = roped.T.astype(BF16)
            mk_ref[:, sl] = (kv[:, sl] + kr).astype(BF16)
        mv_ref[...] = kv[:, HW:].astype(BF16)
        mvt_ref[0] = kv[:, HW:].T.astype(BF16)
        z = s_ref[:, S_F:S_END - LANES] + bf_ref[...]
        lf_ref[...] = jnp.minimum(z, 0.0) - jnp.log(1.0 + jnp.exp(-jnp.abs(z)))

    def row(w):
        return pl.BlockSpec((bt, w), lambda i: (i, 0))

    def full(arr):
        return pl.BlockSpec(arr.shape, lambda i: (0, 0))

    return pl.pallas_call(
        body, name=name, grid=(t // bt,),
        in_specs=[row(S_END), full(g_q), full(g_kv), full(w_uq), full(w_ukv), row(1), full(inv_row), full(b_f)],
        out_specs=[row(HW), row(HW), row(FW), row(LANES), row(Q_RANK), row(KV_RANK),
                   pl.BlockSpec((1, HW, bt), lambda i: (i, 0, 0)), pl.BlockSpec((1, FW, bt), lambda i: (i, 0, 0))],
        out_shape=[jax.ShapeDtypeStruct((t, HW), BF16)] * 2 + [jax.ShapeDtypeStruct((t, FW), BF16), jax.ShapeDtypeStruct((t, LANES), F32),
                   jax.ShapeDtypeStruct((t, Q_RANK), BF16), jax.ShapeDtypeStruct((t, KV_RANK), BF16),
                   jax.ShapeDtypeStruct((t // bt, HW, bt), BF16), jax.ShapeDtypeStruct((t // bt, FW, bt), BF16)],
        compiler_params=_params(("parallel",)),
    )(small, g_q, g_kv, w_uq, w_ukv, pos_col, inv_row, b_f)


def mla_prep_bwd(dmq, dmk, dmv, dlf, small, g_q, g_kv, w_uq, w_ukv, pos_col, inv_row, b_f, *, name, bt=512):
    t = small.shape[0]
    bt = min(bt, t)

    def body(dmq_ref, dmk_ref, dmv_ref, dlf_ref, s_ref, gq_ref, gkv_ref, wq_ref, wkv_ref, pos_ref, inv_ref, bf_ref,
             ds_ref, dq_ref, dkv_ref, dgq_ref, dgkv_ref, db_ref):
        tc, ta, tb = _rope_tables(pos_ref[...], inv_ref[...])
        lane = lax.broadcasted_iota(jnp.int32, (1, LANES), 1)
        dkr = jnp.zeros((bt, LANES), F32)
        for h in range(N_HEADS):
            sl = slice(h * LANES, (h + 1) * LANES)
            dq_ref[:, sl] = _rope_bwd(dmq_ref[:, sl], tc, ta, tb).astype(BF16)
            dkr = dkr + dmk_ref[:, sl]
        dkv_ref[:, :HW] = dmk_ref[...].astype(BF16)
        dkv_ref[:, HW:] = dmv_ref[...].astype(BF16)
        in_rope = (lane >= HEAD_DIM) & (lane < HEAD_DIM + ROPE_DIM)
        ds_ref[:, S_KR:S_F] = jnp.where(in_rope, _rope_bwd(dkr, tc, ta, tb), 0.0).astype(BF16)

        def norm_bwd(raw, g_ref, dn, dg_ref):
            r = lax.rsqrt(jnp.mean(raw * raw, axis=-1, keepdims=True) + EPS)
            u = dn * g_ref[...]
            dot = jnp.mean(u * raw, axis=-1, keepdims=True)
            dg_ref[...] += jnp.sum(dn * (raw * r), axis=0, keepdims=True)
            return r * u - raw * (r * r * r * dot)

        @pl.when(pl.program_id(0) == 0)
        def _():
            dgq_ref[...] = jnp.zeros_like(dgq_ref)
            dgkv_ref[...] = jnp.zeros_like(dgkv_ref)
            db_ref[...] = jnp.zeros_like(db_ref)

        dcqn = lax.dot_general(dq_ref[...], wq_ref[...], NT, preferred_element_type=F32)
        ds_ref[:, S_CQ:S_CKV] = norm_bwd(s_ref[:, S_CQ:S_CKV], gq_ref, dcqn, dgq_ref).astype(BF16)
        dckvn = lax.dot_general(dkv_ref[...], wkv_ref[...], NT, preferred_element_type=F32)
        ds_ref[:, S_CKV:S_KR] = norm_bwd(s_ref[:, S_CKV:S_KR], gkv_ref, dckvn, dgkv_ref).astype(BF16)
        z = s_ref[:, S_F:S_END - LANES] + bf_ref[...]
        dz = jnp.where(lane < N_HEADS, dlf_ref[...] / (1.0 + jnp.exp(z)), 0.0)
        db_ref[...] += jnp.sum(dz, axis=0, keepdims=True)
        ds_ref[:, S_F:S_END - LANES] = dz.astype(BF16)
        ds_ref[:, S_END - LANES:] = jnp.zeros((bt, LANES), BF16)

    def row(w):
        return pl.BlockSpec((bt, w), lambda i: (i, 0))

    def full(arr):
        return pl.BlockSpec(arr.shape, lambda i: (0, 0))

    def vec(w):
        return pl.BlockSpec((1, w), lambda i: (0, 0))

    return pl.pallas_call(
        body, name=name, grid=(t // bt,),
        in_specs=[row(HW), row(HW), row(FW), row(LANES), row(S_END), full(g_q), full(g_kv), full(w_uq), full(w_ukv),
                  row(1), full(inv_row), full(b_f)],
        out_specs=[row(S_END), row(HW), row(HW + FW), vec(Q_RANK), vec(KV_RANK), vec(LANES)],
        out_shape=[jax.ShapeDtypeStruct((t, S_END), BF16), jax.ShapeDtypeStruct((t, HW), BF16),
                   jax.ShapeDtypeStruct((t, HW + FW), BF16), jax.ShapeDtypeStruct((1, Q_RANK), F32),
                   jax.ShapeDtypeStruct((1, KV_RANK), F32), jax.ShapeDtypeStruct((1, LANES), F32)],
        compiler_params=_params(("arbitrary",)),
    )(dmq, dmk, dmv, dlf, small, g_q, g_kv, w_uq, w_ukv, pos_col, inv_row, b_f)


class Side:
    def __init__(self, ins, out_shapes, n_sems, first, last, mid=None):
        self.ins, self.out_shapes, self.n_sems = list(ins), list(out_shapes), n_sems
        self.first, self.mid, self.last = first, mid, last

    def specs(self):
        return [ANY] * len(self.ins), [ANY] * len(self.out_shapes)

    def sems(self):
        return [pltpu.SemaphoreType.DMA((self.n_sems,)), pltpu.SemaphoreType.DMA((self.n_sems,))]


def _lane():
    return lax.broadcasted_iota(jnp.int32, (1, LANES), 1)


def _halves(x):
    zero = jnp.zeros_like(x)
    return [jnp.where(_lane() < HEAD_DIM, x, zero), jnp.where(_lane() >= HEAD_DIM, x, zero)]


def _groups(x):
    return [x[:, :LANES], x[:, LANES:]]


def _pick_row(tile, h):
    row = lax.broadcasted_iota(jnp.int32, (tile.shape[0], 1), 0)
    return jnp.sum(jnp.where(row == h, tile, 0.0), axis=0, keepdims=True)


def _pick_lane(tile, h):
    return jnp.sum(jnp.where(_lane() == h, tile, 0.0), axis=1, keepdims=True)


def _row_halves(x):
    row = lax.broadcasted_iota(jnp.int32, (LANES, 1), 0)
    zero = jnp.zeros_like(x)
    return [jnp.where(row < HEAD_DIM, x, zero), jnp.where(row >= HEAD_DIM, x, zero)]


def _below_diagonal(s):
    r = lax.broadcasted_iota(jnp.int32, s.shape, 0)
    c = lax.broadcasted_iota(jnp.int32, s.shape, 1)
    return jnp.where(c <= r, s, -jnp.inf)


def _above_diagonal(s):
    r = lax.broadcasted_iota(jnp.int32, s.shape, 0)
    c = lax.broadcasted_iota(jnp.int32, s.shape, 1)
    return jnp.where(r <= c, s, -jnp.inf)


def _split_refs(refs, counts):
    out, at = [], 0
    for n in counts:
        out.append(refs[at:at + n])
        at += n
    return out


def flash_fwd(qt_arr, k_arr, vt_arr, f_cum, *, qoff, koff, voff, pair, scale, name, blk=512, side=None):
    t = k_arr.shape[0]
    tblk = qt_arr.shape[2]
    blk = max(min(blk, t), tblk)
    sub = blk // tblk
    nb = t // blk
    steps = PAIRS * nb
    w = LANES if pair else 2 * LANES
    has_bias = f_cum is not None
    ins = [qt_arr, k_arr, vt_arr] + ([f_cum] if has_bias else [])

    def wide(ref, first):
        parts = [ref[first + u] for u in range(sub)]
        return parts[0] if sub == 1 else jnp.concatenate(parts, axis=1)
    s_ins, s_outs = (side.ins, side.out_shapes) if side else ([], [])

    def body(*refs):
        main, si, outs, so, sems = _split_refs(refs, [len(ins), len(s_ins), 2, len(s_outs), 2 if side else 0])
        qt_ref, k_ref, vt_ref = main[:3]
        f_ref = main[3] if has_bias else None
        o_ref, st_ref = outs
        g, i = pl.program_id(0), pl.program_id(1)
        step_id = g * nb + i
        if side:
            @pl.when(step_id == 0)
            def _():
                side.first(si, so, *sems)

            if side.mid is not None:
                @pl.when(step_id == (3 * steps) // 4)
                def _():
                    side.mid(si, so, *sems)

        qt = (wide(qt_ref, 0).astype(F32) * (scale * LOG2E)).astype(BF16)
        qts = _row_halves(qt) if pair else [qt[:LANES], qt[LANES:]]

        def k_of(kk, n):
            return kk if pair else kk[:, n * LANES:(n + 1) * LANES]

        def with_ones(vt_rows):
            return jnp.concatenate([vt_rows, jnp.ones((ACC_ROWS - HEAD_DIM, vt_rows.shape[1]), BF16)], axis=0)

        def step(j, carry, diagonal):
            rows = pl.ds(pl.multiple_of(j * blk, blk), blk)
            kk = k_ref[rows, :]
            vt = wide(vt_ref, sub * j)
            out = []
            for n in range(2):
                m, acc = carry[n]
                s = jnp.dot(k_of(kk, n), qts[n], preferred_element_type=F32)
                if has_bias:
                    s = s - LOG2E * _pick_lane(f_ref[rows, :], 2 * g + n)
                if diagonal:
                    s = _above_diagonal(s)
                m_new = jnp.maximum(m, jnp.max(s, axis=0, keepdims=True))
                p = jnp.exp2(s - m_new).astype(BF16)
                out.append((m_new, jnp.exp2(m - m_new) * acc
                            + jnp.dot(with_ones(vt[n * HEAD_DIM:(n + 1) * HEAD_DIM]), p, preferred_element_type=F32)))
            return tuple(out)

        def diagonal_in_halves(carry):
            h = tblk
            halves = [pl.ds(pl.multiple_of(i * blk, blk), h), pl.ds(pl.multiple_of(i * blk + h, h), h)]
            k_top, k_bot = k_ref[halves[0], :], k_ref[halves[1], :]
            vt_top, vt_bot = vt_ref[sub * i], vt_ref[sub * i + 1]
            out = []
            for n in range(2):
                m, acc = carry[n]
                heads = slice(n * HEAD_DIM, (n + 1) * HEAD_DIM)
                s_top = jnp.dot(k_of(k_top, n), qts[n], preferred_element_type=F32)
                s_bot = jnp.dot(k_of(k_bot, n), qts[n][:, h:], preferred_element_type=F32)
                if has_bias:
                    s_top = s_top - LOG2E * _pick_lane(f_ref[halves[0], :], 2 * g + n)
                    s_bot = s_bot - LOG2E * _pick_lane(f_ref[halves[1], :], 2 * g + n)
                s_top, s_bot = _above_diagonal(s_top), _above_diagonal(s_bot)
                m_top = jnp.maximum(m, jnp.max(s_top, axis=0, keepdims=True))
                m_new = jnp.concatenate([m_top[:, :h], jnp.maximum(m_top[:, h:], jnp.max(s_bot, axis=0, keepdims=True))], axis=1)
                p_top = jnp.exp2(s_top - m_new).astype(BF16)
                p_bot = jnp.exp2(s_bot - m_new[:, h:]).astype(BF16)
                late = jnp.concatenate([jnp.zeros((ACC_ROWS, h), F32),
                                        jnp.dot(with_ones(vt_bot[heads]), p_bot, preferred_element_type=F32)], axis=1)
                out.append((m_new, jnp.exp2(m - m_new) * acc
                            + jnp.dot(with_ones(vt_top[heads]), p_top, preferred_element_type=F32) + late))
            return tuple(out)

        init = tuple((jnp.full((1, blk), -jnp.inf, F32), jnp.zeros((ACC_ROWS, blk), F32)) for _ in range(2))
        carry = lax.fori_loop(0, i, lambda j, c: step(j, c, False), init)
        (ma, acca), (mb, accb) = diagonal_in_halves(carry) if sub == 2 else step(i, carry, True)
        la, lb = acca[HEAD_DIM:HEAD_DIM + 1], accb[HEAD_DIM:HEAD_DIM + 1]
        o_ref[...] = jnp.concatenate([acca[:HEAD_DIM] / la, accb[:HEAD_DIM] / lb], axis=0).T
        row = lax.broadcasted_iota(jnp.int32, (LANES, 1), 0)
        st_ref[0] = jnp.where(row == 0, ma + jnp.log2(la), jnp.where(row == 1, mb + jnp.log2(lb), 0.0)).T
        if side:
            @pl.when(step_id == steps - 1)
            def _():
                side.last(si, so, *sems)

    in_specs = [pl.BlockSpec((sub, w, tblk), lambda g, i: (i, qoff + g, 0)), pl.BlockSpec((t, w), lambda g, i: (0, koff + g)),
                pl.BlockSpec((t // tblk, LANES, tblk), lambda g, i: (0, voff + g, 0))]
    if has_bias:
        in_specs.append(pl.BlockSpec((t, LANES), lambda g, i: (0, 0)))
    s_in_specs, s_out_specs = side.specs() if side else ([], [])
    return pl.pallas_call(
        body, name=name, grid=(PAIRS, nb), in_specs=in_specs + s_in_specs,
        out_specs=[pl.BlockSpec((blk, LANES), lambda g, i: (i, g)), pl.BlockSpec((1, blk, LANES), lambda g, i: (g, i, 0))]
        + s_out_specs,
        out_shape=[jax.ShapeDtypeStruct((t, PAIRS * LANES), F32), jax.ShapeDtypeStruct((PAIRS, t, LANES), F32)] + list(s_outs),
        scratch_shapes=side.sems() if side else [],
        compiler_params=_params(("arbitrary", "arbitrary")),
    )(*ins, *s_ins)


def mix_norm(fo, mo, g_fo, g_mo, *, name, bt=512):
    t, d = fo.shape
    bt = min(bt, t)

    def body(fo_ref, mo_ref, gf_ref, gm_ref, o_ref):
        for n, (x_ref, g_ref) in enumerate(((fo_ref, gf_ref), (mo_ref, gm_ref))):
            xv = x_ref[...]
            r = lax.rsqrt(jnp.mean(xv * xv, axis=-1, keepdims=True) + EPS)
            o_ref[:, n * d:(n + 1) * d] = (xv * r * g_ref[...]).astype(BF16)

    row = pl.BlockSpec((bt, d), lambda i: (i, 0))
    vec = pl.BlockSpec((1, d), lambda i: (0, 0))
    return pl.pallas_call(
        body, name=name, grid=(t // bt,), in_specs=[row, row, vec, vec],
        out_specs=pl.BlockSpec((bt, 2 * d), lambda i: (i, 0)),
        out_shape=jax.ShapeDtypeStruct((t, 2 * d), BF16),
        compiler_params=_params(("parallel",)),
    )(fo, mo, g_fo, g_mo)


def mix_norm_bwd(dx1b, w_o, fo, mo, g_fo, g_mo, st_f, st_m, *, name, bt=512, side=None):
    t, d = fo.shape
    bt = min(bt, t)

    def body(dx_in_ref, w_ref, fo_ref, mo_ref, gf_ref, gm_ref, sf_ref, sm_ref, dfo_ref, dmo_ref, dgf_ref, dgm_ref,
             sfo_ref, smo_ref, dfot_ref, dmot_ref):
        @pl.when(pl.program_id(0) == 0)
        def _():
            dgf_ref[...] = jnp.zeros_like(dgf_ref)
            dgm_ref[...] = jnp.zeros_like(dgm_ref)

        dmixed = lax.dot_general(dx_in_ref[...], w_ref[...], NT, preferred_element_type=F32)
        groups = ((fo_ref, gf_ref, dfo_ref, dgf_ref, sf_ref, sfo_ref, dfot_ref),
                  (mo_ref, gm_ref, dmo_ref, dgm_ref, sm_ref, smo_ref, dmot_ref))
        for n, (x_ref, g_ref, dx_ref, dg_ref, st_ref, sto_ref, dxt_ref) in enumerate(groups):
            xv = x_ref[...]
            dhv = dmixed[:, n * d:(n + 1) * d]
            r = lax.rsqrt(jnp.mean(xv * xv, axis=-1, keepdims=True) + EPS)
            u = dhv * g_ref[...]
            dxf = r * u - xv * (r * r * r * jnp.mean(u * xv, axis=-1, keepdims=True))
            dxb = dxf.astype(BF16)
            dx_ref[...] = dxb
            dxt_ref[0] = dxf.T.astype(BF16)
            dg_ref[...] += jnp.sum(dhv * (xv * r), axis=0, keepdims=True)
            prod = xv * dxb.astype(F32)
            for g in range(PAIRS):
                grp = prod[:, g * LANES:(g + 1) * LANES]
                da = jnp.sum(jnp.where(_lane() < HEAD_DIM, grp, 0.0), axis=1, keepdims=True)
                db = jnp.sum(jnp.where(_lane() >= HEAD_DIM, grp, 0.0), axis=1, keepdims=True)
                sto_ref[g] = jnp.where(_lane() == 2, da, jnp.where(_lane() == 3, db, st_ref[g]))

    row = pl.BlockSpec((bt, d), lambda i: (i, 0))
    vec = pl.BlockSpec((1, d), lambda i: (0, 0))
    stat = pl.BlockSpec((PAIRS, bt, LANES), lambda i: (0, i, 0))
    return gridded(
        body, name=name, grid=(t // bt,),
        in_specs=[pl.BlockSpec((bt, dx1b.shape[1]), lambda i: (i, 0)), pl.BlockSpec(w_o.shape, lambda i: (0, 0)),
                  row, row, vec, vec, stat, stat],
        out_specs=[row, row, vec, vec, stat, stat] + [pl.BlockSpec((1, d, bt), lambda i: (i, 0, 0))] * 2,
        out_shape=[jax.ShapeDtypeStruct((t, d), BF16)] * 2 + [jax.ShapeDtypeStruct((1, d), F32)] * 2
        + [jax.ShapeDtypeStruct(st_f.shape, F32)] * 2 + [jax.ShapeDtypeStruct((t // bt, d, bt), BF16)] * 2,
        ins=[dx1b, w_o, fo, mo, g_fo, g_mo, st_f, st_m], semantics=("arbitrary",), side=side)


def flash_bwd(q_arr, qt_arr, k_arr, v_arr, do_arr, dot_arr, st, f_blocks, *, qoff, koff, voff, pair, scale, name, qblk=1024,
              side=None):
    t = q_arr.shape[0]
    blk = qt_arr.shape[2]
    qblk = max(min(qblk, t), blk)
    sub = qblk // blk
    nb, nbq = t // blk, t // qblk
    w = LANES if pair else 2 * LANES
    hw = w // 2
    has_bias = f_blocks is not None
    split = _halves if pair else _groups
    ins = [q_arr, qt_arr, k_arr, v_arr, do_arr, dot_arr, st] + ([f_blocks] if has_bias else [])
    n_out = 4 if has_bias else 3
    s_ins, s_outs = (side.ins, side.out_shapes) if side else ([], [])

    def wide(ref, first, count):
        parts = [ref[first + u] for u in range(count)]
        return parts[0] if count == 1 else jnp.concatenate(parts, axis=1)

    def body(*refs):
        main, si, outs, so, sems = _split_refs(refs, [len(ins), len(s_ins), n_out, len(s_outs), 2 if side else 0])
        q_ref, qt_ref, k_ref, v_ref, do_ref, dot_ref, st_ref = main[:7]
        dq_ref, dk_ref, dv_ref = outs[:3]
        g, j = pl.program_id(0), pl.program_id(1)
        step_id = g * nb + j
        if side:
            @pl.when(step_id == 0)
            def _():
                side.first(si, so, *sems)

        @pl.when(j == 0)
        def _():
            dq_ref[...] = jnp.zeros_like(dq_ref)

        kk, vv = k_ref[...], v_ref[...]
        ks = [kk, kk] if pair else _groups(kk)
        if has_bias:
            f_ref, df_ref = main[7], outs[3]
            fk = [LOG2E * _pick_row(f_ref[0], 2 * g + n) for n in range(2)]

            @pl.when(step_id == 0)
            def _():
                df_ref[...] = jnp.zeros_like(df_ref)

        def step(tb, count, carry, diagonal):
            rows = pl.ds(pl.multiple_of(tb * blk, blk), count * blk)
            qs = split((q_ref[rows, :].astype(F32) * (scale * LOG2E)).astype(BF16))
            qt = (wide(qt_ref, tb, count).astype(F32) * (scale * LOG2E)).astype(BF16)
            dos = [h.astype(BF16) for h in _halves(do_ref[rows, :].astype(F32))]
            dot = wide(dot_ref, tb, count)
            stats = st_ref[0, rows, :]
            new, dqs, row_sums = [], [], []
            for n in range(2):
                dkt, dvt, dfk = carry[n]
                s = lax.dot_general(qs[n], ks[n], NT, preferred_element_type=F32)
                if has_bias:
                    s = s - fk[n]
                if diagonal:
                    s = _below_diagonal(s)
                p = jnp.exp2(s - stats[:, n:n + 1])
                dp = lax.dot_general(dos[n], vv, NT, preferred_element_type=F32)
                ds = p * (dp - stats[:, 2 + n:3 + n])
                dsb = ds.astype(BF16)
                dvt = dvt + jnp.dot(dot[n * HEAD_DIM:(n + 1) * HEAD_DIM], p.astype(BF16), preferred_element_type=F32)
                dkt = dkt + jnp.dot(qt[n * hw:(n + 1) * hw], dsb, preferred_element_type=F32)
                dqs.append(jnp.dot(dsb, ks[n], preferred_element_type=F32))
                if has_bias:
                    dfk = dfk - jnp.sum(ds, axis=0, keepdims=True)
                    row_sums.append(jnp.sum(ds, axis=1, keepdims=True))
                new.append((dkt, dvt, dfk))
            dq = (jnp.where(_lane() < HEAD_DIM, dqs[0], dqs[1]) if pair else jnp.concatenate(dqs, axis=1)) * scale
            if has_bias:
                df_ref[rows, :] += jnp.where(_lane() == 2 * g, row_sums[0], jnp.where(_lane() == 2 * g + 1, row_sums[1], 0.0))
            dq_ref[rows, :] += dq
            return tuple(new)

        init = tuple((jnp.zeros((hw, blk), F32), jnp.zeros((HEAD_DIM, blk), F32), jnp.zeros((1, blk), F32)) for _ in range(2))
        carry = step(j, 1, init, True)
        whole = j // sub + 1
        carry = lax.fori_loop(j + 1, whole * sub, lambda tb, c: step(tb, 1, c, False), carry)
        (dka, dva, dfa), (dkb, dvb, dfb) = lax.fori_loop(whole, nbq, lambda i, c: step(i * sub, sub, c, False), carry)
        dk_ref[...] = (jnp.concatenate([dka, dkb], axis=0).T * LN2).astype(BF16)
        dv_ref[...] = jnp.concatenate([dva, dvb], axis=0).T.astype(BF16)
        if has_bias:
            row = lax.broadcasted_iota(jnp.int32, (LANES, 1), 0)
            by_head = jnp.where(row == 2 * g, dfa, jnp.where(row == 2 * g + 1, dfb, 0.0))
            df_ref[pl.ds(pl.multiple_of(j * blk, blk), blk), :] += by_head.T
        if side:
            @pl.when(step_id == PAIRS * nb - 1)
            def _():
                side.last(si, so, *sems)

    in_specs = [pl.BlockSpec((t, w), lambda g, j: (0, qoff + g)), pl.BlockSpec((nb, w, blk), lambda g, j: (0, qoff + g, 0)),
                pl.BlockSpec((blk, w), lambda g, j: (j, koff + g)), pl.BlockSpec((blk, LANES), lambda g, j: (j, voff + g)),
                pl.BlockSpec((t, LANES), lambda g, j: (0, g)), pl.BlockSpec((nb, LANES, blk), lambda g, j: (0, g, 0)),
                pl.BlockSpec((1, t, LANES), lambda g, j: (g, 0, 0))]
    out_specs = [pl.BlockSpec((t, w), lambda g, j: (0, g)), pl.BlockSpec((blk, w), lambda g, j: (j, g)),
                 pl.BlockSpec((blk, LANES), lambda g, j: (j, g))]
    out_shape = [jax.ShapeDtypeStruct((t, PAIRS * w), F32), jax.ShapeDtypeStruct((t, PAIRS * w), BF16),
                 jax.ShapeDtypeStruct((t, PAIRS * LANES), BF16)]
    if has_bias:
        in_specs.append(pl.BlockSpec((1, N_HEADS, blk), lambda g, j: (j, 0, 0)))
        out_specs.append(pl.BlockSpec((t, LANES), lambda g, j: (0, 0)))
        out_shape.append(jax.ShapeDtypeStruct((t, LANES), F32))
    s_in_specs, s_out_specs = side.specs() if side else ([], [])
    return pl.pallas_call(
        body, name=name, grid=(PAIRS, nb), in_specs=in_specs + s_in_specs, out_specs=out_specs + s_out_specs,
        out_shape=out_shape + list(s_outs), scratch_shapes=side.sems() if side else [],
        compiler_params=_params(("arbitrary", "arbitrary")),
    )(*ins, *s_ins)


def _adamw_math(w, g, m, v):
    nm = ADAM_B1 * m + (1.0 - ADAM_B1) * g
    nv = ADAM_B2 * v + (1.0 - ADAM_B2) * (g * g)
    m_hat = nm / (1.0 - ADAM_B1 ** ADAM_STEP)
    v_hat = nv / (1.0 - ADAM_B2 ** ADAM_STEP)
    return -ADAM_LR * (m_hat / (jnp.sqrt(v_hat) + ADAM_EPS) + ADAM_WD * w), nm, nv


def adamw_vectors(ws, g_rows, ms, vs, *, name):
    k = len(ws)

    def body(g_ref, *refs):
        w_refs, m_refs, v_refs, outs = _split_refs(refs, [k, k, k, 4 * k])
        for i in range(k):
            g = g_ref[i:i + 1, :w_refs[i].shape[1]]
            outs[4 * i][...] = g
            outs[4 * i + 1][...], outs[4 * i + 2][...], outs[4 * i + 3][...] = _adamw_math(
                w_refs[i][...], g, m_refs[i][...], v_refs[i][...])

    flat = pl.pallas_call(
        body, name=name, out_shape=[jax.ShapeDtypeStruct(w.shape, F32) for w in ws for _ in range(4)],
        compiler_params=_params(),
    )(g_rows, *ws, *ms, *vs)
    return [flat[4 * i:4 * i + 4] for i in range(k)]


def adamw_whole(w, g, m, v, *, name):
    def body(w_ref, g_ref, m_ref, v_ref, d_ref, nm_ref, nv_ref):
        d_ref[...], nm_ref[...], nv_ref[...] = _adamw_math(w_ref[...], g_ref[...], m_ref[...], v_ref[...])

    return pl.pallas_call(body, name=name, out_shape=[jax.ShapeDtypeStruct(w.shape, F32)] * 3,
                          compiler_params=_params())(w, g, m, v)


def adamw_halves(w, g_mine, g_other, m, v, core, *, name):
    _, k, n = w.shape
    hk = k // 2
    r = min(hk, max(8, ADAMW_BLOCK // 2 // n))
    blocks = [(s, mine) for s in range(hk // r) for mine in (True, False)]

    def body(c_ref, w_hbm, gm_hbm, go_hbm, m_hbm, v_hbm, g_out, d_out, nm_out, nv_out,
             wb, gb, mb, vb, db, nmb, nvb, in_sem, out_sem):
        def rows(s, mine):
            half = c_ref[0] if mine else 1 - c_ref[0]
            return pl.ds(pl.multiple_of(half * hk + s * r, 8), r)

        def reads(i):
            s, mine = blocks[i]
            at = rows(s, mine)
            srcs = [w_hbm.at[0, at], (gm_hbm if mine else go_hbm).at[pl.ds(s * r, r)], m_hbm.at[0, at], v_hbm.at[0, at]]
            return [pltpu.make_async_copy(src, buf.at[at], in_sem.at[4 * i + j])
                    for j, (src, buf) in enumerate(zip(srcs, (wb, gb, mb, vb)))]

        def writes(i):
            at = rows(*blocks[i])
            return [pltpu.make_async_copy(buf.at[at], dst.at[0, at], out_sem.at[4 * i + j])
                    for j, (buf, dst) in enumerate(zip((gb, db, nmb, nvb), (g_out, d_out, nm_out, nv_out)))]

        for i in range(len(blocks)):
            for cp in reads(i):
                cp.start()
        for i in range(len(blocks)):
            for cp in reads(i):
                cp.wait()
            at = rows(*blocks[i])
            db[at], nmb[at], nvb[at] = _adamw_math(wb[at], gb[at], mb[at], vb[at])
            for cp in writes(i):
                cp.start()
        for i in range(len(blocks)):
            for cp in writes(i):
                cp.wait()

    return pl.pallas_call(
        body, name=name,
        in_specs=[pl.BlockSpec(memory_space=pltpu.SMEM)] + [ANY] * 5, out_specs=[ANY] * 4,
        out_shape=[jax.ShapeDtypeStruct(w.shape, F32)] * 4,
        scratch_shapes=[pltpu.VMEM((k, n), F32)] * 7 + [pltpu.SemaphoreType.DMA((4 * len(blocks),))] * 2,
        compiler_params=_params(),
    )(core, w, g_mine, g_other, m, v)


def adamw_halves_t(wt, mt, vt, gt_mine, gt_other, core, *, name, bc=512):
    n, _, k = wt.shape
    nh = k // 2 // bc

    def body(c_ref, w_ref, m_ref, v_ref, gm_ref, go_ref, g_out, d_ref, nm_ref, nv_ref):
        gv = jnp.where(pl.program_id(0) == c_ref[0], gm_ref[...], go_ref[...])
        g_out[:, 0, :] = gv
        d_ref[:, 0, :], nm_ref[:, 0, :], nv_ref[:, 0, :] = _adamw_math(w_ref[:, 0, :], gv, m_ref[:, 0, :], v_ref[:, 0, :])

    full = pl.BlockSpec((n, 1, bc), lambda hb, i, c: (0, 0, hb * nh + i))
    half = pl.BlockSpec((n, bc), lambda hb, i, c: (0, i))
    return pl.pallas_call(
        body, name=name,
        grid_spec=pltpu.PrefetchScalarGridSpec(num_scalar_prefetch=1, grid=(2, nh), in_specs=[full, full, full, half, half],
                                               out_specs=[full] * 4),
        out_shape=[jax.ShapeDtypeStruct(wt.shape, F32)] * 4,
        compiler_params=_params(("parallel", "parallel")),
    )(core, wt, mt, vt, gt_mine, gt_other)


def add_pair(dw, recv, core, *, name):
    n4, k, n = dw.shape
    half = (1, k // 2, n) if split_axis(k) == 0 else (1, k, n // 2)
    mine = (lambda q, c: (q, c[0], 0)) if split_axis(k) == 0 else (lambda q, c: (q, 0, c[0]))

    def body(c_ref, a_ref, b_ref, o_ref):
        o_ref[...] = (a_ref[...] + b_ref[...].astype(F32)).astype(BF16)

    return pl.pallas_call(
        body, name=name,
        grid_spec=pltpu.PrefetchScalarGridSpec(
            num_scalar_prefetch=1, grid=(n4,),
            in_specs=[pl.BlockSpec(half, mine), pl.BlockSpec(half, lambda q, c: (q, 0, 0))],
            out_specs=pl.BlockSpec(half, lambda q, c: (q, 0, 0))),
        out_shape=jax.ShapeDtypeStruct((n4,) + half[1:], BF16),
        compiler_params=_params(("parallel",)),
    )(core, dw, recv)


def sum_chips(parts, *, name):
    n4, r, n = parts.shape
    if r % 16 == 0:
        br, bc = _row_block(r), n
    else:
        br, bc = r, LANES

    def body(p_ref, o_ref):
        acc = p_ref[0].astype(F32)
        for q in range(1, n4):
            acc = acc + p_ref[q].astype(F32)
        o_ref[...] = acc

    return pl.pallas_call(
        body, name=name, grid=(r // br, n // bc),
        in_specs=[pl.BlockSpec((n4, br, bc), lambda i, j: (0, i, j))], out_specs=pl.BlockSpec((br, bc), lambda i, j: (i, j)),
        out_shape=jax.ShapeDtypeStruct((r, n), F32),
        compiler_params=_params(("parallel", "parallel")),
    )(parts)


ANY = pl.BlockSpec(memory_space=pl.ANY)


def _place():
    x, y, c = lax.axis_index("x"), lax.axis_index("y"), lax.axis_index("c")
    chips = [(1 - x, y), (x, 1 - y), (1 - x, 1 - y)]
    return x, y, c, chips


def _copy(src, dst, send_sems, recv_sems, k, to):
    return pltpu.make_async_remote_copy(src_ref=src, dst_ref=dst, send_sem=send_sems.at[k], recv_sem=recv_sems.at[k],
                                        device_id=to, device_id_type=MESH)


def split_axis(rows):
    return 0 if rows % 32 == 0 else 1


def _half(ref, lead, hf):
    rows, cols = ref.shape[-2:]
    if split_axis(rows) == 0:
        at = (pl.ds(hf * (rows // 2), rows // 2), slice(None))
    else:
        at = (slice(None), pl.ds(hf * (cols // 2), cols // 2))
    return ref.at[at] if lead is None else ref.at[(lead,) + at]


def _gather_first(srcs, dsts, ssems, rsems):
    x, y, c, chips = _place()
    for ti, (s, d) in enumerate(zip(srcs, dsts)):
        for j, (cx, cy) in enumerate(chips):
            _copy(_half(s, None, c), _half(d, 2 * x + y, c), ssems, rsems, 3 * ti + j, (cx, cy, c)).start()


def _gather_mid(srcs, dsts, ssems, rsems):
    x, y, c, chips = _place()
    n1 = 3 * len(srcs)
    for ti, d in enumerate(dsts):
        for j, (cx, cy) in enumerate(chips):
            landed = _half(d, 2 * cx + cy, c)
            _copy(landed, landed, ssems, rsems, 3 * ti + j, (cx, cy, c)).wait_recv()
            _copy(landed, landed, ssems, rsems, n1 + 3 * ti + j, (x, y, 1 - c)).start()


def _gather_last(srcs, dsts, ssems, rsems):
    x, y, c, chips = _place()
    n1 = 3 * len(srcs)
    for ti, (s, d) in enumerate(zip(srcs, dsts)):
        for j, (cx, cy) in enumerate(chips):
            other = _half(d, 2 * cx + cy, 1 - c)
            _copy(other, other, ssems, rsems, n1 + 3 * ti + j, (x, y, 1 - c)).wait_recv()
        for j, (cx, cy) in enumerate(chips):
            mine = _half(s, None, c)
            _copy(mine, mine, ssems, rsems, 3 * ti + j, (cx, cy, c)).wait_send()
            _copy(mine, mine, ssems, rsems, n1 + 3 * ti + j, (x, y, 1 - c)).wait_send()


def gather_side(shards):
    return Side(shards, [jax.ShapeDtypeStruct((N_CHIPS,) + s.shape, s.dtype) for s in shards], 6 * len(shards),
                _gather_first, _gather_last, _gather_mid)


def _scatter_first(srcs, dsts, ssems, rsems):
    x, y, c, chips = _place()
    for ti, (s, d) in enumerate(zip(srcs, dsts)):
        for j, (cx, cy) in enumerate(chips):
            _copy(s.at[2 * cx + cy], d.at[2 * x + y], ssems, rsems, 3 * ti + j, (cx, cy, c)).start()


def _scatter_last(srcs, dsts, ssems, rsems):
    x, y, c, chips = _place()
    for ti, (s, d) in enumerate(zip(srcs, dsts)):
        for j, (cx, cy) in enumerate(chips):
            _copy(s.at[2 * cx + cy], d.at[2 * cx + cy], ssems, rsems, 3 * ti + j, (cx, cy, c)).wait_recv()
        for j, (cx, cy) in enumerate(chips):
            _copy(s.at[2 * cx + cy], d.at[2 * cx + cy], ssems, rsems, 3 * ti + j, (cx, cy, c)).wait_send()


def scatter_side(parts):
    return Side(parts, [jax.ShapeDtypeStruct(p.shape, p.dtype) for p in parts], 3 * len(parts), _scatter_first, _scatter_last)


def run_side(side, *, name):
    n_in, n_out = len(side.ins), len(side.out_shapes)

    def body(*refs):
        si, so, sems = _split_refs(refs, [n_in, n_out, 2])
        side.first(si, so, *sems)
        if side.mid is not None:
            side.mid(si, so, *sems)
        side.last(si, so, *sems)

    in_specs, out_specs = side.specs()
    return pl.pallas_call(body, name=name, in_specs=in_specs, out_specs=out_specs, out_shape=side.out_shapes,
                          scratch_shapes=side.sems())(*side.ins)


def _swap_first(srcs, dsts, ssems, rsems):
    x, y, c, _ = _place()
    for k, (s, d) in enumerate(zip(srcs, dsts)):
        _copy(s, d, ssems, rsems, k, (x, y, 1 - c)).start()


def _swap_last(srcs, dsts, ssems, rsems):
    x, y, c, _ = _place()
    for k, (s, d) in enumerate(zip(srcs, dsts)):
        _copy(s, d, ssems, rsems, k, (x, y, 1 - c)).wait()


def swap_side(xs):
    return Side(xs, [jax.ShapeDtypeStruct(a.shape, a.dtype) for a in xs], len(xs), _swap_first, _swap_last)


def _swap_halves(srcs, dsts, ssems, rsems):
    x, y, c, _ = _place()
    for k, (s, d) in enumerate(zip(srcs, dsts)):
        hk = s.shape[1] // 2
        yield _copy(s.at[:, pl.ds((1 - c) * hk, hk), :], d, ssems, rsems, k, (x, y, 1 - c))


def _swap_halves_first(srcs, dsts, ssems, rsems):
    for cp in _swap_halves(srcs, dsts, ssems, rsems):
        cp.start()


def _swap_halves_last(srcs, dsts, ssems, rsems):
    for cp in _swap_halves(srcs, dsts, ssems, rsems):
        cp.wait()


def swap_halves_side(xs):
    return Side(xs, [jax.ShapeDtypeStruct((a.shape[0], a.shape[1] // 2, a.shape[2]), a.dtype) for a in xs], len(xs),
                _swap_halves_first, _swap_halves_last)


def allreduce_small(s):
    n_dev = 8

    def body(s_ref, out_ref, buf, send_sems, recv_sems):
        x, y, c, _ = _place()
        me = 4 * x + 2 * y + c
        buf[me] = s_ref[...]
        sends = []
        for k in range(1, n_dev):
            px = 1 - x if k & 4 else x
            py = 1 - y if k & 2 else y
            pc = 1 - c if k & 1 else c
            cp = _copy(s_ref, buf.at[me], send_sems, recv_sems, k - 1, (px, py, pc))
            cp.start()
            sends.append((cp, 4 * px + 2 * py + pc))
        for k, (cp, peer) in enumerate(sends):
            _copy(s_ref, buf.at[peer], send_sems, recv_sems, k, (x, y, c)).wait_recv()
        for cp, _ in sends:
            cp.wait_send()
        acc = buf[0]
        for d in range(1, n_dev):
            acc = acc + buf[d]
        out_ref[...] = acc

    vm = pl.BlockSpec(memory_space=pltpu.VMEM)
    return pl.pallas_call(
        body, name="allreduce_small", in_specs=[vm], out_specs=vm,
        out_shape=jax.ShapeDtypeStruct(s.shape, F32),
        scratch_shapes=[pltpu.VMEM((n_dev,) + s.shape, F32), pltpu.SemaphoreType.DMA((n_dev - 1,)),
                        pltpu.SemaphoreType.DMA((n_dev - 1,))],
    )(s)


def join_cols(sm):
    n4, k, n = sm.shape
    return sm.transpose(1, 0, 2).reshape(k, n4 * n)


def split_cols(full):
    k, n = full.shape
    return full.reshape(k, N_CHIPS, n // N_CHIPS).transpose(1, 0, 2)


def _pad_heads(w, width):
    lead, heads = w.shape[:-1], w.shape[-1] // width
    w = w.reshape(lead + (heads, width))
    return jnp.pad(w, [(0, 0)] * len(lead) + [(0, 0), (0, LANES - width)]).reshape(lead + (heads * LANES,))


def _unpad_heads(w, width):
    lead, heads = w.shape[:-1], w.shape[-1] // LANES
    return w.reshape(lead + (heads, LANES))[..., :width].reshape(lead + (heads * width,))


O_F = 3 * FW
O_CQ = O_F + N_HEADS
O_CKV = O_CQ + Q_RANK
O_KR = O_CKV + KV_RANK
O_END = O_KR + ROPE_DIM


def split_w_in_t(w_sm):
    n4, r, d = w_sm.shape
    segments = [(0, O_F, 0, 0), (O_F, N_HEADS, 1, S_F), (O_CQ, Q_RANK, 1, S_CQ), (O_CKV, KV_RANK, 1, S_CKV),
                (O_KR, ROPE_DIM, 1, S_KR + HEAD_DIM)]
    moves = []
    for v0, n, dst, d0 in segments:
        for q in range(n4):
            a, b = max(v0, q * r), min(v0 + n, (q + 1) * r)
            if a < b:
                assert a % 2 == 0 and b % 2 == 0 and (d0 + a - v0) % 2 == 0
                moves.append((q, (a - q * r) // 2, (b - a) // 2, dst, (d0 + a - v0) // 2))

    def body(w_ref, qkv_ref, small_ref, src, dst_qkv, dst_small):
        for q in range(n4):
            src[q] = pltpu.bitcast(w_ref[q], jnp.uint32)
        dst_small[...] = jnp.zeros(dst_small.shape, jnp.uint32)
        for q, a, n, dst, b in moves:
            (dst_qkv, dst_small)[dst][pl.ds(b, n), :] = src[q, pl.ds(a, n), :]
        qkv_ref[...] = pltpu.bitcast(dst_qkv[...], w_sm.dtype)
        small_ref[...] = pltpu.bitcast(dst_small[...], w_sm.dtype)

    return pl.pallas_call(
        body, name="split_w_in",
        out_shape=[jax.ShapeDtypeStruct((O_F, d), w_sm.dtype), jax.ShapeDtypeStruct((S_END, d), w_sm.dtype)],
        scratch_shapes=[pltpu.VMEM((n4, r // 2, d), jnp.uint32), pltpu.VMEM((O_F // 2, d), jnp.uint32),
                        pltpu.VMEM((S_END // 2, d), jnp.uint32)],
        compiler_params=_params(),
    )(w_sm)


def join_w_in_t(d_qkv_t, d_small_t):
    small = len(d_qkv_t)
    segments = [(n, 0, n * FW, FW) for n in range(small)]
    segments += [(small, S_F, O_F, N_HEADS), (small, S_CQ, O_CQ, Q_RANK), (small, S_CKV, O_CKV, KV_RANK),
                 (small, S_KR + HEAD_DIM, O_KR, ROPE_DIM)]
    r, d = O_END // N_CHIPS, d_small_t.shape[1]

    def body(*refs):
        o_ref = refs[-1]
        for src, s0, v0, n in segments:
            for q in range(N_CHIPS):
                a, b = max(v0, q * r), min(v0 + n, (q + 1) * r)
                if a < b:
                    o_ref[q, pl.ds(a - q * r, b - a), :] = refs[src][pl.ds(s0 + a - v0, b - a), :]

    return pl.pallas_call(
        body, name="join_w_in", out_shape=jax.ShapeDtypeStruct((N_CHIPS, r, d), F32), compiler_params=_params(),
    )(*d_qkv_t, d_small_t)


def rope_inputs(pos):
    inv_freq = ROPE_THETA ** (-jnp.arange(0, ROPE_DIM, 2, dtype=F32) / ROPE_DIM)
    inv_row = jnp.concatenate([jnp.zeros((HEAD_DIM,), F32), inv_freq, inv_freq, jnp.zeros((LANES - HEAD_DIM - ROPE_DIM,), F32)])
    return pos.astype(F32)[:, None], inv_row[None, :]


def _pad_lanes(v, n):
    return jnp.pad(v, ((0, 0), (0, n - v.shape[1])))


ATTN_BLK = 512
ATTN_FWD_BLK = 1024
ATTN_BWD_QBLK = 1024
ACC_ROWS = HEAD_DIM + 16


def local_step(xs, pos, tgt, gains, early_weights, late_weights, early_side=None, fwd_sides=(None, None), reduction=None):
    g_attn, b_forget, g_q, g_kv, g_fo, g_mo, g_mlp, g_fin = gains
    t = xs.shape[0]
    blk = min(ATTN_BLK, t)
    fox_scale = 1.0 / (HEAD_DIM ** 0.5)
    mla_scale = 1.0 / ((HEAD_DIM + ROPE_DIM) ** 0.5)

    h1, *gathered = rmsnorm(xs, g_attn, out_dtype=BF16, name="norm_attn", side=early_side)
    w_in_t, w_uq_p, w_ukv = early_weights(gathered)
    w_qkv_t, w_small_t = split_w_in_t(w_in_t)
    kv = w_ukv.reshape(KV_RANK, N_HEADS, 2 * HEAD_DIM)
    w_ukv_p = jnp.concatenate([_pad_heads(kv[:, :, :HEAD_DIM].reshape(KV_RANK, FW), HEAD_DIM),
                               kv[:, :, HEAD_DIM:].reshape(KV_RANK, FW)], axis=1)
    b_f = _pad_lanes(b_forget, LANES)
    pos_col, inv_row = rope_inputs(pos)

    qkv, qkv_t = mm(h1, w_qkv_t, trans_b=True, out_dtypes=[BF16], t_blk=blk, name="proj_qkv", bn=1536)
    small, = mm(h1, w_small_t, trans_b=True, out_dtypes=[F32], name="proj_small")
    mq, mk, mv, lf, cqn, ckvn, mq_t, mv_t = mla_prep(small, g_q, g_kv, w_uq_p, w_ukv_p, pos_col, inv_row, b_f,
                                                     name="mla_prep", bt=blk)
    f_cum = cumsum_rows(lf, reverse=False, name="gate_cumsum")
    f_blocks = f_cum[:, :N_HEADS].reshape(t // blk, blk, N_HEADS).transpose(0, 2, 1)
    fo, st_f, *gathered = flash_fwd(qkv_t, qkv, qkv_t, f_cum, qoff=0, koff=PAIRS, voff=2 * PAIRS, pair=True,
                                    scale=fox_scale, name="fox_fwd", blk=ATTN_FWD_BLK, side=fwd_sides[0])
    mo, st_m, *more = flash_fwd(mq_t, mk, mv_t, None, qoff=0, koff=0, voff=0, pair=False, scale=mla_scale, name="mla_fwd",
                                blk=ATTN_FWD_BLK, side=fwd_sides[1])
    w_o, w_up, w_down = late_weights(gathered + more)
    mixed = mix_norm(fo, mo, g_fo, g_mo, name="norm_mix")

    def inv_rms(v):
        return lax.rsqrt(jnp.mean(v * v, axis=-1, keepdims=True) + EPS)

    def residual_then_norm(acc, res, g):
        xn = acc + res
        return xn, xn * inv_rms(xn) * g

    def norm_bwd(dh, xn, res, g):
        r = inv_rms(xn)
        uu = dh * g
        return (r * uu - xn * (r * r * r * jnp.mean(uu * xn, axis=-1, keepdims=True)) + res,
                jnp.sum(dh * (xn * r), axis=0, keepdims=True))

    def norm_bwd2(dh, xn, res, g):
        dx, dg = norm_bwd(dh, xn, res, g)
        return dx, dx, dg

    def residual_then_loss(acc, res, target, g):
        xn = acc + res
        r = inv_rms(xn)
        xh = xn * r
        e = xh * g - target
        part = 0.5 * jnp.sum(jnp.mean(e * e, axis=-1, keepdims=True), axis=0, keepdims=True)
        dy = e * (1.0 / xn.shape[1])
        uu = dy * g
        dx = r * uu - xn * (r * r * r * jnp.mean(uu * xn, axis=-1, keepdims=True))
        return dx, dx, jnp.sum(dy * xh, axis=0, keepdims=True), part + jnp.zeros_like(g)

    x1, h2 = mm(mixed, w_o, extras=[xs], vecs=[g_mlp], epilogue=residual_then_norm, out_dtypes=[F32, BF16], name="out_proj")

    def relu2(uu):
        r = jnp.maximum(uu.astype(F32), 0.0)
        return (r * r).astype(BF16)

    u, = mm(h2, w_up, out_dtypes=[BF16], name="mlp_up", bn=2048)
    dx2, dx2b, dg_fin, loss_row = mm(u, w_down, a_pro=relu2, extras=[x1, tgt], vecs=[g_fin], epilogue=residual_then_loss,
                                     out_dtypes=[F32, BF16], n_sums=2, name="mlp_down_loss")
    loss = loss_row[:, :1]

    def relu2_grad(acc, uu):
        return (acc * (2.0 * jnp.maximum(uu.astype(F32), 0.0)),)

    du, = mm(dx2b, w_down, trans_b=True, extras=[u], epilogue=relu2_grad, out_dtypes=[BF16], name="mlp_down_bwd", bn=2048)
    dw_down, dw_down_b = (g.reshape(N_CHIPS, -1, w_down.shape[1])
                          for g in mm_tn(u, dx2b, a_pro=relu2, name="dw_down", bf16_copy=True))
    dx1, dx1b, dg_mlp = mm(du, w_up, trans_b=True, extras=[x1, dx2], vecs=[g_mlp], epilogue=norm_bwd2,
                           out_dtypes=[F32, BF16], n_sums=1, name="mlp_up_bwd")
    dw_up, dw_up_b = mm_tn(h2, du, name="dw_up", col_shards=N_CHIPS, bf16_copy=True)

    dw_o, dw_o_b = (g.reshape(N_CHIPS, -1, w_o.shape[1]) for g in mm_tn(mixed, dx1b, name="dw_o", bf16_copy=True))
    late, late_b = (dw_o, dw_up, dw_down), (dw_o_b, dw_up_b, dw_down_b)
    red = reduction
    dfo, dmo, dg_fo, dg_mo, st_f, st_m, dfo_t, dmo_t, *got = mix_norm_bwd(
        dx1b, w_o, fo, mo, g_fo, g_mo, st_f, st_m, name="out_proj_mix_bwd", bt=blk,
        side=red.late_swap(late, late_b) if red else None)
    dfq, dfk, dfv, d_f, *got = flash_bwd(
        qkv, qkv_t, qkv, qkv, dfo, dfo_t, st_f, f_blocks, qoff=0, koff=PAIRS, voff=2 * PAIRS, pair=True,
        scale=fox_scale, name="fox_bwd", qblk=ATTN_BWD_QBLK, side=red.late_scatter(got) if red else None)
    dmq, dmk, dmv, *got = flash_bwd(mq, mq_t, mk, mv, dmo, dmo_t, st_m, None, qoff=0, koff=0, voff=0,
                                    pair=False, scale=mla_scale, name="mla_bwd", qblk=ATTN_BWD_QBLK,
                                    side=red.late_halves(got) if red else None)
    if red:
        red.late_done(got)
    dlf = cumsum_rows(d_f, reverse=True, name="gate_cumsum_bwd")
    dsmall, dq_u, dkv_u, dg_q, dg_kv, db_f = mla_prep_bwd(dmq, dmk, dmv, dlf, small, g_q, g_kv, w_uq_p, w_ukv_p,
                                                          pos_col, inv_row, b_f, name="mla_prep_bwd")
    dw_uq_p = mm_tn(cqn, dq_u, name="dw_uq")
    dw_ukv_p = mm_tn(ckvn, dkv_u, name="dw_ukv")

    def to_bf16(tile):
        return tile.astype(BF16)

    dqkv = [dfq, dfk, dfv]
    dw_in_t = join_w_in_t([mm_tn(part, h1, a_pro=to_bf16, name="dw_" + nm) for part, nm in zip(dqkv, "qkv")],
                          mm_tn(dsmall, h1, name="dw_small"))
    dk_cols = _unpad_heads(dw_ukv_p[:, :HW], HEAD_DIM).reshape(KV_RANK, N_HEADS, HEAD_DIM)
    dv_cols = dw_ukv_p[:, HW:].reshape(KV_RANK, N_HEADS, HEAD_DIM)
    dw_ukv = jnp.concatenate([dk_cols, dv_cols], axis=2).reshape(KV_RANK, N_HEADS * 2 * HEAD_DIM)
    early = (dw_in_t, split_cols(dw_uq_p), split_cols(dw_ukv))
    w_parts = [w_qkv_t[n * FW:(n + 1) * FW] for n in range(3)]
    grad_x, dg_attn, *got = mm(dqkv + [dsmall], w_parts + [w_small_t], a_pro=to_bf16, extras=[xs, dx1], vecs=[g_attn],
                               epilogue=norm_bwd, out_dtypes=[F32], n_sums=1, name="proj_bwd",
                               side=red.early_scatter(early) if red else None)
    if red:
        red.early_done(got)
    d_gains = (dg_attn, db_f[:, :N_HEADS], dg_q, dg_kv, dg_fo, dg_mo, dg_mlp, dg_fin)
    return loss, grad_x, early, late, d_gains


class GradReduction:
    def __init__(self, core_id, chip):
        self.core_id, self.chip = core_id, chip
        self.core = core_id.reshape(1).astype(jnp.int32)
        self.grads, self.pairs, self.halves, self.others = {}, {}, {}, {}

    def _other_halves(self, grads):
        out = []
        for g in grads:
            axis = 1 + split_axis(g.shape[1])
            size = g.shape[axis] // 2
            out.append(lax.dynamic_slice_in_dim(g, (1 - self.core_id) * size, size, axis=axis).astype(BF16))
        return out

    def _add_pairs(self, group, recvs):
        self.pairs[group] = [add_pair(g, r, self.core, name="add_pair_%s_%d" % (group, n))
                             for n, (g, r) in enumerate(zip(self.grads[group], recvs))]
        return scatter_side(self.pairs[group])

    def _chip_sums(self, group, scattered):
        chip = self.chip
        with_mine = [lax.dynamic_update_index_in_dim(s, lax.dynamic_index_in_dim(p, chip, 0, keepdims=True), chip, 0)
                     for s, p in zip(scattered, self.pairs[group])]
        self.halves[group] = [sum_chips(s, name="sum_chips_%s_%d" % (group, n)) for n, s in enumerate(with_mine)]
        return self.halves[group]

    def late_swap(self, grads, bf16_copies):
        self.grads["late"] = list(grads)
        return swap_halves_side(list(bf16_copies))

    def late_scatter(self, recvs):
        return self._add_pairs("late", recvs)

    def late_halves(self, scattered):
        return swap_side(self._chip_sums("late", scattered))

    def late_done(self, others):
        self.others["late"] = list(others)

    def early_scatter(self, grads):
        self.grads["early"] = list(grads)
        return self._add_pairs("early", run_side(swap_side(self._other_halves(grads)), name="swap_early_sends"))

    def early_done(self, scattered):
        self.others["early"] = list(run_side(swap_side(self._chip_sums("early", scattered)), name="swap_early_halves"))

def kernel(x, positions, attn_norm_g, w_in, b_forget, q_norm_g, w_uq, kv_norm_g, w_ukv, fox_out_g, mla_out_g, w_o, mlp_norm_g, w_up, w_down, final_norm_g, loss_target, m_attn_norm_g, m_w_in, m_b_forget, m_q_norm_g, m_w_uq, m_kv_norm_g, m_w_ukv, m_fox_out_g, m_mla_out_g, m_w_o, m_mlp_norm_g, m_w_up, m_w_down, m_final_norm_g, v_attn_norm_g, v_w_in, v_b_forget, v_q_norm_g, v_w_uq, v_kv_norm_g, v_w_ukv, v_fox_out_g, v_mla_out_g, v_w_o, v_mlp_norm_g, v_w_up, v_w_down, v_final_norm_g):
    core_id = lax.axis_index("c")
    core = core_id.reshape(1).astype(jnp.int32)
    chip = 2 * lax.axis_index("x") + lax.axis_index("y")
    big = [w_in, w_uq, w_ukv, w_o, w_up, w_down]
    big_m = [m_w_in, m_w_uq, m_w_ukv, m_w_o, m_w_up, m_w_down]
    big_v = [v_w_in, v_w_uq, v_w_ukv, v_w_o, v_w_up, v_w_down]
    n_early = 3

    def vec(a):
        return a.reshape(1, -1)

    small = [attn_norm_g, b_forget, q_norm_g, kv_norm_g, fox_out_g, mla_out_g, mlp_norm_g, final_norm_g]
    small_m = [m_attn_norm_g, m_b_forget, m_q_norm_g, m_kv_norm_g, m_fox_out_g, m_mla_out_g, m_mlp_norm_g, m_final_norm_g]
    small_v = [v_attn_norm_g, v_b_forget, v_q_norm_g, v_kv_norm_g, v_fox_out_g, v_mla_out_g, v_mlp_norm_g, v_final_norm_g]
    gains = [vec(a) for a in small]

    views = [big[0][0].T, _pad_heads(big[1][0], HEAD_DIM + ROPE_DIM)] + [w[0] for w in big[2:]]
    shards = [v.astype(BF16) for v in views]

    def with_own(gathered, mine):
        return [lax.dynamic_update_index_in_dim(g, s, chip, 0) for g, s in zip(gathered, mine)]

    def early_weights(gathered):
        g_in, g_uq, g_ukv = with_own(gathered, shards[:n_early])
        return g_in, join_cols(g_uq), join_cols(g_ukv)

    def late_weights(gathered):
        g_o, g_up, g_down = with_own(gathered, shards[n_early:])
        return g_o.reshape(-1, g_o.shape[2]), join_cols(g_up), g_down.reshape(-1, g_down.shape[2])

    reduction = GradReduction(core_id, chip)
    loss, grad_x, _, _, d_small = local_step(
        x[0], positions[0], loss_target[0], gains, early_weights, late_weights, gather_side(shards[:n_early]),
        (gather_side(shards[n_early:-1]), gather_side(shards[-1:])), reduction)
    halves = reduction.halves["early"] + reduction.halves["late"]
    others = reduction.others["early"] + reduction.others["late"]

    def rows8(vs):
        return jnp.concatenate([_pad_lanes(vec(a).astype(F32), 1024) for a in vs], axis=0)

    with_loss = [jnp.concatenate([d, loss], axis=1) if n == 1 else d for n, d in enumerate(d_small)]
    g_small8 = allreduce_small(rows8(with_loss))

    outs_big = []
    for n, (w, gm, go, m, v) in enumerate(zip(big, halves, others, big_m, big_v)):
        if n == 0:
            outs = adamw_halves_t(*(jnp.transpose(a, (2, 0, 1)) for a in (w, m, v)), gm, go, core, name="adamw_%d" % n)
            outs_big.append([jnp.transpose(o, (1, 2, 0)) for o in outs])
        elif n == 1:
            gm, go = (_unpad_heads(gh, HEAD_DIM + ROPE_DIM) for gh in (gm, go))
            g_t = jnp.where(core[0] == 0, jnp.concatenate([gm, go], axis=0), jnp.concatenate([go, gm], axis=0)).T
            outs = (g_t,) + tuple(adamw_whole(w[0].T, g_t, m[0].T, v[0].T, name="adamw_%d" % n))
            outs_big.append([o.T[None] for o in outs])
        else:
            outs_big.append(adamw_halves(w, gm, go, m, v, core, name="adamw_%d" % n))
    outs_small = adamw_vectors(gains, g_small8, [vec(a) for a in small_m], [vec(a) for a in small_v], name="adamw_small")

    loss_all = g_small8[1, N_HEADS]
    grads, deltas, new_m, new_v = [None] * 14, [None] * 14, [None] * 14, [None] * 14
    big_at = [1, 4, 6, 9, 11, 12]
    small_at = [0, 2, 3, 5, 7, 8, 10, 13]
    for n, at in enumerate(big_at):
        grads[at], deltas[at], new_m[at], new_v[at] = outs_big[n]
    for at, s, outs in zip(small_at, small, outs_small):
        grads[at], deltas[at], new_m[at], new_v[at] = (o.reshape(s.shape) for o in outs)
    return (loss_all, grad_x[None], *grads, *deltas, *new_m, *new_v)
```

```python
import jax
import jax.numpy as jnp
from jax import lax
from jax.experimental import pallas as pl
from jax.experimental.pallas import tpu as pltpu

F32 = jnp.float32
BF16 = jnp.bfloat16
MESH = pl.DeviceIdType.MESH

EPS = 1e-6
ROPE_THETA = 10000.0
N_HEADS = 8
PAIRS = N_HEADS // 2
HEAD_DIM = 64
ROPE_DIM = 32
LANES = 128
Q_RANK = 384
KV_RANK = 256
N_CHIPS = 4
ADAM_LR, ADAM_B1, ADAM_B2, ADAM_EPS, ADAM_WD, ADAM_STEP = 0.001, 0.9, 0.999, 1e-08, 0.01, 10
VMEM_LIMIT = 48 * 1024 * 1024
ADAMW_BLOCK = 256 * 1024
LOG2E = 1.4426950408889634
LN2 = 0.6931471805599453
NN = (((1,), (0,)), ((), ()))
NT = (((1,), (1,)), ((), ()))
TN = (((0,), (0,)), ((), ()))


def _params(sem=None):
    return pltpu.CompilerParams(dimension_semantics=sem, vmem_limit_bytes=VMEM_LIMIT)


def _fit(block, dim):
    if dim <= block:
        return dim
    return next(b for b in range(block - block % LANES, 0, -LANES) if dim % b == 0)


def _row_block(rows):
    return next(b for b in (256, 128, 64, 32, 16, 8) if rows % b == 0)


def gridded(body, *, name, grid, in_specs, out_specs, out_shape, ins, semantics, side=None):
    if side is None:
        return pl.pallas_call(body, name=name, grid=grid, in_specs=in_specs, out_specs=out_specs, out_shape=out_shape,
                              compiler_params=_params(semantics))(*ins)
    n_in, n_out = len(in_specs), len(out_specs)
    steps = 1
    for extent in grid:
        steps *= extent

    def riding(*refs):
        main_in, s_in, main_out, s_out, sems = _split_refs(refs, [n_in, len(side.ins), n_out, len(side.out_shapes), 2])
        step = 0
        for axis, extent in enumerate(grid):
            step = step * extent + pl.program_id(axis)

        @pl.when(step == 0)
        def _():
            side.first(s_in, s_out, *sems)

        body(*main_in, *main_out)

        @pl.when(step == steps - 1)
        def _():
            if side.mid is not None:
                side.mid(s_in, s_out, *sems)
            side.last(s_in, s_out, *sems)

    s_in_specs, s_out_specs = side.specs()
    return pl.pallas_call(
        riding, name=name, grid=grid, in_specs=list(in_specs) + s_in_specs, out_specs=list(out_specs) + s_out_specs,
        out_shape=list(out_shape) + side.out_shapes, scratch_shapes=side.sems(),
        compiler_params=_params(("arbitrary",) * len(grid)))(*ins, *side.ins)


def rmsnorm(x, g, *, out_dtype, name, bt=512, side=None):
    t, d = x.shape
    bt = min(bt, t)

    def body(x_ref, g_ref, o_ref):
        xv = x_ref[...].astype(F32)
        r = lax.rsqrt(jnp.mean(xv * xv, axis=-1, keepdims=True) + EPS)
        o_ref[...] = (xv * r * g_ref[...]).astype(o_ref.dtype)

    return gridded(
        body, name=name, grid=(t // bt,),
        in_specs=[pl.BlockSpec((bt, d), lambda i: (i, 0)), pl.BlockSpec((1, d), lambda i: (0, 0))],
        out_specs=[pl.BlockSpec((bt, d), lambda i: (i, 0))],
        out_shape=[jax.ShapeDtypeStruct((t, d), out_dtype)],
        ins=[x, g], semantics=("parallel",), side=side)


def mm(a, b, *, trans_b=False, a_pro=None, extras=(), vecs=(), epilogue=None, out_dtypes, n_sums=0, t_blk=None, name,
       bm=1024, bn=1024, side=None):
    a_list = list(a) if isinstance(a, (list, tuple)) else [a]
    b_list = list(b) if isinstance(b, (list, tuple)) else [b]
    m = a_list[0].shape[0]
    n = b_list[0].shape[0] if trans_b else b_list[0].shape[1]
    ks = [x.shape[1] for x in a_list]
    if sum(ks) > 2048:
        bm = bm // 2
    bm, bn = _fit(bm, m), _fit(bn, n)
    assert n_sums == 0 or bn == n
    n_ab, n_ex, n_vec, n_out = len(a_list), len(extras), len(vecs), len(out_dtypes)
    n_t = 0 if t_blk is None else 1

    def body(*refs):
        a_refs, b_refs, ex, vs, outs, t_outs, sums = _split_refs(refs, [n_ab, n_ab, n_ex, n_vec, n_out, n_t, n_sums])
        acc = None
        for a_ref, b_ref in zip(a_refs, b_refs):
            a_tile = a_ref[...] if a_pro is None else a_pro(a_ref[...])
            part = lax.dot_general(a_tile, b_ref[...], NT if trans_b else NN, preferred_element_type=F32)
            acc = part if acc is None else acc + part
        res = epilogue(acc, *[e[...] for e in ex], *[v[...] for v in vs]) if epilogue is not None else (acc,)
        for o, r in zip(outs, res[:n_out]):
            o[...] = r.astype(o.dtype)
        for t_ref in t_outs:
            for u in range(bm // t_blk):
                t_ref[u] = res[0][u * t_blk:(u + 1) * t_blk, :].T.astype(t_ref.dtype)
        if n_sums:
            @pl.when(pl.program_id(0) == 0)
            def _():
                for s_ref in sums:
                    s_ref[...] = jnp.zeros_like(s_ref)

            for s_ref, r in zip(sums, res[n_out:]):
                s_ref[...] += r

    tile = pl.BlockSpec((bm, bn), lambda i, j: (i, j))
    vec = pl.BlockSpec((1, bn), lambda i, j: (0, j))
    t_specs, t_shapes = [], []
    if t_blk is not None:
        t_specs = [pl.BlockSpec((bm // t_blk, bn, t_blk), lambda i, j: (i, j, 0))]
        t_shapes = [jax.ShapeDtypeStruct((m // t_blk, n, t_blk), out_dtypes[0])]
    a_specs = [pl.BlockSpec((bm, k), lambda i, j: (i, 0)) for k in ks]
    b_specs = [pl.BlockSpec((bn, k), lambda i, j: (j, 0)) if trans_b else pl.BlockSpec((k, bn), lambda i, j: (0, j)) for k in ks]
    return gridded(
        body, name=name, grid=(m // bm, n // bn),
        in_specs=a_specs + b_specs + [tile] * n_ex + [vec] * n_vec,
        out_specs=[tile] * n_out + t_specs + [vec] * n_sums,
        out_shape=[jax.ShapeDtypeStruct((m, n), dt) for dt in out_dtypes] + t_shapes + [jax.ShapeDtypeStruct((1, n), F32)] * n_sums,
        ins=[*a_list, *b_list, *extras, *vecs],
        semantics=("arbitrary", "arbitrary") if n_sums else ("parallel", "parallel"), side=side)


def mm_tn(a, b, *, a_pro=None, name, col_shards=1, bf16_copy=False, bk=1024, bn=1024, bt=2048):
    t, k = a.shape
    n = b.shape[1]
    ns = n // col_shards
    bk, bn, bt = _fit(bk, k), _fit(bn, ns), _fit(bt, t)
    per = ns // bn
    last = t // bt - 1

    def body(a_ref, b_ref, o_ref, *copy_ref):
        @pl.when(pl.program_id(2) == 0)
        def _():
            o_ref[...] = jnp.zeros_like(o_ref)

        a_tile = a_ref[...] if a_pro is None else a_pro(a_ref[...])
        o_ref[...] += lax.dot_general(a_tile, b_ref[...], TN, preferred_element_type=F32)
        if bf16_copy:
            @pl.when(pl.program_id(2) == last)
            def _():
                copy_ref[0][...] = o_ref[...].astype(BF16)

    if col_shards == 1:
        out_spec = pl.BlockSpec((bk, bn), lambda i, j, s: (i, j))
        shape = (k, n)
    else:
        out_spec = pl.BlockSpec((None, bk, bn), lambda i, j, s: (j // per, i, j % per))
        shape = (col_shards, k, ns)
    out_specs, out_shape = out_spec, jax.ShapeDtypeStruct(shape, F32)
    if bf16_copy:
        out_specs, out_shape = [out_spec, out_spec], [out_shape, jax.ShapeDtypeStruct(shape, BF16)]
    return pl.pallas_call(
        body, name=name, grid=(k // bk, n // bn, t // bt),
        in_specs=[pl.BlockSpec((bt, bk), lambda i, j, s: (s, i)), pl.BlockSpec((bt, bn), lambda i, j, s: (s, j))],
        out_specs=out_specs, out_shape=out_shape,
        compiler_params=_params(("parallel", "parallel", "arbitrary")),
    )(a, b)


def _split3(x):
    hi = x.astype(BF16)
    r1 = x - hi.astype(F32)
    mid = r1.astype(BF16)
    lo = (r1 - mid.astype(F32)).astype(BF16)
    return hi, mid, lo


def cumsum_rows(x, *, reverse, name, bc=512):
    t, d = x.shape
    bc = min(bc, t)
    nb = t // bc

    def body(x_ref, o_ref, carry):
        @pl.when(pl.program_id(0) == 0)
        def _():
            carry[...] = jnp.zeros_like(carry)

        r = lax.broadcasted_iota(jnp.int32, (bc, bc), 0)
        c = lax.broadcasted_iota(jnp.int32, (bc, bc), 1)
        tri = jnp.where((r <= c) if reverse else (r >= c), 1.0, 0.0).astype(BF16)
        hi, mid, lo = _split3(x_ref[...])
        s = (lax.dot_general(tri, hi, NN, preferred_element_type=F32)
             + lax.dot_general(tri, mid, NN, preferred_element_type=F32)
             + lax.dot_general(tri, lo, NN, preferred_element_type=F32)) + carry[0:1, :]
        o_ref[...] = s
        carry[0:1, :] = s[0:1, :] if reverse else s[bc - 1:bc, :]

    imap = (lambda i: (nb - 1 - i, 0)) if reverse else (lambda i: (i, 0))
    return pl.pallas_call(
        body, name=name, grid=(nb,),
        in_specs=[pl.BlockSpec((bc, d), imap)], out_specs=pl.BlockSpec((bc, d), imap),
        out_shape=jax.ShapeDtypeStruct((t, d), F32),
        scratch_shapes=[pltpu.VMEM((8, d), F32)],
        compiler_params=_params(("arbitrary",)),
    )(x)


def _rope(x, c, a, b):
    return x * c + pltpu.roll(x, LANES - ROPE_DIM // 2, 1) * a + pltpu.roll(x, ROPE_DIM // 2, 1) * b


def _rope_bwd(d, c, a, b):
    return d * c + pltpu.roll(d * a, ROPE_DIM // 2, 1) + pltpu.roll(d * b, LANES - ROPE_DIM // 2, 1)


S_CQ, S_CKV, S_KR, S_F, S_END = 0, Q_RANK, Q_RANK + KV_RANK, Q_RANK + KV_RANK + LANES, 1024
HW = N_HEADS * LANES
FW = N_HEADS * HEAD_DIM


def _rope_tables(pos_col, inv_row):
    ang = pos_col * inv_row
    cos, sin = jnp.cos(ang), jnp.sin(ang)
    lane = lax.broadcasted_iota(jnp.int32, (1, LANES), 1)
    first = (lane >= HEAD_DIM) & (lane < HEAD_DIM + ROPE_DIM // 2)
    second = (lane >= HEAD_DIM + ROPE_DIM // 2) & (lane < HEAD_DIM + ROPE_DIM)
    return (jnp.where(lane < HEAD_DIM, 1.0, jnp.where(first | second, cos, 0.0)), jnp.where(first, -sin, 0.0),
            jnp.where(second, sin, 0.0))


def mla_prep(small, g_q, g_kv, w_uq, w_ukv, pos_col, inv_row, b_f, *, name, bt=512):
    t = small.shape[0]
    bt = min(bt, t)

    def body(s_ref, gq_ref, gkv_ref, wq_ref, wkv_ref, pos_ref, inv_ref, bf_ref,
             mq_ref, mk_ref, mv_ref, lf_ref, cqn_ref, ckvn_ref, mqt_ref, mvt_ref):
        cq = s_ref[:, S_CQ:S_CKV]
        rq = lax.rsqrt(jnp.mean(cq * cq, axis=-1, keepdims=True) + EPS)
        cqn = (cq * rq * gq_ref[...]).astype(BF16)
        ckv = s_ref[:, S_CKV:S_KR]
        rkv = lax.rsqrt(jnp.mean(ckv * ckv, axis=-1, keepdims=True) + EPS)
        ckvn = (ckv * rkv * gkv_ref[...]).astype(BF16)
        cqn_ref[...] = cqn
        ckvn_ref[...] = ckvn
        tc, ta, tb = _rope_tables(pos_ref[...], inv_ref[...])
        q = jnp.dot(cqn, wq_ref[...], preferred_element_type=F32)
        kv = jnp.dot(ckvn, wkv_ref[...], preferred_element_type=F32)
        kr = _rope(s_ref[:, S_KR:S_F], tc, ta, tb)
        for h in range(N_HEADS):
            sl = slice(h * LANES, (h + 1) * LANES)
            roped = _rope(q[:, sl], tc, ta, tb)
            mq_ref[:, sl] = roped.astype(BF16)
            mqt_ref[0, sl, :] = roped.T.astype(BF16)
            mk_ref[:, sl] = (kv[:, sl] + kr).astype(BF16)
        mv_ref[...] = kv[:, HW:].astype(BF16)
        mvt_ref[0] = kv[:, HW:].T.astype(BF16)
        z = s_ref[:, S_F:S_END - LANES] + bf_ref[...]
        lf_ref[...] = jnp.minimum(z, 0.0) - jnp.log(1.0 + jnp.exp(-jnp.abs(z)))

    def row(w):
        return pl.BlockSpec((bt, w), lambda i: (i, 0))

    def full(arr):
        return pl.BlockSpec(arr.shape, lambda i: (0, 0))

    return pl.pallas_call(
        body, name=name, grid=(t // bt,),
        in_specs=[row(S_END), full(g_q), full(g_kv), full(w_uq), full(w_ukv), row(1), full(inv_row), full(b_f)],
        out_specs=[row(HW), row(HW), row(FW), row(LANES), row(Q_RANK), row(KV_RANK),
                   pl.BlockSpec((1, HW, bt), lambda i: (i, 0, 0)), pl.BlockSpec((1, FW, bt), lambda i: (i, 0, 0))],
        out_shape=[jax.ShapeDtypeStruct((t, HW), BF16)] * 2 + [jax.ShapeDtypeStruct((t, FW), BF16), jax.ShapeDtypeStruct((t, LANES), F32),
                   jax.ShapeDtypeStruct((t, Q_RANK), BF16), jax.ShapeDtypeStruct((t, KV_RANK), BF16),
                   jax.ShapeDtypeStruct((t // bt, HW, bt), BF16), jax.ShapeDtypeStruct((t // bt, FW, bt), BF16)],
        compiler_params=_params(("parallel",)),
    )(small, g_q, g_kv, w_uq, w_ukv, pos_col, inv_row, b_f)


def mla_prep_bwd(dmq, dmk, dmv, dlf, small, g_q, g_kv, w_uq, w_ukv, pos_col, inv_row, b_f, *, name, bt=512):
    t = small.shape[0]
    bt = min(bt, t)

    def body(dmq_ref, dmk_ref, dmv_ref, dlf_ref, s_ref, gq_ref, gkv_ref, wq_ref, wkv_ref, pos_ref, inv_ref, bf_ref,
             ds_ref, dq_ref, dkv_ref, dgq_ref, dgkv_ref, db_ref):
        tc, ta, tb = _rope_tables(pos_ref[...], inv_ref[...])
        lane = lax.broadcasted_iota(jnp.int32, (1, LANES), 1)
        dkr = jnp.zeros((bt, LANES), F32)
        for h in range(N_HEADS):
            sl = slice(h * LANES, (h + 1) * LANES)
            dq_ref[:, sl] = _rope_bwd(dmq_ref[:, sl], tc, ta, tb).astype(BF16)
            dkr = dkr + dmk_ref[:, sl]
        dkv_ref[:, :HW] = dmk_ref[...].astype(BF16)
        dkv_ref[:, HW:] = dmv_ref[...].astype(BF16)
        in_rope = (lane >= HEAD_DIM) & (lane < HEAD_DIM + ROPE_DIM)
        ds_ref[:, S_KR:S_F] = jnp.where(in_rope, _rope_bwd(dkr, tc, ta, tb), 0.0).astype(BF16)

        def norm_bwd(raw, g_ref, dn, dg_ref):
            r = lax.rsqrt(jnp.mean(raw * raw, axis=-1, keepdims=True) + EPS)
            u = dn * g_ref[...]
            dot = jnp.mean(u * raw, axis=-1, keepdims=True)
            dg_ref[...] += jnp.sum(dn * (raw * r), axis=0, keepdims=True)
            return r * u - raw * (r * r * r * dot)

        @pl.when(pl.program_id(0) == 0)
        def _():
            dgq_ref[...] = jnp.zeros_like(dgq_ref)
            dgkv_ref[...] = jnp.zeros_like(dgkv_ref)
            db_ref[...] = jnp.zeros_like(db_ref)

        dcqn = lax.dot_general(dq_ref[...], wq_ref[...], NT, preferred_element_type=F32)
        ds_ref[:, S_CQ:S_CKV] = norm_bwd(s_ref[:, S_CQ:S_CKV], gq_ref, dcqn, dgq_ref).astype(BF16)
        dckvn = lax.dot_general(dkv_ref[...], wkv_ref[...], NT, preferred_element_type=F32)
        ds_ref[:, S_CKV:S_KR] = norm_bwd(s_ref[:, S_CKV:S_KR], gkv_ref, dckvn, dgkv_ref).astype(BF16)
        z = s_ref[:, S_F:S_END - LANES] + bf_ref[...]
        dz = jnp.where(lane < N_HEADS, dlf_ref[...] / (1.0 + jnp.exp(z)), 0.0)
        db_ref[...] += jnp.sum(dz, axis=0, keepdims=True)
        ds_ref[:, S_F:S_END - LANES] = dz.astype(BF16)
        ds_ref[:, S_END - LANES:] = jnp.zeros((bt, LANES), BF16)

    def row(w):
        return pl.BlockSpec((bt, w), lambda i: (i, 0))

    def full(arr):
        return pl.BlockSpec(arr.shape, lambda i: (0, 0))

    def vec(w):
        return pl.BlockSpec((1, w), lambda i: (0, 0))

    return pl.pallas_call(
        body, name=name, grid=(t // bt,),
        in_specs=[row(HW), row(HW), row(FW), row(LANES), row(S_END), full(g_q), full(g_kv), full(w_uq), full(w_ukv),
                  row(1), full(inv_row), full(b_f)],
        out_specs=[row(S_END), row(HW), row(HW + FW), vec(Q_RANK), vec(KV_RANK), vec(LANES)],
        out_shape=[jax.ShapeDtypeStruct((t, S_END), BF16), jax.ShapeDtypeStruct((t, HW), BF16),
                   jax.ShapeDtypeStruct((t, HW + FW), BF16), jax.ShapeDtypeStruct((1, Q_RANK), F32),
                   jax.ShapeDtypeStruct((1, KV_RANK), F32), jax.ShapeDtypeStruct((1, LANES), F32)],
        compiler_params=_params(("arbitrary",)),
    )(dmq, dmk, dmv, dlf, small, g_q, g_kv, w_uq, w_ukv, pos_col, inv_row, b_f)


class Side:
    def __init__(self, ins, out_shapes, n_sems, first, last, mid=None):
        self.ins, self.out_shapes, self.n_sems = list(ins), list(out_shapes), n_sems
        self.first, self.mid, self.last = first, mid, last

    def specs(self):
        return [ANY] * len(self.ins), [ANY] * len(self.out_shapes)

    def sems(self):
        return [pltpu.SemaphoreType.DMA((self.n_sems,)), pltpu.SemaphoreType.DMA((self.n_sems,))]


def _lane():
    return lax.broadcasted_iota(jnp.int32, (1, LANES), 1)


def _halves(x):
    zero = jnp.zeros_like(x)
    return [jnp.where(_lane() < HEAD_DIM, x, zero), jnp.where(_lane() >= HEAD_DIM, x, zero)]


def _groups(x):
    return [x[:, :LANES], x[:, LANES:]]


def _pick_row(tile, h):
    row = lax.broadcasted_iota(jnp.int32, (tile.shape[0], 1), 0)
    return jnp.sum(jnp.where(row == h, tile, 0.0), axis=0, keepdims=True)


def _pick_lane(tile, h):
    return jnp.sum(jnp.where(_lane() == h, tile, 0.0), axis=1, keepdims=True)


def _row_halves(x):
    row = lax.broadcasted_iota(jnp.int32, (LANES, 1), 0)
    zero = jnp.zeros_like(x)
    return [jnp.where(row < HEAD_DIM, x, zero), jnp.where(row >= HEAD_DIM, x, zero)]


def _below_diagonal(s):
    r = lax.broadcasted_iota(jnp.int32, s.shape, 0)
    c = lax.broadcasted_iota(jnp.int32, s.shape, 1)
    return jnp.where(c <= r, s, -jnp.inf)


def _above_diagonal(s):
    r = lax.broadcasted_iota(jnp.int32, s.shape, 0)
    c = lax.broadcasted_iota(jnp.int32, s.shape, 1)
    return jnp.where(r <= c, s, -jnp.inf)


def _split_refs(refs, counts):
    out, at = [], 0
    for n in counts:
        out.append(refs[at:at + n])
        at += n
    return out


def flash_fwd(qt_arr, k_arr, vt_arr, f_cum, *, qoff, koff, voff, pair, scale, name, blk=512, side=None):
    t = k_arr.shape[0]
    tblk = qt_arr.shape[2]
    blk = max(min(blk, t), tblk)
    sub = blk // tblk
    nb = t // blk
    steps = PAIRS * nb
    w = LANES if pair else 2 * LANES
    has_bias = f_cum is not None
    ins = [qt_arr, k_arr, vt_arr] + ([f_cum] if has_bias else [])

    def wide(ref, first):
        parts = [ref[first + u] for u in range(sub)]
        return parts[0] if sub == 1 else jnp.concatenate(parts, axis=1)
    s_ins, s_outs = (side.ins, side.out_shapes) if side else ([], [])

    def body(*refs):
        main, si, outs, so, sems = _split_refs(refs, [len(ins), len(s_ins), 2, len(s_outs), 2 if side else 0])
        qt_ref, k_ref, vt_ref = main[:3]
        f_ref = main[3] if has_bias else None
        o_ref, st_ref = outs
        g, i = pl.program_id(0), pl.program_id(1)
        step_id = g * nb + i
        if side:
            @pl.when(step_id == 0)
            def _():
                side.first(si, so, *sems)

            if side.mid is not None:
                @pl.when(step_id == (3 * steps) // 4)
                def _():
                    side.mid(si, so, *sems)

        qt = (wide(qt_ref, 0).astype(F32) * (scale * LOG2E)).astype(BF16)
        qts = _row_halves(qt) if pair else [qt[:LANES], qt[LANES:]]

        def k_of(kk, n):
            return kk if pair else kk[:, n * LANES:(n + 1) * LANES]

        def with_ones(vt_rows):
            return jnp.concatenate([vt_rows, jnp.ones((ACC_ROWS - HEAD_DIM, vt_rows.shape[1]), BF16)], axis=0)

        def step(j, carry, diagonal):
            rows = pl.ds(pl.multiple_of(j * blk, blk), blk)
            kk = k_ref[rows, :]
            vt = wide(vt_ref, sub * j)
            out = []
            for n in range(2):
                m, acc = carry[n]
                s = jnp.dot(k_of(kk, n), qts[n], preferred_element_type=F32)
                if has_bias:
                    s = s - LOG2E * _pick_lane(f_ref[rows, :], 2 * g + n)
                if diagonal:
                    s = _above_diagonal(s)
                m_new = jnp.maximum(m, jnp.max(s, axis=0, keepdims=True))
                p = jnp.exp2(s - m_new).astype(BF16)
                out.append((m_new, jnp.exp2(m - m_new) * acc
                            + jnp.dot(with_ones(vt[n * HEAD_DIM:(n + 1) * HEAD_DIM]), p, preferred_element_type=F32)))
            return tuple(out)

        def diagonal_in_halves(carry):
            h = tblk
            halves = [pl.ds(pl.multiple_of(i * blk, blk), h), pl.ds(pl.multiple_of(i * blk + h, h), h)]
            k_top, k_bot = k_ref[halves[0], :], k_ref[halves[1], :]
            vt_top, vt_bot = vt_ref[sub * i], vt_ref[sub * i + 1]
            out = []
            for n in range(2):
                m, acc = carry[n]
                heads = slice(n * HEAD_DIM, (n + 1) * HEAD_DIM)
                s_top = jnp.dot(k_of(k_top, n), qts[n], preferred_element_type=F32)
                s_bot = jnp.dot(k_of(k_bot, n), qts[n][:, h:], preferred_element_type=F32)
                if has_bias:
                    s_top = s_top - LOG2E * _pick_lane(f_ref[halves[0], :], 2 * g + n)
                    s_bot = s_bot - LOG2E * _pick_lane(f_ref[halves[1], :], 2 * g + n)
                s_top, s_bot = _above_diagonal(s_top), _above_diagonal(s_bot)
                m_top = jnp.maximum(m, jnp.max(s_top, axis=0, keepdims=True))
                m_new = jnp.concatenate([m_top[:, :h], jnp.maximum(m_top[:, h:], jnp.max(s_bot, axis=0, keepdims=True))], axis=1)
                p_top = jnp.exp2(s_top - m_new).astype(BF16)
                p_bot = jnp.exp2(s_bot - m_new[:, h:]).astype(BF16)
                late = jnp.concatenate([jnp.zeros((ACC_ROWS, h), F32),
                                        jnp.dot(with_ones(vt_bot[heads]), p_bot, preferred_element_type=F32)], axis=1)
                out.append((m_new, jnp.exp2(m - m_new) * acc
                            + jnp.dot(with_ones(vt_top[heads]), p_top, preferred_element_type=F32) + late))
            return tuple(out)

        init = tuple((jnp.full((1, blk), -jnp.inf, F32), jnp.zeros((ACC_ROWS, blk), F32)) for _ in range(2))
        carry = lax.fori_loop(0, i, lambda j, c: step(j, c, False), init)
        (ma, acca), (mb, accb) = diagonal_in_halves(carry) if sub == 2 else step(i, carry, True)
        la, lb = acca[HEAD_DIM:HEAD_DIM + 1], accb[HEAD_DIM:HEAD_DIM + 1]
        o_ref[...] = jnp.concatenate([acca[:HEAD_DIM] / la, accb[:HEAD_DIM] / lb], axis=0).T
        row = lax.broadcasted_iota(jnp.int32, (LANES, 1), 0)
        st_ref[0] = jnp.where(row == 0, ma + jnp.log2(la), jnp.where(row == 1, mb + jnp.log2(lb), 0.0)).T
        if side:
            @pl.when(step_id == steps - 1)
            def _():
                side.last(si, so, *sems)

    in_specs = [pl.BlockSpec((sub, w, tblk), lambda g, i: (i, qoff + g, 0)), pl.BlockSpec((t, w), lambda g, i: (0, koff + g)),
                pl.BlockSpec((t // tblk, LANES, tblk), lambda g, i: (0, voff + g, 0))]
    if has_bias:
        in_specs.append(pl.BlockSpec((t, LANES), lambda g, i: (0, 0)))
    s_in_specs, s_out_specs = side.specs() if side else ([], [])
    return pl.pallas_call(
        body, name=name, grid=(PAIRS, nb), in_specs=in_specs + s_in_specs,
        out_specs=[pl.BlockSpec((blk, LANES), lambda g, i: (i, g)), pl.BlockSpec((1, blk, LANES), lambda g, i: (g, i, 0))]
        + s_out_specs,
        out_shape=[jax.ShapeDtypeStruct((t, PAIRS * LANES), F32), jax.ShapeDtypeStruct((PAIRS, t, LANES), F32)] + list(s_outs),
        scratch_shapes=side.sems() if side else [],
        compiler_params=_params(("arbitrary", "arbitrary")),
    )(*ins, *s_ins)


def mix_norm(fo, mo, g_fo, g_mo, *, name, bt=512):
    t, d = fo.shape
    bt = min(bt, t)

    def body(fo_ref, mo_ref, gf_ref, gm_ref, o_ref):
        for n, (x_ref, g_ref) in enumerate(((fo_ref, gf_ref), (mo_ref, gm_ref))):
            xv = x_ref[...]
            r = lax.rsqrt(jnp.mean(xv * xv, axis=-1, keepdims=True) + EPS)
            o_ref[:, n * d:(n + 1) * d] = (xv * r * g_ref[...]).astype(BF16)

    row = pl.BlockSpec((bt, d), lambda i: (i, 0))
    vec = pl.BlockSpec((1, d), lambda i: (0, 0))
    return pl.pallas_call(
        body, name=name, grid=(t // bt,), in_specs=[row, row, vec, vec],
        out_specs=pl.BlockSpec((bt, 2 * d), lambda i: (i, 0)),
        out_shape=jax.ShapeDtypeStruct((t, 2 * d), BF16),
        compiler_params=_params(("parallel",)),
    )(fo, mo, g_fo, g_mo)


def mix_norm_bwd(dx1b, w_o, fo, mo, g_fo, g_mo, st_f, st_m, *, name, bt=512, side=None):
    t, d = fo.shape
    bt = min(bt, t)

    def body(dx_in_ref, w_ref, fo_ref, mo_ref, gf_ref, gm_ref, sf_ref, sm_ref, dfo_ref, dmo_ref, dgf_ref, dgm_ref,
             sfo_ref, smo_ref, dfot_ref, dmot_ref):
        @pl.when(pl.program_id(0) == 0)
        def _():
            dgf_ref[...] = jnp.zeros_like(dgf_ref)
            dgm_ref[...] = jnp.zeros_like(dgm_ref)

        dmixed = lax.dot_general(dx_in_ref[...], w_ref[...], NT, preferred_element_type=F32)
        groups = ((fo_ref, gf_ref, dfo_ref, dgf_ref, sf_ref, sfo_ref, dfot_ref),
                  (mo_ref, gm_ref, dmo_ref, dgm_ref, sm_ref, smo_ref, dmot_ref))
        for n, (x_ref, g_ref, dx_ref, dg_ref, st_ref, sto_ref, dxt_ref) in enumerate(groups):
            xv = x_ref[...]
            dhv = dmixed[:, n * d:(n + 1) * d]
            r = lax.rsqrt(jnp.mean(xv * xv, axis=-1, keepdims=True) + EPS)
            u = dhv * g_ref[...]
            dxf = r * u - xv * (r * r * r * jnp.mean(u * xv, axis=-1, keepdims=True))
            dxb = dxf.astype(BF16)
            dx_ref[...] = dxb
            dxt_ref[0] = dxf.T.astype(BF16)
            dg_ref[...] += jnp.sum(dhv * (xv * r), axis=0, keepdims=True)
            prod = xv * dxb.astype(F32)
            for g in range(PAIRS):
                grp = prod[:, g * LANES:(g + 1) * LANES]
                da = jnp.sum(jnp.where(_lane() < HEAD_DIM, grp, 0.0), axis=1, keepdims=True)
                db = jnp.sum(jnp.where(_lane() >= HEAD_DIM, grp, 0.0), axis=1, keepdims=True)
                sto_ref[g] = jnp.where(_lane() == 2, da, jnp.where(_lane() == 3, db, st_ref[g]))

    row = pl.BlockSpec((bt, d), lambda i: (i, 0))
    vec = pl.BlockSpec((1, d), lambda i: (0, 0))
    stat = pl.BlockSpec((PAIRS, bt, LANES), lambda i: (0, i, 0))
    return gridded(
        body, name=name, grid=(t // bt,),
        in_specs=[pl.BlockSpec((bt, dx1b.shape[1]), lambda i: (i, 0)), pl.BlockSpec(w_o.shape, lambda i: (0, 0)),
                  row, row, vec, vec, stat, stat],
        out_specs=[row, row, vec, vec, stat, stat] + [pl.BlockSpec((1, d, bt), lambda i: (i, 0, 0))] * 2,
        out_shape=[jax.ShapeDtypeStruct((t, d), BF16)] * 2 + [jax.ShapeDtypeStruct((1, d), F32)] * 2
        + [jax.ShapeDtypeStruct(st_f.shape, F32)] * 2 + [jax.ShapeDtypeStruct((t // bt, d, bt), BF16)] * 2,
        ins=[dx1b, w_o, fo, mo, g_fo, g_mo, st_f, st_m], semantics=("arbitrary",), side=side)


def flash_bwd(q_arr, qt_arr, k_arr, v_arr, do_arr, dot_arr, st, f_blocks, *, qoff, koff, voff, pair, scale, name, qblk=1024,
              side=None):
    t = q_arr.shape[0]
    blk = qt_arr.shape[2]
    qblk = max(min(qblk, t), blk)
    sub = qblk // blk
    nb, nbq = t // blk, t // qblk
    w = LANES if pair else 2 * LANES
    hw = w // 2
    has_bias = f_blocks is not None
    split = _halves if pair else _groups
    ins = [q_arr, qt_arr, k_arr, v_arr, do_arr, dot_arr, st] + ([f_blocks] if has_bias else [])
    n_out = 4 if has_bias else 3
    s_ins, s_outs = (side.ins, side.out_shapes) if side else ([], [])

    def wide(ref, first, count):
        parts = [ref[first + u] for u in range(count)]
        return parts[0] if count == 1 else jnp.concatenate(parts, axis=1)

    def body(*refs):
        main, si, outs, so, sems = _split_refs(refs, [len(ins), len(s_ins), n_out, len(s_outs), 2 if side else 0])
        q_ref, qt_ref, k_ref, v_ref, do_ref, dot_ref, st_ref = main[:7]
        dq_ref, dk_ref, dv_ref = outs[:3]
        g, j = pl.program_id(0), pl.program_id(1)
        step_id = g * nb + j
        if side:
            @pl.when(step_id == 0)
            def _():
                side.first(si, so, *sems)

        @pl.when(j == 0)
        def _():
            dq_ref[...] = jnp.zeros_like(dq_ref)

        kk, vv = k_ref[...], v_ref[...]
        ks = [kk, kk] if pair else _groups(kk)
        if has_bias:
            f_ref, df_ref = main[7], outs[3]
            fk = [LOG2E * _pick_row(f_ref[0], 2 * g + n) for n in range(2)]

            @pl.when(step_id == 0)
            def _():
                df_ref[...] = jnp.zeros_like(df_ref)

        def step(tb, count, carry, diagonal):
            rows = pl.ds(pl.multiple_of(tb * blk, blk), count * blk)
            qs = split((q_ref[rows, :].astype(F32) * (scale * LOG2E)).astype(BF16))
            qt = (wide(qt_ref, tb, count).astype(F32) * (scale * LOG2E)).astype(BF16)
            dos = [h.astype(BF16) for h in _halves(do_ref[rows, :].astype(F32))]
            dot = wide(dot_ref, tb, count)
            stats = st_ref[0, rows, :]
            new, dqs, row_sums = [], [], []
            for n in range(2):
                dkt, dvt, dfk = carry[n]
                s = lax.dot_general(qs[n], ks[n], NT, preferred_element_type=F32)
                if has_bias:
                    s = s - fk[n]
                if diagonal:
                    s = _below_diagonal(s)
                p = jnp.exp2(s - stats[:, n:n + 1])
                dp = lax.dot_general(dos[n], vv, NT, preferred_element_type=F32)
                ds = p * (dp - stats[:, 2 + n:3 + n])
                dsb = ds.astype(BF16)
                dvt = dvt + jnp.dot(dot[n * HEAD_DIM:(n + 1) * HEAD_DIM], p.astype(BF16), preferred_element_type=F32)
                dkt = dkt + jnp.dot(qt[n * hw:(n + 1) * hw], dsb, preferred_element_type=F32)
                dqs.append(jnp.dot(dsb, ks[n], preferred_element_type=F32))
                if has_bias:
                    dfk = dfk - jnp.sum(ds, axis=0, keepdims=True)
                    row_sums.append(jnp.sum(ds, axis=1, keepdims=True))
                new.append((dkt, dvt, dfk))
            dq = (jnp.where(_lane() < HEAD_DIM, dqs[0], dqs[1]) if pair else jnp.concatenate(dqs, axis=1)) * scale
            if has_bias:
                df_ref[rows, :] += jnp.where(_lane() == 2 * g, row_sums[0], jnp.where(_lane() == 2 * g + 1, row_sums[1], 0.0))
            dq_ref[rows, :] += dq
            return tuple(new)

        init = tuple((jnp.zeros((hw, blk), F32), jnp.zeros((HEAD_DIM, blk), F32), jnp.zeros((1, blk), F32)) for _ in range(2))
        carry = step(j, 1, init, True)
        whole = j // sub + 1
        carry = lax.fori_loop(j + 1, whole * sub, lambda tb, c: step(tb, 1, c, False), carry)
        (dka, dva, dfa), (dkb, dvb, dfb) = lax.fori_loop(whole, nbq, lambda i, c: step(i * sub, sub, c, False), carry)
        dk_ref[...] = (jnp.concatenate([dka, dkb], axis=0).T * LN2).astype(BF16)
        dv_ref[...] = jnp.concatenate([dva, dvb], axis=0).T.astype(BF16)
        if has_bias:
            row = lax.broadcasted_iota(jnp.int32, (LANES, 1), 0)
            by_head = jnp.where(row == 2 * g, dfa, jnp.where(row == 2 * g + 1, dfb, 0.0))
            df_ref[pl.ds(pl.multiple_of(j * blk, blk), blk), :] += by_head.T
        if side:
            @pl.when(step_id == PAIRS * nb - 1)
            def _():
                side.last(si, so, *sems)

    in_specs = [pl.BlockSpec((t, w), lambda g, j: (0, qoff + g)), pl.BlockSpec((nb, w, blk), lambda g, j: (0, qoff + g, 0)),
                pl.BlockSpec((blk, w), lambda g, j: (j, koff + g)), pl.BlockSpec((blk, LANES), lambda g, j: (j, voff + g)),
                pl.BlockSpec((t, LANES), lambda g, j: (0, g)), pl.BlockSpec((nb, LANES, blk), lambda g, j: (0, g, 0)),
                pl.BlockSpec((1, t, LANES), lambda g, j: (g, 0, 0))]
    out_specs = [pl.BlockSpec((t, w), lambda g, j: (0, g)), pl.BlockSpec((blk, w), lambda g, j: (j, g)),
                 pl.BlockSpec((blk, LANES), lambda g, j: (j, g))]
    out_shape = [jax.ShapeDtypeStruct((t, PAIRS * w), F32), jax.ShapeDtypeStruct((t, PAIRS * w), BF16),
                 jax.ShapeDtypeStruct((t, PAIRS * LANES), BF16)]
    if has_bias:
        in_specs.append(pl.BlockSpec((1, N_HEADS, blk), lambda g, j: (j, 0, 0)))
        out_specs.append(pl.BlockSpec((t, LANES), lambda g, j: (0, 0)))
        out_shape.append(jax.ShapeDtypeStruct((t, LANES), F32))
    s_in_specs, s_out_specs = side.specs() if side else ([], [])
    return pl.pallas_call(
        body, name=name, grid=(PAIRS, nb), in_specs=in_specs + s_in_specs, out_specs=out_specs + s_out_specs,
        out_shape=out_shape + list(s_outs), scratch_shapes=side.sems() if side else [],
        compiler_params=_params(("arbitrary", "arbitrary")),
    )(*ins, *s_ins)


def _adamw_math(w, g, m, v):
    nm = ADAM_B1 * m + (1.0 - ADAM_B1) * g
    nv = ADAM_B2 * v + (1.0 - ADAM_B2) * (g * g)
    m_hat = nm * (1.0 / (1.0 - ADAM_B1 ** ADAM_STEP))
    v_hat = nv * (1.0 / (1.0 - ADAM_B2 ** ADAM_STEP))
    return -ADAM_LR * (m_hat / (jnp.sqrt(v_hat) + ADAM_EPS) + ADAM_WD * w), nm, nv


def adamw_vectors(ws, g_rows, ms, vs, *, name):
    k = len(ws)

    def body(g_ref, *refs):
        w_refs, m_refs, v_refs, outs = _split_refs(refs, [k, k, k, 4 * k])
        for i in range(k):
            g = g_ref[i:i + 1, :w_refs[i].shape[1]]
            outs[4 * i][...] = g
            outs[4 * i + 1][...], outs[4 * i + 2][...], outs[4 * i + 3][...] = _adamw_math(
                w_refs[i][...], g, m_refs[i][...], v_refs[i][...])

    flat = pl.pallas_call(
        body, name=name, out_shape=[jax.ShapeDtypeStruct(w.shape, F32) for w in ws for _ in range(4)],
        compiler_params=_params(),
    )(g_rows, *ws, *ms, *vs)
    return [flat[4 * i:4 * i + 4] for i in range(k)]


def adamw_whole(w, g, m, v, *, name):
    def body(w_ref, g_ref, m_ref, v_ref, d_ref, nm_ref, nv_ref):
        d_ref[...], nm_ref[...], nv_ref[...] = _adamw_math(w_ref[...], g_ref[...], m_ref[...], v_ref[...])

    return pl.pallas_call(body, name=name, out_shape=[jax.ShapeDtypeStruct(w.shape, F32)] * 3,
                          compiler_params=_params())(w, g, m, v)


def adamw_halves(w, g_mine, g_other, m, v, core, *, name):
    _, k, n = w.shape
    hk = k // 2
    r = min(hk, max(8, ADAMW_BLOCK // 2 // n))
    blocks = [(s, mine) for s in range(hk // r) for mine in (True, False)]

    def body(c_ref, w_hbm, gm_hbm, go_hbm, m_hbm, v_hbm, g_out, d_out, nm_out, nv_out,
             wb, gb, mb, vb, db, nmb, nvb, in_sem, out_sem):
        def rows(s, mine):
            half = c_ref[0] if mine else 1 - c_ref[0]
            return pl.ds(pl.multiple_of(half * hk + s * r, 8), r)

        def reads(i):
            s, mine = blocks[i]
            at = rows(s, mine)
            srcs = [w_hbm.at[0, at], (gm_hbm if mine else go_hbm).at[pl.ds(s * r, r)], m_hbm.at[0, at], v_hbm.at[0, at]]
            return [pltpu.make_async_copy(src, buf.at[at], in_sem.at[4 * i + j])
                    for j, (src, buf) in enumerate(zip(srcs, (wb, gb, mb, vb)))]

        def writes(i):
            at = rows(*blocks[i])
            return [pltpu.make_async_copy(buf.at[at], dst.at[0, at], out_sem.at[4 * i + j])
                    for j, (buf, dst) in enumerate(zip((gb, db, nmb, nvb), (g_out, d_out, nm_out, nv_out)))]

        for i in range(len(blocks)):
            for cp in reads(i):
                cp.start()
        for i in range(len(blocks)):
            for cp in reads(i):
                cp.wait()
            at = rows(*blocks[i])
            db[at], nmb[at], nvb[at] = _adamw_math(wb[at], gb[at], mb[at], vb[at])
            for cp in writes(i):
                cp.start()
        for i in range(len(blocks)):
            for cp in writes(i):
                cp.wait()

    return pl.pallas_call(
        body, name=name,
        in_specs=[pl.BlockSpec(memory_space=pltpu.SMEM)] + [ANY] * 5, out_specs=[ANY] * 4,
        out_shape=[jax.ShapeDtypeStruct(w.shape, F32)] * 4,
        scratch_shapes=[pltpu.VMEM((k, n), F32)] * 7 + [pltpu.SemaphoreType.DMA((4 * len(blocks),))] * 2,
        compiler_params=_params(),
    )(core, w, g_mine, g_other, m, v)


def adamw_halves_t(wt, mt, vt, gt_mine, gt_other, core, *, name, bc=512):
    n, _, k = wt.shape
    nh = k // 2 // bc

    def body(c_ref, w_ref, m_ref, v_ref, gm_ref, go_ref, g_out, d_ref, nm_ref, nv_ref):
        gv = jnp.where(pl.program_id(0) == c_ref[0], gm_ref[...], go_ref[...])
        g_out[:, 0, :] = gv
        d_ref[:, 0, :], nm_ref[:, 0, :], nv_ref[:, 0, :] = _adamw_math(w_ref[:, 0, :], gv, m_ref[:, 0, :], v_ref[:, 0, :])

    full = pl.BlockSpec((n, 1, bc), lambda hb, i, c: (0, 0, hb * nh + i))
    half = pl.BlockSpec((n, bc), lambda hb, i, c: (0, i))
    return pl.pallas_call(
        body, name=name,
        grid_spec=pltpu.PrefetchScalarGridSpec(num_scalar_prefetch=1, grid=(2, nh), in_specs=[full, full, full, half, half],
                                               out_specs=[full] * 4),
        out_shape=[jax.ShapeDtypeStruct(wt.shape, F32)] * 4,
        compiler_params=_params(("parallel", "parallel")),
    )(core, wt, mt, vt, gt_mine, gt_other)


def add_pair(dw, recv, core, *, name):
    n4, k, n = dw.shape
    half = (1, k // 2, n) if split_axis(k) == 0 else (1, k, n // 2)
    mine = (lambda q, c: (q, c[0], 0)) if split_axis(k) == 0 else (lambda q, c: (q, 0, c[0]))

    def body(c_ref, a_ref, b_ref, o_ref):
        o_ref[...] = (a_ref[...] + b_ref[...].astype(F32)).astype(BF16)

    return pl.pallas_call(
        body, name=name,
        grid_spec=pltpu.PrefetchScalarGridSpec(
            num_scalar_prefetch=1, grid=(n4,),
            in_specs=[pl.BlockSpec(half, mine), pl.BlockSpec(half, lambda q, c: (q, 0, 0))],
            out_specs=pl.BlockSpec(half, lambda q, c: (q, 0, 0))),
        out_shape=jax.ShapeDtypeStruct((n4,) + half[1:], BF16),
        compiler_params=_params(("parallel",)),
    )(core, dw, recv)


def sum_chips(parts, *, name):
    n4, r, n = parts.shape
    if r % 16 == 0:
        br, bc = _row_block(r), n
    else:
        br, bc = r, LANES

    def body(p_ref, o_ref):
        acc = p_ref[0].astype(F32)
        for q in range(1, n4):
            acc = acc + p_ref[q].astype(F32)
        o_ref[...] = acc

    return pl.pallas_call(
        body, name=name, grid=(r // br, n // bc),
        in_specs=[pl.BlockSpec((n4, br, bc), lambda i, j: (0, i, j))], out_specs=pl.BlockSpec((br, bc), lambda i, j: (i, j)),
        out_shape=jax.ShapeDtypeStruct((r, n), F32),
        compiler_params=_params(("parallel", "parallel")),
    )(parts)


ANY = pl.BlockSpec(memory_space=pl.ANY)


def _place():
    x, y, c = lax.axis_index("x"), lax.axis_index("y"), lax.axis_index("c")
    chips = [(1 - x, y), (x, 1 - y), (1 - x, 1 - y)]
    return x, y, c, chips


def _copy(src, dst, send_sems, recv_sems, k, to):
    return pltpu.make_async_remote_copy(src_ref=src, dst_ref=dst, send_sem=send_sems.at[k], recv_sem=recv_sems.at[k],
                                        device_id=to, device_id_type=MESH)


def split_axis(rows):
    return 0 if rows % 32 == 0 else 1


def _half(ref, lead, hf):
    rows, cols = ref.shape[-2:]
    if split_axis(rows) == 0:
        at = (pl.ds(hf * (rows // 2), rows // 2), slice(None))
    else:
        at = (slice(None), pl.ds(hf * (cols // 2), cols // 2))
    return ref.at[at] if lead is None else ref.at[(lead,) + at]


def _gather_first(srcs, dsts, ssems, rsems):
    x, y, c, chips = _place()
    for ti, (s, d) in enumerate(zip(srcs, dsts)):
        for j, (cx, cy) in enumerate(chips):
            _copy(_half(s, None, c), _half(d, 2 * x + y, c), ssems, rsems, 3 * ti + j, (cx, cy, c)).start()


def _gather_mid(srcs, dsts, ssems, rsems):
    x, y, c, chips = _place()
    n1 = 3 * len(srcs)
    for ti, d in enumerate(dsts):
        for j, (cx, cy) in enumerate(chips):
            landed = _half(d, 2 * cx + cy, c)
            _copy(landed, landed, ssems, rsems, 3 * ti + j, (cx, cy, c)).wait_recv()
            _copy(landed, landed, ssems, rsems, n1 + 3 * ti + j, (x, y, 1 - c)).start()


def _gather_last(srcs, dsts, ssems, rsems):
    x, y, c, chips = _place()
    n1 = 3 * len(srcs)
    for ti, (s, d) in enumerate(zip(srcs, dsts)):
        for j, (cx, cy) in enumerate(chips):
            other = _half(d, 2 * cx + cy, 1 - c)
            _copy(other, other, ssems, rsems, n1 + 3 * ti + j, (x, y, 1 - c)).wait_recv()
        for j, (cx, cy) in enumerate(chips):
            mine = _half(s, None, c)
            _copy(mine, mine, ssems, rsems, 3 * ti + j, (cx, cy, c)).wait_send()
            _copy(mine, mine, ssems, rsems, n1 + 3 * ti + j, (x, y, 1 - c)).wait_send()


def gather_side(shards):
    return Side(shards, [jax.ShapeDtypeStruct((N_CHIPS,) + s.shape, s.dtype) for s in shards], 6 * len(shards),
                _gather_first, _gather_last, _gather_mid)


def _scatter_first(srcs, dsts, ssems, rsems):
    x, y, c, chips = _place()
    for ti, (s, d) in enumerate(zip(srcs, dsts)):
        for j, (cx, cy) in enumerate(chips):
            _copy(s.at[2 * cx + cy], d.at[2 * x + y], ssems, rsems, 3 * ti + j, (cx, cy, c)).start()


def _scatter_last(srcs, dsts, ssems, rsems):
    x, y, c, chips = _place()
    for ti, (s, d) in enumerate(zip(srcs, dsts)):
        for j, (cx, cy) in enumerate(chips):
            _copy(s.at[2 * cx + cy], d.at[2 * cx + cy], ssems, rsems, 3 * ti + j, (cx, cy, c)).wait_recv()
        for j, (cx, cy) in enumerate(chips):
            _copy(s.at[2 * cx + cy], d.at[2 * cx + cy], ssems, rsems, 3 * ti + j, (cx, cy, c)).wait_send()


def scatter_side(parts):
    return Side(parts, [jax.ShapeDtypeStruct(p.shape, p.dtype) for p in parts], 3 * len(parts), _scatter_first, _scatter_last)


def run_side(side, *, name):
    n_in, n_out = len(side.ins), len(side.out_shapes)

    def body(*refs):
        si, so, sems = _split_refs(refs, [n_in, n_out, 2])
        side.first(si, so, *sems)
        if side.mid is not None:
            side.mid(si, so, *sems)
        side.last(si, so, *sems)

    in_specs, out_specs = side.specs()
    return pl.pallas_call(body, name=name, in_specs=in_specs, out_specs=out_specs, out_shape=side.out_shapes,
                          scratch_shapes=side.sems())(*side.ins)


def _swap_first(srcs, dsts, ssems, rsems):
    x, y, c, _ = _place()
    for k, (s, d) in enumerate(zip(srcs, dsts)):
        _copy(s, d, ssems, rsems, k, (x, y, 1 - c)).start()


def _swap_last(srcs, dsts, ssems, rsems):
    x, y, c, _ = _place()
    for k, (s, d) in enumerate(zip(srcs, dsts)):
        _copy(s, d, ssems, rsems, k, (x, y, 1 - c)).wait()


def swap_side(xs):
    return Side(xs, [jax.ShapeDtypeStruct(a.shape, a.dtype) for a in xs], len(xs), _swap_first, _swap_last)


def _swap_halves(srcs, dsts, ssems, rsems):
    x, y, c, _ = _place()
    for k, (s, d) in enumerate(zip(srcs, dsts)):
        hk = s.shape[1] // 2
        yield _copy(s.at[:, pl.ds((1 - c) * hk, hk), :], d, ssems, rsems, k, (x, y, 1 - c))


def _swap_halves_first(srcs, dsts, ssems, rsems):
    for cp in _swap_halves(srcs, dsts, ssems, rsems):
        cp.start()


def _swap_halves_last(srcs, dsts, ssems, rsems):
    for cp in _swap_halves(srcs, dsts, ssems, rsems):
        cp.wait()


def swap_halves_side(xs):
    return Side(xs, [jax.ShapeDtypeStruct((a.shape[0], a.shape[1] // 2, a.shape[2]), a.dtype) for a in xs], len(xs),
                _swap_halves_first, _swap_halves_last)


def allreduce_small(s):
    n_dev = 8

    def body(s_ref, out_ref, buf, send_sems, recv_sems):
        x, y, c, _ = _place()
        me = 4 * x + 2 * y + c
        buf[me] = s_ref[...]
        sends = []
        for k in range(1, n_dev):
            px = 1 - x if k & 4 else x
            py = 1 - y if k & 2 else y
            pc = 1 - c if k & 1 else c
            cp = _copy(s_ref, buf.at[me], send_sems, recv_sems, k - 1, (px, py, pc))
            cp.start()
            sends.append((cp, 4 * px + 2 * py + pc))
        for k, (cp, peer) in enumerate(sends):
            _copy(s_ref, buf.at[peer], send_sems, recv_sems, k, (x, y, c)).wait_recv()
        for cp, _ in sends:
            cp.wait_send()
        acc = buf[0]
        for d in range(1, n_dev):
            acc = acc + buf[d]
        out_ref[...] = acc

    vm = pl.BlockSpec(memory_space=pltpu.VMEM)
    return pl.pallas_call(
        body, name="allreduce_small", in_specs=[vm], out_specs=vm,
        out_shape=jax.ShapeDtypeStruct(s.shape, F32),
        scratch_shapes=[pltpu.VMEM((n_dev,) + s.shape, F32), pltpu.SemaphoreType.DMA((n_dev - 1,)),
                        pltpu.SemaphoreType.DMA((n_dev - 1,))],
    )(s)


def join_cols(sm):
    n4, k, n = sm.shape
    return sm.transpose(1, 0, 2).reshape(k, n4 * n)


def split_cols(full):
    k, n = full.shape
    return full.reshape(k, N_CHIPS, n // N_CHIPS).transpose(1, 0, 2)


def _pad_heads(w, width):
    lead, heads = w.shape[:-1], w.shape[-1] // width
    w = w.reshape(lead + (heads, width))
    return jnp.pad(w, [(0, 0)] * len(lead) + [(0, 0), (0, LANES - width)]).reshape(lead + (heads * LANES,))


def _unpad_heads(w, width):
    lead, heads = w.shape[:-1], w.shape[-1] // LANES
    return w.reshape(lead + (heads, LANES))[..., :width].reshape(lead + (heads * width,))


O_F = 3 * FW
O_CQ = O_F + N_HEADS
O_CKV = O_CQ + Q_RANK
O_KR = O_CKV + KV_RANK
O_END = O_KR + ROPE_DIM


def split_w_in_t(w_sm):
    n4, r, d = w_sm.shape
    segments = [(0, O_F, 0, 0), (O_F, N_HEADS, 1, S_F), (O_CQ, Q_RANK, 1, S_CQ), (O_CKV, KV_RANK, 1, S_CKV),
                (O_KR, ROPE_DIM, 1, S_KR + HEAD_DIM)]
    moves = []
    for v0, n, dst, d0 in segments:
        for q in range(n4):
            a, b = max(v0, q * r), min(v0 + n, (q + 1) * r)
            if a < b:
                assert a % 2 == 0 and b % 2 == 0 and (d0 + a - v0) % 2 == 0
                moves.append((q, (a - q * r) // 2, (b - a) // 2, dst, (d0 + a - v0) // 2))

    def body(w_ref, qkv_ref, small_ref, src, dst_qkv, dst_small):
        for q in range(n4):
            src[q] = pltpu.bitcast(w_ref[q], jnp.uint32)
        dst_small[...] = jnp.zeros(dst_small.shape, jnp.uint32)
        for q, a, n, dst, b in moves:
            (dst_qkv, dst_small)[dst][pl.ds(b, n), :] = src[q, pl.ds(a, n), :]
        qkv_ref[...] = pltpu.bitcast(dst_qkv[...], w_sm.dtype)
        small_ref[...] = pltpu.bitcast(dst_small[...], w_sm.dtype)

    return pl.pallas_call(
        body, name="split_w_in",
        out_shape=[jax.ShapeDtypeStruct((O_F, d), w_sm.dtype), jax.ShapeDtypeStruct((S_END, d), w_sm.dtype)],
        scratch_shapes=[pltpu.VMEM((n4, r // 2, d), jnp.uint32), pltpu.VMEM((O_F // 2, d), jnp.uint32),
                        pltpu.VMEM((S_END // 2, d), jnp.uint32)],
        compiler_params=_params(),
    )(w_sm)


def join_w_in_t(d_qkv_t, d_small_t):
    small = len(d_qkv_t)
    segments = [(n, 0, n * FW, FW) for n in range(small)]
    segments += [(small, S_F, O_F, N_HEADS), (small, S_CQ, O_CQ, Q_RANK), (small, S_CKV, O_CKV, KV_RANK),
                 (small, S_KR + HEAD_DIM, O_KR, ROPE_DIM)]
    r, d = O_END // N_CHIPS, d_small_t.shape[1]

    def body(*refs):
        o_ref = refs[-1]
        for src, s0, v0, n in segments:
            for q in range(N_CHIPS):
                a, b = max(v0, q * r), min(v0 + n, (q + 1) * r)
                if a < b:
                    o_ref[q, pl.ds(a - q * r, b - a), :] = refs[src][pl.ds(s0 + a - v0, b - a), :]

    return pl.pallas_call(
        body, name="join_w_in", out_shape=jax.ShapeDtypeStruct((N_CHIPS, r, d), F32), compiler_params=_params(),
    )(*d_qkv_t, d_small_t)


def rope_inputs(pos):
    inv_freq = ROPE_THETA ** (-jnp.arange(0, ROPE_DIM, 2, dtype=F32) / ROPE_DIM)
    inv_row = jnp.concatenate([jnp.zeros((HEAD_DIM,), F32), inv_freq, inv_freq, jnp.zeros((LANES - HEAD_DIM - ROPE_DIM,), F32)])
    return pos.astype(F32)[:, None], inv_row[None, :]


def _pad_lanes(v, n):
    return jnp.pad(v, ((0, 0), (0, n - v.shape[1])))


ATTN_BLK = 512
ATTN_FWD_BLK = 1024
ATTN_BWD_QBLK = 1024
ACC_ROWS = HEAD_DIM + 16


def local_step(xs, pos, tgt, gains, early_weights, late_weights, early_side=None, fwd_sides=(None, None), reduction=None):
    g_attn, b_forget, g_q, g_kv, g_fo, g_mo, g_mlp, g_fin = gains
    t = xs.shape[0]
    blk = min(ATTN_BLK, t)
    fox_scale = 1.0 / (HEAD_DIM ** 0.5)
    mla_scale = 1.0 / ((HEAD_DIM + ROPE_DIM) ** 0.5)

    h1, *gathered = rmsnorm(xs, g_attn, out_dtype=BF16, name="norm_attn", side=early_side)
    w_in_t, w_uq_p, w_ukv = early_weights(gathered)
    w_qkv_t, w_small_t = split_w_in_t(w_in_t)
    kv = w_ukv.reshape(KV_RANK, N_HEADS, 2 * HEAD_DIM)
    w_ukv_p = jnp.concatenate([_pad_heads(kv[:, :, :HEAD_DIM].reshape(KV_RANK, FW), HEAD_DIM),
                               kv[:, :, HEAD_DIM:].reshape(KV_RANK, FW)], axis=1)
    b_f = _pad_lanes(b_forget, LANES)
    pos_col, inv_row = rope_inputs(pos)

    qkv, qkv_t = mm(h1, w_qkv_t, trans_b=True, out_dtypes=[BF16], t_blk=blk, name="proj_qkv", bn=1536)
    small, = mm(h1, w_small_t, trans_b=True, out_dtypes=[F32], name="proj_small")
    mq, mk, mv, lf, cqn, ckvn, mq_t, mv_t = mla_prep(small, g_q, g_kv, w_uq_p, w_ukv_p, pos_col, inv_row, b_f,
                                                     name="mla_prep", bt=blk)
    f_cum = cumsum_rows(lf, reverse=False, name="gate_cumsum")
    f_blocks = f_cum[:, :N_HEADS].reshape(t // blk, blk, N_HEADS).transpose(0, 2, 1)
    fo, st_f, *gathered = flash_fwd(qkv_t, qkv, qkv_t, f_cum, qoff=0, koff=PAIRS, voff=2 * PAIRS, pair=True,
                                    scale=fox_scale, name="fox_fwd", blk=ATTN_FWD_BLK, side=fwd_sides[0])
    mo, st_m, *more = flash_fwd(mq_t, mk, mv_t, None, qoff=0, koff=0, voff=0, pair=False, scale=mla_scale, name="mla_fwd",
                                blk=ATTN_FWD_BLK, side=fwd_sides[1])
    w_o, w_up, w_down = late_weights(gathered + more)
    mixed = mix_norm(fo, mo, g_fo, g_mo, name="norm_mix")

    def inv_rms(v):
        return lax.rsqrt(jnp.mean(v * v, axis=-1, keepdims=True) + EPS)

    def residual_then_norm(acc, res, g):
        xn = acc + res
        return xn, xn * inv_rms(xn) * g

    def norm_bwd(dh, xn, res, g):
        r = inv_rms(xn)
        uu = dh * g
        return (r * uu - xn * (r * r * r * jnp.mean(uu * xn, axis=-1, keepdims=True)) + res,
                jnp.sum(dh * (xn * r), axis=0, keepdims=True))

    def norm_bwd2(dh, xn, res, g):
        dx, dg = norm_bwd(dh, xn, res, g)
        return dx, dx, dg

    def residual_then_loss(acc, res, target, g):
        xn = acc + res
        r = inv_rms(xn)
        xh = xn * r
        e = xh * g - target
        part = 0.5 * jnp.sum(jnp.mean(e * e, axis=-1, keepdims=True), axis=0, keepdims=True)
        dy = e * (1.0 / xn.shape[1])
        uu = dy * g
        dx = r * uu - xn * (r * r * r * jnp.mean(uu * xn, axis=-1, keepdims=True))
        return dx, dx, jnp.sum(dy * xh, axis=0, keepdims=True), part + jnp.zeros_like(g)

    x1, h2 = mm(mixed, w_o, extras=[xs], vecs=[g_mlp], epilogue=residual_then_norm, out_dtypes=[F32, BF16], name="out_proj")

    def relu2(uu):
        r = jnp.maximum(uu.astype(F32), 0.0)
        return (r * r).astype(BF16)

    u, = mm(h2, w_up, out_dtypes=[BF16], name="mlp_up", bn=2048)
    dx2, dx2b, dg_fin, loss_row = mm(u, w_down, a_pro=relu2, extras=[x1, tgt], vecs=[g_fin], epilogue=residual_then_loss,
                                     out_dtypes=[F32, BF16], n_sums=2, name="mlp_down_loss")
    loss = loss_row[:, :1]

    def relu2_grad(acc, uu):
        return (acc * (2.0 * jnp.maximum(uu.astype(F32), 0.0)),)

    du, = mm(dx2b, w_down, trans_b=True, extras=[u], epilogue=relu2_grad, out_dtypes=[BF16], name="mlp_down_bwd", bn=2048)
    dw_down, dw_down_b = (g.reshape(N_CHIPS, -1, w_down.shape[1])
                          for g in mm_tn(u, dx2b, a_pro=relu2, name="dw_down", bf16_copy=True))
    dx1, dx1b, dg_mlp = mm(du, w_up, trans_b=True, extras=[x1, dx2], vecs=[g_mlp], epilogue=norm_bwd2,
                           out_dtypes=[F32, BF16], n_sums=1, name="mlp_up_bwd")
    dw_up, dw_up_b = mm_tn(h2, du, name="dw_up", col_shards=N_CHIPS, bf16_copy=True)

    dw_o, dw_o_b = (g.reshape(N_CHIPS, -1, w_o.shape[1]) for g in mm_tn(mixed, dx1b, name="dw_o", bf16_copy=True))
    late, late_b = (dw_o, dw_up, dw_down), (dw_o_b, dw_up_b, dw_down_b)
    red = reduction
    dfo, dmo, dg_fo, dg_mo, st_f, st_m, dfo_t, dmo_t, *got = mix_norm_bwd(
        dx1b, w_o, fo, mo, g_fo, g_mo, st_f, st_m, name="out_proj_mix_bwd", bt=blk,
        side=red.late_swap(late, late_b) if red else None)
    dfq, dfk, dfv, d_f, *got = flash_bwd(
        qkv, qkv_t, qkv, qkv, dfo, dfo_t, st_f, f_blocks, qoff=0, koff=PAIRS, voff=2 * PAIRS, pair=True,
        scale=fox_scale, name="fox_bwd", qblk=ATTN_BWD_QBLK, side=red.late_scatter(got) if red else None)
    dmq, dmk, dmv, *got = flash_bwd(mq, mq_t, mk, mv, dmo, dmo_t, st_m, None, qoff=0, koff=0, voff=0,
                                    pair=False, scale=mla_scale, name="mla_bwd", qblk=ATTN_BWD_QBLK,
                                    side=red.late_halves(got) if red else None)
    if red:
        red.late_done(got)
    dlf = cumsum_rows(d_f, reverse=True, name="gate_cumsum_bwd")
    dsmall, dq_u, dkv_u, dg_q, dg_kv, db_f = mla_prep_bwd(dmq, dmk, dmv, dlf, small, g_q, g_kv, w_uq_p, w_ukv_p,
                                                          pos_col, inv_row, b_f, name="mla_prep_bwd")
    dw_uq_p = mm_tn(cqn, dq_u, name="dw_uq")
    dw_ukv_p = mm_tn(ckvn, dkv_u, name="dw_ukv")

    def to_bf16(tile):
        return tile.astype(BF16)

    dqkv = [dfq, dfk, dfv]
    dw_in_t = join_w_in_t([mm_tn(part, h1, a_pro=to_bf16, name="dw_" + nm) for part, nm in zip(dqkv, "qkv")],
                          mm_tn(dsmall, h1, name="dw_small"))
    dk_cols = _unpad_heads(dw_ukv_p[:, :HW], HEAD_DIM).reshape(KV_RANK, N_HEADS, HEAD_DIM)
    dv_cols = dw_ukv_p[:, HW:].reshape(KV_RANK, N_HEADS, HEAD_DIM)
    dw_ukv = jnp.concatenate([dk_cols, dv_cols], axis=2).reshape(KV_RANK, N_HEADS * 2 * HEAD_DIM)
    early = (dw_in_t, split_cols(dw_uq_p), split_cols(dw_ukv))
    w_parts = [w_qkv_t[n * FW:(n + 1) * FW] for n in range(3)]
    grad_x, dg_attn, *got = mm(dqkv + [dsmall], w_parts + [w_small_t], a_pro=to_bf16, extras=[xs, dx1], vecs=[g_attn],
                               epilogue=norm_bwd, out_dtypes=[F32], n_sums=1, name="proj_bwd",
                               side=red.early_scatter(early) if red else None)
    if red:
        red.early_done(got)
    d_gains = (dg_attn, db_f[:, :N_HEADS], dg_q, dg_kv, dg_fo, dg_mo, dg_mlp, dg_fin)
    return loss, grad_x, early, late, d_gains


class GradReduction:
    def __init__(self, core_id, chip):
        self.core_id, self.chip = core_id, chip
        self.core = core_id.reshape(1).astype(jnp.int32)
        self.grads, self.pairs, self.halves, self.others = {}, {}, {}, {}

    def _other_halves(self, grads):
        out = []
        for g in grads:
            axis = 1 + split_axis(g.shape[1])
            size = g.shape[axis] // 2
            out.append(lax.dynamic_slice_in_dim(g, (1 - self.core_id) * size, size, axis=axis).astype(BF16))
        return out

    def _add_pairs(self, group, recvs):
        self.pairs[group] = [add_pair(g, r, self.core, name="add_pair_%s_%d" % (group, n))
                             for n, (g, r) in enumerate(zip(self.grads[group], recvs))]
        return scatter_side(self.pairs[group])

    def _chip_sums(self, group, scattered):
        chip = self.chip
        with_mine = [lax.dynamic_update_index_in_dim(s, lax.dynamic_index_in_dim(p, chip, 0, keepdims=True), chip, 0)
                     for s, p in zip(scattered, self.pairs[group])]
        self.halves[group] = [sum_chips(s, name="sum_chips_%s_%d" % (group, n)) for n, s in enumerate(with_mine)]
        return self.halves[group]

    def late_swap(self, grads, bf16_copies):
        self.grads["late"] = list(grads)
        return swap_halves_side(list(bf16_copies))

    def late_scatter(self, recvs):
        return self._add_pairs("late", recvs)

    def late_halves(self, scattered):
        return swap_side(self._chip_sums("late", scattered))

    def late_done(self, others):
        self.others["late"] = list(others)

    def early_scatter(self, grads):
        self.grads["early"] = list(grads)
        return self._add_pairs("early", run_side(swap_side(self._other_halves(grads)), name="swap_early_sends"))

    def early_done(self, scattered):
        self.others["early"] = list(run_side(swap_side(self._chip_sums("early", scattered)), name="swap_early_halves"))

def kernel(x, positions, attn_norm_g, w_in, b_forget, q_norm_g, w_uq, kv_norm_g, w_ukv, fox_out_g, mla_out_g, w_o, mlp_norm_g, w_up, w_down, final_norm_g, loss_target, m_attn_norm_g, m_w_in, m_b_forget, m_q_norm_g, m_w_uq, m_kv_norm_g, m_w_ukv, m_fox_out_g, m_mla_out_g, m_w_o, m_mlp_norm_g, m_w_up, m_w_down, m_final_norm_g, v_attn_norm_g, v_w_in, v_b_forget, v_q_norm_g, v_w_uq, v_kv_norm_g, v_w_ukv, v_fox_out_g, v_mla_out_g, v_w_o, v_mlp_norm_g, v_w_up, v_w_down, v_final_norm_g):
    core_id = lax.axis_index("c")
    core = core_id.reshape(1).astype(jnp.int32)
    chip = 2 * lax.axis_index("x") + lax.axis_index("y")
    big = [w_in, w_uq, w_ukv, w_o, w_up, w_down]
    big_m = [m_w_in, m_w_uq, m_w_ukv, m_w_o, m_w_up, m_w_down]
    big_v = [v_w_in, v_w_uq, v_w_ukv, v_w_o, v_w_up, v_w_down]
    n_early = 3

    def vec(a):
        return a.reshape(1, -1)

    small = [attn_norm_g, b_forget, q_norm_g, kv_norm_g, fox_out_g, mla_out_g, mlp_norm_g, final_norm_g]
    small_m = [m_attn_norm_g, m_b_forget, m_q_norm_g, m_kv_norm_g, m_fox_out_g, m_mla_out_g, m_mlp_norm_g, m_final_norm_g]
    small_v = [v_attn_norm_g, v_b_forget, v_q_norm_g, v_kv_norm_g, v_fox_out_g, v_mla_out_g, v_mlp_norm_g, v_final_norm_g]
    gains = [vec(a) for a in small]

    views = [big[0][0].T, _pad_heads(big[1][0], HEAD_DIM + ROPE_DIM)] + [w[0] for w in big[2:]]
    shards = [v.astype(BF16) for v in views]

    def with_own(gathered, mine):
        return [lax.dynamic_update_index_in_dim(g, s, chip, 0) for g, s in zip(gathered, mine)]

    def early_weights(gathered):
        g_in, g_uq, g_ukv = with_own(gathered, shards[:n_early])
        return g_in, join_cols(g_uq), join_cols(g_ukv)

    def late_weights(gathered):
        g_o, g_up, g_down = with_own(gathered, shards[n_early:])
        return g_o.reshape(-1, g_o.shape[2]), join_cols(g_up), g_down.reshape(-1, g_down.shape[2])

    reduction = GradReduction(core_id, chip)
    loss, grad_x, _, _, d_small = local_step(
        x[0], positions[0], loss_target[0], gains, early_weights, late_weights, gather_side(shards[:n_early]),
        (gather_side(shards[n_early:-1]), gather_side(shards[-1:])), reduction)
    halves = reduction.halves["early"] + reduction.halves["late"]
    others = reduction.others["early"] + reduction.others["late"]

    def rows8(vs):
        return jnp.concatenate([_pad_lanes(vec(a).astype(F32), 1024) for a in vs], axis=0)

    with_loss = [jnp.concatenate([d, loss], axis=1) if n == 1 else d for n, d in enumerate(d_small)]
    g_small8 = allreduce_small(rows8(with_loss))

    outs_big = []
    for n, (w, gm, go, m, v) in enumerate(zip(big, halves, others, big_m, big_v)):
        if n == 0:
            outs = adamw_halves_t(*(jnp.transpose(a, (2, 0, 1)) for a in (w, m, v)), gm, go, core, name="adamw_%d" % n)
            outs_big.append([jnp.transpose(o, (1, 2, 0)) for o in outs])
        elif n == 1:
            gm, go = (_unpad_heads(gh, HEAD_DIM + ROPE_DIM) for gh in (gm, go))
            g_t = jnp.where(core[0] == 0, jnp.concatenate([gm, go], axis=0), jnp.concatenate([go, gm], axis=0)).T
            outs = (g_t,) + tuple(adamw_whole(w[0].T, g_t, m[0].T, v[0].T, name="adamw_%d" % n))
            outs_big.append([o.T[None] for o in outs])
        else:
            outs_big.append(adamw_halves(w, gm, go, m, v, core, name="adamw_%d" % n))
    outs_small = adamw_vectors(gains, g_small8, [vec(a) for a in small_m], [vec(a) for a in small_v], name="adamw_small")

    loss_all = g_small8[1, N_HEADS]
    grads, deltas, new_m, new_v = [None] * 14, [None] * 14, [None] * 14, [None] * 14
    big_at = [1, 4, 6, 9, 11, 12]
    small_at = [0, 2, 3, 5, 7, 8, 10, 13]
    for n, at in enumerate(big_at):
        grads[at], deltas[at], new_m[at], new_v[at] = outs_big[n]
    for at, s, outs in zip(small_at, small, outs_small):
        grads[at], deltas[at], new_m[at], new_v[at] = (o.reshape(s.shape) for o in outs)
    return (loss_all, grad_x[None], *grads, *deltas, *new_m, *new_v)
```
